```python
import jax, jax.numpy as jnp
from jax import lax
import numpy as np

D_MODEL = 1024
BATCH = 16
SEQ = 2048
DEPTH = 1

CHUNK = 64
HEAD_DIM = 64
A_HEADS = 8
A_PREV_CHUNKS = 8
A_MAX_REL = 128
B_Q_HEADS = 8
B_KV_HEADS = 2
B_GROUP = B_Q_HEADS // B_KV_HEADS
B_WINDOW = 128
B_PREV_CHUNKS = (B_WINDOW - 1 + CHUNK - 1) // CHUNK
A_WIDTH = A_HEADS * HEAD_DIM
B_Q_WIDTH = B_Q_HEADS * HEAD_DIM
B_KV_WIDTH = B_KV_HEADS * HEAD_DIM
IN_COLS = 3 * A_WIDTH + B_Q_WIDTH + 2 * B_KV_WIDTH
D_FF = 2816
PLE_DIM = 256
EPS = 1e-6
NEG_INF = -1e30

kernel_name = "hybrid_chunked_relpos_swa_sink_macaron_ple"


def rms_norm(x, gain):
    xf = x.astype(jnp.float32)
    y = xf * lax.rsqrt(jnp.mean(xf * xf, axis=-1, keepdims=True) + EPS)
    return (y * gain.astype(jnp.float32)).astype(x.dtype)


def swiglu_ffn(x, w_gu, w_down):
    g, u = jnp.split(x @ w_gu, 2, axis=-1)
    return (jax.nn.silu(g) * u) @ w_down


def alibi_slopes(n_heads):
    return np.array([2.0 ** (-8.0 * (h + 1) / n_heads) for h in range(n_heads)], dtype=np.float32)


def band_distance(n_prev):
    i = np.arange(CHUNK)[:, None]
    j = np.arange((n_prev + 1) * CHUNK)[None, :]
    return i + n_prev * CHUNK - j


def chunk_band_attention(q, k, v, n_prev, bias, sink):
    b, hkv, g, s, dh = q.shape
    n_chunks = s // CHUNK
    band = (n_prev + 1) * CHUNK
    pad = n_prev * CHUNK
    kp = jnp.pad(k, ((0, 0), (0, 0), (pad, 0), (0, 0)))
    vp = jnp.pad(v, ((0, 0), (0, 0), (pad, 0), (0, 0)))
    scale = dh ** -0.5
    key_idx = jnp.arange(band)

    def one_chunk(c):
        start = c * CHUNK
        qc = lax.dynamic_slice_in_dim(q, start, CHUNK, axis=3)
        kc = lax.dynamic_slice_in_dim(kp, start, band, axis=2)
        vc = lax.dynamic_slice_in_dim(vp, start, band, axis=2)
        scores = jnp.einsum('bkgqd,bksd->bkgqs', qc.astype(jnp.float32),
                            kc.astype(jnp.float32)) * scale + bias
        valid = key_idx >= (n_prev - c) * CHUNK
        scores = jnp.where(valid, scores, NEG_INF)
        if sink is None:
            probs = jax.nn.softmax(scores, axis=-1)
        else:
            sink_col = jnp.broadcast_to(sink.astype(jnp.float32).reshape(1, hkv, g, 1, 1),
                                        (b, hkv, g, CHUNK, 1))
            probs = jax.nn.softmax(jnp.concatenate([scores, sink_col], axis=-1), axis=-1)[..., :band]
        out = jnp.einsum('bkgqs,bksd->bkgqd', probs, vc.astype(jnp.float32))
        return out.astype(v.dtype)

    outs = lax.map(one_chunk, jnp.arange(n_chunks))
    outs = jnp.transpose(outs, (1, 0, 4, 2, 3, 5))
    return outs.reshape(b, s, hkv * g * dh)


def _fwd_setup_inputs(seed: int = 0) -> dict:
    key = jax.random.key(seed)
    ks = jax.random.split(key, 24)
    f32 = jnp.float32

    def w(k, shape, fan_in):
        return jax.random.normal(k, shape, f32) * (fan_in ** -0.5)

    def gain(k, n):
        return 1.0 + 0.05 * jax.random.normal(k, (DEPTH, n), f32)

    return {
        "x": jax.random.normal(ks[0], (BATCH, SEQ, D_MODEL), f32),
        "p": jax.random.normal(ks[1], (DEPTH, BATCH, SEQ, PLE_DIM), f32),
        "ffn1_norm": gain(ks[2], D_MODEL),
        "ffn1_w_gu": w(ks[3], (DEPTH, D_MODEL, 2 * D_FF), D_MODEL),
        "ffn1_w_down": w(ks[4], (DEPTH, D_FF, D_MODEL), D_FF),
        "mix_norm": gain(ks[5], D_MODEL),
        "w_in": w(ks[6], (DEPTH, D_MODEL, IN_COLS), D_MODEL),
        "a_q_norm": gain(ks[7], HEAD_DIM),
        "a_k_norm": gain(ks[8], HEAD_DIM),
        "a_rel_bias": 0.1 * jax.random.normal(ks[9], (DEPTH, A_HEADS, 2 * A_MAX_REL + 1), f32),
        "b_q_norm": gain(ks[10], HEAD_DIM),
        "b_k_norm": gain(ks[11], HEAD_DIM),
        "b_sinks": 0.5 * jax.random.normal(ks[12], (DEPTH, B_Q_HEADS), f32),
        "w_gate": w(ks[13], (DEPTH, D_MODEL, 2 * D_MODEL), D_MODEL),
        "w_proj_a": w(ks[14], (DEPTH, A_WIDTH, D_MODEL), A_WIDTH),
        "w_proj_b": w(ks[15], (DEPTH, B_Q_WIDTH, D_MODEL), B_Q_WIDTH),
        "w_out": w(ks[16], (DEPTH, D_MODEL, D_MODEL), D_MODEL),
        "ffn2_norm": gain(ks[17], D_MODEL),
        "ffn2_w_gu": w(ks[18], (DEPTH, D_MODEL, 2 * D_FF), D_MODEL),
        "ffn2_w_down": w(ks[19], (DEPTH, D_FF, D_MODEL), D_FF),
        "ple_norm": gain(ks[20], D_MODEL),
        "w_ple_gate": w(ks[21], (DEPTH, D_MODEL, D_MODEL), D_MODEL),
        "w_ple_proj": w(ks[22], (DEPTH, PLE_DIM, D_MODEL), PLE_DIM),
    }


def _fwd_reference(x, p, ffn1_norm, ffn1_w_gu, ffn1_w_down, mix_norm, w_in, a_q_norm, a_k_norm,
              a_rel_bias, b_q_norm, b_k_norm, b_sinks, w_gate, w_proj_a, w_proj_b, w_out,
              ffn2_norm, ffn2_w_gu, ffn2_w_down, ple_norm, w_ple_gate, w_ple_proj):
    b, s, _ = x.shape
    split_points = list(np.cumsum([A_WIDTH, A_WIDTH, A_WIDTH, B_Q_WIDTH, B_KV_WIDTH]))
    a_rel_idx = np.clip(band_distance(A_PREV_CHUNKS), -A_MAX_REL, A_MAX_REL) + A_MAX_REL
    b_dist = np.abs(band_distance(B_PREV_CHUNKS)).astype(np.float32)
    b_alibi = jnp.asarray((-alibi_slopes(B_Q_HEADS)[:, None, None] * b_dist[None])
                          .reshape(B_KV_HEADS, B_GROUP, CHUNK, -1))

    h = x
    for i in range(DEPTH):
        h = h + 0.5 * swiglu_ffn(rms_norm(h, ffn1_norm[i]), ffn1_w_gu[i], ffn1_w_down[i])

        u = rms_norm(h, mix_norm[i])
        qa, ka, va, qb, kb, vb = jnp.split(u @ w_in[i], split_points, axis=-1)

        qa = rms_norm(qa.reshape(b, s, A_HEADS, HEAD_DIM), a_q_norm[i])
        ka = rms_norm(ka.reshape(b, s, A_HEADS, HEAD_DIM), a_k_norm[i])
        qa = jnp.transpose(qa, (0, 2, 1, 3))[:, :, None]
        ka = jnp.transpose(ka, (0, 2, 1, 3))
        va = jnp.transpose(va.reshape(b, s, A_HEADS, HEAD_DIM), (0, 2, 1, 3))
        a_bias = a_rel_bias[i].astype(jnp.float32)[:, a_rel_idx][:, None]
        ya = chunk_band_attention(qa, ka, va, A_PREV_CHUNKS, a_bias, None)

        qb = rms_norm(qb.reshape(b, s, B_KV_HEADS, B_GROUP, HEAD_DIM), b_q_norm[i])
        kb = rms_norm(kb.reshape(b, s, B_KV_HEADS, HEAD_DIM), b_k_norm[i])
        qb = jnp.transpose(qb, (0, 2, 3, 1, 4))
        kb = jnp.transpose(kb, (0, 2, 1, 3))
        vb = jnp.transpose(vb.reshape(b, s, B_KV_HEADS, HEAD_DIM), (0, 2, 1, 3))
        yb = chunk_band_attention(qb, kb, vb, B_PREV_CHUNKS, b_alibi,
                                  b_sinks[i].reshape(B_KV_HEADS, B_GROUP))

        ga, gb = jnp.split(jax.nn.sigmoid(u @ w_gate[i]), 2, axis=-1)
        merged = ga * (ya @ w_proj_a[i]) + gb * (yb @ w_proj_b[i])
        h = h + merged @ w_out[i]

        h = h + 0.5 * swiglu_ffn(rms_norm(h, ffn2_norm[i]), ffn2_w_gu[i], ffn2_w_down[i])

        ple_gate = jax.nn.sigmoid(rms_norm(h, ple_norm[i]) @ w_ple_gate[i])
        h = h + ple_gate * (p[i] @ w_ple_proj[i])
    return h


import jax as _jax
import jax.numpy as _jnp

TWIN_FORMAT = 'train_step'
FWD_PARAMS = ['x', 'p', 'ffn1_norm', 'ffn1_w_gu', 'ffn1_w_down', 'mix_norm', 'w_in', 'a_q_norm', 'a_k_norm', 'a_rel_bias', 'b_q_norm', 'b_k_norm', 'b_sinks', 'w_gate', 'w_proj_a', 'w_proj_b', 'w_out', 'ffn2_norm', 'ffn2_w_gu', 'ffn2_w_down', 'ple_norm', 'w_ple_gate', 'w_ple_proj']
TWIN_WEIGHTS = ['ffn1_norm', 'ffn1_w_gu', 'ffn1_w_down', 'mix_norm', 'w_in', 'a_q_norm', 'a_k_norm', 'a_rel_bias', 'b_q_norm', 'b_k_norm', 'b_sinks', 'w_gate', 'w_proj_a', 'w_proj_b', 'w_out', 'ffn2_norm', 'ffn2_w_gu', 'ffn2_w_down', 'ple_norm', 'w_ple_gate', 'w_ple_proj']
TWIN_DIFF_INPUT = 'x'
TWIN_INPUTS = ['x', 'p', 'ffn1_norm', 'ffn1_w_gu', 'ffn1_w_down', 'mix_norm', 'w_in', 'a_q_norm', 'a_k_norm', 'a_rel_bias', 'b_q_norm', 'b_k_norm', 'b_sinks', 'w_gate', 'w_proj_a', 'w_proj_b', 'w_out', 'ffn2_norm', 'ffn2_w_gu', 'ffn2_w_down', 'ple_norm', 'w_ple_gate', 'w_ple_proj', 'loss_target', 'm_ffn1_norm', 'm_ffn1_w_gu', 'm_ffn1_w_down', 'm_mix_norm', 'm_w_in', 'm_a_q_norm', 'm_a_k_norm', 'm_a_rel_bias', 'm_b_q_norm', 'm_b_k_norm', 'm_b_sinks', 'm_w_gate', 'm_w_proj_a', 'm_w_proj_b', 'm_w_out', 'm_ffn2_norm', 'm_ffn2_w_gu', 'm_ffn2_w_down', 'm_ple_norm', 'm_w_ple_gate', 'm_w_ple_proj', 'v_ffn1_norm', 'v_ffn1_w_gu', 'v_ffn1_w_down', 'v_mix_norm', 'v_w_in', 'v_a_q_norm', 'v_a_k_norm', 'v_a_rel_bias', 'v_b_q_norm', 'v_b_k_norm', 'v_b_sinks', 'v_w_gate', 'v_w_proj_a', 'v_w_proj_b', 'v_w_out', 'v_ffn2_norm', 'v_ffn2_w_gu', 'v_ffn2_w_down', 'v_ple_norm', 'v_w_ple_gate', 'v_w_ple_proj']
TWIN_OUTPUTS = ['loss', 'grad_x', 'grad_ffn1_norm', 'grad_ffn1_w_gu', 'grad_ffn1_w_down', 'grad_mix_norm', 'grad_w_in', 'grad_a_q_norm', 'grad_a_k_norm', 'grad_a_rel_bias', 'grad_b_q_norm', 'grad_b_k_norm', 'grad_b_sinks', 'grad_w_gate', 'grad_w_proj_a', 'grad_w_proj_b', 'grad_w_out', 'grad_ffn2_norm', 'grad_ffn2_w_gu', 'grad_ffn2_w_down', 'grad_ple_norm', 'grad_w_ple_gate', 'grad_w_ple_proj', 'delta_ffn1_norm', 'delta_ffn1_w_gu', 'delta_ffn1_w_down', 'delta_mix_norm', 'delta_w_in', 'delta_a_q_norm', 'delta_a_k_norm', 'delta_a_rel_bias', 'delta_b_q_norm', 'delta_b_k_norm', 'delta_b_sinks', 'delta_w_gate', 'delta_w_proj_a', 'delta_w_proj_b', 'delta_w_out', 'delta_ffn2_norm', 'delta_ffn2_w_gu', 'delta_ffn2_w_down', 'delta_ple_norm', 'delta_w_ple_gate', 'delta_w_ple_proj', 'new_m_ffn1_norm', 'new_m_ffn1_w_gu', 'new_m_ffn1_w_down', 'new_m_mix_norm', 'new_m_w_in', 'new_m_a_q_norm', 'new_m_a_k_norm', 'new_m_a_rel_bias', 'new_m_b_q_norm', 'new_m_b_k_norm', 'new_m_b_sinks', 'new_m_w_gate', 'new_m_w_proj_a', 'new_m_w_proj_b', 'new_m_w_out', 'new_m_ffn2_norm', 'new_m_ffn2_w_gu', 'new_m_ffn2_w_down', 'new_m_ple_norm', 'new_m_w_ple_gate', 'new_m_w_ple_proj', 'new_v_ffn1_norm', 'new_v_ffn1_w_gu', 'new_v_ffn1_w_down', 'new_v_mix_norm', 'new_v_w_in', 'new_v_a_q_norm', 'new_v_a_k_norm', 'new_v_a_rel_bias', 'new_v_b_q_norm', 'new_v_b_k_norm', 'new_v_b_sinks', 'new_v_w_gate', 'new_v_w_proj_a', 'new_v_w_proj_b', 'new_v_w_out', 'new_v_ffn2_norm', 'new_v_ffn2_w_gu', 'new_v_ffn2_w_down', 'new_v_ple_norm', 'new_v_w_ple_gate', 'new_v_w_ple_proj']
TWIN_LEAF_KINDS = {'loss': 'loss', 'grad_x': 'grad_x', 'grad_ffn1_norm': 'grad_w', 'grad_ffn1_w_gu': 'grad_w', 'grad_ffn1_w_down': 'grad_w', 'grad_mix_norm': 'grad_w', 'grad_w_in': 'grad_w', 'grad_a_q_norm': 'grad_w', 'grad_a_k_norm': 'grad_w', 'grad_a_rel_bias': 'grad_w', 'grad_b_q_norm': 'grad_w', 'grad_b_k_norm': 'grad_w', 'grad_b_sinks': 'grad_w', 'grad_w_gate': 'grad_w', 'grad_w_proj_a': 'grad_w', 'grad_w_proj_b': 'grad_w', 'grad_w_out': 'grad_w', 'grad_ffn2_norm': 'grad_w', 'grad_ffn2_w_gu': 'grad_w', 'grad_ffn2_w_down': 'grad_w', 'grad_ple_norm': 'grad_w', 'grad_w_ple_gate': 'grad_w', 'grad_w_ple_proj': 'grad_w', 'delta_ffn1_norm': 'delta_w', 'delta_ffn1_w_gu': 'delta_w', 'delta_ffn1_w_down': 'delta_w', 'delta_mix_norm': 'delta_w', 'delta_w_in': 'delta_w', 'delta_a_q_norm': 'delta_w', 'delta_a_k_norm': 'delta_w', 'delta_a_rel_bias': 'delta_w', 'delta_b_q_norm': 'delta_w', 'delta_b_k_norm': 'delta_w', 'delta_b_sinks': 'delta_w', 'delta_w_gate': 'delta_w', 'delta_w_proj_a': 'delta_w', 'delta_w_proj_b': 'delta_w', 'delta_w_out': 'delta_w', 'delta_ffn2_norm': 'delta_w', 'delta_ffn2_w_gu': 'delta_w', 'delta_ffn2_w_down': 'delta_w', 'delta_ple_norm': 'delta_w', 'delta_w_ple_gate': 'delta_w', 'delta_w_ple_proj': 'delta_w', 'new_m_ffn1_norm': 'new_m', 'new_m_ffn1_w_gu': 'new_m', 'new_m_ffn1_w_down': 'new_m', 'new_m_mix_norm': 'new_m', 'new_m_w_in': 'new_m', 'new_m_a_q_norm': 'new_m', 'new_m_a_k_norm': 'new_m', 'new_m_a_rel_bias': 'new_m', 'new_m_b_q_norm': 'new_m', 'new_m_b_k_norm': 'new_m', 'new_m_b_sinks': 'new_m', 'new_m_w_gate': 'new_m', 'new_m_w_proj_a': 'new_m', 'new_m_w_proj_b': 'new_m', 'new_m_w_out': 'new_m', 'new_m_ffn2_norm': 'new_m', 'new_m_ffn2_w_gu': 'new_m', 'new_m_ffn2_w_down': 'new_m', 'new_m_ple_norm': 'new_m', 'new_m_w_ple_gate': 'new_m', 'new_m_w_ple_proj': 'new_m', 'new_v_ffn1_norm': 'new_v', 'new_v_ffn1_w_gu': 'new_v', 'new_v_ffn1_w_down': 'new_v', 'new_v_mix_norm': 'new_v', 'new_v_w_in': 'new_v', 'new_v_a_q_norm': 'new_v', 'new_v_a_k_norm': 'new_v', 'new_v_a_rel_bias': 'new_v', 'new_v_b_q_norm': 'new_v', 'new_v_b_k_norm': 'new_v', 'new_v_b_sinks': 'new_v', 'new_v_w_gate': 'new_v', 'new_v_w_proj_a': 'new_v', 'new_v_w_proj_b': 'new_v', 'new_v_w_out': 'new_v', 'new_v_ffn2_norm': 'new_v', 'new_v_ffn2_w_gu': 'new_v', 'new_v_ffn2_w_down': 'new_v', 'new_v_ple_norm': 'new_v', 'new_v_w_ple_gate': 'new_v', 'new_v_w_ple_proj': 'new_v'}


def _forward(args):
    return _fwd_reference(*[args[k] for k in FWD_PARAMS])


def _output_shape():
    out = _jax.eval_shape(lambda: _forward(_fwd_setup_inputs(0)))
    return out.shape, out.dtype

N_MICROBATCH = 1
ADAM_LR = 0.001
ADAM_B1 = 0.9
ADAM_B2 = 0.999
ADAM_EPS = 1e-08
ADAM_WD = 0.01
ADAM_STEP = 10
PER_EXAMPLE_BATCH_AXIS = {'x': 0, 'p': 1, 'loss_target': 0}
SHARED_INPUTS = []
_WEIGHT_DTYPES = {'ffn1_norm': _jnp.float32, 'ffn1_w_gu': _jnp.float32, 'ffn1_w_down': _jnp.float32, 'mix_norm': _jnp.float32, 'w_in': _jnp.float32, 'a_q_norm': _jnp.float32, 'a_k_norm': _jnp.float32, 'a_rel_bias': _jnp.float32, 'b_q_norm': _jnp.float32, 'b_k_norm': _jnp.float32, 'b_sinks': _jnp.float32, 'w_gate': _jnp.float32, 'w_proj_a': _jnp.float32, 'w_proj_b': _jnp.float32, 'w_out': _jnp.float32, 'ffn2_norm': _jnp.float32, 'ffn2_w_gu': _jnp.float32, 'ffn2_w_down': _jnp.float32, 'ple_norm': _jnp.float32, 'w_ple_gate': _jnp.float32, 'w_ple_proj': _jnp.float32}
MOMENT_SCALE = {'ffn1_norm': 6.186479e+00, 'ffn1_w_gu': 6.837919e-02, 'ffn1_w_down': 1.157330e-01, 'mix_norm': 5.153050e-01, 'w_in': 5.912108e-02, 'a_q_norm': 4.687664e-01, 'a_k_norm': 4.693587e-01, 'a_rel_bias': 1.593324e-02, 'b_q_norm': 4.697955e+00, 'b_k_norm': 4.735112e+00, 'b_sinks': 8.503573e+00, 'w_gate': 1.373017e-02, 'w_proj_a': 2.939431e-02, 'w_proj_b': 5.579255e-02, 'w_out': 5.353880e-02, 'ffn2_norm': 6.198000e+00, 'ffn2_w_gu': 6.426119e-02, 'ffn2_w_down': 1.102989e-01, 'ple_norm': 9.486867e-01, 'w_ple_gate': 7.078540e-02, 'w_ple_proj': 4.228320e-01}


def _to_microbatches(a, axis):
    t = _jnp.moveaxis(a, axis, 0)
    t = t.reshape((N_MICROBATCH, t.shape[0] // N_MICROBATCH) + t.shape[1:])
    return _jnp.moveaxis(t, 1, axis + 1)


def setup_inputs(seed: int = 0) -> dict:
    inp = _fwd_setup_inputs(seed)
    key = _jax.random.fold_in(_jax.random.key(seed), 7919)
    shape, _ = _output_shape()
    out = dict(inp)
    out["loss_target"] = _jax.random.normal(_jax.random.fold_in(key, 0), shape, _jnp.float32)
    for i, name in enumerate(TWIN_WEIGHTS):
        w = inp[name].astype(_jnp.float32)
        if MOMENT_SCALE is None:
            s = _jnp.sqrt(_jnp.mean(_jnp.square(w)) + 1e-30)
        else:
            s = MOMENT_SCALE[name]
        km, kv = _jax.random.split(_jax.random.fold_in(key, i + 1))
        out[name] = w
        out["m_" + name] = s * _jax.random.normal(km, w.shape, _jnp.float32)
        out["v_" + name] = (s * s) * _jax.random.uniform(kv, w.shape, _jnp.float32, 0.5, 1.5)
    if N_MICROBATCH > 1:
        for name, axis in PER_EXAMPLE_BATCH_AXIS.items():
            out[name] = _to_microbatches(out[name], axis)
    return {'x': out['x'], 'p': out['p'], 'ffn1_norm': out['ffn1_norm'], 'ffn1_w_gu': out['ffn1_w_gu'], 'ffn1_w_down': out['ffn1_w_down'], 'mix_norm': out['mix_norm'], 'w_in': out['w_in'], 'a_q_norm': out['a_q_norm'], 'a_k_norm': out['a_k_norm'], 'a_rel_bias': out['a_rel_bias'], 'b_q_norm': out['b_q_norm'], 'b_k_norm': out['b_k_norm'], 'b_sinks': out['b_sinks'], 'w_gate': out['w_gate'], 'w_proj_a': out['w_proj_a'], 'w_proj_b': out['w_proj_b'], 'w_out': out['w_out'], 'ffn2_norm': out['ffn2_norm'], 'ffn2_w_gu': out['ffn2_w_gu'], 'ffn2_w_down': out['ffn2_w_down'], 'ple_norm': out['ple_norm'], 'w_ple_gate': out['w_ple_gate'], 'w_ple_proj': out['w_ple_proj'], 'loss_target': out['loss_target'], 'm_ffn1_norm': out['m_ffn1_norm'], 'm_ffn1_w_gu': out['m_ffn1_w_gu'], 'm_ffn1_w_down': out['m_ffn1_w_down'], 'm_mix_norm': out['m_mix_norm'], 'm_w_in': out['m_w_in'], 'm_a_q_norm': out['m_a_q_norm'], 'm_a_k_norm': out['m_a_k_norm'], 'm_a_rel_bias': out['m_a_rel_bias'], 'm_b_q_norm': out['m_b_q_norm'], 'm_b_k_norm': out['m_b_k_norm'], 'm_b_sinks': out['m_b_sinks'], 'm_w_gate': out['m_w_gate'], 'm_w_proj_a': out['m_w_proj_a'], 'm_w_proj_b': out['m_w_proj_b'], 'm_w_out': out['m_w_out'], 'm_ffn2_norm': out['m_ffn2_norm'], 'm_ffn2_w_gu': out['m_ffn2_w_gu'], 'm_ffn2_w_down': out['m_ffn2_w_down'], 'm_ple_norm': out['m_ple_norm'], 'm_w_ple_gate': out['m_w_ple_gate'], 'm_w_ple_proj': out['m_w_ple_proj'], 'v_ffn1_norm': out['v_ffn1_norm'], 'v_ffn1_w_gu': out['v_ffn1_w_gu'], 'v_ffn1_w_down': out['v_ffn1_w_down'], 'v_mix_norm': out['v_mix_norm'], 'v_w_in': out['v_w_in'], 'v_a_q_norm': out['v_a_q_norm'], 'v_a_k_norm': out['v_a_k_norm'], 'v_a_rel_bias': out['v_a_rel_bias'], 'v_b_q_norm': out['v_b_q_norm'], 'v_b_k_norm': out['v_b_k_norm'], 'v_b_sinks': out['v_b_sinks'], 'v_w_gate': out['v_w_gate'], 'v_w_proj_a': out['v_w_proj_a'], 'v_w_proj_b': out['v_w_proj_b'], 'v_w_out': out['v_w_out'], 'v_ffn2_norm': out['v_ffn2_norm'], 'v_ffn2_w_gu': out['v_ffn2_w_gu'], 'v_ffn2_w_down': out['v_ffn2_w_down'], 'v_ple_norm': out['v_ple_norm'], 'v_w_ple_gate': out['v_w_ple_gate'], 'v_w_ple_proj': out['v_w_ple_proj']}


def _loss(weights, diff, rest, loss_target):
    with _jax.named_scope("forward"):
        args = {**rest, TWIN_DIFF_INPUT: diff, **{k: w.astype(_WEIGHT_DTYPES[k]) for k, w in weights.items()}}
        y = _forward(args)
    with _jax.named_scope("loss_head"):
        err = _jnp.square(y.astype(_jnp.float32) - loss_target)
        return 0.5 * _jnp.sum(_jnp.mean(err, axis=-1)) if err.ndim else 0.5 * err


def _adamw(w, g, m, v):
    m = ADAM_B1 * m + (1.0 - ADAM_B1) * g
    v = ADAM_B2 * v + (1.0 - ADAM_B2) * _jnp.square(g)
    m_hat = m / (1.0 - ADAM_B1 ** ADAM_STEP)
    v_hat = v / (1.0 - ADAM_B2 ** ADAM_STEP)
    delta = -ADAM_LR * (m_hat / (_jnp.sqrt(v_hat) + ADAM_EPS) + ADAM_WD * w)
    return delta, m, v


def reference(x, p, ffn1_norm, ffn1_w_gu, ffn1_w_down, mix_norm, w_in, a_q_norm, a_k_norm, a_rel_bias, b_q_norm, b_k_norm, b_sinks, w_gate, w_proj_a, w_proj_b, w_out, ffn2_norm, ffn2_w_gu, ffn2_w_down, ple_norm, w_ple_gate, w_ple_proj, loss_target, m_ffn1_norm, m_ffn1_w_gu, m_ffn1_w_down, m_mix_norm, m_w_in, m_a_q_norm, m_a_k_norm, m_a_rel_bias, m_b_q_norm, m_b_k_norm, m_b_sinks, m_w_gate, m_w_proj_a, m_w_proj_b, m_w_out, m_ffn2_norm, m_ffn2_w_gu, m_ffn2_w_down, m_ple_norm, m_w_ple_gate, m_w_ple_proj, v_ffn1_norm, v_ffn1_w_gu, v_ffn1_w_down, v_mix_norm, v_w_in, v_a_q_norm, v_a_k_norm, v_a_rel_bias, v_b_q_norm, v_b_k_norm, v_b_sinks, v_w_gate, v_w_proj_a, v_w_proj_b, v_w_out, v_ffn2_norm, v_ffn2_w_gu, v_ffn2_w_down, v_ple_norm, v_w_ple_gate, v_w_ple_proj):
    given = dict(x=x, p=p, ffn1_norm=ffn1_norm, ffn1_w_gu=ffn1_w_gu, ffn1_w_down=ffn1_w_down, mix_norm=mix_norm, w_in=w_in, a_q_norm=a_q_norm, a_k_norm=a_k_norm, a_rel_bias=a_rel_bias, b_q_norm=b_q_norm, b_k_norm=b_k_norm, b_sinks=b_sinks, w_gate=w_gate, w_proj_a=w_proj_a, w_proj_b=w_proj_b, w_out=w_out, ffn2_norm=ffn2_norm, ffn2_w_gu=ffn2_w_gu, ffn2_w_down=ffn2_w_down, ple_norm=ple_norm, w_ple_gate=w_ple_gate, w_ple_proj=w_ple_proj, loss_target=loss_target, m_ffn1_norm=m_ffn1_norm, m_ffn1_w_gu=m_ffn1_w_gu, m_ffn1_w_down=m_ffn1_w_down, m_mix_norm=m_mix_norm, m_w_in=m_w_in, m_a_q_norm=m_a_q_norm, m_a_k_norm=m_a_k_norm, m_a_rel_bias=m_a_rel_bias, m_b_q_norm=m_b_q_norm, m_b_k_norm=m_b_k_norm, m_b_sinks=m_b_sinks, m_w_gate=m_w_gate, m_w_proj_a=m_w_proj_a, m_w_proj_b=m_w_proj_b, m_w_out=m_w_out, m_ffn2_norm=m_ffn2_norm, m_ffn2_w_gu=m_ffn2_w_gu, m_ffn2_w_down=m_ffn2_w_down, m_ple_norm=m_ple_norm, m_w_ple_gate=m_w_ple_gate, m_w_ple_proj=m_w_ple_proj, v_ffn1_norm=v_ffn1_norm, v_ffn1_w_gu=v_ffn1_w_gu, v_ffn1_w_down=v_ffn1_w_down, v_mix_norm=v_mix_norm, v_w_in=v_w_in, v_a_q_norm=v_a_q_norm, v_a_k_norm=v_a_k_norm, v_a_rel_bias=v_a_rel_bias, v_b_q_norm=v_b_q_norm, v_b_k_norm=v_b_k_norm, v_b_sinks=v_b_sinks, v_w_gate=v_w_gate, v_w_proj_a=v_w_proj_a, v_w_proj_b=v_w_proj_b, v_w_out=v_w_out, v_ffn2_norm=v_ffn2_norm, v_ffn2_w_gu=v_ffn2_w_gu, v_ffn2_w_down=v_ffn2_w_down, v_ple_norm=v_ple_norm, v_w_ple_gate=v_w_ple_gate, v_w_ple_proj=v_w_ple_proj)
    weights = {n: given[n] for n in TWIN_WEIGHTS}
    shared = {n: given[n] for n in SHARED_INPUTS}
    per_example = {n: given[n] for n in ['x', 'p']}
    grad_fn = _jax.value_and_grad(_loss, argnums=(0, 1))

    def one_microbatch(ex, loss_target):
        ex = dict(ex)
        diff = ex.pop(TWIN_DIFF_INPUT)
        return grad_fn(weights, diff, {**shared, **ex}, loss_target)

    if N_MICROBATCH == 1:
        loss, (grad_w, grad_x) = one_microbatch(per_example, given["loss_target"])
    else:
        def body(carry, xs):
            loss_sum, grad_sum = carry
            l_k, (gw_k, gx_k) = one_microbatch(xs[0], xs[1])
            with _jax.named_scope("update"):
                return (loss_sum + l_k, _jax.tree.map(_jnp.add, grad_sum, gw_k)), gx_k

        init = (_jnp.zeros((), _jnp.float32), _jax.tree.map(_jnp.zeros_like, weights))
        (loss, grad_w), grad_x = _jax.lax.scan(body, init, (per_example, given["loss_target"]))
    with _jax.named_scope("update"):
        delta_w, new_m, new_v = {}, {}, {}
        for n in TWIN_WEIGHTS:
            delta_w[n], new_m[n], new_v[n] = _adamw(weights[n], grad_w[n], given["m_" + n], given["v_" + n])
    return (loss, grad_x, *[grad_w[n] for n in TWIN_WEIGHTS], *[delta_w[n] for n in TWIN_WEIGHTS],
            *[new_m[n] for n in TWIN_WEIGHTS], *[new_v[n] for n in TWIN_WEIGHTS])
```

```python
import functools

import numpy as np
import jax
import jax.numpy as jnp
from jax import lax
from jax.experimental import pallas as pl
from jax.experimental.pallas import tpu as pltpu

F32 = jnp.float32
BF16 = jnp.bfloat16

CHUNK = 64
HEAD_DIM = 64
A_PREV_CHUNKS = 8
A_MAX_REL = 128
N_HEADS = 8
B_KV_HEADS = 2
B_PREV_CHUNKS = 2
A_WIDTH = N_HEADS * HEAD_DIM
B_KV_WIDTH = B_KV_HEADS * HEAD_DIM
EPS = 1e-6
NEG_INF = -1e30
ATTN_SCALE = HEAD_DIM ** -0.5
Q_BLOCK = 128
PAIR = 2 * HEAD_DIM

ADAM_LR = 0.001
ADAM_B1 = 0.9
ADAM_B2 = 0.999
ADAM_EPS = 1e-08
ADAM_WD = 0.01
ADAM_STEP = 10

N_CHIPS = 4
N_DEV = 8
VMEM_LIMIT_V7X = 56 * 1024 * 1024
MESH = pl.DeviceIdType.MESH
ANY = pl.BlockSpec(memory_space=pl.ANY)

_DN = {
    "nn": (((1,), (0,)), ((), ())),
    "nt": (((1,), (1,)), ((), ())),
    "tn": (((0,), (0,)), ((), ())),
}


def _pick(n, target, mult=128):
    best = None
    for d in range(mult, min(n, target) + 1, mult):
        if n % d == 0:
            best = d
    return n if best is None else best


def _dot(a, b, mode):
    return lax.dot_general(a.astype(BF16), b.astype(BF16), _DN[mode], preferred_element_type=F32)


def _params():
    return pltpu.CompilerParams(vmem_limit_bytes=VMEM_LIMIT_V7X)


def _mm(name, mode, grid, pairs, extras, outs, acc_shape, epilogue, steps=None):
    nk = grid[2]
    n_in = 2 * len(pairs) + len(extras)
    n_out = len(outs)
    n_acc = len(pairs) if steps is None else 1

    def body(*refs):
        in_refs = refs[:n_in]
        out_refs = refs[n_in:n_in + n_out]
        accs = refs[n_in + n_out:]
        i = pl.program_id(0)
        j = pl.program_id(1)
        k = pl.program_id(2)

        @pl.when(k == 0)
        def _():
            for acc in accs:
                acc[...] = jnp.zeros(acc.shape, F32)

        def contrib(p, acc):
            acc[...] += _dot(in_refs[2 * p][...], in_refs[2 * p + 1][...], mode)

        if steps is None:
            for p in range(len(pairs)):
                contrib(p, accs[p])
        else:
            lo = 0
            for p, n in enumerate(steps):
                pl.when((k >= lo) & (k < lo + n))(functools.partial(contrib, p, accs[0]))
                lo += n

        @pl.when(k == nk - 1)
        def _():
            epilogue([acc[...] for acc in accs], in_refs[2 * len(pairs):], out_refs, (i, j))

    args, in_specs = [], []
    for a, a_spec, b, b_spec in pairs:
        args += [a, b]
        in_specs += [a_spec, b_spec]
    for e, e_spec in extras:
        args.append(e)
        in_specs.append(e_spec)
    return pl.pallas_call(
        body,
        name=name,
        grid=grid,
        in_specs=in_specs,
        out_specs=[s for _, s in outs],
        out_shape=[o for o, _ in outs],
        scratch_shapes=[pltpu.VMEM(acc_shape, F32) for _ in range(n_acc)],
        compiler_params=_params(),
    )(*args)


def _sds(shape, dtype):
    return jax.ShapeDtypeStruct(shape, dtype)


def _accumulate(ref, value, first):
    @pl.when(first)
    def _():
        ref[...] = value

    @pl.when(jnp.logical_not(first))
    def _():
        ref[...] += value


def _rms_fwd(name, x, gain):
    t, d = x.shape
    tm = _pick(t, 512, 8)

    def body(x_ref, g_ref, y_ref):
        xv = x_ref[...]
        rstd = lax.rsqrt(jnp.mean(xv * xv, axis=-1, keepdims=True) + EPS)
        y_ref[...] = (xv * rstd * g_ref[...]).astype(BF16)

    return pl.pallas_call(
        body, name=name, grid=(t // tm,),
        in_specs=[pl.BlockSpec((tm, d), lambda i: (i, 0)), pl.BlockSpec((1, d), lambda i: (0, 0))],
        out_specs=pl.BlockSpec((tm, d), lambda i: (i, 0)),
        out_shape=_sds((t, d), BF16),
        compiler_params=_params(),
    )(x, gain)


def _rms_bwd_epilogue(accs, extras, outs, ij):
    x_ref, g_ref, r_ref = extras
    dh_ref, dhb_ref, dg_ref = outs
    dn = accs[0]
    xv = x_ref[...]
    rstd = lax.rsqrt(jnp.mean(xv * xv, axis=-1, keepdims=True) + EPS)
    xhat = xv * rstd
    gd = dn * g_ref[...]
    dx = rstd * (gd - xhat * jnp.mean(gd * xhat, axis=-1, keepdims=True))
    dh = r_ref[...] + dx
    dh_ref[...] = dh
    dhb_ref[...] = dh.astype(BF16)
    _accumulate(dg_ref, jnp.sum(dn * xhat, axis=0, keepdims=True), ij[0] == 0)


def _rms_bwd_io(x, gain, dres, tm):
    t, d = x.shape
    row = pl.BlockSpec((tm, d), lambda i, j, k: (i, 0))
    extras = [(x, row), (gain, pl.BlockSpec((1, d), lambda i, j, k: (0, 0))), (dres, row)]
    outs = [(_sds((t, d), F32), row), (_sds((t, d), BF16), row),
            (_sds((1, d), F32), pl.BlockSpec((1, d), lambda i, j, k: (0, 0)))]
    return extras, outs


def _ffn_fwd(tag, h, gain, wgu, wd):
    t, d = h.shape
    fs = wgu.shape[2]
    f = 2 * fs
    tm = _pick(t, 512, 8)
    n = _rms_fwd(tag + "_norm", h, gain)

    def up_epilogue(accs, extras, outs, ij):
        g, u = accs
        gu_ref, a_ref = outs
        gu_ref[0] = g.astype(BF16)
        gu_ref[1] = u.astype(BF16)
        a_ref[...] = (g * jax.nn.sigmoid(g) * u).astype(BF16)

    a_spec = pl.BlockSpec((tm, d), lambda i, j, k: (i, 0))
    gu, a = _mm(
        tag + "_up", "nn", (t // tm, 2, 1),
        [(n, a_spec, wgu, pl.BlockSpec((None, d, fs), lambda i, j, k: (j, 0, 0))),
         (n, a_spec, wgu, pl.BlockSpec((None, d, fs), lambda i, j, k: (j + 2, 0, 0)))],
        [],
        [(_sds((2, t, f), BF16), pl.BlockSpec((2, tm, fs), lambda i, j, k: (0, i, j))),
         (_sds((t, f), BF16), pl.BlockSpec((tm, fs), lambda i, j, k: (i, j)))],
        (tm, fs), up_epilogue)

    def down_epilogue(accs, extras, outs, ij):
        outs[0][...] = extras[0][...] + 0.5 * accs[0]

    row = pl.BlockSpec((tm, d), lambda i, j, k: (i, 0))
    (h_new,) = _mm(
        tag + "_down", "nn", (t // tm, 1, 2),
        [(a, pl.BlockSpec((tm, fs), lambda i, j, k: (i, k)), wd, pl.BlockSpec((fs, d), lambda i, j, k: (k, 0)))],
        [(h, row)], [(_sds((t, d), F32), row)], (tm, d), down_epilogue)
    return h_new, (n, gu, a)


def _ffn_bwd(tag, dh, dh_b, h, gain, wgu, wd, saved):
    n, gu, a = saved
    t, d = h.shape
    fs = wgu.shape[2]
    f = 2 * fs
    tm = _pick(t, 512, 8)
    tk = _pick(t, 512, 8)

    def dact_epilogue(accs, extras, outs, ij):
        da = 0.5 * accs[0]
        g = extras[0][0].astype(F32)
        u = extras[0][1].astype(F32)
        sg = jax.nn.sigmoid(g)
        outs[0][0] = (da * u * sg * (1.0 + g * (1.0 - sg))).astype(BF16)
        outs[0][1] = (da * g * sg).astype(BF16)

    gu_spec = pl.BlockSpec((2, tm, fs), lambda i, j, k: (0, i, j))
    (dgu,) = _mm(
        tag + "_dact", "nt", (t // tm, 2, 1),
        [(dh_b, pl.BlockSpec((tm, d), lambda i, j, k: (i, 0)), wd, pl.BlockSpec((fs, d), lambda i, j, k: (j, 0)))],
        [(gu, gu_spec)], [(_sds((2, t, f), BF16), gu_spec)], (tm, fs), dact_epilogue)

    def half_epilogue(accs, extras, outs, ij):
        outs[0][...] = (0.5 * accs[0]).astype(BF16)

    (dwd,) = _mm(
        tag + "_dwd", "tn", (2, 1, t // tk),
        [(a, pl.BlockSpec((tk, fs), lambda i, j, k: (k, i)), dh_b, pl.BlockSpec((tk, d), lambda i, j, k: (k, 0)))],
        [], [(_sds((f, d), BF16), pl.BlockSpec((fs, d), lambda i, j, k: (i, 0)))], (fs, d), half_epilogue)

    def cast_epilogue(accs, extras, outs, ij):
        outs[0][...] = accs[0].astype(BF16)

    (dwgu,) = _mm(
        tag + "_dwgu", "tn", (1, 4, t // tk),
        [(n, pl.BlockSpec((tk, d), lambda i, j, k: (k, 0)),
          dgu, pl.BlockSpec((None, tk, fs), lambda i, j, k: (j // 2, k, j % 2)))],
        [], [(_sds((4, d, fs), BF16), pl.BlockSpec((None, d, fs), lambda i, j, k: (j, 0, 0)))], (d, fs), cast_epilogue)

    extras, outs = _rms_bwd_io(h, gain, dh, tm)
    dh_in, dh_in_b, dgain = _mm(
        tag + "_dnorm", "nt", (t // tm, 1, 4),
        [(dgu, pl.BlockSpec((None, tm, fs), lambda i, j, k: (k // 2, i, k % 2)),
          wgu, pl.BlockSpec((None, d, fs), lambda i, j, k: (k, 0, 0)))],
        extras, outs, (tm, d), _rms_bwd_epilogue)
    return dh_in, dh_in_b, dgain, dwgu, dwd


def _lane_lo(shape):
    return lax.broadcasted_iota(jnp.int32, shape, 1) < HEAD_DIM


def _pair_norm(xv, gain):
    lo = _lane_lo(xv.shape)
    x2 = xv * xv
    ms_lo = jnp.sum(jnp.where(lo, x2, 0.0), axis=-1, keepdims=True) * (1.0 / HEAD_DIM)
    ms_hi = jnp.sum(jnp.where(lo, 0.0, x2), axis=-1, keepdims=True) * (1.0 / HEAD_DIM)
    rstd = jnp.where(lo, lax.rsqrt(ms_lo + EPS), lax.rsqrt(ms_hi + EPS))
    xhat = xv * rstd
    return xhat * gain, xhat, rstd


def _pair_norm_bwd(dn, xhat, rstd, gain):
    lo = _lane_lo(dn.shape)
    gd = dn * gain
    t = gd * xhat
    m_lo = jnp.sum(jnp.where(lo, t, 0.0), axis=-1, keepdims=True) * (1.0 / HEAD_DIM)
    m_hi = jnp.sum(jnp.where(lo, 0.0, t), axis=-1, keepdims=True) * (1.0 / HEAD_DIM)
    dx = rstd * (gd - xhat * jnp.where(lo, m_lo, m_hi))
    return dx, jnp.sum(dn * xhat, axis=0, keepdims=True)


def _half(xv, hi):
    lo = _lane_lo(xv.shape)
    return jnp.where(lo, 0, xv) if hi else jnp.where(lo, xv, 0)


def _head_place(h, group):
    kh = h // group
    return h // 2, h % 2, kh // 2, kh % 2


def _attn_window(i, prev):
    q0 = i * Q_BLOCK
    start = jnp.maximum(q0 - prev, 0)
    off = start - (q0 - prev)
    return pl.multiple_of(start, Q_BLOCK), pl.multiple_of(off, Q_BLOCK)


def _attn_specs(cfg, s, nq):
    kw = cfg["kw"]
    q_spec = pl.BlockSpec((Q_BLOCK, A_WIDTH), lambda b, i: (b * nq + i, cfg["qblk"]))
    k_spec = pl.BlockSpec((s, kw), lambda b, i: (b, cfg["kblk"]))
    v_spec = pl.BlockSpec((s, kw), lambda b, i: (b, cfg["vblk"]))
    return q_spec, k_spec, v_spec


def _const_spec(shape):
    return pl.BlockSpec(shape, lambda b, i: (0,) * len(shape))


def _attn_fwd(name, qkv, bias_t, sink, gq, gk, cfg, n_batch):
    t = qkv.shape[0]
    s = t // n_batch
    nq = s // Q_BLOCK
    prev, group, kw = cfg["prev"], cfg["group"], cfg["kw"]
    w = prev + Q_BLOCK
    wext = bias_t.shape[1]

    def body(q_ref, k_ref, v_ref, bias_ref, sink_ref, gq_ref, gk_ref, y_ref, lse_ref, kn_ref):
        i = pl.program_id(1)

        @pl.when(i == 0)
        def _():
            for jk in range(kw // PAIR):
                kn, _, _ = _pair_norm(k_ref[:, pl.ds(jk * PAIR, PAIR)].astype(F32), gk_ref[...])
                kn_ref[:, pl.ds(jk * PAIR, PAIR)] = kn.astype(BF16)

        start, off = _attn_window(i, prev)
        sub = lax.broadcasted_iota(jnp.int32, (N_HEADS, Q_BLOCK), 0)
        lse = jnp.zeros((N_HEADS, Q_BLOCK), F32)
        for jq in range(N_HEADS // 2):
            qn, _, _ = _pair_norm(q_ref[:, pl.ds(jq * PAIR, PAIR)].astype(F32), gq_ref[...])
            qn = qn * ATTN_SCALE
            o_pair = jnp.zeros((Q_BLOCK, PAIR), F32)
            for hq in range(2):
                h = 2 * jq + hq
                _, _, jk, hk = _head_place(h, group)
                qm = _half(qn, hq)
                if hq != hk:
                    qm = pltpu.roll(qm, HEAD_DIM, 1)
                k_w = kn_ref[pl.ds(start, w), pl.ds(jk * PAIR, PAIR)]
                st = _dot(k_w, qm, "nt") + bias_ref[h, pl.ds(off, w), :]
                sk = sink_ref[h:h + 1, 0:1]
                m = jnp.maximum(jnp.max(st, axis=0, keepdims=True), sk)
                p = jnp.exp(st - m)
                l = jnp.sum(p, axis=0, keepdims=True) + jnp.exp(sk - m)
                v_w = _half(v_ref[pl.ds(start, w), pl.ds(jk * PAIR, PAIR)], hk)
                o = _dot(p * (1.0 / l), v_w, "tn")
                if hq != hk:
                    o = pltpu.roll(o, HEAD_DIM, 1)
                o_pair = o_pair + o
                lse = jnp.where(sub == h, m + jnp.log(l), lse)
            y_ref[:, pl.ds(jq * PAIR, PAIR)] = o_pair.astype(BF16)
        lse_ref[...] = lse

    q_spec, k_spec, v_spec = _attn_specs(cfg, s, nq)
    return pl.pallas_call(
        body, name=name, grid=(n_batch, nq),
        in_specs=[q_spec, k_spec, v_spec, _const_spec((N_HEADS, wext, Q_BLOCK)), _const_spec((N_HEADS, 128)),
                  _const_spec((1, PAIR)), _const_spec((1, PAIR))],
        out_specs=[pl.BlockSpec((Q_BLOCK, A_WIDTH), lambda b, i: (b * nq + i, 0)),
                   pl.BlockSpec((None, N_HEADS, Q_BLOCK), lambda b, i: (b * nq + i, 0, 0))],
        out_shape=[_sds((t, A_WIDTH), BF16), _sds((t // Q_BLOCK, N_HEADS, Q_BLOCK), F32)],
        scratch_shapes=[pltpu.VMEM((s, kw), BF16)],
        compiler_params=_params(),
    )(qkv, qkv, qkv, bias_t, sink, gq, gk)


def _attn_bwd(name, qkv, bias_t, sink, gq, gk, y, dy, lse, cfg, n_batch, want_dbias):
    t = qkv.shape[0]
    s = t // n_batch
    nq = s // Q_BLOCK
    prev, group, kw = cfg["prev"], cfg["group"], cfg["kw"]
    w = prev + Q_BLOCK
    wext = bias_t.shape[1]

    def body(q_ref, k_ref, v_ref, bias_ref, sink_ref, gq_ref, gk_ref, y_ref, dy_ref, lse_ref,
             dq_ref, dk_ref, dv_ref, db_ref, dsink_ref, dgq_ref, dgk_ref, kn_ref, dkn_ref, dvs_ref):
        b = pl.program_id(0)
        i = pl.program_id(1)
        first = (b == 0) & (i == 0)

        @pl.when(i == 0)
        def _():
            for jk in range(kw // PAIR):
                kn, _, _ = _pair_norm(k_ref[:, pl.ds(jk * PAIR, PAIR)].astype(F32), gk_ref[...])
                kn_ref[:, pl.ds(jk * PAIR, PAIR)] = kn.astype(BF16)
            dkn_ref[...] = jnp.zeros(dkn_ref.shape, F32)
            dvs_ref[...] = jnp.zeros(dvs_ref.shape, F32)

        @pl.when(first)
        def _():
            db_ref[...] = jnp.zeros(db_ref.shape, F32)
            dsink_ref[...] = jnp.zeros(dsink_ref.shape, F32)
            dgq_ref[...] = jnp.zeros(dgq_ref.shape, F32)
            dgk_ref[...] = jnp.zeros(dgk_ref.shape, F32)

        start, off = _attn_window(i, prev)
        sub_lo = lax.broadcasted_iota(jnp.int32, (PAIR, Q_BLOCK), 0) < HEAD_DIM
        for jq in range(N_HEADS // 2):
            cols = pl.ds(jq * PAIR, PAIR)
            qn, q_hat, q_rstd = _pair_norm(q_ref[:, cols].astype(F32), gq_ref[...])
            qn = qn * ATTN_SCALE
            do_pair = dy_ref[:, cols]
            prod_t = (do_pair.astype(F32) * y_ref[:, cols].astype(F32)).T
            dqn = jnp.zeros((Q_BLOCK, PAIR), F32)
            for hq in range(2):
                h = 2 * jq + hq
                _, _, jk, hk = _head_place(h, group)
                kcols = pl.ds(jk * PAIR, PAIR)
                delta = jnp.sum(jnp.where(sub_lo, 0.0, prod_t) if hq else jnp.where(sub_lo, prod_t, 0.0),
                                axis=0, keepdims=True)
                qm = _half(qn, hq)
                do_m = _half(do_pair, hq)
                if hq != hk:
                    qm = pltpu.roll(qm, HEAD_DIM, 1)
                    do_m = pltpu.roll(do_m.astype(F32), HEAD_DIM, 1)
                qm_b = qm.astype(BF16)
                do_b = do_m.astype(BF16)
                k_w = kn_ref[pl.ds(start, w), kcols]
                v_w = v_ref[pl.ds(start, w), kcols]
                lse_row = lse_ref[h:h + 1, :]
                st = _dot(k_w, qm_b, "nt") + bias_ref[h, pl.ds(off, w), :]
                p = jnp.exp(st - lse_row)
                dp = _dot(v_w, do_b, "nt")
                ds = p * (dp - delta)
                dsink_ref[h:h + 1, :] += -jnp.exp(sink_ref[h:h + 1, 0:1] - lse_row) * delta
                if want_dbias:
                    db_ref[h, pl.ds(off, w), :] += ds
                ds_b = ds.astype(BF16)
                dq_h = _half(_dot(ds_b, k_w, "tn"), hk)
                if hq != hk:
                    dq_h = pltpu.roll(dq_h, HEAD_DIM, 1)
                dqn = dqn + dq_h
                dkn_ref[pl.ds(start, w), kcols] += _half(_dot(ds_b, qm_b, "nn"), hk)
                dvs_ref[pl.ds(start, w), kcols] += _half(_dot(p, do_b, "nn"), hk)
            dq_raw, dg = _pair_norm_bwd(dqn * ATTN_SCALE, q_hat, q_rstd, gq_ref[...])
            dq_ref[:, cols] = dq_raw.astype(BF16)
            dgq_ref[...] += dg

        @pl.when(i == nq - 1)
        def _():
            for jk in range(kw // PAIR):
                kcols = pl.ds(jk * PAIR, PAIR)
                _, k_hat, k_rstd = _pair_norm(k_ref[:, kcols].astype(F32), gk_ref[...])
                dk_raw, dg = _pair_norm_bwd(dkn_ref[:, kcols], k_hat, k_rstd, gk_ref[...])
                dk_ref[:, kcols] = dk_raw.astype(BF16)
                dgk_ref[...] += dg
            dv_ref[...] = dvs_ref[...].astype(BF16)

    q_spec, k_spec, v_spec = _attn_specs(cfg, s, nq)
    row = pl.BlockSpec((Q_BLOCK, A_WIDTH), lambda b, i: (b * nq + i, 0))
    kv_out = pl.BlockSpec((s, kw), lambda b, i: (b, 0))
    return pl.pallas_call(
        body, name=name, grid=(n_batch, nq),
        in_specs=[q_spec, k_spec, v_spec, _const_spec((N_HEADS, wext, Q_BLOCK)), _const_spec((N_HEADS, 128)),
                  _const_spec((1, PAIR)), _const_spec((1, PAIR)), row, row,
                  pl.BlockSpec((None, N_HEADS, Q_BLOCK), lambda b, i: (b * nq + i, 0, 0))],
        out_specs=[row, kv_out, kv_out, _const_spec((N_HEADS, wext, Q_BLOCK)), _const_spec((N_HEADS, 128)),
                   _const_spec((1, PAIR)), _const_spec((1, PAIR))],
        out_shape=[_sds((t, A_WIDTH), BF16), _sds((t, kw), BF16), _sds((t, kw), BF16),
                   _sds((N_HEADS, wext, Q_BLOCK), F32), _sds((N_HEADS, 128), F32),
                   _sds((1, PAIR), F32), _sds((1, PAIR), F32)],
        scratch_shapes=[pltpu.VMEM((s, kw), BF16), pltpu.VMEM((s, kw), F32), pltpu.VMEM((s, kw), F32)],
        compiler_params=_params(),
    )(qkv, qkv, qkv, bias_t, sink, gq, gk, y, dy, lse)


def _band_tables(prev_chunks):
    prev = prev_chunks * CHUNK
    wext = 2 * prev + Q_BLOCK
    jj = np.arange(wext)[:, None]
    ii = np.arange(Q_BLOCK)[None, :]
    dist = prev + ii - jj
    rel_chunk = (prev // CHUNK + ii // CHUNK) - jj // CHUNK
    allowed = (rel_chunk >= 0) & (rel_chunk <= prev_chunks)
    return dist, allowed


def _alibi_slopes():
    return np.array([2.0 ** (-8.0 * (h + 1) / N_HEADS) for h in range(N_HEADS)], dtype=np.float32)


def _bias_a(rel_bias):
    dist, allowed = _band_tables(A_PREV_CHUNKS)
    idx = np.clip(dist, -A_MAX_REL, A_MAX_REL) + A_MAX_REL
    return jnp.where(jnp.asarray(allowed)[None], rel_bias[:, idx], NEG_INF)


def _bias_b():
    dist, allowed = _band_tables(B_PREV_CHUNKS)
    bias = -_alibi_slopes()[:, None, None] * np.abs(dist).astype(np.float32)[None]
    return jnp.asarray(np.where(allowed[None], bias, np.float32(NEG_INF)).astype(np.float32))


def _rel_bias_grad(db_t):
    prev = A_PREV_CHUNKS * CHUNK
    wext = db_t.shape[1]
    wp = wext + Q_BLOCK
    xf = jnp.transpose(db_t, (0, 2, 1))[:, ::-1, :]
    xf = jnp.pad(xf, ((0, 0), (0, 0), (0, Q_BLOCK)))
    skew = xf.reshape(N_HEADS, Q_BLOCK * wp)[:, :Q_BLOCK * (wp - 1)].reshape(N_HEADS, Q_BLOCK, wp - 1)
    diag = jnp.sum(skew, axis=1)
    dist = prev + Q_BLOCK - 1 - np.arange(wp - 1)
    idx = np.clip(dist, -A_MAX_REL, A_MAX_REL) + A_MAX_REL
    onehot = np.zeros((wp - 1, 2 * A_MAX_REL + 1), np.float32)
    onehot[np.arange(wp - 1), idx] = 1.0
    return jnp.dot(diag, jnp.asarray(onehot), precision=lax.Precision.HIGHEST)


def _ew(name, fn, ins, out_dtypes):
    r, c = ins[0].shape
    rb = _pick(r, max(16, (1 << 19) // c), 16)
    spec = pl.BlockSpec((rb, c), lambda i: (i, 0))

    def body(*refs):
        vals = fn(*[ref[...] for ref in refs[:len(ins)]])
        for ref, val in zip(refs[len(ins):], vals):
            ref[...] = val.astype(ref.dtype)

    return pl.pallas_call(
        body, name=name, grid=(r // rb,), in_specs=[spec] * len(ins), out_specs=[spec] * len(out_dtypes),
        out_shape=[_sds((r, c), dt) for dt in out_dtypes], compiler_params=_params(),
    )(*ins)


def _adamw_math(w, g, m, v):
    m = ADAM_B1 * m + (1.0 - ADAM_B1) * g
    v = ADAM_B2 * v + (1.0 - ADAM_B2) * (g * g)
    m_hat = m / (1.0 - ADAM_B1 ** ADAM_STEP)
    v_hat = v / (1.0 - ADAM_B2 ** ADAM_STEP)
    delta = -ADAM_LR * (m_hat / (jnp.sqrt(v_hat) + ADAM_EPS) + ADAM_WD * w)
    return delta, m, v


def _adamw_terms(name, terms, w, m, v):
    r, c = w.shape
    hr = r // 2
    rb = _pick(hr, max(16, (1 << 17) // c), 16)
    nb = hr // rb

    def body(t_ref, w_ref, m_ref, v_ref, g_ref, d_ref, nm_ref, nv_ref):
        g = t_ref[0].astype(F32)
        for k in range(1, N_CHIPS):
            g = g + t_ref[k].astype(F32)
        delta, nm, nv = _adamw_math(w_ref[...], g, m_ref[...], v_ref[...])
        g_ref[...] = g
        d_ref[...] = delta
        nm_ref[...] = nm
        nv_ref[...] = nv

    spec = pl.BlockSpec((rb, c), lambda h, i: (h * nb + i, 0))
    return pl.pallas_call(
        body, name=name, grid=(2, nb),
        in_specs=[pl.BlockSpec((None, N_CHIPS, rb, c), lambda h, i: (h, 0, i, 0)), spec, spec, spec],
        out_specs=[spec] * 4, out_shape=[_sds((r, c), F32)] * 4, compiler_params=_params(),
    )(terms, w, m, v)


def _mesh_place():
    x, y, c = lax.axis_index("x"), lax.axis_index("y"), lax.axis_index("c")
    chips = [(x, 1 - y), (1 - x, y), (1 - x, 1 - y)]
    return x, y, c, chips


def _all_gather_weights(shards):
    n = len(shards)

    def body(*refs):
        ins, outs = refs[:n], refs[n:2 * n]
        local_sem, ici_send, ici_recv, d2d_send, d2d_recv = refs[2 * n:]
        x, y, c, chips = _mesh_place()
        me = 2 * x + y
        sibling = (x, y, 1 - c)
        local, sent = [], []
        for wi in range(n):
            hr = ins[wi].shape[0] // 2
            mine = pl.ds(c * hr, hr)
            loc = pltpu.make_async_copy(ins[wi], outs[wi].at[me], local_sem.at[wi])
            loc.start()
            local.append(loc)
            for k, (tx, ty) in enumerate(chips):
                cp = pltpu.make_async_remote_copy(
                    src_ref=ins[wi].at[mine, :], dst_ref=outs[wi].at[me, mine, :],
                    send_sem=ici_send.at[wi * 3 + k], recv_sem=ici_recv.at[wi * 3 + k],
                    device_id=(tx, ty, c), device_id_type=MESH)
                cp.start()
                sent.append(cp)
        passed = []
        for wi in range(n):
            hr = ins[wi].shape[0] // 2
            mine = pl.ds(c * hr, hr)
            for k, (tx, ty) in enumerate(chips):
                slab = outs[wi].at[2 * tx + ty, mine, :]
                pltpu.make_async_remote_copy(
                    src_ref=slab, dst_ref=slab, send_sem=ici_send.at[wi * 3 + k], recv_sem=ici_recv.at[wi * 3 + k],
                    device_id=(tx, ty, c), device_id_type=MESH).wait_recv()
                fw = pltpu.make_async_remote_copy(
                    src_ref=slab, dst_ref=slab, send_sem=d2d_send.at[wi * 3 + k], recv_sem=d2d_recv.at[wi * 3 + k],
                    device_id=sibling, device_id_type=MESH)
                fw.start()
                passed.append(fw)
        for wi in range(n):
            hr = ins[wi].shape[0] // 2
            theirs = pl.ds((1 - c) * hr, hr)
            for k, (tx, ty) in enumerate(chips):
                slab = outs[wi].at[2 * tx + ty, theirs, :]
                pltpu.make_async_remote_copy(
                    src_ref=slab, dst_ref=slab, send_sem=d2d_send.at[wi * 3 + k], recv_sem=d2d_recv.at[wi * 3 + k],
                    device_id=sibling, device_id_type=MESH).wait_recv()
        for loc in local:
            loc.wait()
        for cp in sent + passed:
            cp.wait_send()

    return pl.pallas_call(
        body, name="all_gather_weights",
        in_specs=[ANY] * n, out_specs=[ANY] * n,
        out_shape=[_sds((N_CHIPS,) + s.shape, s.dtype) for s in shards],
        scratch_shapes=[pltpu.SemaphoreType.DMA((n,)), pltpu.SemaphoreType.DMA((3 * n,)),
                        pltpu.SemaphoreType.DMA((3 * n,)), pltpu.SemaphoreType.DMA((3 * n,)),
                        pltpu.SemaphoreType.DMA((3 * n,))],
    )(*shards)


def _exchange_halves(grads):
    n = len(grads)

    def body(*refs):
        ins, kept, got = refs[:n], refs[n:2 * n], refs[2 * n:3 * n]
        local_sem, send_sem, recv_sem = refs[3 * n:]
        x, y, c, _ = _mesh_place()
        copies = []
        for wi in range(n):
            hr = ins[wi].shape[1] // 2
            loc = pltpu.make_async_copy(ins[wi].at[:, pl.ds(c * hr, hr), :], kept[wi], local_sem.at[wi])
            cp = pltpu.make_async_remote_copy(
                src_ref=ins[wi].at[:, pl.ds((1 - c) * hr, hr), :], dst_ref=got[wi],
                send_sem=send_sem.at[wi], recv_sem=recv_sem.at[wi],
                device_id=(x, y, 1 - c), device_id_type=MESH)
            loc.start()
            cp.start()
            copies.append((loc, cp))
        for loc, cp in copies:
            loc.wait()
            cp.wait()

    half = [_sds((N_CHIPS, g.shape[1] // 2, g.shape[2]), g.dtype) for g in grads]
    res = pl.pallas_call(
        body, name="grad_exchange_halves",
        in_specs=[ANY] * n, out_specs=[ANY] * (2 * n), out_shape=half + half,
        scratch_shapes=[pltpu.SemaphoreType.DMA((n,)), pltpu.SemaphoreType.DMA((n,)), pltpu.SemaphoreType.DMA((n,))],
    )(*grads)
    return res[:n], res[n:]


def _scatter_chip_sums(sums):
    n = len(sums)

    def body(*refs):
        ins, outs = refs[:n], refs[n:2 * n]
        local_sem, ici_send, ici_recv, d2d_send, d2d_recv = refs[2 * n:]
        x, y, c, chips = _mesh_place()
        me = 2 * x + y
        sibling = (x, y, 1 - c)
        local, sent = [], []
        for wi in range(n):
            loc = pltpu.make_async_copy(ins[wi].at[me], outs[wi].at[c, 0], local_sem.at[wi])
            loc.start()
            local.append(loc)
            for k, (tx, ty) in enumerate(chips):
                cp = pltpu.make_async_remote_copy(
                    src_ref=ins[wi].at[2 * tx + ty], dst_ref=outs[wi].at[c, k + 1],
                    send_sem=ici_send.at[wi * 3 + k], recv_sem=ici_recv.at[wi * 3 + k],
                    device_id=(tx, ty, c), device_id_type=MESH)
                cp.start()
                sent.append(cp)
        for wi in range(n):
            local[wi].wait()
            for k in range(N_CHIPS):
                slab = outs[wi].at[c, k]
                if k > 0:
                    tx, ty = chips[k - 1]
                    pltpu.make_async_remote_copy(
                        src_ref=slab, dst_ref=slab, send_sem=ici_send.at[wi * 3 + k - 1],
                        recv_sem=ici_recv.at[wi * 3 + k - 1], device_id=(tx, ty, c), device_id_type=MESH).wait_recv()
                fw = pltpu.make_async_remote_copy(
                    src_ref=slab, dst_ref=slab, send_sem=d2d_send.at[wi * 4 + k], recv_sem=d2d_recv.at[wi * 4 + k],
                    device_id=sibling, device_id_type=MESH)
                fw.start()
                sent.append(fw)
        for wi in range(n):
            for k in range(N_CHIPS):
                slab = outs[wi].at[1 - c, k]
                pltpu.make_async_remote_copy(
                    src_ref=slab, dst_ref=slab, send_sem=d2d_send.at[wi * 4 + k], recv_sem=d2d_recv.at[wi * 4 + k],
                    device_id=sibling, device_id_type=MESH).wait_recv()
        for cp in sent:
            cp.wait_send()

    return pl.pallas_call(
        body, name="grad_scatter_chip_sums",
        in_specs=[ANY] * n, out_specs=[ANY] * n,
        out_shape=[_sds((2, N_CHIPS) + s.shape[1:], s.dtype) for s in sums],
        scratch_shapes=[pltpu.SemaphoreType.DMA((n,)), pltpu.SemaphoreType.DMA((3 * n,)),
                        pltpu.SemaphoreType.DMA((3 * n,)), pltpu.SemaphoreType.DMA((4 * n,)),
                        pltpu.SemaphoreType.DMA((4 * n,))],
    )(*sums)


def _all_reduce_small(pack):
    r = pack.shape[0]

    def body(p_ref, o_ref, land_ref, send_sem, recv_sem):
        x, y, c, _ = _mesh_place()
        me = 4 * x + 2 * y + c
        flips = [(k >> 2 & 1, k >> 1 & 1, k & 1) for k in range(1, N_DEV)]

        def peer(fx, fy, fc):
            return (1 - x if fx else x, 1 - y if fy else y, 1 - c if fc else c)

        land_ref[me] = p_ref[...]
        sent = []
        for k, flip in enumerate(flips):
            cp = pltpu.make_async_remote_copy(
                src_ref=p_ref, dst_ref=land_ref.at[me], send_sem=send_sem.at[k], recv_sem=recv_sem.at[k],
                device_id=peer(*flip), device_id_type=MESH)
            cp.start()
            sent.append(cp)
        for k, flip in enumerate(flips):
            px, py, pc = peer(*flip)
            slot = land_ref.at[4 * px + 2 * py + pc]
            pltpu.make_async_remote_copy(
                src_ref=slot, dst_ref=slot, send_sem=send_sem.at[k], recv_sem=recv_sem.at[k],
                device_id=(px, py, pc), device_id_type=MESH).wait_recv()
        total = land_ref[0]
        for d in range(1, N_DEV):
            total = total + land_ref[d]
        o_ref[...] = total
        for cp in sent:
            cp.wait_send()

    vmem = pl.BlockSpec(memory_space=pltpu.VMEM)
    return pl.pallas_call(
        body, name="all_reduce_small", in_specs=[vmem], out_specs=vmem, out_shape=_sds((r, 128), F32),
        scratch_shapes=[pltpu.VMEM((N_DEV, r, 128), F32), pltpu.SemaphoreType.DMA((N_DEV - 1,)),
                        pltpu.SemaphoreType.DMA((N_DEV - 1,))],
    )(pack)


PACK_TILE = 8 * 128


def _pack(items):
    rows = []
    for it in items:
        flat = it.reshape(-1).astype(F32)
        pad = -flat.shape[0] % PACK_TILE
        rows.append(jnp.pad(flat, (0, pad)).reshape(-1, 128))
    return jnp.concatenate(rows, axis=0)


def _unpack(pack, shapes):
    out, row = [], 0
    for shp in shapes:
        size = int(np.prod(shp))
        nrow = -(-size // PACK_TILE) * (PACK_TILE // 128)
        out.append(pack[row:row + nrow].reshape(-1)[:size].reshape(shp))
        row += nrow
    return out


BIG = ["ffn1_w_gu", "ffn1_w_down", "w_in", "w_gate", "w_proj_a", "w_proj_b", "w_out",
       "ffn2_w_gu", "ffn2_w_down", "w_ple_gate", "w_ple_proj"]
SMALL = ["ffn1_norm", "mix_norm", "ffn2_norm", "ple_norm", "a_q_norm", "a_k_norm", "b_q_norm", "b_k_norm",
         "a_rel_bias", "b_sinks"]
WEIGHTS = ["ffn1_norm", "ffn1_w_gu", "ffn1_w_down", "mix_norm", "w_in", "a_q_norm", "a_k_norm", "a_rel_bias",
           "b_q_norm", "b_k_norm", "b_sinks", "w_gate", "w_proj_a", "w_proj_b", "w_out", "ffn2_norm",
           "ffn2_w_gu", "ffn2_w_down", "ple_norm", "w_ple_gate", "w_ple_proj"]
ATTN_A = dict(prev=A_PREV_CHUNKS * CHUNK, group=1, kw=A_WIDTH, qblk=0, kblk=1, vblk=2)
ATTN_B = dict(prev=B_PREV_CHUNKS * CHUNK, group=N_HEADS // B_KV_HEADS, kw=B_KV_WIDTH, qblk=3,
              kblk=4 * A_WIDTH // B_KV_WIDTH, vblk=4 * A_WIDTH // B_KV_WIDTH + 1)


def _cast_epilogue(accs, extras, outs, ij):
    for acc, out in zip(accs, outs):
        out[...] = acc.astype(out.dtype)


def _local_step(xt, pt, tgt, n_batch, w, small):
    t, d = xt.shape
    tm = _pick(t, 512, 8)
    tk = _pick(t, 512, 8)
    nt = t // tm
    row = pl.BlockSpec((tm, d), lambda i, j, k: (i, 0))
    wgate, wpa, wpb, wpe = w["w_gate"], w["w_proj_a"], w["w_proj_b"], w["w_ple_proj"]
    w_in, wout, wpg = w["w_in"], w["w_out"], w["w_ple_gate"]
    gs = wgate.shape[2]
    ps = wpa.shape[2]
    es = wpe.shape[2]
    pdim = pt.shape[1]
    ncols = w_in.shape[1]
    tin = ncols // 2
    assert 2 * gs == d and 4 * ps == d and 4 * es == d and tin % 128 == 0

    h1, ffn1_saved = _ffn_fwd("ffn1", xt, small["ffn1_norm"], w["ffn1_w_gu"], w["ffn1_w_down"])
    un = _rms_fwd("mix_norm", h1, small["mix_norm"])
    (qkv,) = _mm(
        "qkv", "nn", (nt, 2, 1),
        [(un, row, w_in, pl.BlockSpec((d, tin), lambda i, j, k: (0, j)))], [],
        [(_sds((t, ncols), BF16), pl.BlockSpec((tm, tin), lambda i, j, k: (i, j)))], (tm, tin), _cast_epilogue)

    def gate_epilogue(accs, extras, outs, ij):
        outs[0][...] = jax.nn.sigmoid(accs[0]).astype(BF16)

    (gates,) = _mm(
        "gate", "nn", (nt, 4, 1),
        [(un, row, wgate, pl.BlockSpec((None, d, gs), lambda i, j, k: (j, 0, 0)))], [],
        [(_sds((2, t, d), BF16), pl.BlockSpec((None, tm, gs), lambda i, j, k: (j // 2, i, j % 2)))],
        (tm, gs), gate_epilogue)

    bias_a = _bias_a(small["a_rel_bias"][0])
    bias_b = _bias_b()
    sink_a = jnp.full((N_HEADS, 128), NEG_INF, F32)
    sink_b = jnp.broadcast_to(small["b_sinks"][0][:, None], (N_HEADS, 128))
    gqa, gka, gqb, gkb = [jnp.tile(small[k], (1, 2)) for k in ("a_q_norm", "a_k_norm", "b_q_norm", "b_k_norm")]
    ya, lse_a = _attn_fwd("attn_a_fwd", qkv, bias_a, sink_a, gqa, gka, ATTN_A, n_batch)
    yb, lse_b = _attn_fwd("attn_b_fwd", qkv, bias_b, sink_b, gqb, gkb, ATTN_B, n_batch)

    def merge_epilogue(accs, extras, outs, ij):
        pa, pb = accs
        outs[0][...] = (extras[0][...].astype(F32) * pa + extras[1][...].astype(F32) * pb).astype(BF16)
        outs[1][...] = pa.astype(BF16)
        outs[2][...] = pb.astype(BF16)

    y_spec = pl.BlockSpec((tm, A_WIDTH), lambda i, j, k: (i, 0))
    proj_spec = pl.BlockSpec((None, A_WIDTH, ps), lambda i, j, k: (j, 0, 0))
    tile_ps = pl.BlockSpec((tm, ps), lambda i, j, k: (i, j))
    merged, pa, pb = _mm(
        "proj_merge", "nn", (nt, 4, 1),
        [(ya, y_spec, wpa, proj_spec), (yb, y_spec, wpb, proj_spec)],
        [(gates, pl.BlockSpec((None, tm, ps), lambda i, j, k: (0, i, j))),
         (gates, pl.BlockSpec((None, tm, ps), lambda i, j, k: (1, i, j)))],
        [(_sds((t, d), BF16), tile_ps)] * 3, (tm, ps), merge_epilogue)

    def residual_epilogue(accs, extras, outs, ij):
        outs[0][...] = extras[0][...] + accs[0]

    (h2,) = _mm(
        "out_proj", "nn", (nt, 1, 1),
        [(merged, row, wout, pl.BlockSpec((d, d), lambda i, j, k: (0, 0)))],
        [(h1, row)], [(_sds((t, d), F32), row)], (tm, d), residual_epilogue)

    h3, ffn2_saved = _ffn_fwd("ffn2", h2, small["ffn2_norm"], w["ffn2_w_gu"], w["ffn2_w_down"])
    n3 = _rms_fwd("ple_norm", h3, small["ple_norm"])
    tile_es = pl.BlockSpec((tm, es), lambda i, j, k: (i, j))
    (pe,) = _mm(
        "ple_embed", "nn", (nt, 4, 1),
        [(pt, pl.BlockSpec((tm, pdim), lambda i, j, k: (i, 0)), wpe, pl.BlockSpec((None, pdim, es), lambda i, j, k: (j, 0, 0)))],
        [], [(_sds((t, d), F32), tile_es)], (tm, es), _cast_epilogue)

    th = _pick(d, 512)

    def head_epilogue(accs, extras, outs, ij):
        h3_ref, pe_ref, tgt_ref = extras
        dy_ref, dpe_ref, dz_ref, loss_ref = outs
        pg = jax.nn.sigmoid(accs[0])
        pev = pe_ref[...]
        diff = h3_ref[...] + pg * pev - tgt_ref[...]
        dy = diff * (1.0 / d)
        dy_ref[...] = dy
        dpe_ref[...] = (dy * pg).astype(BF16)
        dz_ref[...] = (dy * pev * pg * (1.0 - pg)).astype(BF16)
        _accumulate(loss_ref, jnp.full(loss_ref.shape, jnp.sum(diff * diff), F32), (ij[0] == 0) & (ij[1] == 0))

    tile_h = pl.BlockSpec((tm, th), lambda i, j, k: (i, j))
    dy, dpe, dz, loss_acc = _mm(
        "ple_gate_loss", "nn", (nt, d // th, 1),
        [(n3, row, wpg, pl.BlockSpec((d, th), lambda i, j, k: (0, j)))],
        [(h3, tile_h), (pe, tile_h), (tgt, tile_h)],
        [(_sds((t, d), F32), tile_h), (_sds((t, d), BF16), tile_h), (_sds((t, d), BF16), tile_h),
         (_sds((8, 128), F32), pl.BlockSpec((8, 128), lambda i, j, k: (0, 0)))],
        (tm, th), head_epilogue)
    loss = 0.5 * loss_acc[0, 0] / d

    nk = t // tk
    (dwpe,) = _mm(
        "d_w_ple_proj", "tn", (1, 4, nk),
        [(pt, pl.BlockSpec((tk, pdim), lambda i, j, k: (k, 0)), dpe, pl.BlockSpec((tk, es), lambda i, j, k: (k, j)))],
        [], [(_sds((4, pdim, es), BF16), pl.BlockSpec((None, pdim, es), lambda i, j, k: (j, 0, 0)))],
        (pdim, es), _cast_epilogue)

    def dense_grad(name, a, dyb):
        (res,) = _mm(
            name, "tn", (1, d // th, nk),
            [(a, pl.BlockSpec((tk, d), lambda i, j, k: (k, 0)), dyb, pl.BlockSpec((tk, th), lambda i, j, k: (k, j)))],
            [], [(_sds((d, d), BF16), pl.BlockSpec((d, th), lambda i, j, k: (0, j)))], (d, th), _cast_epilogue)
        return res

    dwpg = dense_grad("d_w_ple_gate", n3, dz)
    extras, outs = _rms_bwd_io(h3, small["ple_norm"], dy, tm)
    dh3, dh3_b, d_ple_norm = _mm(
        "d_ple_norm", "nt", (nt, 1, d // th),
        [(dz, pl.BlockSpec((tm, th), lambda i, j, k: (i, k)), wpg, pl.BlockSpec((d, th), lambda i, j, k: (0, k)))],
        extras, outs, (tm, d), _rms_bwd_epilogue)

    dh2, dh2_b, d_ffn2_norm, dwgu2, dwd2 = _ffn_bwd(
        "ffn2", dh3, dh3_b, h2, small["ffn2_norm"], w["ffn2_w_gu"], w["ffn2_w_down"], ffn2_saved)

    def dmerge_epilogue(accs, extras, outs, ij):
        dmo = accs[0]
        g_ref, pa_ref, pb_ref = extras
        dg_ref, dpa_ref, dpb_ref = outs
        ga = g_ref[0].astype(F32)
        gb = g_ref[1].astype(F32)
        dg_ref[0] = (dmo * pa_ref[...].astype(F32) * ga * (1.0 - ga)).astype(BF16)
        dg_ref[1] = (dmo * pb_ref[...].astype(F32) * gb * (1.0 - gb)).astype(BF16)
        dpa_ref[...] = (dmo * ga).astype(BF16)
        dpb_ref[...] = (dmo * gb).astype(BF16)

    g_spec = pl.BlockSpec((2, tm, th), lambda i, j, k: (0, i, j))
    dgates, dpa, dpb = _mm(
        "d_merge", "nt", (nt, d // th, 1),
        [(dh2_b, row, wout, pl.BlockSpec((th, d), lambda i, j, k: (j, 0)))],
        [(gates, g_spec), (pa, tile_h), (pb, tile_h)],
        [(_sds((2, t, d), BF16), g_spec), (_sds((t, d), BF16), tile_h), (_sds((t, d), BF16), tile_h)],
        (tm, th), dmerge_epilogue)
    dwout = dense_grad("d_w_out", merged, dh2_b)

    yk_spec = pl.BlockSpec((tk, A_WIDTH), lambda i, j, k: (k, 0))
    dk_spec = pl.BlockSpec((tk, ps), lambda i, j, k: (k, j))
    dproj = (_sds((4, A_WIDTH, ps), BF16), proj_spec)
    dwpa, dwpb = _mm(
        "d_w_proj", "tn", (1, 4, nk),
        [(ya, yk_spec, dpa, dk_spec), (yb, yk_spec, dpb, dk_spec)], [], [dproj, dproj], (A_WIDTH, ps), _cast_epilogue)
    dproj_a = pl.BlockSpec((tm, ps), lambda i, j, k: (i, k))
    wproj_k = pl.BlockSpec((None, A_WIDTH, ps), lambda i, j, k: (k, 0, 0))
    dya, dyb = _mm(
        "d_attn_out", "nt", (nt, 1, 4),
        [(dpa, dproj_a, wpa, wproj_k), (dpb, dproj_a, wpb, wproj_k)], [],
        [(_sds((t, A_WIDTH), BF16), y_spec)] * 2, (tm, A_WIDTH), _cast_epilogue)

    dqa, dka, dva, dbias_a, _, dgqa, dgka = _attn_bwd(
        "attn_a_bwd", qkv, bias_a, sink_a, gqa, gka, ya, dya, lse_a, ATTN_A, n_batch, True)
    dqb, dkb, dvb, _, dsink_b, dgqb, dgkb = _attn_bwd(
        "attn_b_bwd", qkv, bias_b, sink_b, gqb, gkb, yb, dyb, lse_b, ATTN_B, n_batch, False)
    dqkv = jnp.concatenate([dqa, dka, dva, dqb, dkb, dvb], axis=1)

    (dwgate,) = _mm(
        "d_w_gate", "tn", (1, 4, nk),
        [(un, pl.BlockSpec((tk, d), lambda i, j, k: (k, 0)),
          dgates, pl.BlockSpec((None, tk, gs), lambda i, j, k: (j // 2, k, j % 2)))],
        [], [(_sds((4, d, gs), BF16), pl.BlockSpec((None, d, gs), lambda i, j, k: (j, 0, 0)))], (d, gs), _cast_epilogue)
    (dwin,) = _mm(
        "d_w_in", "tn", (1, 2, nk),
        [(un, pl.BlockSpec((tk, d), lambda i, j, k: (k, 0)), dqkv, pl.BlockSpec((tk, tin), lambda i, j, k: (k, j)))],
        [], [(_sds((d, ncols), BF16), pl.BlockSpec((d, tin), lambda i, j, k: (0, j)))], (d, tin), _cast_epilogue)

    extras, outs = _rms_bwd_io(h1, small["mix_norm"], dh2, tm)
    dh1, dh1_b, d_mix_norm = _mm(
        "d_mix_norm", "nt", (nt, 1, 6),
        [(dgates, pl.BlockSpec((None, tm, gs), lambda i, j, k: (jnp.minimum(k, 3) // 2, i, jnp.minimum(k, 3) % 2)),
          wgate, pl.BlockSpec((None, d, gs), lambda i, j, k: (jnp.minimum(k, 3), 0, 0))),
         (dqkv, pl.BlockSpec((tm, tin), lambda i, j, k: (i, jnp.maximum(k - 4, 0))),
          w_in, pl.BlockSpec((d, tin), lambda i, j, k: (0, jnp.maximum(k - 4, 0))))],
        extras, outs, (tm, d), _rms_bwd_epilogue, steps=[4, 2])

    dx, _, d_ffn1_norm, dwgu1, dwd1 = _ffn_bwd(
        "ffn1", dh1, dh1_b, xt, small["ffn1_norm"], w["ffn1_w_gu"], w["ffn1_w_down"], ffn1_saved)

    def fold(v):
        return v[0, :HEAD_DIM] + v[0, HEAD_DIM:]

    big_grads = {"ffn1_w_gu": dwgu1, "ffn1_w_down": dwd1, "w_in": dwin, "w_gate": dwgate, "w_proj_a": dwpa,
                 "w_proj_b": dwpb, "w_out": dwout, "ffn2_w_gu": dwgu2, "ffn2_w_down": dwd2,
                 "w_ple_gate": dwpg, "w_ple_proj": dwpe}
    small_grads = {"ffn1_norm": d_ffn1_norm, "mix_norm": d_mix_norm, "ffn2_norm": d_ffn2_norm,
                   "ple_norm": d_ple_norm, "a_q_norm": fold(dgqa), "a_k_norm": fold(dgka),
                   "b_q_norm": fold(dgqb), "b_k_norm": fold(dgkb), "a_rel_bias": _rel_bias_grad(dbias_a),
                   "b_sinks": jnp.sum(dsink_b, axis=1)}
    return loss, dx, big_grads, small_grads


def kernel(x, p, ffn1_norm, ffn1_w_gu, ffn1_w_down, mix_norm, w_in, a_q_norm, a_k_norm, a_rel_bias, b_q_norm, b_k_norm, b_sinks, w_gate, w_proj_a, w_proj_b, w_out, ffn2_norm, ffn2_w_gu, ffn2_w_down, ple_norm, w_ple_gate, w_ple_proj, loss_target, m_ffn1_norm, m_ffn1_w_gu, m_ffn1_w_down, m_mix_norm, m_w_in, m_a_q_norm, m_a_k_norm, m_a_rel_bias, m_b_q_norm, m_b_k_norm, m_b_sinks, m_w_gate, m_w_proj_a, m_w_proj_b, m_w_out, m_ffn2_norm, m_ffn2_w_gu, m_ffn2_w_down, m_ple_norm, m_w_ple_gate, m_w_ple_proj, v_ffn1_norm, v_ffn1_w_gu, v_ffn1_w_down, v_mix_norm, v_w_in, v_a_q_norm, v_a_k_norm, v_a_rel_bias, v_b_q_norm, v_b_k_norm, v_b_sinks, v_w_gate, v_w_proj_a, v_w_proj_b, v_w_out, v_ffn2_norm, v_ffn2_w_gu, v_ffn2_w_down, v_ple_norm, v_w_ple_gate, v_w_ple_proj):
    given = dict(locals())
    n_batch, s, d = x.shape
    t = n_batch * s
    xt = x.reshape(t, d)
    pt = p.reshape(t, p.shape[-1])
    tgt = loss_target.reshape(t, d)

    shards = []
    for name in BIG:
        (sb,) = _ew("cast_" + name, lambda v: (v,), [given[name][0]], [BF16])
        shards.append(sb)
    gathered = dict(zip(BIG, _all_gather_weights(shards)))
    w = dict(gathered)
    for name in ("ffn1_w_down", "ffn2_w_down", "w_out", "w_ple_gate"):
        g = gathered[name]
        w[name] = g.reshape(N_CHIPS * g.shape[1], g.shape[2])
    g = gathered["w_in"]
    w["w_in"] = jnp.transpose(g, (1, 0, 2)).reshape(g.shape[1], N_CHIPS * g.shape[2])
    small = {name: given[name] for name in SMALL}

    loss, dx, big_grads, small_grads = _local_step(xt, pt, tgt, n_batch, w, small)

    slotted = []
    for name in BIG:
        gshape = gathered[name].shape
        gr = big_grads[name]
        if name == "w_in":
            gr = jnp.transpose(gr.reshape(gshape[1], N_CHIPS, gshape[2]), (1, 0, 2))
        slotted.append(gr.reshape(gshape))
    kept, got = _exchange_halves(slotted)
    sums = []
    for name, a, b in zip(BIG, kept, got):
        shp = a.shape
        (sm,) = _ew("chip_sum_" + name, lambda u, v: (u.astype(F32) + v.astype(F32),),
                    [a.reshape(-1, shp[2]), b.reshape(-1, shp[2])], [BF16])
        sums.append(sm.reshape(shp))
    terms = _scatter_chip_sums(sums)

    grads, deltas, new_m, new_v = {}, {}, {}, {}
    for name, tm_ in zip(BIG, terms):
        gw, dl, nm, nv = _adamw_terms("adamw_" + name, tm_, given[name][0], given["m_" + name][0],
                                      given["v_" + name][0])
        grads[name], deltas[name], new_m[name], new_v[name] = gw[None], dl[None], nm[None], nv[None]

    small_shapes = [given[name].shape for name in SMALL] + [()]
    g_pack = _all_reduce_small(_pack([small_grads[name] for name in SMALL] + [loss]))
    zero = jnp.zeros((), F32)
    w_pack = _pack([given[name] for name in SMALL] + [zero])
    m_pack = _pack([given["m_" + name] for name in SMALL] + [zero])
    v_pack = _pack([given["v_" + name] for name in SMALL] + [zero])
    d_pack, nm_pack, nv_pack = _ew("adamw_small", lambda wv, gv, mv, vv: _adamw_math(wv, gv, mv, vv),
                                   [w_pack, g_pack, m_pack, v_pack], [F32] * 3)
    g_small = _unpack(g_pack, small_shapes)
    loss_total = g_small[-1]
    for name, gv, dv, mv, vv in zip(SMALL, g_small, _unpack(d_pack, small_shapes), _unpack(nm_pack, small_shapes),
                                    _unpack(nv_pack, small_shapes)):
        grads[name], deltas[name], new_m[name], new_v[name] = gv, dv, mv, vv

    return (loss_total, dx.reshape(x.shape), *[grads[n] for n in WEIGHTS], *[deltas[n] for n in WEIGHTS],
            *[new_m[n] for n in WEIGHTS], *[new_v[n] for n in WEIGHTS])
```

```python
import functools

import numpy as np
import jax
import jax.numpy as jnp
from jax import lax
from jax.experimental import pallas as pl
from jax.experimental.pallas import tpu as pltpu

F32 = jnp.float32
BF16 = jnp.bfloat16

CHUNK = 64
HEAD_DIM = 64
A_PREV_CHUNKS = 8
A_MAX_REL = 128
N_HEADS = 8
B_KV_HEADS = 2
B_PREV_CHUNKS = 2
A_WIDTH = N_HEADS * HEAD_DIM
B_KV_WIDTH = B_KV_HEADS * HEAD_DIM
EPS = 1e-6
NEG_INF = -1e30
ATTN_SCALE = HEAD_DIM ** -0.5
Q_BLOCK = 128
PAIR = 2 * HEAD_DIM

ADAM_LR = 0.001
ADAM_B1 = 0.9
ADAM_B2 = 0.999
ADAM_EPS = 1e-08
ADAM_WD = 0.01
ADAM_STEP = 10

N_CHIPS = 4
N_DEV = 8
VMEM_LIMIT_V7X = 56 * 1024 * 1024
MESH = pl.DeviceIdType.MESH
ANY = pl.BlockSpec(memory_space=pl.ANY)

_DN = {
    "nn": (((1,), (0,)), ((), ())),
    "nt": (((1,), (1,)), ((), ())),
    "tn": (((0,), (0,)), ((), ())),
}


def _pick(n, target, mult=128):
    best = None
    for d in range(mult, min(n, target) + 1, mult):
        if n % d == 0:
            best = d
    return n if best is None else best


def _dot(a, b, mode):
    return lax.dot_general(a.astype(BF16), b.astype(BF16), _DN[mode], preferred_element_type=F32)


def _params():
    return pltpu.CompilerParams(vmem_limit_bytes=VMEM_LIMIT_V7X)


def _mm(name, mode, grid, pairs, extras, outs, acc_shape, epilogue, steps=None):
    nk = grid[2]
    n_in = 2 * len(pairs) + len(extras)
    n_out = len(outs)
    n_acc = len(pairs) if steps is None else 1

    def body(*refs):
        in_refs = refs[:n_in]
        out_refs = refs[n_in:n_in + n_out]
        accs = refs[n_in + n_out:]
        i = pl.program_id(0)
        j = pl.program_id(1)
        k = pl.program_id(2)

        @pl.when(k == 0)
        def _():
            for acc in accs:
                acc[...] = jnp.zeros(acc.shape, F32)

        def contrib(p, acc):
            acc[...] += _dot(in_refs[2 * p][...], in_refs[2 * p + 1][...], mode)

        if steps is None:
            for p in range(len(pairs)):
                contrib(p, accs[p])
        else:
            lo = 0
            for p, n in enumerate(steps):
                pl.when((k >= lo) & (k < lo + n))(functools.partial(contrib, p, accs[0]))
                lo += n

        @pl.when(k == nk - 1)
        def _():
            epilogue([acc[...] for acc in accs], in_refs[2 * len(pairs):], out_refs, (i, j))

    args, in_specs = [], []
    for a, a_spec, b, b_spec in pairs:
        args += [a, b]
        in_specs += [a_spec, b_spec]
    for e, e_spec in extras:
        args.append(e)
        in_specs.append(e_spec)
    return pl.pallas_call(
        body,
        name=name,
        grid=grid,
        in_specs=in_specs,
        out_specs=[s for _, s in outs],
        out_shape=[o for o, _ in outs],
        scratch_shapes=[pltpu.VMEM(acc_shape, F32) for _ in range(n_acc)],
        compiler_params=_params(),
    )(*args)


def _sds(shape, dtype):
    return jax.ShapeDtypeStruct(shape, dtype)


def _accumulate(ref, value, first):
    @pl.when(first)
    def _():
        ref[...] = value

    @pl.when(jnp.logical_not(first))
    def _():
        ref[...] += value


def _rms_fwd(name, x, gain):
    t, d = x.shape
    tm = _pick(t, 512, 8)

    def body(x_ref, g_ref, y_ref):
        xv = x_ref[...]
        rstd = lax.rsqrt(jnp.mean(xv * xv, axis=-1, keepdims=True) + EPS)
        y_ref[...] = (xv * rstd * g_ref[...]).astype(BF16)

    return pl.pallas_call(
        body, name=name, grid=(t // tm,),
        in_specs=[pl.BlockSpec((tm, d), lambda i: (i, 0)), pl.BlockSpec((1, d), lambda i: (0, 0))],
        out_specs=pl.BlockSpec((tm, d), lambda i: (i, 0)),
        out_shape=_sds((t, d), BF16),
        compiler_params=_params(),
    )(x, gain)


def _rms_bwd_epilogue(accs, extras, outs, ij):
    x_ref, g_ref, r_ref = extras
    dh_ref, dhb_ref, dg_ref = outs
    dn = accs[0]
    xv = x_ref[...]
    rstd = lax.rsqrt(jnp.mean(xv * xv, axis=-1, keepdims=True) + EPS)
    xhat = xv * rstd
    gd = dn * g_ref[...]
    dx = rstd * (gd - xhat * jnp.mean(gd * xhat, axis=-1, keepdims=True))
    dh = r_ref[...] + dx
    dh_ref[...] = dh
    dhb_ref[...] = dh.astype(BF16)
    _accumulate(dg_ref, jnp.sum(dn * xhat, axis=0, keepdims=True), ij[0] == 0)


def _rms_bwd_io(x, gain, dres, tm):
    t, d = x.shape
    row = pl.BlockSpec((tm, d), lambda i, j, k: (i, 0))
    extras = [(x, row), (gain, pl.BlockSpec((1, d), lambda i, j, k: (0, 0))), (dres, row)]
    outs = [(_sds((t, d), F32), row), (_sds((t, d), BF16), row),
            (_sds((1, d), F32), pl.BlockSpec((1, d), lambda i, j, k: (0, 0)))]
    return extras, outs


def _ffn_fwd(tag, h, gain, wgu, wd):
    t, d = h.shape
    fs = wgu.shape[2]
    f = 2 * fs
    tm = _pick(t, 512, 8)
    n = _rms_fwd(tag + "_norm", h, gain)

    def up_epilogue(accs, extras, outs, ij):
        g, u = accs
        gu_ref, a_ref = outs
        gu_ref[0] = g.astype(BF16)
        gu_ref[1] = u.astype(BF16)
        a_ref[...] = (g * jax.nn.sigmoid(g) * u).astype(BF16)

    a_spec = pl.BlockSpec((tm, d), lambda i, j, k: (i, 0))
    gu, a = _mm(
        tag + "_up", "nn", (t // tm, 2, 1),
        [(n, a_spec, wgu, pl.BlockSpec((None, d, fs), lambda i, j, k: (j, 0, 0))),
         (n, a_spec, wgu, pl.BlockSpec((None, d, fs), lambda i, j, k: (j + 2, 0, 0)))],
        [],
        [(_sds((2, t, f), BF16), pl.BlockSpec((2, tm, fs), lambda i, j, k: (0, i, j))),
         (_sds((t, f), BF16), pl.BlockSpec((tm, fs), lambda i, j, k: (i, j)))],
        (tm, fs), up_epilogue)

    def down_epilogue(accs, extras, outs, ij):
        outs[0][...] = extras[0][...] + 0.5 * accs[0]

    row = pl.BlockSpec((tm, d), lambda i, j, k: (i, 0))
    (h_new,) = _mm(
        tag + "_down", "nn", (t // tm, 1, 2),
        [(a, pl.BlockSpec((tm, fs), lambda i, j, k: (i, k)), wd, pl.BlockSpec((fs, d), lambda i, j, k: (k, 0)))],
        [(h, row)], [(_sds((t, d), F32), row)], (tm, d), down_epilogue)
    return h_new, (n, gu, a)


def _ffn_bwd(tag, dh, dh_b, h, gain, wgu, wd, saved):
    n, gu, a = saved
    t, d = h.shape
    fs = wgu.shape[2]
    f = 2 * fs
    tm = _pick(t, 512, 8)
    tk = _pick(t, 512, 8)

    def dact_epilogue(accs, extras, outs, ij):
        da = 0.5 * accs[0]
        g = extras[0][0].astype(F32)
        u = extras[0][1].astype(F32)
        sg = jax.nn.sigmoid(g)
        outs[0][0] = (da * u * sg * (1.0 + g * (1.0 - sg))).astype(BF16)
        outs[0][1] = (da * g * sg).astype(BF16)

    gu_spec = pl.BlockSpec((2, tm, fs), lambda i, j, k: (0, i, j))
    (dgu,) = _mm(
        tag + "_dact", "nt", (t // tm, 2, 1),
        [(dh_b, pl.BlockSpec((tm, d), lambda i, j, k: (i, 0)), wd, pl.BlockSpec((fs, d), lambda i, j, k: (j, 0)))],
        [(gu, gu_spec)], [(_sds((2, t, f), BF16), gu_spec)], (tm, fs), dact_epilogue)

    def half_epilogue(accs, extras, outs, ij):
        outs[0][...] = (0.5 * accs[0]).astype(BF16)

    (dwd,) = _mm(
        tag + "_dwd", "tn", (2, 1, t // tk),
        [(a, pl.BlockSpec((tk, fs), lambda i, j, k: (k, i)), dh_b, pl.BlockSpec((tk, d), lambda i, j, k: (k, 0)))],
        [], [(_sds((f, d), BF16), pl.BlockSpec((fs, d), lambda i, j, k: (i, 0)))], (fs, d), half_epilogue)

    def cast_epilogue(accs, extras, outs, ij):
        outs[0][...] = accs[0].astype(BF16)

    (dwgu,) = _mm(
        tag + "_dwgu", "tn", (1, 4, t // tk),
        [(n, pl.BlockSpec((tk, d), lambda i, j, k: (k, 0)),
          dgu, pl.BlockSpec((None, tk, fs), lambda i, j, k: (j // 2, k, j % 2)))],
        [], [(_sds((4, d, fs), BF16), pl.BlockSpec((None, d, fs), lambda i, j, k: (j, 0, 0)))], (d, fs), cast_epilogue)

    extras, outs = _rms_bwd_io(h, gain, dh, tm)
    dh_in, dh_in_b, dgain = _mm(
        tag + "_dnorm", "nt", (t // tm, 1, 4),
        [(dgu, pl.BlockSpec((None, tm, fs), lambda i, j, k: (k // 2, i, k % 2)),
          wgu, pl.BlockSpec((None, d, fs), lambda i, j, k: (k, 0, 0)))],
        extras, outs, (tm, d), _rms_bwd_epilogue)
    return dh_in, dh_in_b, dgain, dwgu, dwd


def _lane_lo(shape):
    return lax.broadcasted_iota(jnp.int32, shape, 1) < HEAD_DIM


def _pair_norm(xv, gain):
    lo = _lane_lo(xv.shape)
    x2 = xv * xv
    ms_lo = jnp.sum(jnp.where(lo, x2, 0.0), axis=-1, keepdims=True) * (1.0 / HEAD_DIM)
    ms_hi = jnp.sum(jnp.where(lo, 0.0, x2), axis=-1, keepdims=True) * (1.0 / HEAD_DIM)
    rstd = jnp.where(lo, lax.rsqrt(ms_lo + EPS), lax.rsqrt(ms_hi + EPS))
    xhat = xv * rstd
    return xhat * gain, xhat, rstd


def _pair_norm_bwd(dn, xhat, rstd, gain):
    lo = _lane_lo(dn.shape)
    gd = dn * gain
    t = gd * xhat
    m_lo = jnp.sum(jnp.where(lo, t, 0.0), axis=-1, keepdims=True) * (1.0 / HEAD_DIM)
    m_hi = jnp.sum(jnp.where(lo, 0.0, t), axis=-1, keepdims=True) * (1.0 / HEAD_DIM)
    dx = rstd * (gd - xhat * jnp.where(lo, m_lo, m_hi))
    return dx, jnp.sum(dn * xhat, axis=0, keepdims=True)


def _half(xv, hi):
    lo = _lane_lo(xv.shape)
    return jnp.where(lo, 0, xv) if hi else jnp.where(lo, xv, 0)


def _head_place(h, group):
    kh = h // group
    return h // 2, h % 2, kh // 2, kh % 2


def _attn_window(i, prev):
    q0 = i * Q_BLOCK
    start = jnp.maximum(q0 - prev, 0)
    off = start - (q0 - prev)
    return pl.multiple_of(start, Q_BLOCK), pl.multiple_of(off, Q_BLOCK)


def _attn_specs(cfg, s, nq):
    kw = cfg["kw"]
    q_spec = pl.BlockSpec((Q_BLOCK, A_WIDTH), lambda b, i: (b * nq + i, cfg["qblk"]))
    k_spec = pl.BlockSpec((s, kw), lambda b, i: (b, cfg["kblk"]))
    v_spec = pl.BlockSpec((s, kw), lambda b, i: (b, cfg["vblk"]))
    return q_spec, k_spec, v_spec


def _const_spec(shape):
    return pl.BlockSpec(shape, lambda b, i: (0,) * len(shape))


def _attn_fwd(name, qkv, bias_t, sink, gq, gk, cfg, n_batch):
    t = qkv.shape[0]
    s = t // n_batch
    nq = s // Q_BLOCK
    prev, group, kw = cfg["prev"], cfg["group"], cfg["kw"]
    w = prev + Q_BLOCK
    wext = bias_t.shape[1]

    def body(q_ref, k_ref, v_ref, bias_ref, sink_ref, gq_ref, gk_ref, y_ref, lse_ref, kn_ref):
        i = pl.program_id(1)

        @pl.when(i == 0)
        def _():
            for jk in range(kw // PAIR):
                kn, _, _ = _pair_norm(k_ref[:, pl.ds(jk * PAIR, PAIR)].astype(F32), gk_ref[...])
                kn_ref[:, pl.ds(jk * PAIR, PAIR)] = kn.astype(BF16)

        start, off = _attn_window(i, prev)
        sub = lax.broadcasted_iota(jnp.int32, (N_HEADS, Q_BLOCK), 0)
        lse = jnp.zeros((N_HEADS, Q_BLOCK), F32)
        for jq in range(N_HEADS // 2):
            qn, _, _ = _pair_norm(q_ref[:, pl.ds(jq * PAIR, PAIR)].astype(F32), gq_ref[...])
            qn = qn * ATTN_SCALE
            o_pair = jnp.zeros((Q_BLOCK, PAIR), F32)
            for hq in range(2):
                h = 2 * jq + hq
                _, _, jk, hk = _head_place(h, group)
                qm = _half(qn, hq)
                if hq != hk:
                    qm = pltpu.roll(qm, HEAD_DIM, 1)
                k_w = kn_ref[pl.ds(start, w), pl.ds(jk * PAIR, PAIR)]
                st = _dot(k_w, qm, "nt") + bias_ref[h, pl.ds(off, w), :]
                sk = sink_ref[h:h + 1, 0:1]
                m = jnp.maximum(jnp.max(st, axis=0, keepdims=True), sk)
                p = jnp.exp(st - m)
                l = jnp.sum(p, axis=0, keepdims=True) + jnp.exp(sk - m)
                v_w = _half(v_ref[pl.ds(start, w), pl.ds(jk * PAIR, PAIR)], hk)
                o = _dot(p * (1.0 / l), v_w, "tn")
                if hq != hk:
                    o = pltpu.roll(o, HEAD_DIM, 1)
                o_pair = o_pair + o
                lse = jnp.where(sub == h, m + jnp.log(l), lse)
            y_ref[:, pl.ds(jq * PAIR, PAIR)] = o_pair.astype(BF16)
        lse_ref[...] = lse

    q_spec, k_spec, v_spec = _attn_specs(cfg, s, nq)
    return pl.pallas_call(
        body, name=name, grid=(n_batch, nq),
        in_specs=[q_spec, k_spec, v_spec, _const_spec((N_HEADS, wext, Q_BLOCK)), _const_spec((N_HEADS, 128)),
                  _const_spec((1, PAIR)), _const_spec((1, PAIR))],
        out_specs=[pl.BlockSpec((Q_BLOCK, A_WIDTH), lambda b, i: (b * nq + i, 0)),
                   pl.BlockSpec((None, N_HEADS, Q_BLOCK), lambda b, i: (b * nq + i, 0, 0))],
        out_shape=[_sds((t, A_WIDTH), BF16), _sds((t // Q_BLOCK, N_HEADS, Q_BLOCK), F32)],
        scratch_shapes=[pltpu.VMEM((s, kw), BF16)],
        compiler_params=_params(),
    )(qkv, qkv, qkv, bias_t, sink, gq, gk)


def _attn_bwd(name, qkv, bias_t, sink, gq, gk, y, dy, lse, cfg, n_batch, want_dbias):
    t = qkv.shape[0]
    s = t // n_batch
    nq = s // Q_BLOCK
    prev, group, kw = cfg["prev"], cfg["group"], cfg["kw"]
    w = prev + Q_BLOCK
    wext = bias_t.shape[1]

    def body(q_ref, k_ref, v_ref, bias_ref, sink_ref, gq_ref, gk_ref, y_ref, dy_ref, lse_ref,
             dq_ref, dk_ref, dv_ref, db_ref, dsink_ref, dgq_ref, dgk_ref, kn_ref, dkn_ref, dvs_ref):
        b = pl.program_id(0)
        i = pl.program_id(1)
        first = (b == 0) & (i == 0)

        @pl.when(i == 0)
        def _():
            for jk in range(kw // PAIR):
                kn, _, _ = _pair_norm(k_ref[:, pl.ds(jk * PAIR, PAIR)].astype(F32), gk_ref[...])
                kn_ref[:, pl.ds(jk * PAIR, PAIR)] = kn.astype(BF16)
            dkn_ref[...] = jnp.zeros(dkn_ref.shape, F32)
            dvs_ref[...] = jnp.zeros(dvs_ref.shape, F32)

        @pl.when(first)
        def _():
            db_ref[...] = jnp.zeros(db_ref.shape, F32)
            dsink_ref[...] = jnp.zeros(dsink_ref.shape, F32)
            dgq_ref[...] = jnp.zeros(dgq_ref.shape, F32)
            dgk_ref[...] = jnp.zeros(dgk_ref.shape, F32)

        start, off = _attn_window(i, prev)
        sub_lo = lax.broadcasted_iota(jnp.int32, (PAIR, Q_BLOCK), 0) < HEAD_DIM
        for jq in range(N_HEADS // 2):
            cols = pl.ds(jq * PAIR, PAIR)
            qn, q_hat, q_rstd = _pair_norm(q_ref[:, cols].astype(F32), gq_ref[...])
            qn = qn * ATTN_SCALE
            do_pair = dy_ref[:, cols]
            prod_t = (do_pair.astype(F32) * y_ref[:, cols].astype(F32)).T
            dqn = jnp.zeros((Q_BLOCK, PAIR), F32)
            for hq in range(2):
                h = 2 * jq + hq
                _, _, jk, hk = _head_place(h, group)
                kcols = pl.ds(jk * PAIR, PAIR)
                delta = jnp.sum(jnp.where(sub_lo, 0.0, prod_t) if hq else jnp.where(sub_lo, prod_t, 0.0),
                                axis=0, keepdims=True)
                qm = _half(qn, hq)
                do_m = _half(do_pair, hq)
                if hq != hk:
                    qm = pltpu.roll(qm, HEAD_DIM, 1)
                    do_m = pltpu.roll(do_m.astype(F32), HEAD_DIM, 1)
                qm_b = qm.astype(BF16)
                do_b = do_m.astype(BF16)
                k_w = kn_ref[pl.ds(start, w), kcols]
                v_w = v_ref[pl.ds(start, w), kcols]
                lse_row = lse_ref[h:h + 1, :]
                st = _dot(k_w, qm_b, "nt") + bias_ref[h, pl.ds(off, w), :]
                p = jnp.exp(st - lse_row)
                dp = _dot(v_w, do_b, "nt")
                ds = p * (dp - delta)
                dsink_ref[h:h + 1, :] += -jnp.exp(sink_ref[h:h + 1, 0:1] - lse_row) * delta
                if want_dbias:
                    db_ref[h, pl.ds(off, w), :] += ds
                ds_b = ds.astype(BF16)
                dq_h = _half(_dot(ds_b, k_w, "tn"), hk)
                if hq != hk:
                    dq_h = pltpu.roll(dq_h, HEAD_DIM, 1)
                dqn = dqn + dq_h
                dkn_ref[pl.ds(start, w), kcols] += _half(_dot(ds_b, qm_b, "nn"), hk)
                dvs_ref[pl.ds(start, w), kcols] += _half(_dot(p, do_b, "nn"), hk)
            dq_raw, dg = _pair_norm_bwd(dqn * ATTN_SCALE, q_hat, q_rstd, gq_ref[...])
            dq_ref[:, cols] = dq_raw.astype(BF16)
            dgq_ref[...] += dg

        @pl.when(i == nq - 1)
        def _():
            for jk in range(kw // PAIR):
                kcols = pl.ds(jk * PAIR, PAIR)
                _, k_hat, k_rstd = _pair_norm(k_ref[:, kcols].astype(F32), gk_ref[...])
                dk_raw, dg = _pair_norm_bwd(dkn_ref[:, kcols], k_hat, k_rstd, gk_ref[...])
                dk_ref[:, kcols] = dk_raw.astype(BF16)
                dgk_ref[...] += dg
            dv_ref[...] = dvs_ref[...].astype(BF16)

    q_spec, k_spec, v_spec = _attn_specs(cfg, s, nq)
    row = pl.BlockSpec((Q_BLOCK, A_WIDTH), lambda b, i: (b * nq + i, 0))
    kv_out = pl.BlockSpec((s, kw), lambda b, i: (b, 0))
    return pl.pallas_call(
        body, name=name, grid=(n_batch, nq),
        in_specs=[q_spec, k_spec, v_spec, _const_spec((N_HEADS, wext, Q_BLOCK)), _const_spec((N_HEADS, 128)),
                  _const_spec((1, PAIR)), _const_spec((1, PAIR)), row, row,
                  pl.BlockSpec((None, N_HEADS, Q_BLOCK), lambda b, i: (b * nq + i, 0, 0))],
        out_specs=[row, kv_out, kv_out, _const_spec((N_HEADS, wext, Q_BLOCK)), _const_spec((N_HEADS, 128)),
                   _const_spec((1, PAIR)), _const_spec((1, PAIR))],
        out_shape=[_sds((t, A_WIDTH), BF16), _sds((t, kw), BF16), _sds((t, kw), BF16),
                   _sds((N_HEADS, wext, Q_BLOCK), F32), _sds((N_HEADS, 128), F32),
                   _sds((1, PAIR), F32), _sds((1, PAIR), F32)],
        scratch_shapes=[pltpu.VMEM((s, kw), BF16), pltpu.VMEM((s, kw), F32), pltpu.VMEM((s, kw), F32)],
        compiler_params=_params(),
    )(qkv, qkv, qkv, bias_t, sink, gq, gk, y, dy, lse)


def _band_tables(prev_chunks):
    prev = prev_chunks * CHUNK
    wext = 2 * prev + Q_BLOCK
    jj = np.arange(wext)[:, None]
    ii = np.arange(Q_BLOCK)[None, :]
    dist = prev + ii - jj
    rel_chunk = (prev // CHUNK + ii // CHUNK) - jj // CHUNK
    allowed = (rel_chunk >= 0) & (rel_chunk <= prev_chunks)
    return dist, allowed


def _alibi_slopes():
    return np.array([2.0 ** (-8.0 * (h + 1) / N_HEADS) for h in range(N_HEADS)], dtype=np.float32)


def _diag_onehot(prev, wext):
    n_diag = wext + Q_BLOCK - 1
    idx = np.clip(prev + Q_BLOCK - 1 - np.arange(n_diag), -A_MAX_REL, A_MAX_REL) + A_MAX_REL
    onehot = np.zeros((n_diag, 2 * A_MAX_REL + 1), np.float32)
    onehot[np.arange(n_diag), idx] = 1.0
    return onehot


def _bias_a(rel_bias):
    prev = A_PREV_CHUNKS * CHUNK
    _, allowed = _band_tables(A_PREV_CHUNKS)
    wext = allowed.shape[0]
    n_diag = wext + Q_BLOCK - 1
    seq = jnp.dot(rel_bias, jnp.asarray(_diag_onehot(prev, wext).T), precision=lax.Precision.HIGHEST)
    seq = jnp.pad(seq, ((0, 0), (0, 1)))
    rows = jnp.broadcast_to(seq[:, None, :], (N_HEADS, Q_BLOCK, n_diag + 1)).reshape(N_HEADS, -1)
    skew = rows[:, :Q_BLOCK * n_diag].reshape(N_HEADS, Q_BLOCK, n_diag)
    tile = jnp.transpose(skew[:, :, Q_BLOCK - 1:Q_BLOCK - 1 + wext], (0, 2, 1))
    return jnp.where(jnp.asarray(allowed)[None], tile, NEG_INF)


def _bias_b():
    dist, allowed = _band_tables(B_PREV_CHUNKS)
    bias = -_alibi_slopes()[:, None, None] * np.abs(dist).astype(np.float32)[None]
    return jnp.asarray(np.where(allowed[None], bias, np.float32(NEG_INF)).astype(np.float32))


def _rel_bias_grad(db_t):
    prev = A_PREV_CHUNKS * CHUNK
    wext = db_t.shape[1]
    n_diag = wext + Q_BLOCK - 1
    wp = n_diag + Q_BLOCK - 1
    xp = jnp.pad(jnp.transpose(db_t, (0, 2, 1)), ((0, 0), (0, 0), (Q_BLOCK - 1, Q_BLOCK - 1)))
    flat = jnp.pad(xp.reshape(N_HEADS, Q_BLOCK * wp), ((0, 0), (0, Q_BLOCK)))
    skew = flat.reshape(N_HEADS, Q_BLOCK, wp + 1)[:, :, :n_diag]
    diag = jnp.sum(skew, axis=1)
    return jnp.dot(diag, jnp.asarray(_diag_onehot(prev, wext)), precision=lax.Precision.HIGHEST)


def _ew(name, fn, ins, out_dtypes):
    r, c = ins[0].shape
    rb = _pick(r, max(16, (1 << 19) // c), 16)
    spec = pl.BlockSpec((rb, c), lambda i: (i, 0))

    def body(*refs):
        vals = fn(*[ref[...] for ref in refs[:len(ins)]])
        for ref, val in zip(refs[len(ins):], vals):
            ref[...] = val.astype(ref.dtype)

    return pl.pallas_call(
        body, name=name, grid=(r // rb,), in_specs=[spec] * len(ins), out_specs=[spec] * len(out_dtypes),
        out_shape=[_sds((r, c), dt) for dt in out_dtypes], compiler_params=_params(),
    )(*ins)


def _adamw_math(w, g, m, v):
    m = ADAM_B1 * m + (1.0 - ADAM_B1) * g
    v = ADAM_B2 * v + (1.0 - ADAM_B2) * (g * g)
    m_hat = m / (1.0 - ADAM_B1 ** ADAM_STEP)
    v_hat = v / (1.0 - ADAM_B2 ** ADAM_STEP)
    delta = -ADAM_LR * (m_hat / (jnp.sqrt(v_hat) + ADAM_EPS) + ADAM_WD * w)
    return delta, m, v


def _adamw_terms(name, terms, w, m, v):
    r, c = w.shape
    hr = r // 2
    rb = _pick(hr, max(16, (1 << 17) // c), 16)
    nb = hr // rb

    def body(t_ref, w_ref, m_ref, v_ref, g_ref, d_ref, nm_ref, nv_ref):
        g = t_ref[0].astype(F32)
        for k in range(1, N_CHIPS):
            g = g + t_ref[k].astype(F32)
        delta, nm, nv = _adamw_math(w_ref[...], g, m_ref[...], v_ref[...])
        g_ref[...] = g
        d_ref[...] = delta
        nm_ref[...] = nm
        nv_ref[...] = nv

    spec = pl.BlockSpec((rb, c), lambda h, i: (h * nb + i, 0))
    return pl.pallas_call(
        body, name=name, grid=(2, nb),
        in_specs=[pl.BlockSpec((None, N_CHIPS, rb, c), lambda h, i: (h, 0, i, 0)), spec, spec, spec],
        out_specs=[spec] * 4, out_shape=[_sds((r, c), F32)] * 4, compiler_params=_params(),
    )(terms, w, m, v)


def _mesh_place():
    x, y, c = lax.axis_index("x"), lax.axis_index("y"), lax.axis_index("c")
    chips = [(x, 1 - y), (1 - x, y), (1 - x, 1 - y)]
    return x, y, c, chips


def _all_gather_weights(shards):
    n = len(shards)

    def body(*refs):
        ins, outs = refs[:n], refs[n:2 * n]
        local_sem, ici_send, ici_recv, d2d_send, d2d_recv = refs[2 * n:]
        x, y, c, chips = _mesh_place()
        me = 2 * x + y
        sibling = (x, y, 1 - c)
        local, sent = [], []
        for wi in range(n):
            hr = ins[wi].shape[0] // 2
            mine = pl.ds(c * hr, hr)
            loc = pltpu.make_async_copy(ins[wi], outs[wi].at[me], local_sem.at[wi])
            loc.start()
            local.append(loc)
            for k, (tx, ty) in enumerate(chips):
                cp = pltpu.make_async_remote_copy(
                    src_ref=ins[wi].at[mine, :], dst_ref=outs[wi].at[me, mine, :],
                    send_sem=ici_send.at[wi * 3 + k], recv_sem=ici_recv.at[wi * 3 + k],
                    device_id=(tx, ty, c), device_id_type=MESH)
                cp.start()
                sent.append(cp)
        passed = []
        for wi in range(n):
            hr = ins[wi].shape[0] // 2
            mine = pl.ds(c * hr, hr)
            for k, (tx, ty) in enumerate(chips):
                slab = outs[wi].at[2 * tx + ty, mine, :]
                pltpu.make_async_remote_copy(
                    src_ref=slab, dst_ref=slab, send_sem=ici_send.at[wi * 3 + k], recv_sem=ici_recv.at[wi * 3 + k],
                    device_id=(tx, ty, c), device_id_type=MESH).wait_recv()
                fw = pltpu.make_async_remote_copy(
                    src_ref=slab, dst_ref=slab, send_sem=d2d_send.at[wi * 3 + k], recv_sem=d2d_recv.at[wi * 3 + k],
                    device_id=sibling, device_id_type=MESH)
                fw.start()
                passed.append(fw)
        for wi in range(n):
            hr = ins[wi].shape[0] // 2
            theirs = pl.ds((1 - c) * hr, hr)
            for k, (tx, ty) in enumerate(chips):
                slab = outs[wi].at[2 * tx + ty, theirs, :]
                pltpu.make_async_remote_copy(
                    src_ref=slab, dst_ref=slab, send_sem=d2d_send.at[wi * 3 + k], recv_sem=d2d_recv.at[wi * 3 + k],
                    device_id=sibling, device_id_type=MESH).wait_recv()
        for loc in local:
            loc.wait()
        for cp in sent + passed:
            cp.wait_send()

    return pl.pallas_call(
        body, name="all_gather_weights",
        in_specs=[ANY] * n, out_specs=[ANY] * n,
        out_shape=[_sds((N_CHIPS,) + s.shape, s.dtype) for s in shards],
        scratch_shapes=[pltpu.SemaphoreType.DMA((n,)), pltpu.SemaphoreType.DMA((3 * n,)),
                        pltpu.SemaphoreType.DMA((3 * n,)), pltpu.SemaphoreType.DMA((3 * n,)),
                        pltpu.SemaphoreType.DMA((3 * n,))],
    )(*shards)


def _exchange_halves(grads):
    n = len(grads)

    def body(*refs):
        ins, got = refs[:n], refs[n:2 * n]
        send_sem, recv_sem = refs[2 * n:]
        x, y, c, _ = _mesh_place()
        copies = []
        for wi in range(n):
            for t in range(N_CHIPS):
                cp = pltpu.make_async_remote_copy(
                    src_ref=ins[wi].at[t, 1 - c], dst_ref=got[wi].at[t],
                    send_sem=send_sem.at[wi * N_CHIPS + t], recv_sem=recv_sem.at[wi * N_CHIPS + t],
                    device_id=(x, y, 1 - c), device_id_type=MESH)
                cp.start()
                copies.append(cp)
        for cp in copies:
            cp.wait()

    return pl.pallas_call(
        body, name="grad_exchange_halves",
        in_specs=[ANY] * n, out_specs=[ANY] * n,
        out_shape=[_sds((N_CHIPS,) + g.shape[2:], g.dtype) for g in grads],
        scratch_shapes=[pltpu.SemaphoreType.DMA((N_CHIPS * n,)), pltpu.SemaphoreType.DMA((N_CHIPS * n,))],
    )(*grads)


def _chip_sum(name, grad, got, core):
    _, _, hr, c = grad.shape
    rb = _pick(hr, max(16, (1 << 19) // c), 16)

    def body(core_ref, a_ref, b_ref, o_ref):
        o_ref[...] = (a_ref[...].astype(F32) + b_ref[...].astype(F32)).astype(BF16)

    out_spec = pl.BlockSpec((None, rb, c), lambda t, i, core_ref: (t, i, 0))
    return pl.pallas_call(
        body, name=name,
        grid_spec=pltpu.PrefetchScalarGridSpec(
            num_scalar_prefetch=1, grid=(N_CHIPS, hr // rb),
            in_specs=[pl.BlockSpec((None, None, rb, c), lambda t, i, core_ref: (t, core_ref[0], i, 0)), out_spec],
            out_specs=out_spec),
        out_shape=_sds((N_CHIPS, hr, c), BF16), compiler_params=_params(),
    )(core, grad, got)


def _scatter_chip_sums(sums):
    n = len(sums)

    def body(*refs):
        ins, outs = refs[:n], refs[n:2 * n]
        local_sem, ici_send, ici_recv, d2d_send, d2d_recv = refs[2 * n:]
        x, y, c, chips = _mesh_place()
        me = 2 * x + y
        sibling = (x, y, 1 - c)
        local, sent = [], []
        for wi in range(n):
            loc = pltpu.make_async_copy(ins[wi].at[me], outs[wi].at[c, 0], local_sem.at[wi])
            loc.start()
            local.append(loc)
            for k, (tx, ty) in enumerate(chips):
                cp = pltpu.make_async_remote_copy(
                    src_ref=ins[wi].at[2 * tx + ty], dst_ref=outs[wi].at[c, k + 1],
                    send_sem=ici_send.at[wi * 3 + k], recv_sem=ici_recv.at[wi * 3 + k],
                    device_id=(tx, ty, c), device_id_type=MESH)
                cp.start()
                sent.append(cp)
        for wi in range(n):
            local[wi].wait()
            for k in range(N_CHIPS):
                slab = outs[wi].at[c, k]
                if k > 0:
                    tx, ty = chips[k - 1]
                    pltpu.make_async_remote_copy(
                        src_ref=slab, dst_ref=slab, send_sem=ici_send.at[wi * 3 + k - 1],
                        recv_sem=ici_recv.at[wi * 3 + k - 1], device_id=(tx, ty, c), device_id_type=MESH).wait_recv()
                fw = pltpu.make_async_remote_copy(
                    src_ref=slab, dst_ref=slab, send_sem=d2d_send.at[wi * 4 + k], recv_sem=d2d_recv.at[wi * 4 + k],
                    device_id=sibling, device_id_type=MESH)
                fw.start()
                sent.append(fw)
        for wi in range(n):
            for k in range(N_CHIPS):
                slab = outs[wi].at[1 - c, k]
                pltpu.make_async_remote_copy(
                    src_ref=slab, dst_ref=slab, send_sem=d2d_send.at[wi * 4 + k], recv_sem=d2d_recv.at[wi * 4 + k],
                    device_id=sibling, device_id_type=MESH).wait_recv()
        for cp in sent:
            cp.wait_send()

    return pl.pallas_call(
        body, name="grad_scatter_chip_sums",
        in_specs=[ANY] * n, out_specs=[ANY] * n,
        out_shape=[_sds((2, N_CHIPS) + s.shape[1:], s.dtype) for s in sums],
        scratch_shapes=[pltpu.SemaphoreType.DMA((n,)), pltpu.SemaphoreType.DMA((3 * n,)),
                        pltpu.SemaphoreType.DMA((3 * n,)), pltpu.SemaphoreType.DMA((4 * n,)),
                        pltpu.SemaphoreType.DMA((4 * n,))],
    )(*sums)


def _all_reduce_small(pack):
    r = pack.shape[0]

    def body(p_ref, o_ref, land_ref, send_sem, recv_sem):
        x, y, c, _ = _mesh_place()
        me = 4 * x + 2 * y + c
        flips = [(k >> 2 & 1, k >> 1 & 1, k & 1) for k in range(1, N_DEV)]

        def peer(fx, fy, fc):
            return (1 - x if fx else x, 1 - y if fy else y, 1 - c if fc else c)

        land_ref[me] = p_ref[...]
        sent = []
        for k, flip in enumerate(flips):
            cp = pltpu.make_async_remote_copy(
                src_ref=p_ref, dst_ref=land_ref.at[me], send_sem=send_sem.at[k], recv_sem=recv_sem.at[k],
                device_id=peer(*flip), device_id_type=MESH)
            cp.start()
            sent.append(cp)
        for k, flip in enumerate(flips):
            px, py, pc = peer(*flip)
            slot = land_ref.at[4 * px + 2 * py + pc]
            pltpu.make_async_remote_copy(
                src_ref=slot, dst_ref=slot, send_sem=send_sem.at[k], recv_sem=recv_sem.at[k],
                device_id=(px, py, pc), device_id_type=MESH).wait_recv()
        total = land_ref[0]
        for d in range(1, N_DEV):
            total = total + land_ref[d]
        o_ref[...] = total
        for cp in sent:
            cp.wait_send()

    vmem = pl.BlockSpec(memory_space=pltpu.VMEM)
    return pl.pallas_call(
        body, name="all_reduce_small", in_specs=[vmem], out_specs=vmem, out_shape=_sds((r, 128), F32),
        scratch_shapes=[pltpu.VMEM((N_DEV, r, 128), F32), pltpu.SemaphoreType.DMA((N_DEV - 1,)),
                        pltpu.SemaphoreType.DMA((N_DEV - 1,))],
    )(pack)


PACK_TILE = 8 * 128


def _pack(items):
    rows, i = [], 0
    while i < len(items):
        j = i
        while j < len(items) and items[j].size == items[i].size:
            j += 1
        group = jnp.stack([it.reshape(-1).astype(F32) for it in items[i:j]])
        rows.append(jnp.pad(group, ((0, 0), (0, -group.shape[1] % PACK_TILE))).reshape(-1, 128))
        i = j
    return jnp.concatenate(rows, axis=0)


def _unpack(pack, shapes):
    out, row = [], 0
    for shp in shapes:
        size = int(np.prod(shp))
        nrow = -(-size // PACK_TILE) * (PACK_TILE // 128)
        out.append(pack[row:row + nrow].reshape(-1)[:size].reshape(shp))
        row += nrow
    return out


BIG = ["ffn1_w_gu", "ffn1_w_down", "w_in", "w_gate", "w_proj_a", "w_proj_b", "w_out",
       "ffn2_w_gu", "ffn2_w_down", "w_ple_gate", "w_ple_proj"]
SMALL = ["ffn1_norm", "mix_norm", "ffn2_norm", "ple_norm", "a_q_norm", "a_k_norm", "b_q_norm", "b_k_norm",
         "a_rel_bias", "b_sinks"]
WEIGHTS = ["ffn1_norm", "ffn1_w_gu", "ffn1_w_down", "mix_norm", "w_in", "a_q_norm", "a_k_norm", "a_rel_bias",
           "b_q_norm", "b_k_norm", "b_sinks", "w_gate", "w_proj_a", "w_proj_b", "w_out", "ffn2_norm",
           "ffn2_w_gu", "ffn2_w_down", "ple_norm", "w_ple_gate", "w_ple_proj"]
ATTN_A = dict(prev=A_PREV_CHUNKS * CHUNK, group=1, kw=A_WIDTH, qblk=0, kblk=1, vblk=2)
ATTN_B = dict(prev=B_PREV_CHUNKS * CHUNK, group=N_HEADS // B_KV_HEADS, kw=B_KV_WIDTH, qblk=3,
              kblk=4 * A_WIDTH // B_KV_WIDTH, vblk=4 * A_WIDTH // B_KV_WIDTH + 1)


def _cast_epilogue(accs, extras, outs, ij):
    for acc, out in zip(accs, outs):
        out[...] = acc.astype(out.dtype)


def _local_step(xt, pt, tgt, n_batch, w, small):
    t, d = xt.shape
    tm = _pick(t, 512, 8)
    tk = _pick(t, 512, 8)
    nt = t // tm
    row = pl.BlockSpec((tm, d), lambda i, j, k: (i, 0))
    wgate, wpa, wpb, wpe = w["w_gate"], w["w_proj_a"], w["w_proj_b"], w["w_ple_proj"]
    w_in, wout, wpg = w["w_in"], w["w_out"], w["w_ple_gate"]
    gs = wgate.shape[2]
    ps = wpa.shape[2]
    es = wpe.shape[2]
    pdim = pt.shape[1]
    ncols = w_in.shape[1]
    tin = ncols // 2
    assert 2 * gs == d and 4 * ps == d and 4 * es == d and tin % 128 == 0

    h1, ffn1_saved = _ffn_fwd("ffn1", xt, small["ffn1_norm"], w["ffn1_w_gu"], w["ffn1_w_down"])
    un = _rms_fwd("mix_norm", h1, small["mix_norm"])
    (qkv,) = _mm(
        "qkv", "nn", (nt, 2, 1),
        [(un, row, w_in, pl.BlockSpec((d, tin), lambda i, j, k: (0, j)))], [],
        [(_sds((t, ncols), BF16), pl.BlockSpec((tm, tin), lambda i, j, k: (i, j)))], (tm, tin), _cast_epilogue)

    def gate_epilogue(accs, extras, outs, ij):
        outs[0][...] = jax.nn.sigmoid(accs[0]).astype(BF16)

    (gates,) = _mm(
        "gate", "nn", (nt, 4, 1),
        [(un, row, wgate, pl.BlockSpec((None, d, gs), lambda i, j, k: (j, 0, 0)))], [],
        [(_sds((2, t, d), BF16), pl.BlockSpec((None, tm, gs), lambda i, j, k: (j // 2, i, j % 2)))],
        (tm, gs), gate_epilogue)

    bias_a = _bias_a(small["a_rel_bias"][0])
    bias_b = _bias_b()
    sink_a = jnp.full((N_HEADS, 128), NEG_INF, F32)
    sink_b = jnp.broadcast_to(small["b_sinks"][0][:, None], (N_HEADS, 128))
    gqa, gka, gqb, gkb = [jnp.tile(small[k], (1, 2)) for k in ("a_q_norm", "a_k_norm", "b_q_norm", "b_k_norm")]
    ya, lse_a = _attn_fwd("attn_a_fwd", qkv, bias_a, sink_a, gqa, gka, ATTN_A, n_batch)
    yb, lse_b = _attn_fwd("attn_b_fwd", qkv, bias_b, sink_b, gqb, gkb, ATTN_B, n_batch)

    def merge_epilogue(accs, extras, outs, ij):
        pa, pb = accs
        outs[0][...] = (extras[0][...].astype(F32) * pa + extras[1][...].astype(F32) * pb).astype(BF16)
        outs[1][...] = pa.astype(BF16)
        outs[2][...] = pb.astype(BF16)

    y_spec = pl.BlockSpec((tm, A_WIDTH), lambda i, j, k: (i, 0))
    proj_spec = pl.BlockSpec((None, A_WIDTH, ps), lambda i, j, k: (j, 0, 0))
    tile_ps = pl.BlockSpec((tm, ps), lambda i, j, k: (i, j))
    merged, pa, pb = _mm(
        "proj_merge", "nn", (nt, 4, 1),
        [(ya, y_spec, wpa, proj_spec), (yb, y_spec, wpb, proj_spec)],
        [(gates, pl.BlockSpec((None, tm, ps), lambda i, j, k: (0, i, j))),
         (gates, pl.BlockSpec((None, tm, ps), lambda i, j, k: (1, i, j)))],
        [(_sds((t, d), BF16), tile_ps)] * 3, (tm, ps), merge_epilogue)

    def residual_epilogue(accs, extras, outs, ij):
        outs[0][...] = extras[0][...] + accs[0]

    (h2,) = _mm(
        "out_proj", "nn", (nt, 1, 1),
        [(merged, row, wout, pl.BlockSpec((d, d), lambda i, j, k: (0, 0)))],
        [(h1, row)], [(_sds((t, d), F32), row)], (tm, d), residual_epilogue)

    h3, ffn2_saved = _ffn_fwd("ffn2", h2, small["ffn2_norm"], w["ffn2_w_gu"], w["ffn2_w_down"])
    n3 = _rms_fwd("ple_norm", h3, small["ple_norm"])
    tile_es = pl.BlockSpec((tm, es), lambda i, j, k: (i, j))
    (pe,) = _mm(
        "ple_embed", "nn", (nt, 4, 1),
        [(pt, pl.BlockSpec((tm, pdim), lambda i, j, k: (i, 0)), wpe, pl.BlockSpec((None, pdim, es), lambda i, j, k: (j, 0, 0)))],
        [], [(_sds((t, d), F32), tile_es)], (tm, es), _cast_epilogue)

    th = _pick(d, 512)

    def head_epilogue(accs, extras, outs, ij):
        h3_ref, pe_ref, tgt_ref = extras
        dy_ref, dpe_ref, dz_ref, loss_ref = outs
        pg = jax.nn.sigmoid(accs[0])
        pev = pe_ref[...]
        diff = h3_ref[...] + pg * pev - tgt_ref[...]
        dy = diff * (1.0 / d)
        dy_ref[...] = dy
        dpe_ref[...] = (dy * pg).astype(BF16)
        dz_ref[...] = (dy * pev * pg * (1.0 - pg)).astype(BF16)
        _accumulate(loss_ref, jnp.full(loss_ref.shape, jnp.sum(diff * diff), F32), (ij[0] == 0) & (ij[1] == 0))

    tile_h = pl.BlockSpec((tm, th), lambda i, j, k: (i, j))
    dy, dpe, dz, loss_acc = _mm(
        "ple_gate_loss", "nn", (nt, d // th, 1),
        [(n3, row, wpg, pl.BlockSpec((d, th), lambda i, j, k: (0, j)))],
        [(h3, tile_h), (pe, tile_h), (tgt, tile_h)],
        [(_sds((t, d), F32), tile_h), (_sds((t, d), BF16), tile_h), (_sds((t, d), BF16), tile_h),
         (_sds((8, 128), F32), pl.BlockSpec((8, 128), lambda i, j, k: (0, 0)))],
        (tm, th), head_epilogue)
    loss = 0.5 * loss_acc[0, 0] / d

    nk = t // tk
    (dwpe,) = _mm(
        "d_w_ple_proj", "tn", (1, 4, nk),
        [(pt, pl.BlockSpec((tk, pdim), lambda i, j, k: (k, 0)), dpe, pl.BlockSpec((tk, es), lambda i, j, k: (k, j)))],
        [], [(_sds((4, pdim, es), BF16), pl.BlockSpec((None, pdim, es), lambda i, j, k: (j, 0, 0)))],
        (pdim, es), _cast_epilogue)

    def dense_grad(name, a, dyb):
        (res,) = _mm(
            name, "tn", (1, d // th, nk),
            [(a, pl.BlockSpec((tk, d), lambda i, j, k: (k, 0)), dyb, pl.BlockSpec((tk, th), lambda i, j, k: (k, j)))],
            [], [(_sds((d, d), BF16), pl.BlockSpec((d, th), lambda i, j, k: (0, j)))], (d, th), _cast_epilogue)
        return res

    dwpg = dense_grad("d_w_ple_gate", n3, dz)
    extras, outs = _rms_bwd_io(h3, small["ple_norm"], dy, tm)
    dh3, dh3_b, d_ple_norm = _mm(
        "d_ple_norm", "nt", (nt, 1, d // th),
        [(dz, pl.BlockSpec((tm, th), lambda i, j, k: (i, k)), wpg, pl.BlockSpec((d, th), lambda i, j, k: (0, k)))],
        extras, outs, (tm, d), _rms_bwd_epilogue)

    dh2, dh2_b, d_ffn2_norm, dwgu2, dwd2 = _ffn_bwd(
        "ffn2", dh3, dh3_b, h2, small["ffn2_norm"], w["ffn2_w_gu"], w["ffn2_w_down"], ffn2_saved)

    def dmerge_epilogue(accs, extras, outs, ij):
        dmo = accs[0]
        g_ref, pa_ref, pb_ref = extras
        dg_ref, dpa_ref, dpb_ref = outs
        ga = g_ref[0].astype(F32)
        gb = g_ref[1].astype(F32)
        dg_ref[0] = (dmo * pa_ref[...].astype(F32) * ga * (1.0 - ga)).astype(BF16)
        dg_ref[1] = (dmo * pb_ref[...].astype(F32) * gb * (1.0 - gb)).astype(BF16)
        dpa_ref[...] = (dmo * ga).astype(BF16)
        dpb_ref[...] = (dmo * gb).astype(BF16)

    g_spec = pl.BlockSpec((2, tm, th), lambda i, j, k: (0, i, j))
    dgates, dpa, dpb = _mm(
        "d_merge", "nt", (nt, d // th, 1),
        [(dh2_b, row, wout, pl.BlockSpec((th, d), lambda i, j, k: (j, 0)))],
        [(gates, g_spec), (pa, tile_h), (pb, tile_h)],
        [(_sds((2, t, d), BF16), g_spec), (_sds((t, d), BF16), tile_h), (_sds((t, d), BF16), tile_h)],
        (tm, th), dmerge_epilogue)
    dwout = dense_grad("d_w_out", merged, dh2_b)

    yk_spec = pl.BlockSpec((tk, A_WIDTH), lambda i, j, k: (k, 0))
    dk_spec = pl.BlockSpec((tk, ps), lambda i, j, k: (k, j))
    dproj = (_sds((4, A_WIDTH, ps), BF16), proj_spec)
    dwpa, dwpb = _mm(
        "d_w_proj", "tn", (1, 4, nk),
        [(ya, yk_spec, dpa, dk_spec), (yb, yk_spec, dpb, dk_spec)], [], [dproj, dproj], (A_WIDTH, ps), _cast_epilogue)
    dproj_a = pl.BlockSpec((tm, ps), lambda i, j, k: (i, k))
    wproj_k = pl.BlockSpec((None, A_WIDTH, ps), lambda i, j, k: (k, 0, 0))
    dya, dyb = _mm(
        "d_attn_out", "nt", (nt, 1, 4),
        [(dpa, dproj_a, wpa, wproj_k), (dpb, dproj_a, wpb, wproj_k)], [],
        [(_sds((t, A_WIDTH), BF16), y_spec)] * 2, (tm, A_WIDTH), _cast_epilogue)

    dqa, dka, dva, dbias_a, _, dgqa, dgka = _attn_bwd(
        "attn_a_bwd", qkv, bias_a, sink_a, gqa, gka, ya, dya, lse_a, ATTN_A, n_batch, True)
    dqb, dkb, dvb, _, dsink_b, dgqb, dgkb = _attn_bwd(
        "attn_b_bwd", qkv, bias_b, sink_b, gqb, gkb, yb, dyb, lse_b, ATTN_B, n_batch, False)
    dqkv = jnp.concatenate([dqa, dka, dva, dqb, dkb, dvb], axis=1)

    (dwgate,) = _mm(
        "d_w_gate", "tn", (1, 4, nk),
        [(un, pl.BlockSpec((tk, d), lambda i, j, k: (k, 0)),
          dgates, pl.BlockSpec((None, tk, gs), lambda i, j, k: (j // 2, k, j % 2)))],
        [], [(_sds((4, d, gs), BF16), pl.BlockSpec((None, d, gs), lambda i, j, k: (j, 0, 0)))], (d, gs), _cast_epilogue)
    (dwin,) = _mm(
        "d_w_in", "tn", (1, 2, nk),
        [(un, pl.BlockSpec((tk, d), lambda i, j, k: (k, 0)), dqkv, pl.BlockSpec((tk, tin), lambda i, j, k: (k, j)))],
        [], [(_sds((d, ncols), BF16), pl.BlockSpec((d, tin), lambda i, j, k: (0, j)))], (d, tin), _cast_epilogue)

    extras, outs = _rms_bwd_io(h1, small["mix_norm"], dh2, tm)
    dh1, dh1_b, d_mix_norm = _mm(
        "d_mix_norm", "nt", (nt, 1, 6),
        [(dgates, pl.BlockSpec((None, tm, gs), lambda i, j, k: (jnp.minimum(k, 3) // 2, i, jnp.minimum(k, 3) % 2)),
          wgate, pl.BlockSpec((None, d, gs), lambda i, j, k: (jnp.minimum(k, 3), 0, 0))),
         (dqkv, pl.BlockSpec((tm, tin), lambda i, j, k: (i, jnp.maximum(k - 4, 0))),
          w_in, pl.BlockSpec((d, tin), lambda i, j, k: (0, jnp.maximum(k - 4, 0))))],
        extras, outs, (tm, d), _rms_bwd_epilogue, steps=[4, 2])

    dx, _, d_ffn1_norm, dwgu1, dwd1 = _ffn_bwd(
        "ffn1", dh1, dh1_b, xt, small["ffn1_norm"], w["ffn1_w_gu"], w["ffn1_w_down"], ffn1_saved)

    def fold(v):
        return v[0, :HEAD_DIM] + v[0, HEAD_DIM:]

    big_grads = {"ffn1_w_gu": dwgu1, "ffn1_w_down": dwd1, "w_in": dwin, "w_gate": dwgate, "w_proj_a": dwpa,
                 "w_proj_b": dwpb, "w_out": dwout, "ffn2_w_gu": dwgu2, "ffn2_w_down": dwd2,
                 "w_ple_gate": dwpg, "w_ple_proj": dwpe}
    small_grads = {"ffn1_norm": d_ffn1_norm, "mix_norm": d_mix_norm, "ffn2_norm": d_ffn2_norm,
                   "ple_norm": d_ple_norm, "a_q_norm": fold(dgqa), "a_k_norm": fold(dgka),
                   "b_q_norm": fold(dgqb), "b_k_norm": fold(dgkb), "a_rel_bias": _rel_bias_grad(dbias_a),
                   "b_sinks": jnp.sum(dsink_b, axis=1)}
    return loss, dx, big_grads, small_grads


def kernel(x, p, ffn1_norm, ffn1_w_gu, ffn1_w_down, mix_norm, w_in, a_q_norm, a_k_norm, a_rel_bias, b_q_norm, b_k_norm, b_sinks, w_gate, w_proj_a, w_proj_b, w_out, ffn2_norm, ffn2_w_gu, ffn2_w_down, ple_norm, w_ple_gate, w_ple_proj, loss_target, m_ffn1_norm, m_ffn1_w_gu, m_ffn1_w_down, m_mix_norm, m_w_in, m_a_q_norm, m_a_k_norm, m_a_rel_bias, m_b_q_norm, m_b_k_norm, m_b_sinks, m_w_gate, m_w_proj_a, m_w_proj_b, m_w_out, m_ffn2_norm, m_ffn2_w_gu, m_ffn2_w_down, m_ple_norm, m_w_ple_gate, m_w_ple_proj, v_ffn1_norm, v_ffn1_w_gu, v_ffn1_w_down, v_mix_norm, v_w_in, v_a_q_norm, v_a_k_norm, v_a_rel_bias, v_b_q_norm, v_b_k_norm, v_b_sinks, v_w_gate, v_w_proj_a, v_w_proj_b, v_w_out, v_ffn2_norm, v_ffn2_w_gu, v_ffn2_w_down, v_ple_norm, v_w_ple_gate, v_w_ple_proj):
    given = dict(locals())
    n_batch, s, d = x.shape
    t = n_batch * s
    xt = x.reshape(t, d)
    pt = p.reshape(t, p.shape[-1])
    tgt = loss_target.reshape(t, d)

    shards = []
    for name in BIG:
        (sb,) = _ew("cast_" + name, lambda v: (v,), [given[name][0]], [BF16])
        shards.append(sb)
    gathered = dict(zip(BIG, _all_gather_weights(shards)))
    w = dict(gathered)
    for name in ("ffn1_w_down", "ffn2_w_down", "w_out", "w_ple_gate"):
        g = gathered[name]
        w[name] = g.reshape(N_CHIPS * g.shape[1], g.shape[2])
    g = gathered["w_in"]
    w["w_in"] = jnp.transpose(g, (1, 0, 2)).reshape(g.shape[1], N_CHIPS * g.shape[2])
    small = {name: given[name] for name in SMALL}

    loss, dx, big_grads, small_grads = _local_step(xt, pt, tgt, n_batch, w, small)

    slotted = []
    for name in BIG:
        gshape = gathered[name].shape
        gr = big_grads[name]
        if name == "w_in":
            gr = jnp.transpose(gr.reshape(gshape[1], N_CHIPS, gshape[2]), (1, 0, 2))
        slotted.append(gr.reshape(N_CHIPS, 2, gshape[1] // 2, gshape[2]))
    core = lax.axis_index("c").astype(jnp.int32).reshape(1)
    got = _exchange_halves(slotted)
    sums = [_chip_sum("chip_sum_" + name, a, b, core) for name, a, b in zip(BIG, slotted, got)]
    terms = _scatter_chip_sums(sums)

    grads, deltas, new_m, new_v = {}, {}, {}, {}
    for name, tm_ in zip(BIG, terms):
        gw, dl, nm, nv = _adamw_terms("adamw_" + name, tm_, given[name][0], given["m_" + name][0],
                                      given["v_" + name][0])
        grads[name], deltas[name], new_m[name], new_v[name] = gw[None], dl[None], nm[None], nv[None]

    small_shapes = [given[name].shape for name in SMALL] + [()]
    g_pack = _all_reduce_small(_pack([small_grads[name] for name in SMALL] + [loss]))
    zero = jnp.zeros((), F32)
    w_pack = _pack([given[name] for name in SMALL] + [zero])
    m_pack = _pack([given["m_" + name] for name in SMALL] + [zero])
    v_pack = _pack([given["v_" + name] for name in SMALL] + [zero])
    d_pack, nm_pack, nv_pack = _ew("adamw_small", lambda wv, gv, mv, vv: _adamw_math(wv, gv, mv, vv),
                                   [w_pack, g_pack, m_pack, v_pack], [F32] * 3)
    g_small = _unpack(g_pack, small_shapes)
    loss_total = g_small[-1]
    for name, gv, dv, mv, vv in zip(SMALL, g_small, _unpack(d_pack, small_shapes), _unpack(nm_pack, small_shapes),
                                    _unpack(nv_pack, small_shapes)):
        grads[name], deltas[name], new_m[name], new_v[name] = gv, dv, mv, vv

    return (loss_total, dx.reshape(x.shape), *[grads[n] for n in WEIGHTS], *[deltas[n] for n in WEIGHTS],
            *[new_m[n] for n in WEIGHTS], *[new_v[n] for n in WEIGHTS])
```

```python
import functools

import numpy as np
import jax
import jax.numpy as jnp
from jax import lax
from jax.experimental import pallas as pl
from jax.experimental.pallas import tpu as pltpu

F32 = jnp.float32
BF16 = jnp.bfloat16

CHUNK = 64
HEAD_DIM = 64
A_PREV_CHUNKS = 8
A_MAX_REL = 128
N_HEADS = 8
B_KV_HEADS = 2
B_PREV_CHUNKS = 2
A_WIDTH = N_HEADS * HEAD_DIM
B_KV_WIDTH = B_KV_HEADS * HEAD_DIM
EPS = 1e-6
NEG_INF = -1e30
ATTN_SCALE = HEAD_DIM ** -0.5
Q_BLOCK = 128
PAIR = 2 * HEAD_DIM

ADAM_LR = 0.001
ADAM_B1 = 0.9
ADAM_B2 = 0.999
ADAM_EPS = 1e-08
ADAM_WD = 0.01
ADAM_STEP = 10

N_CHIPS = 4
N_DEV = 8
VMEM_LIMIT_V7X = 56 * 1024 * 1024
MESH = pl.DeviceIdType.MESH
ANY = pl.BlockSpec(memory_space=pl.ANY)

_DN = {
    "nn": (((1,), (0,)), ((), ())),
    "nt": (((1,), (1,)), ((), ())),
    "tn": (((0,), (0,)), ((), ())),
}


def _pick(n, target, mult=128):
    best = None
    for d in range(mult, min(n, target) + 1, mult):
        if n % d == 0:
            best = d
    return n if best is None else best


def _dot(a, b, mode):
    return lax.dot_general(a.astype(BF16), b.astype(BF16), _DN[mode], preferred_element_type=F32)


def _params():
    return pltpu.CompilerParams(vmem_limit_bytes=VMEM_LIMIT_V7X)


class _Comm:
    def __init__(self, ins, outs, aliases, sems, start, finish):
        self.ins, self.outs, self.aliases, self.sems = list(ins), list(outs), dict(aliases), list(sems)
        self.start, self.finish = start, finish
        self.results = None


class _CommPlumbing:
    def __init__(self, comms, n_in, n_out, n_scratch):
        self.comms = list(comms)
        self.n_in, self.n_out, self.n_scratch = n_in, n_out, n_scratch
        self.args = [a for cm in self.comms for a in cm.ins]
        self.out_shape = [o for cm in self.comms for o in cm.outs]
        self.scratch = [s for cm in self.comms for s in cm.sems]
        self.aliases = {}
        i0, o0 = n_in, n_out
        for cm in self.comms:
            for a, b in cm.aliases.items():
                self.aliases[i0 + a] = o0 + b
            i0 += len(cm.ins)
            o0 += len(cm.outs)

    def run(self, in_refs, out_refs, scratch_refs, first, last):
        if not self.comms:
            return
        parts = []
        i0, o0, s0 = self.n_in, self.n_out, self.n_scratch
        for cm in self.comms:
            parts.append((in_refs[i0:i0 + len(cm.ins)], out_refs[o0:o0 + len(cm.outs)],
                          scratch_refs[s0:s0 + len(cm.sems)]))
            i0 += len(cm.ins)
            o0 += len(cm.outs)
            s0 += len(cm.sems)

        @pl.when(first)
        def _():
            for cm, part in zip(self.comms, parts):
                cm.start(*part)

        @pl.when(last)
        def _():
            for cm, part in zip(self.comms, parts):
                cm.finish(*part)

    def deliver(self, results):
        o0 = self.n_out
        for cm in self.comms:
            cm.results = list(results[o0:o0 + len(cm.outs)])
            o0 += len(cm.outs)
        return list(results[:self.n_out])


def _mm(name, mode, grid, pairs, extras, outs, acc_shape, epilogue, steps=None, comms=()):
    ni, nj, nk = grid
    n_in = 2 * len(pairs) + len(extras)
    n_out = len(outs)
    n_acc = len(pairs) if steps is None else 1
    plumb = _CommPlumbing(comms, n_in, n_out, n_acc)
    n_all_in = n_in + len(plumb.args)
    n_all_out = n_out + len(plumb.out_shape)

    def body(*refs):
        in_refs = refs[:n_all_in]
        out_refs = refs[n_all_in:n_all_in + n_all_out]
        scratch = refs[n_all_in + n_all_out:]
        accs = scratch[:n_acc]
        i = pl.program_id(0)
        j = pl.program_id(1)
        k = pl.program_id(2)

        @pl.when(k == 0)
        def _():
            for acc in accs:
                acc[...] = jnp.zeros(acc.shape, F32)

        def contrib(p, acc):
            acc[...] += _dot(in_refs[2 * p][...], in_refs[2 * p + 1][...], mode)

        if steps is None:
            for p in range(len(pairs)):
                contrib(p, accs[p])
        else:
            lo = 0
            for p, n in enumerate(steps):
                pl.when((k >= lo) & (k < lo + n))(functools.partial(contrib, p, accs[0]))
                lo += n

        @pl.when(k == nk - 1)
        def _():
            epilogue([acc[...] for acc in accs], in_refs[2 * len(pairs):n_in], out_refs[:n_out], (i, j))

        plumb.run(in_refs, out_refs, scratch, (i == 0) & (j == 0) & (k == 0),
                  (i == ni - 1) & (j == nj - 1) & (k == nk - 1))

    args, in_specs = [], []
    for a, a_spec, b, b_spec in pairs:
        args += [a, b]
        in_specs += [a_spec, b_spec]
    for e, e_spec in extras:
        args.append(e)
        in_specs.append(e_spec)
    res = pl.pallas_call(
        body,
        name=name,
        grid=grid,
        in_specs=in_specs + [ANY] * len(plumb.args),
        out_specs=[s for _, s in outs] + [ANY] * len(plumb.out_shape),
        out_shape=[o for o, _ in outs] + plumb.out_shape,
        scratch_shapes=[pltpu.VMEM(acc_shape, F32) for _ in range(n_acc)] + plumb.scratch,
        input_output_aliases=plumb.aliases,
        compiler_params=_params(),
    )(*args, *plumb.args)
    return plumb.deliver(res)


def _sds(shape, dtype):
    return jax.ShapeDtypeStruct(shape, dtype)


def _accumulate(ref, value, first):
    @pl.when(first)
    def _():
        ref[...] = value

    @pl.when(jnp.logical_not(first))
    def _():
        ref[...] += value


def _rms_fwd(name, x, gain):
    t, d = x.shape
    tm = _pick(t, 512, 8)

    def body(x_ref, g_ref, y_ref):
        xv = x_ref[...]
        rstd = lax.rsqrt(jnp.mean(xv * xv, axis=-1, keepdims=True) + EPS)
        y_ref[...] = (xv * rstd * g_ref[...]).astype(BF16)

    return pl.pallas_call(
        body, name=name, grid=(t // tm,),
        in_specs=[pl.BlockSpec((tm, d), lambda i: (i, 0)), pl.BlockSpec((1, d), lambda i: (0, 0))],
        out_specs=pl.BlockSpec((tm, d), lambda i: (i, 0)),
        out_shape=_sds((t, d), BF16),
        compiler_params=_params(),
    )(x, gain)


def _rms_bwd_epilogue(accs, extras, outs, ij):
    x_ref, g_ref, r_ref = extras
    dh_ref, dhb_ref, dg_ref = outs
    dn = accs[0]
    xv = x_ref[...]
    rstd = lax.rsqrt(jnp.mean(xv * xv, axis=-1, keepdims=True) + EPS)
    xhat = xv * rstd
    gd = dn * g_ref[...]
    dx = rstd * (gd - xhat * jnp.mean(gd * xhat, axis=-1, keepdims=True))
    dh = r_ref[...] + dx
    dh_ref[...] = dh
    dhb_ref[...] = dh.astype(BF16)
    _accumulate(dg_ref, jnp.sum(dn * xhat, axis=0, keepdims=True), ij[0] == 0)


def _rms_bwd_io(x, gain, dres, tm):
    t, d = x.shape
    row = pl.BlockSpec((tm, d), lambda i, j, k: (i, 0))
    extras = [(x, row), (gain, pl.BlockSpec((1, d), lambda i, j, k: (0, 0))), (dres, row)]
    outs = [(_sds((t, d), F32), row), (_sds((t, d), BF16), row),
            (_sds((1, d), F32), pl.BlockSpec((1, d), lambda i, j, k: (0, 0)))]
    return extras, outs


def _ffn_fwd(tag, h, gain, wgu, wd, hooks):
    t, d = h.shape
    fs = wgu.shape[2]
    f = 2 * fs
    tm = _pick(t, 512, 8)
    n = _rms_fwd(tag + "_norm", h, gain)

    def up_epilogue(accs, extras, outs, ij):
        g, u = accs
        gu_ref, a_ref = outs
        gu_ref[0] = g.astype(BF16)
        gu_ref[1] = u.astype(BF16)
        a_ref[...] = (g * jax.nn.sigmoid(g) * u).astype(BF16)

    a_spec = pl.BlockSpec((tm, d), lambda i, j, k: (i, 0))
    gu, a = _mm(
        tag + "_up", "nn", (t // tm, 2, 1),
        [(n, a_spec, wgu, pl.BlockSpec((None, d, fs), lambda i, j, k: (j, 0, 0))),
         (n, a_spec, wgu, pl.BlockSpec((None, d, fs), lambda i, j, k: (j + 2, 0, 0)))],
        [],
        [(_sds((2, t, f), BF16), pl.BlockSpec((2, tm, fs), lambda i, j, k: (0, i, j))),
         (_sds((t, f), BF16), pl.BlockSpec((tm, fs), lambda i, j, k: (i, j)))],
        (tm, fs), up_epilogue, comms=hooks.get("up", lambda: ())())

    def down_epilogue(accs, extras, outs, ij):
        outs[0][...] = extras[0][...] + 0.5 * accs[0]

    row = pl.BlockSpec((tm, d), lambda i, j, k: (i, 0))
    (h_new,) = _mm(
        tag + "_down", "nn", (t // tm, 1, 2),
        [(a, pl.BlockSpec((tm, fs), lambda i, j, k: (i, k)), wd, pl.BlockSpec((fs, d), lambda i, j, k: (k, 0)))],
        [(h, row)], [(_sds((t, d), F32), row)], (tm, d), down_epilogue, comms=hooks.get("down", lambda: ())())
    return h_new, (n, gu, a)


def _ffn_bwd(tag, dh, dh_b, h, gain, wgu, wd, saved, hooks):
    n, gu, a = saved
    t, d = h.shape
    fs = wgu.shape[2]
    f = 2 * fs
    tm = _pick(t, 512, 8)
    tk = _pick(t, 512, 8)

    def dact_epilogue(accs, extras, outs, ij):
        da = 0.5 * accs[0]
        g = extras[0][0].astype(F32)
        u = extras[0][1].astype(F32)
        sg = jax.nn.sigmoid(g)
        outs[0][0] = (da * u * sg * (1.0 + g * (1.0 - sg))).astype(BF16)
        outs[0][1] = (da * g * sg).astype(BF16)

    gu_spec = pl.BlockSpec((2, tm, fs), lambda i, j, k: (0, i, j))
    (dgu,) = _mm(
        tag + "_dact", "nt", (t // tm, 2, 1),
        [(dh_b, pl.BlockSpec((tm, d), lambda i, j, k: (i, 0)), wd, pl.BlockSpec((fs, d), lambda i, j, k: (j, 0)))],
        [(gu, gu_spec)], [(_sds((2, t, f), BF16), gu_spec)], (tm, fs), dact_epilogue)

    def cast_epilogue(accs, extras, outs, ij):
        outs[0][...] = accs[0].astype(BF16)

    (dwgu,) = _mm(
        tag + "_dwgu", "tn", (1, 4, t // tk),
        [(n, pl.BlockSpec((tk, d), lambda i, j, k: (k, 0)),
          dgu, pl.BlockSpec((None, tk, fs), lambda i, j, k: (j // 2, k, j % 2)))],
        [], [(_sds((4, d, fs), BF16), pl.BlockSpec((None, d, fs), lambda i, j, k: (j, 0, 0)))], (d, fs), cast_epilogue,
        comms=hooks.get("dwgu", lambda: ())())

    def half_epilogue(accs, extras, outs, ij):
        outs[0][...] = (0.5 * accs[0]).astype(BF16)

    (dwd,) = _mm(
        tag + "_dwd", "tn", (2, 1, t // tk),
        [(a, pl.BlockSpec((tk, fs), lambda i, j, k: (k, i)), dh_b, pl.BlockSpec((tk, d), lambda i, j, k: (k, 0)))],
        [], [(_sds((f, d), BF16), pl.BlockSpec((fs, d), lambda i, j, k: (i, 0)))], (fs, d), half_epilogue,
        comms=hooks.get("dwd", lambda g: ())(dwgu))

    extras, outs = _rms_bwd_io(h, gain, dh, tm)
    dh_in, dh_in_b, dgain = _mm(
        tag + "_dnorm", "nt", (t // tm, 1, 4),
        [(dgu, pl.BlockSpec((None, tm, fs), lambda i, j, k: (k // 2, i, k % 2)),
          wgu, pl.BlockSpec((None, d, fs), lambda i, j, k: (k, 0, 0)))],
        extras, outs, (tm, d), _rms_bwd_epilogue, comms=hooks.get("dnorm", lambda g, w: ())(dwgu, dwd))
    return dh_in, dh_in_b, dgain, dwgu, dwd


def _lane_lo(shape):
    return lax.broadcasted_iota(jnp.int32, shape, 1) < HEAD_DIM


def _pair_norm(xv, gain):
    lo = _lane_lo(xv.shape)
    x2 = xv * xv
    ms_lo = jnp.sum(jnp.where(lo, x2, 0.0), axis=-1, keepdims=True) * (1.0 / HEAD_DIM)
    ms_hi = jnp.sum(jnp.where(lo, 0.0, x2), axis=-1, keepdims=True) * (1.0 / HEAD_DIM)
    rstd = jnp.where(lo, lax.rsqrt(ms_lo + EPS), lax.rsqrt(ms_hi + EPS))
    xhat = xv * rstd
    return xhat * gain, xhat, rstd


def _pair_norm_bwd(dn, xhat, rstd, gain):
    lo = _lane_lo(dn.shape)
    gd = dn * gain
    t = gd * xhat
    m_lo = jnp.sum(jnp.where(lo, t, 0.0), axis=-1, keepdims=True) * (1.0 / HEAD_DIM)
    m_hi = jnp.sum(jnp.where(lo, 0.0, t), axis=-1, keepdims=True) * (1.0 / HEAD_DIM)
    dx = rstd * (gd - xhat * jnp.where(lo, m_lo, m_hi))
    return dx, jnp.sum(dn * xhat, axis=0, keepdims=True)


def _half(xv, hi):
    lo = _lane_lo(xv.shape)
    return jnp.where(lo, 0, xv) if hi else jnp.where(lo, xv, 0)


def _head_place(h, group):
    kh = h // group
    return h // 2, h % 2, kh // 2, kh % 2


def _attn_window(i, prev):
    q0 = i * Q_BLOCK
    start = jnp.maximum(q0 - prev, 0)
    off = start - (q0 - prev)
    return pl.multiple_of(start, Q_BLOCK), pl.multiple_of(off, Q_BLOCK)


def _attn_specs(cfg, s, nq):
    kw = cfg["kw"]
    q_spec = pl.BlockSpec((Q_BLOCK, A_WIDTH), lambda b, i: (b * nq + i, cfg["qblk"]))
    k_spec = pl.BlockSpec((s, kw), lambda b, i: (b, cfg["kblk"]))
    v_spec = pl.BlockSpec((s, kw), lambda b, i: (b, cfg["vblk"]))
    return q_spec, k_spec, v_spec


def _const_spec(shape):
    return pl.BlockSpec(shape, lambda b, i: (0,) * len(shape))


def _attn_fwd(name, qkv, bias_t, sink, gq, gk, cfg, n_batch, comms=()):
    t = qkv.shape[0]
    s = t // n_batch
    nq = s // Q_BLOCK
    prev, group, kw = cfg["prev"], cfg["group"], cfg["kw"]
    w = prev + Q_BLOCK
    wext = bias_t.shape[1]
    plumb = _CommPlumbing(comms, 7, 2, 1)
    n_all_in = 7 + len(plumb.args)
    n_all_out = 2 + len(plumb.out_shape)

    def body(*refs):
        q_ref, k_ref, v_ref, bias_ref, sink_ref, gq_ref, gk_ref = refs[:7]
        y_ref, lse_ref = refs[n_all_in:n_all_in + 2]
        kn_ref = refs[n_all_in + n_all_out]
        i = pl.program_id(1)
        plumb.run(refs[:n_all_in], refs[n_all_in:n_all_in + n_all_out], refs[n_all_in + n_all_out:],
                  (pl.program_id(0) == 0) & (i == 0), (pl.program_id(0) == n_batch - 1) & (i == nq - 1))

        @pl.when(i == 0)
        def _():
            for jk in range(kw // PAIR):
                kn, _, _ = _pair_norm(k_ref[:, pl.ds(jk * PAIR, PAIR)].astype(F32), gk_ref[...])
                kn_ref[:, pl.ds(jk * PAIR, PAIR)] = kn.astype(BF16)

        start, off = _attn_window(i, prev)
        sub = lax.broadcasted_iota(jnp.int32, (N_HEADS, Q_BLOCK), 0)
        lse = jnp.zeros((N_HEADS, Q_BLOCK), F32)
        for jq in range(N_HEADS // 2):
            qn, _, _ = _pair_norm(q_ref[:, pl.ds(jq * PAIR, PAIR)].astype(F32), gq_ref[...])
            qn = qn * ATTN_SCALE
            o_pair = jnp.zeros((Q_BLOCK, PAIR), F32)
            for hq in range(2):
                h = 2 * jq + hq
                _, _, jk, hk = _head_place(h, group)
                qm = _half(qn, hq)
                if hq != hk:
                    qm = pltpu.roll(qm, HEAD_DIM, 1)
                k_w = kn_ref[pl.ds(start, w), pl.ds(jk * PAIR, PAIR)]
                st = _dot(k_w, qm, "nt") + bias_ref[h, pl.ds(off, w), :]
                sk = sink_ref[h:h + 1, 0:1]
                m = jnp.maximum(jnp.max(st, axis=0, keepdims=True), sk)
                p = jnp.exp(st - m)
                l = jnp.sum(p, axis=0, keepdims=True) + jnp.exp(sk - m)
                v_w = _half(v_ref[pl.ds(start, w), pl.ds(jk * PAIR, PAIR)], hk)
                o = _dot(p * (1.0 / l), v_w, "tn")
                if hq != hk:
                    o = pltpu.roll(o, HEAD_DIM, 1)
                o_pair = o_pair + o
                lse = jnp.where(sub == h, m + jnp.log(l), lse)
            y_ref[:, pl.ds(jq * PAIR, PAIR)] = o_pair.astype(BF16)
        lse_ref[...] = lse

    q_spec, k_spec, v_spec = _attn_specs(cfg, s, nq)
    res = pl.pallas_call(
        body, name=name, grid=(n_batch, nq),
        in_specs=[q_spec, k_spec, v_spec, _const_spec((N_HEADS, wext, Q_BLOCK)), _const_spec((N_HEADS, 128)),
                  _const_spec((1, PAIR)), _const_spec((1, PAIR))] + [ANY] * len(plumb.args),
        out_specs=[pl.BlockSpec((Q_BLOCK, A_WIDTH), lambda b, i: (b * nq + i, 0)),
                   pl.BlockSpec((None, N_HEADS, Q_BLOCK), lambda b, i: (b * nq + i, 0, 0))]
        + [ANY] * len(plumb.out_shape),
        out_shape=[_sds((t, A_WIDTH), BF16), _sds((t // Q_BLOCK, N_HEADS, Q_BLOCK), F32)] + plumb.out_shape,
        scratch_shapes=[pltpu.VMEM((s, kw), BF16)] + plumb.scratch,
        input_output_aliases=plumb.aliases,
        compiler_params=_params(),
    )(qkv, qkv, qkv, bias_t, sink, gq, gk, *plumb.args)
    return plumb.deliver(res)


def _attn_bwd(name, qkv, bias_t, sink, gq, gk, y, dy, lse, cfg, n_batch, want_dbias, comms=()):
    t = qkv.shape[0]
    s = t // n_batch
    nq = s // Q_BLOCK
    prev, group, kw = cfg["prev"], cfg["group"], cfg["kw"]
    w = prev + Q_BLOCK
    wext = bias_t.shape[1]

    plumb = _CommPlumbing(comms, 10, 7, 3)
    n_all_in = 10 + len(plumb.args)
    n_all_out = 7 + len(plumb.out_shape)

    def body(*refs):
        q_ref, k_ref, v_ref, bias_ref, sink_ref, gq_ref, gk_ref, y_ref, dy_ref, lse_ref = refs[:10]
        dq_ref, dk_ref, dv_ref, db_ref, dsink_ref, dgq_ref, dgk_ref = refs[n_all_in:n_all_in + 7]
        kn_ref, dkn_ref, dvs_ref = refs[n_all_in + n_all_out:n_all_in + n_all_out + 3]
        b = pl.program_id(0)
        i = pl.program_id(1)
        first = (b == 0) & (i == 0)
        plumb.run(refs[:n_all_in], refs[n_all_in:n_all_in + n_all_out], refs[n_all_in + n_all_out:],
                  first, (b == n_batch - 1) & (i == nq - 1))

        @pl.when(i == 0)
        def _():
            for jk in range(kw // PAIR):
                kn, _, _ = _pair_norm(k_ref[:, pl.ds(jk * PAIR, PAIR)].astype(F32), gk_ref[...])
                kn_ref[:, pl.ds(jk * PAIR, PAIR)] = kn.astype(BF16)
            dkn_ref[...] = jnp.zeros(dkn_ref.shape, F32)
            dvs_ref[...] = jnp.zeros(dvs_ref.shape, F32)

        @pl.when(first)
        def _():
            db_ref[...] = jnp.zeros(db_ref.shape, F32)
            dsink_ref[...] = jnp.zeros(dsink_ref.shape, F32)
            dgq_ref[...] = jnp.zeros(dgq_ref.shape, F32)
            dgk_ref[...] = jnp.zeros(dgk_ref.shape, F32)

        start, off = _attn_window(i, prev)
        sub_lo = lax.broadcasted_iota(jnp.int32, (PAIR, Q_BLOCK), 0) < HEAD_DIM
        for jq in range(N_HEADS // 2):
            cols = pl.ds(jq * PAIR, PAIR)
            qn, q_hat, q_rstd = _pair_norm(q_ref[:, cols].astype(F32), gq_ref[...])
            qn = qn * ATTN_SCALE
            do_pair = dy_ref[:, cols]
            prod_t = (do_pair.astype(F32) * y_ref[:, cols].astype(F32)).T
            dqn = jnp.zeros((Q_BLOCK, PAIR), F32)
            for hq in range(2):
                h = 2 * jq + hq
                _, _, jk, hk = _head_place(h, group)
                kcols = pl.ds(jk * PAIR, PAIR)
                delta = jnp.sum(jnp.where(sub_lo, 0.0, prod_t) if hq else jnp.where(sub_lo, prod_t, 0.0),
                                axis=0, keepdims=True)
                qm = _half(qn, hq)
                do_m = _half(do_pair, hq)
                if hq != hk:
                    qm = pltpu.roll(qm, HEAD_DIM, 1)
                    do_m = pltpu.roll(do_m.astype(F32), HEAD_DIM, 1)
                qm_b = qm.astype(BF16)
                do_b = do_m.astype(BF16)
                k_w = kn_ref[pl.ds(start, w), kcols]
                v_w = v_ref[pl.ds(start, w), kcols]
                lse_row = lse_ref[h:h + 1, :]
                st = _dot(k_w, qm_b, "nt") + bias_ref[h, pl.ds(off, w), :]
                p = jnp.exp(st - lse_row)
                dp = _dot(v_w, do_b, "nt")
                ds = p * (dp - delta)
                dsink_ref[h:h + 1, :] += -jnp.exp(sink_ref[h:h + 1, 0:1] - lse_row) * delta
                if want_dbias:
                    db_ref[h, pl.ds(off, w), :] += ds
                ds_b = ds.astype(BF16)
                dq_h = _half(_dot(ds_b, k_w, "tn"), hk)
                if hq != hk:
                    dq_h = pltpu.roll(dq_h, HEAD_DIM, 1)
                dqn = dqn + dq_h
                dkn_ref[pl.ds(start, w), kcols] += _half(_dot(ds_b, qm_b, "nn"), hk)
                dvs_ref[pl.ds(start, w), kcols] += _half(_dot(p, do_b, "nn"), hk)
            dq_raw, dg = _pair_norm_bwd(dqn * ATTN_SCALE, q_hat, q_rstd, gq_ref[...])
            dq_ref[:, cols] = dq_raw.astype(BF16)
            dgq_ref[...] += dg

        @pl.when(i == nq - 1)
        def _():
            for jk in range(kw // PAIR):
                kcols = pl.ds(jk * PAIR, PAIR)
                _, k_hat, k_rstd = _pair_norm(k_ref[:, kcols].astype(F32), gk_ref[...])
                dk_raw, dg = _pair_norm_bwd(dkn_ref[:, kcols], k_hat, k_rstd, gk_ref[...])
                dk_ref[:, kcols] = dk_raw.astype(BF16)
                dgk_ref[...] += dg
            dv_ref[...] = dvs_ref[...].astype(BF16)

    q_spec, k_spec, v_spec = _attn_specs(cfg, s, nq)
    row = pl.BlockSpec((Q_BLOCK, A_WIDTH), lambda b, i: (b * nq + i, 0))
    kv_out = pl.BlockSpec((s, kw), lambda b, i: (b, 0))
    res = pl.pallas_call(
        body, name=name, grid=(n_batch, nq),
        in_specs=[q_spec, k_spec, v_spec, _const_spec((N_HEADS, wext, Q_BLOCK)), _const_spec((N_HEADS, 128)),
                  _const_spec((1, PAIR)), _const_spec((1, PAIR)), row, row,
                  pl.BlockSpec((None, N_HEADS, Q_BLOCK), lambda b, i: (b * nq + i, 0, 0))] + [ANY] * len(plumb.args),
        out_specs=[row, kv_out, kv_out, _const_spec((N_HEADS, wext, Q_BLOCK)), _const_spec((N_HEADS, 128)),
                   _const_spec((1, PAIR)), _const_spec((1, PAIR))] + [ANY] * len(plumb.out_shape),
        out_shape=[_sds((t, A_WIDTH), BF16), _sds((t, kw), BF16), _sds((t, kw), BF16),
                   _sds((N_HEADS, wext, Q_BLOCK), F32), _sds((N_HEADS, 128), F32),
                   _sds((1, PAIR), F32), _sds((1, PAIR), F32)] + plumb.out_shape,
        scratch_shapes=[pltpu.VMEM((s, kw), BF16), pltpu.VMEM((s, kw), F32), pltpu.VMEM((s, kw), F32)]
        + plumb.scratch,
        input_output_aliases=plumb.aliases,
        compiler_params=_params(),
    )(qkv, qkv, qkv, bias_t, sink, gq, gk, y, dy, lse, *plumb.args)
    return plumb.deliver(res)


def _band_tables(prev_chunks):
    prev = prev_chunks * CHUNK
    wext = 2 * prev + Q_BLOCK
    jj = np.arange(wext)[:, None]
    ii = np.arange(Q_BLOCK)[None, :]
    dist = prev + ii - jj
    rel_chunk = (prev // CHUNK + ii // CHUNK) - jj // CHUNK
    allowed = (rel_chunk >= 0) & (rel_chunk <= prev_chunks)
    return dist, allowed


def _alibi_slopes():
    return np.array([2.0 ** (-8.0 * (h + 1) / N_HEADS) for h in range(N_HEADS)], dtype=np.float32)


def _diag_onehot(prev, wext):
    n_diag = wext + Q_BLOCK - 1
    idx = np.clip(prev + Q_BLOCK - 1 - np.arange(n_diag), -A_MAX_REL, A_MAX_REL) + A_MAX_REL
    onehot = np.zeros((n_diag, 2 * A_MAX_REL + 1), np.float32)
    onehot[np.arange(n_diag), idx] = 1.0
    return onehot


def _bias_a(rel_bias):
    prev = A_PREV_CHUNKS * CHUNK
    _, allowed = _band_tables(A_PREV_CHUNKS)
    wext = allowed.shape[0]
    n_diag = wext + Q_BLOCK - 1
    seq = jnp.dot(rel_bias, jnp.asarray(_diag_onehot(prev, wext).T), precision=lax.Precision.HIGHEST)
    seq = jnp.pad(seq, ((0, 0), (0, 1)))
    rows = jnp.broadcast_to(seq[:, None, :], (N_HEADS, Q_BLOCK, n_diag + 1)).reshape(N_HEADS, -1)
    skew = rows[:, :Q_BLOCK * n_diag].reshape(N_HEADS, Q_BLOCK, n_diag)
    tile = jnp.transpose(skew[:, :, Q_BLOCK - 1:Q_BLOCK - 1 + wext], (0, 2, 1))
    return jnp.where(jnp.asarray(allowed)[None], tile, NEG_INF)


def _bias_b():
    dist, allowed = _band_tables(B_PREV_CHUNKS)
    bias = -_alibi_slopes()[:, None, None] * np.abs(dist).astype(np.float32)[None]
    return jnp.asarray(np.where(allowed[None], bias, np.float32(NEG_INF)).astype(np.float32))


def _rel_bias_grad(db_t):
    prev = A_PREV_CHUNKS * CHUNK
    wext = db_t.shape[1]
    n_diag = wext + Q_BLOCK - 1
    wp = n_diag + Q_BLOCK - 1
    xp = jnp.pad(jnp.transpose(db_t, (0, 2, 1)), ((0, 0), (0, 0), (Q_BLOCK - 1, Q_BLOCK - 1)))
    flat = jnp.pad(xp.reshape(N_HEADS, Q_BLOCK * wp), ((0, 0), (0, Q_BLOCK)))
    skew = flat.reshape(N_HEADS, Q_BLOCK, wp + 1)[:, :, :n_diag]
    diag = jnp.sum(skew, axis=1)
    return jnp.dot(diag, jnp.asarray(_diag_onehot(prev, wext)), precision=lax.Precision.HIGHEST)


def _ew(name, fn, ins, out_dtypes):
    r, c = ins[0].shape
    rb = _pick(r, max(16, (1 << 19) // c), 16)
    spec = pl.BlockSpec((rb, c), lambda i: (i, 0))

    def body(*refs):
        vals = fn(*[ref[...] for ref in refs[:len(ins)]])
        for ref, val in zip(refs[len(ins):], vals):
            ref[...] = val.astype(ref.dtype)

    return pl.pallas_call(
        body, name=name, grid=(r // rb,), in_specs=[spec] * len(ins), out_specs=[spec] * len(out_dtypes),
        out_shape=[_sds((r, c), dt) for dt in out_dtypes], compiler_params=_params(),
    )(*ins)


def _adamw_math(w, g, m, v):
    m = ADAM_B1 * m + (1.0 - ADAM_B1) * g
    v = ADAM_B2 * v + (1.0 - ADAM_B2) * (g * g)
    m_hat = m / (1.0 - ADAM_B1 ** ADAM_STEP)
    v_hat = v / (1.0 - ADAM_B2 ** ADAM_STEP)
    delta = -ADAM_LR * (m_hat / (jnp.sqrt(v_hat) + ADAM_EPS) + ADAM_WD * w)
    return delta, m, v


def _adamw_terms(name, terms, w, m, v):
    r, c = w.shape
    hr = r // 2
    rb = _pick(hr, max(16, (1 << 17) // c), 16)
    nb = hr // rb

    def body(t_ref, w_ref, m_ref, v_ref, g_ref, d_ref, nm_ref, nv_ref):
        g = t_ref[0].astype(F32)
        for k in range(1, N_CHIPS):
            g = g + t_ref[k].astype(F32)
        delta, nm, nv = _adamw_math(w_ref[...], g, m_ref[...], v_ref[...])
        g_ref[...] = g
        d_ref[...] = delta
        nm_ref[...] = nm
        nv_ref[...] = nv

    spec = pl.BlockSpec((rb, c), lambda h, i: (h * nb + i, 0))
    return pl.pallas_call(
        body, name=name, grid=(2, nb),
        in_specs=[pl.BlockSpec((None, N_CHIPS, rb, c), lambda h, i: (h, 0, i, 0)), spec, spec, spec],
        out_specs=[spec] * 4, out_shape=[_sds((r, c), F32)] * 4, compiler_params=_params(),
    )(terms, w, m, v)


def _mesh_place():
    x, y, c = lax.axis_index("x"), lax.axis_index("y"), lax.axis_index("c")
    chips = [(x, 1 - y), (1 - x, y), (1 - x, 1 - y)]
    return x, y, c, chips


def _all_gather_weights(shards):
    n = len(shards)

    def body(*refs):
        ins, outs = refs[:n], refs[n:2 * n]
        local_sem, ici_send, ici_recv, d2d_send, d2d_recv = refs[2 * n:]
        x, y, c, chips = _mesh_place()
        me = 2 * x + y
        sibling = (x, y, 1 - c)
        local, sent = [], []
        for wi in range(n):
            hr = ins[wi].shape[0] // 2
            mine = pl.ds(c * hr, hr)
            loc = pltpu.make_async_copy(ins[wi], outs[wi].at[me], local_sem.at[wi])
            loc.start()
            local.append(loc)
            for k, (tx, ty) in enumerate(chips):
                cp = pltpu.make_async_remote_copy(
                    src_ref=ins[wi].at[mine, :], dst_ref=outs[wi].at[me, mine, :],
                    send_sem=ici_send.at[wi * 3 + k], recv_sem=ici_recv.at[wi * 3 + k],
                    device_id=(tx, ty, c), device_id_type=MESH)
                cp.start()
                sent.append(cp)
        passed = []
        for wi in range(n):
            hr = ins[wi].shape[0] // 2
            mine = pl.ds(c * hr, hr)
            for k, (tx, ty) in enumerate(chips):
                slab = outs[wi].at[2 * tx + ty, mine, :]
                pltpu.make_async_remote_copy(
                    src_ref=slab, dst_ref=slab, send_sem=ici_send.at[wi * 3 + k], recv_sem=ici_recv.at[wi * 3 + k],
                    device_id=(tx, ty, c), device_id_type=MESH).wait_recv()
                fw = pltpu.make_async_remote_copy(
                    src_ref=slab, dst_ref=slab, send_sem=d2d_send.at[wi * 3 + k], recv_sem=d2d_recv.at[wi * 3 + k],
                    device_id=sibling, device_id_type=MESH)
                fw.start()
                passed.append(fw)
        for wi in range(n):
            hr = ins[wi].shape[0] // 2
            theirs = pl.ds((1 - c) * hr, hr)
            for k, (tx, ty) in enumerate(chips):
                slab = outs[wi].at[2 * tx + ty, theirs, :]
                pltpu.make_async_remote_copy(
                    src_ref=slab, dst_ref=slab, send_sem=d2d_send.at[wi * 3 + k], recv_sem=d2d_recv.at[wi * 3 + k],
                    device_id=sibling, device_id_type=MESH).wait_recv()
        for loc in local:
            loc.wait()
        for cp in sent + passed:
            cp.wait_send()

    return pl.pallas_call(
        body, name="all_gather_weights",
        in_specs=[ANY] * n, out_specs=[ANY] * n,
        out_shape=[_sds((N_CHIPS,) + s.shape, s.dtype) for s in shards],
        scratch_shapes=[pltpu.SemaphoreType.DMA((n,)), pltpu.SemaphoreType.DMA((3 * n,)),
                        pltpu.SemaphoreType.DMA((3 * n,)), pltpu.SemaphoreType.DMA((3 * n,)),
                        pltpu.SemaphoreType.DMA((3 * n,))],
    )(*shards)


def _run_comms(name, comms):
    plumb = _CommPlumbing(comms, 0, 0, 0)
    n_in, n_out = len(plumb.args), len(plumb.out_shape)

    def body(*refs):
        parts = []
        i0, o0, s0 = 0, n_in, n_in + n_out
        for cm in plumb.comms:
            parts.append((refs[i0:i0 + len(cm.ins)], refs[o0:o0 + len(cm.outs)], refs[s0:s0 + len(cm.sems)]))
            i0 += len(cm.ins)
            o0 += len(cm.outs)
            s0 += len(cm.sems)
        for cm, part in zip(plumb.comms, parts):
            cm.start(*part)
        for cm, part in zip(plumb.comms, parts):
            cm.finish(*part)

    res = pl.pallas_call(
        body, name=name, in_specs=[ANY] * n_in, out_specs=[ANY] * n_out, out_shape=plumb.out_shape,
        scratch_shapes=plumb.scratch, input_output_aliases=plumb.aliases,
    )(*plumb.args)
    plumb.deliver(res)


def _gather_ici(shards):
    n = len(shards)

    def copies(ins, outs, sems):
        local_sem, send_sem, recv_sem = sems
        x, y, c, chips = _mesh_place()
        me = 2 * x + y
        local, sends, recvs = [], [], []
        for wi in range(n):
            hr = shards[wi].shape[0] // 2
            mine = pl.ds(c * hr, hr)
            local.append(pltpu.make_async_copy(ins[wi], outs[wi].at[me], local_sem.at[wi]))
            for k, (tx, ty) in enumerate(chips):
                sems_k = dict(send_sem=send_sem.at[wi * 3 + k], recv_sem=recv_sem.at[wi * 3 + k],
                              device_id=(tx, ty, c), device_id_type=MESH)
                sends.append(pltpu.make_async_remote_copy(
                    src_ref=ins[wi].at[mine, :], dst_ref=outs[wi].at[me, mine, :], **sems_k))
                slab = outs[wi].at[2 * tx + ty, mine, :]
                recvs.append(pltpu.make_async_remote_copy(src_ref=slab, dst_ref=slab, **sems_k))
        return local, sends, recvs

    def start(ins, outs, sems):
        local, sends, _ = copies(ins, outs, sems)
        for cp in local + sends:
            cp.start()

    def finish(ins, outs, sems):
        local, sends, recvs = copies(ins, outs, sems)
        for cp in local:
            cp.wait()
        for cp in recvs:
            cp.wait_recv()
        for cp in sends:
            cp.wait_send()

    return _Comm(shards, [_sds((N_CHIPS,) + s.shape, s.dtype) for s in shards], {},
                 [pltpu.SemaphoreType.DMA((n,)), pltpu.SemaphoreType.DMA((3 * n,)), pltpu.SemaphoreType.DMA((3 * n,))],
                 start, finish)


def _gather_d2d(gathered):
    n = len(gathered)

    def copies(outs, sems):
        send_sem, recv_sem = sems
        x, y, c, chips = _mesh_place()
        sends, recvs = [], []
        for wi in range(n):
            hr = gathered[wi].shape[1] // 2
            for k, (tx, ty) in enumerate(chips):
                sems_k = dict(send_sem=send_sem.at[wi * 3 + k], recv_sem=recv_sem.at[wi * 3 + k],
                              device_id=(x, y, 1 - c), device_id_type=MESH)
                mine = outs[wi].at[2 * tx + ty, pl.ds(c * hr, hr), :]
                theirs = outs[wi].at[2 * tx + ty, pl.ds((1 - c) * hr, hr), :]
                sends.append(pltpu.make_async_remote_copy(src_ref=mine, dst_ref=mine, **sems_k))
                recvs.append(pltpu.make_async_remote_copy(src_ref=theirs, dst_ref=theirs, **sems_k))
        return sends, recvs

    def start(ins, outs, sems):
        for cp in copies(outs, sems)[0]:
            cp.start()

    def finish(ins, outs, sems):
        sends, recvs = copies(outs, sems)
        for cp in recvs:
            cp.wait_recv()
        for cp in sends:
            cp.wait_send()

    return _Comm(gathered, [_sds(g.shape, g.dtype) for g in gathered], {i: i for i in range(n)},
                 [pltpu.SemaphoreType.DMA((3 * n,)), pltpu.SemaphoreType.DMA((3 * n,))], start, finish)


def _exchange_halves(grads):
    n = len(grads)

    def copies(ins, outs, sems):
        send_sem, recv_sem = sems
        x, y, c, _ = _mesh_place()
        return [pltpu.make_async_remote_copy(
            src_ref=ins[wi].at[t, 1 - c], dst_ref=outs[wi].at[t],
            send_sem=send_sem.at[wi * N_CHIPS + t], recv_sem=recv_sem.at[wi * N_CHIPS + t],
            device_id=(x, y, 1 - c), device_id_type=MESH) for wi in range(n) for t in range(N_CHIPS)]

    def start(ins, outs, sems):
        for cp in copies(ins, outs, sems):
            cp.start()

    def finish(ins, outs, sems):
        for cp in copies(ins, outs, sems):
            cp.wait()

    return _Comm(grads, [_sds((N_CHIPS,) + g.shape[2:], g.dtype) for g in grads], {},
                 [pltpu.SemaphoreType.DMA((N_CHIPS * n,)), pltpu.SemaphoreType.DMA((N_CHIPS * n,))], start, finish)


def _scatter_ici(sums):
    n = len(sums)

    def copies(ins, outs, sems):
        local_sem, send_sem, recv_sem = sems
        x, y, c, chips = _mesh_place()
        me = 2 * x + y
        local, sends, recvs = [], [], []
        for wi in range(n):
            local.append(pltpu.make_async_copy(ins[wi].at[me], outs[wi].at[c, 0], local_sem.at[wi]))
            for k, (tx, ty) in enumerate(chips):
                sems_k = dict(send_sem=send_sem.at[wi * 3 + k], recv_sem=recv_sem.at[wi * 3 + k],
                              device_id=(tx, ty, c), device_id_type=MESH)
                land = outs[wi].at[c, k + 1]
                sends.append(pltpu.make_async_remote_copy(src_ref=ins[wi].at[2 * tx + ty], dst_ref=land, **sems_k))
                recvs.append(pltpu.make_async_remote_copy(src_ref=land, dst_ref=land, **sems_k))
        return local, sends, recvs

    def start(ins, outs, sems):
        local, sends, _ = copies(ins, outs, sems)
        for cp in local + sends:
            cp.start()

    def finish(ins, outs, sems):
        local, sends, recvs = copies(ins, outs, sems)
        for cp in local:
            cp.wait()
        for cp in recvs:
            cp.wait_recv()
        for cp in sends:
            cp.wait_send()

    return _Comm(sums, [_sds((2, N_CHIPS) + s.shape[1:], s.dtype) for s in sums], {},
                 [pltpu.SemaphoreType.DMA((n,)), pltpu.SemaphoreType.DMA((3 * n,)), pltpu.SemaphoreType.DMA((3 * n,))],
                 start, finish)


def _scatter_d2d(terms):
    n = len(terms)

    def copies(outs, sems):
        send_sem, recv_sem = sems
        x, y, c, _ = _mesh_place()
        sends, recvs = [], []
        for wi in range(n):
            sems_w = dict(send_sem=send_sem.at[wi], recv_sem=recv_sem.at[wi],
                          device_id=(x, y, 1 - c), device_id_type=MESH)
            sends.append(pltpu.make_async_remote_copy(src_ref=outs[wi].at[c], dst_ref=outs[wi].at[c], **sems_w))
            recvs.append(pltpu.make_async_remote_copy(src_ref=outs[wi].at[1 - c], dst_ref=outs[wi].at[1 - c], **sems_w))
        return sends, recvs

    def start(ins, outs, sems):
        for cp in copies(outs, sems)[0]:
            cp.start()

    def finish(ins, outs, sems):
        sends, recvs = copies(outs, sems)
        for cp in recvs:
            cp.wait_recv()
        for cp in sends:
            cp.wait_send()

    return _Comm(terms, [_sds(t.shape, t.dtype) for t in terms], {i: i for i in range(n)},
                 [pltpu.SemaphoreType.DMA((n,)), pltpu.SemaphoreType.DMA((n,))], start, finish)


def _chip_sum(name, grad, got, core):
    _, _, hr, c = grad.shape
    rb = _pick(hr, max(16, (1 << 19) // c), 16)

    def body(core_ref, a_ref, b_ref, o_ref):
        o_ref[...] = (a_ref[...].astype(F32) + b_ref[...].astype(F32)).astype(BF16)

    out_spec = pl.BlockSpec((None, rb, c), lambda t, i, core_ref: (t, i, 0))
    return pl.pallas_call(
        body, name=name,
        grid_spec=pltpu.PrefetchScalarGridSpec(
            num_scalar_prefetch=1, grid=(N_CHIPS, hr // rb),
            in_specs=[pl.BlockSpec((None, None, rb, c), lambda t, i, core_ref: (t, core_ref[0], i, 0)), out_spec],
            out_specs=out_spec),
        out_shape=_sds((N_CHIPS, hr, c), BF16), compiler_params=_params(),
    )(core, grad, got)


def _scatter_chip_sums(sums):
    n = len(sums)

    def body(*refs):
        ins, outs = refs[:n], refs[n:2 * n]
        local_sem, ici_send, ici_recv, d2d_send, d2d_recv = refs[2 * n:]
        x, y, c, chips = _mesh_place()
        me = 2 * x + y
        sibling = (x, y, 1 - c)
        local, sent = [], []
        for wi in range(n):
            loc = pltpu.make_async_copy(ins[wi].at[me], outs[wi].at[c, 0], local_sem.at[wi])
            loc.start()
            local.append(loc)
            for k, (tx, ty) in enumerate(chips):
                cp = pltpu.make_async_remote_copy(
                    src_ref=ins[wi].at[2 * tx + ty], dst_ref=outs[wi].at[c, k + 1],
                    send_sem=ici_send.at[wi * 3 + k], recv_sem=ici_recv.at[wi * 3 + k],
                    device_id=(tx, ty, c), device_id_type=MESH)
                cp.start()
                sent.append(cp)
        for wi in range(n):
            local[wi].wait()
            for k in range(N_CHIPS):
                slab = outs[wi].at[c, k]
                if k > 0:
                    tx, ty = chips[k - 1]
                    pltpu.make_async_remote_copy(
                        src_ref=slab, dst_ref=slab, send_sem=ici_send.at[wi * 3 + k - 1],
                        recv_sem=ici_recv.at[wi * 3 + k - 1], device_id=(tx, ty, c), device_id_type=MESH).wait_recv()
                fw = pltpu.make_async_remote_copy(
                    src_ref=slab, dst_ref=slab, send_sem=d2d_send.at[wi * 4 + k], recv_sem=d2d_recv.at[wi * 4 + k],
                    device_id=sibling, device_id_type=MESH)
                fw.start()
                sent.append(fw)
        for wi in range(n):
            for k in range(N_CHIPS):
                slab = outs[wi].at[1 - c, k]
                pltpu.make_async_remote_copy(
                    src_ref=slab, dst_ref=slab, send_sem=d2d_send.at[wi * 4 + k], recv_sem=d2d_recv.at[wi * 4 + k],
                    device_id=sibling, device_id_type=MESH).wait_recv()
        for cp in sent:
            cp.wait_send()

    return pl.pallas_call(
        body, name="grad_scatter_chip_sums",
        in_specs=[ANY] * n, out_specs=[ANY] * n,
        out_shape=[_sds((2, N_CHIPS) + s.shape[1:], s.dtype) for s in sums],
        scratch_shapes=[pltpu.SemaphoreType.DMA((n,)), pltpu.SemaphoreType.DMA((3 * n,)),
                        pltpu.SemaphoreType.DMA((3 * n,)), pltpu.SemaphoreType.DMA((4 * n,)),
                        pltpu.SemaphoreType.DMA((4 * n,))],
    )(*sums)


def _all_reduce_small(pack):
    r = pack.shape[0]

    def body(p_ref, o_ref, land_ref, send_sem, recv_sem):
        x, y, c, _ = _mesh_place()
        me = 4 * x + 2 * y + c
        flips = [(k >> 2 & 1, k >> 1 & 1, k & 1) for k in range(1, N_DEV)]

        def peer(fx, fy, fc):
            return (1 - x if fx else x, 1 - y if fy else y, 1 - c if fc else c)

        land_ref[me] = p_ref[...]
        sent = []
        for k, flip in enumerate(flips):
            cp = pltpu.make_async_remote_copy(
                src_ref=p_ref, dst_ref=land_ref.at[me], send_sem=send_sem.at[k], recv_sem=recv_sem.at[k],
                device_id=peer(*flip), device_id_type=MESH)
            cp.start()
            sent.append(cp)
        for k, flip in enumerate(flips):
            px, py, pc = peer(*flip)
            slot = land_ref.at[4 * px + 2 * py + pc]
            pltpu.make_async_remote_copy(
                src_ref=slot, dst_ref=slot, send_sem=send_sem.at[k], recv_sem=recv_sem.at[k],
                device_id=(px, py, pc), device_id_type=MESH).wait_recv()
        total = land_ref[0]
        for d in range(1, N_DEV):
            total = total + land_ref[d]
        o_ref[...] = total
        for cp in sent:
            cp.wait_send()

    vmem = pl.BlockSpec(memory_space=pltpu.VMEM)
    return pl.pallas_call(
        body, name="all_reduce_small", in_specs=[vmem], out_specs=vmem, out_shape=_sds((r, 128), F32),
        scratch_shapes=[pltpu.VMEM((N_DEV, r, 128), F32), pltpu.SemaphoreType.DMA((N_DEV - 1,)),
                        pltpu.SemaphoreType.DMA((N_DEV - 1,))],
    )(pack)


PACK_TILE = 8 * 128


def _pack(items):
    rows, i = [], 0
    while i < len(items):
        j = i
        while j < len(items) and items[j].size == items[i].size:
            j += 1
        group = jnp.stack([it.reshape(-1).astype(F32) for it in items[i:j]])
        rows.append(jnp.pad(group, ((0, 0), (0, -group.shape[1] % PACK_TILE))).reshape(-1, 128))
        i = j
    return jnp.concatenate(rows, axis=0)


def _unpack(pack, shapes):
    out, row = [], 0
    for shp in shapes:
        size = int(np.prod(shp))
        nrow = -(-size // PACK_TILE) * (PACK_TILE // 128)
        out.append(pack[row:row + nrow].reshape(-1)[:size].reshape(shp))
        row += nrow
    return out


BIG = ["ffn1_w_gu", "ffn1_w_down", "w_in", "w_gate", "w_proj_a", "w_proj_b", "w_out",
       "ffn2_w_gu", "ffn2_w_down", "w_ple_gate", "w_ple_proj"]
SMALL = ["ffn1_norm", "mix_norm", "ffn2_norm", "ple_norm", "a_q_norm", "a_k_norm", "b_q_norm", "b_k_norm",
         "a_rel_bias", "b_sinks"]
WEIGHTS = ["ffn1_norm", "ffn1_w_gu", "ffn1_w_down", "mix_norm", "w_in", "a_q_norm", "a_k_norm", "a_rel_bias",
           "b_q_norm", "b_k_norm", "b_sinks", "w_gate", "w_proj_a", "w_proj_b", "w_out", "ffn2_norm",
           "ffn2_w_gu", "ffn2_w_down", "ple_norm", "w_ple_gate", "w_ple_proj"]
ATTN_A = dict(prev=A_PREV_CHUNKS * CHUNK, group=1, kw=A_WIDTH, qblk=0, kblk=1, vblk=2)
ATTN_B = dict(prev=B_PREV_CHUNKS * CHUNK, group=N_HEADS // B_KV_HEADS, kw=B_KV_WIDTH, qblk=3,
              kblk=4 * A_WIDTH // B_KV_WIDTH, vblk=4 * A_WIDTH // B_KV_WIDTH + 1)


def _cast_epilogue(accs, extras, outs, ij):
    for acc, out in zip(accs, outs):
        out[...] = acc.astype(out.dtype)


GATHER_FIRST = ["ffn1_w_gu", "ffn1_w_down"]
GATHER_MIXER = ["w_in", "w_gate", "w_proj_a", "w_proj_b", "w_out"]
GATHER_LATE = ["ffn2_w_gu", "ffn2_w_down", "w_ple_gate", "w_ple_proj"]
ROW_SHARDED = ("ffn1_w_down", "ffn2_w_down", "w_out", "w_ple_gate")


def _slotted(name, grad):
    if name == "w_in":
        rows, cols = grad.shape
        grad = jnp.transpose(grad.reshape(rows, N_CHIPS, cols // N_CHIPS), (1, 0, 2))
    elif name in ROW_SHARDED:
        grad = grad.reshape(N_CHIPS, grad.shape[0] // N_CHIPS, grad.shape[1])
    return grad.reshape(N_CHIPS, 2, grad.shape[1] // 2, grad.shape[2])


def _local_step(xt, pt, tgt, n_batch, shards, small, core):
    t, d = xt.shape
    tm = _pick(t, 512, 8)
    tk = _pick(t, 512, 8)
    nt = t // tm
    row = pl.BlockSpec((tm, d), lambda i, j, k: (i, 0))
    gs = shards["w_gate"].shape[1]
    ps = shards["w_proj_a"].shape[1]
    es = shards["w_ple_proj"].shape[1]
    pdim = pt.shape[1]
    ncols = N_CHIPS * shards["w_in"].shape[1]
    tin = ncols // 2
    assert 2 * gs == d and 4 * ps == d and 4 * es == d and tin % 128 == 0

    w = {}

    def publish(names, arrays):
        for name, g in zip(names, arrays):
            if name in ROW_SHARDED:
                g = g.reshape(N_CHIPS * g.shape[1], g.shape[2])
            elif name == "w_in":
                g = jnp.transpose(g, (1, 0, 2)).reshape(g.shape[1], N_CHIPS * g.shape[2])
            w[name] = g

    stage = {}

    def gather_ici(names):
        def make():
            stage["ici"] = _gather_ici([shards[n] for n in names])
            return [stage["ici"]]
        return make

    def gather_d2d():
        stage["d2d"] = _gather_d2d(stage["ici"].results)
        return [stage["d2d"]]

    class GradPipe:
        def __init__(self, names):
            self.names = names

        def exchange(self, grads):
            self.grads = [_slotted(n, g) for n, g in zip(self.names, grads)]
            self.x = _exchange_halves(self.grads)
            return self.x

        def scatter(self):
            sums = [_chip_sum("chip_sum_" + n, g, got, core)
                    for n, g, got in zip(self.names, self.grads, self.x.results)]
            self.s = _scatter_ici(sums)
            return self.s

        def forward(self):
            self.f = _scatter_d2d(self.s.results)
            return self.f

        def terms(self):
            return dict(zip(self.names, self.f.results))

    publish(GATHER_FIRST, _all_gather_weights([shards[n] for n in GATHER_FIRST]))
    h1, ffn1_saved = _ffn_fwd("ffn1", xt, small["ffn1_norm"], w["ffn1_w_gu"], w["ffn1_w_down"],
                              {"up": gather_ici(GATHER_MIXER), "down": gather_d2d})
    publish(GATHER_MIXER, stage["d2d"].results)
    w_in, wgate, wpa, wpb, wout = [w[n] for n in GATHER_MIXER]
    un = _rms_fwd("mix_norm", h1, small["mix_norm"])
    (qkv,) = _mm(
        "qkv", "nn", (nt, 2, 1),
        [(un, row, w_in, pl.BlockSpec((d, tin), lambda i, j, k: (0, j)))], [],
        [(_sds((t, ncols), BF16), pl.BlockSpec((tm, tin), lambda i, j, k: (i, j)))], (tm, tin), _cast_epilogue)

    def gate_epilogue(accs, extras, outs, ij):
        outs[0][...] = jax.nn.sigmoid(accs[0]).astype(BF16)

    (gates,) = _mm(
        "gate", "nn", (nt, 4, 1),
        [(un, row, wgate, pl.BlockSpec((None, d, gs), lambda i, j, k: (j, 0, 0)))], [],
        [(_sds((2, t, d), BF16), pl.BlockSpec((None, tm, gs), lambda i, j, k: (j // 2, i, j % 2)))],
        (tm, gs), gate_epilogue)

    bias_a = _bias_a(small["a_rel_bias"][0])
    bias_b = _bias_b()
    sink_a = jnp.full((N_HEADS, 128), NEG_INF, F32)
    sink_b = jnp.broadcast_to(small["b_sinks"][0][:, None], (N_HEADS, 128))
    gqa, gka, gqb, gkb = [jnp.tile(small[k], (1, 2)) for k in ("a_q_norm", "a_k_norm", "b_q_norm", "b_k_norm")]
    ya, lse_a = _attn_fwd("attn_a_fwd", qkv, bias_a, sink_a, gqa, gka, ATTN_A, n_batch,
                          comms=gather_ici(GATHER_LATE)())
    yb, lse_b = _attn_fwd("attn_b_fwd", qkv, bias_b, sink_b, gqb, gkb, ATTN_B, n_batch, comms=gather_d2d())
    publish(GATHER_LATE, stage["d2d"].results)
    wpg, wpe = w["w_ple_gate"], w["w_ple_proj"]

    def merge_epilogue(accs, extras, outs, ij):
        pa, pb = accs
        outs[0][...] = (extras[0][...].astype(F32) * pa + extras[1][...].astype(F32) * pb).astype(BF16)
        outs[1][...] = pa.astype(BF16)
        outs[2][...] = pb.astype(BF16)

    y_spec = pl.BlockSpec((tm, A_WIDTH), lambda i, j, k: (i, 0))
    proj_spec = pl.BlockSpec((None, A_WIDTH, ps), lambda i, j, k: (j, 0, 0))
    tile_ps = pl.BlockSpec((tm, ps), lambda i, j, k: (i, j))
    merged, pa, pb = _mm(
        "proj_merge", "nn", (nt, 4, 1),
        [(ya, y_spec, wpa, proj_spec), (yb, y_spec, wpb, proj_spec)],
        [(gates, pl.BlockSpec((None, tm, ps), lambda i, j, k: (0, i, j))),
         (gates, pl.BlockSpec((None, tm, ps), lambda i, j, k: (1, i, j)))],
        [(_sds((t, d), BF16), tile_ps)] * 3, (tm, ps), merge_epilogue)

    def residual_epilogue(accs, extras, outs, ij):
        outs[0][...] = extras[0][...] + accs[0]

    (h2,) = _mm(
        "out_proj", "nn", (nt, 1, 1),
        [(merged, row, wout, pl.BlockSpec((d, d), lambda i, j, k: (0, 0)))],
        [(h1, row)], [(_sds((t, d), F32), row)], (tm, d), residual_epilogue)

    h3, ffn2_saved = _ffn_fwd("ffn2", h2, small["ffn2_norm"], w["ffn2_w_gu"], w["ffn2_w_down"], {})
    n3 = _rms_fwd("ple_norm", h3, small["ple_norm"])
    tile_es = pl.BlockSpec((tm, es), lambda i, j, k: (i, j))
    (pe,) = _mm(
        "ple_embed", "nn", (nt, 4, 1),
        [(pt, pl.BlockSpec((tm, pdim), lambda i, j, k: (i, 0)), wpe, pl.BlockSpec((None, pdim, es), lambda i, j, k: (j, 0, 0)))],
        [], [(_sds((t, d), F32), tile_es)], (tm, es), _cast_epilogue)

    th = _pick(d, 512)

    def head_epilogue(accs, extras, outs, ij):
        h3_ref, pe_ref, tgt_ref = extras
        dy_ref, dpe_ref, dz_ref, loss_ref = outs
        pg = jax.nn.sigmoid(accs[0])
        pev = pe_ref[...]
        diff = h3_ref[...] + pg * pev - tgt_ref[...]
        dy = diff * (1.0 / d)
        dy_ref[...] = dy
        dpe_ref[...] = (dy * pg).astype(BF16)
        dz_ref[...] = (dy * pev * pg * (1.0 - pg)).astype(BF16)
        _accumulate(loss_ref, jnp.full(loss_ref.shape, jnp.sum(diff * diff), F32), (ij[0] == 0) & (ij[1] == 0))

    tile_h = pl.BlockSpec((tm, th), lambda i, j, k: (i, j))
    dy, dpe, dz, loss_acc = _mm(
        "ple_gate_loss", "nn", (nt, d // th, 1),
        [(n3, row, wpg, pl.BlockSpec((d, th), lambda i, j, k: (0, j)))],
        [(h3, tile_h), (pe, tile_h), (tgt, tile_h)],
        [(_sds((t, d), F32), tile_h), (_sds((t, d), BF16), tile_h), (_sds((t, d), BF16), tile_h),
         (_sds((8, 128), F32), pl.BlockSpec((8, 128), lambda i, j, k: (0, 0)))],
        (tm, th), head_epilogue)
    loss = 0.5 * loss_acc[0, 0] / d

    nk = t // tk
    (dwpe,) = _mm(
        "d_w_ple_proj", "tn", (1, 4, nk),
        [(pt, pl.BlockSpec((tk, pdim), lambda i, j, k: (k, 0)), dpe, pl.BlockSpec((tk, es), lambda i, j, k: (k, j)))],
        [], [(_sds((4, pdim, es), BF16), pl.BlockSpec((None, pdim, es), lambda i, j, k: (j, 0, 0)))],
        (pdim, es), _cast_epilogue)

    def dense_grad(name, a, dyb):
        (res,) = _mm(
            name, "tn", (1, d // th, nk),
            [(a, pl.BlockSpec((tk, d), lambda i, j, k: (k, 0)), dyb, pl.BlockSpec((tk, th), lambda i, j, k: (k, j)))],
            [], [(_sds((d, d), BF16), pl.BlockSpec((d, th), lambda i, j, k: (0, j)))], (d, th), _cast_epilogue)
        return res

    dwpg = dense_grad("d_w_ple_gate", n3, dz)
    extras, outs = _rms_bwd_io(h3, small["ple_norm"], dy, tm)
    dh3, dh3_b, d_ple_norm = _mm(
        "d_ple_norm", "nt", (nt, 1, d // th),
        [(dz, pl.BlockSpec((tm, th), lambda i, j, k: (i, k)), wpg, pl.BlockSpec((d, th), lambda i, j, k: (0, k)))],
        extras, outs, (tm, d), _rms_bwd_epilogue)

    late = GradPipe(GATHER_LATE)
    dh2, dh2_b, d_ffn2_norm, dwgu2, dwd2 = _ffn_bwd(
        "ffn2", dh3, dh3_b, h2, small["ffn2_norm"], w["ffn2_w_gu"], w["ffn2_w_down"], ffn2_saved,
        {"dnorm": lambda dwgu, dwd: [late.exchange([dwgu, dwd, dwpg, dwpe])]})

    def dmerge_epilogue(accs, extras, outs, ij):
        dmo = accs[0]
        g_ref, pa_ref, pb_ref = extras
        dg_ref, dpa_ref, dpb_ref = outs
        ga = g_ref[0].astype(F32)
        gb = g_ref[1].astype(F32)
        dg_ref[0] = (dmo * pa_ref[...].astype(F32) * ga * (1.0 - ga)).astype(BF16)
        dg_ref[1] = (dmo * pb_ref[...].astype(F32) * gb * (1.0 - gb)).astype(BF16)
        dpa_ref[...] = (dmo * ga).astype(BF16)
        dpb_ref[...] = (dmo * gb).astype(BF16)

    g_spec = pl.BlockSpec((2, tm, th), lambda i, j, k: (0, i, j))
    dgates, dpa, dpb = _mm(
        "d_merge", "nt", (nt, d // th, 1),
        [(dh2_b, row, wout, pl.BlockSpec((th, d), lambda i, j, k: (j, 0)))],
        [(gates, g_spec), (pa, tile_h), (pb, tile_h)],
        [(_sds((2, t, d), BF16), g_spec), (_sds((t, d), BF16), tile_h), (_sds((t, d), BF16), tile_h)],
        (tm, th), dmerge_epilogue)
    dwout = dense_grad("d_w_out", merged, dh2_b)

    yk_spec = pl.BlockSpec((tk, A_WIDTH), lambda i, j, k: (k, 0))
    dk_spec = pl.BlockSpec((tk, ps), lambda i, j, k: (k, j))
    dproj = (_sds((4, A_WIDTH, ps), BF16), proj_spec)
    dwpa, dwpb = _mm(
        "d_w_proj", "tn", (1, 4, nk),
        [(ya, yk_spec, dpa, dk_spec), (yb, yk_spec, dpb, dk_spec)], [], [dproj, dproj], (A_WIDTH, ps), _cast_epilogue)
    dproj_a = pl.BlockSpec((tm, ps), lambda i, j, k: (i, k))
    wproj_k = pl.BlockSpec((None, A_WIDTH, ps), lambda i, j, k: (k, 0, 0))
    dya, dyb = _mm(
        "d_attn_out", "nt", (nt, 1, 4),
        [(dpa, dproj_a, wpa, wproj_k), (dpb, dproj_a, wpb, wproj_k)], [],
        [(_sds((t, A_WIDTH), BF16), y_spec)] * 2, (tm, A_WIDTH), _cast_epilogue)

    dqa, dka, dva, dbias_a, _, dgqa, dgka = _attn_bwd(
        "attn_a_bwd", qkv, bias_a, sink_a, gqa, gka, ya, dya, lse_a, ATTN_A, n_batch, True, comms=[late.scatter()])
    dqb, dkb, dvb, _, dsink_b, dgqb, dgkb = _attn_bwd(
        "attn_b_bwd", qkv, bias_b, sink_b, gqb, gkb, yb, dyb, lse_b, ATTN_B, n_batch, False, comms=[late.forward()])
    dqkv = jnp.concatenate([dqa, dka, dva, dqb, dkb, dvb], axis=1)

    (dwgate,) = _mm(
        "d_w_gate", "tn", (1, 4, nk),
        [(un, pl.BlockSpec((tk, d), lambda i, j, k: (k, 0)),
          dgates, pl.BlockSpec((None, tk, gs), lambda i, j, k: (j // 2, k, j % 2)))],
        [], [(_sds((4, d, gs), BF16), pl.BlockSpec((None, d, gs), lambda i, j, k: (j, 0, 0)))], (d, gs), _cast_epilogue)
    (dwin,) = _mm(
        "d_w_in", "tn", (1, 2, nk),
        [(un, pl.BlockSpec((tk, d), lambda i, j, k: (k, 0)), dqkv, pl.BlockSpec((tk, tin), lambda i, j, k: (k, j)))],
        [], [(_sds((d, ncols), BF16), pl.BlockSpec((d, tin), lambda i, j, k: (0, j)))], (d, tin), _cast_epilogue)

    mixer = GradPipe(GATHER_MIXER)
    extras, outs = _rms_bwd_io(h1, small["mix_norm"], dh2, tm)
    dh1, dh1_b, d_mix_norm = _mm(
        "d_mix_norm", "nt", (nt, 1, 6),
        [(dgates, pl.BlockSpec((None, tm, gs), lambda i, j, k: (jnp.minimum(k, 3) // 2, i, jnp.minimum(k, 3) % 2)),
          wgate, pl.BlockSpec((None, d, gs), lambda i, j, k: (jnp.minimum(k, 3), 0, 0))),
         (dqkv, pl.BlockSpec((tm, tin), lambda i, j, k: (i, jnp.maximum(k - 4, 0))),
          w_in, pl.BlockSpec((d, tin), lambda i, j, k: (0, jnp.maximum(k - 4, 0))))],
        extras, outs, (tm, d), _rms_bwd_epilogue, steps=[4, 2],
        comms=[mixer.exchange([dwin, dwgate, dwpa, dwpb, dwout])])

    up1 = GradPipe(["ffn1_w_gu"])
    down1 = GradPipe(["ffn1_w_down"])
    dx, _, d_ffn1_norm, _, _ = _ffn_bwd(
        "ffn1", dh1, dh1_b, xt, small["ffn1_norm"], w["ffn1_w_gu"], w["ffn1_w_down"], ffn1_saved,
        {"dwgu": lambda: [mixer.scatter()],
         "dwd": lambda dwgu: [mixer.forward(), up1.exchange([dwgu])],
         "dnorm": lambda dwgu, dwd: [up1.scatter(), down1.exchange([dwd])]})
    _run_comms("grad_tail_scatter", [up1.forward(), down1.scatter()])
    _run_comms("grad_tail_forward", [down1.forward()])
    terms = {**late.terms(), **mixer.terms(), **up1.terms(), **down1.terms()}

    def fold(v):
        return v[0, :HEAD_DIM] + v[0, HEAD_DIM:]

    small_grads = {"ffn1_norm": d_ffn1_norm, "mix_norm": d_mix_norm, "ffn2_norm": d_ffn2_norm,
                   "ple_norm": d_ple_norm, "a_q_norm": fold(dgqa), "a_k_norm": fold(dgka),
                   "b_q_norm": fold(dgqb), "b_k_norm": fold(dgkb), "a_rel_bias": _rel_bias_grad(dbias_a),
                   "b_sinks": jnp.sum(dsink_b, axis=1)}
    return loss, dx, terms, small_grads


def kernel(x, p, ffn1_norm, ffn1_w_gu, ffn1_w_down, mix_norm, w_in, a_q_norm, a_k_norm, a_rel_bias, b_q_norm, b_k_norm, b_sinks, w_gate, w_proj_a, w_proj_b, w_out, ffn2_norm, ffn2_w_gu, ffn2_w_down, ple_norm, w_ple_gate, w_ple_proj, loss_target, m_ffn1_norm, m_ffn1_w_gu, m_ffn1_w_down, m_mix_norm, m_w_in, m_a_q_norm, m_a_k_norm, m_a_rel_bias, m_b_q_norm, m_b_k_norm, m_b_sinks, m_w_gate, m_w_proj_a, m_w_proj_b, m_w_out, m_ffn2_norm, m_ffn2_w_gu, m_ffn2_w_down, m_ple_norm, m_w_ple_gate, m_w_ple_proj, v_ffn1_norm, v_ffn1_w_gu, v_ffn1_w_down, v_mix_norm, v_w_in, v_a_q_norm, v_a_k_norm, v_a_rel_bias, v_b_q_norm, v_b_k_norm, v_b_sinks, v_w_gate, v_w_proj_a, v_w_proj_b, v_w_out, v_ffn2_norm, v_ffn2_w_gu, v_ffn2_w_down, v_ple_norm, v_w_ple_gate, v_w_ple_proj):
    given = dict(locals())
    n_batch, s, d = x.shape
    t = n_batch * s
    xt = x.reshape(t, d)
    pt = p.reshape(t, p.shape[-1])
    tgt = loss_target.reshape(t, d)

    shards = {}
    for name in BIG:
        (shards[name],) = _ew("cast_" + name, lambda v: (v,), [given[name][0]], [BF16])
    small = {name: given[name] for name in SMALL}
    core = lax.axis_index("c").astype(jnp.int32).reshape(1)
    loss, dx, terms, small_grads = _local_step(xt, pt, tgt, n_batch, shards, small, core)

    grads, deltas, new_m, new_v = {}, {}, {}, {}
    for name in BIG:
        gw, dl, nm, nv = _adamw_terms("adamw_" + name, terms[name], given[name][0], given["m_" + name][0],
                                      given["v_" + name][0])
        grads[name], deltas[name], new_m[name], new_v[name] = gw[None], dl[None], nm[None], nv[None]

    small_shapes = [given[name].shape for name in SMALL] + [()]
    g_pack = _all_reduce_small(_pack([small_grads[name] for name in SMALL] + [loss]))
    zero = jnp.zeros((), F32)
    w_pack = _pack([given[name] for name in SMALL] + [zero])
    m_pack = _pack([given["m_" + name] for name in SMALL] + [zero])
    v_pack = _pack([given["v_" + name] for name in SMALL] + [zero])
    d_pack, nm_pack, nv_pack = _ew("adamw_small", lambda wv, gv, mv, vv: _adamw_math(wv, gv, mv, vv),
                                   [w_pack, g_pack, m_pack, v_pack], [F32] * 3)
    g_small = _unpack(g_pack, small_shapes)
    loss_total = g_small[-1]
    for name, gv, dv, mv, vv in zip(SMALL, g_small, _unpack(d_pack, small_shapes), _unpack(nm_pack, small_shapes),
                                    _unpack(nv_pack, small_shapes)):
        grads[name], deltas[name], new_m[name], new_v[name] = gv, dv, mv, vv

    return (loss_total, dx.reshape(x.shape), *[grads[n] for n in WEIGHTS], *[deltas[n] for n in WEIGHTS],
            *[new_m[n] for n in WEIGHTS], *[new_v[n] for n in WEIGHTS])
```

```python
import functools

import numpy as np
import jax
import jax.numpy as jnp
from jax import lax
from jax.experimental import pallas as pl
from jax.experimental.pallas import tpu as pltpu

F32 = jnp.float32
BF16 = jnp.bfloat16

CHUNK = 64
HEAD_DIM = 64
A_PREV_CHUNKS = 8
A_MAX_REL = 128
N_HEADS = 8
B_KV_HEADS = 2
B_PREV_CHUNKS = 2
A_WIDTH = N_HEADS * HEAD_DIM
B_KV_WIDTH = B_KV_HEADS * HEAD_DIM
EPS = 1e-6
NEG_INF = -1e30
ATTN_SCALE = HEAD_DIM ** -0.5
Q_BLOCK = 128
PAIR = 2 * HEAD_DIM

ADAM_LR = 0.001
ADAM_B1 = 0.9
ADAM_B2 = 0.999
ADAM_EPS = 1e-08
ADAM_WD = 0.01
ADAM_STEP = 10

N_CHIPS = 4
N_DEV = 8
VMEM_LIMIT_V7X = 56 * 1024 * 1024
MESH = pl.DeviceIdType.MESH
ANY = pl.BlockSpec(memory_space=pl.ANY)

_DN = {
    "nn": (((1,), (0,)), ((), ())),
    "nt": (((1,), (1,)), ((), ())),
    "tn": (((0,), (0,)), ((), ())),
}


def _pick(n, target, mult=128):
    best = None
    for d in range(mult, min(n, target) + 1, mult):
        if n % d == 0:
            best = d
    return n if best is None else best


def _dot(a, b, mode):
    return lax.dot_general(a.astype(BF16), b.astype(BF16), _DN[mode], preferred_element_type=F32)


def _params():
    return pltpu.CompilerParams(vmem_limit_bytes=VMEM_LIMIT_V7X)


class _Comm:
    def __init__(self, ins, outs, aliases, sems, start, finish):
        self.ins, self.outs, self.aliases, self.sems = list(ins), list(outs), dict(aliases), list(sems)
        self.start, self.finish = start, finish
        self.results = None


class _CommPlumbing:
    def __init__(self, comms, n_in, n_out, n_scratch):
        self.comms = list(comms)
        self.n_in, self.n_out, self.n_scratch = n_in, n_out, n_scratch
        self.args = [a for cm in self.comms for a in cm.ins]
        self.out_shape = [o for cm in self.comms for o in cm.outs]
        self.scratch = [s for cm in self.comms for s in cm.sems]
        self.aliases = {}
        i0, o0 = n_in, n_out
        for cm in self.comms:
            for a, b in cm.aliases.items():
                self.aliases[i0 + a] = o0 + b
            i0 += len(cm.ins)
            o0 += len(cm.outs)

    def run(self, in_refs, out_refs, scratch_refs, first, last):
        if not self.comms:
            return
        parts = []
        i0, o0, s0 = self.n_in, self.n_out, self.n_scratch
        for cm in self.comms:
            parts.append((in_refs[i0:i0 + len(cm.ins)], out_refs[o0:o0 + len(cm.outs)],
                          scratch_refs[s0:s0 + len(cm.sems)]))
            i0 += len(cm.ins)
            o0 += len(cm.outs)
            s0 += len(cm.sems)

        @pl.when(first)
        def _():
            for cm, part in zip(self.comms, parts):
                cm.start(*part)

        @pl.when(last)
        def _():
            for cm, part in zip(self.comms, parts):
                cm.finish(*part)

    def deliver(self, results):
        o0 = self.n_out
        for cm in self.comms:
            cm.results = list(results[o0:o0 + len(cm.outs)])
            o0 += len(cm.outs)
        return list(results[:self.n_out])


def _swap_ij(spec):
    index_map = spec.index_map
    return pl.BlockSpec(spec.block_shape, lambda j, i, k: index_map(i, j, k))


def _mm(name, mode, grid, pairs, extras, outs, acc_shape, epilogue, steps=None, comms=(), j_outer=False):
    ni, nj, nk = grid
    n_in = 2 * len(pairs) + len(extras)
    n_out = len(outs)
    n_acc = len(pairs) if steps is None else 1
    plumb = _CommPlumbing(comms, n_in, n_out, n_acc)
    n_all_in = n_in + len(plumb.args)
    n_all_out = n_out + len(plumb.out_shape)
    if j_outer:
        grid = (nj, ni, nk)
        pairs = [(a, _swap_ij(a_spec), b, _swap_ij(b_spec)) for a, a_spec, b, b_spec in pairs]
        extras = [(e, _swap_ij(e_spec)) for e, e_spec in extras]
        outs = [(o, _swap_ij(o_spec)) for o, o_spec in outs]

    def body(*refs):
        in_refs = refs[:n_all_in]
        out_refs = refs[n_all_in:n_all_in + n_all_out]
        scratch = refs[n_all_in + n_all_out:]
        accs = scratch[:n_acc]
        i = pl.program_id(1 if j_outer else 0)
        j = pl.program_id(0 if j_outer else 1)
        k = pl.program_id(2)

        @pl.when(k == 0)
        def _():
            for acc in accs:
                acc[...] = jnp.zeros(acc.shape, F32)

        def contrib(p, acc):
            acc[...] += _dot(in_refs[2 * p][...], in_refs[2 * p + 1][...], mode)

        if steps is None:
            for p in range(len(pairs)):
                contrib(p, accs[p])
        else:
            lo = 0
            for p, n in enumerate(steps):
                pl.when((k >= lo) & (k < lo + n))(functools.partial(contrib, p, accs[0]))
                lo += n

        @pl.when(k == nk - 1)
        def _():
            epilogue([acc[...] for acc in accs], in_refs[2 * len(pairs):n_in], out_refs[:n_out], (i, j))

        plumb.run(in_refs, out_refs, scratch, (i == 0) & (j == 0) & (k == 0),
                  (i == ni - 1) & (j == nj - 1) & (k == nk - 1))

    args, in_specs = [], []
    for a, a_spec, b, b_spec in pairs:
        args += [a, b]
        in_specs += [a_spec, b_spec]
    for e, e_spec in extras:
        args.append(e)
        in_specs.append(e_spec)
    res = pl.pallas_call(
        body,
        name=name,
        grid=grid,
        in_specs=in_specs + [ANY] * len(plumb.args),
        out_specs=[s for _, s in outs] + [ANY] * len(plumb.out_shape),
        out_shape=[o for o, _ in outs] + plumb.out_shape,
        scratch_shapes=[pltpu.VMEM(acc_shape, F32) for _ in range(n_acc)] + plumb.scratch,
        input_output_aliases=plumb.aliases,
        compiler_params=_params(),
    )(*args, *plumb.args)
    return plumb.deliver(res)


def _sds(shape, dtype):
    return jax.ShapeDtypeStruct(shape, dtype)


def _accumulate(ref, value, first):
    @pl.when(first)
    def _():
        ref[...] = value

    @pl.when(jnp.logical_not(first))
    def _():
        ref[...] += value


def _rms_fwd(name, x, gain):
    t, d = x.shape
    tm = _pick(t, 512, 8)

    def body(x_ref, g_ref, y_ref):
        xv = x_ref[...]
        rstd = lax.rsqrt(jnp.mean(xv * xv, axis=-1, keepdims=True) + EPS)
        y_ref[...] = (xv * rstd * g_ref[...]).astype(BF16)

    return pl.pallas_call(
        body, name=name, grid=(t // tm,),
        in_specs=[pl.BlockSpec((tm, d), lambda i: (i, 0)), pl.BlockSpec((1, d), lambda i: (0, 0))],
        out_specs=pl.BlockSpec((tm, d), lambda i: (i, 0)),
        out_shape=_sds((t, d), BF16),
        compiler_params=_params(),
    )(x, gain)


def _rms_bwd_epilogue(accs, extras, outs, ij):
    x_ref, g_ref, r_ref = extras
    dh_ref, dhb_ref, dg_ref = outs
    dn = accs[0]
    xv = x_ref[...]
    rstd = lax.rsqrt(jnp.mean(xv * xv, axis=-1, keepdims=True) + EPS)
    xhat = xv * rstd
    gd = dn * g_ref[...]
    dx = rstd * (gd - xhat * jnp.mean(gd * xhat, axis=-1, keepdims=True))
    dh = r_ref[...] + dx
    dh_ref[...] = dh
    dhb_ref[...] = dh.astype(BF16)
    _accumulate(dg_ref, jnp.sum(dn * xhat, axis=0, keepdims=True), ij[0] == 0)


def _rms_bwd_io(x, gain, dres, tm):
    t, d = x.shape
    row = pl.BlockSpec((tm, d), lambda i, j, k: (i, 0))
    extras = [(x, row), (gain, pl.BlockSpec((1, d), lambda i, j, k: (0, 0))), (dres, row)]
    outs = [(_sds((t, d), F32), row), (_sds((t, d), BF16), row),
            (_sds((1, d), F32), pl.BlockSpec((1, d), lambda i, j, k: (0, 0)))]
    return extras, outs


def _ffn_fwd(tag, h, gain, wgu, wd, hooks):
    t, d = h.shape
    fs = wgu.shape[2]
    f = 2 * fs
    tm = _pick(t, 512, 8)
    n = _rms_fwd(tag + "_norm", h, gain)

    def up_epilogue(accs, extras, outs, ij):
        g, u = accs
        gu_ref, a_ref = outs
        gu_ref[0] = g.astype(BF16)
        gu_ref[1] = u.astype(BF16)
        a_ref[...] = (g * jax.nn.sigmoid(g) * u).astype(BF16)

    a_spec = pl.BlockSpec((tm, d), lambda i, j, k: (i, 0))
    gu, a = _mm(
        tag + "_up", "nn", (t // tm, 2, 1),
        [(n, a_spec, wgu, pl.BlockSpec((None, d, fs), lambda i, j, k: (j, 0, 0))),
         (n, a_spec, wgu, pl.BlockSpec((None, d, fs), lambda i, j, k: (j + 2, 0, 0)))],
        [],
        [(_sds((2, t, f), BF16), pl.BlockSpec((2, tm, fs), lambda i, j, k: (0, i, j))),
         (_sds((t, f), BF16), pl.BlockSpec((tm, fs), lambda i, j, k: (i, j)))],
        (tm, fs), up_epilogue, comms=hooks.get("up", lambda: ())(), j_outer=True)

    def down_epilogue(accs, extras, outs, ij):
        outs[0][...] = extras[0][...] + 0.5 * accs[0]

    row = pl.BlockSpec((tm, d), lambda i, j, k: (i, 0))
    (h_new,) = _mm(
        tag + "_down", "nn", (t // tm, 1, 1),
        [(a, pl.BlockSpec((tm, f), lambda i, j, k: (i, 0)), wd, pl.BlockSpec((f, d), lambda i, j, k: (0, 0)))],
        [(h, row)], [(_sds((t, d), F32), row)], (tm, d), down_epilogue, comms=hooks.get("down", lambda: ())())
    return h_new, (n, gu, a)


def _ffn_bwd(tag, dh, dh_b, h, gain, wgu, wd, saved, hooks):
    n, gu, a = saved
    t, d = h.shape
    fs = wgu.shape[2]
    f = 2 * fs
    tm = _pick(t, 512, 8)
    tk = _pick(t, 512, 8)

    def dact_epilogue(accs, extras, outs, ij):
        da = 0.5 * accs[0]
        g = extras[0][0].astype(F32)
        u = extras[0][1].astype(F32)
        sg = jax.nn.sigmoid(g)
        outs[0][0] = (da * u * sg * (1.0 + g * (1.0 - sg))).astype(BF16)
        outs[0][1] = (da * g * sg).astype(BF16)

    gu_spec = pl.BlockSpec((2, tm, fs), lambda i, j, k: (0, i, j))
    (dgu,) = _mm(
        tag + "_dact", "nt", (t // tm, 2, 1),
        [(dh_b, pl.BlockSpec((tm, d), lambda i, j, k: (i, 0)), wd, pl.BlockSpec((fs, d), lambda i, j, k: (j, 0)))],
        [(gu, gu_spec)], [(_sds((2, t, f), BF16), gu_spec)], (tm, fs), dact_epilogue, j_outer=True)

    def cast_epilogue(accs, extras, outs, ij):
        outs[0][...] = accs[0].astype(BF16)

    (dwgu,) = _mm(
        tag + "_dwgu", "tn", (1, 4, t // tk),
        [(n, pl.BlockSpec((tk, d), lambda i, j, k: (k, 0)),
          dgu, pl.BlockSpec((None, tk, fs), lambda i, j, k: (j // 2, k, j % 2)))],
        [], [(_sds((4, d, fs), BF16), pl.BlockSpec((None, d, fs), lambda i, j, k: (j, 0, 0)))], (d, fs), cast_epilogue,
        comms=hooks.get("dwgu", lambda: ())())

    def half_epilogue(accs, extras, outs, ij):
        outs[0][...] = (0.5 * accs[0]).astype(BF16)

    (dwd,) = _mm(
        tag + "_dwd", "tn", (2, 1, t // tk),
        [(a, pl.BlockSpec((tk, fs), lambda i, j, k: (k, i)), dh_b, pl.BlockSpec((tk, d), lambda i, j, k: (k, 0)))],
        [], [(_sds((f, d), BF16), pl.BlockSpec((fs, d), lambda i, j, k: (i, 0)))], (fs, d), half_epilogue,
        comms=hooks.get("dwd", lambda g: ())(dwgu))

    tmn = _pick(t, 1024, 8)
    extras, outs = _rms_bwd_io(h, gain, dh, tmn)
    dh_in, dh_in_b, dgain = _mm(
        tag + "_dnorm", "nt", (t // tmn, 1, 4),
        [(dgu, pl.BlockSpec((None, tmn, fs), lambda i, j, k: (k // 2, i, k % 2)),
          wgu, pl.BlockSpec((None, d, fs), lambda i, j, k: (k, 0, 0)))],
        extras, outs, (tmn, d), _rms_bwd_epilogue, comms=hooks.get("dnorm", lambda g, w: ())(dwgu, dwd))
    return dh_in, dh_in_b, dgain, dwgu, dwd


def _lane_lo(shape):
    return lax.broadcasted_iota(jnp.int32, shape, 1) < HEAD_DIM


def _pair_norm(xv, gain):
    lo = _lane_lo(xv.shape)
    x2 = xv * xv
    ms_lo = jnp.sum(jnp.where(lo, x2, 0.0), axis=-1, keepdims=True) * (1.0 / HEAD_DIM)
    ms_hi = jnp.sum(jnp.where(lo, 0.0, x2), axis=-1, keepdims=True) * (1.0 / HEAD_DIM)
    rstd = jnp.where(lo, lax.rsqrt(ms_lo + EPS), lax.rsqrt(ms_hi + EPS))
    xhat = xv * rstd
    return xhat * gain, xhat, rstd


def _pair_norm_bwd(dn, xhat, rstd, gain):
    lo = _lane_lo(dn.shape)
    gd = dn * gain
    t = gd * xhat
    m_lo = jnp.sum(jnp.where(lo, t, 0.0), axis=-1, keepdims=True) * (1.0 / HEAD_DIM)
    m_hi = jnp.sum(jnp.where(lo, 0.0, t), axis=-1, keepdims=True) * (1.0 / HEAD_DIM)
    dx = rstd * (gd - xhat * jnp.where(lo, m_lo, m_hi))
    return dx, jnp.sum(dn * xhat, axis=0, keepdims=True)


def _half(xv, hi):
    lo = _lane_lo(xv.shape)
    return jnp.where(lo, 0, xv) if hi else jnp.where(lo, xv, 0)


def _head_place(h, group):
    kh = h // group
    return h // 2, h % 2, kh // 2, kh % 2


def _attn_window(i, prev):
    q0 = i * Q_BLOCK
    start = jnp.maximum(q0 - prev, 0)
    off = start - (q0 - prev)
    return pl.multiple_of(start, Q_BLOCK), pl.multiple_of(off, Q_BLOCK)


def _attn_specs(cfg, s, nq):
    kw = cfg["kw"]
    q_spec = pl.BlockSpec((Q_BLOCK, A_WIDTH), lambda b, i: (b * nq + i, cfg["qblk"]))
    k_spec = pl.BlockSpec((s, kw), lambda b, i: (b, cfg["kblk"]))
    v_spec = pl.BlockSpec((s, kw), lambda b, i: (b, cfg["vblk"]))
    return q_spec, k_spec, v_spec


def _const_spec(shape):
    return pl.BlockSpec(shape, lambda b, i: (0,) * len(shape))


def _attn_fwd(name, qkv, bias_t, sink, gq, gk, cfg, n_batch, comms=()):
    t = qkv.shape[0]
    s = t // n_batch
    nq = s // Q_BLOCK
    prev, group, kw = cfg["prev"], cfg["group"], cfg["kw"]
    w = prev + Q_BLOCK
    wext = bias_t.shape[1]
    plumb = _CommPlumbing(comms, 7, 2, 1)
    n_all_in = 7 + len(plumb.args)
    n_all_out = 2 + len(plumb.out_shape)

    def body(*refs):
        q_ref, k_ref, v_ref, bias_ref, sink_ref, gq_ref, gk_ref = refs[:7]
        y_ref, lse_ref = refs[n_all_in:n_all_in + 2]
        kn_ref = refs[n_all_in + n_all_out]
        i = pl.program_id(1)
        plumb.run(refs[:n_all_in], refs[n_all_in:n_all_in + n_all_out], refs[n_all_in + n_all_out:],
                  (pl.program_id(0) == 0) & (i == 0), (pl.program_id(0) == n_batch - 1) & (i == nq - 1))

        @pl.when(i == 0)
        def _():
            for jk in range(kw // PAIR):
                kn, _, _ = _pair_norm(k_ref[:, pl.ds(jk * PAIR, PAIR)].astype(F32), gk_ref[...])
                kn_ref[:, pl.ds(jk * PAIR, PAIR)] = kn.astype(BF16)

        start, off = _attn_window(i, prev)
        sub = lax.broadcasted_iota(jnp.int32, (N_HEADS, Q_BLOCK), 0)
        lse = jnp.zeros((N_HEADS, Q_BLOCK), F32)
        for jq in range(N_HEADS // 2):
            qn, _, _ = _pair_norm(q_ref[:, pl.ds(jq * PAIR, PAIR)].astype(F32), gq_ref[...])
            qn = qn * ATTN_SCALE
            o_pair = jnp.zeros((Q_BLOCK, PAIR), F32)
            for hq in range(2):
                h = 2 * jq + hq
                _, _, jk, hk = _head_place(h, group)
                qm = _half(qn, hq)
                if hq != hk:
                    qm = pltpu.roll(qm, HEAD_DIM, 1)
                k_w = kn_ref[pl.ds(start, w), pl.ds(jk * PAIR, PAIR)]
                st = _dot(k_w, qm, "nt") + bias_ref[h, pl.ds(off, w), :]
                sk = sink_ref[h:h + 1, 0:1]
                m = jnp.maximum(jnp.max(st, axis=0, keepdims=True), sk)
                p = jnp.exp(st - m)
                l = jnp.sum(p, axis=0, keepdims=True) + jnp.exp(sk - m)
                v_w = _half(v_ref[pl.ds(start, w), pl.ds(jk * PAIR, PAIR)], hk)
                o = _dot(p * (1.0 / l), v_w, "tn")
                if hq != hk:
                    o = pltpu.roll(o, HEAD_DIM, 1)
                o_pair = o_pair + o
                lse = jnp.where(sub == h, m + jnp.log(l), lse)
            y_ref[:, pl.ds(jq * PAIR, PAIR)] = o_pair.astype(BF16)
        lse_ref[...] = lse

    q_spec, k_spec, v_spec = _attn_specs(cfg, s, nq)
    res = pl.pallas_call(
        body, name=name, grid=(n_batch, nq),
        in_specs=[q_spec, k_spec, v_spec, _const_spec((N_HEADS, wext, Q_BLOCK)), _const_spec((N_HEADS, 128)),
                  _const_spec((1, PAIR)), _const_spec((1, PAIR))] + [ANY] * len(plumb.args),
        out_specs=[pl.BlockSpec((Q_BLOCK, A_WIDTH), lambda b, i: (b * nq + i, 0)),
                   pl.BlockSpec((None, N_HEADS, Q_BLOCK), lambda b, i: (b * nq + i, 0, 0))]
        + [ANY] * len(plumb.out_shape),
        out_shape=[_sds((t, A_WIDTH), BF16), _sds((t // Q_BLOCK, N_HEADS, Q_BLOCK), F32)] + plumb.out_shape,
        scratch_shapes=[pltpu.VMEM((s, kw), BF16)] + plumb.scratch,
        input_output_aliases=plumb.aliases,
        compiler_params=_params(),
    )(qkv, qkv, qkv, bias_t, sink, gq, gk, *plumb.args)
    return plumb.deliver(res)


def _attn_bwd(name, qkv, bias_t, sink, gq, gk, y, dy, lse, cfg, n_batch, want_dbias, comms=()):
    t = qkv.shape[0]
    s = t // n_batch
    nq = s // Q_BLOCK
    prev, group, kw = cfg["prev"], cfg["group"], cfg["kw"]
    w = prev + Q_BLOCK
    wext = bias_t.shape[1]

    plumb = _CommPlumbing(comms, 10, 7, 3)
    n_all_in = 10 + len(plumb.args)
    n_all_out = 7 + len(plumb.out_shape)

    def body(*refs):
        q_ref, k_ref, v_ref, bias_ref, sink_ref, gq_ref, gk_ref, y_ref, dy_ref, lse_ref = refs[:10]
        dq_ref, dk_ref, dv_ref, db_ref, dsink_ref, dgq_ref, dgk_ref = refs[n_all_in:n_all_in + 7]
        kn_ref, dkn_ref, dvs_ref = refs[n_all_in + n_all_out:n_all_in + n_all_out + 3]
        b = pl.program_id(0)
        i = pl.program_id(1)
        first = (b == 0) & (i == 0)
        plumb.run(refs[:n_all_in], refs[n_all_in:n_all_in + n_all_out], refs[n_all_in + n_all_out:],
                  first, (b == n_batch - 1) & (i == nq - 1))

        @pl.when(i == 0)
        def _():
            for jk in range(kw // PAIR):
                kn, _, _ = _pair_norm(k_ref[:, pl.ds(jk * PAIR, PAIR)].astype(F32), gk_ref[...])
                kn_ref[:, pl.ds(jk * PAIR, PAIR)] = kn.astype(BF16)
            dkn_ref[...] = jnp.zeros(dkn_ref.shape, F32)
            dvs_ref[...] = jnp.zeros(dvs_ref.shape, F32)

        @pl.when(first)
        def _():
            db_ref[...] = jnp.zeros(db_ref.shape, F32)
            dsink_ref[...] = jnp.zeros(dsink_ref.shape, F32)
            dgq_ref[...] = jnp.zeros(dgq_ref.shape, F32)
            dgk_ref[...] = jnp.zeros(dgk_ref.shape, F32)

        start, off = _attn_window(i, prev)
        sub_lo = lax.broadcasted_iota(jnp.int32, (PAIR, Q_BLOCK), 0) < HEAD_DIM
        for jq in range(N_HEADS // 2):
            cols = pl.ds(jq * PAIR, PAIR)
            qn, q_hat, q_rstd = _pair_norm(q_ref[:, cols].astype(F32), gq_ref[...])
            qn = qn * ATTN_SCALE
            do_pair = dy_ref[:, cols]
            prod_t = (do_pair.astype(F32) * y_ref[:, cols].astype(F32)).T
            dqn = jnp.zeros((Q_BLOCK, PAIR), F32)
            for hq in range(2):
                h = 2 * jq + hq
                _, _, jk, hk = _head_place(h, group)
                kcols = pl.ds(jk * PAIR, PAIR)
                delta = jnp.sum(jnp.where(sub_lo, 0.0, prod_t) if hq else jnp.where(sub_lo, prod_t, 0.0),
                                axis=0, keepdims=True)
                qm = _half(qn, hq)
                do_m = _half(do_pair, hq)
                if hq != hk:
                    qm = pltpu.roll(qm, HEAD_DIM, 1)
                    do_m = pltpu.roll(do_m.astype(F32), HEAD_DIM, 1)
                qm_b = qm.astype(BF16)
                do_b = do_m.astype(BF16)
                k_w = kn_ref[pl.ds(start, w), kcols]
                v_w = v_ref[pl.ds(start, w), kcols]
                lse_row = lse_ref[h:h + 1, :]
                st = _dot(k_w, qm_b, "nt") + bias_ref[h, pl.ds(off, w), :]
                p = jnp.exp(st - lse_row)
                dp = _dot(v_w, do_b, "nt")
                ds = p * (dp - delta)
                dsink_ref[h:h + 1, :] += -jnp.exp(sink_ref[h:h + 1, 0:1] - lse_row) * delta
                if want_dbias:
                    db_ref[h, pl.ds(off, w), :] += ds
                ds_b = ds.astype(BF16)
                dq_h = _half(_dot(ds_b, k_w, "tn"), hk)
                if hq != hk:
                    dq_h = pltpu.roll(dq_h, HEAD_DIM, 1)
                dqn = dqn + dq_h
                dkn_ref[pl.ds(start, w), kcols] += _half(_dot(ds_b, qm_b, "nn"), hk)
                dvs_ref[pl.ds(start, w), kcols] += _half(_dot(p, do_b, "nn"), hk)
            dq_raw, dg = _pair_norm_bwd(dqn * ATTN_SCALE, q_hat, q_rstd, gq_ref[...])
            dq_ref[:, cols] = dq_raw.astype(BF16)
            dgq_ref[...] += dg

        @pl.when(i == nq - 1)
        def _():
            for jk in range(kw // PAIR):
                kcols = pl.ds(jk * PAIR, PAIR)
                _, k_hat, k_rstd = _pair_norm(k_ref[:, kcols].astype(F32), gk_ref[...])
                dk_raw, dg = _pair_norm_bwd(dkn_ref[:, kcols], k_hat, k_rstd, gk_ref[...])
                dk_ref[:, kcols] = dk_raw.astype(BF16)
                dgk_ref[...] += dg
            dv_ref[...] = dvs_ref[...].astype(BF16)

    q_spec, k_spec, v_spec = _attn_specs(cfg, s, nq)
    row = pl.BlockSpec((Q_BLOCK, A_WIDTH), lambda b, i: (b * nq + i, 0))
    kv_out = pl.BlockSpec((s, kw), lambda b, i: (b, 0))
    res = pl.pallas_call(
        body, name=name, grid=(n_batch, nq),
        in_specs=[q_spec, k_spec, v_spec, _const_spec((N_HEADS, wext, Q_BLOCK)), _const_spec((N_HEADS, 128)),
                  _const_spec((1, PAIR)), _const_spec((1, PAIR)), row, row,
                  pl.BlockSpec((None, N_HEADS, Q_BLOCK), lambda b, i: (b * nq + i, 0, 0))] + [ANY] * len(plumb.args),
        out_specs=[row, kv_out, kv_out, _const_spec((N_HEADS, wext, Q_BLOCK)), _const_spec((N_HEADS, 128)),
                   _const_spec((1, PAIR)), _const_spec((1, PAIR))] + [ANY] * len(plumb.out_shape),
        out_shape=[_sds((t, A_WIDTH), BF16), _sds((t, kw), BF16), _sds((t, kw), BF16),
                   _sds((N_HEADS, wext, Q_BLOCK), F32), _sds((N_HEADS, 128), F32),
                   _sds((1, PAIR), F32), _sds((1, PAIR), F32)] + plumb.out_shape,
        scratch_shapes=[pltpu.VMEM((s, kw), BF16), pltpu.VMEM((s, kw), F32), pltpu.VMEM((s, kw), F32)]
        + plumb.scratch,
        input_output_aliases=plumb.aliases,
        compiler_params=_params(),
    )(qkv, qkv, qkv, bias_t, sink, gq, gk, y, dy, lse, *plumb.args)
    return plumb.deliver(res)


def _band_tables(prev_chunks):
    prev = prev_chunks * CHUNK
    wext = 2 * prev + Q_BLOCK
    jj = np.arange(wext)[:, None]
    ii = np.arange(Q_BLOCK)[None, :]
    dist = prev + ii - jj
    rel_chunk = (prev // CHUNK + ii // CHUNK) - jj // CHUNK
    allowed = (rel_chunk >= 0) & (rel_chunk <= prev_chunks)
    return dist, allowed


def _alibi_slopes():
    return np.array([2.0 ** (-8.0 * (h + 1) / N_HEADS) for h in range(N_HEADS)], dtype=np.float32)


def _diag_onehot(prev, wext):
    n_diag = wext + Q_BLOCK - 1
    idx = np.clip(prev + Q_BLOCK - 1 - np.arange(n_diag), -A_MAX_REL, A_MAX_REL) + A_MAX_REL
    onehot = np.zeros((n_diag, 2 * A_MAX_REL + 1), np.float32)
    onehot[np.arange(n_diag), idx] = 1.0
    return onehot


def _bias_a(rel_bias):
    prev = A_PREV_CHUNKS * CHUNK
    _, allowed = _band_tables(A_PREV_CHUNKS)
    wext = allowed.shape[0]
    n_diag = wext + Q_BLOCK - 1
    seq = jnp.dot(rel_bias, jnp.asarray(_diag_onehot(prev, wext).T), precision=lax.Precision.HIGHEST)
    seq = jnp.pad(seq, ((0, 0), (0, 1)))
    rows = jnp.broadcast_to(seq[:, None, :], (N_HEADS, Q_BLOCK, n_diag + 1)).reshape(N_HEADS, -1)
    skew = rows[:, :Q_BLOCK * n_diag].reshape(N_HEADS, Q_BLOCK, n_diag)
    tile = jnp.transpose(skew[:, :, Q_BLOCK - 1:Q_BLOCK - 1 + wext], (0, 2, 1))
    return jnp.where(jnp.asarray(allowed)[None], tile, NEG_INF)


def _bias_b():
    dist, allowed = _band_tables(B_PREV_CHUNKS)
    bias = -_alibi_slopes()[:, None, None] * np.abs(dist).astype(np.float32)[None]
    return jnp.asarray(np.where(allowed[None], bias, np.float32(NEG_INF)).astype(np.float32))


def _rel_bias_grad(db_t):
    prev = A_PREV_CHUNKS * CHUNK
    wext = db_t.shape[1]
    n_diag = wext + Q_BLOCK - 1
    wp = n_diag + Q_BLOCK - 1
    xp = jnp.pad(jnp.transpose(db_t, (0, 2, 1)), ((0, 0), (0, 0), (Q_BLOCK - 1, Q_BLOCK - 1)))
    flat = jnp.pad(xp.reshape(N_HEADS, Q_BLOCK * wp), ((0, 0), (0, Q_BLOCK)))
    skew = flat.reshape(N_HEADS, Q_BLOCK, wp + 1)[:, :, :n_diag]
    diag = jnp.sum(skew, axis=1)
    return jnp.dot(diag, jnp.asarray(_diag_onehot(prev, wext)), precision=lax.Precision.HIGHEST)


def _ew(name, fn, ins, out_dtypes):
    r, c = ins[0].shape
    rb = _pick(r, max(16, (1 << 19) // c), 16)
    spec = pl.BlockSpec((rb, c), lambda i: (i, 0))

    def body(*refs):
        vals = fn(*[ref[...] for ref in refs[:len(ins)]])
        for ref, val in zip(refs[len(ins):], vals):
            ref[...] = val.astype(ref.dtype)

    return pl.pallas_call(
        body, name=name, grid=(r // rb,), in_specs=[spec] * len(ins), out_specs=[spec] * len(out_dtypes),
        out_shape=[_sds((r, c), dt) for dt in out_dtypes], compiler_params=_params(),
    )(*ins)


def _adamw_math(w, g, m, v):
    m = ADAM_B1 * m + (1.0 - ADAM_B1) * g
    v = ADAM_B2 * v + (1.0 - ADAM_B2) * (g * g)
    m_hat = m / (1.0 - ADAM_B1 ** ADAM_STEP)
    v_hat = v / (1.0 - ADAM_B2 ** ADAM_STEP)
    delta = -ADAM_LR * (m_hat / (jnp.sqrt(v_hat) + ADAM_EPS) + ADAM_WD * w)
    return delta, m, v


def _adamw_terms(name, terms, w, m, v):
    r, c = w.shape
    hr = r // 2
    rb = _pick(hr, max(16, (1 << 17) // c), 16)
    nb = hr // rb

    def body(t_ref, w_ref, m_ref, v_ref, g_ref, d_ref, nm_ref, nv_ref):
        g = t_ref[0].astype(F32)
        for k in range(1, N_CHIPS):
            g = g + t_ref[k].astype(F32)
        delta, nm, nv = _adamw_math(w_ref[...], g, m_ref[...], v_ref[...])
        g_ref[...] = g
        d_ref[...] = delta
        nm_ref[...] = nm
        nv_ref[...] = nv

    spec = pl.BlockSpec((rb, c), lambda h, i: (h * nb + i, 0))
    return pl.pallas_call(
        body, name=name, grid=(2, nb),
        in_specs=[pl.BlockSpec((None, N_CHIPS, rb, c), lambda h, i: (h, 0, i, 0)), spec, spec, spec],
        out_specs=[spec] * 4, out_shape=[_sds((r, c), F32)] * 4, compiler_params=_params(),
    )(terms, w, m, v)


def _mesh_place():
    x, y, c = lax.axis_index("x"), lax.axis_index("y"), lax.axis_index("c")
    chips = [(x, 1 - y), (1 - x, y), (1 - x, 1 - y)]
    return x, y, c, chips


def _all_gather_weights(shards):
    n = len(shards)

    def body(*refs):
        ins, outs = refs[:n], refs[n:2 * n]
        local_sem, ici_send, ici_recv, d2d_send, d2d_recv = refs[2 * n:]
        x, y, c, chips = _mesh_place()
        me = 2 * x + y
        sibling = (x, y, 1 - c)
        local, sent = [], []
        for wi in range(n):
            loc = pltpu.make_async_copy(ins[wi], outs[wi].at[me], local_sem.at[wi])
            loc.start()
            local.append(loc)
            for k, (tx, ty) in enumerate(chips):
                cp = pltpu.make_async_remote_copy(
                    src_ref=ins[wi].at[c], dst_ref=outs[wi].at[me, c],
                    send_sem=ici_send.at[wi * 3 + k], recv_sem=ici_recv.at[wi * 3 + k],
                    device_id=(tx, ty, c), device_id_type=MESH)
                cp.start()
                sent.append(cp)
        passed = []
        for wi in range(n):
            for k, (tx, ty) in enumerate(chips):
                slab = outs[wi].at[2 * tx + ty, c]
                pltpu.make_async_remote_copy(
                    src_ref=slab, dst_ref=slab, send_sem=ici_send.at[wi * 3 + k], recv_sem=ici_recv.at[wi * 3 + k],
                    device_id=(tx, ty, c), device_id_type=MESH).wait_recv()
                fw = pltpu.make_async_remote_copy(
                    src_ref=slab, dst_ref=slab, send_sem=d2d_send.at[wi * 3 + k], recv_sem=d2d_recv.at[wi * 3 + k],
                    device_id=sibling, device_id_type=MESH)
                fw.start()
                passed.append(fw)
        for wi in range(n):
            for k, (tx, ty) in enumerate(chips):
                slab = outs[wi].at[2 * tx + ty, 1 - c]
                pltpu.make_async_remote_copy(
                    src_ref=slab, dst_ref=slab, send_sem=d2d_send.at[wi * 3 + k], recv_sem=d2d_recv.at[wi * 3 + k],
                    device_id=sibling, device_id_type=MESH).wait_recv()
        for loc in local:
            loc.wait()
        for cp in sent + passed:
            cp.wait_send()

    return pl.pallas_call(
        body, name="all_gather_weights",
        in_specs=[ANY] * n, out_specs=[ANY] * n,
        out_shape=[_sds((N_CHIPS,) + s.shape, s.dtype) for s in shards],
        scratch_shapes=[pltpu.SemaphoreType.DMA((n,)), pltpu.SemaphoreType.DMA((3 * n,)),
                        pltpu.SemaphoreType.DMA((3 * n,)), pltpu.SemaphoreType.DMA((3 * n,)),
                        pltpu.SemaphoreType.DMA((3 * n,))],
    )(*shards)


def _run_comms(name, comms):
    plumb = _CommPlumbing(comms, 0, 0, 0)
    n_in, n_out = len(plumb.args), len(plumb.out_shape)

    def body(*refs):
        parts = []
        i0, o0, s0 = 0, n_in, n_in + n_out
        for cm in plumb.comms:
            parts.append((refs[i0:i0 + len(cm.ins)], refs[o0:o0 + len(cm.outs)], refs[s0:s0 + len(cm.sems)]))
            i0 += len(cm.ins)
            o0 += len(cm.outs)
            s0 += len(cm.sems)
        for cm, part in zip(plumb.comms, parts):
            cm.start(*part)
        for cm, part in zip(plumb.comms, parts):
            cm.finish(*part)

    res = pl.pallas_call(
        body, name=name, in_specs=[ANY] * n_in, out_specs=[ANY] * n_out, out_shape=plumb.out_shape,
        scratch_shapes=plumb.scratch, input_output_aliases=plumb.aliases,
    )(*plumb.args)
    plumb.deliver(res)


def _gather_ici(shards):
    n = len(shards)

    def copies(ins, outs, sems):
        local_sem, send_sem, recv_sem = sems
        x, y, c, chips = _mesh_place()
        me = 2 * x + y
        local, sends, recvs = [], [], []
        for wi in range(n):
            local.append(pltpu.make_async_copy(ins[wi], outs[wi].at[me], local_sem.at[wi]))
            for k, (tx, ty) in enumerate(chips):
                sems_k = dict(send_sem=send_sem.at[wi * 3 + k], recv_sem=recv_sem.at[wi * 3 + k],
                              device_id=(tx, ty, c), device_id_type=MESH)
                sends.append(pltpu.make_async_remote_copy(
                    src_ref=ins[wi].at[c], dst_ref=outs[wi].at[me, c], **sems_k))
                slab = outs[wi].at[2 * tx + ty, c]
                recvs.append(pltpu.make_async_remote_copy(src_ref=slab, dst_ref=slab, **sems_k))
        return local, sends, recvs

    def start(ins, outs, sems):
        local, sends, _ = copies(ins, outs, sems)
        for cp in local + sends:
            cp.start()

    def finish(ins, outs, sems):
        local, sends, recvs = copies(ins, outs, sems)
        for cp in local:
            cp.wait()
        for cp in recvs:
            cp.wait_recv()
        for cp in sends:
            cp.wait_send()

    return _Comm(shards, [_sds((N_CHIPS,) + s.shape, s.dtype) for s in shards], {},
                 [pltpu.SemaphoreType.DMA((n,)), pltpu.SemaphoreType.DMA((3 * n,)), pltpu.SemaphoreType.DMA((3 * n,))],
                 start, finish)


def _gather_d2d(gathered):
    n = len(gathered)

    def copies(outs, sems):
        send_sem, recv_sem = sems
        x, y, c, chips = _mesh_place()
        sends, recvs = [], []
        for wi in range(n):
            for k, (tx, ty) in enumerate(chips):
                sems_k = dict(send_sem=send_sem.at[wi * 3 + k], recv_sem=recv_sem.at[wi * 3 + k],
                              device_id=(x, y, 1 - c), device_id_type=MESH)
                mine = outs[wi].at[2 * tx + ty, c]
                theirs = outs[wi].at[2 * tx + ty, 1 - c]
                sends.append(pltpu.make_async_remote_copy(src_ref=mine, dst_ref=mine, **sems_k))
                recvs.append(pltpu.make_async_remote_copy(src_ref=theirs, dst_ref=theirs, **sems_k))
        return sends, recvs

    def start(ins, outs, sems):
        for cp in copies(outs, sems)[0]:
            cp.start()

    def finish(ins, outs, sems):
        sends, recvs = copies(outs, sems)
        for cp in recvs:
            cp.wait_recv()
        for cp in sends:
            cp.wait_send()

    return _Comm(gathered, [_sds(g.shape, g.dtype) for g in gathered], {i: i for i in range(n)},
                 [pltpu.SemaphoreType.DMA((3 * n,)), pltpu.SemaphoreType.DMA((3 * n,))], start, finish)


def _exchange_halves(grads):
    n = len(grads)

    def copies(ins, outs, sems):
        send_sem, recv_sem = sems
        x, y, c, _ = _mesh_place()
        return [pltpu.make_async_remote_copy(
            src_ref=ins[wi].at[t, 1 - c], dst_ref=outs[wi].at[t],
            send_sem=send_sem.at[wi * N_CHIPS + t], recv_sem=recv_sem.at[wi * N_CHIPS + t],
            device_id=(x, y, 1 - c), device_id_type=MESH) for wi in range(n) for t in range(N_CHIPS)]

    def start(ins, outs, sems):
        for cp in copies(ins, outs, sems):
            cp.start()

    def finish(ins, outs, sems):
        for cp in copies(ins, outs, sems):
            cp.wait()

    return _Comm(grads, [_sds((N_CHIPS,) + g.shape[2:], g.dtype) for g in grads], {},
                 [pltpu.SemaphoreType.DMA((N_CHIPS * n,)), pltpu.SemaphoreType.DMA((N_CHIPS * n,))], start, finish)


def _scatter_ici(sums):
    n = len(sums)

    def copies(ins, outs, sems):
        local_sem, send_sem, recv_sem = sems
        x, y, c, chips = _mesh_place()
        me = 2 * x + y
        local, sends, recvs = [], [], []
        for wi in range(n):
            local.append(pltpu.make_async_copy(ins[wi].at[me], outs[wi].at[c, 0], local_sem.at[wi]))
            for k, (tx, ty) in enumerate(chips):
                sems_k = dict(send_sem=send_sem.at[wi * 3 + k], recv_sem=recv_sem.at[wi * 3 + k],
                              device_id=(tx, ty, c), device_id_type=MESH)
                land = outs[wi].at[c, k + 1]
                sends.append(pltpu.make_async_remote_copy(src_ref=ins[wi].at[2 * tx + ty], dst_ref=land, **sems_k))
                recvs.append(pltpu.make_async_remote_copy(src_ref=land, dst_ref=land, **sems_k))
        return local, sends, recvs

    def start(ins, outs, sems):
        local, sends, _ = copies(ins, outs, sems)
        for cp in local + sends:
            cp.start()

    def finish(ins, outs, sems):
        local, sends, recvs = copies(ins, outs, sems)
        for cp in local:
            cp.wait()
        for cp in recvs:
            cp.wait_recv()
        for cp in sends:
            cp.wait_send()

    return _Comm(sums, [_sds((2, N_CHIPS) + s.shape[1:], s.dtype) for s in sums], {},
                 [pltpu.SemaphoreType.DMA((n,)), pltpu.SemaphoreType.DMA((3 * n,)), pltpu.SemaphoreType.DMA((3 * n,))],
                 start, finish)


def _scatter_d2d(terms):
    n = len(terms)

    def copies(outs, sems):
        send_sem, recv_sem = sems
        x, y, c, _ = _mesh_place()
        sends, recvs = [], []
        for wi in range(n):
            sems_w = dict(send_sem=send_sem.at[wi], recv_sem=recv_sem.at[wi],
                          device_id=(x, y, 1 - c), device_id_type=MESH)
            sends.append(pltpu.make_async_remote_copy(src_ref=outs[wi].at[c], dst_ref=outs[wi].at[c], **sems_w))
            recvs.append(pltpu.make_async_remote_copy(src_ref=outs[wi].at[1 - c], dst_ref=outs[wi].at[1 - c], **sems_w))
        return sends, recvs

    def start(ins, outs, sems):
        for cp in copies(outs, sems)[0]:
            cp.start()

    def finish(ins, outs, sems):
        sends, recvs = copies(outs, sems)
        for cp in recvs:
            cp.wait_recv()
        for cp in sends:
            cp.wait_send()

    return _Comm(terms, [_sds(t.shape, t.dtype) for t in terms], {i: i for i in range(n)},
                 [pltpu.SemaphoreType.DMA((n,)), pltpu.SemaphoreType.DMA((n,))], start, finish)


def _chip_sum(name, grad, got, core):
    _, _, hr, c = grad.shape
    rb = _pick(hr, max(16, (1 << 19) // c), 16)

    def body(core_ref, a_ref, b_ref, o_ref):
        o_ref[...] = (a_ref[...].astype(F32) + b_ref[...].astype(F32)).astype(BF16)

    out_spec = pl.BlockSpec((None, rb, c), lambda t, i, core_ref: (t, i, 0))
    return pl.pallas_call(
        body, name=name,
        grid_spec=pltpu.PrefetchScalarGridSpec(
            num_scalar_prefetch=1, grid=(N_CHIPS, hr // rb),
            in_specs=[pl.BlockSpec((None, None, rb, c), lambda t, i, core_ref: (t, core_ref[0], i, 0)), out_spec],
            out_specs=out_spec),
        out_shape=_sds((N_CHIPS, hr, c), BF16), compiler_params=_params(),
    )(core, grad, got)


def _all_reduce_small(pack):
    r = pack.shape[0]

    def body(p_ref, o_ref, land_ref, send_sem, recv_sem):
        x, y, c, _ = _mesh_place()
        me = 4 * x + 2 * y + c
        flips = [(k >> 2 & 1, k >> 1 & 1, k & 1) for k in range(1, N_DEV)]

        def peer(fx, fy, fc):
            return (1 - x if fx else x, 1 - y if fy else y, 1 - c if fc else c)

        land_ref[me] = p_ref[...]
        sent = []
        for k, flip in enumerate(flips):
            cp = pltpu.make_async_remote_copy(
                src_ref=p_ref, dst_ref=land_ref.at[me], send_sem=send_sem.at[k], recv_sem=recv_sem.at[k],
                device_id=peer(*flip), device_id_type=MESH)
            cp.start()
            sent.append(cp)
        for k, flip in enumerate(flips):
            px, py, pc = peer(*flip)
            slot = land_ref.at[4 * px + 2 * py + pc]
            pltpu.make_async_remote_copy(
                src_ref=slot, dst_ref=slot, send_sem=send_sem.at[k], recv_sem=recv_sem.at[k],
                device_id=(px, py, pc), device_id_type=MESH).wait_recv()
        total = land_ref[0]
        for d in range(1, N_DEV):
            total = total + land_ref[d]
        o_ref[...] = total
        for cp in sent:
            cp.wait_send()

    vmem = pl.BlockSpec(memory_space=pltpu.VMEM)
    return pl.pallas_call(
        body, name="all_reduce_small", in_specs=[vmem], out_specs=vmem, out_shape=_sds((r, 128), F32),
        scratch_shapes=[pltpu.VMEM((N_DEV, r, 128), F32), pltpu.SemaphoreType.DMA((N_DEV - 1,)),
                        pltpu.SemaphoreType.DMA((N_DEV - 1,))],
    )(pack)


PACK_TILE = 8 * 128


def _pack(items):
    rows, i = [], 0
    while i < len(items):
        j = i
        while j < len(items) and items[j].size == items[i].size:
            j += 1
        group = jnp.stack([it.reshape(-1).astype(F32) for it in items[i:j]])
        rows.append(jnp.pad(group, ((0, 0), (0, -group.shape[1] % PACK_TILE))).reshape(-1, 128))
        i = j
    return jnp.concatenate(rows, axis=0)


def _unpack(pack, shapes):
    out, row = [], 0
    for shp in shapes:
        size = int(np.prod(shp))
        nrow = -(-size // PACK_TILE) * (PACK_TILE // 128)
        out.append(pack[row:row + nrow].reshape(-1)[:size].reshape(shp))
        row += nrow
    return out


BIG = ["ffn1_w_gu", "ffn1_w_down", "w_in", "w_gate", "w_proj_a", "w_proj_b", "w_out",
       "ffn2_w_gu", "ffn2_w_down", "w_ple_gate", "w_ple_proj"]
SMALL = ["ffn1_norm", "mix_norm", "ffn2_norm", "ple_norm", "a_q_norm", "a_k_norm", "b_q_norm", "b_k_norm",
         "a_rel_bias", "b_sinks"]
WEIGHTS = ["ffn1_norm", "ffn1_w_gu", "ffn1_w_down", "mix_norm", "w_in", "a_q_norm", "a_k_norm", "a_rel_bias",
           "b_q_norm", "b_k_norm", "b_sinks", "w_gate", "w_proj_a", "w_proj_b", "w_out", "ffn2_norm",
           "ffn2_w_gu", "ffn2_w_down", "ple_norm", "w_ple_gate", "w_ple_proj"]
ATTN_A = dict(prev=A_PREV_CHUNKS * CHUNK, group=1, kw=A_WIDTH, qblk=0, kblk=1, vblk=2)
ATTN_B = dict(prev=B_PREV_CHUNKS * CHUNK, group=N_HEADS // B_KV_HEADS, kw=B_KV_WIDTH, qblk=3,
              kblk=4 * A_WIDTH // B_KV_WIDTH, vblk=4 * A_WIDTH // B_KV_WIDTH + 1)


def _cast_epilogue(accs, extras, outs, ij):
    for acc, out in zip(accs, outs):
        out[...] = acc.astype(out.dtype)


GATHER_FIRST = ["ffn1_w_gu", "ffn1_w_down"]
GATHER_MIXER = ["w_in", "w_gate", "w_proj_a", "w_proj_b", "w_out"]
GATHER_LATE = ["ffn2_w_gu", "ffn2_w_down", "w_ple_gate", "w_ple_proj"]
ROW_SHARDED = ("ffn1_w_down", "ffn2_w_down", "w_out", "w_ple_gate")


def _slotted(name, grad):
    if name == "w_in":
        rows, cols = grad.shape
        grad = jnp.transpose(grad.reshape(rows, N_CHIPS, cols // N_CHIPS), (1, 0, 2))
    elif name in ROW_SHARDED:
        grad = grad.reshape(N_CHIPS, grad.shape[0] // N_CHIPS, grad.shape[1])
    return grad.reshape(N_CHIPS, 2, grad.shape[1] // 2, grad.shape[2])


def _local_step(xt, pt, tgt, n_batch, shards, small, core):
    t, d = xt.shape
    tm = _pick(t, 512, 8)
    tk = _pick(t, 512, 8)
    nt = t // tm
    row = pl.BlockSpec((tm, d), lambda i, j, k: (i, 0))
    gs = shards["w_gate"].shape[1]
    ps = shards["w_proj_a"].shape[1]
    es = shards["w_ple_proj"].shape[1]
    pdim = pt.shape[1]
    ncols = N_CHIPS * shards["w_in"].shape[1]
    tin = ncols // 2
    assert 2 * gs == d and 4 * ps == d and 4 * es == d and tin % 128 == 0

    w = {}
    halves = {n: s.reshape(2, s.shape[0] // 2, s.shape[1]) for n, s in shards.items()}

    def publish(names, arrays):
        for name, g in zip(names, arrays):
            g = g.reshape(N_CHIPS, 2 * g.shape[2], g.shape[3])
            if name in ROW_SHARDED:
                g = g.reshape(N_CHIPS * g.shape[1], g.shape[2])
            elif name == "w_in":
                g = jnp.transpose(g, (1, 0, 2)).reshape(g.shape[1], N_CHIPS * g.shape[2])
            w[name] = g

    stage = {}

    def gather_ici(names):
        def make():
            stage["ici"] = _gather_ici([halves[n] for n in names])
            return [stage["ici"]]
        return make

    def gather_d2d():
        stage["d2d"] = _gather_d2d(stage["ici"].results)
        return [stage["d2d"]]

    class GradPipe:
        def __init__(self, names):
            self.names = names

        def exchange(self, grads):
            self.grads = [_slotted(n, g) for n, g in zip(self.names, grads)]
            self.x = _exchange_halves(self.grads)
            return self.x

        def scatter(self):
            sums = [_chip_sum("chip_sum_" + n, g, got, core)
                    for n, g, got in zip(self.names, self.grads, self.x.results)]
            self.s = _scatter_ici(sums)
            return self.s

        def forward(self):
            self.f = _scatter_d2d(self.s.results)
            return self.f

        def terms(self):
            return dict(zip(self.names, self.f.results))

    publish(GATHER_FIRST, _all_gather_weights([halves[n] for n in GATHER_FIRST]))
    h1, ffn1_saved = _ffn_fwd("ffn1", xt, small["ffn1_norm"], w["ffn1_w_gu"], w["ffn1_w_down"],
                              {"up": gather_ici(GATHER_MIXER), "down": gather_d2d})
    publish(GATHER_MIXER, stage["d2d"].results)
    w_in, wgate, wpa, wpb, wout = [w[n] for n in GATHER_MIXER]
    un = _rms_fwd("mix_norm", h1, small["mix_norm"])
    (qkv,) = _mm(
        "qkv", "nn", (nt, 2, 1),
        [(un, row, w_in, pl.BlockSpec((d, tin), lambda i, j, k: (0, j)))], [],
        [(_sds((t, ncols), BF16), pl.BlockSpec((tm, tin), lambda i, j, k: (i, j)))], (tm, tin), _cast_epilogue,
        j_outer=True)

    def gate_epilogue(accs, extras, outs, ij):
        outs[0][...] = jax.nn.sigmoid(accs[0]).astype(BF16)

    (gates,) = _mm(
        "gate", "nn", (nt, 4, 1),
        [(un, row, wgate, pl.BlockSpec((None, d, gs), lambda i, j, k: (j, 0, 0)))], [],
        [(_sds((2, t, d), BF16), pl.BlockSpec((None, tm, gs), lambda i, j, k: (j // 2, i, j % 2)))],
        (tm, gs), gate_epilogue, j_outer=True)

    bias_a = _bias_a(small["a_rel_bias"][0])
    bias_b = _bias_b()
    sink_a = jnp.full((N_HEADS, 128), NEG_INF, F32)
    sink_b = jnp.broadcast_to(small["b_sinks"][0][:, None], (N_HEADS, 128))
    gqa, gka, gqb, gkb = [jnp.tile(small[k], (1, 2)) for k in ("a_q_norm", "a_k_norm", "b_q_norm", "b_k_norm")]
    ya, lse_a = _attn_fwd("attn_a_fwd", qkv, bias_a, sink_a, gqa, gka, ATTN_A, n_batch,
                          comms=gather_ici(GATHER_LATE)())
    yb, lse_b = _attn_fwd("attn_b_fwd", qkv, bias_b, sink_b, gqb, gkb, ATTN_B, n_batch, comms=gather_d2d())
    publish(GATHER_LATE, stage["d2d"].results)
    wpg, wpe = w["w_ple_gate"], w["w_ple_proj"]

    def merge_epilogue(accs, extras, outs, ij):
        pa, pb = accs
        outs[0][...] = (extras[0][...].astype(F32) * pa + extras[1][...].astype(F32) * pb).astype(BF16)
        outs[1][...] = pa.astype(BF16)
        outs[2][...] = pb.astype(BF16)

    y_spec = pl.BlockSpec((tm, A_WIDTH), lambda i, j, k: (i, 0))
    proj_spec = pl.BlockSpec((None, A_WIDTH, ps), lambda i, j, k: (j, 0, 0))
    tile_ps = pl.BlockSpec((tm, ps), lambda i, j, k: (i, j))
    merged, pa, pb = _mm(
        "proj_merge", "nn", (nt, 4, 1),
        [(ya, y_spec, wpa, proj_spec), (yb, y_spec, wpb, proj_spec)],
        [(gates, pl.BlockSpec((None, tm, ps), lambda i, j, k: (0, i, j))),
         (gates, pl.BlockSpec((None, tm, ps), lambda i, j, k: (1, i, j)))],
        [(_sds((t, d), BF16), tile_ps)] * 3, (tm, ps), merge_epilogue)

    def residual_epilogue(accs, extras, outs, ij):
        outs[0][...] = extras[0][...] + accs[0]

    (h2,) = _mm(
        "out_proj", "nn", (nt, 1, 1),
        [(merged, row, wout, pl.BlockSpec((d, d), lambda i, j, k: (0, 0)))],
        [(h1, row)], [(_sds((t, d), F32), row)], (tm, d), residual_epilogue)

    h3, ffn2_saved = _ffn_fwd("ffn2", h2, small["ffn2_norm"], w["ffn2_w_gu"], w["ffn2_w_down"], {})
    n3 = _rms_fwd("ple_norm", h3, small["ple_norm"])
    tile_es = pl.BlockSpec((tm, es), lambda i, j, k: (i, j))
    (pe,) = _mm(
        "ple_embed", "nn", (nt, 4, 1),
        [(pt, pl.BlockSpec((tm, pdim), lambda i, j, k: (i, 0)), wpe, pl.BlockSpec((None, pdim, es), lambda i, j, k: (j, 0, 0)))],
        [], [(_sds((t, d), F32), tile_es)], (tm, es), _cast_epilogue)

    th = _pick(d, 512)

    def head_epilogue(accs, extras, outs, ij):
        h3_ref, pe_ref, tgt_ref = extras
        dy_ref, dpe_ref, dz_ref, loss_ref = outs
        pg = jax.nn.sigmoid(accs[0])
        pev = pe_ref[...]
        diff = h3_ref[...] + pg * pev - tgt_ref[...]
        dy = diff * (1.0 / d)
        dy_ref[...] = dy
        dpe_ref[...] = (dy * pg).astype(BF16)
        dz_ref[...] = (dy * pev * pg * (1.0 - pg)).astype(BF16)
        _accumulate(loss_ref, jnp.full(loss_ref.shape, jnp.sum(diff * diff), F32), (ij[0] == 0) & (ij[1] == 0))

    tile_h = pl.BlockSpec((tm, th), lambda i, j, k: (i, j))
    dy, dpe, dz, loss_acc = _mm(
        "ple_gate_loss", "nn", (nt, d // th, 1),
        [(n3, row, wpg, pl.BlockSpec((d, th), lambda i, j, k: (0, j)))],
        [(h3, tile_h), (pe, tile_h), (tgt, tile_h)],
        [(_sds((t, d), F32), tile_h), (_sds((t, d), BF16), tile_h), (_sds((t, d), BF16), tile_h),
         (_sds((8, 128), F32), pl.BlockSpec((8, 128), lambda i, j, k: (0, 0)))],
        (tm, th), head_epilogue, j_outer=True)
    loss = 0.5 * loss_acc[0, 0] / d

    nk = t // tk
    (dwpe,) = _mm(
        "d_w_ple_proj", "tn", (1, 4, nk),
        [(pt, pl.BlockSpec((tk, pdim), lambda i, j, k: (k, 0)), dpe, pl.BlockSpec((tk, es), lambda i, j, k: (k, j)))],
        [], [(_sds((4, pdim, es), BF16), pl.BlockSpec((None, pdim, es), lambda i, j, k: (j, 0, 0)))],
        (pdim, es), _cast_epilogue)

    def dense_grad(name, a, dyb):
        (res,) = _mm(
            name, "tn", (1, d // th, nk),
            [(a, pl.BlockSpec((tk, d), lambda i, j, k: (k, 0)), dyb, pl.BlockSpec((tk, th), lambda i, j, k: (k, j)))],
            [], [(_sds((d, d), BF16), pl.BlockSpec((d, th), lambda i, j, k: (0, j)))], (d, th), _cast_epilogue)
        return res

    dwpg = dense_grad("d_w_ple_gate", n3, dz)
    tmn = _pick(t, 1024, 8)
    extras, outs = _rms_bwd_io(h3, small["ple_norm"], dy, tmn)
    dh3, dh3_b, d_ple_norm = _mm(
        "d_ple_norm", "nt", (t // tmn, 1, 1),
        [(dz, pl.BlockSpec((tmn, d), lambda i, j, k: (i, 0)), wpg, pl.BlockSpec((d, d), lambda i, j, k: (0, 0)))],
        extras, outs, (tmn, d), _rms_bwd_epilogue)

    late = GradPipe(GATHER_LATE)
    dh2, dh2_b, d_ffn2_norm, dwgu2, dwd2 = _ffn_bwd(
        "ffn2", dh3, dh3_b, h2, small["ffn2_norm"], w["ffn2_w_gu"], w["ffn2_w_down"], ffn2_saved,
        {"dnorm": lambda dwgu, dwd: [late.exchange([dwgu, dwd, dwpg, dwpe])]})

    def dmerge_epilogue(accs, extras, outs, ij):
        dmo = accs[0]
        g_ref, pa_ref, pb_ref = extras
        dg_ref, dpa_ref, dpb_ref = outs
        ga = g_ref[0].astype(F32)
        gb = g_ref[1].astype(F32)
        dg_ref[0] = (dmo * pa_ref[...].astype(F32) * ga * (1.0 - ga)).astype(BF16)
        dg_ref[1] = (dmo * pb_ref[...].astype(F32) * gb * (1.0 - gb)).astype(BF16)
        dpa_ref[...] = (dmo * ga).astype(BF16)
        dpb_ref[...] = (dmo * gb).astype(BF16)

    g_spec = pl.BlockSpec((2, tm, th), lambda i, j, k: (0, i, j))
    dgates, dpa, dpb = _mm(
        "d_merge", "nt", (nt, d // th, 1),
        [(dh2_b, row, wout, pl.BlockSpec((th, d), lambda i, j, k: (j, 0)))],
        [(gates, g_spec), (pa, tile_h), (pb, tile_h)],
        [(_sds((2, t, d), BF16), g_spec), (_sds((t, d), BF16), tile_h), (_sds((t, d), BF16), tile_h)],
        (tm, th), dmerge_epilogue, j_outer=True)
    dwout = dense_grad("d_w_out", merged, dh2_b)

    yk_spec = pl.BlockSpec((tk, A_WIDTH), lambda i, j, k: (k, 0))
    dk_spec = pl.BlockSpec((tk, ps), lambda i, j, k: (k, j))
    dproj = (_sds((4, A_WIDTH, ps), BF16), proj_spec)
    dwpa, dwpb = _mm(
        "d_w_proj", "tn", (1, 4, nk),
        [(ya, yk_spec, dpa, dk_spec), (yb, yk_spec, dpb, dk_spec)], [], [dproj, dproj], (A_WIDTH, ps), _cast_epilogue)
    dproj_a = pl.BlockSpec((tm, ps), lambda i, j, k: (i, k))
    wproj_k = pl.BlockSpec((None, A_WIDTH, ps), lambda i, j, k: (k, 0, 0))
    dya, dyb = _mm(
        "d_attn_out", "nt", (nt, 1, 4),
        [(dpa, dproj_a, wpa, wproj_k), (dpb, dproj_a, wpb, wproj_k)], [],
        [(_sds((t, A_WIDTH), BF16), y_spec)] * 2, (tm, A_WIDTH), _cast_epilogue)

    dqa, dka, dva, dbias_a, _, dgqa, dgka = _attn_bwd(
        "attn_a_bwd", qkv, bias_a, sink_a, gqa, gka, ya, dya, lse_a, ATTN_A, n_batch, True, comms=[late.scatter()])
    dqb, dkb, dvb, _, dsink_b, dgqb, dgkb = _attn_bwd(
        "attn_b_bwd", qkv, bias_b, sink_b, gqb, gkb, yb, dyb, lse_b, ATTN_B, n_batch, False, comms=[late.forward()])
    dqkv = jnp.concatenate([dqa, dka, dva, dqb, dkb, dvb], axis=1)

    (dwgate,) = _mm(
        "d_w_gate", "tn", (1, 4, nk),
        [(un, pl.BlockSpec((tk, d), lambda i, j, k: (k, 0)),
          dgates, pl.BlockSpec((None, tk, gs), lambda i, j, k: (j // 2, k, j % 2)))],
        [], [(_sds((4, d, gs), BF16), pl.BlockSpec((None, d, gs), lambda i, j, k: (j, 0, 0)))], (d, gs), _cast_epilogue)
    (dwin,) = _mm(
        "d_w_in", "tn", (1, 2, nk),
        [(un, pl.BlockSpec((tk, d), lambda i, j, k: (k, 0)), dqkv, pl.BlockSpec((tk, tin), lambda i, j, k: (k, j)))],
        [], [(_sds((d, ncols), BF16), pl.BlockSpec((d, tin), lambda i, j, k: (0, j)))], (d, tin), _cast_epilogue)

    mixer = GradPipe(GATHER_MIXER)
    extras, outs = _rms_bwd_io(h1, small["mix_norm"], dh2, tmn)
    dh1, dh1_b, d_mix_norm = _mm(
        "d_mix_norm", "nt", (t // tmn, 1, 6),
        [(dgates, pl.BlockSpec((None, tmn, gs), lambda i, j, k: (jnp.minimum(k, 3) // 2, i, jnp.minimum(k, 3) % 2)),
          wgate, pl.BlockSpec((None, d, gs), lambda i, j, k: (jnp.minimum(k, 3), 0, 0))),
         (dqkv, pl.BlockSpec((tmn, tin), lambda i, j, k: (i, jnp.maximum(k - 4, 0))),
          w_in, pl.BlockSpec((d, tin), lambda i, j, k: (0, jnp.maximum(k - 4, 0))))],
        extras, outs, (tmn, d), _rms_bwd_epilogue, steps=[4, 2],
        comms=[mixer.exchange([dwin, dwgate, dwpa, dwpb, dwout])])

    up1 = GradPipe(["ffn1_w_gu"])
    down1 = GradPipe(["ffn1_w_down"])
    dx, _, d_ffn1_norm, _, _ = _ffn_bwd(
        "ffn1", dh1, dh1_b, xt, small["ffn1_norm"], w["ffn1_w_gu"], w["ffn1_w_down"], ffn1_saved,
        {"dwgu": lambda: [mixer.scatter()],
         "dwd": lambda dwgu: [mixer.forward(), up1.exchange([dwgu])],
         "dnorm": lambda dwgu, dwd: [up1.scatter(), down1.exchange([dwd])]})
    _run_comms("grad_tail_scatter", [up1.forward(), down1.scatter()])
    _run_comms("grad_tail_forward", [down1.forward()])
    terms = {**late.terms(), **mixer.terms(), **up1.terms(), **down1.terms()}

    def fold(v):
        return v[0, :HEAD_DIM] + v[0, HEAD_DIM:]

    small_grads = {"ffn1_norm": d_ffn1_norm, "mix_norm": d_mix_norm, "ffn2_norm": d_ffn2_norm,
                   "ple_norm": d_ple_norm, "a_q_norm": fold(dgqa), "a_k_norm": fold(dgka),
                   "b_q_norm": fold(dgqb), "b_k_norm": fold(dgkb), "a_rel_bias": _rel_bias_grad(dbias_a),
                   "b_sinks": jnp.sum(dsink_b, axis=1)}
    return loss, dx, terms, small_grads


def kernel(x, p, ffn1_norm, ffn1_w_gu, ffn1_w_down, mix_norm, w_in, a_q_norm, a_k_norm, a_rel_bias, b_q_norm, b_k_norm, b_sinks, w_gate, w_proj_a, w_proj_b, w_out, ffn2_norm, ffn2_w_gu, ffn2_w_down, ple_norm, w_ple_gate, w_ple_proj, loss_target, m_ffn1_norm, m_ffn1_w_gu, m_ffn1_w_down, m_mix_norm, m_w_in, m_a_q_norm, m_a_k_norm, m_a_rel_bias, m_b_q_norm, m_b_k_norm, m_b_sinks, m_w_gate, m_w_proj_a, m_w_proj_b, m_w_out, m_ffn2_norm, m_ffn2_w_gu, m_ffn2_w_down, m_ple_norm, m_w_ple_gate, m_w_ple_proj, v_ffn1_norm, v_ffn1_w_gu, v_ffn1_w_down, v_mix_norm, v_w_in, v_a_q_norm, v_a_k_norm, v_a_rel_bias, v_b_q_norm, v_b_k_norm, v_b_sinks, v_w_gate, v_w_proj_a, v_w_proj_b, v_w_out, v_ffn2_norm, v_ffn2_w_gu, v_ffn2_w_down, v_ple_norm, v_w_ple_gate, v_w_ple_proj):
    given = dict(locals())
    n_batch, s, d = x.shape
    t = n_batch * s
    xt = x.reshape(t, d)
    pt = p.reshape(t, p.shape[-1])
    tgt = loss_target.reshape(t, d)

    shards = {}
    for name in BIG:
        (shards[name],) = _ew("cast_" + name, lambda v: (v,), [given[name][0]], [BF16])
    small = {name: given[name] for name in SMALL}
    core = lax.axis_index("c").astype(jnp.int32).reshape(1)
    loss, dx, terms, small_grads = _local_step(xt, pt, tgt, n_batch, shards, small, core)

    grads, deltas, new_m, new_v = {}, {}, {}, {}
    for name in BIG:
        gw, dl, nm, nv = _adamw_terms("adamw_" + name, terms[name], given[name][0], given["m_" + name][0],
                                      given["v_" + name][0])
        grads[name], deltas[name], new_m[name], new_v[name] = gw[None], dl[None], nm[None], nv[None]

    small_shapes = [given[name].shape for name in SMALL] + [()]
    g_pack = _all_reduce_small(_pack([small_grads[name] for name in SMALL] + [loss]))
    zero = jnp.zeros((), F32)
    w_pack = _pack([given[name] for name in SMALL] + [zero])
    m_pack = _pack([given["m_" + name] for name in SMALL] + [zero])
    v_pack = _pack([given["v_" + name] for name in SMALL] + [zero])
    d_pack, nm_pack, nv_pack = _ew("adamw_small", lambda wv, gv, mv, vv: _adamw_math(wv, gv, mv, vv),
                                   [w_pack, g_pack, m_pack, v_pack], [F32] * 3)
    g_small = _unpack(g_pack, small_shapes)
    loss_total = g_small[-1]
    for name, gv, dv, mv, vv in zip(SMALL, g_small, _unpack(d_pack, small_shapes), _unpack(nm_pack, small_shapes),
                                    _unpack(nv_pack, small_shapes)):
        grads[name], deltas[name], new_m[name], new_v[name] = gv, dv, mv, vv

    return (loss_total, dx.reshape(x.shape), *[grads[n] for n in WEIGHTS], *[deltas[n] for n in WEIGHTS],
            *[new_m[n] for n in WEIGHTS], *[new_v[n] for n in WEIGHTS])
```

```python
import functools

import numpy as np
import jax
import jax.numpy as jnp
from jax import lax
from jax.experimental import pallas as pl
from jax.experimental.pallas import tpu as pltpu

F32 = jnp.float32
BF16 = jnp.bfloat16

CHUNK = 64
HEAD_DIM = 64
A_PREV_CHUNKS = 8
A_MAX_REL = 128
N_HEADS = 8
B_KV_HEADS = 2
B_PREV_CHUNKS = 2
A_WIDTH = N_HEADS * HEAD_DIM
B_KV_WIDTH = B_KV_HEADS * HEAD_DIM
EPS = 1e-6
NEG_INF = -1e30
ATTN_SCALE = HEAD_DIM ** -0.5
Q_BLOCK = 128
PAIR = 2 * HEAD_DIM

ADAM_LR = 0.001
ADAM_B1 = 0.9
ADAM_B2 = 0.999
ADAM_EPS = 1e-08
ADAM_WD = 0.01
ADAM_STEP = 10

N_CHIPS = 4
N_DEV = 8
VMEM_LIMIT_V7X = 56 * 1024 * 1024
MESH = pl.DeviceIdType.MESH
ANY = pl.BlockSpec(memory_space=pl.ANY)

_DN = {
    "nn": (((1,), (0,)), ((), ())),
    "nt": (((1,), (1,)), ((), ())),
    "tn": (((0,), (0,)), ((), ())),
}


def _pick(n, target, mult=128):
    best = None
    for d in range(mult, min(n, target) + 1, mult):
        if n % d == 0:
            best = d
    return n if best is None else best


def _dot(a, b, mode):
    return lax.dot_general(a.astype(BF16), b.astype(BF16), _DN[mode], preferred_element_type=F32)


def _params():
    return pltpu.CompilerParams(vmem_limit_bytes=VMEM_LIMIT_V7X)


class _Comm:
    def __init__(self, ins, outs, aliases, sems, start, finish):
        self.ins, self.outs, self.aliases, self.sems = list(ins), list(outs), dict(aliases), list(sems)
        self.start, self.finish = start, finish
        self.results = None


class _CommPlumbing:
    def __init__(self, comms, n_in, n_out, n_scratch):
        self.comms = list(comms)
        self.n_in, self.n_out, self.n_scratch = n_in, n_out, n_scratch
        self.args = [a for cm in self.comms for a in cm.ins]
        self.out_shape = [o for cm in self.comms for o in cm.outs]
        self.scratch = [s for cm in self.comms for s in cm.sems]
        self.aliases = {}
        i0, o0 = n_in, n_out
        for cm in self.comms:
            for a, b in cm.aliases.items():
                self.aliases[i0 + a] = o0 + b
            i0 += len(cm.ins)
            o0 += len(cm.outs)

    def run(self, in_refs, out_refs, scratch_refs, first, last):
        if not self.comms:
            return
        parts = []
        i0, o0, s0 = self.n_in, self.n_out, self.n_scratch
        for cm in self.comms:
            parts.append((in_refs[i0:i0 + len(cm.ins)], out_refs[o0:o0 + len(cm.outs)],
                          scratch_refs[s0:s0 + len(cm.sems)]))
            i0 += len(cm.ins)
            o0 += len(cm.outs)
            s0 += len(cm.sems)

        @pl.when(first)
        def _():
            for cm, part in zip(self.comms, parts):
                cm.start(*part)

        @pl.when(last)
        def _():
            for cm, part in zip(self.comms, parts):
                cm.finish(*part)

    def deliver(self, results):
        o0 = self.n_out
        for cm in self.comms:
            cm.results = list(results[o0:o0 + len(cm.outs)])
            o0 += len(cm.outs)
        return list(results[:self.n_out])


def _swap_ij(spec):
    index_map = spec.index_map
    return pl.BlockSpec(spec.block_shape, lambda j, i, k: index_map(i, j, k))


MXU_COLUMNS_V7X = 256


def _mm(name, mode, grid, pairs, extras, outs, acc_shape, epilogue, steps=None, comms=(), j_outer=False,
        chunked=False):
    ni, nj, nk = grid
    n_in = 2 * len(pairs) + len(extras)
    n_out = len(outs)
    tn = acc_shape[1]
    col_chunks = None
    if chunked:
        assert nk == 1 and steps is None and mode in ("nn", "nt")
        col_chunks = [(c0, min(MXU_COLUMNS_V7X, tn - c0)) for c0 in range(0, tn, MXU_COLUMNS_V7X)]
    n_acc = 0 if chunked else (len(pairs) if steps is None else 1)
    plumb = _CommPlumbing(comms, n_in, n_out, n_acc)
    n_all_in = n_in + len(plumb.args)
    n_all_out = n_out + len(plumb.out_shape)
    if j_outer:
        grid = (nj, ni, nk)
        pairs = [(a, _swap_ij(a_spec), b, _swap_ij(b_spec)) for a, a_spec, b, b_spec in pairs]
        extras = [(e, _swap_ij(e_spec)) for e, e_spec in extras]
        outs = [(o, _swap_ij(o_spec)) for o, o_spec in outs]

    def body(*refs):
        in_refs = refs[:n_all_in]
        out_refs = refs[n_all_in:n_all_in + n_all_out]
        scratch = refs[n_all_in + n_all_out:]
        accs = scratch[:n_acc]
        i = pl.program_id(1 if j_outer else 0)
        j = pl.program_id(0 if j_outer else 1)
        k = pl.program_id(2)

        def contrib(p, acc):
            acc[...] += _dot(in_refs[2 * p][...], in_refs[2 * p + 1][...], mode)

        if col_chunks:
            def cols(ref, c0, cs):
                if ref.shape[-1] != tn:
                    return ref
                return ref.at[(slice(None),) * (len(ref.shape) - 1) + (pl.ds(c0, cs),)]

            lhs = [in_refs[2 * p][...] for p in range(len(pairs))]
            for ci, (c0, cs) in enumerate(col_chunks):
                vals = []
                for p in range(len(pairs)):
                    b_ref = in_refs[2 * p + 1]
                    rhs = b_ref[:, c0:c0 + cs] if mode == "nn" else b_ref[c0:c0 + cs, :]
                    vals.append(_dot(lhs[p], rhs, mode))
                epilogue(vals, [cols(r, c0, cs) for r in in_refs[2 * len(pairs):n_in]],
                         [cols(r, c0, cs) for r in out_refs[:n_out]], (i, j * len(col_chunks) + ci))
        else:
            @pl.when(k == 0)
            def _():
                for acc in accs:
                    acc[...] = jnp.zeros(acc.shape, F32)

            if steps is None:
                for p in range(len(pairs)):
                    contrib(p, accs[p])
            else:
                lo = 0
                for p, n in enumerate(steps):
                    pl.when((k >= lo) & (k < lo + n))(functools.partial(contrib, p, accs[0]))
                    lo += n

            @pl.when(k == nk - 1)
            def _():
                epilogue([acc[...] for acc in accs], in_refs[2 * len(pairs):n_in], out_refs[:n_out], (i, j))

        plumb.run(in_refs, out_refs, scratch, (i == 0) & (j == 0) & (k == 0),
                  (i == ni - 1) & (j == nj - 1) & (k == nk - 1))

    args, in_specs = [], []
    for a, a_spec, b, b_spec in pairs:
        args += [a, b]
        in_specs += [a_spec, b_spec]
    for e, e_spec in extras:
        args.append(e)
        in_specs.append(e_spec)
    res = pl.pallas_call(
        body,
        name=name,
        grid=grid,
        in_specs=in_specs + [ANY] * len(plumb.args),
        out_specs=[s for _, s in outs] + [ANY] * len(plumb.out_shape),
        out_shape=[o for o, _ in outs] + plumb.out_shape,
        scratch_shapes=[pltpu.VMEM(acc_shape, F32) for _ in range(n_acc)] + plumb.scratch,
        input_output_aliases=plumb.aliases,
        compiler_params=_params(),
    )(*args, *plumb.args)
    return plumb.deliver(res)


def _sds(shape, dtype):
    return jax.ShapeDtypeStruct(shape, dtype)


def _accumulate(ref, value, first):
    @pl.when(first)
    def _():
        ref[...] = value

    @pl.when(jnp.logical_not(first))
    def _():
        ref[...] += value


def _rms_fwd(name, x, gain):
    t, d = x.shape
    tm = _pick(t, 512, 8)

    def body(x_ref, g_ref, y_ref):
        xv = x_ref[...]
        rstd = lax.rsqrt(jnp.mean(xv * xv, axis=-1, keepdims=True) + EPS)
        y_ref[...] = (xv * rstd * g_ref[...]).astype(BF16)

    return pl.pallas_call(
        body, name=name, grid=(t // tm,),
        in_specs=[pl.BlockSpec((tm, d), lambda i: (i, 0)), pl.BlockSpec((1, d), lambda i: (0, 0))],
        out_specs=pl.BlockSpec((tm, d), lambda i: (i, 0)),
        out_shape=_sds((t, d), BF16),
        compiler_params=_params(),
    )(x, gain)


def _rms_bwd_epilogue(accs, extras, outs, ij):
    x_ref, g_ref, r_ref = extras
    dh_ref, dhb_ref, dg_ref = outs
    dn = accs[0]
    xv = x_ref[...]
    rstd = lax.rsqrt(jnp.mean(xv * xv, axis=-1, keepdims=True) + EPS)
    xhat = xv * rstd
    gd = dn * g_ref[...]
    dx = rstd * (gd - xhat * jnp.mean(gd * xhat, axis=-1, keepdims=True))
    dh = r_ref[...] + dx
    dh_ref[...] = dh
    dhb_ref[...] = dh.astype(BF16)
    _accumulate(dg_ref, jnp.sum(dn * xhat, axis=0, keepdims=True), ij[0] == 0)


def _rms_bwd_io(x, gain, dres, tm):
    t, d = x.shape
    row = pl.BlockSpec((tm, d), lambda i, j, k: (i, 0))
    extras = [(x, row), (gain, pl.BlockSpec((1, d), lambda i, j, k: (0, 0))), (dres, row)]
    outs = [(_sds((t, d), F32), row), (_sds((t, d), BF16), row),
            (_sds((1, d), F32), pl.BlockSpec((1, d), lambda i, j, k: (0, 0)))]
    return extras, outs


def _ffn_fwd(tag, h, gain, wgu, wd, hooks):
    t, d = h.shape
    fs = wgu.shape[2]
    f = 2 * fs
    tm = _pick(t, 512, 8)
    n = _rms_fwd(tag + "_norm", h, gain)

    def up_epilogue(accs, extras, outs, ij):
        g, u = accs
        gu_ref, a_ref = outs
        gu_ref[0] = g.astype(BF16)
        gu_ref[1] = u.astype(BF16)
        a_ref[...] = (g * jax.nn.sigmoid(g) * u).astype(BF16)

    a_spec = pl.BlockSpec((tm, d), lambda i, j, k: (i, 0))
    gu, a = _mm(
        tag + "_up", "nn", (t // tm, 2, 1),
        [(n, a_spec, wgu, pl.BlockSpec((None, d, fs), lambda i, j, k: (j, 0, 0))),
         (n, a_spec, wgu, pl.BlockSpec((None, d, fs), lambda i, j, k: (j + 2, 0, 0)))],
        [],
        [(_sds((2, t, f), BF16), pl.BlockSpec((2, tm, fs), lambda i, j, k: (0, i, j))),
         (_sds((t, f), BF16), pl.BlockSpec((tm, fs), lambda i, j, k: (i, j)))],
        (tm, fs), up_epilogue, comms=hooks.get("up", lambda: ())(), j_outer=True, chunked=True)

    def down_epilogue(accs, extras, outs, ij):
        outs[0][...] = extras[0][...] + 0.5 * accs[0]

    row = pl.BlockSpec((tm, d), lambda i, j, k: (i, 0))
    (h_new,) = _mm(
        tag + "_down", "nn", (t // tm, 1, 1),
        [(a, pl.BlockSpec((tm, f), lambda i, j, k: (i, 0)), wd, pl.BlockSpec((f, d), lambda i, j, k: (0, 0)))],
        [(h, row)], [(_sds((t, d), F32), row)], (tm, d), down_epilogue, comms=hooks.get("down", lambda: ())())
    return h_new, (n, gu, a)


def _ffn_bwd(tag, dh, dh_b, h, gain, wgu, wd, saved, hooks):
    n, gu, a = saved
    t, d = h.shape
    fs = wgu.shape[2]
    f = 2 * fs
    tm = _pick(t, 512, 8)
    tk = _pick(t, 512, 8)

    def dact_epilogue(accs, extras, outs, ij):
        da = 0.5 * accs[0]
        g = extras[0][0].astype(F32)
        u = extras[0][1].astype(F32)
        sg = jax.nn.sigmoid(g)
        outs[0][0] = (da * u * sg * (1.0 + g * (1.0 - sg))).astype(BF16)
        outs[0][1] = (da * g * sg).astype(BF16)

    gu_spec = pl.BlockSpec((2, tm, fs), lambda i, j, k: (0, i, j))
    (dgu,) = _mm(
        tag + "_dact", "nt", (t // tm, 2, 1),
        [(dh_b, pl.BlockSpec((tm, d), lambda i, j, k: (i, 0)), wd, pl.BlockSpec((fs, d), lambda i, j, k: (j, 0)))],
        [(gu, gu_spec)], [(_sds((2, t, f), BF16), gu_spec)], (tm, fs), dact_epilogue, j_outer=True, chunked=True,
        comms=hooks.get("dact", lambda: ())())

    def cast_epilogue(accs, extras, outs, ij):
        outs[0][...] = accs[0].astype(BF16)

    (dwgu,) = _mm(
        tag + "_dwgu", "tn", (1, 4, t // tk),
        [(n, pl.BlockSpec((tk, d), lambda i, j, k: (k, 0)),
          dgu, pl.BlockSpec((None, tk, fs), lambda i, j, k: (j // 2, k, j % 2)))],
        [], [(_sds((4, d, fs), BF16), pl.BlockSpec((None, d, fs), lambda i, j, k: (j, 0, 0)))], (d, fs), cast_epilogue,
        comms=hooks.get("dwgu", lambda: ())())

    def half_epilogue(accs, extras, outs, ij):
        outs[0][...] = (0.5 * accs[0]).astype(BF16)

    (dwd,) = _mm(
        tag + "_dwd", "tn", (2, 1, t // tk),
        [(a, pl.BlockSpec((tk, fs), lambda i, j, k: (k, i)), dh_b, pl.BlockSpec((tk, d), lambda i, j, k: (k, 0)))],
        [], [(_sds((f, d), BF16), pl.BlockSpec((fs, d), lambda i, j, k: (i, 0)))], (fs, d), half_epilogue,
        comms=hooks.get("dwd", lambda g: ())(dwgu))

    tmn = _pick(t, 1024, 8)
    extras, outs = _rms_bwd_io(h, gain, dh, tmn)
    dh_in, dh_in_b, dgain = _mm(
        tag + "_dnorm", "nt", (t // tmn, 1, 4),
        [(dgu, pl.BlockSpec((None, tmn, fs), lambda i, j, k: (k // 2, i, k % 2)),
          wgu, pl.BlockSpec((None, d, fs), lambda i, j, k: (k, 0, 0)))],
        extras, outs, (tmn, d), _rms_bwd_epilogue, comms=hooks.get("dnorm", lambda g, w: ())(dwgu, dwd))
    return dh_in, dh_in_b, dgain, dwgu, dwd


def _lane_lo(shape):
    return lax.broadcasted_iota(jnp.int32, shape, 1) < HEAD_DIM


def _pair_norm(xv, gain):
    lo = _lane_lo(xv.shape)
    x2 = xv * xv
    ms_lo = jnp.sum(jnp.where(lo, x2, 0.0), axis=-1, keepdims=True) * (1.0 / HEAD_DIM)
    ms_hi = jnp.sum(jnp.where(lo, 0.0, x2), axis=-1, keepdims=True) * (1.0 / HEAD_DIM)
    rstd = jnp.where(lo, lax.rsqrt(ms_lo + EPS), lax.rsqrt(ms_hi + EPS))
    xhat = xv * rstd
    return xhat * gain, xhat, rstd


def _pair_norm_bwd(dn, xhat, rstd, gain):
    lo = _lane_lo(dn.shape)
    gd = dn * gain
    t = gd * xhat
    m_lo = jnp.sum(jnp.where(lo, t, 0.0), axis=-1, keepdims=True) * (1.0 / HEAD_DIM)
    m_hi = jnp.sum(jnp.where(lo, 0.0, t), axis=-1, keepdims=True) * (1.0 / HEAD_DIM)
    dx = rstd * (gd - xhat * jnp.where(lo, m_lo, m_hi))
    return dx, jnp.sum(dn * xhat, axis=0, keepdims=True)


def _half(xv, hi):
    lo = _lane_lo(xv.shape)
    return jnp.where(lo, 0, xv) if hi else jnp.where(lo, xv, 0)


def _head_place(h, group):
    kh = h // group
    return h // 2, h % 2, kh // 2, kh % 2


def _attn_window(i, prev):
    q0 = i * Q_BLOCK
    start = jnp.maximum(q0 - prev, 0)
    off = start - (q0 - prev)
    return pl.multiple_of(start, Q_BLOCK), pl.multiple_of(off, Q_BLOCK)


def _attn_specs(cfg, s, nq):
    kw = cfg["kw"]
    q_spec = pl.BlockSpec((Q_BLOCK, A_WIDTH), lambda b, i: (b * nq + i, cfg["qblk"]))
    k_spec = pl.BlockSpec((s, kw), lambda b, i: (b, cfg["kblk"]))
    v_spec = pl.BlockSpec((s, kw), lambda b, i: (b, cfg["vblk"]))
    return q_spec, k_spec, v_spec


def _const_spec(shape):
    return pl.BlockSpec(shape, lambda b, i: (0,) * len(shape))


def _attn_fwd(name, qkv, bias_t, sink, gq, gk, cfg, n_batch, comms=()):
    t = qkv.shape[0]
    s = t // n_batch
    nq = s // Q_BLOCK
    prev, group, kw = cfg["prev"], cfg["group"], cfg["kw"]
    w = prev + Q_BLOCK
    wext = bias_t.shape[1]
    plumb = _CommPlumbing(comms, 7, 2, 1)
    n_all_in = 7 + len(plumb.args)
    n_all_out = 2 + len(plumb.out_shape)

    def body(*refs):
        q_ref, k_ref, v_ref, bias_ref, sink_ref, gq_ref, gk_ref = refs[:7]
        y_ref, lse_ref = refs[n_all_in:n_all_in + 2]
        kn_ref = refs[n_all_in + n_all_out]
        i = pl.program_id(1)
        plumb.run(refs[:n_all_in], refs[n_all_in:n_all_in + n_all_out], refs[n_all_in + n_all_out:],
                  (pl.program_id(0) == 0) & (i == 0), (pl.program_id(0) == n_batch - 1) & (i == nq - 1))

        @pl.when(i == 0)
        def _():
            for jk in range(kw // PAIR):
                kn, _, _ = _pair_norm(k_ref[:, pl.ds(jk * PAIR, PAIR)].astype(F32), gk_ref[...])
                kn_ref[:, pl.ds(jk * PAIR, PAIR)] = kn.astype(BF16)

        start, off = _attn_window(i, prev)
        sub = lax.broadcasted_iota(jnp.int32, (N_HEADS, Q_BLOCK), 0)
        lse = jnp.zeros((N_HEADS, Q_BLOCK), F32)
        for jq in range(N_HEADS // 2):
            qn, _, _ = _pair_norm(q_ref[:, pl.ds(jq * PAIR, PAIR)].astype(F32), gq_ref[...])
            qn = qn * ATTN_SCALE
            o_pair = jnp.zeros((Q_BLOCK, PAIR), F32)
            for hq in range(2):
                h = 2 * jq + hq
                _, _, jk, hk = _head_place(h, group)
                qm = _half(qn, hq)
                if hq != hk:
                    qm = pltpu.roll(qm, HEAD_DIM, 1)
                k_w = kn_ref[pl.ds(start, w), pl.ds(jk * PAIR, PAIR)]
                st = _dot(k_w, qm, "nt") + bias_ref[h, pl.ds(off, w), :]
                sk = sink_ref[h:h + 1, 0:1]
                m = jnp.maximum(jnp.max(st, axis=0, keepdims=True), sk)
                p = jnp.exp(st - m)
                l = jnp.sum(p, axis=0, keepdims=True) + jnp.exp(sk - m)
                v_w = _half(v_ref[pl.ds(start, w), pl.ds(jk * PAIR, PAIR)], hk)
                o = _dot(p * (1.0 / l), v_w, "tn")
                if hq != hk:
                    o = pltpu.roll(o, HEAD_DIM, 1)
                o_pair = o_pair + o
                lse = jnp.where(sub == h, m + jnp.log(l), lse)
            y_ref[:, pl.ds(jq * PAIR, PAIR)] = o_pair.astype(BF16)
        lse_ref[...] = lse

    q_spec, k_spec, v_spec = _attn_specs(cfg, s, nq)
    res = pl.pallas_call(
        body, name=name, grid=(n_batch, nq),
        in_specs=[q_spec, k_spec, v_spec, _const_spec((N_HEADS, wext, Q_BLOCK)), _const_spec((N_HEADS, 128)),
                  _const_spec((1, PAIR)), _const_spec((1, PAIR))] + [ANY] * len(plumb.args),
        out_specs=[pl.BlockSpec((Q_BLOCK, A_WIDTH), lambda b, i: (b * nq + i, 0)),
                   pl.BlockSpec((None, N_HEADS, Q_BLOCK), lambda b, i: (b * nq + i, 0, 0))]
        + [ANY] * len(plumb.out_shape),
        out_shape=[_sds((t, A_WIDTH), BF16), _sds((t // Q_BLOCK, N_HEADS, Q_BLOCK), F32)] + plumb.out_shape,
        scratch_shapes=[pltpu.VMEM((s, kw), BF16)] + plumb.scratch,
        input_output_aliases=plumb.aliases,
        compiler_params=_params(),
    )(qkv, qkv, qkv, bias_t, sink, gq, gk, *plumb.args)
    return plumb.deliver(res)


def _attn_bwd(name, qkv, bias_t, sink, gq, gk, y, dy, lse, cfg, n_batch, want_dbias, comms=()):
    t = qkv.shape[0]
    s = t // n_batch
    nq = s // Q_BLOCK
    prev, group, kw = cfg["prev"], cfg["group"], cfg["kw"]
    w = prev + Q_BLOCK
    wext = bias_t.shape[1]

    plumb = _CommPlumbing(comms, 10, 7, 3)
    n_all_in = 10 + len(plumb.args)
    n_all_out = 7 + len(plumb.out_shape)

    def body(*refs):
        q_ref, k_ref, v_ref, bias_ref, sink_ref, gq_ref, gk_ref, y_ref, dy_ref, lse_ref = refs[:10]
        dq_ref, dk_ref, dv_ref, db_ref, dsink_ref, dgq_ref, dgk_ref = refs[n_all_in:n_all_in + 7]
        kn_ref, dkn_ref, dvs_ref = refs[n_all_in + n_all_out:n_all_in + n_all_out + 3]
        b = pl.program_id(0)
        i = pl.program_id(1)
        first = (b == 0) & (i == 0)
        plumb.run(refs[:n_all_in], refs[n_all_in:n_all_in + n_all_out], refs[n_all_in + n_all_out:],
                  first, (b == n_batch - 1) & (i == nq - 1))

        @pl.when(i == 0)
        def _():
            for jk in range(kw // PAIR):
                kn, _, _ = _pair_norm(k_ref[:, pl.ds(jk * PAIR, PAIR)].astype(F32), gk_ref[...])
                kn_ref[:, pl.ds(jk * PAIR, PAIR)] = kn.astype(BF16)
            dkn_ref[...] = jnp.zeros(dkn_ref.shape, F32)
            dvs_ref[...] = jnp.zeros(dvs_ref.shape, F32)

        @pl.when(first)
        def _():
            db_ref[...] = jnp.zeros(db_ref.shape, F32)
            dsink_ref[...] = jnp.zeros(dsink_ref.shape, F32)
            dgq_ref[...] = jnp.zeros(dgq_ref.shape, F32)
            dgk_ref[...] = jnp.zeros(dgk_ref.shape, F32)

        start, off = _attn_window(i, prev)
        sub_lo = lax.broadcasted_iota(jnp.int32, (PAIR, Q_BLOCK), 0) < HEAD_DIM
        for jq in range(N_HEADS // 2):
            cols = pl.ds(jq * PAIR, PAIR)
            qn, q_hat, q_rstd = _pair_norm(q_ref[:, cols].astype(F32), gq_ref[...])
            qn = qn * ATTN_SCALE
            do_pair = dy_ref[:, cols]
            prod_t = (do_pair.astype(F32) * y_ref[:, cols].astype(F32)).T
            dqn = jnp.zeros((Q_BLOCK, PAIR), F32)
            for hq in range(2):
                h = 2 * jq + hq
                _, _, jk, hk = _head_place(h, group)
                kcols = pl.ds(jk * PAIR, PAIR)
                delta = jnp.sum(jnp.where(sub_lo, 0.0, prod_t) if hq else jnp.where(sub_lo, prod_t, 0.0),
                                axis=0, keepdims=True)
                qm = _half(qn, hq)
                do_m = _half(do_pair, hq)
                if hq != hk:
                    qm = pltpu.roll(qm, HEAD_DIM, 1)
                    do_m = pltpu.roll(do_m.astype(F32), HEAD_DIM, 1)
                qm_b = qm.astype(BF16)
                do_b = do_m.astype(BF16)
                k_w = kn_ref[pl.ds(start, w), kcols]
                v_w = v_ref[pl.ds(start, w), kcols]
                lse_row = lse_ref[h:h + 1, :]
                st = _dot(k_w, qm_b, "nt") + bias_ref[h, pl.ds(off, w), :]
                p = jnp.exp(st - lse_row)
                dp = _dot(v_w, do_b, "nt")
                ds = p * (dp - delta)
                dsink_ref[h:h + 1, :] += -jnp.exp(sink_ref[h:h + 1, 0:1] - lse_row) * delta
                if want_dbias:
                    db_ref[h, pl.ds(off, w), :] += ds
                ds_b = ds.astype(BF16)
                dq_h = _half(_dot(ds_b, k_w, "tn"), hk)
                if hq != hk:
                    dq_h = pltpu.roll(dq_h, HEAD_DIM, 1)
                dqn = dqn + dq_h
                dkn_ref[pl.ds(start, w), kcols] += _half(_dot(ds_b, qm_b, "nn"), hk)
                dvs_ref[pl.ds(start, w), kcols] += _half(_dot(p, do_b, "nn"), hk)
            dq_raw, dg = _pair_norm_bwd(dqn * ATTN_SCALE, q_hat, q_rstd, gq_ref[...])
            dq_ref[:, cols] = dq_raw.astype(BF16)
            dgq_ref[...] += dg

        @pl.when(i == nq - 1)
        def _():
            for jk in range(kw // PAIR):
                kcols = pl.ds(jk * PAIR, PAIR)
                _, k_hat, k_rstd = _pair_norm(k_ref[:, kcols].astype(F32), gk_ref[...])
                dk_raw, dg = _pair_norm_bwd(dkn_ref[:, kcols], k_hat, k_rstd, gk_ref[...])
                dk_ref[:, kcols] = dk_raw.astype(BF16)
                dgk_ref[...] += dg
            dv_ref[...] = dvs_ref[...].astype(BF16)

    q_spec, k_spec, v_spec = _attn_specs(cfg, s, nq)
    row = pl.BlockSpec((Q_BLOCK, A_WIDTH), lambda b, i: (b * nq + i, 0))
    kv_out = pl.BlockSpec((s, kw), lambda b, i: (b, 0))
    res = pl.pallas_call(
        body, name=name, grid=(n_batch, nq),
        in_specs=[q_spec, k_spec, v_spec, _const_spec((N_HEADS, wext, Q_BLOCK)), _const_spec((N_HEADS, 128)),
                  _const_spec((1, PAIR)), _const_spec((1, PAIR)), row, row,
                  pl.BlockSpec((None, N_HEADS, Q_BLOCK), lambda b, i: (b * nq + i, 0, 0))] + [ANY] * len(plumb.args),
        out_specs=[row, kv_out, kv_out, _const_spec((N_HEADS, wext, Q_BLOCK)), _const_spec((N_HEADS, 128)),
                   _const_spec((1, PAIR)), _const_spec((1, PAIR))] + [ANY] * len(plumb.out_shape),
        out_shape=[_sds((t, A_WIDTH), BF16), _sds((t, kw), BF16), _sds((t, kw), BF16),
                   _sds((N_HEADS, wext, Q_BLOCK), F32), _sds((N_HEADS, 128), F32),
                   _sds((1, PAIR), F32), _sds((1, PAIR), F32)] + plumb.out_shape,
        scratch_shapes=[pltpu.VMEM((s, kw), BF16), pltpu.VMEM((s, kw), F32), pltpu.VMEM((s, kw), F32)]
        + plumb.scratch,
        input_output_aliases=plumb.aliases,
        compiler_params=_params(),
    )(qkv, qkv, qkv, bias_t, sink, gq, gk, y, dy, lse, *plumb.args)
    return plumb.deliver(res)


def _band_tables(prev_chunks):
    prev = prev_chunks * CHUNK
    wext = 2 * prev + Q_BLOCK
    jj = np.arange(wext)[:, None]
    ii = np.arange(Q_BLOCK)[None, :]
    dist = prev + ii - jj
    rel_chunk = (prev // CHUNK + ii // CHUNK) - jj // CHUNK
    allowed = (rel_chunk >= 0) & (rel_chunk <= prev_chunks)
    return dist, allowed


def _alibi_slopes():
    return np.array([2.0 ** (-8.0 * (h + 1) / N_HEADS) for h in range(N_HEADS)], dtype=np.float32)


def _diag_onehot(prev, wext):
    n_diag = wext + Q_BLOCK - 1
    idx = np.clip(prev + Q_BLOCK - 1 - np.arange(n_diag), -A_MAX_REL, A_MAX_REL) + A_MAX_REL
    onehot = np.zeros((n_diag, 2 * A_MAX_REL + 1), np.float32)
    onehot[np.arange(n_diag), idx] = 1.0
    return onehot


def _bias_a(rel_bias):
    prev = A_PREV_CHUNKS * CHUNK
    _, allowed = _band_tables(A_PREV_CHUNKS)
    wext = allowed.shape[0]
    n_diag = wext + Q_BLOCK - 1
    seq = jnp.dot(rel_bias, jnp.asarray(_diag_onehot(prev, wext).T), precision=lax.Precision.HIGHEST)
    seq = jnp.pad(seq, ((0, 0), (0, 1)))
    rows = jnp.broadcast_to(seq[:, None, :], (N_HEADS, Q_BLOCK, n_diag + 1)).reshape(N_HEADS, -1)
    skew = rows[:, :Q_BLOCK * n_diag].reshape(N_HEADS, Q_BLOCK, n_diag)
    tile = jnp.transpose(skew[:, :, Q_BLOCK - 1:Q_BLOCK - 1 + wext], (0, 2, 1))
    return jnp.where(jnp.asarray(allowed)[None], tile, NEG_INF)


def _bias_b():
    dist, allowed = _band_tables(B_PREV_CHUNKS)
    bias = -_alibi_slopes()[:, None, None] * np.abs(dist).astype(np.float32)[None]
    return jnp.asarray(np.where(allowed[None], bias, np.float32(NEG_INF)).astype(np.float32))


def _rel_bias_grad(db_t):
    prev = A_PREV_CHUNKS * CHUNK
    wext = db_t.shape[1]
    n_diag = wext + Q_BLOCK - 1
    wp = n_diag + Q_BLOCK - 1
    xp = jnp.pad(jnp.transpose(db_t, (0, 2, 1)), ((0, 0), (0, 0), (Q_BLOCK - 1, Q_BLOCK - 1)))
    flat = jnp.pad(xp.reshape(N_HEADS, Q_BLOCK * wp), ((0, 0), (0, Q_BLOCK)))
    skew = flat.reshape(N_HEADS, Q_BLOCK, wp + 1)[:, :, :n_diag]
    diag = jnp.sum(skew, axis=1)
    return jnp.dot(diag, jnp.asarray(_diag_onehot(prev, wext)), precision=lax.Precision.HIGHEST)


def _ew(name, fn, ins, out_dtypes):
    r, c = ins[0].shape
    rb = _pick(r, max(16, (1 << 19) // c), 16)
    spec = pl.BlockSpec((rb, c), lambda i: (i, 0))

    def body(*refs):
        vals = fn(*[ref[...] for ref in refs[:len(ins)]])
        for ref, val in zip(refs[len(ins):], vals):
            ref[...] = val.astype(ref.dtype)

    return pl.pallas_call(
        body, name=name, grid=(r // rb,), in_specs=[spec] * len(ins), out_specs=[spec] * len(out_dtypes),
        out_shape=[_sds((r, c), dt) for dt in out_dtypes], compiler_params=_params(),
    )(*ins)


def _adamw_math(w, g, m, v):
    m = ADAM_B1 * m + (1.0 - ADAM_B1) * g
    v = ADAM_B2 * v + (1.0 - ADAM_B2) * (g * g)
    m_hat = m / (1.0 - ADAM_B1 ** ADAM_STEP)
    v_hat = v / (1.0 - ADAM_B2 ** ADAM_STEP)
    delta = -ADAM_LR * (m_hat / (jnp.sqrt(v_hat) + ADAM_EPS) + ADAM_WD * w)
    return delta, m, v


def _adamw_terms(name, terms, w, m, v):
    r, c = w.shape
    hr = r // 2
    rb = _pick(hr, max(16, (1 << 17) // c), 16)
    nb = hr // rb

    def body(t_ref, w_ref, m_ref, v_ref, g_ref, d_ref, nm_ref, nv_ref):
        g = t_ref[0].astype(F32)
        for k in range(1, N_CHIPS):
            g = g + t_ref[k].astype(F32)
        delta, nm, nv = _adamw_math(w_ref[...], g, m_ref[...], v_ref[...])
        g_ref[...] = g
        d_ref[...] = delta
        nm_ref[...] = nm
        nv_ref[...] = nv

    spec = pl.BlockSpec((rb, c), lambda h, i: (h * nb + i, 0))
    return pl.pallas_call(
        body, name=name, grid=(2, nb),
        in_specs=[pl.BlockSpec((None, N_CHIPS, rb, c), lambda h, i: (h, 0, i, 0)), spec, spec, spec],
        out_specs=[spec] * 4, out_shape=[_sds((r, c), F32)] * 4, compiler_params=_params(),
    )(terms, w, m, v)


def _mesh_place():
    x, y, c = lax.axis_index("x"), lax.axis_index("y"), lax.axis_index("c")
    chips = [(x, 1 - y), (1 - x, y), (1 - x, 1 - y)]
    return x, y, c, chips


def _all_gather_weights(shards):
    n = len(shards)

    def body(*refs):
        ins, outs = refs[:n], refs[n:2 * n]
        local_sem, ici_send, ici_recv, d2d_send, d2d_recv = refs[2 * n:]
        x, y, c, chips = _mesh_place()
        me = 2 * x + y
        sibling = (x, y, 1 - c)
        local, sent = [], []
        for wi in range(n):
            loc = pltpu.make_async_copy(ins[wi], outs[wi].at[me], local_sem.at[wi])
            loc.start()
            local.append(loc)
            for k, (tx, ty) in enumerate(chips):
                cp = pltpu.make_async_remote_copy(
                    src_ref=ins[wi].at[c], dst_ref=outs[wi].at[me, c],
                    send_sem=ici_send.at[wi * 3 + k], recv_sem=ici_recv.at[wi * 3 + k],
                    device_id=(tx, ty, c), device_id_type=MESH)
                cp.start()
                sent.append(cp)
        passed = []
        for wi in range(n):
            for k, (tx, ty) in enumerate(chips):
                slab = outs[wi].at[2 * tx + ty, c]
                pltpu.make_async_remote_copy(
                    src_ref=slab, dst_ref=slab, send_sem=ici_send.at[wi * 3 + k], recv_sem=ici_recv.at[wi * 3 + k],
                    device_id=(tx, ty, c), device_id_type=MESH).wait_recv()
                fw = pltpu.make_async_remote_copy(
                    src_ref=slab, dst_ref=slab, send_sem=d2d_send.at[wi * 3 + k], recv_sem=d2d_recv.at[wi * 3 + k],
                    device_id=sibling, device_id_type=MESH)
                fw.start()
                passed.append(fw)
        for wi in range(n):
            for k, (tx, ty) in enumerate(chips):
                slab = outs[wi].at[2 * tx + ty, 1 - c]
                pltpu.make_async_remote_copy(
                    src_ref=slab, dst_ref=slab, send_sem=d2d_send.at[wi * 3 + k], recv_sem=d2d_recv.at[wi * 3 + k],
                    device_id=sibling, device_id_type=MESH).wait_recv()
        for loc in local:
            loc.wait()
        for cp in sent + passed:
            cp.wait_send()

    return pl.pallas_call(
        body, name="all_gather_weights",
        in_specs=[ANY] * n, out_specs=[ANY] * n,
        out_shape=[_sds((N_CHIPS,) + s.shape, s.dtype) for s in shards],
        scratch_shapes=[pltpu.SemaphoreType.DMA((n,)), pltpu.SemaphoreType.DMA((3 * n,)),
                        pltpu.SemaphoreType.DMA((3 * n,)), pltpu.SemaphoreType.DMA((3 * n,)),
                        pltpu.SemaphoreType.DMA((3 * n,))],
    )(*shards)


def _run_comms(name, comms):
    plumb = _CommPlumbing(comms, 0, 0, 0)
    n_in, n_out = len(plumb.args), len(plumb.out_shape)

    def body(*refs):
        parts = []
        i0, o0, s0 = 0, n_in, n_in + n_out
        for cm in plumb.comms:
            parts.append((refs[i0:i0 + len(cm.ins)], refs[o0:o0 + len(cm.outs)], refs[s0:s0 + len(cm.sems)]))
            i0 += len(cm.ins)
            o0 += len(cm.outs)
            s0 += len(cm.sems)
        for cm, part in zip(plumb.comms, parts):
            cm.start(*part)
        for cm, part in zip(plumb.comms, parts):
            cm.finish(*part)

    res = pl.pallas_call(
        body, name=name, in_specs=[ANY] * n_in, out_specs=[ANY] * n_out, out_shape=plumb.out_shape,
        scratch_shapes=plumb.scratch, input_output_aliases=plumb.aliases,
    )(*plumb.args)
    plumb.deliver(res)


def _gather_ici(shards):
    n = len(shards)

    def copies(ins, outs, sems):
        local_sem, send_sem, recv_sem = sems
        x, y, c, chips = _mesh_place()
        me = 2 * x + y
        local, sends, recvs = [], [], []
        for wi in range(n):
            local.append(pltpu.make_async_copy(ins[wi], outs[wi].at[me], local_sem.at[wi]))
            for k, (tx, ty) in enumerate(chips):
                sems_k = dict(send_sem=send_sem.at[wi * 3 + k], recv_sem=recv_sem.at[wi * 3 + k],
                              device_id=(tx, ty, c), device_id_type=MESH)
                sends.append(pltpu.make_async_remote_copy(
                    src_ref=ins[wi].at[c], dst_ref=outs[wi].at[me, c], **sems_k))
                slab = outs[wi].at[2 * tx + ty, c]
                recvs.append(pltpu.make_async_remote_copy(src_ref=slab, dst_ref=slab, **sems_k))
        return local, sends, recvs

    def start(ins, outs, sems):
        local, sends, _ = copies(ins, outs, sems)
        for cp in local + sends:
            cp.start()

    def finish(ins, outs, sems):
        local, sends, recvs = copies(ins, outs, sems)
        for cp in local:
            cp.wait()
        for cp in recvs:
            cp.wait_recv()
        for cp in sends:
            cp.wait_send()

    return _Comm(shards, [_sds((N_CHIPS,) + s.shape, s.dtype) for s in shards], {},
                 [pltpu.SemaphoreType.DMA((n,)), pltpu.SemaphoreType.DMA((3 * n,)), pltpu.SemaphoreType.DMA((3 * n,))],
                 start, finish)


def _gather_d2d(gathered):
    n = len(gathered)

    def copies(outs, sems):
        send_sem, recv_sem = sems
        x, y, c, chips = _mesh_place()
        sends, recvs = [], []
        for wi in range(n):
            for k, (tx, ty) in enumerate(chips):
                sems_k = dict(send_sem=send_sem.at[wi * 3 + k], recv_sem=recv_sem.at[wi * 3 + k],
                              device_id=(x, y, 1 - c), device_id_type=MESH)
                mine = outs[wi].at[2 * tx + ty, c]
                theirs = outs[wi].at[2 * tx + ty, 1 - c]
                sends.append(pltpu.make_async_remote_copy(src_ref=mine, dst_ref=mine, **sems_k))
                recvs.append(pltpu.make_async_remote_copy(src_ref=theirs, dst_ref=theirs, **sems_k))
        return sends, recvs

    def start(ins, outs, sems):
        for cp in copies(outs, sems)[0]:
            cp.start()

    def finish(ins, outs, sems):
        sends, recvs = copies(outs, sems)
        for cp in recvs:
            cp.wait_recv()
        for cp in sends:
            cp.wait_send()

    return _Comm(gathered, [_sds(g.shape, g.dtype) for g in gathered], {i: i for i in range(n)},
                 [pltpu.SemaphoreType.DMA((3 * n,)), pltpu.SemaphoreType.DMA((3 * n,))], start, finish)


def _exchange_halves(grads):
    n = len(grads)

    def copies(ins, outs, sems):
        send_sem, recv_sem = sems
        x, y, c, _ = _mesh_place()
        return [pltpu.make_async_remote_copy(
            src_ref=ins[wi].at[t, 1 - c], dst_ref=outs[wi].at[t],
            send_sem=send_sem.at[wi * N_CHIPS + t], recv_sem=recv_sem.at[wi * N_CHIPS + t],
            device_id=(x, y, 1 - c), device_id_type=MESH) for wi in range(n) for t in range(N_CHIPS)]

    def start(ins, outs, sems):
        for cp in copies(ins, outs, sems):
            cp.start()

    def finish(ins, outs, sems):
        for cp in copies(ins, outs, sems):
            cp.wait()

    return _Comm(grads, [_sds((N_CHIPS,) + g.shape[2:], g.dtype) for g in grads], {},
                 [pltpu.SemaphoreType.DMA((N_CHIPS * n,)), pltpu.SemaphoreType.DMA((N_CHIPS * n,))], start, finish)


def _scatter_ici(sums):
    n = len(sums)

    def copies(ins, outs, sems):
        local_sem, send_sem, recv_sem = sems
        x, y, c, chips = _mesh_place()
        me = 2 * x + y
        local, sends, recvs = [], [], []
        for wi in range(n):
            local.append(pltpu.make_async_copy(ins[wi].at[me], outs[wi].at[c, 0], local_sem.at[wi]))
            for k, (tx, ty) in enumerate(chips):
                sems_k = dict(send_sem=send_sem.at[wi * 3 + k], recv_sem=recv_sem.at[wi * 3 + k],
                              device_id=(tx, ty, c), device_id_type=MESH)
                land = outs[wi].at[c, k + 1]
                sends.append(pltpu.make_async_remote_copy(src_ref=ins[wi].at[2 * tx + ty], dst_ref=land, **sems_k))
                recvs.append(pltpu.make_async_remote_copy(src_ref=land, dst_ref=land, **sems_k))
        return local, sends, recvs

    def start(ins, outs, sems):
        local, sends, _ = copies(ins, outs, sems)
        for cp in local + sends:
            cp.start()

    def finish(ins, outs, sems):
        local, sends, recvs = copies(ins, outs, sems)
        for cp in local:
            cp.wait()
        for cp in recvs:
            cp.wait_recv()
        for cp in sends:
            cp.wait_send()

    return _Comm(sums, [_sds((2, N_CHIPS) + s.shape[1:], s.dtype) for s in sums], {},
                 [pltpu.SemaphoreType.DMA((n,)), pltpu.SemaphoreType.DMA((3 * n,)), pltpu.SemaphoreType.DMA((3 * n,))],
                 start, finish)


def _scatter_d2d(terms):
    n = len(terms)

    def copies(outs, sems):
        send_sem, recv_sem = sems
        x, y, c, _ = _mesh_place()
        sends, recvs = [], []
        for wi in range(n):
            sems_w = dict(send_sem=send_sem.at[wi], recv_sem=recv_sem.at[wi],
                          device_id=(x, y, 1 - c), device_id_type=MESH)
            sends.append(pltpu.make_async_remote_copy(src_ref=outs[wi].at[c], dst_ref=outs[wi].at[c], **sems_w))
            recvs.append(pltpu.make_async_remote_copy(src_ref=outs[wi].at[1 - c], dst_ref=outs[wi].at[1 - c], **sems_w))
        return sends, recvs

    def start(ins, outs, sems):
        for cp in copies(outs, sems)[0]:
            cp.start()

    def finish(ins, outs, sems):
        sends, recvs = copies(outs, sems)
        for cp in recvs:
            cp.wait_recv()
        for cp in sends:
            cp.wait_send()

    return _Comm(terms, [_sds(t.shape, t.dtype) for t in terms], {i: i for i in range(n)},
                 [pltpu.SemaphoreType.DMA((n,)), pltpu.SemaphoreType.DMA((n,))], start, finish)


def _chip_sum(name, grad, got, core):
    _, _, hr, c = grad.shape
    rb = _pick(hr, max(16, (1 << 19) // c), 16)

    def body(core_ref, a_ref, b_ref, o_ref):
        o_ref[...] = (a_ref[...].astype(F32) + b_ref[...].astype(F32)).astype(BF16)

    out_spec = pl.BlockSpec((None, rb, c), lambda t, i, core_ref: (t, i, 0))
    return pl.pallas_call(
        body, name=name,
        grid_spec=pltpu.PrefetchScalarGridSpec(
            num_scalar_prefetch=1, grid=(N_CHIPS, hr // rb),
            in_specs=[pl.BlockSpec((None, None, rb, c), lambda t, i, core_ref: (t, core_ref[0], i, 0)), out_spec],
            out_specs=out_spec),
        out_shape=_sds((N_CHIPS, hr, c), BF16), compiler_params=_params(),
    )(core, grad, got)


def _all_reduce_small(pack):
    r = pack.shape[0]

    def body(p_ref, o_ref, land_ref, send_sem, recv_sem):
        x, y, c, _ = _mesh_place()
        me = 4 * x + 2 * y + c
        flips = [(k >> 2 & 1, k >> 1 & 1, k & 1) for k in range(1, N_DEV)]

        def peer(fx, fy, fc):
            return (1 - x if fx else x, 1 - y if fy else y, 1 - c if fc else c)

        land_ref[me] = p_ref[...]
        sent = []
        for k, flip in enumerate(flips):
            cp = pltpu.make_async_remote_copy(
                src_ref=p_ref, dst_ref=land_ref.at[me], send_sem=send_sem.at[k], recv_sem=recv_sem.at[k],
                device_id=peer(*flip), device_id_type=MESH)
            cp.start()
            sent.append(cp)
        for k, flip in enumerate(flips):
            px, py, pc = peer(*flip)
            slot = land_ref.at[4 * px + 2 * py + pc]
            pltpu.make_async_remote_copy(
                src_ref=slot, dst_ref=slot, send_sem=send_sem.at[k], recv_sem=recv_sem.at[k],
                device_id=(px, py, pc), device_id_type=MESH).wait_recv()
        total = land_ref[0]
        for d in range(1, N_DEV):
            total = total + land_ref[d]
        o_ref[...] = total
        for cp in sent:
            cp.wait_send()

    vmem = pl.BlockSpec(memory_space=pltpu.VMEM)
    return pl.pallas_call(
        body, name="all_reduce_small", in_specs=[vmem], out_specs=vmem, out_shape=_sds((r, 128), F32),
        scratch_shapes=[pltpu.VMEM((N_DEV, r, 128), F32), pltpu.SemaphoreType.DMA((N_DEV - 1,)),
                        pltpu.SemaphoreType.DMA((N_DEV - 1,))],
    )(pack)


PACK_TILE = 8 * 128


def _pack(items):
    rows, i = [], 0
    while i < len(items):
        j = i
        while j < len(items) and items[j].size == items[i].size:
            j += 1
        group = jnp.stack([it.reshape(-1).astype(F32) for it in items[i:j]])
        rows.append(jnp.pad(group, ((0, 0), (0, -group.shape[1] % PACK_TILE))).reshape(-1, 128))
        i = j
    return jnp.concatenate(rows, axis=0)


def _unpack(pack, shapes):
    out, row = [], 0
    for shp in shapes:
        size = int(np.prod(shp))
        nrow = -(-size // PACK_TILE) * (PACK_TILE // 128)
        out.append(pack[row:row + nrow].reshape(-1)[:size].reshape(shp))
        row += nrow
    return out


BIG = ["ffn1_w_gu", "ffn1_w_down", "w_in", "w_gate", "w_proj_a", "w_proj_b", "w_out",
       "ffn2_w_gu", "ffn2_w_down", "w_ple_gate", "w_ple_proj"]
SMALL = ["ffn1_norm", "mix_norm", "ffn2_norm", "ple_norm", "a_q_norm", "a_k_norm", "b_q_norm", "b_k_norm",
         "a_rel_bias", "b_sinks"]
WEIGHTS = ["ffn1_norm", "ffn1_w_gu", "ffn1_w_down", "mix_norm", "w_in", "a_q_norm", "a_k_norm", "a_rel_bias",
           "b_q_norm", "b_k_norm", "b_sinks", "w_gate", "w_proj_a", "w_proj_b", "w_out", "ffn2_norm",
           "ffn2_w_gu", "ffn2_w_down", "ple_norm", "w_ple_gate", "w_ple_proj"]
ATTN_A = dict(prev=A_PREV_CHUNKS * CHUNK, group=1, kw=A_WIDTH, qblk=0, kblk=1, vblk=2)
ATTN_B = dict(prev=B_PREV_CHUNKS * CHUNK, group=N_HEADS // B_KV_HEADS, kw=B_KV_WIDTH, qblk=3,
              kblk=4 * A_WIDTH // B_KV_WIDTH, vblk=4 * A_WIDTH // B_KV_WIDTH + 1)


def _cast_epilogue(accs, extras, outs, ij):
    for acc, out in zip(accs, outs):
        out[...] = acc.astype(out.dtype)


GATHER_FIRST = ["ffn1_w_gu", "ffn1_w_down"]
GATHER_LATE = ["ffn2_w_gu", "ffn2_w_down", "w_ple_gate", "w_ple_proj"]
ROW_SHARDED = ("ffn1_w_down", "ffn2_w_down", "w_out", "w_ple_gate")


def _slotted(name, grad):
    if name == "w_in":
        rows, cols = grad.shape
        grad = jnp.transpose(grad.reshape(rows, N_CHIPS, cols // N_CHIPS), (1, 0, 2))
    elif name in ROW_SHARDED:
        grad = grad.reshape(N_CHIPS, grad.shape[0] // N_CHIPS, grad.shape[1])
    return grad.reshape(N_CHIPS, 2, grad.shape[1] // 2, grad.shape[2])


def _local_step(xt, pt, tgt, n_batch, shards, small, core):
    t, d = xt.shape
    tm = _pick(t, 512, 8)
    tk = _pick(t, 512, 8)
    nt = t // tm
    row = pl.BlockSpec((tm, d), lambda i, j, k: (i, 0))
    gs = shards["w_gate"].shape[1]
    ps = shards["w_proj_a"].shape[1]
    es = shards["w_ple_proj"].shape[1]
    pdim = pt.shape[1]
    ncols = N_CHIPS * shards["w_in"].shape[1]
    tin = ncols // 2
    assert 2 * gs == d and 4 * ps == d and 4 * es == d and tin % 128 == 0

    w = {}
    halves = {n: s.reshape(2, s.shape[0] // 2, s.shape[1]) for n, s in shards.items()}

    def publish(names, arrays):
        for name, g in zip(names, arrays):
            g = g.reshape(N_CHIPS, 2 * g.shape[2], g.shape[3])
            if name in ROW_SHARDED:
                g = g.reshape(N_CHIPS * g.shape[1], g.shape[2])
            elif name == "w_in":
                g = jnp.transpose(g, (1, 0, 2)).reshape(g.shape[1], N_CHIPS * g.shape[2])
            w[name] = g

    class GatherPipe:
        def __init__(self, names):
            self.names = names

        def ici(self):
            self.first = _gather_ici([halves[n] for n in self.names])
            return self.first

        def d2d(self):
            self.second = _gather_d2d(self.first.results)
            return self.second

        def publish(self):
            publish(self.names, self.second.results)

    class GradPipe:
        def __init__(self, names):
            self.names = names

        def exchange(self, grads):
            self.grads = [_slotted(n, g) for n, g in zip(self.names, grads)]
            self.x = _exchange_halves(self.grads)
            return self.x

        def scatter(self):
            sums = [_chip_sum("chip_sum_" + n, g, got, core)
                    for n, g, got in zip(self.names, self.grads, self.x.results)]
            self.s = _scatter_ici(sums)
            return self.s

        def forward(self):
            self.f = _scatter_d2d(self.s.results)
            return self.f

        def terms(self):
            return dict(zip(self.names, self.f.results))

    publish(GATHER_FIRST, _all_gather_weights([halves[n] for n in GATHER_FIRST]))
    g_in, g_proj, g_ple = GatherPipe(["w_in", "w_gate"]), GatherPipe(["w_proj_a", "w_proj_b", "w_out"]), \
        GatherPipe(["w_ple_gate", "w_ple_proj"])
    g_down2, g_up2 = GatherPipe(["ffn2_w_down"]), GatherPipe(["ffn2_w_gu"])
    h1, ffn1_saved = _ffn_fwd("ffn1", xt, small["ffn1_norm"], w["ffn1_w_gu"], w["ffn1_w_down"],
                              {"up": lambda: [g_in.ici()], "down": lambda: [g_in.d2d(), g_proj.ici()]})
    g_in.publish()
    w_in, wgate = w["w_in"], w["w_gate"]
    un = _rms_fwd("mix_norm", h1, small["mix_norm"])
    (qkv,) = _mm(
        "qkv", "nn", (nt, 2, 1),
        [(un, row, w_in, pl.BlockSpec((d, tin), lambda i, j, k: (0, j)))], [],
        [(_sds((t, ncols), BF16), pl.BlockSpec((tm, tin), lambda i, j, k: (i, j)))], (tm, tin), _cast_epilogue,
        j_outer=True, comms=[g_proj.d2d(), g_ple.ici()])
    g_proj.publish()
    wpa, wpb, wout = w["w_proj_a"], w["w_proj_b"], w["w_out"]

    def gate_epilogue(accs, extras, outs, ij):
        outs[0][...] = jax.nn.sigmoid(accs[0]).astype(BF16)

    (gates,) = _mm(
        "gate", "nn", (nt, 4, 1),
        [(un, row, wgate, pl.BlockSpec((None, d, gs), lambda i, j, k: (j, 0, 0)))], [],
        [(_sds((2, t, d), BF16), pl.BlockSpec((None, tm, gs), lambda i, j, k: (j // 2, i, j % 2)))],
        (tm, gs), gate_epilogue, j_outer=True, chunked=True, comms=[g_ple.d2d(), g_down2.ici()])
    g_ple.publish()
    wpg, wpe = w["w_ple_gate"], w["w_ple_proj"]

    bias_a = _bias_a(small["a_rel_bias"][0])
    bias_b = _bias_b()
    sink_a = jnp.full((N_HEADS, 128), NEG_INF, F32)
    sink_b = jnp.broadcast_to(small["b_sinks"][0][:, None], (N_HEADS, 128))
    gqa, gka, gqb, gkb = [jnp.tile(small[k], (1, 2)) for k in ("a_q_norm", "a_k_norm", "b_q_norm", "b_k_norm")]
    ya, lse_a = _attn_fwd("attn_a_fwd", qkv, bias_a, sink_a, gqa, gka, ATTN_A, n_batch,
                          comms=[g_down2.d2d(), g_up2.ici()])
    g_down2.publish()
    yb, lse_b = _attn_fwd("attn_b_fwd", qkv, bias_b, sink_b, gqb, gkb, ATTN_B, n_batch, comms=[g_up2.d2d()])
    g_up2.publish()

    def merge_epilogue(accs, extras, outs, ij):
        pa, pb = accs
        outs[0][...] = (extras[0][...].astype(F32) * pa + extras[1][...].astype(F32) * pb).astype(BF16)
        outs[1][...] = pa.astype(BF16)
        outs[2][...] = pb.astype(BF16)

    y_spec = pl.BlockSpec((tm, A_WIDTH), lambda i, j, k: (i, 0))
    proj_spec = pl.BlockSpec((None, A_WIDTH, ps), lambda i, j, k: (j, 0, 0))
    tile_ps = pl.BlockSpec((tm, ps), lambda i, j, k: (i, j))
    merged, pa, pb = _mm(
        "proj_merge", "nn", (nt, 4, 1),
        [(ya, y_spec, wpa, proj_spec), (yb, y_spec, wpb, proj_spec)],
        [(gates, pl.BlockSpec((None, tm, ps), lambda i, j, k: (0, i, j))),
         (gates, pl.BlockSpec((None, tm, ps), lambda i, j, k: (1, i, j)))],
        [(_sds((t, d), BF16), tile_ps)] * 3, (tm, ps), merge_epilogue)

    def residual_epilogue(accs, extras, outs, ij):
        outs[0][...] = extras[0][...] + accs[0]

    (h2,) = _mm(
        "out_proj", "nn", (nt, 1, 1),
        [(merged, row, wout, pl.BlockSpec((d, d), lambda i, j, k: (0, 0)))],
        [(h1, row)], [(_sds((t, d), F32), row)], (tm, d), residual_epilogue)

    h3, ffn2_saved = _ffn_fwd("ffn2", h2, small["ffn2_norm"], w["ffn2_w_gu"], w["ffn2_w_down"], {})
    n3 = _rms_fwd("ple_norm", h3, small["ple_norm"])
    tile_es = pl.BlockSpec((tm, es), lambda i, j, k: (i, j))
    (pe,) = _mm(
        "ple_embed", "nn", (nt, 4, 1),
        [(pt, pl.BlockSpec((tm, pdim), lambda i, j, k: (i, 0)), wpe, pl.BlockSpec((None, pdim, es), lambda i, j, k: (j, 0, 0)))],
        [], [(_sds((t, d), F32), tile_es)], (tm, es), _cast_epilogue)

    th = _pick(d, 512)

    def head_epilogue(accs, extras, outs, ij):
        h3_ref, pe_ref, tgt_ref = extras
        dy_ref, dpe_ref, dz_ref, loss_ref = outs
        pg = jax.nn.sigmoid(accs[0])
        pev = pe_ref[...]
        diff = h3_ref[...] + pg * pev - tgt_ref[...]
        dy = diff * (1.0 / d)
        dy_ref[...] = dy
        dpe_ref[...] = (dy * pg).astype(BF16)
        dz_ref[...] = (dy * pev * pg * (1.0 - pg)).astype(BF16)
        _accumulate(loss_ref, jnp.full(loss_ref.shape, jnp.sum(diff * diff), F32), (ij[0] == 0) & (ij[1] == 0))

    tile_h = pl.BlockSpec((tm, th), lambda i, j, k: (i, j))
    dy, dpe, dz, loss_acc = _mm(
        "ple_gate_loss", "nn", (nt, d // th, 1),
        [(n3, row, wpg, pl.BlockSpec((d, th), lambda i, j, k: (0, j)))],
        [(h3, tile_h), (pe, tile_h), (tgt, tile_h)],
        [(_sds((t, d), F32), tile_h), (_sds((t, d), BF16), tile_h), (_sds((t, d), BF16), tile_h),
         (_sds((8, 128), F32), pl.BlockSpec((8, 128), lambda i, j, k: (0, 0)))],
        (tm, th), head_epilogue, j_outer=True, chunked=True)
    loss = 0.5 * loss_acc[0, 0] / d

    nk = t // tk
    (dwpe,) = _mm(
        "d_w_ple_proj", "tn", (1, 4, nk),
        [(pt, pl.BlockSpec((tk, pdim), lambda i, j, k: (k, 0)), dpe, pl.BlockSpec((tk, es), lambda i, j, k: (k, j)))],
        [], [(_sds((4, pdim, es), BF16), pl.BlockSpec((None, pdim, es), lambda i, j, k: (j, 0, 0)))],
        (pdim, es), _cast_epilogue)

    def dense_grad(name, a, dyb):
        (res,) = _mm(
            name, "tn", (1, d // th, nk),
            [(a, pl.BlockSpec((tk, d), lambda i, j, k: (k, 0)), dyb, pl.BlockSpec((tk, th), lambda i, j, k: (k, j)))],
            [], [(_sds((d, d), BF16), pl.BlockSpec((d, th), lambda i, j, k: (0, j)))], (d, th), _cast_epilogue)
        return res

    dwpg = dense_grad("d_w_ple_gate", n3, dz)
    tmn = _pick(t, 1024, 8)
    extras, outs = _rms_bwd_io(h3, small["ple_norm"], dy, tmn)
    dh3, dh3_b, d_ple_norm = _mm(
        "d_ple_norm", "nt", (t // tmn, 1, 1),
        [(dz, pl.BlockSpec((tmn, d), lambda i, j, k: (i, 0)), wpg, pl.BlockSpec((d, d), lambda i, j, k: (0, 0)))],
        extras, outs, (tmn, d), _rms_bwd_epilogue)

    late = GradPipe(GATHER_LATE)
    proj = GradPipe(["w_proj_a", "w_proj_b", "w_out"])
    dh2, dh2_b, d_ffn2_norm, dwgu2, dwd2 = _ffn_bwd(
        "ffn2", dh3, dh3_b, h2, small["ffn2_norm"], w["ffn2_w_gu"], w["ffn2_w_down"], ffn2_saved,
        {"dnorm": lambda dwgu, dwd: [late.exchange([dwgu, dwd, dwpg, dwpe])]})

    def dmerge_epilogue(accs, extras, outs, ij):
        dmo = accs[0]
        g_ref, pa_ref, pb_ref = extras
        dg_ref, dpa_ref, dpb_ref = outs
        ga = g_ref[0].astype(F32)
        gb = g_ref[1].astype(F32)
        dg_ref[0] = (dmo * pa_ref[...].astype(F32) * ga * (1.0 - ga)).astype(BF16)
        dg_ref[1] = (dmo * pb_ref[...].astype(F32) * gb * (1.0 - gb)).astype(BF16)
        dpa_ref[...] = (dmo * ga).astype(BF16)
        dpb_ref[...] = (dmo * gb).astype(BF16)

    g_spec = pl.BlockSpec((2, tm, th), lambda i, j, k: (0, i, j))
    dgates, dpa, dpb = _mm(
        "d_merge", "nt", (nt, d // th, 1),
        [(dh2_b, row, wout, pl.BlockSpec((th, d), lambda i, j, k: (j, 0)))],
        [(gates, g_spec), (pa, tile_h), (pb, tile_h)],
        [(_sds((2, t, d), BF16), g_spec), (_sds((t, d), BF16), tile_h), (_sds((t, d), BF16), tile_h)],
        (tm, th), dmerge_epilogue, j_outer=True, chunked=True)
    dwout = dense_grad("d_w_out", merged, dh2_b)

    yk_spec = pl.BlockSpec((tk, A_WIDTH), lambda i, j, k: (k, 0))
    dk_spec = pl.BlockSpec((tk, ps), lambda i, j, k: (k, j))
    dproj = (_sds((4, A_WIDTH, ps), BF16), proj_spec)
    dwpa, dwpb = _mm(
        "d_w_proj", "tn", (1, 4, nk),
        [(ya, yk_spec, dpa, dk_spec), (yb, yk_spec, dpb, dk_spec)], [], [dproj, dproj], (A_WIDTH, ps), _cast_epilogue)
    dproj_a = pl.BlockSpec((tm, ps), lambda i, j, k: (i, k))
    wproj_k = pl.BlockSpec((None, A_WIDTH, ps), lambda i, j, k: (k, 0, 0))
    dya, dyb = _mm(
        "d_attn_out", "nt", (nt, 1, 4),
        [(dpa, dproj_a, wpa, wproj_k), (dpb, dproj_a, wpb, wproj_k)], [],
        [(_sds((t, A_WIDTH), BF16), y_spec)] * 2, (tm, A_WIDTH), _cast_epilogue,
        comms=[proj.exchange([dwpa, dwpb, dwout])])

    dqa, dka, dva, dbias_a, _, dgqa, dgka = _attn_bwd(
        "attn_a_bwd", qkv, bias_a, sink_a, gqa, gka, ya, dya, lse_a, ATTN_A, n_batch, True,
        comms=[late.scatter(), proj.scatter()])
    dqb, dkb, dvb, _, dsink_b, dgqb, dgkb = _attn_bwd(
        "attn_b_bwd", qkv, bias_b, sink_b, gqb, gkb, yb, dyb, lse_b, ATTN_B, n_batch, False,
        comms=[late.forward(), proj.forward()])
    dqkv = jnp.concatenate([dqa, dka, dva, dqb, dkb, dvb], axis=1)

    (dwgate,) = _mm(
        "d_w_gate", "tn", (1, 4, nk),
        [(un, pl.BlockSpec((tk, d), lambda i, j, k: (k, 0)),
          dgates, pl.BlockSpec((None, tk, gs), lambda i, j, k: (j // 2, k, j % 2)))],
        [], [(_sds((4, d, gs), BF16), pl.BlockSpec((None, d, gs), lambda i, j, k: (j, 0, 0)))], (d, gs), _cast_epilogue)
    (dwin,) = _mm(
        "d_w_in", "tn", (1, 2, nk),
        [(un, pl.BlockSpec((tk, d), lambda i, j, k: (k, 0)), dqkv, pl.BlockSpec((tk, tin), lambda i, j, k: (k, j)))],
        [], [(_sds((d, ncols), BF16), pl.BlockSpec((d, tin), lambda i, j, k: (0, j)))], (d, tin), _cast_epilogue)

    mixer = GradPipe(["w_in", "w_gate"])
    extras, outs = _rms_bwd_io(h1, small["mix_norm"], dh2, tmn)
    dh1, dh1_b, d_mix_norm = _mm(
        "d_mix_norm", "nt", (t // tmn, 1, 6),
        [(dgates, pl.BlockSpec((None, tmn, gs), lambda i, j, k: (jnp.minimum(k, 3) // 2, i, jnp.minimum(k, 3) % 2)),
          wgate, pl.BlockSpec((None, d, gs), lambda i, j, k: (jnp.minimum(k, 3), 0, 0))),
         (dqkv, pl.BlockSpec((tmn, tin), lambda i, j, k: (i, jnp.maximum(k - 4, 0))),
          w_in, pl.BlockSpec((d, tin), lambda i, j, k: (0, jnp.maximum(k - 4, 0))))],
        extras, outs, (tmn, d), _rms_bwd_epilogue, steps=[4, 2],
        comms=[mixer.exchange([dwin, dwgate])])

    up1 = GradPipe(["ffn1_w_gu"])
    down1 = GradPipe(["ffn1_w_down"])
    dx, _, d_ffn1_norm, _, _ = _ffn_bwd(
        "ffn1", dh1, dh1_b, xt, small["ffn1_norm"], w["ffn1_w_gu"], w["ffn1_w_down"], ffn1_saved,
        {"dact": lambda: [mixer.scatter()],
         "dwgu": lambda: [mixer.forward()],
         "dwd": lambda dwgu: [up1.exchange([dwgu])],
         "dnorm": lambda dwgu, dwd: [up1.scatter(), down1.exchange([dwd])]})
    _run_comms("grad_tail_scatter", [up1.forward(), down1.scatter()])
    _run_comms("grad_tail_forward", [down1.forward()])
    terms = {**late.terms(), **proj.terms(), **mixer.terms(), **up1.terms(), **down1.terms()}

    def fold(v):
        return v[0, :HEAD_DIM] + v[0, HEAD_DIM:]

    small_grads = {"ffn1_norm": d_ffn1_norm, "mix_norm": d_mix_norm, "ffn2_norm": d_ffn2_norm,
                   "ple_norm": d_ple_norm, "a_q_norm": fold(dgqa), "a_k_norm": fold(dgka),
                   "b_q_norm": fold(dgqb), "b_k_norm": fold(dgkb), "a_rel_bias": _rel_bias_grad(dbias_a),
                   "b_sinks": jnp.sum(dsink_b, axis=1)}
    return loss, dx, terms, small_grads


def kernel(x, p, ffn1_norm, ffn1_w_gu, ffn1_w_down, mix_norm, w_in, a_q_norm, a_k_norm, a_rel_bias, b_q_norm, b_k_norm, b_sinks, w_gate, w_proj_a, w_proj_b, w_out, ffn2_norm, ffn2_w_gu, ffn2_w_down, ple_norm, w_ple_gate, w_ple_proj, loss_target, m_ffn1_norm, m_ffn1_w_gu, m_ffn1_w_down, m_mix_norm, m_w_in, m_a_q_norm, m_a_k_norm, m_a_rel_bias, m_b_q_norm, m_b_k_norm, m_b_sinks, m_w_gate, m_w_proj_a, m_w_proj_b, m_w_out, m_ffn2_norm, m_ffn2_w_gu, m_ffn2_w_down, m_ple_norm, m_w_ple_gate, m_w_ple_proj, v_ffn1_norm, v_ffn1_w_gu, v_ffn1_w_down, v_mix_norm, v_w_in, v_a_q_norm, v_a_k_norm, v_a_rel_bias, v_b_q_norm, v_b_k_norm, v_b_sinks, v_w_gate, v_w_proj_a, v_w_proj_b, v_w_out, v_ffn2_norm, v_ffn2_w_gu, v_ffn2_w_down, v_ple_norm, v_w_ple_gate, v_w_ple_proj):
    given = dict(locals())
    n_batch, s, d = x.shape
    t = n_batch * s
    xt = x.reshape(t, d)
    pt = p.reshape(t, p.shape[-1])
    tgt = loss_target.reshape(t, d)

    shards = {}
    for name in BIG:
        (shards[name],) = _ew("cast_" + name, lambda v: (v,), [given[name][0]], [BF16])
    small = {name: given[name] for name in SMALL}
    core = lax.axis_index("c").astype(jnp.int32).reshape(1)
    loss, dx, terms, small_grads = _local_step(xt, pt, tgt, n_batch, shards, small, core)

    grads, deltas, new_m, new_v = {}, {}, {}, {}
    for name in BIG:
        gw, dl, nm, nv = _adamw_terms("adamw_" + name, terms[name], given[name][0], given["m_" + name][0],
                                      given["v_" + name][0])
        grads[name], deltas[name], new_m[name], new_v[name] = gw[None], dl[None], nm[None], nv[None]

    small_shapes = [given[name].shape for name in SMALL] + [()]
    g_pack = _all_reduce_small(_pack([small_grads[name] for name in SMALL] + [loss]))
    zero = jnp.zeros((), F32)
    w_pack = _pack([given[name] for name in SMALL] + [zero])
    m_pack = _pack([given["m_" + name] for name in SMALL] + [zero])
    v_pack = _pack([given["v_" + name] for name in SMALL] + [zero])
    d_pack, nm_pack, nv_pack = _ew("adamw_small", lambda wv, gv, mv, vv: _adamw_math(wv, gv, mv, vv),
                                   [w_pack, g_pack, m_pack, v_pack], [F32] * 3)
    g_small = _unpack(g_pack, small_shapes)
    loss_total = g_small[-1]
    for name, gv, dv, mv, vv in zip(SMALL, g_small, _unpack(d_pack, small_shapes), _unpack(nm_pack, small_shapes),
                                    _unpack(nv_pack, small_shapes)):
        grads[name], deltas[name], new_m[name], new_v[name] = gv, dv, mv, vv

    return (loss_total, dx.reshape(x.shape), *[grads[n] for n in WEIGHTS], *[deltas[n] for n in WEIGHTS],
            *[new_m[n] for n in WEIGHTS], *[new_v[n] for n in WEIGHTS])
```

```python
import functools

import numpy as np
import jax
import jax.numpy as jnp
from jax import lax
from jax.experimental import pallas as pl
from jax.experimental.pallas import tpu as pltpu

F32 = jnp.float32
BF16 = jnp.bfloat16

CHUNK = 64
HEAD_DIM = 64
A_PREV_CHUNKS = 8
A_MAX_REL = 128
N_HEADS = 8
B_KV_HEADS = 2
B_PREV_CHUNKS = 2
A_WIDTH = N_HEADS * HEAD_DIM
B_KV_WIDTH = B_KV_HEADS * HEAD_DIM
EPS = 1e-6
NEG_INF = -1e30
ATTN_SCALE = HEAD_DIM ** -0.5
Q_BLOCK = 128
PAIR = 2 * HEAD_DIM

ADAM_LR = 0.001
ADAM_B1 = 0.9
ADAM_B2 = 0.999
ADAM_EPS = 1e-08
ADAM_WD = 0.01
ADAM_STEP = 10

N_CHIPS = 4
N_DEV = 8
VMEM_LIMIT_V7X = 56 * 1024 * 1024
MESH = pl.DeviceIdType.MESH
ANY = pl.BlockSpec(memory_space=pl.ANY)

_DN = {
    "nn": (((1,), (0,)), ((), ())),
    "nt": (((1,), (1,)), ((), ())),
    "tn": (((0,), (0,)), ((), ())),
}


def _pick(n, target, mult=128):
    best = None
    for d in range(mult, min(n, target) + 1, mult):
        if n % d == 0:
            best = d
    return n if best is None else best


def _dot(a, b, mode):
    return lax.dot_general(a.astype(BF16), b.astype(BF16), _DN[mode], preferred_element_type=F32)


def _params():
    return pltpu.CompilerParams(vmem_limit_bytes=VMEM_LIMIT_V7X)


class _Comm:
    def __init__(self, ins, outs, aliases, sems, start, finish):
        self.ins, self.outs, self.aliases, self.sems = list(ins), list(outs), dict(aliases), list(sems)
        self.start, self.finish = start, finish
        self.results = None


class _CommPlumbing:
    def __init__(self, comms, n_in, n_out, n_scratch):
        self.comms = list(comms)
        self.n_in, self.n_out, self.n_scratch = n_in, n_out, n_scratch
        self.args = [a for cm in self.comms for a in cm.ins]
        self.out_shape = [o for cm in self.comms for o in cm.outs]
        self.scratch = [s for cm in self.comms for s in cm.sems]
        self.aliases = {}
        i0, o0 = n_in, n_out
        for cm in self.comms:
            for a, b in cm.aliases.items():
                self.aliases[i0 + a] = o0 + b
            i0 += len(cm.ins)
            o0 += len(cm.outs)

    def run(self, in_refs, out_refs, scratch_refs, first, last):
        if not self.comms:
            return
        parts = []
        i0, o0, s0 = self.n_in, self.n_out, self.n_scratch
        for cm in self.comms:
            parts.append((in_refs[i0:i0 + len(cm.ins)], out_refs[o0:o0 + len(cm.outs)],
                          scratch_refs[s0:s0 + len(cm.sems)]))
            i0 += len(cm.ins)
            o0 += len(cm.outs)
            s0 += len(cm.sems)

        @pl.when(first)
        def _():
            for cm, part in zip(self.comms, parts):
                cm.start(*part)

        @pl.when(last)
        def _():
            for cm, part in zip(self.comms, parts):
                cm.finish(*part)

    def deliver(self, results):
        o0 = self.n_out
        for cm in self.comms:
            cm.results = list(results[o0:o0 + len(cm.outs)])
            o0 += len(cm.outs)
        return list(results[:self.n_out])


def _swap_ij(spec):
    index_map = spec.index_map
    return pl.BlockSpec(spec.block_shape, lambda j, i, k: index_map(i, j, k))


MXU_COLUMNS_V7X = 256


def _mm(name, mode, grid, pairs, extras, outs, acc_shape, epilogue, steps=None, comms=(), j_outer=False,
        chunked=False):
    ni, nj, nk = grid
    n_in = 2 * len(pairs) + len(extras)
    n_out = len(outs)
    tn = acc_shape[1]
    col_chunks = None
    if chunked:
        assert nk == 1 and steps is None and mode in ("nn", "nt")
        col_chunks = [(c0, min(MXU_COLUMNS_V7X, tn - c0)) for c0 in range(0, tn, MXU_COLUMNS_V7X)]
    n_acc = 0 if chunked else (len(pairs) if steps is None else 1)
    plumb = _CommPlumbing(comms, n_in, n_out, n_acc)
    n_all_in = n_in + len(plumb.args)
    n_all_out = n_out + len(plumb.out_shape)
    if j_outer:
        grid = (nj, ni, nk)
        pairs = [(a, _swap_ij(a_spec), b, _swap_ij(b_spec)) for a, a_spec, b, b_spec in pairs]
        extras = [(e, _swap_ij(e_spec)) for e, e_spec in extras]
        outs = [(o, _swap_ij(o_spec)) for o, o_spec in outs]

    def body(*refs):
        in_refs = refs[:n_all_in]
        out_refs = refs[n_all_in:n_all_in + n_all_out]
        scratch = refs[n_all_in + n_all_out:]
        accs = scratch[:n_acc]
        i = pl.program_id(1 if j_outer else 0)
        j = pl.program_id(0 if j_outer else 1)
        k = pl.program_id(2)

        def contrib(p, acc):
            acc[...] += _dot(in_refs[2 * p][...], in_refs[2 * p + 1][...], mode)

        if col_chunks:
            def cols(ref, c0, cs):
                if ref.shape[-1] != tn:
                    return ref
                return ref.at[(slice(None),) * (len(ref.shape) - 1) + (pl.ds(c0, cs),)]

            lhs = [in_refs[2 * p][...] for p in range(len(pairs))]
            for ci, (c0, cs) in enumerate(col_chunks):
                vals = []
                for p in range(len(pairs)):
                    b_ref = in_refs[2 * p + 1]
                    rhs = b_ref[:, c0:c0 + cs] if mode == "nn" else b_ref[c0:c0 + cs, :]
                    vals.append(_dot(lhs[p], rhs, mode))
                epilogue(vals, [cols(r, c0, cs) for r in in_refs[2 * len(pairs):n_in]],
                         [cols(r, c0, cs) for r in out_refs[:n_out]], (i, j * len(col_chunks) + ci))
        else:
            @pl.when(k == 0)
            def _():
                for acc in accs:
                    acc[...] = jnp.zeros(acc.shape, F32)

            if steps is None:
                for p in range(len(pairs)):
                    contrib(p, accs[p])
            else:
                lo = 0
                for p, n in enumerate(steps):
                    pl.when((k >= lo) & (k < lo + n))(functools.partial(contrib, p, accs[0]))
                    lo += n

            @pl.when(k == nk - 1)
            def _():
                epilogue([acc[...] for acc in accs], in_refs[2 * len(pairs):n_in], out_refs[:n_out], (i, j))

        plumb.run(in_refs, out_refs, scratch, (i == 0) & (j == 0) & (k == 0),
                  (i == ni - 1) & (j == nj - 1) & (k == nk - 1))

    args, in_specs = [], []
    for a, a_spec, b, b_spec in pairs:
        args += [a, b]
        in_specs += [a_spec, b_spec]
    for e, e_spec in extras:
        args.append(e)
        in_specs.append(e_spec)
    res = pl.pallas_call(
        body,
        name=name,
        grid=grid,
        in_specs=in_specs + [ANY] * len(plumb.args),
        out_specs=[s for _, s in outs] + [ANY] * len(plumb.out_shape),
        out_shape=[o for o, _ in outs] + plumb.out_shape,
        scratch_shapes=[pltpu.VMEM(acc_shape, F32) for _ in range(n_acc)] + plumb.scratch,
        input_output_aliases=plumb.aliases,
        compiler_params=_params(),
    )(*args, *plumb.args)
    return plumb.deliver(res)


def _sds(shape, dtype):
    return jax.ShapeDtypeStruct(shape, dtype)


def _accumulate(ref, value, first):
    @pl.when(first)
    def _():
        ref[...] = value

    @pl.when(jnp.logical_not(first))
    def _():
        ref[...] += value


def _rms_fwd(name, x, gain):
    t, d = x.shape
    tm = _pick(t, 512, 8)

    def body(x_ref, g_ref, y_ref):
        xv = x_ref[...]
        rstd = lax.rsqrt(jnp.mean(xv * xv, axis=-1, keepdims=True) + EPS)
        y_ref[...] = (xv * rstd * g_ref[...]).astype(BF16)

    return pl.pallas_call(
        body, name=name, grid=(t // tm,),
        in_specs=[pl.BlockSpec((tm, d), lambda i: (i, 0)), pl.BlockSpec((1, d), lambda i: (0, 0))],
        out_specs=pl.BlockSpec((tm, d), lambda i: (i, 0)),
        out_shape=_sds((t, d), BF16),
        compiler_params=_params(),
    )(x, gain)


def _rms_bwd_epilogue(accs, extras, outs, ij):
    x_ref, g_ref, r_ref = extras
    dh_ref, dhb_ref, dg_ref = outs
    dn = accs[0]
    xv = x_ref[...]
    rstd = lax.rsqrt(jnp.mean(xv * xv, axis=-1, keepdims=True) + EPS)
    xhat = xv * rstd
    gd = dn * g_ref[...]
    dx = rstd * (gd - xhat * jnp.mean(gd * xhat, axis=-1, keepdims=True))
    dh = r_ref[...] + dx
    dh_ref[...] = dh
    dhb_ref[...] = dh.astype(BF16)
    _accumulate(dg_ref, jnp.sum(dn * xhat, axis=0, keepdims=True), ij[0] == 0)


def _rms_bwd_io(x, gain, dres, tm):
    t, d = x.shape
    row = pl.BlockSpec((tm, d), lambda i, j, k: (i, 0))
    extras = [(x, row), (gain, pl.BlockSpec((1, d), lambda i, j, k: (0, 0))), (dres, row)]
    outs = [(_sds((t, d), F32), row), (_sds((t, d), BF16), row),
            (_sds((1, d), F32), pl.BlockSpec((1, d), lambda i, j, k: (0, 0)))]
    return extras, outs


def _ffn_fwd(tag, h, gain, wgu, wd, hooks):
    t, d = h.shape
    fs = wgu.shape[2]
    f = 2 * fs
    tm = _pick(t, 512, 8)
    n = _rms_fwd(tag + "_norm", h, gain)

    def up_epilogue(accs, extras, outs, ij):
        g, u = accs
        gu_ref, a_ref = outs
        gu_ref[0] = g.astype(BF16)
        gu_ref[1] = u.astype(BF16)
        a_ref[...] = (g * jax.nn.sigmoid(g) * u).astype(BF16)

    a_spec = pl.BlockSpec((tm, d), lambda i, j, k: (i, 0))
    gu, a = _mm(
        tag + "_up", "nn", (t // tm, 2, 1),
        [(n, a_spec, wgu, pl.BlockSpec((None, d, fs), lambda i, j, k: (j, 0, 0))),
         (n, a_spec, wgu, pl.BlockSpec((None, d, fs), lambda i, j, k: (j + 2, 0, 0)))],
        [],
        [(_sds((2, t, f), BF16), pl.BlockSpec((2, tm, fs), lambda i, j, k: (0, i, j))),
         (_sds((t, f), BF16), pl.BlockSpec((tm, fs), lambda i, j, k: (i, j)))],
        (tm, fs), up_epilogue, comms=hooks.get("up", lambda: ())(), j_outer=True, chunked=True)

    def down_epilogue(accs, extras, outs, ij):
        outs[0][...] = extras[0][...] + 0.5 * accs[0]

    row = pl.BlockSpec((tm, d), lambda i, j, k: (i, 0))
    (h_new,) = _mm(
        tag + "_down", "nn", (t // tm, 1, 1),
        [(a, pl.BlockSpec((tm, f), lambda i, j, k: (i, 0)), wd, pl.BlockSpec((f, d), lambda i, j, k: (0, 0)))],
        [(h, row)], [(_sds((t, d), F32), row)], (tm, d), down_epilogue, comms=hooks.get("down", lambda: ())())
    return h_new, (n, gu, a)


def _ffn_bwd(tag, dh, dh_b, h, gain, wgu, wd, saved, hooks):
    n, gu, a = saved
    t, d = h.shape
    fs = wgu.shape[2]
    f = 2 * fs
    tm = _pick(t, 512, 8)
    tk = _pick(t, 512, 8)

    def dact_epilogue(accs, extras, outs, ij):
        da = 0.5 * accs[0]
        g = extras[0][0].astype(F32)
        u = extras[0][1].astype(F32)
        sg = jax.nn.sigmoid(g)
        outs[0][0] = (da * u * sg * (1.0 + g * (1.0 - sg))).astype(BF16)
        outs[0][1] = (da * g * sg).astype(BF16)

    gu_spec = pl.BlockSpec((2, tm, fs), lambda i, j, k: (0, i, j))
    (dgu,) = _mm(
        tag + "_dact", "nt", (t // tm, 2, 1),
        [(dh_b, pl.BlockSpec((tm, d), lambda i, j, k: (i, 0)), wd, pl.BlockSpec((fs, d), lambda i, j, k: (j, 0)))],
        [(gu, gu_spec)], [(_sds((2, t, f), BF16), gu_spec)], (tm, fs), dact_epilogue, j_outer=True, chunked=True,
        comms=hooks.get("dact", lambda: ())())

    def cast_epilogue(accs, extras, outs, ij):
        outs[0][...] = accs[0].astype(BF16)

    (dwgu,) = _mm(
        tag + "_dwgu", "tn", (1, 4, t // tk),
        [(n, pl.BlockSpec((tk, d), lambda i, j, k: (k, 0)),
          dgu, pl.BlockSpec((None, tk, fs), lambda i, j, k: (j // 2, k, j % 2)))],
        [], [(_sds((4, d, fs), BF16), pl.BlockSpec((None, d, fs), lambda i, j, k: (j, 0, 0)))], (d, fs), cast_epilogue,
        comms=hooks.get("dwgu", lambda: ())())

    def half_epilogue(accs, extras, outs, ij):
        outs[0][...] = (0.5 * accs[0]).astype(BF16)

    (dwd,) = _mm(
        tag + "_dwd", "tn", (2, 1, t // tk),
        [(a, pl.BlockSpec((tk, fs), lambda i, j, k: (k, i)), dh_b, pl.BlockSpec((tk, d), lambda i, j, k: (k, 0)))],
        [], [(_sds((f, d), BF16), pl.BlockSpec((fs, d), lambda i, j, k: (i, 0)))], (fs, d), half_epilogue,
        comms=hooks.get("dwd", lambda g: ())(dwgu))

    tmn = _pick(t, 1024, 8)
    extras, outs = _rms_bwd_io(h, gain, dh, tmn)
    dh_in, dh_in_b, dgain = _mm(
        tag + "_dnorm", "nt", (t // tmn, 1, 4),
        [(dgu, pl.BlockSpec((None, tmn, fs), lambda i, j, k: (k // 2, i, k % 2)),
          wgu, pl.BlockSpec((None, d, fs), lambda i, j, k: (k, 0, 0)))],
        extras, outs, (tmn, d), _rms_bwd_epilogue, comms=hooks.get("dnorm", lambda g, w: ())(dwgu, dwd))
    return dh_in, dh_in_b, dgain, dwgu, dwd


def _lane_lo(shape):
    return lax.broadcasted_iota(jnp.int32, shape, 1) < HEAD_DIM


def _pair_norm(xv, gain):
    lo = _lane_lo(xv.shape)
    x2 = xv * xv
    ms_lo = jnp.sum(jnp.where(lo, x2, 0.0), axis=-1, keepdims=True) * (1.0 / HEAD_DIM)
    ms_hi = jnp.sum(jnp.where(lo, 0.0, x2), axis=-1, keepdims=True) * (1.0 / HEAD_DIM)
    rstd = jnp.where(lo, lax.rsqrt(ms_lo + EPS), lax.rsqrt(ms_hi + EPS))
    xhat = xv * rstd
    return xhat * gain, xhat, rstd


def _pair_norm_bwd(dn, xhat, rstd, gain):
    lo = _lane_lo(dn.shape)
    gd = dn * gain
    t = gd * xhat
    m_lo = jnp.sum(jnp.where(lo, t, 0.0), axis=-1, keepdims=True) * (1.0 / HEAD_DIM)
    m_hi = jnp.sum(jnp.where(lo, 0.0, t), axis=-1, keepdims=True) * (1.0 / HEAD_DIM)
    dx = rstd * (gd - xhat * jnp.where(lo, m_lo, m_hi))
    return dx, jnp.sum(dn * xhat, axis=0, keepdims=True)


def _half(xv, hi):
    lo = _lane_lo(xv.shape)
    return jnp.where(lo, 0, xv) if hi else jnp.where(lo, xv, 0)


def _attn_window(i, prev):
    q0 = i * Q_BLOCK
    start = jnp.maximum(q0 - prev, 0)
    off = start - (q0 - prev)
    return pl.multiple_of(start, Q_BLOCK), pl.multiple_of(off, Q_BLOCK)


def _attn_specs(cfg, s, nq):
    kw = cfg["kw"]
    q_spec = pl.BlockSpec((Q_BLOCK, A_WIDTH), lambda b, i: (b * nq + i, cfg["qblk"]))
    k_spec = pl.BlockSpec((s, kw), lambda b, i: (b, cfg["kblk"]))
    v_spec = pl.BlockSpec((s, kw), lambda b, i: (b, cfg["vblk"]))
    return q_spec, k_spec, v_spec


def _const_spec(shape):
    return pl.BlockSpec(shape, lambda b, i: (0,) * len(shape))


KEY_CHUNK = 128


def _pair_bias(bias_t):
    wext = bias_t.shape[1]
    return jnp.transpose(bias_t.reshape(N_HEADS // 2, 2, wext, Q_BLOCK), (0, 2, 1, 3)).reshape(
        N_HEADS // 2, wext, 2 * Q_BLOCK)


def _unpair_bias(db2):
    wext = db2.shape[1]
    return jnp.transpose(db2.reshape(N_HEADS // 2, wext, 2, Q_BLOCK), (0, 2, 1, 3)).reshape(N_HEADS, wext, Q_BLOCK)


def _pair_rows(rows):
    two = rows.reshape(N_HEADS // 2, 2 * rows.shape[1])
    return jnp.broadcast_to(two[:, None, :], (N_HEADS // 2, 8, two.shape[1]))


def _sub_lo(shape):
    return lax.broadcasted_iota(jnp.int32, shape, 0) < HEAD_DIM


def _by_half(lo_row, hi_row, rows):
    return jnp.where(_sub_lo((rows, lo_row.shape[1])), lo_row, hi_row)


def _stack_pair(xn, jq, group):
    parts = []
    for hq in range(2):
        hk = ((2 * jq + hq) // group) % 2
        xm = _half(xn, hq)
        if hq != hk:
            xm = pltpu.roll(xm, HEAD_DIM, 1)
        parts.append(xm)
    return jnp.concatenate(parts, axis=0).astype(BF16)


def _place_transposed(blk, dst_ref, c, heads, group):
    bt = blk.T
    lo = _sub_lo(bt.shape)
    for h in heads:
        src_hi = ((h // group) % 2) == 1
        part = jnp.where(lo, 0.0, bt) if src_hi else jnp.where(lo, bt, 0.0)
        if src_hi != (h % 2 == 1):
            part = pltpu.roll(part, HEAD_DIM, 0)
        dst_ref[h, c] = part.astype(BF16)


def _attn_fwd(name, qkv, bias2, sink2, gq, gk, cfg, n_batch, comms=()):
    t = qkv.shape[0]
    s = t // n_batch
    nq = s // Q_BLOCK
    nkc = s // KEY_CHUNK
    prev, group, kw = cfg["prev"], cfg["group"], cfg["kw"]
    n_chunks = (prev + Q_BLOCK) // KEY_CHUNK
    wext = bias2.shape[1]
    plumb = _CommPlumbing(comms, 7, 2, 2)
    n_all_in = 7 + len(plumb.args)
    n_all_out = 2 + len(plumb.out_shape)

    def body(*refs):
        q_ref, k_ref, v_ref, bias_ref, sink_ref, gq_ref, gk_ref = refs[:7]
        y_ref, lse_ref = refs[n_all_in:n_all_in + 2]
        kn_ref, vt_ref = refs[n_all_in + n_all_out:n_all_in + n_all_out + 2]
        i = pl.program_id(1)
        plumb.run(refs[:n_all_in], refs[n_all_in:n_all_in + n_all_out], refs[n_all_in + n_all_out:],
                  (pl.program_id(0) == 0) & (i == 0), (pl.program_id(0) == n_batch - 1) & (i == nq - 1))

        @pl.when(i == 0)
        def _():
            for jk in range(kw // PAIR):
                cols = pl.ds(jk * PAIR, PAIR)
                heads = [h for h in range(N_HEADS) if (h // group) // 2 == jk]
                kn, _, _ = _pair_norm(k_ref[:, cols].astype(F32), gk_ref[...])
                kn_ref[:, cols] = kn.astype(BF16)
                for c in range(nkc):
                    _place_transposed(v_ref[pl.ds(c * KEY_CHUNK, KEY_CHUNK), cols].astype(F32), vt_ref, c, heads, group)

        start, off = _attn_window(i, prev)
        c0 = start // KEY_CHUNK
        sub8 = lax.broadcasted_iota(jnp.int32, (N_HEADS, Q_BLOCK), 0)
        lse = jnp.zeros((N_HEADS, Q_BLOCK), F32)
        for jq in range(N_HEADS // 2):
            kcols = pl.ds((((2 * jq) // group) // 2) * PAIR, PAIR)
            qn, _, _ = _pair_norm(q_ref[:, pl.ds(jq * PAIR, PAIR)].astype(F32), gq_ref[...])
            qs = _stack_pair(qn * ATTN_SCALE, jq, group)
            m = sink_ref[jq, 0:1, :]
            l = jnp.ones((1, 2 * Q_BLOCK), F32)
            ot = jnp.zeros((PAIR, Q_BLOCK), F32)
            for c in range(n_chunks):
                rows = pl.ds(start + c * KEY_CHUNK, KEY_CHUNK)
                s2 = _dot(kn_ref[rows, kcols], qs, "nt") + bias_ref[jq, pl.ds(off + c * KEY_CHUNK, KEY_CHUNK), :]
                m_new = jnp.maximum(m, jnp.max(s2, axis=0, keepdims=True))
                alpha = jnp.exp(m - m_new)
                p = jnp.exp(s2 - m_new)
                l = alpha * l + jnp.sum(p, axis=0, keepdims=True)
                m = m_new
                pst = jnp.concatenate([p[:, :Q_BLOCK], p[:, Q_BLOCK:]], axis=0)
                vl = jnp.concatenate([vt_ref[2 * jq, c0 + c], vt_ref[2 * jq + 1, c0 + c]], axis=1)
                ot = ot * _by_half(alpha[:, :Q_BLOCK], alpha[:, Q_BLOCK:], PAIR) + _dot(vl, pst, "nn")
            inv = 1.0 / l
            ot = ot * _by_half(inv[:, :Q_BLOCK], inv[:, Q_BLOCK:], PAIR)
            y_ref[:, pl.ds(jq * PAIR, PAIR)] = ot.T.astype(BF16)
            lse2 = m + jnp.log(l)
            lse = jnp.where(sub8 == 2 * jq, lse2[:, :Q_BLOCK], lse)
            lse = jnp.where(sub8 == 2 * jq + 1, lse2[:, Q_BLOCK:], lse)
        lse_ref[...] = lse

    q_spec, k_spec, v_spec = _attn_specs(cfg, s, nq)
    res = pl.pallas_call(
        body, name=name, grid=(n_batch, nq),
        in_specs=[q_spec, k_spec, v_spec, _const_spec((N_HEADS // 2, wext, 2 * Q_BLOCK)),
                  _const_spec((N_HEADS // 2, 8, 2 * Q_BLOCK)), _const_spec((1, PAIR)), _const_spec((1, PAIR))]
        + [ANY] * len(plumb.args),
        out_specs=[pl.BlockSpec((Q_BLOCK, A_WIDTH), lambda b, i: (b * nq + i, 0)),
                   pl.BlockSpec((None, N_HEADS, Q_BLOCK), lambda b, i: (b * nq + i, 0, 0))]
        + [ANY] * len(plumb.out_shape),
        out_shape=[_sds((t, A_WIDTH), BF16), _sds((t // Q_BLOCK, N_HEADS, Q_BLOCK), F32)] + plumb.out_shape,
        scratch_shapes=[pltpu.VMEM((s, kw), BF16), pltpu.VMEM((N_HEADS, nkc, PAIR, KEY_CHUNK), BF16)] + plumb.scratch,
        input_output_aliases=plumb.aliases,
        compiler_params=_params(),
    )(qkv, qkv, qkv, bias2, sink2, gq, gk, *plumb.args)
    return plumb.deliver(res)


def _attn_bwd(name, qkv, bias2, sink2, gq, gk, y, dy, lse, cfg, n_batch, want_dbias, comms=()):
    t = qkv.shape[0]
    s = t // n_batch
    nq = s // Q_BLOCK
    nkc = s // KEY_CHUNK
    prev, group, kw = cfg["prev"], cfg["group"], cfg["kw"]
    n_chunks = (prev + Q_BLOCK) // KEY_CHUNK
    wext = bias2.shape[1]
    plumb = _CommPlumbing(comms, 10, 7, 4)
    n_all_in = 10 + len(plumb.args)
    n_all_out = 7 + len(plumb.out_shape)

    def body(*refs):
        q_ref, k_ref, v_ref, bias_ref, sink_ref, gq_ref, gk_ref, y_ref, dy_ref, lse_ref = refs[:10]
        dq_ref, dk_ref, dv_ref, db_ref, dsink_ref, dgq_ref, dgk_ref = refs[n_all_in:n_all_in + 7]
        kn_ref, knt_ref, dkn_ref, dvs_ref = refs[n_all_in + n_all_out:n_all_in + n_all_out + 4]
        b = pl.program_id(0)
        i = pl.program_id(1)
        first = (b == 0) & (i == 0)
        plumb.run(refs[:n_all_in], refs[n_all_in:n_all_in + n_all_out], refs[n_all_in + n_all_out:],
                  first, (b == n_batch - 1) & (i == nq - 1))

        @pl.when(i == 0)
        def _():
            for jk in range(kw // PAIR):
                cols = pl.ds(jk * PAIR, PAIR)
                heads = [h for h in range(N_HEADS) if (h // group) // 2 == jk]
                for c in range(nkc):
                    rows = pl.ds(c * KEY_CHUNK, KEY_CHUNK)
                    kn, _, _ = _pair_norm(k_ref[rows, cols].astype(F32), gk_ref[...])
                    kn_ref[rows, cols] = kn.astype(BF16)
                    _place_transposed(kn, knt_ref, c, heads, group)
            dkn_ref[...] = jnp.zeros(dkn_ref.shape, F32)
            dvs_ref[...] = jnp.zeros(dvs_ref.shape, F32)

        @pl.when(first)
        def _():
            db_ref[...] = jnp.zeros(db_ref.shape, F32)
            dsink_ref[...] = jnp.zeros(dsink_ref.shape, F32)
            dgq_ref[...] = jnp.zeros(dgq_ref.shape, F32)
            dgk_ref[...] = jnp.zeros(dgk_ref.shape, F32)

        start, off = _attn_window(i, prev)
        c0 = start // KEY_CHUNK
        for jq in range(N_HEADS // 2):
            cols = pl.ds(jq * PAIR, PAIR)
            kcols = pl.ds((((2 * jq) // group) // 2) * PAIR, PAIR)
            qn, q_hat, q_rstd = _pair_norm(q_ref[:, cols].astype(F32), gq_ref[...])
            qs = _stack_pair(qn * ATTN_SCALE, jq, group)
            do_pair = dy_ref[:, cols].astype(F32)
            dos = _stack_pair(do_pair, jq, group)
            prod_t = (do_pair * y_ref[:, cols].astype(F32)).T
            lo = _sub_lo(prod_t.shape)
            delta2 = jnp.concatenate([jnp.sum(jnp.where(lo, prod_t, 0.0), axis=0, keepdims=True),
                                      jnp.sum(jnp.where(lo, 0.0, prod_t), axis=0, keepdims=True)], axis=1)
            lse2 = jnp.concatenate([lse_ref[2 * jq:2 * jq + 1, :], lse_ref[2 * jq + 1:2 * jq + 2, :]], axis=1)
            dsk = -jnp.exp(sink_ref[jq, 0:1, :] - lse2) * delta2
            dsink_ref[2 * jq:2 * jq + 1, :] += dsk[:, :Q_BLOCK]
            dsink_ref[2 * jq + 1:2 * jq + 2, :] += dsk[:, Q_BLOCK:]
            dqt = jnp.zeros((PAIR, Q_BLOCK), F32)
            for c in range(n_chunks):
                rows = pl.ds(start + c * KEY_CHUNK, KEY_CHUNK)
                brows = pl.ds(off + c * KEY_CHUNK, KEY_CHUNK)
                s2 = _dot(kn_ref[rows, kcols], qs, "nt") + bias_ref[jq, brows, :]
                p = jnp.exp(s2 - lse2)
                ds = p * (_dot(v_ref[rows, kcols], dos, "nt") - delta2)
                if want_dbias:
                    db_ref[jq, brows, :] += ds
                ds_b = ds.astype(BF16)
                dkn_ref[rows, kcols] += _dot(ds_b, qs, "nn")
                dvs_ref[rows, kcols] += _dot(p, dos, "nn")
                dst = jnp.concatenate([ds_b[:, :Q_BLOCK], ds_b[:, Q_BLOCK:]], axis=0)
                kl = jnp.concatenate([knt_ref[2 * jq, c0 + c], knt_ref[2 * jq + 1, c0 + c]], axis=1)
                dqt = dqt + _dot(kl, dst, "nn")
            dq_raw, dg = _pair_norm_bwd(dqt.T * ATTN_SCALE, q_hat, q_rstd, gq_ref[...])
            dq_ref[:, cols] = dq_raw.astype(BF16)
            dgq_ref[...] += dg

        @pl.when(i == nq - 1)
        def _():
            for jk in range(kw // PAIR):
                kcols = pl.ds(jk * PAIR, PAIR)
                _, k_hat, k_rstd = _pair_norm(k_ref[:, kcols].astype(F32), gk_ref[...])
                dk_raw, dg = _pair_norm_bwd(dkn_ref[:, kcols], k_hat, k_rstd, gk_ref[...])
                dk_ref[:, kcols] = dk_raw.astype(BF16)
                dgk_ref[...] += dg
            dv_ref[...] = dvs_ref[...].astype(BF16)

    q_spec, k_spec, v_spec = _attn_specs(cfg, s, nq)
    row = pl.BlockSpec((Q_BLOCK, A_WIDTH), lambda b, i: (b * nq + i, 0))
    kv_out = pl.BlockSpec((s, kw), lambda b, i: (b, 0))
    pair_bias = _const_spec((N_HEADS // 2, wext, 2 * Q_BLOCK))
    res = pl.pallas_call(
        body, name=name, grid=(n_batch, nq),
        in_specs=[q_spec, k_spec, v_spec, pair_bias, _const_spec((N_HEADS // 2, 8, 2 * Q_BLOCK)),
                  _const_spec((1, PAIR)), _const_spec((1, PAIR)), row, row,
                  pl.BlockSpec((None, N_HEADS, Q_BLOCK), lambda b, i: (b * nq + i, 0, 0))] + [ANY] * len(plumb.args),
        out_specs=[row, kv_out, kv_out, pair_bias, _const_spec((N_HEADS, 128)),
                   _const_spec((1, PAIR)), _const_spec((1, PAIR))] + [ANY] * len(plumb.out_shape),
        out_shape=[_sds((t, A_WIDTH), BF16), _sds((t, kw), BF16), _sds((t, kw), BF16),
                   _sds((N_HEADS // 2, wext, 2 * Q_BLOCK), F32), _sds((N_HEADS, 128), F32),
                   _sds((1, PAIR), F32), _sds((1, PAIR), F32)] + plumb.out_shape,
        scratch_shapes=[pltpu.VMEM((s, kw), BF16), pltpu.VMEM((N_HEADS, nkc, PAIR, KEY_CHUNK), BF16),
                        pltpu.VMEM((s, kw), F32), pltpu.VMEM((s, kw), F32)] + plumb.scratch,
        input_output_aliases=plumb.aliases,
        compiler_params=_params(),
    )(qkv, qkv, qkv, bias2, sink2, gq, gk, y, dy, lse, *plumb.args)
    return plumb.deliver(res)


def _band_tables(prev_chunks):
    prev = prev_chunks * CHUNK
    wext = 2 * prev + Q_BLOCK
    jj = np.arange(wext)[:, None]
    ii = np.arange(Q_BLOCK)[None, :]
    dist = prev + ii - jj
    rel_chunk = (prev // CHUNK + ii // CHUNK) - jj // CHUNK
    allowed = (rel_chunk >= 0) & (rel_chunk <= prev_chunks)
    return dist, allowed


def _alibi_slopes():
    return np.array([2.0 ** (-8.0 * (h + 1) / N_HEADS) for h in range(N_HEADS)], dtype=np.float32)


def _diag_onehot(prev, wext):
    n_diag = wext + Q_BLOCK - 1
    idx = np.clip(prev + Q_BLOCK - 1 - np.arange(n_diag), -A_MAX_REL, A_MAX_REL) + A_MAX_REL
    onehot = np.zeros((n_diag, 2 * A_MAX_REL + 1), np.float32)
    onehot[np.arange(n_diag), idx] = 1.0
    return onehot


def _bias_a(rel_bias):
    prev = A_PREV_CHUNKS * CHUNK
    _, allowed = _band_tables(A_PREV_CHUNKS)
    wext = allowed.shape[0]
    n_diag = wext + Q_BLOCK - 1
    seq = jnp.dot(rel_bias, jnp.asarray(_diag_onehot(prev, wext).T), precision=lax.Precision.HIGHEST)
    seq = jnp.pad(seq, ((0, 0), (0, 1)))
    rows = jnp.broadcast_to(seq[:, None, :], (N_HEADS, Q_BLOCK, n_diag + 1)).reshape(N_HEADS, -1)
    skew = rows[:, :Q_BLOCK * n_diag].reshape(N_HEADS, Q_BLOCK, n_diag)
    tile = jnp.transpose(skew[:, :, Q_BLOCK - 1:Q_BLOCK - 1 + wext], (0, 2, 1))
    return jnp.where(jnp.asarray(allowed)[None], tile, NEG_INF)


def _bias_b():
    dist, allowed = _band_tables(B_PREV_CHUNKS)
    bias = -_alibi_slopes()[:, None, None] * np.abs(dist).astype(np.float32)[None]
    return jnp.asarray(np.where(allowed[None], bias, np.float32(NEG_INF)).astype(np.float32))


def _rel_bias_grad(db_t):
    prev = A_PREV_CHUNKS * CHUNK
    wext = db_t.shape[1]
    n_diag = wext + Q_BLOCK - 1
    wp = n_diag + Q_BLOCK - 1
    xp = jnp.pad(jnp.transpose(db_t, (0, 2, 1)), ((0, 0), (0, 0), (Q_BLOCK - 1, Q_BLOCK - 1)))
    flat = jnp.pad(xp.reshape(N_HEADS, Q_BLOCK * wp), ((0, 0), (0, Q_BLOCK)))
    skew = flat.reshape(N_HEADS, Q_BLOCK, wp + 1)[:, :, :n_diag]
    diag = jnp.sum(skew, axis=1)
    return jnp.dot(diag, jnp.asarray(_diag_onehot(prev, wext)), precision=lax.Precision.HIGHEST)


def _ew(name, fn, ins, out_dtypes):
    r, c = ins[0].shape
    rb = _pick(r, max(16, (1 << 19) // c), 16)
    spec = pl.BlockSpec((rb, c), lambda i: (i, 0))

    def body(*refs):
        vals = fn(*[ref[...] for ref in refs[:len(ins)]])
        for ref, val in zip(refs[len(ins):], vals):
            ref[...] = val.astype(ref.dtype)

    return pl.pallas_call(
        body, name=name, grid=(r // rb,), in_specs=[spec] * len(ins), out_specs=[spec] * len(out_dtypes),
        out_shape=[_sds((r, c), dt) for dt in out_dtypes], compiler_params=_params(),
    )(*ins)


def _adamw_math(w, g, m, v):
    m = ADAM_B1 * m + (1.0 - ADAM_B1) * g
    v = ADAM_B2 * v + (1.0 - ADAM_B2) * (g * g)
    m_hat = m / (1.0 - ADAM_B1 ** ADAM_STEP)
    v_hat = v / (1.0 - ADAM_B2 ** ADAM_STEP)
    delta = -ADAM_LR * (m_hat / (jnp.sqrt(v_hat) + ADAM_EPS) + ADAM_WD * w)
    return delta, m, v


def _adamw_terms(name, terms, w, m, v):
    r, c = w.shape
    hr = r // 2
    rb = _pick(hr, max(16, (1 << 17) // c), 16)
    nb = hr // rb

    def body(t_ref, w_ref, m_ref, v_ref, g_ref, d_ref, nm_ref, nv_ref):
        g = t_ref[0].astype(F32)
        for k in range(1, N_CHIPS):
            g = g + t_ref[k].astype(F32)
        delta, nm, nv = _adamw_math(w_ref[...], g, m_ref[...], v_ref[...])
        g_ref[...] = g
        d_ref[...] = delta
        nm_ref[...] = nm
        nv_ref[...] = nv

    spec = pl.BlockSpec((rb, c), lambda h, i: (h * nb + i, 0))
    return pl.pallas_call(
        body, name=name, grid=(2, nb),
        in_specs=[pl.BlockSpec((None, N_CHIPS, rb, c), lambda h, i: (h, 0, i, 0)), spec, spec, spec],
        out_specs=[spec] * 4, out_shape=[_sds((r, c), F32)] * 4, compiler_params=_params(),
    )(terms, w, m, v)


def _mesh_place():
    x, y, c = lax.axis_index("x"), lax.axis_index("y"), lax.axis_index("c")
    chips = [(x, 1 - y), (1 - x, y), (1 - x, 1 - y)]
    return x, y, c, chips


def _all_gather_weights(shards):
    n = len(shards)

    def body(*refs):
        ins, outs = refs[:n], refs[n:2 * n]
        local_sem, ici_send, ici_recv, d2d_send, d2d_recv = refs[2 * n:]
        x, y, c, chips = _mesh_place()
        me = 2 * x + y
        sibling = (x, y, 1 - c)
        local, sent = [], []
        for wi in range(n):
            loc = pltpu.make_async_copy(ins[wi], outs[wi].at[me], local_sem.at[wi])
            loc.start()
            local.append(loc)
            for k, (tx, ty) in enumerate(chips):
                cp = pltpu.make_async_remote_copy(
                    src_ref=ins[wi].at[c], dst_ref=outs[wi].at[me, c],
                    send_sem=ici_send.at[wi * 3 + k], recv_sem=ici_recv.at[wi * 3 + k],
                    device_id=(tx, ty, c), device_id_type=MESH)
                cp.start()
                sent.append(cp)
        passed = []
        for wi in range(n):
            for k, (tx, ty) in enumerate(chips):
                slab = outs[wi].at[2 * tx + ty, c]
                pltpu.make_async_remote_copy(
                    src_ref=slab, dst_ref=slab, send_sem=ici_send.at[wi * 3 + k], recv_sem=ici_recv.at[wi * 3 + k],
                    device_id=(tx, ty, c), device_id_type=MESH).wait_recv()
                fw = pltpu.make_async_remote_copy(
                    src_ref=slab, dst_ref=slab, send_sem=d2d_send.at[wi * 3 + k], recv_sem=d2d_recv.at[wi * 3 + k],
                    device_id=sibling, device_id_type=MESH)
                fw.start()
                passed.append(fw)
        for wi in range(n):
            for k, (tx, ty) in enumerate(chips):
                slab = outs[wi].at[2 * tx + ty, 1 - c]
                pltpu.make_async_remote_copy(
                    src_ref=slab, dst_ref=slab, send_sem=d2d_send.at[wi * 3 + k], recv_sem=d2d_recv.at[wi * 3 + k],
                    device_id=sibling, device_id_type=MESH).wait_recv()
        for loc in local:
            loc.wait()
        for cp in sent + passed:
            cp.wait_send()

    return pl.pallas_call(
        body, name="all_gather_weights",
        in_specs=[ANY] * n, out_specs=[ANY] * n,
        out_shape=[_sds((N_CHIPS,) + s.shape, s.dtype) for s in shards],
        scratch_shapes=[pltpu.SemaphoreType.DMA((n,)), pltpu.SemaphoreType.DMA((3 * n,)),
                        pltpu.SemaphoreType.DMA((3 * n,)), pltpu.SemaphoreType.DMA((3 * n,)),
                        pltpu.SemaphoreType.DMA((3 * n,))],
    )(*shards)


def _run_comms(name, comms):
    plumb = _CommPlumbing(comms, 0, 0, 0)
    n_in, n_out = len(plumb.args), len(plumb.out_shape)

    def body(*refs):
        parts = []
        i0, o0, s0 = 0, n_in, n_in + n_out
        for cm in plumb.comms:
            parts.append((refs[i0:i0 + len(cm.ins)], refs[o0:o0 + len(cm.outs)], refs[s0:s0 + len(cm.sems)]))
            i0 += len(cm.ins)
            o0 += len(cm.outs)
            s0 += len(cm.sems)
        for cm, part in zip(plumb.comms, parts):
            cm.start(*part)
        for cm, part in zip(plumb.comms, parts):
            cm.finish(*part)

    res = pl.pallas_call(
        body, name=name, in_specs=[ANY] * n_in, out_specs=[ANY] * n_out, out_shape=plumb.out_shape,
        scratch_shapes=plumb.scratch, input_output_aliases=plumb.aliases,
    )(*plumb.args)
    plumb.deliver(res)


def _gather_ici(shards):
    n = len(shards)

    def copies(ins, outs, sems):
        local_sem, send_sem, recv_sem = sems
        x, y, c, chips = _mesh_place()
        me = 2 * x + y
        local, sends, recvs = [], [], []
        for wi in range(n):
            local.append(pltpu.make_async_copy(ins[wi], outs[wi].at[me], local_sem.at[wi]))
            for k, (tx, ty) in enumerate(chips):
                sems_k = dict(send_sem=send_sem.at[wi * 3 + k], recv_sem=recv_sem.at[wi * 3 + k],
                              device_id=(tx, ty, c), device_id_type=MESH)
                sends.append(pltpu.make_async_remote_copy(
                    src_ref=ins[wi].at[c], dst_ref=outs[wi].at[me, c], **sems_k))
                slab = outs[wi].at[2 * tx + ty, c]
                recvs.append(pltpu.make_async_remote_copy(src_ref=slab, dst_ref=slab, **sems_k))
        return local, sends, recvs

    def start(ins, outs, sems):
        local, sends, _ = copies(ins, outs, sems)
        for cp in local + sends:
            cp.start()

    def finish(ins, outs, sems):
        local, sends, recvs = copies(ins, outs, sems)
        for cp in local:
            cp.wait()
        for cp in recvs:
            cp.wait_recv()
        for cp in sends:
            cp.wait_send()

    return _Comm(shards, [_sds((N_CHIPS,) + s.shape, s.dtype) for s in shards], {},
                 [pltpu.SemaphoreType.DMA((n,)), pltpu.SemaphoreType.DMA((3 * n,)), pltpu.SemaphoreType.DMA((3 * n,))],
                 start, finish)


def _gather_d2d(gathered):
    n = len(gathered)

    def copies(outs, sems):
        send_sem, recv_sem = sems
        x, y, c, chips = _mesh_place()
        sends, recvs = [], []
        for wi in range(n):
            for k, (tx, ty) in enumerate(chips):
                sems_k = dict(send_sem=send_sem.at[wi * 3 + k], recv_sem=recv_sem.at[wi * 3 + k],
                              device_id=(x, y, 1 - c), device_id_type=MESH)
                mine = outs[wi].at[2 * tx + ty, c]
                theirs = outs[wi].at[2 * tx + ty, 1 - c]
                sends.append(pltpu.make_async_remote_copy(src_ref=mine, dst_ref=mine, **sems_k))
                recvs.append(pltpu.make_async_remote_copy(src_ref=theirs, dst_ref=theirs, **sems_k))
        return sends, recvs

    def start(ins, outs, sems):
        for cp in copies(outs, sems)[0]:
            cp.start()

    def finish(ins, outs, sems):
        sends, recvs = copies(outs, sems)
        for cp in recvs:
            cp.wait_recv()
        for cp in sends:
            cp.wait_send()

    return _Comm(gathered, [_sds(g.shape, g.dtype) for g in gathered], {i: i for i in range(n)},
                 [pltpu.SemaphoreType.DMA((3 * n,)), pltpu.SemaphoreType.DMA((3 * n,))], start, finish)


def _exchange_halves(grads):
    n = len(grads)

    def copies(ins, outs, sems):
        send_sem, recv_sem = sems
        x, y, c, _ = _mesh_place()
        return [pltpu.make_async_remote_copy(
            src_ref=ins[wi].at[t, 1 - c], dst_ref=outs[wi].at[t],
            send_sem=send_sem.at[wi * N_CHIPS + t], recv_sem=recv_sem.at[wi * N_CHIPS + t],
            device_id=(x, y, 1 - c), device_id_type=MESH) for wi in range(n) for t in range(N_CHIPS)]

    def start(ins, outs, sems):
        for cp in copies(ins, outs, sems):
            cp.start()

    def finish(ins, outs, sems):
        for cp in copies(ins, outs, sems):
            cp.wait()

    return _Comm(grads, [_sds((N_CHIPS,) + g.shape[2:], g.dtype) for g in grads], {},
                 [pltpu.SemaphoreType.DMA((N_CHIPS * n,)), pltpu.SemaphoreType.DMA((N_CHIPS * n,))], start, finish)


def _scatter_ici(sums):
    n = len(sums)

    def copies(ins, outs, sems):
        local_sem, send_sem, recv_sem = sems
        x, y, c, chips = _mesh_place()
        me = 2 * x + y
        local, sends, recvs = [], [], []
        for wi in range(n):
            local.append(pltpu.make_async_copy(ins[wi].at[me], outs[wi].at[c, 0], local_sem.at[wi]))
            for k, (tx, ty) in enumerate(chips):
                sems_k = dict(send_sem=send_sem.at[wi * 3 + k], recv_sem=recv_sem.at[wi * 3 + k],
                              device_id=(tx, ty, c), device_id_type=MESH)
                land = outs[wi].at[c, k + 1]
                sends.append(pltpu.make_async_remote_copy(src_ref=ins[wi].at[2 * tx + ty], dst_ref=land, **sems_k))
                recvs.append(pltpu.make_async_remote_copy(src_ref=land, dst_ref=land, **sems_k))
        return local, sends, recvs

    def start(ins, outs, sems):
        local, sends, _ = copies(ins, outs, sems)
        for cp in local + sends:
            cp.start()

    def finish(ins, outs, sems):
        local, sends, recvs = copies(ins, outs, sems)
        for cp in local:
            cp.wait()
        for cp in recvs:
            cp.wait_recv()
        for cp in sends:
            cp.wait_send()

    return _Comm(sums, [_sds((2, N_CHIPS) + s.shape[1:], s.dtype) for s in sums], {},
                 [pltpu.SemaphoreType.DMA((n,)), pltpu.SemaphoreType.DMA((3 * n,)), pltpu.SemaphoreType.DMA((3 * n,))],
                 start, finish)


def _scatter_d2d(terms):
    n = len(terms)

    def copies(outs, sems):
        send_sem, recv_sem = sems
        x, y, c, _ = _mesh_place()
        sends, recvs = [], []
        for wi in range(n):
            sems_w = dict(send_sem=send_sem.at[wi], recv_sem=recv_sem.at[wi],
                          device_id=(x, y, 1 - c), device_id_type=MESH)
            sends.append(pltpu.make_async_remote_copy(src_ref=outs[wi].at[c], dst_ref=outs[wi].at[c], **sems_w))
            recvs.append(pltpu.make_async_remote_copy(src_ref=outs[wi].at[1 - c], dst_ref=outs[wi].at[1 - c], **sems_w))
        return sends, recvs

    def start(ins, outs, sems):
        for cp in copies(outs, sems)[0]:
            cp.start()

    def finish(ins, outs, sems):
        sends, recvs = copies(outs, sems)
        for cp in recvs:
            cp.wait_recv()
        for cp in sends:
            cp.wait_send()

    return _Comm(terms, [_sds(t.shape, t.dtype) for t in terms], {i: i for i in range(n)},
                 [pltpu.SemaphoreType.DMA((n,)), pltpu.SemaphoreType.DMA((n,))], start, finish)


def _chip_sum(name, grad, got, core):
    _, _, hr, c = grad.shape
    rb = _pick(hr, max(16, (1 << 19) // c), 16)

    def body(core_ref, a_ref, b_ref, o_ref):
        o_ref[...] = (a_ref[...].astype(F32) + b_ref[...].astype(F32)).astype(BF16)

    out_spec = pl.BlockSpec((None, rb, c), lambda t, i, core_ref: (t, i, 0))
    return pl.pallas_call(
        body, name=name,
        grid_spec=pltpu.PrefetchScalarGridSpec(
            num_scalar_prefetch=1, grid=(N_CHIPS, hr // rb),
            in_specs=[pl.BlockSpec((None, None, rb, c), lambda t, i, core_ref: (t, core_ref[0], i, 0)), out_spec],
            out_specs=out_spec),
        out_shape=_sds((N_CHIPS, hr, c), BF16), compiler_params=_params(),
    )(core, grad, got)


def _all_reduce_small(pack):
    r = pack.shape[0]

    def body(p_ref, o_ref, land_ref, send_sem, recv_sem):
        x, y, c, _ = _mesh_place()
        me = 4 * x + 2 * y + c
        flips = [(k >> 2 & 1, k >> 1 & 1, k & 1) for k in range(1, N_DEV)]

        def peer(fx, fy, fc):
            return (1 - x if fx else x, 1 - y if fy else y, 1 - c if fc else c)

        land_ref[me] = p_ref[...]
        sent = []
        for k, flip in enumerate(flips):
            cp = pltpu.make_async_remote_copy(
                src_ref=p_ref, dst_ref=land_ref.at[me], send_sem=send_sem.at[k], recv_sem=recv_sem.at[k],
                device_id=peer(*flip), device_id_type=MESH)
            cp.start()
            sent.append(cp)
        for k, flip in enumerate(flips):
            px, py, pc = peer(*flip)
            slot = land_ref.at[4 * px + 2 * py + pc]
            pltpu.make_async_remote_copy(
                src_ref=slot, dst_ref=slot, send_sem=send_sem.at[k], recv_sem=recv_sem.at[k],
                device_id=(px, py, pc), device_id_type=MESH).wait_recv()
        total = land_ref[0]
        for d in range(1, N_DEV):
            total = total + land_ref[d]
        o_ref[...] = total
        for cp in sent:
            cp.wait_send()

    vmem = pl.BlockSpec(memory_space=pltpu.VMEM)
    return pl.pallas_call(
        body, name="all_reduce_small", in_specs=[vmem], out_specs=vmem, out_shape=_sds((r, 128), F32),
        scratch_shapes=[pltpu.VMEM((N_DEV, r, 128), F32), pltpu.SemaphoreType.DMA((N_DEV - 1,)),
                        pltpu.SemaphoreType.DMA((N_DEV - 1,))],
    )(pack)


PACK_TILE = 8 * 128


def _pack(items):
    rows, i = [], 0
    while i < len(items):
        j = i
        while j < len(items) and items[j].size == items[i].size:
            j += 1
        group = jnp.stack([it.reshape(-1).astype(F32) for it in items[i:j]])
        rows.append(jnp.pad(group, ((0, 0), (0, -group.shape[1] % PACK_TILE))).reshape(-1, 128))
        i = j
    return jnp.concatenate(rows, axis=0)


def _unpack(pack, shapes):
    out, row = [], 0
    for shp in shapes:
        size = int(np.prod(shp))
        nrow = -(-size // PACK_TILE) * (PACK_TILE // 128)
        out.append(pack[row:row + nrow].reshape(-1)[:size].reshape(shp))
        row += nrow
    return out


BIG = ["ffn1_w_gu", "ffn1_w_down", "w_in", "w_gate", "w_proj_a", "w_proj_b", "w_out",
       "ffn2_w_gu", "ffn2_w_down", "w_ple_gate", "w_ple_proj"]
SMALL = ["ffn1_norm", "mix_norm", "ffn2_norm", "ple_norm", "a_q_norm", "a_k_norm", "b_q_norm", "b_k_norm",
         "a_rel_bias", "b_sinks"]
WEIGHTS = ["ffn1_norm", "ffn1_w_gu", "ffn1_w_down", "mix_norm", "w_in", "a_q_norm", "a_k_norm", "a_rel_bias",
           "b_q_norm", "b_k_norm", "b_sinks", "w_gate", "w_proj_a", "w_proj_b", "w_out", "ffn2_norm",
           "ffn2_w_gu", "ffn2_w_down", "ple_norm", "w_ple_gate", "w_ple_proj"]
ATTN_A = dict(prev=A_PREV_CHUNKS * CHUNK, group=1, kw=A_WIDTH, qblk=0, kblk=1, vblk=2)
ATTN_B = dict(prev=B_PREV_CHUNKS * CHUNK, group=N_HEADS // B_KV_HEADS, kw=B_KV_WIDTH, qblk=3,
              kblk=4 * A_WIDTH // B_KV_WIDTH, vblk=4 * A_WIDTH // B_KV_WIDTH + 1)


def _cast_epilogue(accs, extras, outs, ij):
    for acc, out in zip(accs, outs):
        out[...] = acc.astype(out.dtype)


GATHER_FIRST = ["ffn1_w_gu", "ffn1_w_down"]
GATHER_LATE = ["ffn2_w_gu", "ffn2_w_down", "w_ple_gate", "w_ple_proj"]
ROW_SHARDED = ("ffn1_w_down", "ffn2_w_down", "w_out", "w_ple_gate")


def _slotted(name, grad):
    if name == "w_in":
        rows, cols = grad.shape
        grad = jnp.transpose(grad.reshape(rows, N_CHIPS, cols // N_CHIPS), (1, 0, 2))
    elif name in ROW_SHARDED:
        grad = grad.reshape(N_CHIPS, grad.shape[0] // N_CHIPS, grad.shape[1])
    return grad.reshape(N_CHIPS, 2, grad.shape[1] // 2, grad.shape[2])


def _local_step(xt, pt, tgt, n_batch, shards, small, core):
    t, d = xt.shape
    tm = _pick(t, 512, 8)
    tk = _pick(t, 512, 8)
    nt = t // tm
    row = pl.BlockSpec((tm, d), lambda i, j, k: (i, 0))
    gs = shards["w_gate"].shape[1]
    ps = shards["w_proj_a"].shape[1]
    es = shards["w_ple_proj"].shape[1]
    pdim = pt.shape[1]
    ncols = N_CHIPS * shards["w_in"].shape[1]
    tin = ncols // 2
    assert 2 * gs == d and 4 * ps == d and 4 * es == d and tin % 128 == 0

    w = {}
    halves = {n: s.reshape(2, s.shape[0] // 2, s.shape[1]) for n, s in shards.items()}

    def publish(names, arrays):
        for name, g in zip(names, arrays):
            g = g.reshape(N_CHIPS, 2 * g.shape[2], g.shape[3])
            if name in ROW_SHARDED:
                g = g.reshape(N_CHIPS * g.shape[1], g.shape[2])
            elif name == "w_in":
                g = jnp.transpose(g, (1, 0, 2)).reshape(g.shape[1], N_CHIPS * g.shape[2])
            w[name] = g

    class GatherPipe:
        def __init__(self, names):
            self.names = names

        def ici(self):
            self.first = _gather_ici([halves[n] for n in self.names])
            return self.first

        def d2d(self):
            self.second = _gather_d2d(self.first.results)
            return self.second

        def publish(self):
            publish(self.names, self.second.results)

    class GradPipe:
        def __init__(self, names):
            self.names = names

        def exchange(self, grads):
            self.grads = [_slotted(n, g) for n, g in zip(self.names, grads)]
            self.x = _exchange_halves(self.grads)
            return self.x

        def scatter(self):
            sums = [_chip_sum("chip_sum_" + n, g, got, core)
                    for n, g, got in zip(self.names, self.grads, self.x.results)]
            self.s = _scatter_ici(sums)
            return self.s

        def forward(self):
            self.f = _scatter_d2d(self.s.results)
            return self.f

        def terms(self):
            return dict(zip(self.names, self.f.results))

    publish(GATHER_FIRST, _all_gather_weights([halves[n] for n in GATHER_FIRST]))
    g_in, g_proj, g_ple = GatherPipe(["w_in", "w_gate"]), GatherPipe(["w_proj_a", "w_proj_b", "w_out"]), \
        GatherPipe(["w_ple_gate", "w_ple_proj"])
    g_down2, g_up2 = GatherPipe(["ffn2_w_down"]), GatherPipe(["ffn2_w_gu"])
    h1, ffn1_saved = _ffn_fwd("ffn1", xt, small["ffn1_norm"], w["ffn1_w_gu"], w["ffn1_w_down"],
                              {"up": lambda: [g_in.ici()], "down": lambda: [g_in.d2d(), g_proj.ici()]})
    g_in.publish()
    w_in, wgate = w["w_in"], w["w_gate"]
    un = _rms_fwd("mix_norm", h1, small["mix_norm"])
    (qkv,) = _mm(
        "qkv", "nn", (nt, 2, 1),
        [(un, row, w_in, pl.BlockSpec((d, tin), lambda i, j, k: (0, j)))], [],
        [(_sds((t, ncols), BF16), pl.BlockSpec((tm, tin), lambda i, j, k: (i, j)))], (tm, tin), _cast_epilogue,
        j_outer=True, comms=[g_proj.d2d(), g_ple.ici()])
    g_proj.publish()
    wpa, wpb, wout = w["w_proj_a"], w["w_proj_b"], w["w_out"]

    def gate_epilogue(accs, extras, outs, ij):
        outs[0][...] = jax.nn.sigmoid(accs[0]).astype(BF16)

    (gates,) = _mm(
        "gate", "nn", (nt, 4, 1),
        [(un, row, wgate, pl.BlockSpec((None, d, gs), lambda i, j, k: (j, 0, 0)))], [],
        [(_sds((2, t, d), BF16), pl.BlockSpec((None, tm, gs), lambda i, j, k: (j // 2, i, j % 2)))],
        (tm, gs), gate_epilogue, j_outer=True, chunked=True, comms=[g_ple.d2d(), g_down2.ici()])
    g_ple.publish()
    wpg, wpe = w["w_ple_gate"], w["w_ple_proj"]

    bias_a = _pair_bias(_bias_a(small["a_rel_bias"][0]))
    bias_b = _pair_bias(_bias_b())
    sink_a = _pair_rows(jnp.full((N_HEADS, 128), NEG_INF, F32))
    sink_b = _pair_rows(jnp.broadcast_to(small["b_sinks"][0][:, None], (N_HEADS, 128)))
    gqa, gka, gqb, gkb = [jnp.tile(small[k], (1, 2)) for k in ("a_q_norm", "a_k_norm", "b_q_norm", "b_k_norm")]
    ya, lse_a = _attn_fwd("attn_a_fwd", qkv, bias_a, sink_a, gqa, gka, ATTN_A, n_batch,
                          comms=[g_down2.d2d(), g_up2.ici()])
    g_down2.publish()
    yb, lse_b = _attn_fwd("attn_b_fwd", qkv, bias_b, sink_b, gqb, gkb, ATTN_B, n_batch, comms=[g_up2.d2d()])
    g_up2.publish()

    def merge_epilogue(accs, extras, outs, ij):
        pa, pb = accs
        outs[0][...] = (extras[0][...].astype(F32) * pa + extras[1][...].astype(F32) * pb).astype(BF16)
        outs[1][...] = pa.astype(BF16)
        outs[2][...] = pb.astype(BF16)

    y_spec = pl.BlockSpec((tm, A_WIDTH), lambda i, j, k: (i, 0))
    proj_spec = pl.BlockSpec((None, A_WIDTH, ps), lambda i, j, k: (j, 0, 0))
    tile_ps = pl.BlockSpec((tm, ps), lambda i, j, k: (i, j))
    merged, pa, pb = _mm(
        "proj_merge", "nn", (nt, 4, 1),
        [(ya, y_spec, wpa, proj_spec), (yb, y_spec, wpb, proj_spec)],
        [(gates, pl.BlockSpec((None, tm, ps), lambda i, j, k: (0, i, j))),
         (gates, pl.BlockSpec((None, tm, ps), lambda i, j, k: (1, i, j)))],
        [(_sds((t, d), BF16), tile_ps)] * 3, (tm, ps), merge_epilogue)

    def residual_epilogue(accs, extras, outs, ij):
        outs[0][...] = extras[0][...] + accs[0]

    (h2,) = _mm(
        "out_proj", "nn", (nt, 1, 1),
        [(merged, row, wout, pl.BlockSpec((d, d), lambda i, j, k: (0, 0)))],
        [(h1, row)], [(_sds((t, d), F32), row)], (tm, d), residual_epilogue)

    h3, ffn2_saved = _ffn_fwd("ffn2", h2, small["ffn2_norm"], w["ffn2_w_gu"], w["ffn2_w_down"], {})
    n3 = _rms_fwd("ple_norm", h3, small["ple_norm"])
    tile_es = pl.BlockSpec((tm, es), lambda i, j, k: (i, j))
    (pe,) = _mm(
        "ple_embed", "nn", (nt, 4, 1),
        [(pt, pl.BlockSpec((tm, pdim), lambda i, j, k: (i, 0)), wpe, pl.BlockSpec((None, pdim, es), lambda i, j, k: (j, 0, 0)))],
        [], [(_sds((t, d), F32), tile_es)], (tm, es), _cast_epilogue)

    th = _pick(d, 512)

    def head_epilogue(accs, extras, outs, ij):
        h3_ref, pe_ref, tgt_ref = extras
        dy_ref, dpe_ref, dz_ref, loss_ref = outs
        pg = jax.nn.sigmoid(accs[0])
        pev = pe_ref[...]
        diff = h3_ref[...] + pg * pev - tgt_ref[...]
        dy = diff * (1.0 / d)
        dy_ref[...] = dy
        dpe_ref[...] = (dy * pg).astype(BF16)
        dz_ref[...] = (dy * pev * pg * (1.0 - pg)).astype(BF16)
        _accumulate(loss_ref, jnp.full(loss_ref.shape, jnp.sum(diff * diff), F32), (ij[0] == 0) & (ij[1] == 0))

    tile_h = pl.BlockSpec((tm, th), lambda i, j, k: (i, j))
    dy, dpe, dz, loss_acc = _mm(
        "ple_gate_loss", "nn", (nt, d // th, 1),
        [(n3, row, wpg, pl.BlockSpec((d, th), lambda i, j, k: (0, j)))],
        [(h3, tile_h), (pe, tile_h), (tgt, tile_h)],
        [(_sds((t, d), F32), tile_h), (_sds((t, d), BF16), tile_h), (_sds((t, d), BF16), tile_h),
         (_sds((8, 128), F32), pl.BlockSpec((8, 128), lambda i, j, k: (0, 0)))],
        (tm, th), head_epilogue, j_outer=True, chunked=True)
    loss = 0.5 * loss_acc[0, 0] / d

    nk = t // tk
    (dwpe,) = _mm(
        "d_w_ple_proj", "tn", (1, 4, nk),
        [(pt, pl.BlockSpec((tk, pdim), lambda i, j, k: (k, 0)), dpe, pl.BlockSpec((tk, es), lambda i, j, k: (k, j)))],
        [], [(_sds((4, pdim, es), BF16), pl.BlockSpec((None, pdim, es), lambda i, j, k: (j, 0, 0)))],
        (pdim, es), _cast_epilogue)

    def dense_grad(name, a, dyb):
        (res,) = _mm(
            name, "tn", (1, d // th, nk),
            [(a, pl.BlockSpec((tk, d), lambda i, j, k: (k, 0)), dyb, pl.BlockSpec((tk, th), lambda i, j, k: (k, j)))],
            [], [(_sds((d, d), BF16), pl.BlockSpec((d, th), lambda i, j, k: (0, j)))], (d, th), _cast_epilogue)
        return res

    dwpg = dense_grad("d_w_ple_gate", n3, dz)
    tmn = _pick(t, 1024, 8)
    extras, outs = _rms_bwd_io(h3, small["ple_norm"], dy, tmn)
    dh3, dh3_b, d_ple_norm = _mm(
        "d_ple_norm", "nt", (t // tmn, 1, 1),
        [(dz, pl.BlockSpec((tmn, d), lambda i, j, k: (i, 0)), wpg, pl.BlockSpec((d, d), lambda i, j, k: (0, 0)))],
        extras, outs, (tmn, d), _rms_bwd_epilogue)

    late = GradPipe(GATHER_LATE)
    proj = GradPipe(["w_proj_a", "w_proj_b", "w_out"])
    dh2, dh2_b, d_ffn2_norm, dwgu2, dwd2 = _ffn_bwd(
        "ffn2", dh3, dh3_b, h2, small["ffn2_norm"], w["ffn2_w_gu"], w["ffn2_w_down"], ffn2_saved,
        {"dnorm": lambda dwgu, dwd: [late.exchange([dwgu, dwd, dwpg, dwpe])]})

    def dmerge_epilogue(accs, extras, outs, ij):
        dmo = accs[0]
        g_ref, pa_ref, pb_ref = extras
        dg_ref, dpa_ref, dpb_ref = outs
        ga = g_ref[0].astype(F32)
        gb = g_ref[1].astype(F32)
        dg_ref[0] = (dmo * pa_ref[...].astype(F32) * ga * (1.0 - ga)).astype(BF16)
        dg_ref[1] = (dmo * pb_ref[...].astype(F32) * gb * (1.0 - gb)).astype(BF16)
        dpa_ref[...] = (dmo * ga).astype(BF16)
        dpb_ref[...] = (dmo * gb).astype(BF16)

    g_spec = pl.BlockSpec((2, tm, th), lambda i, j, k: (0, i, j))
    dgates, dpa, dpb = _mm(
        "d_merge", "nt", (nt, d // th, 1),
        [(dh2_b, row, wout, pl.BlockSpec((th, d), lambda i, j, k: (j, 0)))],
        [(gates, g_spec), (pa, tile_h), (pb, tile_h)],
        [(_sds((2, t, d), BF16), g_spec), (_sds((t, d), BF16), tile_h), (_sds((t, d), BF16), tile_h)],
        (tm, th), dmerge_epilogue, j_outer=True, chunked=True)
    dwout = dense_grad("d_w_out", merged, dh2_b)

    yk_spec = pl.BlockSpec((tk, A_WIDTH), lambda i, j, k: (k, 0))
    dk_spec = pl.BlockSpec((tk, ps), lambda i, j, k: (k, j))
    dproj = (_sds((4, A_WIDTH, ps), BF16), proj_spec)
    dwpa, dwpb = _mm(
        "d_w_proj", "tn", (1, 4, nk),
        [(ya, yk_spec, dpa, dk_spec), (yb, yk_spec, dpb, dk_spec)], [], [dproj, dproj], (A_WIDTH, ps), _cast_epilogue)
    dproj_a = pl.BlockSpec((tm, ps), lambda i, j, k: (i, k))
    wproj_k = pl.BlockSpec((None, A_WIDTH, ps), lambda i, j, k: (k, 0, 0))
    dya, dyb = _mm(
        "d_attn_out", "nt", (nt, 1, 4),
        [(dpa, dproj_a, wpa, wproj_k), (dpb, dproj_a, wpb, wproj_k)], [],
        [(_sds((t, A_WIDTH), BF16), y_spec)] * 2, (tm, A_WIDTH), _cast_epilogue,
        comms=[proj.exchange([dwpa, dwpb, dwout])])

    dqa, dka, dva, dbias_a, _, dgqa, dgka = _attn_bwd(
        "attn_a_bwd", qkv, bias_a, sink_a, gqa, gka, ya, dya, lse_a, ATTN_A, n_batch, True,
        comms=[late.scatter(), proj.scatter()])
    dqb, dkb, dvb, _, dsink_b, dgqb, dgkb = _attn_bwd(
        "attn_b_bwd", qkv, bias_b, sink_b, gqb, gkb, yb, dyb, lse_b, ATTN_B, n_batch, False,
        comms=[late.forward(), proj.forward()])
    dqkv = jnp.concatenate([dqa, dka, dva, dqb, dkb, dvb], axis=1)

    (dwgate,) = _mm(
        "d_w_gate", "tn", (1, 4, nk),
        [(un, pl.BlockSpec((tk, d), lambda i, j, k: (k, 0)),
          dgates, pl.BlockSpec((None, tk, gs), lambda i, j, k: (j // 2, k, j % 2)))],
        [], [(_sds((4, d, gs), BF16), pl.BlockSpec((None, d, gs), lambda i, j, k: (j, 0, 0)))], (d, gs), _cast_epilogue)
    (dwin,) = _mm(
        "d_w_in", "tn", (1, 2, nk),
        [(un, pl.BlockSpec((tk, d), lambda i, j, k: (k, 0)), dqkv, pl.BlockSpec((tk, tin), lambda i, j, k: (k, j)))],
        [], [(_sds((d, ncols), BF16), pl.BlockSpec((d, tin), lambda i, j, k: (0, j)))], (d, tin), _cast_epilogue)

    mixer = GradPipe(["w_in", "w_gate"])
    extras, outs = _rms_bwd_io(h1, small["mix_norm"], dh2, tmn)
    dh1, dh1_b, d_mix_norm = _mm(
        "d_mix_norm", "nt", (t // tmn, 1, 6),
        [(dgates, pl.BlockSpec((None, tmn, gs), lambda i, j, k: (jnp.minimum(k, 3) // 2, i, jnp.minimum(k, 3) % 2)),
          wgate, pl.BlockSpec((None, d, gs), lambda i, j, k: (jnp.minimum(k, 3), 0, 0))),
         (dqkv, pl.BlockSpec((tmn, tin), lambda i, j, k: (i, jnp.maximum(k - 4, 0))),
          w_in, pl.BlockSpec((d, tin), lambda i, j, k: (0, jnp.maximum(k - 4, 0))))],
        extras, outs, (tmn, d), _rms_bwd_epilogue, steps=[4, 2],
        comms=[mixer.exchange([dwin, dwgate])])

    up1 = GradPipe(["ffn1_w_gu"])
    down1 = GradPipe(["ffn1_w_down"])
    dx, _, d_ffn1_norm, _, _ = _ffn_bwd(
        "ffn1", dh1, dh1_b, xt, small["ffn1_norm"], w["ffn1_w_gu"], w["ffn1_w_down"], ffn1_saved,
        {"dact": lambda: [mixer.scatter()],
         "dwgu": lambda: [mixer.forward()],
         "dwd": lambda dwgu: [up1.exchange([dwgu])],
         "dnorm": lambda dwgu, dwd: [up1.scatter(), down1.exchange([dwd])]})
    _run_comms("grad_tail_scatter", [up1.forward(), down1.scatter()])
    _run_comms("grad_tail_forward", [down1.forward()])
    terms = {**late.terms(), **proj.terms(), **mixer.terms(), **up1.terms(), **down1.terms()}

    def fold(v):
        return v[0, :HEAD_DIM] + v[0, HEAD_DIM:]

    small_grads = {"ffn1_norm": d_ffn1_norm, "mix_norm": d_mix_norm, "ffn2_norm": d_ffn2_norm,
                   "ple_norm": d_ple_norm, "a_q_norm": fold(dgqa), "a_k_norm": fold(dgka),
                   "b_q_norm": fold(dgqb), "b_k_norm": fold(dgkb), "a_rel_bias": _rel_bias_grad(_unpair_bias(dbias_a)),
                   "b_sinks": jnp.sum(dsink_b, axis=1)}
    return loss, dx, terms, small_grads


def kernel(x, p, ffn1_norm, ffn1_w_gu, ffn1_w_down, mix_norm, w_in, a_q_norm, a_k_norm, a_rel_bias, b_q_norm, b_k_norm, b_sinks, w_gate, w_proj_a, w_proj_b, w_out, ffn2_norm, ffn2_w_gu, ffn2_w_down, ple_norm, w_ple_gate, w_ple_proj, loss_target, m_ffn1_norm, m_ffn1_w_gu, m_ffn1_w_down, m_mix_norm, m_w_in, m_a_q_norm, m_a_k_norm, m_a_rel_bias, m_b_q_norm, m_b_k_norm, m_b_sinks, m_w_gate, m_w_proj_a, m_w_proj_b, m_w_out, m_ffn2_norm, m_ffn2_w_gu, m_ffn2_w_down, m_ple_norm, m_w_ple_gate, m_w_ple_proj, v_ffn1_norm, v_ffn1_w_gu, v_ffn1_w_down, v_mix_norm, v_w_in, v_a_q_norm, v_a_k_norm, v_a_rel_bias, v_b_q_norm, v_b_k_norm, v_b_sinks, v_w_gate, v_w_proj_a, v_w_proj_b, v_w_out, v_ffn2_norm, v_ffn2_w_gu, v_ffn2_w_down, v_ple_norm, v_w_ple_gate, v_w_ple_proj):
    given = dict(locals())
    n_batch, s, d = x.shape
    t = n_batch * s
    xt = x.reshape(t, d)
    pt = p.reshape(t, p.shape[-1])
    tgt = loss_target.reshape(t, d)

    shards = {}
    for name in BIG:
        (shards[name],) = _ew("cast_" + name, lambda v: (v,), [given[name][0]], [BF16])
    small = {name: given[name] for name in SMALL}
    core = lax.axis_index("c").astype(jnp.int32).reshape(1)
    loss, dx, terms, small_grads = _local_step(xt, pt, tgt, n_batch, shards, small, core)

    grads, deltas, new_m, new_v = {}, {}, {}, {}
    for name in BIG:
        gw, dl, nm, nv = _adamw_terms("adamw_" + name, terms[name], given[name][0], given["m_" + name][0],
                                      given["v_" + name][0])
        grads[name], deltas[name], new_m[name], new_v[name] = gw[None], dl[None], nm[None], nv[None]

    small_shapes = [given[name].shape for name in SMALL] + [()]
    g_pack = _all_reduce_small(_pack([small_grads[name] for name in SMALL] + [loss]))
    zero = jnp.zeros((), F32)
    w_pack = _pack([given[name] for name in SMALL] + [zero])
    m_pack = _pack([given["m_" + name] for name in SMALL] + [zero])
    v_pack = _pack([given["v_" + name] for name in SMALL] + [zero])
    d_pack, nm_pack, nv_pack = _ew("adamw_small", lambda wv, gv, mv, vv: _adamw_math(wv, gv, mv, vv),
                                   [w_pack, g_pack, m_pack, v_pack], [F32] * 3)
    g_small = _unpack(g_pack, small_shapes)
    loss_total = g_small[-1]
    for name, gv, dv, mv, vv in zip(SMALL, g_small, _unpack(d_pack, small_shapes), _unpack(nm_pack, small_shapes),
                                    _unpack(nv_pack, small_shapes)):
        grads[name], deltas[name], new_m[name], new_v[name] = gv, dv, mv, vv

    return (loss_total, dx.reshape(x.shape), *[grads[n] for n in WEIGHTS], *[deltas[n] for n in WEIGHTS],
            *[new_m[n] for n in WEIGHTS], *[new_v[n] for n in WEIGHTS])
```

```python
import functools

import numpy as np
import jax
import jax.numpy as jnp
from jax import lax
from jax.experimental import pallas as pl
from jax.experimental.pallas import tpu as pltpu

F32 = jnp.float32
BF16 = jnp.bfloat16

CHUNK = 64
HEAD_DIM = 64
A_PREV_CHUNKS = 8
A_MAX_REL = 128
N_HEADS = 8
B_KV_HEADS = 2
B_PREV_CHUNKS = 2
A_WIDTH = N_HEADS * HEAD_DIM
B_KV_WIDTH = B_KV_HEADS * HEAD_DIM
EPS = 1e-6
NEG_INF = -1e30
ATTN_SCALE = HEAD_DIM ** -0.5
Q_BLOCK = 128
PAIR = 2 * HEAD_DIM

ADAM_LR = 0.001
ADAM_B1 = 0.9
ADAM_B2 = 0.999
ADAM_EPS = 1e-08
ADAM_WD = 0.01
ADAM_STEP = 10

N_CHIPS = 4
N_DEV = 8
VMEM_LIMIT_V7X = 56 * 1024 * 1024
MESH = pl.DeviceIdType.MESH
ANY = pl.BlockSpec(memory_space=pl.ANY)

_DN = {
    "nn": (((1,), (0,)), ((), ())),
    "nt": (((1,), (1,)), ((), ())),
    "tn": (((0,), (0,)), ((), ())),
}


def _pick(n, target, mult=128):
    best = None
    for d in range(mult, min(n, target) + 1, mult):
        if n % d == 0:
            best = d
    return n if best is None else best


def _dot(a, b, mode):
    return lax.dot_general(a.astype(BF16), b.astype(BF16), _DN[mode], preferred_element_type=F32)


def _params():
    return pltpu.CompilerParams(vmem_limit_bytes=VMEM_LIMIT_V7X)


class _Comm:
    def __init__(self, ins, outs, aliases, sems, start, finish):
        self.ins, self.outs, self.aliases, self.sems = list(ins), list(outs), dict(aliases), list(sems)
        self.start, self.finish = start, finish
        self.results = None


class _CommPlumbing:
    def __init__(self, comms, n_in, n_out, n_scratch):
        self.comms = list(comms)
        self.n_in, self.n_out, self.n_scratch = n_in, n_out, n_scratch
        self.args = [a for cm in self.comms for a in cm.ins]
        self.out_shape = [o for cm in self.comms for o in cm.outs]
        self.scratch = [s for cm in self.comms for s in cm.sems]
        self.aliases = {}
        i0, o0 = n_in, n_out
        for cm in self.comms:
            for a, b in cm.aliases.items():
                self.aliases[i0 + a] = o0 + b
            i0 += len(cm.ins)
            o0 += len(cm.outs)

    def run(self, in_refs, out_refs, scratch_refs, first, last):
        if not self.comms:
            return
        parts = []
        i0, o0, s0 = self.n_in, self.n_out, self.n_scratch
        for cm in self.comms:
            parts.append((in_refs[i0:i0 + len(cm.ins)], out_refs[o0:o0 + len(cm.outs)],
                          scratch_refs[s0:s0 + len(cm.sems)]))
            i0 += len(cm.ins)
            o0 += len(cm.outs)
            s0 += len(cm.sems)

        @pl.when(first)
        def _():
            for cm, part in zip(self.comms, parts):
                cm.start(*part)

        @pl.when(last)
        def _():
            for cm, part in zip(self.comms, parts):
                cm.finish(*part)

    def deliver(self, results):
        o0 = self.n_out
        for cm in self.comms:
            cm.results = list(results[o0:o0 + len(cm.outs)])
            o0 += len(cm.outs)
        return list(results[:self.n_out])


def _swap_ij(spec):
    index_map = spec.index_map
    return pl.BlockSpec(spec.block_shape, lambda j, i, k: index_map(i, j, k))


MXU_COLUMNS_V7X = 256


def _mm(name, mode, grid, pairs, extras, outs, acc_shape, epilogue, steps=None, comms=(), j_outer=False,
        chunked=False):
    ni, nj, nk = grid
    n_in = 2 * len(pairs) + len(extras)
    n_out = len(outs)
    tn = acc_shape[1]
    col_chunks = None
    if chunked:
        assert nk == 1 and steps is None and mode in ("nn", "nt")
        col_chunks = [(c0, min(MXU_COLUMNS_V7X, tn - c0)) for c0 in range(0, tn, MXU_COLUMNS_V7X)]
    n_acc = 0 if chunked else (len(pairs) if steps is None else 1)
    plumb = _CommPlumbing(comms, n_in, n_out, n_acc)
    n_all_in = n_in + len(plumb.args)
    n_all_out = n_out + len(plumb.out_shape)
    if j_outer:
        grid = (nj, ni, nk)
        pairs = [(a, _swap_ij(a_spec), b, _swap_ij(b_spec)) for a, a_spec, b, b_spec in pairs]
        extras = [(e, _swap_ij(e_spec)) for e, e_spec in extras]
        outs = [(o, _swap_ij(o_spec)) for o, o_spec in outs]

    def body(*refs):
        in_refs = refs[:n_all_in]
        out_refs = refs[n_all_in:n_all_in + n_all_out]
        scratch = refs[n_all_in + n_all_out:]
        accs = scratch[:n_acc]
        i = pl.program_id(1 if j_outer else 0)
        j = pl.program_id(0 if j_outer else 1)
        k = pl.program_id(2)

        def contrib(p, acc):
            acc[...] += _dot(in_refs[2 * p][...], in_refs[2 * p + 1][...], mode)

        if col_chunks:
            def cols(ref, c0, cs):
                if ref.shape[-1] != tn:
                    return ref
                return ref.at[(slice(None),) * (len(ref.shape) - 1) + (pl.ds(c0, cs),)]

            lhs = [in_refs[2 * p][...] for p in range(len(pairs))]
            for ci, (c0, cs) in enumerate(col_chunks):
                vals = []
                for p in range(len(pairs)):
                    b_ref = in_refs[2 * p + 1]
                    rhs = b_ref[:, c0:c0 + cs] if mode == "nn" else b_ref[c0:c0 + cs, :]
                    vals.append(_dot(lhs[p], rhs, mode))
                epilogue(vals, [cols(r, c0, cs) for r in in_refs[2 * len(pairs):n_in]],
                         [cols(r, c0, cs) for r in out_refs[:n_out]], (i, j * len(col_chunks) + ci))
        else:
            @pl.when(k == 0)
            def _():
                for acc in accs:
                    acc[...] = jnp.zeros(acc.shape, F32)

            if steps is None:
                for p in range(len(pairs)):
                    contrib(p, accs[p])
            else:
                lo = 0
                for p, n in enumerate(steps):
                    pl.when((k >= lo) & (k < lo + n))(functools.partial(contrib, p, accs[0]))
                    lo += n

            @pl.when(k == nk - 1)
            def _():
                epilogue([acc[...] for acc in accs], in_refs[2 * len(pairs):n_in], out_refs[:n_out], (i, j))

        plumb.run(in_refs, out_refs, scratch, (i == 0) & (j == 0) & (k == 0),
                  (i == ni - 1) & (j == nj - 1) & (k == nk - 1))

    args, in_specs = [], []
    for a, a_spec, b, b_spec in pairs:
        args += [a, b]
        in_specs += [a_spec, b_spec]
    for e, e_spec in extras:
        args.append(e)
        in_specs.append(e_spec)
    res = pl.pallas_call(
        body,
        name=name,
        grid=grid,
        in_specs=in_specs + [ANY] * len(plumb.args),
        out_specs=[s for _, s in outs] + [ANY] * len(plumb.out_shape),
        out_shape=[o for o, _ in outs] + plumb.out_shape,
        scratch_shapes=[pltpu.VMEM(acc_shape, F32) for _ in range(n_acc)] + plumb.scratch,
        input_output_aliases=plumb.aliases,
        compiler_params=_params(),
    )(*args, *plumb.args)
    return plumb.deliver(res)


def _sds(shape, dtype):
    return jax.ShapeDtypeStruct(shape, dtype)


def _accumulate(ref, value, first):
    @pl.when(first)
    def _():
        ref[...] = value

    @pl.when(jnp.logical_not(first))
    def _():
        ref[...] += value


def _rms_fwd(name, x, gain):
    t, d = x.shape
    tm = _pick(t, 512, 8)

    def body(x_ref, g_ref, y_ref):
        xv = x_ref[...]
        rstd = lax.rsqrt(jnp.mean(xv * xv, axis=-1, keepdims=True) + EPS)
        y_ref[...] = (xv * rstd * g_ref[...]).astype(BF16)

    return pl.pallas_call(
        body, name=name, grid=(t // tm,),
        in_specs=[pl.BlockSpec((tm, d), lambda i: (i, 0)), pl.BlockSpec((1, d), lambda i: (0, 0))],
        out_specs=pl.BlockSpec((tm, d), lambda i: (i, 0)),
        out_shape=_sds((t, d), BF16),
        compiler_params=_params(),
    )(x, gain)


def _rms_bwd_epilogue(accs, extras, outs, ij):
    x_ref, g_ref, r_ref = extras
    dh_ref, dhb_ref, dg_ref = outs
    dn = accs[0]
    xv = x_ref[...]
    rstd = lax.rsqrt(jnp.mean(xv * xv, axis=-1, keepdims=True) + EPS)
    xhat = xv * rstd
    gd = dn * g_ref[...]
    dx = rstd * (gd - xhat * jnp.mean(gd * xhat, axis=-1, keepdims=True))
    dh = r_ref[...] + dx
    dh_ref[...] = dh
    dhb_ref[...] = dh.astype(BF16)
    _accumulate(dg_ref, jnp.sum(dn * xhat, axis=0, keepdims=True), ij[0] == 0)


def _rms_bwd_io(x, gain, dres, tm):
    t, d = x.shape
    row = pl.BlockSpec((tm, d), lambda i, j, k: (i, 0))
    extras = [(x, row), (gain, pl.BlockSpec((1, d), lambda i, j, k: (0, 0))), (dres, row)]
    outs = [(_sds((t, d), F32), row), (_sds((t, d), BF16), row),
            (_sds((1, d), F32), pl.BlockSpec((1, d), lambda i, j, k: (0, 0)))]
    return extras, outs


def _ffn_fwd(tag, h, gain, wgu, wd, hooks):
    t, d = h.shape
    fs = wgu.shape[2]
    f = 2 * fs
    tm = _pick(t, 512, 8)
    n = _rms_fwd(tag + "_norm", h, gain)

    def up_epilogue(accs, extras, outs, ij):
        g, u = accs
        gu_ref, a_ref = outs
        gu_ref[0] = g.astype(BF16)
        gu_ref[1] = u.astype(BF16)
        a_ref[...] = (g * jax.nn.sigmoid(g) * u).astype(BF16)

    a_spec = pl.BlockSpec((tm, d), lambda i, j, k: (i, 0))
    gu, a = _mm(
        tag + "_up", "nn", (t // tm, 2, 1),
        [(n, a_spec, wgu, pl.BlockSpec((None, d, fs), lambda i, j, k: (j, 0, 0))),
         (n, a_spec, wgu, pl.BlockSpec((None, d, fs), lambda i, j, k: (j + 2, 0, 0)))],
        [],
        [(_sds((2, t, f), BF16), pl.BlockSpec((2, tm, fs), lambda i, j, k: (0, i, j))),
         (_sds((t, f), BF16), pl.BlockSpec((tm, fs), lambda i, j, k: (i, j)))],
        (tm, fs), up_epilogue, comms=hooks.get("up", lambda: ())(), j_outer=True, chunked=True)

    def down_epilogue(accs, extras, outs, ij):
        outs[0][...] = extras[0][...] + 0.5 * accs[0]

    row = pl.BlockSpec((tm, d), lambda i, j, k: (i, 0))
    (h_new,) = _mm(
        tag + "_down", "nn", (t // tm, 1, 1),
        [(a, pl.BlockSpec((tm, f), lambda i, j, k: (i, 0)), wd, pl.BlockSpec((f, d), lambda i, j, k: (0, 0)))],
        [(h, row)], [(_sds((t, d), F32), row)], (tm, d), down_epilogue, comms=hooks.get("down", lambda: ())())
    return h_new, (n, gu, a)


def _ffn_bwd(tag, dh, dh_b, h, gain, wgu, wd, saved, hooks):
    n, gu, a = saved
    t, d = h.shape
    fs = wgu.shape[2]
    f = 2 * fs
    tm = _pick(t, 512, 8)
    tk = _pick(t, 512, 8)

    def dact_epilogue(accs, extras, outs, ij):
        da = 0.5 * accs[0]
        g = extras[0][0].astype(F32)
        u = extras[0][1].astype(F32)
        sg = jax.nn.sigmoid(g)
        outs[0][0] = (da * u * sg * (1.0 + g * (1.0 - sg))).astype(BF16)
        outs[0][1] = (da * g * sg).astype(BF16)

    gu_spec = pl.BlockSpec((2, tm, fs), lambda i, j, k: (0, i, j))
    (dgu,) = _mm(
        tag + "_dact", "nt", (t // tm, 2, 1),
        [(dh_b, pl.BlockSpec((tm, d), lambda i, j, k: (i, 0)), wd, pl.BlockSpec((fs, d), lambda i, j, k: (j, 0)))],
        [(gu, gu_spec)], [(_sds((2, t, f), BF16), gu_spec)], (tm, fs), dact_epilogue, j_outer=True, chunked=True,
        comms=hooks.get("dact", lambda: ())())

    def cast_epilogue(accs, extras, outs, ij):
        outs[0][...] = accs[0].astype(BF16)

    (dwgu,) = _mm(
        tag + "_dwgu", "tn", (1, 4, t // tk),
        [(n, pl.BlockSpec((tk, d), lambda i, j, k: (k, 0)),
          dgu, pl.BlockSpec((None, tk, fs), lambda i, j, k: (j // 2, k, j % 2)))],
        [], [(_sds((4, d, fs), BF16), pl.BlockSpec((None, d, fs), lambda i, j, k: (j, 0, 0)))], (d, fs), cast_epilogue,
        comms=hooks.get("dwgu", lambda: ())())

    def half_epilogue(accs, extras, outs, ij):
        outs[0][...] = (0.5 * accs[0]).astype(BF16)

    (dwd,) = _mm(
        tag + "_dwd", "tn", (2, 1, t // tk),
        [(a, pl.BlockSpec((tk, fs), lambda i, j, k: (k, i)), dh_b, pl.BlockSpec((tk, d), lambda i, j, k: (k, 0)))],
        [], [(_sds((f, d), BF16), pl.BlockSpec((fs, d), lambda i, j, k: (i, 0)))], (fs, d), half_epilogue,
        comms=hooks.get("dwd", lambda g: ())(dwgu))

    tmn = _pick(t, 1024, 8)
    extras, outs = _rms_bwd_io(h, gain, dh, tmn)
    dh_in, dh_in_b, dgain = _mm(
        tag + "_dnorm", "nt", (t // tmn, 1, 4),
        [(dgu, pl.BlockSpec((None, tmn, fs), lambda i, j, k: (k // 2, i, k % 2)),
          wgu, pl.BlockSpec((None, d, fs), lambda i, j, k: (k, 0, 0)))],
        extras, outs, (tmn, d), _rms_bwd_epilogue, comms=hooks.get("dnorm", lambda g, w: ())(dwgu, dwd))
    return dh_in, dh_in_b, dgain, dwgu, dwd


def _lane_lo(shape):
    return lax.broadcasted_iota(jnp.int32, shape, 1) < HEAD_DIM


def _pair_norm(xv, gain):
    lo = _lane_lo(xv.shape)
    x2 = xv * xv
    ms_lo = jnp.sum(jnp.where(lo, x2, 0.0), axis=-1, keepdims=True) * (1.0 / HEAD_DIM)
    ms_hi = jnp.sum(jnp.where(lo, 0.0, x2), axis=-1, keepdims=True) * (1.0 / HEAD_DIM)
    rstd = jnp.where(lo, lax.rsqrt(ms_lo + EPS), lax.rsqrt(ms_hi + EPS))
    xhat = xv * rstd
    return xhat * gain, xhat, rstd


def _pair_norm_bwd(dn, xhat, rstd, gain):
    lo = _lane_lo(dn.shape)
    gd = dn * gain
    t = gd * xhat
    m_lo = jnp.sum(jnp.where(lo, t, 0.0), axis=-1, keepdims=True) * (1.0 / HEAD_DIM)
    m_hi = jnp.sum(jnp.where(lo, 0.0, t), axis=-1, keepdims=True) * (1.0 / HEAD_DIM)
    dx = rstd * (gd - xhat * jnp.where(lo, m_lo, m_hi))
    return dx, jnp.sum(dn * xhat, axis=0, keepdims=True)


def _half(xv, hi):
    lo = _lane_lo(xv.shape)
    return jnp.where(lo, 0, xv) if hi else jnp.where(lo, xv, 0)


def _attn_window(i, prev):
    q0 = i * Q_BLOCK
    start = jnp.maximum(q0 - prev, 0)
    off = start - (q0 - prev)
    return pl.multiple_of(start, Q_BLOCK), pl.multiple_of(off, Q_BLOCK)


def _attn_specs(cfg, s, nq):
    kw = cfg["kw"]
    q_spec = pl.BlockSpec((Q_BLOCK, A_WIDTH), lambda b, i: (b * nq + i, cfg["qblk"]))
    k_spec = pl.BlockSpec((s, kw), lambda b, i: (b, cfg["kblk"]))
    v_spec = pl.BlockSpec((s, kw), lambda b, i: (b, cfg["vblk"]))
    return q_spec, k_spec, v_spec


def _const_spec(shape):
    return pl.BlockSpec(shape, lambda b, i: (0,) * len(shape))


KEY_CHUNK = 128


def _pair_bias(bias_t):
    wext = bias_t.shape[1]
    return jnp.transpose(bias_t.reshape(N_HEADS // 2, 2, wext, Q_BLOCK), (0, 2, 1, 3)).reshape(
        N_HEADS // 2, wext, 2 * Q_BLOCK)


def _unpair_bias(db2):
    wext = db2.shape[1]
    return jnp.transpose(db2.reshape(N_HEADS // 2, wext, 2, Q_BLOCK), (0, 2, 1, 3)).reshape(N_HEADS, wext, Q_BLOCK)


def _pair_rows(rows):
    two = rows.reshape(N_HEADS // 2, 2 * rows.shape[1])
    return jnp.broadcast_to(two[:, None, :], (N_HEADS // 2, 8, two.shape[1]))


def _sub_lo(shape):
    return lax.broadcasted_iota(jnp.int32, shape, 0) < HEAD_DIM


def _by_half(lo_row, hi_row, rows):
    return jnp.where(_sub_lo((rows, lo_row.shape[1])), lo_row, hi_row)


def _stack_pair(xn, jq, group):
    parts = []
    for hq in range(2):
        hk = ((2 * jq + hq) // group) % 2
        xm = _half(xn, hq)
        if hq != hk:
            xm = pltpu.roll(xm, HEAD_DIM, 1)
        parts.append(xm)
    return jnp.concatenate(parts, axis=0).astype(BF16)


def _place_transposed(blk, dst_ref, c, heads, group):
    bt = blk.T
    lo = _sub_lo(bt.shape)
    for h in heads:
        src_hi = ((h // group) % 2) == 1
        part = jnp.where(lo, 0.0, bt) if src_hi else jnp.where(lo, bt, 0.0)
        if src_hi != (h % 2 == 1):
            part = pltpu.roll(part, HEAD_DIM, 0)
        dst_ref[h, c] = part.astype(BF16)


def _attn_fwd(name, qkv, bias2, sink2, gq, gk, cfg, n_batch, comms=()):
    t = qkv.shape[0]
    s = t // n_batch
    nq = s // Q_BLOCK
    nkc = s // KEY_CHUNK
    prev, group, kw = cfg["prev"], cfg["group"], cfg["kw"]
    n_chunks = (prev + Q_BLOCK) // KEY_CHUNK
    wext = bias2.shape[1]
    plumb = _CommPlumbing(comms, 7, 2, 2)
    n_all_in = 7 + len(plumb.args)
    n_all_out = 2 + len(plumb.out_shape)

    def body(*refs):
        q_ref, k_ref, v_ref, bias_ref, sink_ref, gq_ref, gk_ref = refs[:7]
        y_ref, lse_ref = refs[n_all_in:n_all_in + 2]
        kn_ref, vt_ref = refs[n_all_in + n_all_out:n_all_in + n_all_out + 2]
        i = pl.program_id(1)
        plumb.run(refs[:n_all_in], refs[n_all_in:n_all_in + n_all_out], refs[n_all_in + n_all_out:],
                  (pl.program_id(0) == 0) & (i == 0), (pl.program_id(0) == n_batch - 1) & (i == nq - 1))

        @pl.when(i == 0)
        def _():
            for jk in range(kw // PAIR):
                cols = pl.ds(jk * PAIR, PAIR)
                heads = [h for h in range(N_HEADS) if (h // group) // 2 == jk]
                kn, _, _ = _pair_norm(k_ref[:, cols].astype(F32), gk_ref[...])
                kn_ref[:, cols] = kn.astype(BF16)
                for c in range(nkc):
                    _place_transposed(v_ref[pl.ds(c * KEY_CHUNK, KEY_CHUNK), cols].astype(F32), vt_ref, c, heads, group)

        start, off = _attn_window(i, prev)
        c0 = start // KEY_CHUNK
        sub8 = lax.broadcasted_iota(jnp.int32, (N_HEADS, Q_BLOCK), 0)
        lse = jnp.zeros((N_HEADS, Q_BLOCK), F32)
        for jq in range(N_HEADS // 2):
            kcols = pl.ds((((2 * jq) // group) // 2) * PAIR, PAIR)
            qn, _, _ = _pair_norm(q_ref[:, pl.ds(jq * PAIR, PAIR)].astype(F32), gq_ref[...])
            qs = _stack_pair(qn * ATTN_SCALE, jq, group)
            m = sink_ref[jq, 0:1, :]
            l = jnp.ones((1, 2 * Q_BLOCK), F32)
            ot = jnp.zeros((PAIR, Q_BLOCK), F32)
            for c in range(n_chunks):
                rows = pl.ds(start + c * KEY_CHUNK, KEY_CHUNK)
                s2 = _dot(kn_ref[rows, kcols], qs, "nt") + bias_ref[jq, pl.ds(off + c * KEY_CHUNK, KEY_CHUNK), :]
                m_new = jnp.maximum(m, jnp.max(s2, axis=0, keepdims=True))
                alpha = jnp.exp(m - m_new)
                p = jnp.exp(s2 - m_new)
                l = alpha * l + jnp.sum(p, axis=0, keepdims=True)
                m = m_new
                pst = jnp.concatenate([p[:, :Q_BLOCK], p[:, Q_BLOCK:]], axis=0)
                vl = jnp.concatenate([vt_ref[2 * jq, c0 + c], vt_ref[2 * jq + 1, c0 + c]], axis=1)
                ot = ot * _by_half(alpha[:, :Q_BLOCK], alpha[:, Q_BLOCK:], PAIR) + _dot(vl, pst, "nn")
            inv = 1.0 / l
            ot = ot * _by_half(inv[:, :Q_BLOCK], inv[:, Q_BLOCK:], PAIR)
            y_ref[:, pl.ds(jq * PAIR, PAIR)] = ot.T.astype(BF16)
            lse2 = m + jnp.log(l)
            lse = jnp.where(sub8 == 2 * jq, lse2[:, :Q_BLOCK], lse)
            lse = jnp.where(sub8 == 2 * jq + 1, lse2[:, Q_BLOCK:], lse)
        lse_ref[...] = lse

    q_spec, k_spec, v_spec = _attn_specs(cfg, s, nq)
    res = pl.pallas_call(
        body, name=name, grid=(n_batch, nq),
        in_specs=[q_spec, k_spec, v_spec, _const_spec((N_HEADS // 2, wext, 2 * Q_BLOCK)),
                  _const_spec((N_HEADS // 2, 8, 2 * Q_BLOCK)), _const_spec((1, PAIR)), _const_spec((1, PAIR))]
        + [ANY] * len(plumb.args),
        out_specs=[pl.BlockSpec((Q_BLOCK, A_WIDTH), lambda b, i: (b * nq + i, 0)),
                   pl.BlockSpec((None, N_HEADS, Q_BLOCK), lambda b, i: (b * nq + i, 0, 0))]
        + [ANY] * len(plumb.out_shape),
        out_shape=[_sds((t, A_WIDTH), BF16), _sds((t // Q_BLOCK, N_HEADS, Q_BLOCK), F32)] + plumb.out_shape,
        scratch_shapes=[pltpu.VMEM((s, kw), BF16), pltpu.VMEM((N_HEADS, nkc, PAIR, KEY_CHUNK), BF16)] + plumb.scratch,
        input_output_aliases=plumb.aliases,
        compiler_params=_params(),
    )(qkv, qkv, qkv, bias2, sink2, gq, gk, *plumb.args)
    return plumb.deliver(res)


def _attn_bwd(name, qkv, bias2, sink2, gq, gk, y, dy, lse, cfg, n_batch, want_dbias, comms=()):
    t = qkv.shape[0]
    s = t // n_batch
    nq = s // Q_BLOCK
    nkc = s // KEY_CHUNK
    prev, group, kw = cfg["prev"], cfg["group"], cfg["kw"]
    w = prev + Q_BLOCK
    n_chunks = w // KEY_CHUNK
    wext = bias2.shape[1]
    plumb = _CommPlumbing(comms, 10, 7, 9)
    n_all_in = 10 + len(plumb.args)
    n_all_out = 7 + len(plumb.out_shape)

    def body(*refs):
        q_ref, k_ref, v_ref, bias_ref, sink_ref, gq_ref, gk_ref, y_ref, dy_ref, lse_ref = refs[:10]
        dq_ref, dk_ref, dv_ref, db_ref, dsink_ref, dgq_ref, dgk_ref = refs[n_all_in:n_all_in + 7]
        kn_ref, knt_ref, dkn_ref, dvs_ref, s_ref, dp_ref, pb_ref, dsb_ref, dst_ref = \
            refs[n_all_in + n_all_out:n_all_in + n_all_out + 9]
        b = pl.program_id(0)
        i = pl.program_id(1)
        first = (b == 0) & (i == 0)
        plumb.run(refs[:n_all_in], refs[n_all_in:n_all_in + n_all_out], refs[n_all_in + n_all_out:],
                  first, (b == n_batch - 1) & (i == nq - 1))

        @pl.when(i == 0)
        def _():
            for jk in range(kw // PAIR):
                cols = pl.ds(jk * PAIR, PAIR)
                heads = [h for h in range(N_HEADS) if (h // group) // 2 == jk]
                for c in range(nkc):
                    rows = pl.ds(c * KEY_CHUNK, KEY_CHUNK)
                    kn, _, _ = _pair_norm(k_ref[rows, cols].astype(F32), gk_ref[...])
                    kn_ref[rows, cols] = kn.astype(BF16)
                    _place_transposed(kn, knt_ref, c, heads, group)
            dkn_ref[...] = jnp.zeros(dkn_ref.shape, F32)
            dvs_ref[...] = jnp.zeros(dvs_ref.shape, F32)

        @pl.when(first)
        def _():
            db_ref[...] = jnp.zeros(db_ref.shape, F32)
            dsink_ref[...] = jnp.zeros(dsink_ref.shape, F32)
            dgq_ref[...] = jnp.zeros(dgq_ref.shape, F32)
            dgk_ref[...] = jnp.zeros(dgk_ref.shape, F32)

        start, off = _attn_window(i, prev)
        c0 = start // KEY_CHUNK
        for jq in range(N_HEADS // 2):
            cols = pl.ds(jq * PAIR, PAIR)
            kcols = pl.ds((((2 * jq) // group) // 2) * PAIR, PAIR)
            qn, q_hat, q_rstd = _pair_norm(q_ref[:, cols].astype(F32), gq_ref[...])
            qs = _stack_pair(qn * ATTN_SCALE, jq, group)
            do_pair = dy_ref[:, cols].astype(F32)
            dos = _stack_pair(do_pair, jq, group)
            prod_t = (do_pair * y_ref[:, cols].astype(F32)).T
            lo = _sub_lo(prod_t.shape)
            delta2 = jnp.concatenate([jnp.sum(jnp.where(lo, prod_t, 0.0), axis=0, keepdims=True),
                                      jnp.sum(jnp.where(lo, 0.0, prod_t), axis=0, keepdims=True)], axis=1)
            lse2 = jnp.concatenate([lse_ref[2 * jq:2 * jq + 1, :], lse_ref[2 * jq + 1:2 * jq + 2, :]], axis=1)
            dsk = -jnp.exp(sink_ref[jq, 0:1, :] - lse2) * delta2
            dsink_ref[2 * jq:2 * jq + 1, :] += dsk[:, :Q_BLOCK]
            dsink_ref[2 * jq + 1:2 * jq + 2, :] += dsk[:, Q_BLOCK:]
            rows_w = pl.ds(start, w)
            s_ref[...] = _dot(kn_ref[rows_w, kcols], qs, "nt")
            dp_ref[...] = _dot(v_ref[rows_w, kcols], dos, "nt")
            for c in range(n_chunks):
                r = pl.ds(c * KEY_CHUNK, KEY_CHUNK)
                brows = pl.ds(off + c * KEY_CHUNK, KEY_CHUNK)
                p = jnp.exp(s_ref[r, :] + bias_ref[jq, brows, :] - lse2)
                ds = p * (dp_ref[r, :] - delta2)
                if want_dbias:
                    db_ref[jq, brows, :] += ds
                ds_b = ds.astype(BF16)
                pb_ref[r, :] = p.astype(BF16)
                dsb_ref[r, :] = ds_b
                dst_ref[pl.ds(2 * c * KEY_CHUNK, KEY_CHUNK), :] = ds_b[:, :Q_BLOCK]
                dst_ref[pl.ds((2 * c + 1) * KEY_CHUNK, KEY_CHUNK), :] = ds_b[:, Q_BLOCK:]
            dkn_ref[rows_w, kcols] += _dot(dsb_ref[...], qs, "nn")
            dvs_ref[rows_w, kcols] += _dot(pb_ref[...], dos, "nn")
            kl = jnp.concatenate([knt_ref[2 * jq + hq, c0 + c] for c in range(n_chunks) for hq in range(2)], axis=1)
            dqt = _dot(kl, dst_ref[...], "nn")
            dq_raw, dg = _pair_norm_bwd(dqt.T * ATTN_SCALE, q_hat, q_rstd, gq_ref[...])
            dq_ref[:, cols] = dq_raw.astype(BF16)
            dgq_ref[...] += dg

        @pl.when(i == nq - 1)
        def _():
            for jk in range(kw // PAIR):
                kcols = pl.ds(jk * PAIR, PAIR)
                _, k_hat, k_rstd = _pair_norm(k_ref[:, kcols].astype(F32), gk_ref[...])
                dk_raw, dg = _pair_norm_bwd(dkn_ref[:, kcols], k_hat, k_rstd, gk_ref[...])
                dk_ref[:, kcols] = dk_raw.astype(BF16)
                dgk_ref[...] += dg
            dv_ref[...] = dvs_ref[...].astype(BF16)

    q_spec, k_spec, v_spec = _attn_specs(cfg, s, nq)
    row = pl.BlockSpec((Q_BLOCK, A_WIDTH), lambda b, i: (b * nq + i, 0))
    kv_out = pl.BlockSpec((s, kw), lambda b, i: (b, 0))
    pair_bias = _const_spec((N_HEADS // 2, wext, 2 * Q_BLOCK))
    res = pl.pallas_call(
        body, name=name, grid=(n_batch, nq),
        in_specs=[q_spec, k_spec, v_spec, pair_bias, _const_spec((N_HEADS // 2, 8, 2 * Q_BLOCK)),
                  _const_spec((1, PAIR)), _const_spec((1, PAIR)), row, row,
                  pl.BlockSpec((None, N_HEADS, Q_BLOCK), lambda b, i: (b * nq + i, 0, 0))] + [ANY] * len(plumb.args),
        out_specs=[row, kv_out, kv_out, pair_bias, _const_spec((N_HEADS, 128)),
                   _const_spec((1, PAIR)), _const_spec((1, PAIR))] + [ANY] * len(plumb.out_shape),
        out_shape=[_sds((t, A_WIDTH), BF16), _sds((t, kw), BF16), _sds((t, kw), BF16),
                   _sds((N_HEADS // 2, wext, 2 * Q_BLOCK), F32), _sds((N_HEADS, 128), F32),
                   _sds((1, PAIR), F32), _sds((1, PAIR), F32)] + plumb.out_shape,
        scratch_shapes=[pltpu.VMEM((s, kw), BF16), pltpu.VMEM((N_HEADS, nkc, PAIR, KEY_CHUNK), BF16),
                        pltpu.VMEM((s, kw), F32), pltpu.VMEM((s, kw), F32),
                        pltpu.VMEM((w, 2 * Q_BLOCK), F32), pltpu.VMEM((w, 2 * Q_BLOCK), F32),
                        pltpu.VMEM((w, 2 * Q_BLOCK), BF16), pltpu.VMEM((w, 2 * Q_BLOCK), BF16),
                        pltpu.VMEM((2 * w, Q_BLOCK), BF16)] + plumb.scratch,
        input_output_aliases=plumb.aliases,
        compiler_params=_params(),
    )(qkv, qkv, qkv, bias2, sink2, gq, gk, y, dy, lse, *plumb.args)
    return plumb.deliver(res)


def _band_tables(prev_chunks):
    prev = prev_chunks * CHUNK
    wext = 2 * prev + Q_BLOCK
    jj = np.arange(wext)[:, None]
    ii = np.arange(Q_BLOCK)[None, :]
    dist = prev + ii - jj
    rel_chunk = (prev // CHUNK + ii // CHUNK) - jj // CHUNK
    allowed = (rel_chunk >= 0) & (rel_chunk <= prev_chunks)
    return dist, allowed


def _alibi_slopes():
    return np.array([2.0 ** (-8.0 * (h + 1) / N_HEADS) for h in range(N_HEADS)], dtype=np.float32)


def _diag_onehot(prev, wext):
    n_diag = wext + Q_BLOCK - 1
    idx = np.clip(prev + Q_BLOCK - 1 - np.arange(n_diag), -A_MAX_REL, A_MAX_REL) + A_MAX_REL
    onehot = np.zeros((n_diag, 2 * A_MAX_REL + 1), np.float32)
    onehot[np.arange(n_diag), idx] = 1.0
    return onehot


def _bias_a(rel_bias):
    prev = A_PREV_CHUNKS * CHUNK
    _, allowed = _band_tables(A_PREV_CHUNKS)
    wext = allowed.shape[0]
    n_diag = wext + Q_BLOCK - 1
    seq = jnp.dot(rel_bias, jnp.asarray(_diag_onehot(prev, wext).T), precision=lax.Precision.HIGHEST)
    seq = jnp.pad(seq, ((0, 0), (0, 1)))
    rows = jnp.broadcast_to(seq[:, None, :], (N_HEADS, Q_BLOCK, n_diag + 1)).reshape(N_HEADS, -1)
    skew = rows[:, :Q_BLOCK * n_diag].reshape(N_HEADS, Q_BLOCK, n_diag)
    tile = jnp.transpose(skew[:, :, Q_BLOCK - 1:Q_BLOCK - 1 + wext], (0, 2, 1))
    return jnp.where(jnp.asarray(allowed)[None], tile, NEG_INF)


def _bias_b():
    dist, allowed = _band_tables(B_PREV_CHUNKS)
    bias = -_alibi_slopes()[:, None, None] * np.abs(dist).astype(np.float32)[None]
    return jnp.asarray(np.where(allowed[None], bias, np.float32(NEG_INF)).astype(np.float32))


def _rel_bias_grad(db_t):
    prev = A_PREV_CHUNKS * CHUNK
    wext = db_t.shape[1]
    n_diag = wext + Q_BLOCK - 1
    wp = n_diag + Q_BLOCK - 1
    xp = jnp.pad(jnp.transpose(db_t, (0, 2, 1)), ((0, 0), (0, 0), (Q_BLOCK - 1, Q_BLOCK - 1)))
    flat = jnp.pad(xp.reshape(N_HEADS, Q_BLOCK * wp), ((0, 0), (0, Q_BLOCK)))
    skew = flat.reshape(N_HEADS, Q_BLOCK, wp + 1)[:, :, :n_diag]
    diag = jnp.sum(skew, axis=1)
    return jnp.dot(diag, jnp.asarray(_diag_onehot(prev, wext)), precision=lax.Precision.HIGHEST)


def _ew(name, fn, ins, out_dtypes):
    r, c = ins[0].shape
    rb = _pick(r, max(16, (1 << 19) // c), 16)
    spec = pl.BlockSpec((rb, c), lambda i: (i, 0))

    def body(*refs):
        vals = fn(*[ref[...] for ref in refs[:len(ins)]])
        for ref, val in zip(refs[len(ins):], vals):
            ref[...] = val.astype(ref.dtype)

    return pl.pallas_call(
        body, name=name, grid=(r // rb,), in_specs=[spec] * len(ins), out_specs=[spec] * len(out_dtypes),
        out_shape=[_sds((r, c), dt) for dt in out_dtypes], compiler_params=_params(),
    )(*ins)


def _adamw_math(w, g, m, v):
    m = ADAM_B1 * m + (1.0 - ADAM_B1) * g
    v = ADAM_B2 * v + (1.0 - ADAM_B2) * (g * g)
    m_hat = m / (1.0 - ADAM_B1 ** ADAM_STEP)
    v_hat = v / (1.0 - ADAM_B2 ** ADAM_STEP)
    delta = -ADAM_LR * (m_hat / (jnp.sqrt(v_hat) + ADAM_EPS) + ADAM_WD * w)
    return delta, m, v


def _adamw_terms(name, terms, w, m, v):
    r, c = w.shape
    hr = r // 2
    rb = _pick(hr, max(16, (1 << 19) // c), 16)
    nb = hr // rb

    def body(t_ref, w_ref, m_ref, v_ref, g_ref, d_ref, nm_ref, nv_ref):
        g = t_ref[0].astype(F32)
        for k in range(1, N_CHIPS):
            g = g + t_ref[k].astype(F32)
        delta, nm, nv = _adamw_math(w_ref[...], g, m_ref[...], v_ref[...])
        g_ref[...] = g
        d_ref[...] = delta
        nm_ref[...] = nm
        nv_ref[...] = nv

    spec = pl.BlockSpec((rb, c), lambda h, i: (h * nb + i, 0))
    return pl.pallas_call(
        body, name=name, grid=(2, nb),
        in_specs=[pl.BlockSpec((None, N_CHIPS, rb, c), lambda h, i: (h, 0, i, 0)), spec, spec, spec],
        out_specs=[spec] * 4, out_shape=[_sds((r, c), F32)] * 4, compiler_params=_params(),
    )(terms, w, m, v)


def _mesh_place():
    x, y, c = lax.axis_index("x"), lax.axis_index("y"), lax.axis_index("c")
    chips = [(x, 1 - y), (1 - x, y), (1 - x, 1 - y)]
    return x, y, c, chips


def _all_gather_weights(shards):
    n = len(shards)

    def body(*refs):
        ins, outs = refs[:n], refs[n:2 * n]
        local_sem, ici_send, ici_recv, d2d_send, d2d_recv = refs[2 * n:]
        x, y, c, chips = _mesh_place()
        me = 2 * x + y
        sibling = (x, y, 1 - c)
        local, sent = [], []
        for wi in range(n):
            loc = pltpu.make_async_copy(ins[wi], outs[wi].at[me], local_sem.at[wi])
            loc.start()
            local.append(loc)
            for k, (tx, ty) in enumerate(chips):
                cp = pltpu.make_async_remote_copy(
                    src_ref=ins[wi].at[c], dst_ref=outs[wi].at[me, c],
                    send_sem=ici_send.at[wi * 3 + k], recv_sem=ici_recv.at[wi * 3 + k],
                    device_id=(tx, ty, c), device_id_type=MESH)
                cp.start()
                sent.append(cp)
        passed = []
        for wi in range(n):
            for k, (tx, ty) in enumerate(chips):
                slab = outs[wi].at[2 * tx + ty, c]
                pltpu.make_async_remote_copy(
                    src_ref=slab, dst_ref=slab, send_sem=ici_send.at[wi * 3 + k], recv_sem=ici_recv.at[wi * 3 + k],
                    device_id=(tx, ty, c), device_id_type=MESH).wait_recv()
                fw = pltpu.make_async_remote_copy(
                    src_ref=slab, dst_ref=slab, send_sem=d2d_send.at[wi * 3 + k], recv_sem=d2d_recv.at[wi * 3 + k],
                    device_id=sibling, device_id_type=MESH)
                fw.start()
                passed.append(fw)
        for wi in range(n):
            for k, (tx, ty) in enumerate(chips):
                slab = outs[wi].at[2 * tx + ty, 1 - c]
                pltpu.make_async_remote_copy(
                    src_ref=slab, dst_ref=slab, send_sem=d2d_send.at[wi * 3 + k], recv_sem=d2d_recv.at[wi * 3 + k],
                    device_id=sibling, device_id_type=MESH).wait_recv()
        for loc in local:
            loc.wait()
        for cp in sent + passed:
            cp.wait_send()

    return pl.pallas_call(
        body, name="all_gather_weights",
        in_specs=[ANY] * n, out_specs=[ANY] * n,
        out_shape=[_sds((N_CHIPS,) + s.shape, s.dtype) for s in shards],
        scratch_shapes=[pltpu.SemaphoreType.DMA((n,)), pltpu.SemaphoreType.DMA((3 * n,)),
                        pltpu.SemaphoreType.DMA((3 * n,)), pltpu.SemaphoreType.DMA((3 * n,)),
                        pltpu.SemaphoreType.DMA((3 * n,))],
    )(*shards)


def _run_comms(name, comms):
    plumb = _CommPlumbing(comms, 0, 0, 0)
    n_in, n_out = len(plumb.args), len(plumb.out_shape)

    def body(*refs):
        parts = []
        i0, o0, s0 = 0, n_in, n_in + n_out
        for cm in plumb.comms:
            parts.append((refs[i0:i0 + len(cm.ins)], refs[o0:o0 + len(cm.outs)], refs[s0:s0 + len(cm.sems)]))
            i0 += len(cm.ins)
            o0 += len(cm.outs)
            s0 += len(cm.sems)
        for cm, part in zip(plumb.comms, parts):
            cm.start(*part)
        for cm, part in zip(plumb.comms, parts):
            cm.finish(*part)

    res = pl.pallas_call(
        body, name=name, in_specs=[ANY] * n_in, out_specs=[ANY] * n_out, out_shape=plumb.out_shape,
        scratch_shapes=plumb.scratch, input_output_aliases=plumb.aliases,
    )(*plumb.args)
    plumb.deliver(res)


def _gather_ici(shards):
    n = len(shards)

    def copies(ins, outs, sems):
        local_sem, send_sem, recv_sem = sems
        x, y, c, chips = _mesh_place()
        me = 2 * x + y
        local, sends, recvs = [], [], []
        for wi in range(n):
            local.append(pltpu.make_async_copy(ins[wi], outs[wi].at[me], local_sem.at[wi]))
            for k, (tx, ty) in enumerate(chips):
                sems_k = dict(send_sem=send_sem.at[wi * 3 + k], recv_sem=recv_sem.at[wi * 3 + k],
                              device_id=(tx, ty, c), device_id_type=MESH)
                sends.append(pltpu.make_async_remote_copy(
                    src_ref=ins[wi].at[c], dst_ref=outs[wi].at[me, c], **sems_k))
                slab = outs[wi].at[2 * tx + ty, c]
                recvs.append(pltpu.make_async_remote_copy(src_ref=slab, dst_ref=slab, **sems_k))
        return local, sends, recvs

    def start(ins, outs, sems):
        local, sends, _ = copies(ins, outs, sems)
        for cp in local + sends:
            cp.start()

    def finish(ins, outs, sems):
        local, sends, recvs = copies(ins, outs, sems)
        for cp in local:
            cp.wait()
        for cp in recvs:
            cp.wait_recv()
        for cp in sends:
            cp.wait_send()

    return _Comm(shards, [_sds((N_CHIPS,) + s.shape, s.dtype) for s in shards], {},
                 [pltpu.SemaphoreType.DMA((n,)), pltpu.SemaphoreType.DMA((3 * n,)), pltpu.SemaphoreType.DMA((3 * n,))],
                 start, finish)


def _gather_d2d(gathered):
    n = len(gathered)

    def copies(outs, sems):
        send_sem, recv_sem = sems
        x, y, c, chips = _mesh_place()
        sends, recvs = [], []
        for wi in range(n):
            for k, (tx, ty) in enumerate(chips):
                sems_k = dict(send_sem=send_sem.at[wi * 3 + k], recv_sem=recv_sem.at[wi * 3 + k],
                              device_id=(x, y, 1 - c), device_id_type=MESH)
                mine = outs[wi].at[2 * tx + ty, c]
                theirs = outs[wi].at[2 * tx + ty, 1 - c]
                sends.append(pltpu.make_async_remote_copy(src_ref=mine, dst_ref=mine, **sems_k))
                recvs.append(pltpu.make_async_remote_copy(src_ref=theirs, dst_ref=theirs, **sems_k))
        return sends, recvs

    def start(ins, outs, sems):
        for cp in copies(outs, sems)[0]:
            cp.start()

    def finish(ins, outs, sems):
        sends, recvs = copies(outs, sems)
        for cp in recvs:
            cp.wait_recv()
        for cp in sends:
            cp.wait_send()

    return _Comm(gathered, [_sds(g.shape, g.dtype) for g in gathered], {i: i for i in range(n)},
                 [pltpu.SemaphoreType.DMA((3 * n,)), pltpu.SemaphoreType.DMA((3 * n,))], start, finish)


def _exchange_halves(grads):
    n = len(grads)

    def copies(ins, outs, sems):
        send_sem, recv_sem = sems
        x, y, c, _ = _mesh_place()
        return [pltpu.make_async_remote_copy(
            src_ref=ins[wi].at[t, 1 - c], dst_ref=outs[wi].at[t],
            send_sem=send_sem.at[wi * N_CHIPS + t], recv_sem=recv_sem.at[wi * N_CHIPS + t],
            device_id=(x, y, 1 - c), device_id_type=MESH) for wi in range(n) for t in range(N_CHIPS)]

    def start(ins, outs, sems):
        for cp in copies(ins, outs, sems):
            cp.start()

    def finish(ins, outs, sems):
        for cp in copies(ins, outs, sems):
            cp.wait()

    return _Comm(grads, [_sds((N_CHIPS,) + g.shape[2:], g.dtype) for g in grads], {},
                 [pltpu.SemaphoreType.DMA((N_CHIPS * n,)), pltpu.SemaphoreType.DMA((N_CHIPS * n,))], start, finish)


def _scatter_ici(sums):
    n = len(sums)

    def copies(ins, outs, sems):
        local_sem, send_sem, recv_sem = sems
        x, y, c, chips = _mesh_place()
        me = 2 * x + y
        local, sends, recvs = [], [], []
        for wi in range(n):
            local.append(pltpu.make_async_copy(ins[wi].at[me], outs[wi].at[c, 0], local_sem.at[wi]))
            for k, (tx, ty) in enumerate(chips):
                sems_k = dict(send_sem=send_sem.at[wi * 3 + k], recv_sem=recv_sem.at[wi * 3 + k],
                              device_id=(tx, ty, c), device_id_type=MESH)
                land = outs[wi].at[c, k + 1]
                sends.append(pltpu.make_async_remote_copy(src_ref=ins[wi].at[2 * tx + ty], dst_ref=land, **sems_k))
                recvs.append(pltpu.make_async_remote_copy(src_ref=land, dst_ref=land, **sems_k))
        return local, sends, recvs

    def start(ins, outs, sems):
        local, sends, _ = copies(ins, outs, sems)
        for cp in local + sends:
            cp.start()

    def finish(ins, outs, sems):
        local, sends, recvs = copies(ins, outs, sems)
        for cp in local:
            cp.wait()
        for cp in recvs:
            cp.wait_recv()
        for cp in sends:
            cp.wait_send()

    return _Comm(sums, [_sds((2, N_CHIPS) + s.shape[1:], s.dtype) for s in sums], {},
                 [pltpu.SemaphoreType.DMA((n,)), pltpu.SemaphoreType.DMA((3 * n,)), pltpu.SemaphoreType.DMA((3 * n,))],
                 start, finish)


def _scatter_d2d(terms):
    n = len(terms)

    def copies(outs, sems):
        send_sem, recv_sem = sems
        x, y, c, _ = _mesh_place()
        sends, recvs = [], []
        for wi in range(n):
            sems_w = dict(send_sem=send_sem.at[wi], recv_sem=recv_sem.at[wi],
                          device_id=(x, y, 1 - c), device_id_type=MESH)
            sends.append(pltpu.make_async_remote_copy(src_ref=outs[wi].at[c], dst_ref=outs[wi].at[c], **sems_w))
            recvs.append(pltpu.make_async_remote_copy(src_ref=outs[wi].at[1 - c], dst_ref=outs[wi].at[1 - c], **sems_w))
        return sends, recvs

    def start(ins, outs, sems):
        for cp in copies(outs, sems)[0]:
            cp.start()

    def finish(ins, outs, sems):
        sends, recvs = copies(outs, sems)
        for cp in recvs:
            cp.wait_recv()
        for cp in sends:
            cp.wait_send()

    return _Comm(terms, [_sds(t.shape, t.dtype) for t in terms], {i: i for i in range(n)},
                 [pltpu.SemaphoreType.DMA((n,)), pltpu.SemaphoreType.DMA((n,))], start, finish)


def _chip_sum(name, grad, got, core):
    _, _, hr, c = grad.shape
    rb = _pick(hr, max(16, (1 << 19) // c), 16)

    def body(core_ref, a_ref, b_ref, o_ref):
        o_ref[...] = (a_ref[...].astype(F32) + b_ref[...].astype(F32)).astype(BF16)

    out_spec = pl.BlockSpec((None, rb, c), lambda t, i, core_ref: (t, i, 0))
    return pl.pallas_call(
        body, name=name,
        grid_spec=pltpu.PrefetchScalarGridSpec(
            num_scalar_prefetch=1, grid=(N_CHIPS, hr // rb),
            in_specs=[pl.BlockSpec((None, None, rb, c), lambda t, i, core_ref: (t, core_ref[0], i, 0)), out_spec],
            out_specs=out_spec),
        out_shape=_sds((N_CHIPS, hr, c), BF16), compiler_params=_params(),
    )(core, grad, got)


def _all_reduce_small(pack):
    r = pack.shape[0]

    def body(p_ref, o_ref, land_ref, send_sem, recv_sem):
        x, y, c, _ = _mesh_place()
        me = 4 * x + 2 * y + c
        flips = [(k >> 2 & 1, k >> 1 & 1, k & 1) for k in range(1, N_DEV)]

        def peer(fx, fy, fc):
            return (1 - x if fx else x, 1 - y if fy else y, 1 - c if fc else c)

        land_ref[me] = p_ref[...]
        sent = []
        for k, flip in enumerate(flips):
            cp = pltpu.make_async_remote_copy(
                src_ref=p_ref, dst_ref=land_ref.at[me], send_sem=send_sem.at[k], recv_sem=recv_sem.at[k],
                device_id=peer(*flip), device_id_type=MESH)
            cp.start()
            sent.append(cp)
        for k, flip in enumerate(flips):
            px, py, pc = peer(*flip)
            slot = land_ref.at[4 * px + 2 * py + pc]
            pltpu.make_async_remote_copy(
                src_ref=slot, dst_ref=slot, send_sem=send_sem.at[k], recv_sem=recv_sem.at[k],
                device_id=(px, py, pc), device_id_type=MESH).wait_recv()
        total = land_ref[0]
        for d in range(1, N_DEV):
            total = total + land_ref[d]
        o_ref[...] = total
        for cp in sent:
            cp.wait_send()

    vmem = pl.BlockSpec(memory_space=pltpu.VMEM)
    return pl.pallas_call(
        body, name="all_reduce_small", in_specs=[vmem], out_specs=vmem, out_shape=_sds((r, 128), F32),
        scratch_shapes=[pltpu.VMEM((N_DEV, r, 128), F32), pltpu.SemaphoreType.DMA((N_DEV - 1,)),
                        pltpu.SemaphoreType.DMA((N_DEV - 1,))],
    )(pack)


PACK_TILE = 8 * 128


def _pack(items):
    rows, i = [], 0
    while i < len(items):
        j = i
        while j < len(items) and items[j].size == items[i].size:
            j += 1
        group = jnp.stack([it.reshape(-1).astype(F32) for it in items[i:j]])
        rows.append(jnp.pad(group, ((0, 0), (0, -group.shape[1] % PACK_TILE))).reshape(-1, 128))
        i = j
    return jnp.concatenate(rows, axis=0)


def _unpack(pack, shapes):
    out, row = [], 0
    for shp in shapes:
        size = int(np.prod(shp))
        nrow = -(-size // PACK_TILE) * (PACK_TILE // 128)
        out.append(pack[row:row + nrow].reshape(-1)[:size].reshape(shp))
        row += nrow
    return out


BIG = ["ffn1_w_gu", "ffn1_w_down", "w_in", "w_gate", "w_proj_a", "w_proj_b", "w_out",
       "ffn2_w_gu", "ffn2_w_down", "w_ple_gate", "w_ple_proj"]
SMALL = ["ffn1_norm", "mix_norm", "ffn2_norm", "ple_norm", "a_q_norm", "a_k_norm", "b_q_norm", "b_k_norm",
         "a_rel_bias", "b_sinks"]
WEIGHTS = ["ffn1_norm", "ffn1_w_gu", "ffn1_w_down", "mix_norm", "w_in", "a_q_norm", "a_k_norm", "a_rel_bias",
           "b_q_norm", "b_k_norm", "b_sinks", "w_gate", "w_proj_a", "w_proj_b", "w_out", "ffn2_norm",
           "ffn2_w_gu", "ffn2_w_down", "ple_norm", "w_ple_gate", "w_ple_proj"]
ATTN_A = dict(prev=A_PREV_CHUNKS * CHUNK, group=1, kw=A_WIDTH, qblk=0, kblk=1, vblk=2)
ATTN_B = dict(prev=B_PREV_CHUNKS * CHUNK, group=N_HEADS // B_KV_HEADS, kw=B_KV_WIDTH, qblk=3,
              kblk=4 * A_WIDTH // B_KV_WIDTH, vblk=4 * A_WIDTH // B_KV_WIDTH + 1)


def _cast_epilogue(accs, extras, outs, ij):
    for acc, out in zip(accs, outs):
        out[...] = acc.astype(out.dtype)


GATHER_FIRST = ["ffn1_w_gu", "ffn1_w_down"]
GATHER_LATE = ["ffn2_w_gu", "ffn2_w_down", "w_ple_gate", "w_ple_proj"]
ROW_SHARDED = ("ffn1_w_down", "ffn2_w_down", "w_out", "w_ple_gate")


def _slotted(name, grad):
    if name == "w_in":
        rows, cols = grad.shape
        grad = jnp.transpose(grad.reshape(rows, N_CHIPS, cols // N_CHIPS), (1, 0, 2))
    elif name in ROW_SHARDED:
        grad = grad.reshape(N_CHIPS, grad.shape[0] // N_CHIPS, grad.shape[1])
    return grad.reshape(N_CHIPS, 2, grad.shape[1] // 2, grad.shape[2])


def _local_step(xt, pt, tgt, n_batch, shards, small, core):
    t, d = xt.shape
    tm = _pick(t, 512, 8)
    tk = _pick(t, 512, 8)
    nt = t // tm
    row = pl.BlockSpec((tm, d), lambda i, j, k: (i, 0))
    gs = shards["w_gate"].shape[1]
    ps = shards["w_proj_a"].shape[1]
    es = shards["w_ple_proj"].shape[1]
    pdim = pt.shape[1]
    ncols = N_CHIPS * shards["w_in"].shape[1]
    tin = ncols // 2
    assert 2 * gs == d and 4 * ps == d and 4 * es == d and tin % 128 == 0

    w = {}
    halves = {n: s.reshape(2, s.shape[0] // 2, s.shape[1]) for n, s in shards.items()}

    def publish(names, arrays):
        for name, g in zip(names, arrays):
            g = g.reshape(N_CHIPS, 2 * g.shape[2], g.shape[3])
            if name in ROW_SHARDED:
                g = g.reshape(N_CHIPS * g.shape[1], g.shape[2])
            elif name == "w_in":
                g = jnp.transpose(g, (1, 0, 2)).reshape(g.shape[1], N_CHIPS * g.shape[2])
            w[name] = g

    class GatherPipe:
        def __init__(self, names):
            self.names = names

        def ici(self):
            self.first = _gather_ici([halves[n] for n in self.names])
            return self.first

        def d2d(self):
            self.second = _gather_d2d(self.first.results)
            return self.second

        def publish(self):
            publish(self.names, self.second.results)

    class GradPipe:
        def __init__(self, names):
            self.names = names

        def exchange(self, grads):
            self.grads = [_slotted(n, g) for n, g in zip(self.names, grads)]
            self.x = _exchange_halves(self.grads)
            return self.x

        def scatter(self):
            sums = [_chip_sum("chip_sum_" + n, g, got, core)
                    for n, g, got in zip(self.names, self.grads, self.x.results)]
            self.s = _scatter_ici(sums)
            return self.s

        def forward(self):
            self.f = _scatter_d2d(self.s.results)
            return self.f

        def terms(self):
            return dict(zip(self.names, self.f.results))

    publish(GATHER_FIRST, _all_gather_weights([halves[n] for n in GATHER_FIRST]))
    g_in, g_proj, g_ple = GatherPipe(["w_in", "w_gate"]), GatherPipe(["w_proj_a", "w_proj_b", "w_out"]), \
        GatherPipe(["w_ple_gate", "w_ple_proj"])
    g_down2, g_up2 = GatherPipe(["ffn2_w_down"]), GatherPipe(["ffn2_w_gu"])
    h1, ffn1_saved = _ffn_fwd("ffn1", xt, small["ffn1_norm"], w["ffn1_w_gu"], w["ffn1_w_down"],
                              {"up": lambda: [g_in.ici()], "down": lambda: [g_in.d2d(), g_proj.ici()]})
    g_in.publish()
    w_in, wgate = w["w_in"], w["w_gate"]
    un = _rms_fwd("mix_norm", h1, small["mix_norm"])
    (qkv,) = _mm(
        "qkv", "nn", (nt, 2, 1),
        [(un, row, w_in, pl.BlockSpec((d, tin), lambda i, j, k: (0, j)))], [],
        [(_sds((t, ncols), BF16), pl.BlockSpec((tm, tin), lambda i, j, k: (i, j)))], (tm, tin), _cast_epilogue,
        j_outer=True, comms=[g_proj.d2d(), g_ple.ici()])
    g_proj.publish()
    wpa, wpb, wout = w["w_proj_a"], w["w_proj_b"], w["w_out"]

    def gate_epilogue(accs, extras, outs, ij):
        outs[0][...] = jax.nn.sigmoid(accs[0]).astype(BF16)

    (gates,) = _mm(
        "gate", "nn", (nt, 4, 1),
        [(un, row, wgate, pl.BlockSpec((None, d, gs), lambda i, j, k: (j, 0, 0)))], [],
        [(_sds((2, t, d), BF16), pl.BlockSpec((None, tm, gs), lambda i, j, k: (j // 2, i, j % 2)))],
        (tm, gs), gate_epilogue, j_outer=True, chunked=True, comms=[g_ple.d2d(), g_down2.ici()])
    g_ple.publish()
    wpg, wpe = w["w_ple_gate"], w["w_ple_proj"]

    bias_a = _pair_bias(_bias_a(small["a_rel_bias"][0]))
    bias_b = _pair_bias(_bias_b())
    sink_a = _pair_rows(jnp.full((N_HEADS, 128), NEG_INF, F32))
    sink_b = _pair_rows(jnp.broadcast_to(small["b_sinks"][0][:, None], (N_HEADS, 128)))
    gqa, gka, gqb, gkb = [jnp.tile(small[k], (1, 2)) for k in ("a_q_norm", "a_k_norm", "b_q_norm", "b_k_norm")]
    ya, lse_a = _attn_fwd("attn_a_fwd", qkv, bias_a, sink_a, gqa, gka, ATTN_A, n_batch,
                          comms=[g_down2.d2d(), g_up2.ici()])
    g_down2.publish()
    yb, lse_b = _attn_fwd("attn_b_fwd", qkv, bias_b, sink_b, gqb, gkb, ATTN_B, n_batch, comms=[g_up2.d2d()])
    g_up2.publish()

    def merge_epilogue(accs, extras, outs, ij):
        pa, pb = accs
        outs[0][...] = (extras[0][...].astype(F32) * pa + extras[1][...].astype(F32) * pb).astype(BF16)
        outs[1][...] = pa.astype(BF16)
        outs[2][...] = pb.astype(BF16)

    y_spec = pl.BlockSpec((tm, A_WIDTH), lambda i, j, k: (i, 0))
    proj_spec = pl.BlockSpec((None, A_WIDTH, ps), lambda i, j, k: (j, 0, 0))
    tile_ps = pl.BlockSpec((tm, ps), lambda i, j, k: (i, j))
    merged, pa, pb = _mm(
        "proj_merge", "nn", (nt, 4, 1),
        [(ya, y_spec, wpa, proj_spec), (yb, y_spec, wpb, proj_spec)],
        [(gates, pl.BlockSpec((None, tm, ps), lambda i, j, k: (0, i, j))),
         (gates, pl.BlockSpec((None, tm, ps), lambda i, j, k: (1, i, j)))],
        [(_sds((t, d), BF16), tile_ps)] * 3, (tm, ps), merge_epilogue)

    def residual_epilogue(accs, extras, outs, ij):
        outs[0][...] = extras[0][...] + accs[0]

    (h2,) = _mm(
        "out_proj", "nn", (nt, 1, 1),
        [(merged, row, wout, pl.BlockSpec((d, d), lambda i, j, k: (0, 0)))],
        [(h1, row)], [(_sds((t, d), F32), row)], (tm, d), residual_epilogue)

    h3, ffn2_saved = _ffn_fwd("ffn2", h2, small["ffn2_norm"], w["ffn2_w_gu"], w["ffn2_w_down"], {})
    n3 = _rms_fwd("ple_norm", h3, small["ple_norm"])
    tile_es = pl.BlockSpec((tm, es), lambda i, j, k: (i, j))
    (pe,) = _mm(
        "ple_embed", "nn", (nt, 4, 1),
        [(pt, pl.BlockSpec((tm, pdim), lambda i, j, k: (i, 0)), wpe, pl.BlockSpec((None, pdim, es), lambda i, j, k: (j, 0, 0)))],
        [], [(_sds((t, d), F32), tile_es)], (tm, es), _cast_epilogue)

    th = _pick(d, 512)

    def head_epilogue(accs, extras, outs, ij):
        h3_ref, pe_ref, tgt_ref = extras
        dy_ref, dpe_ref, dz_ref, loss_ref = outs
        pg = jax.nn.sigmoid(accs[0])
        pev = pe_ref[...]
        diff = h3_ref[...] + pg * pev - tgt_ref[...]
        dy = diff * (1.0 / d)
        dy_ref[...] = dy
        dpe_ref[...] = (dy * pg).astype(BF16)
        dz_ref[...] = (dy * pev * pg * (1.0 - pg)).astype(BF16)
        _accumulate(loss_ref, jnp.full(loss_ref.shape, jnp.sum(diff * diff), F32), (ij[0] == 0) & (ij[1] == 0))

    tile_h = pl.BlockSpec((tm, th), lambda i, j, k: (i, j))
    dy, dpe, dz, loss_acc = _mm(
        "ple_gate_loss", "nn", (nt, d // th, 1),
        [(n3, row, wpg, pl.BlockSpec((d, th), lambda i, j, k: (0, j)))],
        [(h3, tile_h), (pe, tile_h), (tgt, tile_h)],
        [(_sds((t, d), F32), tile_h), (_sds((t, d), BF16), tile_h), (_sds((t, d), BF16), tile_h),
         (_sds((8, 128), F32), pl.BlockSpec((8, 128), lambda i, j, k: (0, 0)))],
        (tm, th), head_epilogue, j_outer=True, chunked=True)
    loss = 0.5 * loss_acc[0, 0] / d

    nk = t // tk
    (dwpe,) = _mm(
        "d_w_ple_proj", "tn", (1, 4, nk),
        [(pt, pl.BlockSpec((tk, pdim), lambda i, j, k: (k, 0)), dpe, pl.BlockSpec((tk, es), lambda i, j, k: (k, j)))],
        [], [(_sds((4, pdim, es), BF16), pl.BlockSpec((None, pdim, es), lambda i, j, k: (j, 0, 0)))],
        (pdim, es), _cast_epilogue)

    def dense_grad(name, a, dyb):
        (res,) = _mm(
            name, "tn", (1, d // th, nk),
            [(a, pl.BlockSpec((tk, d), lambda i, j, k: (k, 0)), dyb, pl.BlockSpec((tk, th), lambda i, j, k: (k, j)))],
            [], [(_sds((d, d), BF16), pl.BlockSpec((d, th), lambda i, j, k: (0, j)))], (d, th), _cast_epilogue)
        return res

    dwpg = dense_grad("d_w_ple_gate", n3, dz)
    tmn = _pick(t, 1024, 8)
    extras, outs = _rms_bwd_io(h3, small["ple_norm"], dy, tmn)
    dh3, dh3_b, d_ple_norm = _mm(
        "d_ple_norm", "nt", (t // tmn, 1, 1),
        [(dz, pl.BlockSpec((tmn, d), lambda i, j, k: (i, 0)), wpg, pl.BlockSpec((d, d), lambda i, j, k: (0, 0)))],
        extras, outs, (tmn, d), _rms_bwd_epilogue)

    late = GradPipe(GATHER_LATE)
    proj = GradPipe(["w_proj_a", "w_proj_b", "w_out"])
    dh2, dh2_b, d_ffn2_norm, dwgu2, dwd2 = _ffn_bwd(
        "ffn2", dh3, dh3_b, h2, small["ffn2_norm"], w["ffn2_w_gu"], w["ffn2_w_down"], ffn2_saved,
        {"dnorm": lambda dwgu, dwd: [late.exchange([dwgu, dwd, dwpg, dwpe])]})

    def dmerge_epilogue(accs, extras, outs, ij):
        dmo = accs[0]
        g_ref, pa_ref, pb_ref = extras
        dg_ref, dpa_ref, dpb_ref = outs
        ga = g_ref[0].astype(F32)
        gb = g_ref[1].astype(F32)
        dg_ref[0] = (dmo * pa_ref[...].astype(F32) * ga * (1.0 - ga)).astype(BF16)
        dg_ref[1] = (dmo * pb_ref[...].astype(F32) * gb * (1.0 - gb)).astype(BF16)
        dpa_ref[...] = (dmo * ga).astype(BF16)
        dpb_ref[...] = (dmo * gb).astype(BF16)

    g_spec = pl.BlockSpec((2, tm, th), lambda i, j, k: (0, i, j))
    dgates, dpa, dpb = _mm(
        "d_merge", "nt", (nt, d // th, 1),
        [(dh2_b, row, wout, pl.BlockSpec((th, d), lambda i, j, k: (j, 0)))],
        [(gates, g_spec), (pa, tile_h), (pb, tile_h)],
        [(_sds((2, t, d), BF16), g_spec), (_sds((t, d), BF16), tile_h), (_sds((t, d), BF16), tile_h)],
        (tm, th), dmerge_epilogue, j_outer=True, chunked=True)
    dwout = dense_grad("d_w_out", merged, dh2_b)

    yk_spec = pl.BlockSpec((tk, A_WIDTH), lambda i, j, k: (k, 0))
    dk_spec = pl.BlockSpec((tk, ps), lambda i, j, k: (k, j))
    dproj = (_sds((4, A_WIDTH, ps), BF16), proj_spec)
    dwpa, dwpb = _mm(
        "d_w_proj", "tn", (1, 4, nk),
        [(ya, yk_spec, dpa, dk_spec), (yb, yk_spec, dpb, dk_spec)], [], [dproj, dproj], (A_WIDTH, ps), _cast_epilogue)
    dproj_a = pl.BlockSpec((tm, ps), lambda i, j, k: (i, k))
    wproj_k = pl.BlockSpec((None, A_WIDTH, ps), lambda i, j, k: (k, 0, 0))
    dya, dyb = _mm(
        "d_attn_out", "nt", (nt, 1, 4),
        [(dpa, dproj_a, wpa, wproj_k), (dpb, dproj_a, wpb, wproj_k)], [],
        [(_sds((t, A_WIDTH), BF16), y_spec)] * 2, (tm, A_WIDTH), _cast_epilogue,
        comms=[proj.exchange([dwpa, dwpb, dwout])])

    dqa, dka, dva, dbias_a, _, dgqa, dgka = _attn_bwd(
        "attn_a_bwd", qkv, bias_a, sink_a, gqa, gka, ya, dya, lse_a, ATTN_A, n_batch, True,
        comms=[late.scatter(), proj.scatter()])
    dqb, dkb, dvb, _, dsink_b, dgqb, dgkb = _attn_bwd(
        "attn_b_bwd", qkv, bias_b, sink_b, gqb, gkb, yb, dyb, lse_b, ATTN_B, n_batch, False,
        comms=[late.forward(), proj.forward()])
    dqkv = jnp.concatenate([dqa, dka, dva, dqb, dkb, dvb], axis=1)

    (dwgate,) = _mm(
        "d_w_gate", "tn", (1, 4, nk),
        [(un, pl.BlockSpec((tk, d), lambda i, j, k: (k, 0)),
          dgates, pl.BlockSpec((None, tk, gs), lambda i, j, k: (j // 2, k, j % 2)))],
        [], [(_sds((4, d, gs), BF16), pl.BlockSpec((None, d, gs), lambda i, j, k: (j, 0, 0)))], (d, gs), _cast_epilogue)
    (dwin,) = _mm(
        "d_w_in", "tn", (1, 2, nk),
        [(un, pl.BlockSpec((tk, d), lambda i, j, k: (k, 0)), dqkv, pl.BlockSpec((tk, tin), lambda i, j, k: (k, j)))],
        [], [(_sds((d, ncols), BF16), pl.BlockSpec((d, tin), lambda i, j, k: (0, j)))], (d, tin), _cast_epilogue)

    mixer = GradPipe(["w_in", "w_gate"])
    extras, outs = _rms_bwd_io(h1, small["mix_norm"], dh2, tmn)
    dh1, dh1_b, d_mix_norm = _mm(
        "d_mix_norm", "nt", (t // tmn, 1, 6),
        [(dgates, pl.BlockSpec((None, tmn, gs), lambda i, j, k: (jnp.minimum(k, 3) // 2, i, jnp.minimum(k, 3) % 2)),
          wgate, pl.BlockSpec((None, d, gs), lambda i, j, k: (jnp.minimum(k, 3), 0, 0))),
         (dqkv, pl.BlockSpec((tmn, tin), lambda i, j, k: (i, jnp.maximum(k - 4, 0))),
          w_in, pl.BlockSpec((d, tin), lambda i, j, k: (0, jnp.maximum(k - 4, 0))))],
        extras, outs, (tmn, d), _rms_bwd_epilogue, steps=[4, 2],
        comms=[mixer.exchange([dwin, dwgate])])

    up1 = GradPipe(["ffn1_w_gu"])
    down1 = GradPipe(["ffn1_w_down"])
    dx, _, d_ffn1_norm, _, _ = _ffn_bwd(
        "ffn1", dh1, dh1_b, xt, small["ffn1_norm"], w["ffn1_w_gu"], w["ffn1_w_down"], ffn1_saved,
        {"dact": lambda: [mixer.scatter()],
         "dwgu": lambda: [mixer.forward()],
         "dwd": lambda dwgu: [up1.exchange([dwgu])],
         "dnorm": lambda dwgu, dwd: [up1.scatter(), down1.exchange([dwd])]})
    _run_comms("grad_tail_scatter", [up1.forward(), down1.scatter()])
    _run_comms("grad_tail_forward", [down1.forward()])
    terms = {**late.terms(), **proj.terms(), **mixer.terms(), **up1.terms(), **down1.terms()}

    def fold(v):
        return v[0, :HEAD_DIM] + v[0, HEAD_DIM:]

    small_grads = {"ffn1_norm": d_ffn1_norm, "mix_norm": d_mix_norm, "ffn2_norm": d_ffn2_norm,
                   "ple_norm": d_ple_norm, "a_q_norm": fold(dgqa), "a_k_norm": fold(dgka),
                   "b_q_norm": fold(dgqb), "b_k_norm": fold(dgkb), "a_rel_bias": _rel_bias_grad(_unpair_bias(dbias_a)),
                   "b_sinks": jnp.sum(dsink_b, axis=1)}
    return loss, dx, terms, small_grads


def kernel(x, p, ffn1_norm, ffn1_w_gu, ffn1_w_down, mix_norm, w_in, a_q_norm, a_k_norm, a_rel_bias, b_q_norm, b_k_norm, b_sinks, w_gate, w_proj_a, w_proj_b, w_out, ffn2_norm, ffn2_w_gu, ffn2_w_down, ple_norm, w_ple_gate, w_ple_proj, loss_target, m_ffn1_norm, m_ffn1_w_gu, m_ffn1_w_down, m_mix_norm, m_w_in, m_a_q_norm, m_a_k_norm, m_a_rel_bias, m_b_q_norm, m_b_k_norm, m_b_sinks, m_w_gate, m_w_proj_a, m_w_proj_b, m_w_out, m_ffn2_norm, m_ffn2_w_gu, m_ffn2_w_down, m_ple_norm, m_w_ple_gate, m_w_ple_proj, v_ffn1_norm, v_ffn1_w_gu, v_ffn1_w_down, v_mix_norm, v_w_in, v_a_q_norm, v_a_k_norm, v_a_rel_bias, v_b_q_norm, v_b_k_norm, v_b_sinks, v_w_gate, v_w_proj_a, v_w_proj_b, v_w_out, v_ffn2_norm, v_ffn2_w_gu, v_ffn2_w_down, v_ple_norm, v_w_ple_gate, v_w_ple_proj):
    given = dict(locals())
    n_batch, s, d = x.shape
    t = n_batch * s
    xt = x.reshape(t, d)
    pt = p.reshape(t, p.shape[-1])
    tgt = loss_target.reshape(t, d)

    shards = {}
    for name in BIG:
        (shards[name],) = _ew("cast_" + name, lambda v: (v,), [given[name][0]], [BF16])
    small = {name: given[name] for name in SMALL}
    core = lax.axis_index("c").astype(jnp.int32).reshape(1)
    loss, dx, terms, small_grads = _local_step(xt, pt, tgt, n_batch, shards, small, core)

    grads, deltas, new_m, new_v = {}, {}, {}, {}
    for name in BIG:
        gw, dl, nm, nv = _adamw_terms("adamw_" + name, terms[name], given[name][0], given["m_" + name][0],
                                      given["v_" + name][0])
        grads[name], deltas[name], new_m[name], new_v[name] = gw[None], dl[None], nm[None], nv[None]

    small_shapes = [given[name].shape for name in SMALL] + [()]
    g_pack = _all_reduce_small(_pack([small_grads[name] for name in SMALL] + [loss]))
    zero = jnp.zeros((), F32)
    w_pack = _pack([given[name] for name in SMALL] + [zero])
    m_pack = _pack([given["m_" + name] for name in SMALL] + [zero])
    v_pack = _pack([given["v_" + name] for name in SMALL] + [zero])
    d_pack, nm_pack, nv_pack = _ew("adamw_small", lambda wv, gv, mv, vv: _adamw_math(wv, gv, mv, vv),
                                   [w_pack, g_pack, m_pack, v_pack], [F32] * 3)
    g_small = _unpack(g_pack, small_shapes)
    loss_total = g_small[-1]
    for name, gv, dv, mv, vv in zip(SMALL, g_small, _unpack(d_pack, small_shapes), _unpack(nm_pack, small_shapes),
                                    _unpack(nv_pack, small_shapes)):
        grads[name], deltas[name], new_m[name], new_v[name] = gv, dv, mv, vv

    return (loss_total, dx.reshape(x.shape), *[grads[n] for n in WEIGHTS], *[deltas[n] for n in WEIGHTS],
            *[new_m[n] for n in WEIGHTS], *[new_v[n] for n in WEIGHTS])
```

```python
import functools

import numpy as np
import jax
import jax.numpy as jnp
from jax import lax
from jax.experimental import pallas as pl
from jax.experimental.pallas import tpu as pltpu

F32 = jnp.float32
BF16 = jnp.bfloat16

CHUNK = 64
HEAD_DIM = 64
A_PREV_CHUNKS = 8
A_MAX_REL = 128
N_HEADS = 8
B_KV_HEADS = 2
B_PREV_CHUNKS = 2
A_WIDTH = N_HEADS * HEAD_DIM
B_KV_WIDTH = B_KV_HEADS * HEAD_DIM
EPS = 1e-6
NEG_INF = -1e30
ATTN_SCALE = HEAD_DIM ** -0.5
Q_BLOCK = 128
PAIR = 2 * HEAD_DIM

ADAM_LR = 0.001
ADAM_B1 = 0.9
ADAM_B2 = 0.999
ADAM_EPS = 1e-08
ADAM_WD = 0.01
ADAM_STEP = 10

N_CHIPS = 4
N_DEV = 8
VMEM_LIMIT_V7X = 56 * 1024 * 1024
MESH = pl.DeviceIdType.MESH
ANY = pl.BlockSpec(memory_space=pl.ANY)

_DN = {
    "nn": (((1,), (0,)), ((), ())),
    "nt": (((1,), (1,)), ((), ())),
    "tn": (((0,), (0,)), ((), ())),
}


def _pick(n, target, mult=128):
    best = None
    for d in range(mult, min(n, target) + 1, mult):
        if n % d == 0:
            best = d
    return n if best is None else best


def _dot(a, b, mode):
    return lax.dot_general(a.astype(BF16), b.astype(BF16), _DN[mode], preferred_element_type=F32)


def _params():
    return pltpu.CompilerParams(vmem_limit_bytes=VMEM_LIMIT_V7X)


class _Comm:
    def __init__(self, ins, outs, aliases, sems, start, finish):
        self.ins, self.outs, self.aliases, self.sems = list(ins), list(outs), dict(aliases), list(sems)
        self.start, self.finish = start, finish
        self.results = None


class _CommPlumbing:
    def __init__(self, comms, n_in, n_out, n_scratch):
        self.comms = list(comms)
        self.n_in, self.n_out, self.n_scratch = n_in, n_out, n_scratch
        self.args = [a for cm in self.comms for a in cm.ins]
        self.out_shape = [o for cm in self.comms for o in cm.outs]
        self.scratch = [s for cm in self.comms for s in cm.sems]
        self.aliases = {}
        i0, o0 = n_in, n_out
        for cm in self.comms:
            for a, b in cm.aliases.items():
                self.aliases[i0 + a] = o0 + b
            i0 += len(cm.ins)
            o0 += len(cm.outs)

    def run(self, in_refs, out_refs, scratch_refs, first, last):
        if not self.comms:
            return
        parts = []
        i0, o0, s0 = self.n_in, self.n_out, self.n_scratch
        for cm in self.comms:
            parts.append((in_refs[i0:i0 + len(cm.ins)], out_refs[o0:o0 + len(cm.outs)],
                          scratch_refs[s0:s0 + len(cm.sems)]))
            i0 += len(cm.ins)
            o0 += len(cm.outs)
            s0 += len(cm.sems)

        @pl.when(first)
        def _():
            for cm, part in zip(self.comms, parts):
                cm.start(*part)

        @pl.when(last)
        def _():
            for cm, part in zip(self.comms, parts):
                cm.finish(*part)

    def deliver(self, results):
        o0 = self.n_out
        for cm in self.comms:
            cm.results = list(results[o0:o0 + len(cm.outs)])
            o0 += len(cm.outs)
        return list(results[:self.n_out])


def _swap_ij(spec):
    index_map = spec.index_map
    return pl.BlockSpec(spec.block_shape, lambda j, i, k: index_map(i, j, k))


MXU_COLUMNS_V7X = 256


def _mm(name, mode, grid, pairs, extras, outs, acc_shape, epilogue, steps=None, comms=(), j_outer=False,
        chunked=False):
    ni, nj, nk = grid
    n_in = 2 * len(pairs) + len(extras)
    n_out = len(outs)
    tn = acc_shape[1]
    col_chunks = None
    if chunked:
        assert nk == 1 and steps is None and mode in ("nn", "nt")
        col_chunks = [(c0, min(MXU_COLUMNS_V7X, tn - c0)) for c0 in range(0, tn, MXU_COLUMNS_V7X)]
    n_acc = 0 if chunked else (len(pairs) if steps is None else 1)
    plumb = _CommPlumbing(comms, n_in, n_out, n_acc)
    n_all_in = n_in + len(plumb.args)
    n_all_out = n_out + len(plumb.out_shape)
    if j_outer:
        grid = (nj, ni, nk)
        pairs = [(a, _swap_ij(a_spec), b, _swap_ij(b_spec)) for a, a_spec, b, b_spec in pairs]
        extras = [(e, _swap_ij(e_spec)) for e, e_spec in extras]
        outs = [(o, _swap_ij(o_spec)) for o, o_spec in outs]

    def body(*refs):
        in_refs = refs[:n_all_in]
        out_refs = refs[n_all_in:n_all_in + n_all_out]
        scratch = refs[n_all_in + n_all_out:]
        accs = scratch[:n_acc]
        i = pl.program_id(1 if j_outer else 0)
        j = pl.program_id(0 if j_outer else 1)
        k = pl.program_id(2)

        def contrib(p, acc):
            acc[...] += _dot(in_refs[2 * p][...], in_refs[2 * p + 1][...], mode)

        if col_chunks:
            def cols(ref, c0, cs):
                if ref.shape[-1] != tn:
                    return ref
                return ref.at[(slice(None),) * (len(ref.shape) - 1) + (pl.ds(c0, cs),)]

            lhs = [in_refs[2 * p][...] for p in range(len(pairs))]
            for ci, (c0, cs) in enumerate(col_chunks):
                vals = []
                for p in range(len(pairs)):
                    b_ref = in_refs[2 * p + 1]
                    rhs = b_ref[:, c0:c0 + cs] if mode == "nn" else b_ref[c0:c0 + cs, :]
                    vals.append(_dot(lhs[p], rhs, mode))
                epilogue(vals, [cols(r, c0, cs) for r in in_refs[2 * len(pairs):n_in]],
                         [cols(r, c0, cs) for r in out_refs[:n_out]], (i, j * len(col_chunks) + ci))
        else:
            @pl.when(k == 0)
            def _():
                for acc in accs:
                    acc[...] = jnp.zeros(acc.shape, F32)

            if steps is None:
                for p in range(len(pairs)):
                    contrib(p, accs[p])
            else:
                lo = 0
                for p, n in enumerate(steps):
                    pl.when((k >= lo) & (k < lo + n))(functools.partial(contrib, p, accs[0]))
                    lo += n

            @pl.when(k == nk - 1)
            def _():
                epilogue([acc[...] for acc in accs], in_refs[2 * len(pairs):n_in], out_refs[:n_out], (i, j))

        plumb.run(in_refs, out_refs, scratch, (i == 0) & (j == 0) & (k == 0),
                  (i == ni - 1) & (j == nj - 1) & (k == nk - 1))

    args, in_specs = [], []
    for a, a_spec, b, b_spec in pairs:
        args += [a, b]
        in_specs += [a_spec, b_spec]
    for e, e_spec in extras:
        args.append(e)
        in_specs.append(e_spec)
    res = pl.pallas_call(
        body,
        name=name,
        grid=grid,
        in_specs=in_specs + [ANY] * len(plumb.args),
        out_specs=[s for _, s in outs] + [ANY] * len(plumb.out_shape),
        out_shape=[o for o, _ in outs] + plumb.out_shape,
        scratch_shapes=[pltpu.VMEM(acc_shape, F32) for _ in range(n_acc)] + plumb.scratch,
        input_output_aliases=plumb.aliases,
        compiler_params=_params(),
    )(*args, *plumb.args)
    return plumb.deliver(res)


def _sds(shape, dtype):
    return jax.ShapeDtypeStruct(shape, dtype)


def _accumulate(ref, value, first):
    @pl.when(first)
    def _():
        ref[...] = value

    @pl.when(jnp.logical_not(first))
    def _():
        ref[...] += value


def _rms_fwd(name, x, gain):
    t, d = x.shape
    tm = _pick(t, 512, 8)

    def body(x_ref, g_ref, y_ref):
        xv = x_ref[...]
        rstd = lax.rsqrt(jnp.mean(xv * xv, axis=-1, keepdims=True) + EPS)
        y_ref[...] = (xv * rstd * g_ref[...]).astype(BF16)

    return pl.pallas_call(
        body, name=name, grid=(t // tm,),
        in_specs=[pl.BlockSpec((tm, d), lambda i: (i, 0)), pl.BlockSpec((1, d), lambda i: (0, 0))],
        out_specs=pl.BlockSpec((tm, d), lambda i: (i, 0)),
        out_shape=_sds((t, d), BF16),
        compiler_params=_params(),
    )(x, gain)


def _rms_bwd_epilogue(accs, extras, outs, ij):
    x_ref, g_ref, r_ref = extras
    dh_ref, dhb_ref, dg_ref = outs
    dn = accs[0]
    xv = x_ref[...]
    rstd = lax.rsqrt(jnp.mean(xv * xv, axis=-1, keepdims=True) + EPS)
    xhat = xv * rstd
    gd = dn * g_ref[...]
    dx = rstd * (gd - xhat * jnp.mean(gd * xhat, axis=-1, keepdims=True))
    dh = r_ref[...] + dx
    dh_ref[...] = dh
    dhb_ref[...] = dh.astype(BF16)
    _accumulate(dg_ref, jnp.sum(dn * xhat, axis=0, keepdims=True), ij[0] == 0)


def _rms_bwd_io(x, gain, dres, tm):
    t, d = x.shape
    row = pl.BlockSpec((tm, d), lambda i, j, k: (i, 0))
    extras = [(x, row), (gain, pl.BlockSpec((1, d), lambda i, j, k: (0, 0))), (dres, row)]
    outs = [(_sds((t, d), F32), row), (_sds((t, d), BF16), row),
            (_sds((1, d), F32), pl.BlockSpec((1, d), lambda i, j, k: (0, 0)))]
    return extras, outs


def _ffn_fwd(tag, h, gain, wgu, wd, hooks):
    t, d = h.shape
    fs = wgu.shape[2]
    f = 2 * fs
    tm = _pick(t, 512, 8)
    n = _rms_fwd(tag + "_norm", h, gain)

    def up_epilogue(accs, extras, outs, ij):
        g, u = accs
        gu_ref, a_ref = outs
        gu_ref[0] = g.astype(BF16)
        gu_ref[1] = u.astype(BF16)
        a_ref[...] = (g * jax.nn.sigmoid(g) * u).astype(BF16)

    a_spec = pl.BlockSpec((tm, d), lambda i, j, k: (i, 0))
    gu, a = _mm(
        tag + "_up", "nn", (t // tm, 2, 1),
        [(n, a_spec, wgu, pl.BlockSpec((None, d, fs), lambda i, j, k: (j, 0, 0))),
         (n, a_spec, wgu, pl.BlockSpec((None, d, fs), lambda i, j, k: (j + 2, 0, 0)))],
        [],
        [(_sds((2, t, f), BF16), pl.BlockSpec((2, tm, fs), lambda i, j, k: (0, i, j))),
         (_sds((t, f), BF16), pl.BlockSpec((tm, fs), lambda i, j, k: (i, j)))],
        (tm, fs), up_epilogue, comms=hooks.get("up", lambda: ())(), j_outer=True, chunked=True)

    def down_epilogue(accs, extras, outs, ij):
        outs[0][...] = extras[0][...] + 0.5 * accs[0]

    row = pl.BlockSpec((tm, d), lambda i, j, k: (i, 0))
    (h_new,) = _mm(
        tag + "_down", "nn", (t // tm, 1, 1),
        [(a, pl.BlockSpec((tm, f), lambda i, j, k: (i, 0)), wd, pl.BlockSpec((f, d), lambda i, j, k: (0, 0)))],
        [(h, row)], [(_sds((t, d), F32), row)], (tm, d), down_epilogue, comms=hooks.get("down", lambda: ())())
    return h_new, (n, gu, a)


def _ffn_bwd(tag, dh, dh_b, h, gain, wgu, wd, saved, hooks):
    n, gu, a = saved
    t, d = h.shape
    fs = wgu.shape[2]
    f = 2 * fs
    tm = _pick(t, 512, 8)
    tk = _pick(t, 512, 8)

    def dact_epilogue(accs, extras, outs, ij):
        da = 0.5 * accs[0]
        g = extras[0][0].astype(F32)
        u = extras[0][1].astype(F32)
        sg = jax.nn.sigmoid(g)
        outs[0][0] = (da * u * sg * (1.0 + g * (1.0 - sg))).astype(BF16)
        outs[0][1] = (da * g * sg).astype(BF16)

    gu_spec = pl.BlockSpec((2, tm, fs), lambda i, j, k: (0, i, j))
    (dgu,) = _mm(
        tag + "_dact", "nt", (t // tm, 2, 1),
        [(dh_b, pl.BlockSpec((tm, d), lambda i, j, k: (i, 0)), wd, pl.BlockSpec((fs, d), lambda i, j, k: (j, 0)))],
        [(gu, gu_spec)], [(_sds((2, t, f), BF16), gu_spec)], (tm, fs), dact_epilogue, j_outer=True, chunked=True,
        comms=hooks.get("dact", lambda: ())())

    def cast_epilogue(accs, extras, outs, ij):
        outs[0][...] = accs[0].astype(BF16)

    (dwgu,) = _mm(
        tag + "_dwgu", "tn", (1, 4, t // tk),
        [(n, pl.BlockSpec((tk, d), lambda i, j, k: (k, 0)),
          dgu, pl.BlockSpec((None, tk, fs), lambda i, j, k: (j // 2, k, j % 2)))],
        [], [(_sds((4, d, fs), BF16), pl.BlockSpec((None, d, fs), lambda i, j, k: (j, 0, 0)))], (d, fs), cast_epilogue,
        comms=hooks.get("dwgu", lambda: ())())

    def half_epilogue(accs, extras, outs, ij):
        outs[0][...] = (0.5 * accs[0]).astype(BF16)

    (dwd,) = _mm(
        tag + "_dwd", "tn", (2, 1, t // tk),
        [(a, pl.BlockSpec((tk, fs), lambda i, j, k: (k, i)), dh_b, pl.BlockSpec((tk, d), lambda i, j, k: (k, 0)))],
        [], [(_sds((f, d), BF16), pl.BlockSpec((fs, d), lambda i, j, k: (i, 0)))], (fs, d), half_epilogue,
        comms=hooks.get("dwd", lambda g: ())(dwgu))

    tmn = _pick(t, 1024, 8)
    extras, outs = _rms_bwd_io(h, gain, dh, tmn)
    dh_in, dh_in_b, dgain = _mm(
        tag + "_dnorm", "nt", (t // tmn, 1, 4),
        [(dgu, pl.BlockSpec((None, tmn, fs), lambda i, j, k: (k // 2, i, k % 2)),
          wgu, pl.BlockSpec((None, d, fs), lambda i, j, k: (k, 0, 0)))],
        extras, outs, (tmn, d), _rms_bwd_epilogue, comms=hooks.get("dnorm", lambda g, w: ())(dwgu, dwd))
    return dh_in, dh_in_b, dgain, dwgu, dwd


def _lane_lo(shape):
    return lax.broadcasted_iota(jnp.int32, shape, 1) < HEAD_DIM


def _pair_norm(xv, gain):
    lo = _lane_lo(xv.shape)
    x2 = xv * xv
    ms_lo = jnp.sum(jnp.where(lo, x2, 0.0), axis=-1, keepdims=True) * (1.0 / HEAD_DIM)
    ms_hi = jnp.sum(jnp.where(lo, 0.0, x2), axis=-1, keepdims=True) * (1.0 / HEAD_DIM)
    rstd = jnp.where(lo, lax.rsqrt(ms_lo + EPS), lax.rsqrt(ms_hi + EPS))
    xhat = xv * rstd
    return xhat * gain, xhat, rstd


def _pair_norm_bwd(dn, xhat, rstd, gain):
    lo = _lane_lo(dn.shape)
    gd = dn * gain
    t = gd * xhat
    m_lo = jnp.sum(jnp.where(lo, t, 0.0), axis=-1, keepdims=True) * (1.0 / HEAD_DIM)
    m_hi = jnp.sum(jnp.where(lo, 0.0, t), axis=-1, keepdims=True) * (1.0 / HEAD_DIM)
    dx = rstd * (gd - xhat * jnp.where(lo, m_lo, m_hi))
    return dx, jnp.sum(dn * xhat, axis=0, keepdims=True)


def _half(xv, hi):
    lo = _lane_lo(xv.shape)
    return jnp.where(lo, 0, xv) if hi else jnp.where(lo, xv, 0)


def _attn_window(i, prev):
    q0 = i * Q_BLOCK
    start = jnp.maximum(q0 - prev, 0)
    off = start - (q0 - prev)
    return pl.multiple_of(start, Q_BLOCK), pl.multiple_of(off, Q_BLOCK)


def _attn_specs(cfg, s, nq):
    kw = cfg["kw"]
    q_spec = pl.BlockSpec((Q_BLOCK, A_WIDTH), lambda b, i: (b * nq + i, cfg["qblk"]))
    k_spec = pl.BlockSpec((s, kw), lambda b, i: (b, cfg["kblk"]))
    v_spec = pl.BlockSpec((s, kw), lambda b, i: (b, cfg["vblk"]))
    return q_spec, k_spec, v_spec


def _const_spec(shape):
    return pl.BlockSpec(shape, lambda b, i: (0,) * len(shape))


KEY_CHUNK = 128


def _pair_bias(bias_t):
    wext = bias_t.shape[1]
    return jnp.transpose(bias_t.reshape(N_HEADS // 2, 2, wext, Q_BLOCK), (0, 2, 1, 3)).reshape(
        N_HEADS // 2, wext, 2 * Q_BLOCK)


def _unpair_bias(db2):
    wext = db2.shape[1]
    return jnp.transpose(db2.reshape(N_HEADS // 2, wext, 2, Q_BLOCK), (0, 2, 1, 3)).reshape(N_HEADS, wext, Q_BLOCK)


def _pair_rows(rows):
    two = rows.reshape(N_HEADS // 2, 2 * rows.shape[1])
    return jnp.broadcast_to(two[:, None, :], (N_HEADS // 2, 8, two.shape[1]))


def _sub_lo(shape):
    return lax.broadcasted_iota(jnp.int32, shape, 0) < HEAD_DIM


def _by_half(lo_row, hi_row, rows):
    return jnp.where(_sub_lo((rows, lo_row.shape[1])), lo_row, hi_row)


def _stack_pair(xn, jq, group):
    parts = []
    for hq in range(2):
        hk = ((2 * jq + hq) // group) % 2
        xm = _half(xn, hq)
        if hq != hk:
            xm = pltpu.roll(xm, HEAD_DIM, 1)
        parts.append(xm)
    return jnp.concatenate(parts, axis=0).astype(BF16)


def _place_transposed(blk, dst_ref, c, heads, group):
    bt = blk.T
    lo = _sub_lo(bt.shape)
    for h in heads:
        src_hi = ((h // group) % 2) == 1
        part = jnp.where(lo, 0.0, bt) if src_hi else jnp.where(lo, bt, 0.0)
        if src_hi != (h % 2 == 1):
            part = pltpu.roll(part, HEAD_DIM, 0)
        dst_ref[h, c] = part.astype(BF16)


def _attn_fwd(name, qkv, bias2, sink2, gq, gk, cfg, n_batch, comms=()):
    t = qkv.shape[0]
    s = t // n_batch
    nq = s // Q_BLOCK
    nkc = s // KEY_CHUNK
    prev, group, kw = cfg["prev"], cfg["group"], cfg["kw"]
    n_chunks = (prev + Q_BLOCK) // KEY_CHUNK
    wext = bias2.shape[1]
    plumb = _CommPlumbing(comms, 7, 2, 2)
    n_all_in = 7 + len(plumb.args)
    n_all_out = 2 + len(plumb.out_shape)

    def body(*refs):
        q_ref, k_ref, v_ref, bias_ref, sink_ref, gq_ref, gk_ref = refs[:7]
        y_ref, lse_ref = refs[n_all_in:n_all_in + 2]
        kn_ref, vt_ref = refs[n_all_in + n_all_out:n_all_in + n_all_out + 2]
        i = pl.program_id(1)
        plumb.run(refs[:n_all_in], refs[n_all_in:n_all_in + n_all_out], refs[n_all_in + n_all_out:],
                  (pl.program_id(0) == 0) & (i == 0), (pl.program_id(0) == n_batch - 1) & (i == nq - 1))

        @pl.when(i == 0)
        def _():
            for jk in range(kw // PAIR):
                cols = pl.ds(jk * PAIR, PAIR)
                heads = [h for h in range(N_HEADS) if (h // group) // 2 == jk]
                kn, _, _ = _pair_norm(k_ref[:, cols].astype(F32), gk_ref[...])
                kn_ref[:, cols] = kn.astype(BF16)
                for c in range(nkc):
                    _place_transposed(v_ref[pl.ds(c * KEY_CHUNK, KEY_CHUNK), cols].astype(F32), vt_ref, c, heads, group)

        start, off = _attn_window(i, prev)
        c0 = start // KEY_CHUNK
        sub8 = lax.broadcasted_iota(jnp.int32, (N_HEADS, Q_BLOCK), 0)
        lse = jnp.zeros((N_HEADS, Q_BLOCK), F32)
        for jq in range(N_HEADS // 2):
            kcols = pl.ds((((2 * jq) // group) // 2) * PAIR, PAIR)
            qn, _, _ = _pair_norm(q_ref[:, pl.ds(jq * PAIR, PAIR)].astype(F32), gq_ref[...])
            qs = _stack_pair(qn * ATTN_SCALE, jq, group)
            m = sink_ref[jq, 0:1, :]
            l = jnp.ones((1, 2 * Q_BLOCK), F32)
            ot = jnp.zeros((PAIR, Q_BLOCK), F32)
            for c in range(n_chunks):
                rows = pl.ds(start + c * KEY_CHUNK, KEY_CHUNK)
                s2 = _dot(kn_ref[rows, kcols], qs, "nt") + bias_ref[jq, pl.ds(off + c * KEY_CHUNK, KEY_CHUNK), :]
                m_new = jnp.maximum(m, jnp.max(s2, axis=0, keepdims=True))
                alpha = jnp.exp(m - m_new)
                p = jnp.exp(s2 - m_new)
                l = alpha * l + jnp.sum(p, axis=0, keepdims=True)
                m = m_new
                pst = jnp.concatenate([p[:, :Q_BLOCK], p[:, Q_BLOCK:]], axis=0)
                vl = jnp.concatenate([vt_ref[2 * jq, c0 + c], vt_ref[2 * jq + 1, c0 + c]], axis=1)
                ot = ot * _by_half(alpha[:, :Q_BLOCK], alpha[:, Q_BLOCK:], PAIR) + _dot(vl, pst, "nn")
            inv = 1.0 / l
            ot = ot * _by_half(inv[:, :Q_BLOCK], inv[:, Q_BLOCK:], PAIR)
            y_ref[:, pl.ds(jq * PAIR, PAIR)] = ot.T.astype(BF16)
            lse2 = m + jnp.log(l)
            lse = jnp.where(sub8 == 2 * jq, lse2[:, :Q_BLOCK], lse)
            lse = jnp.where(sub8 == 2 * jq + 1, lse2[:, Q_BLOCK:], lse)
        lse_ref[...] = lse

    q_spec, k_spec, v_spec = _attn_specs(cfg, s, nq)
    res = pl.pallas_call(
        body, name=name, grid=(n_batch, nq),
        in_specs=[q_spec, k_spec, v_spec, _const_spec((N_HEADS // 2, wext, 2 * Q_BLOCK)),
                  _const_spec((N_HEADS // 2, 8, 2 * Q_BLOCK)), _const_spec((1, PAIR)), _const_spec((1, PAIR))]
        + [ANY] * len(plumb.args),
        out_specs=[pl.BlockSpec((Q_BLOCK, A_WIDTH), lambda b, i: (b * nq + i, 0)),
                   pl.BlockSpec((None, N_HEADS, Q_BLOCK), lambda b, i: (b * nq + i, 0, 0))]
        + [ANY] * len(plumb.out_shape),
        out_shape=[_sds((t, A_WIDTH), BF16), _sds((t // Q_BLOCK, N_HEADS, Q_BLOCK), F32)] + plumb.out_shape,
        scratch_shapes=[pltpu.VMEM((s, kw), BF16), pltpu.VMEM((N_HEADS, nkc, PAIR, KEY_CHUNK), BF16)] + plumb.scratch,
        input_output_aliases=plumb.aliases,
        compiler_params=_params(),
    )(qkv, qkv, qkv, bias2, sink2, gq, gk, *plumb.args)
    return plumb.deliver(res)


def _attn_bwd(name, qkv, bias2, sink2, gq, gk, y, dy, lse, cfg, n_batch, want_dbias, comms=()):
    t = qkv.shape[0]
    s = t // n_batch
    nq = s // Q_BLOCK
    nkc = s // KEY_CHUNK
    prev, group, kw = cfg["prev"], cfg["group"], cfg["kw"]
    w = prev + Q_BLOCK
    n_chunks = w // KEY_CHUNK
    wext = bias2.shape[1]
    plumb = _CommPlumbing(comms, 10, 7, 9)
    n_all_in = 10 + len(plumb.args)
    n_all_out = 7 + len(plumb.out_shape)

    def body(*refs):
        q_ref, k_ref, v_ref, bias_ref, sink_ref, gq_ref, gk_ref, y_ref, dy_ref, lse_ref = refs[:10]
        dq_ref, dk_ref, dv_ref, db_ref, dsink_ref, dgq_ref, dgk_ref = refs[n_all_in:n_all_in + 7]
        kn_ref, knt_ref, dkn_ref, dvs_ref, s_ref, dp_ref, pb_ref, dsb_ref, dst_ref = \
            refs[n_all_in + n_all_out:n_all_in + n_all_out + 9]
        b = pl.program_id(0)
        i = pl.program_id(1)
        first = (b == 0) & (i == 0)
        plumb.run(refs[:n_all_in], refs[n_all_in:n_all_in + n_all_out], refs[n_all_in + n_all_out:],
                  first, (b == n_batch - 1) & (i == nq - 1))

        @pl.when(i == 0)
        def _():
            for jk in range(kw // PAIR):
                cols = pl.ds(jk * PAIR, PAIR)
                heads = [h for h in range(N_HEADS) if (h // group) // 2 == jk]
                for c in range(nkc):
                    rows = pl.ds(c * KEY_CHUNK, KEY_CHUNK)
                    kn, _, _ = _pair_norm(k_ref[rows, cols].astype(F32), gk_ref[...])
                    kn_ref[rows, cols] = kn.astype(BF16)
                    _place_transposed(kn, knt_ref, c, heads, group)
            dkn_ref[...] = jnp.zeros(dkn_ref.shape, F32)
            dvs_ref[...] = jnp.zeros(dvs_ref.shape, F32)

        @pl.when(first)
        def _():
            db_ref[...] = jnp.zeros(db_ref.shape, F32)
            dsink_ref[...] = jnp.zeros(dsink_ref.shape, F32)
            dgq_ref[...] = jnp.zeros(dgq_ref.shape, F32)
            dgk_ref[...] = jnp.zeros(dgk_ref.shape, F32)

        start, off = _attn_window(i, prev)
        c0 = start // KEY_CHUNK
        for jq in range(N_HEADS // 2):
            cols = pl.ds(jq * PAIR, PAIR)
            kcols = pl.ds((((2 * jq) // group) // 2) * PAIR, PAIR)
            qn, q_hat, q_rstd = _pair_norm(q_ref[:, cols].astype(F32), gq_ref[...])
            qs = _stack_pair(qn * ATTN_SCALE, jq, group)
            do_pair = dy_ref[:, cols].astype(F32)
            dos = _stack_pair(do_pair, jq, group)
            prod_t = (do_pair * y_ref[:, cols].astype(F32)).T
            lo = _sub_lo(prod_t.shape)
            delta2 = jnp.concatenate([jnp.sum(jnp.where(lo, prod_t, 0.0), axis=0, keepdims=True),
                                      jnp.sum(jnp.where(lo, 0.0, prod_t), axis=0, keepdims=True)], axis=1)
            lse2 = jnp.concatenate([lse_ref[2 * jq:2 * jq + 1, :], lse_ref[2 * jq + 1:2 * jq + 2, :]], axis=1)
            dsk = -jnp.exp(sink_ref[jq, 0:1, :] - lse2) * delta2
            dsink_ref[2 * jq:2 * jq + 1, :] += dsk[:, :Q_BLOCK]
            dsink_ref[2 * jq + 1:2 * jq + 2, :] += dsk[:, Q_BLOCK:]
            rows_w = pl.ds(start, w)
            s_ref[...] = _dot(kn_ref[rows_w, kcols], qs, "nt")
            dp_ref[...] = _dot(v_ref[rows_w, kcols], dos, "nt")
            for c in range(n_chunks):
                r = pl.ds(c * KEY_CHUNK, KEY_CHUNK)
                brows = pl.ds(off + c * KEY_CHUNK, KEY_CHUNK)
                p = jnp.exp(s_ref[r, :] + bias_ref[jq, brows, :] - lse2)
                ds = p * (dp_ref[r, :] - delta2)
                if want_dbias:
                    db_ref[jq, brows, :] += ds
                ds_b = ds.astype(BF16)
                pb_ref[r, :] = p.astype(BF16)
                dsb_ref[r, :] = ds_b
                dst_ref[pl.ds(2 * c * KEY_CHUNK, KEY_CHUNK), :] = ds_b[:, :Q_BLOCK]
                dst_ref[pl.ds((2 * c + 1) * KEY_CHUNK, KEY_CHUNK), :] = ds_b[:, Q_BLOCK:]
            dkn_ref[rows_w, kcols] += _dot(dsb_ref[...], qs, "nn")
            dvs_ref[rows_w, kcols] += _dot(pb_ref[...], dos, "nn")
            kl = jnp.concatenate([knt_ref[2 * jq + hq, c0 + c] for c in range(n_chunks) for hq in range(2)], axis=1)
            dqt = _dot(kl, dst_ref[...], "nn")
            dq_raw, dg = _pair_norm_bwd(dqt.T * ATTN_SCALE, q_hat, q_rstd, gq_ref[...])
            dq_ref[:, cols] = dq_raw.astype(BF16)
            dgq_ref[...] += dg

        @pl.when(i == nq - 1)
        def _():
            for jk in range(kw // PAIR):
                kcols = pl.ds(jk * PAIR, PAIR)
                _, k_hat, k_rstd = _pair_norm(k_ref[:, kcols].astype(F32), gk_ref[...])
                dk_raw, dg = _pair_norm_bwd(dkn_ref[:, kcols], k_hat, k_rstd, gk_ref[...])
                dk_ref[:, kcols] = dk_raw.astype(BF16)
                dgk_ref[...] += dg
            dv_ref[...] = dvs_ref[...].astype(BF16)

    q_spec, k_spec, v_spec = _attn_specs(cfg, s, nq)
    row = pl.BlockSpec((Q_BLOCK, A_WIDTH), lambda b, i: (b * nq + i, 0))
    kv_out = pl.BlockSpec((s, kw), lambda b, i: (b, 0))
    pair_bias = _const_spec((N_HEADS // 2, wext, 2 * Q_BLOCK))
    res = pl.pallas_call(
        body, name=name, grid=(n_batch, nq),
        in_specs=[q_spec, k_spec, v_spec, pair_bias, _const_spec((N_HEADS // 2, 8, 2 * Q_BLOCK)),
                  _const_spec((1, PAIR)), _const_spec((1, PAIR)), row, row,
                  pl.BlockSpec((None, N_HEADS, Q_BLOCK), lambda b, i: (b * nq + i, 0, 0))] + [ANY] * len(plumb.args),
        out_specs=[row, kv_out, kv_out, pair_bias, _const_spec((N_HEADS, 128)),
                   _const_spec((1, PAIR)), _const_spec((1, PAIR))] + [ANY] * len(plumb.out_shape),
        out_shape=[_sds((t, A_WIDTH), BF16), _sds((t, kw), BF16), _sds((t, kw), BF16),
                   _sds((N_HEADS // 2, wext, 2 * Q_BLOCK), F32), _sds((N_HEADS, 128), F32),
                   _sds((1, PAIR), F32), _sds((1, PAIR), F32)] + plumb.out_shape,
        scratch_shapes=[pltpu.VMEM((s, kw), BF16), pltpu.VMEM((N_HEADS, nkc, PAIR, KEY_CHUNK), BF16),
                        pltpu.VMEM((s, kw), F32), pltpu.VMEM((s, kw), F32),
                        pltpu.VMEM((w, 2 * Q_BLOCK), F32), pltpu.VMEM((w, 2 * Q_BLOCK), F32),
                        pltpu.VMEM((w, 2 * Q_BLOCK), BF16), pltpu.VMEM((w, 2 * Q_BLOCK), BF16),
                        pltpu.VMEM((2 * w, Q_BLOCK), BF16)] + plumb.scratch,
        input_output_aliases=plumb.aliases,
        compiler_params=_params(),
    )(qkv, qkv, qkv, bias2, sink2, gq, gk, y, dy, lse, *plumb.args)
    return plumb.deliver(res)


def _band_tables(prev_chunks):
    prev = prev_chunks * CHUNK
    wext = 2 * prev + Q_BLOCK
    jj = np.arange(wext)[:, None]
    ii = np.arange(Q_BLOCK)[None, :]
    dist = prev + ii - jj
    rel_chunk = (prev // CHUNK + ii // CHUNK) - jj // CHUNK
    allowed = (rel_chunk >= 0) & (rel_chunk <= prev_chunks)
    return dist, allowed


def _alibi_slopes():
    return np.array([2.0 ** (-8.0 * (h + 1) / N_HEADS) for h in range(N_HEADS)], dtype=np.float32)


def _diag_onehot(prev, wext):
    n_diag = wext + Q_BLOCK - 1
    idx = np.clip(prev + Q_BLOCK - 1 - np.arange(n_diag), -A_MAX_REL, A_MAX_REL) + A_MAX_REL
    onehot = np.zeros((n_diag, 2 * A_MAX_REL + 1), np.float32)
    onehot[np.arange(n_diag), idx] = 1.0
    return onehot


def _bias_a(rel_bias):
    prev = A_PREV_CHUNKS * CHUNK
    _, allowed = _band_tables(A_PREV_CHUNKS)
    wext = allowed.shape[0]
    n_diag = wext + Q_BLOCK - 1
    seq = jnp.dot(rel_bias, jnp.asarray(_diag_onehot(prev, wext).T), precision=lax.Precision.HIGHEST)
    seq = jnp.pad(seq, ((0, 0), (0, 1)))
    rows = jnp.broadcast_to(seq[:, None, :], (N_HEADS, Q_BLOCK, n_diag + 1)).reshape(N_HEADS, -1)
    skew = rows[:, :Q_BLOCK * n_diag].reshape(N_HEADS, Q_BLOCK, n_diag)
    tile = jnp.transpose(skew[:, :, Q_BLOCK - 1:Q_BLOCK - 1 + wext], (0, 2, 1))
    return jnp.where(jnp.asarray(allowed)[None], tile, NEG_INF)


def _bias_b():
    dist, allowed = _band_tables(B_PREV_CHUNKS)
    bias = -_alibi_slopes()[:, None, None] * np.abs(dist).astype(np.float32)[None]
    return jnp.asarray(np.where(allowed[None], bias, np.float32(NEG_INF)).astype(np.float32))


def _rel_bias_grad(db_t):
    prev = A_PREV_CHUNKS * CHUNK
    wext = db_t.shape[1]
    n_diag = wext + Q_BLOCK - 1
    wp = n_diag + Q_BLOCK - 1
    xp = jnp.pad(jnp.transpose(db_t, (0, 2, 1)), ((0, 0), (0, 0), (Q_BLOCK - 1, Q_BLOCK - 1)))
    flat = jnp.pad(xp.reshape(N_HEADS, Q_BLOCK * wp), ((0, 0), (0, Q_BLOCK)))
    skew = flat.reshape(N_HEADS, Q_BLOCK, wp + 1)[:, :, :n_diag]
    diag = jnp.sum(skew, axis=1)
    return jnp.dot(diag, jnp.asarray(_diag_onehot(prev, wext)), precision=lax.Precision.HIGHEST)


def _ew(name, fn, ins, out_dtypes):
    r, c = ins[0].shape
    rb = _pick(r, max(16, (1 << 19) // c), 16)
    spec = pl.BlockSpec((rb, c), lambda i: (i, 0))

    def body(*refs):
        vals = fn(*[ref[...] for ref in refs[:len(ins)]])
        for ref, val in zip(refs[len(ins):], vals):
            ref[...] = val.astype(ref.dtype)

    return pl.pallas_call(
        body, name=name, grid=(r // rb,), in_specs=[spec] * len(ins), out_specs=[spec] * len(out_dtypes),
        out_shape=[_sds((r, c), dt) for dt in out_dtypes], compiler_params=_params(),
    )(*ins)


def _adamw_math(w, g, m, v):
    m = ADAM_B1 * m + (1.0 - ADAM_B1) * g
    v = ADAM_B2 * v + (1.0 - ADAM_B2) * (g * g)
    m_hat = m / (1.0 - ADAM_B1 ** ADAM_STEP)
    v_hat = v / (1.0 - ADAM_B2 ** ADAM_STEP)
    delta = -ADAM_LR * (m_hat / (jnp.sqrt(v_hat) + ADAM_EPS) + ADAM_WD * w)
    return delta, m, v


def _adamw_terms(name, terms, w, m, v):
    r, c = w.shape
    hr = r // 2
    rb = _pick(hr, max(16, (1 << 19) // c), 16)
    nb = hr // rb

    def body(t_ref, w_ref, m_ref, v_ref, g_ref, d_ref, nm_ref, nv_ref):
        g = t_ref[0].astype(F32)
        for k in range(1, N_CHIPS):
            g = g + t_ref[k].astype(F32)
        delta, nm, nv = _adamw_math(w_ref[...], g, m_ref[...], v_ref[...])
        g_ref[...] = g
        d_ref[...] = delta
        nm_ref[...] = nm
        nv_ref[...] = nv

    spec = pl.BlockSpec((rb, c), lambda h, i: (h * nb + i, 0))
    return pl.pallas_call(
        body, name=name, grid=(2, nb),
        in_specs=[pl.BlockSpec((None, N_CHIPS, rb, c), lambda h, i: (h, 0, i, 0)), spec, spec, spec],
        out_specs=[spec] * 4, out_shape=[_sds((r, c), F32)] * 4, compiler_params=_params(),
    )(terms, w, m, v)


def _mesh_place():
    x, y, c = lax.axis_index("x"), lax.axis_index("y"), lax.axis_index("c")
    chips = [(x, 1 - y), (1 - x, y), (1 - x, 1 - y)]
    return x, y, c, chips


def _all_gather_weights(shards):
    n = len(shards)

    def body(*refs):
        ins, outs = refs[:n], refs[n:2 * n]
        local_sem, ici_send, ici_recv, d2d_send, d2d_recv = refs[2 * n:]
        x, y, c, chips = _mesh_place()
        me = 2 * x + y
        sibling = (x, y, 1 - c)
        local, sent = [], []
        for wi in range(n):
            loc = pltpu.make_async_copy(ins[wi], outs[wi].at[me], local_sem.at[wi])
            loc.start()
            local.append(loc)
            for k, (tx, ty) in enumerate(chips):
                for pi, rows in enumerate(_row_pieces(shards[wi].shape[1])):
                    sem = (wi * 3 + k) * GATHER_PIECES + pi
                    cp = pltpu.make_async_remote_copy(
                        src_ref=ins[wi].at[c, rows], dst_ref=outs[wi].at[me, c, rows],
                        send_sem=ici_send.at[sem], recv_sem=ici_recv.at[sem],
                        device_id=(tx, ty, c), device_id_type=MESH)
                    cp.start()
                    sent.append(cp)
        passed = []
        for wi in range(n):
            for k, (tx, ty) in enumerate(chips):
                for pi, rows in enumerate(_row_pieces(shards[wi].shape[1])):
                    sem = (wi * 3 + k) * GATHER_PIECES + pi
                    slab = outs[wi].at[2 * tx + ty, c, rows]
                    pltpu.make_async_remote_copy(
                        src_ref=slab, dst_ref=slab, send_sem=ici_send.at[sem], recv_sem=ici_recv.at[sem],
                        device_id=(tx, ty, c), device_id_type=MESH).wait_recv()
                    fw = pltpu.make_async_remote_copy(
                        src_ref=slab, dst_ref=slab, send_sem=d2d_send.at[sem], recv_sem=d2d_recv.at[sem],
                        device_id=sibling, device_id_type=MESH)
                    fw.start()
                    passed.append(fw)
        for wi in range(n):
            for k, (tx, ty) in enumerate(chips):
                for pi, rows in enumerate(_row_pieces(shards[wi].shape[1])):
                    sem = (wi * 3 + k) * GATHER_PIECES + pi
                    slab = outs[wi].at[2 * tx + ty, 1 - c, rows]
                    pltpu.make_async_remote_copy(
                        src_ref=slab, dst_ref=slab, send_sem=d2d_send.at[sem], recv_sem=d2d_recv.at[sem],
                        device_id=sibling, device_id_type=MESH).wait_recv()
        for loc in local:
            loc.wait()
        for cp in sent + passed:
            cp.wait_send()

    return pl.pallas_call(
        body, name="all_gather_weights",
        in_specs=[ANY] * n, out_specs=[ANY] * n,
        out_shape=[_sds((N_CHIPS,) + s.shape, s.dtype) for s in shards],
        scratch_shapes=[pltpu.SemaphoreType.DMA((n,))] + [pltpu.SemaphoreType.DMA((3 * n * GATHER_PIECES,))] * 4,
    )(*shards)


def _run_comms(name, comms):
    plumb = _CommPlumbing(comms, 0, 0, 0)
    n_in, n_out = len(plumb.args), len(plumb.out_shape)

    def body(*refs):
        parts = []
        i0, o0, s0 = 0, n_in, n_in + n_out
        for cm in plumb.comms:
            parts.append((refs[i0:i0 + len(cm.ins)], refs[o0:o0 + len(cm.outs)], refs[s0:s0 + len(cm.sems)]))
            i0 += len(cm.ins)
            o0 += len(cm.outs)
            s0 += len(cm.sems)
        for cm, part in zip(plumb.comms, parts):
            cm.start(*part)
        for cm, part in zip(plumb.comms, parts):
            cm.finish(*part)

    res = pl.pallas_call(
        body, name=name, in_specs=[ANY] * n_in, out_specs=[ANY] * n_out, out_shape=plumb.out_shape,
        scratch_shapes=plumb.scratch, input_output_aliases=plumb.aliases,
    )(*plumb.args)
    plumb.deliver(res)


GATHER_PIECES = 4
BF16_TILE_ROWS = 16


def _row_pieces(rows):
    n = GATHER_PIECES
    while rows % (n * BF16_TILE_ROWS):
        n //= 2
    return [pl.ds(i * (rows // n), rows // n) for i in range(n)]


def _gather_ici(shards):
    n = len(shards)

    def copies(ins, outs, sems):
        local_sem, send_sem, recv_sem = sems
        x, y, c, chips = _mesh_place()
        me = 2 * x + y
        local, sends, recvs = [], [], []
        for wi in range(n):
            local.append(pltpu.make_async_copy(ins[wi], outs[wi].at[me], local_sem.at[wi]))
            pieces = _row_pieces(shards[wi].shape[1])
            for k, (tx, ty) in enumerate(chips):
                for pi, rows in enumerate(pieces):
                    sem = (wi * 3 + k) * GATHER_PIECES + pi
                    sems_k = dict(send_sem=send_sem.at[sem], recv_sem=recv_sem.at[sem],
                                  device_id=(tx, ty, c), device_id_type=MESH)
                    sends.append(pltpu.make_async_remote_copy(
                        src_ref=ins[wi].at[c, rows], dst_ref=outs[wi].at[me, c, rows], **sems_k))
                    slab = outs[wi].at[2 * tx + ty, c, rows]
                    recvs.append(pltpu.make_async_remote_copy(src_ref=slab, dst_ref=slab, **sems_k))
        return local, sends, recvs

    def start(ins, outs, sems):
        local, sends, _ = copies(ins, outs, sems)
        for cp in local + sends:
            cp.start()

    def finish(ins, outs, sems):
        local, sends, recvs = copies(ins, outs, sems)
        for cp in local:
            cp.wait()
        for cp in recvs:
            cp.wait_recv()
        for cp in sends:
            cp.wait_send()

    return _Comm(shards, [_sds((N_CHIPS,) + s.shape, s.dtype) for s in shards], {},
                 [pltpu.SemaphoreType.DMA((n,)), pltpu.SemaphoreType.DMA((3 * n * GATHER_PIECES,)),
                  pltpu.SemaphoreType.DMA((3 * n * GATHER_PIECES,))], start, finish)


def _gather_d2d(gathered):
    n = len(gathered)

    def copies(outs, sems):
        send_sem, recv_sem = sems
        x, y, c, chips = _mesh_place()
        sends, recvs = [], []
        for wi in range(n):
            for k, (tx, ty) in enumerate(chips):
                sems_k = dict(send_sem=send_sem.at[wi * 3 + k], recv_sem=recv_sem.at[wi * 3 + k],
                              device_id=(x, y, 1 - c), device_id_type=MESH)
                mine = outs[wi].at[2 * tx + ty, c]
                theirs = outs[wi].at[2 * tx + ty, 1 - c]
                sends.append(pltpu.make_async_remote_copy(src_ref=mine, dst_ref=mine, **sems_k))
                recvs.append(pltpu.make_async_remote_copy(src_ref=theirs, dst_ref=theirs, **sems_k))
        return sends, recvs

    def start(ins, outs, sems):
        for cp in copies(outs, sems)[0]:
            cp.start()

    def finish(ins, outs, sems):
        sends, recvs = copies(outs, sems)
        for cp in recvs:
            cp.wait_recv()
        for cp in sends:
            cp.wait_send()

    return _Comm(gathered, [_sds(g.shape, g.dtype) for g in gathered], {i: i for i in range(n)},
                 [pltpu.SemaphoreType.DMA((3 * n,)), pltpu.SemaphoreType.DMA((3 * n,))], start, finish)


def _exchange_halves(grads):
    n = len(grads)

    def copies(ins, outs, sems):
        send_sem, recv_sem = sems
        x, y, c, _ = _mesh_place()
        return [pltpu.make_async_remote_copy(
            src_ref=ins[wi].at[t, 1 - c], dst_ref=outs[wi].at[t],
            send_sem=send_sem.at[wi * N_CHIPS + t], recv_sem=recv_sem.at[wi * N_CHIPS + t],
            device_id=(x, y, 1 - c), device_id_type=MESH) for wi in range(n) for t in range(N_CHIPS)]

    def start(ins, outs, sems):
        for cp in copies(ins, outs, sems):
            cp.start()

    def finish(ins, outs, sems):
        for cp in copies(ins, outs, sems):
            cp.wait()

    return _Comm(grads, [_sds((N_CHIPS,) + g.shape[2:], g.dtype) for g in grads], {},
                 [pltpu.SemaphoreType.DMA((N_CHIPS * n,)), pltpu.SemaphoreType.DMA((N_CHIPS * n,))], start, finish)


def _scatter_ici(sums):
    n = len(sums)

    def copies(ins, outs, sems):
        local_sem, send_sem, recv_sem = sems
        x, y, c, chips = _mesh_place()
        me = 2 * x + y
        local, sends, recvs = [], [], []
        for wi in range(n):
            local.append(pltpu.make_async_copy(ins[wi].at[me], outs[wi].at[c, 0], local_sem.at[wi]))
            for k, (tx, ty) in enumerate(chips):
                sems_k = dict(send_sem=send_sem.at[wi * 3 + k], recv_sem=recv_sem.at[wi * 3 + k],
                              device_id=(tx, ty, c), device_id_type=MESH)
                land = outs[wi].at[c, k + 1]
                sends.append(pltpu.make_async_remote_copy(src_ref=ins[wi].at[2 * tx + ty], dst_ref=land, **sems_k))
                recvs.append(pltpu.make_async_remote_copy(src_ref=land, dst_ref=land, **sems_k))
        return local, sends, recvs

    def start(ins, outs, sems):
        local, sends, _ = copies(ins, outs, sems)
        for cp in local + sends:
            cp.start()

    def finish(ins, outs, sems):
        local, sends, recvs = copies(ins, outs, sems)
        for cp in local:
            cp.wait()
        for cp in recvs:
            cp.wait_recv()
        for cp in sends:
            cp.wait_send()

    return _Comm(sums, [_sds((2, N_CHIPS) + s.shape[1:], s.dtype) for s in sums], {},
                 [pltpu.SemaphoreType.DMA((n,)), pltpu.SemaphoreType.DMA((3 * n,)), pltpu.SemaphoreType.DMA((3 * n,))],
                 start, finish)


def _scatter_d2d(terms):
    n = len(terms)

    def copies(outs, sems):
        send_sem, recv_sem = sems
        x, y, c, _ = _mesh_place()
        sends, recvs = [], []
        for wi in range(n):
            sems_w = dict(send_sem=send_sem.at[wi], recv_sem=recv_sem.at[wi],
                          device_id=(x, y, 1 - c), device_id_type=MESH)
            sends.append(pltpu.make_async_remote_copy(src_ref=outs[wi].at[c], dst_ref=outs[wi].at[c], **sems_w))
            recvs.append(pltpu.make_async_remote_copy(src_ref=outs[wi].at[1 - c], dst_ref=outs[wi].at[1 - c], **sems_w))
        return sends, recvs

    def start(ins, outs, sems):
        for cp in copies(outs, sems)[0]:
            cp.start()

    def finish(ins, outs, sems):
        sends, recvs = copies(outs, sems)
        for cp in recvs:
            cp.wait_recv()
        for cp in sends:
            cp.wait_send()

    return _Comm(terms, [_sds(t.shape, t.dtype) for t in terms], {i: i for i in range(n)},
                 [pltpu.SemaphoreType.DMA((n,)), pltpu.SemaphoreType.DMA((n,))], start, finish)


def _chip_sum(name, grad, got, core):
    _, _, hr, c = grad.shape
    rb = _pick(hr, max(16, (1 << 19) // c), 16)

    def body(core_ref, a_ref, b_ref, o_ref):
        o_ref[...] = (a_ref[...].astype(F32) + b_ref[...].astype(F32)).astype(BF16)

    out_spec = pl.BlockSpec((None, rb, c), lambda t, i, core_ref: (t, i, 0))
    return pl.pallas_call(
        body, name=name,
        grid_spec=pltpu.PrefetchScalarGridSpec(
            num_scalar_prefetch=1, grid=(N_CHIPS, hr // rb),
            in_specs=[pl.BlockSpec((None, None, rb, c), lambda t, i, core_ref: (t, core_ref[0], i, 0)), out_spec],
            out_specs=out_spec),
        out_shape=_sds((N_CHIPS, hr, c), BF16), compiler_params=_params(),
    )(core, grad, got)


def _all_reduce_small(pack):
    r = pack.shape[0]

    def body(p_ref, o_ref, land_ref, send_sem, recv_sem):
        x, y, c, _ = _mesh_place()
        me = 4 * x + 2 * y + c
        flips = [(k >> 2 & 1, k >> 1 & 1, k & 1) for k in range(1, N_DEV)]

        def peer(fx, fy, fc):
            return (1 - x if fx else x, 1 - y if fy else y, 1 - c if fc else c)

        land_ref[me] = p_ref[...]
        sent = []
        for k, flip in enumerate(flips):
            cp = pltpu.make_async_remote_copy(
                src_ref=p_ref, dst_ref=land_ref.at[me], send_sem=send_sem.at[k], recv_sem=recv_sem.at[k],
                device_id=peer(*flip), device_id_type=MESH)
            cp.start()
            sent.append(cp)
        for k, flip in enumerate(flips):
            px, py, pc = peer(*flip)
            slot = land_ref.at[4 * px + 2 * py + pc]
            pltpu.make_async_remote_copy(
                src_ref=slot, dst_ref=slot, send_sem=send_sem.at[k], recv_sem=recv_sem.at[k],
                device_id=(px, py, pc), device_id_type=MESH).wait_recv()
        total = land_ref[0]
        for d in range(1, N_DEV):
            total = total + land_ref[d]
        o_ref[...] = total
        for cp in sent:
            cp.wait_send()

    vmem = pl.BlockSpec(memory_space=pltpu.VMEM)
    return pl.pallas_call(
        body, name="all_reduce_small", in_specs=[vmem], out_specs=vmem, out_shape=_sds((r, 128), F32),
        scratch_shapes=[pltpu.VMEM((N_DEV, r, 128), F32), pltpu.SemaphoreType.DMA((N_DEV - 1,)),
                        pltpu.SemaphoreType.DMA((N_DEV - 1,))],
    )(pack)


PACK_TILE = 8 * 128


def _pack(items):
    rows, i = [], 0
    while i < len(items):
        j = i
        while j < len(items) and items[j].size == items[i].size:
            j += 1
        group = jnp.stack([it.reshape(-1).astype(F32) for it in items[i:j]])
        rows.append(jnp.pad(group, ((0, 0), (0, -group.shape[1] % PACK_TILE))).reshape(-1, 128))
        i = j
    return jnp.concatenate(rows, axis=0)


def _unpack(pack, shapes):
    out, row = [], 0
    for shp in shapes:
        size = int(np.prod(shp))
        nrow = -(-size // PACK_TILE) * (PACK_TILE // 128)
        out.append(pack[row:row + nrow].reshape(-1)[:size].reshape(shp))
        row += nrow
    return out


BIG = ["ffn1_w_gu", "ffn1_w_down", "w_in", "w_gate", "w_proj_a", "w_proj_b", "w_out",
       "ffn2_w_gu", "ffn2_w_down", "w_ple_gate", "w_ple_proj"]
SMALL = ["ffn1_norm", "mix_norm", "ffn2_norm", "ple_norm", "a_q_norm", "a_k_norm", "b_q_norm", "b_k_norm",
         "a_rel_bias", "b_sinks"]
WEIGHTS = ["ffn1_norm", "ffn1_w_gu", "ffn1_w_down", "mix_norm", "w_in", "a_q_norm", "a_k_norm", "a_rel_bias",
           "b_q_norm", "b_k_norm", "b_sinks", "w_gate", "w_proj_a", "w_proj_b", "w_out", "ffn2_norm",
           "ffn2_w_gu", "ffn2_w_down", "ple_norm", "w_ple_gate", "w_ple_proj"]
ATTN_A = dict(prev=A_PREV_CHUNKS * CHUNK, group=1, kw=A_WIDTH, qblk=0, kblk=1, vblk=2)
ATTN_B = dict(prev=B_PREV_CHUNKS * CHUNK, group=N_HEADS // B_KV_HEADS, kw=B_KV_WIDTH, qblk=3,
              kblk=4 * A_WIDTH // B_KV_WIDTH, vblk=4 * A_WIDTH // B_KV_WIDTH + 1)


def _cast_epilogue(accs, extras, outs, ij):
    for acc, out in zip(accs, outs):
        out[...] = acc.astype(out.dtype)


GATHER_FIRST = ["ffn1_w_gu", "ffn1_w_down"]
ROW_SHARDED = ("ffn1_w_down", "ffn2_w_down", "w_out", "w_ple_gate")


def _slotted(name, grad):
    if name == "w_in":
        rows, cols = grad.shape
        grad = jnp.transpose(grad.reshape(rows, N_CHIPS, cols // N_CHIPS), (1, 0, 2))
    elif name in ROW_SHARDED:
        grad = grad.reshape(N_CHIPS, grad.shape[0] // N_CHIPS, grad.shape[1])
    return grad.reshape(N_CHIPS, 2, grad.shape[1] // 2, grad.shape[2])


def _local_step(xt, pt, tgt, n_batch, shards, small, core):
    t, d = xt.shape
    tm = _pick(t, 512, 8)
    tk = _pick(t, 512, 8)
    nt = t // tm
    row = pl.BlockSpec((tm, d), lambda i, j, k: (i, 0))
    gs = shards["w_gate"].shape[1]
    ps = shards["w_proj_a"].shape[1]
    es = shards["w_ple_proj"].shape[1]
    pdim = pt.shape[1]
    ncols = N_CHIPS * shards["w_in"].shape[1]
    tin = ncols // 2
    assert 2 * gs == d and 4 * ps == d and 4 * es == d and tin % 128 == 0

    w = {}
    halves = {n: s.reshape(2, s.shape[0] // 2, s.shape[1]) for n, s in shards.items()}

    def publish(names, arrays):
        for name, g in zip(names, arrays):
            g = g.reshape(N_CHIPS, 2 * g.shape[2], g.shape[3])
            if name in ROW_SHARDED:
                g = g.reshape(N_CHIPS * g.shape[1], g.shape[2])
            elif name == "w_in":
                g = jnp.transpose(g, (1, 0, 2)).reshape(g.shape[1], N_CHIPS * g.shape[2])
            w[name] = g

    class GatherPipe:
        def __init__(self, names):
            self.names = names

        def ici(self):
            self.first = _gather_ici([halves[n] for n in self.names])
            return self.first

        def d2d(self):
            self.second = _gather_d2d(self.first.results)
            return self.second

        def publish(self):
            publish(self.names, self.second.results)

    class GradPipe:
        def __init__(self, names):
            self.names = names

        def exchange(self, grads):
            self.grads = [_slotted(n, g) for n, g in zip(self.names, grads)]
            self.x = _exchange_halves(self.grads)
            return self.x

        def scatter(self):
            sums = [_chip_sum("chip_sum_" + n, g, got, core)
                    for n, g, got in zip(self.names, self.grads, self.x.results)]
            self.s = _scatter_ici(sums)
            return self.s

        def forward(self):
            self.f = _scatter_d2d(self.s.results)
            return self.f

        def terms(self):
            return dict(zip(self.names, self.f.results))

    publish(GATHER_FIRST, _all_gather_weights([halves[n] for n in GATHER_FIRST]))
    g_in, g_proj, g_ple = GatherPipe(["w_in", "w_gate"]), GatherPipe(["w_proj_a", "w_proj_b", "w_out"]), \
        GatherPipe(["w_ple_gate", "w_ple_proj"])
    g_down2, g_up2 = GatherPipe(["ffn2_w_down"]), GatherPipe(["ffn2_w_gu"])
    h1, ffn1_saved = _ffn_fwd("ffn1", xt, small["ffn1_norm"], w["ffn1_w_gu"], w["ffn1_w_down"],
                              {"up": lambda: [g_in.ici()], "down": lambda: [g_in.d2d(), g_proj.ici()]})
    g_in.publish()
    w_in, wgate = w["w_in"], w["w_gate"]
    un = _rms_fwd("mix_norm", h1, small["mix_norm"])
    (qkv,) = _mm(
        "qkv", "nn", (nt, 2, 1),
        [(un, row, w_in, pl.BlockSpec((d, tin), lambda i, j, k: (0, j)))], [],
        [(_sds((t, ncols), BF16), pl.BlockSpec((tm, tin), lambda i, j, k: (i, j)))], (tm, tin), _cast_epilogue,
        j_outer=True, comms=[g_proj.d2d(), g_ple.ici()])
    g_proj.publish()
    wpa, wpb, wout = w["w_proj_a"], w["w_proj_b"], w["w_out"]

    def gate_epilogue(accs, extras, outs, ij):
        outs[0][...] = jax.nn.sigmoid(accs[0]).astype(BF16)

    (gates,) = _mm(
        "gate", "nn", (nt, 4, 1),
        [(un, row, wgate, pl.BlockSpec((None, d, gs), lambda i, j, k: (j, 0, 0)))], [],
        [(_sds((2, t, d), BF16), pl.BlockSpec((None, tm, gs), lambda i, j, k: (j // 2, i, j % 2)))],
        (tm, gs), gate_epilogue, j_outer=True, chunked=True, comms=[g_ple.d2d(), g_down2.ici()])
    g_ple.publish()
    wpg, wpe = w["w_ple_gate"], w["w_ple_proj"]

    bias_a = _pair_bias(_bias_a(small["a_rel_bias"][0]))
    bias_b = _pair_bias(_bias_b())
    sink_a = _pair_rows(jnp.full((N_HEADS, 128), NEG_INF, F32))
    sink_b = _pair_rows(jnp.broadcast_to(small["b_sinks"][0][:, None], (N_HEADS, 128)))
    gqa, gka, gqb, gkb = [jnp.tile(small[k], (1, 2)) for k in ("a_q_norm", "a_k_norm", "b_q_norm", "b_k_norm")]
    ya, lse_a = _attn_fwd("attn_a_fwd", qkv, bias_a, sink_a, gqa, gka, ATTN_A, n_batch,
                          comms=[g_down2.d2d(), g_up2.ici()])
    g_down2.publish()
    yb, lse_b = _attn_fwd("attn_b_fwd", qkv, bias_b, sink_b, gqb, gkb, ATTN_B, n_batch, comms=[g_up2.d2d()])
    g_up2.publish()

    def merge_epilogue(accs, extras, outs, ij):
        pa, pb = accs
        outs[0][...] = (extras[0][...].astype(F32) * pa + extras[1][...].astype(F32) * pb).astype(BF16)
        outs[1][...] = pa.astype(BF16)
        outs[2][...] = pb.astype(BF16)

    y_spec = pl.BlockSpec((tm, A_WIDTH), lambda i, j, k: (i, 0))
    proj_spec = pl.BlockSpec((None, A_WIDTH, ps), lambda i, j, k: (j, 0, 0))
    tile_ps = pl.BlockSpec((tm, ps), lambda i, j, k: (i, j))
    merged, pa, pb = _mm(
        "proj_merge", "nn", (nt, 4, 1),
        [(ya, y_spec, wpa, proj_spec), (yb, y_spec, wpb, proj_spec)],
        [(gates, pl.BlockSpec((None, tm, ps), lambda i, j, k: (0, i, j))),
         (gates, pl.BlockSpec((None, tm, ps), lambda i, j, k: (1, i, j)))],
        [(_sds((t, d), BF16), tile_ps)] * 3, (tm, ps), merge_epilogue)

    def residual_epilogue(accs, extras, outs, ij):
        outs[0][...] = extras[0][...] + accs[0]

    (h2,) = _mm(
        "out_proj", "nn", (nt, 1, 1),
        [(merged, row, wout, pl.BlockSpec((d, d), lambda i, j, k: (0, 0)))],
        [(h1, row)], [(_sds((t, d), F32), row)], (tm, d), residual_epilogue)

    h3, ffn2_saved = _ffn_fwd("ffn2", h2, small["ffn2_norm"], w["ffn2_w_gu"], w["ffn2_w_down"], {})
    n3 = _rms_fwd("ple_norm", h3, small["ple_norm"])
    tile_es = pl.BlockSpec((tm, es), lambda i, j, k: (i, j))
    (pe,) = _mm(
        "ple_embed", "nn", (nt, 4, 1),
        [(pt, pl.BlockSpec((tm, pdim), lambda i, j, k: (i, 0)), wpe, pl.BlockSpec((None, pdim, es), lambda i, j, k: (j, 0, 0)))],
        [], [(_sds((t, d), F32), tile_es)], (tm, es), _cast_epilogue)

    th = _pick(d, 512)

    def head_epilogue(accs, extras, outs, ij):
        h3_ref, pe_ref, tgt_ref = extras
        dy_ref, dpe_ref, dz_ref, loss_ref = outs
        pg = jax.nn.sigmoid(accs[0])
        pev = pe_ref[...]
        diff = h3_ref[...] + pg * pev - tgt_ref[...]
        dy = diff * (1.0 / d)
        dy_ref[...] = dy
        dpe_ref[...] = (dy * pg).astype(BF16)
        dz_ref[...] = (dy * pev * pg * (1.0 - pg)).astype(BF16)
        _accumulate(loss_ref, jnp.full(loss_ref.shape, jnp.sum(diff * diff), F32), (ij[0] == 0) & (ij[1] == 0))

    tile_h = pl.BlockSpec((tm, th), lambda i, j, k: (i, j))
    dy, dpe, dz, loss_acc = _mm(
        "ple_gate_loss", "nn", (nt, d // th, 1),
        [(n3, row, wpg, pl.BlockSpec((d, th), lambda i, j, k: (0, j)))],
        [(h3, tile_h), (pe, tile_h), (tgt, tile_h)],
        [(_sds((t, d), F32), tile_h), (_sds((t, d), BF16), tile_h), (_sds((t, d), BF16), tile_h),
         (_sds((8, 128), F32), pl.BlockSpec((8, 128), lambda i, j, k: (0, 0)))],
        (tm, th), head_epilogue, j_outer=True, chunked=True)
    loss = 0.5 * loss_acc[0, 0] / d

    nk = t // tk
    (dwpe,) = _mm(
        "d_w_ple_proj", "tn", (1, 4, nk),
        [(pt, pl.BlockSpec((tk, pdim), lambda i, j, k: (k, 0)), dpe, pl.BlockSpec((tk, es), lambda i, j, k: (k, j)))],
        [], [(_sds((4, pdim, es), BF16), pl.BlockSpec((None, pdim, es), lambda i, j, k: (j, 0, 0)))],
        (pdim, es), _cast_epilogue)

    def dense_grad(name, a, dyb, comms=()):
        (res,) = _mm(
            name, "tn", (1, d // th, nk),
            [(a, pl.BlockSpec((tk, d), lambda i, j, k: (k, 0)), dyb, pl.BlockSpec((tk, th), lambda i, j, k: (k, j)))],
            [], [(_sds((d, d), BF16), pl.BlockSpec((d, th), lambda i, j, k: (0, j)))], (d, th), _cast_epilogue,
            comms=comms)
        return res

    dwpg = dense_grad("d_w_ple_gate", n3, dz)
    tmn = _pick(t, 1024, 8)
    extras, outs = _rms_bwd_io(h3, small["ple_norm"], dy, tmn)
    dh3, dh3_b, d_ple_norm = _mm(
        "d_ple_norm", "nt", (t // tmn, 1, 1),
        [(dz, pl.BlockSpec((tmn, d), lambda i, j, k: (i, 0)), wpg, pl.BlockSpec((d, d), lambda i, j, k: (0, 0)))],
        extras, outs, (tmn, d), _rms_bwd_epilogue)

    up2, down2, ple = GradPipe(["ffn2_w_gu"]), GradPipe(["ffn2_w_down"]), GradPipe(["w_ple_gate", "w_ple_proj"])
    proj = GradPipe(["w_proj_a", "w_proj_b", "w_out"])
    dh2, dh2_b, d_ffn2_norm, dwgu2, dwd2 = _ffn_bwd(
        "ffn2", dh3, dh3_b, h2, small["ffn2_norm"], w["ffn2_w_gu"], w["ffn2_w_down"], ffn2_saved,
        {"dnorm": lambda dwgu, dwd: [up2.exchange([dwgu]), down2.exchange([dwd]), ple.exchange([dwpg, dwpe])]})

    def dmerge_epilogue(accs, extras, outs, ij):
        dmo = accs[0]
        g_ref, pa_ref, pb_ref = extras
        dg_ref, dpa_ref, dpb_ref = outs
        ga = g_ref[0].astype(F32)
        gb = g_ref[1].astype(F32)
        dg_ref[0] = (dmo * pa_ref[...].astype(F32) * ga * (1.0 - ga)).astype(BF16)
        dg_ref[1] = (dmo * pb_ref[...].astype(F32) * gb * (1.0 - gb)).astype(BF16)
        dpa_ref[...] = (dmo * ga).astype(BF16)
        dpb_ref[...] = (dmo * gb).astype(BF16)

    g_spec = pl.BlockSpec((2, tm, th), lambda i, j, k: (0, i, j))
    dgates, dpa, dpb = _mm(
        "d_merge", "nt", (nt, d // th, 1),
        [(dh2_b, row, wout, pl.BlockSpec((th, d), lambda i, j, k: (j, 0)))],
        [(gates, g_spec), (pa, tile_h), (pb, tile_h)],
        [(_sds((2, t, d), BF16), g_spec), (_sds((t, d), BF16), tile_h), (_sds((t, d), BF16), tile_h)],
        (tm, th), dmerge_epilogue, j_outer=True, chunked=True, comms=[down2.scatter()])
    dwout = dense_grad("d_w_out", merged, dh2_b, comms=[down2.forward(), ple.scatter()])

    yk_spec = pl.BlockSpec((tk, A_WIDTH), lambda i, j, k: (k, 0))
    dk_spec = pl.BlockSpec((tk, ps), lambda i, j, k: (k, j))
    dproj = (_sds((4, A_WIDTH, ps), BF16), proj_spec)
    dwpa, dwpb = _mm(
        "d_w_proj", "tn", (1, 4, nk),
        [(ya, yk_spec, dpa, dk_spec), (yb, yk_spec, dpb, dk_spec)], [], [dproj, dproj], (A_WIDTH, ps), _cast_epilogue,
        comms=[ple.forward()])
    dproj_a = pl.BlockSpec((tm, ps), lambda i, j, k: (i, k))
    wproj_k = pl.BlockSpec((None, A_WIDTH, ps), lambda i, j, k: (k, 0, 0))
    dya, dyb = _mm(
        "d_attn_out", "nt", (nt, 1, 4),
        [(dpa, dproj_a, wpa, wproj_k), (dpb, dproj_a, wpb, wproj_k)], [],
        [(_sds((t, A_WIDTH), BF16), y_spec)] * 2, (tm, A_WIDTH), _cast_epilogue,
        comms=[proj.exchange([dwpa, dwpb, dwout])])

    dqa, dka, dva, dbias_a, _, dgqa, dgka = _attn_bwd(
        "attn_a_bwd", qkv, bias_a, sink_a, gqa, gka, ya, dya, lse_a, ATTN_A, n_batch, True,
        comms=[up2.scatter(), proj.scatter()])
    dqb, dkb, dvb, _, dsink_b, dgqb, dgkb = _attn_bwd(
        "attn_b_bwd", qkv, bias_b, sink_b, gqb, gkb, yb, dyb, lse_b, ATTN_B, n_batch, False,
        comms=[up2.forward(), proj.forward()])
    dqkv = jnp.concatenate([dqa, dka, dva, dqb, dkb, dvb], axis=1)

    (dwgate,) = _mm(
        "d_w_gate", "tn", (1, 4, nk),
        [(un, pl.BlockSpec((tk, d), lambda i, j, k: (k, 0)),
          dgates, pl.BlockSpec((None, tk, gs), lambda i, j, k: (j // 2, k, j % 2)))],
        [], [(_sds((4, d, gs), BF16), pl.BlockSpec((None, d, gs), lambda i, j, k: (j, 0, 0)))], (d, gs), _cast_epilogue)
    (dwin,) = _mm(
        "d_w_in", "tn", (1, 2, nk),
        [(un, pl.BlockSpec((tk, d), lambda i, j, k: (k, 0)), dqkv, pl.BlockSpec((tk, tin), lambda i, j, k: (k, j)))],
        [], [(_sds((d, ncols), BF16), pl.BlockSpec((d, tin), lambda i, j, k: (0, j)))], (d, tin), _cast_epilogue)

    mixer = GradPipe(["w_in", "w_gate"])
    extras, outs = _rms_bwd_io(h1, small["mix_norm"], dh2, tmn)
    dh1, dh1_b, d_mix_norm = _mm(
        "d_mix_norm", "nt", (t // tmn, 1, 6),
        [(dgates, pl.BlockSpec((None, tmn, gs), lambda i, j, k: (jnp.minimum(k, 3) // 2, i, jnp.minimum(k, 3) % 2)),
          wgate, pl.BlockSpec((None, d, gs), lambda i, j, k: (jnp.minimum(k, 3), 0, 0))),
         (dqkv, pl.BlockSpec((tmn, tin), lambda i, j, k: (i, jnp.maximum(k - 4, 0))),
          w_in, pl.BlockSpec((d, tin), lambda i, j, k: (0, jnp.maximum(k - 4, 0))))],
        extras, outs, (tmn, d), _rms_bwd_epilogue, steps=[4, 2],
        comms=[mixer.exchange([dwin, dwgate])])

    up1 = GradPipe(["ffn1_w_gu"])
    down1 = GradPipe(["ffn1_w_down"])
    dx, _, d_ffn1_norm, _, _ = _ffn_bwd(
        "ffn1", dh1, dh1_b, xt, small["ffn1_norm"], w["ffn1_w_gu"], w["ffn1_w_down"], ffn1_saved,
        {"dact": lambda: [mixer.scatter()],
         "dwgu": lambda: [mixer.forward()],
         "dwd": lambda dwgu: [up1.exchange([dwgu])],
         "dnorm": lambda dwgu, dwd: [up1.scatter(), down1.exchange([dwd])]})
    _run_comms("grad_tail_scatter", [up1.forward(), down1.scatter()])
    _run_comms("grad_tail_forward", [down1.forward()])
    terms = {}
    for pipe in (up2, down2, ple, proj, mixer, up1, down1):
        terms.update(pipe.terms())

    def fold(v):
        return v[0, :HEAD_DIM] + v[0, HEAD_DIM:]

    small_grads = {"ffn1_norm": d_ffn1_norm, "mix_norm": d_mix_norm, "ffn2_norm": d_ffn2_norm,
                   "ple_norm": d_ple_norm, "a_q_norm": fold(dgqa), "a_k_norm": fold(dgka),
                   "b_q_norm": fold(dgqb), "b_k_norm": fold(dgkb), "a_rel_bias": _rel_bias_grad(_unpair_bias(dbias_a)),
                   "b_sinks": jnp.sum(dsink_b, axis=1)}
    return loss, dx, terms, small_grads


def kernel(x, p, ffn1_norm, ffn1_w_gu, ffn1_w_down, mix_norm, w_in, a_q_norm, a_k_norm, a_rel_bias, b_q_norm, b_k_norm, b_sinks, w_gate, w_proj_a, w_proj_b, w_out, ffn2_norm, ffn2_w_gu, ffn2_w_down, ple_norm, w_ple_gate, w_ple_proj, loss_target, m_ffn1_norm, m_ffn1_w_gu, m_ffn1_w_down, m_mix_norm, m_w_in, m_a_q_norm, m_a_k_norm, m_a_rel_bias, m_b_q_norm, m_b_k_norm, m_b_sinks, m_w_gate, m_w_proj_a, m_w_proj_b, m_w_out, m_ffn2_norm, m_ffn2_w_gu, m_ffn2_w_down, m_ple_norm, m_w_ple_gate, m_w_ple_proj, v_ffn1_norm, v_ffn1_w_gu, v_ffn1_w_down, v_mix_norm, v_w_in, v_a_q_norm, v_a_k_norm, v_a_rel_bias, v_b_q_norm, v_b_k_norm, v_b_sinks, v_w_gate, v_w_proj_a, v_w_proj_b, v_w_out, v_ffn2_norm, v_ffn2_w_gu, v_ffn2_w_down, v_ple_norm, v_w_ple_gate, v_w_ple_proj):
    given = dict(locals())
    n_batch, s, d = x.shape
    t = n_batch * s
    xt = x.reshape(t, d)
    pt = p.reshape(t, p.shape[-1])
    tgt = loss_target.reshape(t, d)

    shards = {}
    for name in BIG:
        (shards[name],) = _ew("cast_" + name, lambda v: (v,), [given[name][0]], [BF16])
    small = {name: given[name] for name in SMALL}
    core = lax.axis_index("c").astype(jnp.int32).reshape(1)
    loss, dx, terms, small_grads = _local_step(xt, pt, tgt, n_batch, shards, small, core)

    grads, deltas, new_m, new_v = {}, {}, {}, {}
    for name in BIG:
        gw, dl, nm, nv = _adamw_terms("adamw_" + name, terms[name], given[name][0], given["m_" + name][0],
                                      given["v_" + name][0])
        grads[name], deltas[name], new_m[name], new_v[name] = gw[None], dl[None], nm[None], nv[None]

    small_shapes = [given[name].shape for name in SMALL] + [()]
    g_pack = _all_reduce_small(_pack([small_grads[name] for name in SMALL] + [loss]))
    zero = jnp.zeros((), F32)
    w_pack = _pack([given[name] for name in SMALL] + [zero])
    m_pack = _pack([given["m_" + name] for name in SMALL] + [zero])
    v_pack = _pack([given["v_" + name] for name in SMALL] + [zero])
    d_pack, nm_pack, nv_pack = _ew("adamw_small", lambda wv, gv, mv, vv: _adamw_math(wv, gv, mv, vv),
                                   [w_pack, g_pack, m_pack, v_pack], [F32] * 3)
    g_small = _unpack(g_pack, small_shapes)
    loss_total = g_small[-1]
    for name, gv, dv, mv, vv in zip(SMALL, g_small, _unpack(d_pack, small_shapes), _unpack(nm_pack, small_shapes),
                                    _unpack(nv_pack, small_shapes)):
        grads[name], deltas[name], new_m[name], new_v[name] = gv, dv, mv, vv

    return (loss_total, dx.reshape(x.shape), *[grads[n] for n in WEIGHTS], *[deltas[n] for n in WEIGHTS],
            *[new_m[n] for n in WEIGHTS], *[new_v[n] for n in WEIGHTS])
```

```python
import functools

import numpy as np
import jax
import jax.numpy as jnp
from jax import lax
from jax.experimental import pallas as pl
from jax.experimental.pallas import tpu as pltpu

F32 = jnp.float32
BF16 = jnp.bfloat16

CHUNK = 64
HEAD_DIM = 64
A_PREV_CHUNKS = 8
A_MAX_REL = 128
N_HEADS = 8
B_KV_HEADS = 2
B_PREV_CHUNKS = 2
A_WIDTH = N_HEADS * HEAD_DIM
B_KV_WIDTH = B_KV_HEADS * HEAD_DIM
EPS = 1e-6
NEG_INF = -1e30
ATTN_SCALE = HEAD_DIM ** -0.5
Q_BLOCK = 128
PAIR = 2 * HEAD_DIM

ADAM_LR = 0.001
ADAM_B1 = 0.9
ADAM_B2 = 0.999
ADAM_EPS = 1e-08
ADAM_WD = 0.01
ADAM_STEP = 10

N_CHIPS = 4
N_DEV = 8
VMEM_LIMIT_V7X = 56 * 1024 * 1024
MESH = pl.DeviceIdType.MESH
ANY = pl.BlockSpec(memory_space=pl.ANY)

_DN = {
    "nn": (((1,), (0,)), ((), ())),
    "nt": (((1,), (1,)), ((), ())),
    "tn": (((0,), (0,)), ((), ())),
}


def _pick(n, target, mult=128):
    best = None
    for d in range(mult, min(n, target) + 1, mult):
        if n % d == 0:
            best = d
    return n if best is None else best


def _dot(a, b, mode):
    return lax.dot_general(a.astype(BF16), b.astype(BF16), _DN[mode], preferred_element_type=F32)


def _params():
    return pltpu.CompilerParams(vmem_limit_bytes=VMEM_LIMIT_V7X)


class _Comm:
    def __init__(self, ins, outs, aliases, sems, start, finish):
        self.ins, self.outs, self.aliases, self.sems = list(ins), list(outs), dict(aliases), list(sems)
        self.start, self.finish = start, finish
        self.results = None


class _CommPlumbing:
    def __init__(self, comms, n_in, n_out, n_scratch):
        self.comms = list(comms)
        self.n_in, self.n_out, self.n_scratch = n_in, n_out, n_scratch
        self.args = [a for cm in self.comms for a in cm.ins]
        self.out_shape = [o for cm in self.comms for o in cm.outs]
        self.scratch = [s for cm in self.comms for s in cm.sems]
        self.aliases = {}
        i0, o0 = n_in, n_out
        for cm in self.comms:
            for a, b in cm.aliases.items():
                self.aliases[i0 + a] = o0 + b
            i0 += len(cm.ins)
            o0 += len(cm.outs)

    def run(self, in_refs, out_refs, scratch_refs, first, last):
        if not self.comms:
            return
        parts = []
        i0, o0, s0 = self.n_in, self.n_out, self.n_scratch
        for cm in self.comms:
            parts.append((in_refs[i0:i0 + len(cm.ins)], out_refs[o0:o0 + len(cm.outs)],
                          scratch_refs[s0:s0 + len(cm.sems)]))
            i0 += len(cm.ins)
            o0 += len(cm.outs)
            s0 += len(cm.sems)

        @pl.when(first)
        def _():
            for cm, part in zip(self.comms, parts):
                cm.start(*part)

        @pl.when(last)
        def _():
            for cm, part in zip(self.comms, parts):
                cm.finish(*part)

    def deliver(self, results):
        o0 = self.n_out
        for cm in self.comms:
            cm.results = list(results[o0:o0 + len(cm.outs)])
            o0 += len(cm.outs)
        return list(results[:self.n_out])


def _swap_ij(spec):
    index_map = spec.index_map
    return pl.BlockSpec(spec.block_shape, lambda j, i, k: index_map(i, j, k))


MXU_COLUMNS_V7X = 256


def _mm(name, mode, grid, pairs, extras, outs, acc_shape, epilogue, steps=None, comms=(), j_outer=False,
        chunked=False):
    ni, nj, nk = grid
    slots = [pair[4] if len(pair) > 4 else None for pair in pairs]
    pairs = [pair[:4] for pair in pairs]
    n_in = 2 * len(pairs) + len(extras)
    n_out = len(outs)
    tn = acc_shape[1]
    col_chunks = None
    if chunked:
        assert nk == 1 and steps is None and mode in ("nn", "nt")
        col_chunks = [(c0, min(MXU_COLUMNS_V7X, tn - c0)) for c0 in range(0, tn, MXU_COLUMNS_V7X)]
    n_acc = 0 if chunked else (len(pairs) if steps is None else 1)
    plumb = _CommPlumbing(comms, n_in, n_out, n_acc)
    n_all_in = n_in + len(plumb.args)
    n_all_out = n_out + len(plumb.out_shape)
    if j_outer:
        grid = (nj, ni, nk)
        pairs = [(a, _swap_ij(a_spec), b, _swap_ij(b_spec)) for a, a_spec, b, b_spec in pairs]
        extras = [(e, _swap_ij(e_spec)) for e, e_spec in extras]
        outs = [(o, _swap_ij(o_spec)) for o, o_spec in outs]

    def body(*refs):
        in_refs = refs[:n_all_in]
        out_refs = refs[n_all_in:n_all_in + n_all_out]
        scratch = refs[n_all_in + n_all_out:]
        accs = scratch[:n_acc]
        i = pl.program_id(1 if j_outer else 0)
        j = pl.program_id(0 if j_outer else 1)
        k = pl.program_id(2)

        def contrib(p, acc):
            b_ref = in_refs[2 * p + 1]
            rhs = b_ref[...] if slots[p] is None else b_ref[slots[p](i, j, k)]
            acc[...] += _dot(in_refs[2 * p][...], rhs, mode)

        if col_chunks:
            def cols(ref, c0, cs):
                if ref.shape[-1] != tn:
                    return ref
                return ref.at[(slice(None),) * (len(ref.shape) - 1) + (pl.ds(c0, cs),)]

            lhs = [in_refs[2 * p][...] for p in range(len(pairs))]
            for ci, (c0, cs) in enumerate(col_chunks):
                vals = []
                for p in range(len(pairs)):
                    b_ref = in_refs[2 * p + 1]
                    rhs = b_ref[:, c0:c0 + cs] if mode == "nn" else b_ref[c0:c0 + cs, :]
                    vals.append(_dot(lhs[p], rhs, mode))
                epilogue(vals, [cols(r, c0, cs) for r in in_refs[2 * len(pairs):n_in]],
                         [cols(r, c0, cs) for r in out_refs[:n_out]], (i, j * len(col_chunks) + ci))
        else:
            @pl.when(k == 0)
            def _():
                for acc in accs:
                    acc[...] = jnp.zeros(acc.shape, F32)

            if steps is None:
                for p in range(len(pairs)):
                    contrib(p, accs[p])
            else:
                lo = 0
                for p, n in enumerate(steps):
                    pl.when((k >= lo) & (k < lo + n))(functools.partial(contrib, p, accs[0]))
                    lo += n

            @pl.when(k == nk - 1)
            def _():
                epilogue([acc[...] for acc in accs], in_refs[2 * len(pairs):n_in], out_refs[:n_out], (i, j))

        plumb.run(in_refs, out_refs, scratch, (i == 0) & (j == 0) & (k == 0),
                  (i == ni - 1) & (j == nj - 1) & (k == nk - 1))

    args, in_specs = [], []
    for a, a_spec, b, b_spec in pairs:
        args += [a, b]
        in_specs += [a_spec, b_spec]
    for e, e_spec in extras:
        args.append(e)
        in_specs.append(e_spec)
    res = pl.pallas_call(
        body,
        name=name,
        grid=grid,
        in_specs=in_specs + [ANY] * len(plumb.args),
        out_specs=[s for _, s in outs] + [ANY] * len(plumb.out_shape),
        out_shape=[o for o, _ in outs] + plumb.out_shape,
        scratch_shapes=[pltpu.VMEM(acc_shape, F32) for _ in range(n_acc)] + plumb.scratch,
        input_output_aliases=plumb.aliases,
        compiler_params=_params(),
    )(*args, *plumb.args)
    return plumb.deliver(res)


def _sds(shape, dtype):
    return jax.ShapeDtypeStruct(shape, dtype)


def _accumulate(ref, value, first):
    @pl.when(first)
    def _():
        ref[...] = value

    @pl.when(jnp.logical_not(first))
    def _():
        ref[...] += value


def _rms_fwd(name, x, gain):
    t, d = x.shape
    tm = _pick(t, 512, 8)

    def body(x_ref, g_ref, y_ref):
        xv = x_ref[...]
        rstd = lax.rsqrt(jnp.mean(xv * xv, axis=-1, keepdims=True) + EPS)
        y_ref[...] = (xv * rstd * g_ref[...]).astype(BF16)

    return pl.pallas_call(
        body, name=name, grid=(t // tm,),
        in_specs=[pl.BlockSpec((tm, d), lambda i: (i, 0)), pl.BlockSpec((1, d), lambda i: (0, 0))],
        out_specs=pl.BlockSpec((tm, d), lambda i: (i, 0)),
        out_shape=_sds((t, d), BF16),
        compiler_params=_params(),
    )(x, gain)


def _rms_bwd_epilogue(accs, extras, outs, ij):
    x_ref, g_ref, r_ref = extras
    dh_ref, dhb_ref, dg_ref = outs
    dn = accs[0]
    xv = x_ref[...]
    rstd = lax.rsqrt(jnp.mean(xv * xv, axis=-1, keepdims=True) + EPS)
    xhat = xv * rstd
    gd = dn * g_ref[...]
    dx = rstd * (gd - xhat * jnp.mean(gd * xhat, axis=-1, keepdims=True))
    dh = r_ref[...] + dx
    dh_ref[...] = dh
    dhb_ref[...] = dh.astype(BF16)
    _accumulate(dg_ref, jnp.sum(dn * xhat, axis=0, keepdims=True), ij[0] == 0)


def _rms_bwd_io(x, gain, dres, tm):
    t, d = x.shape
    row = pl.BlockSpec((tm, d), lambda i, j, k: (i, 0))
    extras = [(x, row), (gain, pl.BlockSpec((1, d), lambda i, j, k: (0, 0))), (dres, row)]
    outs = [(_sds((t, d), F32), row), (_sds((t, d), BF16), row),
            (_sds((1, d), F32), pl.BlockSpec((1, d), lambda i, j, k: (0, 0)))]
    return extras, outs


def _ffn_fwd(tag, h, gain, wgu, wd, hooks):
    t, d = h.shape
    fs = wgu.shape[2]
    f = 2 * fs
    tm = _pick(t, 512, 8)
    n = _rms_fwd(tag + "_norm", h, gain)

    def up_epilogue(accs, extras, outs, ij):
        g, u = accs
        gu_ref, a_ref = outs
        gu_ref[0] = g.astype(BF16)
        gu_ref[1] = u.astype(BF16)
        a_ref[...] = (g * jax.nn.sigmoid(g) * u).astype(BF16)

    a_spec = pl.BlockSpec((tm, d), lambda i, j, k: (i, 0))
    gu, a = _mm(
        tag + "_up", "nn", (t // tm, 2, 1),
        [(n, a_spec, wgu, pl.BlockSpec((None, d, fs), lambda i, j, k: (j, 0, 0))),
         (n, a_spec, wgu, pl.BlockSpec((None, d, fs), lambda i, j, k: (j + 2, 0, 0)))],
        [],
        [(_sds((2, t, f), BF16), pl.BlockSpec((2, tm, fs), lambda i, j, k: (0, i, j))),
         (_sds((t, f), BF16), pl.BlockSpec((tm, fs), lambda i, j, k: (i, j)))],
        (tm, fs), up_epilogue, comms=hooks.get("up", lambda: ())(), j_outer=True, chunked=True)

    def down_epilogue(accs, extras, outs, ij):
        outs[0][...] = extras[0][...] + 0.5 * accs[0]

    if callable(wd):
        wd = wd()

    row = pl.BlockSpec((tm, d), lambda i, j, k: (i, 0))
    (h_new,) = _mm(
        tag + "_down", "nn", (t // tm, 1, 1),
        [(a, pl.BlockSpec((tm, f), lambda i, j, k: (i, 0)), wd, pl.BlockSpec((f, d), lambda i, j, k: (0, 0)))],
        [(h, row)], [(_sds((t, d), F32), row)], (tm, d), down_epilogue, comms=hooks.get("down", lambda: ())())
    return h_new, (n, gu, a)


def _ffn_bwd(tag, dh, dh_b, h, gain, wgu, wd, saved, hooks):
    n, gu, a = saved
    t, d = h.shape
    fs = wgu.shape[2]
    f = 2 * fs
    tm = _pick(t, 512, 8)
    tk = _pick(t, 512, 8)

    def dact_epilogue(accs, extras, outs, ij):
        da = 0.5 * accs[0]
        g = extras[0][0].astype(F32)
        u = extras[0][1].astype(F32)
        sg = jax.nn.sigmoid(g)
        outs[0][0] = (da * u * sg * (1.0 + g * (1.0 - sg))).astype(BF16)
        outs[0][1] = (da * g * sg).astype(BF16)

    gu_spec = pl.BlockSpec((2, tm, fs), lambda i, j, k: (0, i, j))
    (dgu,) = _mm(
        tag + "_dact", "nt", (t // tm, 2, 1),
        [(dh_b, pl.BlockSpec((tm, d), lambda i, j, k: (i, 0)), wd, pl.BlockSpec((fs, d), lambda i, j, k: (j, 0)))],
        [(gu, gu_spec)], [(_sds((2, t, f), BF16), gu_spec)], (tm, fs), dact_epilogue, j_outer=True, chunked=True,
        comms=hooks.get("dact", lambda: ())())

    def cast_epilogue(accs, extras, outs, ij):
        outs[0][...] = accs[0].astype(BF16)

    (dwgu,) = _mm(
        tag + "_dwgu", "tn", (1, 4, t // tk),
        [(n, pl.BlockSpec((tk, d), lambda i, j, k: (k, 0)),
          dgu, pl.BlockSpec((None, tk, fs), lambda i, j, k: (j // 2, k, j % 2)))],
        [], [(_sds((4, d, fs), BF16), pl.BlockSpec((None, d, fs), lambda i, j, k: (j, 0, 0)))], (d, fs), cast_epilogue,
        comms=hooks.get("dwgu", lambda: ())())

    def half_epilogue(accs, extras, outs, ij):
        outs[0][...] = (0.5 * accs[0]).astype(BF16)

    (dwd,) = _mm(
        tag + "_dwd", "tn", (2, 1, t // tk),
        [(a, pl.BlockSpec((tk, fs), lambda i, j, k: (k, i)), dh_b, pl.BlockSpec((tk, d), lambda i, j, k: (k, 0)))],
        [], [(_sds((f, d), BF16), pl.BlockSpec((fs, d), lambda i, j, k: (i, 0)))], (fs, d), half_epilogue,
        comms=hooks.get("dwd", lambda g: ())(dwgu))

    extras, outs = _rms_bwd_io(h, gain, dh, tm)
    dh_in, dh_in_b, dgain = _mm(
        tag + "_dnorm", "nt", (t // tm, 1, 4),
        [(dgu, pl.BlockSpec((None, tm, fs), lambda i, j, k: (k // 2, i, k % 2)),
          wgu, pl.BlockSpec((4, d, fs), lambda i, j, k: (0, 0, 0)), lambda i, j, k: k)],
        extras, outs, (tm, d), _rms_bwd_epilogue, comms=hooks.get("dnorm", lambda g, w: ())(dwgu, dwd))
    return dh_in, dh_in_b, dgain, dwgu, dwd


def _lane_lo(shape):
    return lax.broadcasted_iota(jnp.int32, shape, 1) < HEAD_DIM


def _pair_norm(xv, gain):
    lo = _lane_lo(xv.shape)
    x2 = xv * xv
    ms_lo = jnp.sum(jnp.where(lo, x2, 0.0), axis=-1, keepdims=True) * (1.0 / HEAD_DIM)
    ms_hi = jnp.sum(jnp.where(lo, 0.0, x2), axis=-1, keepdims=True) * (1.0 / HEAD_DIM)
    rstd = jnp.where(lo, lax.rsqrt(ms_lo + EPS), lax.rsqrt(ms_hi + EPS))
    xhat = xv * rstd
    return xhat * gain, xhat, rstd


def _pair_norm_bwd(dn, xhat, rstd, gain):
    lo = _lane_lo(dn.shape)
    gd = dn * gain
    t = gd * xhat
    m_lo = jnp.sum(jnp.where(lo, t, 0.0), axis=-1, keepdims=True) * (1.0 / HEAD_DIM)
    m_hi = jnp.sum(jnp.where(lo, 0.0, t), axis=-1, keepdims=True) * (1.0 / HEAD_DIM)
    dx = rstd * (gd - xhat * jnp.where(lo, m_lo, m_hi))
    return dx, jnp.sum(dn * xhat, axis=0, keepdims=True)


def _half(xv, hi):
    lo = _lane_lo(xv.shape)
    return jnp.where(lo, 0, xv) if hi else jnp.where(lo, xv, 0)


def _attn_window(i, prev):
    q0 = i * Q_BLOCK
    start = jnp.maximum(q0 - prev, 0)
    off = start - (q0 - prev)
    return pl.multiple_of(start, Q_BLOCK), pl.multiple_of(off, Q_BLOCK)


def _attn_specs(cfg, s, nq):
    kw = cfg["kw"]
    q_spec = pl.BlockSpec((Q_BLOCK, A_WIDTH), lambda b, i: (b * nq + i, cfg["qblk"]))
    k_spec = pl.BlockSpec((s, kw), lambda b, i: (b, cfg["kblk"]))
    v_spec = pl.BlockSpec((s, kw), lambda b, i: (b, cfg["vblk"]))
    return q_spec, k_spec, v_spec


def _const_spec(shape):
    return pl.BlockSpec(shape, lambda b, i: (0,) * len(shape))


KEY_CHUNK = 128


def _pair_bias(bias_t):
    wext = bias_t.shape[1]
    return jnp.transpose(bias_t.reshape(N_HEADS // 2, 2, wext, Q_BLOCK), (0, 2, 1, 3)).reshape(
        N_HEADS // 2, wext, 2 * Q_BLOCK)


def _unpair_bias(db2):
    wext = db2.shape[1]
    return jnp.transpose(db2.reshape(N_HEADS // 2, wext, 2, Q_BLOCK), (0, 2, 1, 3)).reshape(N_HEADS, wext, Q_BLOCK)


def _pair_rows(rows):
    two = rows.reshape(N_HEADS // 2, 2 * rows.shape[1])
    return jnp.broadcast_to(two[:, None, :], (N_HEADS // 2, 8, two.shape[1]))


def _sub_lo(shape):
    return lax.broadcasted_iota(jnp.int32, shape, 0) < HEAD_DIM


def _by_half(lo_row, hi_row, rows):
    return jnp.where(_sub_lo((rows, lo_row.shape[1])), lo_row, hi_row)


def _stack_pair(xn, jq, group):
    parts = []
    for hq in range(2):
        hk = ((2 * jq + hq) // group) % 2
        xm = _half(xn, hq)
        if hq != hk:
            xm = pltpu.roll(xm, HEAD_DIM, 1)
        parts.append(xm)
    return jnp.concatenate(parts, axis=0).astype(BF16)


def _place_transposed(blk, dst_ref, c, heads, group):
    bt = blk.T
    lo = _sub_lo(bt.shape)
    for h in heads:
        src_hi = ((h // group) % 2) == 1
        part = jnp.where(lo, 0.0, bt) if src_hi else jnp.where(lo, bt, 0.0)
        if src_hi != (h % 2 == 1):
            part = pltpu.roll(part, HEAD_DIM, 0)
        dst_ref[h, c] = part.astype(BF16)


def _attn_fwd(name, qkv, bias2, sink2, gq, gk, cfg, n_batch, comms=()):
    t = qkv.shape[0]
    s = t // n_batch
    nq = s // Q_BLOCK
    nkc = s // KEY_CHUNK
    prev, group, kw = cfg["prev"], cfg["group"], cfg["kw"]
    n_chunks = (prev + Q_BLOCK) // KEY_CHUNK
    wext = bias2.shape[1]
    plumb = _CommPlumbing(comms, 7, 2, 2)
    n_all_in = 7 + len(plumb.args)
    n_all_out = 2 + len(plumb.out_shape)

    def body(*refs):
        q_ref, k_ref, v_ref, bias_ref, sink_ref, gq_ref, gk_ref = refs[:7]
        y_ref, lse_ref = refs[n_all_in:n_all_in + 2]
        kn_ref, vt_ref = refs[n_all_in + n_all_out:n_all_in + n_all_out + 2]
        i = pl.program_id(1)
        plumb.run(refs[:n_all_in], refs[n_all_in:n_all_in + n_all_out], refs[n_all_in + n_all_out:],
                  (pl.program_id(0) == 0) & (i == 0), (pl.program_id(0) == n_batch - 1) & (i == nq - 1))

        @pl.when(i == 0)
        def _():
            for jk in range(kw // PAIR):
                cols = pl.ds(jk * PAIR, PAIR)
                heads = [h for h in range(N_HEADS) if (h // group) // 2 == jk]
                kn, _, _ = _pair_norm(k_ref[:, cols].astype(F32), gk_ref[...])
                kn_ref[:, cols] = kn.astype(BF16)
                for c in range(nkc):
                    _place_transposed(v_ref[pl.ds(c * KEY_CHUNK, KEY_CHUNK), cols].astype(F32), vt_ref, c, heads, group)

        start, off = _attn_window(i, prev)
        c0 = start // KEY_CHUNK
        sub8 = lax.broadcasted_iota(jnp.int32, (N_HEADS, Q_BLOCK), 0)
        lse = jnp.zeros((N_HEADS, Q_BLOCK), F32)
        for jq in range(N_HEADS // 2):
            kcols = pl.ds((((2 * jq) // group) // 2) * PAIR, PAIR)
            qn, _, _ = _pair_norm(q_ref[:, pl.ds(jq * PAIR, PAIR)].astype(F32), gq_ref[...])
            qs = _stack_pair(qn * ATTN_SCALE, jq, group)
            m = sink_ref[jq, 0:1, :]
            l = jnp.ones((1, 2 * Q_BLOCK), F32)
            ot = jnp.zeros((PAIR, Q_BLOCK), F32)
            for c in range(n_chunks):
                rows = pl.ds(start + c * KEY_CHUNK, KEY_CHUNK)
                s2 = _dot(kn_ref[rows, kcols], qs, "nt") + bias_ref[jq, pl.ds(off + c * KEY_CHUNK, KEY_CHUNK), :]
                m_new = jnp.maximum(m, jnp.max(s2, axis=0, keepdims=True))
                alpha = jnp.exp(m - m_new)
                p = jnp.exp(s2 - m_new)
                l = alpha * l + jnp.sum(p, axis=0, keepdims=True)
                m = m_new
                pst = jnp.concatenate([p[:, :Q_BLOCK], p[:, Q_BLOCK:]], axis=0)
                vl = jnp.concatenate([vt_ref[2 * jq, c0 + c], vt_ref[2 * jq + 1, c0 + c]], axis=1)
                ot = ot * _by_half(alpha[:, :Q_BLOCK], alpha[:, Q_BLOCK:], PAIR) + _dot(vl, pst, "nn")
            inv = 1.0 / l
            ot = ot * _by_half(inv[:, :Q_BLOCK], inv[:, Q_BLOCK:], PAIR)
            y_ref[:, pl.ds(jq * PAIR, PAIR)] = ot.T.astype(BF16)
            lse2 = m + jnp.log(l)
            lse = jnp.where(sub8 == 2 * jq, lse2[:, :Q_BLOCK], lse)
            lse = jnp.where(sub8 == 2 * jq + 1, lse2[:, Q_BLOCK:], lse)
        lse_ref[...] = lse

    q_spec, k_spec, v_spec = _attn_specs(cfg, s, nq)
    res = pl.pallas_call(
        body, name=name, grid=(n_batch, nq),
        in_specs=[q_spec, k_spec, v_spec, _const_spec((N_HEADS // 2, wext, 2 * Q_BLOCK)),
                  _const_spec((N_HEADS // 2, 8, 2 * Q_BLOCK)), _const_spec((1, PAIR)), _const_spec((1, PAIR))]
        + [ANY] * len(plumb.args),
        out_specs=[pl.BlockSpec((Q_BLOCK, A_WIDTH), lambda b, i: (b * nq + i, 0)),
                   pl.BlockSpec((None, N_HEADS, Q_BLOCK), lambda b, i: (b * nq + i, 0, 0))]
        + [ANY] * len(plumb.out_shape),
        out_shape=[_sds((t, A_WIDTH), BF16), _sds((t // Q_BLOCK, N_HEADS, Q_BLOCK), F32)] + plumb.out_shape,
        scratch_shapes=[pltpu.VMEM((s, kw), BF16), pltpu.VMEM((N_HEADS, nkc, PAIR, KEY_CHUNK), BF16)] + plumb.scratch,
        input_output_aliases=plumb.aliases,
        compiler_params=_params(),
    )(qkv, qkv, qkv, bias2, sink2, gq, gk, *plumb.args)
    return plumb.deliver(res)


def _attn_bwd(name, qkv, bias2, sink2, gq, gk, y, dy, lse, cfg, n_batch, want_dbias, comms=()):
    t = qkv.shape[0]
    s = t // n_batch
    nq = s // Q_BLOCK
    nkc = s // KEY_CHUNK
    prev, group, kw = cfg["prev"], cfg["group"], cfg["kw"]
    w = prev + Q_BLOCK
    n_chunks = w // KEY_CHUNK
    wext = bias2.shape[1]
    plumb = _CommPlumbing(comms, 10, 7, 9)
    n_all_in = 10 + len(plumb.args)
    n_all_out = 7 + len(plumb.out_shape)

    def body(*refs):
        q_ref, k_ref, v_ref, bias_ref, sink_ref, gq_ref, gk_ref, y_ref, dy_ref, lse_ref = refs[:10]
        dq_ref, dk_ref, dv_ref, db_ref, dsink_ref, dgq_ref, dgk_ref = refs[n_all_in:n_all_in + 7]
        kn_ref, knt_ref, dkn_ref, dvs_ref, s_ref, dp_ref, pb_ref, dsb_ref, dst_ref = \
            refs[n_all_in + n_all_out:n_all_in + n_all_out + 9]
        b = pl.program_id(0)
        i = pl.program_id(1)
        first = (b == 0) & (i == 0)
        plumb.run(refs[:n_all_in], refs[n_all_in:n_all_in + n_all_out], refs[n_all_in + n_all_out:],
                  first, (b == n_batch - 1) & (i == nq - 1))

        @pl.when(i == 0)
        def _():
            for jk in range(kw // PAIR):
                cols = pl.ds(jk * PAIR, PAIR)
                heads = [h for h in range(N_HEADS) if (h // group) // 2 == jk]
                for c in range(nkc):
                    rows = pl.ds(c * KEY_CHUNK, KEY_CHUNK)
                    kn, _, _ = _pair_norm(k_ref[rows, cols].astype(F32), gk_ref[...])
                    kn_ref[rows, cols] = kn.astype(BF16)
                    _place_transposed(kn, knt_ref, c, heads, group)
            dkn_ref[...] = jnp.zeros(dkn_ref.shape, F32)
            dvs_ref[...] = jnp.zeros(dvs_ref.shape, F32)

        @pl.when(first)
        def _():
            db_ref[...] = jnp.zeros(db_ref.shape, F32)
            dsink_ref[...] = jnp.zeros(dsink_ref.shape, F32)
            dgq_ref[...] = jnp.zeros(dgq_ref.shape, F32)
            dgk_ref[...] = jnp.zeros(dgk_ref.shape, F32)

        start, off = _attn_window(i, prev)
        c0 = start // KEY_CHUNK
        for jq in range(N_HEADS // 2):
            cols = pl.ds(jq * PAIR, PAIR)
            kcols = pl.ds((((2 * jq) // group) // 2) * PAIR, PAIR)
            qn, q_hat, q_rstd = _pair_norm(q_ref[:, cols].astype(F32), gq_ref[...])
            qs = _stack_pair(qn * ATTN_SCALE, jq, group)
            do_pair = dy_ref[:, cols].astype(F32)
            dos = _stack_pair(do_pair, jq, group)
            prod_t = (do_pair * y_ref[:, cols].astype(F32)).T
            lo = _sub_lo(prod_t.shape)
            delta2 = jnp.concatenate([jnp.sum(jnp.where(lo, prod_t, 0.0), axis=0, keepdims=True),
                                      jnp.sum(jnp.where(lo, 0.0, prod_t), axis=0, keepdims=True)], axis=1)
            lse2 = jnp.concatenate([lse_ref[2 * jq:2 * jq + 1, :], lse_ref[2 * jq + 1:2 * jq + 2, :]], axis=1)
            dsk = -jnp.exp(sink_ref[jq, 0:1, :] - lse2) * delta2
            dsink_ref[2 * jq:2 * jq + 1, :] += dsk[:, :Q_BLOCK]
            dsink_ref[2 * jq + 1:2 * jq + 2, :] += dsk[:, Q_BLOCK:]
            rows_w = pl.ds(start, w)
            s_ref[...] = _dot(kn_ref[rows_w, kcols], qs, "nt")
            dp_ref[...] = _dot(v_ref[rows_w, kcols], dos, "nt")
            for c in range(n_chunks):
                r = pl.ds(c * KEY_CHUNK, KEY_CHUNK)
                brows = pl.ds(off + c * KEY_CHUNK, KEY_CHUNK)
                p = jnp.exp(s_ref[r, :] + bias_ref[jq, brows, :] - lse2)
                ds = p * (dp_ref[r, :] - delta2)
                if want_dbias:
                    db_ref[jq, brows, :] += ds
                ds_b = ds.astype(BF16)
                pb_ref[r, :] = p.astype(BF16)
                dsb_ref[r, :] = ds_b
                dst_ref[pl.ds(2 * c * KEY_CHUNK, KEY_CHUNK), :] = ds_b[:, :Q_BLOCK]
                dst_ref[pl.ds((2 * c + 1) * KEY_CHUNK, KEY_CHUNK), :] = ds_b[:, Q_BLOCK:]
            dkn_ref[rows_w, kcols] += _dot(dsb_ref[...], qs, "nn")
            dvs_ref[rows_w, kcols] += _dot(pb_ref[...], dos, "nn")
            kl = jnp.concatenate([knt_ref[2 * jq + hq, c0 + c] for c in range(n_chunks) for hq in range(2)], axis=1)
            dqt = _dot(kl, dst_ref[...], "nn")
            dq_raw, dg = _pair_norm_bwd(dqt.T * ATTN_SCALE, q_hat, q_rstd, gq_ref[...])
            dq_ref[:, cols] = dq_raw.astype(BF16)
            dgq_ref[...] += dg

        @pl.when(i == nq - 1)
        def _():
            for jk in range(kw // PAIR):
                kcols = pl.ds(jk * PAIR, PAIR)
                _, k_hat, k_rstd = _pair_norm(k_ref[:, kcols].astype(F32), gk_ref[...])
                dk_raw, dg = _pair_norm_bwd(dkn_ref[:, kcols], k_hat, k_rstd, gk_ref[...])
                dk_ref[:, kcols] = dk_raw.astype(BF16)
                dgk_ref[...] += dg
            dv_ref[...] = dvs_ref[...].astype(BF16)

    q_spec, k_spec, v_spec = _attn_specs(cfg, s, nq)
    row = pl.BlockSpec((Q_BLOCK, A_WIDTH), lambda b, i: (b * nq + i, 0))
    kv_out = pl.BlockSpec((s, kw), lambda b, i: (b, 0))
    pair_bias = _const_spec((N_HEADS // 2, wext, 2 * Q_BLOCK))
    res = pl.pallas_call(
        body, name=name, grid=(n_batch, nq),
        in_specs=[q_spec, k_spec, v_spec, pair_bias, _const_spec((N_HEADS // 2, 8, 2 * Q_BLOCK)),
                  _const_spec((1, PAIR)), _const_spec((1, PAIR)), row, row,
                  pl.BlockSpec((None, N_HEADS, Q_BLOCK), lambda b, i: (b * nq + i, 0, 0))] + [ANY] * len(plumb.args),
        out_specs=[row, kv_out, kv_out, pair_bias, _const_spec((N_HEADS, 128)),
                   _const_spec((1, PAIR)), _const_spec((1, PAIR))] + [ANY] * len(plumb.out_shape),
        out_shape=[_sds((t, A_WIDTH), BF16), _sds((t, kw), BF16), _sds((t, kw), BF16),
                   _sds((N_HEADS // 2, wext, 2 * Q_BLOCK), F32), _sds((N_HEADS, 128), F32),
                   _sds((1, PAIR), F32), _sds((1, PAIR), F32)] + plumb.out_shape,
        scratch_shapes=[pltpu.VMEM((s, kw), BF16), pltpu.VMEM((N_HEADS, nkc, PAIR, KEY_CHUNK), BF16),
                        pltpu.VMEM((s, kw), F32), pltpu.VMEM((s, kw), F32),
                        pltpu.VMEM((w, 2 * Q_BLOCK), F32), pltpu.VMEM((w, 2 * Q_BLOCK), F32),
                        pltpu.VMEM((w, 2 * Q_BLOCK), BF16), pltpu.VMEM((w, 2 * Q_BLOCK), BF16),
                        pltpu.VMEM((2 * w, Q_BLOCK), BF16)] + plumb.scratch,
        input_output_aliases=plumb.aliases,
        compiler_params=_params(),
    )(qkv, qkv, qkv, bias2, sink2, gq, gk, y, dy, lse, *plumb.args)
    return plumb.deliver(res)


def _band_tables(prev_chunks):
    prev = prev_chunks * CHUNK
    wext = 2 * prev + Q_BLOCK
    jj = np.arange(wext)[:, None]
    ii = np.arange(Q_BLOCK)[None, :]
    dist = prev + ii - jj
    rel_chunk = (prev // CHUNK + ii // CHUNK) - jj // CHUNK
    allowed = (rel_chunk >= 0) & (rel_chunk <= prev_chunks)
    return dist, allowed


def _alibi_slopes():
    return np.array([2.0 ** (-8.0 * (h + 1) / N_HEADS) for h in range(N_HEADS)], dtype=np.float32)


def _diag_onehot(prev, wext):
    n_diag = wext + Q_BLOCK - 1
    idx = np.clip(prev + Q_BLOCK - 1 - np.arange(n_diag), -A_MAX_REL, A_MAX_REL) + A_MAX_REL
    onehot = np.zeros((n_diag, 2 * A_MAX_REL + 1), np.float32)
    onehot[np.arange(n_diag), idx] = 1.0
    return onehot


def _bias_a(rel_bias):
    prev = A_PREV_CHUNKS * CHUNK
    _, allowed = _band_tables(A_PREV_CHUNKS)
    wext = allowed.shape[0]
    n_diag = wext + Q_BLOCK - 1
    seq = jnp.dot(rel_bias, jnp.asarray(_diag_onehot(prev, wext).T), precision=lax.Precision.HIGHEST)
    seq = jnp.pad(seq, ((0, 0), (0, 1)))
    rows = jnp.broadcast_to(seq[:, None, :], (N_HEADS, Q_BLOCK, n_diag + 1)).reshape(N_HEADS, -1)
    skew = rows[:, :Q_BLOCK * n_diag].reshape(N_HEADS, Q_BLOCK, n_diag)
    tile = jnp.transpose(skew[:, :, Q_BLOCK - 1:Q_BLOCK - 1 + wext], (0, 2, 1))
    return jnp.where(jnp.asarray(allowed)[None], tile, NEG_INF)


def _bias_b():
    dist, allowed = _band_tables(B_PREV_CHUNKS)
    bias = -_alibi_slopes()[:, None, None] * np.abs(dist).astype(np.float32)[None]
    return jnp.asarray(np.where(allowed[None], bias, np.float32(NEG_INF)).astype(np.float32))


def _rel_bias_grad(db_t):
    prev = A_PREV_CHUNKS * CHUNK
    wext = db_t.shape[1]
    n_diag = wext + Q_BLOCK - 1
    wp = n_diag + Q_BLOCK - 1
    xp = jnp.pad(jnp.transpose(db_t, (0, 2, 1)), ((0, 0), (0, 0), (Q_BLOCK - 1, Q_BLOCK - 1)))
    flat = jnp.pad(xp.reshape(N_HEADS, Q_BLOCK * wp), ((0, 0), (0, Q_BLOCK)))
    skew = flat.reshape(N_HEADS, Q_BLOCK, wp + 1)[:, :, :n_diag]
    diag = jnp.sum(skew, axis=1)
    return jnp.dot(diag, jnp.asarray(_diag_onehot(prev, wext)), precision=lax.Precision.HIGHEST)


def _ew(name, fn, ins, out_dtypes):
    r, c = ins[0].shape
    rb = _pick(r, max(16, (1 << 19) // c), 16)
    spec = pl.BlockSpec((rb, c), lambda i: (i, 0))

    def body(*refs):
        vals = fn(*[ref[...] for ref in refs[:len(ins)]])
        for ref, val in zip(refs[len(ins):], vals):
            ref[...] = val.astype(ref.dtype)

    return pl.pallas_call(
        body, name=name, grid=(r // rb,), in_specs=[spec] * len(ins), out_specs=[spec] * len(out_dtypes),
        out_shape=[_sds((r, c), dt) for dt in out_dtypes], compiler_params=_params(),
    )(*ins)


def _adamw_math(w, g, m, v):
    m = ADAM_B1 * m + (1.0 - ADAM_B1) * g
    v = ADAM_B2 * v + (1.0 - ADAM_B2) * (g * g)
    m_hat = m / (1.0 - ADAM_B1 ** ADAM_STEP)
    v_hat = v / (1.0 - ADAM_B2 ** ADAM_STEP)
    delta = -ADAM_LR * (m_hat / (jnp.sqrt(v_hat) + ADAM_EPS) + ADAM_WD * w)
    return delta, m, v


def _adamw_terms(name, terms, w, m, v):
    r, c = w.shape
    hr = r // 2
    rb = _pick(hr, max(16, (1 << 19) // c), 16)
    nb = hr // rb

    def body(t_ref, w_ref, m_ref, v_ref, g_ref, d_ref, nm_ref, nv_ref):
        g = t_ref[0].astype(F32)
        for k in range(1, N_CHIPS):
            g = g + t_ref[k].astype(F32)
        delta, nm, nv = _adamw_math(w_ref[...], g, m_ref[...], v_ref[...])
        g_ref[...] = g
        d_ref[...] = delta
        nm_ref[...] = nm
        nv_ref[...] = nv

    spec = pl.BlockSpec((rb, c), lambda h, i: (h * nb + i, 0))
    return pl.pallas_call(
        body, name=name, grid=(2, nb),
        in_specs=[pl.BlockSpec((None, N_CHIPS, rb, c), lambda h, i: (h, 0, i, 0)), spec, spec, spec],
        out_specs=[spec] * 4, out_shape=[_sds((r, c), F32)] * 4, compiler_params=_params(),
    )(terms, w, m, v)


def _mesh_place():
    x, y, c = lax.axis_index("x"), lax.axis_index("y"), lax.axis_index("c")
    chips = [(x, 1 - y), (1 - x, y), (1 - x, 1 - y)]
    return x, y, c, chips


def _all_gather_weights(shards):
    n = len(shards)

    def body(*refs):
        ins, outs = refs[:n], refs[n:2 * n]
        local_sem, ici_send, ici_recv, d2d_send, d2d_recv = refs[2 * n:]
        x, y, c, chips = _mesh_place()
        me = 2 * x + y
        sibling = (x, y, 1 - c)
        local, sent = [], []
        for wi in range(n):
            loc = pltpu.make_async_copy(ins[wi], outs[wi].at[me], local_sem.at[wi])
            loc.start()
            local.append(loc)
            for k, (tx, ty) in enumerate(chips):
                for pi, rows in enumerate(_row_pieces(shards[wi].shape[1])):
                    sem = (wi * 3 + k) * GATHER_PIECES + pi
                    cp = pltpu.make_async_remote_copy(
                        src_ref=ins[wi].at[c, rows], dst_ref=outs[wi].at[me, c, rows],
                        send_sem=ici_send.at[sem], recv_sem=ici_recv.at[sem],
                        device_id=(tx, ty, c), device_id_type=MESH)
                    cp.start()
                    sent.append(cp)
        passed = []
        for wi in range(n):
            for k, (tx, ty) in enumerate(chips):
                for pi, rows in enumerate(_row_pieces(shards[wi].shape[1])):
                    sem = (wi * 3 + k) * GATHER_PIECES + pi
                    slab = outs[wi].at[2 * tx + ty, c, rows]
                    pltpu.make_async_remote_copy(
                        src_ref=slab, dst_ref=slab, send_sem=ici_send.at[sem], recv_sem=ici_recv.at[sem],
                        device_id=(tx, ty, c), device_id_type=MESH).wait_recv()
                    fw = pltpu.make_async_remote_copy(
                        src_ref=slab, dst_ref=slab, send_sem=d2d_send.at[sem], recv_sem=d2d_recv.at[sem],
                        device_id=sibling, device_id_type=MESH)
                    fw.start()
                    passed.append(fw)
        for wi in range(n):
            for k, (tx, ty) in enumerate(chips):
                for pi, rows in enumerate(_row_pieces(shards[wi].shape[1])):
                    sem = (wi * 3 + k) * GATHER_PIECES + pi
                    slab = outs[wi].at[2 * tx + ty, 1 - c, rows]
                    pltpu.make_async_remote_copy(
                        src_ref=slab, dst_ref=slab, send_sem=d2d_send.at[sem], recv_sem=d2d_recv.at[sem],
                        device_id=sibling, device_id_type=MESH).wait_recv()
        for loc in local:
            loc.wait()
        for cp in sent + passed:
            cp.wait_send()

    return pl.pallas_call(
        body, name="all_gather_weights",
        in_specs=[ANY] * n, out_specs=[ANY] * n,
        out_shape=[_sds((N_CHIPS,) + s.shape, s.dtype) for s in shards],
        scratch_shapes=[pltpu.SemaphoreType.DMA((n,))] + [pltpu.SemaphoreType.DMA((3 * n * GATHER_PIECES,))] * 4,
    )(*shards)


def _run_comms(name, comms):
    plumb = _CommPlumbing(comms, 0, 0, 0)
    n_in, n_out = len(plumb.args), len(plumb.out_shape)

    def body(*refs):
        parts = []
        i0, o0, s0 = 0, n_in, n_in + n_out
        for cm in plumb.comms:
            parts.append((refs[i0:i0 + len(cm.ins)], refs[o0:o0 + len(cm.outs)], refs[s0:s0 + len(cm.sems)]))
            i0 += len(cm.ins)
            o0 += len(cm.outs)
            s0 += len(cm.sems)
        for cm, part in zip(plumb.comms, parts):
            cm.start(*part)
        for cm, part in zip(plumb.comms, parts):
            cm.finish(*part)

    res = pl.pallas_call(
        body, name=name, in_specs=[ANY] * n_in, out_specs=[ANY] * n_out, out_shape=plumb.out_shape,
        scratch_shapes=plumb.scratch, input_output_aliases=plumb.aliases,
    )(*plumb.args)
    plumb.deliver(res)


GATHER_PIECES = 4
BF16_TILE_ROWS = 16


def _row_pieces(rows):
    n = GATHER_PIECES
    while rows % (n * BF16_TILE_ROWS):
        n //= 2
    return [pl.ds(i * (rows // n), rows // n) for i in range(n)]


def _gather_ici(shards):
    n = len(shards)

    def copies(ins, outs, sems):
        local_sem, send_sem, recv_sem = sems
        x, y, c, chips = _mesh_place()
        me = 2 * x + y
        local, sends, recvs = [], [], []
        for wi in range(n):
            local.append(pltpu.make_async_copy(ins[wi], outs[wi].at[me], local_sem.at[wi]))
            pieces = _row_pieces(shards[wi].shape[1])
            for k, (tx, ty) in enumerate(chips):
                for pi, rows in enumerate(pieces):
                    sem = (wi * 3 + k) * GATHER_PIECES + pi
                    sems_k = dict(send_sem=send_sem.at[sem], recv_sem=recv_sem.at[sem],
                                  device_id=(tx, ty, c), device_id_type=MESH)
                    sends.append(pltpu.make_async_remote_copy(
                        src_ref=ins[wi].at[c, rows], dst_ref=outs[wi].at[me, c, rows], **sems_k))
                    slab = outs[wi].at[2 * tx + ty, c, rows]
                    recvs.append(pltpu.make_async_remote_copy(src_ref=slab, dst_ref=slab, **sems_k))
        return local, sends, recvs

    def start(ins, outs, sems):
        local, sends, _ = copies(ins, outs, sems)
        for cp in local + sends:
            cp.start()

    def finish(ins, outs, sems):
        local, sends, recvs = copies(ins, outs, sems)
        for cp in local:
            cp.wait()
        for cp in recvs:
            cp.wait_recv()
        for cp in sends:
            cp.wait_send()

    return _Comm(shards, [_sds((N_CHIPS,) + s.shape, s.dtype) for s in shards], {},
                 [pltpu.SemaphoreType.DMA((n,)), pltpu.SemaphoreType.DMA((3 * n * GATHER_PIECES,)),
                  pltpu.SemaphoreType.DMA((3 * n * GATHER_PIECES,))], start, finish)


def _gather_d2d(gathered):
    n = len(gathered)

    def copies(outs, sems):
        send_sem, recv_sem = sems
        x, y, c, chips = _mesh_place()
        sends, recvs = [], []
        for wi in range(n):
            for k, (tx, ty) in enumerate(chips):
                sems_k = dict(send_sem=send_sem.at[wi * 3 + k], recv_sem=recv_sem.at[wi * 3 + k],
                              device_id=(x, y, 1 - c), device_id_type=MESH)
                mine = outs[wi].at[2 * tx + ty, c]
                theirs = outs[wi].at[2 * tx + ty, 1 - c]
                sends.append(pltpu.make_async_remote_copy(src_ref=mine, dst_ref=mine, **sems_k))
                recvs.append(pltpu.make_async_remote_copy(src_ref=theirs, dst_ref=theirs, **sems_k))
        return sends, recvs

    def start(ins, outs, sems):
        for cp in copies(outs, sems)[0]:
            cp.start()

    def finish(ins, outs, sems):
        sends, recvs = copies(outs, sems)
        for cp in recvs:
            cp.wait_recv()
        for cp in sends:
            cp.wait_send()

    return _Comm(gathered, [_sds(g.shape, g.dtype) for g in gathered], {i: i for i in range(n)},
                 [pltpu.SemaphoreType.DMA((3 * n,)), pltpu.SemaphoreType.DMA((3 * n,))], start, finish)


def _exchange_halves(grads):
    n = len(grads)

    def copies(ins, outs, sems):
        send_sem, recv_sem = sems
        x, y, c, _ = _mesh_place()
        return [pltpu.make_async_remote_copy(
            src_ref=ins[wi].at[t, 1 - c], dst_ref=outs[wi].at[t],
            send_sem=send_sem.at[wi * N_CHIPS + t], recv_sem=recv_sem.at[wi * N_CHIPS + t],
            device_id=(x, y, 1 - c), device_id_type=MESH) for wi in range(n) for t in range(N_CHIPS)]

    def start(ins, outs, sems):
        for cp in copies(ins, outs, sems):
            cp.start()

    def finish(ins, outs, sems):
        for cp in copies(ins, outs, sems):
            cp.wait()

    return _Comm(grads, [_sds((N_CHIPS,) + g.shape[2:], g.dtype) for g in grads], {},
                 [pltpu.SemaphoreType.DMA((N_CHIPS * n,)), pltpu.SemaphoreType.DMA((N_CHIPS * n,))], start, finish)


def _scatter_ici(sums):
    n = len(sums)

    def copies(ins, outs, sems):
        local_sem, send_sem, recv_sem = sems
        x, y, c, chips = _mesh_place()
        me = 2 * x + y
        local, sends, recvs = [], [], []
        for wi in range(n):
            local.append(pltpu.make_async_copy(ins[wi].at[me], outs[wi].at[c, 0], local_sem.at[wi]))
            for k, (tx, ty) in enumerate(chips):
                sems_k = dict(send_sem=send_sem.at[wi * 3 + k], recv_sem=recv_sem.at[wi * 3 + k],
                              device_id=(tx, ty, c), device_id_type=MESH)
                land = outs[wi].at[c, k + 1]
                sends.append(pltpu.make_async_remote_copy(src_ref=ins[wi].at[2 * tx + ty], dst_ref=land, **sems_k))
                recvs.append(pltpu.make_async_remote_copy(src_ref=land, dst_ref=land, **sems_k))
        return local, sends, recvs

    def start(ins, outs, sems):
        local, sends, _ = copies(ins, outs, sems)
        for cp in local + sends:
            cp.start()

    def finish(ins, outs, sems):
        local, sends, recvs = copies(ins, outs, sems)
        for cp in local:
            cp.wait()
        for cp in recvs:
            cp.wait_recv()
        for cp in sends:
            cp.wait_send()

    return _Comm(sums, [_sds((2, N_CHIPS) + s.shape[1:], s.dtype) for s in sums], {},
                 [pltpu.SemaphoreType.DMA((n,)), pltpu.SemaphoreType.DMA((3 * n,)), pltpu.SemaphoreType.DMA((3 * n,))],
                 start, finish)


def _scatter_d2d(terms):
    n = len(terms)

    def copies(outs, sems):
        send_sem, recv_sem = sems
        x, y, c, _ = _mesh_place()
        sends, recvs = [], []
        for wi in range(n):
            sems_w = dict(send_sem=send_sem.at[wi], recv_sem=recv_sem.at[wi],
                          device_id=(x, y, 1 - c), device_id_type=MESH)
            sends.append(pltpu.make_async_remote_copy(src_ref=outs[wi].at[c], dst_ref=outs[wi].at[c], **sems_w))
            recvs.append(pltpu.make_async_remote_copy(src_ref=outs[wi].at[1 - c], dst_ref=outs[wi].at[1 - c], **sems_w))
        return sends, recvs

    def start(ins, outs, sems):
        for cp in copies(outs, sems)[0]:
            cp.start()

    def finish(ins, outs, sems):
        sends, recvs = copies(outs, sems)
        for cp in recvs:
            cp.wait_recv()
        for cp in sends:
            cp.wait_send()

    return _Comm(terms, [_sds(t.shape, t.dtype) for t in terms], {i: i for i in range(n)},
                 [pltpu.SemaphoreType.DMA((n,)), pltpu.SemaphoreType.DMA((n,))], start, finish)


def _chip_sum(name, grad, got, core):
    _, _, hr, c = grad.shape
    rb = _pick(hr, max(16, (1 << 19) // c), 16)

    def body(core_ref, a_ref, b_ref, o_ref):
        o_ref[...] = (a_ref[...].astype(F32) + b_ref[...].astype(F32)).astype(BF16)

    out_spec = pl.BlockSpec((None, rb, c), lambda t, i, core_ref: (t, i, 0))
    return pl.pallas_call(
        body, name=name,
        grid_spec=pltpu.PrefetchScalarGridSpec(
            num_scalar_prefetch=1, grid=(N_CHIPS, hr // rb),
            in_specs=[pl.BlockSpec((None, None, rb, c), lambda t, i, core_ref: (t, core_ref[0], i, 0)), out_spec],
            out_specs=out_spec),
        out_shape=_sds((N_CHIPS, hr, c), BF16), compiler_params=_params(),
    )(core, grad, got)


def _all_reduce_small(pack):
    r = pack.shape[0]

    def body(p_ref, o_ref, land_ref, send_sem, recv_sem):
        x, y, c, _ = _mesh_place()
        me = 4 * x + 2 * y + c
        flips = [(k >> 2 & 1, k >> 1 & 1, k & 1) for k in range(1, N_DEV)]

        def peer(fx, fy, fc):
            return (1 - x if fx else x, 1 - y if fy else y, 1 - c if fc else c)

        land_ref[me] = p_ref[...]
        sent = []
        for k, flip in enumerate(flips):
            cp = pltpu.make_async_remote_copy(
                src_ref=p_ref, dst_ref=land_ref.at[me], send_sem=send_sem.at[k], recv_sem=recv_sem.at[k],
                device_id=peer(*flip), device_id_type=MESH)
            cp.start()
            sent.append(cp)
        for k, flip in enumerate(flips):
            px, py, pc = peer(*flip)
            slot = land_ref.at[4 * px + 2 * py + pc]
            pltpu.make_async_remote_copy(
                src_ref=slot, dst_ref=slot, send_sem=send_sem.at[k], recv_sem=recv_sem.at[k],
                device_id=(px, py, pc), device_id_type=MESH).wait_recv()
        total = land_ref[0]
        for d in range(1, N_DEV):
            total = total + land_ref[d]
        o_ref[...] = total
        for cp in sent:
            cp.wait_send()

    vmem = pl.BlockSpec(memory_space=pltpu.VMEM)
    return pl.pallas_call(
        body, name="all_reduce_small", in_specs=[vmem], out_specs=vmem, out_shape=_sds((r, 128), F32),
        scratch_shapes=[pltpu.VMEM((N_DEV, r, 128), F32), pltpu.SemaphoreType.DMA((N_DEV - 1,)),
                        pltpu.SemaphoreType.DMA((N_DEV - 1,))],
    )(pack)


PACK_TILE = 8 * 128


def _pack(items):
    rows, i = [], 0
    while i < len(items):
        j = i
        while j < len(items) and items[j].size == items[i].size:
            j += 1
        group = jnp.stack([it.reshape(-1).astype(F32) for it in items[i:j]])
        rows.append(jnp.pad(group, ((0, 0), (0, -group.shape[1] % PACK_TILE))).reshape(-1, 128))
        i = j
    return jnp.concatenate(rows, axis=0)


def _unpack(pack, shapes):
    out, row = [], 0
    for shp in shapes:
        size = int(np.prod(shp))
        nrow = -(-size // PACK_TILE) * (PACK_TILE // 128)
        out.append(pack[row:row + nrow].reshape(-1)[:size].reshape(shp))
        row += nrow
    return out


BIG = ["ffn1_w_gu", "ffn1_w_down", "w_in", "w_gate", "w_proj_a", "w_proj_b", "w_out",
       "ffn2_w_gu", "ffn2_w_down", "w_ple_gate", "w_ple_proj"]
SMALL = ["ffn1_norm", "mix_norm", "ffn2_norm", "ple_norm", "a_q_norm", "a_k_norm", "b_q_norm", "b_k_norm",
         "a_rel_bias", "b_sinks"]
WEIGHTS = ["ffn1_norm", "ffn1_w_gu", "ffn1_w_down", "mix_norm", "w_in", "a_q_norm", "a_k_norm", "a_rel_bias",
           "b_q_norm", "b_k_norm", "b_sinks", "w_gate", "w_proj_a", "w_proj_b", "w_out", "ffn2_norm",
           "ffn2_w_gu", "ffn2_w_down", "ple_norm", "w_ple_gate", "w_ple_proj"]
ATTN_A = dict(prev=A_PREV_CHUNKS * CHUNK, group=1, kw=A_WIDTH, qblk=0, kblk=1, vblk=2)
ATTN_B = dict(prev=B_PREV_CHUNKS * CHUNK, group=N_HEADS // B_KV_HEADS, kw=B_KV_WIDTH, qblk=3,
              kblk=4 * A_WIDTH // B_KV_WIDTH, vblk=4 * A_WIDTH // B_KV_WIDTH + 1)


def _cast_epilogue(accs, extras, outs, ij):
    for acc, out in zip(accs, outs):
        out[...] = acc.astype(out.dtype)


GATHER_FIRST = ["ffn1_w_gu"]
ROW_SHARDED = ("ffn1_w_down", "ffn2_w_down", "w_out", "w_ple_gate")


def _slotted(name, grad):
    if name == "w_in":
        rows, cols = grad.shape
        grad = jnp.transpose(grad.reshape(rows, N_CHIPS, cols // N_CHIPS), (1, 0, 2))
    elif name in ROW_SHARDED:
        grad = grad.reshape(N_CHIPS, grad.shape[0] // N_CHIPS, grad.shape[1])
    return grad.reshape(N_CHIPS, 2, grad.shape[1] // 2, grad.shape[2])


def _local_step(xt, pt, tgt, n_batch, shards, small, core):
    t, d = xt.shape
    tm = _pick(t, 512, 8)
    tk = _pick(t, 512, 8)
    nt = t // tm
    row = pl.BlockSpec((tm, d), lambda i, j, k: (i, 0))
    gs = shards["w_gate"].shape[1]
    ps = shards["w_proj_a"].shape[1]
    es = shards["w_ple_proj"].shape[1]
    pdim = pt.shape[1]
    ncols = N_CHIPS * shards["w_in"].shape[1]
    tin = ncols // 2
    assert 2 * gs == d and 4 * ps == d and 4 * es == d and tin % 128 == 0

    w = {}
    halves = {n: s.reshape(2, s.shape[0] // 2, s.shape[1]) for n, s in shards.items()}

    def publish(names, arrays):
        for name, g in zip(names, arrays):
            g = g.reshape(N_CHIPS, 2 * g.shape[2], g.shape[3])
            if name in ROW_SHARDED:
                g = g.reshape(N_CHIPS * g.shape[1], g.shape[2])
            elif name == "w_in":
                g = jnp.transpose(g, (1, 0, 2)).reshape(g.shape[1], N_CHIPS * g.shape[2])
            w[name] = g

    class GatherPipe:
        def __init__(self, names):
            self.names = names

        def ici(self):
            self.first = _gather_ici([halves[n] for n in self.names])
            return self.first

        def d2d(self):
            self.second = _gather_d2d(self.first.results)
            return self.second

        def publish(self):
            publish(self.names, self.second.results)

    class GradPipe:
        def __init__(self, names):
            self.names = names

        def exchange(self, grads):
            self.grads = [_slotted(n, g) for n, g in zip(self.names, grads)]
            self.x = _exchange_halves(self.grads)
            return self.x

        def scatter(self):
            sums = [_chip_sum("chip_sum_" + n, g, got, core)
                    for n, g, got in zip(self.names, self.grads, self.x.results)]
            self.s = _scatter_ici(sums)
            return self.s

        def forward(self):
            self.f = _scatter_d2d(self.s.results)
            return self.f

        def terms(self):
            return dict(zip(self.names, self.f.results))

    publish(GATHER_FIRST, _all_gather_weights([halves[n] for n in GATHER_FIRST]))
    g_in, g_proj, g_ple = GatherPipe(["w_in", "w_gate"]), GatherPipe(["w_proj_a", "w_proj_b", "w_out"]), \
        GatherPipe(["w_ple_gate", "w_ple_proj"])
    g_down1, g_down2, g_up2 = GatherPipe(["ffn1_w_down"]), GatherPipe(["ffn2_w_down"]), GatherPipe(["ffn2_w_gu"])

    def ffn1_down_weight():
        _run_comms("gather_ffn1_down", [g_down1.d2d()])
        g_down1.publish()
        return w["ffn1_w_down"]

    h1, ffn1_saved = _ffn_fwd("ffn1", xt, small["ffn1_norm"], w["ffn1_w_gu"], ffn1_down_weight,
                              {"up": lambda: [g_down1.ici(), g_in.ici()],
                               "down": lambda: [g_in.d2d(), g_proj.ici()]})
    g_in.publish()
    w_in, wgate = w["w_in"], w["w_gate"]
    un = _rms_fwd("mix_norm", h1, small["mix_norm"])
    (qkv,) = _mm(
        "qkv", "nn", (nt, 2, 1),
        [(un, row, w_in, pl.BlockSpec((d, tin), lambda i, j, k: (0, j)))], [],
        [(_sds((t, ncols), BF16), pl.BlockSpec((tm, tin), lambda i, j, k: (i, j)))], (tm, tin), _cast_epilogue,
        j_outer=True, comms=[g_proj.d2d(), g_ple.ici()])
    g_proj.publish()
    wpa, wpb, wout = w["w_proj_a"], w["w_proj_b"], w["w_out"]

    def gate_epilogue(accs, extras, outs, ij):
        outs[0][...] = jax.nn.sigmoid(accs[0]).astype(BF16)

    (gates,) = _mm(
        "gate", "nn", (nt, 4, 1),
        [(un, row, wgate, pl.BlockSpec((None, d, gs), lambda i, j, k: (j, 0, 0)))], [],
        [(_sds((2, t, d), BF16), pl.BlockSpec((None, tm, gs), lambda i, j, k: (j // 2, i, j % 2)))],
        (tm, gs), gate_epilogue, j_outer=True, chunked=True, comms=[g_ple.d2d(), g_down2.ici()])
    g_ple.publish()
    wpg, wpe = w["w_ple_gate"], w["w_ple_proj"]

    bias_a = _pair_bias(_bias_a(small["a_rel_bias"][0]))
    bias_b = _pair_bias(_bias_b())
    sink_a = _pair_rows(jnp.full((N_HEADS, 128), NEG_INF, F32))
    sink_b = _pair_rows(jnp.broadcast_to(small["b_sinks"][0][:, None], (N_HEADS, 128)))
    gqa, gka, gqb, gkb = [jnp.tile(small[k], (1, 2)) for k in ("a_q_norm", "a_k_norm", "b_q_norm", "b_k_norm")]
    ya, lse_a = _attn_fwd("attn_a_fwd", qkv, bias_a, sink_a, gqa, gka, ATTN_A, n_batch,
                          comms=[g_down2.d2d(), g_up2.ici()])
    g_down2.publish()
    yb, lse_b = _attn_fwd("attn_b_fwd", qkv, bias_b, sink_b, gqb, gkb, ATTN_B, n_batch, comms=[g_up2.d2d()])
    g_up2.publish()

    def merge_epilogue(accs, extras, outs, ij):
        pa, pb = accs
        outs[0][...] = (extras[0][...].astype(F32) * pa + extras[1][...].astype(F32) * pb).astype(BF16)
        outs[1][...] = pa.astype(BF16)
        outs[2][...] = pb.astype(BF16)

    y_spec = pl.BlockSpec((tm, A_WIDTH), lambda i, j, k: (i, 0))
    proj_spec = pl.BlockSpec((None, A_WIDTH, ps), lambda i, j, k: (j, 0, 0))
    tile_ps = pl.BlockSpec((tm, ps), lambda i, j, k: (i, j))
    merged, pa, pb = _mm(
        "proj_merge", "nn", (nt, 4, 1),
        [(ya, y_spec, wpa, proj_spec), (yb, y_spec, wpb, proj_spec)],
        [(gates, pl.BlockSpec((None, tm, ps), lambda i, j, k: (0, i, j))),
         (gates, pl.BlockSpec((None, tm, ps), lambda i, j, k: (1, i, j)))],
        [(_sds((t, d), BF16), tile_ps)] * 3, (tm, ps), merge_epilogue)

    def residual_epilogue(accs, extras, outs, ij):
        outs[0][...] = extras[0][...] + accs[0]

    (h2,) = _mm(
        "out_proj", "nn", (nt, 1, 1),
        [(merged, row, wout, pl.BlockSpec((d, d), lambda i, j, k: (0, 0)))],
        [(h1, row)], [(_sds((t, d), F32), row)], (tm, d), residual_epilogue)

    h3, ffn2_saved = _ffn_fwd("ffn2", h2, small["ffn2_norm"], w["ffn2_w_gu"], w["ffn2_w_down"], {})
    n3 = _rms_fwd("ple_norm", h3, small["ple_norm"])
    tile_es = pl.BlockSpec((tm, es), lambda i, j, k: (i, j))
    (pe,) = _mm(
        "ple_embed", "nn", (nt, 4, 1),
        [(pt, pl.BlockSpec((tm, pdim), lambda i, j, k: (i, 0)), wpe, pl.BlockSpec((None, pdim, es), lambda i, j, k: (j, 0, 0)))],
        [], [(_sds((t, d), F32), tile_es)], (tm, es), _cast_epilogue)

    th = _pick(d, 512)

    def head_epilogue(accs, extras, outs, ij):
        h3_ref, pe_ref, tgt_ref = extras
        dy_ref, dpe_ref, dz_ref, loss_ref = outs
        pg = jax.nn.sigmoid(accs[0])
        pev = pe_ref[...]
        diff = h3_ref[...] + pg * pev - tgt_ref[...]
        dy = diff * (1.0 / d)
        dy_ref[...] = dy
        dpe_ref[...] = (dy * pg).astype(BF16)
        dz_ref[...] = (dy * pev * pg * (1.0 - pg)).astype(BF16)
        _accumulate(loss_ref, jnp.full(loss_ref.shape, jnp.sum(diff * diff), F32), (ij[0] == 0) & (ij[1] == 0))

    tile_h = pl.BlockSpec((tm, th), lambda i, j, k: (i, j))
    dy, dpe, dz, loss_acc = _mm(
        "ple_gate_loss", "nn", (nt, d // th, 1),
        [(n3, row, wpg, pl.BlockSpec((d, th), lambda i, j, k: (0, j)))],
        [(h3, tile_h), (pe, tile_h), (tgt, tile_h)],
        [(_sds((t, d), F32), tile_h), (_sds((t, d), BF16), tile_h), (_sds((t, d), BF16), tile_h),
         (_sds((8, 128), F32), pl.BlockSpec((8, 128), lambda i, j, k: (0, 0)))],
        (tm, th), head_epilogue, j_outer=True, chunked=True)
    loss = 0.5 * loss_acc[0, 0] / d

    nk = t // tk
    (dwpe,) = _mm(
        "d_w_ple_proj", "tn", (1, 4, nk),
        [(pt, pl.BlockSpec((tk, pdim), lambda i, j, k: (k, 0)), dpe, pl.BlockSpec((tk, es), lambda i, j, k: (k, j)))],
        [], [(_sds((4, pdim, es), BF16), pl.BlockSpec((None, pdim, es), lambda i, j, k: (j, 0, 0)))],
        (pdim, es), _cast_epilogue)

    def dense_grad(name, a, dyb, comms=()):
        (res,) = _mm(
            name, "tn", (1, d // th, nk),
            [(a, pl.BlockSpec((tk, d), lambda i, j, k: (k, 0)), dyb, pl.BlockSpec((tk, th), lambda i, j, k: (k, j)))],
            [], [(_sds((d, d), BF16), pl.BlockSpec((d, th), lambda i, j, k: (0, j)))], (d, th), _cast_epilogue,
            comms=comms)
        return res

    dwpg = dense_grad("d_w_ple_gate", n3, dz)
    tmn = _pick(t, 1024, 8)
    extras, outs = _rms_bwd_io(h3, small["ple_norm"], dy, tmn)
    dh3, dh3_b, d_ple_norm = _mm(
        "d_ple_norm", "nt", (t // tmn, 1, 1),
        [(dz, pl.BlockSpec((tmn, d), lambda i, j, k: (i, 0)), wpg, pl.BlockSpec((d, d), lambda i, j, k: (0, 0)))],
        extras, outs, (tmn, d), _rms_bwd_epilogue)

    up2, down2, ple = GradPipe(["ffn2_w_gu"]), GradPipe(["ffn2_w_down"]), GradPipe(["w_ple_gate", "w_ple_proj"])
    proj = GradPipe(["w_proj_a", "w_proj_b", "w_out"])
    dh2, dh2_b, d_ffn2_norm, dwgu2, dwd2 = _ffn_bwd(
        "ffn2", dh3, dh3_b, h2, small["ffn2_norm"], w["ffn2_w_gu"], w["ffn2_w_down"], ffn2_saved,
        {"dnorm": lambda dwgu, dwd: [up2.exchange([dwgu]), down2.exchange([dwd]), ple.exchange([dwpg, dwpe])]})

    def dmerge_epilogue(accs, extras, outs, ij):
        dmo = accs[0]
        g_ref, pa_ref, pb_ref = extras
        dg_ref, dpa_ref, dpb_ref = outs
        ga = g_ref[0].astype(F32)
        gb = g_ref[1].astype(F32)
        dg_ref[0] = (dmo * pa_ref[...].astype(F32) * ga * (1.0 - ga)).astype(BF16)
        dg_ref[1] = (dmo * pb_ref[...].astype(F32) * gb * (1.0 - gb)).astype(BF16)
        dpa_ref[...] = (dmo * ga).astype(BF16)
        dpb_ref[...] = (dmo * gb).astype(BF16)

    g_spec = pl.BlockSpec((2, tm, th), lambda i, j, k: (0, i, j))
    dgates, dpa, dpb = _mm(
        "d_merge", "nt", (nt, d // th, 1),
        [(dh2_b, row, wout, pl.BlockSpec((th, d), lambda i, j, k: (j, 0)))],
        [(gates, g_spec), (pa, tile_h), (pb, tile_h)],
        [(_sds((2, t, d), BF16), g_spec), (_sds((t, d), BF16), tile_h), (_sds((t, d), BF16), tile_h)],
        (tm, th), dmerge_epilogue, j_outer=True, chunked=True, comms=[down2.scatter()])
    dwout = dense_grad("d_w_out", merged, dh2_b, comms=[down2.forward(), ple.scatter()])

    yk_spec = pl.BlockSpec((tk, A_WIDTH), lambda i, j, k: (k, 0))
    dk_spec = pl.BlockSpec((tk, ps), lambda i, j, k: (k, j))
    dproj = (_sds((4, A_WIDTH, ps), BF16), proj_spec)
    dwpa, dwpb = _mm(
        "d_w_proj", "tn", (1, 4, nk),
        [(ya, yk_spec, dpa, dk_spec), (yb, yk_spec, dpb, dk_spec)], [], [dproj, dproj], (A_WIDTH, ps), _cast_epilogue,
        comms=[ple.forward()])
    dproj_a = pl.BlockSpec((tm, ps), lambda i, j, k: (i, k))
    wproj_k = pl.BlockSpec((None, A_WIDTH, ps), lambda i, j, k: (k, 0, 0))
    dya, dyb = _mm(
        "d_attn_out", "nt", (nt, 1, 4),
        [(dpa, dproj_a, wpa, wproj_k), (dpb, dproj_a, wpb, wproj_k)], [],
        [(_sds((t, A_WIDTH), BF16), y_spec)] * 2, (tm, A_WIDTH), _cast_epilogue,
        comms=[proj.exchange([dwpa, dwpb, dwout])])

    dqa, dka, dva, dbias_a, _, dgqa, dgka = _attn_bwd(
        "attn_a_bwd", qkv, bias_a, sink_a, gqa, gka, ya, dya, lse_a, ATTN_A, n_batch, True,
        comms=[up2.scatter(), proj.scatter()])
    dqb, dkb, dvb, _, dsink_b, dgqb, dgkb = _attn_bwd(
        "attn_b_bwd", qkv, bias_b, sink_b, gqb, gkb, yb, dyb, lse_b, ATTN_B, n_batch, False,
        comms=[up2.forward(), proj.forward()])
    dqkv = jnp.concatenate([dqa, dka, dva, dqb, dkb, dvb], axis=1)

    (dwgate,) = _mm(
        "d_w_gate", "tn", (1, 4, nk),
        [(un, pl.BlockSpec((tk, d), lambda i, j, k: (k, 0)),
          dgates, pl.BlockSpec((None, tk, gs), lambda i, j, k: (j // 2, k, j % 2)))],
        [], [(_sds((4, d, gs), BF16), pl.BlockSpec((None, d, gs), lambda i, j, k: (j, 0, 0)))], (d, gs), _cast_epilogue)
    (dwin,) = _mm(
        "d_w_in", "tn", (1, 2, nk),
        [(un, pl.BlockSpec((tk, d), lambda i, j, k: (k, 0)), dqkv, pl.BlockSpec((tk, tin), lambda i, j, k: (k, j)))],
        [], [(_sds((d, ncols), BF16), pl.BlockSpec((d, tin), lambda i, j, k: (0, j)))], (d, tin), _cast_epilogue)

    mixer = GradPipe(["w_in", "w_gate"])
    extras, outs = _rms_bwd_io(h1, small["mix_norm"], dh2, tmn)
    dh1, dh1_b, d_mix_norm = _mm(
        "d_mix_norm", "nt", (t // tmn, 1, 6),
        [(dgates, pl.BlockSpec((None, tmn, gs), lambda i, j, k: (jnp.minimum(k, 3) // 2, i, jnp.minimum(k, 3) % 2)),
          wgate, pl.BlockSpec((None, d, gs), lambda i, j, k: (jnp.minimum(k, 3), 0, 0))),
         (dqkv, pl.BlockSpec((tmn, tin), lambda i, j, k: (i, jnp.maximum(k - 4, 0))),
          w_in, pl.BlockSpec((d, tin), lambda i, j, k: (0, jnp.maximum(k - 4, 0))))],
        extras, outs, (tmn, d), _rms_bwd_epilogue, steps=[4, 2],
        comms=[mixer.exchange([dwin, dwgate])])

    up1 = GradPipe(["ffn1_w_gu"])
    down1 = GradPipe(["ffn1_w_down"])
    dx, _, d_ffn1_norm, _, _ = _ffn_bwd(
        "ffn1", dh1, dh1_b, xt, small["ffn1_norm"], w["ffn1_w_gu"], w["ffn1_w_down"], ffn1_saved,
        {"dact": lambda: [mixer.scatter()],
         "dwgu": lambda: [mixer.forward()],
         "dwd": lambda dwgu: [up1.exchange([dwgu])],
         "dnorm": lambda dwgu, dwd: [up1.scatter(), down1.exchange([dwd])]})
    _run_comms("grad_tail_scatter", [up1.forward(), down1.scatter()])
    _run_comms("grad_tail_forward", [down1.forward()])
    terms = {}
    for pipe in (up2, down2, ple, proj, mixer, up1, down1):
        terms.update(pipe.terms())

    def fold(v):
        return v[0, :HEAD_DIM] + v[0, HEAD_DIM:]

    small_grads = {"ffn1_norm": d_ffn1_norm, "mix_norm": d_mix_norm, "ffn2_norm": d_ffn2_norm,
                   "ple_norm": d_ple_norm, "a_q_norm": fold(dgqa), "a_k_norm": fold(dgka),
                   "b_q_norm": fold(dgqb), "b_k_norm": fold(dgkb), "a_rel_bias": _rel_bias_grad(_unpair_bias(dbias_a)),
                   "b_sinks": jnp.sum(dsink_b, axis=1)}
    return loss, dx, terms, small_grads


def kernel(x, p, ffn1_norm, ffn1_w_gu, ffn1_w_down, mix_norm, w_in, a_q_norm, a_k_norm, a_rel_bias, b_q_norm, b_k_norm, b_sinks, w_gate, w_proj_a, w_proj_b, w_out, ffn2_norm, ffn2_w_gu, ffn2_w_down, ple_norm, w_ple_gate, w_ple_proj, loss_target, m_ffn1_norm, m_ffn1_w_gu, m_ffn1_w_down, m_mix_norm, m_w_in, m_a_q_norm, m_a_k_norm, m_a_rel_bias, m_b_q_norm, m_b_k_norm, m_b_sinks, m_w_gate, m_w_proj_a, m_w_proj_b, m_w_out, m_ffn2_norm, m_ffn2_w_gu, m_ffn2_w_down, m_ple_norm, m_w_ple_gate, m_w_ple_proj, v_ffn1_norm, v_ffn1_w_gu, v_ffn1_w_down, v_mix_norm, v_w_in, v_a_q_norm, v_a_k_norm, v_a_rel_bias, v_b_q_norm, v_b_k_norm, v_b_sinks, v_w_gate, v_w_proj_a, v_w_proj_b, v_w_out, v_ffn2_norm, v_ffn2_w_gu, v_ffn2_w_down, v_ple_norm, v_w_ple_gate, v_w_ple_proj):
    given = dict(locals())
    n_batch, s, d = x.shape
    t = n_batch * s
    xt = x.reshape(t, d)
    pt = p.reshape(t, p.shape[-1])
    tgt = loss_target.reshape(t, d)

    shards = {}
    for name in BIG:
        (shards[name],) = _ew("cast_" + name, lambda v: (v,), [given[name][0]], [BF16])
    small = {name: given[name] for name in SMALL}
    core = lax.axis_index("c").astype(jnp.int32).reshape(1)
    loss, dx, terms, small_grads = _local_step(xt, pt, tgt, n_batch, shards, small, core)

    grads, deltas, new_m, new_v = {}, {}, {}, {}
    for name in BIG:
        gw, dl, nm, nv = _adamw_terms("adamw_" + name, terms[name], given[name][0], given["m_" + name][0],
                                      given["v_" + name][0])
        grads[name], deltas[name], new_m[name], new_v[name] = gw[None], dl[None], nm[None], nv[None]

    small_shapes = [given[name].shape for name in SMALL] + [()]
    g_pack = _all_reduce_small(_pack([small_grads[name] for name in SMALL] + [loss]))
    zero = jnp.zeros((), F32)
    w_pack = _pack([given[name] for name in SMALL] + [zero])
    m_pack = _pack([given["m_" + name] for name in SMALL] + [zero])
    v_pack = _pack([given["v_" + name] for name in SMALL] + [zero])
    d_pack, nm_pack, nv_pack = _ew("adamw_small", lambda wv, gv, mv, vv: _adamw_math(wv, gv, mv, vv),
                                   [w_pack, g_pack, m_pack, v_pack], [F32] * 3)
    g_small = _unpack(g_pack, small_shapes)
    loss_total = g_small[-1]
    for name, gv, dv, mv, vv in zip(SMALL, g_small, _unpack(d_pack, small_shapes), _unpack(nm_pack, small_shapes),
                                    _unpack(nv_pack, small_shapes)):
        grads[name], deltas[name], new_m[name], new_v[name] = gv, dv, mv, vv

    return (loss_total, dx.reshape(x.shape), *[grads[n] for n in WEIGHTS], *[deltas[n] for n in WEIGHTS],
            *[new_m[n] for n in WEIGHTS], *[new_v[n] for n in WEIGHTS])
```

```python
import functools

import numpy as np
import jax
import jax.numpy as jnp
from jax import lax
from jax.experimental import pallas as pl
from jax.experimental.pallas import tpu as pltpu

F32 = jnp.float32
BF16 = jnp.bfloat16

CHUNK = 64
HEAD_DIM = 64
A_PREV_CHUNKS = 8
A_MAX_REL = 128
N_HEADS = 8
B_KV_HEADS = 2
B_PREV_CHUNKS = 2
A_WIDTH = N_HEADS * HEAD_DIM
B_KV_WIDTH = B_KV_HEADS * HEAD_DIM
EPS = 1e-6
NEG_INF = -1e30
ATTN_SCALE = HEAD_DIM ** -0.5
Q_BLOCK = 128
PAIR = 2 * HEAD_DIM

ADAM_LR = 0.001
ADAM_B1 = 0.9
ADAM_B2 = 0.999
ADAM_EPS = 1e-08
ADAM_WD = 0.01
ADAM_STEP = 10

N_CHIPS = 4
N_DEV = 8
VMEM_LIMIT_V7X = 56 * 1024 * 1024
MESH = pl.DeviceIdType.MESH
ANY = pl.BlockSpec(memory_space=pl.ANY)

_DN = {
    "nn": (((1,), (0,)), ((), ())),
    "nt": (((1,), (1,)), ((), ())),
    "tn": (((0,), (0,)), ((), ())),
}


def _pick(n, target, mult=128):
    best = None
    for d in range(mult, min(n, target) + 1, mult):
        if n % d == 0:
            best = d
    return n if best is None else best


def _dot(a, b, mode):
    return lax.dot_general(a.astype(BF16), b.astype(BF16), _DN[mode], preferred_element_type=F32)


def _params():
    return pltpu.CompilerParams(vmem_limit_bytes=VMEM_LIMIT_V7X)


class _Comm:
    def __init__(self, ins, outs, aliases, sems, start, finish):
        self.ins, self.outs, self.aliases, self.sems = list(ins), list(outs), dict(aliases), list(sems)
        self.start, self.finish = start, finish
        self.results = None


class _CommPlumbing:
    def __init__(self, comms, n_in, n_out, n_scratch):
        self.comms = list(comms)
        self.n_in, self.n_out, self.n_scratch = n_in, n_out, n_scratch
        self.args = [a for cm in self.comms for a in cm.ins]
        self.out_shape = [o for cm in self.comms for o in cm.outs]
        self.scratch = [s for cm in self.comms for s in cm.sems]
        self.aliases = {}
        i0, o0 = n_in, n_out
        for cm in self.comms:
            for a, b in cm.aliases.items():
                self.aliases[i0 + a] = o0 + b
            i0 += len(cm.ins)
            o0 += len(cm.outs)

    def run(self, in_refs, out_refs, scratch_refs, first, last):
        if not self.comms:
            return
        parts = []
        i0, o0, s0 = self.n_in, self.n_out, self.n_scratch
        for cm in self.comms:
            parts.append((in_refs[i0:i0 + len(cm.ins)], out_refs[o0:o0 + len(cm.outs)],
                          scratch_refs[s0:s0 + len(cm.sems)]))
            i0 += len(cm.ins)
            o0 += len(cm.outs)
            s0 += len(cm.sems)

        @pl.when(first)
        def _():
            for cm, part in zip(self.comms, parts):
                cm.start(*part)

        @pl.when(last)
        def _():
            for cm, part in zip(self.comms, parts):
                cm.finish(*part)

    def deliver(self, results):
        o0 = self.n_out
        for cm in self.comms:
            cm.results = list(results[o0:o0 + len(cm.outs)])
            o0 += len(cm.outs)
        return list(results[:self.n_out])


def _swap_ij(spec):
    index_map = spec.index_map
    return pl.BlockSpec(spec.block_shape, lambda j, i, k: index_map(i, j, k))


MXU_COLUMNS_V7X = 256


def _mm(name, mode, grid, pairs, extras, outs, acc_shape, epilogue, steps=None, comms=(), j_outer=False,
        chunked=False):
    ni, nj, nk = grid
    slots = [pair[4] if len(pair) > 4 else None for pair in pairs]
    pairs = [pair[:4] for pair in pairs]
    n_in = 2 * len(pairs) + len(extras)
    n_out = len(outs)
    tn = acc_shape[1]
    col_chunks = None
    if chunked:
        assert nk == 1 and steps is None and mode in ("nn", "nt")
        col_chunks = [(c0, min(MXU_COLUMNS_V7X, tn - c0)) for c0 in range(0, tn, MXU_COLUMNS_V7X)]
    n_acc = 0 if chunked else (len(pairs) if steps is None else 1)
    plumb = _CommPlumbing(comms, n_in, n_out, n_acc)
    n_all_in = n_in + len(plumb.args)
    n_all_out = n_out + len(plumb.out_shape)
    if j_outer:
        grid = (nj, ni, nk)
        pairs = [(a, _swap_ij(a_spec), b, _swap_ij(b_spec)) for a, a_spec, b, b_spec in pairs]
        extras = [(e, _swap_ij(e_spec)) for e, e_spec in extras]
        outs = [(o, _swap_ij(o_spec)) for o, o_spec in outs]

    def body(*refs):
        in_refs = refs[:n_all_in]
        out_refs = refs[n_all_in:n_all_in + n_all_out]
        scratch = refs[n_all_in + n_all_out:]
        accs = scratch[:n_acc]
        i = pl.program_id(1 if j_outer else 0)
        j = pl.program_id(0 if j_outer else 1)
        k = pl.program_id(2)

        def contrib(p, acc):
            b_ref = in_refs[2 * p + 1]
            rhs = b_ref[...] if slots[p] is None else b_ref[slots[p](i, j, k)]
            acc[...] += _dot(in_refs[2 * p][...], rhs, mode)

        if col_chunks:
            def cols(ref, c0, cs):
                if ref.shape[-1] != tn:
                    return ref
                return ref.at[(slice(None),) * (len(ref.shape) - 1) + (pl.ds(c0, cs),)]

            lhs = [in_refs[2 * p][...] for p in range(len(pairs))]
            for ci, (c0, cs) in enumerate(col_chunks):
                vals = []
                for p in range(len(pairs)):
                    b_ref = in_refs[2 * p + 1]
                    rhs = b_ref[:, c0:c0 + cs] if mode == "nn" else b_ref[c0:c0 + cs, :]
                    vals.append(_dot(lhs[p], rhs, mode))
                epilogue(vals, [cols(r, c0, cs) for r in in_refs[2 * len(pairs):n_in]],
                         [cols(r, c0, cs) for r in out_refs[:n_out]], (i, j * len(col_chunks) + ci))
        else:
            @pl.when(k == 0)
            def _():
                for acc in accs:
                    acc[...] = jnp.zeros(acc.shape, F32)

            if steps is None:
                for p in range(len(pairs)):
                    contrib(p, accs[p])
            else:
                lo = 0
                for p, n in enumerate(steps):
                    pl.when((k >= lo) & (k < lo + n))(functools.partial(contrib, p, accs[0]))
                    lo += n

            @pl.when(k == nk - 1)
            def _():
                epilogue([acc[...] for acc in accs], in_refs[2 * len(pairs):n_in], out_refs[:n_out], (i, j))

        plumb.run(in_refs, out_refs, scratch, (i == 0) & (j == 0) & (k == 0),
                  (i == ni - 1) & (j == nj - 1) & (k == nk - 1))

    args, in_specs = [], []
    for a, a_spec, b, b_spec in pairs:
        args += [a, b]
        in_specs += [a_spec, b_spec]
    for e, e_spec in extras:
        args.append(e)
        in_specs.append(e_spec)
    res = pl.pallas_call(
        body,
        name=name,
        grid=grid,
        in_specs=in_specs + [ANY] * len(plumb.args),
        out_specs=[s for _, s in outs] + [ANY] * len(plumb.out_shape),
        out_shape=[o for o, _ in outs] + plumb.out_shape,
        scratch_shapes=[pltpu.VMEM(acc_shape, F32) for _ in range(n_acc)] + plumb.scratch,
        input_output_aliases=plumb.aliases,
        compiler_params=_params(),
    )(*args, *plumb.args)
    return plumb.deliver(res)


def _sds(shape, dtype):
    return jax.ShapeDtypeStruct(shape, dtype)


def _accumulate(ref, value, first):
    @pl.when(first)
    def _():
        ref[...] = value

    @pl.when(jnp.logical_not(first))
    def _():
        ref[...] += value


def _rms_fwd(name, x, gain):
    t, d = x.shape
    tm = _pick(t, 512, 8)

    def body(x_ref, g_ref, y_ref):
        xv = x_ref[...]
        rstd = lax.rsqrt(jnp.mean(xv * xv, axis=-1, keepdims=True) + EPS)
        y_ref[...] = (xv * rstd * g_ref[...]).astype(BF16)

    return pl.pallas_call(
        body, name=name, grid=(t // tm,),
        in_specs=[pl.BlockSpec((tm, d), lambda i: (i, 0)), pl.BlockSpec((1, d), lambda i: (0, 0))],
        out_specs=pl.BlockSpec((tm, d), lambda i: (i, 0)),
        out_shape=_sds((t, d), BF16),
        compiler_params=_params(),
    )(x, gain)


def _rms_bwd_epilogue(accs, extras, outs, ij):
    x_ref, g_ref, r_ref = extras
    dh_ref, dhb_ref, dg_ref = outs
    dn = accs[0]
    xv = x_ref[...]
    rstd = lax.rsqrt(jnp.mean(xv * xv, axis=-1, keepdims=True) + EPS)
    xhat = xv * rstd
    gd = dn * g_ref[...]
    dx = rstd * (gd - xhat * jnp.mean(gd * xhat, axis=-1, keepdims=True))
    dh = r_ref[...] + dx
    dh_ref[...] = dh
    dhb_ref[...] = dh.astype(BF16)
    _accumulate(dg_ref, jnp.sum(dn * xhat, axis=0, keepdims=True), ij[0] == 0)


def _rms_bwd_io(x, gain, dres, tm):
    t, d = x.shape
    row = pl.BlockSpec((tm, d), lambda i, j, k: (i, 0))
    extras = [(x, row), (gain, pl.BlockSpec((1, d), lambda i, j, k: (0, 0))), (dres, row)]
    outs = [(_sds((t, d), F32), row), (_sds((t, d), BF16), row),
            (_sds((1, d), F32), pl.BlockSpec((1, d), lambda i, j, k: (0, 0)))]
    return extras, outs


def _ffn_fwd(tag, h, gain, wgu, wd, hooks):
    t, d = h.shape
    fs = wgu.shape[2]
    f = 2 * fs
    tm = _pick(t, 512, 8)
    n = _rms_fwd(tag + "_norm", h, gain)

    def up_epilogue(accs, extras, outs, ij):
        g, u = accs
        gu_ref, a_ref = outs
        gu_ref[0] = g.astype(BF16)
        gu_ref[1] = u.astype(BF16)
        a_ref[...] = (g * jax.nn.sigmoid(g) * u).astype(BF16)

    a_spec = pl.BlockSpec((tm, d), lambda i, j, k: (i, 0))
    gu, a = _mm(
        tag + "_up", "nn", (t // tm, 2, 1),
        [(n, a_spec, wgu, pl.BlockSpec((None, d, fs), lambda i, j, k: (j, 0, 0))),
         (n, a_spec, wgu, pl.BlockSpec((None, d, fs), lambda i, j, k: (j + 2, 0, 0)))],
        [],
        [(_sds((2, t, f), BF16), pl.BlockSpec((2, tm, fs), lambda i, j, k: (0, i, j))),
         (_sds((t, f), BF16), pl.BlockSpec((tm, fs), lambda i, j, k: (i, j)))],
        (tm, fs), up_epilogue, comms=hooks.get("up", lambda: ())(), j_outer=True, chunked=True)

    def down_epilogue(accs, extras, outs, ij):
        outs[0][...] = extras[0][...] + 0.5 * accs[0]


    row = pl.BlockSpec((tm, d), lambda i, j, k: (i, 0))
    (h_new,) = _mm(
        tag + "_down", "nn", (t // tm, 1, 1),
        [(a, pl.BlockSpec((tm, f), lambda i, j, k: (i, 0)), wd, pl.BlockSpec((f, d), lambda i, j, k: (0, 0)))],
        [(h, row)], [(_sds((t, d), F32), row)], (tm, d), down_epilogue, comms=hooks.get("down", lambda: ())())
    return h_new, (n, gu, a)


def _ffn_bwd(tag, dh, dh_b, h, gain, wgu, wd, saved, hooks):
    n, gu, a = saved
    t, d = h.shape
    fs = wgu.shape[2]
    f = 2 * fs
    tm = _pick(t, 512, 8)
    tk = _pick(t, 512, 8)

    def dact_epilogue(accs, extras, outs, ij):
        da = 0.5 * accs[0]
        g = extras[0][0].astype(F32)
        u = extras[0][1].astype(F32)
        sg = jax.nn.sigmoid(g)
        outs[0][0] = (da * u * sg * (1.0 + g * (1.0 - sg))).astype(BF16)
        outs[0][1] = (da * g * sg).astype(BF16)

    gu_spec = pl.BlockSpec((2, tm, fs), lambda i, j, k: (0, i, j))
    (dgu,) = _mm(
        tag + "_dact", "nt", (t // tm, 2, 1),
        [(dh_b, pl.BlockSpec((tm, d), lambda i, j, k: (i, 0)), wd, pl.BlockSpec((fs, d), lambda i, j, k: (j, 0)))],
        [(gu, gu_spec)], [(_sds((2, t, f), BF16), gu_spec)], (tm, fs), dact_epilogue, j_outer=True, chunked=True,
        comms=hooks.get("dact", lambda: ())())

    def cast_epilogue(accs, extras, outs, ij):
        outs[0][...] = accs[0].astype(BF16)

    (dwgu,) = _mm(
        tag + "_dwgu", "tn", (1, 4, t // tk),
        [(n, pl.BlockSpec((tk, d), lambda i, j, k: (k, 0)),
          dgu, pl.BlockSpec((None, tk, fs), lambda i, j, k: (j // 2, k, j % 2)))],
        [], [(_sds((4, d, fs), BF16), pl.BlockSpec((None, d, fs), lambda i, j, k: (j, 0, 0)))], (d, fs), cast_epilogue,
        comms=hooks.get("dwgu", lambda: ())())

    def half_epilogue(accs, extras, outs, ij):
        outs[0][...] = (0.5 * accs[0]).astype(BF16)

    (dwd,) = _mm(
        tag + "_dwd", "tn", (2, 1, t // tk),
        [(a, pl.BlockSpec((tk, fs), lambda i, j, k: (k, i)), dh_b, pl.BlockSpec((tk, d), lambda i, j, k: (k, 0)))],
        [], [(_sds((f, d), BF16), pl.BlockSpec((fs, d), lambda i, j, k: (i, 0)))], (fs, d), half_epilogue,
        comms=hooks.get("dwd", lambda g: ())(dwgu))

    tmn = _pick(t, 1024, 8)
    extras, outs = _rms_bwd_io(h, gain, dh, tmn)
    dh_in, dh_in_b, dgain = _mm(
        tag + "_dnorm", "nt", (t // tmn, 1, 4),
        [(dgu, pl.BlockSpec((None, tmn, fs), lambda i, j, k: (k // 2, i, k % 2)),
          wgu, pl.BlockSpec((None, d, fs), lambda i, j, k: (k, 0, 0)))],
        extras, outs, (tmn, d), _rms_bwd_epilogue, comms=hooks.get("dnorm", lambda g, w: ())(dwgu, dwd))
    return dh_in, dh_in_b, dgain, dwgu, dwd


def _lane_lo(shape):
    return lax.broadcasted_iota(jnp.int32, shape, 1) < HEAD_DIM


def _pair_norm(xv, gain):
    lo = _lane_lo(xv.shape)
    x2 = xv * xv
    ms_lo = jnp.sum(jnp.where(lo, x2, 0.0), axis=-1, keepdims=True) * (1.0 / HEAD_DIM)
    ms_hi = jnp.sum(jnp.where(lo, 0.0, x2), axis=-1, keepdims=True) * (1.0 / HEAD_DIM)
    rstd = jnp.where(lo, lax.rsqrt(ms_lo + EPS), lax.rsqrt(ms_hi + EPS))
    xhat = xv * rstd
    return xhat * gain, xhat, rstd


def _pair_norm_bwd(dn, xhat, rstd, gain):
    lo = _lane_lo(dn.shape)
    gd = dn * gain
    t = gd * xhat
    m_lo = jnp.sum(jnp.where(lo, t, 0.0), axis=-1, keepdims=True) * (1.0 / HEAD_DIM)
    m_hi = jnp.sum(jnp.where(lo, 0.0, t), axis=-1, keepdims=True) * (1.0 / HEAD_DIM)
    dx = rstd * (gd - xhat * jnp.where(lo, m_lo, m_hi))
    return dx, jnp.sum(dn * xhat, axis=0, keepdims=True)


def _half(xv, hi):
    lo = _lane_lo(xv.shape)
    return jnp.where(lo, 0, xv) if hi else jnp.where(lo, xv, 0)


def _attn_window(i, prev):
    q0 = i * Q_BLOCK
    start = jnp.maximum(q0 - prev, 0)
    off = start - (q0 - prev)
    return pl.multiple_of(start, Q_BLOCK), pl.multiple_of(off, Q_BLOCK)


def _attn_specs(cfg, s, nq):
    kw = cfg["kw"]
    q_spec = pl.BlockSpec((Q_BLOCK, A_WIDTH), lambda b, i: (b * nq + i, cfg["qblk"]))
    k_spec = pl.BlockSpec((s, kw), lambda b, i: (b, cfg["kblk"]))
    v_spec = pl.BlockSpec((s, kw), lambda b, i: (b, cfg["vblk"]))
    return q_spec, k_spec, v_spec


def _const_spec(shape):
    return pl.BlockSpec(shape, lambda b, i: (0,) * len(shape))


KEY_CHUNK = 128


def _pair_bias(bias_t):
    wext = bias_t.shape[1]
    return jnp.transpose(bias_t.reshape(N_HEADS // 2, 2, wext, Q_BLOCK), (0, 2, 1, 3)).reshape(
        N_HEADS // 2, wext, 2 * Q_BLOCK)


def _unpair_bias(db2):
    wext = db2.shape[1]
    return jnp.transpose(db2.reshape(N_HEADS // 2, wext, 2, Q_BLOCK), (0, 2, 1, 3)).reshape(N_HEADS, wext, Q_BLOCK)


def _pair_rows(rows):
    two = rows.reshape(N_HEADS // 2, 2 * rows.shape[1])
    return jnp.broadcast_to(two[:, None, :], (N_HEADS // 2, 8, two.shape[1]))


def _sub_lo(shape):
    return lax.broadcasted_iota(jnp.int32, shape, 0) < HEAD_DIM


def _by_half(lo_row, hi_row, rows):
    return jnp.where(_sub_lo((rows, lo_row.shape[1])), lo_row, hi_row)


def _stack_pair(xn, jq, group):
    parts = []
    for hq in range(2):
        hk = ((2 * jq + hq) // group) % 2
        xm = _half(xn, hq)
        if hq != hk:
            xm = pltpu.roll(xm, HEAD_DIM, 1)
        parts.append(xm)
    return jnp.concatenate(parts, axis=0).astype(BF16)


def _place_transposed(blk, dst_ref, c, heads, group):
    bt = blk.T
    lo = _sub_lo(bt.shape)
    for h in heads:
        src_hi = ((h // group) % 2) == 1
        part = jnp.where(lo, 0.0, bt) if src_hi else jnp.where(lo, bt, 0.0)
        if src_hi != (h % 2 == 1):
            part = pltpu.roll(part, HEAD_DIM, 0)
        dst_ref[h, c] = part.astype(BF16)


def _attn_fwd(name, qkv, bias2, sink2, gq, gk, cfg, n_batch, comms=()):
    t = qkv.shape[0]
    s = t // n_batch
    nq = s // Q_BLOCK
    nkc = s // KEY_CHUNK
    prev, group, kw = cfg["prev"], cfg["group"], cfg["kw"]
    n_chunks = (prev + Q_BLOCK) // KEY_CHUNK
    wext = bias2.shape[1]
    plumb = _CommPlumbing(comms, 7, 2, 2)
    n_all_in = 7 + len(plumb.args)
    n_all_out = 2 + len(plumb.out_shape)

    def body(*refs):
        q_ref, k_ref, v_ref, bias_ref, sink_ref, gq_ref, gk_ref = refs[:7]
        y_ref, lse_ref = refs[n_all_in:n_all_in + 2]
        kn_ref, vt_ref = refs[n_all_in + n_all_out:n_all_in + n_all_out + 2]
        i = pl.program_id(1)
        plumb.run(refs[:n_all_in], refs[n_all_in:n_all_in + n_all_out], refs[n_all_in + n_all_out:],
                  (pl.program_id(0) == 0) & (i == 0), (pl.program_id(0) == n_batch - 1) & (i == nq - 1))

        @pl.when(i == 0)
        def _():
            for jk in range(kw // PAIR):
                cols = pl.ds(jk * PAIR, PAIR)
                heads = [h for h in range(N_HEADS) if (h // group) // 2 == jk]
                kn, _, _ = _pair_norm(k_ref[:, cols].astype(F32), gk_ref[...])
                kn_ref[:, cols] = kn.astype(BF16)
                for c in range(nkc):
                    _place_transposed(v_ref[pl.ds(c * KEY_CHUNK, KEY_CHUNK), cols].astype(F32), vt_ref, c, heads, group)

        start, off = _attn_window(i, prev)
        c0 = start // KEY_CHUNK
        sub8 = lax.broadcasted_iota(jnp.int32, (N_HEADS, Q_BLOCK), 0)
        lse = jnp.zeros((N_HEADS, Q_BLOCK), F32)
        for jq in range(N_HEADS // 2):
            kcols = pl.ds((((2 * jq) // group) // 2) * PAIR, PAIR)
            qn, _, _ = _pair_norm(q_ref[:, pl.ds(jq * PAIR, PAIR)].astype(F32), gq_ref[...])
            qs = _stack_pair(qn * ATTN_SCALE, jq, group)
            m = sink_ref[jq, 0:1, :]
            l = jnp.ones((1, 2 * Q_BLOCK), F32)
            ot = jnp.zeros((PAIR, Q_BLOCK), F32)
            for c in range(n_chunks):
                rows = pl.ds(start + c * KEY_CHUNK, KEY_CHUNK)
                s2 = _dot(kn_ref[rows, kcols], qs, "nt") + bias_ref[jq, pl.ds(off + c * KEY_CHUNK, KEY_CHUNK), :]
                m_new = jnp.maximum(m, jnp.max(s2, axis=0, keepdims=True))
                alpha = jnp.exp(m - m_new)
                p = jnp.exp(s2 - m_new)
                l = alpha * l + jnp.sum(p, axis=0, keepdims=True)
                m = m_new
                pst = jnp.concatenate([p[:, :Q_BLOCK], p[:, Q_BLOCK:]], axis=0)
                vl = jnp.concatenate([vt_ref[2 * jq, c0 + c], vt_ref[2 * jq + 1, c0 + c]], axis=1)
                ot = ot * _by_half(alpha[:, :Q_BLOCK], alpha[:, Q_BLOCK:], PAIR) + _dot(vl, pst, "nn")
            inv = 1.0 / l
            ot = ot * _by_half(inv[:, :Q_BLOCK], inv[:, Q_BLOCK:], PAIR)
            y_ref[:, pl.ds(jq * PAIR, PAIR)] = ot.T.astype(BF16)
            lse2 = m + jnp.log(l)
            lse = jnp.where(sub8 == 2 * jq, lse2[:, :Q_BLOCK], lse)
            lse = jnp.where(sub8 == 2 * jq + 1, lse2[:, Q_BLOCK:], lse)
        lse_ref[...] = lse

    q_spec, k_spec, v_spec = _attn_specs(cfg, s, nq)
    res = pl.pallas_call(
        body, name=name, grid=(n_batch, nq),
        in_specs=[q_spec, k_spec, v_spec, _const_spec((N_HEADS // 2, wext, 2 * Q_BLOCK)),
                  _const_spec((N_HEADS // 2, 8, 2 * Q_BLOCK)), _const_spec((1, PAIR)), _const_spec((1, PAIR))]
        + [ANY] * len(plumb.args),
        out_specs=[pl.BlockSpec((Q_BLOCK, A_WIDTH), lambda b, i: (b * nq + i, 0)),
                   pl.BlockSpec((None, N_HEADS, Q_BLOCK), lambda b, i: (b * nq + i, 0, 0))]
        + [ANY] * len(plumb.out_shape),
        out_shape=[_sds((t, A_WIDTH), BF16), _sds((t // Q_BLOCK, N_HEADS, Q_BLOCK), F32)] + plumb.out_shape,
        scratch_shapes=[pltpu.VMEM((s, kw), BF16), pltpu.VMEM((N_HEADS, nkc, PAIR, KEY_CHUNK), BF16)] + plumb.scratch,
        input_output_aliases=plumb.aliases,
        compiler_params=_params(),
    )(qkv, qkv, qkv, bias2, sink2, gq, gk, *plumb.args)
    return plumb.deliver(res)


def _attn_bwd(name, qkv, bias2, sink2, gq, gk, y, dy, lse, cfg, n_batch, want_dbias, comms=()):
    t = qkv.shape[0]
    s = t // n_batch
    nq = s // Q_BLOCK
    nkc = s // KEY_CHUNK
    prev, group, kw = cfg["prev"], cfg["group"], cfg["kw"]
    w = prev + Q_BLOCK
    n_chunks = w // KEY_CHUNK
    wext = bias2.shape[1]
    plumb = _CommPlumbing(comms, 10, 7, 9)
    n_all_in = 10 + len(plumb.args)
    n_all_out = 7 + len(plumb.out_shape)

    def body(*refs):
        q_ref, k_ref, v_ref, bias_ref, sink_ref, gq_ref, gk_ref, y_ref, dy_ref, lse_ref = refs[:10]
        dq_ref, dk_ref, dv_ref, db_ref, dsink_ref, dgq_ref, dgk_ref = refs[n_all_in:n_all_in + 7]
        kn_ref, knt_ref, dkn_ref, dvs_ref, s_ref, dp_ref, pb_ref, dsb_ref, dst_ref = \
            refs[n_all_in + n_all_out:n_all_in + n_all_out + 9]
        b = pl.program_id(0)
        i = pl.program_id(1)
        first = (b == 0) & (i == 0)
        plumb.run(refs[:n_all_in], refs[n_all_in:n_all_in + n_all_out], refs[n_all_in + n_all_out:],
                  first, (b == n_batch - 1) & (i == nq - 1))

        @pl.when(i == 0)
        def _():
            for jk in range(kw // PAIR):
                cols = pl.ds(jk * PAIR, PAIR)
                heads = [h for h in range(N_HEADS) if (h // group) // 2 == jk]
                for c in range(nkc):
                    rows = pl.ds(c * KEY_CHUNK, KEY_CHUNK)
                    kn, _, _ = _pair_norm(k_ref[rows, cols].astype(F32), gk_ref[...])
                    kn_ref[rows, cols] = kn.astype(BF16)
                    _place_transposed(kn, knt_ref, c, heads, group)
            dkn_ref[...] = jnp.zeros(dkn_ref.shape, F32)
            dvs_ref[...] = jnp.zeros(dvs_ref.shape, F32)

        @pl.when(first)
        def _():
            db_ref[...] = jnp.zeros(db_ref.shape, F32)
            dsink_ref[...] = jnp.zeros(dsink_ref.shape, F32)
            dgq_ref[...] = jnp.zeros(dgq_ref.shape, F32)
            dgk_ref[...] = jnp.zeros(dgk_ref.shape, F32)

        start, off = _attn_window(i, prev)
        c0 = start // KEY_CHUNK
        for jq in range(N_HEADS // 2):
            cols = pl.ds(jq * PAIR, PAIR)
            kcols = pl.ds((((2 * jq) // group) // 2) * PAIR, PAIR)
            qn, q_hat, q_rstd = _pair_norm(q_ref[:, cols].astype(F32), gq_ref[...])
            qs = _stack_pair(qn * ATTN_SCALE, jq, group)
            do_pair = dy_ref[:, cols].astype(F32)
            dos = _stack_pair(do_pair, jq, group)
            prod_t = (do_pair * y_ref[:, cols].astype(F32)).T
            lo = _sub_lo(prod_t.shape)
            delta2 = jnp.concatenate([jnp.sum(jnp.where(lo, prod_t, 0.0), axis=0, keepdims=True),
                                      jnp.sum(jnp.where(lo, 0.0, prod_t), axis=0, keepdims=True)], axis=1)
            lse2 = jnp.concatenate([lse_ref[2 * jq:2 * jq + 1, :], lse_ref[2 * jq + 1:2 * jq + 2, :]], axis=1)
            dsk = -jnp.exp(sink_ref[jq, 0:1, :] - lse2) * delta2
            dsink_ref[2 * jq:2 * jq + 1, :] += dsk[:, :Q_BLOCK]
            dsink_ref[2 * jq + 1:2 * jq + 2, :] += dsk[:, Q_BLOCK:]
            rows_w = pl.ds(start, w)
            s_ref[...] = _dot(kn_ref[rows_w, kcols], qs, "nt")
            dp_ref[...] = _dot(v_ref[rows_w, kcols], dos, "nt")
            for c in range(n_chunks):
                r = pl.ds(c * KEY_CHUNK, KEY_CHUNK)
                brows = pl.ds(off + c * KEY_CHUNK, KEY_CHUNK)
                p = jnp.exp(s_ref[r, :] + bias_ref[jq, brows, :] - lse2)
                ds = p * (dp_ref[r, :] - delta2)
                if want_dbias:
                    db_ref[jq, brows, :] += ds
                ds_b = ds.astype(BF16)
                pb_ref[r, :] = p.astype(BF16)
                dsb_ref[r, :] = ds_b
                dst_ref[pl.ds(2 * c * KEY_CHUNK, KEY_CHUNK), :] = ds_b[:, :Q_BLOCK]
                dst_ref[pl.ds((2 * c + 1) * KEY_CHUNK, KEY_CHUNK), :] = ds_b[:, Q_BLOCK:]
            dkn_ref[rows_w, kcols] += _dot(dsb_ref[...], qs, "nn")
            dvs_ref[rows_w, kcols] += _dot(pb_ref[...], dos, "nn")
            kl = jnp.concatenate([knt_ref[2 * jq + hq, c0 + c] for c in range(n_chunks) for hq in range(2)], axis=1)
            dqt = _dot(kl, dst_ref[...], "nn")
            dq_raw, dg = _pair_norm_bwd(dqt.T * ATTN_SCALE, q_hat, q_rstd, gq_ref[...])
            dq_ref[:, cols] = dq_raw.astype(BF16)
            dgq_ref[...] += dg

        @pl.when(i == nq - 1)
        def _():
            for jk in range(kw // PAIR):
                kcols = pl.ds(jk * PAIR, PAIR)
                _, k_hat, k_rstd = _pair_norm(k_ref[:, kcols].astype(F32), gk_ref[...])
                dk_raw, dg = _pair_norm_bwd(dkn_ref[:, kcols], k_hat, k_rstd, gk_ref[...])
                dk_ref[:, kcols] = dk_raw.astype(BF16)
                dgk_ref[...] += dg
            dv_ref[...] = dvs_ref[...].astype(BF16)

    q_spec, k_spec, v_spec = _attn_specs(cfg, s, nq)
    row = pl.BlockSpec((Q_BLOCK, A_WIDTH), lambda b, i: (b * nq + i, 0))
    kv_out = pl.BlockSpec((s, kw), lambda b, i: (b, 0))
    pair_bias = _const_spec((N_HEADS // 2, wext, 2 * Q_BLOCK))
    res = pl.pallas_call(
        body, name=name, grid=(n_batch, nq),
        in_specs=[q_spec, k_spec, v_spec, pair_bias, _const_spec((N_HEADS // 2, 8, 2 * Q_BLOCK)),
                  _const_spec((1, PAIR)), _const_spec((1, PAIR)), row, row,
                  pl.BlockSpec((None, N_HEADS, Q_BLOCK), lambda b, i: (b * nq + i, 0, 0))] + [ANY] * len(plumb.args),
        out_specs=[row, kv_out, kv_out, pair_bias, _const_spec((N_HEADS, 128)),
                   _const_spec((1, PAIR)), _const_spec((1, PAIR))] + [ANY] * len(plumb.out_shape),
        out_shape=[_sds((t, A_WIDTH), BF16), _sds((t, kw), BF16), _sds((t, kw), BF16),
                   _sds((N_HEADS // 2, wext, 2 * Q_BLOCK), F32), _sds((N_HEADS, 128), F32),
                   _sds((1, PAIR), F32), _sds((1, PAIR), F32)] + plumb.out_shape,
        scratch_shapes=[pltpu.VMEM((s, kw), BF16), pltpu.VMEM((N_HEADS, nkc, PAIR, KEY_CHUNK), BF16),
                        pltpu.VMEM((s, kw), F32), pltpu.VMEM((s, kw), F32),
                        pltpu.VMEM((w, 2 * Q_BLOCK), F32), pltpu.VMEM((w, 2 * Q_BLOCK), F32),
                        pltpu.VMEM((w, 2 * Q_BLOCK), BF16), pltpu.VMEM((w, 2 * Q_BLOCK), BF16),
                        pltpu.VMEM((2 * w, Q_BLOCK), BF16)] + plumb.scratch,
        input_output_aliases=plumb.aliases,
        compiler_params=_params(),
    )(qkv, qkv, qkv, bias2, sink2, gq, gk, y, dy, lse, *plumb.args)
    return plumb.deliver(res)


def _band_tables(prev_chunks):
    prev = prev_chunks * CHUNK
    wext = 2 * prev + Q_BLOCK
    jj = np.arange(wext)[:, None]
    ii = np.arange(Q_BLOCK)[None, :]
    dist = prev + ii - jj
    rel_chunk = (prev // CHUNK + ii // CHUNK) - jj // CHUNK
    allowed = (rel_chunk >= 0) & (rel_chunk <= prev_chunks)
    return dist, allowed


def _alibi_slopes():
    return np.array([2.0 ** (-8.0 * (h + 1) / N_HEADS) for h in range(N_HEADS)], dtype=np.float32)


def _diag_onehot(prev, wext):
    n_diag = wext + Q_BLOCK - 1
    idx = np.clip(prev + Q_BLOCK - 1 - np.arange(n_diag), -A_MAX_REL, A_MAX_REL) + A_MAX_REL
    onehot = np.zeros((n_diag, 2 * A_MAX_REL + 1), np.float32)
    onehot[np.arange(n_diag), idx] = 1.0
    return onehot


def _bias_a(rel_bias):
    prev = A_PREV_CHUNKS * CHUNK
    _, allowed = _band_tables(A_PREV_CHUNKS)
    wext = allowed.shape[0]
    n_diag = wext + Q_BLOCK - 1
    seq = jnp.dot(rel_bias, jnp.asarray(_diag_onehot(prev, wext).T), precision=lax.Precision.HIGHEST)
    seq = jnp.pad(seq, ((0, 0), (0, 1)))
    rows = jnp.broadcast_to(seq[:, None, :], (N_HEADS, Q_BLOCK, n_diag + 1)).reshape(N_HEADS, -1)
    skew = rows[:, :Q_BLOCK * n_diag].reshape(N_HEADS, Q_BLOCK, n_diag)
    tile = jnp.transpose(skew[:, :, Q_BLOCK - 1:Q_BLOCK - 1 + wext], (0, 2, 1))
    return jnp.where(jnp.asarray(allowed)[None], tile, NEG_INF)


def _bias_b():
    dist, allowed = _band_tables(B_PREV_CHUNKS)
    bias = -_alibi_slopes()[:, None, None] * np.abs(dist).astype(np.float32)[None]
    return jnp.asarray(np.where(allowed[None], bias, np.float32(NEG_INF)).astype(np.float32))


def _rel_bias_grad(db_t):
    prev = A_PREV_CHUNKS * CHUNK
    wext = db_t.shape[1]
    n_diag = wext + Q_BLOCK - 1
    wp = n_diag + Q_BLOCK - 1
    xp = jnp.pad(jnp.transpose(db_t, (0, 2, 1)), ((0, 0), (0, 0), (Q_BLOCK - 1, Q_BLOCK - 1)))
    flat = jnp.pad(xp.reshape(N_HEADS, Q_BLOCK * wp), ((0, 0), (0, Q_BLOCK)))
    skew = flat.reshape(N_HEADS, Q_BLOCK, wp + 1)[:, :, :n_diag]
    diag = jnp.sum(skew, axis=1)
    return jnp.dot(diag, jnp.asarray(_diag_onehot(prev, wext)), precision=lax.Precision.HIGHEST)


def _ew(name, fn, ins, out_dtypes):
    r, c = ins[0].shape
    rb = _pick(r, max(16, (1 << 19) // c), 16)
    spec = pl.BlockSpec((rb, c), lambda i: (i, 0))

    def body(*refs):
        vals = fn(*[ref[...] for ref in refs[:len(ins)]])
        for ref, val in zip(refs[len(ins):], vals):
            ref[...] = val.astype(ref.dtype)

    return pl.pallas_call(
        body, name=name, grid=(r // rb,), in_specs=[spec] * len(ins), out_specs=[spec] * len(out_dtypes),
        out_shape=[_sds((r, c), dt) for dt in out_dtypes], compiler_params=_params(),
    )(*ins)


def _adamw_math(w, g, m, v):
    m = ADAM_B1 * m + (1.0 - ADAM_B1) * g
    v = ADAM_B2 * v + (1.0 - ADAM_B2) * (g * g)
    m_hat = m / (1.0 - ADAM_B1 ** ADAM_STEP)
    v_hat = v / (1.0 - ADAM_B2 ** ADAM_STEP)
    delta = -ADAM_LR * (m_hat / (jnp.sqrt(v_hat) + ADAM_EPS) + ADAM_WD * w)
    return delta, m, v


def _adamw_terms(name, terms, w, m, v):
    r, c = w.shape
    hr = r // 2
    rb = _pick(hr, max(16, (1 << 19) // c), 16)
    nb = hr // rb

    def body(t_ref, w_ref, m_ref, v_ref, g_ref, d_ref, nm_ref, nv_ref):
        g = t_ref[0].astype(F32)
        for k in range(1, N_CHIPS):
            g = g + t_ref[k].astype(F32)
        delta, nm, nv = _adamw_math(w_ref[...], g, m_ref[...], v_ref[...])
        g_ref[...] = g
        d_ref[...] = delta
        nm_ref[...] = nm
        nv_ref[...] = nv

    spec = pl.BlockSpec((rb, c), lambda h, i: (h * nb + i, 0))
    return pl.pallas_call(
        body, name=name, grid=(2, nb),
        in_specs=[pl.BlockSpec((None, N_CHIPS, rb, c), lambda h, i: (h, 0, i, 0)), spec, spec, spec],
        out_specs=[spec] * 4, out_shape=[_sds((r, c), F32)] * 4, compiler_params=_params(),
    )(terms, w, m, v)


def _mesh_place():
    x, y, c = lax.axis_index("x"), lax.axis_index("y"), lax.axis_index("c")
    chips = [(x, 1 - y), (1 - x, y), (1 - x, 1 - y)]
    return x, y, c, chips


def _all_gather_weights(shards):
    n = len(shards)

    def body(*refs):
        ins, outs = refs[:n], refs[n:2 * n]
        local_sem, ici_send, ici_recv, d2d_send, d2d_recv = refs[2 * n:]
        x, y, c, chips = _mesh_place()
        me = 2 * x + y
        sibling = (x, y, 1 - c)
        local, sent = [], []
        for wi in range(n):
            loc = pltpu.make_async_copy(ins[wi], outs[wi].at[me], local_sem.at[wi])
            loc.start()
            local.append(loc)
            for k, (tx, ty) in enumerate(chips):
                for pi, rows in _rotated_pieces(shards[wi].shape[1], k):
                    sem = (wi * 3 + k) * GATHER_PIECES + pi
                    cp = pltpu.make_async_remote_copy(
                        src_ref=ins[wi].at[c, rows], dst_ref=outs[wi].at[me, c, rows],
                        send_sem=ici_send.at[sem], recv_sem=ici_recv.at[sem],
                        device_id=(tx, ty, c), device_id_type=MESH)
                    cp.start()
                    sent.append(cp)
        passed = []
        for wi in range(n):
            for k, (tx, ty) in enumerate(chips):
                for pi, rows in _rotated_pieces(shards[wi].shape[1], k):
                    sem = (wi * 3 + k) * GATHER_PIECES + pi
                    slab = outs[wi].at[2 * tx + ty, c, rows]
                    pltpu.make_async_remote_copy(
                        src_ref=slab, dst_ref=slab, send_sem=ici_send.at[sem], recv_sem=ici_recv.at[sem],
                        device_id=(tx, ty, c), device_id_type=MESH).wait_recv()
                    fw = pltpu.make_async_remote_copy(
                        src_ref=slab, dst_ref=slab, send_sem=d2d_send.at[sem], recv_sem=d2d_recv.at[sem],
                        device_id=sibling, device_id_type=MESH)
                    fw.start()
                    passed.append(fw)
        for wi in range(n):
            for k, (tx, ty) in enumerate(chips):
                for pi, rows in enumerate(_row_pieces(shards[wi].shape[1])):
                    sem = (wi * 3 + k) * GATHER_PIECES + pi
                    slab = outs[wi].at[2 * tx + ty, 1 - c, rows]
                    pltpu.make_async_remote_copy(
                        src_ref=slab, dst_ref=slab, send_sem=d2d_send.at[sem], recv_sem=d2d_recv.at[sem],
                        device_id=sibling, device_id_type=MESH).wait_recv()
        for loc in local:
            loc.wait()
        for cp in sent + passed:
            cp.wait_send()

    return pl.pallas_call(
        body, name="all_gather_weights",
        in_specs=[ANY] * n, out_specs=[ANY] * n,
        out_shape=[_sds((N_CHIPS,) + s.shape, s.dtype) for s in shards],
        scratch_shapes=[pltpu.SemaphoreType.DMA((n,))] + [pltpu.SemaphoreType.DMA((3 * n * GATHER_PIECES,))] * 4,
    )(*shards)


def _run_comms(name, comms):
    plumb = _CommPlumbing(comms, 0, 0, 0)
    n_in, n_out = len(plumb.args), len(plumb.out_shape)

    def body(*refs):
        parts = []
        i0, o0, s0 = 0, n_in, n_in + n_out
        for cm in plumb.comms:
            parts.append((refs[i0:i0 + len(cm.ins)], refs[o0:o0 + len(cm.outs)], refs[s0:s0 + len(cm.sems)]))
            i0 += len(cm.ins)
            o0 += len(cm.outs)
            s0 += len(cm.sems)
        for cm, part in zip(plumb.comms, parts):
            cm.start(*part)
        for cm, part in zip(plumb.comms, parts):
            cm.finish(*part)

    res = pl.pallas_call(
        body, name=name, in_specs=[ANY] * n_in, out_specs=[ANY] * n_out, out_shape=plumb.out_shape,
        scratch_shapes=plumb.scratch, input_output_aliases=plumb.aliases,
    )(*plumb.args)
    plumb.deliver(res)


GATHER_PIECES = 4
BF16_TILE_ROWS = 16


def _row_pieces(rows):
    n = GATHER_PIECES
    while rows % (n * BF16_TILE_ROWS):
        n //= 2
    return [pl.ds(i * (rows // n), rows // n) for i in range(n)]


def _rotated_pieces(rows, k):
    pieces = list(enumerate(_row_pieces(rows)))
    k %= len(pieces)
    return pieces[k:] + pieces[:k]


def _gather_ici(shards):
    n = len(shards)

    def copies(ins, outs, sems):
        local_sem, send_sem, recv_sem = sems
        x, y, c, chips = _mesh_place()
        me = 2 * x + y
        local, sends, recvs = [], [], []
        for wi in range(n):
            local.append(pltpu.make_async_copy(ins[wi], outs[wi].at[me], local_sem.at[wi]))
            for k, (tx, ty) in enumerate(chips):
                for pi, rows in _rotated_pieces(shards[wi].shape[1], k):
                    sem = (wi * 3 + k) * GATHER_PIECES + pi
                    sems_k = dict(send_sem=send_sem.at[sem], recv_sem=recv_sem.at[sem],
                                  device_id=(tx, ty, c), device_id_type=MESH)
                    sends.append(pltpu.make_async_remote_copy(
                        src_ref=ins[wi].at[c, rows], dst_ref=outs[wi].at[me, c, rows], **sems_k))
                    slab = outs[wi].at[2 * tx + ty, c, rows]
                    recvs.append(pltpu.make_async_remote_copy(src_ref=slab, dst_ref=slab, **sems_k))
        return local, sends, recvs

    def start(ins, outs, sems):
        local, sends, _ = copies(ins, outs, sems)
        for cp in local + sends:
            cp.start()

    def finish(ins, outs, sems):
        local, sends, recvs = copies(ins, outs, sems)
        for cp in local:
            cp.wait()
        for cp in recvs:
            cp.wait_recv()
        for cp in sends:
            cp.wait_send()

    return _Comm(shards, [_sds((N_CHIPS,) + s.shape, s.dtype) for s in shards], {},
                 [pltpu.SemaphoreType.DMA((n,)), pltpu.SemaphoreType.DMA((3 * n * GATHER_PIECES,)),
                  pltpu.SemaphoreType.DMA((3 * n * GATHER_PIECES,))], start, finish)


def _gather_d2d(gathered):
    n = len(gathered)

    def copies(outs, sems):
        send_sem, recv_sem = sems
        x, y, c, chips = _mesh_place()
        sends, recvs = [], []
        for wi in range(n):
            for k, (tx, ty) in enumerate(chips):
                sems_k = dict(send_sem=send_sem.at[wi * 3 + k], recv_sem=recv_sem.at[wi * 3 + k],
                              device_id=(x, y, 1 - c), device_id_type=MESH)
                mine = outs[wi].at[2 * tx + ty, c]
                theirs = outs[wi].at[2 * tx + ty, 1 - c]
                sends.append(pltpu.make_async_remote_copy(src_ref=mine, dst_ref=mine, **sems_k))
                recvs.append(pltpu.make_async_remote_copy(src_ref=theirs, dst_ref=theirs, **sems_k))
        return sends, recvs

    def start(ins, outs, sems):
        for cp in copies(outs, sems)[0]:
            cp.start()

    def finish(ins, outs, sems):
        sends, recvs = copies(outs, sems)
        for cp in recvs:
            cp.wait_recv()
        for cp in sends:
            cp.wait_send()

    return _Comm(gathered, [_sds(g.shape, g.dtype) for g in gathered], {i: i for i in range(n)},
                 [pltpu.SemaphoreType.DMA((3 * n,)), pltpu.SemaphoreType.DMA((3 * n,))], start, finish)


def _exchange_halves(grads):
    n = len(grads)

    def copies(ins, outs, sems):
        send_sem, recv_sem = sems
        x, y, c, _ = _mesh_place()
        return [pltpu.make_async_remote_copy(
            src_ref=ins[wi].at[t, 1 - c], dst_ref=outs[wi].at[t],
            send_sem=send_sem.at[wi * N_CHIPS + t], recv_sem=recv_sem.at[wi * N_CHIPS + t],
            device_id=(x, y, 1 - c), device_id_type=MESH) for wi in range(n) for t in range(N_CHIPS)]

    def start(ins, outs, sems):
        for cp in copies(ins, outs, sems):
            cp.start()

    def finish(ins, outs, sems):
        for cp in copies(ins, outs, sems):
            cp.wait()

    return _Comm(grads, [_sds((N_CHIPS,) + g.shape[2:], g.dtype) for g in grads], {},
                 [pltpu.SemaphoreType.DMA((N_CHIPS * n,)), pltpu.SemaphoreType.DMA((N_CHIPS * n,))], start, finish)


def _scatter_ici(sums):
    n = len(sums)

    def copies(ins, outs, sems):
        local_sem, send_sem, recv_sem = sems
        x, y, c, chips = _mesh_place()
        me = 2 * x + y
        local, sends, recvs = [], [], []
        for wi in range(n):
            local.append(pltpu.make_async_copy(ins[wi].at[me], outs[wi].at[c, 0], local_sem.at[wi]))
            for k, (tx, ty) in enumerate(chips):
                sems_k = dict(send_sem=send_sem.at[wi * 3 + k], recv_sem=recv_sem.at[wi * 3 + k],
                              device_id=(tx, ty, c), device_id_type=MESH)
                land = outs[wi].at[c, k + 1]
                sends.append(pltpu.make_async_remote_copy(src_ref=ins[wi].at[2 * tx + ty], dst_ref=land, **sems_k))
                recvs.append(pltpu.make_async_remote_copy(src_ref=land, dst_ref=land, **sems_k))
        return local, sends, recvs

    def start(ins, outs, sems):
        local, sends, _ = copies(ins, outs, sems)
        for cp in local + sends:
            cp.start()

    def finish(ins, outs, sems):
        local, sends, recvs = copies(ins, outs, sems)
        for cp in local:
            cp.wait()
        for cp in recvs:
            cp.wait_recv()
        for cp in sends:
            cp.wait_send()

    return _Comm(sums, [_sds((2, N_CHIPS) + s.shape[1:], s.dtype) for s in sums], {},
                 [pltpu.SemaphoreType.DMA((n,)), pltpu.SemaphoreType.DMA((3 * n,)), pltpu.SemaphoreType.DMA((3 * n,))],
                 start, finish)


def _scatter_d2d(terms):
    n = len(terms)

    def copies(outs, sems):
        send_sem, recv_sem = sems
        x, y, c, _ = _mesh_place()
        sends, recvs = [], []
        for wi in range(n):
            sems_w = dict(send_sem=send_sem.at[wi], recv_sem=recv_sem.at[wi],
                          device_id=(x, y, 1 - c), device_id_type=MESH)
            sends.append(pltpu.make_async_remote_copy(src_ref=outs[wi].at[c], dst_ref=outs[wi].at[c], **sems_w))
            recvs.append(pltpu.make_async_remote_copy(src_ref=outs[wi].at[1 - c], dst_ref=outs[wi].at[1 - c], **sems_w))
        return sends, recvs

    def start(ins, outs, sems):
        for cp in copies(outs, sems)[0]:
            cp.start()

    def finish(ins, outs, sems):
        sends, recvs = copies(outs, sems)
        for cp in recvs:
            cp.wait_recv()
        for cp in sends:
            cp.wait_send()

    return _Comm(terms, [_sds(t.shape, t.dtype) for t in terms], {i: i for i in range(n)},
                 [pltpu.SemaphoreType.DMA((n,)), pltpu.SemaphoreType.DMA((n,))], start, finish)


def _chip_sum(name, grad, got, core):
    _, _, hr, c = grad.shape
    rb = _pick(hr, max(16, (1 << 19) // c), 16)

    def body(core_ref, a_ref, b_ref, o_ref):
        o_ref[...] = (a_ref[...].astype(F32) + b_ref[...].astype(F32)).astype(BF16)

    out_spec = pl.BlockSpec((None, rb, c), lambda t, i, core_ref: (t, i, 0))
    return pl.pallas_call(
        body, name=name,
        grid_spec=pltpu.PrefetchScalarGridSpec(
            num_scalar_prefetch=1, grid=(N_CHIPS, hr // rb),
            in_specs=[pl.BlockSpec((None, None, rb, c), lambda t, i, core_ref: (t, core_ref[0], i, 0)), out_spec],
            out_specs=out_spec),
        out_shape=_sds((N_CHIPS, hr, c), BF16), compiler_params=_params(),
    )(core, grad, got)


def _all_reduce_small(pack):
    r = pack.shape[0]

    def body(p_ref, o_ref, land_ref, send_sem, recv_sem):
        x, y, c, _ = _mesh_place()
        me = 4 * x + 2 * y + c
        flips = [(k >> 2 & 1, k >> 1 & 1, k & 1) for k in range(1, N_DEV)]

        def peer(fx, fy, fc):
            return (1 - x if fx else x, 1 - y if fy else y, 1 - c if fc else c)

        land_ref[me] = p_ref[...]
        sent = []
        for k, flip in enumerate(flips):
            cp = pltpu.make_async_remote_copy(
                src_ref=p_ref, dst_ref=land_ref.at[me], send_sem=send_sem.at[k], recv_sem=recv_sem.at[k],
                device_id=peer(*flip), device_id_type=MESH)
            cp.start()
            sent.append(cp)
        for k, flip in enumerate(flips):
            px, py, pc = peer(*flip)
            slot = land_ref.at[4 * px + 2 * py + pc]
            pltpu.make_async_remote_copy(
                src_ref=slot, dst_ref=slot, send_sem=send_sem.at[k], recv_sem=recv_sem.at[k],
                device_id=(px, py, pc), device_id_type=MESH).wait_recv()
        total = land_ref[0]
        for d in range(1, N_DEV):
            total = total + land_ref[d]
        o_ref[...] = total
        for cp in sent:
            cp.wait_send()

    vmem = pl.BlockSpec(memory_space=pltpu.VMEM)
    return pl.pallas_call(
        body, name="all_reduce_small", in_specs=[vmem], out_specs=vmem, out_shape=_sds((r, 128), F32),
        scratch_shapes=[pltpu.VMEM((N_DEV, r, 128), F32), pltpu.SemaphoreType.DMA((N_DEV - 1,)),
                        pltpu.SemaphoreType.DMA((N_DEV - 1,))],
    )(pack)


PACK_TILE = 8 * 128


def _pack(items):
    rows, i = [], 0
    while i < len(items):
        j = i
        while j < len(items) and items[j].size == items[i].size:
            j += 1
        group = jnp.stack([it.reshape(-1).astype(F32) for it in items[i:j]])
        rows.append(jnp.pad(group, ((0, 0), (0, -group.shape[1] % PACK_TILE))).reshape(-1, 128))
        i = j
    return jnp.concatenate(rows, axis=0)


def _unpack(pack, shapes):
    out, row = [], 0
    for shp in shapes:
        size = int(np.prod(shp))
        nrow = -(-size // PACK_TILE) * (PACK_TILE // 128)
        out.append(pack[row:row + nrow].reshape(-1)[:size].reshape(shp))
        row += nrow
    return out


BIG = ["ffn1_w_gu", "ffn1_w_down", "w_in", "w_gate", "w_proj_a", "w_proj_b", "w_out",
       "ffn2_w_gu", "ffn2_w_down", "w_ple_gate", "w_ple_proj"]
SMALL = ["ffn1_norm", "mix_norm", "ffn2_norm", "ple_norm", "a_q_norm", "a_k_norm", "b_q_norm", "b_k_norm",
         "a_rel_bias", "b_sinks"]
WEIGHTS = ["ffn1_norm", "ffn1_w_gu", "ffn1_w_down", "mix_norm", "w_in", "a_q_norm", "a_k_norm", "a_rel_bias",
           "b_q_norm", "b_k_norm", "b_sinks", "w_gate", "w_proj_a", "w_proj_b", "w_out", "ffn2_norm",
           "ffn2_w_gu", "ffn2_w_down", "ple_norm", "w_ple_gate", "w_ple_proj"]
ATTN_A = dict(prev=A_PREV_CHUNKS * CHUNK, group=1, kw=A_WIDTH, qblk=0, kblk=1, vblk=2)
ATTN_B = dict(prev=B_PREV_CHUNKS * CHUNK, group=N_HEADS // B_KV_HEADS, kw=B_KV_WIDTH, qblk=3,
              kblk=4 * A_WIDTH // B_KV_WIDTH, vblk=4 * A_WIDTH // B_KV_WIDTH + 1)


def _cast_epilogue(accs, extras, outs, ij):
    for acc, out in zip(accs, outs):
        out[...] = acc.astype(out.dtype)


GATHER_FIRST = ["ffn1_w_gu", "ffn1_w_down"]
ROW_SHARDED = ("ffn1_w_down", "ffn2_w_down", "w_out", "w_ple_gate")


def _slotted(name, grad):
    if name == "w_in":
        rows, cols = grad.shape
        grad = jnp.transpose(grad.reshape(rows, N_CHIPS, cols // N_CHIPS), (1, 0, 2))
    elif name in ROW_SHARDED:
        grad = grad.reshape(N_CHIPS, grad.shape[0] // N_CHIPS, grad.shape[1])
    return grad.reshape(N_CHIPS, 2, grad.shape[1] // 2, grad.shape[2])


def _local_step(xt, pt, tgt, n_batch, shards, small, core):
    t, d = xt.shape
    tm = _pick(t, 512, 8)
    tk = _pick(t, 512, 8)
    nt = t // tm
    row = pl.BlockSpec((tm, d), lambda i, j, k: (i, 0))
    gs = shards["w_gate"].shape[1]
    ps = shards["w_proj_a"].shape[1]
    es = shards["w_ple_proj"].shape[1]
    pdim = pt.shape[1]
    ncols = N_CHIPS * shards["w_in"].shape[1]
    tin = ncols // 2
    assert 2 * gs == d and 4 * ps == d and 4 * es == d and tin % 128 == 0

    w = {}
    halves = {n: s.reshape(2, s.shape[0] // 2, s.shape[1]) for n, s in shards.items()}

    def publish(names, arrays):
        for name, g in zip(names, arrays):
            g = g.reshape(N_CHIPS, 2 * g.shape[2], g.shape[3])
            if name in ROW_SHARDED:
                g = g.reshape(N_CHIPS * g.shape[1], g.shape[2])
            elif name == "w_in":
                g = jnp.transpose(g, (1, 0, 2)).reshape(g.shape[1], N_CHIPS * g.shape[2])
            w[name] = g

    class GatherPipe:
        def __init__(self, names):
            self.names = names

        def ici(self):
            self.first = _gather_ici([halves[n] for n in self.names])
            return self.first

        def d2d(self):
            self.second = _gather_d2d(self.first.results)
            return self.second

        def publish(self):
            publish(self.names, self.second.results)

    class GradPipe:
        def __init__(self, names):
            self.names = names

        def exchange(self, grads):
            self.grads = [_slotted(n, g) for n, g in zip(self.names, grads)]
            self.x = _exchange_halves(self.grads)
            return self.x

        def scatter(self):
            sums = [_chip_sum("chip_sum_" + n, g, got, core)
                    for n, g, got in zip(self.names, self.grads, self.x.results)]
            self.s = _scatter_ici(sums)
            return self.s

        def forward(self):
            self.f = _scatter_d2d(self.s.results)
            return self.f

        def terms(self):
            return dict(zip(self.names, self.f.results))

    publish(GATHER_FIRST, _all_gather_weights([halves[n] for n in GATHER_FIRST]))
    g_in, g_proj, g_ple = GatherPipe(["w_in", "w_gate"]), GatherPipe(["w_proj_a", "w_proj_b", "w_out"]), \
        GatherPipe(["w_ple_gate", "w_ple_proj"])
    g_down2, g_up2 = GatherPipe(["ffn2_w_down"]), GatherPipe(["ffn2_w_gu"])
    h1, ffn1_saved = _ffn_fwd("ffn1", xt, small["ffn1_norm"], w["ffn1_w_gu"], w["ffn1_w_down"],
                              {"up": lambda: [g_in.ici()], "down": lambda: [g_in.d2d(), g_proj.ici()]})
    g_in.publish()
    w_in, wgate = w["w_in"], w["w_gate"]
    un = _rms_fwd("mix_norm", h1, small["mix_norm"])
    (qkv,) = _mm(
        "qkv", "nn", (nt, 2, 1),
        [(un, row, w_in, pl.BlockSpec((d, tin), lambda i, j, k: (0, j)))], [],
        [(_sds((t, ncols), BF16), pl.BlockSpec((tm, tin), lambda i, j, k: (i, j)))], (tm, tin), _cast_epilogue,
        j_outer=True, comms=[g_proj.d2d(), g_ple.ici()])
    g_proj.publish()
    wpa, wpb, wout = w["w_proj_a"], w["w_proj_b"], w["w_out"]

    def gate_epilogue(accs, extras, outs, ij):
        outs[0][...] = jax.nn.sigmoid(accs[0]).astype(BF16)

    (gates,) = _mm(
        "gate", "nn", (nt, 4, 1),
        [(un, row, wgate, pl.BlockSpec((None, d, gs), lambda i, j, k: (j, 0, 0)))], [],
        [(_sds((2, t, d), BF16), pl.BlockSpec((None, tm, gs), lambda i, j, k: (j // 2, i, j % 2)))],
        (tm, gs), gate_epilogue, j_outer=True, chunked=True, comms=[g_ple.d2d(), g_down2.ici()])
    g_ple.publish()
    wpg, wpe = w["w_ple_gate"], w["w_ple_proj"]

    bias_a = _pair_bias(_bias_a(small["a_rel_bias"][0]))
    bias_b = _pair_bias(_bias_b())
    sink_a = _pair_rows(jnp.full((N_HEADS, 128), NEG_INF, F32))
    sink_b = _pair_rows(jnp.broadcast_to(small["b_sinks"][0][:, None], (N_HEADS, 128)))
    gqa, gka, gqb, gkb = [jnp.tile(small[k], (1, 2)) for k in ("a_q_norm", "a_k_norm", "b_q_norm", "b_k_norm")]
    ya, lse_a = _attn_fwd("attn_a_fwd", qkv, bias_a, sink_a, gqa, gka, ATTN_A, n_batch,
                          comms=[g_down2.d2d(), g_up2.ici()])
    g_down2.publish()
    yb, lse_b = _attn_fwd("attn_b_fwd", qkv, bias_b, sink_b, gqb, gkb, ATTN_B, n_batch, comms=[g_up2.d2d()])
    g_up2.publish()

    def merge_epilogue(accs, extras, outs, ij):
        pa, pb = accs
        outs[0][...] = (extras[0][...].astype(F32) * pa + extras[1][...].astype(F32) * pb).astype(BF16)
        outs[1][...] = pa.astype(BF16)
        outs[2][...] = pb.astype(BF16)

    y_spec = pl.BlockSpec((tm, A_WIDTH), lambda i, j, k: (i, 0))
    proj_spec = pl.BlockSpec((None, A_WIDTH, ps), lambda i, j, k: (j, 0, 0))
    tile_ps = pl.BlockSpec((tm, ps), lambda i, j, k: (i, j))
    merged, pa, pb = _mm(
        "proj_merge", "nn", (nt, 4, 1),
        [(ya, y_spec, wpa, proj_spec), (yb, y_spec, wpb, proj_spec)],
        [(gates, pl.BlockSpec((None, tm, ps), lambda i, j, k: (0, i, j))),
         (gates, pl.BlockSpec((None, tm, ps), lambda i, j, k: (1, i, j)))],
        [(_sds((t, d), BF16), tile_ps)] * 3, (tm, ps), merge_epilogue)

    def residual_epilogue(accs, extras, outs, ij):
        outs[0][...] = extras[0][...] + accs[0]

    (h2,) = _mm(
        "out_proj", "nn", (nt, 1, 1),
        [(merged, row, wout, pl.BlockSpec((d, d), lambda i, j, k: (0, 0)))],
        [(h1, row)], [(_sds((t, d), F32), row)], (tm, d), residual_epilogue)

    h3, ffn2_saved = _ffn_fwd("ffn2", h2, small["ffn2_norm"], w["ffn2_w_gu"], w["ffn2_w_down"], {})
    n3 = _rms_fwd("ple_norm", h3, small["ple_norm"])
    tile_es = pl.BlockSpec((tm, es), lambda i, j, k: (i, j))
    (pe,) = _mm(
        "ple_embed", "nn", (nt, 4, 1),
        [(pt, pl.BlockSpec((tm, pdim), lambda i, j, k: (i, 0)), wpe, pl.BlockSpec((None, pdim, es), lambda i, j, k: (j, 0, 0)))],
        [], [(_sds((t, d), F32), tile_es)], (tm, es), _cast_epilogue)

    th = _pick(d, 512)

    def head_epilogue(accs, extras, outs, ij):
        h3_ref, pe_ref, tgt_ref = extras
        dy_ref, dpe_ref, dz_ref, loss_ref = outs
        pg = jax.nn.sigmoid(accs[0])
        pev = pe_ref[...]
        diff = h3_ref[...] + pg * pev - tgt_ref[...]
        dy = diff * (1.0 / d)
        dy_ref[...] = dy
        dpe_ref[...] = (dy * pg).astype(BF16)
        dz_ref[...] = (dy * pev * pg * (1.0 - pg)).astype(BF16)
        _accumulate(loss_ref, jnp.full(loss_ref.shape, jnp.sum(diff * diff), F32), (ij[0] == 0) & (ij[1] == 0))

    tile_h = pl.BlockSpec((tm, th), lambda i, j, k: (i, j))
    dy, dpe, dz, loss_acc = _mm(
        "ple_gate_loss", "nn", (nt, d // th, 1),
        [(n3, row, wpg, pl.BlockSpec((d, th), lambda i, j, k: (0, j)))],
        [(h3, tile_h), (pe, tile_h), (tgt, tile_h)],
        [(_sds((t, d), F32), tile_h), (_sds((t, d), BF16), tile_h), (_sds((t, d), BF16), tile_h),
         (_sds((8, 128), F32), pl.BlockSpec((8, 128), lambda i, j, k: (0, 0)))],
        (tm, th), head_epilogue, j_outer=True, chunked=True)
    loss = 0.5 * loss_acc[0, 0] / d

    nk = t // tk
    (dwpe,) = _mm(
        "d_w_ple_proj", "tn", (1, 4, nk),
        [(pt, pl.BlockSpec((tk, pdim), lambda i, j, k: (k, 0)), dpe, pl.BlockSpec((tk, es), lambda i, j, k: (k, j)))],
        [], [(_sds((4, pdim, es), BF16), pl.BlockSpec((None, pdim, es), lambda i, j, k: (j, 0, 0)))],
        (pdim, es), _cast_epilogue)

    def dense_grad(name, a, dyb, comms=()):
        (res,) = _mm(
            name, "tn", (1, d // th, nk),
            [(a, pl.BlockSpec((tk, d), lambda i, j, k: (k, 0)), dyb, pl.BlockSpec((tk, th), lambda i, j, k: (k, j)))],
            [], [(_sds((d, d), BF16), pl.BlockSpec((d, th), lambda i, j, k: (0, j)))], (d, th), _cast_epilogue,
            comms=comms)
        return res

    dwpg = dense_grad("d_w_ple_gate", n3, dz)
    tmn = _pick(t, 1024, 8)
    extras, outs = _rms_bwd_io(h3, small["ple_norm"], dy, tmn)
    dh3, dh3_b, d_ple_norm = _mm(
        "d_ple_norm", "nt", (t // tmn, 1, 1),
        [(dz, pl.BlockSpec((tmn, d), lambda i, j, k: (i, 0)), wpg, pl.BlockSpec((d, d), lambda i, j, k: (0, 0)))],
        extras, outs, (tmn, d), _rms_bwd_epilogue)

    up2, down2, ple = GradPipe(["ffn2_w_gu"]), GradPipe(["ffn2_w_down"]), GradPipe(["w_ple_gate", "w_ple_proj"])
    proj = GradPipe(["w_proj_a", "w_proj_b", "w_out"])
    dh2, dh2_b, d_ffn2_norm, dwgu2, dwd2 = _ffn_bwd(
        "ffn2", dh3, dh3_b, h2, small["ffn2_norm"], w["ffn2_w_gu"], w["ffn2_w_down"], ffn2_saved,
        {"dnorm": lambda dwgu, dwd: [up2.exchange([dwgu]), down2.exchange([dwd]), ple.exchange([dwpg, dwpe])]})

    def dmerge_epilogue(accs, extras, outs, ij):
        dmo = accs[0]
        g_ref, pa_ref, pb_ref = extras
        dg_ref, dpa_ref, dpb_ref = outs
        ga = g_ref[0].astype(F32)
        gb = g_ref[1].astype(F32)
        dg_ref[0] = (dmo * pa_ref[...].astype(F32) * ga * (1.0 - ga)).astype(BF16)
        dg_ref[1] = (dmo * pb_ref[...].astype(F32) * gb * (1.0 - gb)).astype(BF16)
        dpa_ref[...] = (dmo * ga).astype(BF16)
        dpb_ref[...] = (dmo * gb).astype(BF16)

    g_spec = pl.BlockSpec((2, tm, th), lambda i, j, k: (0, i, j))
    dgates, dpa, dpb = _mm(
        "d_merge", "nt", (nt, d // th, 1),
        [(dh2_b, row, wout, pl.BlockSpec((th, d), lambda i, j, k: (j, 0)))],
        [(gates, g_spec), (pa, tile_h), (pb, tile_h)],
        [(_sds((2, t, d), BF16), g_spec), (_sds((t, d), BF16), tile_h), (_sds((t, d), BF16), tile_h)],
        (tm, th), dmerge_epilogue, j_outer=True, chunked=True, comms=[down2.scatter()])
    dwout = dense_grad("d_w_out", merged, dh2_b, comms=[down2.forward(), ple.scatter()])

    yk_spec = pl.BlockSpec((tk, A_WIDTH), lambda i, j, k: (k, 0))
    dk_spec = pl.BlockSpec((tk, ps), lambda i, j, k: (k, j))
    dproj = (_sds((4, A_WIDTH, ps), BF16), proj_spec)
    dwpa, dwpb = _mm(
        "d_w_proj", "tn", (1, 4, nk),
        [(ya, yk_spec, dpa, dk_spec), (yb, yk_spec, dpb, dk_spec)], [], [dproj, dproj], (A_WIDTH, ps), _cast_epilogue,
        comms=[ple.forward()])
    dproj_a = pl.BlockSpec((tm, ps), lambda i, j, k: (i, k))
    wproj_k = pl.BlockSpec((None, A_WIDTH, ps), lambda i, j, k: (k, 0, 0))
    dya, dyb = _mm(
        "d_attn_out", "nt", (nt, 1, 4),
        [(dpa, dproj_a, wpa, wproj_k), (dpb, dproj_a, wpb, wproj_k)], [],
        [(_sds((t, A_WIDTH), BF16), y_spec)] * 2, (tm, A_WIDTH), _cast_epilogue,
        comms=[proj.exchange([dwpa, dwpb, dwout])])

    dqa, dka, dva, dbias_a, _, dgqa, dgka = _attn_bwd(
        "attn_a_bwd", qkv, bias_a, sink_a, gqa, gka, ya, dya, lse_a, ATTN_A, n_batch, True,
        comms=[up2.scatter(), proj.scatter()])
    dqb, dkb, dvb, _, dsink_b, dgqb, dgkb = _attn_bwd(
        "attn_b_bwd", qkv, bias_b, sink_b, gqb, gkb, yb, dyb, lse_b, ATTN_B, n_batch, False,
        comms=[up2.forward(), proj.forward()])
    dqkv = jnp.concatenate([dqa, dka, dva, dqb, dkb, dvb], axis=1)

    (dwgate,) = _mm(
        "d_w_gate", "tn", (1, 4, nk),
        [(un, pl.BlockSpec((tk, d), lambda i, j, k: (k, 0)),
          dgates, pl.BlockSpec((None, tk, gs), lambda i, j, k: (j // 2, k, j % 2)))],
        [], [(_sds((4, d, gs), BF16), pl.BlockSpec((None, d, gs), lambda i, j, k: (j, 0, 0)))], (d, gs), _cast_epilogue)
    (dwin,) = _mm(
        "d_w_in", "tn", (1, 2, nk),
        [(un, pl.BlockSpec((tk, d), lambda i, j, k: (k, 0)), dqkv, pl.BlockSpec((tk, tin), lambda i, j, k: (k, j)))],
        [], [(_sds((d, ncols), BF16), pl.BlockSpec((d, tin), lambda i, j, k: (0, j)))], (d, tin), _cast_epilogue)

    mixer = GradPipe(["w_in", "w_gate"])
    extras, outs = _rms_bwd_io(h1, small["mix_norm"], dh2, tmn)
    dh1, dh1_b, d_mix_norm = _mm(
        "d_mix_norm", "nt", (t // tmn, 1, 6),
        [(dgates, pl.BlockSpec((None, tmn, gs), lambda i, j, k: (jnp.minimum(k, 3) // 2, i, jnp.minimum(k, 3) % 2)),
          wgate, pl.BlockSpec((None, d, gs), lambda i, j, k: (jnp.minimum(k, 3), 0, 0))),
         (dqkv, pl.BlockSpec((tmn, tin), lambda i, j, k: (i, jnp.maximum(k - 4, 0))),
          w_in, pl.BlockSpec((d, tin), lambda i, j, k: (0, jnp.maximum(k - 4, 0))))],
        extras, outs, (tmn, d), _rms_bwd_epilogue, steps=[4, 2],
        comms=[mixer.exchange([dwin, dwgate])])

    up1 = GradPipe(["ffn1_w_gu"])
    down1 = GradPipe(["ffn1_w_down"])
    dx, _, d_ffn1_norm, _, _ = _ffn_bwd(
        "ffn1", dh1, dh1_b, xt, small["ffn1_norm"], w["ffn1_w_gu"], w["ffn1_w_down"], ffn1_saved,
        {"dact": lambda: [mixer.scatter()],
         "dwgu": lambda: [mixer.forward()],
         "dwd": lambda dwgu: [up1.exchange([dwgu])],
         "dnorm": lambda dwgu, dwd: [up1.scatter(), down1.exchange([dwd])]})
    _run_comms("grad_tail_scatter", [up1.forward(), down1.scatter()])
    _run_comms("grad_tail_forward", [down1.forward()])
    terms = {}
    for pipe in (up2, down2, ple, proj, mixer, up1, down1):
        terms.update(pipe.terms())

    def fold(v):
        return v[0, :HEAD_DIM] + v[0, HEAD_DIM:]

    small_grads = {"ffn1_norm": d_ffn1_norm, "mix_norm": d_mix_norm, "ffn2_norm": d_ffn2_norm,
                   "ple_norm": d_ple_norm, "a_q_norm": fold(dgqa), "a_k_norm": fold(dgka),
                   "b_q_norm": fold(dgqb), "b_k_norm": fold(dgkb), "a_rel_bias": _rel_bias_grad(_unpair_bias(dbias_a)),
                   "b_sinks": jnp.sum(dsink_b, axis=1)}
    return loss, dx, terms, small_grads


def kernel(x, p, ffn1_norm, ffn1_w_gu, ffn1_w_down, mix_norm, w_in, a_q_norm, a_k_norm, a_rel_bias, b_q_norm, b_k_norm, b_sinks, w_gate, w_proj_a, w_proj_b, w_out, ffn2_norm, ffn2_w_gu, ffn2_w_down, ple_norm, w_ple_gate, w_ple_proj, loss_target, m_ffn1_norm, m_ffn1_w_gu, m_ffn1_w_down, m_mix_norm, m_w_in, m_a_q_norm, m_a_k_norm, m_a_rel_bias, m_b_q_norm, m_b_k_norm, m_b_sinks, m_w_gate, m_w_proj_a, m_w_proj_b, m_w_out, m_ffn2_norm, m_ffn2_w_gu, m_ffn2_w_down, m_ple_norm, m_w_ple_gate, m_w_ple_proj, v_ffn1_norm, v_ffn1_w_gu, v_ffn1_w_down, v_mix_norm, v_w_in, v_a_q_norm, v_a_k_norm, v_a_rel_bias, v_b_q_norm, v_b_k_norm, v_b_sinks, v_w_gate, v_w_proj_a, v_w_proj_b, v_w_out, v_ffn2_norm, v_ffn2_w_gu, v_ffn2_w_down, v_ple_norm, v_w_ple_gate, v_w_ple_proj):
    given = dict(locals())
    n_batch, s, d = x.shape
    t = n_batch * s
    xt = x.reshape(t, d)
    pt = p.reshape(t, p.shape[-1])
    tgt = loss_target.reshape(t, d)

    shards = {}
    for name in BIG:
        (shards[name],) = _ew("cast_" + name, lambda v: (v,), [given[name][0]], [BF16])
    small = {name: given[name] for name in SMALL}
    core = lax.axis_index("c").astype(jnp.int32).reshape(1)
    loss, dx, terms, small_grads = _local_step(xt, pt, tgt, n_batch, shards, small, core)

    grads, deltas, new_m, new_v = {}, {}, {}, {}
    for name in BIG:
        gw, dl, nm, nv = _adamw_terms("adamw_" + name, terms[name], given[name][0], given["m_" + name][0],
                                      given["v_" + name][0])
        grads[name], deltas[name], new_m[name], new_v[name] = gw[None], dl[None], nm[None], nv[None]

    small_shapes = [given[name].shape for name in SMALL] + [()]
    g_pack = _all_reduce_small(_pack([small_grads[name] for name in SMALL] + [loss]))
    zero = jnp.zeros((), F32)
    w_pack = _pack([given[name] for name in SMALL] + [zero])
    m_pack = _pack([given["m_" + name] for name in SMALL] + [zero])
    v_pack = _pack([given["v_" + name] for name in SMALL] + [zero])
    d_pack, nm_pack, nv_pack = _ew("adamw_small", lambda wv, gv, mv, vv: _adamw_math(wv, gv, mv, vv),
                                   [w_pack, g_pack, m_pack, v_pack], [F32] * 3)
    g_small = _unpack(g_pack, small_shapes)
    loss_total = g_small[-1]
    for name, gv, dv, mv, vv in zip(SMALL, g_small, _unpack(d_pack, small_shapes), _unpack(nm_pack, small_shapes),
                                    _unpack(nv_pack, small_shapes)):
        grads[name], deltas[name], new_m[name], new_v[name] = gv, dv, mv, vv

    return (loss_total, dx.reshape(x.shape), *[grads[n] for n in WEIGHTS], *[deltas[n] for n in WEIGHTS],
            *[new_m[n] for n in WEIGHTS], *[new_v[n] for n in WEIGHTS])
```

```python
import functools

import numpy as np
import jax
import jax.numpy as jnp
from jax import lax
from jax.experimental import pallas as pl
from jax.experimental.pallas import tpu as pltpu

F32 = jnp.float32
BF16 = jnp.bfloat16

CHUNK = 64
HEAD_DIM = 64
A_PREV_CHUNKS = 8
A_MAX_REL = 128
N_HEADS = 8
B_KV_HEADS = 2
B_PREV_CHUNKS = 2
A_WIDTH = N_HEADS * HEAD_DIM
B_KV_WIDTH = B_KV_HEADS * HEAD_DIM
EPS = 1e-6
NEG_INF = -1e30
ATTN_SCALE = HEAD_DIM ** -0.5
Q_BLOCK = 128
PAIR = 2 * HEAD_DIM

ADAM_LR = 0.001
ADAM_B1 = 0.9
ADAM_B2 = 0.999
ADAM_EPS = 1e-08
ADAM_WD = 0.01
ADAM_STEP = 10

N_CHIPS = 4
N_DEV = 8
VMEM_LIMIT_V7X = 56 * 1024 * 1024
MESH = pl.DeviceIdType.MESH
ANY = pl.BlockSpec(memory_space=pl.ANY)

_DN = {
    "nn": (((1,), (0,)), ((), ())),
    "nt": (((1,), (1,)), ((), ())),
    "tn": (((0,), (0,)), ((), ())),
}


def _pick(n, target, mult=128):
    best = None
    for d in range(mult, min(n, target) + 1, mult):
        if n % d == 0:
            best = d
    return n if best is None else best


def _dot(a, b, mode):
    return lax.dot_general(a.astype(BF16), b.astype(BF16), _DN[mode], preferred_element_type=F32)


def _params():
    return pltpu.CompilerParams(vmem_limit_bytes=VMEM_LIMIT_V7X)


class _Comm:
    def __init__(self, ins, outs, aliases, sems, start, finish):
        self.ins, self.outs, self.aliases, self.sems = list(ins), list(outs), dict(aliases), list(sems)
        self.start, self.finish = start, finish
        self.results = None


class _CommPlumbing:
    def __init__(self, comms, n_in, n_out, n_scratch):
        self.comms = list(comms)
        self.n_in, self.n_out, self.n_scratch = n_in, n_out, n_scratch
        self.args = [a for cm in self.comms for a in cm.ins]
        self.out_shape = [o for cm in self.comms for o in cm.outs]
        self.scratch = [s for cm in self.comms for s in cm.sems]
        self.aliases = {}
        i0, o0 = n_in, n_out
        for cm in self.comms:
            for a, b in cm.aliases.items():
                self.aliases[i0 + a] = o0 + b
            i0 += len(cm.ins)
            o0 += len(cm.outs)

    def run(self, in_refs, out_refs, scratch_refs, first, last):
        if not self.comms:
            return
        parts = []
        i0, o0, s0 = self.n_in, self.n_out, self.n_scratch
        for cm in self.comms:
            parts.append((in_refs[i0:i0 + len(cm.ins)], out_refs[o0:o0 + len(cm.outs)],
                          scratch_refs[s0:s0 + len(cm.sems)]))
            i0 += len(cm.ins)
            o0 += len(cm.outs)
            s0 += len(cm.sems)

        @pl.when(first)
        def _():
            for cm, part in zip(self.comms, parts):
                cm.start(*part)

        @pl.when(last)
        def _():
            for cm, part in zip(self.comms, parts):
                cm.finish(*part)

    def deliver(self, results):
        o0 = self.n_out
        for cm in self.comms:
            cm.results = list(results[o0:o0 + len(cm.outs)])
            o0 += len(cm.outs)
        return list(results[:self.n_out])


def _swap_ij(spec):
    index_map = spec.index_map
    return pl.BlockSpec(spec.block_shape, lambda j, i, k: index_map(i, j, k))


MXU_COLUMNS_V7X = 256


def _mm(name, mode, grid, pairs, extras, outs, acc_shape, epilogue, steps=None, comms=(), j_outer=False,
        chunked=False):
    ni, nj, nk = grid
    slots = [pair[4] if len(pair) > 4 else None for pair in pairs]
    pairs = [pair[:4] for pair in pairs]
    n_in = 2 * len(pairs) + len(extras)
    n_out = len(outs)
    tn = acc_shape[1]
    col_chunks = None
    if chunked:
        assert nk == 1 and steps is None and mode in ("nn", "nt")
        col_chunks = [(c0, min(MXU_COLUMNS_V7X, tn - c0)) for c0 in range(0, tn, MXU_COLUMNS_V7X)]
    n_acc = 0 if chunked else (len(pairs) if steps is None else 1)
    plumb = _CommPlumbing(comms, n_in, n_out, n_acc)
    n_all_in = n_in + len(plumb.args)
    n_all_out = n_out + len(plumb.out_shape)
    if j_outer:
        grid = (nj, ni, nk)
        pairs = [(a, _swap_ij(a_spec), b, _swap_ij(b_spec)) for a, a_spec, b, b_spec in pairs]
        extras = [(e, _swap_ij(e_spec)) for e, e_spec in extras]
        outs = [(o, _swap_ij(o_spec)) for o, o_spec in outs]

    def body(*refs):
        in_refs = refs[:n_all_in]
        out_refs = refs[n_all_in:n_all_in + n_all_out]
        scratch = refs[n_all_in + n_all_out:]
        accs = scratch[:n_acc]
        i = pl.program_id(1 if j_outer else 0)
        j = pl.program_id(0 if j_outer else 1)
        k = pl.program_id(2)

        def contrib(p, acc):
            b_ref = in_refs[2 * p + 1]
            rhs = b_ref[...] if slots[p] is None else b_ref[slots[p](i, j, k)]
            acc[...] += _dot(in_refs[2 * p][...], rhs, mode)

        if col_chunks:
            def cols(ref, c0, cs):
                if ref.shape[-1] != tn:
                    return ref
                return ref.at[(slice(None),) * (len(ref.shape) - 1) + (pl.ds(c0, cs),)]

            lhs = [in_refs[2 * p][...] for p in range(len(pairs))]
            for ci, (c0, cs) in enumerate(col_chunks):
                vals = []
                for p in range(len(pairs)):
                    b_ref = in_refs[2 * p + 1]
                    rhs = b_ref[:, c0:c0 + cs] if mode == "nn" else b_ref[c0:c0 + cs, :]
                    vals.append(_dot(lhs[p], rhs, mode))
                epilogue(vals, [cols(r, c0, cs) for r in in_refs[2 * len(pairs):n_in]],
                         [cols(r, c0, cs) for r in out_refs[:n_out]], (i, j * len(col_chunks) + ci))
        else:
            @pl.when(k == 0)
            def _():
                for acc in accs:
                    acc[...] = jnp.zeros(acc.shape, F32)

            if steps is None:
                for p in range(len(pairs)):
                    contrib(p, accs[p])
            else:
                lo = 0
                for p, n in enumerate(steps):
                    pl.when((k >= lo) & (k < lo + n))(functools.partial(contrib, p, accs[0]))
                    lo += n

            @pl.when(k == nk - 1)
            def _():
                epilogue([acc[...] for acc in accs], in_refs[2 * len(pairs):n_in], out_refs[:n_out], (i, j))

        plumb.run(in_refs, out_refs, scratch, (i == 0) & (j == 0) & (k == 0),
                  (i == ni - 1) & (j == nj - 1) & (k == nk - 1))

    args, in_specs = [], []
    for a, a_spec, b, b_spec in pairs:
        args += [a, b]
        in_specs += [a_spec, b_spec]
    for e, e_spec in extras:
        args.append(e)
        in_specs.append(e_spec)
    res = pl.pallas_call(
        body,
        name=name,
        grid=grid,
        in_specs=in_specs + [ANY] * len(plumb.args),
        out_specs=[s for _, s in outs] + [ANY] * len(plumb.out_shape),
        out_shape=[o for o, _ in outs] + plumb.out_shape,
        scratch_shapes=[pltpu.VMEM(acc_shape, F32) for _ in range(n_acc)] + plumb.scratch,
        input_output_aliases=plumb.aliases,
        compiler_params=_params(),
    )(*args, *plumb.args)
    return plumb.deliver(res)


def _sds(shape, dtype):
    return jax.ShapeDtypeStruct(shape, dtype)


def _accumulate(ref, value, first):
    @pl.when(first)
    def _():
        ref[...] = value

    @pl.when(jnp.logical_not(first))
    def _():
        ref[...] += value


def _rms_fwd(name, x, gain):
    t, d = x.shape
    tm = _pick(t, 512, 8)

    def body(x_ref, g_ref, y_ref):
        xv = x_ref[...]
        rstd = lax.rsqrt(jnp.mean(xv * xv, axis=-1, keepdims=True) + EPS)
        y_ref[...] = (xv * rstd * g_ref[...]).astype(BF16)

    return pl.pallas_call(
        body, name=name, grid=(t // tm,),
        in_specs=[pl.BlockSpec((tm, d), lambda i: (i, 0)), pl.BlockSpec((1, d), lambda i: (0, 0))],
        out_specs=pl.BlockSpec((tm, d), lambda i: (i, 0)),
        out_shape=_sds((t, d), BF16),
        compiler_params=_params(),
    )(x, gain)


def _rms_bwd_epilogue(accs, extras, outs, ij):
    x_ref, g_ref, r_ref = extras
    dh_ref, dhb_ref, dg_ref = outs
    dn = accs[0]
    xv = x_ref[...]
    rstd = lax.rsqrt(jnp.mean(xv * xv, axis=-1, keepdims=True) + EPS)
    xhat = xv * rstd
    gd = dn * g_ref[...]
    dx = rstd * (gd - xhat * jnp.mean(gd * xhat, axis=-1, keepdims=True))
    dh = r_ref[...] + dx
    dh_ref[...] = dh
    dhb_ref[...] = dh.astype(BF16)
    _accumulate(dg_ref, jnp.sum(dn * xhat, axis=0, keepdims=True), ij[0] == 0)


def _rms_bwd_io(x, gain, dres, tm):
    t, d = x.shape
    row = pl.BlockSpec((tm, d), lambda i, j, k: (i, 0))
    extras = [(x, row), (gain, pl.BlockSpec((1, d), lambda i, j, k: (0, 0))), (dres, row)]
    outs = [(_sds((t, d), F32), row), (_sds((t, d), BF16), row),
            (_sds((1, d), F32), pl.BlockSpec((1, d), lambda i, j, k: (0, 0)))]
    return extras, outs


def _ffn_fwd(tag, h, gain, wgu, wd, hooks):
    t, d = h.shape
    fs = wgu.shape[2]
    f = 2 * fs
    tm = _pick(t, 512, 8)
    n = _rms_fwd(tag + "_norm", h, gain)

    def up_epilogue(accs, extras, outs, ij):
        g, u = accs
        gu_ref, a_ref = outs
        gu_ref[0] = g.astype(BF16)
        gu_ref[1] = u.astype(BF16)
        a_ref[...] = (g * jax.nn.sigmoid(g) * u).astype(BF16)

    a_spec = pl.BlockSpec((tm, d), lambda i, j, k: (i, 0))
    gu, a = _mm(
        tag + "_up", "nn", (t // tm, 2, 1),
        [(n, a_spec, wgu, pl.BlockSpec((None, d, fs), lambda i, j, k: (j, 0, 0))),
         (n, a_spec, wgu, pl.BlockSpec((None, d, fs), lambda i, j, k: (j + 2, 0, 0)))],
        [],
        [(_sds((2, t, f), BF16), pl.BlockSpec((2, tm, fs), lambda i, j, k: (0, i, j))),
         (_sds((t, f), BF16), pl.BlockSpec((tm, fs), lambda i, j, k: (i, j)))],
        (tm, fs), up_epilogue, comms=hooks.get("up", lambda: ())(), j_outer=True, chunked=True)

    def down_epilogue(accs, extras, outs, ij):
        outs[0][...] = extras[0][...] + 0.5 * accs[0]


    row = pl.BlockSpec((tm, d), lambda i, j, k: (i, 0))
    (h_new,) = _mm(
        tag + "_down", "nn", (t // tm, 1, 1),
        [(a, pl.BlockSpec((tm, f), lambda i, j, k: (i, 0)), wd, pl.BlockSpec((f, d), lambda i, j, k: (0, 0)))],
        [(h, row)], [(_sds((t, d), F32), row)], (tm, d), down_epilogue, comms=hooks.get("down", lambda: ())())
    return h_new, (n, gu, a)


def _ffn_bwd(tag, dh, dh_b, h, gain, wgu, wd, saved, hooks):
    n, gu, a = saved
    t, d = h.shape
    fs = wgu.shape[2]
    f = 2 * fs
    tm = _pick(t, 512, 8)
    tk = _pick(t, 512, 8)

    def dact_epilogue(accs, extras, outs, ij):
        da = 0.5 * accs[0]
        g = extras[0][0].astype(F32)
        u = extras[0][1].astype(F32)
        sg = jax.nn.sigmoid(g)
        outs[0][0] = (da * u * sg * (1.0 + g * (1.0 - sg))).astype(BF16)
        outs[0][1] = (da * g * sg).astype(BF16)

    gu_spec = pl.BlockSpec((2, tm, fs), lambda i, j, k: (0, i, j))
    (dgu,) = _mm(
        tag + "_dact", "nt", (t // tm, 2, 1),
        [(dh_b, pl.BlockSpec((tm, d), lambda i, j, k: (i, 0)), wd, pl.BlockSpec((fs, d), lambda i, j, k: (j, 0)))],
        [(gu, gu_spec)], [(_sds((2, t, f), BF16), gu_spec)], (tm, fs), dact_epilogue, j_outer=True, chunked=True,
        comms=hooks.get("dact", lambda: ())())

    def cast_epilogue(accs, extras, outs, ij):
        outs[0][...] = accs[0].astype(BF16)

    (dwgu,) = _mm(
        tag + "_dwgu", "tn", (1, 4, t // tk),
        [(n, pl.BlockSpec((tk, d), lambda i, j, k: (k, 0)),
          dgu, pl.BlockSpec((None, tk, fs), lambda i, j, k: (j // 2, k, j % 2)))],
        [], [(_sds((4, d, fs), BF16), pl.BlockSpec((None, d, fs), lambda i, j, k: (j, 0, 0)))], (d, fs), cast_epilogue,
        comms=hooks.get("dwgu", lambda: ())())

    def half_epilogue(accs, extras, outs, ij):
        outs[0][...] = (0.5 * accs[0]).astype(BF16)

    (dwd,) = _mm(
        tag + "_dwd", "tn", (2, 1, t // tk),
        [(a, pl.BlockSpec((tk, fs), lambda i, j, k: (k, i)), dh_b, pl.BlockSpec((tk, d), lambda i, j, k: (k, 0)))],
        [], [(_sds((f, d), BF16), pl.BlockSpec((fs, d), lambda i, j, k: (i, 0)))], (fs, d), half_epilogue,
        comms=hooks.get("dwd", lambda g: ())(dwgu))

    tmn = _pick(t, 1024, 8)
    extras, outs = _rms_bwd_io(h, gain, dh, tmn)
    dh_in, dh_in_b, dgain = _mm(
        tag + "_dnorm", "nt", (t // tmn, 1, 4),
        [(dgu, pl.BlockSpec((None, tmn, fs), lambda i, j, k: (k // 2, i, k % 2)),
          wgu, pl.BlockSpec((None, d, fs), lambda i, j, k: (k, 0, 0)))],
        extras, outs, (tmn, d), _rms_bwd_epilogue, comms=hooks.get("dnorm", lambda g, w: ())(dwgu, dwd))
    return dh_in, dh_in_b, dgain, dwgu, dwd


def _lane_lo(shape):
    return lax.broadcasted_iota(jnp.int32, shape, 1) < HEAD_DIM


def _pair_norm(xv, gain):
    lo = _lane_lo(xv.shape)
    x2 = xv * xv
    ms_lo = jnp.sum(jnp.where(lo, x2, 0.0), axis=-1, keepdims=True) * (1.0 / HEAD_DIM)
    ms_hi = jnp.sum(jnp.where(lo, 0.0, x2), axis=-1, keepdims=True) * (1.0 / HEAD_DIM)
    rstd = jnp.where(lo, lax.rsqrt(ms_lo + EPS), lax.rsqrt(ms_hi + EPS))
    xhat = xv * rstd
    return xhat * gain, xhat, rstd


def _pair_norm_bwd(dn, xhat, rstd, gain):
    lo = _lane_lo(dn.shape)
    gd = dn * gain
    t = gd * xhat
    m_lo = jnp.sum(jnp.where(lo, t, 0.0), axis=-1, keepdims=True) * (1.0 / HEAD_DIM)
    m_hi = jnp.sum(jnp.where(lo, 0.0, t), axis=-1, keepdims=True) * (1.0 / HEAD_DIM)
    dx = rstd * (gd - xhat * jnp.where(lo, m_lo, m_hi))
    return dx, jnp.sum(dn * xhat, axis=0, keepdims=True)


def _half(xv, hi):
    lo = _lane_lo(xv.shape)
    return jnp.where(lo, 0, xv) if hi else jnp.where(lo, xv, 0)


def _attn_window(i, prev):
    q0 = i * Q_BLOCK
    start = jnp.maximum(q0 - prev, 0)
    off = start - (q0 - prev)
    return pl.multiple_of(start, Q_BLOCK), pl.multiple_of(off, Q_BLOCK)


def _attn_specs(cfg, s, nq):
    kw = cfg["kw"]
    q_spec = pl.BlockSpec((Q_BLOCK, A_WIDTH), lambda b, i: (b * nq + i, cfg["qblk"]))
    k_spec = pl.BlockSpec((s, kw), lambda b, i: (b, cfg["kblk"]))
    v_spec = pl.BlockSpec((s, kw), lambda b, i: (b, cfg["vblk"]))
    return q_spec, k_spec, v_spec


def _const_spec(shape):
    return pl.BlockSpec(shape, lambda b, i: (0,) * len(shape))


KEY_CHUNK = 128


def _pair_bias(bias_t):
    wext = bias_t.shape[1]
    return jnp.transpose(bias_t.reshape(N_HEADS // 2, 2, wext, Q_BLOCK), (0, 2, 1, 3)).reshape(
        N_HEADS // 2, wext, 2 * Q_BLOCK)


def _unpair_bias(db2):
    wext = db2.shape[1]
    return jnp.transpose(db2.reshape(N_HEADS // 2, wext, 2, Q_BLOCK), (0, 2, 1, 3)).reshape(N_HEADS, wext, Q_BLOCK)


def _pair_rows(rows):
    two = rows.reshape(N_HEADS // 2, 2 * rows.shape[1])
    return jnp.broadcast_to(two[:, None, :], (N_HEADS // 2, 8, two.shape[1]))


def _sub_lo(shape):
    return lax.broadcasted_iota(jnp.int32, shape, 0) < HEAD_DIM


def _by_half(lo_row, hi_row, rows):
    return jnp.where(_sub_lo((rows, lo_row.shape[1])), lo_row, hi_row)


def _stack_pair(xn, jq, group):
    parts = []
    for hq in range(2):
        hk = ((2 * jq + hq) // group) % 2
        xm = _half(xn, hq)
        if hq != hk:
            xm = pltpu.roll(xm, HEAD_DIM, 1)
        parts.append(xm)
    return jnp.concatenate(parts, axis=0).astype(BF16)


def _place_transposed(blk, dst_ref, c, heads, group):
    bt = blk.T
    lo = _sub_lo(bt.shape)
    for h in heads:
        src_hi = ((h // group) % 2) == 1
        part = jnp.where(lo, 0.0, bt) if src_hi else jnp.where(lo, bt, 0.0)
        if src_hi != (h % 2 == 1):
            part = pltpu.roll(part, HEAD_DIM, 0)
        dst_ref[h, c] = part.astype(BF16)


def _attn_fwd(name, qkv, bias2, sink2, gq, gk, cfg, n_batch, comms=()):
    t = qkv.shape[0]
    s = t // n_batch
    nq = s // Q_BLOCK
    nkc = s // KEY_CHUNK
    prev, group, kw = cfg["prev"], cfg["group"], cfg["kw"]
    n_chunks = (prev + Q_BLOCK) // KEY_CHUNK
    wext = bias2.shape[1]
    plumb = _CommPlumbing(comms, 7, 2, 2)
    n_all_in = 7 + len(plumb.args)
    n_all_out = 2 + len(plumb.out_shape)

    def body(*refs):
        q_ref, k_ref, v_ref, bias_ref, sink_ref, gq_ref, gk_ref = refs[:7]
        y_ref, lse_ref = refs[n_all_in:n_all_in + 2]
        kn_ref, vt_ref = refs[n_all_in + n_all_out:n_all_in + n_all_out + 2]
        i = pl.program_id(1)
        plumb.run(refs[:n_all_in], refs[n_all_in:n_all_in + n_all_out], refs[n_all_in + n_all_out:],
                  (pl.program_id(0) == 0) & (i == 0), (pl.program_id(0) == n_batch - 1) & (i == nq - 1))

        @pl.when(i == 0)
        def _():
            for jk in range(kw // PAIR):
                cols = pl.ds(jk * PAIR, PAIR)
                heads = [h for h in range(N_HEADS) if (h // group) // 2 == jk]
                kn, _, _ = _pair_norm(k_ref[:, cols].astype(F32), gk_ref[...])
                kn_ref[:, cols] = kn.astype(BF16)
                for c in range(nkc):
                    _place_transposed(v_ref[pl.ds(c * KEY_CHUNK, KEY_CHUNK), cols].astype(F32), vt_ref, c, heads, group)

        start, off = _attn_window(i, prev)
        c0 = start // KEY_CHUNK
        sub8 = lax.broadcasted_iota(jnp.int32, (N_HEADS, Q_BLOCK), 0)
        lse = jnp.zeros((N_HEADS, Q_BLOCK), F32)
        for jq in range(N_HEADS // 2):
            kcols = pl.ds((((2 * jq) // group) // 2) * PAIR, PAIR)
            qn, _, _ = _pair_norm(q_ref[:, pl.ds(jq * PAIR, PAIR)].astype(F32), gq_ref[...])
            qs = _stack_pair(qn * ATTN_SCALE, jq, group)
            m = sink_ref[jq, 0:1, :]
            l = jnp.ones((1, 2 * Q_BLOCK), F32)
            ot = jnp.zeros((PAIR, Q_BLOCK), F32)
            for c in range(n_chunks):
                rows = pl.ds(start + c * KEY_CHUNK, KEY_CHUNK)
                s2 = _dot(kn_ref[rows, kcols], qs, "nt") + bias_ref[jq, pl.ds(off + c * KEY_CHUNK, KEY_CHUNK), :]
                m_new = jnp.maximum(m, jnp.max(s2, axis=0, keepdims=True))
                alpha = jnp.exp(m - m_new)
                p = jnp.exp(s2 - m_new)
                l = alpha * l + jnp.sum(p, axis=0, keepdims=True)
                m = m_new
                pst = jnp.concatenate([p[:, :Q_BLOCK], p[:, Q_BLOCK:]], axis=0)
                vl = jnp.concatenate([vt_ref[2 * jq, c0 + c], vt_ref[2 * jq + 1, c0 + c]], axis=1)
                ot = ot * _by_half(alpha[:, :Q_BLOCK], alpha[:, Q_BLOCK:], PAIR) + _dot(vl, pst, "nn")
            inv = 1.0 / l
            ot = ot * _by_half(inv[:, :Q_BLOCK], inv[:, Q_BLOCK:], PAIR)
            y_ref[:, pl.ds(jq * PAIR, PAIR)] = ot.T.astype(BF16)
            lse2 = m + jnp.log(l)
            lse = jnp.where(sub8 == 2 * jq, lse2[:, :Q_BLOCK], lse)
            lse = jnp.where(sub8 == 2 * jq + 1, lse2[:, Q_BLOCK:], lse)
        lse_ref[...] = lse

    q_spec, k_spec, v_spec = _attn_specs(cfg, s, nq)
    res = pl.pallas_call(
        body, name=name, grid=(n_batch, nq),
        in_specs=[q_spec, k_spec, v_spec, _const_spec((N_HEADS // 2, wext, 2 * Q_BLOCK)),
                  _const_spec((N_HEADS // 2, 8, 2 * Q_BLOCK)), _const_spec((1, PAIR)), _const_spec((1, PAIR))]
        + [ANY] * len(plumb.args),
        out_specs=[pl.BlockSpec((Q_BLOCK, A_WIDTH), lambda b, i: (b * nq + i, 0)),
                   pl.BlockSpec((None, N_HEADS, Q_BLOCK), lambda b, i: (b * nq + i, 0, 0))]
        + [ANY] * len(plumb.out_shape),
        out_shape=[_sds((t, A_WIDTH), BF16), _sds((t // Q_BLOCK, N_HEADS, Q_BLOCK), F32)] + plumb.out_shape,
        scratch_shapes=[pltpu.VMEM((s, kw), BF16), pltpu.VMEM((N_HEADS, nkc, PAIR, KEY_CHUNK), BF16)] + plumb.scratch,
        input_output_aliases=plumb.aliases,
        compiler_params=_params(),
    )(qkv, qkv, qkv, bias2, sink2, gq, gk, *plumb.args)
    return plumb.deliver(res)


def _attn_bwd(name, qkv, bias2, sink2, gq, gk, y, dy, lse, cfg, n_batch, want_dbias, comms=()):
    t = qkv.shape[0]
    s = t // n_batch
    nq = s // Q_BLOCK
    nkc = s // KEY_CHUNK
    prev, group, kw = cfg["prev"], cfg["group"], cfg["kw"]
    w = prev + Q_BLOCK
    n_chunks = w // KEY_CHUNK
    wext = bias2.shape[1]
    plumb = _CommPlumbing(comms, 10, 7, 9)
    n_all_in = 10 + len(plumb.args)
    n_all_out = 7 + len(plumb.out_shape)

    def body(*refs):
        q_ref, k_ref, v_ref, bias_ref, sink_ref, gq_ref, gk_ref, y_ref, dy_ref, lse_ref = refs[:10]
        dq_ref, dk_ref, dv_ref, db_ref, dsink_ref, dgq_ref, dgk_ref = refs[n_all_in:n_all_in + 7]
        kn_ref, knt_ref, dkn_ref, dvs_ref, s_ref, dp_ref, pb_ref, dsb_ref, dst_ref = \
            refs[n_all_in + n_all_out:n_all_in + n_all_out + 9]
        b = pl.program_id(0)
        i = pl.program_id(1)
        first = (b == 0) & (i == 0)
        plumb.run(refs[:n_all_in], refs[n_all_in:n_all_in + n_all_out], refs[n_all_in + n_all_out:],
                  first, (b == n_batch - 1) & (i == nq - 1))

        @pl.when(i == 0)
        def _():
            for jk in range(kw // PAIR):
                cols = pl.ds(jk * PAIR, PAIR)
                heads = [h for h in range(N_HEADS) if (h // group) // 2 == jk]
                for c in range(nkc):
                    rows = pl.ds(c * KEY_CHUNK, KEY_CHUNK)
                    kn, _, _ = _pair_norm(k_ref[rows, cols].astype(F32), gk_ref[...])
                    kn_ref[rows, cols] = kn.astype(BF16)
                    _place_transposed(kn, knt_ref, c, heads, group)
            dkn_ref[...] = jnp.zeros(dkn_ref.shape, F32)
            dvs_ref[...] = jnp.zeros(dvs_ref.shape, F32)

        @pl.when(first)
        def _():
            db_ref[...] = jnp.zeros(db_ref.shape, F32)
            dsink_ref[...] = jnp.zeros(dsink_ref.shape, F32)
            dgq_ref[...] = jnp.zeros(dgq_ref.shape, F32)
            dgk_ref[...] = jnp.zeros(dgk_ref.shape, F32)

        start, off = _attn_window(i, prev)
        c0 = start // KEY_CHUNK
        for jq in range(N_HEADS // 2):
            cols = pl.ds(jq * PAIR, PAIR)
            kcols = pl.ds((((2 * jq) // group) // 2) * PAIR, PAIR)
            qn, q_hat, q_rstd = _pair_norm(q_ref[:, cols].astype(F32), gq_ref[...])
            qs = _stack_pair(qn * ATTN_SCALE, jq, group)
            do_pair = dy_ref[:, cols].astype(F32)
            dos = _stack_pair(do_pair, jq, group)
            prod_t = (do_pair * y_ref[:, cols].astype(F32)).T
            lo = _sub_lo(prod_t.shape)
            delta2 = jnp.concatenate([jnp.sum(jnp.where(lo, prod_t, 0.0), axis=0, keepdims=True),
                                      jnp.sum(jnp.where(lo, 0.0, prod_t), axis=0, keepdims=True)], axis=1)
            lse2 = jnp.concatenate([lse_ref[2 * jq:2 * jq + 1, :], lse_ref[2 * jq + 1:2 * jq + 2, :]], axis=1)
            dsk = -jnp.exp(sink_ref[jq, 0:1, :] - lse2) * delta2
            dsink_ref[2 * jq:2 * jq + 1, :] += dsk[:, :Q_BLOCK]
            dsink_ref[2 * jq + 1:2 * jq + 2, :] += dsk[:, Q_BLOCK:]
            rows_w = pl.ds(start, w)
            s_ref[...] = _dot(kn_ref[rows_w, kcols], qs, "nt")
            dp_ref[...] = _dot(v_ref[rows_w, kcols], dos, "nt")
            for c in range(n_chunks):
                r = pl.ds(c * KEY_CHUNK, KEY_CHUNK)
                brows = pl.ds(off + c * KEY_CHUNK, KEY_CHUNK)
                p = jnp.exp(s_ref[r, :] + bias_ref[jq, brows, :] - lse2)
                ds = p * (dp_ref[r, :] - delta2)
                if want_dbias:
                    db_ref[jq, brows, :] += ds
                ds_b = ds.astype(BF16)
                pb_ref[r, :] = p.astype(BF16)
                dsb_ref[r, :] = ds_b
                dst_ref[pl.ds(2 * c * KEY_CHUNK, KEY_CHUNK), :] = ds_b[:, :Q_BLOCK]
                dst_ref[pl.ds((2 * c + 1) * KEY_CHUNK, KEY_CHUNK), :] = ds_b[:, Q_BLOCK:]
            dkn_ref[rows_w, kcols] += _dot(dsb_ref[...], qs, "nn")
            dvs_ref[rows_w, kcols] += _dot(pb_ref[...], dos, "nn")
            kl = jnp.concatenate([knt_ref[2 * jq + hq, c0 + c] for c in range(n_chunks) for hq in range(2)], axis=1)
            dqt = _dot(kl, dst_ref[...], "nn")
            dq_raw, dg = _pair_norm_bwd(dqt.T * ATTN_SCALE, q_hat, q_rstd, gq_ref[...])
            dq_ref[:, cols] = dq_raw.astype(BF16)
            dgq_ref[...] += dg

        @pl.when(i == nq - 1)
        def _():
            for jk in range(kw // PAIR):
                kcols = pl.ds(jk * PAIR, PAIR)
                _, k_hat, k_rstd = _pair_norm(k_ref[:, kcols].astype(F32), gk_ref[...])
                dk_raw, dg = _pair_norm_bwd(dkn_ref[:, kcols], k_hat, k_rstd, gk_ref[...])
                dk_ref[:, kcols] = dk_raw.astype(BF16)
                dgk_ref[...] += dg
            dv_ref[...] = dvs_ref[...].astype(BF16)

    q_spec, k_spec, v_spec = _attn_specs(cfg, s, nq)
    row = pl.BlockSpec((Q_BLOCK, A_WIDTH), lambda b, i: (b * nq + i, 0))
    kv_out = pl.BlockSpec((s, kw), lambda b, i: (b, 0))
    pair_bias = _const_spec((N_HEADS // 2, wext, 2 * Q_BLOCK))
    res = pl.pallas_call(
        body, name=name, grid=(n_batch, nq),
        in_specs=[q_spec, k_spec, v_spec, pair_bias, _const_spec((N_HEADS // 2, 8, 2 * Q_BLOCK)),
                  _const_spec((1, PAIR)), _const_spec((1, PAIR)), row, row,
                  pl.BlockSpec((None, N_HEADS, Q_BLOCK), lambda b, i: (b * nq + i, 0, 0))] + [ANY] * len(plumb.args),
        out_specs=[row, kv_out, kv_out, pair_bias, _const_spec((N_HEADS, 128)),
                   _const_spec((1, PAIR)), _const_spec((1, PAIR))] + [ANY] * len(plumb.out_shape),
        out_shape=[_sds((t, A_WIDTH), BF16), _sds((t, kw), BF16), _sds((t, kw), BF16),
                   _sds((N_HEADS // 2, wext, 2 * Q_BLOCK), F32), _sds((N_HEADS, 128), F32),
                   _sds((1, PAIR), F32), _sds((1, PAIR), F32)] + plumb.out_shape,
        scratch_shapes=[pltpu.VMEM((s, kw), BF16), pltpu.VMEM((N_HEADS, nkc, PAIR, KEY_CHUNK), BF16),
                        pltpu.VMEM((s, kw), F32), pltpu.VMEM((s, kw), F32),
                        pltpu.VMEM((w, 2 * Q_BLOCK), F32), pltpu.VMEM((w, 2 * Q_BLOCK), F32),
                        pltpu.VMEM((w, 2 * Q_BLOCK), BF16), pltpu.VMEM((w, 2 * Q_BLOCK), BF16),
                        pltpu.VMEM((2 * w, Q_BLOCK), BF16)] + plumb.scratch,
        input_output_aliases=plumb.aliases,
        compiler_params=_params(),
    )(qkv, qkv, qkv, bias2, sink2, gq, gk, y, dy, lse, *plumb.args)
    return plumb.deliver(res)


def _band_tables(prev_chunks):
    prev = prev_chunks * CHUNK
    wext = 2 * prev + Q_BLOCK
    jj = np.arange(wext)[:, None]
    ii = np.arange(Q_BLOCK)[None, :]
    dist = prev + ii - jj
    rel_chunk = (prev // CHUNK + ii // CHUNK) - jj // CHUNK
    allowed = (rel_chunk >= 0) & (rel_chunk <= prev_chunks)
    return dist, allowed


def _alibi_slopes():
    return np.array([2.0 ** (-8.0 * (h + 1) / N_HEADS) for h in range(N_HEADS)], dtype=np.float32)


def _diag_onehot(prev, wext):
    n_diag = wext + Q_BLOCK - 1
    idx = np.clip(prev + Q_BLOCK - 1 - np.arange(n_diag), -A_MAX_REL, A_MAX_REL) + A_MAX_REL
    onehot = np.zeros((n_diag, 2 * A_MAX_REL + 1), np.float32)
    onehot[np.arange(n_diag), idx] = 1.0
    return onehot


def _bias_a(rel_bias):
    prev = A_PREV_CHUNKS * CHUNK
    _, allowed = _band_tables(A_PREV_CHUNKS)
    wext = allowed.shape[0]
    n_diag = wext + Q_BLOCK - 1
    seq = jnp.dot(rel_bias, jnp.asarray(_diag_onehot(prev, wext).T), precision=lax.Precision.HIGHEST)
    seq = jnp.pad(seq, ((0, 0), (0, 1)))
    rows = jnp.broadcast_to(seq[:, None, :], (N_HEADS, Q_BLOCK, n_diag + 1)).reshape(N_HEADS, -1)
    skew = rows[:, :Q_BLOCK * n_diag].reshape(N_HEADS, Q_BLOCK, n_diag)
    tile = jnp.transpose(skew[:, :, Q_BLOCK - 1:Q_BLOCK - 1 + wext], (0, 2, 1))
    return jnp.where(jnp.asarray(allowed)[None], tile, NEG_INF)


def _bias_b():
    dist, allowed = _band_tables(B_PREV_CHUNKS)
    bias = -_alibi_slopes()[:, None, None] * np.abs(dist).astype(np.float32)[None]
    return jnp.asarray(np.where(allowed[None], bias, np.float32(NEG_INF)).astype(np.float32))


def _rel_bias_grad(db_t):
    prev = A_PREV_CHUNKS * CHUNK
    wext = db_t.shape[1]
    n_diag = wext + Q_BLOCK - 1
    wp = n_diag + Q_BLOCK - 1
    xp = jnp.pad(jnp.transpose(db_t, (0, 2, 1)), ((0, 0), (0, 0), (Q_BLOCK - 1, Q_BLOCK - 1)))
    flat = jnp.pad(xp.reshape(N_HEADS, Q_BLOCK * wp), ((0, 0), (0, Q_BLOCK)))
    skew = flat.reshape(N_HEADS, Q_BLOCK, wp + 1)[:, :, :n_diag]
    diag = jnp.sum(skew, axis=1)
    return jnp.dot(diag, jnp.asarray(_diag_onehot(prev, wext)), precision=lax.Precision.HIGHEST)


def _ew(name, fn, ins, out_dtypes):
    r, c = ins[0].shape
    rb = _pick(r, max(16, (1 << 19) // c), 16)
    spec = pl.BlockSpec((rb, c), lambda i: (i, 0))

    def body(*refs):
        vals = fn(*[ref[...] for ref in refs[:len(ins)]])
        for ref, val in zip(refs[len(ins):], vals):
            ref[...] = val.astype(ref.dtype)

    return pl.pallas_call(
        body, name=name, grid=(r // rb,), in_specs=[spec] * len(ins), out_specs=[spec] * len(out_dtypes),
        out_shape=[_sds((r, c), dt) for dt in out_dtypes], compiler_params=_params(),
    )(*ins)


def _adamw_math(w, g, m, v):
    m = ADAM_B1 * m + (1.0 - ADAM_B1) * g
    v = ADAM_B2 * v + (1.0 - ADAM_B2) * (g * g)
    m_hat = m / (1.0 - ADAM_B1 ** ADAM_STEP)
    v_hat = v / (1.0 - ADAM_B2 ** ADAM_STEP)
    delta = -ADAM_LR * (m_hat / (jnp.sqrt(v_hat) + ADAM_EPS) + ADAM_WD * w)
    return delta, m, v


def _adamw_terms(name, terms, w, m, v):
    r, c = w.shape
    hr = r // 2
    rb = _pick(hr, max(16, (1 << 19) // c), 16)
    nb = hr // rb

    def body(t_ref, w_ref, m_ref, v_ref, g_ref, d_ref, nm_ref, nv_ref):
        g = t_ref[0].astype(F32)
        for k in range(1, N_CHIPS):
            g = g + t_ref[k].astype(F32)
        delta, nm, nv = _adamw_math(w_ref[...], g, m_ref[...], v_ref[...])
        g_ref[...] = g
        d_ref[...] = delta
        nm_ref[...] = nm
        nv_ref[...] = nv

    spec = pl.BlockSpec((rb, c), lambda h, i: (h * nb + i, 0))
    return pl.pallas_call(
        body, name=name, grid=(2, nb),
        in_specs=[pl.BlockSpec((None, N_CHIPS, rb, c), lambda h, i: (h, 0, i, 0)), spec, spec, spec],
        out_specs=[spec] * 4, out_shape=[_sds((r, c), F32)] * 4, compiler_params=_params(),
    )(terms, w, m, v)


def _mesh_place():
    x, y, c = lax.axis_index("x"), lax.axis_index("y"), lax.axis_index("c")
    chips = [(x, 1 - y), (1 - x, y), (1 - x, 1 - y)]
    return x, y, c, chips


def _all_gather_weights(shards):
    n = len(shards)

    def body(*refs):
        ins, outs = refs[:n], refs[n:2 * n]
        local_sem, ici_send, ici_recv, d2d_send, d2d_recv = refs[2 * n:]
        x, y, c, chips = _mesh_place()
        me = 2 * x + y
        sibling = (x, y, 1 - c)
        local, sent = [], []
        for wi in range(n):
            loc = pltpu.make_async_copy(ins[wi], outs[wi].at[me], local_sem.at[wi])
            loc.start()
            local.append(loc)
            for k, (tx, ty) in enumerate(chips):
                for pi, rows in _rotated_pieces(shards[wi].shape[1], k):
                    sem = (wi * 3 + k) * GATHER_PIECES + pi
                    cp = pltpu.make_async_remote_copy(
                        src_ref=ins[wi].at[c, rows], dst_ref=outs[wi].at[me, c, rows],
                        send_sem=ici_send.at[sem], recv_sem=ici_recv.at[sem],
                        device_id=(tx, ty, c), device_id_type=MESH)
                    cp.start()
                    sent.append(cp)
        passed = []
        for wi in range(n):
            for k, (tx, ty) in enumerate(chips):
                for pi, rows in _rotated_pieces(shards[wi].shape[1], k):
                    sem = (wi * 3 + k) * GATHER_PIECES + pi
                    slab = outs[wi].at[2 * tx + ty, c, rows]
                    pltpu.make_async_remote_copy(
                        src_ref=slab, dst_ref=slab, send_sem=ici_send.at[sem], recv_sem=ici_recv.at[sem],
                        device_id=(tx, ty, c), device_id_type=MESH).wait_recv()
                    fw = pltpu.make_async_remote_copy(
                        src_ref=slab, dst_ref=slab, send_sem=d2d_send.at[sem], recv_sem=d2d_recv.at[sem],
                        device_id=sibling, device_id_type=MESH)
                    fw.start()
                    passed.append(fw)
        for wi in range(n):
            for k, (tx, ty) in enumerate(chips):
                for pi, rows in enumerate(_row_pieces(shards[wi].shape[1])):
                    sem = (wi * 3 + k) * GATHER_PIECES + pi
                    slab = outs[wi].at[2 * tx + ty, 1 - c, rows]
                    pltpu.make_async_remote_copy(
                        src_ref=slab, dst_ref=slab, send_sem=d2d_send.at[sem], recv_sem=d2d_recv.at[sem],
                        device_id=sibling, device_id_type=MESH).wait_recv()
        for loc in local:
            loc.wait()
        for cp in sent + passed:
            cp.wait_send()

    return pl.pallas_call(
        body, name="all_gather_weights",
        in_specs=[ANY] * n, out_specs=[ANY] * n,
        out_shape=[_sds((N_CHIPS,) + s.shape, s.dtype) for s in shards],
        scratch_shapes=[pltpu.SemaphoreType.DMA((n,))] + [pltpu.SemaphoreType.DMA((3 * n * GATHER_PIECES,))] * 4,
    )(*shards)


def _run_comms(name, comms):
    plumb = _CommPlumbing(comms, 0, 0, 0)
    n_in, n_out = len(plumb.args), len(plumb.out_shape)

    def body(*refs):
        parts = []
        i0, o0, s0 = 0, n_in, n_in + n_out
        for cm in plumb.comms:
            parts.append((refs[i0:i0 + len(cm.ins)], refs[o0:o0 + len(cm.outs)], refs[s0:s0 + len(cm.sems)]))
            i0 += len(cm.ins)
            o0 += len(cm.outs)
            s0 += len(cm.sems)
        for cm, part in zip(plumb.comms, parts):
            cm.start(*part)
        for cm, part in zip(plumb.comms, parts):
            cm.finish(*part)

    res = pl.pallas_call(
        body, name=name, in_specs=[ANY] * n_in, out_specs=[ANY] * n_out, out_shape=plumb.out_shape,
        scratch_shapes=plumb.scratch, input_output_aliases=plumb.aliases,
    )(*plumb.args)
    plumb.deliver(res)


GATHER_PIECES = 4
BF16_TILE_ROWS = 16


def _row_pieces(rows):
    n = GATHER_PIECES
    while rows % (n * BF16_TILE_ROWS):
        n //= 2
    return [pl.ds(i * (rows // n), rows // n) for i in range(n)]


def _rotated_pieces(rows, k):
    pieces = list(enumerate(_row_pieces(rows)))
    k %= len(pieces)
    return pieces[k:] + pieces[:k]


def _gather_ici(shards, targets=(0, 1, 2), into=None):
    n = len(shards)

    def copies(ins, outs, sems):
        local_sem, send_sem, recv_sem = sems
        x, y, c, chips = _mesh_place()
        me = 2 * x + y
        local, sends, recvs = [], [], []
        for wi in range(n):
            if into is None:
                local.append(pltpu.make_async_copy(ins[wi], outs[wi].at[me], local_sem.at[wi]))
            for k in targets:
                tx, ty = chips[k]
                for pi, rows in _rotated_pieces(shards[wi].shape[1], k):
                    sem = (wi * 3 + k) * GATHER_PIECES + pi
                    sems_k = dict(send_sem=send_sem.at[sem], recv_sem=recv_sem.at[sem],
                                  device_id=(tx, ty, c), device_id_type=MESH)
                    sends.append(pltpu.make_async_remote_copy(
                        src_ref=ins[wi].at[c, rows], dst_ref=outs[wi].at[me, c, rows], **sems_k))
                    slab = outs[wi].at[2 * tx + ty, c, rows]
                    recvs.append(pltpu.make_async_remote_copy(src_ref=slab, dst_ref=slab, **sems_k))
        return local, sends, recvs

    def start(ins, outs, sems):
        local, sends, _ = copies(ins, outs, sems)
        for cp in local + sends:
            cp.start()

    def finish(ins, outs, sems):
        local, sends, recvs = copies(ins, outs, sems)
        for cp in local:
            cp.wait()
        for cp in recvs:
            cp.wait_recv()
        for cp in sends:
            cp.wait_send()

    sems = [pltpu.SemaphoreType.DMA((n,)), pltpu.SemaphoreType.DMA((3 * n * GATHER_PIECES,)),
            pltpu.SemaphoreType.DMA((3 * n * GATHER_PIECES,))]
    if into is None:
        return _Comm(shards, [_sds((N_CHIPS,) + s.shape, s.dtype) for s in shards], {}, sems, start, finish)
    return _Comm(list(shards) + list(into), [_sds(g.shape, g.dtype) for g in into],
                 {n + i: i for i in range(n)}, sems, start, finish)


def _gather_d2d(gathered):
    n = len(gathered)

    def copies(outs, sems):
        send_sem, recv_sem = sems
        x, y, c, chips = _mesh_place()
        sends, recvs = [], []
        for wi in range(n):
            for k, (tx, ty) in enumerate(chips):
                sems_k = dict(send_sem=send_sem.at[wi * 3 + k], recv_sem=recv_sem.at[wi * 3 + k],
                              device_id=(x, y, 1 - c), device_id_type=MESH)
                mine = outs[wi].at[2 * tx + ty, c]
                theirs = outs[wi].at[2 * tx + ty, 1 - c]
                sends.append(pltpu.make_async_remote_copy(src_ref=mine, dst_ref=mine, **sems_k))
                recvs.append(pltpu.make_async_remote_copy(src_ref=theirs, dst_ref=theirs, **sems_k))
        return sends, recvs

    def start(ins, outs, sems):
        for cp in copies(outs, sems)[0]:
            cp.start()

    def finish(ins, outs, sems):
        sends, recvs = copies(outs, sems)
        for cp in recvs:
            cp.wait_recv()
        for cp in sends:
            cp.wait_send()

    return _Comm(gathered, [_sds(g.shape, g.dtype) for g in gathered], {i: i for i in range(n)},
                 [pltpu.SemaphoreType.DMA((3 * n,)), pltpu.SemaphoreType.DMA((3 * n,))], start, finish)


def _exchange_halves(grads):
    n = len(grads)

    def copies(ins, outs, sems):
        send_sem, recv_sem = sems
        x, y, c, _ = _mesh_place()
        return [pltpu.make_async_remote_copy(
            src_ref=ins[wi].at[t, 1 - c], dst_ref=outs[wi].at[t],
            send_sem=send_sem.at[wi * N_CHIPS + t], recv_sem=recv_sem.at[wi * N_CHIPS + t],
            device_id=(x, y, 1 - c), device_id_type=MESH) for wi in range(n) for t in range(N_CHIPS)]

    def start(ins, outs, sems):
        for cp in copies(ins, outs, sems):
            cp.start()

    def finish(ins, outs, sems):
        for cp in copies(ins, outs, sems):
            cp.wait()

    return _Comm(grads, [_sds((N_CHIPS,) + g.shape[2:], g.dtype) for g in grads], {},
                 [pltpu.SemaphoreType.DMA((N_CHIPS * n,)), pltpu.SemaphoreType.DMA((N_CHIPS * n,))], start, finish)


def _scatter_ici(sums, targets=(0, 1, 2), into=None):
    n = len(sums)

    def copies(ins, outs, sems):
        local_sem, send_sem, recv_sem = sems
        x, y, c, chips = _mesh_place()
        me = 2 * x + y
        local, sends, recvs = [], [], []
        for wi in range(n):
            if into is None:
                local.append(pltpu.make_async_copy(ins[wi].at[me], outs[wi].at[c, 0], local_sem.at[wi]))
            for k in targets:
                tx, ty = chips[k]
                sems_k = dict(send_sem=send_sem.at[wi * 3 + k], recv_sem=recv_sem.at[wi * 3 + k],
                              device_id=(tx, ty, c), device_id_type=MESH)
                land = outs[wi].at[c, k + 1]
                sends.append(pltpu.make_async_remote_copy(src_ref=ins[wi].at[2 * tx + ty], dst_ref=land, **sems_k))
                recvs.append(pltpu.make_async_remote_copy(src_ref=land, dst_ref=land, **sems_k))
        return local, sends, recvs

    def start(ins, outs, sems):
        local, sends, _ = copies(ins, outs, sems)
        for cp in local + sends:
            cp.start()

    def finish(ins, outs, sems):
        local, sends, recvs = copies(ins, outs, sems)
        for cp in local:
            cp.wait()
        for cp in recvs:
            cp.wait_recv()
        for cp in sends:
            cp.wait_send()

    sem_shapes = [pltpu.SemaphoreType.DMA((n,)), pltpu.SemaphoreType.DMA((3 * n,)), pltpu.SemaphoreType.DMA((3 * n,))]
    if into is None:
        return _Comm(sums, [_sds((2, N_CHIPS) + s.shape[1:], s.dtype) for s in sums], {}, sem_shapes, start, finish)
    return _Comm(list(sums) + list(into), [_sds(t.shape, t.dtype) for t in into],
                 {n + i: i for i in range(n)}, sem_shapes, start, finish)


def _scatter_d2d(terms):
    n = len(terms)

    def copies(outs, sems):
        send_sem, recv_sem = sems
        x, y, c, _ = _mesh_place()
        sends, recvs = [], []
        for wi in range(n):
            sems_w = dict(send_sem=send_sem.at[wi], recv_sem=recv_sem.at[wi],
                          device_id=(x, y, 1 - c), device_id_type=MESH)
            sends.append(pltpu.make_async_remote_copy(src_ref=outs[wi].at[c], dst_ref=outs[wi].at[c], **sems_w))
            recvs.append(pltpu.make_async_remote_copy(src_ref=outs[wi].at[1 - c], dst_ref=outs[wi].at[1 - c], **sems_w))
        return sends, recvs

    def start(ins, outs, sems):
        for cp in copies(outs, sems)[0]:
            cp.start()

    def finish(ins, outs, sems):
        sends, recvs = copies(outs, sems)
        for cp in recvs:
            cp.wait_recv()
        for cp in sends:
            cp.wait_send()

    return _Comm(terms, [_sds(t.shape, t.dtype) for t in terms], {i: i for i in range(n)},
                 [pltpu.SemaphoreType.DMA((n,)), pltpu.SemaphoreType.DMA((n,))], start, finish)


def _chip_sum(name, grad, got, core):
    _, _, hr, c = grad.shape
    rb = _pick(hr, max(16, (1 << 19) // c), 16)

    def body(core_ref, a_ref, b_ref, o_ref):
        o_ref[...] = (a_ref[...].astype(F32) + b_ref[...].astype(F32)).astype(BF16)

    out_spec = pl.BlockSpec((None, rb, c), lambda t, i, core_ref: (t, i, 0))
    return pl.pallas_call(
        body, name=name,
        grid_spec=pltpu.PrefetchScalarGridSpec(
            num_scalar_prefetch=1, grid=(N_CHIPS, hr // rb),
            in_specs=[pl.BlockSpec((None, None, rb, c), lambda t, i, core_ref: (t, core_ref[0], i, 0)), out_spec],
            out_specs=out_spec),
        out_shape=_sds((N_CHIPS, hr, c), BF16), compiler_params=_params(),
    )(core, grad, got)


def _all_reduce_small(pack):
    r = pack.shape[0]

    def body(p_ref, o_ref, land_ref, send_sem, recv_sem):
        x, y, c, _ = _mesh_place()
        me = 4 * x + 2 * y + c
        flips = [(k >> 2 & 1, k >> 1 & 1, k & 1) for k in range(1, N_DEV)]

        def peer(fx, fy, fc):
            return (1 - x if fx else x, 1 - y if fy else y, 1 - c if fc else c)

        land_ref[me] = p_ref[...]
        sent = []
        for k, flip in enumerate(flips):
            cp = pltpu.make_async_remote_copy(
                src_ref=p_ref, dst_ref=land_ref.at[me], send_sem=send_sem.at[k], recv_sem=recv_sem.at[k],
                device_id=peer(*flip), device_id_type=MESH)
            cp.start()
            sent.append(cp)
        for k, flip in enumerate(flips):
            px, py, pc = peer(*flip)
            slot = land_ref.at[4 * px + 2 * py + pc]
            pltpu.make_async_remote_copy(
                src_ref=slot, dst_ref=slot, send_sem=send_sem.at[k], recv_sem=recv_sem.at[k],
                device_id=(px, py, pc), device_id_type=MESH).wait_recv()
        total = land_ref[0]
        for d in range(1, N_DEV):
            total = total + land_ref[d]
        o_ref[...] = total
        for cp in sent:
            cp.wait_send()

    vmem = pl.BlockSpec(memory_space=pltpu.VMEM)
    return pl.pallas_call(
        body, name="all_reduce_small", in_specs=[vmem], out_specs=vmem, out_shape=_sds((r, 128), F32),
        scratch_shapes=[pltpu.VMEM((N_DEV, r, 128), F32), pltpu.SemaphoreType.DMA((N_DEV - 1,)),
                        pltpu.SemaphoreType.DMA((N_DEV - 1,))],
    )(pack)


PACK_TILE = 8 * 128


def _pack(items):
    rows, i = [], 0
    while i < len(items):
        j = i
        while j < len(items) and items[j].size == items[i].size:
            j += 1
        group = jnp.stack([it.reshape(-1).astype(F32) for it in items[i:j]])
        rows.append(jnp.pad(group, ((0, 0), (0, -group.shape[1] % PACK_TILE))).reshape(-1, 128))
        i = j
    return jnp.concatenate(rows, axis=0)


def _unpack(pack, shapes):
    out, row = [], 0
    for shp in shapes:
        size = int(np.prod(shp))
        nrow = -(-size // PACK_TILE) * (PACK_TILE // 128)
        out.append(pack[row:row + nrow].reshape(-1)[:size].reshape(shp))
        row += nrow
    return out


BIG = ["ffn1_w_gu", "ffn1_w_down", "w_in", "w_gate", "w_proj_a", "w_proj_b", "w_out",
       "ffn2_w_gu", "ffn2_w_down", "w_ple_gate", "w_ple_proj"]
SMALL = ["ffn1_norm", "mix_norm", "ffn2_norm", "ple_norm", "a_q_norm", "a_k_norm", "b_q_norm", "b_k_norm",
         "a_rel_bias", "b_sinks"]
WEIGHTS = ["ffn1_norm", "ffn1_w_gu", "ffn1_w_down", "mix_norm", "w_in", "a_q_norm", "a_k_norm", "a_rel_bias",
           "b_q_norm", "b_k_norm", "b_sinks", "w_gate", "w_proj_a", "w_proj_b", "w_out", "ffn2_norm",
           "ffn2_w_gu", "ffn2_w_down", "ple_norm", "w_ple_gate", "w_ple_proj"]
ATTN_A = dict(prev=A_PREV_CHUNKS * CHUNK, group=1, kw=A_WIDTH, qblk=0, kblk=1, vblk=2)
ATTN_B = dict(prev=B_PREV_CHUNKS * CHUNK, group=N_HEADS // B_KV_HEADS, kw=B_KV_WIDTH, qblk=3,
              kblk=4 * A_WIDTH // B_KV_WIDTH, vblk=4 * A_WIDTH // B_KV_WIDTH + 1)


def _cast_epilogue(accs, extras, outs, ij):
    for acc, out in zip(accs, outs):
        out[...] = acc.astype(out.dtype)


GATHER_FIRST = ["ffn1_w_gu", "ffn1_w_down"]
ROW_SHARDED = ("ffn1_w_down", "ffn2_w_down", "w_out", "w_ple_gate")


def _slotted(name, grad):
    if name == "w_in":
        rows, cols = grad.shape
        grad = jnp.transpose(grad.reshape(rows, N_CHIPS, cols // N_CHIPS), (1, 0, 2))
    elif name in ROW_SHARDED:
        grad = grad.reshape(N_CHIPS, grad.shape[0] // N_CHIPS, grad.shape[1])
    return grad.reshape(N_CHIPS, 2, grad.shape[1] // 2, grad.shape[2])


def _local_step(xt, pt, tgt, n_batch, shards, small, core):
    t, d = xt.shape
    tm = _pick(t, 512, 8)
    tk = _pick(t, 512, 8)
    nt = t // tm
    row = pl.BlockSpec((tm, d), lambda i, j, k: (i, 0))
    gs = shards["w_gate"].shape[1]
    ps = shards["w_proj_a"].shape[1]
    es = shards["w_ple_proj"].shape[1]
    pdim = pt.shape[1]
    ncols = N_CHIPS * shards["w_in"].shape[1]
    tin = ncols // 2
    assert 2 * gs == d and 4 * ps == d and 4 * es == d and tin % 128 == 0

    w = {}
    halves = {n: s.reshape(2, s.shape[0] // 2, s.shape[1]) for n, s in shards.items()}

    def publish(names, arrays):
        for name, g in zip(names, arrays):
            g = g.reshape(N_CHIPS, 2 * g.shape[2], g.shape[3])
            if name in ROW_SHARDED:
                g = g.reshape(N_CHIPS * g.shape[1], g.shape[2])
            elif name == "w_in":
                g = jnp.transpose(g, (1, 0, 2)).reshape(g.shape[1], N_CHIPS * g.shape[2])
            w[name] = g

    class GatherPipe:
        def __init__(self, names):
            self.names = names

        def ici(self, targets=(0, 1, 2)):
            self.first = _gather_ici([halves[n] for n in self.names], targets)
            return self.first

        def ici_more(self, targets):
            self.first = _gather_ici([halves[n] for n in self.names], targets, into=self.first.results)
            return self.first

        def d2d(self):
            self.second = _gather_d2d(self.first.results)
            return self.second

        def publish(self):
            publish(self.names, self.second.results)

    class GradPipe:
        def __init__(self, names):
            self.names = names

        def exchange(self, grads):
            self.grads = [_slotted(n, g) for n, g in zip(self.names, grads)]
            self.x = _exchange_halves(self.grads)
            return self.x

        def scatter(self, targets=(0, 1, 2)):
            self.sums = [_chip_sum("chip_sum_" + n, g, got, core)
                         for n, g, got in zip(self.names, self.grads, self.x.results)]
            self.s = _scatter_ici(self.sums, targets)
            return self.s

        def scatter_more(self, targets):
            self.s = _scatter_ici(self.sums, targets, into=self.s.results)
            return self.s

        def forward(self):
            self.f = _scatter_d2d(self.s.results)
            return self.f

        def terms(self):
            return dict(zip(self.names, self.f.results))

    publish(GATHER_FIRST, _all_gather_weights([halves[n] for n in GATHER_FIRST]))
    g_in, g_proj, g_ple = GatherPipe(["w_in", "w_gate"]), GatherPipe(["w_proj_a", "w_proj_b", "w_out"]), \
        GatherPipe(["w_ple_gate", "w_ple_proj"])
    g_down2, g_up2 = GatherPipe(["ffn2_w_down"]), GatherPipe(["ffn2_w_gu"])
    h1, ffn1_saved = _ffn_fwd("ffn1", xt, small["ffn1_norm"], w["ffn1_w_gu"], w["ffn1_w_down"],
                              {"up": lambda: [g_in.ici()], "down": lambda: [g_in.d2d(), g_proj.ici()]})
    g_in.publish()
    w_in, wgate = w["w_in"], w["w_gate"]
    un = _rms_fwd("mix_norm", h1, small["mix_norm"])
    (qkv,) = _mm(
        "qkv", "nn", (nt, 2, 1),
        [(un, row, w_in, pl.BlockSpec((d, tin), lambda i, j, k: (0, j)))], [],
        [(_sds((t, ncols), BF16), pl.BlockSpec((tm, tin), lambda i, j, k: (i, j)))], (tm, tin), _cast_epilogue,
        j_outer=True, comms=[g_proj.d2d(), g_ple.ici()])
    g_proj.publish()
    wpa, wpb, wout = w["w_proj_a"], w["w_proj_b"], w["w_out"]

    def gate_epilogue(accs, extras, outs, ij):
        outs[0][...] = jax.nn.sigmoid(accs[0]).astype(BF16)

    (gates,) = _mm(
        "gate", "nn", (nt, 4, 1),
        [(un, row, wgate, pl.BlockSpec((None, d, gs), lambda i, j, k: (j, 0, 0)))], [],
        [(_sds((2, t, d), BF16), pl.BlockSpec((None, tm, gs), lambda i, j, k: (j // 2, i, j % 2)))],
        (tm, gs), gate_epilogue, j_outer=True, chunked=True, comms=[g_ple.d2d(), g_down2.ici()])
    g_ple.publish()
    wpg, wpe = w["w_ple_gate"], w["w_ple_proj"]

    bias_a = _pair_bias(_bias_a(small["a_rel_bias"][0]))
    bias_b = _pair_bias(_bias_b())
    sink_a = _pair_rows(jnp.full((N_HEADS, 128), NEG_INF, F32))
    sink_b = _pair_rows(jnp.broadcast_to(small["b_sinks"][0][:, None], (N_HEADS, 128)))
    gqa, gka, gqb, gkb = [jnp.tile(small[k], (1, 2)) for k in ("a_q_norm", "a_k_norm", "b_q_norm", "b_k_norm")]
    ya, lse_a = _attn_fwd("attn_a_fwd", qkv, bias_a, sink_a, gqa, gka, ATTN_A, n_batch,
                          comms=[g_down2.d2d(), g_up2.ici(targets=(0, 1))])
    g_down2.publish()
    yb, lse_b = _attn_fwd("attn_b_fwd", qkv, bias_b, sink_b, gqb, gkb, ATTN_B, n_batch,
                          comms=[g_up2.ici_more(targets=(2,))])

    def merge_epilogue(accs, extras, outs, ij):
        pa, pb = accs
        outs[0][...] = (extras[0][...].astype(F32) * pa + extras[1][...].astype(F32) * pb).astype(BF16)
        outs[1][...] = pa.astype(BF16)
        outs[2][...] = pb.astype(BF16)

    y_spec = pl.BlockSpec((tm, A_WIDTH), lambda i, j, k: (i, 0))
    proj_spec = pl.BlockSpec((None, A_WIDTH, ps), lambda i, j, k: (j, 0, 0))
    tile_ps = pl.BlockSpec((tm, ps), lambda i, j, k: (i, j))
    merged, pa, pb = _mm(
        "proj_merge", "nn", (nt, 4, 1),
        [(ya, y_spec, wpa, proj_spec), (yb, y_spec, wpb, proj_spec)],
        [(gates, pl.BlockSpec((None, tm, ps), lambda i, j, k: (0, i, j))),
         (gates, pl.BlockSpec((None, tm, ps), lambda i, j, k: (1, i, j)))],
        [(_sds((t, d), BF16), tile_ps)] * 3, (tm, ps), merge_epilogue, comms=[g_up2.d2d()])
    g_up2.publish()

    def residual_epilogue(accs, extras, outs, ij):
        outs[0][...] = extras[0][...] + accs[0]

    (h2,) = _mm(
        "out_proj", "nn", (nt, 1, 1),
        [(merged, row, wout, pl.BlockSpec((d, d), lambda i, j, k: (0, 0)))],
        [(h1, row)], [(_sds((t, d), F32), row)], (tm, d), residual_epilogue)

    h3, ffn2_saved = _ffn_fwd("ffn2", h2, small["ffn2_norm"], w["ffn2_w_gu"], w["ffn2_w_down"], {})
    n3 = _rms_fwd("ple_norm", h3, small["ple_norm"])
    tile_es = pl.BlockSpec((tm, es), lambda i, j, k: (i, j))
    (pe,) = _mm(
        "ple_embed", "nn", (nt, 4, 1),
        [(pt, pl.BlockSpec((tm, pdim), lambda i, j, k: (i, 0)), wpe, pl.BlockSpec((None, pdim, es), lambda i, j, k: (j, 0, 0)))],
        [], [(_sds((t, d), F32), tile_es)], (tm, es), _cast_epilogue)

    th = _pick(d, 512)

    def head_epilogue(accs, extras, outs, ij):
        h3_ref, pe_ref, tgt_ref = extras
        dy_ref, dpe_ref, dz_ref, loss_ref = outs
        pg = jax.nn.sigmoid(accs[0])
        pev = pe_ref[...]
        diff = h3_ref[...] + pg * pev - tgt_ref[...]
        dy = diff * (1.0 / d)
        dy_ref[...] = dy
        dpe_ref[...] = (dy * pg).astype(BF16)
        dz_ref[...] = (dy * pev * pg * (1.0 - pg)).astype(BF16)
        _accumulate(loss_ref, jnp.full(loss_ref.shape, jnp.sum(diff * diff), F32), (ij[0] == 0) & (ij[1] == 0))

    tile_h = pl.BlockSpec((tm, th), lambda i, j, k: (i, j))
    dy, dpe, dz, loss_acc = _mm(
        "ple_gate_loss", "nn", (nt, d // th, 1),
        [(n3, row, wpg, pl.BlockSpec((d, th), lambda i, j, k: (0, j)))],
        [(h3, tile_h), (pe, tile_h), (tgt, tile_h)],
        [(_sds((t, d), F32), tile_h), (_sds((t, d), BF16), tile_h), (_sds((t, d), BF16), tile_h),
         (_sds((8, 128), F32), pl.BlockSpec((8, 128), lambda i, j, k: (0, 0)))],
        (tm, th), head_epilogue, j_outer=True, chunked=True)
    loss = 0.5 * loss_acc[0, 0] / d

    nk = t // tk
    (dwpe,) = _mm(
        "d_w_ple_proj", "tn", (1, 4, nk),
        [(pt, pl.BlockSpec((tk, pdim), lambda i, j, k: (k, 0)), dpe, pl.BlockSpec((tk, es), lambda i, j, k: (k, j)))],
        [], [(_sds((4, pdim, es), BF16), pl.BlockSpec((None, pdim, es), lambda i, j, k: (j, 0, 0)))],
        (pdim, es), _cast_epilogue)

    def dense_grad(name, a, dyb, comms=()):
        (res,) = _mm(
            name, "tn", (1, d // th, nk),
            [(a, pl.BlockSpec((tk, d), lambda i, j, k: (k, 0)), dyb, pl.BlockSpec((tk, th), lambda i, j, k: (k, j)))],
            [], [(_sds((d, d), BF16), pl.BlockSpec((d, th), lambda i, j, k: (0, j)))], (d, th), _cast_epilogue,
            comms=comms)
        return res

    dwpg = dense_grad("d_w_ple_gate", n3, dz)
    tmn = _pick(t, 1024, 8)
    extras, outs = _rms_bwd_io(h3, small["ple_norm"], dy, tmn)
    dh3, dh3_b, d_ple_norm = _mm(
        "d_ple_norm", "nt", (t // tmn, 1, 1),
        [(dz, pl.BlockSpec((tmn, d), lambda i, j, k: (i, 0)), wpg, pl.BlockSpec((d, d), lambda i, j, k: (0, 0)))],
        extras, outs, (tmn, d), _rms_bwd_epilogue)

    up2, down2, ple = GradPipe(["ffn2_w_gu"]), GradPipe(["ffn2_w_down"]), GradPipe(["w_ple_gate", "w_ple_proj"])
    proj = GradPipe(["w_proj_a", "w_proj_b", "w_out"])
    dh2, dh2_b, d_ffn2_norm, dwgu2, dwd2 = _ffn_bwd(
        "ffn2", dh3, dh3_b, h2, small["ffn2_norm"], w["ffn2_w_gu"], w["ffn2_w_down"], ffn2_saved,
        {"dnorm": lambda dwgu, dwd: [up2.exchange([dwgu]), down2.exchange([dwd]), ple.exchange([dwpg, dwpe])]})

    def dmerge_epilogue(accs, extras, outs, ij):
        dmo = accs[0]
        g_ref, pa_ref, pb_ref = extras
        dg_ref, dpa_ref, dpb_ref = outs
        ga = g_ref[0].astype(F32)
        gb = g_ref[1].astype(F32)
        dg_ref[0] = (dmo * pa_ref[...].astype(F32) * ga * (1.0 - ga)).astype(BF16)
        dg_ref[1] = (dmo * pb_ref[...].astype(F32) * gb * (1.0 - gb)).astype(BF16)
        dpa_ref[...] = (dmo * ga).astype(BF16)
        dpb_ref[...] = (dmo * gb).astype(BF16)

    g_spec = pl.BlockSpec((2, tm, th), lambda i, j, k: (0, i, j))
    dgates, dpa, dpb = _mm(
        "d_merge", "nt", (nt, d // th, 1),
        [(dh2_b, row, wout, pl.BlockSpec((th, d), lambda i, j, k: (j, 0)))],
        [(gates, g_spec), (pa, tile_h), (pb, tile_h)],
        [(_sds((2, t, d), BF16), g_spec), (_sds((t, d), BF16), tile_h), (_sds((t, d), BF16), tile_h)],
        (tm, th), dmerge_epilogue, j_outer=True, chunked=True, comms=[down2.scatter()])
    dwout = dense_grad("d_w_out", merged, dh2_b, comms=[down2.forward(), ple.scatter()])

    yk_spec = pl.BlockSpec((tk, A_WIDTH), lambda i, j, k: (k, 0))
    dk_spec = pl.BlockSpec((tk, ps), lambda i, j, k: (k, j))
    dproj = (_sds((4, A_WIDTH, ps), BF16), proj_spec)
    dwpa, dwpb = _mm(
        "d_w_proj", "tn", (1, 4, nk),
        [(ya, yk_spec, dpa, dk_spec), (yb, yk_spec, dpb, dk_spec)], [], [dproj, dproj], (A_WIDTH, ps), _cast_epilogue,
        comms=[ple.forward()])
    dproj_a = pl.BlockSpec((tm, ps), lambda i, j, k: (i, k))
    wproj_k = pl.BlockSpec((None, A_WIDTH, ps), lambda i, j, k: (k, 0, 0))
    dya, dyb = _mm(
        "d_attn_out", "nt", (nt, 1, 4),
        [(dpa, dproj_a, wpa, wproj_k), (dpb, dproj_a, wpb, wproj_k)], [],
        [(_sds((t, A_WIDTH), BF16), y_spec)] * 2, (tm, A_WIDTH), _cast_epilogue,
        comms=[proj.exchange([dwpa, dwpb, dwout])])

    dqa, dka, dva, dbias_a, _, dgqa, dgka = _attn_bwd(
        "attn_a_bwd", qkv, bias_a, sink_a, gqa, gka, ya, dya, lse_a, ATTN_A, n_batch, True,
        comms=[up2.scatter(), proj.scatter()])
    dqb, dkb, dvb, _, dsink_b, dgqb, dgkb = _attn_bwd(
        "attn_b_bwd", qkv, bias_b, sink_b, gqb, gkb, yb, dyb, lse_b, ATTN_B, n_batch, False,
        comms=[up2.forward(), proj.forward()])
    dqkv = jnp.concatenate([dqa, dka, dva, dqb, dkb, dvb], axis=1)

    (dwgate,) = _mm(
        "d_w_gate", "tn", (1, 4, nk),
        [(un, pl.BlockSpec((tk, d), lambda i, j, k: (k, 0)),
          dgates, pl.BlockSpec((None, tk, gs), lambda i, j, k: (j // 2, k, j % 2)))],
        [], [(_sds((4, d, gs), BF16), pl.BlockSpec((None, d, gs), lambda i, j, k: (j, 0, 0)))], (d, gs), _cast_epilogue)
    (dwin,) = _mm(
        "d_w_in", "tn", (1, 2, nk),
        [(un, pl.BlockSpec((tk, d), lambda i, j, k: (k, 0)), dqkv, pl.BlockSpec((tk, tin), lambda i, j, k: (k, j)))],
        [], [(_sds((d, ncols), BF16), pl.BlockSpec((d, tin), lambda i, j, k: (0, j)))], (d, tin), _cast_epilogue)

    mixer = GradPipe(["w_in", "w_gate"])
    extras, outs = _rms_bwd_io(h1, small["mix_norm"], dh2, tmn)
    dh1, dh1_b, d_mix_norm = _mm(
        "d_mix_norm", "nt", (t // tmn, 1, 6),
        [(dgates, pl.BlockSpec((None, tmn, gs), lambda i, j, k: (jnp.minimum(k, 3) // 2, i, jnp.minimum(k, 3) % 2)),
          wgate, pl.BlockSpec((None, d, gs), lambda i, j, k: (jnp.minimum(k, 3), 0, 0))),
         (dqkv, pl.BlockSpec((tmn, tin), lambda i, j, k: (i, jnp.maximum(k - 4, 0))),
          w_in, pl.BlockSpec((d, tin), lambda i, j, k: (0, jnp.maximum(k - 4, 0))))],
        extras, outs, (tmn, d), _rms_bwd_epilogue, steps=[4, 2],
        comms=[mixer.exchange([dwin, dwgate])])

    up1 = GradPipe(["ffn1_w_gu"])
    down1 = GradPipe(["ffn1_w_down"])
    dx, _, d_ffn1_norm, _, _ = _ffn_bwd(
        "ffn1", dh1, dh1_b, xt, small["ffn1_norm"], w["ffn1_w_gu"], w["ffn1_w_down"], ffn1_saved,
        {"dact": lambda: [mixer.scatter()],
         "dwgu": lambda: [mixer.forward()],
         "dwd": lambda dwgu: [up1.exchange([dwgu])],
         "dnorm": lambda dwgu, dwd: [up1.scatter(), down1.exchange([dwd])]})
    _run_comms("grad_tail_scatter", [up1.forward(), down1.scatter()])
    _run_comms("grad_tail_forward", [down1.forward()])
    terms = {}
    for pipe in (up2, down2, ple, proj, mixer, up1, down1):
        terms.update(pipe.terms())

    def fold(v):
        return v[0, :HEAD_DIM] + v[0, HEAD_DIM:]

    small_grads = {"ffn1_norm": d_ffn1_norm, "mix_norm": d_mix_norm, "ffn2_norm": d_ffn2_norm,
                   "ple_norm": d_ple_norm, "a_q_norm": fold(dgqa), "a_k_norm": fold(dgka),
                   "b_q_norm": fold(dgqb), "b_k_norm": fold(dgkb), "a_rel_bias": _rel_bias_grad(_unpair_bias(dbias_a)),
                   "b_sinks": jnp.sum(dsink_b, axis=1)}
    return loss, dx, terms, small_grads


def kernel(x, p, ffn1_norm, ffn1_w_gu, ffn1_w_down, mix_norm, w_in, a_q_norm, a_k_norm, a_rel_bias, b_q_norm, b_k_norm, b_sinks, w_gate, w_proj_a, w_proj_b, w_out, ffn2_norm, ffn2_w_gu, ffn2_w_down, ple_norm, w_ple_gate, w_ple_proj, loss_target, m_ffn1_norm, m_ffn1_w_gu, m_ffn1_w_down, m_mix_norm, m_w_in, m_a_q_norm, m_a_k_norm, m_a_rel_bias, m_b_q_norm, m_b_k_norm, m_b_sinks, m_w_gate, m_w_proj_a, m_w_proj_b, m_w_out, m_ffn2_norm, m_ffn2_w_gu, m_ffn2_w_down, m_ple_norm, m_w_ple_gate, m_w_ple_proj, v_ffn1_norm, v_ffn1_w_gu, v_ffn1_w_down, v_mix_norm, v_w_in, v_a_q_norm, v_a_k_norm, v_a_rel_bias, v_b_q_norm, v_b_k_norm, v_b_sinks, v_w_gate, v_w_proj_a, v_w_proj_b, v_w_out, v_ffn2_norm, v_ffn2_w_gu, v_ffn2_w_down, v_ple_norm, v_w_ple_gate, v_w_ple_proj):
    given = dict(locals())
    n_batch, s, d = x.shape
    t = n_batch * s
    xt = x.reshape(t, d)
    pt = p.reshape(t, p.shape[-1])
    tgt = loss_target.reshape(t, d)

    shards = {}
    for name in BIG:
        (shards[name],) = _ew("cast_" + name, lambda v: (v,), [given[name][0]], [BF16])
    small = {name: given[name] for name in SMALL}
    core = lax.axis_index("c").astype(jnp.int32).reshape(1)
    loss, dx, terms, small_grads = _local_step(xt, pt, tgt, n_batch, shards, small, core)

    grads, deltas, new_m, new_v = {}, {}, {}, {}
    for name in BIG:
        gw, dl, nm, nv = _adamw_terms("adamw_" + name, terms[name], given[name][0], given["m_" + name][0],
                                      given["v_" + name][0])
        grads[name], deltas[name], new_m[name], new_v[name] = gw[None], dl[None], nm[None], nv[None]

    small_shapes = [given[name].shape for name in SMALL] + [()]
    g_pack = _all_reduce_small(_pack([small_grads[name] for name in SMALL] + [loss]))
    zero = jnp.zeros((), F32)
    w_pack = _pack([given[name] for name in SMALL] + [zero])
    m_pack = _pack([given["m_" + name] for name in SMALL] + [zero])
    v_pack = _pack([given["v_" + name] for name in SMALL] + [zero])
    d_pack, nm_pack, nv_pack = _ew("adamw_small", lambda wv, gv, mv, vv: _adamw_math(wv, gv, mv, vv),
                                   [w_pack, g_pack, m_pack, v_pack], [F32] * 3)
    g_small = _unpack(g_pack, small_shapes)
    loss_total = g_small[-1]
    for name, gv, dv, mv, vv in zip(SMALL, g_small, _unpack(d_pack, small_shapes), _unpack(nm_pack, small_shapes),
                                    _unpack(nv_pack, small_shapes)):
        grads[name], deltas[name], new_m[name], new_v[name] = gv, dv, mv, vv

    return (loss_total, dx.reshape(x.shape), *[grads[n] for n in WEIGHTS], *[deltas[n] for n in WEIGHTS],
            *[new_m[n] for n in WEIGHTS], *[new_v[n] for n in WEIGHTS])
```

```python
import functools

import numpy as np
import jax
import jax.numpy as jnp
from jax import lax
from jax.experimental import pallas as pl
from jax.experimental.pallas import tpu as pltpu

F32 = jnp.float32
BF16 = jnp.bfloat16

CHUNK = 64
HEAD_DIM = 64
A_PREV_CHUNKS = 8
A_MAX_REL = 128
N_HEADS = 8
B_KV_HEADS = 2
B_PREV_CHUNKS = 2
A_WIDTH = N_HEADS * HEAD_DIM
B_KV_WIDTH = B_KV_HEADS * HEAD_DIM
EPS = 1e-6
NEG_INF = -1e30
ATTN_SCALE = HEAD_DIM ** -0.5
Q_BLOCK = 128
PAIR = 2 * HEAD_DIM

ADAM_LR = 0.001
ADAM_B1 = 0.9
ADAM_B2 = 0.999
ADAM_EPS = 1e-08
ADAM_WD = 0.01
ADAM_STEP = 10

N_CHIPS = 4
N_DEV = 8
VMEM_LIMIT_V7X = 56 * 1024 * 1024
ROW_TILE = 1024
MESH = pl.DeviceIdType.MESH
ANY = pl.BlockSpec(memory_space=pl.ANY)

_DN = {
    "nn": (((1,), (0,)), ((), ())),
    "nt": (((1,), (1,)), ((), ())),
    "tn": (((0,), (0,)), ((), ())),
}


def _pick(n, target, mult=128):
    best = None
    for d in range(mult, min(n, target) + 1, mult):
        if n % d == 0:
            best = d
    return n if best is None else best


def _dot(a, b, mode):
    return lax.dot_general(a.astype(BF16), b.astype(BF16), _DN[mode], preferred_element_type=F32)


def _params():
    return pltpu.CompilerParams(vmem_limit_bytes=VMEM_LIMIT_V7X)


class _Comm:
    def __init__(self, ins, outs, aliases, sems, start, finish):
        self.ins, self.outs, self.aliases, self.sems = list(ins), list(outs), dict(aliases), list(sems)
        self.start, self.finish = start, finish
        self.results = None


class _CommPlumbing:
    def __init__(self, comms, n_in, n_out, n_scratch):
        self.comms = list(comms)
        self.n_in, self.n_out, self.n_scratch = n_in, n_out, n_scratch
        self.args = [a for cm in self.comms for a in cm.ins]
        self.out_shape = [o for cm in self.comms for o in cm.outs]
        self.scratch = [s for cm in self.comms for s in cm.sems]
        self.aliases = {}
        i0, o0 = n_in, n_out
        for cm in self.comms:
            for a, b in cm.aliases.items():
                self.aliases[i0 + a] = o0 + b
            i0 += len(cm.ins)
            o0 += len(cm.outs)

    def run(self, in_refs, out_refs, scratch_refs, first, last):
        if not self.comms:
            return
        parts = []
        i0, o0, s0 = self.n_in, self.n_out, self.n_scratch
        for cm in self.comms:
            parts.append((in_refs[i0:i0 + len(cm.ins)], out_refs[o0:o0 + len(cm.outs)],
                          scratch_refs[s0:s0 + len(cm.sems)]))
            i0 += len(cm.ins)
            o0 += len(cm.outs)
            s0 += len(cm.sems)

        @pl.when(first)
        def _():
            for cm, part in zip(self.comms, parts):
                cm.start(*part)

        @pl.when(last)
        def _():
            for cm, part in zip(self.comms, parts):
                cm.finish(*part)

    def deliver(self, results):
        o0 = self.n_out
        for cm in self.comms:
            cm.results = list(results[o0:o0 + len(cm.outs)])
            o0 += len(cm.outs)
        return list(results[:self.n_out])


def _swap_ij(spec):
    index_map = spec.index_map
    return pl.BlockSpec(spec.block_shape, lambda j, i, k: index_map(i, j, k))


MXU_COLUMNS_V7X = 256


def _mm(name, mode, grid, pairs, extras, outs, acc_shape, epilogue, steps=None, comms=(), j_outer=False,
        chunked=False):
    ni, nj, nk = grid
    slots = [pair[4] if len(pair) > 4 else None for pair in pairs]
    pairs = [pair[:4] for pair in pairs]
    n_in = 2 * len(pairs) + len(extras)
    n_out = len(outs)
    tn = acc_shape[1]
    col_chunks = None
    if chunked:
        assert nk == 1 and steps is None and mode in ("nn", "nt")
        col_chunks = [(c0, min(MXU_COLUMNS_V7X, tn - c0)) for c0 in range(0, tn, MXU_COLUMNS_V7X)]
    n_acc = 0 if chunked else (len(pairs) if steps is None else 1)
    plumb = _CommPlumbing(comms, n_in, n_out, n_acc)
    n_all_in = n_in + len(plumb.args)
    n_all_out = n_out + len(plumb.out_shape)
    if j_outer:
        grid = (nj, ni, nk)
        pairs = [(a, _swap_ij(a_spec), b, _swap_ij(b_spec)) for a, a_spec, b, b_spec in pairs]
        extras = [(e, _swap_ij(e_spec)) for e, e_spec in extras]
        outs = [(o, _swap_ij(o_spec)) for o, o_spec in outs]

    def body(*refs):
        in_refs = refs[:n_all_in]
        out_refs = refs[n_all_in:n_all_in + n_all_out]
        scratch = refs[n_all_in + n_all_out:]
        accs = scratch[:n_acc]
        i = pl.program_id(1 if j_outer else 0)
        j = pl.program_id(0 if j_outer else 1)
        k = pl.program_id(2)

        def contrib(p, acc):
            b_ref = in_refs[2 * p + 1]
            rhs = b_ref[...] if slots[p] is None else b_ref[slots[p](i, j, k)]
            acc[...] += _dot(in_refs[2 * p][...], rhs, mode)

        if col_chunks:
            def cols(ref, c0, cs):
                if ref.shape[-1] != tn:
                    return ref
                return ref.at[(slice(None),) * (len(ref.shape) - 1) + (pl.ds(c0, cs),)]

            lhs = [in_refs[2 * p][...] for p in range(len(pairs))]
            for ci, (c0, cs) in enumerate(col_chunks):
                vals = []
                for p in range(len(pairs)):
                    b_ref = in_refs[2 * p + 1]
                    rhs = b_ref[:, c0:c0 + cs] if mode == "nn" else b_ref[c0:c0 + cs, :]
                    vals.append(_dot(lhs[p], rhs, mode))
                epilogue(vals, [cols(r, c0, cs) for r in in_refs[2 * len(pairs):n_in]],
                         [cols(r, c0, cs) for r in out_refs[:n_out]], (i, j * len(col_chunks) + ci))
        else:
            @pl.when(k == 0)
            def _():
                for acc in accs:
                    acc[...] = jnp.zeros(acc.shape, F32)

            if steps is None:
                for p in range(len(pairs)):
                    contrib(p, accs[p])
            else:
                lo = 0
                for p, n in enumerate(steps):
                    pl.when((k >= lo) & (k < lo + n))(functools.partial(contrib, p, accs[0]))
                    lo += n

            @pl.when(k == nk - 1)
            def _():
                epilogue([acc[...] for acc in accs], in_refs[2 * len(pairs):n_in], out_refs[:n_out], (i, j))

        plumb.run(in_refs, out_refs, scratch, (i == 0) & (j == 0) & (k == 0),
                  (i == ni - 1) & (j == nj - 1) & (k == nk - 1))

    args, in_specs = [], []
    for a, a_spec, b, b_spec in pairs:
        args += [a, b]
        in_specs += [a_spec, b_spec]
    for e, e_spec in extras:
        args.append(e)
        in_specs.append(e_spec)
    res = pl.pallas_call(
        body,
        name=name,
        grid=grid,
        in_specs=in_specs + [ANY] * len(plumb.args),
        out_specs=[s for _, s in outs] + [ANY] * len(plumb.out_shape),
        out_shape=[o for o, _ in outs] + plumb.out_shape,
        scratch_shapes=[pltpu.VMEM(acc_shape, F32) for _ in range(n_acc)] + plumb.scratch,
        input_output_aliases=plumb.aliases,
        compiler_params=_params(),
    )(*args, *plumb.args)
    return plumb.deliver(res)


def _sds(shape, dtype):
    return jax.ShapeDtypeStruct(shape, dtype)


def _accumulate(ref, value, first):
    @pl.when(first)
    def _():
        ref[...] = value

    @pl.when(jnp.logical_not(first))
    def _():
        ref[...] += value


def _rms_fwd(name, x, gain):
    t, d = x.shape
    tm = _pick(t, ROW_TILE, 8)

    def body(x_ref, g_ref, y_ref):
        xv = x_ref[...]
        rstd = lax.rsqrt(jnp.mean(xv * xv, axis=-1, keepdims=True) + EPS)
        y_ref[...] = (xv * rstd * g_ref[...]).astype(BF16)

    return pl.pallas_call(
        body, name=name, grid=(t // tm,),
        in_specs=[pl.BlockSpec((tm, d), lambda i: (i, 0)), pl.BlockSpec((1, d), lambda i: (0, 0))],
        out_specs=pl.BlockSpec((tm, d), lambda i: (i, 0)),
        out_shape=_sds((t, d), BF16),
        compiler_params=_params(),
    )(x, gain)


def _rms_bwd_epilogue(accs, extras, outs, ij):
    x_ref, g_ref, r_ref = extras
    dh_ref, dhb_ref, dg_ref = outs
    dn = accs[0]
    xv = x_ref[...]
    rstd = lax.rsqrt(jnp.mean(xv * xv, axis=-1, keepdims=True) + EPS)
    xhat = xv * rstd
    gd = dn * g_ref[...]
    dx = rstd * (gd - xhat * jnp.mean(gd * xhat, axis=-1, keepdims=True))
    dh = r_ref[...] + dx
    dh_ref[...] = dh
    dhb_ref[...] = dh.astype(BF16)
    _accumulate(dg_ref, jnp.sum(dn * xhat, axis=0, keepdims=True), ij[0] == 0)


def _rms_bwd_io(x, gain, dres, tm):
    t, d = x.shape
    row = pl.BlockSpec((tm, d), lambda i, j, k: (i, 0))
    extras = [(x, row), (gain, pl.BlockSpec((1, d), lambda i, j, k: (0, 0))), (dres, row)]
    outs = [(_sds((t, d), F32), row), (_sds((t, d), BF16), row),
            (_sds((1, d), F32), pl.BlockSpec((1, d), lambda i, j, k: (0, 0)))]
    return extras, outs


def _ffn_fwd(tag, h, gain, wgu, wd, hooks):
    t, d = h.shape
    fs = wgu.shape[2]
    f = 2 * fs
    tm = _pick(t, ROW_TILE, 8)
    n = _rms_fwd(tag + "_norm", h, gain)

    def up_epilogue(accs, extras, outs, ij):
        g, u = accs
        gu_ref, a_ref = outs
        gu_ref[0] = g.astype(BF16)
        gu_ref[1] = u.astype(BF16)
        a_ref[...] = (g * jax.nn.sigmoid(g) * u).astype(BF16)

    a_spec = pl.BlockSpec((tm, d), lambda i, j, k: (i, 0))
    gu, a = _mm(
        tag + "_up", "nn", (t // tm, 2, 1),
        [(n, a_spec, wgu, pl.BlockSpec((None, d, fs), lambda i, j, k: (j, 0, 0))),
         (n, a_spec, wgu, pl.BlockSpec((None, d, fs), lambda i, j, k: (j + 2, 0, 0)))],
        [],
        [(_sds((2, t, f), BF16), pl.BlockSpec((2, tm, fs), lambda i, j, k: (0, i, j))),
         (_sds((t, f), BF16), pl.BlockSpec((tm, fs), lambda i, j, k: (i, j)))],
        (tm, fs), up_epilogue, comms=hooks.get("up", lambda: ())(), j_outer=True, chunked=True)

    def down_epilogue(accs, extras, outs, ij):
        outs[0][...] = extras[0][...] + 0.5 * accs[0]


    row = pl.BlockSpec((tm, d), lambda i, j, k: (i, 0))
    (h_new,) = _mm(
        tag + "_down", "nn", (t // tm, 1, 1),
        [(a, pl.BlockSpec((tm, f), lambda i, j, k: (i, 0)), wd, pl.BlockSpec((f, d), lambda i, j, k: (0, 0)))],
        [(h, row)], [(_sds((t, d), F32), row)], (tm, d), down_epilogue, comms=hooks.get("down", lambda: ())())
    return h_new, (n, gu, a)


def _ffn_bwd(tag, dh, dh_b, h, gain, wgu, wd, saved, hooks):
    n, gu, a = saved
    t, d = h.shape
    fs = wgu.shape[2]
    f = 2 * fs
    tm = _pick(t, ROW_TILE, 8)
    tk = _pick(t, ROW_TILE, 8)

    def dact_epilogue(accs, extras, outs, ij):
        da = 0.5 * accs[0]
        g = extras[0][0].astype(F32)
        u = extras[0][1].astype(F32)
        sg = jax.nn.sigmoid(g)
        outs[0][0] = (da * u * sg * (1.0 + g * (1.0 - sg))).astype(BF16)
        outs[0][1] = (da * g * sg).astype(BF16)

    gu_spec = pl.BlockSpec((2, tm, fs), lambda i, j, k: (0, i, j))
    (dgu,) = _mm(
        tag + "_dact", "nt", (t // tm, 2, 1),
        [(dh_b, pl.BlockSpec((tm, d), lambda i, j, k: (i, 0)), wd, pl.BlockSpec((fs, d), lambda i, j, k: (j, 0)))],
        [(gu, gu_spec)], [(_sds((2, t, f), BF16), gu_spec)], (tm, fs), dact_epilogue, j_outer=True, chunked=True,
        comms=hooks.get("dact", lambda: ())())

    def cast_epilogue(accs, extras, outs, ij):
        outs[0][...] = accs[0].astype(BF16)

    (dwgu,) = _mm(
        tag + "_dwgu", "tn", (1, 4, t // tk),
        [(n, pl.BlockSpec((tk, d), lambda i, j, k: (k, 0)),
          dgu, pl.BlockSpec((None, tk, fs), lambda i, j, k: (j // 2, k, j % 2)))],
        [], [(_sds((4, d, fs), BF16), pl.BlockSpec((None, d, fs), lambda i, j, k: (j, 0, 0)))], (d, fs), cast_epilogue,
        comms=hooks.get("dwgu", lambda: ())())

    def half_epilogue(accs, extras, outs, ij):
        outs[0][...] = (0.5 * accs[0]).astype(BF16)

    (dwd,) = _mm(
        tag + "_dwd", "tn", (2, 1, t // tk),
        [(a, pl.BlockSpec((tk, fs), lambda i, j, k: (k, i)), dh_b, pl.BlockSpec((tk, d), lambda i, j, k: (k, 0)))],
        [], [(_sds((f, d), BF16), pl.BlockSpec((fs, d), lambda i, j, k: (i, 0)))], (fs, d), half_epilogue,
        comms=hooks.get("dwd", lambda g: ())(dwgu))

    tmn = _pick(t, ROW_TILE, 8)
    extras, outs = _rms_bwd_io(h, gain, dh, tmn)
    dh_in, dh_in_b, dgain = _mm(
        tag + "_dnorm", "nt", (t // tmn, 1, 4),
        [(dgu, pl.BlockSpec((None, tmn, fs), lambda i, j, k: (k // 2, i, k % 2)),
          wgu, pl.BlockSpec((None, d, fs), lambda i, j, k: (k, 0, 0)))],
        extras, outs, (tmn, d), _rms_bwd_epilogue, comms=hooks.get("dnorm", lambda g, w: ())(dwgu, dwd))
    return dh_in, dh_in_b, dgain, dwgu, dwd


def _lane_lo(shape):
    return lax.broadcasted_iota(jnp.int32, shape, 1) < HEAD_DIM


def _pair_norm(xv, gain):
    lo = _lane_lo(xv.shape)
    x2 = xv * xv
    ms_lo = jnp.sum(jnp.where(lo, x2, 0.0), axis=-1, keepdims=True) * (1.0 / HEAD_DIM)
    ms_hi = jnp.sum(jnp.where(lo, 0.0, x2), axis=-1, keepdims=True) * (1.0 / HEAD_DIM)
    rstd = jnp.where(lo, lax.rsqrt(ms_lo + EPS), lax.rsqrt(ms_hi + EPS))
    xhat = xv * rstd
    return xhat * gain, xhat, rstd


def _pair_norm_bwd(dn, xhat, rstd, gain):
    lo = _lane_lo(dn.shape)
    gd = dn * gain
    t = gd * xhat
    m_lo = jnp.sum(jnp.where(lo, t, 0.0), axis=-1, keepdims=True) * (1.0 / HEAD_DIM)
    m_hi = jnp.sum(jnp.where(lo, 0.0, t), axis=-1, keepdims=True) * (1.0 / HEAD_DIM)
    dx = rstd * (gd - xhat * jnp.where(lo, m_lo, m_hi))
    return dx, jnp.sum(dn * xhat, axis=0, keepdims=True)


def _half(xv, hi):
    lo = _lane_lo(xv.shape)
    return jnp.where(lo, 0, xv) if hi else jnp.where(lo, xv, 0)


def _attn_window(i, prev):
    q0 = i * Q_BLOCK
    start = jnp.maximum(q0 - prev, 0)
    off = start - (q0 - prev)
    return pl.multiple_of(start, Q_BLOCK), pl.multiple_of(off, Q_BLOCK)


def _attn_specs(cfg, s, nq):
    kw = cfg["kw"]
    q_spec = pl.BlockSpec((Q_BLOCK, A_WIDTH), lambda b, i: (b * nq + i, cfg["qblk"]))
    k_spec = pl.BlockSpec((s, kw), lambda b, i: (b, cfg["kblk"]))
    v_spec = pl.BlockSpec((s, kw), lambda b, i: (b, cfg["vblk"]))
    return q_spec, k_spec, v_spec


def _const_spec(shape):
    return pl.BlockSpec(shape, lambda b, i: (0,) * len(shape))


KEY_CHUNK = 128


def _pair_bias(bias_t):
    wext = bias_t.shape[1]
    return jnp.transpose(bias_t.reshape(N_HEADS // 2, 2, wext, Q_BLOCK), (0, 2, 1, 3)).reshape(
        N_HEADS // 2, wext, 2 * Q_BLOCK)


def _unpair_bias(db2):
    wext = db2.shape[1]
    return jnp.transpose(db2.reshape(N_HEADS // 2, wext, 2, Q_BLOCK), (0, 2, 1, 3)).reshape(N_HEADS, wext, Q_BLOCK)


def _pair_rows(rows):
    two = rows.reshape(N_HEADS // 2, 2 * rows.shape[1])
    return jnp.broadcast_to(two[:, None, :], (N_HEADS // 2, 8, two.shape[1]))


def _sub_lo(shape):
    return lax.broadcasted_iota(jnp.int32, shape, 0) < HEAD_DIM


def _by_half(lo_row, hi_row, rows):
    return jnp.where(_sub_lo((rows, lo_row.shape[1])), lo_row, hi_row)


def _stack_pair(xn, jq, group):
    parts = []
    for hq in range(2):
        hk = ((2 * jq + hq) // group) % 2
        xm = _half(xn, hq)
        if hq != hk:
            xm = pltpu.roll(xm, HEAD_DIM, 1)
        parts.append(xm)
    return jnp.concatenate(parts, axis=0).astype(BF16)


def _place_transposed(blk, dst_ref, c, heads, group):
    bt = blk.T
    lo = _sub_lo(bt.shape)
    for h in heads:
        src_hi = ((h // group) % 2) == 1
        part = jnp.where(lo, 0.0, bt) if src_hi else jnp.where(lo, bt, 0.0)
        if src_hi != (h % 2 == 1):
            part = pltpu.roll(part, HEAD_DIM, 0)
        dst_ref[h, c] = part.astype(BF16)


def _attn_fwd(name, qkv, bias2, sink2, gq, gk, cfg, n_batch, comms=()):
    t = qkv.shape[0]
    s = t // n_batch
    nq = s // Q_BLOCK
    nkc = s // KEY_CHUNK
    prev, group, kw = cfg["prev"], cfg["group"], cfg["kw"]
    w = prev + Q_BLOCK
    n_chunks = w // KEY_CHUNK
    wext = bias2.shape[1]
    plumb = _CommPlumbing(comms, 7, 2, 4)
    n_all_in = 7 + len(plumb.args)
    n_all_out = 2 + len(plumb.out_shape)

    def body(*refs):
        q_ref, k_ref, v_ref, bias_ref, sink_ref, gq_ref, gk_ref = refs[:7]
        y_ref, lse_ref = refs[n_all_in:n_all_in + 2]
        kn_ref, vt_ref, s_ref, pst_ref = refs[n_all_in + n_all_out:n_all_in + n_all_out + 4]
        i = pl.program_id(1)
        plumb.run(refs[:n_all_in], refs[n_all_in:n_all_in + n_all_out], refs[n_all_in + n_all_out:],
                  (pl.program_id(0) == 0) & (i == 0), (pl.program_id(0) == n_batch - 1) & (i == nq - 1))

        @pl.when(i == 0)
        def _():
            for jk in range(kw // PAIR):
                cols = pl.ds(jk * PAIR, PAIR)
                heads = [h for h in range(N_HEADS) if (h // group) // 2 == jk]
                kn, _, _ = _pair_norm(k_ref[:, cols].astype(F32), gk_ref[...])
                kn_ref[:, cols] = kn.astype(BF16)
                for c in range(nkc):
                    _place_transposed(v_ref[pl.ds(c * KEY_CHUNK, KEY_CHUNK), cols].astype(F32), vt_ref, c, heads, group)

        start, off = _attn_window(i, prev)
        c0 = start // KEY_CHUNK
        sub8 = lax.broadcasted_iota(jnp.int32, (N_HEADS, Q_BLOCK), 0)
        lse = jnp.zeros((N_HEADS, Q_BLOCK), F32)
        for jq in range(N_HEADS // 2):
            kcols = pl.ds((((2 * jq) // group) // 2) * PAIR, PAIR)
            qn, _, _ = _pair_norm(q_ref[:, pl.ds(jq * PAIR, PAIR)].astype(F32), gq_ref[...])
            qs = _stack_pair(qn * ATTN_SCALE, jq, group)
            s_ref[...] = _dot(kn_ref[pl.ds(start, w), kcols], qs, "nt")
            m = sink_ref[jq, 0:1, :]
            for c in range(n_chunks):
                r = pl.ds(c * KEY_CHUNK, KEY_CHUNK)
                s2 = s_ref[r, :] + bias_ref[jq, pl.ds(off + c * KEY_CHUNK, KEY_CHUNK), :]
                s_ref[r, :] = s2
                m = jnp.maximum(m, jnp.max(s2, axis=0, keepdims=True))
            l = jnp.exp(sink_ref[jq, 0:1, :] - m)
            for c in range(n_chunks):
                p = jnp.exp(s_ref[pl.ds(c * KEY_CHUNK, KEY_CHUNK), :] - m)
                l = l + jnp.sum(p, axis=0, keepdims=True)
                pst_ref[pl.ds(2 * c * KEY_CHUNK, KEY_CHUNK), :] = p[:, :Q_BLOCK].astype(BF16)
                pst_ref[pl.ds((2 * c + 1) * KEY_CHUNK, KEY_CHUNK), :] = p[:, Q_BLOCK:].astype(BF16)
            vl = jnp.concatenate([vt_ref[2 * jq + hq, c0 + c] for c in range(n_chunks) for hq in range(2)], axis=1)
            ot = _dot(vl, pst_ref[...], "nn")
            inv = 1.0 / l
            ot = ot * _by_half(inv[:, :Q_BLOCK], inv[:, Q_BLOCK:], PAIR)
            y_ref[:, pl.ds(jq * PAIR, PAIR)] = ot.T.astype(BF16)
            lse2 = m + jnp.log(l)
            lse = jnp.where(sub8 == 2 * jq, lse2[:, :Q_BLOCK], lse)
            lse = jnp.where(sub8 == 2 * jq + 1, lse2[:, Q_BLOCK:], lse)
        lse_ref[...] = lse

    q_spec, k_spec, v_spec = _attn_specs(cfg, s, nq)
    res = pl.pallas_call(
        body, name=name, grid=(n_batch, nq),
        in_specs=[q_spec, k_spec, v_spec, _const_spec((N_HEADS // 2, wext, 2 * Q_BLOCK)),
                  _const_spec((N_HEADS // 2, 8, 2 * Q_BLOCK)), _const_spec((1, PAIR)), _const_spec((1, PAIR))]
        + [ANY] * len(plumb.args),
        out_specs=[pl.BlockSpec((Q_BLOCK, A_WIDTH), lambda b, i: (b * nq + i, 0)),
                   pl.BlockSpec((None, N_HEADS, Q_BLOCK), lambda b, i: (b * nq + i, 0, 0))]
        + [ANY] * len(plumb.out_shape),
        out_shape=[_sds((t, A_WIDTH), BF16), _sds((t // Q_BLOCK, N_HEADS, Q_BLOCK), F32)] + plumb.out_shape,
        scratch_shapes=[pltpu.VMEM((s, kw), BF16), pltpu.VMEM((N_HEADS, nkc, PAIR, KEY_CHUNK), BF16),
                        pltpu.VMEM((w, 2 * Q_BLOCK), F32), pltpu.VMEM((2 * w, Q_BLOCK), BF16)] + plumb.scratch,
        input_output_aliases=plumb.aliases,
        compiler_params=_params(),
    )(qkv, qkv, qkv, bias2, sink2, gq, gk, *plumb.args)
    return plumb.deliver(res)


def _attn_bwd(name, qkv, bias2, sink2, gq, gk, y, dy, lse, cfg, n_batch, want_dbias, comms=()):
    t = qkv.shape[0]
    s = t // n_batch
    nq = s // Q_BLOCK
    nkc = s // KEY_CHUNK
    prev, group, kw = cfg["prev"], cfg["group"], cfg["kw"]
    w = prev + Q_BLOCK
    n_chunks = w // KEY_CHUNK
    wext = bias2.shape[1]
    plumb = _CommPlumbing(comms, 10, 7, 9)
    n_all_in = 10 + len(plumb.args)
    n_all_out = 7 + len(plumb.out_shape)

    def body(*refs):
        q_ref, k_ref, v_ref, bias_ref, sink_ref, gq_ref, gk_ref, y_ref, dy_ref, lse_ref = refs[:10]
        dq_ref, dk_ref, dv_ref, db_ref, dsink_ref, dgq_ref, dgk_ref = refs[n_all_in:n_all_in + 7]
        kn_ref, knt_ref, dkn_ref, dvs_ref, s_ref, dp_ref, pb_ref, dsb_ref, dst_ref = \
            refs[n_all_in + n_all_out:n_all_in + n_all_out + 9]
        b = pl.program_id(0)
        i = pl.program_id(1)
        first = (b == 0) & (i == 0)
        plumb.run(refs[:n_all_in], refs[n_all_in:n_all_in + n_all_out], refs[n_all_in + n_all_out:],
                  first, (b == n_batch - 1) & (i == nq - 1))

        @pl.when(i == 0)
        def _():
            for jk in range(kw // PAIR):
                cols = pl.ds(jk * PAIR, PAIR)
                heads = [h for h in range(N_HEADS) if (h // group) // 2 == jk]
                for c in range(nkc):
                    rows = pl.ds(c * KEY_CHUNK, KEY_CHUNK)
                    kn, _, _ = _pair_norm(k_ref[rows, cols].astype(F32), gk_ref[...])
                    kn_ref[rows, cols] = kn.astype(BF16)
                    _place_transposed(kn, knt_ref, c, heads, group)
            dkn_ref[...] = jnp.zeros(dkn_ref.shape, F32)
            dvs_ref[...] = jnp.zeros(dvs_ref.shape, F32)

        @pl.when(first)
        def _():
            db_ref[...] = jnp.zeros(db_ref.shape, F32)
            dsink_ref[...] = jnp.zeros(dsink_ref.shape, F32)
            dgq_ref[...] = jnp.zeros(dgq_ref.shape, F32)
            dgk_ref[...] = jnp.zeros(dgk_ref.shape, F32)

        start, off = _attn_window(i, prev)
        c0 = start // KEY_CHUNK
        for jq in range(N_HEADS // 2):
            cols = pl.ds(jq * PAIR, PAIR)
            kcols = pl.ds((((2 * jq) // group) // 2) * PAIR, PAIR)
            qn, q_hat, q_rstd = _pair_norm(q_ref[:, cols].astype(F32), gq_ref[...])
            qs = _stack_pair(qn * ATTN_SCALE, jq, group)
            do_pair = dy_ref[:, cols].astype(F32)
            dos = _stack_pair(do_pair, jq, group)
            prod_t = (do_pair * y_ref[:, cols].astype(F32)).T
            lo = _sub_lo(prod_t.shape)
            delta2 = jnp.concatenate([jnp.sum(jnp.where(lo, prod_t, 0.0), axis=0, keepdims=True),
                                      jnp.sum(jnp.where(lo, 0.0, prod_t), axis=0, keepdims=True)], axis=1)
            lse2 = jnp.concatenate([lse_ref[2 * jq:2 * jq + 1, :], lse_ref[2 * jq + 1:2 * jq + 2, :]], axis=1)
            dsk = -jnp.exp(sink_ref[jq, 0:1, :] - lse2) * delta2
            dsink_ref[2 * jq:2 * jq + 1, :] += dsk[:, :Q_BLOCK]
            dsink_ref[2 * jq + 1:2 * jq + 2, :] += dsk[:, Q_BLOCK:]
            rows_w = pl.ds(start, w)
            s_ref[...] = _dot(kn_ref[rows_w, kcols], qs, "nt")
            dp_ref[...] = _dot(v_ref[rows_w, kcols], dos, "nt")
            for c in range(n_chunks):
                r = pl.ds(c * KEY_CHUNK, KEY_CHUNK)
                brows = pl.ds(off + c * KEY_CHUNK, KEY_CHUNK)
                p = jnp.exp(s_ref[r, :] + bias_ref[jq, brows, :] - lse2)
                ds = p * (dp_ref[r, :] - delta2)
                if want_dbias:
                    db_ref[jq, brows, :] += ds
                ds_b = ds.astype(BF16)
                pb_ref[r, :] = p.astype(BF16)
                dsb_ref[r, :] = ds_b
                dst_ref[pl.ds(2 * c * KEY_CHUNK, KEY_CHUNK), :] = ds_b[:, :Q_BLOCK]
                dst_ref[pl.ds((2 * c + 1) * KEY_CHUNK, KEY_CHUNK), :] = ds_b[:, Q_BLOCK:]
            dkn_ref[rows_w, kcols] += _dot(dsb_ref[...], qs, "nn")
            dvs_ref[rows_w, kcols] += _dot(pb_ref[...], dos, "nn")
            kl = jnp.concatenate([knt_ref[2 * jq + hq, c0 + c] for c in range(n_chunks) for hq in range(2)], axis=1)
            dqt = _dot(kl, dst_ref[...], "nn")
            dq_raw, dg = _pair_norm_bwd(dqt.T * ATTN_SCALE, q_hat, q_rstd, gq_ref[...])
            dq_ref[:, cols] = dq_raw.astype(BF16)
            dgq_ref[...] += dg

        @pl.when(i == nq - 1)
        def _():
            for jk in range(kw // PAIR):
                kcols = pl.ds(jk * PAIR, PAIR)
                _, k_hat, k_rstd = _pair_norm(k_ref[:, kcols].astype(F32), gk_ref[...])
                dk_raw, dg = _pair_norm_bwd(dkn_ref[:, kcols], k_hat, k_rstd, gk_ref[...])
                dk_ref[:, kcols] = dk_raw.astype(BF16)
                dgk_ref[...] += dg
            dv_ref[...] = dvs_ref[...].astype(BF16)

    q_spec, k_spec, v_spec = _attn_specs(cfg, s, nq)
    row = pl.BlockSpec((Q_BLOCK, A_WIDTH), lambda b, i: (b * nq + i, 0))
    kv_out = pl.BlockSpec((s, kw), lambda b, i: (b, 0))
    pair_bias = _const_spec((N_HEADS // 2, wext, 2 * Q_BLOCK))
    res = pl.pallas_call(
        body, name=name, grid=(n_batch, nq),
        in_specs=[q_spec, k_spec, v_spec, pair_bias, _const_spec((N_HEADS // 2, 8, 2 * Q_BLOCK)),
                  _const_spec((1, PAIR)), _const_spec((1, PAIR)), row, row,
                  pl.BlockSpec((None, N_HEADS, Q_BLOCK), lambda b, i: (b * nq + i, 0, 0))] + [ANY] * len(plumb.args),
        out_specs=[row, kv_out, kv_out, pair_bias, _const_spec((N_HEADS, 128)),
                   _const_spec((1, PAIR)), _const_spec((1, PAIR))] + [ANY] * len(plumb.out_shape),
        out_shape=[_sds((t, A_WIDTH), BF16), _sds((t, kw), BF16), _sds((t, kw), BF16),
                   _sds((N_HEADS // 2, wext, 2 * Q_BLOCK), F32), _sds((N_HEADS, 128), F32),
                   _sds((1, PAIR), F32), _sds((1, PAIR), F32)] + plumb.out_shape,
        scratch_shapes=[pltpu.VMEM((s, kw), BF16), pltpu.VMEM((N_HEADS, nkc, PAIR, KEY_CHUNK), BF16),
                        pltpu.VMEM((s, kw), F32), pltpu.VMEM((s, kw), F32),
                        pltpu.VMEM((w, 2 * Q_BLOCK), F32), pltpu.VMEM((w, 2 * Q_BLOCK), F32),
                        pltpu.VMEM((w, 2 * Q_BLOCK), BF16), pltpu.VMEM((w, 2 * Q_BLOCK), BF16),
                        pltpu.VMEM((2 * w, Q_BLOCK), BF16)] + plumb.scratch,
        input_output_aliases=plumb.aliases,
        compiler_params=_params(),
    )(qkv, qkv, qkv, bias2, sink2, gq, gk, y, dy, lse, *plumb.args)
    return plumb.deliver(res)


def _band_tables(prev_chunks):
    prev = prev_chunks * CHUNK
    wext = 2 * prev + Q_BLOCK
    jj = np.arange(wext)[:, None]
    ii = np.arange(Q_BLOCK)[None, :]
    dist = prev + ii - jj
    rel_chunk = (prev // CHUNK + ii // CHUNK) - jj // CHUNK
    allowed = (rel_chunk >= 0) & (rel_chunk <= prev_chunks)
    return dist, allowed


def _alibi_slopes():
    return np.array([2.0 ** (-8.0 * (h + 1) / N_HEADS) for h in range(N_HEADS)], dtype=np.float32)


def _diag_onehot(prev, wext):
    n_diag = wext + Q_BLOCK - 1
    idx = np.clip(prev + Q_BLOCK - 1 - np.arange(n_diag), -A_MAX_REL, A_MAX_REL) + A_MAX_REL
    onehot = np.zeros((n_diag, 2 * A_MAX_REL + 1), np.float32)
    onehot[np.arange(n_diag), idx] = 1.0
    return onehot


def _bias_a(rel_bias):
    prev = A_PREV_CHUNKS * CHUNK
    _, allowed = _band_tables(A_PREV_CHUNKS)
    wext = allowed.shape[0]
    n_diag = wext + Q_BLOCK - 1
    seq = jnp.dot(rel_bias, jnp.asarray(_diag_onehot(prev, wext).T), precision=lax.Precision.HIGHEST)
    seq = jnp.pad(seq, ((0, 0), (0, 1)))
    rows = jnp.broadcast_to(seq[:, None, :], (N_HEADS, Q_BLOCK, n_diag + 1)).reshape(N_HEADS, -1)
    skew = rows[:, :Q_BLOCK * n_diag].reshape(N_HEADS, Q_BLOCK, n_diag)
    tile = jnp.transpose(skew[:, :, Q_BLOCK - 1:Q_BLOCK - 1 + wext], (0, 2, 1))
    return jnp.where(jnp.asarray(allowed)[None], tile, NEG_INF)


def _bias_b():
    dist, allowed = _band_tables(B_PREV_CHUNKS)
    bias = -_alibi_slopes()[:, None, None] * np.abs(dist).astype(np.float32)[None]
    return jnp.asarray(np.where(allowed[None], bias, np.float32(NEG_INF)).astype(np.float32))


def _rel_bias_grad(db_t):
    prev = A_PREV_CHUNKS * CHUNK
    wext = db_t.shape[1]
    n_diag = wext + Q_BLOCK - 1
    wp = n_diag + Q_BLOCK - 1
    xp = jnp.pad(jnp.transpose(db_t, (0, 2, 1)), ((0, 0), (0, 0), (Q_BLOCK - 1, Q_BLOCK - 1)))
    flat = jnp.pad(xp.reshape(N_HEADS, Q_BLOCK * wp), ((0, 0), (0, Q_BLOCK)))
    skew = flat.reshape(N_HEADS, Q_BLOCK, wp + 1)[:, :, :n_diag]
    diag = jnp.sum(skew, axis=1)
    return jnp.dot(diag, jnp.asarray(_diag_onehot(prev, wext)), precision=lax.Precision.HIGHEST)


def _ew(name, fn, ins, out_dtypes):
    r, c = ins[0].shape
    rb = _pick(r, max(16, (1 << 19) // c), 16)
    spec = pl.BlockSpec((rb, c), lambda i: (i, 0))

    def body(*refs):
        vals = fn(*[ref[...] for ref in refs[:len(ins)]])
        for ref, val in zip(refs[len(ins):], vals):
            ref[...] = val.astype(ref.dtype)

    return pl.pallas_call(
        body, name=name, grid=(r // rb,), in_specs=[spec] * len(ins), out_specs=[spec] * len(out_dtypes),
        out_shape=[_sds((r, c), dt) for dt in out_dtypes], compiler_params=_params(),
    )(*ins)


def _adamw_math(w, g, m, v):
    m = ADAM_B1 * m + (1.0 - ADAM_B1) * g
    v = ADAM_B2 * v + (1.0 - ADAM_B2) * (g * g)
    m_hat = m / (1.0 - ADAM_B1 ** ADAM_STEP)
    v_hat = v / (1.0 - ADAM_B2 ** ADAM_STEP)
    delta = -ADAM_LR * (m_hat / (jnp.sqrt(v_hat) + ADAM_EPS) + ADAM_WD * w)
    return delta, m, v


def _adamw_terms(name, terms, w, m, v):
    r, c = w.shape
    hr = r // 2
    rb = _pick(hr, max(16, (1 << 19) // c), 16)
    nb = hr // rb

    def body(t_ref, w_ref, m_ref, v_ref, g_ref, d_ref, nm_ref, nv_ref):
        g = t_ref[0].astype(F32)
        for k in range(1, N_CHIPS):
            g = g + t_ref[k].astype(F32)
        delta, nm, nv = _adamw_math(w_ref[...], g, m_ref[...], v_ref[...])
        g_ref[...] = g
        d_ref[...] = delta
        nm_ref[...] = nm
        nv_ref[...] = nv

    spec = pl.BlockSpec((rb, c), lambda h, i: (h * nb + i, 0))
    return pl.pallas_call(
        body, name=name, grid=(2, nb),
        in_specs=[pl.BlockSpec((None, N_CHIPS, rb, c), lambda h, i: (h, 0, i, 0)), spec, spec, spec],
        out_specs=[spec] * 4, out_shape=[_sds((r, c), F32)] * 4, compiler_params=_params(),
    )(terms, w, m, v)


def _mesh_place():
    x, y, c = lax.axis_index("x"), lax.axis_index("y"), lax.axis_index("c")
    chips = [(x, 1 - y), (1 - x, y), (1 - x, 1 - y)]
    return x, y, c, chips


def _all_gather_weights(shards):
    n = len(shards)

    def body(*refs):
        ins, outs = refs[:n], refs[n:2 * n]
        local_sem, ici_send, ici_recv, d2d_send, d2d_recv = refs[2 * n:]
        x, y, c, chips = _mesh_place()
        me = 2 * x + y
        sibling = (x, y, 1 - c)
        local, sent = [], []
        for wi in range(n):
            loc = pltpu.make_async_copy(ins[wi], outs[wi].at[me], local_sem.at[wi])
            loc.start()
            local.append(loc)
            for k, (tx, ty) in enumerate(chips):
                for pi, rows in _rotated_pieces(shards[wi].shape[1], k):
                    sem = (wi * 3 + k) * GATHER_PIECES + pi
                    cp = pltpu.make_async_remote_copy(
                        src_ref=ins[wi].at[c, rows], dst_ref=outs[wi].at[me, c, rows],
                        send_sem=ici_send.at[sem], recv_sem=ici_recv.at[sem],
                        device_id=(tx, ty, c), device_id_type=MESH)
                    cp.start()
                    sent.append(cp)
        passed = []
        for wi in range(n):
            for k, (tx, ty) in enumerate(chips):
                for pi, rows in _rotated_pieces(shards[wi].shape[1], k):
                    sem = (wi * 3 + k) * GATHER_PIECES + pi
                    slab = outs[wi].at[2 * tx + ty, c, rows]
                    pltpu.make_async_remote_copy(
                        src_ref=slab, dst_ref=slab, send_sem=ici_send.at[sem], recv_sem=ici_recv.at[sem],
                        device_id=(tx, ty, c), device_id_type=MESH).wait_recv()
                    fw = pltpu.make_async_remote_copy(
                        src_ref=slab, dst_ref=slab, send_sem=d2d_send.at[sem], recv_sem=d2d_recv.at[sem],
                        device_id=sibling, device_id_type=MESH)
                    fw.start()
                    passed.append(fw)
        for wi in range(n):
            for k, (tx, ty) in enumerate(chips):
                for pi, rows in enumerate(_row_pieces(shards[wi].shape[1])):
                    sem = (wi * 3 + k) * GATHER_PIECES + pi
                    slab = outs[wi].at[2 * tx + ty, 1 - c, rows]
                    pltpu.make_async_remote_copy(
                        src_ref=slab, dst_ref=slab, send_sem=d2d_send.at[sem], recv_sem=d2d_recv.at[sem],
                        device_id=sibling, device_id_type=MESH).wait_recv()
        for loc in local:
            loc.wait()
        for cp in sent + passed:
            cp.wait_send()

    return pl.pallas_call(
        body, name="all_gather_weights",
        in_specs=[ANY] * n, out_specs=[ANY] * n,
        out_shape=[_sds((N_CHIPS,) + s.shape, s.dtype) for s in shards],
        scratch_shapes=[pltpu.SemaphoreType.DMA((n,))] + [pltpu.SemaphoreType.DMA((3 * n * GATHER_PIECES,))] * 4,
    )(*shards)


def _run_comms(name, comms):
    plumb = _CommPlumbing(comms, 0, 0, 0)
    n_in, n_out = len(plumb.args), len(plumb.out_shape)

    def body(*refs):
        parts = []
        i0, o0, s0 = 0, n_in, n_in + n_out
        for cm in plumb.comms:
            parts.append((refs[i0:i0 + len(cm.ins)], refs[o0:o0 + len(cm.outs)], refs[s0:s0 + len(cm.sems)]))
            i0 += len(cm.ins)
            o0 += len(cm.outs)
            s0 += len(cm.sems)
        for cm, part in zip(plumb.comms, parts):
            cm.start(*part)
        for cm, part in zip(plumb.comms, parts):
            cm.finish(*part)

    res = pl.pallas_call(
        body, name=name, in_specs=[ANY] * n_in, out_specs=[ANY] * n_out, out_shape=plumb.out_shape,
        scratch_shapes=plumb.scratch, input_output_aliases=plumb.aliases,
    )(*plumb.args)
    plumb.deliver(res)


GATHER_PIECES = 4
BF16_TILE_ROWS = 16


def _row_pieces(rows):
    n = GATHER_PIECES
    while rows % (n * BF16_TILE_ROWS):
        n //= 2
    return [pl.ds(i * (rows // n), rows // n) for i in range(n)]


def _rotated_pieces(rows, k):
    pieces = list(enumerate(_row_pieces(rows)))
    k %= len(pieces)
    return pieces[k:] + pieces[:k]


def _gather_ici(shards, targets=(0, 1, 2), into=None):
    n = len(shards)

    def copies(ins, outs, sems):
        local_sem, send_sem, recv_sem = sems
        x, y, c, chips = _mesh_place()
        me = 2 * x + y
        local, sends, recvs = [], [], []
        for wi in range(n):
            if into is None:
                local.append(pltpu.make_async_copy(ins[wi], outs[wi].at[me], local_sem.at[wi]))
            for k in targets:
                tx, ty = chips[k]
                for pi, rows in _rotated_pieces(shards[wi].shape[1], k):
                    sem = (wi * 3 + k) * GATHER_PIECES + pi
                    sems_k = dict(send_sem=send_sem.at[sem], recv_sem=recv_sem.at[sem],
                                  device_id=(tx, ty, c), device_id_type=MESH)
                    sends.append(pltpu.make_async_remote_copy(
                        src_ref=ins[wi].at[c, rows], dst_ref=outs[wi].at[me, c, rows], **sems_k))
                    slab = outs[wi].at[2 * tx + ty, c, rows]
                    recvs.append(pltpu.make_async_remote_copy(src_ref=slab, dst_ref=slab, **sems_k))
        return local, sends, recvs

    def start(ins, outs, sems):
        local, sends, _ = copies(ins, outs, sems)
        for cp in local + sends:
            cp.start()

    def finish(ins, outs, sems):
        local, sends, recvs = copies(ins, outs, sems)
        for cp in local:
            cp.wait()
        for cp in recvs:
            cp.wait_recv()
        for cp in sends:
            cp.wait_send()

    sems = [pltpu.SemaphoreType.DMA((n,)), pltpu.SemaphoreType.DMA((3 * n * GATHER_PIECES,)),
            pltpu.SemaphoreType.DMA((3 * n * GATHER_PIECES,))]
    if into is None:
        return _Comm(shards, [_sds((N_CHIPS,) + s.shape, s.dtype) for s in shards], {}, sems, start, finish)
    return _Comm(list(shards) + list(into), [_sds(g.shape, g.dtype) for g in into],
                 {n + i: i for i in range(n)}, sems, start, finish)


def _gather_d2d(gathered):
    n = len(gathered)

    def copies(outs, sems):
        send_sem, recv_sem = sems
        x, y, c, chips = _mesh_place()
        sends, recvs = [], []
        for wi in range(n):
            for k, (tx, ty) in enumerate(chips):
                sems_k = dict(send_sem=send_sem.at[wi * 3 + k], recv_sem=recv_sem.at[wi * 3 + k],
                              device_id=(x, y, 1 - c), device_id_type=MESH)
                mine = outs[wi].at[2 * tx + ty, c]
                theirs = outs[wi].at[2 * tx + ty, 1 - c]
                sends.append(pltpu.make_async_remote_copy(src_ref=mine, dst_ref=mine, **sems_k))
                recvs.append(pltpu.make_async_remote_copy(src_ref=theirs, dst_ref=theirs, **sems_k))
        return sends, recvs

    def start(ins, outs, sems):
        for cp in copies(outs, sems)[0]:
            cp.start()

    def finish(ins, outs, sems):
        sends, recvs = copies(outs, sems)
        for cp in recvs:
            cp.wait_recv()
        for cp in sends:
            cp.wait_send()

    return _Comm(gathered, [_sds(g.shape, g.dtype) for g in gathered], {i: i for i in range(n)},
                 [pltpu.SemaphoreType.DMA((3 * n,)), pltpu.SemaphoreType.DMA((3 * n,))], start, finish)


def _exchange_halves(grads):
    n = len(grads)

    def copies(ins, outs, sems):
        send_sem, recv_sem = sems
        x, y, c, _ = _mesh_place()
        return [pltpu.make_async_remote_copy(
            src_ref=ins[wi].at[t, 1 - c], dst_ref=outs[wi].at[t],
            send_sem=send_sem.at[wi * N_CHIPS + t], recv_sem=recv_sem.at[wi * N_CHIPS + t],
            device_id=(x, y, 1 - c), device_id_type=MESH) for wi in range(n) for t in range(N_CHIPS)]

    def start(ins, outs, sems):
        for cp in copies(ins, outs, sems):
            cp.start()

    def finish(ins, outs, sems):
        for cp in copies(ins, outs, sems):
            cp.wait()

    return _Comm(grads, [_sds((N_CHIPS,) + g.shape[2:], g.dtype) for g in grads], {},
                 [pltpu.SemaphoreType.DMA((N_CHIPS * n,)), pltpu.SemaphoreType.DMA((N_CHIPS * n,))], start, finish)


def _scatter_ici(sums, targets=(0, 1, 2), into=None):
    n = len(sums)

    def copies(ins, outs, sems):
        local_sem, send_sem, recv_sem = sems
        x, y, c, chips = _mesh_place()
        me = 2 * x + y
        local, sends, recvs = [], [], []
        for wi in range(n):
            if into is None:
                local.append(pltpu.make_async_copy(ins[wi].at[me], outs[wi].at[c, 0], local_sem.at[wi]))
            for k in targets:
                tx, ty = chips[k]
                sems_k = dict(send_sem=send_sem.at[wi * 3 + k], recv_sem=recv_sem.at[wi * 3 + k],
                              device_id=(tx, ty, c), device_id_type=MESH)
                land = outs[wi].at[c, k + 1]
                sends.append(pltpu.make_async_remote_copy(src_ref=ins[wi].at[2 * tx + ty], dst_ref=land, **sems_k))
                recvs.append(pltpu.make_async_remote_copy(src_ref=land, dst_ref=land, **sems_k))
        return local, sends, recvs

    def start(ins, outs, sems):
        local, sends, _ = copies(ins, outs, sems)
        for cp in local + sends:
            cp.start()

    def finish(ins, outs, sems):
        local, sends, recvs = copies(ins, outs, sems)
        for cp in local:
            cp.wait()
        for cp in recvs:
            cp.wait_recv()
        for cp in sends:
            cp.wait_send()

    sem_shapes = [pltpu.SemaphoreType.DMA((n,)), pltpu.SemaphoreType.DMA((3 * n,)), pltpu.SemaphoreType.DMA((3 * n,))]
    if into is None:
        return _Comm(sums, [_sds((2, N_CHIPS) + s.shape[1:], s.dtype) for s in sums], {}, sem_shapes, start, finish)
    return _Comm(list(sums) + list(into), [_sds(t.shape, t.dtype) for t in into],
                 {n + i: i for i in range(n)}, sem_shapes, start, finish)


def _scatter_d2d(terms):
    n = len(terms)

    def copies(outs, sems):
        send_sem, recv_sem = sems
        x, y, c, _ = _mesh_place()
        sends, recvs = [], []
        for wi in range(n):
            sems_w = dict(send_sem=send_sem.at[wi], recv_sem=recv_sem.at[wi],
                          device_id=(x, y, 1 - c), device_id_type=MESH)
            sends.append(pltpu.make_async_remote_copy(src_ref=outs[wi].at[c], dst_ref=outs[wi].at[c], **sems_w))
            recvs.append(pltpu.make_async_remote_copy(src_ref=outs[wi].at[1 - c], dst_ref=outs[wi].at[1 - c], **sems_w))
        return sends, recvs

    def start(ins, outs, sems):
        for cp in copies(outs, sems)[0]:
            cp.start()

    def finish(ins, outs, sems):
        sends, recvs = copies(outs, sems)
        for cp in recvs:
            cp.wait_recv()
        for cp in sends:
            cp.wait_send()

    return _Comm(terms, [_sds(t.shape, t.dtype) for t in terms], {i: i for i in range(n)},
                 [pltpu.SemaphoreType.DMA((n,)), pltpu.SemaphoreType.DMA((n,))], start, finish)


def _chip_sum(name, grad, got, core):
    _, _, hr, c = grad.shape
    rb = _pick(hr, max(16, (1 << 19) // c), 16)

    def body(core_ref, a_ref, b_ref, o_ref):
        o_ref[...] = (a_ref[...].astype(F32) + b_ref[...].astype(F32)).astype(BF16)

    out_spec = pl.BlockSpec((None, rb, c), lambda t, i, core_ref: (t, i, 0))
    return pl.pallas_call(
        body, name=name,
        grid_spec=pltpu.PrefetchScalarGridSpec(
            num_scalar_prefetch=1, grid=(N_CHIPS, hr // rb),
            in_specs=[pl.BlockSpec((None, None, rb, c), lambda t, i, core_ref: (t, core_ref[0], i, 0)), out_spec],
            out_specs=out_spec),
        out_shape=_sds((N_CHIPS, hr, c), BF16), compiler_params=_params(),
    )(core, grad, got)


def _all_reduce_small(pack):
    r = pack.shape[0]

    def body(p_ref, o_ref, land_ref, send_sem, recv_sem):
        x, y, c, _ = _mesh_place()
        me = 4 * x + 2 * y + c
        flips = [(k >> 2 & 1, k >> 1 & 1, k & 1) for k in range(1, N_DEV)]

        def peer(fx, fy, fc):
            return (1 - x if fx else x, 1 - y if fy else y, 1 - c if fc else c)

        land_ref[me] = p_ref[...]
        sent = []
        for k, flip in enumerate(flips):
            cp = pltpu.make_async_remote_copy(
                src_ref=p_ref, dst_ref=land_ref.at[me], send_sem=send_sem.at[k], recv_sem=recv_sem.at[k],
                device_id=peer(*flip), device_id_type=MESH)
            cp.start()
            sent.append(cp)
        for k, flip in enumerate(flips):
            px, py, pc = peer(*flip)
            slot = land_ref.at[4 * px + 2 * py + pc]
            pltpu.make_async_remote_copy(
                src_ref=slot, dst_ref=slot, send_sem=send_sem.at[k], recv_sem=recv_sem.at[k],
                device_id=(px, py, pc), device_id_type=MESH).wait_recv()
        total = land_ref[0]
        for d in range(1, N_DEV):
            total = total + land_ref[d]
        o_ref[...] = total
        for cp in sent:
            cp.wait_send()

    vmem = pl.BlockSpec(memory_space=pltpu.VMEM)
    return pl.pallas_call(
        body, name="all_reduce_small", in_specs=[vmem], out_specs=vmem, out_shape=_sds((r, 128), F32),
        scratch_shapes=[pltpu.VMEM((N_DEV, r, 128), F32), pltpu.SemaphoreType.DMA((N_DEV - 1,)),
                        pltpu.SemaphoreType.DMA((N_DEV - 1,))],
    )(pack)


PACK_TILE = 8 * 128


def _pack(items):
    rows, i = [], 0
    while i < len(items):
        j = i
        while j < len(items) and items[j].size == items[i].size:
            j += 1
        group = jnp.stack([it.reshape(-1).astype(F32) for it in items[i:j]])
        rows.append(jnp.pad(group, ((0, 0), (0, -group.shape[1] % PACK_TILE))).reshape(-1, 128))
        i = j
    return jnp.concatenate(rows, axis=0)


def _unpack(pack, shapes):
    out, row = [], 0
    for shp in shapes:
        size = int(np.prod(shp))
        nrow = -(-size // PACK_TILE) * (PACK_TILE // 128)
        out.append(pack[row:row + nrow].reshape(-1)[:size].reshape(shp))
        row += nrow
    return out


BIG = ["ffn1_w_gu", "ffn1_w_down", "w_in", "w_gate", "w_proj_a", "w_proj_b", "w_out",
       "ffn2_w_gu", "ffn2_w_down", "w_ple_gate", "w_ple_proj"]
SMALL = ["ffn1_norm", "mix_norm", "ffn2_norm", "ple_norm", "a_q_norm", "a_k_norm", "b_q_norm", "b_k_norm",
         "a_rel_bias", "b_sinks"]
WEIGHTS = ["ffn1_norm", "ffn1_w_gu", "ffn1_w_down", "mix_norm", "w_in", "a_q_norm", "a_k_norm", "a_rel_bias",
           "b_q_norm", "b_k_norm", "b_sinks", "w_gate", "w_proj_a", "w_proj_b", "w_out", "ffn2_norm",
           "ffn2_w_gu", "ffn2_w_down", "ple_norm", "w_ple_gate", "w_ple_proj"]
ATTN_A = dict(prev=A_PREV_CHUNKS * CHUNK, group=1, kw=A_WIDTH, qblk=0, kblk=1, vblk=2)
ATTN_B = dict(prev=B_PREV_CHUNKS * CHUNK, group=N_HEADS // B_KV_HEADS, kw=B_KV_WIDTH, qblk=3,
              kblk=4 * A_WIDTH // B_KV_WIDTH, vblk=4 * A_WIDTH // B_KV_WIDTH + 1)


def _cast_epilogue(accs, extras, outs, ij):
    for acc, out in zip(accs, outs):
        out[...] = acc.astype(out.dtype)


GATHER_FIRST = ["ffn1_w_gu", "ffn1_w_down"]
ROW_SHARDED = ("ffn1_w_down", "ffn2_w_down", "w_out", "w_ple_gate")


def _slotted(name, grad):
    if name == "w_in":
        rows, cols = grad.shape
        grad = jnp.transpose(grad.reshape(rows, N_CHIPS, cols // N_CHIPS), (1, 0, 2))
    elif name in ROW_SHARDED:
        grad = grad.reshape(N_CHIPS, grad.shape[0] // N_CHIPS, grad.shape[1])
    return grad.reshape(N_CHIPS, 2, grad.shape[1] // 2, grad.shape[2])


def _local_step(xt, pt, tgt, n_batch, shards, small, core):
    t, d = xt.shape
    tm = _pick(t, ROW_TILE, 8)
    tk = _pick(t, ROW_TILE, 8)
    nt = t // tm
    row = pl.BlockSpec((tm, d), lambda i, j, k: (i, 0))
    gs = shards["w_gate"].shape[1]
    ps = shards["w_proj_a"].shape[1]
    es = shards["w_ple_proj"].shape[1]
    pdim = pt.shape[1]
    ncols = N_CHIPS * shards["w_in"].shape[1]
    tin = ncols // 2
    assert 2 * gs == d and 4 * ps == d and 4 * es == d and tin % 128 == 0

    w = {}
    halves = {n: s.reshape(2, s.shape[0] // 2, s.shape[1]) for n, s in shards.items()}

    def publish(names, arrays):
        for name, g in zip(names, arrays):
            g = g.reshape(N_CHIPS, 2 * g.shape[2], g.shape[3])
            if name in ROW_SHARDED:
                g = g.reshape(N_CHIPS * g.shape[1], g.shape[2])
            elif name == "w_in":
                g = jnp.transpose(g, (1, 0, 2)).reshape(g.shape[1], N_CHIPS * g.shape[2])
            w[name] = g

    class GatherPipe:
        def __init__(self, names):
            self.names = names

        def ici(self, targets=(0, 1, 2)):
            self.first = _gather_ici([halves[n] for n in self.names], targets)
            return self.first

        def ici_more(self, targets):
            self.first = _gather_ici([halves[n] for n in self.names], targets, into=self.first.results)
            return self.first

        def d2d(self):
            self.second = _gather_d2d(self.first.results)
            return self.second

        def publish(self):
            publish(self.names, self.second.results)

    class GradPipe:
        def __init__(self, names):
            self.names = names

        def exchange(self, grads):
            self.grads = [_slotted(n, g) for n, g in zip(self.names, grads)]
            self.x = _exchange_halves(self.grads)
            return self.x

        def scatter(self, targets=(0, 1, 2)):
            self.sums = [_chip_sum("chip_sum_" + n, g, got, core)
                         for n, g, got in zip(self.names, self.grads, self.x.results)]
            self.s = _scatter_ici(self.sums, targets)
            return self.s

        def scatter_more(self, targets):
            self.s = _scatter_ici(self.sums, targets, into=self.s.results)
            return self.s

        def forward(self):
            self.f = _scatter_d2d(self.s.results)
            return self.f

        def terms(self):
            return dict(zip(self.names, self.f.results))

    publish(GATHER_FIRST, _all_gather_weights([halves[n] for n in GATHER_FIRST]))
    g_in, g_proj, g_ple = GatherPipe(["w_in", "w_gate"]), GatherPipe(["w_proj_a", "w_proj_b", "w_out"]), \
        GatherPipe(["w_ple_gate", "w_ple_proj"])
    g_down2, g_up2 = GatherPipe(["ffn2_w_down"]), GatherPipe(["ffn2_w_gu"])
    h1, ffn1_saved = _ffn_fwd("ffn1", xt, small["ffn1_norm"], w["ffn1_w_gu"], w["ffn1_w_down"],
                              {"up": lambda: [g_in.ici()], "down": lambda: [g_in.d2d(), g_proj.ici()]})
    g_in.publish()
    w_in, wgate = w["w_in"], w["w_gate"]
    un = _rms_fwd("mix_norm", h1, small["mix_norm"])
    (qkv,) = _mm(
        "qkv", "nn", (nt, 2, 1),
        [(un, row, w_in, pl.BlockSpec((d, tin), lambda i, j, k: (0, j)))], [],
        [(_sds((t, ncols), BF16), pl.BlockSpec((tm, tin), lambda i, j, k: (i, j)))], (tm, tin), _cast_epilogue,
        j_outer=True, comms=[g_proj.d2d(), g_ple.ici()])
    g_proj.publish()
    wpa, wpb, wout = w["w_proj_a"], w["w_proj_b"], w["w_out"]

    def gate_epilogue(accs, extras, outs, ij):
        outs[0][...] = jax.nn.sigmoid(accs[0]).astype(BF16)

    (gates,) = _mm(
        "gate", "nn", (nt, 4, 1),
        [(un, row, wgate, pl.BlockSpec((None, d, gs), lambda i, j, k: (j, 0, 0)))], [],
        [(_sds((2, t, d), BF16), pl.BlockSpec((None, tm, gs), lambda i, j, k: (j // 2, i, j % 2)))],
        (tm, gs), gate_epilogue, j_outer=True, chunked=True, comms=[g_ple.d2d(), g_down2.ici()])
    g_ple.publish()
    wpg, wpe = w["w_ple_gate"], w["w_ple_proj"]

    bias_a = _pair_bias(_bias_a(small["a_rel_bias"][0]))
    bias_b = _pair_bias(_bias_b())
    sink_a = _pair_rows(jnp.full((N_HEADS, 128), NEG_INF, F32))
    sink_b = _pair_rows(jnp.broadcast_to(small["b_sinks"][0][:, None], (N_HEADS, 128)))
    gqa, gka, gqb, gkb = [jnp.tile(small[k], (1, 2)) for k in ("a_q_norm", "a_k_norm", "b_q_norm", "b_k_norm")]
    ya, lse_a = _attn_fwd("attn_a_fwd", qkv, bias_a, sink_a, gqa, gka, ATTN_A, n_batch,
                          comms=[g_down2.d2d(), g_up2.ici(targets=(0, 1))])
    g_down2.publish()
    yb, lse_b = _attn_fwd("attn_b_fwd", qkv, bias_b, sink_b, gqb, gkb, ATTN_B, n_batch,
                          comms=[g_up2.ici_more(targets=(2,))])

    def merge_epilogue(accs, extras, outs, ij):
        pa, pb = accs
        outs[0][...] = (extras[0][...].astype(F32) * pa + extras[1][...].astype(F32) * pb).astype(BF16)
        outs[1][...] = pa.astype(BF16)
        outs[2][...] = pb.astype(BF16)

    y_spec = pl.BlockSpec((tm, A_WIDTH), lambda i, j, k: (i, 0))
    proj_spec = pl.BlockSpec((None, A_WIDTH, ps), lambda i, j, k: (j, 0, 0))
    tile_ps = pl.BlockSpec((tm, ps), lambda i, j, k: (i, j))
    merged, pa, pb = _mm(
        "proj_merge", "nn", (nt, 4, 1),
        [(ya, y_spec, wpa, proj_spec), (yb, y_spec, wpb, proj_spec)],
        [(gates, pl.BlockSpec((None, tm, ps), lambda i, j, k: (0, i, j))),
         (gates, pl.BlockSpec((None, tm, ps), lambda i, j, k: (1, i, j)))],
        [(_sds((t, d), BF16), tile_ps)] * 3, (tm, ps), merge_epilogue, comms=[g_up2.d2d()])
    g_up2.publish()

    def residual_epilogue(accs, extras, outs, ij):
        outs[0][...] = extras[0][...] + accs[0]

    (h2,) = _mm(
        "out_proj", "nn", (nt, 1, 1),
        [(merged, row, wout, pl.BlockSpec((d, d), lambda i, j, k: (0, 0)))],
        [(h1, row)], [(_sds((t, d), F32), row)], (tm, d), residual_epilogue)

    h3, ffn2_saved = _ffn_fwd("ffn2", h2, small["ffn2_norm"], w["ffn2_w_gu"], w["ffn2_w_down"], {})
    n3 = _rms_fwd("ple_norm", h3, small["ple_norm"])
    tile_es = pl.BlockSpec((tm, es), lambda i, j, k: (i, j))
    (pe,) = _mm(
        "ple_embed", "nn", (nt, 4, 1),
        [(pt, pl.BlockSpec((tm, pdim), lambda i, j, k: (i, 0)), wpe, pl.BlockSpec((None, pdim, es), lambda i, j, k: (j, 0, 0)))],
        [], [(_sds((t, d), F32), tile_es)], (tm, es), _cast_epilogue)

    th = _pick(d, 512)

    def head_epilogue(accs, extras, outs, ij):
        h3_ref, pe_ref, tgt_ref = extras
        dy_ref, dpe_ref, dz_ref, loss_ref = outs
        pg = jax.nn.sigmoid(accs[0])
        pev = pe_ref[...]
        diff = h3_ref[...] + pg * pev - tgt_ref[...]
        dy = diff * (1.0 / d)
        dy_ref[...] = dy
        dpe_ref[...] = (dy * pg).astype(BF16)
        dz_ref[...] = (dy * pev * pg * (1.0 - pg)).astype(BF16)
        _accumulate(loss_ref, jnp.full(loss_ref.shape, jnp.sum(diff * diff), F32), (ij[0] == 0) & (ij[1] == 0))

    tile_h = pl.BlockSpec((tm, th), lambda i, j, k: (i, j))
    dy, dpe, dz, loss_acc = _mm(
        "ple_gate_loss", "nn", (nt, d // th, 1),
        [(n3, row, wpg, pl.BlockSpec((d, th), lambda i, j, k: (0, j)))],
        [(h3, tile_h), (pe, tile_h), (tgt, tile_h)],
        [(_sds((t, d), F32), tile_h), (_sds((t, d), BF16), tile_h), (_sds((t, d), BF16), tile_h),
         (_sds((8, 128), F32), pl.BlockSpec((8, 128), lambda i, j, k: (0, 0)))],
        (tm, th), head_epilogue, j_outer=True, chunked=True)
    loss = 0.5 * loss_acc[0, 0] / d

    nk = t // tk
    (dwpe,) = _mm(
        "d_w_ple_proj", "tn", (1, 4, nk),
        [(pt, pl.BlockSpec((tk, pdim), lambda i, j, k: (k, 0)), dpe, pl.BlockSpec((tk, es), lambda i, j, k: (k, j)))],
        [], [(_sds((4, pdim, es), BF16), pl.BlockSpec((None, pdim, es), lambda i, j, k: (j, 0, 0)))],
        (pdim, es), _cast_epilogue)

    def dense_grad(name, a, dyb, comms=()):
        (res,) = _mm(
            name, "tn", (1, d // th, nk),
            [(a, pl.BlockSpec((tk, d), lambda i, j, k: (k, 0)), dyb, pl.BlockSpec((tk, th), lambda i, j, k: (k, j)))],
            [], [(_sds((d, d), BF16), pl.BlockSpec((d, th), lambda i, j, k: (0, j)))], (d, th), _cast_epilogue,
            comms=comms)
        return res

    dwpg = dense_grad("d_w_ple_gate", n3, dz)
    tmn = _pick(t, ROW_TILE, 8)
    extras, outs = _rms_bwd_io(h3, small["ple_norm"], dy, tmn)
    dh3, dh3_b, d_ple_norm = _mm(
        "d_ple_norm", "nt", (t // tmn, 1, 1),
        [(dz, pl.BlockSpec((tmn, d), lambda i, j, k: (i, 0)), wpg, pl.BlockSpec((d, d), lambda i, j, k: (0, 0)))],
        extras, outs, (tmn, d), _rms_bwd_epilogue)

    up2, down2, ple = GradPipe(["ffn2_w_gu"]), GradPipe(["ffn2_w_down"]), GradPipe(["w_ple_gate", "w_ple_proj"])
    proj = GradPipe(["w_proj_a", "w_proj_b", "w_out"])
    dh2, dh2_b, d_ffn2_norm, dwgu2, dwd2 = _ffn_bwd(
        "ffn2", dh3, dh3_b, h2, small["ffn2_norm"], w["ffn2_w_gu"], w["ffn2_w_down"], ffn2_saved,
        {"dnorm": lambda dwgu, dwd: [up2.exchange([dwgu]), down2.exchange([dwd]), ple.exchange([dwpg, dwpe])]})

    def dmerge_epilogue(accs, extras, outs, ij):
        dmo = accs[0]
        g_ref, pa_ref, pb_ref = extras
        dg_ref, dpa_ref, dpb_ref = outs
        ga = g_ref[0].astype(F32)
        gb = g_ref[1].astype(F32)
        dg_ref[0] = (dmo * pa_ref[...].astype(F32) * ga * (1.0 - ga)).astype(BF16)
        dg_ref[1] = (dmo * pb_ref[...].astype(F32) * gb * (1.0 - gb)).astype(BF16)
        dpa_ref[...] = (dmo * ga).astype(BF16)
        dpb_ref[...] = (dmo * gb).astype(BF16)

    g_spec = pl.BlockSpec((2, tm, th), lambda i, j, k: (0, i, j))
    dgates, dpa, dpb = _mm(
        "d_merge", "nt", (nt, d // th, 1),
        [(dh2_b, row, wout, pl.BlockSpec((th, d), lambda i, j, k: (j, 0)))],
        [(gates, g_spec), (pa, tile_h), (pb, tile_h)],
        [(_sds((2, t, d), BF16), g_spec), (_sds((t, d), BF16), tile_h), (_sds((t, d), BF16), tile_h)],
        (tm, th), dmerge_epilogue, j_outer=True, chunked=True, comms=[down2.scatter()])
    dwout = dense_grad("d_w_out", merged, dh2_b, comms=[down2.forward(), ple.scatter()])

    yk_spec = pl.BlockSpec((tk, A_WIDTH), lambda i, j, k: (k, 0))
    dk_spec = pl.BlockSpec((tk, ps), lambda i, j, k: (k, j))
    dproj = (_sds((4, A_WIDTH, ps), BF16), proj_spec)
    dwpa, dwpb = _mm(
        "d_w_proj", "tn", (1, 4, nk),
        [(ya, yk_spec, dpa, dk_spec), (yb, yk_spec, dpb, dk_spec)], [], [dproj, dproj], (A_WIDTH, ps), _cast_epilogue,
        comms=[ple.forward()])
    dproj_a = pl.BlockSpec((tm, ps), lambda i, j, k: (i, k))
    wproj_k = pl.BlockSpec((None, A_WIDTH, ps), lambda i, j, k: (k, 0, 0))
    dya, dyb = _mm(
        "d_attn_out", "nt", (nt, 1, 4),
        [(dpa, dproj_a, wpa, wproj_k), (dpb, dproj_a, wpb, wproj_k)], [],
        [(_sds((t, A_WIDTH), BF16), y_spec)] * 2, (tm, A_WIDTH), _cast_epilogue,
        comms=[proj.exchange([dwpa, dwpb, dwout])])

    dqa, dka, dva, dbias_a, _, dgqa, dgka = _attn_bwd(
        "attn_a_bwd", qkv, bias_a, sink_a, gqa, gka, ya, dya, lse_a, ATTN_A, n_batch, True,
        comms=[up2.scatter(), proj.scatter()])
    dqb, dkb, dvb, _, dsink_b, dgqb, dgkb = _attn_bwd(
        "attn_b_bwd", qkv, bias_b, sink_b, gqb, gkb, yb, dyb, lse_b, ATTN_B, n_batch, False,
        comms=[up2.forward(), proj.forward()])
    dqkv = jnp.concatenate([dqa, dka, dva, dqb, dkb, dvb], axis=1)

    (dwgate,) = _mm(
        "d_w_gate", "tn", (1, 4, nk),
        [(un, pl.BlockSpec((tk, d), lambda i, j, k: (k, 0)),
          dgates, pl.BlockSpec((None, tk, gs), lambda i, j, k: (j // 2, k, j % 2)))],
        [], [(_sds((4, d, gs), BF16), pl.BlockSpec((None, d, gs), lambda i, j, k: (j, 0, 0)))], (d, gs), _cast_epilogue)
    (dwin,) = _mm(
        "d_w_in", "tn", (1, 2, nk),
        [(un, pl.BlockSpec((tk, d), lambda i, j, k: (k, 0)), dqkv, pl.BlockSpec((tk, tin), lambda i, j, k: (k, j)))],
        [], [(_sds((d, ncols), BF16), pl.BlockSpec((d, tin), lambda i, j, k: (0, j)))], (d, tin), _cast_epilogue)

    mixer = GradPipe(["w_in", "w_gate"])
    extras, outs = _rms_bwd_io(h1, small["mix_norm"], dh2, tmn)
    dh1, dh1_b, d_mix_norm = _mm(
        "d_mix_norm", "nt", (t // tmn, 1, 6),
        [(dgates, pl.BlockSpec((None, tmn, gs), lambda i, j, k: (jnp.minimum(k, 3) // 2, i, jnp.minimum(k, 3) % 2)),
          wgate, pl.BlockSpec((None, d, gs), lambda i, j, k: (jnp.minimum(k, 3), 0, 0))),
         (dqkv, pl.BlockSpec((tmn, tin), lambda i, j, k: (i, jnp.maximum(k - 4, 0))),
          w_in, pl.BlockSpec((d, tin), lambda i, j, k: (0, jnp.maximum(k - 4, 0))))],
        extras, outs, (tmn, d), _rms_bwd_epilogue, steps=[4, 2],
        comms=[mixer.exchange([dwin, dwgate])])

    up1 = GradPipe(["ffn1_w_gu"])
    down1 = GradPipe(["ffn1_w_down"])
    dx, _, d_ffn1_norm, _, _ = _ffn_bwd(
        "ffn1", dh1, dh1_b, xt, small["ffn1_norm"], w["ffn1_w_gu"], w["ffn1_w_down"], ffn1_saved,
        {"dact": lambda: [mixer.scatter()],
         "dwgu": lambda: [mixer.forward()],
         "dwd": lambda dwgu: [up1.exchange([dwgu])],
         "dnorm": lambda dwgu, dwd: [up1.scatter(), down1.exchange([dwd])]})
    _run_comms("grad_tail_scatter", [up1.forward(), down1.scatter()])
    _run_comms("grad_tail_forward", [down1.forward()])
    terms = {}
    for pipe in (up2, down2, ple, proj, mixer, up1, down1):
        terms.update(pipe.terms())

    def fold(v):
        return v[0, :HEAD_DIM] + v[0, HEAD_DIM:]

    small_grads = {"ffn1_norm": d_ffn1_norm, "mix_norm": d_mix_norm, "ffn2_norm": d_ffn2_norm,
                   "ple_norm": d_ple_norm, "a_q_norm": fold(dgqa), "a_k_norm": fold(dgka),
                   "b_q_norm": fold(dgqb), "b_k_norm": fold(dgkb), "a_rel_bias": _rel_bias_grad(_unpair_bias(dbias_a)),
                   "b_sinks": jnp.sum(dsink_b, axis=1)}
    return loss, dx, terms, small_grads


def kernel(x, p, ffn1_norm, ffn1_w_gu, ffn1_w_down, mix_norm, w_in, a_q_norm, a_k_norm, a_rel_bias, b_q_norm, b_k_norm, b_sinks, w_gate, w_proj_a, w_proj_b, w_out, ffn2_norm, ffn2_w_gu, ffn2_w_down, ple_norm, w_ple_gate, w_ple_proj, loss_target, m_ffn1_norm, m_ffn1_w_gu, m_ffn1_w_down, m_mix_norm, m_w_in, m_a_q_norm, m_a_k_norm, m_a_rel_bias, m_b_q_norm, m_b_k_norm, m_b_sinks, m_w_gate, m_w_proj_a, m_w_proj_b, m_w_out, m_ffn2_norm, m_ffn2_w_gu, m_ffn2_w_down, m_ple_norm, m_w_ple_gate, m_w_ple_proj, v_ffn1_norm, v_ffn1_w_gu, v_ffn1_w_down, v_mix_norm, v_w_in, v_a_q_norm, v_a_k_norm, v_a_rel_bias, v_b_q_norm, v_b_k_norm, v_b_sinks, v_w_gate, v_w_proj_a, v_w_proj_b, v_w_out, v_ffn2_norm, v_ffn2_w_gu, v_ffn2_w_down, v_ple_norm, v_w_ple_gate, v_w_ple_proj):
    given = dict(locals())
    n_batch, s, d = x.shape
    t = n_batch * s
    xt = x.reshape(t, d)
    pt = p.reshape(t, p.shape[-1])
    tgt = loss_target.reshape(t, d)

    shards = {}
    for name in BIG:
        (shards[name],) = _ew("cast_" + name, lambda v: (v,), [given[name][0]], [BF16])
    small = {name: given[name] for name in SMALL}
    core = lax.axis_index("c").astype(jnp.int32).reshape(1)
    loss, dx, terms, small_grads = _local_step(xt, pt, tgt, n_batch, shards, small, core)

    grads, deltas, new_m, new_v = {}, {}, {}, {}
    for name in BIG:
        gw, dl, nm, nv = _adamw_terms("adamw_" + name, terms[name], given[name][0], given["m_" + name][0],
                                      given["v_" + name][0])
        grads[name], deltas[name], new_m[name], new_v[name] = gw[None], dl[None], nm[None], nv[None]

    small_shapes = [given[name].shape for name in SMALL] + [()]
    g_pack = _all_reduce_small(_pack([small_grads[name] for name in SMALL] + [loss]))
    zero = jnp.zeros((), F32)
    w_pack = _pack([given[name] for name in SMALL] + [zero])
    m_pack = _pack([given["m_" + name] for name in SMALL] + [zero])
    v_pack = _pack([given["v_" + name] for name in SMALL] + [zero])
    d_pack, nm_pack, nv_pack = _ew("adamw_small", lambda wv, gv, mv, vv: _adamw_math(wv, gv, mv, vv),
                                   [w_pack, g_pack, m_pack, v_pack], [F32] * 3)
    g_small = _unpack(g_pack, small_shapes)
    loss_total = g_small[-1]
    for name, gv, dv, mv, vv in zip(SMALL, g_small, _unpack(d_pack, small_shapes), _unpack(nm_pack, small_shapes),
                                    _unpack(nv_pack, small_shapes)):
        grads[name], deltas[name], new_m[name], new_v[name] = gv, dv, mv, vv

    return (loss_total, dx.reshape(x.shape), *[grads[n] for n in WEIGHTS], *[deltas[n] for n in WEIGHTS],
            *[new_m[n] for n in WEIGHTS], *[new_v[n] for n in WEIGHTS])
```

```python
import functools

import numpy as np
import jax
import jax.numpy as jnp
from jax import lax
from jax.experimental import pallas as pl
from jax.experimental.pallas import tpu as pltpu

F32 = jnp.float32
BF16 = jnp.bfloat16

CHUNK = 64
HEAD_DIM = 64
A_PREV_CHUNKS = 8
A_MAX_REL = 128
N_HEADS = 8
B_KV_HEADS = 2
B_PREV_CHUNKS = 2
A_WIDTH = N_HEADS * HEAD_DIM
B_KV_WIDTH = B_KV_HEADS * HEAD_DIM
EPS = 1e-6
NEG_INF = -1e30
ATTN_SCALE = HEAD_DIM ** -0.5
Q_BLOCK = 128
PAIR = 2 * HEAD_DIM

ADAM_LR = 0.001
ADAM_B1 = 0.9
ADAM_B2 = 0.999
ADAM_EPS = 1e-08
ADAM_WD = 0.01
ADAM_STEP = 10

N_CHIPS = 4
N_DEV = 8
VMEM_LIMIT_V7X = 56 * 1024 * 1024
ROW_TILE = 1024
MESH = pl.DeviceIdType.MESH
ANY = pl.BlockSpec(memory_space=pl.ANY)

_DN = {
    "nn": (((1,), (0,)), ((), ())),
    "nt": (((1,), (1,)), ((), ())),
    "tn": (((0,), (0,)), ((), ())),
}


def _pick(n, target, mult=128):
    best = None
    for d in range(mult, min(n, target) + 1, mult):
        if n % d == 0:
            best = d
    return n if best is None else best


def _dot(a, b, mode):
    return lax.dot_general(a.astype(BF16), b.astype(BF16), _DN[mode], preferred_element_type=F32)


def _params():
    return pltpu.CompilerParams(vmem_limit_bytes=VMEM_LIMIT_V7X)


class _Comm:
    def __init__(self, ins, outs, aliases, sems, start, finish):
        self.ins, self.outs, self.aliases, self.sems = list(ins), list(outs), dict(aliases), list(sems)
        self.start, self.finish = start, finish
        self.results = None


class _CommPlumbing:
    def __init__(self, comms, n_in, n_out, n_scratch):
        self.comms = list(comms)
        self.n_in, self.n_out, self.n_scratch = n_in, n_out, n_scratch
        self.args = [a for cm in self.comms for a in cm.ins]
        self.out_shape = [o for cm in self.comms for o in cm.outs]
        self.scratch = [s for cm in self.comms for s in cm.sems]
        self.aliases = {}
        i0, o0 = n_in, n_out
        for cm in self.comms:
            for a, b in cm.aliases.items():
                self.aliases[i0 + a] = o0 + b
            i0 += len(cm.ins)
            o0 += len(cm.outs)

    def _parts(self, in_refs, out_refs, scratch_refs):
        parts = []
        i0, o0, s0 = self.n_in, self.n_out, self.n_scratch
        for cm in self.comms:
            parts.append((in_refs[i0:i0 + len(cm.ins)], out_refs[o0:o0 + len(cm.outs)],
                          scratch_refs[s0:s0 + len(cm.sems)]))
            i0 += len(cm.ins)
            o0 += len(cm.outs)
            s0 += len(cm.sems)
        return parts

    def start_at(self, in_refs, out_refs, scratch_refs, first):
        if self.comms:
            parts = self._parts(in_refs, out_refs, scratch_refs)

            @pl.when(first)
            def _():
                for cm, part in zip(self.comms, parts):
                    cm.start(*part)

    def finish_at(self, in_refs, out_refs, scratch_refs, last):
        if self.comms:
            parts = self._parts(in_refs, out_refs, scratch_refs)

            @pl.when(last)
            def _():
                for cm, part in zip(self.comms, parts):
                    cm.finish(*part)

    def deliver(self, results):
        o0 = self.n_out
        for cm in self.comms:
            cm.results = list(results[o0:o0 + len(cm.outs)])
            o0 += len(cm.outs)
        return list(results[:self.n_out])


def _swap_ij(spec):
    index_map = spec.index_map
    return pl.BlockSpec(spec.block_shape, lambda j, i, k: index_map(i, j, k))


MXU_COLUMNS_V7X = 256


def _mm(name, mode, grid, pairs, extras, outs, acc_shape, epilogue, steps=None, comms=(), j_outer=False,
        chunked=False):
    ni, nj, nk = grid
    slots = [pair[4] if len(pair) > 4 else None for pair in pairs]
    pairs = [pair[:4] for pair in pairs]
    n_in = 2 * len(pairs) + len(extras)
    n_out = len(outs)
    tn = acc_shape[1]
    col_chunks = None
    if chunked:
        assert nk == 1 and steps is None and mode in ("nn", "nt")
        col_chunks = [(c0, min(MXU_COLUMNS_V7X, tn - c0)) for c0 in range(0, tn, MXU_COLUMNS_V7X)]
    n_acc = 0 if chunked else (len(pairs) if steps is None else 1)
    plumb = _CommPlumbing(comms, n_in, n_out, n_acc)
    n_all_in = n_in + len(plumb.args)
    n_all_out = n_out + len(plumb.out_shape)
    if j_outer:
        grid = (nj, ni, nk)
        pairs = [(a, _swap_ij(a_spec), b, _swap_ij(b_spec)) for a, a_spec, b, b_spec in pairs]
        extras = [(e, _swap_ij(e_spec)) for e, e_spec in extras]
        outs = [(o, _swap_ij(o_spec)) for o, o_spec in outs]

    def body(*refs):
        in_refs = refs[:n_all_in]
        out_refs = refs[n_all_in:n_all_in + n_all_out]
        scratch = refs[n_all_in + n_all_out:]
        accs = scratch[:n_acc]
        i = pl.program_id(1 if j_outer else 0)
        j = pl.program_id(0 if j_outer else 1)
        k = pl.program_id(2)
        plumb.start_at(in_refs, out_refs, scratch, (i == 0) & (j == 0) & (k == 0))

        def contrib(p, acc):
            b_ref = in_refs[2 * p + 1]
            rhs = b_ref[...] if slots[p] is None else b_ref[slots[p](i, j, k)]
            acc[...] += _dot(in_refs[2 * p][...], rhs, mode)

        if col_chunks:
            def cols(ref, c0, cs):
                if ref.shape[-1] != tn:
                    return ref
                return ref.at[(slice(None),) * (len(ref.shape) - 1) + (pl.ds(c0, cs),)]

            lhs = [in_refs[2 * p][...] for p in range(len(pairs))]
            for ci, (c0, cs) in enumerate(col_chunks):
                vals = []
                for p in range(len(pairs)):
                    b_ref = in_refs[2 * p + 1]
                    rhs = b_ref[:, c0:c0 + cs] if mode == "nn" else b_ref[c0:c0 + cs, :]
                    vals.append(_dot(lhs[p], rhs, mode))
                epilogue(vals, [cols(r, c0, cs) for r in in_refs[2 * len(pairs):n_in]],
                         [cols(r, c0, cs) for r in out_refs[:n_out]], (i, j * len(col_chunks) + ci))
        else:
            @pl.when(k == 0)
            def _():
                for acc in accs:
                    acc[...] = jnp.zeros(acc.shape, F32)

            if steps is None:
                for p in range(len(pairs)):
                    contrib(p, accs[p])
            else:
                lo = 0
                for p, n in enumerate(steps):
                    pl.when((k >= lo) & (k < lo + n))(functools.partial(contrib, p, accs[0]))
                    lo += n

            @pl.when(k == nk - 1)
            def _():
                epilogue([acc[...] for acc in accs], in_refs[2 * len(pairs):n_in], out_refs[:n_out], (i, j))

        plumb.finish_at(in_refs, out_refs, scratch, (i == ni - 1) & (j == nj - 1) & (k == nk - 1))

    args, in_specs = [], []
    for a, a_spec, b, b_spec in pairs:
        args += [a, b]
        in_specs += [a_spec, b_spec]
    for e, e_spec in extras:
        args.append(e)
        in_specs.append(e_spec)
    res = pl.pallas_call(
        body,
        name=name,
        grid=grid,
        in_specs=in_specs + [ANY] * len(plumb.args),
        out_specs=[s for _, s in outs] + [ANY] * len(plumb.out_shape),
        out_shape=[o for o, _ in outs] + plumb.out_shape,
        scratch_shapes=[pltpu.VMEM(acc_shape, F32) for _ in range(n_acc)] + plumb.scratch,
        input_output_aliases=plumb.aliases,
        compiler_params=_params(),
    )(*args, *plumb.args)
    return plumb.deliver(res)


def _sds(shape, dtype):
    return jax.ShapeDtypeStruct(shape, dtype)


def _accumulate(ref, value, first):
    @pl.when(first)
    def _():
        ref[...] = value

    @pl.when(jnp.logical_not(first))
    def _():
        ref[...] += value


def _rms_fwd(name, x, gain):
    t, d = x.shape
    tm = _pick(t, ROW_TILE, 8)

    def body(x_ref, g_ref, y_ref):
        xv = x_ref[...]
        rstd = lax.rsqrt(jnp.mean(xv * xv, axis=-1, keepdims=True) + EPS)
        y_ref[...] = (xv * rstd * g_ref[...]).astype(BF16)

    return pl.pallas_call(
        body, name=name, grid=(t // tm,),
        in_specs=[pl.BlockSpec((tm, d), lambda i: (i, 0)), pl.BlockSpec((1, d), lambda i: (0, 0))],
        out_specs=pl.BlockSpec((tm, d), lambda i: (i, 0)),
        out_shape=_sds((t, d), BF16),
        compiler_params=_params(),
    )(x, gain)


def _rms_bwd_epilogue(accs, extras, outs, ij):
    x_ref, g_ref, r_ref = extras
    dh_ref, dhb_ref, dg_ref = outs
    dn = accs[0]
    xv = x_ref[...]
    rstd = lax.rsqrt(jnp.mean(xv * xv, axis=-1, keepdims=True) + EPS)
    xhat = xv * rstd
    gd = dn * g_ref[...]
    dx = rstd * (gd - xhat * jnp.mean(gd * xhat, axis=-1, keepdims=True))
    dh = r_ref[...] + dx
    dh_ref[...] = dh
    dhb_ref[...] = dh.astype(BF16)
    _accumulate(dg_ref, jnp.sum(dn * xhat, axis=0, keepdims=True), ij[0] == 0)


def _rms_bwd_io(x, gain, dres, tm):
    t, d = x.shape
    row = pl.BlockSpec((tm, d), lambda i, j, k: (i, 0))
    extras = [(x, row), (gain, pl.BlockSpec((1, d), lambda i, j, k: (0, 0))), (dres, row)]
    outs = [(_sds((t, d), F32), row), (_sds((t, d), BF16), row),
            (_sds((1, d), F32), pl.BlockSpec((1, d), lambda i, j, k: (0, 0)))]
    return extras, outs


def _ffn_fwd(tag, h, gain, wgu, wd, hooks):
    t, d = h.shape
    fs = wgu.shape[2]
    f = 2 * fs
    tm = _pick(t, ROW_TILE, 8)
    n = _rms_fwd(tag + "_norm", h, gain)

    def up_epilogue(accs, extras, outs, ij):
        g, u = accs
        gu_ref, a_ref = outs
        gu_ref[0] = g.astype(BF16)
        gu_ref[1] = u.astype(BF16)
        a_ref[...] = (g * jax.nn.sigmoid(g) * u).astype(BF16)

    a_spec = pl.BlockSpec((tm, d), lambda i, j, k: (i, 0))
    gu, a = _mm(
        tag + "_up", "nn", (t // tm, 2, 1),
        [(n, a_spec, wgu, pl.BlockSpec((None, d, fs), lambda i, j, k: (j, 0, 0))),
         (n, a_spec, wgu, pl.BlockSpec((None, d, fs), lambda i, j, k: (j + 2, 0, 0)))],
        [],
        [(_sds((2, t, f), BF16), pl.BlockSpec((2, tm, fs), lambda i, j, k: (0, i, j))),
         (_sds((t, f), BF16), pl.BlockSpec((tm, fs), lambda i, j, k: (i, j)))],
        (tm, fs), up_epilogue, comms=hooks.get("up", lambda: ())(), j_outer=True, chunked=True)

    def down_epilogue(accs, extras, outs, ij):
        outs[0][...] = extras[0][...] + 0.5 * accs[0]


    row = pl.BlockSpec((tm, d), lambda i, j, k: (i, 0))
    (h_new,) = _mm(
        tag + "_down", "nn", (t // tm, 1, 1),
        [(a, pl.BlockSpec((tm, f), lambda i, j, k: (i, 0)), wd, pl.BlockSpec((f, d), lambda i, j, k: (0, 0)))],
        [(h, row)], [(_sds((t, d), F32), row)], (tm, d), down_epilogue, comms=hooks.get("down", lambda: ())())
    return h_new, (n, gu, a)


def _ffn_bwd(tag, dh, dh_b, h, gain, wgu, wd, saved, hooks):
    n, gu, a = saved
    t, d = h.shape
    fs = wgu.shape[2]
    f = 2 * fs
    tm = _pick(t, ROW_TILE, 8)
    tk = _pick(t, ROW_TILE, 8)

    def dact_epilogue(accs, extras, outs, ij):
        da = 0.5 * accs[0]
        g = extras[0][0].astype(F32)
        u = extras[0][1].astype(F32)
        sg = jax.nn.sigmoid(g)
        outs[0][0] = (da * u * sg * (1.0 + g * (1.0 - sg))).astype(BF16)
        outs[0][1] = (da * g * sg).astype(BF16)

    gu_spec = pl.BlockSpec((2, tm, fs), lambda i, j, k: (0, i, j))
    (dgu,) = _mm(
        tag + "_dact", "nt", (t // tm, 2, 1),
        [(dh_b, pl.BlockSpec((tm, d), lambda i, j, k: (i, 0)), wd, pl.BlockSpec((fs, d), lambda i, j, k: (j, 0)))],
        [(gu, gu_spec)], [(_sds((2, t, f), BF16), gu_spec)], (tm, fs), dact_epilogue, j_outer=True, chunked=True,
        comms=hooks.get("dact", lambda: ())())

    def cast_epilogue(accs, extras, outs, ij):
        outs[0][...] = accs[0].astype(BF16)

    (dwgu,) = _mm(
        tag + "_dwgu", "tn", (1, 4, t // tk),
        [(n, pl.BlockSpec((tk, d), lambda i, j, k: (k, 0)),
          dgu, pl.BlockSpec((None, tk, fs), lambda i, j, k: (j // 2, k, j % 2)))],
        [], [(_sds((4, d, fs), BF16), pl.BlockSpec((None, d, fs), lambda i, j, k: (j, 0, 0)))], (d, fs), cast_epilogue,
        comms=hooks.get("dwgu", lambda: ())())

    def half_epilogue(accs, extras, outs, ij):
        outs[0][...] = (0.5 * accs[0]).astype(BF16)

    (dwd,) = _mm(
        tag + "_dwd", "tn", (2, 1, t // tk),
        [(a, pl.BlockSpec((tk, fs), lambda i, j, k: (k, i)), dh_b, pl.BlockSpec((tk, d), lambda i, j, k: (k, 0)))],
        [], [(_sds((f, d), BF16), pl.BlockSpec((fs, d), lambda i, j, k: (i, 0)))], (fs, d), half_epilogue,
        comms=hooks.get("dwd", lambda g: ())(dwgu))

    tmn = _pick(t, ROW_TILE, 8)
    extras, outs = _rms_bwd_io(h, gain, dh, tmn)
    dh_in, dh_in_b, dgain = _mm(
        tag + "_dnorm", "nt", (t // tmn, 1, 4),
        [(dgu, pl.BlockSpec((None, tmn, fs), lambda i, j, k: (k // 2, i, k % 2)),
          wgu, pl.BlockSpec((None, d, fs), lambda i, j, k: (k, 0, 0)))],
        extras, outs, (tmn, d), _rms_bwd_epilogue, comms=hooks.get("dnorm", lambda g, w: ())(dwgu, dwd))
    return dh_in, dh_in_b, dgain, dwgu, dwd


def _lane_lo(shape):
    return lax.broadcasted_iota(jnp.int32, shape, 1) < HEAD_DIM


def _pair_norm(xv, gain):
    lo = _lane_lo(xv.shape)
    x2 = xv * xv
    ms_lo = jnp.sum(jnp.where(lo, x2, 0.0), axis=-1, keepdims=True) * (1.0 / HEAD_DIM)
    ms_hi = jnp.sum(jnp.where(lo, 0.0, x2), axis=-1, keepdims=True) * (1.0 / HEAD_DIM)
    rstd = jnp.where(lo, lax.rsqrt(ms_lo + EPS), lax.rsqrt(ms_hi + EPS))
    xhat = xv * rstd
    return xhat * gain, xhat, rstd


def _pair_norm_bwd(dn, xhat, rstd, gain):
    lo = _lane_lo(dn.shape)
    gd = dn * gain
    t = gd * xhat
    m_lo = jnp.sum(jnp.where(lo, t, 0.0), axis=-1, keepdims=True) * (1.0 / HEAD_DIM)
    m_hi = jnp.sum(jnp.where(lo, 0.0, t), axis=-1, keepdims=True) * (1.0 / HEAD_DIM)
    dx = rstd * (gd - xhat * jnp.where(lo, m_lo, m_hi))
    return dx, jnp.sum(dn * xhat, axis=0, keepdims=True)


def _half(xv, hi):
    lo = _lane_lo(xv.shape)
    return jnp.where(lo, 0, xv) if hi else jnp.where(lo, xv, 0)


def _attn_window(i, prev):
    q0 = i * Q_BLOCK
    start = jnp.maximum(q0 - prev, 0)
    off = start - (q0 - prev)
    return pl.multiple_of(start, Q_BLOCK), pl.multiple_of(off, Q_BLOCK)


def _attn_specs(cfg, s, nq):
    kw = cfg["kw"]
    q_spec = pl.BlockSpec((Q_BLOCK, A_WIDTH), lambda b, i: (b * nq + i, cfg["qblk"]))
    k_spec = pl.BlockSpec((s, kw), lambda b, i: (b, cfg["kblk"]))
    v_spec = pl.BlockSpec((s, kw), lambda b, i: (b, cfg["vblk"]))
    return q_spec, k_spec, v_spec


def _const_spec(shape):
    return pl.BlockSpec(shape, lambda b, i: (0,) * len(shape))


KEY_CHUNK = 128


def _pair_bias(bias_t):
    wext = bias_t.shape[1]
    return jnp.transpose(bias_t.reshape(N_HEADS // 2, 2, wext, Q_BLOCK), (0, 2, 1, 3)).reshape(
        N_HEADS // 2, wext, 2 * Q_BLOCK)


def _unpair_bias(db2):
    wext = db2.shape[1]
    return jnp.transpose(db2.reshape(N_HEADS // 2, wext, 2, Q_BLOCK), (0, 2, 1, 3)).reshape(N_HEADS, wext, Q_BLOCK)


def _pair_rows(rows):
    two = rows.reshape(N_HEADS // 2, 2 * rows.shape[1])
    return jnp.broadcast_to(two[:, None, :], (N_HEADS // 2, 8, two.shape[1]))


def _sub_lo(shape):
    return lax.broadcasted_iota(jnp.int32, shape, 0) < HEAD_DIM


def _by_half(lo_row, hi_row, rows):
    return jnp.where(_sub_lo((rows, lo_row.shape[1])), lo_row, hi_row)


def _stack_pair(xn, jq, group):
    parts = []
    for hq in range(2):
        hk = ((2 * jq + hq) // group) % 2
        xm = _half(xn, hq)
        if hq != hk:
            xm = pltpu.roll(xm, HEAD_DIM, 1)
        parts.append(xm)
    return jnp.concatenate(parts, axis=0).astype(BF16)


def _place_transposed(blk, dst_ref, c, heads, group):
    bt = blk.T
    lo = _sub_lo(bt.shape)
    for h in heads:
        src_hi = ((h // group) % 2) == 1
        part = jnp.where(lo, 0.0, bt) if src_hi else jnp.where(lo, bt, 0.0)
        if src_hi != (h % 2 == 1):
            part = pltpu.roll(part, HEAD_DIM, 0)
        dst_ref[h, c] = part.astype(BF16)


def _attn_fwd(name, qkv, bias2, sink2, gq, gk, cfg, n_batch, comms=()):
    t = qkv.shape[0]
    s = t // n_batch
    nq = s // Q_BLOCK
    nkc = s // KEY_CHUNK
    prev, group, kw = cfg["prev"], cfg["group"], cfg["kw"]
    w = prev + Q_BLOCK
    n_chunks = w // KEY_CHUNK
    wext = bias2.shape[1]
    plumb = _CommPlumbing(comms, 7, 2, 4)
    n_all_in = 7 + len(plumb.args)
    n_all_out = 2 + len(plumb.out_shape)

    def body(*refs):
        q_ref, k_ref, v_ref, bias_ref, sink_ref, gq_ref, gk_ref = refs[:7]
        y_ref, lse_ref = refs[n_all_in:n_all_in + 2]
        kn_ref, vt_ref, s_ref, pst_ref = refs[n_all_in + n_all_out:n_all_in + n_all_out + 4]
        i = pl.program_id(1)
        comm_refs = (refs[:n_all_in], refs[n_all_in:n_all_in + n_all_out], refs[n_all_in + n_all_out:])
        plumb.start_at(*comm_refs, (pl.program_id(0) == 0) & (i == 0))

        @pl.when(i == 0)
        def _():
            for jk in range(kw // PAIR):
                cols = pl.ds(jk * PAIR, PAIR)
                heads = [h for h in range(N_HEADS) if (h // group) // 2 == jk]
                kn, _, _ = _pair_norm(k_ref[:, cols].astype(F32), gk_ref[...])
                kn_ref[:, cols] = kn.astype(BF16)
                for c in range(nkc):
                    _place_transposed(v_ref[pl.ds(c * KEY_CHUNK, KEY_CHUNK), cols].astype(F32), vt_ref, c, heads, group)

        start, off = _attn_window(i, prev)
        c0 = start // KEY_CHUNK
        sub8 = lax.broadcasted_iota(jnp.int32, (N_HEADS, Q_BLOCK), 0)
        lse = jnp.zeros((N_HEADS, Q_BLOCK), F32)
        for jq in range(N_HEADS // 2):
            kcols = pl.ds((((2 * jq) // group) // 2) * PAIR, PAIR)
            qn, _, _ = _pair_norm(q_ref[:, pl.ds(jq * PAIR, PAIR)].astype(F32), gq_ref[...])
            qs = _stack_pair(qn * ATTN_SCALE, jq, group)
            s_ref[...] = _dot(kn_ref[pl.ds(start, w), kcols], qs, "nt")
            m = sink_ref[jq, 0:1, :]
            for c in range(n_chunks):
                r = pl.ds(c * KEY_CHUNK, KEY_CHUNK)
                s2 = s_ref[r, :] + bias_ref[jq, pl.ds(off + c * KEY_CHUNK, KEY_CHUNK), :]
                s_ref[r, :] = s2
                m = jnp.maximum(m, jnp.max(s2, axis=0, keepdims=True))
            l = jnp.exp(sink_ref[jq, 0:1, :] - m)
            for c in range(n_chunks):
                p = jnp.exp(s_ref[pl.ds(c * KEY_CHUNK, KEY_CHUNK), :] - m)
                l = l + jnp.sum(p, axis=0, keepdims=True)
                pst_ref[pl.ds(2 * c * KEY_CHUNK, KEY_CHUNK), :] = p[:, :Q_BLOCK].astype(BF16)
                pst_ref[pl.ds((2 * c + 1) * KEY_CHUNK, KEY_CHUNK), :] = p[:, Q_BLOCK:].astype(BF16)
            vl = jnp.concatenate([vt_ref[2 * jq + hq, c0 + c] for c in range(n_chunks) for hq in range(2)], axis=1)
            ot = _dot(vl, pst_ref[...], "nn")
            inv = 1.0 / l
            ot = ot * _by_half(inv[:, :Q_BLOCK], inv[:, Q_BLOCK:], PAIR)
            y_ref[:, pl.ds(jq * PAIR, PAIR)] = ot.T.astype(BF16)
            lse2 = m + jnp.log(l)
            lse = jnp.where(sub8 == 2 * jq, lse2[:, :Q_BLOCK], lse)
            lse = jnp.where(sub8 == 2 * jq + 1, lse2[:, Q_BLOCK:], lse)
        lse_ref[...] = lse
        plumb.finish_at(*comm_refs, (pl.program_id(0) == n_batch - 1) & (i == nq - 1))

    q_spec, k_spec, v_spec = _attn_specs(cfg, s, nq)
    res = pl.pallas_call(
        body, name=name, grid=(n_batch, nq),
        in_specs=[q_spec, k_spec, v_spec, _const_spec((N_HEADS // 2, wext, 2 * Q_BLOCK)),
                  _const_spec((N_HEADS // 2, 8, 2 * Q_BLOCK)), _const_spec((1, PAIR)), _const_spec((1, PAIR))]
        + [ANY] * len(plumb.args),
        out_specs=[pl.BlockSpec((Q_BLOCK, A_WIDTH), lambda b, i: (b * nq + i, 0)),
                   pl.BlockSpec((None, N_HEADS, Q_BLOCK), lambda b, i: (b * nq + i, 0, 0))]
        + [ANY] * len(plumb.out_shape),
        out_shape=[_sds((t, A_WIDTH), BF16), _sds((t // Q_BLOCK, N_HEADS, Q_BLOCK), F32)] + plumb.out_shape,
        scratch_shapes=[pltpu.VMEM((s, kw), BF16), pltpu.VMEM((N_HEADS, nkc, PAIR, KEY_CHUNK), BF16),
                        pltpu.VMEM((w, 2 * Q_BLOCK), F32), pltpu.VMEM((2 * w, Q_BLOCK), BF16)] + plumb.scratch,
        input_output_aliases=plumb.aliases,
        compiler_params=_params(),
    )(qkv, qkv, qkv, bias2, sink2, gq, gk, *plumb.args)
    return plumb.deliver(res)


def _attn_bwd(name, qkv, bias2, sink2, gq, gk, y, dy, lse, cfg, n_batch, want_dbias, comms=()):
    t = qkv.shape[0]
    s = t // n_batch
    nq = s // Q_BLOCK
    nkc = s // KEY_CHUNK
    prev, group, kw = cfg["prev"], cfg["group"], cfg["kw"]
    w = prev + Q_BLOCK
    n_chunks = w // KEY_CHUNK
    wext = bias2.shape[1]
    plumb = _CommPlumbing(comms, 10, 7, 9)
    n_all_in = 10 + len(plumb.args)
    n_all_out = 7 + len(plumb.out_shape)

    def body(*refs):
        q_ref, k_ref, v_ref, bias_ref, sink_ref, gq_ref, gk_ref, y_ref, dy_ref, lse_ref = refs[:10]
        dq_ref, dk_ref, dv_ref, db_ref, dsink_ref, dgq_ref, dgk_ref = refs[n_all_in:n_all_in + 7]
        kn_ref, knt_ref, dkn_ref, dvs_ref, s_ref, dp_ref, pb_ref, dsb_ref, dst_ref = \
            refs[n_all_in + n_all_out:n_all_in + n_all_out + 9]
        b = pl.program_id(0)
        i = pl.program_id(1)
        first = (b == 0) & (i == 0)
        comm_refs = (refs[:n_all_in], refs[n_all_in:n_all_in + n_all_out], refs[n_all_in + n_all_out:])
        plumb.start_at(*comm_refs, first)

        @pl.when(i == 0)
        def _():
            for jk in range(kw // PAIR):
                cols = pl.ds(jk * PAIR, PAIR)
                heads = [h for h in range(N_HEADS) if (h // group) // 2 == jk]
                for c in range(nkc):
                    rows = pl.ds(c * KEY_CHUNK, KEY_CHUNK)
                    kn, _, _ = _pair_norm(k_ref[rows, cols].astype(F32), gk_ref[...])
                    kn_ref[rows, cols] = kn.astype(BF16)
                    _place_transposed(kn, knt_ref, c, heads, group)
            dkn_ref[...] = jnp.zeros(dkn_ref.shape, F32)
            dvs_ref[...] = jnp.zeros(dvs_ref.shape, F32)

        @pl.when(first)
        def _():
            db_ref[...] = jnp.zeros(db_ref.shape, F32)
            dsink_ref[...] = jnp.zeros(dsink_ref.shape, F32)
            dgq_ref[...] = jnp.zeros(dgq_ref.shape, F32)
            dgk_ref[...] = jnp.zeros(dgk_ref.shape, F32)

        start, off = _attn_window(i, prev)
        c0 = start // KEY_CHUNK
        for jq in range(N_HEADS // 2):
            cols = pl.ds(jq * PAIR, PAIR)
            kcols = pl.ds((((2 * jq) // group) // 2) * PAIR, PAIR)
            qn, q_hat, q_rstd = _pair_norm(q_ref[:, cols].astype(F32), gq_ref[...])
            qs = _stack_pair(qn * ATTN_SCALE, jq, group)
            do_pair = dy_ref[:, cols].astype(F32)
            dos = _stack_pair(do_pair, jq, group)
            prod_t = (do_pair * y_ref[:, cols].astype(F32)).T
            lo = _sub_lo(prod_t.shape)
            delta2 = jnp.concatenate([jnp.sum(jnp.where(lo, prod_t, 0.0), axis=0, keepdims=True),
                                      jnp.sum(jnp.where(lo, 0.0, prod_t), axis=0, keepdims=True)], axis=1)
            lse2 = jnp.concatenate([lse_ref[2 * jq:2 * jq + 1, :], lse_ref[2 * jq + 1:2 * jq + 2, :]], axis=1)
            dsk = -jnp.exp(sink_ref[jq, 0:1, :] - lse2) * delta2
            dsink_ref[2 * jq:2 * jq + 1, :] += dsk[:, :Q_BLOCK]
            dsink_ref[2 * jq + 1:2 * jq + 2, :] += dsk[:, Q_BLOCK:]
            rows_w = pl.ds(start, w)
            s_ref[...] = _dot(kn_ref[rows_w, kcols], qs, "nt")
            dp_ref[...] = _dot(v_ref[rows_w, kcols], dos, "nt")
            for c in range(n_chunks):
                r = pl.ds(c * KEY_CHUNK, KEY_CHUNK)
                brows = pl.ds(off + c * KEY_CHUNK, KEY_CHUNK)
                p = jnp.exp(s_ref[r, :] + bias_ref[jq, brows, :] - lse2)
                ds = p * (dp_ref[r, :] - delta2)
                if want_dbias:
                    db_ref[jq, brows, :] += ds
                ds_b = ds.astype(BF16)
                pb_ref[r, :] = p.astype(BF16)
                dsb_ref[r, :] = ds_b
                dst_ref[pl.ds(2 * c * KEY_CHUNK, KEY_CHUNK), :] = ds_b[:, :Q_BLOCK]
                dst_ref[pl.ds((2 * c + 1) * KEY_CHUNK, KEY_CHUNK), :] = ds_b[:, Q_BLOCK:]
            dkn_ref[rows_w, kcols] += _dot(dsb_ref[...], qs, "nn")
            dvs_ref[rows_w, kcols] += _dot(pb_ref[...], dos, "nn")
            kl = jnp.concatenate([knt_ref[2 * jq + hq, c0 + c] for c in range(n_chunks) for hq in range(2)], axis=1)
            dqt = _dot(kl, dst_ref[...], "nn")
            dq_raw, dg = _pair_norm_bwd(dqt.T * ATTN_SCALE, q_hat, q_rstd, gq_ref[...])
            dq_ref[:, cols] = dq_raw.astype(BF16)
            dgq_ref[...] += dg

        @pl.when(i == nq - 1)
        def _():
            for jk in range(kw // PAIR):
                kcols = pl.ds(jk * PAIR, PAIR)
                _, k_hat, k_rstd = _pair_norm(k_ref[:, kcols].astype(F32), gk_ref[...])
                dk_raw, dg = _pair_norm_bwd(dkn_ref[:, kcols], k_hat, k_rstd, gk_ref[...])
                dk_ref[:, kcols] = dk_raw.astype(BF16)
                dgk_ref[...] += dg
            dv_ref[...] = dvs_ref[...].astype(BF16)

        plumb.finish_at(*comm_refs, (b == n_batch - 1) & (i == nq - 1))

    q_spec, k_spec, v_spec = _attn_specs(cfg, s, nq)
    row = pl.BlockSpec((Q_BLOCK, A_WIDTH), lambda b, i: (b * nq + i, 0))
    kv_out = pl.BlockSpec((s, kw), lambda b, i: (b, 0))
    pair_bias = _const_spec((N_HEADS // 2, wext, 2 * Q_BLOCK))
    res = pl.pallas_call(
        body, name=name, grid=(n_batch, nq),
        in_specs=[q_spec, k_spec, v_spec, pair_bias, _const_spec((N_HEADS // 2, 8, 2 * Q_BLOCK)),
                  _const_spec((1, PAIR)), _const_spec((1, PAIR)), row, row,
                  pl.BlockSpec((None, N_HEADS, Q_BLOCK), lambda b, i: (b * nq + i, 0, 0))] + [ANY] * len(plumb.args),
        out_specs=[row, kv_out, kv_out, pair_bias, _const_spec((N_HEADS, 128)),
                   _const_spec((1, PAIR)), _const_spec((1, PAIR))] + [ANY] * len(plumb.out_shape),
        out_shape=[_sds((t, A_WIDTH), BF16), _sds((t, kw), BF16), _sds((t, kw), BF16),
                   _sds((N_HEADS // 2, wext, 2 * Q_BLOCK), F32), _sds((N_HEADS, 128), F32),
                   _sds((1, PAIR), F32), _sds((1, PAIR), F32)] + plumb.out_shape,
        scratch_shapes=[pltpu.VMEM((s, kw), BF16), pltpu.VMEM((N_HEADS, nkc, PAIR, KEY_CHUNK), BF16),
                        pltpu.VMEM((s, kw), F32), pltpu.VMEM((s, kw), F32),
                        pltpu.VMEM((w, 2 * Q_BLOCK), F32), pltpu.VMEM((w, 2 * Q_BLOCK), F32),
                        pltpu.VMEM((w, 2 * Q_BLOCK), BF16), pltpu.VMEM((w, 2 * Q_BLOCK), BF16),
                        pltpu.VMEM((2 * w, Q_BLOCK), BF16)] + plumb.scratch,
        input_output_aliases=plumb.aliases,
        compiler_params=_params(),
    )(qkv, qkv, qkv, bias2, sink2, gq, gk, y, dy, lse, *plumb.args)
    return plumb.deliver(res)


def _band_tables(prev_chunks):
    prev = prev_chunks * CHUNK
    wext = 2 * prev + Q_BLOCK
    jj = np.arange(wext)[:, None]
    ii = np.arange(Q_BLOCK)[None, :]
    dist = prev + ii - jj
    rel_chunk = (prev // CHUNK + ii // CHUNK) - jj // CHUNK
    allowed = (rel_chunk >= 0) & (rel_chunk <= prev_chunks)
    return dist, allowed


def _alibi_slopes():
    return np.array([2.0 ** (-8.0 * (h + 1) / N_HEADS) for h in range(N_HEADS)], dtype=np.float32)


def _diag_onehot(prev, wext):
    n_diag = wext + Q_BLOCK - 1
    idx = np.clip(prev + Q_BLOCK - 1 - np.arange(n_diag), -A_MAX_REL, A_MAX_REL) + A_MAX_REL
    onehot = np.zeros((n_diag, 2 * A_MAX_REL + 1), np.float32)
    onehot[np.arange(n_diag), idx] = 1.0
    return onehot


def _bias_a(rel_bias):
    prev = A_PREV_CHUNKS * CHUNK
    _, allowed = _band_tables(A_PREV_CHUNKS)
    wext = allowed.shape[0]
    n_diag = wext + Q_BLOCK - 1
    seq = jnp.dot(rel_bias, jnp.asarray(_diag_onehot(prev, wext).T), precision=lax.Precision.HIGHEST)
    seq = jnp.pad(seq, ((0, 0), (0, 1)))
    rows = jnp.broadcast_to(seq[:, None, :], (N_HEADS, Q_BLOCK, n_diag + 1)).reshape(N_HEADS, -1)
    skew = rows[:, :Q_BLOCK * n_diag].reshape(N_HEADS, Q_BLOCK, n_diag)
    tile = jnp.transpose(skew[:, :, Q_BLOCK - 1:Q_BLOCK - 1 + wext], (0, 2, 1))
    return jnp.where(jnp.asarray(allowed)[None], tile, NEG_INF)


def _bias_b():
    dist, allowed = _band_tables(B_PREV_CHUNKS)
    bias = -_alibi_slopes()[:, None, None] * np.abs(dist).astype(np.float32)[None]
    return jnp.asarray(np.where(allowed[None], bias, np.float32(NEG_INF)).astype(np.float32))


def _rel_bias_grad(db_t):
    prev = A_PREV_CHUNKS * CHUNK
    wext = db_t.shape[1]
    n_diag = wext + Q_BLOCK - 1
    wp = n_diag + Q_BLOCK - 1
    xp = jnp.pad(jnp.transpose(db_t, (0, 2, 1)), ((0, 0), (0, 0), (Q_BLOCK - 1, Q_BLOCK - 1)))
    flat = jnp.pad(xp.reshape(N_HEADS, Q_BLOCK * wp), ((0, 0), (0, Q_BLOCK)))
    skew = flat.reshape(N_HEADS, Q_BLOCK, wp + 1)[:, :, :n_diag]
    diag = jnp.sum(skew, axis=1)
    return jnp.dot(diag, jnp.asarray(_diag_onehot(prev, wext)), precision=lax.Precision.HIGHEST)


def _ew(name, fn, ins, out_dtypes):
    r, c = ins[0].shape
    rb = _pick(r, max(16, (1 << 19) // c), 16)
    spec = pl.BlockSpec((rb, c), lambda i: (i, 0))

    def body(*refs):
        vals = fn(*[ref[...] for ref in refs[:len(ins)]])
        for ref, val in zip(refs[len(ins):], vals):
            ref[...] = val.astype(ref.dtype)

    return pl.pallas_call(
        body, name=name, grid=(r // rb,), in_specs=[spec] * len(ins), out_specs=[spec] * len(out_dtypes),
        out_shape=[_sds((r, c), dt) for dt in out_dtypes], compiler_params=_params(),
    )(*ins)


def _adamw_math(w, g, m, v):
    m = ADAM_B1 * m + (1.0 - ADAM_B1) * g
    v = ADAM_B2 * v + (1.0 - ADAM_B2) * (g * g)
    m_hat = m / (1.0 - ADAM_B1 ** ADAM_STEP)
    v_hat = v / (1.0 - ADAM_B2 ** ADAM_STEP)
    delta = -ADAM_LR * (m_hat / (jnp.sqrt(v_hat) + ADAM_EPS) + ADAM_WD * w)
    return delta, m, v


def _adamw_terms(name, terms, w, m, v):
    r, c = w.shape
    hr = r // 2
    rb = _pick(hr, max(16, (1 << 19) // c), 16)
    nb = hr // rb

    def body(t_ref, w_ref, m_ref, v_ref, g_ref, d_ref, nm_ref, nv_ref):
        g = t_ref[0].astype(F32)
        for k in range(1, N_CHIPS):
            g = g + t_ref[k].astype(F32)
        delta, nm, nv = _adamw_math(w_ref[...], g, m_ref[...], v_ref[...])
        g_ref[...] = g
        d_ref[...] = delta
        nm_ref[...] = nm
        nv_ref[...] = nv

    spec = pl.BlockSpec((rb, c), lambda h, i: (h * nb + i, 0))
    return pl.pallas_call(
        body, name=name, grid=(2, nb),
        in_specs=[pl.BlockSpec((None, N_CHIPS, rb, c), lambda h, i: (h, 0, i, 0)), spec, spec, spec],
        out_specs=[spec] * 4, out_shape=[_sds((r, c), F32)] * 4, compiler_params=_params(),
    )(terms, w, m, v)


def _mesh_place():
    x, y, c = lax.axis_index("x"), lax.axis_index("y"), lax.axis_index("c")
    chips = [(x, 1 - y), (1 - x, y), (1 - x, 1 - y)]
    return x, y, c, chips


def _all_gather_weights(shards):
    n = len(shards)

    def body(*refs):
        ins, outs = refs[:n], refs[n:2 * n]
        local_sem, ici_send, ici_recv, d2d_send, d2d_recv = refs[2 * n:]
        x, y, c, chips = _mesh_place()
        me = 2 * x + y
        sibling = (x, y, 1 - c)
        local, sent = [], []
        for wi in range(n):
            loc = pltpu.make_async_copy(ins[wi], outs[wi].at[me], local_sem.at[wi])
            loc.start()
            local.append(loc)
            for k, (tx, ty) in enumerate(chips):
                for pi, rows in _rotated_pieces(shards[wi].shape[1], k):
                    sem = (wi * 3 + k) * GATHER_PIECES + pi
                    cp = pltpu.make_async_remote_copy(
                        src_ref=ins[wi].at[c, rows], dst_ref=outs[wi].at[me, c, rows],
                        send_sem=ici_send.at[sem], recv_sem=ici_recv.at[sem],
                        device_id=(tx, ty, c), device_id_type=MESH)
                    cp.start()
                    sent.append(cp)
        passed = []
        for wi in range(n):
            for k, (tx, ty) in enumerate(chips):
                for pi, rows in _rotated_pieces(shards[wi].shape[1], k):
                    sem = (wi * 3 + k) * GATHER_PIECES + pi
                    slab = outs[wi].at[2 * tx + ty, c, rows]
                    pltpu.make_async_remote_copy(
                        src_ref=slab, dst_ref=slab, send_sem=ici_send.at[sem], recv_sem=ici_recv.at[sem],
                        device_id=(tx, ty, c), device_id_type=MESH).wait_recv()
                    fw = pltpu.make_async_remote_copy(
                        src_ref=slab, dst_ref=slab, send_sem=d2d_send.at[sem], recv_sem=d2d_recv.at[sem],
                        device_id=sibling, device_id_type=MESH)
                    fw.start()
                    passed.append(fw)
        for wi in range(n):
            for k, (tx, ty) in enumerate(chips):
                for pi, rows in enumerate(_row_pieces(shards[wi].shape[1])):
                    sem = (wi * 3 + k) * GATHER_PIECES + pi
                    slab = outs[wi].at[2 * tx + ty, 1 - c, rows]
                    pltpu.make_async_remote_copy(
                        src_ref=slab, dst_ref=slab, send_sem=d2d_send.at[sem], recv_sem=d2d_recv.at[sem],
                        device_id=sibling, device_id_type=MESH).wait_recv()
        for loc in local:
            loc.wait()
        for cp in sent + passed:
            cp.wait_send()

    return pl.pallas_call(
        body, name="all_gather_weights",
        in_specs=[ANY] * n, out_specs=[ANY] * n,
        out_shape=[_sds((N_CHIPS,) + s.shape, s.dtype) for s in shards],
        scratch_shapes=[pltpu.SemaphoreType.DMA((n,))] + [pltpu.SemaphoreType.DMA((3 * n * GATHER_PIECES,))] * 4,
    )(*shards)


def _run_comms(name, comms):
    plumb = _CommPlumbing(comms, 0, 0, 0)
    n_in, n_out = len(plumb.args), len(plumb.out_shape)

    def body(*refs):
        parts = []
        i0, o0, s0 = 0, n_in, n_in + n_out
        for cm in plumb.comms:
            parts.append((refs[i0:i0 + len(cm.ins)], refs[o0:o0 + len(cm.outs)], refs[s0:s0 + len(cm.sems)]))
            i0 += len(cm.ins)
            o0 += len(cm.outs)
            s0 += len(cm.sems)
        for cm, part in zip(plumb.comms, parts):
            cm.start(*part)
        for cm, part in zip(plumb.comms, parts):
            cm.finish(*part)

    res = pl.pallas_call(
        body, name=name, in_specs=[ANY] * n_in, out_specs=[ANY] * n_out, out_shape=plumb.out_shape,
        scratch_shapes=plumb.scratch, input_output_aliases=plumb.aliases,
    )(*plumb.args)
    plumb.deliver(res)


GATHER_PIECES = 4
BF16_TILE_ROWS = 16


def _row_pieces(rows):
    n = GATHER_PIECES
    while rows % (n * BF16_TILE_ROWS):
        n //= 2
    return [pl.ds(i * (rows // n), rows // n) for i in range(n)]


def _rotated_pieces(rows, k):
    pieces = list(enumerate(_row_pieces(rows)))
    k %= len(pieces)
    return pieces[k:] + pieces[:k]


def _gather_ici(shards, targets=(0, 1, 2), into=None):
    n = len(shards)

    def copies(ins, outs, sems):
        local_sem, send_sem, recv_sem = sems
        x, y, c, chips = _mesh_place()
        me = 2 * x + y
        local, sends, recvs = [], [], []
        for wi in range(n):
            if into is None:
                local.append(pltpu.make_async_copy(ins[wi], outs[wi].at[me], local_sem.at[wi]))
            for k in targets:
                tx, ty = chips[k]
                for pi, rows in _rotated_pieces(shards[wi].shape[1], k):
                    sem = (wi * 3 + k) * GATHER_PIECES + pi
                    sems_k = dict(send_sem=send_sem.at[sem], recv_sem=recv_sem.at[sem],
                                  device_id=(tx, ty, c), device_id_type=MESH)
                    sends.append(pltpu.make_async_remote_copy(
                        src_ref=ins[wi].at[c, rows], dst_ref=outs[wi].at[me, c, rows], **sems_k))
                    slab = outs[wi].at[2 * tx + ty, c, rows]
                    recvs.append(pltpu.make_async_remote_copy(src_ref=slab, dst_ref=slab, **sems_k))
        return local, sends, recvs

    def start(ins, outs, sems):
        local, sends, _ = copies(ins, outs, sems)
        for cp in local + sends:
            cp.start()

    def finish(ins, outs, sems):
        local, sends, recvs = copies(ins, outs, sems)
        for cp in local:
            cp.wait()
        for cp in recvs:
            cp.wait_recv()
        for cp in sends:
            cp.wait_send()

    sems = [pltpu.SemaphoreType.DMA((n,)), pltpu.SemaphoreType.DMA((3 * n * GATHER_PIECES,)),
            pltpu.SemaphoreType.DMA((3 * n * GATHER_PIECES,))]
    if into is None:
        return _Comm(shards, [_sds((N_CHIPS,) + s.shape, s.dtype) for s in shards], {}, sems, start, finish)
    return _Comm(list(shards) + list(into), [_sds(g.shape, g.dtype) for g in into],
                 {n + i: i for i in range(n)}, sems, start, finish)


def _gather_d2d(gathered):
    n = len(gathered)

    def copies(outs, sems):
        send_sem, recv_sem = sems
        x, y, c, chips = _mesh_place()
        sends, recvs = [], []
        for wi in range(n):
            for k, (tx, ty) in enumerate(chips):
                sems_k = dict(send_sem=send_sem.at[wi * 3 + k], recv_sem=recv_sem.at[wi * 3 + k],
                              device_id=(x, y, 1 - c), device_id_type=MESH)
                mine = outs[wi].at[2 * tx + ty, c]
                theirs = outs[wi].at[2 * tx + ty, 1 - c]
                sends.append(pltpu.make_async_remote_copy(src_ref=mine, dst_ref=mine, **sems_k))
                recvs.append(pltpu.make_async_remote_copy(src_ref=theirs, dst_ref=theirs, **sems_k))
        return sends, recvs

    def start(ins, outs, sems):
        for cp in copies(outs, sems)[0]:
            cp.start()

    def finish(ins, outs, sems):
        sends, recvs = copies(outs, sems)
        for cp in recvs:
            cp.wait_recv()
        for cp in sends:
            cp.wait_send()

    return _Comm(gathered, [_sds(g.shape, g.dtype) for g in gathered], {i: i for i in range(n)},
                 [pltpu.SemaphoreType.DMA((3 * n,)), pltpu.SemaphoreType.DMA((3 * n,))], start, finish)


def _exchange_halves(grads):
    n = len(grads)

    def copies(ins, outs, sems):
        send_sem, recv_sem = sems
        x, y, c, _ = _mesh_place()
        return [pltpu.make_async_remote_copy(
            src_ref=ins[wi].at[t, 1 - c], dst_ref=outs[wi].at[t],
            send_sem=send_sem.at[wi * N_CHIPS + t], recv_sem=recv_sem.at[wi * N_CHIPS + t],
            device_id=(x, y, 1 - c), device_id_type=MESH) for wi in range(n) for t in range(N_CHIPS)]

    def start(ins, outs, sems):
        for cp in copies(ins, outs, sems):
            cp.start()

    def finish(ins, outs, sems):
        for cp in copies(ins, outs, sems):
            cp.wait()

    return _Comm(grads, [_sds((N_CHIPS,) + g.shape[2:], g.dtype) for g in grads], {},
                 [pltpu.SemaphoreType.DMA((N_CHIPS * n,)), pltpu.SemaphoreType.DMA((N_CHIPS * n,))], start, finish)


def _scatter_ici(sums, targets=(0, 1, 2), into=None):
    n = len(sums)

    def copies(ins, outs, sems):
        local_sem, send_sem, recv_sem = sems
        x, y, c, chips = _mesh_place()
        me = 2 * x + y
        local, sends, recvs = [], [], []
        for wi in range(n):
            if into is None:
                local.append(pltpu.make_async_copy(ins[wi].at[me], outs[wi].at[c, 0], local_sem.at[wi]))
            for k in targets:
                tx, ty = chips[k]
                sems_k = dict(send_sem=send_sem.at[wi * 3 + k], recv_sem=recv_sem.at[wi * 3 + k],
                              device_id=(tx, ty, c), device_id_type=MESH)
                land = outs[wi].at[c, k + 1]
                sends.append(pltpu.make_async_remote_copy(src_ref=ins[wi].at[2 * tx + ty], dst_ref=land, **sems_k))
                recvs.append(pltpu.make_async_remote_copy(src_ref=land, dst_ref=land, **sems_k))
        return local, sends, recvs

    def start(ins, outs, sems):
        local, sends, _ = copies(ins, outs, sems)
        for cp in local + sends:
            cp.start()

    def finish(ins, outs, sems):
        local, sends, recvs = copies(ins, outs, sems)
        for cp in local:
            cp.wait()
        for cp in recvs:
            cp.wait_recv()
        for cp in sends:
            cp.wait_send()

    sem_shapes = [pltpu.SemaphoreType.DMA((n,)), pltpu.SemaphoreType.DMA((3 * n,)), pltpu.SemaphoreType.DMA((3 * n,))]
    if into is None:
        return _Comm(sums, [_sds((2, N_CHIPS) + s.shape[1:], s.dtype) for s in sums], {}, sem_shapes, start, finish)
    return _Comm(list(sums) + list(into), [_sds(t.shape, t.dtype) for t in into],
                 {n + i: i for i in range(n)}, sem_shapes, start, finish)


def _scatter_d2d(terms):
    n = len(terms)

    def copies(outs, sems):
        send_sem, recv_sem = sems
        x, y, c, _ = _mesh_place()
        sends, recvs = [], []
        for wi in range(n):
            sems_w = dict(send_sem=send_sem.at[wi], recv_sem=recv_sem.at[wi],
                          device_id=(x, y, 1 - c), device_id_type=MESH)
            sends.append(pltpu.make_async_remote_copy(src_ref=outs[wi].at[c], dst_ref=outs[wi].at[c], **sems_w))
            recvs.append(pltpu.make_async_remote_copy(src_ref=outs[wi].at[1 - c], dst_ref=outs[wi].at[1 - c], **sems_w))
        return sends, recvs

    def start(ins, outs, sems):
        for cp in copies(outs, sems)[0]:
            cp.start()

    def finish(ins, outs, sems):
        sends, recvs = copies(outs, sems)
        for cp in recvs:
            cp.wait_recv()
        for cp in sends:
            cp.wait_send()

    return _Comm(terms, [_sds(t.shape, t.dtype) for t in terms], {i: i for i in range(n)},
                 [pltpu.SemaphoreType.DMA((n,)), pltpu.SemaphoreType.DMA((n,))], start, finish)


def _chip_sum(name, grad, got, core):
    _, _, hr, c = grad.shape
    rb = _pick(hr, max(16, (1 << 19) // c), 16)

    def body(core_ref, a_ref, b_ref, o_ref):
        o_ref[...] = (a_ref[...].astype(F32) + b_ref[...].astype(F32)).astype(BF16)

    out_spec = pl.BlockSpec((None, rb, c), lambda t, i, core_ref: (t, i, 0))
    return pl.pallas_call(
        body, name=name,
        grid_spec=pltpu.PrefetchScalarGridSpec(
            num_scalar_prefetch=1, grid=(N_CHIPS, hr // rb),
            in_specs=[pl.BlockSpec((None, None, rb, c), lambda t, i, core_ref: (t, core_ref[0], i, 0)), out_spec],
            out_specs=out_spec),
        out_shape=_sds((N_CHIPS, hr, c), BF16), compiler_params=_params(),
    )(core, grad, got)


def _all_reduce_small(pack):
    r = pack.shape[0]

    def body(p_ref, o_ref, land_ref, send_sem, recv_sem):
        x, y, c, _ = _mesh_place()
        me = 4 * x + 2 * y + c
        flips = [(k >> 2 & 1, k >> 1 & 1, k & 1) for k in range(1, N_DEV)]

        def peer(fx, fy, fc):
            return (1 - x if fx else x, 1 - y if fy else y, 1 - c if fc else c)

        land_ref[me] = p_ref[...]
        sent = []
        for k, flip in enumerate(flips):
            cp = pltpu.make_async_remote_copy(
                src_ref=p_ref, dst_ref=land_ref.at[me], send_sem=send_sem.at[k], recv_sem=recv_sem.at[k],
                device_id=peer(*flip), device_id_type=MESH)
            cp.start()
            sent.append(cp)
        for k, flip in enumerate(flips):
            px, py, pc = peer(*flip)
            slot = land_ref.at[4 * px + 2 * py + pc]
            pltpu.make_async_remote_copy(
                src_ref=slot, dst_ref=slot, send_sem=send_sem.at[k], recv_sem=recv_sem.at[k],
                device_id=(px, py, pc), device_id_type=MESH).wait_recv()
        total = land_ref[0]
        for d in range(1, N_DEV):
            total = total + land_ref[d]
        o_ref[...] = total
        for cp in sent:
            cp.wait_send()

    vmem = pl.BlockSpec(memory_space=pltpu.VMEM)
    return pl.pallas_call(
        body, name="all_reduce_small", in_specs=[vmem], out_specs=vmem, out_shape=_sds((r, 128), F32),
        scratch_shapes=[pltpu.VMEM((N_DEV, r, 128), F32), pltpu.SemaphoreType.DMA((N_DEV - 1,)),
                        pltpu.SemaphoreType.DMA((N_DEV - 1,))],
    )(pack)


PACK_TILE = 8 * 128


def _pack(items):
    rows, i = [], 0
    while i < len(items):
        j = i
        while j < len(items) and items[j].size == items[i].size:
            j += 1
        group = jnp.stack([it.reshape(-1).astype(F32) for it in items[i:j]])
        rows.append(jnp.pad(group, ((0, 0), (0, -group.shape[1] % PACK_TILE))).reshape(-1, 128))
        i = j
    return jnp.concatenate(rows, axis=0)


def _unpack(pack, shapes):
    out, row = [], 0
    for shp in shapes:
        size = int(np.prod(shp))
        nrow = -(-size // PACK_TILE) * (PACK_TILE // 128)
        out.append(pack[row:row + nrow].reshape(-1)[:size].reshape(shp))
        row += nrow
    return out


BIG = ["ffn1_w_gu", "ffn1_w_down", "w_in", "w_gate", "w_proj_a", "w_proj_b", "w_out",
       "ffn2_w_gu", "ffn2_w_down", "w_ple_gate", "w_ple_proj"]
SMALL = ["ffn1_norm", "mix_norm", "ffn2_norm", "ple_norm", "a_q_norm", "a_k_norm", "b_q_norm", "b_k_norm",
         "a_rel_bias", "b_sinks"]
WEIGHTS = ["ffn1_norm", "ffn1_w_gu", "ffn1_w_down", "mix_norm", "w_in", "a_q_norm", "a_k_norm", "a_rel_bias",
           "b_q_norm", "b_k_norm", "b_sinks", "w_gate", "w_proj_a", "w_proj_b", "w_out", "ffn2_norm",
           "ffn2_w_gu", "ffn2_w_down", "ple_norm", "w_ple_gate", "w_ple_proj"]
ATTN_A = dict(prev=A_PREV_CHUNKS * CHUNK, group=1, kw=A_WIDTH, qblk=0, kblk=1, vblk=2)
ATTN_B = dict(prev=B_PREV_CHUNKS * CHUNK, group=N_HEADS // B_KV_HEADS, kw=B_KV_WIDTH, qblk=3,
              kblk=4 * A_WIDTH // B_KV_WIDTH, vblk=4 * A_WIDTH // B_KV_WIDTH + 1)


def _cast_epilogue(accs, extras, outs, ij):
    for acc, out in zip(accs, outs):
        out[...] = acc.astype(out.dtype)


GATHER_FIRST = ["ffn1_w_gu", "ffn1_w_down"]
ROW_SHARDED = ("ffn1_w_down", "ffn2_w_down", "w_out", "w_ple_gate")


def _slotted(name, grad):
    if name == "w_in":
        rows, cols = grad.shape
        grad = jnp.transpose(grad.reshape(rows, N_CHIPS, cols // N_CHIPS), (1, 0, 2))
    elif name in ROW_SHARDED:
        grad = grad.reshape(N_CHIPS, grad.shape[0] // N_CHIPS, grad.shape[1])
    return grad.reshape(N_CHIPS, 2, grad.shape[1] // 2, grad.shape[2])


def _local_step(xt, pt, tgt, n_batch, shards, small, core):
    t, d = xt.shape
    tm = _pick(t, ROW_TILE, 8)
    tk = _pick(t, ROW_TILE, 8)
    nt = t // tm
    row = pl.BlockSpec((tm, d), lambda i, j, k: (i, 0))
    gs = shards["w_gate"].shape[1]
    ps = shards["w_proj_a"].shape[1]
    es = shards["w_ple_proj"].shape[1]
    pdim = pt.shape[1]
    ncols = N_CHIPS * shards["w_in"].shape[1]
    tin = ncols // 2
    assert 2 * gs == d and 4 * ps == d and 4 * es == d and tin % 128 == 0

    w = {}
    halves = {n: s.reshape(2, s.shape[0] // 2, s.shape[1]) for n, s in shards.items()}

    def publish(names, arrays):
        for name, g in zip(names, arrays):
            g = g.reshape(N_CHIPS, 2 * g.shape[2], g.shape[3])
            if name in ROW_SHARDED:
                g = g.reshape(N_CHIPS * g.shape[1], g.shape[2])
            elif name == "w_in":
                g = jnp.transpose(g, (1, 0, 2)).reshape(g.shape[1], N_CHIPS * g.shape[2])
            w[name] = g

    class GatherPipe:
        def __init__(self, names):
            self.names = names

        def ici(self, targets=(0, 1, 2)):
            self.first = _gather_ici([halves[n] for n in self.names], targets)
            return self.first

        def ici_more(self, targets):
            self.first = _gather_ici([halves[n] for n in self.names], targets, into=self.first.results)
            return self.first

        def d2d(self):
            self.second = _gather_d2d(self.first.results)
            return self.second

        def publish(self):
            publish(self.names, self.second.results)

    class GradPipe:
        def __init__(self, names):
            self.names = names

        def exchange(self, grads):
            self.grads = [_slotted(n, g) for n, g in zip(self.names, grads)]
            self.x = _exchange_halves(self.grads)
            return self.x

        def scatter(self, targets=(0, 1, 2)):
            self.sums = [_chip_sum("chip_sum_" + n, g, got, core)
                         for n, g, got in zip(self.names, self.grads, self.x.results)]
            self.s = _scatter_ici(self.sums, targets)
            return self.s

        def scatter_more(self, targets):
            self.s = _scatter_ici(self.sums, targets, into=self.s.results)
            return self.s

        def forward(self):
            self.f = _scatter_d2d(self.s.results)
            return self.f

        def terms(self):
            return dict(zip(self.names, self.f.results))

    publish(GATHER_FIRST, _all_gather_weights([halves[n] for n in GATHER_FIRST]))
    g_in, g_proj, g_ple = GatherPipe(["w_in", "w_gate"]), GatherPipe(["w_proj_a", "w_proj_b", "w_out"]), \
        GatherPipe(["w_ple_gate", "w_ple_proj"])
    g_down2, g_up2 = GatherPipe(["ffn2_w_down"]), GatherPipe(["ffn2_w_gu"])
    h1, ffn1_saved = _ffn_fwd("ffn1", xt, small["ffn1_norm"], w["ffn1_w_gu"], w["ffn1_w_down"],
                              {"up": lambda: [g_in.ici()], "down": lambda: [g_in.d2d(), g_proj.ici()]})
    g_in.publish()
    w_in, wgate = w["w_in"], w["w_gate"]
    un = _rms_fwd("mix_norm", h1, small["mix_norm"])
    (qkv,) = _mm(
        "qkv", "nn", (nt, 2, 1),
        [(un, row, w_in, pl.BlockSpec((d, tin), lambda i, j, k: (0, j)))], [],
        [(_sds((t, ncols), BF16), pl.BlockSpec((tm, tin), lambda i, j, k: (i, j)))], (tm, tin), _cast_epilogue,
        j_outer=True, comms=[g_proj.d2d(), g_ple.ici()])
    g_proj.publish()
    wpa, wpb, wout = w["w_proj_a"], w["w_proj_b"], w["w_out"]

    def gate_epilogue(accs, extras, outs, ij):
        outs[0][...] = jax.nn.sigmoid(accs[0]).astype(BF16)

    (gates,) = _mm(
        "gate", "nn", (nt, 4, 1),
        [(un, row, wgate, pl.BlockSpec((None, d, gs), lambda i, j, k: (j, 0, 0)))], [],
        [(_sds((2, t, d), BF16), pl.BlockSpec((None, tm, gs), lambda i, j, k: (j // 2, i, j % 2)))],
        (tm, gs), gate_epilogue, j_outer=True, chunked=True, comms=[g_ple.d2d(), g_down2.ici()])
    g_ple.publish()
    wpg, wpe = w["w_ple_gate"], w["w_ple_proj"]

    bias_a = _pair_bias(_bias_a(small["a_rel_bias"][0]))
    bias_b = _pair_bias(_bias_b())
    sink_a = _pair_rows(jnp.full((N_HEADS, 128), NEG_INF, F32))
    sink_b = _pair_rows(jnp.broadcast_to(small["b_sinks"][0][:, None], (N_HEADS, 128)))
    gqa, gka, gqb, gkb = [jnp.tile(small[k], (1, 2)) for k in ("a_q_norm", "a_k_norm", "b_q_norm", "b_k_norm")]
    ya, lse_a = _attn_fwd("attn_a_fwd", qkv, bias_a, sink_a, gqa, gka, ATTN_A, n_batch,
                          comms=[g_down2.d2d(), g_up2.ici(targets=(0, 1))])
    g_down2.publish()
    yb, lse_b = _attn_fwd("attn_b_fwd", qkv, bias_b, sink_b, gqb, gkb, ATTN_B, n_batch,
                          comms=[g_up2.ici_more(targets=(2,))])

    def merge_epilogue(accs, extras, outs, ij):
        pa, pb = accs
        outs[0][...] = (extras[0][...].astype(F32) * pa + extras[1][...].astype(F32) * pb).astype(BF16)
        outs[1][...] = pa.astype(BF16)
        outs[2][...] = pb.astype(BF16)

    y_spec = pl.BlockSpec((tm, A_WIDTH), lambda i, j, k: (i, 0))
    proj_spec = pl.BlockSpec((None, A_WIDTH, ps), lambda i, j, k: (j, 0, 0))
    tile_ps = pl.BlockSpec((tm, ps), lambda i, j, k: (i, j))
    merged, pa, pb = _mm(
        "proj_merge", "nn", (nt, 4, 1),
        [(ya, y_spec, wpa, proj_spec), (yb, y_spec, wpb, proj_spec)],
        [(gates, pl.BlockSpec((None, tm, ps), lambda i, j, k: (0, i, j))),
         (gates, pl.BlockSpec((None, tm, ps), lambda i, j, k: (1, i, j)))],
        [(_sds((t, d), BF16), tile_ps)] * 3, (tm, ps), merge_epilogue, comms=[g_up2.d2d()])
    g_up2.publish()

    def residual_epilogue(accs, extras, outs, ij):
        outs[0][...] = extras[0][...] + accs[0]

    (h2,) = _mm(
        "out_proj", "nn", (nt, 1, 1),
        [(merged, row, wout, pl.BlockSpec((d, d), lambda i, j, k: (0, 0)))],
        [(h1, row)], [(_sds((t, d), F32), row)], (tm, d), residual_epilogue)

    h3, ffn2_saved = _ffn_fwd("ffn2", h2, small["ffn2_norm"], w["ffn2_w_gu"], w["ffn2_w_down"], {})
    n3 = _rms_fwd("ple_norm", h3, small["ple_norm"])
    tile_es = pl.BlockSpec((tm, es), lambda i, j, k: (i, j))
    (pe,) = _mm(
        "ple_embed", "nn", (nt, 4, 1),
        [(pt, pl.BlockSpec((tm, pdim), lambda i, j, k: (i, 0)), wpe, pl.BlockSpec((None, pdim, es), lambda i, j, k: (j, 0, 0)))],
        [], [(_sds((t, d), F32), tile_es)], (tm, es), _cast_epilogue)

    th = _pick(d, 512)

    def head_epilogue(accs, extras, outs, ij):
        h3_ref, pe_ref, tgt_ref = extras
        dy_ref, dpe_ref, dz_ref, loss_ref = outs
        pg = jax.nn.sigmoid(accs[0])
        pev = pe_ref[...]
        diff = h3_ref[...] + pg * pev - tgt_ref[...]
        dy = diff * (1.0 / d)
        dy_ref[...] = dy
        dpe_ref[...] = (dy * pg).astype(BF16)
        dz_ref[...] = (dy * pev * pg * (1.0 - pg)).astype(BF16)
        _accumulate(loss_ref, jnp.full(loss_ref.shape, jnp.sum(diff * diff), F32), (ij[0] == 0) & (ij[1] == 0))

    tile_h = pl.BlockSpec((tm, th), lambda i, j, k: (i, j))
    dy, dpe, dz, loss_acc = _mm(
        "ple_gate_loss", "nn", (nt, d // th, 1),
        [(n3, row, wpg, pl.BlockSpec((d, th), lambda i, j, k: (0, j)))],
        [(h3, tile_h), (pe, tile_h), (tgt, tile_h)],
        [(_sds((t, d), F32), tile_h), (_sds((t, d), BF16), tile_h), (_sds((t, d), BF16), tile_h),
         (_sds((8, 128), F32), pl.BlockSpec((8, 128), lambda i, j, k: (0, 0)))],
        (tm, th), head_epilogue, j_outer=True, chunked=True)
    loss = 0.5 * loss_acc[0, 0] / d

    nk = t // tk
    (dwpe,) = _mm(
        "d_w_ple_proj", "tn", (1, 4, nk),
        [(pt, pl.BlockSpec((tk, pdim), lambda i, j, k: (k, 0)), dpe, pl.BlockSpec((tk, es), lambda i, j, k: (k, j)))],
        [], [(_sds((4, pdim, es), BF16), pl.BlockSpec((None, pdim, es), lambda i, j, k: (j, 0, 0)))],
        (pdim, es), _cast_epilogue)

    def dense_grad(name, a, dyb, comms=()):
        (res,) = _mm(
            name, "tn", (1, d // th, nk),
            [(a, pl.BlockSpec((tk, d), lambda i, j, k: (k, 0)), dyb, pl.BlockSpec((tk, th), lambda i, j, k: (k, j)))],
            [], [(_sds((d, d), BF16), pl.BlockSpec((d, th), lambda i, j, k: (0, j)))], (d, th), _cast_epilogue,
            comms=comms)
        return res

    dwpg = dense_grad("d_w_ple_gate", n3, dz)
    tmn = _pick(t, ROW_TILE, 8)
    extras, outs = _rms_bwd_io(h3, small["ple_norm"], dy, tmn)
    dh3, dh3_b, d_ple_norm = _mm(
        "d_ple_norm", "nt", (t // tmn, 1, 1),
        [(dz, pl.BlockSpec((tmn, d), lambda i, j, k: (i, 0)), wpg, pl.BlockSpec((d, d), lambda i, j, k: (0, 0)))],
        extras, outs, (tmn, d), _rms_bwd_epilogue)

    up2, down2, ple = GradPipe(["ffn2_w_gu"]), GradPipe(["ffn2_w_down"]), GradPipe(["w_ple_gate", "w_ple_proj"])
    proj = GradPipe(["w_proj_a", "w_proj_b", "w_out"])
    dh2, dh2_b, d_ffn2_norm, dwgu2, dwd2 = _ffn_bwd(
        "ffn2", dh3, dh3_b, h2, small["ffn2_norm"], w["ffn2_w_gu"], w["ffn2_w_down"], ffn2_saved,
        {"dnorm": lambda dwgu, dwd: [up2.exchange([dwgu]), down2.exchange([dwd]), ple.exchange([dwpg, dwpe])]})

    def dmerge_epilogue(accs, extras, outs, ij):
        dmo = accs[0]
        g_ref, pa_ref, pb_ref = extras
        dg_ref, dpa_ref, dpb_ref = outs
        ga = g_ref[0].astype(F32)
        gb = g_ref[1].astype(F32)
        dg_ref[0] = (dmo * pa_ref[...].astype(F32) * ga * (1.0 - ga)).astype(BF16)
        dg_ref[1] = (dmo * pb_ref[...].astype(F32) * gb * (1.0 - gb)).astype(BF16)
        dpa_ref[...] = (dmo * ga).astype(BF16)
        dpb_ref[...] = (dmo * gb).astype(BF16)

    g_spec = pl.BlockSpec((2, tm, th), lambda i, j, k: (0, i, j))
    dgates, dpa, dpb = _mm(
        "d_merge", "nt", (nt, d // th, 1),
        [(dh2_b, row, wout, pl.BlockSpec((th, d), lambda i, j, k: (j, 0)))],
        [(gates, g_spec), (pa, tile_h), (pb, tile_h)],
        [(_sds((2, t, d), BF16), g_spec), (_sds((t, d), BF16), tile_h), (_sds((t, d), BF16), tile_h)],
        (tm, th), dmerge_epilogue, j_outer=True, chunked=True, comms=[down2.scatter()])
    dwout = dense_grad("d_w_out", merged, dh2_b, comms=[down2.forward(), ple.scatter()])

    yk_spec = pl.BlockSpec((tk, A_WIDTH), lambda i, j, k: (k, 0))
    dk_spec = pl.BlockSpec((tk, ps), lambda i, j, k: (k, j))
    dproj = (_sds((4, A_WIDTH, ps), BF16), proj_spec)
    dwpa, dwpb = _mm(
        "d_w_proj", "tn", (1, 4, nk),
        [(ya, yk_spec, dpa, dk_spec), (yb, yk_spec, dpb, dk_spec)], [], [dproj, dproj], (A_WIDTH, ps), _cast_epilogue,
        comms=[ple.forward()])
    dproj_a = pl.BlockSpec((tm, ps), lambda i, j, k: (i, k))
    wproj_k = pl.BlockSpec((None, A_WIDTH, ps), lambda i, j, k: (k, 0, 0))
    dya, dyb = _mm(
        "d_attn_out", "nt", (nt, 1, 4),
        [(dpa, dproj_a, wpa, wproj_k), (dpb, dproj_a, wpb, wproj_k)], [],
        [(_sds((t, A_WIDTH), BF16), y_spec)] * 2, (tm, A_WIDTH), _cast_epilogue,
        comms=[proj.exchange([dwpa, dwpb, dwout])])

    dqa, dka, dva, dbias_a, _, dgqa, dgka = _attn_bwd(
        "attn_a_bwd", qkv, bias_a, sink_a, gqa, gka, ya, dya, lse_a, ATTN_A, n_batch, True,
        comms=[up2.scatter(), proj.scatter()])
    dqb, dkb, dvb, _, dsink_b, dgqb, dgkb = _attn_bwd(
        "attn_b_bwd", qkv, bias_b, sink_b, gqb, gkb, yb, dyb, lse_b, ATTN_B, n_batch, False,
        comms=[up2.forward(), proj.forward()])
    dqkv = jnp.concatenate([dqa, dka, dva, dqb, dkb, dvb], axis=1)

    (dwgate,) = _mm(
        "d_w_gate", "tn", (1, 4, nk),
        [(un, pl.BlockSpec((tk, d), lambda i, j, k: (k, 0)),
          dgates, pl.BlockSpec((None, tk, gs), lambda i, j, k: (j // 2, k, j % 2)))],
        [], [(_sds((4, d, gs), BF16), pl.BlockSpec((None, d, gs), lambda i, j, k: (j, 0, 0)))], (d, gs), _cast_epilogue)
    (dwin,) = _mm(
        "d_w_in", "tn", (1, 2, nk),
        [(un, pl.BlockSpec((tk, d), lambda i, j, k: (k, 0)), dqkv, pl.BlockSpec((tk, tin), lambda i, j, k: (k, j)))],
        [], [(_sds((d, ncols), BF16), pl.BlockSpec((d, tin), lambda i, j, k: (0, j)))], (d, tin), _cast_epilogue)

    mixer = GradPipe(["w_in", "w_gate"])
    extras, outs = _rms_bwd_io(h1, small["mix_norm"], dh2, tmn)
    dh1, dh1_b, d_mix_norm = _mm(
        "d_mix_norm", "nt", (t // tmn, 1, 6),
        [(dgates, pl.BlockSpec((None, tmn, gs), lambda i, j, k: (jnp.minimum(k, 3) // 2, i, jnp.minimum(k, 3) % 2)),
          wgate, pl.BlockSpec((None, d, gs), lambda i, j, k: (jnp.minimum(k, 3), 0, 0))),
         (dqkv, pl.BlockSpec((tmn, tin), lambda i, j, k: (i, jnp.maximum(k - 4, 0))),
          w_in, pl.BlockSpec((d, tin), lambda i, j, k: (0, jnp.maximum(k - 4, 0))))],
        extras, outs, (tmn, d), _rms_bwd_epilogue, steps=[4, 2],
        comms=[mixer.exchange([dwin, dwgate])])

    up1 = GradPipe(["ffn1_w_gu"])
    down1 = GradPipe(["ffn1_w_down"])
    dx, _, d_ffn1_norm, _, _ = _ffn_bwd(
        "ffn1", dh1, dh1_b, xt, small["ffn1_norm"], w["ffn1_w_gu"], w["ffn1_w_down"], ffn1_saved,
        {"dwgu": lambda: [mixer.scatter()],
         "dwd": lambda dwgu: [mixer.forward(), up1.exchange([dwgu])],
         "dnorm": lambda dwgu, dwd: [up1.scatter(), down1.exchange([dwd])]})
    _run_comms("grad_tail_scatter", [up1.forward(), down1.scatter()])
    _run_comms("grad_tail_forward", [down1.forward()])
    terms = {}
    for pipe in (up2, down2, ple, proj, mixer, up1, down1):
        terms.update(pipe.terms())

    def fold(v):
        return v[0, :HEAD_DIM] + v[0, HEAD_DIM:]

    small_grads = {"ffn1_norm": d_ffn1_norm, "mix_norm": d_mix_norm, "ffn2_norm": d_ffn2_norm,
                   "ple_norm": d_ple_norm, "a_q_norm": fold(dgqa), "a_k_norm": fold(dgka),
                   "b_q_norm": fold(dgqb), "b_k_norm": fold(dgkb), "a_rel_bias": _rel_bias_grad(_unpair_bias(dbias_a)),
                   "b_sinks": jnp.sum(dsink_b, axis=1)}
    return loss, dx, terms, small_grads


def kernel(x, p, ffn1_norm, ffn1_w_gu, ffn1_w_down, mix_norm, w_in, a_q_norm, a_k_norm, a_rel_bias, b_q_norm, b_k_norm, b_sinks, w_gate, w_proj_a, w_proj_b, w_out, ffn2_norm, ffn2_w_gu, ffn2_w_down, ple_norm, w_ple_gate, w_ple_proj, loss_target, m_ffn1_norm, m_ffn1_w_gu, m_ffn1_w_down, m_mix_norm, m_w_in, m_a_q_norm, m_a_k_norm, m_a_rel_bias, m_b_q_norm, m_b_k_norm, m_b_sinks, m_w_gate, m_w_proj_a, m_w_proj_b, m_w_out, m_ffn2_norm, m_ffn2_w_gu, m_ffn2_w_down, m_ple_norm, m_w_ple_gate, m_w_ple_proj, v_ffn1_norm, v_ffn1_w_gu, v_ffn1_w_down, v_mix_norm, v_w_in, v_a_q_norm, v_a_k_norm, v_a_rel_bias, v_b_q_norm, v_b_k_norm, v_b_sinks, v_w_gate, v_w_proj_a, v_w_proj_b, v_w_out, v_ffn2_norm, v_ffn2_w_gu, v_ffn2_w_down, v_ple_norm, v_w_ple_gate, v_w_ple_proj):
    given = dict(locals())
    n_batch, s, d = x.shape
    t = n_batch * s
    xt = x.reshape(t, d)
    pt = p.reshape(t, p.shape[-1])
    tgt = loss_target.reshape(t, d)

    shards = {}
    for name in BIG:
        (shards[name],) = _ew("cast_" + name, lambda v: (v,), [given[name][0]], [BF16])
    small = {name: given[name] for name in SMALL}
    core = lax.axis_index("c").astype(jnp.int32).reshape(1)
    loss, dx, terms, small_grads = _local_step(xt, pt, tgt, n_batch, shards, small, core)

    grads, deltas, new_m, new_v = {}, {}, {}, {}
    for name in BIG:
        gw, dl, nm, nv = _adamw_terms("adamw_" + name, terms[name], given[name][0], given["m_" + name][0],
                                      given["v_" + name][0])
        grads[name], deltas[name], new_m[name], new_v[name] = gw[None], dl[None], nm[None], nv[None]

    small_shapes = [given[name].shape for name in SMALL] + [()]
    g_pack = _all_reduce_small(_pack([small_grads[name] for name in SMALL] + [loss]))
    zero = jnp.zeros((), F32)
    w_pack = _pack([given[name] for name in SMALL] + [zero])
    m_pack = _pack([given["m_" + name] for name in SMALL] + [zero])
    v_pack = _pack([given["v_" + name] for name in SMALL] + [zero])
    d_pack, nm_pack, nv_pack = _ew("adamw_small", lambda wv, gv, mv, vv: _adamw_math(wv, gv, mv, vv),
                                   [w_pack, g_pack, m_pack, v_pack], [F32] * 3)
    g_small = _unpack(g_pack, small_shapes)
    loss_total = g_small[-1]
    for name, gv, dv, mv, vv in zip(SMALL, g_small, _unpack(d_pack, small_shapes), _unpack(nm_pack, small_shapes),
                                    _unpack(nv_pack, small_shapes)):
        grads[name], deltas[name], new_m[name], new_v[name] = gv, dv, mv, vv

    return (loss_total, dx.reshape(x.shape), *[grads[n] for n in WEIGHTS], *[deltas[n] for n in WEIGHTS],
            *[new_m[n] for n in WEIGHTS], *[new_v[n] for n in WEIGHTS])
```

```python
import functools

import numpy as np
import jax
import jax.numpy as jnp
from jax import lax
from jax.experimental import pallas as pl
from jax.experimental.pallas import tpu as pltpu

F32 = jnp.float32
BF16 = jnp.bfloat16

CHUNK = 64
HEAD_DIM = 64
A_PREV_CHUNKS = 8
A_MAX_REL = 128
N_HEADS = 8
B_KV_HEADS = 2
B_PREV_CHUNKS = 2
A_WIDTH = N_HEADS * HEAD_DIM
B_KV_WIDTH = B_KV_HEADS * HEAD_DIM
EPS = 1e-6
NEG_INF = -1e30
ATTN_SCALE = HEAD_DIM ** -0.5
Q_BLOCK = 128
PAIR = 2 * HEAD_DIM

ADAM_LR = 0.001
ADAM_B1 = 0.9
ADAM_B2 = 0.999
ADAM_EPS = 1e-08
ADAM_WD = 0.01
ADAM_STEP = 10

N_CHIPS = 4
N_DEV = 8
VMEM_LIMIT_V7X = 56 * 1024 * 1024
ROW_TILE = 1024
MESH = pl.DeviceIdType.MESH
ANY = pl.BlockSpec(memory_space=pl.ANY)

_DN = {
    "nn": (((1,), (0,)), ((), ())),
    "nt": (((1,), (1,)), ((), ())),
    "tn": (((0,), (0,)), ((), ())),
}


def _pick(n, target, mult=128):
    best = None
    for d in range(mult, min(n, target) + 1, mult):
        if n % d == 0:
            best = d
    return n if best is None else best


def _dot(a, b, mode):
    return lax.dot_general(a.astype(BF16), b.astype(BF16), _DN[mode], preferred_element_type=F32)


def _params():
    return pltpu.CompilerParams(vmem_limit_bytes=VMEM_LIMIT_V7X)


class _Comm:
    def __init__(self, ins, outs, aliases, sems, start, finish):
        self.ins, self.outs, self.aliases, self.sems = list(ins), list(outs), dict(aliases), list(sems)
        self.start, self.finish = start, finish
        self.results = None


class _CommPlumbing:
    def __init__(self, comms, n_in, n_out, n_scratch):
        self.comms = list(comms)
        self.n_in, self.n_out, self.n_scratch = n_in, n_out, n_scratch
        self.args = [a for cm in self.comms for a in cm.ins]
        self.out_shape = [o for cm in self.comms for o in cm.outs]
        self.scratch = [s for cm in self.comms for s in cm.sems]
        self.aliases = {}
        i0, o0 = n_in, n_out
        for cm in self.comms:
            for a, b in cm.aliases.items():
                self.aliases[i0 + a] = o0 + b
            i0 += len(cm.ins)
            o0 += len(cm.outs)

    def _parts(self, in_refs, out_refs, scratch_refs):
        parts = []
        i0, o0, s0 = self.n_in, self.n_out, self.n_scratch
        for cm in self.comms:
            parts.append((in_refs[i0:i0 + len(cm.ins)], out_refs[o0:o0 + len(cm.outs)],
                          scratch_refs[s0:s0 + len(cm.sems)]))
            i0 += len(cm.ins)
            o0 += len(cm.outs)
            s0 += len(cm.sems)
        return parts

    def start_at(self, in_refs, out_refs, scratch_refs, first):
        if self.comms:
            parts = self._parts(in_refs, out_refs, scratch_refs)

            @pl.when(first)
            def _():
                for cm, part in zip(self.comms, parts):
                    cm.start(*part)

    def finish_at(self, in_refs, out_refs, scratch_refs, last):
        if self.comms:
            parts = self._parts(in_refs, out_refs, scratch_refs)

            @pl.when(last)
            def _():
                for cm, part in zip(self.comms, parts):
                    cm.finish(*part)

    def deliver(self, results):
        o0 = self.n_out
        for cm in self.comms:
            cm.results = list(results[o0:o0 + len(cm.outs)])
            o0 += len(cm.outs)
        return list(results[:self.n_out])


def _swap_ij(spec):
    index_map = spec.index_map
    return pl.BlockSpec(spec.block_shape, lambda j, i, k: index_map(i, j, k))


MXU_COLUMNS_V7X = 256


def _mm(name, mode, grid, pairs, extras, outs, acc_shape, epilogue, steps=None, comms=(), j_outer=False,
        chunked=False):
    ni, nj, nk = grid
    n_in = 2 * len(pairs) + len(extras)
    n_out = len(outs)
    tn = acc_shape[1]
    col_chunks = None
    if chunked:
        assert nk == 1 and steps is None and mode in ("nn", "nt")
        col_chunks = [(c0, min(MXU_COLUMNS_V7X, tn - c0)) for c0 in range(0, tn, MXU_COLUMNS_V7X)]
    n_acc = 0 if chunked else (len(pairs) if steps is None else 1)
    plumb = _CommPlumbing(comms, n_in, n_out, n_acc)
    n_all_in = n_in + len(plumb.args)
    n_all_out = n_out + len(plumb.out_shape)
    if j_outer:
        grid = (nj, ni, nk)
        pairs = [(a, _swap_ij(a_spec), b, _swap_ij(b_spec)) for a, a_spec, b, b_spec in pairs]
        extras = [(e, _swap_ij(e_spec)) for e, e_spec in extras]
        outs = [(o, _swap_ij(o_spec)) for o, o_spec in outs]

    def body(*refs):
        in_refs = refs[:n_all_in]
        out_refs = refs[n_all_in:n_all_in + n_all_out]
        scratch = refs[n_all_in + n_all_out:]
        accs = scratch[:n_acc]
        i = pl.program_id(1 if j_outer else 0)
        j = pl.program_id(0 if j_outer else 1)
        k = pl.program_id(2)
        plumb.start_at(in_refs, out_refs, scratch, (i == 0) & (j == 0) & (k == 0))

        def contrib(p, acc):
            acc[...] += _dot(in_refs[2 * p][...], in_refs[2 * p + 1][...], mode)

        if col_chunks:
            def cols(ref, c0, cs):
                if ref.shape[-1] != tn:
                    return ref
                return ref.at[(slice(None),) * (len(ref.shape) - 1) + (pl.ds(c0, cs),)]

            lhs = [in_refs[2 * p][...] for p in range(len(pairs))]
            for ci, (c0, cs) in enumerate(col_chunks):
                vals = []
                for p in range(len(pairs)):
                    b_ref = in_refs[2 * p + 1]
                    rhs = b_ref[:, c0:c0 + cs] if mode == "nn" else b_ref[c0:c0 + cs, :]
                    vals.append(_dot(lhs[p], rhs, mode))
                epilogue(vals, [cols(r, c0, cs) for r in in_refs[2 * len(pairs):n_in]],
                         [cols(r, c0, cs) for r in out_refs[:n_out]], (i, j * len(col_chunks) + ci))
        else:
            @pl.when(k == 0)
            def _():
                for acc in accs:
                    acc[...] = jnp.zeros(acc.shape, F32)

            if steps is None:
                for p in range(len(pairs)):
                    contrib(p, accs[p])
            else:
                lo = 0
                for p, n in enumerate(steps):
                    pl.when((k >= lo) & (k < lo + n))(functools.partial(contrib, p, accs[0]))
                    lo += n

            @pl.when(k == nk - 1)
            def _():
                epilogue([acc[...] for acc in accs], in_refs[2 * len(pairs):n_in], out_refs[:n_out], (i, j))

        plumb.finish_at(in_refs, out_refs, scratch, (i == ni - 1) & (j == nj - 1) & (k == nk - 1))

    args, in_specs = [], []
    for a, a_spec, b, b_spec in pairs:
        args += [a, b]
        in_specs += [a_spec, b_spec]
    for e, e_spec in extras:
        args.append(e)
        in_specs.append(e_spec)
    res = pl.pallas_call(
        body,
        name=name,
        grid=grid,
        in_specs=in_specs + [ANY] * len(plumb.args),
        out_specs=[s for _, s in outs] + [ANY] * len(plumb.out_shape),
        out_shape=[o for o, _ in outs] + plumb.out_shape,
        scratch_shapes=[pltpu.VMEM(acc_shape, F32) for _ in range(n_acc)] + plumb.scratch,
        input_output_aliases=plumb.aliases,
        compiler_params=_params(),
    )(*args, *plumb.args)
    return plumb.deliver(res)


def _sds(shape, dtype):
    return jax.ShapeDtypeStruct(shape, dtype)


def _accumulate(ref, value, first):
    @pl.when(first)
    def _():
        ref[...] = value

    @pl.when(jnp.logical_not(first))
    def _():
        ref[...] += value


def _rms_fwd(name, x, gain, comms=()):
    t, d = x.shape
    tm = _pick(t, ROW_TILE, 8)
    steps = t // tm
    plumb = _CommPlumbing(comms, 2, 1, 0)
    n_all_in = 2 + len(plumb.args)
    n_all_out = 1 + len(plumb.out_shape)

    def body(*refs):
        x_ref, g_ref = refs[:2]
        y_ref = refs[n_all_in]
        comm_refs = (refs[:n_all_in], refs[n_all_in:n_all_in + n_all_out], refs[n_all_in + n_all_out:])
        i = pl.program_id(0)
        plumb.start_at(*comm_refs, i == 0)
        xv = x_ref[...]
        rstd = lax.rsqrt(jnp.mean(xv * xv, axis=-1, keepdims=True) + EPS)
        y_ref[...] = (xv * rstd * g_ref[...]).astype(BF16)
        plumb.finish_at(*comm_refs, i == steps - 1)

    res = pl.pallas_call(
        body, name=name, grid=(steps,),
        in_specs=[pl.BlockSpec((tm, d), lambda i: (i, 0)), pl.BlockSpec((1, d), lambda i: (0, 0))]
        + [ANY] * len(plumb.args),
        out_specs=[pl.BlockSpec((tm, d), lambda i: (i, 0))] + [ANY] * len(plumb.out_shape),
        out_shape=[_sds((t, d), BF16)] + plumb.out_shape,
        scratch_shapes=plumb.scratch,
        input_output_aliases=plumb.aliases,
        compiler_params=_params(),
    )(x, gain, *plumb.args)
    return plumb.deliver(res)[0]


def _rms_bwd_epilogue(accs, extras, outs, ij):
    x_ref, g_ref, r_ref = extras
    dh_ref, dhb_ref, dg_ref = outs
    dn = accs[0]
    xv = x_ref[...]
    rstd = lax.rsqrt(jnp.mean(xv * xv, axis=-1, keepdims=True) + EPS)
    xhat = xv * rstd
    gd = dn * g_ref[...]
    dx = rstd * (gd - xhat * jnp.mean(gd * xhat, axis=-1, keepdims=True))
    dh = r_ref[...] + dx
    dh_ref[...] = dh
    dhb_ref[...] = dh.astype(BF16)
    _accumulate(dg_ref, jnp.sum(dn * xhat, axis=0, keepdims=True), ij[0] == 0)


def _rms_bwd_io(x, gain, dres, tm):
    t, d = x.shape
    row = pl.BlockSpec((tm, d), lambda i, j, k: (i, 0))
    extras = [(x, row), (gain, pl.BlockSpec((1, d), lambda i, j, k: (0, 0))), (dres, row)]
    outs = [(_sds((t, d), F32), row), (_sds((t, d), BF16), row),
            (_sds((1, d), F32), pl.BlockSpec((1, d), lambda i, j, k: (0, 0)))]
    return extras, outs


def _ffn_fwd(tag, h, gain, wgu, wd, hooks):
    t, d = h.shape
    fs = wgu.shape[2]
    f = 2 * fs
    tm = _pick(t, ROW_TILE, 8)
    n = _rms_fwd(tag + "_norm", h, gain)

    def up_epilogue(accs, extras, outs, ij):
        g, u = accs
        gu_ref, a_ref = outs
        gu_ref[0] = g.astype(BF16)
        gu_ref[1] = u.astype(BF16)
        a_ref[...] = (g * jax.nn.sigmoid(g) * u).astype(BF16)

    a_spec = pl.BlockSpec((tm, d), lambda i, j, k: (i, 0))
    gu, a = _mm(
        tag + "_up", "nn", (t // tm, 2, 1),
        [(n, a_spec, wgu, pl.BlockSpec((None, d, fs), lambda i, j, k: (j, 0, 0))),
         (n, a_spec, wgu, pl.BlockSpec((None, d, fs), lambda i, j, k: (j + 2, 0, 0)))],
        [],
        [(_sds((2, t, f), BF16), pl.BlockSpec((2, tm, fs), lambda i, j, k: (0, i, j))),
         (_sds((t, f), BF16), pl.BlockSpec((tm, fs), lambda i, j, k: (i, j)))],
        (tm, fs), up_epilogue, comms=hooks.get("up", lambda: ())(), j_outer=True, chunked=True)

    def down_epilogue(accs, extras, outs, ij):
        outs[0][...] = extras[0][...] + 0.5 * accs[0]


    row = pl.BlockSpec((tm, d), lambda i, j, k: (i, 0))
    (h_new,) = _mm(
        tag + "_down", "nn", (t // tm, 1, 1),
        [(a, pl.BlockSpec((tm, f), lambda i, j, k: (i, 0)), wd, pl.BlockSpec((f, d), lambda i, j, k: (0, 0)))],
        [(h, row)], [(_sds((t, d), F32), row)], (tm, d), down_epilogue, comms=hooks.get("down", lambda: ())())
    return h_new, (n, gu, a)


def _ffn_bwd(tag, dh, dh_b, h, gain, wgu, wd, saved, hooks):
    n, gu, a = saved
    t, d = h.shape
    fs = wgu.shape[2]
    f = 2 * fs
    tm = _pick(t, ROW_TILE, 8)
    tk = _pick(t, ROW_TILE, 8)

    def dact_epilogue(accs, extras, outs, ij):
        da = 0.5 * accs[0]
        g = extras[0][0].astype(F32)
        u = extras[0][1].astype(F32)
        sg = jax.nn.sigmoid(g)
        outs[0][0] = (da * u * sg * (1.0 + g * (1.0 - sg))).astype(BF16)
        outs[0][1] = (da * g * sg).astype(BF16)

    gu_spec = pl.BlockSpec((2, tm, fs), lambda i, j, k: (0, i, j))
    (dgu,) = _mm(
        tag + "_dact", "nt", (t // tm, 2, 1),
        [(dh_b, pl.BlockSpec((tm, d), lambda i, j, k: (i, 0)), wd, pl.BlockSpec((fs, d), lambda i, j, k: (j, 0)))],
        [(gu, gu_spec)], [(_sds((2, t, f), BF16), gu_spec)], (tm, fs), dact_epilogue, j_outer=True, chunked=True,
        comms=hooks.get("dact", lambda: ())())

    def cast_epilogue(accs, extras, outs, ij):
        outs[0][...] = accs[0].astype(BF16)

    (dwgu,) = _mm(
        tag + "_dwgu", "tn", (1, 4, t // tk),
        [(n, pl.BlockSpec((tk, d), lambda i, j, k: (k, 0)),
          dgu, pl.BlockSpec((None, tk, fs), lambda i, j, k: (j // 2, k, j % 2)))],
        [], [(_sds((4, d, fs), BF16), pl.BlockSpec((None, d, fs), lambda i, j, k: (j, 0, 0)))], (d, fs), cast_epilogue,
        comms=hooks.get("dwgu", lambda: ())())

    def half_epilogue(accs, extras, outs, ij):
        outs[0][...] = (0.5 * accs[0]).astype(BF16)

    (dwd,) = _mm(
        tag + "_dwd", "tn", (2, 1, t // tk),
        [(a, pl.BlockSpec((tk, fs), lambda i, j, k: (k, i)), dh_b, pl.BlockSpec((tk, d), lambda i, j, k: (k, 0)))],
        [], [(_sds((f, d), BF16), pl.BlockSpec((fs, d), lambda i, j, k: (i, 0)))], (fs, d), half_epilogue,
        comms=hooks.get("dwd", lambda g: ())(dwgu))

    tmn = _pick(t, ROW_TILE, 8)
    extras, outs = _rms_bwd_io(h, gain, dh, tmn)
    dh_in, dh_in_b, dgain = _mm(
        tag + "_dnorm", "nt", (t // tmn, 1, 4),
        [(dgu, pl.BlockSpec((None, tmn, fs), lambda i, j, k: (k // 2, i, k % 2)),
          wgu, pl.BlockSpec((None, d, fs), lambda i, j, k: (k, 0, 0)))],
        extras, outs, (tmn, d), _rms_bwd_epilogue, comms=hooks.get("dnorm", lambda g, w: ())(dwgu, dwd))
    return dh_in, dh_in_b, dgain, dwgu, dwd


def _lane_lo(shape):
    return lax.broadcasted_iota(jnp.int32, shape, 1) < HEAD_DIM


def _pair_norm(xv, gain):
    lo = _lane_lo(xv.shape)
    x2 = xv * xv
    ms_lo = jnp.sum(jnp.where(lo, x2, 0.0), axis=-1, keepdims=True) * (1.0 / HEAD_DIM)
    ms_hi = jnp.sum(jnp.where(lo, 0.0, x2), axis=-1, keepdims=True) * (1.0 / HEAD_DIM)
    rstd = jnp.where(lo, lax.rsqrt(ms_lo + EPS), lax.rsqrt(ms_hi + EPS))
    xhat = xv * rstd
    return xhat * gain, xhat, rstd


def _pair_norm_bwd(dn, xhat, rstd, gain):
    lo = _lane_lo(dn.shape)
    gd = dn * gain
    t = gd * xhat
    m_lo = jnp.sum(jnp.where(lo, t, 0.0), axis=-1, keepdims=True) * (1.0 / HEAD_DIM)
    m_hi = jnp.sum(jnp.where(lo, 0.0, t), axis=-1, keepdims=True) * (1.0 / HEAD_DIM)
    dx = rstd * (gd - xhat * jnp.where(lo, m_lo, m_hi))
    return dx, jnp.sum(dn * xhat, axis=0, keepdims=True)


def _half(xv, hi):
    lo = _lane_lo(xv.shape)
    return jnp.where(lo, 0, xv) if hi else jnp.where(lo, xv, 0)


def _attn_window(i, prev):
    q0 = i * Q_BLOCK
    start = jnp.maximum(q0 - prev, 0)
    off = start - (q0 - prev)
    return pl.multiple_of(start, Q_BLOCK), pl.multiple_of(off, Q_BLOCK)


def _attn_specs(cfg, s, nq):
    kw = cfg["kw"]
    q_spec = pl.BlockSpec((Q_BLOCK, A_WIDTH), lambda b, i: (b * nq + i, cfg["qblk"]))
    k_spec = pl.BlockSpec((s, kw), lambda b, i: (b, cfg["kblk"]))
    v_spec = pl.BlockSpec((s, kw), lambda b, i: (b, cfg["vblk"]))
    return q_spec, k_spec, v_spec


def _const_spec(shape):
    return pl.BlockSpec(shape, lambda b, i: (0,) * len(shape))


KEY_CHUNK = 128


def _pair_bias(bias_t):
    wext = bias_t.shape[1]
    return jnp.transpose(bias_t.reshape(N_HEADS // 2, 2, wext, Q_BLOCK), (0, 2, 1, 3)).reshape(
        N_HEADS // 2, wext, 2 * Q_BLOCK)


def _unpair_bias(db2):
    wext = db2.shape[1]
    return jnp.transpose(db2.reshape(N_HEADS // 2, wext, 2, Q_BLOCK), (0, 2, 1, 3)).reshape(N_HEADS, wext, Q_BLOCK)


def _pair_rows(rows):
    two = rows.reshape(N_HEADS // 2, 2 * rows.shape[1])
    return jnp.broadcast_to(two[:, None, :], (N_HEADS // 2, 8, two.shape[1]))


def _sub_lo(shape):
    return lax.broadcasted_iota(jnp.int32, shape, 0) < HEAD_DIM


def _by_half(lo_row, hi_row, rows):
    return jnp.where(_sub_lo((rows, lo_row.shape[1])), lo_row, hi_row)


def _stack_pair(xn, jq, group):
    parts = []
    for hq in range(2):
        hk = ((2 * jq + hq) // group) % 2
        xm = _half(xn, hq)
        if hq != hk:
            xm = pltpu.roll(xm, HEAD_DIM, 1)
        parts.append(xm)
    return jnp.concatenate(parts, axis=0).astype(BF16)


def _place_transposed(blk, dst_ref, c, heads, group):
    bt = blk.T
    lo = _sub_lo(bt.shape)
    for h in heads:
        src_hi = ((h // group) % 2) == 1
        part = jnp.where(lo, 0.0, bt) if src_hi else jnp.where(lo, bt, 0.0)
        if src_hi != (h % 2 == 1):
            part = pltpu.roll(part, HEAD_DIM, 0)
        dst_ref[h, c] = part.astype(BF16)


def _attn_fwd(name, qkv, bias2, sink2, gq, gk, cfg, n_batch, comms=()):
    t = qkv.shape[0]
    s = t // n_batch
    nq = s // Q_BLOCK
    nkc = s // KEY_CHUNK
    prev, group, kw = cfg["prev"], cfg["group"], cfg["kw"]
    w = prev + Q_BLOCK
    n_chunks = w // KEY_CHUNK
    wext = bias2.shape[1]
    plumb = _CommPlumbing(comms, 7, 2, 4)
    n_all_in = 7 + len(plumb.args)
    n_all_out = 2 + len(plumb.out_shape)

    def body(*refs):
        q_ref, k_ref, v_ref, bias_ref, sink_ref, gq_ref, gk_ref = refs[:7]
        y_ref, lse_ref = refs[n_all_in:n_all_in + 2]
        kn_ref, vt_ref, s_ref, pst_ref = refs[n_all_in + n_all_out:n_all_in + n_all_out + 4]
        i = pl.program_id(1)
        comm_refs = (refs[:n_all_in], refs[n_all_in:n_all_in + n_all_out], refs[n_all_in + n_all_out:])
        plumb.start_at(*comm_refs, (pl.program_id(0) == 0) & (i == 0))

        @pl.when(i == 0)
        def _():
            for jk in range(kw // PAIR):
                cols = pl.ds(jk * PAIR, PAIR)
                heads = [h for h in range(N_HEADS) if (h // group) // 2 == jk]
                kn, _, _ = _pair_norm(k_ref[:, cols].astype(F32), gk_ref[...])
                kn_ref[:, cols] = kn.astype(BF16)
                for c in range(nkc):
                    _place_transposed(v_ref[pl.ds(c * KEY_CHUNK, KEY_CHUNK), cols].astype(F32), vt_ref, c, heads, group)

        start, off = _attn_window(i, prev)
        c0 = start // KEY_CHUNK
        sub8 = lax.broadcasted_iota(jnp.int32, (N_HEADS, Q_BLOCK), 0)
        lse = jnp.zeros((N_HEADS, Q_BLOCK), F32)
        for jq in range(N_HEADS // 2):
            kcols = pl.ds((((2 * jq) // group) // 2) * PAIR, PAIR)
            qn, _, _ = _pair_norm(q_ref[:, pl.ds(jq * PAIR, PAIR)].astype(F32), gq_ref[...])
            qs = _stack_pair(qn * ATTN_SCALE, jq, group)
            s_ref[...] = _dot(kn_ref[pl.ds(start, w), kcols], qs, "nt")
            m = sink_ref[jq, 0:1, :]
            for c in range(n_chunks):
                r = pl.ds(c * KEY_CHUNK, KEY_CHUNK)
                s2 = s_ref[r, :] + bias_ref[jq, pl.ds(off + c * KEY_CHUNK, KEY_CHUNK), :]
                s_ref[r, :] = s2
                m = jnp.maximum(m, jnp.max(s2, axis=0, keepdims=True))
            l = jnp.exp(sink_ref[jq, 0:1, :] - m)
            for c in range(n_chunks):
                p = jnp.exp(s_ref[pl.ds(c * KEY_CHUNK, KEY_CHUNK), :] - m)
                l = l + jnp.sum(p, axis=0, keepdims=True)
                pst_ref[pl.ds(2 * c * KEY_CHUNK, KEY_CHUNK), :] = p[:, :Q_BLOCK].astype(BF16)
                pst_ref[pl.ds((2 * c + 1) * KEY_CHUNK, KEY_CHUNK), :] = p[:, Q_BLOCK:].astype(BF16)
            vl = jnp.concatenate([vt_ref[2 * jq + hq, c0 + c] for c in range(n_chunks) for hq in range(2)], axis=1)
            ot = _dot(vl, pst_ref[...], "nn")
            inv = 1.0 / l
            ot = ot * _by_half(inv[:, :Q_BLOCK], inv[:, Q_BLOCK:], PAIR)
            y_ref[:, pl.ds(jq * PAIR, PAIR)] = ot.T.astype(BF16)
            lse2 = m + jnp.log(l)
            lse = jnp.where(sub8 == 2 * jq, lse2[:, :Q_BLOCK], lse)
            lse = jnp.where(sub8 == 2 * jq + 1, lse2[:, Q_BLOCK:], lse)
        lse_ref[...] = lse
        plumb.finish_at(*comm_refs, (pl.program_id(0) == n_batch - 1) & (i == nq - 1))

    q_spec, k_spec, v_spec = _attn_specs(cfg, s, nq)
    res = pl.pallas_call(
        body, name=name, grid=(n_batch, nq),
        in_specs=[q_spec, k_spec, v_spec, _const_spec((N_HEADS // 2, wext, 2 * Q_BLOCK)),
                  _const_spec((N_HEADS // 2, 8, 2 * Q_BLOCK)), _const_spec((1, PAIR)), _const_spec((1, PAIR))]
        + [ANY] * len(plumb.args),
        out_specs=[pl.BlockSpec((Q_BLOCK, A_WIDTH), lambda b, i: (b * nq + i, 0)),
                   pl.BlockSpec((None, N_HEADS, Q_BLOCK), lambda b, i: (b * nq + i, 0, 0))]
        + [ANY] * len(plumb.out_shape),
        out_shape=[_sds((t, A_WIDTH), BF16), _sds((t // Q_BLOCK, N_HEADS, Q_BLOCK), F32)] + plumb.out_shape,
        scratch_shapes=[pltpu.VMEM((s, kw), BF16), pltpu.VMEM((N_HEADS, nkc, PAIR, KEY_CHUNK), BF16),
                        pltpu.VMEM((w, 2 * Q_BLOCK), F32), pltpu.VMEM((2 * w, Q_BLOCK), BF16)] + plumb.scratch,
        input_output_aliases=plumb.aliases,
        compiler_params=_params(),
    )(qkv, qkv, qkv, bias2, sink2, gq, gk, *plumb.args)
    return plumb.deliver(res)


def _attn_bwd(name, qkv, bias2, sink2, gq, gk, y, dy, lse, cfg, n_batch, want_dbias, comms=()):
    t = qkv.shape[0]
    s = t // n_batch
    nq = s // Q_BLOCK
    nkc = s // KEY_CHUNK
    prev, group, kw = cfg["prev"], cfg["group"], cfg["kw"]
    w = prev + Q_BLOCK
    n_chunks = w // KEY_CHUNK
    wext = bias2.shape[1]
    plumb = _CommPlumbing(comms, 10, 7, 9)
    n_all_in = 10 + len(plumb.args)
    n_all_out = 7 + len(plumb.out_shape)

    def body(*refs):
        q_ref, k_ref, v_ref, bias_ref, sink_ref, gq_ref, gk_ref, y_ref, dy_ref, lse_ref = refs[:10]
        dq_ref, dk_ref, dv_ref, db_ref, dsink_ref, dgq_ref, dgk_ref = refs[n_all_in:n_all_in + 7]
        kn_ref, knt_ref, dkn_ref, dvs_ref, s_ref, dp_ref, pb_ref, dsb_ref, dst_ref = \
            refs[n_all_in + n_all_out:n_all_in + n_all_out + 9]
        b = pl.program_id(0)
        i = pl.program_id(1)
        first = (b == 0) & (i == 0)
        comm_refs = (refs[:n_all_in], refs[n_all_in:n_all_in + n_all_out], refs[n_all_in + n_all_out:])
        plumb.start_at(*comm_refs, first)

        @pl.when(i == 0)
        def _():
            for jk in range(kw // PAIR):
                cols = pl.ds(jk * PAIR, PAIR)
                heads = [h for h in range(N_HEADS) if (h // group) // 2 == jk]
                for c in range(nkc):
                    rows = pl.ds(c * KEY_CHUNK, KEY_CHUNK)
                    kn, _, _ = _pair_norm(k_ref[rows, cols].astype(F32), gk_ref[...])
                    kn_ref[rows, cols] = kn.astype(BF16)
                    _place_transposed(kn, knt_ref, c, heads, group)
            dkn_ref[...] = jnp.zeros(dkn_ref.shape, F32)
            dvs_ref[...] = jnp.zeros(dvs_ref.shape, F32)

        @pl.when(first)
        def _():
            db_ref[...] = jnp.zeros(db_ref.shape, F32)
            dsink_ref[...] = jnp.zeros(dsink_ref.shape, F32)
            dgq_ref[...] = jnp.zeros(dgq_ref.shape, F32)
            dgk_ref[...] = jnp.zeros(dgk_ref.shape, F32)

        start, off = _attn_window(i, prev)
        c0 = start // KEY_CHUNK
        for jq in range(N_HEADS // 2):
            cols = pl.ds(jq * PAIR, PAIR)
            kcols = pl.ds((((2 * jq) // group) // 2) * PAIR, PAIR)
            qn, q_hat, q_rstd = _pair_norm(q_ref[:, cols].astype(F32), gq_ref[...])
            qs = _stack_pair(qn * ATTN_SCALE, jq, group)
            do_pair = dy_ref[:, cols].astype(F32)
            dos = _stack_pair(do_pair, jq, group)
            prod_t = (do_pair * y_ref[:, cols].astype(F32)).T
            lo = _sub_lo(prod_t.shape)
            delta2 = jnp.concatenate([jnp.sum(jnp.where(lo, prod_t, 0.0), axis=0, keepdims=True),
                                      jnp.sum(jnp.where(lo, 0.0, prod_t), axis=0, keepdims=True)], axis=1)
            lse2 = jnp.concatenate([lse_ref[2 * jq:2 * jq + 1, :], lse_ref[2 * jq + 1:2 * jq + 2, :]], axis=1)
            dsk = -jnp.exp(sink_ref[jq, 0:1, :] - lse2) * delta2
            dsink_ref[2 * jq:2 * jq + 1, :] += dsk[:, :Q_BLOCK]
            dsink_ref[2 * jq + 1:2 * jq + 2, :] += dsk[:, Q_BLOCK:]
            rows_w = pl.ds(start, w)
            s_ref[...] = _dot(kn_ref[rows_w, kcols], qs, "nt")
            dp_ref[...] = _dot(v_ref[rows_w, kcols], dos, "nt")
            for c in range(n_chunks):
                r = pl.ds(c * KEY_CHUNK, KEY_CHUNK)
                brows = pl.ds(off + c * KEY_CHUNK, KEY_CHUNK)
                p = jnp.exp(s_ref[r, :] + bias_ref[jq, brows, :] - lse2)
                ds = p * (dp_ref[r, :] - delta2)
                if want_dbias:
                    db_ref[jq, brows, :] += ds
                ds_b = ds.astype(BF16)
                pb_ref[r, :] = p.astype(BF16)
                dsb_ref[r, :] = ds_b
                dst_ref[pl.ds(2 * c * KEY_CHUNK, KEY_CHUNK), :] = ds_b[:, :Q_BLOCK]
                dst_ref[pl.ds((2 * c + 1) * KEY_CHUNK, KEY_CHUNK), :] = ds_b[:, Q_BLOCK:]
            dkn_ref[rows_w, kcols] += _dot(dsb_ref[...], qs, "nn")
            dvs_ref[rows_w, kcols] += _dot(pb_ref[...], dos, "nn")
            kl = jnp.concatenate([knt_ref[2 * jq + hq, c0 + c] for c in range(n_chunks) for hq in range(2)], axis=1)
            dqt = _dot(kl, dst_ref[...], "nn")
            dq_raw, dg = _pair_norm_bwd(dqt.T * ATTN_SCALE, q_hat, q_rstd, gq_ref[...])
            dq_ref[:, cols] = dq_raw.astype(BF16)
            dgq_ref[...] += dg

        @pl.when(i == nq - 1)
        def _():
            for jk in range(kw // PAIR):
                kcols = pl.ds(jk * PAIR, PAIR)
                _, k_hat, k_rstd = _pair_norm(k_ref[:, kcols].astype(F32), gk_ref[...])
                dk_raw, dg = _pair_norm_bwd(dkn_ref[:, kcols], k_hat, k_rstd, gk_ref[...])
                dk_ref[:, kcols] = dk_raw.astype(BF16)
                dgk_ref[...] += dg
            dv_ref[...] = dvs_ref[...].astype(BF16)

        plumb.finish_at(*comm_refs, (b == n_batch - 1) & (i == nq - 1))

    q_spec, k_spec, v_spec = _attn_specs(cfg, s, nq)
    row = pl.BlockSpec((Q_BLOCK, A_WIDTH), lambda b, i: (b * nq + i, 0))
    kv_out = pl.BlockSpec((s, kw), lambda b, i: (b, 0))
    pair_bias = _const_spec((N_HEADS // 2, wext, 2 * Q_BLOCK))
    res = pl.pallas_call(
        body, name=name, grid=(n_batch, nq),
        in_specs=[q_spec, k_spec, v_spec, pair_bias, _const_spec((N_HEADS // 2, 8, 2 * Q_BLOCK)),
                  _const_spec((1, PAIR)), _const_spec((1, PAIR)), row, row,
                  pl.BlockSpec((None, N_HEADS, Q_BLOCK), lambda b, i: (b * nq + i, 0, 0))] + [ANY] * len(plumb.args),
        out_specs=[row, kv_out, kv_out, pair_bias, _const_spec((N_HEADS, 128)),
                   _const_spec((1, PAIR)), _const_spec((1, PAIR))] + [ANY] * len(plumb.out_shape),
        out_shape=[_sds((t, A_WIDTH), BF16), _sds((t, kw), BF16), _sds((t, kw), BF16),
                   _sds((N_HEADS // 2, wext, 2 * Q_BLOCK), F32), _sds((N_HEADS, 128), F32),
                   _sds((1, PAIR), F32), _sds((1, PAIR), F32)] + plumb.out_shape,
        scratch_shapes=[pltpu.VMEM((s, kw), BF16), pltpu.VMEM((N_HEADS, nkc, PAIR, KEY_CHUNK), BF16),
                        pltpu.VMEM((s, kw), F32), pltpu.VMEM((s, kw), F32),
                        pltpu.VMEM((w, 2 * Q_BLOCK), F32), pltpu.VMEM((w, 2 * Q_BLOCK), F32),
                        pltpu.VMEM((w, 2 * Q_BLOCK), BF16), pltpu.VMEM((w, 2 * Q_BLOCK), BF16),
                        pltpu.VMEM((2 * w, Q_BLOCK), BF16)] + plumb.scratch,
        input_output_aliases=plumb.aliases,
        compiler_params=_params(),
    )(qkv, qkv, qkv, bias2, sink2, gq, gk, y, dy, lse, *plumb.args)
    return plumb.deliver(res)


def _band_tables(prev_chunks):
    prev = prev_chunks * CHUNK
    wext = 2 * prev + Q_BLOCK
    jj = np.arange(wext)[:, None]
    ii = np.arange(Q_BLOCK)[None, :]
    dist = prev + ii - jj
    rel_chunk = (prev // CHUNK + ii // CHUNK) - jj // CHUNK
    allowed = (rel_chunk >= 0) & (rel_chunk <= prev_chunks)
    return dist, allowed


def _alibi_slopes():
    return np.array([2.0 ** (-8.0 * (h + 1) / N_HEADS) for h in range(N_HEADS)], dtype=np.float32)


def _diag_onehot(prev, wext):
    n_diag = wext + Q_BLOCK - 1
    idx = np.clip(prev + Q_BLOCK - 1 - np.arange(n_diag), -A_MAX_REL, A_MAX_REL) + A_MAX_REL
    onehot = np.zeros((n_diag, 2 * A_MAX_REL + 1), np.float32)
    onehot[np.arange(n_diag), idx] = 1.0
    return onehot


def _bias_a(rel_bias):
    prev = A_PREV_CHUNKS * CHUNK
    _, allowed = _band_tables(A_PREV_CHUNKS)
    wext = allowed.shape[0]
    n_diag = wext + Q_BLOCK - 1
    seq = jnp.dot(rel_bias, jnp.asarray(_diag_onehot(prev, wext).T), precision=lax.Precision.HIGHEST)
    seq = jnp.pad(seq, ((0, 0), (0, 1)))
    rows = jnp.broadcast_to(seq[:, None, :], (N_HEADS, Q_BLOCK, n_diag + 1)).reshape(N_HEADS, -1)
    skew = rows[:, :Q_BLOCK * n_diag].reshape(N_HEADS, Q_BLOCK, n_diag)
    tile = jnp.transpose(skew[:, :, Q_BLOCK - 1:Q_BLOCK - 1 + wext], (0, 2, 1))
    return jnp.where(jnp.asarray(allowed)[None], tile, NEG_INF)


def _bias_b():
    dist, allowed = _band_tables(B_PREV_CHUNKS)
    bias = -_alibi_slopes()[:, None, None] * np.abs(dist).astype(np.float32)[None]
    return jnp.asarray(np.where(allowed[None], bias, np.float32(NEG_INF)).astype(np.float32))


def _rel_bias_grad(db_t):
    prev = A_PREV_CHUNKS * CHUNK
    wext = db_t.shape[1]
    n_diag = wext + Q_BLOCK - 1
    wp = n_diag + Q_BLOCK - 1
    xp = jnp.pad(jnp.transpose(db_t, (0, 2, 1)), ((0, 0), (0, 0), (Q_BLOCK - 1, Q_BLOCK - 1)))
    flat = jnp.pad(xp.reshape(N_HEADS, Q_BLOCK * wp), ((0, 0), (0, Q_BLOCK)))
    skew = flat.reshape(N_HEADS, Q_BLOCK, wp + 1)[:, :, :n_diag]
    diag = jnp.sum(skew, axis=1)
    return jnp.dot(diag, jnp.asarray(_diag_onehot(prev, wext)), precision=lax.Precision.HIGHEST)


def _ew(name, fn, ins, out_dtypes):
    r, c = ins[0].shape
    rb = _pick(r, max(16, (1 << 19) // c), 16)
    spec = pl.BlockSpec((rb, c), lambda i: (i, 0))

    def body(*refs):
        vals = fn(*[ref[...] for ref in refs[:len(ins)]])
        for ref, val in zip(refs[len(ins):], vals):
            ref[...] = val.astype(ref.dtype)

    return pl.pallas_call(
        body, name=name, grid=(r // rb,), in_specs=[spec] * len(ins), out_specs=[spec] * len(out_dtypes),
        out_shape=[_sds((r, c), dt) for dt in out_dtypes], compiler_params=_params(),
    )(*ins)


def _adamw_math(w, g, m, v):
    m = ADAM_B1 * m + (1.0 - ADAM_B1) * g
    v = ADAM_B2 * v + (1.0 - ADAM_B2) * (g * g)
    m_hat = m / (1.0 - ADAM_B1 ** ADAM_STEP)
    v_hat = v / (1.0 - ADAM_B2 ** ADAM_STEP)
    delta = -ADAM_LR * (m_hat / (jnp.sqrt(v_hat) + ADAM_EPS) + ADAM_WD * w)
    return delta, m, v


def _adamw_terms(name, terms, w, m, v):
    r, c = w.shape
    hr = r // 2
    rb = _pick(hr, max(16, (1 << 19) // c), 16)
    nb = hr // rb

    def body(t_ref, w_ref, m_ref, v_ref, g_ref, d_ref, nm_ref, nv_ref):
        g = t_ref[0].astype(F32)
        for k in range(1, N_CHIPS):
            g = g + t_ref[k].astype(F32)
        delta, nm, nv = _adamw_math(w_ref[...], g, m_ref[...], v_ref[...])
        g_ref[...] = g
        d_ref[...] = delta
        nm_ref[...] = nm
        nv_ref[...] = nv

    spec = pl.BlockSpec((rb, c), lambda h, i: (h * nb + i, 0))
    return pl.pallas_call(
        body, name=name, grid=(2, nb),
        in_specs=[pl.BlockSpec((None, N_CHIPS, rb, c), lambda h, i: (h, 0, i, 0)), spec, spec, spec],
        out_specs=[spec] * 4, out_shape=[_sds((r, c), F32)] * 4, compiler_params=_params(),
    )(terms, w, m, v)


def _mesh_place():
    x, y, c = lax.axis_index("x"), lax.axis_index("y"), lax.axis_index("c")
    chips = [(x, 1 - y), (1 - x, y), (1 - x, 1 - y)]
    return x, y, c, chips


def _all_gather_weights(shards):
    n = len(shards)

    def body(*refs):
        ins, outs = refs[:n], refs[n:2 * n]
        local_sem, ici_send, ici_recv, d2d_send, d2d_recv = refs[2 * n:]
        x, y, c, chips = _mesh_place()
        me = 2 * x + y
        sibling = (x, y, 1 - c)
        local, sent = [], []
        for wi in range(n):
            loc = pltpu.make_async_copy(ins[wi], outs[wi].at[me], local_sem.at[wi])
            loc.start()
            local.append(loc)
            for k, (tx, ty) in enumerate(chips):
                cp = pltpu.make_async_remote_copy(
                    src_ref=ins[wi].at[c], dst_ref=outs[wi].at[me, c],
                    send_sem=ici_send.at[wi * 3 + k], recv_sem=ici_recv.at[wi * 3 + k],
                    device_id=(tx, ty, c), device_id_type=MESH)
                cp.start()
                sent.append(cp)
        passed = []
        for wi in range(n):
            for k, (tx, ty) in enumerate(chips):
                slab = outs[wi].at[2 * tx + ty, c]
                pltpu.make_async_remote_copy(
                    src_ref=slab, dst_ref=slab, send_sem=ici_send.at[wi * 3 + k], recv_sem=ici_recv.at[wi * 3 + k],
                    device_id=(tx, ty, c), device_id_type=MESH).wait_recv()
                fw = pltpu.make_async_remote_copy(
                    src_ref=slab, dst_ref=slab, send_sem=d2d_send.at[wi * 3 + k], recv_sem=d2d_recv.at[wi * 3 + k],
                    device_id=sibling, device_id_type=MESH)
                fw.start()
                passed.append(fw)
        for wi in range(n):
            for k, (tx, ty) in enumerate(chips):
                slab = outs[wi].at[2 * tx + ty, 1 - c]
                pltpu.make_async_remote_copy(
                    src_ref=slab, dst_ref=slab, send_sem=d2d_send.at[wi * 3 + k], recv_sem=d2d_recv.at[wi * 3 + k],
                    device_id=sibling, device_id_type=MESH).wait_recv()
        for loc in local:
            loc.wait()
        for cp in sent + passed:
            cp.wait_send()

    return pl.pallas_call(
        body, name="all_gather_weights",
        in_specs=[ANY] * n, out_specs=[ANY] * n,
        out_shape=[_sds((N_CHIPS,) + s.shape, s.dtype) for s in shards],
        scratch_shapes=[pltpu.SemaphoreType.DMA((n,))] + [pltpu.SemaphoreType.DMA((3 * n,))] * 4,
    )(*shards)


def _run_comms(name, comms):
    plumb = _CommPlumbing(comms, 0, 0, 0)
    n_in, n_out = len(plumb.args), len(plumb.out_shape)

    def body(*refs):
        parts = []
        i0, o0, s0 = 0, n_in, n_in + n_out
        for cm in plumb.comms:
            parts.append((refs[i0:i0 + len(cm.ins)], refs[o0:o0 + len(cm.outs)], refs[s0:s0 + len(cm.sems)]))
            i0 += len(cm.ins)
            o0 += len(cm.outs)
            s0 += len(cm.sems)
        for cm, part in zip(plumb.comms, parts):
            cm.start(*part)
        for cm, part in zip(plumb.comms, parts):
            cm.finish(*part)

    res = pl.pallas_call(
        body, name=name, in_specs=[ANY] * n_in, out_specs=[ANY] * n_out, out_shape=plumb.out_shape,
        scratch_shapes=plumb.scratch, input_output_aliases=plumb.aliases,
    )(*plumb.args)
    plumb.deliver(res)


def _gather_ici(shards, targets=(0, 1, 2), into=None):
    n = len(shards)

    def copies(ins, outs, sems):
        local_sem, send_sem, recv_sem = sems
        x, y, c, chips = _mesh_place()
        me = 2 * x + y
        local, sends, recvs = [], [], []
        for wi in range(n):
            if into is None:
                local.append(pltpu.make_async_copy(ins[wi], outs[wi].at[me], local_sem.at[wi]))
            for k in targets:
                tx, ty = chips[k]
                sems_k = dict(send_sem=send_sem.at[wi * 3 + k], recv_sem=recv_sem.at[wi * 3 + k],
                              device_id=(tx, ty, c), device_id_type=MESH)
                sends.append(pltpu.make_async_remote_copy(
                    src_ref=ins[wi].at[c], dst_ref=outs[wi].at[me, c], **sems_k))
                slab = outs[wi].at[2 * tx + ty, c]
                recvs.append(pltpu.make_async_remote_copy(src_ref=slab, dst_ref=slab, **sems_k))
        return local, sends, recvs

    def start(ins, outs, sems):
        local, sends, _ = copies(ins, outs, sems)
        for cp in local + sends:
            cp.start()

    def finish(ins, outs, sems):
        local, sends, recvs = copies(ins, outs, sems)
        for cp in local:
            cp.wait()
        for cp in recvs:
            cp.wait_recv()
        for cp in sends:
            cp.wait_send()

    sems = [pltpu.SemaphoreType.DMA((n,)), pltpu.SemaphoreType.DMA((3 * n,)), pltpu.SemaphoreType.DMA((3 * n,))]
    if into is None:
        return _Comm(shards, [_sds((N_CHIPS,) + s.shape, s.dtype) for s in shards], {}, sems, start, finish)
    return _Comm(list(shards) + list(into), [_sds(g.shape, g.dtype) for g in into],
                 {n + i: i for i in range(n)}, sems, start, finish)


def _gather_d2d(gathered):
    n = len(gathered)

    def copies(outs, sems):
        send_sem, recv_sem = sems
        x, y, c, chips = _mesh_place()
        sends, recvs = [], []
        for wi in range(n):
            for k, (tx, ty) in enumerate(chips):
                sems_k = dict(send_sem=send_sem.at[wi * 3 + k], recv_sem=recv_sem.at[wi * 3 + k],
                              device_id=(x, y, 1 - c), device_id_type=MESH)
                mine = outs[wi].at[2 * tx + ty, c]
                theirs = outs[wi].at[2 * tx + ty, 1 - c]
                sends.append(pltpu.make_async_remote_copy(src_ref=mine, dst_ref=mine, **sems_k))
                recvs.append(pltpu.make_async_remote_copy(src_ref=theirs, dst_ref=theirs, **sems_k))
        return sends, recvs

    def start(ins, outs, sems):
        for cp in copies(outs, sems)[0]:
            cp.start()

    def finish(ins, outs, sems):
        sends, recvs = copies(outs, sems)
        for cp in recvs:
            cp.wait_recv()
        for cp in sends:
            cp.wait_send()

    return _Comm(gathered, [_sds(g.shape, g.dtype) for g in gathered], {i: i for i in range(n)},
                 [pltpu.SemaphoreType.DMA((3 * n,)), pltpu.SemaphoreType.DMA((3 * n,))], start, finish)


def _exchange_halves(grads):
    n = len(grads)

    def copies(ins, outs, sems):
        send_sem, recv_sem = sems
        x, y, c, _ = _mesh_place()
        return [pltpu.make_async_remote_copy(
            src_ref=ins[wi].at[t, 1 - c], dst_ref=outs[wi].at[t],
            send_sem=send_sem.at[wi * N_CHIPS + t], recv_sem=recv_sem.at[wi * N_CHIPS + t],
            device_id=(x, y, 1 - c), device_id_type=MESH) for wi in range(n) for t in range(N_CHIPS)]

    def start(ins, outs, sems):
        for cp in copies(ins, outs, sems):
            cp.start()

    def finish(ins, outs, sems):
        for cp in copies(ins, outs, sems):
            cp.wait()

    return _Comm(grads, [_sds((N_CHIPS,) + g.shape[2:], g.dtype) for g in grads], {},
                 [pltpu.SemaphoreType.DMA((N_CHIPS * n,)), pltpu.SemaphoreType.DMA((N_CHIPS * n,))], start, finish)


def _scatter_ici(sums):
    n = len(sums)

    def copies(ins, outs, sems):
        local_sem, send_sem, recv_sem = sems
        x, y, c, chips = _mesh_place()
        me = 2 * x + y
        local, sends, recvs = [], [], []
        for wi in range(n):
            local.append(pltpu.make_async_copy(ins[wi].at[me], outs[wi].at[c, 0], local_sem.at[wi]))
            for k, (tx, ty) in enumerate(chips):
                sems_k = dict(send_sem=send_sem.at[wi * 3 + k], recv_sem=recv_sem.at[wi * 3 + k],
                              device_id=(tx, ty, c), device_id_type=MESH)
                land = outs[wi].at[c, k + 1]
                sends.append(pltpu.make_async_remote_copy(src_ref=ins[wi].at[2 * tx + ty], dst_ref=land, **sems_k))
                recvs.append(pltpu.make_async_remote_copy(src_ref=land, dst_ref=land, **sems_k))
        return local, sends, recvs

    def start(ins, outs, sems):
        local, sends, _ = copies(ins, outs, sems)
        for cp in local + sends:
            cp.start()

    def finish(ins, outs, sems):
        local, sends, recvs = copies(ins, outs, sems)
        for cp in local:
            cp.wait()
        for cp in recvs:
            cp.wait_recv()
        for cp in sends:
            cp.wait_send()

    return _Comm(sums, [_sds((2, N_CHIPS) + s.shape[1:], s.dtype) for s in sums], {},
                 [pltpu.SemaphoreType.DMA((n,)), pltpu.SemaphoreType.DMA((3 * n,)), pltpu.SemaphoreType.DMA((3 * n,))],
                 start, finish)


def _scatter_d2d(terms):
    n = len(terms)

    def copies(outs, sems):
        send_sem, recv_sem = sems
        x, y, c, _ = _mesh_place()
        sends, recvs = [], []
        for wi in range(n):
            sems_w = dict(send_sem=send_sem.at[wi], recv_sem=recv_sem.at[wi],
                          device_id=(x, y, 1 - c), device_id_type=MESH)
            sends.append(pltpu.make_async_remote_copy(src_ref=outs[wi].at[c], dst_ref=outs[wi].at[c], **sems_w))
            recvs.append(pltpu.make_async_remote_copy(src_ref=outs[wi].at[1 - c], dst_ref=outs[wi].at[1 - c], **sems_w))
        return sends, recvs

    def start(ins, outs, sems):
        for cp in copies(outs, sems)[0]:
            cp.start()

    def finish(ins, outs, sems):
        sends, recvs = copies(outs, sems)
        for cp in recvs:
            cp.wait_recv()
        for cp in sends:
            cp.wait_send()

    return _Comm(terms, [_sds(t.shape, t.dtype) for t in terms], {i: i for i in range(n)},
                 [pltpu.SemaphoreType.DMA((n,)), pltpu.SemaphoreType.DMA((n,))], start, finish)


def _chip_sum(name, grad, got, core):
    _, _, hr, c = grad.shape
    rb = _pick(hr, max(16, (1 << 19) // c), 16)

    def body(core_ref, a_ref, b_ref, o_ref):
        o_ref[...] = (a_ref[...].astype(F32) + b_ref[...].astype(F32)).astype(BF16)

    out_spec = pl.BlockSpec((None, rb, c), lambda t, i, core_ref: (t, i, 0))
    return pl.pallas_call(
        body, name=name,
        grid_spec=pltpu.PrefetchScalarGridSpec(
            num_scalar_prefetch=1, grid=(N_CHIPS, hr // rb),
            in_specs=[pl.BlockSpec((None, None, rb, c), lambda t, i, core_ref: (t, core_ref[0], i, 0)), out_spec],
            out_specs=out_spec),
        out_shape=_sds((N_CHIPS, hr, c), BF16), compiler_params=_params(),
    )(core, grad, got)


def _all_reduce_small(pack):
    r = pack.shape[0]

    def body(p_ref, o_ref, land_ref, send_sem, recv_sem):
        x, y, c, _ = _mesh_place()
        me = 4 * x + 2 * y + c
        flips = [(k >> 2 & 1, k >> 1 & 1, k & 1) for k in range(1, N_DEV)]

        def peer(fx, fy, fc):
            return (1 - x if fx else x, 1 - y if fy else y, 1 - c if fc else c)

        land_ref[me] = p_ref[...]
        sent = []
        for k, flip in enumerate(flips):
            cp = pltpu.make_async_remote_copy(
                src_ref=p_ref, dst_ref=land_ref.at[me], send_sem=send_sem.at[k], recv_sem=recv_sem.at[k],
                device_id=peer(*flip), device_id_type=MESH)
            cp.start()
            sent.append(cp)
        for k, flip in enumerate(flips):
            px, py, pc = peer(*flip)
            slot = land_ref.at[4 * px + 2 * py + pc]
            pltpu.make_async_remote_copy(
                src_ref=slot, dst_ref=slot, send_sem=send_sem.at[k], recv_sem=recv_sem.at[k],
                device_id=(px, py, pc), device_id_type=MESH).wait_recv()
        total = land_ref[0]
        for d in range(1, N_DEV):
            total = total + land_ref[d]
        o_ref[...] = total
        for cp in sent:
            cp.wait_send()

    vmem = pl.BlockSpec(memory_space=pltpu.VMEM)
    return pl.pallas_call(
        body, name="all_reduce_small", in_specs=[vmem], out_specs=vmem, out_shape=_sds((r, 128), F32),
        scratch_shapes=[pltpu.VMEM((N_DEV, r, 128), F32), pltpu.SemaphoreType.DMA((N_DEV - 1,)),
                        pltpu.SemaphoreType.DMA((N_DEV - 1,))],
    )(pack)


PACK_TILE = 8 * 128


def _pack(items):
    rows, i = [], 0
    while i < len(items):
        j = i
        while j < len(items) and items[j].size == items[i].size:
            j += 1
        group = jnp.stack([it.reshape(-1).astype(F32) for it in items[i:j]])
        rows.append(jnp.pad(group, ((0, 0), (0, -group.shape[1] % PACK_TILE))).reshape(-1, 128))
        i = j
    return jnp.concatenate(rows, axis=0)


def _unpack(pack, shapes):
    out, row = [], 0
    for shp in shapes:
        size = int(np.prod(shp))
        nrow = -(-size // PACK_TILE) * (PACK_TILE // 128)
        out.append(pack[row:row + nrow].reshape(-1)[:size].reshape(shp))
        row += nrow
    return out


BIG = ["ffn1_w_gu", "ffn1_w_down", "w_in", "w_gate", "w_proj_a", "w_proj_b", "w_out",
       "ffn2_w_gu", "ffn2_w_down", "w_ple_gate", "w_ple_proj"]
SMALL = ["ffn1_norm", "mix_norm", "ffn2_norm", "ple_norm", "a_q_norm", "a_k_norm", "b_q_norm", "b_k_norm",
         "a_rel_bias", "b_sinks"]
WEIGHTS = ["ffn1_norm", "ffn1_w_gu", "ffn1_w_down", "mix_norm", "w_in", "a_q_norm", "a_k_norm", "a_rel_bias",
           "b_q_norm", "b_k_norm", "b_sinks", "w_gate", "w_proj_a", "w_proj_b", "w_out", "ffn2_norm",
           "ffn2_w_gu", "ffn2_w_down", "ple_norm", "w_ple_gate", "w_ple_proj"]
ATTN_A = dict(prev=A_PREV_CHUNKS * CHUNK, group=1, kw=A_WIDTH, qblk=0, kblk=1, vblk=2)
ATTN_B = dict(prev=B_PREV_CHUNKS * CHUNK, group=N_HEADS // B_KV_HEADS, kw=B_KV_WIDTH, qblk=3,
              kblk=4 * A_WIDTH // B_KV_WIDTH, vblk=4 * A_WIDTH // B_KV_WIDTH + 1)


def _cast_epilogue(accs, extras, outs, ij):
    for acc, out in zip(accs, outs):
        out[...] = acc.astype(out.dtype)


GATHER_FIRST = ["ffn1_w_gu", "ffn1_w_down"]
ROW_SHARDED = ("ffn1_w_down", "ffn2_w_down", "w_out", "w_ple_gate")


def _slotted(name, grad):
    if name == "w_in":
        rows, cols = grad.shape
        grad = jnp.transpose(grad.reshape(rows, N_CHIPS, cols // N_CHIPS), (1, 0, 2))
    elif name in ROW_SHARDED:
        grad = grad.reshape(N_CHIPS, grad.shape[0] // N_CHIPS, grad.shape[1])
    return grad.reshape(N_CHIPS, 2, grad.shape[1] // 2, grad.shape[2])


def _local_step(xt, pt, tgt, n_batch, shards, small, core):
    t, d = xt.shape
    tm = _pick(t, ROW_TILE, 8)
    tk = _pick(t, ROW_TILE, 8)
    nt = t // tm
    row = pl.BlockSpec((tm, d), lambda i, j, k: (i, 0))
    gs = shards["w_gate"].shape[1]
    ps = shards["w_proj_a"].shape[1]
    es = shards["w_ple_proj"].shape[1]
    pdim = pt.shape[1]
    ncols = N_CHIPS * shards["w_in"].shape[1]
    tin = ncols // 2
    assert 2 * gs == d and 4 * ps == d and 4 * es == d and tin % 128 == 0

    w = {}
    halves = {n: s.reshape(2, s.shape[0] // 2, s.shape[1]) for n, s in shards.items()}

    def publish(names, arrays):
        for name, g in zip(names, arrays):
            g = g.reshape(N_CHIPS, 2 * g.shape[2], g.shape[3])
            if name in ROW_SHARDED:
                g = g.reshape(N_CHIPS * g.shape[1], g.shape[2])
            elif name == "w_in":
                g = jnp.transpose(g, (1, 0, 2)).reshape(g.shape[1], N_CHIPS * g.shape[2])
            w[name] = g

    class GatherPipe:
        def __init__(self, names):
            self.names = names

        def ici(self, targets=(0, 1, 2)):
            self.first = _gather_ici([halves[n] for n in self.names], targets)
            return self.first

        def ici_more(self, targets):
            self.first = _gather_ici([halves[n] for n in self.names], targets, into=self.first.results)
            return self.first

        def d2d(self):
            self.second = _gather_d2d(self.first.results)
            return self.second

        def publish(self):
            publish(self.names, self.second.results)

    class GradPipe:
        def __init__(self, names):
            self.names = names

        def exchange(self, grads):
            self.grads = [_slotted(n, g) for n, g in zip(self.names, grads)]
            self.x = _exchange_halves(self.grads)
            return self.x

        def scatter(self):
            self.sums = [_chip_sum("chip_sum_" + n, g, got, core)
                         for n, g, got in zip(self.names, self.grads, self.x.results)]
            self.s = _scatter_ici(self.sums)
            return self.s

        def forward(self):
            self.f = _scatter_d2d(self.s.results)
            return self.f

        def terms(self):
            return dict(zip(self.names, self.f.results))

    publish(GATHER_FIRST, _all_gather_weights([halves[n] for n in GATHER_FIRST]))
    g_in, g_proj, g_ple = GatherPipe(["w_in", "w_gate"]), GatherPipe(["w_proj_a", "w_proj_b", "w_out"]), \
        GatherPipe(["w_ple_gate", "w_ple_proj"])
    g_down2, g_up2 = GatherPipe(["ffn2_w_down"]), GatherPipe(["ffn2_w_gu"])
    h1, ffn1_saved = _ffn_fwd("ffn1", xt, small["ffn1_norm"], w["ffn1_w_gu"], w["ffn1_w_down"],
                              {"up": lambda: [g_in.ici(targets=(0, 1))], "down": lambda: [g_in.ici_more(targets=(2,))]})
    un = _rms_fwd("mix_norm", h1, small["mix_norm"], comms=[g_in.d2d()])
    g_in.publish()
    w_in, wgate = w["w_in"], w["w_gate"]
    (qkv,) = _mm(
        "qkv", "nn", (nt, 2, 1),
        [(un, row, w_in, pl.BlockSpec((d, tin), lambda i, j, k: (0, j)))], [],
        [(_sds((t, ncols), BF16), pl.BlockSpec((tm, tin), lambda i, j, k: (i, j)))], (tm, tin), _cast_epilogue,
        j_outer=True, comms=[g_proj.ici()])

    def gate_epilogue(accs, extras, outs, ij):
        outs[0][...] = jax.nn.sigmoid(accs[0]).astype(BF16)

    (gates,) = _mm(
        "gate", "nn", (nt, 4, 1),
        [(un, row, wgate, pl.BlockSpec((None, d, gs), lambda i, j, k: (j, 0, 0)))], [],
        [(_sds((2, t, d), BF16), pl.BlockSpec((None, tm, gs), lambda i, j, k: (j // 2, i, j % 2)))],
        (tm, gs), gate_epilogue, j_outer=True, chunked=True, comms=[g_proj.d2d(), g_ple.ici()])
    g_proj.publish()
    wpa, wpb, wout = w["w_proj_a"], w["w_proj_b"], w["w_out"]

    bias_a = _pair_bias(_bias_a(small["a_rel_bias"][0]))
    bias_b = _pair_bias(_bias_b())
    sink_a = _pair_rows(jnp.full((N_HEADS, 128), NEG_INF, F32))
    sink_b = _pair_rows(jnp.broadcast_to(small["b_sinks"][0][:, None], (N_HEADS, 128)))
    gqa, gka, gqb, gkb = [jnp.tile(small[k], (1, 2)) for k in ("a_q_norm", "a_k_norm", "b_q_norm", "b_k_norm")]
    ya, lse_a = _attn_fwd("attn_a_fwd", qkv, bias_a, sink_a, gqa, gka, ATTN_A, n_batch,
                          comms=[g_ple.d2d(), g_down2.ici(), g_up2.ici(targets=(2,))])
    g_ple.publish()
    wpg, wpe = w["w_ple_gate"], w["w_ple_proj"]
    yb, lse_b = _attn_fwd("attn_b_fwd", qkv, bias_b, sink_b, gqb, gkb, ATTN_B, n_batch,
                          comms=[g_down2.d2d(), g_up2.ici_more(targets=(0, 1))])
    g_down2.publish()

    def merge_epilogue(accs, extras, outs, ij):
        pa, pb = accs
        outs[0][...] = (extras[0][...].astype(F32) * pa + extras[1][...].astype(F32) * pb).astype(BF16)
        outs[1][...] = pa.astype(BF16)
        outs[2][...] = pb.astype(BF16)

    y_spec = pl.BlockSpec((tm, A_WIDTH), lambda i, j, k: (i, 0))
    proj_spec = pl.BlockSpec((None, A_WIDTH, ps), lambda i, j, k: (j, 0, 0))
    tile_ps = pl.BlockSpec((tm, ps), lambda i, j, k: (i, j))
    merged, pa, pb = _mm(
        "proj_merge", "nn", (nt, 4, 1),
        [(ya, y_spec, wpa, proj_spec), (yb, y_spec, wpb, proj_spec)],
        [(gates, pl.BlockSpec((None, tm, ps), lambda i, j, k: (0, i, j))),
         (gates, pl.BlockSpec((None, tm, ps), lambda i, j, k: (1, i, j)))],
        [(_sds((t, d), BF16), tile_ps)] * 3, (tm, ps), merge_epilogue, comms=[g_up2.d2d()])
    g_up2.publish()

    def residual_epilogue(accs, extras, outs, ij):
        outs[0][...] = extras[0][...] + accs[0]

    (h2,) = _mm(
        "out_proj", "nn", (nt, 1, 1),
        [(merged, row, wout, pl.BlockSpec((d, d), lambda i, j, k: (0, 0)))],
        [(h1, row)], [(_sds((t, d), F32), row)], (tm, d), residual_epilogue)

    h3, ffn2_saved = _ffn_fwd("ffn2", h2, small["ffn2_norm"], w["ffn2_w_gu"], w["ffn2_w_down"], {})
    n3 = _rms_fwd("ple_norm", h3, small["ple_norm"])
    tile_es = pl.BlockSpec((tm, es), lambda i, j, k: (i, j))
    (pe,) = _mm(
        "ple_embed", "nn", (nt, 4, 1),
        [(pt, pl.BlockSpec((tm, pdim), lambda i, j, k: (i, 0)), wpe, pl.BlockSpec((None, pdim, es), lambda i, j, k: (j, 0, 0)))],
        [], [(_sds((t, d), F32), tile_es)], (tm, es), _cast_epilogue)

    th = _pick(d, 512)

    def head_epilogue(accs, extras, outs, ij):
        h3_ref, pe_ref, tgt_ref = extras
        dy_ref, dpe_ref, dz_ref, loss_ref = outs
        pg = jax.nn.sigmoid(accs[0])
        pev = pe_ref[...]
        diff = h3_ref[...] + pg * pev - tgt_ref[...]
        dy = diff * (1.0 / d)
        dy_ref[...] = dy
        dpe_ref[...] = (dy * pg).astype(BF16)
        dz_ref[...] = (dy * pev * pg * (1.0 - pg)).astype(BF16)
        _accumulate(loss_ref, jnp.full(loss_ref.shape, jnp.sum(diff * diff), F32), (ij[0] == 0) & (ij[1] == 0))

    tile_h = pl.BlockSpec((tm, th), lambda i, j, k: (i, j))
    dy, dpe, dz, loss_acc = _mm(
        "ple_gate_loss", "nn", (nt, d // th, 1),
        [(n3, row, wpg, pl.BlockSpec((d, th), lambda i, j, k: (0, j)))],
        [(h3, tile_h), (pe, tile_h), (tgt, tile_h)],
        [(_sds((t, d), F32), tile_h), (_sds((t, d), BF16), tile_h), (_sds((t, d), BF16), tile_h),
         (_sds((8, 128), F32), pl.BlockSpec((8, 128), lambda i, j, k: (0, 0)))],
        (tm, th), head_epilogue, j_outer=True, chunked=True)
    loss = 0.5 * loss_acc[0, 0] / d

    nk = t // tk
    (dwpe,) = _mm(
        "d_w_ple_proj", "tn", (1, 4, nk),
        [(pt, pl.BlockSpec((tk, pdim), lambda i, j, k: (k, 0)), dpe, pl.BlockSpec((tk, es), lambda i, j, k: (k, j)))],
        [], [(_sds((4, pdim, es), BF16), pl.BlockSpec((None, pdim, es), lambda i, j, k: (j, 0, 0)))],
        (pdim, es), _cast_epilogue)

    def dense_grad(name, a, dyb, comms=()):
        (res,) = _mm(
            name, "tn", (1, d // th, nk),
            [(a, pl.BlockSpec((tk, d), lambda i, j, k: (k, 0)), dyb, pl.BlockSpec((tk, th), lambda i, j, k: (k, j)))],
            [], [(_sds((d, d), BF16), pl.BlockSpec((d, th), lambda i, j, k: (0, j)))], (d, th), _cast_epilogue,
            comms=comms)
        return res

    dwpg = dense_grad("d_w_ple_gate", n3, dz)
    tmn = _pick(t, ROW_TILE, 8)
    extras, outs = _rms_bwd_io(h3, small["ple_norm"], dy, tmn)
    dh3, dh3_b, d_ple_norm = _mm(
        "d_ple_norm", "nt", (t // tmn, 1, 1),
        [(dz, pl.BlockSpec((tmn, d), lambda i, j, k: (i, 0)), wpg, pl.BlockSpec((d, d), lambda i, j, k: (0, 0)))],
        extras, outs, (tmn, d), _rms_bwd_epilogue)

    up2, down2, ple = GradPipe(["ffn2_w_gu"]), GradPipe(["ffn2_w_down"]), GradPipe(["w_ple_gate", "w_ple_proj"])
    proj = GradPipe(["w_proj_a", "w_proj_b", "w_out"])
    dh2, dh2_b, d_ffn2_norm, dwgu2, dwd2 = _ffn_bwd(
        "ffn2", dh3, dh3_b, h2, small["ffn2_norm"], w["ffn2_w_gu"], w["ffn2_w_down"], ffn2_saved,
        {"dnorm": lambda dwgu, dwd: [up2.exchange([dwgu]), down2.exchange([dwd]), ple.exchange([dwpg, dwpe])]})

    def dmerge_epilogue(accs, extras, outs, ij):
        dmo = accs[0]
        g_ref, pa_ref, pb_ref = extras
        dg_ref, dpa_ref, dpb_ref = outs
        ga = g_ref[0].astype(F32)
        gb = g_ref[1].astype(F32)
        dg_ref[0] = (dmo * pa_ref[...].astype(F32) * ga * (1.0 - ga)).astype(BF16)
        dg_ref[1] = (dmo * pb_ref[...].astype(F32) * gb * (1.0 - gb)).astype(BF16)
        dpa_ref[...] = (dmo * ga).astype(BF16)
        dpb_ref[...] = (dmo * gb).astype(BF16)

    g_spec = pl.BlockSpec((2, tm, th), lambda i, j, k: (0, i, j))
    dgates, dpa, dpb = _mm(
        "d_merge", "nt", (nt, d // th, 1),
        [(dh2_b, row, wout, pl.BlockSpec((th, d), lambda i, j, k: (j, 0)))],
        [(gates, g_spec), (pa, tile_h), (pb, tile_h)],
        [(_sds((2, t, d), BF16), g_spec), (_sds((t, d), BF16), tile_h), (_sds((t, d), BF16), tile_h)],
        (tm, th), dmerge_epilogue, j_outer=True, chunked=True, comms=[down2.scatter()])
    dwout = dense_grad("d_w_out", merged, dh2_b, comms=[down2.forward(), ple.scatter()])

    yk_spec = pl.BlockSpec((tk, A_WIDTH), lambda i, j, k: (k, 0))
    dk_spec = pl.BlockSpec((tk, ps), lambda i, j, k: (k, j))
    dproj = (_sds((4, A_WIDTH, ps), BF16), proj_spec)
    dwpa, dwpb = _mm(
        "d_w_proj", "tn", (1, 4, nk),
        [(ya, yk_spec, dpa, dk_spec), (yb, yk_spec, dpb, dk_spec)], [], [dproj, dproj], (A_WIDTH, ps), _cast_epilogue,
        comms=[ple.forward()])
    dproj_a = pl.BlockSpec((tm, ps), lambda i, j, k: (i, k))
    wproj_k = pl.BlockSpec((None, A_WIDTH, ps), lambda i, j, k: (k, 0, 0))
    dya, dyb = _mm(
        "d_attn_out", "nt", (nt, 1, 4),
        [(dpa, dproj_a, wpa, wproj_k), (dpb, dproj_a, wpb, wproj_k)], [],
        [(_sds((t, A_WIDTH), BF16), y_spec)] * 2, (tm, A_WIDTH), _cast_epilogue,
        comms=[proj.exchange([dwpa, dwpb, dwout])])

    dqa, dka, dva, dbias_a, _, dgqa, dgka = _attn_bwd(
        "attn_a_bwd", qkv, bias_a, sink_a, gqa, gka, ya, dya, lse_a, ATTN_A, n_batch, True,
        comms=[up2.scatter(), proj.scatter()])
    dqb, dkb, dvb, _, dsink_b, dgqb, dgkb = _attn_bwd(
        "attn_b_bwd", qkv, bias_b, sink_b, gqb, gkb, yb, dyb, lse_b, ATTN_B, n_batch, False,
        comms=[up2.forward(), proj.forward()])
    dqkv = jnp.concatenate([dqa, dka, dva, dqb, dkb, dvb], axis=1)

    (dwgate,) = _mm(
        "d_w_gate", "tn", (1, 4, nk),
        [(un, pl.BlockSpec((tk, d), lambda i, j, k: (k, 0)),
          dgates, pl.BlockSpec((None, tk, gs), lambda i, j, k: (j // 2, k, j % 2)))],
        [], [(_sds((4, d, gs), BF16), pl.BlockSpec((None, d, gs), lambda i, j, k: (j, 0, 0)))], (d, gs), _cast_epilogue)
    (dwin,) = _mm(
        "d_w_in", "tn", (1, 2, nk),
        [(un, pl.BlockSpec((tk, d), lambda i, j, k: (k, 0)), dqkv, pl.BlockSpec((tk, tin), lambda i, j, k: (k, j)))],
        [], [(_sds((d, ncols), BF16), pl.BlockSpec((d, tin), lambda i, j, k: (0, j)))], (d, tin), _cast_epilogue)

    mixer = GradPipe(["w_in", "w_gate"])
    extras, outs = _rms_bwd_io(h1, small["mix_norm"], dh2, tmn)
    dh1, dh1_b, d_mix_norm = _mm(
        "d_mix_norm", "nt", (t // tmn, 1, 6),
        [(dgates, pl.BlockSpec((None, tmn, gs), lambda i, j, k: (jnp.minimum(k, 3) // 2, i, jnp.minimum(k, 3) % 2)),
          wgate, pl.BlockSpec((None, d, gs), lambda i, j, k: (jnp.minimum(k, 3), 0, 0))),
         (dqkv, pl.BlockSpec((tmn, tin), lambda i, j, k: (i, jnp.maximum(k - 4, 0))),
          w_in, pl.BlockSpec((d, tin), lambda i, j, k: (0, jnp.maximum(k - 4, 0))))],
        extras, outs, (tmn, d), _rms_bwd_epilogue, steps=[4, 2],
        comms=[mixer.exchange([dwin, dwgate])])

    up1 = GradPipe(["ffn1_w_gu"])
    down1 = GradPipe(["ffn1_w_down"])
    dx, _, d_ffn1_norm, _, _ = _ffn_bwd(
        "ffn1", dh1, dh1_b, xt, small["ffn1_norm"], w["ffn1_w_gu"], w["ffn1_w_down"], ffn1_saved,
        {"dwgu": lambda: [mixer.scatter()],
         "dwd": lambda dwgu: [mixer.forward(), up1.exchange([dwgu])],
         "dnorm": lambda dwgu, dwd: [up1.scatter(), down1.exchange([dwd])]})
    _run_comms("grad_tail_scatter", [up1.forward(), down1.scatter()])
    _run_comms("grad_tail_forward", [down1.forward()])
    terms = {}
    for pipe in (up2, down2, ple, proj, mixer, up1, down1):
        terms.update(pipe.terms())

    def fold(v):
        return v[0, :HEAD_DIM] + v[0, HEAD_DIM:]

    small_grads = {"ffn1_norm": d_ffn1_norm, "mix_norm": d_mix_norm, "ffn2_norm": d_ffn2_norm,
                   "ple_norm": d_ple_norm, "a_q_norm": fold(dgqa), "a_k_norm": fold(dgka),
                   "b_q_norm": fold(dgqb), "b_k_norm": fold(dgkb), "a_rel_bias": _rel_bias_grad(_unpair_bias(dbias_a)),
                   "b_sinks": jnp.sum(dsink_b, axis=1)}
    return loss, dx, terms, small_grads


def kernel(x, p, ffn1_norm, ffn1_w_gu, ffn1_w_down, mix_norm, w_in, a_q_norm, a_k_norm, a_rel_bias, b_q_norm, b_k_norm, b_sinks, w_gate, w_proj_a, w_proj_b, w_out, ffn2_norm, ffn2_w_gu, ffn2_w_down, ple_norm, w_ple_gate, w_ple_proj, loss_target, m_ffn1_norm, m_ffn1_w_gu, m_ffn1_w_down, m_mix_norm, m_w_in, m_a_q_norm, m_a_k_norm, m_a_rel_bias, m_b_q_norm, m_b_k_norm, m_b_sinks, m_w_gate, m_w_proj_a, m_w_proj_b, m_w_out, m_ffn2_norm, m_ffn2_w_gu, m_ffn2_w_down, m_ple_norm, m_w_ple_gate, m_w_ple_proj, v_ffn1_norm, v_ffn1_w_gu, v_ffn1_w_down, v_mix_norm, v_w_in, v_a_q_norm, v_a_k_norm, v_a_rel_bias, v_b_q_norm, v_b_k_norm, v_b_sinks, v_w_gate, v_w_proj_a, v_w_proj_b, v_w_out, v_ffn2_norm, v_ffn2_w_gu, v_ffn2_w_down, v_ple_norm, v_w_ple_gate, v_w_ple_proj):
    given = dict(locals())
    n_batch, s, d = x.shape
    t = n_batch * s
    xt = x.reshape(t, d)
    pt = p.reshape(t, p.shape[-1])
    tgt = loss_target.reshape(t, d)

    shards = {}
    for name in BIG:
        (shards[name],) = _ew("cast_" + name, lambda v: (v,), [given[name][0]], [BF16])
    small = {name: given[name] for name in SMALL}
    core = lax.axis_index("c").astype(jnp.int32).reshape(1)
    loss, dx, terms, small_grads = _local_step(xt, pt, tgt, n_batch, shards, small, core)

    grads, deltas, new_m, new_v = {}, {}, {}, {}
    for name in BIG:
        gw, dl, nm, nv = _adamw_terms("adamw_" + name, terms[name], given[name][0], given["m_" + name][0],
                                      given["v_" + name][0])
        grads[name], deltas[name], new_m[name], new_v[name] = gw[None], dl[None], nm[None], nv[None]

    small_shapes = [given[name].shape for name in SMALL] + [()]
    g_pack = _all_reduce_small(_pack([small_grads[name] for name in SMALL] + [loss]))
    zero = jnp.zeros((), F32)
    w_pack = _pack([given[name] for name in SMALL] + [zero])
    m_pack = _pack([given["m_" + name] for name in SMALL] + [zero])
    v_pack = _pack([given["v_" + name] for name in SMALL] + [zero])
    d_pack, nm_pack, nv_pack = _ew("adamw_small", lambda wv, gv, mv, vv: _adamw_math(wv, gv, mv, vv),
                                   [w_pack, g_pack, m_pack, v_pack], [F32] * 3)
    g_small = _unpack(g_pack, small_shapes)
    loss_total = g_small[-1]
    for name, gv, dv, mv, vv in zip(SMALL, g_small, _unpack(d_pack, small_shapes), _unpack(nm_pack, small_shapes),
                                    _unpack(nv_pack, small_shapes)):
        grads[name], deltas[name], new_m[name], new_v[name] = gv, dv, mv, vv

    return (loss_total, dx.reshape(x.shape), *[grads[n] for n in WEIGHTS], *[deltas[n] for n in WEIGHTS],
            *[new_m[n] for n in WEIGHTS], *[new_v[n] for n in WEIGHTS])
```

```python
import functools

import numpy as np
import jax
import jax.numpy as jnp
from jax import lax
from jax.experimental import pallas as pl
from jax.experimental.pallas import tpu as pltpu

F32 = jnp.float32
BF16 = jnp.bfloat16

CHUNK = 64
HEAD_DIM = 64
A_PREV_CHUNKS = 8
A_MAX_REL = 128
N_HEADS = 8
B_KV_HEADS = 2
B_PREV_CHUNKS = 2
A_WIDTH = N_HEADS * HEAD_DIM
B_KV_WIDTH = B_KV_HEADS * HEAD_DIM
EPS = 1e-6
NEG_INF = -1e30
ATTN_SCALE = HEAD_DIM ** -0.5
Q_BLOCK = 128
PAIR = 2 * HEAD_DIM

ADAM_LR = 0.001
ADAM_B1 = 0.9
ADAM_B2 = 0.999
ADAM_EPS = 1e-08
ADAM_WD = 0.01
ADAM_STEP = 10

N_CHIPS = 4
N_DEV = 8
VMEM_LIMIT_V7X = 56 * 1024 * 1024
ROW_TILE = 1024
MESH = pl.DeviceIdType.MESH
ANY = pl.BlockSpec(memory_space=pl.ANY)

_DN = {
    "nn": (((1,), (0,)), ((), ())),
    "nt": (((1,), (1,)), ((), ())),
    "tn": (((0,), (0,)), ((), ())),
}


def _pick(n, target, mult=128):
    best = None
    for d in range(mult, min(n, target) + 1, mult):
        if n % d == 0:
            best = d
    return n if best is None else best


def _dot(a, b, mode):
    return lax.dot_general(a.astype(BF16), b.astype(BF16), _DN[mode], preferred_element_type=F32)


def _params():
    return pltpu.CompilerParams(vmem_limit_bytes=VMEM_LIMIT_V7X)


class _Comm:
    def __init__(self, ins, outs, aliases, sems, start, finish):
        self.ins, self.outs, self.aliases, self.sems = list(ins), list(outs), dict(aliases), list(sems)
        self.start, self.finish = start, finish
        self.results = None


class _CommPlumbing:
    def __init__(self, comms, n_in, n_out, n_scratch):
        self.comms = list(comms)
        self.n_in, self.n_out, self.n_scratch = n_in, n_out, n_scratch
        self.args = [a for cm in self.comms for a in cm.ins]
        self.out_shape = [o for cm in self.comms for o in cm.outs]
        self.scratch = [s for cm in self.comms for s in cm.sems]
        self.aliases = {}
        i0, o0 = n_in, n_out
        for cm in self.comms:
            for a, b in cm.aliases.items():
                self.aliases[i0 + a] = o0 + b
            i0 += len(cm.ins)
            o0 += len(cm.outs)

    def _parts(self, in_refs, out_refs, scratch_refs):
        parts = []
        i0, o0, s0 = self.n_in, self.n_out, self.n_scratch
        for cm in self.comms:
            parts.append((in_refs[i0:i0 + len(cm.ins)], out_refs[o0:o0 + len(cm.outs)],
                          scratch_refs[s0:s0 + len(cm.sems)]))
            i0 += len(cm.ins)
            o0 += len(cm.outs)
            s0 += len(cm.sems)
        return parts

    def start_at(self, in_refs, out_refs, scratch_refs, first):
        if self.comms:
            parts = self._parts(in_refs, out_refs, scratch_refs)

            @pl.when(first)
            def _():
                for cm, part in zip(self.comms, parts):
                    cm.start(*part)

    def finish_at(self, in_refs, out_refs, scratch_refs, last):
        if self.comms:
            parts = self._parts(in_refs, out_refs, scratch_refs)

            @pl.when(last)
            def _():
                for cm, part in zip(self.comms, parts):
                    cm.finish(*part)

    def deliver(self, results):
        o0 = self.n_out
        for cm in self.comms:
            cm.results = list(results[o0:o0 + len(cm.outs)])
            o0 += len(cm.outs)
        return list(results[:self.n_out])


def _swap_ij(spec):
    index_map = spec.index_map
    return pl.BlockSpec(spec.block_shape, lambda j, i, k: index_map(i, j, k))


MXU_COLUMNS_V7X = 256


def _mm(name, mode, grid, pairs, extras, outs, acc_shape, epilogue, steps=None, comms=(), j_outer=False,
        chunked=False):
    ni, nj, nk = grid
    n_in = 2 * len(pairs) + len(extras)
    n_out = len(outs)
    tn = acc_shape[1]
    col_chunks = None
    if chunked:
        assert nk == 1 and steps is None and mode in ("nn", "nt")
        col_chunks = [(c0, min(MXU_COLUMNS_V7X, tn - c0)) for c0 in range(0, tn, MXU_COLUMNS_V7X)]
    n_acc = 0 if chunked else (len(pairs) if steps is None else 1)
    plumb = _CommPlumbing(comms, n_in, n_out, n_acc)
    n_all_in = n_in + len(plumb.args)
    n_all_out = n_out + len(plumb.out_shape)
    if j_outer:
        grid = (nj, ni, nk)
        pairs = [(a, _swap_ij(a_spec), b, _swap_ij(b_spec)) for a, a_spec, b, b_spec in pairs]
        extras = [(e, _swap_ij(e_spec)) for e, e_spec in extras]
        outs = [(o, _swap_ij(o_spec)) for o, o_spec in outs]

    def body(*refs):
        in_refs = refs[:n_all_in]
        out_refs = refs[n_all_in:n_all_in + n_all_out]
        scratch = refs[n_all_in + n_all_out:]
        accs = scratch[:n_acc]
        i = pl.program_id(1 if j_outer else 0)
        j = pl.program_id(0 if j_outer else 1)
        k = pl.program_id(2)
        plumb.start_at(in_refs, out_refs, scratch, (i == 0) & (j == 0) & (k == 0))

        def contrib(p, acc):
            acc[...] += _dot(in_refs[2 * p][...], in_refs[2 * p + 1][...], mode)

        if col_chunks:
            def cols(ref, c0, cs):
                if ref.shape[-1] != tn:
                    return ref
                return ref.at[(slice(None),) * (len(ref.shape) - 1) + (pl.ds(c0, cs),)]

            lhs = [in_refs[2 * p][...] for p in range(len(pairs))]
            for ci, (c0, cs) in enumerate(col_chunks):
                vals = []
                for p in range(len(pairs)):
                    b_ref = in_refs[2 * p + 1]
                    rhs = b_ref[:, c0:c0 + cs] if mode == "nn" else b_ref[c0:c0 + cs, :]
                    vals.append(_dot(lhs[p], rhs, mode))
                epilogue(vals, [cols(r, c0, cs) for r in in_refs[2 * len(pairs):n_in]],
                         [cols(r, c0, cs) for r in out_refs[:n_out]], (i, j * len(col_chunks) + ci))
        else:
            @pl.when(k == 0)
            def _():
                for acc in accs:
                    acc[...] = jnp.zeros(acc.shape, F32)

            if steps is None:
                for p in range(len(pairs)):
                    contrib(p, accs[p])
            else:
                lo = 0
                for p, n in enumerate(steps):
                    pl.when((k >= lo) & (k < lo + n))(functools.partial(contrib, p, accs[0]))
                    lo += n

            @pl.when(k == nk - 1)
            def _():
                epilogue([acc[...] for acc in accs], in_refs[2 * len(pairs):n_in], out_refs[:n_out], (i, j))

        plumb.finish_at(in_refs, out_refs, scratch, (i == ni - 1) & (j == nj - 1) & (k == nk - 1))

    args, in_specs = [], []
    for a, a_spec, b, b_spec in pairs:
        args += [a, b]
        in_specs += [a_spec, b_spec]
    for e, e_spec in extras:
        args.append(e)
        in_specs.append(e_spec)
    res = pl.pallas_call(
        body,
        name=name,
        grid=grid,
        in_specs=in_specs + [ANY] * len(plumb.args),
        out_specs=[s for _, s in outs] + [ANY] * len(plumb.out_shape),
        out_shape=[o for o, _ in outs] + plumb.out_shape,
        scratch_shapes=[pltpu.VMEM(acc_shape, F32) for _ in range(n_acc)] + plumb.scratch,
        input_output_aliases=plumb.aliases,
        compiler_params=_params(),
    )(*args, *plumb.args)
    return plumb.deliver(res)


def _sds(shape, dtype):
    return jax.ShapeDtypeStruct(shape, dtype)


def _accumulate(ref, value, first):
    @pl.when(first)
    def _():
        ref[...] = value

    @pl.when(jnp.logical_not(first))
    def _():
        ref[...] += value


def _rms_fwd(name, x, gain, comms=()):
    t, d = x.shape
    tm = _pick(t, ROW_TILE, 8)
    steps = t // tm
    plumb = _CommPlumbing(comms, 2, 1, 0)
    n_all_in = 2 + len(plumb.args)
    n_all_out = 1 + len(plumb.out_shape)

    def body(*refs):
        x_ref, g_ref = refs[:2]
        y_ref = refs[n_all_in]
        comm_refs = (refs[:n_all_in], refs[n_all_in:n_all_in + n_all_out], refs[n_all_in + n_all_out:])
        i = pl.program_id(0)
        plumb.start_at(*comm_refs, i == 0)
        xv = x_ref[...]
        rstd = lax.rsqrt(jnp.mean(xv * xv, axis=-1, keepdims=True) + EPS)
        y_ref[...] = (xv * rstd * g_ref[...]).astype(BF16)
        plumb.finish_at(*comm_refs, i == steps - 1)

    res = pl.pallas_call(
        body, name=name, grid=(steps,),
        in_specs=[pl.BlockSpec((tm, d), lambda i: (i, 0)), pl.BlockSpec((1, d), lambda i: (0, 0))]
        + [ANY] * len(plumb.args),
        out_specs=[pl.BlockSpec((tm, d), lambda i: (i, 0))] + [ANY] * len(plumb.out_shape),
        out_shape=[_sds((t, d), BF16)] + plumb.out_shape,
        scratch_shapes=plumb.scratch,
        input_output_aliases=plumb.aliases,
        compiler_params=_params(),
    )(x, gain, *plumb.args)
    return plumb.deliver(res)[0]


def _rms_bwd_epilogue(accs, extras, outs, ij):
    x_ref, g_ref, r_ref = extras
    dh_ref, dhb_ref, dg_ref = outs
    dn = accs[0]
    xv = x_ref[...]
    rstd = lax.rsqrt(jnp.mean(xv * xv, axis=-1, keepdims=True) + EPS)
    xhat = xv * rstd
    gd = dn * g_ref[...]
    dx = rstd * (gd - xhat * jnp.mean(gd * xhat, axis=-1, keepdims=True))
    dh = r_ref[...] + dx
    dh_ref[...] = dh
    dhb_ref[...] = dh.astype(BF16)
    _accumulate(dg_ref, jnp.sum(dn * xhat, axis=0, keepdims=True), ij[0] == 0)


def _rms_bwd_io(x, gain, dres, tm):
    t, d = x.shape
    row = pl.BlockSpec((tm, d), lambda i, j, k: (i, 0))
    extras = [(x, row), (gain, pl.BlockSpec((1, d), lambda i, j, k: (0, 0))), (dres, row)]
    outs = [(_sds((t, d), F32), row), (_sds((t, d), BF16), row),
            (_sds((1, d), F32), pl.BlockSpec((1, d), lambda i, j, k: (0, 0)))]
    return extras, outs


def _ffn_fwd(tag, h, gain, wgu, wd, hooks):
    t, d = h.shape
    fs = wgu.shape[2]
    f = 2 * fs
    tm = _pick(t, ROW_TILE, 8)
    n = _rms_fwd(tag + "_norm", h, gain)

    def up_epilogue(accs, extras, outs, ij):
        g, u = accs
        gu_ref, a_ref = outs
        gu_ref[0] = g.astype(BF16)
        gu_ref[1] = u.astype(BF16)
        a_ref[...] = (g * jax.nn.sigmoid(g) * u).astype(BF16)

    a_spec = pl.BlockSpec((tm, d), lambda i, j, k: (i, 0))
    gu, a = _mm(
        tag + "_up", "nn", (t // tm, 2, 1),
        [(n, a_spec, wgu, pl.BlockSpec((None, d, fs), lambda i, j, k: (j, 0, 0))),
         (n, a_spec, wgu, pl.BlockSpec((None, d, fs), lambda i, j, k: (j + 2, 0, 0)))],
        [],
        [(_sds((2, t, f), BF16), pl.BlockSpec((2, tm, fs), lambda i, j, k: (0, i, j))),
         (_sds((t, f), BF16), pl.BlockSpec((tm, fs), lambda i, j, k: (i, j)))],
        (tm, fs), up_epilogue, comms=hooks.get("up", lambda: ())(), j_outer=True, chunked=True)

    def down_epilogue(accs, extras, outs, ij):
        outs[0][...] = extras[0][...] + 0.5 * accs[0]


    row = pl.BlockSpec((tm, d), lambda i, j, k: (i, 0))
    (h_new,) = _mm(
        tag + "_down", "nn", (t // tm, 1, 1),
        [(a, pl.BlockSpec((tm, f), lambda i, j, k: (i, 0)), wd, pl.BlockSpec((f, d), lambda i, j, k: (0, 0)))],
        [(h, row)], [(_sds((t, d), F32), row)], (tm, d), down_epilogue, comms=hooks.get("down", lambda: ())())
    return h_new, (n, gu, a)


def _ffn_bwd(tag, dh, dh_b, h, gain, wgu, wd, saved, hooks):
    n, gu, a = saved
    t, d = h.shape
    fs = wgu.shape[2]
    f = 2 * fs
    tm = _pick(t, ROW_TILE, 8)
    tk = _pick(t, ROW_TILE, 8)

    def dact_epilogue(accs, extras, outs, ij):
        da = 0.5 * accs[0]
        g = extras[0][0].astype(F32)
        u = extras[0][1].astype(F32)
        sg = jax.nn.sigmoid(g)
        outs[0][0] = (da * u * sg * (1.0 + g * (1.0 - sg))).astype(BF16)
        outs[0][1] = (da * g * sg).astype(BF16)

    gu_spec = pl.BlockSpec((2, tm, fs), lambda i, j, k: (0, i, j))
    (dgu,) = _mm(
        tag + "_dact", "nt", (t // tm, 2, 1),
        [(dh_b, pl.BlockSpec((tm, d), lambda i, j, k: (i, 0)), wd, pl.BlockSpec((fs, d), lambda i, j, k: (j, 0)))],
        [(gu, gu_spec)], [(_sds((2, t, f), BF16), gu_spec)], (tm, fs), dact_epilogue, j_outer=True, chunked=True,
        comms=hooks.get("dact", lambda: ())())

    def cast_epilogue(accs, extras, outs, ij):
        outs[0][...] = accs[0].astype(BF16)

    (dwgu,) = _mm(
        tag + "_dwgu", "tn", (1, 4, t // tk),
        [(n, pl.BlockSpec((tk, d), lambda i, j, k: (k, 0)),
          dgu, pl.BlockSpec((None, tk, fs), lambda i, j, k: (j // 2, k, j % 2)))],
        [], [(_sds((4, d, fs), BF16), pl.BlockSpec((None, d, fs), lambda i, j, k: (j, 0, 0)))], (d, fs), cast_epilogue,
        comms=hooks.get("dwgu", lambda: ())())

    def half_epilogue(accs, extras, outs, ij):
        outs[0][...] = (0.5 * accs[0]).astype(BF16)

    (dwd,) = _mm(
        tag + "_dwd", "tn", (2, 1, t // tk),
        [(a, pl.BlockSpec((tk, fs), lambda i, j, k: (k, i)), dh_b, pl.BlockSpec((tk, d), lambda i, j, k: (k, 0)))],
        [], [(_sds((f, d), BF16), pl.BlockSpec((fs, d), lambda i, j, k: (i, 0)))], (fs, d), half_epilogue,
        comms=hooks.get("dwd", lambda g: ())(dwgu))

    tmn = _pick(t, ROW_TILE, 8)
    extras, outs = _rms_bwd_io(h, gain, dh, tmn)
    dh_in, dh_in_b, dgain = _mm(
        tag + "_dnorm", "nt", (t // tmn, 1, 4),
        [(dgu, pl.BlockSpec((None, tmn, fs), lambda i, j, k: (k // 2, i, k % 2)),
          wgu, pl.BlockSpec((None, d, fs), lambda i, j, k: (k, 0, 0)))],
        extras, outs, (tmn, d), _rms_bwd_epilogue, comms=hooks.get("dnorm", lambda g, w: ())(dwgu, dwd))
    return dh_in, dh_in_b, dgain, dwgu, dwd


def _lane_lo(shape):
    return lax.broadcasted_iota(jnp.int32, shape, 1) < HEAD_DIM


def _pair_norm(xv, gain):
    lo = _lane_lo(xv.shape)
    x2 = xv * xv
    ms_lo = jnp.sum(jnp.where(lo, x2, 0.0), axis=-1, keepdims=True) * (1.0 / HEAD_DIM)
    ms_hi = jnp.sum(jnp.where(lo, 0.0, x2), axis=-1, keepdims=True) * (1.0 / HEAD_DIM)
    rstd = jnp.where(lo, lax.rsqrt(ms_lo + EPS), lax.rsqrt(ms_hi + EPS))
    xhat = xv * rstd
    return xhat * gain, xhat, rstd


def _pair_norm_bwd(dn, xhat, rstd, gain):
    lo = _lane_lo(dn.shape)
    gd = dn * gain
    t = gd * xhat
    m_lo = jnp.sum(jnp.where(lo, t, 0.0), axis=-1, keepdims=True) * (1.0 / HEAD_DIM)
    m_hi = jnp.sum(jnp.where(lo, 0.0, t), axis=-1, keepdims=True) * (1.0 / HEAD_DIM)
    dx = rstd * (gd - xhat * jnp.where(lo, m_lo, m_hi))
    return dx, jnp.sum(dn * xhat, axis=0, keepdims=True)


def _half(xv, hi):
    lo = _lane_lo(xv.shape)
    return jnp.where(lo, 0, xv) if hi else jnp.where(lo, xv, 0)


def _attn_window(i, prev):
    q0 = i * Q_BLOCK
    start = jnp.maximum(q0 - prev, 0)
    off = start - (q0 - prev)
    return pl.multiple_of(start, Q_BLOCK), pl.multiple_of(off, Q_BLOCK)


def _attn_specs(cfg, s, nq):
    kw = cfg["kw"]
    q_spec = pl.BlockSpec((Q_BLOCK, A_WIDTH), lambda b, i: (b * nq + i, cfg["qblk"]))
    k_spec = pl.BlockSpec((s, kw), lambda b, i: (b, cfg["kblk"]))
    v_spec = pl.BlockSpec((s, kw), lambda b, i: (b, cfg["vblk"]))
    return q_spec, k_spec, v_spec


def _const_spec(shape):
    return pl.BlockSpec(shape, lambda b, i: (0,) * len(shape))


KEY_CHUNK = 128


def _pair_bias(bias_t):
    wext = bias_t.shape[1]
    return jnp.transpose(bias_t.reshape(N_HEADS // 2, 2, wext, Q_BLOCK), (0, 2, 1, 3)).reshape(
        N_HEADS // 2, wext, 2 * Q_BLOCK)


def _unpair_bias(db2):
    wext = db2.shape[1]
    return jnp.transpose(db2.reshape(N_HEADS // 2, wext, 2, Q_BLOCK), (0, 2, 1, 3)).reshape(N_HEADS, wext, Q_BLOCK)


def _pair_rows(rows):
    two = rows.reshape(N_HEADS // 2, 2 * rows.shape[1])
    return jnp.broadcast_to(two[:, None, :], (N_HEADS // 2, 8, two.shape[1]))


def _sub_lo(shape):
    return lax.broadcasted_iota(jnp.int32, shape, 0) < HEAD_DIM


def _by_half(lo_row, hi_row, rows):
    return jnp.where(_sub_lo((rows, lo_row.shape[1])), lo_row, hi_row)


def _stack_pair(xn, jq, group):
    parts = []
    for hq in range(2):
        hk = ((2 * jq + hq) // group) % 2
        xm = _half(xn, hq)
        if hq != hk:
            xm = pltpu.roll(xm, HEAD_DIM, 1)
        parts.append(xm)
    return jnp.concatenate(parts, axis=0).astype(BF16)


def _place_transposed(blk, dst_ref, c, heads, group):
    bt = blk.T
    lo = _sub_lo(bt.shape)
    for h in heads:
        src_hi = ((h // group) % 2) == 1
        part = jnp.where(lo, 0.0, bt) if src_hi else jnp.where(lo, bt, 0.0)
        if src_hi != (h % 2 == 1):
            part = pltpu.roll(part, HEAD_DIM, 0)
        dst_ref[h, c] = part.astype(BF16)


def _attn_fwd(name, qkv, bias2, sink2, gq, gk, cfg, n_batch, comms=()):
    t = qkv.shape[0]
    s = t // n_batch
    nq = s // Q_BLOCK
    nkc = s // KEY_CHUNK
    prev, group, kw = cfg["prev"], cfg["group"], cfg["kw"]
    w = prev + Q_BLOCK
    n_chunks = w // KEY_CHUNK
    wext = bias2.shape[1]
    plumb = _CommPlumbing(comms, 7, 2, 4)
    n_all_in = 7 + len(plumb.args)
    n_all_out = 2 + len(plumb.out_shape)

    def body(*refs):
        q_ref, k_ref, v_ref, bias_ref, sink_ref, gq_ref, gk_ref = refs[:7]
        y_ref, lse_ref = refs[n_all_in:n_all_in + 2]
        kn_ref, vt_ref, s_ref, pst_ref = refs[n_all_in + n_all_out:n_all_in + n_all_out + 4]
        i = pl.program_id(1)
        comm_refs = (refs[:n_all_in], refs[n_all_in:n_all_in + n_all_out], refs[n_all_in + n_all_out:])
        plumb.start_at(*comm_refs, (pl.program_id(0) == 0) & (i == 0))

        @pl.when(i == 0)
        def _():
            for jk in range(kw // PAIR):
                cols = pl.ds(jk * PAIR, PAIR)
                heads = [h for h in range(N_HEADS) if (h // group) // 2 == jk]
                kn, _, _ = _pair_norm(k_ref[:, cols].astype(F32), gk_ref[...])
                kn_ref[:, cols] = kn.astype(BF16)
                for c in range(nkc):
                    _place_transposed(v_ref[pl.ds(c * KEY_CHUNK, KEY_CHUNK), cols].astype(F32), vt_ref, c, heads, group)

        start, off = _attn_window(i, prev)
        c0 = start // KEY_CHUNK
        sub8 = lax.broadcasted_iota(jnp.int32, (N_HEADS, Q_BLOCK), 0)
        lse = jnp.zeros((N_HEADS, Q_BLOCK), F32)
        for jq in range(N_HEADS // 2):
            kcols = pl.ds((((2 * jq) // group) // 2) * PAIR, PAIR)
            qn, _, _ = _pair_norm(q_ref[:, pl.ds(jq * PAIR, PAIR)].astype(F32), gq_ref[...])
            qs = _stack_pair(qn * ATTN_SCALE, jq, group)
            s_ref[...] = _dot(kn_ref[pl.ds(start, w), kcols], qs, "nt")
            m = sink_ref[jq, 0:1, :]
            for c in range(n_chunks):
                r = pl.ds(c * KEY_CHUNK, KEY_CHUNK)
                s2 = s_ref[r, :] + bias_ref[jq, pl.ds(off + c * KEY_CHUNK, KEY_CHUNK), :]
                s_ref[r, :] = s2
                m = jnp.maximum(m, jnp.max(s2, axis=0, keepdims=True))
            l = jnp.exp(sink_ref[jq, 0:1, :] - m)
            for c in range(n_chunks):
                p = jnp.exp(s_ref[pl.ds(c * KEY_CHUNK, KEY_CHUNK), :] - m)
                l = l + jnp.sum(p, axis=0, keepdims=True)
                pst_ref[pl.ds(2 * c * KEY_CHUNK, KEY_CHUNK), :] = p[:, :Q_BLOCK].astype(BF16)
                pst_ref[pl.ds((2 * c + 1) * KEY_CHUNK, KEY_CHUNK), :] = p[:, Q_BLOCK:].astype(BF16)
            vl = jnp.concatenate([vt_ref[2 * jq + hq, c0 + c] for c in range(n_chunks) for hq in range(2)], axis=1)
            ot = _dot(vl, pst_ref[...], "nn")
            inv = 1.0 / l
            ot = ot * _by_half(inv[:, :Q_BLOCK], inv[:, Q_BLOCK:], PAIR)
            y_ref[:, pl.ds(jq * PAIR, PAIR)] = ot.T.astype(BF16)
            lse2 = m + jnp.log(l)
            lse = jnp.where(sub8 == 2 * jq, lse2[:, :Q_BLOCK], lse)
            lse = jnp.where(sub8 == 2 * jq + 1, lse2[:, Q_BLOCK:], lse)
        lse_ref[...] = lse
        plumb.finish_at(*comm_refs, (pl.program_id(0) == n_batch - 1) & (i == nq - 1))

    q_spec, k_spec, v_spec = _attn_specs(cfg, s, nq)
    res = pl.pallas_call(
        body, name=name, grid=(n_batch, nq),
        in_specs=[q_spec, k_spec, v_spec, _const_spec((N_HEADS // 2, wext, 2 * Q_BLOCK)),
                  _const_spec((N_HEADS // 2, 8, 2 * Q_BLOCK)), _const_spec((1, PAIR)), _const_spec((1, PAIR))]
        + [ANY] * len(plumb.args),
        out_specs=[pl.BlockSpec((Q_BLOCK, A_WIDTH), lambda b, i: (b * nq + i, 0)),
                   pl.BlockSpec((None, N_HEADS, Q_BLOCK), lambda b, i: (b * nq + i, 0, 0))]
        + [ANY] * len(plumb.out_shape),
        out_shape=[_sds((t, A_WIDTH), BF16), _sds((t // Q_BLOCK, N_HEADS, Q_BLOCK), F32)] + plumb.out_shape,
        scratch_shapes=[pltpu.VMEM((s, kw), BF16), pltpu.VMEM((N_HEADS, nkc, PAIR, KEY_CHUNK), BF16),
                        pltpu.VMEM((w, 2 * Q_BLOCK), F32), pltpu.VMEM((2 * w, Q_BLOCK), BF16)] + plumb.scratch,
        input_output_aliases=plumb.aliases,
        compiler_params=_params(),
    )(qkv, qkv, qkv, bias2, sink2, gq, gk, *plumb.args)
    return plumb.deliver(res)


def _attn_bwd(name, qkv, bias2, sink2, gq, gk, y, dy, lse, cfg, n_batch, want_dbias, comms=()):
    t = qkv.shape[0]
    s = t // n_batch
    nq = s // Q_BLOCK
    nkc = s // KEY_CHUNK
    prev, group, kw = cfg["prev"], cfg["group"], cfg["kw"]
    w = prev + Q_BLOCK
    n_chunks = w // KEY_CHUNK
    wext = bias2.shape[1]
    plumb = _CommPlumbing(comms, 10, 7, 9)
    n_all_in = 10 + len(plumb.args)
    n_all_out = 7 + len(plumb.out_shape)

    def body(*refs):
        q_ref, k_ref, v_ref, bias_ref, sink_ref, gq_ref, gk_ref, y_ref, dy_ref, lse_ref = refs[:10]
        dq_ref, dk_ref, dv_ref, db_ref, dsink_ref, dgq_ref, dgk_ref = refs[n_all_in:n_all_in + 7]
        kn_ref, knt_ref, dkn_ref, dvs_ref, s_ref, dp_ref, pb_ref, dsb_ref, dst_ref = \
            refs[n_all_in + n_all_out:n_all_in + n_all_out + 9]
        b = pl.program_id(0)
        i = pl.program_id(1)
        first = (b == 0) & (i == 0)
        comm_refs = (refs[:n_all_in], refs[n_all_in:n_all_in + n_all_out], refs[n_all_in + n_all_out:])
        plumb.start_at(*comm_refs, first)

        @pl.when(i == 0)
        def _():
            for jk in range(kw // PAIR):
                cols = pl.ds(jk * PAIR, PAIR)
                heads = [h for h in range(N_HEADS) if (h // group) // 2 == jk]
                for c in range(nkc):
                    rows = pl.ds(c * KEY_CHUNK, KEY_CHUNK)
                    kn, _, _ = _pair_norm(k_ref[rows, cols].astype(F32), gk_ref[...])
                    kn_ref[rows, cols] = kn.astype(BF16)
                    _place_transposed(kn, knt_ref, c, heads, group)
            dkn_ref[...] = jnp.zeros(dkn_ref.shape, F32)
            dvs_ref[...] = jnp.zeros(dvs_ref.shape, F32)

        @pl.when(first)
        def _():
            db_ref[...] = jnp.zeros(db_ref.shape, F32)
            dsink_ref[...] = jnp.zeros(dsink_ref.shape, F32)
            dgq_ref[...] = jnp.zeros(dgq_ref.shape, F32)
            dgk_ref[...] = jnp.zeros(dgk_ref.shape, F32)

        start, off = _attn_window(i, prev)
        c0 = start // KEY_CHUNK
        for jq in range(N_HEADS // 2):
            cols = pl.ds(jq * PAIR, PAIR)
            kcols = pl.ds((((2 * jq) // group) // 2) * PAIR, PAIR)
            qn, q_hat, q_rstd = _pair_norm(q_ref[:, cols].astype(F32), gq_ref[...])
            qs = _stack_pair(qn * ATTN_SCALE, jq, group)
            do_pair = dy_ref[:, cols].astype(F32)
            dos = _stack_pair(do_pair, jq, group)
            prod_t = (do_pair * y_ref[:, cols].astype(F32)).T
            lo = _sub_lo(prod_t.shape)
            delta2 = jnp.concatenate([jnp.sum(jnp.where(lo, prod_t, 0.0), axis=0, keepdims=True),
                                      jnp.sum(jnp.where(lo, 0.0, prod_t), axis=0, keepdims=True)], axis=1)
            lse2 = jnp.concatenate([lse_ref[2 * jq:2 * jq + 1, :], lse_ref[2 * jq + 1:2 * jq + 2, :]], axis=1)
            dsk = -jnp.exp(sink_ref[jq, 0:1, :] - lse2) * delta2
            dsink_ref[2 * jq:2 * jq + 1, :] += dsk[:, :Q_BLOCK]
            dsink_ref[2 * jq + 1:2 * jq + 2, :] += dsk[:, Q_BLOCK:]
            rows_w = pl.ds(start, w)
            s_ref[...] = _dot(kn_ref[rows_w, kcols], qs, "nt")
            dp_ref[...] = _dot(v_ref[rows_w, kcols], dos, "nt")
            for c in range(n_chunks):
                r = pl.ds(c * KEY_CHUNK, KEY_CHUNK)
                brows = pl.ds(off + c * KEY_CHUNK, KEY_CHUNK)
                p = jnp.exp(s_ref[r, :] + bias_ref[jq, brows, :] - lse2)
                ds = p * (dp_ref[r, :] - delta2)
                if want_dbias:
                    db_ref[jq, brows, :] += ds
                ds_b = ds.astype(BF16)
                pb_ref[r, :] = p.astype(BF16)
                dsb_ref[r, :] = ds_b
                dst_ref[pl.ds(2 * c * KEY_CHUNK, KEY_CHUNK), :] = ds_b[:, :Q_BLOCK]
                dst_ref[pl.ds((2 * c + 1) * KEY_CHUNK, KEY_CHUNK), :] = ds_b[:, Q_BLOCK:]
            dkn_ref[rows_w, kcols] += _dot(dsb_ref[...], qs, "nn")
            dvs_ref[rows_w, kcols] += _dot(pb_ref[...], dos, "nn")
            kl = jnp.concatenate([knt_ref[2 * jq + hq, c0 + c] for c in range(n_chunks) for hq in range(2)], axis=1)
            dqt = _dot(kl, dst_ref[...], "nn")
            dq_raw, dg = _pair_norm_bwd(dqt.T * ATTN_SCALE, q_hat, q_rstd, gq_ref[...])
            dq_ref[:, cols] = dq_raw.astype(BF16)
            dgq_ref[...] += dg

        @pl.when(i == nq - 1)
        def _():
            for jk in range(kw // PAIR):
                kcols = pl.ds(jk * PAIR, PAIR)
                _, k_hat, k_rstd = _pair_norm(k_ref[:, kcols].astype(F32), gk_ref[...])
                dk_raw, dg = _pair_norm_bwd(dkn_ref[:, kcols], k_hat, k_rstd, gk_ref[...])
                dk_ref[:, kcols] = dk_raw.astype(BF16)
                dgk_ref[...] += dg
            dv_ref[...] = dvs_ref[...].astype(BF16)

        plumb.finish_at(*comm_refs, (b == n_batch - 1) & (i == nq - 1))

    q_spec, k_spec, v_spec = _attn_specs(cfg, s, nq)
    row = pl.BlockSpec((Q_BLOCK, A_WIDTH), lambda b, i: (b * nq + i, 0))
    kv_out = pl.BlockSpec((s, kw), lambda b, i: (b, 0))
    pair_bias = _const_spec((N_HEADS // 2, wext, 2 * Q_BLOCK))
    res = pl.pallas_call(
        body, name=name, grid=(n_batch, nq),
        in_specs=[q_spec, k_spec, v_spec, pair_bias, _const_spec((N_HEADS // 2, 8, 2 * Q_BLOCK)),
                  _const_spec((1, PAIR)), _const_spec((1, PAIR)), row, row,
                  pl.BlockSpec((None, N_HEADS, Q_BLOCK), lambda b, i: (b * nq + i, 0, 0))] + [ANY] * len(plumb.args),
        out_specs=[row, kv_out, kv_out, pair_bias, _const_spec((N_HEADS, 128)),
                   _const_spec((1, PAIR)), _const_spec((1, PAIR))] + [ANY] * len(plumb.out_shape),
        out_shape=[_sds((t, A_WIDTH), BF16), _sds((t, kw), BF16), _sds((t, kw), BF16),
                   _sds((N_HEADS // 2, wext, 2 * Q_BLOCK), F32), _sds((N_HEADS, 128), F32),
                   _sds((1, PAIR), F32), _sds((1, PAIR), F32)] + plumb.out_shape,
        scratch_shapes=[pltpu.VMEM((s, kw), BF16), pltpu.VMEM((N_HEADS, nkc, PAIR, KEY_CHUNK), BF16),
                        pltpu.VMEM((s, kw), F32), pltpu.VMEM((s, kw), F32),
                        pltpu.VMEM((w, 2 * Q_BLOCK), F32), pltpu.VMEM((w, 2 * Q_BLOCK), F32),
                        pltpu.VMEM((w, 2 * Q_BLOCK), BF16), pltpu.VMEM((w, 2 * Q_BLOCK), BF16),
                        pltpu.VMEM((2 * w, Q_BLOCK), BF16)] + plumb.scratch,
        input_output_aliases=plumb.aliases,
        compiler_params=_params(),
    )(qkv, qkv, qkv, bias2, sink2, gq, gk, y, dy, lse, *plumb.args)
    return plumb.deliver(res)


def _band_tables(prev_chunks):
    prev = prev_chunks * CHUNK
    wext = 2 * prev + Q_BLOCK
    jj = np.arange(wext)[:, None]
    ii = np.arange(Q_BLOCK)[None, :]
    dist = prev + ii - jj
    rel_chunk = (prev // CHUNK + ii // CHUNK) - jj // CHUNK
    allowed = (rel_chunk >= 0) & (rel_chunk <= prev_chunks)
    return dist, allowed


def _alibi_slopes():
    return np.array([2.0 ** (-8.0 * (h + 1) / N_HEADS) for h in range(N_HEADS)], dtype=np.float32)


def _diag_onehot(prev, wext):
    n_diag = wext + Q_BLOCK - 1
    idx = np.clip(prev + Q_BLOCK - 1 - np.arange(n_diag), -A_MAX_REL, A_MAX_REL) + A_MAX_REL
    onehot = np.zeros((n_diag, 2 * A_MAX_REL + 1), np.float32)
    onehot[np.arange(n_diag), idx] = 1.0
    return onehot


def _bias_a(rel_bias):
    prev = A_PREV_CHUNKS * CHUNK
    _, allowed = _band_tables(A_PREV_CHUNKS)
    wext = allowed.shape[0]
    n_diag = wext + Q_BLOCK - 1
    seq = jnp.dot(rel_bias, jnp.asarray(_diag_onehot(prev, wext).T), precision=lax.Precision.HIGHEST)
    seq = jnp.pad(seq, ((0, 0), (0, 1)))
    rows = jnp.broadcast_to(seq[:, None, :], (N_HEADS, Q_BLOCK, n_diag + 1)).reshape(N_HEADS, -1)
    skew = rows[:, :Q_BLOCK * n_diag].reshape(N_HEADS, Q_BLOCK, n_diag)
    tile = jnp.transpose(skew[:, :, Q_BLOCK - 1:Q_BLOCK - 1 + wext], (0, 2, 1))
    return jnp.where(jnp.asarray(allowed)[None], tile, NEG_INF)


def _bias_b():
    dist, allowed = _band_tables(B_PREV_CHUNKS)
    bias = -_alibi_slopes()[:, None, None] * np.abs(dist).astype(np.float32)[None]
    return jnp.asarray(np.where(allowed[None], bias, np.float32(NEG_INF)).astype(np.float32))


def _rel_bias_grad(db_t):
    prev = A_PREV_CHUNKS * CHUNK
    wext = db_t.shape[1]
    n_diag = wext + Q_BLOCK - 1
    wp = n_diag + Q_BLOCK - 1
    xp = jnp.pad(jnp.transpose(db_t, (0, 2, 1)), ((0, 0), (0, 0), (Q_BLOCK - 1, Q_BLOCK - 1)))
    flat = jnp.pad(xp.reshape(N_HEADS, Q_BLOCK * wp), ((0, 0), (0, Q_BLOCK)))
    skew = flat.reshape(N_HEADS, Q_BLOCK, wp + 1)[:, :, :n_diag]
    diag = jnp.sum(skew, axis=1)
    return jnp.dot(diag, jnp.asarray(_diag_onehot(prev, wext)), precision=lax.Precision.HIGHEST)


def _ew(name, fn, ins, out_dtypes):
    r, c = ins[0].shape
    rb = _pick(r, max(16, (1 << 19) // c), 16)
    spec = pl.BlockSpec((rb, c), lambda i: (i, 0))

    def body(*refs):
        vals = fn(*[ref[...] for ref in refs[:len(ins)]])
        for ref, val in zip(refs[len(ins):], vals):
            ref[...] = val.astype(ref.dtype)

    return pl.pallas_call(
        body, name=name, grid=(r // rb,), in_specs=[spec] * len(ins), out_specs=[spec] * len(out_dtypes),
        out_shape=[_sds((r, c), dt) for dt in out_dtypes], compiler_params=_params(),
    )(*ins)


def _cast_into_slot(name, w, chip):
    r, c = w.shape
    rb = _pick(r, max(16, (1 << 19) // c), 16)

    def body(chip_ref, w_ref, o_ref):
        o_ref[...] = w_ref[...].astype(BF16)

    return pl.pallas_call(
        body, name=name,
        grid_spec=pltpu.PrefetchScalarGridSpec(
            num_scalar_prefetch=1, grid=(r // rb,),
            in_specs=[pl.BlockSpec((rb, c), lambda i, chip_ref: (i, 0))],
            out_specs=pl.BlockSpec((None, rb, c), lambda i, chip_ref: (chip_ref[0], i, 0))),
        out_shape=_sds((N_CHIPS, r, c), BF16), compiler_params=_params(),
    )(chip, w)


def _adamw_math(w, g, m, v):
    m = ADAM_B1 * m + (1.0 - ADAM_B1) * g
    v = ADAM_B2 * v + (1.0 - ADAM_B2) * (g * g)
    m_hat = m / (1.0 - ADAM_B1 ** ADAM_STEP)
    v_hat = v / (1.0 - ADAM_B2 ** ADAM_STEP)
    delta = -ADAM_LR * (m_hat / (jnp.sqrt(v_hat) + ADAM_EPS) + ADAM_WD * w)
    return delta, m, v


def _adamw_terms(name, terms, w, m, v):
    r, c = w.shape
    hr = r // 2
    rb = _pick(hr, max(16, (1 << 19) // c), 16)
    nb = hr // rb

    def body(t_ref, w_ref, m_ref, v_ref, g_ref, d_ref, nm_ref, nv_ref):
        g = t_ref[0].astype(F32)
        for k in range(1, N_CHIPS):
            g = g + t_ref[k].astype(F32)
        delta, nm, nv = _adamw_math(w_ref[...], g, m_ref[...], v_ref[...])
        g_ref[...] = g
        d_ref[...] = delta
        nm_ref[...] = nm
        nv_ref[...] = nv

    spec = pl.BlockSpec((rb, c), lambda h, i: (h * nb + i, 0))
    return pl.pallas_call(
        body, name=name, grid=(2, nb),
        in_specs=[pl.BlockSpec((None, N_CHIPS, rb, c), lambda h, i: (h, 0, i, 0)), spec, spec, spec],
        out_specs=[spec] * 4, out_shape=[_sds((r, c), F32)] * 4, compiler_params=_params(),
    )(terms, w, m, v)


def _mesh_place():
    x, y, c = lax.axis_index("x"), lax.axis_index("y"), lax.axis_index("c")
    chips = [(x, 1 - y), (1 - x, y), (1 - x, 1 - y)]
    return x, y, c, chips


def _all_gather_weights(bufs):
    n = len(bufs)

    def body(*refs):
        outs = refs[n:2 * n]
        ici_send, ici_recv, d2d_send, d2d_recv = refs[2 * n:]
        x, y, c, chips = _mesh_place()
        me = 2 * x + y
        sibling = (x, y, 1 - c)
        sent = []
        for wi in range(n):
            for k, (tx, ty) in enumerate(chips):
                own = outs[wi].at[me, c]
                cp = pltpu.make_async_remote_copy(
                    src_ref=own, dst_ref=own, send_sem=ici_send.at[wi * 3 + k], recv_sem=ici_recv.at[wi * 3 + k],
                    device_id=(tx, ty, c), device_id_type=MESH)
                cp.start()
                sent.append(cp)
        passed = []
        for wi in range(n):
            for k, (tx, ty) in enumerate(chips):
                slab = outs[wi].at[2 * tx + ty, c]
                pltpu.make_async_remote_copy(
                    src_ref=slab, dst_ref=slab, send_sem=ici_send.at[wi * 3 + k], recv_sem=ici_recv.at[wi * 3 + k],
                    device_id=(tx, ty, c), device_id_type=MESH).wait_recv()
                fw = pltpu.make_async_remote_copy(
                    src_ref=slab, dst_ref=slab, send_sem=d2d_send.at[wi * 3 + k], recv_sem=d2d_recv.at[wi * 3 + k],
                    device_id=sibling, device_id_type=MESH)
                fw.start()
                passed.append(fw)
        for wi in range(n):
            for k, (tx, ty) in enumerate(chips):
                slab = outs[wi].at[2 * tx + ty, 1 - c]
                pltpu.make_async_remote_copy(
                    src_ref=slab, dst_ref=slab, send_sem=d2d_send.at[wi * 3 + k], recv_sem=d2d_recv.at[wi * 3 + k],
                    device_id=sibling, device_id_type=MESH).wait_recv()
        for cp in sent + passed:
            cp.wait_send()

    return pl.pallas_call(
        body, name="all_gather_weights",
        in_specs=[ANY] * n, out_specs=[ANY] * n,
        out_shape=[_sds(g.shape, g.dtype) for g in bufs],
        scratch_shapes=[pltpu.SemaphoreType.DMA((3 * n,))] * 4,
        input_output_aliases={i: i for i in range(n)},
    )(*bufs)


def _run_comms(name, comms):
    plumb = _CommPlumbing(comms, 0, 0, 0)
    n_in, n_out = len(plumb.args), len(plumb.out_shape)

    def body(*refs):
        parts = []
        i0, o0, s0 = 0, n_in, n_in + n_out
        for cm in plumb.comms:
            parts.append((refs[i0:i0 + len(cm.ins)], refs[o0:o0 + len(cm.outs)], refs[s0:s0 + len(cm.sems)]))
            i0 += len(cm.ins)
            o0 += len(cm.outs)
            s0 += len(cm.sems)
        for cm, part in zip(plumb.comms, parts):
            cm.start(*part)
        for cm, part in zip(plumb.comms, parts):
            cm.finish(*part)

    res = pl.pallas_call(
        body, name=name, in_specs=[ANY] * n_in, out_specs=[ANY] * n_out, out_shape=plumb.out_shape,
        scratch_shapes=plumb.scratch, input_output_aliases=plumb.aliases,
    )(*plumb.args)
    plumb.deliver(res)


def _gather_ici(bufs, targets=(0, 1, 2)):
    n = len(bufs)

    def copies(outs, sems):
        send_sem, recv_sem = sems
        x, y, c, chips = _mesh_place()
        me = 2 * x + y
        sends, recvs = [], []
        for wi in range(n):
            for k in targets:
                tx, ty = chips[k]
                sems_k = dict(send_sem=send_sem.at[wi * 3 + k], recv_sem=recv_sem.at[wi * 3 + k],
                              device_id=(tx, ty, c), device_id_type=MESH)
                own = outs[wi].at[me, c]
                sends.append(pltpu.make_async_remote_copy(src_ref=own, dst_ref=own, **sems_k))
                slab = outs[wi].at[2 * tx + ty, c]
                recvs.append(pltpu.make_async_remote_copy(src_ref=slab, dst_ref=slab, **sems_k))
        return sends, recvs

    def start(ins, outs, sems):
        for cp in copies(outs, sems)[0]:
            cp.start()

    def finish(ins, outs, sems):
        sends, recvs = copies(outs, sems)
        for cp in recvs:
            cp.wait_recv()
        for cp in sends:
            cp.wait_send()

    return _Comm(bufs, [_sds(g.shape, g.dtype) for g in bufs], {i: i for i in range(n)},
                 [pltpu.SemaphoreType.DMA((3 * n,)), pltpu.SemaphoreType.DMA((3 * n,))], start, finish)


def _gather_d2d(gathered):
    n = len(gathered)

    def copies(outs, sems):
        send_sem, recv_sem = sems
        x, y, c, chips = _mesh_place()
        sends, recvs = [], []
        for wi in range(n):
            for k, (tx, ty) in enumerate(chips):
                sems_k = dict(send_sem=send_sem.at[wi * 3 + k], recv_sem=recv_sem.at[wi * 3 + k],
                              device_id=(x, y, 1 - c), device_id_type=MESH)
                mine = outs[wi].at[2 * tx + ty, c]
                theirs = outs[wi].at[2 * tx + ty, 1 - c]
                sends.append(pltpu.make_async_remote_copy(src_ref=mine, dst_ref=mine, **sems_k))
                recvs.append(pltpu.make_async_remote_copy(src_ref=theirs, dst_ref=theirs, **sems_k))
        return sends, recvs

    def start(ins, outs, sems):
        for cp in copies(outs, sems)[0]:
            cp.start()

    def finish(ins, outs, sems):
        sends, recvs = copies(outs, sems)
        for cp in recvs:
            cp.wait_recv()
        for cp in sends:
            cp.wait_send()

    return _Comm(gathered, [_sds(g.shape, g.dtype) for g in gathered], {i: i for i in range(n)},
                 [pltpu.SemaphoreType.DMA((3 * n,)), pltpu.SemaphoreType.DMA((3 * n,))], start, finish)


def _exchange_halves(grads):
    n = len(grads)

    def copies(ins, outs, sems):
        send_sem, recv_sem = sems
        x, y, c, _ = _mesh_place()
        return [pltpu.make_async_remote_copy(
            src_ref=ins[wi].at[t, 1 - c], dst_ref=outs[wi].at[t],
            send_sem=send_sem.at[wi * N_CHIPS + t], recv_sem=recv_sem.at[wi * N_CHIPS + t],
            device_id=(x, y, 1 - c), device_id_type=MESH) for wi in range(n) for t in range(N_CHIPS)]

    def start(ins, outs, sems):
        for cp in copies(ins, outs, sems):
            cp.start()

    def finish(ins, outs, sems):
        for cp in copies(ins, outs, sems):
            cp.wait()

    return _Comm(grads, [_sds((N_CHIPS,) + g.shape[2:], g.dtype) for g in grads], {},
                 [pltpu.SemaphoreType.DMA((N_CHIPS * n,)), pltpu.SemaphoreType.DMA((N_CHIPS * n,))], start, finish)


def _scatter_ici(sums):
    n = len(sums)

    def copies(ins, outs, sems):
        local_sem, send_sem, recv_sem = sems
        x, y, c, chips = _mesh_place()
        me = 2 * x + y
        local, sends, recvs = [], [], []
        for wi in range(n):
            local.append(pltpu.make_async_copy(ins[wi].at[me], outs[wi].at[c, 0], local_sem.at[wi]))
            for k, (tx, ty) in enumerate(chips):
                sems_k = dict(send_sem=send_sem.at[wi * 3 + k], recv_sem=recv_sem.at[wi * 3 + k],
                              device_id=(tx, ty, c), device_id_type=MESH)
                land = outs[wi].at[c, k + 1]
                sends.append(pltpu.make_async_remote_copy(src_ref=ins[wi].at[2 * tx + ty], dst_ref=land, **sems_k))
                recvs.append(pltpu.make_async_remote_copy(src_ref=land, dst_ref=land, **sems_k))
        return local, sends, recvs

    def start(ins, outs, sems):
        local, sends, _ = copies(ins, outs, sems)
        for cp in local + sends:
            cp.start()

    def finish(ins, outs, sems):
        local, sends, recvs = copies(ins, outs, sems)
        for cp in local:
            cp.wait()
        for cp in recvs:
            cp.wait_recv()
        for cp in sends:
            cp.wait_send()

    return _Comm(sums, [_sds((2, N_CHIPS) + s.shape[1:], s.dtype) for s in sums], {},
                 [pltpu.SemaphoreType.DMA((n,)), pltpu.SemaphoreType.DMA((3 * n,)), pltpu.SemaphoreType.DMA((3 * n,))],
                 start, finish)


def _scatter_d2d(terms):
    n = len(terms)

    def copies(outs, sems):
        send_sem, recv_sem = sems
        x, y, c, _ = _mesh_place()
        sends, recvs = [], []
        for wi in range(n):
            sems_w = dict(send_sem=send_sem.at[wi], recv_sem=recv_sem.at[wi],
                          device_id=(x, y, 1 - c), device_id_type=MESH)
            sends.append(pltpu.make_async_remote_copy(src_ref=outs[wi].at[c], dst_ref=outs[wi].at[c], **sems_w))
            recvs.append(pltpu.make_async_remote_copy(src_ref=outs[wi].at[1 - c], dst_ref=outs[wi].at[1 - c], **sems_w))
        return sends, recvs

    def start(ins, outs, sems):
        for cp in copies(outs, sems)[0]:
            cp.start()

    def finish(ins, outs, sems):
        sends, recvs = copies(outs, sems)
        for cp in recvs:
            cp.wait_recv()
        for cp in sends:
            cp.wait_send()

    return _Comm(terms, [_sds(t.shape, t.dtype) for t in terms], {i: i for i in range(n)},
                 [pltpu.SemaphoreType.DMA((n,)), pltpu.SemaphoreType.DMA((n,))], start, finish)


def _chip_sum(name, grad, got, core):
    _, _, hr, c = grad.shape
    rb = _pick(hr, max(16, (1 << 19) // c), 16)

    def body(core_ref, a_ref, b_ref, o_ref):
        o_ref[...] = (a_ref[...].astype(F32) + b_ref[...].astype(F32)).astype(BF16)

    out_spec = pl.BlockSpec((None, rb, c), lambda t, i, core_ref: (t, i, 0))
    return pl.pallas_call(
        body, name=name,
        grid_spec=pltpu.PrefetchScalarGridSpec(
            num_scalar_prefetch=1, grid=(N_CHIPS, hr // rb),
            in_specs=[pl.BlockSpec((None, None, rb, c), lambda t, i, core_ref: (t, core_ref[0], i, 0)), out_spec],
            out_specs=out_spec),
        out_shape=_sds((N_CHIPS, hr, c), BF16), compiler_params=_params(),
    )(core, grad, got)


def _all_reduce_small(pack):
    r = pack.shape[0]

    def body(p_ref, o_ref, land_ref, send_sem, recv_sem):
        x, y, c, _ = _mesh_place()
        me = 4 * x + 2 * y + c
        flips = [(k >> 2 & 1, k >> 1 & 1, k & 1) for k in range(1, N_DEV)]

        def peer(fx, fy, fc):
            return (1 - x if fx else x, 1 - y if fy else y, 1 - c if fc else c)

        land_ref[me] = p_ref[...]
        sent = []
        for k, flip in enumerate(flips):
            cp = pltpu.make_async_remote_copy(
                src_ref=p_ref, dst_ref=land_ref.at[me], send_sem=send_sem.at[k], recv_sem=recv_sem.at[k],
                device_id=peer(*flip), device_id_type=MESH)
            cp.start()
            sent.append(cp)
        for k, flip in enumerate(flips):
            px, py, pc = peer(*flip)
            slot = land_ref.at[4 * px + 2 * py + pc]
            pltpu.make_async_remote_copy(
                src_ref=slot, dst_ref=slot, send_sem=send_sem.at[k], recv_sem=recv_sem.at[k],
                device_id=(px, py, pc), device_id_type=MESH).wait_recv()
        total = land_ref[0]
        for d in range(1, N_DEV):
            total = total + land_ref[d]
        o_ref[...] = total
        for cp in sent:
            cp.wait_send()

    vmem = pl.BlockSpec(memory_space=pltpu.VMEM)
    return pl.pallas_call(
        body, name="all_reduce_small", in_specs=[vmem], out_specs=vmem, out_shape=_sds((r, 128), F32),
        scratch_shapes=[pltpu.VMEM((N_DEV, r, 128), F32), pltpu.SemaphoreType.DMA((N_DEV - 1,)),
                        pltpu.SemaphoreType.DMA((N_DEV - 1,))],
    )(pack)


PACK_TILE = 8 * 128


def _pack(items):
    rows, i = [], 0
    while i < len(items):
        j = i
        while j < len(items) and items[j].size == items[i].size:
            j += 1
        group = jnp.stack([it.reshape(-1).astype(F32) for it in items[i:j]])
        rows.append(jnp.pad(group, ((0, 0), (0, -group.shape[1] % PACK_TILE))).reshape(-1, 128))
        i = j
    return jnp.concatenate(rows, axis=0)


def _unpack(pack, shapes):
    out, row = [], 0
    for shp in shapes:
        size = int(np.prod(shp))
        nrow = -(-size // PACK_TILE) * (PACK_TILE // 128)
        out.append(pack[row:row + nrow].reshape(-1)[:size].reshape(shp))
        row += nrow
    return out


BIG = ["ffn1_w_gu", "ffn1_w_down", "w_in", "w_gate", "w_proj_a", "w_proj_b", "w_out",
       "ffn2_w_gu", "ffn2_w_down", "w_ple_gate", "w_ple_proj"]
SMALL = ["ffn1_norm", "mix_norm", "ffn2_norm", "ple_norm", "a_q_norm", "a_k_norm", "b_q_norm", "b_k_norm",
         "a_rel_bias", "b_sinks"]
WEIGHTS = ["ffn1_norm", "ffn1_w_gu", "ffn1_w_down", "mix_norm", "w_in", "a_q_norm", "a_k_norm", "a_rel_bias",
           "b_q_norm", "b_k_norm", "b_sinks", "w_gate", "w_proj_a", "w_proj_b", "w_out", "ffn2_norm",
           "ffn2_w_gu", "ffn2_w_down", "ple_norm", "w_ple_gate", "w_ple_proj"]
ATTN_A = dict(prev=A_PREV_CHUNKS * CHUNK, group=1, kw=A_WIDTH, qblk=0, kblk=1, vblk=2)
ATTN_B = dict(prev=B_PREV_CHUNKS * CHUNK, group=N_HEADS // B_KV_HEADS, kw=B_KV_WIDTH, qblk=3,
              kblk=4 * A_WIDTH // B_KV_WIDTH, vblk=4 * A_WIDTH // B_KV_WIDTH + 1)


def _cast_epilogue(accs, extras, outs, ij):
    for acc, out in zip(accs, outs):
        out[...] = acc.astype(out.dtype)


GATHER_FIRST = ["ffn1_w_gu", "ffn1_w_down"]
ROW_SHARDED = ("ffn1_w_down", "ffn2_w_down", "w_out", "w_ple_gate")


def _slotted(name, grad):
    if name == "w_in":
        rows, cols = grad.shape
        grad = jnp.transpose(grad.reshape(rows, N_CHIPS, cols // N_CHIPS), (1, 0, 2))
    elif name in ROW_SHARDED:
        grad = grad.reshape(N_CHIPS, grad.shape[0] // N_CHIPS, grad.shape[1])
    return grad.reshape(N_CHIPS, 2, grad.shape[1] // 2, grad.shape[2])


def _local_step(xt, pt, tgt, n_batch, bufs, small, core):
    t, d = xt.shape
    tm = _pick(t, ROW_TILE, 8)
    tk = _pick(t, ROW_TILE, 8)
    nt = t // tm
    row = pl.BlockSpec((tm, d), lambda i, j, k: (i, 0))
    gs = bufs["w_gate"].shape[2]
    ps = bufs["w_proj_a"].shape[2]
    es = bufs["w_ple_proj"].shape[2]
    pdim = pt.shape[1]
    ncols = N_CHIPS * bufs["w_in"].shape[2]
    tin = ncols // 2
    assert 2 * gs == d and 4 * ps == d and 4 * es == d and tin % 128 == 0

    w = {}
    halves = {n: b.reshape(N_CHIPS, 2, b.shape[1] // 2, b.shape[2]) for n, b in bufs.items()}

    def publish(names, arrays):
        for name, g in zip(names, arrays):
            g = g.reshape(N_CHIPS, 2 * g.shape[2], g.shape[3])
            if name in ROW_SHARDED:
                g = g.reshape(N_CHIPS * g.shape[1], g.shape[2])
            elif name == "w_in":
                g = jnp.transpose(g, (1, 0, 2)).reshape(g.shape[1], N_CHIPS * g.shape[2])
            w[name] = g

    class GatherPipe:
        def __init__(self, names):
            self.names = names
            self.stage = None

        def ici(self, targets=(0, 1, 2)):
            self.stage = _gather_ici(self.bufs(), targets)
            return self.stage

        def d2d(self):
            self.stage = _gather_d2d(self.bufs())
            return self.stage

        def bufs(self):
            return self.stage.results if self.stage is not None else [halves[n] for n in self.names]

        def publish(self):
            publish(self.names, self.stage.results)

    class GradPipe:
        def __init__(self, names):
            self.names = names

        def exchange(self, grads):
            self.grads = [_slotted(n, g) for n, g in zip(self.names, grads)]
            self.x = _exchange_halves(self.grads)
            return self.x

        def scatter(self):
            self.sums = [_chip_sum("chip_sum_" + n, g, got, core)
                         for n, g, got in zip(self.names, self.grads, self.x.results)]
            self.s = _scatter_ici(self.sums)
            return self.s

        def forward(self):
            self.f = _scatter_d2d(self.s.results)
            return self.f

        def terms(self):
            return dict(zip(self.names, self.f.results))

    publish(GATHER_FIRST, _all_gather_weights([halves[n] for n in GATHER_FIRST]))
    g_in, g_proj, g_ple = GatherPipe(["w_in", "w_gate"]), GatherPipe(["w_proj_a", "w_proj_b", "w_out"]), \
        GatherPipe(["w_ple_gate", "w_ple_proj"])
    g_down2, g_up2 = GatherPipe(["ffn2_w_down"]), GatherPipe(["ffn2_w_gu"])
    h1, ffn1_saved = _ffn_fwd("ffn1", xt, small["ffn1_norm"], w["ffn1_w_gu"], w["ffn1_w_down"],
                              {"up": lambda: [g_in.ici()], "down": lambda: [g_in.d2d(), g_proj.ici()]})
    g_in.publish()
    un = _rms_fwd("mix_norm", h1, small["mix_norm"])
    w_in, wgate = w["w_in"], w["w_gate"]
    (qkv,) = _mm(
        "qkv", "nn", (nt, 2, 1),
        [(un, row, w_in, pl.BlockSpec((d, tin), lambda i, j, k: (0, j)))], [],
        [(_sds((t, ncols), BF16), pl.BlockSpec((tm, tin), lambda i, j, k: (i, j)))], (tm, tin), _cast_epilogue,
        j_outer=True, comms=[g_proj.d2d(), g_ple.ici()])
    g_proj.publish()
    wpa, wpb, wout = w["w_proj_a"], w["w_proj_b"], w["w_out"]

    def gate_epilogue(accs, extras, outs, ij):
        outs[0][...] = jax.nn.sigmoid(accs[0]).astype(BF16)

    (gates,) = _mm(
        "gate", "nn", (nt, 4, 1),
        [(un, row, wgate, pl.BlockSpec((None, d, gs), lambda i, j, k: (j, 0, 0)))], [],
        [(_sds((2, t, d), BF16), pl.BlockSpec((None, tm, gs), lambda i, j, k: (j // 2, i, j % 2)))],
        (tm, gs), gate_epilogue, j_outer=True, chunked=True, comms=[g_ple.d2d(), g_down2.ici()])
    g_ple.publish()
    wpg, wpe = w["w_ple_gate"], w["w_ple_proj"]

    bias_a = _pair_bias(_bias_a(small["a_rel_bias"][0]))
    bias_b = _pair_bias(_bias_b())
    sink_a = _pair_rows(jnp.full((N_HEADS, 128), NEG_INF, F32))
    sink_b = _pair_rows(jnp.broadcast_to(small["b_sinks"][0][:, None], (N_HEADS, 128)))
    gqa, gka, gqb, gkb = [jnp.tile(small[k], (1, 2)) for k in ("a_q_norm", "a_k_norm", "b_q_norm", "b_k_norm")]
    ya, lse_a = _attn_fwd("attn_a_fwd", qkv, bias_a, sink_a, gqa, gka, ATTN_A, n_batch,
                          comms=[g_down2.d2d(), g_up2.ici(targets=(0, 1))])
    g_down2.publish()
    yb, lse_b = _attn_fwd("attn_b_fwd", qkv, bias_b, sink_b, gqb, gkb, ATTN_B, n_batch,
                          comms=[g_up2.ici(targets=(2,))])

    def merge_epilogue(accs, extras, outs, ij):
        pa, pb = accs
        outs[0][...] = (extras[0][...].astype(F32) * pa + extras[1][...].astype(F32) * pb).astype(BF16)
        outs[1][...] = pa.astype(BF16)
        outs[2][...] = pb.astype(BF16)

    y_spec = pl.BlockSpec((tm, A_WIDTH), lambda i, j, k: (i, 0))
    proj_spec = pl.BlockSpec((None, A_WIDTH, ps), lambda i, j, k: (j, 0, 0))
    tile_ps = pl.BlockSpec((tm, ps), lambda i, j, k: (i, j))
    merged, pa, pb = _mm(
        "proj_merge", "nn", (nt, 4, 1),
        [(ya, y_spec, wpa, proj_spec), (yb, y_spec, wpb, proj_spec)],
        [(gates, pl.BlockSpec((None, tm, ps), lambda i, j, k: (0, i, j))),
         (gates, pl.BlockSpec((None, tm, ps), lambda i, j, k: (1, i, j)))],
        [(_sds((t, d), BF16), tile_ps)] * 3, (tm, ps), merge_epilogue, comms=[g_up2.d2d()])
    g_up2.publish()

    def residual_epilogue(accs, extras, outs, ij):
        outs[0][...] = extras[0][...] + accs[0]

    (h2,) = _mm(
        "out_proj", "nn", (nt, 1, 1),
        [(merged, row, wout, pl.BlockSpec((d, d), lambda i, j, k: (0, 0)))],
        [(h1, row)], [(_sds((t, d), F32), row)], (tm, d), residual_epilogue)

    h3, ffn2_saved = _ffn_fwd("ffn2", h2, small["ffn2_norm"], w["ffn2_w_gu"], w["ffn2_w_down"], {})
    n3 = _rms_fwd("ple_norm", h3, small["ple_norm"])
    tile_es = pl.BlockSpec((tm, es), lambda i, j, k: (i, j))
    (pe,) = _mm(
        "ple_embed", "nn", (nt, 4, 1),
        [(pt, pl.BlockSpec((tm, pdim), lambda i, j, k: (i, 0)), wpe, pl.BlockSpec((None, pdim, es), lambda i, j, k: (j, 0, 0)))],
        [], [(_sds((t, d), F32), tile_es)], (tm, es), _cast_epilogue)

    th = _pick(d, 512)

    def head_epilogue(accs, extras, outs, ij):
        h3_ref, pe_ref, tgt_ref = extras
        dy_ref, dpe_ref, dz_ref, loss_ref = outs
        pg = jax.nn.sigmoid(accs[0])
        pev = pe_ref[...]
        diff = h3_ref[...] + pg * pev - tgt_ref[...]
        dy = diff * (1.0 / d)
        dy_ref[...] = dy
        dpe_ref[...] = (dy * pg).astype(BF16)
        dz_ref[...] = (dy * pev * pg * (1.0 - pg)).astype(BF16)
        _accumulate(loss_ref, jnp.full(loss_ref.shape, jnp.sum(diff * diff), F32), (ij[0] == 0) & (ij[1] == 0))

    tile_h = pl.BlockSpec((tm, th), lambda i, j, k: (i, j))
    dy, dpe, dz, loss_acc = _mm(
        "ple_gate_loss", "nn", (nt, d // th, 1),
        [(n3, row, wpg, pl.BlockSpec((d, th), lambda i, j, k: (0, j)))],
        [(h3, tile_h), (pe, tile_h), (tgt, tile_h)],
        [(_sds((t, d), F32), tile_h), (_sds((t, d), BF16), tile_h), (_sds((t, d), BF16), tile_h),
         (_sds((8, 128), F32), pl.BlockSpec((8, 128), lambda i, j, k: (0, 0)))],
        (tm, th), head_epilogue, j_outer=True, chunked=True)
    loss = 0.5 * loss_acc[0, 0] / d

    nk = t // tk
    (dwpe,) = _mm(
        "d_w_ple_proj", "tn", (1, 4, nk),
        [(pt, pl.BlockSpec((tk, pdim), lambda i, j, k: (k, 0)), dpe, pl.BlockSpec((tk, es), lambda i, j, k: (k, j)))],
        [], [(_sds((4, pdim, es), BF16), pl.BlockSpec((None, pdim, es), lambda i, j, k: (j, 0, 0)))],
        (pdim, es), _cast_epilogue)

    def dense_grad(name, a, dyb, comms=()):
        (res,) = _mm(
            name, "tn", (1, d // th, nk),
            [(a, pl.BlockSpec((tk, d), lambda i, j, k: (k, 0)), dyb, pl.BlockSpec((tk, th), lambda i, j, k: (k, j)))],
            [], [(_sds((d, d), BF16), pl.BlockSpec((d, th), lambda i, j, k: (0, j)))], (d, th), _cast_epilogue,
            comms=comms)
        return res

    dwpg = dense_grad("d_w_ple_gate", n3, dz)
    tmn = _pick(t, ROW_TILE, 8)
    extras, outs = _rms_bwd_io(h3, small["ple_norm"], dy, tmn)
    dh3, dh3_b, d_ple_norm = _mm(
        "d_ple_norm", "nt", (t // tmn, 1, 1),
        [(dz, pl.BlockSpec((tmn, d), lambda i, j, k: (i, 0)), wpg, pl.BlockSpec((d, d), lambda i, j, k: (0, 0)))],
        extras, outs, (tmn, d), _rms_bwd_epilogue)

    up2, down2, ple = GradPipe(["ffn2_w_gu"]), GradPipe(["ffn2_w_down"]), GradPipe(["w_ple_gate", "w_ple_proj"])
    proj = GradPipe(["w_proj_a", "w_proj_b", "w_out"])
    dh2, dh2_b, d_ffn2_norm, dwgu2, dwd2 = _ffn_bwd(
        "ffn2", dh3, dh3_b, h2, small["ffn2_norm"], w["ffn2_w_gu"], w["ffn2_w_down"], ffn2_saved,
        {"dnorm": lambda dwgu, dwd: [up2.exchange([dwgu]), down2.exchange([dwd]), ple.exchange([dwpg, dwpe])]})

    def dmerge_epilogue(accs, extras, outs, ij):
        dmo = accs[0]
        g_ref, pa_ref, pb_ref = extras
        dg_ref, dpa_ref, dpb_ref = outs
        ga = g_ref[0].astype(F32)
        gb = g_ref[1].astype(F32)
        dg_ref[0] = (dmo * pa_ref[...].astype(F32) * ga * (1.0 - ga)).astype(BF16)
        dg_ref[1] = (dmo * pb_ref[...].astype(F32) * gb * (1.0 - gb)).astype(BF16)
        dpa_ref[...] = (dmo * ga).astype(BF16)
        dpb_ref[...] = (dmo * gb).astype(BF16)

    g_spec = pl.BlockSpec((2, tm, th), lambda i, j, k: (0, i, j))
    dgates, dpa, dpb = _mm(
        "d_merge", "nt", (nt, d // th, 1),
        [(dh2_b, row, wout, pl.BlockSpec((th, d), lambda i, j, k: (j, 0)))],
        [(gates, g_spec), (pa, tile_h), (pb, tile_h)],
        [(_sds((2, t, d), BF16), g_spec), (_sds((t, d), BF16), tile_h), (_sds((t, d), BF16), tile_h)],
        (tm, th), dmerge_epilogue, j_outer=True, chunked=True, comms=[down2.scatter()])
    dwout = dense_grad("d_w_out", merged, dh2_b, comms=[down2.forward(), ple.scatter()])

    yk_spec = pl.BlockSpec((tk, A_WIDTH), lambda i, j, k: (k, 0))
    dk_spec = pl.BlockSpec((tk, ps), lambda i, j, k: (k, j))
    dproj = (_sds((4, A_WIDTH, ps), BF16), proj_spec)
    dwpa, dwpb = _mm(
        "d_w_proj", "tn", (1, 4, nk),
        [(ya, yk_spec, dpa, dk_spec), (yb, yk_spec, dpb, dk_spec)], [], [dproj, dproj], (A_WIDTH, ps), _cast_epilogue,
        comms=[ple.forward()])
    dproj_a = pl.BlockSpec((tm, ps), lambda i, j, k: (i, k))
    wproj_k = pl.BlockSpec((None, A_WIDTH, ps), lambda i, j, k: (k, 0, 0))
    dya, dyb = _mm(
        "d_attn_out", "nt", (nt, 1, 4),
        [(dpa, dproj_a, wpa, wproj_k), (dpb, dproj_a, wpb, wproj_k)], [],
        [(_sds((t, A_WIDTH), BF16), y_spec)] * 2, (tm, A_WIDTH), _cast_epilogue,
        comms=[proj.exchange([dwpa, dwpb, dwout])])

    dqa, dka, dva, dbias_a, _, dgqa, dgka = _attn_bwd(
        "attn_a_bwd", qkv, bias_a, sink_a, gqa, gka, ya, dya, lse_a, ATTN_A, n_batch, True,
        comms=[up2.scatter(), proj.scatter()])
    dqb, dkb, dvb, _, dsink_b, dgqb, dgkb = _attn_bwd(
        "attn_b_bwd", qkv, bias_b, sink_b, gqb, gkb, yb, dyb, lse_b, ATTN_B, n_batch, False,
        comms=[up2.forward(), proj.forward()])
    dqkv = jnp.concatenate([dqa, dka, dva, dqb, dkb, dvb], axis=1)

    (dwgate,) = _mm(
        "d_w_gate", "tn", (1, 4, nk),
        [(un, pl.BlockSpec((tk, d), lambda i, j, k: (k, 0)),
          dgates, pl.BlockSpec((None, tk, gs), lambda i, j, k: (j // 2, k, j % 2)))],
        [], [(_sds((4, d, gs), BF16), pl.BlockSpec((None, d, gs), lambda i, j, k: (j, 0, 0)))], (d, gs), _cast_epilogue)
    (dwin,) = _mm(
        "d_w_in", "tn", (1, 2, nk),
        [(un, pl.BlockSpec((tk, d), lambda i, j, k: (k, 0)), dqkv, pl.BlockSpec((tk, tin), lambda i, j, k: (k, j)))],
        [], [(_sds((d, ncols), BF16), pl.BlockSpec((d, tin), lambda i, j, k: (0, j)))], (d, tin), _cast_epilogue)

    mixer = GradPipe(["w_in", "w_gate"])
    extras, outs = _rms_bwd_io(h1, small["mix_norm"], dh2, tmn)
    dh1, dh1_b, d_mix_norm = _mm(
        "d_mix_norm", "nt", (t // tmn, 1, 6),
        [(dgates, pl.BlockSpec((None, tmn, gs), lambda i, j, k: (jnp.minimum(k, 3) // 2, i, jnp.minimum(k, 3) % 2)),
          wgate, pl.BlockSpec((None, d, gs), lambda i, j, k: (jnp.minimum(k, 3), 0, 0))),
         (dqkv, pl.BlockSpec((tmn, tin), lambda i, j, k: (i, jnp.maximum(k - 4, 0))),
          w_in, pl.BlockSpec((d, tin), lambda i, j, k: (0, jnp.maximum(k - 4, 0))))],
        extras, outs, (tmn, d), _rms_bwd_epilogue, steps=[4, 2],
        comms=[mixer.exchange([dwin, dwgate])])

    up1 = GradPipe(["ffn1_w_gu"])
    down1 = GradPipe(["ffn1_w_down"])
    dx, _, d_ffn1_norm, _, _ = _ffn_bwd(
        "ffn1", dh1, dh1_b, xt, small["ffn1_norm"], w["ffn1_w_gu"], w["ffn1_w_down"], ffn1_saved,
        {"dwgu": lambda: [mixer.scatter()],
         "dwd": lambda dwgu: [mixer.forward(), up1.exchange([dwgu])],
         "dnorm": lambda dwgu, dwd: [up1.scatter(), down1.exchange([dwd])]})
    _run_comms("grad_tail_scatter", [up1.forward(), down1.scatter()])
    _run_comms("grad_tail_forward", [down1.forward()])
    terms = {}
    for pipe in (up2, down2, ple, proj, mixer, up1, down1):
        terms.update(pipe.terms())

    def fold(v):
        return v[0, :HEAD_DIM] + v[0, HEAD_DIM:]

    small_grads = {"ffn1_norm": d_ffn1_norm, "mix_norm": d_mix_norm, "ffn2_norm": d_ffn2_norm,
                   "ple_norm": d_ple_norm, "a_q_norm": fold(dgqa), "a_k_norm": fold(dgka),
                   "b_q_norm": fold(dgqb), "b_k_norm": fold(dgkb), "a_rel_bias": _rel_bias_grad(_unpair_bias(dbias_a)),
                   "b_sinks": jnp.sum(dsink_b, axis=1)}
    return loss, dx, terms, small_grads


def kernel(x, p, ffn1_norm, ffn1_w_gu, ffn1_w_down, mix_norm, w_in, a_q_norm, a_k_norm, a_rel_bias, b_q_norm, b_k_norm, b_sinks, w_gate, w_proj_a, w_proj_b, w_out, ffn2_norm, ffn2_w_gu, ffn2_w_down, ple_norm, w_ple_gate, w_ple_proj, loss_target, m_ffn1_norm, m_ffn1_w_gu, m_ffn1_w_down, m_mix_norm, m_w_in, m_a_q_norm, m_a_k_norm, m_a_rel_bias, m_b_q_norm, m_b_k_norm, m_b_sinks, m_w_gate, m_w_proj_a, m_w_proj_b, m_w_out, m_ffn2_norm, m_ffn2_w_gu, m_ffn2_w_down, m_ple_norm, m_w_ple_gate, m_w_ple_proj, v_ffn1_norm, v_ffn1_w_gu, v_ffn1_w_down, v_mix_norm, v_w_in, v_a_q_norm, v_a_k_norm, v_a_rel_bias, v_b_q_norm, v_b_k_norm, v_b_sinks, v_w_gate, v_w_proj_a, v_w_proj_b, v_w_out, v_ffn2_norm, v_ffn2_w_gu, v_ffn2_w_down, v_ple_norm, v_w_ple_gate, v_w_ple_proj):
    given = dict(locals())
    n_batch, s, d = x.shape
    t = n_batch * s
    xt = x.reshape(t, d)
    pt = p.reshape(t, p.shape[-1])
    tgt = loss_target.reshape(t, d)

    chip = (2 * lax.axis_index("x") + lax.axis_index("y")).astype(jnp.int32).reshape(1)
    bufs = {name: _cast_into_slot("cast_" + name, given[name][0], chip) for name in BIG}
    small = {name: given[name] for name in SMALL}
    core = lax.axis_index("c").astype(jnp.int32).reshape(1)
    loss, dx, terms, small_grads = _local_step(xt, pt, tgt, n_batch, bufs, small, core)

    grads, deltas, new_m, new_v = {}, {}, {}, {}
    for name in BIG:
        gw, dl, nm, nv = _adamw_terms("adamw_" + name, terms[name], given[name][0], given["m_" + name][0],
                                      given["v_" + name][0])
        grads[name], deltas[name], new_m[name], new_v[name] = gw[None], dl[None], nm[None], nv[None]

    small_shapes = [given[name].shape for name in SMALL] + [()]
    g_pack = _all_reduce_small(_pack([small_grads[name] for name in SMALL] + [loss]))
    zero = jnp.zeros((), F32)
    w_pack = _pack([given[name] for name in SMALL] + [zero])
    m_pack = _pack([given["m_" + name] for name in SMALL] + [zero])
    v_pack = _pack([given["v_" + name] for name in SMALL] + [zero])
    d_pack, nm_pack, nv_pack = _ew("adamw_small", lambda wv, gv, mv, vv: _adamw_math(wv, gv, mv, vv),
                                   [w_pack, g_pack, m_pack, v_pack], [F32] * 3)
    g_small = _unpack(g_pack, small_shapes)
    loss_total = g_small[-1]
    for name, gv, dv, mv, vv in zip(SMALL, g_small, _unpack(d_pack, small_shapes), _unpack(nm_pack, small_shapes),
                                    _unpack(nv_pack, small_shapes)):
        grads[name], deltas[name], new_m[name], new_v[name] = gv, dv, mv, vv

    return (loss_total, dx.reshape(x.shape), *[grads[n] for n in WEIGHTS], *[deltas[n] for n in WEIGHTS],
            *[new_m[n] for n in WEIGHTS], *[new_v[n] for n in WEIGHTS])
```

```python
import functools

import numpy as np
import jax
import jax.numpy as jnp
from jax import lax
from jax.experimental import pallas as pl
from jax.experimental.pallas import tpu as pltpu

F32 = jnp.float32
BF16 = jnp.bfloat16

CHUNK = 64
HEAD_DIM = 64
A_PREV_CHUNKS = 8
A_MAX_REL = 128
N_HEADS = 8
B_KV_HEADS = 2
B_PREV_CHUNKS = 2
A_WIDTH = N_HEADS * HEAD_DIM
B_KV_WIDTH = B_KV_HEADS * HEAD_DIM
EPS = 1e-6
NEG_INF = -1e30
ATTN_SCALE = HEAD_DIM ** -0.5
Q_BLOCK = 128
PAIR = 2 * HEAD_DIM

ADAM_LR = 0.001
ADAM_B1 = 0.9
ADAM_B2 = 0.999
ADAM_EPS = 1e-08
ADAM_WD = 0.01
ADAM_STEP = 10

N_CHIPS = 4
N_DEV = 8
VMEM_LIMIT_V7X = 56 * 1024 * 1024
ROW_TILE = 1024
MESH = pl.DeviceIdType.MESH
ANY = pl.BlockSpec(memory_space=pl.ANY)

_DN = {
    "nn": (((1,), (0,)), ((), ())),
    "nt": (((1,), (1,)), ((), ())),
    "tn": (((0,), (0,)), ((), ())),
}


def _pick(n, target, mult=128):
    best = None
    for d in range(mult, min(n, target) + 1, mult):
        if n % d == 0:
            best = d
    return n if best is None else best


def _dot(a, b, mode):
    return lax.dot_general(a.astype(BF16), b.astype(BF16), _DN[mode], preferred_element_type=F32)


def _params():
    return pltpu.CompilerParams(vmem_limit_bytes=VMEM_LIMIT_V7X)


class _Comm:
    def __init__(self, ins, outs, aliases, sems, start, finish):
        self.ins, self.outs, self.aliases, self.sems = list(ins), list(outs), dict(aliases), list(sems)
        self.start, self.finish = start, finish
        self.results = None


class _CommPlumbing:
    def __init__(self, comms, n_in, n_out, n_scratch):
        self.comms = list(comms)
        self.n_in, self.n_out, self.n_scratch = n_in, n_out, n_scratch
        self.args = [a for cm in self.comms for a in cm.ins]
        self.out_shape = [o for cm in self.comms for o in cm.outs]
        self.scratch = [s for cm in self.comms for s in cm.sems]
        self.aliases = {}
        i0, o0 = n_in, n_out
        for cm in self.comms:
            for a, b in cm.aliases.items():
                self.aliases[i0 + a] = o0 + b
            i0 += len(cm.ins)
            o0 += len(cm.outs)

    def _parts(self, in_refs, out_refs, scratch_refs):
        parts = []
        i0, o0, s0 = self.n_in, self.n_out, self.n_scratch
        for cm in self.comms:
            parts.append((in_refs[i0:i0 + len(cm.ins)], out_refs[o0:o0 + len(cm.outs)],
                          scratch_refs[s0:s0 + len(cm.sems)]))
            i0 += len(cm.ins)
            o0 += len(cm.outs)
            s0 += len(cm.sems)
        return parts

    def start_at(self, in_refs, out_refs, scratch_refs, first):
        if self.comms:
            parts = self._parts(in_refs, out_refs, scratch_refs)

            @pl.when(first)
            def _():
                for cm, part in zip(self.comms, parts):
                    cm.start(*part)

    def finish_at(self, in_refs, out_refs, scratch_refs, last):
        if self.comms:
            parts = self._parts(in_refs, out_refs, scratch_refs)

            @pl.when(last)
            def _():
                for cm, part in zip(self.comms, parts):
                    cm.finish(*part)

    def deliver(self, results):
        o0 = self.n_out
        for cm in self.comms:
            cm.results = list(results[o0:o0 + len(cm.outs)])
            o0 += len(cm.outs)
        return list(results[:self.n_out])


def _swap_ij(spec):
    index_map = spec.index_map
    return pl.BlockSpec(spec.block_shape, lambda j, i, k: index_map(i, j, k))


MXU_COLUMNS_V7X = 256


def _mm(name, mode, grid, pairs, extras, outs, acc_shape, epilogue, steps=None, comms=(), j_outer=False,
        chunked=False):
    ni, nj, nk = grid
    n_in = 2 * len(pairs) + len(extras)
    n_out = len(outs)
    tn = acc_shape[1]
    col_chunks = None
    if chunked:
        assert nk == 1 and steps is None and mode in ("nn", "nt")
        col_chunks = [(c0, min(MXU_COLUMNS_V7X, tn - c0)) for c0 in range(0, tn, MXU_COLUMNS_V7X)]
    n_acc = 0 if chunked else (len(pairs) if steps is None else 1)
    plumb = _CommPlumbing(comms, n_in, n_out, n_acc)
    n_all_in = n_in + len(plumb.args)
    n_all_out = n_out + len(plumb.out_shape)
    if j_outer:
        grid = (nj, ni, nk)
        pairs = [(a, _swap_ij(a_spec), b, _swap_ij(b_spec)) for a, a_spec, b, b_spec in pairs]
        extras = [(e, _swap_ij(e_spec)) for e, e_spec in extras]
        outs = [(o, _swap_ij(o_spec)) for o, o_spec in outs]

    def body(*refs):
        in_refs = refs[:n_all_in]
        out_refs = refs[n_all_in:n_all_in + n_all_out]
        scratch = refs[n_all_in + n_all_out:]
        accs = scratch[:n_acc]
        i = pl.program_id(1 if j_outer else 0)
        j = pl.program_id(0 if j_outer else 1)
        k = pl.program_id(2)
        plumb.start_at(in_refs, out_refs, scratch, (i == 0) & (j == 0) & (k == 0))

        def contrib(p, acc):
            acc[...] += _dot(in_refs[2 * p][...], in_refs[2 * p + 1][...], mode)

        if col_chunks:
            def cols(ref, c0, cs):
                if ref.shape[-1] != tn:
                    return ref
                return ref.at[(slice(None),) * (len(ref.shape) - 1) + (pl.ds(c0, cs),)]

            lhs = [in_refs[2 * p][...] for p in range(len(pairs))]
            for ci, (c0, cs) in enumerate(col_chunks):
                vals = []
                for p in range(len(pairs)):
                    b_ref = in_refs[2 * p + 1]
                    rhs = b_ref[:, c0:c0 + cs] if mode == "nn" else b_ref[c0:c0 + cs, :]
                    vals.append(_dot(lhs[p], rhs, mode))
                epilogue(vals, [cols(r, c0, cs) for r in in_refs[2 * len(pairs):n_in]],
                         [cols(r, c0, cs) for r in out_refs[:n_out]], (i, j * len(col_chunks) + ci))
        else:
            @pl.when(k == 0)
            def _():
                for acc in accs:
                    acc[...] = jnp.zeros(acc.shape, F32)

            if steps is None:
                for p in range(len(pairs)):
                    contrib(p, accs[p])
            else:
                lo = 0
                for p, n in enumerate(steps):
                    pl.when((k >= lo) & (k < lo + n))(functools.partial(contrib, p, accs[0]))
                    lo += n

            @pl.when(k == nk - 1)
            def _():
                epilogue([acc[...] for acc in accs], in_refs[2 * len(pairs):n_in], out_refs[:n_out], (i, j))

        plumb.finish_at(in_refs, out_refs, scratch, (i == ni - 1) & (j == nj - 1) & (k == nk - 1))

    args, in_specs = [], []
    for a, a_spec, b, b_spec in pairs:
        args += [a, b]
        in_specs += [a_spec, b_spec]
    for e, e_spec in extras:
        args.append(e)
        in_specs.append(e_spec)
    res = pl.pallas_call(
        body,
        name=name,
        grid=grid,
        in_specs=in_specs + [ANY] * len(plumb.args),
        out_specs=[s for _, s in outs] + [ANY] * len(plumb.out_shape),
        out_shape=[o for o, _ in outs] + plumb.out_shape,
        scratch_shapes=[pltpu.VMEM(acc_shape, F32) for _ in range(n_acc)] + plumb.scratch,
        input_output_aliases=plumb.aliases,
        compiler_params=_params(),
    )(*args, *plumb.args)
    return plumb.deliver(res)


def _sds(shape, dtype):
    return jax.ShapeDtypeStruct(shape, dtype)


def _accumulate(ref, value, first):
    @pl.when(first)
    def _():
        ref[...] = value

    @pl.when(jnp.logical_not(first))
    def _():
        ref[...] += value


def _rms_fwd(name, x, gain, comms=()):
    t, d = x.shape
    tm = _pick(t, ROW_TILE, 8)
    steps = t // tm
    plumb = _CommPlumbing(comms, 2, 1, 0)
    n_all_in = 2 + len(plumb.args)
    n_all_out = 1 + len(plumb.out_shape)

    def body(*refs):
        x_ref, g_ref = refs[:2]
        y_ref = refs[n_all_in]
        comm_refs = (refs[:n_all_in], refs[n_all_in:n_all_in + n_all_out], refs[n_all_in + n_all_out:])
        i = pl.program_id(0)
        plumb.start_at(*comm_refs, i == 0)
        xv = x_ref[...]
        rstd = lax.rsqrt(jnp.mean(xv * xv, axis=-1, keepdims=True) + EPS)
        y_ref[...] = (xv * rstd * g_ref[...]).astype(BF16)
        plumb.finish_at(*comm_refs, i == steps - 1)

    res = pl.pallas_call(
        body, name=name, grid=(steps,),
        in_specs=[pl.BlockSpec((tm, d), lambda i: (i, 0)), pl.BlockSpec((1, d), lambda i: (0, 0))]
        + [ANY] * len(plumb.args),
        out_specs=[pl.BlockSpec((tm, d), lambda i: (i, 0))] + [ANY] * len(plumb.out_shape),
        out_shape=[_sds((t, d), BF16)] + plumb.out_shape,
        scratch_shapes=plumb.scratch,
        input_output_aliases=plumb.aliases,
        compiler_params=_params(),
    )(x, gain, *plumb.args)
    return plumb.deliver(res)[0]


def _rms_bwd_epilogue(accs, extras, outs, ij):
    x_ref, g_ref, r_ref = extras
    dh_ref, dhb_ref, dg_ref = outs
    dn = accs[0]
    xv = x_ref[...]
    rstd = lax.rsqrt(jnp.mean(xv * xv, axis=-1, keepdims=True) + EPS)
    xhat = xv * rstd
    gd = dn * g_ref[...]
    dx = rstd * (gd - xhat * jnp.mean(gd * xhat, axis=-1, keepdims=True))
    dh = r_ref[...] + dx
    dh_ref[...] = dh
    dhb_ref[...] = dh.astype(BF16)
    _accumulate(dg_ref, jnp.sum(dn * xhat, axis=0, keepdims=True), ij[0] == 0)


def _rms_bwd_io(x, gain, dres, tm):
    t, d = x.shape
    row = pl.BlockSpec((tm, d), lambda i, j, k: (i, 0))
    extras = [(x, row), (gain, pl.BlockSpec((1, d), lambda i, j, k: (0, 0))), (dres, row)]
    outs = [(_sds((t, d), F32), row), (_sds((t, d), BF16), row),
            (_sds((1, d), F32), pl.BlockSpec((1, d), lambda i, j, k: (0, 0)))]
    return extras, outs


def _ffn_fwd(tag, h, gain, wgu, wd, hooks):
    t, d = h.shape
    fs = wgu.shape[2]
    f = 2 * fs
    tm = _pick(t, ROW_TILE, 8)
    n = _rms_fwd(tag + "_norm", h, gain)

    def up_epilogue(accs, extras, outs, ij):
        g, u = accs
        gu_ref, a_ref = outs
        gu_ref[0] = g.astype(BF16)
        gu_ref[1] = u.astype(BF16)
        a_ref[...] = (g * jax.nn.sigmoid(g) * u).astype(BF16)

    a_spec = pl.BlockSpec((tm, d), lambda i, j, k: (i, 0))
    gu, a = _mm(
        tag + "_up", "nn", (t // tm, 2, 1),
        [(n, a_spec, wgu, pl.BlockSpec((None, d, fs), lambda i, j, k: (j, 0, 0))),
         (n, a_spec, wgu, pl.BlockSpec((None, d, fs), lambda i, j, k: (j + 2, 0, 0)))],
        [],
        [(_sds((2, t, f), BF16), pl.BlockSpec((2, tm, fs), lambda i, j, k: (0, i, j))),
         (_sds((t, f), BF16), pl.BlockSpec((tm, fs), lambda i, j, k: (i, j)))],
        (tm, fs), up_epilogue, comms=hooks.get("up", lambda: ())(), j_outer=True, chunked=True)

    def down_epilogue(accs, extras, outs, ij):
        outs[0][...] = extras[0][...] + 0.5 * accs[0]


    row = pl.BlockSpec((tm, d), lambda i, j, k: (i, 0))
    (h_new,) = _mm(
        tag + "_down", "nn", (t // tm, 1, 1),
        [(a, pl.BlockSpec((tm, f), lambda i, j, k: (i, 0)), wd, pl.BlockSpec((f, d), lambda i, j, k: (0, 0)))],
        [(h, row)], [(_sds((t, d), F32), row)], (tm, d), down_epilogue, comms=hooks.get("down", lambda: ())())
    return h_new, (n, gu, a)


def _ffn_bwd(tag, dh, dh_b, h, gain, wgu, wd, saved, hooks):
    n, gu, a = saved
    t, d = h.shape
    fs = wgu.shape[2]
    f = 2 * fs
    tm = _pick(t, ROW_TILE, 8)
    tk = _pick(t, ROW_TILE, 8)

    def dact_epilogue(accs, extras, outs, ij):
        da = 0.5 * accs[0]
        g = extras[0][0].astype(F32)
        u = extras[0][1].astype(F32)
        sg = jax.nn.sigmoid(g)
        outs[0][0] = (da * u * sg * (1.0 + g * (1.0 - sg))).astype(BF16)
        outs[0][1] = (da * g * sg).astype(BF16)

    gu_spec = pl.BlockSpec((2, tm, fs), lambda i, j, k: (0, i, j))
    (dgu,) = _mm(
        tag + "_dact", "nt", (t // tm, 2, 1),
        [(dh_b, pl.BlockSpec((tm, d), lambda i, j, k: (i, 0)), wd, pl.BlockSpec((fs, d), lambda i, j, k: (j, 0)))],
        [(gu, gu_spec)], [(_sds((2, t, f), BF16), gu_spec)], (tm, fs), dact_epilogue, j_outer=True, chunked=True,
        comms=hooks.get("dact", lambda: ())())

    def cast_epilogue(accs, extras, outs, ij):
        outs[0][...] = accs[0].astype(BF16)

    (dwgu,) = _mm(
        tag + "_dwgu", "tn", (1, 4, t // tk),
        [(n, pl.BlockSpec((tk, d), lambda i, j, k: (k, 0)),
          dgu, pl.BlockSpec((None, tk, fs), lambda i, j, k: (j // 2, k, j % 2)))],
        [], [(_sds((4, d, fs), BF16), pl.BlockSpec((None, d, fs), lambda i, j, k: (j, 0, 0)))], (d, fs), cast_epilogue,
        comms=hooks.get("dwgu", lambda: ())())

    def half_epilogue(accs, extras, outs, ij):
        outs[0][...] = (0.5 * accs[0]).astype(BF16)

    (dwd,) = _mm(
        tag + "_dwd", "tn", (2, 1, t // tk),
        [(a, pl.BlockSpec((tk, fs), lambda i, j, k: (k, i)), dh_b, pl.BlockSpec((tk, d), lambda i, j, k: (k, 0)))],
        [], [(_sds((f, d), BF16), pl.BlockSpec((fs, d), lambda i, j, k: (i, 0)))], (fs, d), half_epilogue,
        comms=hooks.get("dwd", lambda g: ())(dwgu))

    tmn = _pick(t, ROW_TILE, 8)
    extras, outs = _rms_bwd_io(h, gain, dh, tmn)
    dh_in, dh_in_b, dgain = _mm(
        tag + "_dnorm", "nt", (t // tmn, 1, 4),
        [(dgu, pl.BlockSpec((None, tmn, fs), lambda i, j, k: (k // 2, i, k % 2)),
          wgu, pl.BlockSpec((None, d, fs), lambda i, j, k: (k, 0, 0)))],
        extras, outs, (tmn, d), _rms_bwd_epilogue, comms=hooks.get("dnorm", lambda g, w: ())(dwgu, dwd))
    return dh_in, dh_in_b, dgain, dwgu, dwd


def _lane_lo(shape):
    return lax.broadcasted_iota(jnp.int32, shape, 1) < HEAD_DIM


def _pair_norm(xv, gain):
    lo = _lane_lo(xv.shape)
    x2 = xv * xv
    ms_lo = jnp.sum(jnp.where(lo, x2, 0.0), axis=-1, keepdims=True) * (1.0 / HEAD_DIM)
    ms_hi = jnp.sum(jnp.where(lo, 0.0, x2), axis=-1, keepdims=True) * (1.0 / HEAD_DIM)
    rstd = jnp.where(lo, lax.rsqrt(ms_lo + EPS), lax.rsqrt(ms_hi + EPS))
    xhat = xv * rstd
    return xhat * gain, xhat, rstd


def _pair_norm_bwd(dn, xhat, rstd, gain):
    lo = _lane_lo(dn.shape)
    gd = dn * gain
    t = gd * xhat
    m_lo = jnp.sum(jnp.where(lo, t, 0.0), axis=-1, keepdims=True) * (1.0 / HEAD_DIM)
    m_hi = jnp.sum(jnp.where(lo, 0.0, t), axis=-1, keepdims=True) * (1.0 / HEAD_DIM)
    dx = rstd * (gd - xhat * jnp.where(lo, m_lo, m_hi))
    return dx, jnp.sum(dn * xhat, axis=0, keepdims=True)


def _half(xv, hi):
    lo = _lane_lo(xv.shape)
    return jnp.where(lo, 0, xv) if hi else jnp.where(lo, xv, 0)


def _attn_window(i, prev):
    q0 = i * Q_BLOCK
    start = jnp.maximum(q0 - prev, 0)
    off = start - (q0 - prev)
    return pl.multiple_of(start, Q_BLOCK), pl.multiple_of(off, Q_BLOCK)


def _attn_specs(cfg, s, nq):
    kw = cfg["kw"]
    q_spec = pl.BlockSpec((Q_BLOCK, A_WIDTH), lambda b, i: (b * nq + i, cfg["qblk"]))
    k_spec = pl.BlockSpec((s, kw), lambda b, i: (b, cfg["kblk"]))
    v_spec = pl.BlockSpec((s, kw), lambda b, i: (b, cfg["vblk"]))
    return q_spec, k_spec, v_spec


def _const_spec(shape):
    return pl.BlockSpec(shape, lambda b, i: (0,) * len(shape))


KEY_CHUNK = 128


def _pair_bias(bias_t):
    wext = bias_t.shape[1]
    return jnp.transpose(bias_t.reshape(N_HEADS // 2, 2, wext, Q_BLOCK), (0, 2, 1, 3)).reshape(
        N_HEADS // 2, wext, 2 * Q_BLOCK)


def _unpair_bias(db2):
    wext = db2.shape[1]
    return jnp.transpose(db2.reshape(N_HEADS // 2, wext, 2, Q_BLOCK), (0, 2, 1, 3)).reshape(N_HEADS, wext, Q_BLOCK)


def _pair_rows(rows):
    two = rows.reshape(N_HEADS // 2, 2 * rows.shape[1])
    return jnp.broadcast_to(two[:, None, :], (N_HEADS // 2, 8, two.shape[1]))


def _sub_lo(shape):
    return lax.broadcasted_iota(jnp.int32, shape, 0) < HEAD_DIM


def _by_half(lo_row, hi_row, rows):
    return jnp.where(_sub_lo((rows, lo_row.shape[1])), lo_row, hi_row)


def _stack_pair(xn, jq, group):
    parts = []
    for hq in range(2):
        hk = ((2 * jq + hq) // group) % 2
        xm = _half(xn, hq)
        if hq != hk:
            xm = pltpu.roll(xm, HEAD_DIM, 1)
        parts.append(xm)
    return jnp.concatenate(parts, axis=0).astype(BF16)


def _place_transposed(blk, dst_ref, c, heads, group):
    bt = blk.T
    lo = _sub_lo(bt.shape)
    for h in heads:
        src_hi = ((h // group) % 2) == 1
        part = jnp.where(lo, 0.0, bt) if src_hi else jnp.where(lo, bt, 0.0)
        if src_hi != (h % 2 == 1):
            part = pltpu.roll(part, HEAD_DIM, 0)
        dst_ref[h, c] = part.astype(BF16)


def _attn_fwd(name, qkv, bias2, sink2, gq, gk, cfg, n_batch, comms=()):
    t = qkv.shape[0]
    s = t // n_batch
    nq = s // Q_BLOCK
    nkc = s // KEY_CHUNK
    prev, group, kw = cfg["prev"], cfg["group"], cfg["kw"]
    w = prev + Q_BLOCK
    n_chunks = w // KEY_CHUNK
    wext = bias2.shape[1]
    plumb = _CommPlumbing(comms, 7, 2, 4)
    n_all_in = 7 + len(plumb.args)
    n_all_out = 2 + len(plumb.out_shape)

    def body(*refs):
        q_ref, k_ref, v_ref, bias_ref, sink_ref, gq_ref, gk_ref = refs[:7]
        y_ref, lse_ref = refs[n_all_in:n_all_in + 2]
        kn_ref, vt_ref, s_ref, pst_ref = refs[n_all_in + n_all_out:n_all_in + n_all_out + 4]
        i = pl.program_id(1)
        comm_refs = (refs[:n_all_in], refs[n_all_in:n_all_in + n_all_out], refs[n_all_in + n_all_out:])
        plumb.start_at(*comm_refs, (pl.program_id(0) == 0) & (i == 0))

        @pl.when(i == 0)
        def _():
            for jk in range(kw // PAIR):
                cols = pl.ds(jk * PAIR, PAIR)
                heads = [h for h in range(N_HEADS) if (h // group) // 2 == jk]
                kn, _, _ = _pair_norm(k_ref[:, cols].astype(F32), gk_ref[...])
                kn_ref[:, cols] = kn.astype(BF16)
                for c in range(nkc):
                    _place_transposed(v_ref[pl.ds(c * KEY_CHUNK, KEY_CHUNK), cols].astype(F32), vt_ref, c, heads, group)

        start, off = _attn_window(i, prev)
        c0 = start // KEY_CHUNK
        sub8 = lax.broadcasted_iota(jnp.int32, (N_HEADS, Q_BLOCK), 0)
        lse = jnp.zeros((N_HEADS, Q_BLOCK), F32)
        for jq in range(N_HEADS // 2):
            kcols = pl.ds((((2 * jq) // group) // 2) * PAIR, PAIR)
            qn, _, _ = _pair_norm(q_ref[:, pl.ds(jq * PAIR, PAIR)].astype(F32), gq_ref[...])
            qs = _stack_pair(qn * ATTN_SCALE, jq, group)
            s_ref[...] = _dot(kn_ref[pl.ds(start, w), kcols], qs, "nt")
            m = sink_ref[jq, 0:1, :]
            for c in range(n_chunks):
                r = pl.ds(c * KEY_CHUNK, KEY_CHUNK)
                s2 = s_ref[r, :] + bias_ref[jq, pl.ds(off + c * KEY_CHUNK, KEY_CHUNK), :]
                s_ref[r, :] = s2
                m = jnp.maximum(m, jnp.max(s2, axis=0, keepdims=True))
            l = jnp.exp(sink_ref[jq, 0:1, :] - m)
            for c in range(n_chunks):
                p = jnp.exp(s_ref[pl.ds(c * KEY_CHUNK, KEY_CHUNK), :] - m)
                l = l + jnp.sum(p, axis=0, keepdims=True)
                pst_ref[pl.ds(2 * c * KEY_CHUNK, KEY_CHUNK), :] = p[:, :Q_BLOCK].astype(BF16)
                pst_ref[pl.ds((2 * c + 1) * KEY_CHUNK, KEY_CHUNK), :] = p[:, Q_BLOCK:].astype(BF16)
            vl = jnp.concatenate([vt_ref[2 * jq + hq, c0 + c] for c in range(n_chunks) for hq in range(2)], axis=1)
            ot = _dot(vl, pst_ref[...], "nn")
            inv = 1.0 / l
            ot = ot * _by_half(inv[:, :Q_BLOCK], inv[:, Q_BLOCK:], PAIR)
            y_ref[:, pl.ds(jq * PAIR, PAIR)] = ot.T.astype(BF16)
            lse2 = m + jnp.log(l)
            lse = jnp.where(sub8 == 2 * jq, lse2[:, :Q_BLOCK], lse)
            lse = jnp.where(sub8 == 2 * jq + 1, lse2[:, Q_BLOCK:], lse)
        lse_ref[...] = lse
        plumb.finish_at(*comm_refs, (pl.program_id(0) == n_batch - 1) & (i == nq - 1))

    q_spec, k_spec, v_spec = _attn_specs(cfg, s, nq)
    res = pl.pallas_call(
        body, name=name, grid=(n_batch, nq),
        in_specs=[q_spec, k_spec, v_spec, _const_spec((N_HEADS // 2, wext, 2 * Q_BLOCK)),
                  _const_spec((N_HEADS // 2, 8, 2 * Q_BLOCK)), _const_spec((1, PAIR)), _const_spec((1, PAIR))]
        + [ANY] * len(plumb.args),
        out_specs=[pl.BlockSpec((Q_BLOCK, A_WIDTH), lambda b, i: (b * nq + i, 0)),
                   pl.BlockSpec((None, N_HEADS, Q_BLOCK), lambda b, i: (b * nq + i, 0, 0))]
        + [ANY] * len(plumb.out_shape),
        out_shape=[_sds((t, A_WIDTH), BF16), _sds((t // Q_BLOCK, N_HEADS, Q_BLOCK), F32)] + plumb.out_shape,
        scratch_shapes=[pltpu.VMEM((s, kw), BF16), pltpu.VMEM((N_HEADS, nkc, PAIR, KEY_CHUNK), BF16),
                        pltpu.VMEM((w, 2 * Q_BLOCK), F32), pltpu.VMEM((2 * w, Q_BLOCK), BF16)] + plumb.scratch,
        input_output_aliases=plumb.aliases,
        compiler_params=_params(),
    )(qkv, qkv, qkv, bias2, sink2, gq, gk, *plumb.args)
    return plumb.deliver(res)


def _attn_bwd(name, qkv, bias2, sink2, gq, gk, y, dy, lse, cfg, n_batch, want_dbias, comms=()):
    t = qkv.shape[0]
    s = t // n_batch
    nq = s // Q_BLOCK
    nkc = s // KEY_CHUNK
    prev, group, kw = cfg["prev"], cfg["group"], cfg["kw"]
    w = prev + Q_BLOCK
    n_chunks = w // KEY_CHUNK
    wext = bias2.shape[1]
    plumb = _CommPlumbing(comms, 10, 7, 9)
    n_all_in = 10 + len(plumb.args)
    n_all_out = 7 + len(plumb.out_shape)

    def body(*refs):
        q_ref, k_ref, v_ref, bias_ref, sink_ref, gq_ref, gk_ref, y_ref, dy_ref, lse_ref = refs[:10]
        dq_ref, dk_ref, dv_ref, db_ref, dsink_ref, dgq_ref, dgk_ref = refs[n_all_in:n_all_in + 7]
        kn_ref, knt_ref, dkn_ref, dvs_ref, s_ref, dp_ref, pb_ref, dsb_ref, dst_ref = \
            refs[n_all_in + n_all_out:n_all_in + n_all_out + 9]
        b = pl.program_id(0)
        i = pl.program_id(1)
        first = (b == 0) & (i == 0)
        comm_refs = (refs[:n_all_in], refs[n_all_in:n_all_in + n_all_out], refs[n_all_in + n_all_out:])
        plumb.start_at(*comm_refs, first)

        @pl.when(i == 0)
        def _():
            for jk in range(kw // PAIR):
                cols = pl.ds(jk * PAIR, PAIR)
                heads = [h for h in range(N_HEADS) if (h // group) // 2 == jk]
                for c in range(nkc):
                    rows = pl.ds(c * KEY_CHUNK, KEY_CHUNK)
                    kn, _, _ = _pair_norm(k_ref[rows, cols].astype(F32), gk_ref[...])
                    kn_ref[rows, cols] = kn.astype(BF16)
                    _place_transposed(kn, knt_ref, c, heads, group)
            dkn_ref[...] = jnp.zeros(dkn_ref.shape, F32)
            dvs_ref[...] = jnp.zeros(dvs_ref.shape, F32)

        @pl.when(first)
        def _():
            db_ref[...] = jnp.zeros(db_ref.shape, F32)
            dsink_ref[...] = jnp.zeros(dsink_ref.shape, F32)
            dgq_ref[...] = jnp.zeros(dgq_ref.shape, F32)
            dgk_ref[...] = jnp.zeros(dgk_ref.shape, F32)

        start, off = _attn_window(i, prev)
        c0 = start // KEY_CHUNK
        for jq in range(N_HEADS // 2):
            cols = pl.ds(jq * PAIR, PAIR)
            kcols = pl.ds((((2 * jq) // group) // 2) * PAIR, PAIR)
            qn, q_hat, q_rstd = _pair_norm(q_ref[:, cols].astype(F32), gq_ref[...])
            qs = _stack_pair(qn * ATTN_SCALE, jq, group)
            do_pair = dy_ref[:, cols].astype(F32)
            dos = _stack_pair(do_pair, jq, group)
            prod_t = (do_pair * y_ref[:, cols].astype(F32)).T
            lo = _sub_lo(prod_t.shape)
            delta2 = jnp.concatenate([jnp.sum(jnp.where(lo, prod_t, 0.0), axis=0, keepdims=True),
                                      jnp.sum(jnp.where(lo, 0.0, prod_t), axis=0, keepdims=True)], axis=1)
            lse2 = jnp.concatenate([lse_ref[2 * jq:2 * jq + 1, :], lse_ref[2 * jq + 1:2 * jq + 2, :]], axis=1)
            dsk = -jnp.exp(sink_ref[jq, 0:1, :] - lse2) * delta2
            dsink_ref[2 * jq:2 * jq + 1, :] += dsk[:, :Q_BLOCK]
            dsink_ref[2 * jq + 1:2 * jq + 2, :] += dsk[:, Q_BLOCK:]
            rows_w = pl.ds(start, w)
            s_ref[...] = _dot(kn_ref[rows_w, kcols], qs, "nt")
            dp_ref[...] = _dot(v_ref[rows_w, kcols], dos, "nt")
            for c in range(n_chunks):
                r = pl.ds(c * KEY_CHUNK, KEY_CHUNK)
                brows = pl.ds(off + c * KEY_CHUNK, KEY_CHUNK)
                p = jnp.exp(s_ref[r, :] + bias_ref[jq, brows, :] - lse2)
                ds = p * (dp_ref[r, :] - delta2)
                if want_dbias:
                    db_ref[jq, brows, :] += ds
                ds_b = ds.astype(BF16)
                pb_ref[r, :] = p.astype(BF16)
                dsb_ref[r, :] = ds_b
                dst_ref[pl.ds(2 * c * KEY_CHUNK, KEY_CHUNK), :] = ds_b[:, :Q_BLOCK]
                dst_ref[pl.ds((2 * c + 1) * KEY_CHUNK, KEY_CHUNK), :] = ds_b[:, Q_BLOCK:]
            dkn_ref[rows_w, kcols] += _dot(dsb_ref[...], qs, "nn")
            dvs_ref[rows_w, kcols] += _dot(pb_ref[...], dos, "nn")
            kl = jnp.concatenate([knt_ref[2 * jq + hq, c0 + c] for c in range(n_chunks) for hq in range(2)], axis=1)
            dqt = _dot(kl, dst_ref[...], "nn")
            dq_raw, dg = _pair_norm_bwd(dqt.T * ATTN_SCALE, q_hat, q_rstd, gq_ref[...])
            dq_ref[:, cols] = dq_raw.astype(BF16)
            dgq_ref[...] += dg

        @pl.when(i == nq - 1)
        def _():
            for jk in range(kw // PAIR):
                kcols = pl.ds(jk * PAIR, PAIR)
                _, k_hat, k_rstd = _pair_norm(k_ref[:, kcols].astype(F32), gk_ref[...])
                dk_raw, dg = _pair_norm_bwd(dkn_ref[:, kcols], k_hat, k_rstd, gk_ref[...])
                dk_ref[:, kcols] = dk_raw.astype(BF16)
                dgk_ref[...] += dg
            dv_ref[...] = dvs_ref[...].astype(BF16)

        plumb.finish_at(*comm_refs, (b == n_batch - 1) & (i == nq - 1))

    q_spec, k_spec, v_spec = _attn_specs(cfg, s, nq)
    row = pl.BlockSpec((Q_BLOCK, A_WIDTH), lambda b, i: (b * nq + i, 0))
    kv_out = pl.BlockSpec((s, kw), lambda b, i: (b, 0))
    pair_bias = _const_spec((N_HEADS // 2, wext, 2 * Q_BLOCK))
    res = pl.pallas_call(
        body, name=name, grid=(n_batch, nq),
        in_specs=[q_spec, k_spec, v_spec, pair_bias, _const_spec((N_HEADS // 2, 8, 2 * Q_BLOCK)),
                  _const_spec((1, PAIR)), _const_spec((1, PAIR)), row, row,
                  pl.BlockSpec((None, N_HEADS, Q_BLOCK), lambda b, i: (b * nq + i, 0, 0))] + [ANY] * len(plumb.args),
        out_specs=[row, kv_out, kv_out, pair_bias, _const_spec((N_HEADS, 128)),
                   _const_spec((1, PAIR)), _const_spec((1, PAIR))] + [ANY] * len(plumb.out_shape),
        out_shape=[_sds((t, A_WIDTH), BF16), _sds((t, kw), BF16), _sds((t, kw), BF16),
                   _sds((N_HEADS // 2, wext, 2 * Q_BLOCK), F32), _sds((N_HEADS, 128), F32),
                   _sds((1, PAIR), F32), _sds((1, PAIR), F32)] + plumb.out_shape,
        scratch_shapes=[pltpu.VMEM((s, kw), BF16), pltpu.VMEM((N_HEADS, nkc, PAIR, KEY_CHUNK), BF16),
                        pltpu.VMEM((s, kw), F32), pltpu.VMEM((s, kw), F32),
                        pltpu.VMEM((w, 2 * Q_BLOCK), F32), pltpu.VMEM((w, 2 * Q_BLOCK), F32),
                        pltpu.VMEM((w, 2 * Q_BLOCK), BF16), pltpu.VMEM((w, 2 * Q_BLOCK), BF16),
                        pltpu.VMEM((2 * w, Q_BLOCK), BF16)] + plumb.scratch,
        input_output_aliases=plumb.aliases,
        compiler_params=_params(),
    )(qkv, qkv, qkv, bias2, sink2, gq, gk, y, dy, lse, *plumb.args)
    return plumb.deliver(res)


def _band_tables(prev_chunks):
    prev = prev_chunks * CHUNK
    wext = 2 * prev + Q_BLOCK
    jj = np.arange(wext)[:, None]
    ii = np.arange(Q_BLOCK)[None, :]
    dist = prev + ii - jj
    rel_chunk = (prev // CHUNK + ii // CHUNK) - jj // CHUNK
    allowed = (rel_chunk >= 0) & (rel_chunk <= prev_chunks)
    return dist, allowed


def _alibi_slopes():
    return np.array([2.0 ** (-8.0 * (h + 1) / N_HEADS) for h in range(N_HEADS)], dtype=np.float32)


def _diag_onehot(prev, wext):
    n_diag = wext + Q_BLOCK - 1
    idx = np.clip(prev + Q_BLOCK - 1 - np.arange(n_diag), -A_MAX_REL, A_MAX_REL) + A_MAX_REL
    onehot = np.zeros((n_diag, 2 * A_MAX_REL + 1), np.float32)
    onehot[np.arange(n_diag), idx] = 1.0
    return onehot


def _bias_a(rel_bias):
    prev = A_PREV_CHUNKS * CHUNK
    _, allowed = _band_tables(A_PREV_CHUNKS)
    wext = allowed.shape[0]
    n_diag = wext + Q_BLOCK - 1
    seq = jnp.dot(rel_bias, jnp.asarray(_diag_onehot(prev, wext).T), precision=lax.Precision.HIGHEST)
    seq = jnp.pad(seq, ((0, 0), (0, 1)))
    rows = jnp.broadcast_to(seq[:, None, :], (N_HEADS, Q_BLOCK, n_diag + 1)).reshape(N_HEADS, -1)
    skew = rows[:, :Q_BLOCK * n_diag].reshape(N_HEADS, Q_BLOCK, n_diag)
    tile = jnp.transpose(skew[:, :, Q_BLOCK - 1:Q_BLOCK - 1 + wext], (0, 2, 1))
    return jnp.where(jnp.asarray(allowed)[None], tile, NEG_INF)


def _bias_b():
    dist, allowed = _band_tables(B_PREV_CHUNKS)
    bias = -_alibi_slopes()[:, None, None] * np.abs(dist).astype(np.float32)[None]
    return jnp.asarray(np.where(allowed[None], bias, np.float32(NEG_INF)).astype(np.float32))


def _rel_bias_grad(db_t):
    prev = A_PREV_CHUNKS * CHUNK
    wext = db_t.shape[1]
    n_diag = wext + Q_BLOCK - 1
    wp = n_diag + Q_BLOCK - 1
    xp = jnp.pad(jnp.transpose(db_t, (0, 2, 1)), ((0, 0), (0, 0), (Q_BLOCK - 1, Q_BLOCK - 1)))
    flat = jnp.pad(xp.reshape(N_HEADS, Q_BLOCK * wp), ((0, 0), (0, Q_BLOCK)))
    skew = flat.reshape(N_HEADS, Q_BLOCK, wp + 1)[:, :, :n_diag]
    diag = jnp.sum(skew, axis=1)
    return jnp.dot(diag, jnp.asarray(_diag_onehot(prev, wext)), precision=lax.Precision.HIGHEST)


def _ew(name, fn, ins, out_dtypes):
    r, c = ins[0].shape
    rb = _pick(r, max(16, (1 << 19) // c), 16)
    spec = pl.BlockSpec((rb, c), lambda i: (i, 0))

    def body(*refs):
        vals = fn(*[ref[...] for ref in refs[:len(ins)]])
        for ref, val in zip(refs[len(ins):], vals):
            ref[...] = val.astype(ref.dtype)

    return pl.pallas_call(
        body, name=name, grid=(r // rb,), in_specs=[spec] * len(ins), out_specs=[spec] * len(out_dtypes),
        out_shape=[_sds((r, c), dt) for dt in out_dtypes], compiler_params=_params(),
    )(*ins)


def _cast_into_slot(name, w, chip):
    r, c = w.shape
    rb = _pick(r, max(16, (1 << 19) // c), 16)

    def body(chip_ref, w_ref, o_ref):
        o_ref[...] = w_ref[...].astype(BF16)

    return pl.pallas_call(
        body, name=name,
        grid_spec=pltpu.PrefetchScalarGridSpec(
            num_scalar_prefetch=1, grid=(r // rb,),
            in_specs=[pl.BlockSpec((rb, c), lambda i, chip_ref: (i, 0))],
            out_specs=pl.BlockSpec((None, rb, c), lambda i, chip_ref: (chip_ref[0], i, 0))),
        out_shape=_sds((N_CHIPS, r, c), BF16), compiler_params=_params(),
    )(chip, w)


def _adamw_math(w, g, m, v):
    m = ADAM_B1 * m + (1.0 - ADAM_B1) * g
    v = ADAM_B2 * v + (1.0 - ADAM_B2) * (g * g)
    m_hat = m / (1.0 - ADAM_B1 ** ADAM_STEP)
    v_hat = v / (1.0 - ADAM_B2 ** ADAM_STEP)
    delta = -ADAM_LR * (m_hat / (jnp.sqrt(v_hat) + ADAM_EPS) + ADAM_WD * w)
    return delta, m, v


def _adamw_terms(name, terms, w, m, v):
    r, c = w.shape
    hr = r // 2
    rb = _pick(hr, max(16, (1 << 19) // c), 16)
    nb = hr // rb

    def body(t_ref, w_ref, m_ref, v_ref, g_ref, d_ref, nm_ref, nv_ref):
        g = t_ref[0].astype(F32)
        for k in range(1, N_CHIPS):
            g = g + t_ref[k].astype(F32)
        delta, nm, nv = _adamw_math(w_ref[...], g, m_ref[...], v_ref[...])
        g_ref[...] = g
        d_ref[...] = delta
        nm_ref[...] = nm
        nv_ref[...] = nv

    spec = pl.BlockSpec((rb, c), lambda h, i: (h * nb + i, 0))
    return pl.pallas_call(
        body, name=name, grid=(2, nb),
        in_specs=[pl.BlockSpec((None, N_CHIPS, rb, c), lambda h, i: (h, 0, i, 0)), spec, spec, spec],
        out_specs=[spec] * 4, out_shape=[_sds((r, c), F32)] * 4, compiler_params=_params(),
    )(terms, w, m, v)


def _mesh_place():
    x, y, c = lax.axis_index("x"), lax.axis_index("y"), lax.axis_index("c")
    chips = [(x, 1 - y), (1 - x, y), (1 - x, 1 - y)]
    return x, y, c, chips


def _all_gather_weights(bufs):
    n = len(bufs)

    def body(*refs):
        outs = refs[n:2 * n]
        ici_send, ici_recv, d2d_send, d2d_recv = refs[2 * n:]
        x, y, c, chips = _mesh_place()
        me = 2 * x + y
        sibling = (x, y, 1 - c)
        sent = []
        for wi in range(n):
            for k, (tx, ty) in enumerate(chips):
                own = outs[wi].at[me, c]
                cp = pltpu.make_async_remote_copy(
                    src_ref=own, dst_ref=own, send_sem=ici_send.at[wi * 3 + k], recv_sem=ici_recv.at[wi * 3 + k],
                    device_id=(tx, ty, c), device_id_type=MESH)
                cp.start()
                sent.append(cp)
        passed = []
        for wi in range(n):
            for k, (tx, ty) in enumerate(chips):
                slab = outs[wi].at[2 * tx + ty, c]
                pltpu.make_async_remote_copy(
                    src_ref=slab, dst_ref=slab, send_sem=ici_send.at[wi * 3 + k], recv_sem=ici_recv.at[wi * 3 + k],
                    device_id=(tx, ty, c), device_id_type=MESH).wait_recv()
                fw = pltpu.make_async_remote_copy(
                    src_ref=slab, dst_ref=slab, send_sem=d2d_send.at[wi * 3 + k], recv_sem=d2d_recv.at[wi * 3 + k],
                    device_id=sibling, device_id_type=MESH)
                fw.start()
                passed.append(fw)
        for wi in range(n):
            for k, (tx, ty) in enumerate(chips):
                slab = outs[wi].at[2 * tx + ty, 1 - c]
                pltpu.make_async_remote_copy(
                    src_ref=slab, dst_ref=slab, send_sem=d2d_send.at[wi * 3 + k], recv_sem=d2d_recv.at[wi * 3 + k],
                    device_id=sibling, device_id_type=MESH).wait_recv()
        for cp in sent + passed:
            cp.wait_send()

    return pl.pallas_call(
        body, name="all_gather_weights",
        in_specs=[ANY] * n, out_specs=[ANY] * n,
        out_shape=[_sds(g.shape, g.dtype) for g in bufs],
        scratch_shapes=[pltpu.SemaphoreType.DMA((3 * n,))] * 4,
        input_output_aliases={i: i for i in range(n)},
    )(*bufs)


def _run_comms(name, comms):
    plumb = _CommPlumbing(comms, 0, 0, 0)
    n_in, n_out = len(plumb.args), len(plumb.out_shape)

    def body(*refs):
        parts = []
        i0, o0, s0 = 0, n_in, n_in + n_out
        for cm in plumb.comms:
            parts.append((refs[i0:i0 + len(cm.ins)], refs[o0:o0 + len(cm.outs)], refs[s0:s0 + len(cm.sems)]))
            i0 += len(cm.ins)
            o0 += len(cm.outs)
            s0 += len(cm.sems)
        for cm, part in zip(plumb.comms, parts):
            cm.start(*part)
        for cm, part in zip(plumb.comms, parts):
            cm.finish(*part)

    res = pl.pallas_call(
        body, name=name, in_specs=[ANY] * n_in, out_specs=[ANY] * n_out, out_shape=plumb.out_shape,
        scratch_shapes=plumb.scratch, input_output_aliases=plumb.aliases,
    )(*plumb.args)
    plumb.deliver(res)


def _gather_ici(bufs, targets=(0, 1, 2)):
    n = len(bufs)

    def copies(outs, sems):
        send_sem, recv_sem = sems
        x, y, c, chips = _mesh_place()
        me = 2 * x + y
        sends, recvs = [], []
        for wi in range(n):
            for k in targets:
                tx, ty = chips[k]
                sems_k = dict(send_sem=send_sem.at[wi * 3 + k], recv_sem=recv_sem.at[wi * 3 + k],
                              device_id=(tx, ty, c), device_id_type=MESH)
                own = outs[wi].at[me, c]
                sends.append(pltpu.make_async_remote_copy(src_ref=own, dst_ref=own, **sems_k))
                slab = outs[wi].at[2 * tx + ty, c]
                recvs.append(pltpu.make_async_remote_copy(src_ref=slab, dst_ref=slab, **sems_k))
        return sends, recvs

    def start(ins, outs, sems):
        for cp in copies(outs, sems)[0]:
            cp.start()

    def finish(ins, outs, sems):
        sends, recvs = copies(outs, sems)
        for cp in recvs:
            cp.wait_recv()
        for cp in sends:
            cp.wait_send()

    return _Comm(bufs, [_sds(g.shape, g.dtype) for g in bufs], {i: i for i in range(n)},
                 [pltpu.SemaphoreType.DMA((3 * n,)), pltpu.SemaphoreType.DMA((3 * n,))], start, finish)


def _gather_d2d(gathered):
    n = len(gathered)

    def copies(outs, sems):
        send_sem, recv_sem = sems
        x, y, c, chips = _mesh_place()
        sends, recvs = [], []
        for wi in range(n):
            for k, (tx, ty) in enumerate(chips):
                sems_k = dict(send_sem=send_sem.at[wi * 3 + k], recv_sem=recv_sem.at[wi * 3 + k],
                              device_id=(x, y, 1 - c), device_id_type=MESH)
                mine = outs[wi].at[2 * tx + ty, c]
                theirs = outs[wi].at[2 * tx + ty, 1 - c]
                sends.append(pltpu.make_async_remote_copy(src_ref=mine, dst_ref=mine, **sems_k))
                recvs.append(pltpu.make_async_remote_copy(src_ref=theirs, dst_ref=theirs, **sems_k))
        return sends, recvs

    def start(ins, outs, sems):
        for cp in copies(outs, sems)[0]:
            cp.start()

    def finish(ins, outs, sems):
        sends, recvs = copies(outs, sems)
        for cp in recvs:
            cp.wait_recv()
        for cp in sends:
            cp.wait_send()

    return _Comm(gathered, [_sds(g.shape, g.dtype) for g in gathered], {i: i for i in range(n)},
                 [pltpu.SemaphoreType.DMA((3 * n,)), pltpu.SemaphoreType.DMA((3 * n,))], start, finish)


def _exchange_halves(grads):
    n = len(grads)

    def copies(ins, outs, sems):
        send_sem, recv_sem = sems
        x, y, c, _ = _mesh_place()
        return [pltpu.make_async_remote_copy(
            src_ref=ins[wi].at[t, 1 - c], dst_ref=outs[wi].at[t],
            send_sem=send_sem.at[wi * N_CHIPS + t], recv_sem=recv_sem.at[wi * N_CHIPS + t],
            device_id=(x, y, 1 - c), device_id_type=MESH) for wi in range(n) for t in range(N_CHIPS)]

    def start(ins, outs, sems):
        for cp in copies(ins, outs, sems):
            cp.start()

    def finish(ins, outs, sems):
        for cp in copies(ins, outs, sems):
            cp.wait()

    return _Comm(grads, [_sds((N_CHIPS,) + g.shape[2:], g.dtype) for g in grads], {},
                 [pltpu.SemaphoreType.DMA((N_CHIPS * n,)), pltpu.SemaphoreType.DMA((N_CHIPS * n,))], start, finish)


def _scatter_ici(sums, terms):
    n = len(sums)

    def copies(ins, outs, sems):
        send_sem, recv_sem = sems
        x, y, c, chips = _mesh_place()
        sends, recvs = [], []
        for wi in range(n):
            for k, (tx, ty) in enumerate(chips):
                sems_k = dict(send_sem=send_sem.at[wi * 3 + k], recv_sem=recv_sem.at[wi * 3 + k],
                              device_id=(tx, ty, c), device_id_type=MESH)
                land = outs[wi].at[c, k + 1]
                sends.append(pltpu.make_async_remote_copy(src_ref=ins[wi].at[2 * tx + ty], dst_ref=land, **sems_k))
                recvs.append(pltpu.make_async_remote_copy(src_ref=land, dst_ref=land, **sems_k))
        return sends, recvs

    def start(ins, outs, sems):
        for cp in copies(ins, outs, sems)[0]:
            cp.start()

    def finish(ins, outs, sems):
        sends, recvs = copies(ins, outs, sems)
        for cp in recvs:
            cp.wait_recv()
        for cp in sends:
            cp.wait_send()

    return _Comm(list(sums) + list(terms), [_sds(t.shape, t.dtype) for t in terms], {n + i: i for i in range(n)},
                 [pltpu.SemaphoreType.DMA((3 * n,)), pltpu.SemaphoreType.DMA((3 * n,))], start, finish)


def _scatter_d2d(terms):
    n = len(terms)

    def copies(outs, sems):
        send_sem, recv_sem = sems
        x, y, c, _ = _mesh_place()
        sends, recvs = [], []
        for wi in range(n):
            sems_w = dict(send_sem=send_sem.at[wi], recv_sem=recv_sem.at[wi],
                          device_id=(x, y, 1 - c), device_id_type=MESH)
            sends.append(pltpu.make_async_remote_copy(src_ref=outs[wi].at[c], dst_ref=outs[wi].at[c], **sems_w))
            recvs.append(pltpu.make_async_remote_copy(src_ref=outs[wi].at[1 - c], dst_ref=outs[wi].at[1 - c], **sems_w))
        return sends, recvs

    def start(ins, outs, sems):
        for cp in copies(outs, sems)[0]:
            cp.start()

    def finish(ins, outs, sems):
        sends, recvs = copies(outs, sems)
        for cp in recvs:
            cp.wait_recv()
        for cp in sends:
            cp.wait_send()

    return _Comm(terms, [_sds(t.shape, t.dtype) for t in terms], {i: i for i in range(n)},
                 [pltpu.SemaphoreType.DMA((n,)), pltpu.SemaphoreType.DMA((n,))], start, finish)


def _chip_sum(name, grad, got, place):
    _, _, hr, c = grad.shape
    rb = _pick(hr, max(16, (1 << 19) // c), 16)

    def body(place_ref, a_ref, b_ref, o_ref, own_ref):
        total = (a_ref[...].astype(F32) + b_ref[...].astype(F32)).astype(BF16)
        o_ref[...] = total

        @pl.when(pl.program_id(1) == place_ref[1])
        def _():
            own_ref[...] = total

    out_spec = pl.BlockSpec((None, rb, c), lambda i, t, place_ref: (t, i, 0))
    return pl.pallas_call(
        body, name=name,
        grid_spec=pltpu.PrefetchScalarGridSpec(
            num_scalar_prefetch=1, grid=(hr // rb, N_CHIPS),
            in_specs=[pl.BlockSpec((None, None, rb, c), lambda i, t, place_ref: (t, place_ref[0], i, 0)), out_spec],
            out_specs=[out_spec,
                       pl.BlockSpec((None, None, rb, c), lambda i, t, place_ref: (place_ref[0], 0, i, 0))]),
        out_shape=[_sds((N_CHIPS, hr, c), BF16), _sds((2, N_CHIPS, hr, c), BF16)], compiler_params=_params(),
    )(place, grad, got)


def _all_reduce_small(pack):
    r = pack.shape[0]

    def body(p_ref, o_ref, land_ref, send_sem, recv_sem):
        x, y, c, _ = _mesh_place()
        me = 4 * x + 2 * y + c
        flips = [(k >> 2 & 1, k >> 1 & 1, k & 1) for k in range(1, N_DEV)]

        def peer(fx, fy, fc):
            return (1 - x if fx else x, 1 - y if fy else y, 1 - c if fc else c)

        land_ref[me] = p_ref[...]
        sent = []
        for k, flip in enumerate(flips):
            cp = pltpu.make_async_remote_copy(
                src_ref=p_ref, dst_ref=land_ref.at[me], send_sem=send_sem.at[k], recv_sem=recv_sem.at[k],
                device_id=peer(*flip), device_id_type=MESH)
            cp.start()
            sent.append(cp)
        for k, flip in enumerate(flips):
            px, py, pc = peer(*flip)
            slot = land_ref.at[4 * px + 2 * py + pc]
            pltpu.make_async_remote_copy(
                src_ref=slot, dst_ref=slot, send_sem=send_sem.at[k], recv_sem=recv_sem.at[k],
                device_id=(px, py, pc), device_id_type=MESH).wait_recv()
        total = land_ref[0]
        for d in range(1, N_DEV):
            total = total + land_ref[d]
        o_ref[...] = total
        for cp in sent:
            cp.wait_send()

    vmem = pl.BlockSpec(memory_space=pltpu.VMEM)
    return pl.pallas_call(
        body, name="all_reduce_small", in_specs=[vmem], out_specs=vmem, out_shape=_sds((r, 128), F32),
        scratch_shapes=[pltpu.VMEM((N_DEV, r, 128), F32), pltpu.SemaphoreType.DMA((N_DEV - 1,)),
                        pltpu.SemaphoreType.DMA((N_DEV - 1,))],
    )(pack)


PACK_TILE = 8 * 128


def _pack(items):
    rows, i = [], 0
    while i < len(items):
        j = i
        while j < len(items) and items[j].size == items[i].size:
            j += 1
        group = jnp.stack([it.reshape(-1).astype(F32) for it in items[i:j]])
        rows.append(jnp.pad(group, ((0, 0), (0, -group.shape[1] % PACK_TILE))).reshape(-1, 128))
        i = j
    return jnp.concatenate(rows, axis=0)


def _unpack(pack, shapes):
    out, row = [], 0
    for shp in shapes:
        size = int(np.prod(shp))
        nrow = -(-size // PACK_TILE) * (PACK_TILE // 128)
        out.append(pack[row:row + nrow].reshape(-1)[:size].reshape(shp))
        row += nrow
    return out


BIG = ["ffn1_w_gu", "ffn1_w_down", "w_in", "w_gate", "w_proj_a", "w_proj_b", "w_out",
       "ffn2_w_gu", "ffn2_w_down", "w_ple_gate", "w_ple_proj"]
SMALL = ["ffn1_norm", "mix_norm", "ffn2_norm", "ple_norm", "a_q_norm", "a_k_norm", "b_q_norm", "b_k_norm",
         "a_rel_bias", "b_sinks"]
WEIGHTS = ["ffn1_norm", "ffn1_w_gu", "ffn1_w_down", "mix_norm", "w_in", "a_q_norm", "a_k_norm", "a_rel_bias",
           "b_q_norm", "b_k_norm", "b_sinks", "w_gate", "w_proj_a", "w_proj_b", "w_out", "ffn2_norm",
           "ffn2_w_gu", "ffn2_w_down", "ple_norm", "w_ple_gate", "w_ple_proj"]
ATTN_A = dict(prev=A_PREV_CHUNKS * CHUNK, group=1, kw=A_WIDTH, qblk=0, kblk=1, vblk=2)
ATTN_B = dict(prev=B_PREV_CHUNKS * CHUNK, group=N_HEADS // B_KV_HEADS, kw=B_KV_WIDTH, qblk=3,
              kblk=4 * A_WIDTH // B_KV_WIDTH, vblk=4 * A_WIDTH // B_KV_WIDTH + 1)


def _cast_epilogue(accs, extras, outs, ij):
    for acc, out in zip(accs, outs):
        out[...] = acc.astype(out.dtype)


GATHER_FIRST = ["ffn1_w_gu", "ffn1_w_down"]
ROW_SHARDED = ("ffn1_w_down", "ffn2_w_down", "w_out", "w_ple_gate")


def _slotted(name, grad):
    if name == "w_in":
        rows, cols = grad.shape
        grad = jnp.transpose(grad.reshape(rows, N_CHIPS, cols // N_CHIPS), (1, 0, 2))
    elif name in ROW_SHARDED:
        grad = grad.reshape(N_CHIPS, grad.shape[0] // N_CHIPS, grad.shape[1])
    return grad.reshape(N_CHIPS, 2, grad.shape[1] // 2, grad.shape[2])


def _local_step(xt, pt, tgt, n_batch, bufs, small, place):
    t, d = xt.shape
    tm = _pick(t, ROW_TILE, 8)
    tk = _pick(t, ROW_TILE, 8)
    nt = t // tm
    row = pl.BlockSpec((tm, d), lambda i, j, k: (i, 0))
    gs = bufs["w_gate"].shape[2]
    ps = bufs["w_proj_a"].shape[2]
    es = bufs["w_ple_proj"].shape[2]
    pdim = pt.shape[1]
    ncols = N_CHIPS * bufs["w_in"].shape[2]
    tin = ncols // 2
    assert 2 * gs == d and 4 * ps == d and 4 * es == d and tin % 128 == 0

    w = {}
    halves = {n: b.reshape(N_CHIPS, 2, b.shape[1] // 2, b.shape[2]) for n, b in bufs.items()}

    def publish(names, arrays):
        for name, g in zip(names, arrays):
            g = g.reshape(N_CHIPS, 2 * g.shape[2], g.shape[3])
            if name in ROW_SHARDED:
                g = g.reshape(N_CHIPS * g.shape[1], g.shape[2])
            elif name == "w_in":
                g = jnp.transpose(g, (1, 0, 2)).reshape(g.shape[1], N_CHIPS * g.shape[2])
            w[name] = g

    class GatherPipe:
        def __init__(self, names):
            self.names = names
            self.stage = None

        def ici(self, targets=(0, 1, 2)):
            self.stage = _gather_ici(self.bufs(), targets)
            return self.stage

        def d2d(self):
            self.stage = _gather_d2d(self.bufs())
            return self.stage

        def bufs(self):
            return self.stage.results if self.stage is not None else [halves[n] for n in self.names]

        def publish(self):
            publish(self.names, self.stage.results)

    class GradPipe:
        def __init__(self, names):
            self.names = names

        def exchange(self, grads):
            self.grads = [_slotted(n, g) for n, g in zip(self.names, grads)]
            self.x = _exchange_halves(self.grads)
            return self.x

        def scatter(self):
            both = [_chip_sum("chip_sum_" + n, g, got, place)
                    for n, g, got in zip(self.names, self.grads, self.x.results)]
            self.s = _scatter_ici([b[0] for b in both], [b[1] for b in both])
            return self.s

        def forward(self):
            self.f = _scatter_d2d(self.s.results)
            return self.f

        def terms(self):
            return dict(zip(self.names, self.f.results))

    publish(GATHER_FIRST, _all_gather_weights([halves[n] for n in GATHER_FIRST]))
    g_in, g_proj, g_ple = GatherPipe(["w_in", "w_gate"]), GatherPipe(["w_proj_a", "w_proj_b", "w_out"]), \
        GatherPipe(["w_ple_gate", "w_ple_proj"])
    g_down2, g_up2 = GatherPipe(["ffn2_w_down"]), GatherPipe(["ffn2_w_gu"])
    h1, ffn1_saved = _ffn_fwd("ffn1", xt, small["ffn1_norm"], w["ffn1_w_gu"], w["ffn1_w_down"],
                              {"up": lambda: [g_in.ici()], "down": lambda: [g_in.d2d(), g_proj.ici()]})
    g_in.publish()
    un = _rms_fwd("mix_norm", h1, small["mix_norm"])
    w_in, wgate = w["w_in"], w["w_gate"]
    (qkv,) = _mm(
        "qkv", "nn", (nt, 2, 1),
        [(un, row, w_in, pl.BlockSpec((d, tin), lambda i, j, k: (0, j)))], [],
        [(_sds((t, ncols), BF16), pl.BlockSpec((tm, tin), lambda i, j, k: (i, j)))], (tm, tin), _cast_epilogue,
        j_outer=True, comms=[g_proj.d2d(), g_ple.ici()])
    g_proj.publish()
    wpa, wpb, wout = w["w_proj_a"], w["w_proj_b"], w["w_out"]

    def gate_epilogue(accs, extras, outs, ij):
        outs[0][...] = jax.nn.sigmoid(accs[0]).astype(BF16)

    (gates,) = _mm(
        "gate", "nn", (nt, 4, 1),
        [(un, row, wgate, pl.BlockSpec((None, d, gs), lambda i, j, k: (j, 0, 0)))], [],
        [(_sds((2, t, d), BF16), pl.BlockSpec((None, tm, gs), lambda i, j, k: (j // 2, i, j % 2)))],
        (tm, gs), gate_epilogue, j_outer=True, chunked=True, comms=[g_ple.d2d(), g_down2.ici()])
    g_ple.publish()
    wpg, wpe = w["w_ple_gate"], w["w_ple_proj"]

    bias_a = _pair_bias(_bias_a(small["a_rel_bias"][0]))
    bias_b = _pair_bias(_bias_b())
    sink_a = _pair_rows(jnp.full((N_HEADS, 128), NEG_INF, F32))
    sink_b = _pair_rows(jnp.broadcast_to(small["b_sinks"][0][:, None], (N_HEADS, 128)))
    gqa, gka, gqb, gkb = [jnp.tile(small[k], (1, 2)) for k in ("a_q_norm", "a_k_norm", "b_q_norm", "b_k_norm")]
    ya, lse_a = _attn_fwd("attn_a_fwd", qkv, bias_a, sink_a, gqa, gka, ATTN_A, n_batch,
                          comms=[g_down2.d2d(), g_up2.ici(targets=(0, 1))])
    g_down2.publish()
    yb, lse_b = _attn_fwd("attn_b_fwd", qkv, bias_b, sink_b, gqb, gkb, ATTN_B, n_batch,
                          comms=[g_up2.ici(targets=(2,))])

    def merge_epilogue(accs, extras, outs, ij):
        pa, pb = accs
        outs[0][...] = (extras[0][...].astype(F32) * pa + extras[1][...].astype(F32) * pb).astype(BF16)
        outs[1][...] = pa.astype(BF16)
        outs[2][...] = pb.astype(BF16)

    y_spec = pl.BlockSpec((tm, A_WIDTH), lambda i, j, k: (i, 0))
    proj_spec = pl.BlockSpec((None, A_WIDTH, ps), lambda i, j, k: (j, 0, 0))
    tile_ps = pl.BlockSpec((tm, ps), lambda i, j, k: (i, j))
    merged, pa, pb = _mm(
        "proj_merge", "nn", (nt, 4, 1),
        [(ya, y_spec, wpa, proj_spec), (yb, y_spec, wpb, proj_spec)],
        [(gates, pl.BlockSpec((None, tm, ps), lambda i, j, k: (0, i, j))),
         (gates, pl.BlockSpec((None, tm, ps), lambda i, j, k: (1, i, j)))],
        [(_sds((t, d), BF16), tile_ps)] * 3, (tm, ps), merge_epilogue, comms=[g_up2.d2d()])
    g_up2.publish()

    def residual_epilogue(accs, extras, outs, ij):
        outs[0][...] = extras[0][...] + accs[0]

    (h2,) = _mm(
        "out_proj", "nn", (nt, 1, 1),
        [(merged, row, wout, pl.BlockSpec((d, d), lambda i, j, k: (0, 0)))],
        [(h1, row)], [(_sds((t, d), F32), row)], (tm, d), residual_epilogue)

    h3, ffn2_saved = _ffn_fwd("ffn2", h2, small["ffn2_norm"], w["ffn2_w_gu"], w["ffn2_w_down"], {})
    n3 = _rms_fwd("ple_norm", h3, small["ple_norm"])
    tile_es = pl.BlockSpec((tm, es), lambda i, j, k: (i, j))
    (pe,) = _mm(
        "ple_embed", "nn", (nt, 4, 1),
        [(pt, pl.BlockSpec((tm, pdim), lambda i, j, k: (i, 0)), wpe, pl.BlockSpec((None, pdim, es), lambda i, j, k: (j, 0, 0)))],
        [], [(_sds((t, d), F32), tile_es)], (tm, es), _cast_epilogue)

    th = _pick(d, 512)

    def head_epilogue(accs, extras, outs, ij):
        h3_ref, pe_ref, tgt_ref = extras
        dy_ref, dpe_ref, dz_ref, loss_ref = outs
        pg = jax.nn.sigmoid(accs[0])
        pev = pe_ref[...]
        diff = h3_ref[...] + pg * pev - tgt_ref[...]
        dy = diff * (1.0 / d)
        dy_ref[...] = dy
        dpe_ref[...] = (dy * pg).astype(BF16)
        dz_ref[...] = (dy * pev * pg * (1.0 - pg)).astype(BF16)
        _accumulate(loss_ref, jnp.full(loss_ref.shape, jnp.sum(diff * diff), F32), (ij[0] == 0) & (ij[1] == 0))

    tile_h = pl.BlockSpec((tm, th), lambda i, j, k: (i, j))
    dy, dpe, dz, loss_acc = _mm(
        "ple_gate_loss", "nn", (nt, d // th, 1),
        [(n3, row, wpg, pl.BlockSpec((d, th), lambda i, j, k: (0, j)))],
        [(h3, tile_h), (pe, tile_h), (tgt, tile_h)],
        [(_sds((t, d), F32), tile_h), (_sds((t, d), BF16), tile_h), (_sds((t, d), BF16), tile_h),
         (_sds((8, 128), F32), pl.BlockSpec((8, 128), lambda i, j, k: (0, 0)))],
        (tm, th), head_epilogue, j_outer=True, chunked=True)
    loss = 0.5 * loss_acc[0, 0] / d

    nk = t // tk
    (dwpe,) = _mm(
        "d_w_ple_proj", "tn", (1, 4, nk),
        [(pt, pl.BlockSpec((tk, pdim), lambda i, j, k: (k, 0)), dpe, pl.BlockSpec((tk, es), lambda i, j, k: (k, j)))],
        [], [(_sds((4, pdim, es), BF16), pl.BlockSpec((None, pdim, es), lambda i, j, k: (j, 0, 0)))],
        (pdim, es), _cast_epilogue)

    def dense_grad(name, a, dyb, comms=()):
        (res,) = _mm(
            name, "tn", (1, d // th, nk),
            [(a, pl.BlockSpec((tk, d), lambda i, j, k: (k, 0)), dyb, pl.BlockSpec((tk, th), lambda i, j, k: (k, j)))],
            [], [(_sds((d, d), BF16), pl.BlockSpec((d, th), lambda i, j, k: (0, j)))], (d, th), _cast_epilogue,
            comms=comms)
        return res

    dwpg = dense_grad("d_w_ple_gate", n3, dz)
    tmn = _pick(t, ROW_TILE, 8)
    extras, outs = _rms_bwd_io(h3, small["ple_norm"], dy, tmn)
    dh3, dh3_b, d_ple_norm = _mm(
        "d_ple_norm", "nt", (t // tmn, 1, 1),
        [(dz, pl.BlockSpec((tmn, d), lambda i, j, k: (i, 0)), wpg, pl.BlockSpec((d, d), lambda i, j, k: (0, 0)))],
        extras, outs, (tmn, d), _rms_bwd_epilogue)

    up2, down2, ple = GradPipe(["ffn2_w_gu"]), GradPipe(["ffn2_w_down"]), GradPipe(["w_ple_gate", "w_ple_proj"])
    proj = GradPipe(["w_proj_a", "w_proj_b", "w_out"])
    dh2, dh2_b, d_ffn2_norm, dwgu2, dwd2 = _ffn_bwd(
        "ffn2", dh3, dh3_b, h2, small["ffn2_norm"], w["ffn2_w_gu"], w["ffn2_w_down"], ffn2_saved,
        {"dnorm": lambda dwgu, dwd: [up2.exchange([dwgu]), down2.exchange([dwd]), ple.exchange([dwpg, dwpe])]})

    def dmerge_epilogue(accs, extras, outs, ij):
        dmo = accs[0]
        g_ref, pa_ref, pb_ref = extras
        dg_ref, dpa_ref, dpb_ref = outs
        ga = g_ref[0].astype(F32)
        gb = g_ref[1].astype(F32)
        dg_ref[0] = (dmo * pa_ref[...].astype(F32) * ga * (1.0 - ga)).astype(BF16)
        dg_ref[1] = (dmo * pb_ref[...].astype(F32) * gb * (1.0 - gb)).astype(BF16)
        dpa_ref[...] = (dmo * ga).astype(BF16)
        dpb_ref[...] = (dmo * gb).astype(BF16)

    g_spec = pl.BlockSpec((2, tm, th), lambda i, j, k: (0, i, j))
    dgates, dpa, dpb = _mm(
        "d_merge", "nt", (nt, d // th, 1),
        [(dh2_b, row, wout, pl.BlockSpec((th, d), lambda i, j, k: (j, 0)))],
        [(gates, g_spec), (pa, tile_h), (pb, tile_h)],
        [(_sds((2, t, d), BF16), g_spec), (_sds((t, d), BF16), tile_h), (_sds((t, d), BF16), tile_h)],
        (tm, th), dmerge_epilogue, j_outer=True, chunked=True, comms=[down2.scatter()])
    dwout = dense_grad("d_w_out", merged, dh2_b, comms=[down2.forward(), ple.scatter()])

    yk_spec = pl.BlockSpec((tk, A_WIDTH), lambda i, j, k: (k, 0))
    dk_spec = pl.BlockSpec((tk, ps), lambda i, j, k: (k, j))
    dproj = (_sds((4, A_WIDTH, ps), BF16), proj_spec)
    dwpa, dwpb = _mm(
        "d_w_proj", "tn", (1, 4, nk),
        [(ya, yk_spec, dpa, dk_spec), (yb, yk_spec, dpb, dk_spec)], [], [dproj, dproj], (A_WIDTH, ps), _cast_epilogue,
        comms=[ple.forward()])
    dproj_a = pl.BlockSpec((tm, ps), lambda i, j, k: (i, k))
    wproj_k = pl.BlockSpec((None, A_WIDTH, ps), lambda i, j, k: (k, 0, 0))
    dya, dyb = _mm(
        "d_attn_out", "nt", (nt, 1, 4),
        [(dpa, dproj_a, wpa, wproj_k), (dpb, dproj_a, wpb, wproj_k)], [],
        [(_sds((t, A_WIDTH), BF16), y_spec)] * 2, (tm, A_WIDTH), _cast_epilogue,
        comms=[proj.exchange([dwpa, dwpb, dwout])])

    dqa, dka, dva, dbias_a, _, dgqa, dgka = _attn_bwd(
        "attn_a_bwd", qkv, bias_a, sink_a, gqa, gka, ya, dya, lse_a, ATTN_A, n_batch, True,
        comms=[up2.scatter(), proj.scatter()])
    dqb, dkb, dvb, _, dsink_b, dgqb, dgkb = _attn_bwd(
        "attn_b_bwd", qkv, bias_b, sink_b, gqb, gkb, yb, dyb, lse_b, ATTN_B, n_batch, False,
        comms=[up2.forward(), proj.forward()])
    dqkv = jnp.concatenate([dqa, dka, dva, dqb, dkb, dvb], axis=1)

    (dwgate,) = _mm(
        "d_w_gate", "tn", (1, 4, nk),
        [(un, pl.BlockSpec((tk, d), lambda i, j, k: (k, 0)),
          dgates, pl.BlockSpec((None, tk, gs), lambda i, j, k: (j // 2, k, j % 2)))],
        [], [(_sds((4, d, gs), BF16), pl.BlockSpec((None, d, gs), lambda i, j, k: (j, 0, 0)))], (d, gs), _cast_epilogue)
    (dwin,) = _mm(
        "d_w_in", "tn", (1, 2, nk),
        [(un, pl.BlockSpec((tk, d), lambda i, j, k: (k, 0)), dqkv, pl.BlockSpec((tk, tin), lambda i, j, k: (k, j)))],
        [], [(_sds((d, ncols), BF16), pl.BlockSpec((d, tin), lambda i, j, k: (0, j)))], (d, tin), _cast_epilogue)

    mixer = GradPipe(["w_in", "w_gate"])
    extras, outs = _rms_bwd_io(h1, small["mix_norm"], dh2, tmn)
    dh1, dh1_b, d_mix_norm = _mm(
        "d_mix_norm", "nt", (t // tmn, 1, 6),
        [(dgates, pl.BlockSpec((None, tmn, gs), lambda i, j, k: (jnp.minimum(k, 3) // 2, i, jnp.minimum(k, 3) % 2)),
          wgate, pl.BlockSpec((None, d, gs), lambda i, j, k: (jnp.minimum(k, 3), 0, 0))),
         (dqkv, pl.BlockSpec((tmn, tin), lambda i, j, k: (i, jnp.maximum(k - 4, 0))),
          w_in, pl.BlockSpec((d, tin), lambda i, j, k: (0, jnp.maximum(k - 4, 0))))],
        extras, outs, (tmn, d), _rms_bwd_epilogue, steps=[4, 2],
        comms=[mixer.exchange([dwin, dwgate])])

    up1 = GradPipe(["ffn1_w_gu"])
    down1 = GradPipe(["ffn1_w_down"])
    dx, _, d_ffn1_norm, _, _ = _ffn_bwd(
        "ffn1", dh1, dh1_b, xt, small["ffn1_norm"], w["ffn1_w_gu"], w["ffn1_w_down"], ffn1_saved,
        {"dwgu": lambda: [mixer.scatter()],
         "dwd": lambda dwgu: [mixer.forward(), up1.exchange([dwgu])],
         "dnorm": lambda dwgu, dwd: [up1.scatter(), down1.exchange([dwd])]})
    _run_comms("grad_tail_scatter", [up1.forward(), down1.scatter()])
    _run_comms("grad_tail_forward", [down1.forward()])
    terms = {}
    for pipe in (up2, down2, ple, proj, mixer, up1, down1):
        terms.update(pipe.terms())

    def fold(v):
        return v[0, :HEAD_DIM] + v[0, HEAD_DIM:]

    small_grads = {"ffn1_norm": d_ffn1_norm, "mix_norm": d_mix_norm, "ffn2_norm": d_ffn2_norm,
                   "ple_norm": d_ple_norm, "a_q_norm": fold(dgqa), "a_k_norm": fold(dgka),
                   "b_q_norm": fold(dgqb), "b_k_norm": fold(dgkb), "a_rel_bias": _rel_bias_grad(_unpair_bias(dbias_a)),
                   "b_sinks": jnp.sum(dsink_b, axis=1)}
    return loss, dx, terms, small_grads


def kernel(x, p, ffn1_norm, ffn1_w_gu, ffn1_w_down, mix_norm, w_in, a_q_norm, a_k_norm, a_rel_bias, b_q_norm, b_k_norm, b_sinks, w_gate, w_proj_a, w_proj_b, w_out, ffn2_norm, ffn2_w_gu, ffn2_w_down, ple_norm, w_ple_gate, w_ple_proj, loss_target, m_ffn1_norm, m_ffn1_w_gu, m_ffn1_w_down, m_mix_norm, m_w_in, m_a_q_norm, m_a_k_norm, m_a_rel_bias, m_b_q_norm, m_b_k_norm, m_b_sinks, m_w_gate, m_w_proj_a, m_w_proj_b, m_w_out, m_ffn2_norm, m_ffn2_w_gu, m_ffn2_w_down, m_ple_norm, m_w_ple_gate, m_w_ple_proj, v_ffn1_norm, v_ffn1_w_gu, v_ffn1_w_down, v_mix_norm, v_w_in, v_a_q_norm, v_a_k_norm, v_a_rel_bias, v_b_q_norm, v_b_k_norm, v_b_sinks, v_w_gate, v_w_proj_a, v_w_proj_b, v_w_out, v_ffn2_norm, v_ffn2_w_gu, v_ffn2_w_down, v_ple_norm, v_w_ple_gate, v_w_ple_proj):
    given = dict(locals())
    n_batch, s, d = x.shape
    t = n_batch * s
    xt = x.reshape(t, d)
    pt = p.reshape(t, p.shape[-1])
    tgt = loss_target.reshape(t, d)

    chip = (2 * lax.axis_index("x") + lax.axis_index("y")).astype(jnp.int32).reshape(1)
    bufs = {name: _cast_into_slot("cast_" + name, given[name][0], chip) for name in BIG}
    small = {name: given[name] for name in SMALL}
    place = jnp.concatenate([lax.axis_index("c").astype(jnp.int32).reshape(1), chip])
    loss, dx, terms, small_grads = _local_step(xt, pt, tgt, n_batch, bufs, small, place)

    grads, deltas, new_m, new_v = {}, {}, {}, {}
    for name in BIG:
        gw, dl, nm, nv = _adamw_terms("adamw_" + name, terms[name], given[name][0], given["m_" + name][0],
                                      given["v_" + name][0])
        grads[name], deltas[name], new_m[name], new_v[name] = gw[None], dl[None], nm[None], nv[None]

    small_shapes = [given[name].shape for name in SMALL] + [()]
    g_pack = _all_reduce_small(_pack([small_grads[name] for name in SMALL] + [loss]))
    zero = jnp.zeros((), F32)
    w_pack = _pack([given[name] for name in SMALL] + [zero])
    m_pack = _pack([given["m_" + name] for name in SMALL] + [zero])
    v_pack = _pack([given["v_" + name] for name in SMALL] + [zero])
    d_pack, nm_pack, nv_pack = _ew("adamw_small", lambda wv, gv, mv, vv: _adamw_math(wv, gv, mv, vv),
                                   [w_pack, g_pack, m_pack, v_pack], [F32] * 3)
    g_small = _unpack(g_pack, small_shapes)
    loss_total = g_small[-1]
    for name, gv, dv, mv, vv in zip(SMALL, g_small, _unpack(d_pack, small_shapes), _unpack(nm_pack, small_shapes),
                                    _unpack(nv_pack, small_shapes)):
        grads[name], deltas[name], new_m[name], new_v[name] = gv, dv, mv, vv

    return (loss_total, dx.reshape(x.shape), *[grads[n] for n in WEIGHTS], *[deltas[n] for n in WEIGHTS],
            *[new_m[n] for n in WEIGHTS], *[new_v[n] for n in WEIGHTS])
```

```python
import functools

import numpy as np
import jax
import jax.numpy as jnp
from jax import lax
from jax.experimental import pallas as pl
from jax.experimental.pallas import tpu as pltpu

F32 = jnp.float32
BF16 = jnp.bfloat16

CHUNK = 64
HEAD_DIM = 64
A_PREV_CHUNKS = 8
A_MAX_REL = 128
N_HEADS = 8
B_KV_HEADS = 2
B_PREV_CHUNKS = 2
A_WIDTH = N_HEADS * HEAD_DIM
B_KV_WIDTH = B_KV_HEADS * HEAD_DIM
EPS = 1e-6
NEG_INF = -1e30
ATTN_SCALE = HEAD_DIM ** -0.5
Q_BLOCK = 128
PAIR = 2 * HEAD_DIM

ADAM_LR = 0.001
ADAM_B1 = 0.9
ADAM_B2 = 0.999
ADAM_EPS = 1e-08
ADAM_WD = 0.01
ADAM_STEP = 10

N_CHIPS = 4
N_DEV = 8
VMEM_LIMIT_V7X = 56 * 1024 * 1024
ROW_TILE = 1024
MESH = pl.DeviceIdType.MESH
ANY = pl.BlockSpec(memory_space=pl.ANY)

_DN = {
    "nn": (((1,), (0,)), ((), ())),
    "nt": (((1,), (1,)), ((), ())),
    "tn": (((0,), (0,)), ((), ())),
}


def _pick(n, target, mult=128):
    best = None
    for d in range(mult, min(n, target) + 1, mult):
        if n % d == 0:
            best = d
    return n if best is None else best


def _dot(a, b, mode):
    return lax.dot_general(a.astype(BF16), b.astype(BF16), _DN[mode], preferred_element_type=F32)


def _params():
    return pltpu.CompilerParams(vmem_limit_bytes=VMEM_LIMIT_V7X)


class _Comm:
    def __init__(self, ins, outs, aliases, sems, start, finish):
        self.ins, self.outs, self.aliases, self.sems = list(ins), list(outs), dict(aliases), list(sems)
        self.start, self.finish = start, finish
        self.results = None


class _CommPlumbing:
    def __init__(self, comms, n_in, n_out, n_scratch):
        self.comms = list(comms)
        self.n_in, self.n_out, self.n_scratch = n_in, n_out, n_scratch
        self.args = [a for cm in self.comms for a in cm.ins]
        self.out_shape = [o for cm in self.comms for o in cm.outs]
        self.scratch = [s for cm in self.comms for s in cm.sems]
        self.aliases = {}
        i0, o0 = n_in, n_out
        for cm in self.comms:
            for a, b in cm.aliases.items():
                self.aliases[i0 + a] = o0 + b
            i0 += len(cm.ins)
            o0 += len(cm.outs)

    def _parts(self, in_refs, out_refs, scratch_refs):
        parts = []
        i0, o0, s0 = self.n_in, self.n_out, self.n_scratch
        for cm in self.comms:
            parts.append((in_refs[i0:i0 + len(cm.ins)], out_refs[o0:o0 + len(cm.outs)],
                          scratch_refs[s0:s0 + len(cm.sems)]))
            i0 += len(cm.ins)
            o0 += len(cm.outs)
            s0 += len(cm.sems)
        return parts

    def start_at(self, in_refs, out_refs, scratch_refs, first):
        if self.comms:
            parts = self._parts(in_refs, out_refs, scratch_refs)

            @pl.when(first)
            def _():
                for cm, part in zip(self.comms, parts):
                    cm.start(*part)

    def finish_at(self, in_refs, out_refs, scratch_refs, last):
        if self.comms:
            parts = self._parts(in_refs, out_refs, scratch_refs)

            @pl.when(last)
            def _():
                for cm, part in zip(self.comms, parts):
                    cm.finish(*part)

    def deliver(self, results):
        o0 = self.n_out
        for cm in self.comms:
            cm.results = list(results[o0:o0 + len(cm.outs)])
            o0 += len(cm.outs)
        return list(results[:self.n_out])


def _swap_ij(spec):
    index_map = spec.index_map
    return pl.BlockSpec(spec.block_shape, lambda j, i, k: index_map(i, j, k))


MXU_COLUMNS_V7X = 256


def _mm(name, mode, grid, pairs, extras, outs, acc_shape, epilogue, steps=None, comms=(), j_outer=False,
        chunked=False):
    ni, nj, nk = grid
    n_in = 2 * len(pairs) + len(extras)
    n_out = len(outs)
    tn = acc_shape[1]
    col_chunks = None
    if chunked:
        assert nk == 1 and steps is None and mode in ("nn", "nt")
        col_chunks = [(c0, min(MXU_COLUMNS_V7X, tn - c0)) for c0 in range(0, tn, MXU_COLUMNS_V7X)]
    n_acc = 0 if chunked else (len(pairs) if steps is None else 1)
    plumb = _CommPlumbing(comms, n_in, n_out, n_acc)
    n_all_in = n_in + len(plumb.args)
    n_all_out = n_out + len(plumb.out_shape)
    if j_outer:
        grid = (nj, ni, nk)
        pairs = [(a, _swap_ij(a_spec), b, _swap_ij(b_spec)) for a, a_spec, b, b_spec in pairs]
        extras = [(e, _swap_ij(e_spec)) for e, e_spec in extras]
        outs = [(o, _swap_ij(o_spec)) for o, o_spec in outs]

    def body(*refs):
        in_refs = refs[:n_all_in]
        out_refs = refs[n_all_in:n_all_in + n_all_out]
        scratch = refs[n_all_in + n_all_out:]
        accs = scratch[:n_acc]
        i = pl.program_id(1 if j_outer else 0)
        j = pl.program_id(0 if j_outer else 1)
        k = pl.program_id(2)
        plumb.start_at(in_refs, out_refs, scratch, (i == 0) & (j == 0) & (k == 0))

        def contrib(p, acc):
            acc[...] += _dot(in_refs[2 * p][...], in_refs[2 * p + 1][...], mode)

        if col_chunks:
            def cols(ref, c0, cs):
                if ref.shape[-1] != tn:
                    return ref
                return ref.at[(slice(None),) * (len(ref.shape) - 1) + (pl.ds(c0, cs),)]

            lhs = [in_refs[2 * p][...] for p in range(len(pairs))]
            for ci, (c0, cs) in enumerate(col_chunks):
                vals = []
                for p in range(len(pairs)):
                    b_ref = in_refs[2 * p + 1]
                    rhs = b_ref[:, c0:c0 + cs] if mode == "nn" else b_ref[c0:c0 + cs, :]
                    vals.append(_dot(lhs[p], rhs, mode))
                epilogue(vals, [cols(r, c0, cs) for r in in_refs[2 * len(pairs):n_in]],
                         [cols(r, c0, cs) for r in out_refs[:n_out]], (i, j * len(col_chunks) + ci))
        else:
            @pl.when(k == 0)
            def _():
                for acc in accs:
                    acc[...] = jnp.zeros(acc.shape, F32)

            if steps is None:
                for p in range(len(pairs)):
                    contrib(p, accs[p])
            else:
                lo = 0
                for p, n in enumerate(steps):
                    pl.when((k >= lo) & (k < lo + n))(functools.partial(contrib, p, accs[0]))
                    lo += n

            @pl.when(k == nk - 1)
            def _():
                epilogue([acc[...] for acc in accs], in_refs[2 * len(pairs):n_in], out_refs[:n_out], (i, j))

        plumb.finish_at(in_refs, out_refs, scratch, (i == ni - 1) & (j == nj - 1) & (k == nk - 1))

    args, in_specs = [], []
    for a, a_spec, b, b_spec in pairs:
        args += [a, b]
        in_specs += [a_spec, b_spec]
    for e, e_spec in extras:
        args.append(e)
        in_specs.append(e_spec)
    res = pl.pallas_call(
        body,
        name=name,
        grid=grid,
        in_specs=in_specs + [ANY] * len(plumb.args),
        out_specs=[s for _, s in outs] + [ANY] * len(plumb.out_shape),
        out_shape=[o for o, _ in outs] + plumb.out_shape,
        scratch_shapes=[pltpu.VMEM(acc_shape, F32) for _ in range(n_acc)] + plumb.scratch,
        input_output_aliases=plumb.aliases,
        compiler_params=_params(),
    )(*args, *plumb.args)
    return plumb.deliver(res)


def _sds(shape, dtype):
    return jax.ShapeDtypeStruct(shape, dtype)


def _accumulate(ref, value, first):
    @pl.when(first)
    def _():
        ref[...] = value

    @pl.when(jnp.logical_not(first))
    def _():
        ref[...] += value


def _rms_fwd(name, x, gain, comms=()):
    t, d = x.shape
    tm = _pick(t, ROW_TILE, 8)
    steps = t // tm
    plumb = _CommPlumbing(comms, 2, 1, 0)
    n_all_in = 2 + len(plumb.args)
    n_all_out = 1 + len(plumb.out_shape)

    def body(*refs):
        x_ref, g_ref = refs[:2]
        y_ref = refs[n_all_in]
        comm_refs = (refs[:n_all_in], refs[n_all_in:n_all_in + n_all_out], refs[n_all_in + n_all_out:])
        i = pl.program_id(0)
        plumb.start_at(*comm_refs, i == 0)
        xv = x_ref[...]
        rstd = lax.rsqrt(jnp.mean(xv * xv, axis=-1, keepdims=True) + EPS)
        y_ref[...] = (xv * rstd * g_ref[...]).astype(BF16)
        plumb.finish_at(*comm_refs, i == steps - 1)

    res = pl.pallas_call(
        body, name=name, grid=(steps,),
        in_specs=[pl.BlockSpec((tm, d), lambda i: (i, 0)), pl.BlockSpec((1, d), lambda i: (0, 0))]
        + [ANY] * len(plumb.args),
        out_specs=[pl.BlockSpec((tm, d), lambda i: (i, 0))] + [ANY] * len(plumb.out_shape),
        out_shape=[_sds((t, d), BF16)] + plumb.out_shape,
        scratch_shapes=plumb.scratch,
        input_output_aliases=plumb.aliases,
        compiler_params=_params(),
    )(x, gain, *plumb.args)
    return plumb.deliver(res)[0]


def _rms_bwd_epilogue(accs, extras, outs, ij):
    x_ref, g_ref, r_ref = extras
    dh_ref, dhb_ref, dg_ref = outs
    dn = accs[0]
    xv = x_ref[...]
    rstd = lax.rsqrt(jnp.mean(xv * xv, axis=-1, keepdims=True) + EPS)
    xhat = xv * rstd
    gd = dn * g_ref[...]
    dx = rstd * (gd - xhat * jnp.mean(gd * xhat, axis=-1, keepdims=True))
    dh = r_ref[...] + dx
    dh_ref[...] = dh
    dhb_ref[...] = dh.astype(BF16)
    _accumulate(dg_ref, jnp.sum(dn * xhat, axis=0, keepdims=True), ij[0] == 0)


def _rms_bwd_io(x, gain, dres, tm):
    t, d = x.shape
    row = pl.BlockSpec((tm, d), lambda i, j, k: (i, 0))
    extras = [(x, row), (gain, pl.BlockSpec((1, d), lambda i, j, k: (0, 0))), (dres, row)]
    outs = [(_sds((t, d), F32), row), (_sds((t, d), BF16), row),
            (_sds((1, d), F32), pl.BlockSpec((1, d), lambda i, j, k: (0, 0)))]
    return extras, outs


def _ffn_fwd(tag, h, gain, wgu, wd, hooks):
    t, d = h.shape
    fs = wgu.shape[2]
    f = 2 * fs
    tm = _pick(t, ROW_TILE, 8)
    n = _rms_fwd(tag + "_norm", h, gain)

    def up_epilogue(accs, extras, outs, ij):
        g, u = accs
        gu_ref, a_ref = outs
        gu_ref[0] = g.astype(BF16)
        gu_ref[1] = u.astype(BF16)
        a_ref[...] = (g * jax.nn.sigmoid(g) * u).astype(BF16)

    a_spec = pl.BlockSpec((tm, d), lambda i, j, k: (i, 0))
    gu, a = _mm(
        tag + "_up", "nn", (t // tm, 2, 1),
        [(n, a_spec, wgu, pl.BlockSpec((None, d, fs), lambda i, j, k: (j, 0, 0))),
         (n, a_spec, wgu, pl.BlockSpec((None, d, fs), lambda i, j, k: (j + 2, 0, 0)))],
        [],
        [(_sds((2, t, f), BF16), pl.BlockSpec((2, tm, fs), lambda i, j, k: (0, i, j))),
         (_sds((t, f), BF16), pl.BlockSpec((tm, fs), lambda i, j, k: (i, j)))],
        (tm, fs), up_epilogue, comms=hooks.get("up", lambda: ())(), j_outer=True, chunked=True)

    def down_epilogue(accs, extras, outs, ij):
        outs[0][...] = extras[0][...] + 0.5 * accs[0]

    if callable(wd):
        wd = wd()

    row = pl.BlockSpec((tm, d), lambda i, j, k: (i, 0))
    (h_new,) = _mm(
        tag + "_down", "nn", (t // tm, 1, 1),
        [(a, pl.BlockSpec((tm, f), lambda i, j, k: (i, 0)), wd, pl.BlockSpec((f, d), lambda i, j, k: (0, 0)))],
        [(h, row)], [(_sds((t, d), F32), row)], (tm, d), down_epilogue, comms=hooks.get("down", lambda: ())())
    return h_new, (n, gu, a)


def _ffn_bwd(tag, dh, dh_b, h, gain, wgu, wd, saved, hooks):
    n, gu, a = saved
    t, d = h.shape
    fs = wgu.shape[2]
    f = 2 * fs
    tm = _pick(t, ROW_TILE, 8)
    tk = _pick(t, ROW_TILE, 8)

    def dact_epilogue(accs, extras, outs, ij):
        da = 0.5 * accs[0]
        g = extras[0][0].astype(F32)
        u = extras[0][1].astype(F32)
        sg = jax.nn.sigmoid(g)
        outs[0][0] = (da * u * sg * (1.0 + g * (1.0 - sg))).astype(BF16)
        outs[0][1] = (da * g * sg).astype(BF16)

    gu_spec = pl.BlockSpec((2, tm, fs), lambda i, j, k: (0, i, j))
    (dgu,) = _mm(
        tag + "_dact", "nt", (t // tm, 2, 1),
        [(dh_b, pl.BlockSpec((tm, d), lambda i, j, k: (i, 0)), wd, pl.BlockSpec((fs, d), lambda i, j, k: (j, 0)))],
        [(gu, gu_spec)], [(_sds((2, t, f), BF16), gu_spec)], (tm, fs), dact_epilogue, j_outer=True, chunked=True,
        comms=hooks.get("dact", lambda: ())())

    def cast_epilogue(accs, extras, outs, ij):
        outs[0][...] = accs[0].astype(BF16)

    (dwgu,) = _mm(
        tag + "_dwgu", "tn", (1, 4, t // tk),
        [(n, pl.BlockSpec((tk, d), lambda i, j, k: (k, 0)),
          dgu, pl.BlockSpec((None, tk, fs), lambda i, j, k: (j // 2, k, j % 2)))],
        [], [(_sds((4, d, fs), BF16), pl.BlockSpec((None, d, fs), lambda i, j, k: (j, 0, 0)))], (d, fs), cast_epilogue,
        comms=hooks.get("dwgu", lambda: ())())

    def half_epilogue(accs, extras, outs, ij):
        outs[0][...] = (0.5 * accs[0]).astype(BF16)

    (dwd,) = _mm(
        tag + "_dwd", "tn", (2, 1, t // tk),
        [(a, pl.BlockSpec((tk, fs), lambda i, j, k: (k, i)), dh_b, pl.BlockSpec((tk, d), lambda i, j, k: (k, 0)))],
        [], [(_sds((f, d), BF16), pl.BlockSpec((fs, d), lambda i, j, k: (i, 0)))], (fs, d), half_epilogue,
        comms=hooks.get("dwd", lambda g: ())(dwgu))

    tmn = _pick(t, ROW_TILE, 8)
    extras, outs = _rms_bwd_io(h, gain, dh, tmn)
    dh_in, dh_in_b, dgain = _mm(
        tag + "_dnorm", "nt", (t // tmn, 1, 4),
        [(dgu, pl.BlockSpec((None, tmn, fs), lambda i, j, k: (k // 2, i, k % 2)),
          wgu, pl.BlockSpec((None, d, fs), lambda i, j, k: (k, 0, 0)))],
        extras, outs, (tmn, d), _rms_bwd_epilogue, comms=hooks.get("dnorm", lambda g, w: ())(dwgu, dwd))
    return dh_in, dh_in_b, dgain, dwgu, dwd


def _lane_lo(shape):
    return lax.broadcasted_iota(jnp.int32, shape, 1) < HEAD_DIM


def _pair_norm(xv, gain):
    lo = _lane_lo(xv.shape)
    x2 = xv * xv
    ms_lo = jnp.sum(jnp.where(lo, x2, 0.0), axis=-1, keepdims=True) * (1.0 / HEAD_DIM)
    ms_hi = jnp.sum(jnp.where(lo, 0.0, x2), axis=-1, keepdims=True) * (1.0 / HEAD_DIM)
    rstd = jnp.where(lo, lax.rsqrt(ms_lo + EPS), lax.rsqrt(ms_hi + EPS))
    xhat = xv * rstd
    return xhat * gain, xhat, rstd


def _pair_norm_bwd(dn, xhat, rstd, gain):
    lo = _lane_lo(dn.shape)
    gd = dn * gain
    t = gd * xhat
    m_lo = jnp.sum(jnp.where(lo, t, 0.0), axis=-1, keepdims=True) * (1.0 / HEAD_DIM)
    m_hi = jnp.sum(jnp.where(lo, 0.0, t), axis=-1, keepdims=True) * (1.0 / HEAD_DIM)
    dx = rstd * (gd - xhat * jnp.where(lo, m_lo, m_hi))
    return dx, jnp.sum(dn * xhat, axis=0, keepdims=True)


def _half(xv, hi):
    lo = _lane_lo(xv.shape)
    return jnp.where(lo, 0, xv) if hi else jnp.where(lo, xv, 0)


def _attn_window(i, prev):
    q0 = i * Q_BLOCK
    start = jnp.maximum(q0 - prev, 0)
    off = start - (q0 - prev)
    return pl.multiple_of(start, Q_BLOCK), pl.multiple_of(off, Q_BLOCK)


def _attn_specs(cfg, s, nq):
    kw = cfg["kw"]
    q_spec = pl.BlockSpec((Q_BLOCK, A_WIDTH), lambda b, i: (b * nq + i, cfg["qblk"]))
    k_spec = pl.BlockSpec((s, kw), lambda b, i: (b, cfg["kblk"]))
    v_spec = pl.BlockSpec((s, kw), lambda b, i: (b, cfg["vblk"]))
    return q_spec, k_spec, v_spec


def _const_spec(shape):
    return pl.BlockSpec(shape, lambda b, i: (0,) * len(shape))


KEY_CHUNK = 128


def _pair_bias(bias_t):
    wext = bias_t.shape[1]
    return jnp.transpose(bias_t.reshape(N_HEADS // 2, 2, wext, Q_BLOCK), (0, 2, 1, 3)).reshape(
        N_HEADS // 2, wext, 2 * Q_BLOCK)


def _unpair_bias(db2):
    wext = db2.shape[1]
    return jnp.transpose(db2.reshape(N_HEADS // 2, wext, 2, Q_BLOCK), (0, 2, 1, 3)).reshape(N_HEADS, wext, Q_BLOCK)


def _pair_rows(rows):
    two = rows.reshape(N_HEADS // 2, 2 * rows.shape[1])
    return jnp.broadcast_to(two[:, None, :], (N_HEADS // 2, 8, two.shape[1]))


def _sub_lo(shape):
    return lax.broadcasted_iota(jnp.int32, shape, 0) < HEAD_DIM


def _by_half(lo_row, hi_row, rows):
    return jnp.where(_sub_lo((rows, lo_row.shape[1])), lo_row, hi_row)


def _stack_pair(xn, jq, group):
    parts = []
    for hq in range(2):
        hk = ((2 * jq + hq) // group) % 2
        xm = _half(xn, hq)
        if hq != hk:
            xm = pltpu.roll(xm, HEAD_DIM, 1)
        parts.append(xm)
    return jnp.concatenate(parts, axis=0).astype(BF16)


def _place_transposed(blk, dst_ref, c, heads, group):
    bt = blk.T
    lo = _sub_lo(bt.shape)
    for h in heads:
        src_hi = ((h // group) % 2) == 1
        part = jnp.where(lo, 0.0, bt) if src_hi else jnp.where(lo, bt, 0.0)
        if src_hi != (h % 2 == 1):
            part = pltpu.roll(part, HEAD_DIM, 0)
        dst_ref[h, c] = part.astype(BF16)


def _attn_fwd(name, qkv, bias2, sink2, gq, gk, cfg, n_batch, comms=()):
    t = qkv.shape[0]
    s = t // n_batch
    nq = s // Q_BLOCK
    nkc = s // KEY_CHUNK
    prev, group, kw = cfg["prev"], cfg["group"], cfg["kw"]
    w = prev + Q_BLOCK
    n_chunks = w // KEY_CHUNK
    wext = bias2.shape[1]
    plumb = _CommPlumbing(comms, 7, 2, 4)
    n_all_in = 7 + len(plumb.args)
    n_all_out = 2 + len(plumb.out_shape)

    def body(*refs):
        q_ref, k_ref, v_ref, bias_ref, sink_ref, gq_ref, gk_ref = refs[:7]
        y_ref, lse_ref = refs[n_all_in:n_all_in + 2]
        kn_ref, vt_ref, s_ref, pst_ref = refs[n_all_in + n_all_out:n_all_in + n_all_out + 4]
        i = pl.program_id(1)
        comm_refs = (refs[:n_all_in], refs[n_all_in:n_all_in + n_all_out], refs[n_all_in + n_all_out:])
        plumb.start_at(*comm_refs, (pl.program_id(0) == 0) & (i == 0))

        @pl.when(i == 0)
        def _():
            for jk in range(kw // PAIR):
                cols = pl.ds(jk * PAIR, PAIR)
                heads = [h for h in range(N_HEADS) if (h // group) // 2 == jk]
                kn, _, _ = _pair_norm(k_ref[:, cols].astype(F32), gk_ref[...])
                kn_ref[:, cols] = kn.astype(BF16)
                for c in range(nkc):
                    _place_transposed(v_ref[pl.ds(c * KEY_CHUNK, KEY_CHUNK), cols].astype(F32), vt_ref, c, heads, group)

        start, off = _attn_window(i, prev)
        c0 = start // KEY_CHUNK
        sub8 = lax.broadcasted_iota(jnp.int32, (N_HEADS, Q_BLOCK), 0)
        lse = jnp.zeros((N_HEADS, Q_BLOCK), F32)
        for jq in range(N_HEADS // 2):
            kcols = pl.ds((((2 * jq) // group) // 2) * PAIR, PAIR)
            qn, _, _ = _pair_norm(q_ref[:, pl.ds(jq * PAIR, PAIR)].astype(F32), gq_ref[...])
            qs = _stack_pair(qn * ATTN_SCALE, jq, group)
            s_ref[...] = _dot(kn_ref[pl.ds(start, w), kcols], qs, "nt")
            m = sink_ref[jq, 0:1, :]
            for c in range(n_chunks):
                r = pl.ds(c * KEY_CHUNK, KEY_CHUNK)
                s2 = s_ref[r, :] + bias_ref[jq, pl.ds(off + c * KEY_CHUNK, KEY_CHUNK), :]
                s_ref[r, :] = s2
                m = jnp.maximum(m, jnp.max(s2, axis=0, keepdims=True))
            l = jnp.exp(sink_ref[jq, 0:1, :] - m)
            for c in range(n_chunks):
                p = jnp.exp(s_ref[pl.ds(c * KEY_CHUNK, KEY_CHUNK), :] - m)
                l = l + jnp.sum(p, axis=0, keepdims=True)
                pst_ref[pl.ds(2 * c * KEY_CHUNK, KEY_CHUNK), :] = p[:, :Q_BLOCK].astype(BF16)
                pst_ref[pl.ds((2 * c + 1) * KEY_CHUNK, KEY_CHUNK), :] = p[:, Q_BLOCK:].astype(BF16)
            vl = jnp.concatenate([vt_ref[2 * jq + hq, c0 + c] for c in range(n_chunks) for hq in range(2)], axis=1)
            ot = _dot(vl, pst_ref[...], "nn")
            inv = 1.0 / l
            ot = ot * _by_half(inv[:, :Q_BLOCK], inv[:, Q_BLOCK:], PAIR)
            y_ref[:, pl.ds(jq * PAIR, PAIR)] = ot.T.astype(BF16)
            lse2 = m + jnp.log(l)
            lse = jnp.where(sub8 == 2 * jq, lse2[:, :Q_BLOCK], lse)
            lse = jnp.where(sub8 == 2 * jq + 1, lse2[:, Q_BLOCK:], lse)
        lse_ref[...] = lse
        plumb.finish_at(*comm_refs, (pl.program_id(0) == n_batch - 1) & (i == nq - 1))

    q_spec, k_spec, v_spec = _attn_specs(cfg, s, nq)
    res = pl.pallas_call(
        body, name=name, grid=(n_batch, nq),
        in_specs=[q_spec, k_spec, v_spec, _const_spec((N_HEADS // 2, wext, 2 * Q_BLOCK)),
                  _const_spec((N_HEADS // 2, 8, 2 * Q_BLOCK)), _const_spec((1, PAIR)), _const_spec((1, PAIR))]
        + [ANY] * len(plumb.args),
        out_specs=[pl.BlockSpec((Q_BLOCK, A_WIDTH), lambda b, i: (b * nq + i, 0)),
                   pl.BlockSpec((None, N_HEADS, Q_BLOCK), lambda b, i: (b * nq + i, 0, 0))]
        + [ANY] * len(plumb.out_shape),
        out_shape=[_sds((t, A_WIDTH), BF16), _sds((t // Q_BLOCK, N_HEADS, Q_BLOCK), F32)] + plumb.out_shape,
        scratch_shapes=[pltpu.VMEM((s, kw), BF16), pltpu.VMEM((N_HEADS, nkc, PAIR, KEY_CHUNK), BF16),
                        pltpu.VMEM((w, 2 * Q_BLOCK), F32), pltpu.VMEM((2 * w, Q_BLOCK), BF16)] + plumb.scratch,
        input_output_aliases=plumb.aliases,
        compiler_params=_params(),
    )(qkv, qkv, qkv, bias2, sink2, gq, gk, *plumb.args)
    return plumb.deliver(res)


def _attn_bwd(name, qkv, bias2, sink2, gq, gk, y, dy, lse, cfg, n_batch, want_dbias, comms=()):
    t = qkv.shape[0]
    s = t // n_batch
    nq = s // Q_BLOCK
    nkc = s // KEY_CHUNK
    prev, group, kw = cfg["prev"], cfg["group"], cfg["kw"]
    w = prev + Q_BLOCK
    n_chunks = w // KEY_CHUNK
    wext = bias2.shape[1]
    plumb = _CommPlumbing(comms, 10, 7, 9)
    n_all_in = 10 + len(plumb.args)
    n_all_out = 7 + len(plumb.out_shape)

    def body(*refs):
        q_ref, k_ref, v_ref, bias_ref, sink_ref, gq_ref, gk_ref, y_ref, dy_ref, lse_ref = refs[:10]
        dq_ref, dk_ref, dv_ref, db_ref, dsink_ref, dgq_ref, dgk_ref = refs[n_all_in:n_all_in + 7]
        kn_ref, knt_ref, dkn_ref, dvs_ref, s_ref, dp_ref, pb_ref, dsb_ref, dst_ref = \
            refs[n_all_in + n_all_out:n_all_in + n_all_out + 9]
        b = pl.program_id(0)
        i = pl.program_id(1)
        first = (b == 0) & (i == 0)
        comm_refs = (refs[:n_all_in], refs[n_all_in:n_all_in + n_all_out], refs[n_all_in + n_all_out:])
        plumb.start_at(*comm_refs, first)

        @pl.when(i == 0)
        def _():
            for jk in range(kw // PAIR):
                cols = pl.ds(jk * PAIR, PAIR)
                heads = [h for h in range(N_HEADS) if (h // group) // 2 == jk]
                for c in range(nkc):
                    rows = pl.ds(c * KEY_CHUNK, KEY_CHUNK)
                    kn, _, _ = _pair_norm(k_ref[rows, cols].astype(F32), gk_ref[...])
                    kn_ref[rows, cols] = kn.astype(BF16)
                    _place_transposed(kn, knt_ref, c, heads, group)
            dkn_ref[...] = jnp.zeros(dkn_ref.shape, F32)
            dvs_ref[...] = jnp.zeros(dvs_ref.shape, F32)

        @pl.when(first)
        def _():
            db_ref[...] = jnp.zeros(db_ref.shape, F32)
            dsink_ref[...] = jnp.zeros(dsink_ref.shape, F32)
            dgq_ref[...] = jnp.zeros(dgq_ref.shape, F32)
            dgk_ref[...] = jnp.zeros(dgk_ref.shape, F32)

        start, off = _attn_window(i, prev)
        c0 = start // KEY_CHUNK
        for jq in range(N_HEADS // 2):
            cols = pl.ds(jq * PAIR, PAIR)
            kcols = pl.ds((((2 * jq) // group) // 2) * PAIR, PAIR)
            qn, q_hat, q_rstd = _pair_norm(q_ref[:, cols].astype(F32), gq_ref[...])
            qs = _stack_pair(qn * ATTN_SCALE, jq, group)
            do_pair = dy_ref[:, cols].astype(F32)
            dos = _stack_pair(do_pair, jq, group)
            prod_t = (do_pair * y_ref[:, cols].astype(F32)).T
            lo = _sub_lo(prod_t.shape)
            delta2 = jnp.concatenate([jnp.sum(jnp.where(lo, prod_t, 0.0), axis=0, keepdims=True),
                                      jnp.sum(jnp.where(lo, 0.0, prod_t), axis=0, keepdims=True)], axis=1)
            lse2 = jnp.concatenate([lse_ref[2 * jq:2 * jq + 1, :], lse_ref[2 * jq + 1:2 * jq + 2, :]], axis=1)
            dsk = -jnp.exp(sink_ref[jq, 0:1, :] - lse2) * delta2
            dsink_ref[2 * jq:2 * jq + 1, :] += dsk[:, :Q_BLOCK]
            dsink_ref[2 * jq + 1:2 * jq + 2, :] += dsk[:, Q_BLOCK:]
            rows_w = pl.ds(start, w)
            s_ref[...] = _dot(kn_ref[rows_w, kcols], qs, "nt")
            dp_ref[...] = _dot(v_ref[rows_w, kcols], dos, "nt")
            for c in range(n_chunks):
                r = pl.ds(c * KEY_CHUNK, KEY_CHUNK)
                brows = pl.ds(off + c * KEY_CHUNK, KEY_CHUNK)
                p = jnp.exp(s_ref[r, :] + bias_ref[jq, brows, :] - lse2)
                ds = p * (dp_ref[r, :] - delta2)
                if want_dbias:
                    db_ref[jq, brows, :] += ds
                ds_b = ds.astype(BF16)
                pb_ref[r, :] = p.astype(BF16)
                dsb_ref[r, :] = ds_b
                dst_ref[pl.ds(2 * c * KEY_CHUNK, KEY_CHUNK), :] = ds_b[:, :Q_BLOCK]
                dst_ref[pl.ds((2 * c + 1) * KEY_CHUNK, KEY_CHUNK), :] = ds_b[:, Q_BLOCK:]
            dkn_ref[rows_w, kcols] += _dot(dsb_ref[...], qs, "nn")
            dvs_ref[rows_w, kcols] += _dot(pb_ref[...], dos, "nn")
            kl = jnp.concatenate([knt_ref[2 * jq + hq, c0 + c] for c in range(n_chunks) for hq in range(2)], axis=1)
            dqt = _dot(kl, dst_ref[...], "nn")
            dq_raw, dg = _pair_norm_bwd(dqt.T * ATTN_SCALE, q_hat, q_rstd, gq_ref[...])
            dq_ref[:, cols] = dq_raw.astype(BF16)
            dgq_ref[...] += dg

        @pl.when(i == nq - 1)
        def _():
            for jk in range(kw // PAIR):
                kcols = pl.ds(jk * PAIR, PAIR)
                _, k_hat, k_rstd = _pair_norm(k_ref[:, kcols].astype(F32), gk_ref[...])
                dk_raw, dg = _pair_norm_bwd(dkn_ref[:, kcols], k_hat, k_rstd, gk_ref[...])
                dk_ref[:, kcols] = dk_raw.astype(BF16)
                dgk_ref[...] += dg
            dv_ref[...] = dvs_ref[...].astype(BF16)

        plumb.finish_at(*comm_refs, (b == n_batch - 1) & (i == nq - 1))

    q_spec, k_spec, v_spec = _attn_specs(cfg, s, nq)
    row = pl.BlockSpec((Q_BLOCK, A_WIDTH), lambda b, i: (b * nq + i, 0))
    kv_out = pl.BlockSpec((s, kw), lambda b, i: (b, 0))
    pair_bias = _const_spec((N_HEADS // 2, wext, 2 * Q_BLOCK))
    res = pl.pallas_call(
        body, name=name, grid=(n_batch, nq),
        in_specs=[q_spec, k_spec, v_spec, pair_bias, _const_spec((N_HEADS // 2, 8, 2 * Q_BLOCK)),
                  _const_spec((1, PAIR)), _const_spec((1, PAIR)), row, row,
                  pl.BlockSpec((None, N_HEADS, Q_BLOCK), lambda b, i: (b * nq + i, 0, 0))] + [ANY] * len(plumb.args),
        out_specs=[row, kv_out, kv_out, pair_bias, _const_spec((N_HEADS, 128)),
                   _const_spec((1, PAIR)), _const_spec((1, PAIR))] + [ANY] * len(plumb.out_shape),
        out_shape=[_sds((t, A_WIDTH), BF16), _sds((t, kw), BF16), _sds((t, kw), BF16),
                   _sds((N_HEADS // 2, wext, 2 * Q_BLOCK), F32), _sds((N_HEADS, 128), F32),
                   _sds((1, PAIR), F32), _sds((1, PAIR), F32)] + plumb.out_shape,
        scratch_shapes=[pltpu.VMEM((s, kw), BF16), pltpu.VMEM((N_HEADS, nkc, PAIR, KEY_CHUNK), BF16),
                        pltpu.VMEM((s, kw), F32), pltpu.VMEM((s, kw), F32),
                        pltpu.VMEM((w, 2 * Q_BLOCK), F32), pltpu.VMEM((w, 2 * Q_BLOCK), F32),
                        pltpu.VMEM((w, 2 * Q_BLOCK), BF16), pltpu.VMEM((w, 2 * Q_BLOCK), BF16),
                        pltpu.VMEM((2 * w, Q_BLOCK), BF16)] + plumb.scratch,
        input_output_aliases=plumb.aliases,
        compiler_params=_params(),
    )(qkv, qkv, qkv, bias2, sink2, gq, gk, y, dy, lse, *plumb.args)
    return plumb.deliver(res)


def _band_tables(prev_chunks):
    prev = prev_chunks * CHUNK
    wext = 2 * prev + Q_BLOCK
    jj = np.arange(wext)[:, None]
    ii = np.arange(Q_BLOCK)[None, :]
    dist = prev + ii - jj
    rel_chunk = (prev // CHUNK + ii // CHUNK) - jj // CHUNK
    allowed = (rel_chunk >= 0) & (rel_chunk <= prev_chunks)
    return dist, allowed


def _alibi_slopes():
    return np.array([2.0 ** (-8.0 * (h + 1) / N_HEADS) for h in range(N_HEADS)], dtype=np.float32)


def _diag_onehot(prev, wext):
    n_diag = wext + Q_BLOCK - 1
    idx = np.clip(prev + Q_BLOCK - 1 - np.arange(n_diag), -A_MAX_REL, A_MAX_REL) + A_MAX_REL
    onehot = np.zeros((n_diag, 2 * A_MAX_REL + 1), np.float32)
    onehot[np.arange(n_diag), idx] = 1.0
    return onehot


def _bias_a(rel_bias):
    prev = A_PREV_CHUNKS * CHUNK
    _, allowed = _band_tables(A_PREV_CHUNKS)
    wext = allowed.shape[0]
    n_diag = wext + Q_BLOCK - 1
    seq = jnp.dot(rel_bias, jnp.asarray(_diag_onehot(prev, wext).T), precision=lax.Precision.HIGHEST)
    seq = jnp.pad(seq, ((0, 0), (0, 1)))
    rows = jnp.broadcast_to(seq[:, None, :], (N_HEADS, Q_BLOCK, n_diag + 1)).reshape(N_HEADS, -1)
    skew = rows[:, :Q_BLOCK * n_diag].reshape(N_HEADS, Q_BLOCK, n_diag)
    tile = jnp.transpose(skew[:, :, Q_BLOCK - 1:Q_BLOCK - 1 + wext], (0, 2, 1))
    return jnp.where(jnp.asarray(allowed)[None], tile, NEG_INF)


def _bias_b():
    dist, allowed = _band_tables(B_PREV_CHUNKS)
    bias = -_alibi_slopes()[:, None, None] * np.abs(dist).astype(np.float32)[None]
    return jnp.asarray(np.where(allowed[None], bias, np.float32(NEG_INF)).astype(np.float32))


def _rel_bias_grad(db_t):
    prev = A_PREV_CHUNKS * CHUNK
    wext = db_t.shape[1]
    n_diag = wext + Q_BLOCK - 1
    wp = n_diag + Q_BLOCK - 1
    xp = jnp.pad(jnp.transpose(db_t, (0, 2, 1)), ((0, 0), (0, 0), (Q_BLOCK - 1, Q_BLOCK - 1)))
    flat = jnp.pad(xp.reshape(N_HEADS, Q_BLOCK * wp), ((0, 0), (0, Q_BLOCK)))
    skew = flat.reshape(N_HEADS, Q_BLOCK, wp + 1)[:, :, :n_diag]
    diag = jnp.sum(skew, axis=1)
    return jnp.dot(diag, jnp.asarray(_diag_onehot(prev, wext)), precision=lax.Precision.HIGHEST)


def _ew(name, fn, ins, out_dtypes):
    r, c = ins[0].shape
    rb = _pick(r, max(16, (1 << 19) // c), 16)
    spec = pl.BlockSpec((rb, c), lambda i: (i, 0))

    def body(*refs):
        vals = fn(*[ref[...] for ref in refs[:len(ins)]])
        for ref, val in zip(refs[len(ins):], vals):
            ref[...] = val.astype(ref.dtype)

    return pl.pallas_call(
        body, name=name, grid=(r // rb,), in_specs=[spec] * len(ins), out_specs=[spec] * len(out_dtypes),
        out_shape=[_sds((r, c), dt) for dt in out_dtypes], compiler_params=_params(),
    )(*ins)


def _cast_into_slot(name, w, chip):
    r, c = w.shape
    rb = _pick(r, max(16, (1 << 19) // c), 16)

    def body(chip_ref, w_ref, o_ref):
        o_ref[...] = w_ref[...].astype(BF16)

    return pl.pallas_call(
        body, name=name,
        grid_spec=pltpu.PrefetchScalarGridSpec(
            num_scalar_prefetch=1, grid=(r // rb,),
            in_specs=[pl.BlockSpec((rb, c), lambda i, chip_ref: (i, 0))],
            out_specs=pl.BlockSpec((None, rb, c), lambda i, chip_ref: (chip_ref[0], i, 0))),
        out_shape=_sds((N_CHIPS, r, c), BF16), compiler_params=_params(),
    )(chip, w)


def _adamw_math(w, g, m, v):
    m = ADAM_B1 * m + (1.0 - ADAM_B1) * g
    v = ADAM_B2 * v + (1.0 - ADAM_B2) * (g * g)
    m_hat = m / (1.0 - ADAM_B1 ** ADAM_STEP)
    v_hat = v / (1.0 - ADAM_B2 ** ADAM_STEP)
    delta = -ADAM_LR * (m_hat / (jnp.sqrt(v_hat) + ADAM_EPS) + ADAM_WD * w)
    return delta, m, v


def _adamw_terms(name, terms, w, m, v):
    r, c = w.shape
    hr = r // 2
    rb = _pick(hr, max(16, (1 << 19) // c), 16)
    nb = hr // rb

    def body(t_ref, w_ref, m_ref, v_ref, g_ref, d_ref, nm_ref, nv_ref):
        g = t_ref[0].astype(F32)
        for k in range(1, N_CHIPS):
            g = g + t_ref[k].astype(F32)
        delta, nm, nv = _adamw_math(w_ref[...], g, m_ref[...], v_ref[...])
        g_ref[...] = g
        d_ref[...] = delta
        nm_ref[...] = nm
        nv_ref[...] = nv

    spec = pl.BlockSpec((rb, c), lambda h, i: (h * nb + i, 0))
    return pl.pallas_call(
        body, name=name, grid=(2, nb),
        in_specs=[pl.BlockSpec((None, N_CHIPS, rb, c), lambda h, i: (h, 0, i, 0)), spec, spec, spec],
        out_specs=[spec] * 4, out_shape=[_sds((r, c), F32)] * 4, compiler_params=_params(),
    )(terms, w, m, v)


def _mesh_place():
    x, y, c = lax.axis_index("x"), lax.axis_index("y"), lax.axis_index("c")
    chips = [(x, 1 - y), (1 - x, y), (1 - x, 1 - y)]
    return x, y, c, chips


def _all_gather_weights(bufs):
    n = len(bufs)

    def body(*refs):
        outs = refs[n:2 * n]
        ici_send, ici_recv, d2d_send, d2d_recv = refs[2 * n:]
        x, y, c, chips = _mesh_place()
        me = 2 * x + y
        sibling = (x, y, 1 - c)
        sent = []
        for wi in range(n):
            for k, (tx, ty) in enumerate(chips):
                own = outs[wi].at[me, c]
                cp = pltpu.make_async_remote_copy(
                    src_ref=own, dst_ref=own, send_sem=ici_send.at[wi * 3 + k], recv_sem=ici_recv.at[wi * 3 + k],
                    device_id=(tx, ty, c), device_id_type=MESH)
                cp.start()
                sent.append(cp)
        passed = []
        for wi in range(n):
            for k, (tx, ty) in enumerate(chips):
                slab = outs[wi].at[2 * tx + ty, c]
                pltpu.make_async_remote_copy(
                    src_ref=slab, dst_ref=slab, send_sem=ici_send.at[wi * 3 + k], recv_sem=ici_recv.at[wi * 3 + k],
                    device_id=(tx, ty, c), device_id_type=MESH).wait_recv()
                fw = pltpu.make_async_remote_copy(
                    src_ref=slab, dst_ref=slab, send_sem=d2d_send.at[wi * 3 + k], recv_sem=d2d_recv.at[wi * 3 + k],
                    device_id=sibling, device_id_type=MESH)
                fw.start()
                passed.append(fw)
        for wi in range(n):
            for k, (tx, ty) in enumerate(chips):
                slab = outs[wi].at[2 * tx + ty, 1 - c]
                pltpu.make_async_remote_copy(
                    src_ref=slab, dst_ref=slab, send_sem=d2d_send.at[wi * 3 + k], recv_sem=d2d_recv.at[wi * 3 + k],
                    device_id=sibling, device_id_type=MESH).wait_recv()
        for cp in sent + passed:
            cp.wait_send()

    return pl.pallas_call(
        body, name="all_gather_weights",
        in_specs=[ANY] * n, out_specs=[ANY] * n,
        out_shape=[_sds(g.shape, g.dtype) for g in bufs],
        scratch_shapes=[pltpu.SemaphoreType.DMA((3 * n,))] * 4,
        input_output_aliases={i: i for i in range(n)},
    )(*bufs)


def _run_comms(name, comms):
    plumb = _CommPlumbing(comms, 0, 0, 0)
    n_in, n_out = len(plumb.args), len(plumb.out_shape)

    def body(*refs):
        parts = []
        i0, o0, s0 = 0, n_in, n_in + n_out
        for cm in plumb.comms:
            parts.append((refs[i0:i0 + len(cm.ins)], refs[o0:o0 + len(cm.outs)], refs[s0:s0 + len(cm.sems)]))
            i0 += len(cm.ins)
            o0 += len(cm.outs)
            s0 += len(cm.sems)
        for cm, part in zip(plumb.comms, parts):
            cm.start(*part)
        for cm, part in zip(plumb.comms, parts):
            cm.finish(*part)

    res = pl.pallas_call(
        body, name=name, in_specs=[ANY] * n_in, out_specs=[ANY] * n_out, out_shape=plumb.out_shape,
        scratch_shapes=plumb.scratch, input_output_aliases=plumb.aliases,
    )(*plumb.args)
    plumb.deliver(res)


def _gather_ici(bufs):
    n = len(bufs)

    def copies(outs, sems):
        send_sem, recv_sem = sems
        x, y, c, chips = _mesh_place()
        me = 2 * x + y
        sends, recvs = [], []
        for wi in range(n):
            for k, (tx, ty) in enumerate(chips):
                sems_k = dict(send_sem=send_sem.at[wi * 3 + k], recv_sem=recv_sem.at[wi * 3 + k],
                              device_id=(tx, ty, c), device_id_type=MESH)
                own = outs[wi].at[me, c]
                sends.append(pltpu.make_async_remote_copy(src_ref=own, dst_ref=own, **sems_k))
                slab = outs[wi].at[2 * tx + ty, c]
                recvs.append(pltpu.make_async_remote_copy(src_ref=slab, dst_ref=slab, **sems_k))
        return sends, recvs

    def start(ins, outs, sems):
        for cp in copies(outs, sems)[0]:
            cp.start()

    def finish(ins, outs, sems):
        sends, recvs = copies(outs, sems)
        for cp in recvs:
            cp.wait_recv()
        for cp in sends:
            cp.wait_send()

    return _Comm(bufs, [_sds(g.shape, g.dtype) for g in bufs], {i: i for i in range(n)},
                 [pltpu.SemaphoreType.DMA((3 * n,)), pltpu.SemaphoreType.DMA((3 * n,))], start, finish)


def _gather_d2d(gathered):
    n = len(gathered)

    def copies(outs, sems):
        send_sem, recv_sem = sems
        x, y, c, chips = _mesh_place()
        sends, recvs = [], []
        for wi in range(n):
            for k, (tx, ty) in enumerate(chips):
                sems_k = dict(send_sem=send_sem.at[wi * 3 + k], recv_sem=recv_sem.at[wi * 3 + k],
                              device_id=(x, y, 1 - c), device_id_type=MESH)
                mine = outs[wi].at[2 * tx + ty, c]
                theirs = outs[wi].at[2 * tx + ty, 1 - c]
                sends.append(pltpu.make_async_remote_copy(src_ref=mine, dst_ref=mine, **sems_k))
                recvs.append(pltpu.make_async_remote_copy(src_ref=theirs, dst_ref=theirs, **sems_k))
        return sends, recvs

    def start(ins, outs, sems):
        for cp in copies(outs, sems)[0]:
            cp.start()

    def finish(ins, outs, sems):
        sends, recvs = copies(outs, sems)
        for cp in recvs:
            cp.wait_recv()
        for cp in sends:
            cp.wait_send()

    return _Comm(gathered, [_sds(g.shape, g.dtype) for g in gathered], {i: i for i in range(n)},
                 [pltpu.SemaphoreType.DMA((3 * n,)), pltpu.SemaphoreType.DMA((3 * n,))], start, finish)


def _exchange_halves(grads):
    n = len(grads)

    def copies(ins, outs, sems):
        send_sem, recv_sem = sems
        x, y, c, _ = _mesh_place()
        return [pltpu.make_async_remote_copy(
            src_ref=ins[wi].at[t, 1 - c], dst_ref=outs[wi].at[t],
            send_sem=send_sem.at[wi * N_CHIPS + t], recv_sem=recv_sem.at[wi * N_CHIPS + t],
            device_id=(x, y, 1 - c), device_id_type=MESH) for wi in range(n) for t in range(N_CHIPS)]

    def start(ins, outs, sems):
        for cp in copies(ins, outs, sems):
            cp.start()

    def finish(ins, outs, sems):
        for cp in copies(ins, outs, sems):
            cp.wait()

    return _Comm(grads, [_sds((N_CHIPS,) + g.shape[2:], g.dtype) for g in grads], {},
                 [pltpu.SemaphoreType.DMA((N_CHIPS * n,)), pltpu.SemaphoreType.DMA((N_CHIPS * n,))], start, finish)


def _scatter_ici(sums):
    n = len(sums)

    def copies(ins, outs, sems):
        local_sem, send_sem, recv_sem = sems
        x, y, c, chips = _mesh_place()
        me = 2 * x + y
        local, sends, recvs = [], [], []
        for wi in range(n):
            local.append(pltpu.make_async_copy(ins[wi].at[me], outs[wi].at[c, 0], local_sem.at[wi]))
            for k, (tx, ty) in enumerate(chips):
                sems_k = dict(send_sem=send_sem.at[wi * 3 + k], recv_sem=recv_sem.at[wi * 3 + k],
                              device_id=(tx, ty, c), device_id_type=MESH)
                land = outs[wi].at[c, k + 1]
                sends.append(pltpu.make_async_remote_copy(src_ref=ins[wi].at[2 * tx + ty], dst_ref=land, **sems_k))
                recvs.append(pltpu.make_async_remote_copy(src_ref=land, dst_ref=land, **sems_k))
        return local, sends, recvs

    def start(ins, outs, sems):
        local, sends, _ = copies(ins, outs, sems)
        for cp in local + sends:
            cp.start()

    def finish(ins, outs, sems):
        local, sends, recvs = copies(ins, outs, sems)
        for cp in local:
            cp.wait()
        for cp in recvs:
            cp.wait_recv()
        for cp in sends:
            cp.wait_send()

    return _Comm(sums, [_sds((2, N_CHIPS) + s.shape[1:], s.dtype) for s in sums], {},
                 [pltpu.SemaphoreType.DMA((n,)), pltpu.SemaphoreType.DMA((3 * n,)), pltpu.SemaphoreType.DMA((3 * n,))],
                 start, finish)


def _scatter_d2d(terms):
    n = len(terms)

    def copies(outs, sems):
        send_sem, recv_sem = sems
        x, y, c, _ = _mesh_place()
        sends, recvs = [], []
        for wi in range(n):
            sems_w = dict(send_sem=send_sem.at[wi], recv_sem=recv_sem.at[wi],
                          device_id=(x, y, 1 - c), device_id_type=MESH)
            sends.append(pltpu.make_async_remote_copy(src_ref=outs[wi].at[c], dst_ref=outs[wi].at[c], **sems_w))
            recvs.append(pltpu.make_async_remote_copy(src_ref=outs[wi].at[1 - c], dst_ref=outs[wi].at[1 - c], **sems_w))
        return sends, recvs

    def start(ins, outs, sems):
        for cp in copies(outs, sems)[0]:
            cp.start()

    def finish(ins, outs, sems):
        sends, recvs = copies(outs, sems)
        for cp in recvs:
            cp.wait_recv()
        for cp in sends:
            cp.wait_send()

    return _Comm(terms, [_sds(t.shape, t.dtype) for t in terms], {i: i for i in range(n)},
                 [pltpu.SemaphoreType.DMA((n,)), pltpu.SemaphoreType.DMA((n,))], start, finish)


def _chip_sum(name, grad, got, core):
    _, _, hr, c = grad.shape
    rb = _pick(hr, max(16, (1 << 19) // c), 16)

    def body(core_ref, a_ref, b_ref, o_ref):
        o_ref[...] = (a_ref[...].astype(F32) + b_ref[...].astype(F32)).astype(BF16)

    out_spec = pl.BlockSpec((None, rb, c), lambda t, i, core_ref: (t, i, 0))
    return pl.pallas_call(
        body, name=name,
        grid_spec=pltpu.PrefetchScalarGridSpec(
            num_scalar_prefetch=1, grid=(N_CHIPS, hr // rb),
            in_specs=[pl.BlockSpec((None, None, rb, c), lambda t, i, core_ref: (t, core_ref[0], i, 0)), out_spec],
            out_specs=out_spec),
        out_shape=_sds((N_CHIPS, hr, c), BF16), compiler_params=_params(),
    )(core, grad, got)


def _all_reduce_small(pack):
    r = pack.shape[0]

    def body(p_ref, o_ref, land_ref, send_sem, recv_sem):
        x, y, c, _ = _mesh_place()
        me = 4 * x + 2 * y + c
        flips = [(k >> 2 & 1, k >> 1 & 1, k & 1) for k in range(1, N_DEV)]

        def peer(fx, fy, fc):
            return (1 - x if fx else x, 1 - y if fy else y, 1 - c if fc else c)

        land_ref[me] = p_ref[...]
        sent = []
        for k, flip in enumerate(flips):
            cp = pltpu.make_async_remote_copy(
                src_ref=p_ref, dst_ref=land_ref.at[me], send_sem=send_sem.at[k], recv_sem=recv_sem.at[k],
                device_id=peer(*flip), device_id_type=MESH)
            cp.start()
            sent.append(cp)
        for k, flip in enumerate(flips):
            px, py, pc = peer(*flip)
            slot = land_ref.at[4 * px + 2 * py + pc]
            pltpu.make_async_remote_copy(
                src_ref=slot, dst_ref=slot, send_sem=send_sem.at[k], recv_sem=recv_sem.at[k],
                device_id=(px, py, pc), device_id_type=MESH).wait_recv()
        total = land_ref[0]
        for d in range(1, N_DEV):
            total = total + land_ref[d]
        o_ref[...] = total
        for cp in sent:
            cp.wait_send()

    vmem = pl.BlockSpec(memory_space=pltpu.VMEM)
    return pl.pallas_call(
        body, name="all_reduce_small", in_specs=[vmem], out_specs=vmem, out_shape=_sds((r, 128), F32),
        scratch_shapes=[pltpu.VMEM((N_DEV, r, 128), F32), pltpu.SemaphoreType.DMA((N_DEV - 1,)),
                        pltpu.SemaphoreType.DMA((N_DEV - 1,))],
    )(pack)


PACK_TILE = 8 * 128


def _pack(items):
    rows, i = [], 0
    while i < len(items):
        j = i
        while j < len(items) and items[j].size == items[i].size:
            j += 1
        group = jnp.stack([it.reshape(-1).astype(F32) for it in items[i:j]])
        rows.append(jnp.pad(group, ((0, 0), (0, -group.shape[1] % PACK_TILE))).reshape(-1, 128))
        i = j
    return jnp.concatenate(rows, axis=0)


def _unpack(pack, shapes):
    out, row = [], 0
    for shp in shapes:
        size = int(np.prod(shp))
        nrow = -(-size // PACK_TILE) * (PACK_TILE // 128)
        out.append(pack[row:row + nrow].reshape(-1)[:size].reshape(shp))
        row += nrow
    return out


BIG = ["ffn1_w_gu", "ffn1_w_down", "w_in", "w_gate", "w_proj_a", "w_proj_b", "w_out",
       "ffn2_w_gu", "ffn2_w_down", "w_ple_gate", "w_ple_proj"]
SMALL = ["ffn1_norm", "mix_norm", "ffn2_norm", "ple_norm", "a_q_norm", "a_k_norm", "b_q_norm", "b_k_norm",
         "a_rel_bias", "b_sinks"]
WEIGHTS = ["ffn1_norm", "ffn1_w_gu", "ffn1_w_down", "mix_norm", "w_in", "a_q_norm", "a_k_norm", "a_rel_bias",
           "b_q_norm", "b_k_norm", "b_sinks", "w_gate", "w_proj_a", "w_proj_b", "w_out", "ffn2_norm",
           "ffn2_w_gu", "ffn2_w_down", "ple_norm", "w_ple_gate", "w_ple_proj"]
ATTN_A = dict(prev=A_PREV_CHUNKS * CHUNK, group=1, kw=A_WIDTH, qblk=0, kblk=1, vblk=2)
ATTN_B = dict(prev=B_PREV_CHUNKS * CHUNK, group=N_HEADS // B_KV_HEADS, kw=B_KV_WIDTH, qblk=3,
              kblk=4 * A_WIDTH // B_KV_WIDTH, vblk=4 * A_WIDTH // B_KV_WIDTH + 1)


def _cast_epilogue(accs, extras, outs, ij):
    for acc, out in zip(accs, outs):
        out[...] = acc.astype(out.dtype)


GATHER_FIRST = ["ffn1_w_gu"]
ROW_SHARDED = ("ffn1_w_down", "ffn2_w_down", "w_out", "w_ple_gate")


def _slotted(name, grad):
    if name == "w_in":
        rows, cols = grad.shape
        grad = jnp.transpose(grad.reshape(rows, N_CHIPS, cols // N_CHIPS), (1, 0, 2))
    elif name in ROW_SHARDED:
        grad = grad.reshape(N_CHIPS, grad.shape[0] // N_CHIPS, grad.shape[1])
    return grad.reshape(N_CHIPS, 2, grad.shape[1] // 2, grad.shape[2])


def _local_step(xt, pt, tgt, n_batch, bufs, small, core):
    t, d = xt.shape
    tm = _pick(t, ROW_TILE, 8)
    tk = _pick(t, ROW_TILE, 8)
    nt = t // tm
    row = pl.BlockSpec((tm, d), lambda i, j, k: (i, 0))
    gs = bufs["w_gate"].shape[2]
    ps = bufs["w_proj_a"].shape[2]
    es = bufs["w_ple_proj"].shape[2]
    pdim = pt.shape[1]
    ncols = N_CHIPS * bufs["w_in"].shape[2]
    tin = ncols // 2
    assert 2 * gs == d and 4 * ps == d and 4 * es == d and tin % 128 == 0

    w = {}
    halves = {n: b.reshape(N_CHIPS, 2, b.shape[1] // 2, b.shape[2]) for n, b in bufs.items()}

    def publish(names, arrays):
        for name, g in zip(names, arrays):
            g = g.reshape(N_CHIPS, 2 * g.shape[2], g.shape[3])
            if name in ROW_SHARDED:
                g = g.reshape(N_CHIPS * g.shape[1], g.shape[2])
            elif name == "w_in":
                g = jnp.transpose(g, (1, 0, 2)).reshape(g.shape[1], N_CHIPS * g.shape[2])
            w[name] = g

    class GatherPipe:
        def __init__(self, names):
            self.names = names
            self.stage = None

        def ici(self):
            self.stage = _gather_ici(self.bufs())
            return self.stage

        def d2d(self):
            self.stage = _gather_d2d(self.bufs())
            return self.stage

        def bufs(self):
            return self.stage.results if self.stage is not None else [halves[n] for n in self.names]

        def publish(self):
            publish(self.names, self.stage.results)

    class GradPipe:
        def __init__(self, names):
            self.names = names

        def exchange(self, grads):
            self.grads = [_slotted(n, g) for n, g in zip(self.names, grads)]
            self.x = _exchange_halves(self.grads)
            return self.x

        def scatter(self):
            self.sums = [_chip_sum("chip_sum_" + n, g, got, core)
                         for n, g, got in zip(self.names, self.grads, self.x.results)]
            self.s = _scatter_ici(self.sums)
            return self.s

        def forward(self):
            self.f = _scatter_d2d(self.s.results)
            return self.f

        def terms(self):
            return dict(zip(self.names, self.f.results))

    publish(GATHER_FIRST, _all_gather_weights([halves[n] for n in GATHER_FIRST]))
    g_in, g_proj, g_ple = GatherPipe(["w_in", "w_gate"]), GatherPipe(["w_proj_a", "w_proj_b", "w_out"]), \
        GatherPipe(["w_ple_gate", "w_ple_proj"])
    g_down1, g_down2, g_up2 = GatherPipe(["ffn1_w_down"]), GatherPipe(["ffn2_w_down"]), GatherPipe(["ffn2_w_gu"])

    def ffn1_down_weight():
        _run_comms("gather_ffn1_down", [g_down1.d2d()])
        g_down1.publish()
        return w["ffn1_w_down"]

    h1, ffn1_saved = _ffn_fwd("ffn1", xt, small["ffn1_norm"], w["ffn1_w_gu"], ffn1_down_weight,
                              {"up": lambda: [g_down1.ici(), g_in.ici()],
                               "down": lambda: [g_in.d2d(), g_proj.ici()]})
    g_in.publish()
    un = _rms_fwd("mix_norm", h1, small["mix_norm"])
    w_in, wgate = w["w_in"], w["w_gate"]
    (qkv,) = _mm(
        "qkv", "nn", (nt, 2, 1),
        [(un, row, w_in, pl.BlockSpec((d, tin), lambda i, j, k: (0, j)))], [],
        [(_sds((t, ncols), BF16), pl.BlockSpec((tm, tin), lambda i, j, k: (i, j)))], (tm, tin), _cast_epilogue,
        j_outer=True, comms=[g_proj.d2d(), g_ple.ici()])
    g_proj.publish()
    wpa, wpb, wout = w["w_proj_a"], w["w_proj_b"], w["w_out"]

    def gate_epilogue(accs, extras, outs, ij):
        outs[0][...] = jax.nn.sigmoid(accs[0]).astype(BF16)

    (gates,) = _mm(
        "gate", "nn", (nt, 4, 1),
        [(un, row, wgate, pl.BlockSpec((None, d, gs), lambda i, j, k: (j, 0, 0)))], [],
        [(_sds((2, t, d), BF16), pl.BlockSpec((None, tm, gs), lambda i, j, k: (j // 2, i, j % 2)))],
        (tm, gs), gate_epilogue, j_outer=True, chunked=True, comms=[g_ple.d2d(), g_down2.ici()])
    g_ple.publish()
    wpg, wpe = w["w_ple_gate"], w["w_ple_proj"]

    bias_a = _pair_bias(_bias_a(small["a_rel_bias"][0]))
    bias_b = _pair_bias(_bias_b())
    sink_a = _pair_rows(jnp.full((N_HEADS, 128), NEG_INF, F32))
    sink_b = _pair_rows(jnp.broadcast_to(small["b_sinks"][0][:, None], (N_HEADS, 128)))
    gqa, gka, gqb, gkb = [jnp.tile(small[k], (1, 2)) for k in ("a_q_norm", "a_k_norm", "b_q_norm", "b_k_norm")]
    ya, lse_a = _attn_fwd("attn_a_fwd", qkv, bias_a, sink_a, gqa, gka, ATTN_A, n_batch,
                          comms=[g_down2.d2d(), g_up2.ici()])
    g_down2.publish()
    yb, lse_b = _attn_fwd("attn_b_fwd", qkv, bias_b, sink_b, gqb, gkb, ATTN_B, n_batch, comms=[g_up2.d2d()])
    g_up2.publish()

    def merge_epilogue(accs, extras, outs, ij):
        pa, pb = accs
        outs[0][...] = (extras[0][...].astype(F32) * pa + extras[1][...].astype(F32) * pb).astype(BF16)
        outs[1][...] = pa.astype(BF16)
        outs[2][...] = pb.astype(BF16)

    y_spec = pl.BlockSpec((tm, A_WIDTH), lambda i, j, k: (i, 0))
    proj_spec = pl.BlockSpec((None, A_WIDTH, ps), lambda i, j, k: (j, 0, 0))
    tile_ps = pl.BlockSpec((tm, ps), lambda i, j, k: (i, j))
    merged, pa, pb = _mm(
        "proj_merge", "nn", (nt, 4, 1),
        [(ya, y_spec, wpa, proj_spec), (yb, y_spec, wpb, proj_spec)],
        [(gates, pl.BlockSpec((None, tm, ps), lambda i, j, k: (0, i, j))),
         (gates, pl.BlockSpec((None, tm, ps), lambda i, j, k: (1, i, j)))],
        [(_sds((t, d), BF16), tile_ps)] * 3, (tm, ps), merge_epilogue)

    def residual_epilogue(accs, extras, outs, ij):
        outs[0][...] = extras[0][...] + accs[0]

    (h2,) = _mm(
        "out_proj", "nn", (nt, 1, 1),
        [(merged, row, wout, pl.BlockSpec((d, d), lambda i, j, k: (0, 0)))],
        [(h1, row)], [(_sds((t, d), F32), row)], (tm, d), residual_epilogue)

    h3, ffn2_saved = _ffn_fwd("ffn2", h2, small["ffn2_norm"], w["ffn2_w_gu"], w["ffn2_w_down"], {})
    n3 = _rms_fwd("ple_norm", h3, small["ple_norm"])
    tile_es = pl.BlockSpec((tm, es), lambda i, j, k: (i, j))
    (pe,) = _mm(
        "ple_embed", "nn", (nt, 4, 1),
        [(pt, pl.BlockSpec((tm, pdim), lambda i, j, k: (i, 0)), wpe, pl.BlockSpec((None, pdim, es), lambda i, j, k: (j, 0, 0)))],
        [], [(_sds((t, d), F32), tile_es)], (tm, es), _cast_epilogue)

    th = _pick(d, 512)

    def head_epilogue(accs, extras, outs, ij):
        h3_ref, pe_ref, tgt_ref = extras
        dy_ref, dpe_ref, dz_ref, loss_ref = outs
        pg = jax.nn.sigmoid(accs[0])
        pev = pe_ref[...]
        diff = h3_ref[...] + pg * pev - tgt_ref[...]
        dy = diff * (1.0 / d)
        dy_ref[...] = dy
        dpe_ref[...] = (dy * pg).astype(BF16)
        dz_ref[...] = (dy * pev * pg * (1.0 - pg)).astype(BF16)
        _accumulate(loss_ref, jnp.full(loss_ref.shape, jnp.sum(diff * diff), F32), (ij[0] == 0) & (ij[1] == 0))

    tile_h = pl.BlockSpec((tm, th), lambda i, j, k: (i, j))
    dy, dpe, dz, loss_acc = _mm(
        "ple_gate_loss", "nn", (nt, d // th, 1),
        [(n3, row, wpg, pl.BlockSpec((d, th), lambda i, j, k: (0, j)))],
        [(h3, tile_h), (pe, tile_h), (tgt, tile_h)],
        [(_sds((t, d), F32), tile_h), (_sds((t, d), BF16), tile_h), (_sds((t, d), BF16), tile_h),
         (_sds((8, 128), F32), pl.BlockSpec((8, 128), lambda i, j, k: (0, 0)))],
        (tm, th), head_epilogue, j_outer=True, chunked=True)
    loss = 0.5 * loss_acc[0, 0] / d

    nk = t // tk
    (dwpe,) = _mm(
        "d_w_ple_proj", "tn", (1, 4, nk),
        [(pt, pl.BlockSpec((tk, pdim), lambda i, j, k: (k, 0)), dpe, pl.BlockSpec((tk, es), lambda i, j, k: (k, j)))],
        [], [(_sds((4, pdim, es), BF16), pl.BlockSpec((None, pdim, es), lambda i, j, k: (j, 0, 0)))],
        (pdim, es), _cast_epilogue)

    def dense_grad(name, a, dyb, comms=()):
        (res,) = _mm(
            name, "tn", (1, d // th, nk),
            [(a, pl.BlockSpec((tk, d), lambda i, j, k: (k, 0)), dyb, pl.BlockSpec((tk, th), lambda i, j, k: (k, j)))],
            [], [(_sds((d, d), BF16), pl.BlockSpec((d, th), lambda i, j, k: (0, j)))], (d, th), _cast_epilogue,
            comms=comms)
        return res

    dwpg = dense_grad("d_w_ple_gate", n3, dz)
    tmn = _pick(t, ROW_TILE, 8)
    extras, outs = _rms_bwd_io(h3, small["ple_norm"], dy, tmn)
    dh3, dh3_b, d_ple_norm = _mm(
        "d_ple_norm", "nt", (t // tmn, 1, 1),
        [(dz, pl.BlockSpec((tmn, d), lambda i, j, k: (i, 0)), wpg, pl.BlockSpec((d, d), lambda i, j, k: (0, 0)))],
        extras, outs, (tmn, d), _rms_bwd_epilogue)

    up2, down2, ple = GradPipe(["ffn2_w_gu"]), GradPipe(["ffn2_w_down"]), GradPipe(["w_ple_gate", "w_ple_proj"])
    proj = GradPipe(["w_proj_a", "w_proj_b", "w_out"])
    dh2, dh2_b, d_ffn2_norm, dwgu2, dwd2 = _ffn_bwd(
        "ffn2", dh3, dh3_b, h2, small["ffn2_norm"], w["ffn2_w_gu"], w["ffn2_w_down"], ffn2_saved,
        {"dnorm": lambda dwgu, dwd: [up2.exchange([dwgu]), down2.exchange([dwd]), ple.exchange([dwpg, dwpe])]})

    def dmerge_epilogue(accs, extras, outs, ij):
        dmo = accs[0]
        g_ref, pa_ref, pb_ref = extras
        dg_ref, dpa_ref, dpb_ref = outs
        ga = g_ref[0].astype(F32)
        gb = g_ref[1].astype(F32)
        dg_ref[0] = (dmo * pa_ref[...].astype(F32) * ga * (1.0 - ga)).astype(BF16)
        dg_ref[1] = (dmo * pb_ref[...].astype(F32) * gb * (1.0 - gb)).astype(BF16)
        dpa_ref[...] = (dmo * ga).astype(BF16)
        dpb_ref[...] = (dmo * gb).astype(BF16)

    g_spec = pl.BlockSpec((2, tm, th), lambda i, j, k: (0, i, j))
    dgates, dpa, dpb = _mm(
        "d_merge", "nt", (nt, d // th, 1),
        [(dh2_b, row, wout, pl.BlockSpec((th, d), lambda i, j, k: (j, 0)))],
        [(gates, g_spec), (pa, tile_h), (pb, tile_h)],
        [(_sds((2, t, d), BF16), g_spec), (_sds((t, d), BF16), tile_h), (_sds((t, d), BF16), tile_h)],
        (tm, th), dmerge_epilogue, j_outer=True, chunked=True, comms=[down2.scatter()])
    dwout = dense_grad("d_w_out", merged, dh2_b, comms=[down2.forward(), ple.scatter()])

    yk_spec = pl.BlockSpec((tk, A_WIDTH), lambda i, j, k: (k, 0))
    dk_spec = pl.BlockSpec((tk, ps), lambda i, j, k: (k, j))
    dproj = (_sds((4, A_WIDTH, ps), BF16), proj_spec)
    dwpa, dwpb = _mm(
        "d_w_proj", "tn", (1, 4, nk),
        [(ya, yk_spec, dpa, dk_spec), (yb, yk_spec, dpb, dk_spec)], [], [dproj, dproj], (A_WIDTH, ps), _cast_epilogue,
        comms=[ple.forward()])
    dproj_a = pl.BlockSpec((tm, ps), lambda i, j, k: (i, k))
    wproj_k = pl.BlockSpec((None, A_WIDTH, ps), lambda i, j, k: (k, 0, 0))
    dya, dyb = _mm(
        "d_attn_out", "nt", (nt, 1, 4),
        [(dpa, dproj_a, wpa, wproj_k), (dpb, dproj_a, wpb, wproj_k)], [],
        [(_sds((t, A_WIDTH), BF16), y_spec)] * 2, (tm, A_WIDTH), _cast_epilogue,
        comms=[proj.exchange([dwpa, dwpb, dwout])])

    dqa, dka, dva, dbias_a, _, dgqa, dgka = _attn_bwd(
        "attn_a_bwd", qkv, bias_a, sink_a, gqa, gka, ya, dya, lse_a, ATTN_A, n_batch, True,
        comms=[up2.scatter(), proj.scatter()])
    dqb, dkb, dvb, _, dsink_b, dgqb, dgkb = _attn_bwd(
        "attn_b_bwd", qkv, bias_b, sink_b, gqb, gkb, yb, dyb, lse_b, ATTN_B, n_batch, False,
        comms=[up2.forward(), proj.forward()])
    dqkv = jnp.concatenate([dqa, dka, dva, dqb, dkb, dvb], axis=1)

    (dwgate,) = _mm(
        "d_w_gate", "tn", (1, 4, nk),
        [(un, pl.BlockSpec((tk, d), lambda i, j, k: (k, 0)),
          dgates, pl.BlockSpec((None, tk, gs), lambda i, j, k: (j // 2, k, j % 2)))],
        [], [(_sds((4, d, gs), BF16), pl.BlockSpec((None, d, gs), lambda i, j, k: (j, 0, 0)))], (d, gs), _cast_epilogue)
    (dwin,) = _mm(
        "d_w_in", "tn", (1, 2, nk),
        [(un, pl.BlockSpec((tk, d), lambda i, j, k: (k, 0)), dqkv, pl.BlockSpec((tk, tin), lambda i, j, k: (k, j)))],
        [], [(_sds((d, ncols), BF16), pl.BlockSpec((d, tin), lambda i, j, k: (0, j)))], (d, tin), _cast_epilogue)

    mixer = GradPipe(["w_in", "w_gate"])
    extras, outs = _rms_bwd_io(h1, small["mix_norm"], dh2, tmn)
    dh1, dh1_b, d_mix_norm = _mm(
        "d_mix_norm", "nt", (t // tmn, 1, 6),
        [(dgates, pl.BlockSpec((None, tmn, gs), lambda i, j, k: (jnp.minimum(k, 3) // 2, i, jnp.minimum(k, 3) % 2)),
          wgate, pl.BlockSpec((None, d, gs), lambda i, j, k: (jnp.minimum(k, 3), 0, 0))),
         (dqkv, pl.BlockSpec((tmn, tin), lambda i, j, k: (i, jnp.maximum(k - 4, 0))),
          w_in, pl.BlockSpec((d, tin), lambda i, j, k: (0, jnp.maximum(k - 4, 0))))],
        extras, outs, (tmn, d), _rms_bwd_epilogue, steps=[4, 2],
        comms=[mixer.exchange([dwin, dwgate])])

    up1 = GradPipe(["ffn1_w_gu"])
    down1 = GradPipe(["ffn1_w_down"])
    dx, _, d_ffn1_norm, _, _ = _ffn_bwd(
        "ffn1", dh1, dh1_b, xt, small["ffn1_norm"], w["ffn1_w_gu"], w["ffn1_w_down"], ffn1_saved,
        {"dwgu": lambda: [mixer.scatter()],
         "dwd": lambda dwgu: [mixer.forward(), up1.exchange([dwgu])],
         "dnorm": lambda dwgu, dwd: [up1.scatter(), down1.exchange([dwd])]})
    _run_comms("grad_tail_scatter", [up1.forward(), down1.scatter()])
    _run_comms("grad_tail_forward", [down1.forward()])
    terms = {}
    for pipe in (up2, down2, ple, proj, mixer, up1, down1):
        terms.update(pipe.terms())

    def fold(v):
        return v[0, :HEAD_DIM] + v[0, HEAD_DIM:]

    small_grads = {"ffn1_norm": d_ffn1_norm, "mix_norm": d_mix_norm, "ffn2_norm": d_ffn2_norm,
                   "ple_norm": d_ple_norm, "a_q_norm": fold(dgqa), "a_k_norm": fold(dgka),
                   "b_q_norm": fold(dgqb), "b_k_norm": fold(dgkb), "a_rel_bias": _rel_bias_grad(_unpair_bias(dbias_a)),
                   "b_sinks": jnp.sum(dsink_b, axis=1)}
    return loss, dx, terms, small_grads


def kernel(x, p, ffn1_norm, ffn1_w_gu, ffn1_w_down, mix_norm, w_in, a_q_norm, a_k_norm, a_rel_bias, b_q_norm, b_k_norm, b_sinks, w_gate, w_proj_a, w_proj_b, w_out, ffn2_norm, ffn2_w_gu, ffn2_w_down, ple_norm, w_ple_gate, w_ple_proj, loss_target, m_ffn1_norm, m_ffn1_w_gu, m_ffn1_w_down, m_mix_norm, m_w_in, m_a_q_norm, m_a_k_norm, m_a_rel_bias, m_b_q_norm, m_b_k_norm, m_b_sinks, m_w_gate, m_w_proj_a, m_w_proj_b, m_w_out, m_ffn2_norm, m_ffn2_w_gu, m_ffn2_w_down, m_ple_norm, m_w_ple_gate, m_w_ple_proj, v_ffn1_norm, v_ffn1_w_gu, v_ffn1_w_down, v_mix_norm, v_w_in, v_a_q_norm, v_a_k_norm, v_a_rel_bias, v_b_q_norm, v_b_k_norm, v_b_sinks, v_w_gate, v_w_proj_a, v_w_proj_b, v_w_out, v_ffn2_norm, v_ffn2_w_gu, v_ffn2_w_down, v_ple_norm, v_w_ple_gate, v_w_ple_proj):
    given = dict(locals())
    n_batch, s, d = x.shape
    t = n_batch * s
    xt = x.reshape(t, d)
    pt = p.reshape(t, p.shape[-1])
    tgt = loss_target.reshape(t, d)

    chip = (2 * lax.axis_index("x") + lax.axis_index("y")).astype(jnp.int32).reshape(1)
    bufs = {name: _cast_into_slot("cast_" + name, given[name][0], chip) for name in BIG}
    small = {name: given[name] for name in SMALL}
    core = lax.axis_index("c").astype(jnp.int32).reshape(1)
    loss, dx, terms, small_grads = _local_step(xt, pt, tgt, n_batch, bufs, small, core)

    grads, deltas, new_m, new_v = {}, {}, {}, {}
    for name in BIG:
        gw, dl, nm, nv = _adamw_terms("adamw_" + name, terms[name], given[name][0], given["m_" + name][0],
                                      given["v_" + name][0])
        grads[name], deltas[name], new_m[name], new_v[name] = gw[None], dl[None], nm[None], nv[None]

    small_shapes = [given[name].shape for name in SMALL] + [()]
    g_pack = _all_reduce_small(_pack([small_grads[name] for name in SMALL] + [loss]))
    zero = jnp.zeros((), F32)
    w_pack = _pack([given[name] for name in SMALL] + [zero])
    m_pack = _pack([given["m_" + name] for name in SMALL] + [zero])
    v_pack = _pack([given["v_" + name] for name in SMALL] + [zero])
    d_pack, nm_pack, nv_pack = _ew("adamw_small", lambda wv, gv, mv, vv: _adamw_math(wv, gv, mv, vv),
                                   [w_pack, g_pack, m_pack, v_pack], [F32] * 3)
    g_small = _unpack(g_pack, small_shapes)
    loss_total = g_small[-1]
    for name, gv, dv, mv, vv in zip(SMALL, g_small, _unpack(d_pack, small_shapes), _unpack(nm_pack, small_shapes),
                                    _unpack(nv_pack, small_shapes)):
        grads[name], deltas[name], new_m[name], new_v[name] = gv, dv, mv, vv

    return (loss_total, dx.reshape(x.shape), *[grads[n] for n in WEIGHTS], *[deltas[n] for n in WEIGHTS],
            *[new_m[n] for n in WEIGHTS], *[new_v[n] for n in WEIGHTS])
```

```python
import functools

import numpy as np
import jax
import jax.numpy as jnp
from jax import lax
from jax.experimental import pallas as pl
from jax.experimental.pallas import tpu as pltpu

F32 = jnp.float32
BF16 = jnp.bfloat16

CHUNK = 64
HEAD_DIM = 64
A_PREV_CHUNKS = 8
A_MAX_REL = 128
N_HEADS = 8
B_KV_HEADS = 2
B_PREV_CHUNKS = 2
A_WIDTH = N_HEADS * HEAD_DIM
B_KV_WIDTH = B_KV_HEADS * HEAD_DIM
EPS = 1e-6
NEG_INF = -1e30
ATTN_SCALE = HEAD_DIM ** -0.5
Q_BLOCK = 128
PAIR = 2 * HEAD_DIM

ADAM_LR = 0.001
ADAM_B1 = 0.9
ADAM_B2 = 0.999
ADAM_EPS = 1e-08
ADAM_WD = 0.01
ADAM_STEP = 10

N_CHIPS = 4
N_DEV = 8
VMEM_LIMIT_V7X = 56 * 1024 * 1024
ROW_TILE = 1024
MESH = pl.DeviceIdType.MESH
COLLECTIVE_IDS = {("sibling",): 1, ("chips",): 2, ("chips", "sibling"): 3}
ANY = pl.BlockSpec(memory_space=pl.ANY)

_DN = {
    "nn": (((1,), (0,)), ((), ())),
    "nt": (((1,), (1,)), ((), ())),
    "tn": (((0,), (0,)), ((), ())),
}


def _pick(n, target, mult=128):
    best = None
    for d in range(mult, min(n, target) + 1, mult):
        if n % d == 0:
            best = d
    return n if best is None else best


def _dot(a, b, mode):
    return lax.dot_general(a.astype(BF16), b.astype(BF16), _DN[mode], preferred_element_type=F32)


def _params():
    return pltpu.CompilerParams(vmem_limit_bytes=VMEM_LIMIT_V7X)


class _Comm:
    def __init__(self, peers, ins, outs, aliases, sems, start, finish):
        self.peers = peers
        self.ins, self.outs, self.aliases, self.sems = list(ins), list(outs), dict(aliases), list(sems)
        self.start, self.finish = start, finish
        self.results = None


class _CommPlumbing:
    def __init__(self, comms, n_in, n_out, n_scratch):
        self.comms = list(comms)
        self.n_in, self.n_out, self.n_scratch = n_in, n_out, n_scratch
        self.args = [a for cm in self.comms for a in cm.ins]
        self.out_shape = [o for cm in self.comms for o in cm.outs]
        self.scratch = [s for cm in self.comms for s in cm.sems]
        self.aliases = {}
        i0, o0 = n_in, n_out
        for cm in self.comms:
            for a, b in cm.aliases.items():
                self.aliases[i0 + a] = o0 + b
            i0 += len(cm.ins)
            o0 += len(cm.outs)

    def _parts(self, in_refs, out_refs, scratch_refs):
        parts = []
        i0, o0, s0 = self.n_in, self.n_out, self.n_scratch
        for cm in self.comms:
            parts.append((in_refs[i0:i0 + len(cm.ins)], out_refs[o0:o0 + len(cm.outs)],
                          scratch_refs[s0:s0 + len(cm.sems)]))
            i0 += len(cm.ins)
            o0 += len(cm.outs)
            s0 += len(cm.sems)
        return parts

    def kinds(self):
        return sorted(set(cm.peers for cm in self.comms))

    def params(self, **kwargs):
        if self.comms:
            kwargs["collective_id"] = COLLECTIVE_IDS[tuple(self.kinds())]
        return pltpu.CompilerParams(**kwargs)

    def handshake(self):
        x, y, c, chips = _mesh_place()
        peers = []
        if "sibling" in self.kinds():
            peers.append((x, y, 1 - c))
        if "chips" in self.kinds():
            peers += [(tx, ty, c) for tx, ty in chips]
        barrier = pltpu.get_barrier_semaphore()
        for peer in peers:
            pl.semaphore_signal(barrier, inc=1, device_id=peer, device_id_type=MESH)
        pl.semaphore_wait(barrier, len(peers))

    def start_at(self, in_refs, out_refs, scratch_refs, first):
        if self.comms:
            parts = self._parts(in_refs, out_refs, scratch_refs)

            @pl.when(first)
            def _():
                self.handshake()
                for cm, part in zip(self.comms, parts):
                    cm.start(*part)

    def finish_at(self, in_refs, out_refs, scratch_refs, last):
        if self.comms:
            parts = self._parts(in_refs, out_refs, scratch_refs)

            @pl.when(last)
            def _():
                for cm, part in zip(self.comms, parts):
                    cm.finish(*part)

    def deliver(self, results):
        o0 = self.n_out
        for cm in self.comms:
            cm.results = list(results[o0:o0 + len(cm.outs)])
            o0 += len(cm.outs)
        return list(results[:self.n_out])


def _swap_ij(spec):
    index_map = spec.index_map
    return pl.BlockSpec(spec.block_shape, lambda j, i, k: index_map(i, j, k))


MXU_COLUMNS_V7X = 256


def _mm(name, mode, grid, pairs, extras, outs, acc_shape, epilogue, steps=None, comms=(), j_outer=False,
        chunked=False):
    ni, nj, nk = grid
    n_in = 2 * len(pairs) + len(extras)
    n_out = len(outs)
    tn = acc_shape[1]
    col_chunks = None
    if chunked:
        assert nk == 1 and steps is None and mode in ("nn", "nt")
        col_chunks = [(c0, min(MXU_COLUMNS_V7X, tn - c0)) for c0 in range(0, tn, MXU_COLUMNS_V7X)]
    n_acc = 0 if chunked else (len(pairs) if steps is None else 1)
    plumb = _CommPlumbing(comms, n_in, n_out, n_acc)
    n_all_in = n_in + len(plumb.args)
    n_all_out = n_out + len(plumb.out_shape)
    if j_outer:
        grid = (nj, ni, nk)
        pairs = [(a, _swap_ij(a_spec), b, _swap_ij(b_spec)) for a, a_spec, b, b_spec in pairs]
        extras = [(e, _swap_ij(e_spec)) for e, e_spec in extras]
        outs = [(o, _swap_ij(o_spec)) for o, o_spec in outs]

    def body(*refs):
        in_refs = refs[:n_all_in]
        out_refs = refs[n_all_in:n_all_in + n_all_out]
        scratch = refs[n_all_in + n_all_out:]
        accs = scratch[:n_acc]
        i = pl.program_id(1 if j_outer else 0)
        j = pl.program_id(0 if j_outer else 1)
        k = pl.program_id(2)
        plumb.start_at(in_refs, out_refs, scratch, (i == 0) & (j == 0) & (k == 0))

        def contrib(p, acc):
            acc[...] += _dot(in_refs[2 * p][...], in_refs[2 * p + 1][...], mode)

        if col_chunks:
            def cols(ref, c0, cs):
                if ref.shape[-1] != tn:
                    return ref
                return ref.at[(slice(None),) * (len(ref.shape) - 1) + (pl.ds(c0, cs),)]

            lhs = [in_refs[2 * p][...] for p in range(len(pairs))]
            for ci, (c0, cs) in enumerate(col_chunks):
                vals = []
                for p in range(len(pairs)):
                    b_ref = in_refs[2 * p + 1]
                    rhs = b_ref[:, c0:c0 + cs] if mode == "nn" else b_ref[c0:c0 + cs, :]
                    vals.append(_dot(lhs[p], rhs, mode))
                epilogue(vals, [cols(r, c0, cs) for r in in_refs[2 * len(pairs):n_in]],
                         [cols(r, c0, cs) for r in out_refs[:n_out]], (i, j * len(col_chunks) + ci))
        else:
            @pl.when(k == 0)
            def _():
                for acc in accs:
                    acc[...] = jnp.zeros(acc.shape, F32)

            if steps is None:
                for p in range(len(pairs)):
                    contrib(p, accs[p])
            else:
                lo = 0
                for p, n in enumerate(steps):
                    pl.when((k >= lo) & (k < lo + n))(functools.partial(contrib, p, accs[0]))
                    lo += n

            @pl.when(k == nk - 1)
            def _():
                epilogue([acc[...] for acc in accs], in_refs[2 * len(pairs):n_in], out_refs[:n_out], (i, j))

        plumb.finish_at(in_refs, out_refs, scratch, (i == ni - 1) & (j == nj - 1) & (k == nk - 1))

    args, in_specs = [], []
    for a, a_spec, b, b_spec in pairs:
        args += [a, b]
        in_specs += [a_spec, b_spec]
    for e, e_spec in extras:
        args.append(e)
        in_specs.append(e_spec)
    res = pl.pallas_call(
        body,
        name=name,
        grid=grid,
        in_specs=in_specs + [ANY] * len(plumb.args),
        out_specs=[s for _, s in outs] + [ANY] * len(plumb.out_shape),
        out_shape=[o for o, _ in outs] + plumb.out_shape,
        scratch_shapes=[pltpu.VMEM(acc_shape, F32) for _ in range(n_acc)] + plumb.scratch,
        input_output_aliases=plumb.aliases,
        compiler_params=plumb.params(vmem_limit_bytes=VMEM_LIMIT_V7X),
    )(*args, *plumb.args)
    return plumb.deliver(res)


def _sds(shape, dtype):
    return jax.ShapeDtypeStruct(shape, dtype)


def _accumulate(ref, value, first):
    @pl.when(first)
    def _():
        ref[...] = value

    @pl.when(jnp.logical_not(first))
    def _():
        ref[...] += value


def _rms_fwd(name, x, gain, comms=()):
    t, d = x.shape
    tm = _pick(t, ROW_TILE, 8)
    steps = t // tm
    plumb = _CommPlumbing(comms, 2, 1, 0)
    n_all_in = 2 + len(plumb.args)
    n_all_out = 1 + len(plumb.out_shape)

    def body(*refs):
        x_ref, g_ref = refs[:2]
        y_ref = refs[n_all_in]
        comm_refs = (refs[:n_all_in], refs[n_all_in:n_all_in + n_all_out], refs[n_all_in + n_all_out:])
        i = pl.program_id(0)
        plumb.start_at(*comm_refs, i == 0)
        xv = x_ref[...]
        rstd = lax.rsqrt(jnp.mean(xv * xv, axis=-1, keepdims=True) + EPS)
        y_ref[...] = (xv * rstd * g_ref[...]).astype(BF16)
        plumb.finish_at(*comm_refs, i == steps - 1)

    res = pl.pallas_call(
        body, name=name, grid=(steps,),
        in_specs=[pl.BlockSpec((tm, d), lambda i: (i, 0)), pl.BlockSpec((1, d), lambda i: (0, 0))]
        + [ANY] * len(plumb.args),
        out_specs=[pl.BlockSpec((tm, d), lambda i: (i, 0))] + [ANY] * len(plumb.out_shape),
        out_shape=[_sds((t, d), BF16)] + plumb.out_shape,
        scratch_shapes=plumb.scratch,
        input_output_aliases=plumb.aliases,
        compiler_params=plumb.params(vmem_limit_bytes=VMEM_LIMIT_V7X),
    )(x, gain, *plumb.args)
    return plumb.deliver(res)[0]


def _rms_bwd_epilogue(accs, extras, outs, ij):
    x_ref, g_ref, r_ref = extras
    dh_ref, dhb_ref, dg_ref = outs
    dn = accs[0]
    xv = x_ref[...]
    rstd = lax.rsqrt(jnp.mean(xv * xv, axis=-1, keepdims=True) + EPS)
    xhat = xv * rstd
    gd = dn * g_ref[...]
    dx = rstd * (gd - xhat * jnp.mean(gd * xhat, axis=-1, keepdims=True))
    dh = r_ref[...] + dx
    dh_ref[...] = dh
    dhb_ref[...] = dh.astype(BF16)
    _accumulate(dg_ref, jnp.sum(dn * xhat, axis=0, keepdims=True), ij[0] == 0)


def _rms_bwd_io(x, gain, dres, tm):
    t, d = x.shape
    row = pl.BlockSpec((tm, d), lambda i, j, k: (i, 0))
    extras = [(x, row), (gain, pl.BlockSpec((1, d), lambda i, j, k: (0, 0))), (dres, row)]
    outs = [(_sds((t, d), F32), row), (_sds((t, d), BF16), row),
            (_sds((1, d), F32), pl.BlockSpec((1, d), lambda i, j, k: (0, 0)))]
    return extras, outs


def _ffn_fwd(tag, h, gain, wgu, wd, hooks):
    t, d = h.shape
    fs = wgu.shape[2]
    f = 2 * fs
    tm = _pick(t, ROW_TILE, 8)
    n = _rms_fwd(tag + "_norm", h, gain)

    def up_epilogue(accs, extras, outs, ij):
        g, u = accs
        gu_ref, a_ref = outs
        gu_ref[0] = g.astype(BF16)
        gu_ref[1] = u.astype(BF16)
        a_ref[...] = (g * jax.nn.sigmoid(g) * u).astype(BF16)

    a_spec = pl.BlockSpec((tm, d), lambda i, j, k: (i, 0))
    gu, a = _mm(
        tag + "_up", "nn", (t // tm, 2, 1),
        [(n, a_spec, wgu, pl.BlockSpec((None, d, fs), lambda i, j, k: (j, 0, 0))),
         (n, a_spec, wgu, pl.BlockSpec((None, d, fs), lambda i, j, k: (j + 2, 0, 0)))],
        [],
        [(_sds((2, t, f), BF16), pl.BlockSpec((2, tm, fs), lambda i, j, k: (0, i, j))),
         (_sds((t, f), BF16), pl.BlockSpec((tm, fs), lambda i, j, k: (i, j)))],
        (tm, fs), up_epilogue, comms=hooks.get("up", lambda: ())(), j_outer=True, chunked=True)

    def down_epilogue(accs, extras, outs, ij):
        outs[0][...] = extras[0][...] + 0.5 * accs[0]


    row = pl.BlockSpec((tm, d), lambda i, j, k: (i, 0))
    (h_new,) = _mm(
        tag + "_down", "nn", (t // tm, 1, 1),
        [(a, pl.BlockSpec((tm, f), lambda i, j, k: (i, 0)), wd, pl.BlockSpec((f, d), lambda i, j, k: (0, 0)))],
        [(h, row)], [(_sds((t, d), F32), row)], (tm, d), down_epilogue, comms=hooks.get("down", lambda: ())())
    return h_new, (n, gu, a)


def _ffn_bwd(tag, dh, dh_b, h, gain, wgu, wd, saved, hooks):
    n, gu, a = saved
    t, d = h.shape
    fs = wgu.shape[2]
    f = 2 * fs
    tm = _pick(t, ROW_TILE, 8)
    tk = _pick(t, ROW_TILE, 8)

    def dact_epilogue(accs, extras, outs, ij):
        da = 0.5 * accs[0]
        g = extras[0][0].astype(F32)
        u = extras[0][1].astype(F32)
        sg = jax.nn.sigmoid(g)
        outs[0][0] = (da * u * sg * (1.0 + g * (1.0 - sg))).astype(BF16)
        outs[0][1] = (da * g * sg).astype(BF16)

    gu_spec = pl.BlockSpec((2, tm, fs), lambda i, j, k: (0, i, j))
    (dgu,) = _mm(
        tag + "_dact", "nt", (t // tm, 2, 1),
        [(dh_b, pl.BlockSpec((tm, d), lambda i, j, k: (i, 0)), wd, pl.BlockSpec((fs, d), lambda i, j, k: (j, 0)))],
        [(gu, gu_spec)], [(_sds((2, t, f), BF16), gu_spec)], (tm, fs), dact_epilogue, j_outer=True, chunked=True,
        comms=hooks.get("dact", lambda: ())())

    def cast_epilogue(accs, extras, outs, ij):
        outs[0][...] = accs[0].astype(BF16)

    (dwgu,) = _mm(
        tag + "_dwgu", "tn", (1, 4, t // tk),
        [(n, pl.BlockSpec((tk, d), lambda i, j, k: (k, 0)),
          dgu, pl.BlockSpec((None, tk, fs), lambda i, j, k: (j // 2, k, j % 2)))],
        [], [(_sds((4, d, fs), BF16), pl.BlockSpec((None, d, fs), lambda i, j, k: (j, 0, 0)))], (d, fs), cast_epilogue,
        comms=hooks.get("dwgu", lambda: ())())

    def half_epilogue(accs, extras, outs, ij):
        outs[0][...] = (0.5 * accs[0]).astype(BF16)

    (dwd,) = _mm(
        tag + "_dwd", "tn", (2, 1, t // tk),
        [(a, pl.BlockSpec((tk, fs), lambda i, j, k: (k, i)), dh_b, pl.BlockSpec((tk, d), lambda i, j, k: (k, 0)))],
        [], [(_sds((f, d), BF16), pl.BlockSpec((fs, d), lambda i, j, k: (i, 0)))], (fs, d), half_epilogue,
        comms=hooks.get("dwd", lambda g: ())(dwgu))

    tmn = _pick(t, ROW_TILE, 8)
    extras, outs = _rms_bwd_io(h, gain, dh, tmn)
    dh_in, dh_in_b, dgain = _mm(
        tag + "_dnorm", "nt", (t // tmn, 1, 4),
        [(dgu, pl.BlockSpec((None, tmn, fs), lambda i, j, k: (k // 2, i, k % 2)),
          wgu, pl.BlockSpec((None, d, fs), lambda i, j, k: (k, 0, 0)))],
        extras, outs, (tmn, d), _rms_bwd_epilogue, comms=hooks.get("dnorm", lambda g, w: ())(dwgu, dwd))
    return dh_in, dh_in_b, dgain, dwgu, dwd


def _lane_lo(shape):
    return lax.broadcasted_iota(jnp.int32, shape, 1) < HEAD_DIM


def _pair_norm(xv, gain):
    lo = _lane_lo(xv.shape)
    x2 = xv * xv
    ms_lo = jnp.sum(jnp.where(lo, x2, 0.0), axis=-1, keepdims=True) * (1.0 / HEAD_DIM)
    ms_hi = jnp.sum(jnp.where(lo, 0.0, x2), axis=-1, keepdims=True) * (1.0 / HEAD_DIM)
    rstd = jnp.where(lo, lax.rsqrt(ms_lo + EPS), lax.rsqrt(ms_hi + EPS))
    xhat = xv * rstd
    return xhat * gain, xhat, rstd


def _pair_norm_bwd(dn, xhat, rstd, gain):
    lo = _lane_lo(dn.shape)
    gd = dn * gain
    t = gd * xhat
    m_lo = jnp.sum(jnp.where(lo, t, 0.0), axis=-1, keepdims=True) * (1.0 / HEAD_DIM)
    m_hi = jnp.sum(jnp.where(lo, 0.0, t), axis=-1, keepdims=True) * (1.0 / HEAD_DIM)
    dx = rstd * (gd - xhat * jnp.where(lo, m_lo, m_hi))
    return dx, jnp.sum(dn * xhat, axis=0, keepdims=True)


def _half(xv, hi):
    lo = _lane_lo(xv.shape)
    return jnp.where(lo, 0, xv) if hi else jnp.where(lo, xv, 0)


def _attn_window(i, prev):
    q0 = i * Q_BLOCK
    start = jnp.maximum(q0 - prev, 0)
    off = start - (q0 - prev)
    return pl.multiple_of(start, Q_BLOCK), pl.multiple_of(off, Q_BLOCK)


def _attn_specs(cfg, s, nq):
    kw = cfg["kw"]
    q_spec = pl.BlockSpec((Q_BLOCK, A_WIDTH), lambda b, i: (b * nq + i, cfg["qblk"]))
    k_spec = pl.BlockSpec((s, kw), lambda b, i: (b, cfg["kblk"]))
    v_spec = pl.BlockSpec((s, kw), lambda b, i: (b, cfg["vblk"]))
    return q_spec, k_spec, v_spec


def _const_spec(shape):
    return pl.BlockSpec(shape, lambda b, i: (0,) * len(shape))


KEY_CHUNK = 128


def _pair_bias(bias_t):
    wext = bias_t.shape[1]
    return jnp.transpose(bias_t.reshape(N_HEADS // 2, 2, wext, Q_BLOCK), (0, 2, 1, 3)).reshape(
        N_HEADS // 2, wext, 2 * Q_BLOCK)


def _unpair_bias(db2):
    wext = db2.shape[1]
    return jnp.transpose(db2.reshape(N_HEADS // 2, wext, 2, Q_BLOCK), (0, 2, 1, 3)).reshape(N_HEADS, wext, Q_BLOCK)


def _pair_rows(rows):
    two = rows.reshape(N_HEADS // 2, 2 * rows.shape[1])
    return jnp.broadcast_to(two[:, None, :], (N_HEADS // 2, 8, two.shape[1]))


def _sub_lo(shape):
    return lax.broadcasted_iota(jnp.int32, shape, 0) < HEAD_DIM


def _by_half(lo_row, hi_row, rows):
    return jnp.where(_sub_lo((rows, lo_row.shape[1])), lo_row, hi_row)


def _stack_pair(xn, jq, group):
    parts = []
    for hq in range(2):
        hk = ((2 * jq + hq) // group) % 2
        xm = _half(xn, hq)
        if hq != hk:
            xm = pltpu.roll(xm, HEAD_DIM, 1)
        parts.append(xm)
    return jnp.concatenate(parts, axis=0).astype(BF16)


def _place_transposed(blk, dst_ref, c, heads, group):
    bt = blk.T
    lo = _sub_lo(bt.shape)
    for h in heads:
        src_hi = ((h // group) % 2) == 1
        part = jnp.where(lo, 0.0, bt) if src_hi else jnp.where(lo, bt, 0.0)
        if src_hi != (h % 2 == 1):
            part = pltpu.roll(part, HEAD_DIM, 0)
        dst_ref[h, c] = part.astype(BF16)


def _attn_fwd(name, qkv, bias2, sink2, gq, gk, cfg, n_batch, comms=()):
    t = qkv.shape[0]
    s = t // n_batch
    nq = s // Q_BLOCK
    nkc = s // KEY_CHUNK
    prev, group, kw = cfg["prev"], cfg["group"], cfg["kw"]
    w = prev + Q_BLOCK
    n_chunks = w // KEY_CHUNK
    wext = bias2.shape[1]
    plumb = _CommPlumbing(comms, 7, 2, 4)
    n_all_in = 7 + len(plumb.args)
    n_all_out = 2 + len(plumb.out_shape)

    def body(*refs):
        q_ref, k_ref, v_ref, bias_ref, sink_ref, gq_ref, gk_ref = refs[:7]
        y_ref, lse_ref = refs[n_all_in:n_all_in + 2]
        kn_ref, vt_ref, s_ref, pst_ref = refs[n_all_in + n_all_out:n_all_in + n_all_out + 4]
        i = pl.program_id(1)
        comm_refs = (refs[:n_all_in], refs[n_all_in:n_all_in + n_all_out], refs[n_all_in + n_all_out:])
        plumb.start_at(*comm_refs, (pl.program_id(0) == 0) & (i == 0))

        @pl.when(i == 0)
        def _():
            for jk in range(kw // PAIR):
                cols = pl.ds(jk * PAIR, PAIR)
                heads = [h for h in range(N_HEADS) if (h // group) // 2 == jk]
                kn, _, _ = _pair_norm(k_ref[:, cols].astype(F32), gk_ref[...])
                kn_ref[:, cols] = kn.astype(BF16)
                for c in range(nkc):
                    _place_transposed(v_ref[pl.ds(c * KEY_CHUNK, KEY_CHUNK), cols].astype(F32), vt_ref, c, heads, group)

        start, off = _attn_window(i, prev)
        c0 = start // KEY_CHUNK
        sub8 = lax.broadcasted_iota(jnp.int32, (N_HEADS, Q_BLOCK), 0)
        lse = jnp.zeros((N_HEADS, Q_BLOCK), F32)
        for jq in range(N_HEADS // 2):
            kcols = pl.ds((((2 * jq) // group) // 2) * PAIR, PAIR)
            qn, _, _ = _pair_norm(q_ref[:, pl.ds(jq * PAIR, PAIR)].astype(F32), gq_ref[...])
            qs = _stack_pair(qn * ATTN_SCALE, jq, group)
            s_ref[...] = _dot(kn_ref[pl.ds(start, w), kcols], qs, "nt")
            m = sink_ref[jq, 0:1, :]
            for c in range(n_chunks):
                r = pl.ds(c * KEY_CHUNK, KEY_CHUNK)
                s2 = s_ref[r, :] + bias_ref[jq, pl.ds(off + c * KEY_CHUNK, KEY_CHUNK), :]
                s_ref[r, :] = s2
                m = jnp.maximum(m, jnp.max(s2, axis=0, keepdims=True))
            l = jnp.exp(sink_ref[jq, 0:1, :] - m)
            for c in range(n_chunks):
                p = jnp.exp(s_ref[pl.ds(c * KEY_CHUNK, KEY_CHUNK), :] - m)
                l = l + jnp.sum(p, axis=0, keepdims=True)
                pst_ref[pl.ds(2 * c * KEY_CHUNK, KEY_CHUNK), :] = p[:, :Q_BLOCK].astype(BF16)
                pst_ref[pl.ds((2 * c + 1) * KEY_CHUNK, KEY_CHUNK), :] = p[:, Q_BLOCK:].astype(BF16)
            vl = jnp.concatenate([vt_ref[2 * jq + hq, c0 + c] for c in range(n_chunks) for hq in range(2)], axis=1)
            ot = _dot(vl, pst_ref[...], "nn")
            inv = 1.0 / l
            ot = ot * _by_half(inv[:, :Q_BLOCK], inv[:, Q_BLOCK:], PAIR)
            y_ref[:, pl.ds(jq * PAIR, PAIR)] = ot.T.astype(BF16)
            lse2 = m + jnp.log(l)
            lse = jnp.where(sub8 == 2 * jq, lse2[:, :Q_BLOCK], lse)
            lse = jnp.where(sub8 == 2 * jq + 1, lse2[:, Q_BLOCK:], lse)
        lse_ref[...] = lse
        plumb.finish_at(*comm_refs, (pl.program_id(0) == n_batch - 1) & (i == nq - 1))

    q_spec, k_spec, v_spec = _attn_specs(cfg, s, nq)
    res = pl.pallas_call(
        body, name=name, grid=(n_batch, nq),
        in_specs=[q_spec, k_spec, v_spec, _const_spec((N_HEADS // 2, wext, 2 * Q_BLOCK)),
                  _const_spec((N_HEADS // 2, 8, 2 * Q_BLOCK)), _const_spec((1, PAIR)), _const_spec((1, PAIR))]
        + [ANY] * len(plumb.args),
        out_specs=[pl.BlockSpec((Q_BLOCK, A_WIDTH), lambda b, i: (b * nq + i, 0)),
                   pl.BlockSpec((None, N_HEADS, Q_BLOCK), lambda b, i: (b * nq + i, 0, 0))]
        + [ANY] * len(plumb.out_shape),
        out_shape=[_sds((t, A_WIDTH), BF16), _sds((t // Q_BLOCK, N_HEADS, Q_BLOCK), F32)] + plumb.out_shape,
        scratch_shapes=[pltpu.VMEM((s, kw), BF16), pltpu.VMEM((N_HEADS, nkc, PAIR, KEY_CHUNK), BF16),
                        pltpu.VMEM((w, 2 * Q_BLOCK), F32), pltpu.VMEM((2 * w, Q_BLOCK), BF16)] + plumb.scratch,
        input_output_aliases=plumb.aliases,
        compiler_params=plumb.params(vmem_limit_bytes=VMEM_LIMIT_V7X),
    )(qkv, qkv, qkv, bias2, sink2, gq, gk, *plumb.args)
    return plumb.deliver(res)


def _attn_bwd(name, qkv, bias2, sink2, gq, gk, y, dy, lse, cfg, n_batch, want_dbias, comms=()):
    t = qkv.shape[0]
    s = t // n_batch
    nq = s // Q_BLOCK
    nkc = s // KEY_CHUNK
    prev, group, kw = cfg["prev"], cfg["group"], cfg["kw"]
    w = prev + Q_BLOCK
    n_chunks = w // KEY_CHUNK
    wext = bias2.shape[1]
    plumb = _CommPlumbing(comms, 10, 7, 9)
    n_all_in = 10 + len(plumb.args)
    n_all_out = 7 + len(plumb.out_shape)

    def body(*refs):
        q_ref, k_ref, v_ref, bias_ref, sink_ref, gq_ref, gk_ref, y_ref, dy_ref, lse_ref = refs[:10]
        dq_ref, dk_ref, dv_ref, db_ref, dsink_ref, dgq_ref, dgk_ref = refs[n_all_in:n_all_in + 7]
        kn_ref, knt_ref, dkn_ref, dvs_ref, s_ref, dp_ref, pb_ref, dsb_ref, dst_ref = \
            refs[n_all_in + n_all_out:n_all_in + n_all_out + 9]
        b = pl.program_id(0)
        i = pl.program_id(1)
        first = (b == 0) & (i == 0)
        comm_refs = (refs[:n_all_in], refs[n_all_in:n_all_in + n_all_out], refs[n_all_in + n_all_out:])
        plumb.start_at(*comm_refs, first)

        @pl.when(i == 0)
        def _():
            for jk in range(kw // PAIR):
                cols = pl.ds(jk * PAIR, PAIR)
                heads = [h for h in range(N_HEADS) if (h // group) // 2 == jk]
                for c in range(nkc):
                    rows = pl.ds(c * KEY_CHUNK, KEY_CHUNK)
                    kn, _, _ = _pair_norm(k_ref[rows, cols].astype(F32), gk_ref[...])
                    kn_ref[rows, cols] = kn.astype(BF16)
                    _place_transposed(kn, knt_ref, c, heads, group)
            dkn_ref[...] = jnp.zeros(dkn_ref.shape, F32)
            dvs_ref[...] = jnp.zeros(dvs_ref.shape, F32)

        @pl.when(first)
        def _():
            db_ref[...] = jnp.zeros(db_ref.shape, F32)
            dsink_ref[...] = jnp.zeros(dsink_ref.shape, F32)
            dgq_ref[...] = jnp.zeros(dgq_ref.shape, F32)
            dgk_ref[...] = jnp.zeros(dgk_ref.shape, F32)

        start, off = _attn_window(i, prev)
        c0 = start // KEY_CHUNK
        for jq in range(N_HEADS // 2):
            cols = pl.ds(jq * PAIR, PAIR)
            kcols = pl.ds((((2 * jq) // group) // 2) * PAIR, PAIR)
            qn, q_hat, q_rstd = _pair_norm(q_ref[:, cols].astype(F32), gq_ref[...])
            qs = _stack_pair(qn * ATTN_SCALE, jq, group)
            do_pair = dy_ref[:, cols].astype(F32)
            dos = _stack_pair(do_pair, jq, group)
            prod_t = (do_pair * y_ref[:, cols].astype(F32)).T
            lo = _sub_lo(prod_t.shape)
            delta2 = jnp.concatenate([jnp.sum(jnp.where(lo, prod_t, 0.0), axis=0, keepdims=True),
                                      jnp.sum(jnp.where(lo, 0.0, prod_t), axis=0, keepdims=True)], axis=1)
            lse2 = jnp.concatenate([lse_ref[2 * jq:2 * jq + 1, :], lse_ref[2 * jq + 1:2 * jq + 2, :]], axis=1)
            dsk = -jnp.exp(sink_ref[jq, 0:1, :] - lse2) * delta2
            dsink_ref[2 * jq:2 * jq + 1, :] += dsk[:, :Q_BLOCK]
            dsink_ref[2 * jq + 1:2 * jq + 2, :] += dsk[:, Q_BLOCK:]
            rows_w = pl.ds(start, w)
            s_ref[...] = _dot(kn_ref[rows_w, kcols], qs, "nt")
            dp_ref[...] = _dot(v_ref[rows_w, kcols], dos, "nt")
            for c in range(n_chunks):
                r = pl.ds(c * KEY_CHUNK, KEY_CHUNK)
                brows = pl.ds(off + c * KEY_CHUNK, KEY_CHUNK)
                p = jnp.exp(s_ref[r, :] + bias_ref[jq, brows, :] - lse2)
                ds = p * (dp_ref[r, :] - delta2)
                if want_dbias:
                    db_ref[jq, brows, :] += ds
                ds_b = ds.astype(BF16)
                pb_ref[r, :] = p.astype(BF16)
                dsb_ref[r, :] = ds_b
                dst_ref[pl.ds(2 * c * KEY_CHUNK, KEY_CHUNK), :] = ds_b[:, :Q_BLOCK]
                dst_ref[pl.ds((2 * c + 1) * KEY_CHUNK, KEY_CHUNK), :] = ds_b[:, Q_BLOCK:]
            dkn_ref[rows_w, kcols] += _dot(dsb_ref[...], qs, "nn")
            dvs_ref[rows_w, kcols] += _dot(pb_ref[...], dos, "nn")
            kl = jnp.concatenate([knt_ref[2 * jq + hq, c0 + c] for c in range(n_chunks) for hq in range(2)], axis=1)
            dqt = _dot(kl, dst_ref[...], "nn")
            dq_raw, dg = _pair_norm_bwd(dqt.T * ATTN_SCALE, q_hat, q_rstd, gq_ref[...])
            dq_ref[:, cols] = dq_raw.astype(BF16)
            dgq_ref[...] += dg

        @pl.when(i == nq - 1)
        def _():
            for jk in range(kw // PAIR):
                kcols = pl.ds(jk * PAIR, PAIR)
                _, k_hat, k_rstd = _pair_norm(k_ref[:, kcols].astype(F32), gk_ref[...])
                dk_raw, dg = _pair_norm_bwd(dkn_ref[:, kcols], k_hat, k_rstd, gk_ref[...])
                dk_ref[:, kcols] = dk_raw.astype(BF16)
                dgk_ref[...] += dg
            dv_ref[...] = dvs_ref[...].astype(BF16)

        plumb.finish_at(*comm_refs, (b == n_batch - 1) & (i == nq - 1))

    q_spec, k_spec, v_spec = _attn_specs(cfg, s, nq)
    row = pl.BlockSpec((Q_BLOCK, A_WIDTH), lambda b, i: (b * nq + i, 0))
    kv_out = pl.BlockSpec((s, kw), lambda b, i: (b, 0))
    pair_bias = _const_spec((N_HEADS // 2, wext, 2 * Q_BLOCK))
    res = pl.pallas_call(
        body, name=name, grid=(n_batch, nq),
        in_specs=[q_spec, k_spec, v_spec, pair_bias, _const_spec((N_HEADS // 2, 8, 2 * Q_BLOCK)),
                  _const_spec((1, PAIR)), _const_spec((1, PAIR)), row, row,
                  pl.BlockSpec((None, N_HEADS, Q_BLOCK), lambda b, i: (b * nq + i, 0, 0))] + [ANY] * len(plumb.args),
        out_specs=[row, kv_out, kv_out, pair_bias, _const_spec((N_HEADS, 128)),
                   _const_spec((1, PAIR)), _const_spec((1, PAIR))] + [ANY] * len(plumb.out_shape),
        out_shape=[_sds((t, A_WIDTH), BF16), _sds((t, kw), BF16), _sds((t, kw), BF16),
                   _sds((N_HEADS // 2, wext, 2 * Q_BLOCK), F32), _sds((N_HEADS, 128), F32),
                   _sds((1, PAIR), F32), _sds((1, PAIR), F32)] + plumb.out_shape,
        scratch_shapes=[pltpu.VMEM((s, kw), BF16), pltpu.VMEM((N_HEADS, nkc, PAIR, KEY_CHUNK), BF16),
                        pltpu.VMEM((s, kw), F32), pltpu.VMEM((s, kw), F32),
                        pltpu.VMEM((w, 2 * Q_BLOCK), F32), pltpu.VMEM((w, 2 * Q_BLOCK), F32),
                        pltpu.VMEM((w, 2 * Q_BLOCK), BF16), pltpu.VMEM((w, 2 * Q_BLOCK), BF16),
                        pltpu.VMEM((2 * w, Q_BLOCK), BF16)] + plumb.scratch,
        input_output_aliases=plumb.aliases,
        compiler_params=plumb.params(vmem_limit_bytes=VMEM_LIMIT_V7X),
    )(qkv, qkv, qkv, bias2, sink2, gq, gk, y, dy, lse, *plumb.args)
    return plumb.deliver(res)


def _band_tables(prev_chunks):
    prev = prev_chunks * CHUNK
    wext = 2 * prev + Q_BLOCK
    jj = np.arange(wext)[:, None]
    ii = np.arange(Q_BLOCK)[None, :]
    dist = prev + ii - jj
    rel_chunk = (prev // CHUNK + ii // CHUNK) - jj // CHUNK
    allowed = (rel_chunk >= 0) & (rel_chunk <= prev_chunks)
    return dist, allowed


def _alibi_slopes():
    return np.array([2.0 ** (-8.0 * (h + 1) / N_HEADS) for h in range(N_HEADS)], dtype=np.float32)


def _diag_onehot(prev, wext):
    n_diag = wext + Q_BLOCK - 1
    idx = np.clip(prev + Q_BLOCK - 1 - np.arange(n_diag), -A_MAX_REL, A_MAX_REL) + A_MAX_REL
    onehot = np.zeros((n_diag, 2 * A_MAX_REL + 1), np.float32)
    onehot[np.arange(n_diag), idx] = 1.0
    return onehot


def _bias_a(rel_bias):
    prev = A_PREV_CHUNKS * CHUNK
    _, allowed = _band_tables(A_PREV_CHUNKS)
    wext = allowed.shape[0]
    n_diag = wext + Q_BLOCK - 1
    seq = jnp.dot(rel_bias, jnp.asarray(_diag_onehot(prev, wext).T), precision=lax.Precision.HIGHEST)
    seq = jnp.pad(seq, ((0, 0), (0, 1)))
    rows = jnp.broadcast_to(seq[:, None, :], (N_HEADS, Q_BLOCK, n_diag + 1)).reshape(N_HEADS, -1)
    skew = rows[:, :Q_BLOCK * n_diag].reshape(N_HEADS, Q_BLOCK, n_diag)
    tile = jnp.transpose(skew[:, :, Q_BLOCK - 1:Q_BLOCK - 1 + wext], (0, 2, 1))
    return jnp.where(jnp.asarray(allowed)[None], tile, NEG_INF)


def _bias_b():
    dist, allowed = _band_tables(B_PREV_CHUNKS)
    bias = -_alibi_slopes()[:, None, None] * np.abs(dist).astype(np.float32)[None]
    return jnp.asarray(np.where(allowed[None], bias, np.float32(NEG_INF)).astype(np.float32))


def _rel_bias_grad(db_t):
    prev = A_PREV_CHUNKS * CHUNK
    wext = db_t.shape[1]
    n_diag = wext + Q_BLOCK - 1
    wp = n_diag + Q_BLOCK - 1
    xp = jnp.pad(jnp.transpose(db_t, (0, 2, 1)), ((0, 0), (0, 0), (Q_BLOCK - 1, Q_BLOCK - 1)))
    flat = jnp.pad(xp.reshape(N_HEADS, Q_BLOCK * wp), ((0, 0), (0, Q_BLOCK)))
    skew = flat.reshape(N_HEADS, Q_BLOCK, wp + 1)[:, :, :n_diag]
    diag = jnp.sum(skew, axis=1)
    return jnp.dot(diag, jnp.asarray(_diag_onehot(prev, wext)), precision=lax.Precision.HIGHEST)


def _ew(name, fn, ins, out_dtypes):
    r, c = ins[0].shape
    rb = _pick(r, max(16, (1 << 19) // c), 16)
    spec = pl.BlockSpec((rb, c), lambda i: (i, 0))

    def body(*refs):
        vals = fn(*[ref[...] for ref in refs[:len(ins)]])
        for ref, val in zip(refs[len(ins):], vals):
            ref[...] = val.astype(ref.dtype)

    return pl.pallas_call(
        body, name=name, grid=(r // rb,), in_specs=[spec] * len(ins), out_specs=[spec] * len(out_dtypes),
        out_shape=[_sds((r, c), dt) for dt in out_dtypes], compiler_params=_params(),
    )(*ins)


def _cast_into_slot(name, w, chip):
    r, c = w.shape
    rb = _pick(r, max(16, (1 << 19) // c), 16)

    def body(chip_ref, w_ref, o_ref):
        o_ref[...] = w_ref[...].astype(BF16)

    return pl.pallas_call(
        body, name=name,
        grid_spec=pltpu.PrefetchScalarGridSpec(
            num_scalar_prefetch=1, grid=(r // rb,),
            in_specs=[pl.BlockSpec((rb, c), lambda i, chip_ref: (i, 0))],
            out_specs=pl.BlockSpec((None, rb, c), lambda i, chip_ref: (chip_ref[0], i, 0))),
        out_shape=_sds((N_CHIPS, r, c), BF16), compiler_params=_params(),
    )(chip, w)


def _adamw_math(w, g, m, v):
    m = ADAM_B1 * m + (1.0 - ADAM_B1) * g
    v = ADAM_B2 * v + (1.0 - ADAM_B2) * (g * g)
    m_hat = m / (1.0 - ADAM_B1 ** ADAM_STEP)
    v_hat = v / (1.0 - ADAM_B2 ** ADAM_STEP)
    delta = -ADAM_LR * (m_hat / (jnp.sqrt(v_hat) + ADAM_EPS) + ADAM_WD * w)
    return delta, m, v


def _adamw_terms(name, terms, w, m, v):
    r, c = w.shape
    hr = r // 2
    rb = _pick(hr, max(16, (1 << 19) // c), 16)
    nb = hr // rb

    def body(t_ref, w_ref, m_ref, v_ref, g_ref, d_ref, nm_ref, nv_ref):
        g = t_ref[0].astype(F32)
        for k in range(1, N_CHIPS):
            g = g + t_ref[k].astype(F32)
        delta, nm, nv = _adamw_math(w_ref[...], g, m_ref[...], v_ref[...])
        g_ref[...] = g
        d_ref[...] = delta
        nm_ref[...] = nm
        nv_ref[...] = nv

    spec = pl.BlockSpec((rb, c), lambda h, i: (h * nb + i, 0))
    return pl.pallas_call(
        body, name=name, grid=(2, nb),
        in_specs=[pl.BlockSpec((None, N_CHIPS, rb, c), lambda h, i: (h, 0, i, 0)), spec, spec, spec],
        out_specs=[spec] * 4, out_shape=[_sds((r, c), F32)] * 4, compiler_params=_params(),
    )(terms, w, m, v)


def _mesh_place():
    x, y, c = lax.axis_index("x"), lax.axis_index("y"), lax.axis_index("c")
    chips = [(x, 1 - y), (1 - x, y), (1 - x, 1 - y)]
    return x, y, c, chips


def _all_gather_weights(bufs):
    n = len(bufs)

    def body(*refs):
        outs = refs[n:2 * n]
        ici_send, ici_recv, d2d_send, d2d_recv = refs[2 * n:]
        x, y, c, chips = _mesh_place()
        me = 2 * x + y
        sibling = (x, y, 1 - c)
        barrier = pltpu.get_barrier_semaphore()
        for peer in [sibling] + [(tx, ty, c) for tx, ty in chips]:
            pl.semaphore_signal(barrier, inc=1, device_id=peer, device_id_type=MESH)
        pl.semaphore_wait(barrier, N_CHIPS)
        sent = []
        for wi in range(n):
            for k, (tx, ty) in enumerate(chips):
                own = outs[wi].at[me, c]
                cp = pltpu.make_async_remote_copy(
                    src_ref=own, dst_ref=own, send_sem=ici_send.at[wi * 3 + k], recv_sem=ici_recv.at[wi * 3 + k],
                    device_id=(tx, ty, c), device_id_type=MESH)
                cp.start()
                sent.append(cp)
        passed = []
        for wi in range(n):
            for k, (tx, ty) in enumerate(chips):
                slab = outs[wi].at[2 * tx + ty, c]
                pltpu.make_async_remote_copy(
                    src_ref=slab, dst_ref=slab, send_sem=ici_send.at[wi * 3 + k], recv_sem=ici_recv.at[wi * 3 + k],
                    device_id=(tx, ty, c), device_id_type=MESH).wait_recv()
                fw = pltpu.make_async_remote_copy(
                    src_ref=slab, dst_ref=slab, send_sem=d2d_send.at[wi * 3 + k], recv_sem=d2d_recv.at[wi * 3 + k],
                    device_id=sibling, device_id_type=MESH)
                fw.start()
                passed.append(fw)
        for wi in range(n):
            for k, (tx, ty) in enumerate(chips):
                slab = outs[wi].at[2 * tx + ty, 1 - c]
                pltpu.make_async_remote_copy(
                    src_ref=slab, dst_ref=slab, send_sem=d2d_send.at[wi * 3 + k], recv_sem=d2d_recv.at[wi * 3 + k],
                    device_id=sibling, device_id_type=MESH).wait_recv()
        for cp in sent + passed:
            cp.wait_send()

    return pl.pallas_call(
        body, name="all_gather_weights",
        in_specs=[ANY] * n, out_specs=[ANY] * n,
        out_shape=[_sds(g.shape, g.dtype) for g in bufs],
        scratch_shapes=[pltpu.SemaphoreType.DMA((3 * n,))] * 4,
        input_output_aliases={i: i for i in range(n)},
        compiler_params=pltpu.CompilerParams(collective_id=COLLECTIVE_IDS[("chips", "sibling")]),
    )(*bufs)


def _run_comms(name, comms):
    plumb = _CommPlumbing(comms, 0, 0, 0)
    n_in, n_out = len(plumb.args), len(plumb.out_shape)

    def body(*refs):
        parts = []
        i0, o0, s0 = 0, n_in, n_in + n_out
        for cm in plumb.comms:
            parts.append((refs[i0:i0 + len(cm.ins)], refs[o0:o0 + len(cm.outs)], refs[s0:s0 + len(cm.sems)]))
            i0 += len(cm.ins)
            o0 += len(cm.outs)
            s0 += len(cm.sems)
        plumb.handshake()
        for cm, part in zip(plumb.comms, parts):
            cm.start(*part)
        for cm, part in zip(plumb.comms, parts):
            cm.finish(*part)

    res = pl.pallas_call(
        body, name=name, in_specs=[ANY] * n_in, out_specs=[ANY] * n_out, out_shape=plumb.out_shape,
        scratch_shapes=plumb.scratch, input_output_aliases=plumb.aliases, compiler_params=plumb.params(),
    )(*plumb.args)
    plumb.deliver(res)


def _gather_ici(bufs):
    n = len(bufs)

    def copies(outs, sems):
        send_sem, recv_sem = sems
        x, y, c, chips = _mesh_place()
        me = 2 * x + y
        sends, recvs = [], []
        for wi in range(n):
            for k, (tx, ty) in enumerate(chips):
                sems_k = dict(send_sem=send_sem.at[wi * 3 + k], recv_sem=recv_sem.at[wi * 3 + k],
                              device_id=(tx, ty, c), device_id_type=MESH)
                own = outs[wi].at[me, c]
                sends.append(pltpu.make_async_remote_copy(src_ref=own, dst_ref=own, **sems_k))
                slab = outs[wi].at[2 * tx + ty, c]
                recvs.append(pltpu.make_async_remote_copy(src_ref=slab, dst_ref=slab, **sems_k))
        return sends, recvs

    def start(ins, outs, sems):
        for cp in copies(outs, sems)[0]:
            cp.start()

    def finish(ins, outs, sems):
        sends, recvs = copies(outs, sems)
        for cp in recvs:
            cp.wait_recv()
        for cp in sends:
            cp.wait_send()

    return _Comm("chips", bufs, [_sds(g.shape, g.dtype) for g in bufs], {i: i for i in range(n)},
                 [pltpu.SemaphoreType.DMA((3 * n,)), pltpu.SemaphoreType.DMA((3 * n,))], start, finish)


def _gather_d2d(gathered):
    n = len(gathered)

    def copies(outs, sems):
        send_sem, recv_sem = sems
        x, y, c, chips = _mesh_place()
        sends, recvs = [], []
        for wi in range(n):
            for k, (tx, ty) in enumerate(chips):
                sems_k = dict(send_sem=send_sem.at[wi * 3 + k], recv_sem=recv_sem.at[wi * 3 + k],
                              device_id=(x, y, 1 - c), device_id_type=MESH)
                mine = outs[wi].at[2 * tx + ty, c]
                theirs = outs[wi].at[2 * tx + ty, 1 - c]
                sends.append(pltpu.make_async_remote_copy(src_ref=mine, dst_ref=mine, **sems_k))
                recvs.append(pltpu.make_async_remote_copy(src_ref=theirs, dst_ref=theirs, **sems_k))
        return sends, recvs

    def start(ins, outs, sems):
        for cp in copies(outs, sems)[0]:
            cp.start()

    def finish(ins, outs, sems):
        sends, recvs = copies(outs, sems)
        for cp in recvs:
            cp.wait_recv()
        for cp in sends:
            cp.wait_send()

    return _Comm("sibling", gathered, [_sds(g.shape, g.dtype) for g in gathered], {i: i for i in range(n)},
                 [pltpu.SemaphoreType.DMA((3 * n,)), pltpu.SemaphoreType.DMA((3 * n,))], start, finish)


def _exchange_halves(grads):
    n = len(grads)

    def copies(ins, outs, sems):
        send_sem, recv_sem = sems
        x, y, c, _ = _mesh_place()
        return [pltpu.make_async_remote_copy(
            src_ref=ins[wi].at[t, 1 - c], dst_ref=outs[wi].at[t],
            send_sem=send_sem.at[wi * N_CHIPS + t], recv_sem=recv_sem.at[wi * N_CHIPS + t],
            device_id=(x, y, 1 - c), device_id_type=MESH) for wi in range(n) for t in range(N_CHIPS)]

    def start(ins, outs, sems):
        for cp in copies(ins, outs, sems):
            cp.start()

    def finish(ins, outs, sems):
        for cp in copies(ins, outs, sems):
            cp.wait()

    return _Comm("sibling", grads, [_sds((N_CHIPS,) + g.shape[2:], g.dtype) for g in grads], {},
                 [pltpu.SemaphoreType.DMA((N_CHIPS * n,)), pltpu.SemaphoreType.DMA((N_CHIPS * n,))], start, finish)


def _scatter_ici(sums):
    n = len(sums)

    def copies(ins, outs, sems):
        local_sem, send_sem, recv_sem = sems
        x, y, c, chips = _mesh_place()
        me = 2 * x + y
        local, sends, recvs = [], [], []
        for wi in range(n):
            local.append(pltpu.make_async_copy(ins[wi].at[me], outs[wi].at[c, 0], local_sem.at[wi]))
            for k, (tx, ty) in enumerate(chips):
                sems_k = dict(send_sem=send_sem.at[wi * 3 + k], recv_sem=recv_sem.at[wi * 3 + k],
                              device_id=(tx, ty, c), device_id_type=MESH)
                land = outs[wi].at[c, k + 1]
                sends.append(pltpu.make_async_remote_copy(src_ref=ins[wi].at[2 * tx + ty], dst_ref=land, **sems_k))
                recvs.append(pltpu.make_async_remote_copy(src_ref=land, dst_ref=land, **sems_k))
        return local, sends, recvs

    def start(ins, outs, sems):
        local, sends, _ = copies(ins, outs, sems)
        for cp in local + sends:
            cp.start()

    def finish(ins, outs, sems):
        local, sends, recvs = copies(ins, outs, sems)
        for cp in local:
            cp.wait()
        for cp in recvs:
            cp.wait_recv()
        for cp in sends:
            cp.wait_send()

    return _Comm("chips", sums, [_sds((2, N_CHIPS) + s.shape[1:], s.dtype) for s in sums], {},
                 [pltpu.SemaphoreType.DMA((n,)), pltpu.SemaphoreType.DMA((3 * n,)), pltpu.SemaphoreType.DMA((3 * n,))],
                 start, finish)


def _scatter_d2d(terms):
    n = len(terms)

    def copies(outs, sems):
        send_sem, recv_sem = sems
        x, y, c, _ = _mesh_place()
        sends, recvs = [], []
        for wi in range(n):
            sems_w = dict(send_sem=send_sem.at[wi], recv_sem=recv_sem.at[wi],
                          device_id=(x, y, 1 - c), device_id_type=MESH)
            sends.append(pltpu.make_async_remote_copy(src_ref=outs[wi].at[c], dst_ref=outs[wi].at[c], **sems_w))
            recvs.append(pltpu.make_async_remote_copy(src_ref=outs[wi].at[1 - c], dst_ref=outs[wi].at[1 - c], **sems_w))
        return sends, recvs

    def start(ins, outs, sems):
        for cp in copies(outs, sems)[0]:
            cp.start()

    def finish(ins, outs, sems):
        sends, recvs = copies(outs, sems)
        for cp in recvs:
            cp.wait_recv()
        for cp in sends:
            cp.wait_send()

    return _Comm("sibling", terms, [_sds(t.shape, t.dtype) for t in terms], {i: i for i in range(n)},
                 [pltpu.SemaphoreType.DMA((n,)), pltpu.SemaphoreType.DMA((n,))], start, finish)


def _chip_sum(name, grad, got, core):
    _, _, hr, c = grad.shape
    rb = _pick(hr, max(16, (1 << 19) // c), 16)

    def body(core_ref, a_ref, b_ref, o_ref):
        o_ref[...] = (a_ref[...].astype(F32) + b_ref[...].astype(F32)).astype(BF16)

    out_spec = pl.BlockSpec((None, rb, c), lambda t, i, core_ref: (t, i, 0))
    return pl.pallas_call(
        body, name=name,
        grid_spec=pltpu.PrefetchScalarGridSpec(
            num_scalar_prefetch=1, grid=(N_CHIPS, hr // rb),
            in_specs=[pl.BlockSpec((None, None, rb, c), lambda t, i, core_ref: (t, core_ref[0], i, 0)), out_spec],
            out_specs=out_spec),
        out_shape=_sds((N_CHIPS, hr, c), BF16), compiler_params=_params(),
    )(core, grad, got)


def _all_reduce_small(pack):
    r = pack.shape[0]

    def body(p_ref, o_ref, land_ref, send_sem, recv_sem):
        x, y, c, _ = _mesh_place()
        me = 4 * x + 2 * y + c
        flips = [(k >> 2 & 1, k >> 1 & 1, k & 1) for k in range(1, N_DEV)]

        def peer(fx, fy, fc):
            return (1 - x if fx else x, 1 - y if fy else y, 1 - c if fc else c)

        land_ref[me] = p_ref[...]
        sent = []
        for k, flip in enumerate(flips):
            cp = pltpu.make_async_remote_copy(
                src_ref=p_ref, dst_ref=land_ref.at[me], send_sem=send_sem.at[k], recv_sem=recv_sem.at[k],
                device_id=peer(*flip), device_id_type=MESH)
            cp.start()
            sent.append(cp)
        for k, flip in enumerate(flips):
            px, py, pc = peer(*flip)
            slot = land_ref.at[4 * px + 2 * py + pc]
            pltpu.make_async_remote_copy(
                src_ref=slot, dst_ref=slot, send_sem=send_sem.at[k], recv_sem=recv_sem.at[k],
                device_id=(px, py, pc), device_id_type=MESH).wait_recv()
        total = land_ref[0]
        for d in range(1, N_DEV):
            total = total + land_ref[d]
        o_ref[...] = total
        for cp in sent:
            cp.wait_send()

    vmem = pl.BlockSpec(memory_space=pltpu.VMEM)
    return pl.pallas_call(
        body, name="all_reduce_small", in_specs=[vmem], out_specs=vmem, out_shape=_sds((r, 128), F32),
        scratch_shapes=[pltpu.VMEM((N_DEV, r, 128), F32), pltpu.SemaphoreType.DMA((N_DEV - 1,)),
                        pltpu.SemaphoreType.DMA((N_DEV - 1,))],
    )(pack)


PACK_TILE = 8 * 128


def _pack(items):
    rows, i = [], 0
    while i < len(items):
        j = i
        while j < len(items) and items[j].size == items[i].size:
            j += 1
        group = jnp.stack([it.reshape(-1).astype(F32) for it in items[i:j]])
        rows.append(jnp.pad(group, ((0, 0), (0, -group.shape[1] % PACK_TILE))).reshape(-1, 128))
        i = j
    return jnp.concatenate(rows, axis=0)


def _unpack(pack, shapes):
    out, row = [], 0
    for shp in shapes:
        size = int(np.prod(shp))
        nrow = -(-size // PACK_TILE) * (PACK_TILE // 128)
        out.append(pack[row:row + nrow].reshape(-1)[:size].reshape(shp))
        row += nrow
    return out


BIG = ["ffn1_w_gu", "ffn1_w_down", "w_in", "w_gate", "w_proj_a", "w_proj_b", "w_out",
       "ffn2_w_gu", "ffn2_w_down", "w_ple_gate", "w_ple_proj"]
SMALL = ["ffn1_norm", "mix_norm", "ffn2_norm", "ple_norm", "a_q_norm", "a_k_norm", "b_q_norm", "b_k_norm",
         "a_rel_bias", "b_sinks"]
WEIGHTS = ["ffn1_norm", "ffn1_w_gu", "ffn1_w_down", "mix_norm", "w_in", "a_q_norm", "a_k_norm", "a_rel_bias",
           "b_q_norm", "b_k_norm", "b_sinks", "w_gate", "w_proj_a", "w_proj_b", "w_out", "ffn2_norm",
           "ffn2_w_gu", "ffn2_w_down", "ple_norm", "w_ple_gate", "w_ple_proj"]
ATTN_A = dict(prev=A_PREV_CHUNKS * CHUNK, group=1, kw=A_WIDTH, qblk=0, kblk=1, vblk=2)
ATTN_B = dict(prev=B_PREV_CHUNKS * CHUNK, group=N_HEADS // B_KV_HEADS, kw=B_KV_WIDTH, qblk=3,
              kblk=4 * A_WIDTH // B_KV_WIDTH, vblk=4 * A_WIDTH // B_KV_WIDTH + 1)


def _cast_epilogue(accs, extras, outs, ij):
    for acc, out in zip(accs, outs):
        out[...] = acc.astype(out.dtype)


GATHER_FIRST = ["ffn1_w_gu", "ffn1_w_down"]
ROW_SHARDED = ("ffn1_w_down", "ffn2_w_down", "w_out", "w_ple_gate")


def _slotted(name, grad):
    if name == "w_in":
        rows, cols = grad.shape
        grad = jnp.transpose(grad.reshape(rows, N_CHIPS, cols // N_CHIPS), (1, 0, 2))
    elif name in ROW_SHARDED:
        grad = grad.reshape(N_CHIPS, grad.shape[0] // N_CHIPS, grad.shape[1])
    return grad.reshape(N_CHIPS, 2, grad.shape[1] // 2, grad.shape[2])


def _local_step(xt, pt, tgt, n_batch, bufs, small, core):
    t, d = xt.shape
    tm = _pick(t, ROW_TILE, 8)
    tk = _pick(t, ROW_TILE, 8)
    nt = t // tm
    row = pl.BlockSpec((tm, d), lambda i, j, k: (i, 0))
    gs = bufs["w_gate"].shape[2]
    ps = bufs["w_proj_a"].shape[2]
    es = bufs["w_ple_proj"].shape[2]
    pdim = pt.shape[1]
    ncols = N_CHIPS * bufs["w_in"].shape[2]
    tin = ncols // 2
    assert 2 * gs == d and 4 * ps == d and 4 * es == d and tin % 128 == 0

    w = {}
    halves = {n: b.reshape(N_CHIPS, 2, b.shape[1] // 2, b.shape[2]) for n, b in bufs.items()}

    def publish(names, arrays):
        for name, g in zip(names, arrays):
            g = g.reshape(N_CHIPS, 2 * g.shape[2], g.shape[3])
            if name in ROW_SHARDED:
                g = g.reshape(N_CHIPS * g.shape[1], g.shape[2])
            elif name == "w_in":
                g = jnp.transpose(g, (1, 0, 2)).reshape(g.shape[1], N_CHIPS * g.shape[2])
            w[name] = g

    class GatherPipe:
        def __init__(self, names):
            self.names = names
            self.stage = None

        def ici(self):
            self.stage = _gather_ici(self.bufs())
            return self.stage

        def d2d(self):
            self.stage = _gather_d2d(self.bufs())
            return self.stage

        def bufs(self):
            return self.stage.results if self.stage is not None else [halves[n] for n in self.names]

        def publish(self):
            publish(self.names, self.stage.results)

    class GradPipe:
        def __init__(self, names):
            self.names = names

        def exchange(self, grads):
            self.grads = [_slotted(n, g) for n, g in zip(self.names, grads)]
            self.x = _exchange_halves(self.grads)
            return self.x

        def scatter(self):
            self.sums = [_chip_sum("chip_sum_" + n, g, got, core)
                         for n, g, got in zip(self.names, self.grads, self.x.results)]
            self.s = _scatter_ici(self.sums)
            return self.s

        def forward(self):
            self.f = _scatter_d2d(self.s.results)
            return self.f

        def terms(self):
            return dict(zip(self.names, self.f.results))

    publish(GATHER_FIRST, _all_gather_weights([halves[n] for n in GATHER_FIRST]))
    g_in, g_proj, g_ple = GatherPipe(["w_in", "w_gate"]), GatherPipe(["w_proj_a", "w_proj_b", "w_out"]), \
        GatherPipe(["w_ple_gate", "w_ple_proj"])
    g_down2, g_up2 = GatherPipe(["ffn2_w_down"]), GatherPipe(["ffn2_w_gu"])
    h1, ffn1_saved = _ffn_fwd("ffn1", xt, small["ffn1_norm"], w["ffn1_w_gu"], w["ffn1_w_down"],
                              {"up": lambda: [g_in.ici()], "down": lambda: [g_in.d2d(), g_proj.ici()]})
    g_in.publish()
    un = _rms_fwd("mix_norm", h1, small["mix_norm"])
    w_in, wgate = w["w_in"], w["w_gate"]
    (qkv,) = _mm(
        "qkv", "nn", (nt, 2, 1),
        [(un, row, w_in, pl.BlockSpec((d, tin), lambda i, j, k: (0, j)))], [],
        [(_sds((t, ncols), BF16), pl.BlockSpec((tm, tin), lambda i, j, k: (i, j)))], (tm, tin), _cast_epilogue,
        j_outer=True, comms=[g_proj.d2d(), g_ple.ici()])
    g_proj.publish()
    wpa, wpb, wout = w["w_proj_a"], w["w_proj_b"], w["w_out"]

    def gate_epilogue(accs, extras, outs, ij):
        outs[0][...] = jax.nn.sigmoid(accs[0]).astype(BF16)

    (gates,) = _mm(
        "gate", "nn", (nt, 4, 1),
        [(un, row, wgate, pl.BlockSpec((None, d, gs), lambda i, j, k: (j, 0, 0)))], [],
        [(_sds((2, t, d), BF16), pl.BlockSpec((None, tm, gs), lambda i, j, k: (j // 2, i, j % 2)))],
        (tm, gs), gate_epilogue, j_outer=True, chunked=True, comms=[g_ple.d2d(), g_down2.ici()])
    g_ple.publish()
    wpg, wpe = w["w_ple_gate"], w["w_ple_proj"]

    bias_a = _pair_bias(_bias_a(small["a_rel_bias"][0]))
    bias_b = _pair_bias(_bias_b())
    sink_a = _pair_rows(jnp.full((N_HEADS, 128), NEG_INF, F32))
    sink_b = _pair_rows(jnp.broadcast_to(small["b_sinks"][0][:, None], (N_HEADS, 128)))
    gqa, gka, gqb, gkb = [jnp.tile(small[k], (1, 2)) for k in ("a_q_norm", "a_k_norm", "b_q_norm", "b_k_norm")]
    ya, lse_a = _attn_fwd("attn_a_fwd", qkv, bias_a, sink_a, gqa, gka, ATTN_A, n_batch,
                          comms=[g_down2.d2d(), g_up2.ici()])
    g_down2.publish()
    yb, lse_b = _attn_fwd("attn_b_fwd", qkv, bias_b, sink_b, gqb, gkb, ATTN_B, n_batch, comms=[g_up2.d2d()])
    g_up2.publish()

    def merge_epilogue(accs, extras, outs, ij):
        pa, pb = accs
        outs[0][...] = (extras[0][...].astype(F32) * pa + extras[1][...].astype(F32) * pb).astype(BF16)
        outs[1][...] = pa.astype(BF16)
        outs[2][...] = pb.astype(BF16)

    y_spec = pl.BlockSpec((tm, A_WIDTH), lambda i, j, k: (i, 0))
    proj_spec = pl.BlockSpec((None, A_WIDTH, ps), lambda i, j, k: (j, 0, 0))
    tile_ps = pl.BlockSpec((tm, ps), lambda i, j, k: (i, j))
    merged, pa, pb = _mm(
        "proj_merge", "nn", (nt, 4, 1),
        [(ya, y_spec, wpa, proj_spec), (yb, y_spec, wpb, proj_spec)],
        [(gates, pl.BlockSpec((None, tm, ps), lambda i, j, k: (0, i, j))),
         (gates, pl.BlockSpec((None, tm, ps), lambda i, j, k: (1, i, j)))],
        [(_sds((t, d), BF16), tile_ps)] * 3, (tm, ps), merge_epilogue)

    def residual_epilogue(accs, extras, outs, ij):
        outs[0][...] = extras[0][...] + accs[0]

    (h2,) = _mm(
        "out_proj", "nn", (nt, 1, 1),
        [(merged, row, wout, pl.BlockSpec((d, d), lambda i, j, k: (0, 0)))],
        [(h1, row)], [(_sds((t, d), F32), row)], (tm, d), residual_epilogue)

    h3, ffn2_saved = _ffn_fwd("ffn2", h2, small["ffn2_norm"], w["ffn2_w_gu"], w["ffn2_w_down"], {})
    n3 = _rms_fwd("ple_norm", h3, small["ple_norm"])
    tile_es = pl.BlockSpec((tm, es), lambda i, j, k: (i, j))
    (pe,) = _mm(
        "ple_embed", "nn", (nt, 4, 1),
        [(pt, pl.BlockSpec((tm, pdim), lambda i, j, k: (i, 0)), wpe, pl.BlockSpec((None, pdim, es), lambda i, j, k: (j, 0, 0)))],
        [], [(_sds((t, d), F32), tile_es)], (tm, es), _cast_epilogue)

    th = _pick(d, 512)

    def head_epilogue(accs, extras, outs, ij):
        h3_ref, pe_ref, tgt_ref = extras
        dy_ref, dpe_ref, dz_ref, loss_ref = outs
        pg = jax.nn.sigmoid(accs[0])
        pev = pe_ref[...]
        diff = h3_ref[...] + pg * pev - tgt_ref[...]
        dy = diff * (1.0 / d)
        dy_ref[...] = dy
        dpe_ref[...] = (dy * pg).astype(BF16)
        dz_ref[...] = (dy * pev * pg * (1.0 - pg)).astype(BF16)
        _accumulate(loss_ref, jnp.full(loss_ref.shape, jnp.sum(diff * diff), F32), (ij[0] == 0) & (ij[1] == 0))

    tile_h = pl.BlockSpec((tm, th), lambda i, j, k: (i, j))
    dy, dpe, dz, loss_acc = _mm(
        "ple_gate_loss", "nn", (nt, d // th, 1),
        [(n3, row, wpg, pl.BlockSpec((d, th), lambda i, j, k: (0, j)))],
        [(h3, tile_h), (pe, tile_h), (tgt, tile_h)],
        [(_sds((t, d), F32), tile_h), (_sds((t, d), BF16), tile_h), (_sds((t, d), BF16), tile_h),
         (_sds((8, 128), F32), pl.BlockSpec((8, 128), lambda i, j, k: (0, 0)))],
        (tm, th), head_epilogue, j_outer=True, chunked=True)
    loss = 0.5 * loss_acc[0, 0] / d

    nk = t // tk
    (dwpe,) = _mm(
        "d_w_ple_proj", "tn", (1, 4, nk),
        [(pt, pl.BlockSpec((tk, pdim), lambda i, j, k: (k, 0)), dpe, pl.BlockSpec((tk, es), lambda i, j, k: (k, j)))],
        [], [(_sds((4, pdim, es), BF16), pl.BlockSpec((None, pdim, es), lambda i, j, k: (j, 0, 0)))],
        (pdim, es), _cast_epilogue)

    def dense_grad(name, a, dyb, comms=()):
        (res,) = _mm(
            name, "tn", (1, d // th, nk),
            [(a, pl.BlockSpec((tk, d), lambda i, j, k: (k, 0)), dyb, pl.BlockSpec((tk, th), lambda i, j, k: (k, j)))],
            [], [(_sds((d, d), BF16), pl.BlockSpec((d, th), lambda i, j, k: (0, j)))], (d, th), _cast_epilogue,
            comms=comms)
        return res

    dwpg = dense_grad("d_w_ple_gate", n3, dz)
    tmn = _pick(t, ROW_TILE, 8)
    extras, outs = _rms_bwd_io(h3, small["ple_norm"], dy, tmn)
    dh3, dh3_b, d_ple_norm = _mm(
        "d_ple_norm", "nt", (t // tmn, 1, 1),
        [(dz, pl.BlockSpec((tmn, d), lambda i, j, k: (i, 0)), wpg, pl.BlockSpec((d, d), lambda i, j, k: (0, 0)))],
        extras, outs, (tmn, d), _rms_bwd_epilogue)

    up2, down2, ple = GradPipe(["ffn2_w_gu"]), GradPipe(["ffn2_w_down"]), GradPipe(["w_ple_gate", "w_ple_proj"])
    proj = GradPipe(["w_proj_a", "w_proj_b", "w_out"])
    dh2, dh2_b, d_ffn2_norm, dwgu2, dwd2 = _ffn_bwd(
        "ffn2", dh3, dh3_b, h2, small["ffn2_norm"], w["ffn2_w_gu"], w["ffn2_w_down"], ffn2_saved,
        {"dnorm": lambda dwgu, dwd: [up2.exchange([dwgu]), down2.exchange([dwd]), ple.exchange([dwpg, dwpe])]})

    def dmerge_epilogue(accs, extras, outs, ij):
        dmo = accs[0]
        g_ref, pa_ref, pb_ref = extras
        dg_ref, dpa_ref, dpb_ref = outs
        ga = g_ref[0].astype(F32)
        gb = g_ref[1].astype(F32)
        dg_ref[0] = (dmo * pa_ref[...].astype(F32) * ga * (1.0 - ga)).astype(BF16)
        dg_ref[1] = (dmo * pb_ref[...].astype(F32) * gb * (1.0 - gb)).astype(BF16)
        dpa_ref[...] = (dmo * ga).astype(BF16)
        dpb_ref[...] = (dmo * gb).astype(BF16)

    g_spec = pl.BlockSpec((2, tm, th), lambda i, j, k: (0, i, j))
    dgates, dpa, dpb = _mm(
        "d_merge", "nt", (nt, d // th, 1),
        [(dh2_b, row, wout, pl.BlockSpec((th, d), lambda i, j, k: (j, 0)))],
        [(gates, g_spec), (pa, tile_h), (pb, tile_h)],
        [(_sds((2, t, d), BF16), g_spec), (_sds((t, d), BF16), tile_h), (_sds((t, d), BF16), tile_h)],
        (tm, th), dmerge_epilogue, j_outer=True, chunked=True, comms=[down2.scatter()])
    dwout = dense_grad("d_w_out", merged, dh2_b, comms=[down2.forward(), ple.scatter()])

    yk_spec = pl.BlockSpec((tk, A_WIDTH), lambda i, j, k: (k, 0))
    dk_spec = pl.BlockSpec((tk, ps), lambda i, j, k: (k, j))
    dproj = (_sds((4, A_WIDTH, ps), BF16), proj_spec)
    dwpa, dwpb = _mm(
        "d_w_proj", "tn", (1, 4, nk),
        [(ya, yk_spec, dpa, dk_spec), (yb, yk_spec, dpb, dk_spec)], [], [dproj, dproj], (A_WIDTH, ps), _cast_epilogue,
        comms=[ple.forward()])
    dproj_a = pl.BlockSpec((tm, ps), lambda i, j, k: (i, k))
    wproj_k = pl.BlockSpec((None, A_WIDTH, ps), lambda i, j, k: (k, 0, 0))
    dya, dyb = _mm(
        "d_attn_out", "nt", (nt, 1, 4),
        [(dpa, dproj_a, wpa, wproj_k), (dpb, dproj_a, wpb, wproj_k)], [],
        [(_sds((t, A_WIDTH), BF16), y_spec)] * 2, (tm, A_WIDTH), _cast_epilogue,
        comms=[proj.exchange([dwpa, dwpb, dwout])])

    dqa, dka, dva, dbias_a, _, dgqa, dgka = _attn_bwd(
        "attn_a_bwd", qkv, bias_a, sink_a, gqa, gka, ya, dya, lse_a, ATTN_A, n_batch, True,
        comms=[up2.scatter(), proj.scatter()])
    dqb, dkb, dvb, _, dsink_b, dgqb, dgkb = _attn_bwd(
        "attn_b_bwd", qkv, bias_b, sink_b, gqb, gkb, yb, dyb, lse_b, ATTN_B, n_batch, False,
        comms=[up2.forward(), proj.forward()])
    dqkv = jnp.concatenate([dqa, dka, dva, dqb, dkb, dvb], axis=1)

    (dwgate,) = _mm(
        "d_w_gate", "tn", (1, 4, nk),
        [(un, pl.BlockSpec((tk, d), lambda i, j, k: (k, 0)),
          dgates, pl.BlockSpec((None, tk, gs), lambda i, j, k: (j // 2, k, j % 2)))],
        [], [(_sds((4, d, gs), BF16), pl.BlockSpec((None, d, gs), lambda i, j, k: (j, 0, 0)))], (d, gs), _cast_epilogue)
    (dwin,) = _mm(
        "d_w_in", "tn", (1, 2, nk),
        [(un, pl.BlockSpec((tk, d), lambda i, j, k: (k, 0)), dqkv, pl.BlockSpec((tk, tin), lambda i, j, k: (k, j)))],
        [], [(_sds((d, ncols), BF16), pl.BlockSpec((d, tin), lambda i, j, k: (0, j)))], (d, tin), _cast_epilogue)

    mixer = GradPipe(["w_in", "w_gate"])
    extras, outs = _rms_bwd_io(h1, small["mix_norm"], dh2, tmn)
    dh1, dh1_b, d_mix_norm = _mm(
        "d_mix_norm", "nt", (t // tmn, 1, 6),
        [(dgates, pl.BlockSpec((None, tmn, gs), lambda i, j, k: (jnp.minimum(k, 3) // 2, i, jnp.minimum(k, 3) % 2)),
          wgate, pl.BlockSpec((None, d, gs), lambda i, j, k: (jnp.minimum(k, 3), 0, 0))),
         (dqkv, pl.BlockSpec((tmn, tin), lambda i, j, k: (i, jnp.maximum(k - 4, 0))),
          w_in, pl.BlockSpec((d, tin), lambda i, j, k: (0, jnp.maximum(k - 4, 0))))],
        extras, outs, (tmn, d), _rms_bwd_epilogue, steps=[4, 2],
        comms=[mixer.exchange([dwin, dwgate])])

    up1 = GradPipe(["ffn1_w_gu"])
    down1 = GradPipe(["ffn1_w_down"])
    dx, _, d_ffn1_norm, _, _ = _ffn_bwd(
        "ffn1", dh1, dh1_b, xt, small["ffn1_norm"], w["ffn1_w_gu"], w["ffn1_w_down"], ffn1_saved,
        {"dwgu": lambda: [mixer.scatter()],
         "dwd": lambda dwgu: [mixer.forward(), up1.exchange([dwgu])],
         "dnorm": lambda dwgu, dwd: [up1.scatter(), down1.exchange([dwd])]})
    _run_comms("grad_tail_scatter", [up1.forward(), down1.scatter()])
    _run_comms("grad_tail_forward", [down1.forward()])
    terms = {}
    for pipe in (up2, down2, ple, proj, mixer, up1, down1):
        terms.update(pipe.terms())

    def fold(v):
        return v[0, :HEAD_DIM] + v[0, HEAD_DIM:]

    small_grads = {"ffn1_norm": d_ffn1_norm, "mix_norm": d_mix_norm, "ffn2_norm": d_ffn2_norm,
                   "ple_norm": d_ple_norm, "a_q_norm": fold(dgqa), "a_k_norm": fold(dgka),
                   "b_q_norm": fold(dgqb), "b_k_norm": fold(dgkb), "a_rel_bias": _rel_bias_grad(_unpair_bias(dbias_a)),
                   "b_sinks": jnp.sum(dsink_b, axis=1)}
    return loss, dx, terms, small_grads


def kernel(x, p, ffn1_norm, ffn1_w_gu, ffn1_w_down, mix_norm, w_in, a_q_norm, a_k_norm, a_rel_bias, b_q_norm, b_k_norm, b_sinks, w_gate, w_proj_a, w_proj_b, w_out, ffn2_norm, ffn2_w_gu, ffn2_w_down, ple_norm, w_ple_gate, w_ple_proj, loss_target, m_ffn1_norm, m_ffn1_w_gu, m_ffn1_w_down, m_mix_norm, m_w_in, m_a_q_norm, m_a_k_norm, m_a_rel_bias, m_b_q_norm, m_b_k_norm, m_b_sinks, m_w_gate, m_w_proj_a, m_w_proj_b, m_w_out, m_ffn2_norm, m_ffn2_w_gu, m_ffn2_w_down, m_ple_norm, m_w_ple_gate, m_w_ple_proj, v_ffn1_norm, v_ffn1_w_gu, v_ffn1_w_down, v_mix_norm, v_w_in, v_a_q_norm, v_a_k_norm, v_a_rel_bias, v_b_q_norm, v_b_k_norm, v_b_sinks, v_w_gate, v_w_proj_a, v_w_proj_b, v_w_out, v_ffn2_norm, v_ffn2_w_gu, v_ffn2_w_down, v_ple_norm, v_w_ple_gate, v_w_ple_proj):
    given = dict(locals())
    n_batch, s, d = x.shape
    t = n_batch * s
    xt = x.reshape(t, d)
    pt = p.reshape(t, p.shape[-1])
    tgt = loss_target.reshape(t, d)

    chip = (2 * lax.axis_index("x") + lax.axis_index("y")).astype(jnp.int32).reshape(1)
    bufs = {name: _cast_into_slot("cast_" + name, given[name][0], chip) for name in BIG}
    small = {name: given[name] for name in SMALL}
    core = lax.axis_index("c").astype(jnp.int32).reshape(1)
    loss, dx, terms, small_grads = _local_step(xt, pt, tgt, n_batch, bufs, small, core)

    grads, deltas, new_m, new_v = {}, {}, {}, {}
    for name in BIG:
        gw, dl, nm, nv = _adamw_terms("adamw_" + name, terms[name], given[name][0], given["m_" + name][0],
                                      given["v_" + name][0])
        grads[name], deltas[name], new_m[name], new_v[name] = gw[None], dl[None], nm[None], nv[None]

    small_shapes = [given[name].shape for name in SMALL] + [()]
    g_pack = _all_reduce_small(_pack([small_grads[name] for name in SMALL] + [loss]))
    zero = jnp.zeros((), F32)
    w_pack = _pack([given[name] for name in SMALL] + [zero])
    m_pack = _pack([given["m_" + name] for name in SMALL] + [zero])
    v_pack = _pack([given["v_" + name] for name in SMALL] + [zero])
    d_pack, nm_pack, nv_pack = _ew("adamw_small", lambda wv, gv, mv, vv: _adamw_math(wv, gv, mv, vv),
                                   [w_pack, g_pack, m_pack, v_pack], [F32] * 3)
    g_small = _unpack(g_pack, small_shapes)
    loss_total = g_small[-1]
    for name, gv, dv, mv, vv in zip(SMALL, g_small, _unpack(d_pack, small_shapes), _unpack(nm_pack, small_shapes),
                                    _unpack(nv_pack, small_shapes)):
        grads[name], deltas[name], new_m[name], new_v[name] = gv, dv, mv, vv

    return (loss_total, dx.reshape(x.shape), *[grads[n] for n in WEIGHTS], *[deltas[n] for n in WEIGHTS],
            *[new_m[n] for n in WEIGHTS], *[new_v[n] for n in WEIGHTS])
```

```python
import functools

import numpy as np
import jax
import jax.numpy as jnp
from jax import lax
from jax.experimental import pallas as pl
from jax.experimental.pallas import tpu as pltpu

F32 = jnp.float32
BF16 = jnp.bfloat16

CHUNK = 64
HEAD_DIM = 64
A_PREV_CHUNKS = 8
A_MAX_REL = 128
N_HEADS = 8
B_KV_HEADS = 2
B_PREV_CHUNKS = 2
A_WIDTH = N_HEADS * HEAD_DIM
B_KV_WIDTH = B_KV_HEADS * HEAD_DIM
EPS = 1e-6
NEG_INF = -1e30
ATTN_SCALE = HEAD_DIM ** -0.5
Q_BLOCK = 128
PAIR = 2 * HEAD_DIM

ADAM_LR = 0.001
ADAM_B1 = 0.9
ADAM_B2 = 0.999
ADAM_EPS = 1e-08
ADAM_WD = 0.01
ADAM_STEP = 10

N_CHIPS = 4
N_DEV = 8
VMEM_LIMIT_V7X = 56 * 1024 * 1024
ROW_TILE = 1024
MESH = pl.DeviceIdType.MESH
COLLECTIVE_IDS = {("sibling",): 1, ("chips",): 2, ("chips", "sibling"): 3}
ANY = pl.BlockSpec(memory_space=pl.ANY)

_DN = {
    "nn": (((1,), (0,)), ((), ())),
    "nt": (((1,), (1,)), ((), ())),
    "tn": (((0,), (0,)), ((), ())),
}


def _pick(n, target, mult=128):
    best = None
    for d in range(mult, min(n, target) + 1, mult):
        if n % d == 0:
            best = d
    return n if best is None else best


def _dot(a, b, mode):
    return lax.dot_general(a.astype(BF16), b.astype(BF16), _DN[mode], preferred_element_type=F32)


def _params():
    return pltpu.CompilerParams(vmem_limit_bytes=VMEM_LIMIT_V7X)


class _Comm:
    def __init__(self, peers, ins, outs, aliases, sems, start, finish):
        self.peers = peers
        self.ins, self.outs, self.aliases, self.sems = list(ins), list(outs), dict(aliases), list(sems)
        self.start, self.finish = start, finish
        self.results = None


class _CommPlumbing:
    def __init__(self, comms, n_in, n_out, n_scratch):
        self.comms = list(comms)
        self.n_in, self.n_out, self.n_scratch = n_in, n_out, n_scratch
        self.args = [a for cm in self.comms for a in cm.ins]
        self.out_shape = [o for cm in self.comms for o in cm.outs]
        self.scratch = [s for cm in self.comms for s in cm.sems]
        self.aliases = {}
        i0, o0 = n_in, n_out
        for cm in self.comms:
            for a, b in cm.aliases.items():
                self.aliases[i0 + a] = o0 + b
            i0 += len(cm.ins)
            o0 += len(cm.outs)

    def _parts(self, in_refs, out_refs, scratch_refs):
        parts = []
        i0, o0, s0 = self.n_in, self.n_out, self.n_scratch
        for cm in self.comms:
            parts.append((in_refs[i0:i0 + len(cm.ins)], out_refs[o0:o0 + len(cm.outs)],
                          scratch_refs[s0:s0 + len(cm.sems)]))
            i0 += len(cm.ins)
            o0 += len(cm.outs)
            s0 += len(cm.sems)
        return parts

    def kinds(self):
        return sorted(set(cm.peers for cm in self.comms))

    def params(self, **kwargs):
        if self.comms:
            kwargs["collective_id"] = COLLECTIVE_IDS[tuple(self.kinds())]
        return pltpu.CompilerParams(**kwargs)

    def handshake(self):
        x, y, c, chips = _mesh_place()
        peers = []
        if "sibling" in self.kinds():
            peers.append((x, y, 1 - c))
        if "chips" in self.kinds():
            peers += [(tx, ty, c) for tx, ty in chips]
        barrier = pltpu.get_barrier_semaphore()
        for peer in peers:
            pl.semaphore_signal(barrier, inc=1, device_id=peer, device_id_type=MESH)
        pl.semaphore_wait(barrier, len(peers))

    def start_at(self, in_refs, out_refs, scratch_refs, first):
        if self.comms:
            parts = self._parts(in_refs, out_refs, scratch_refs)

            @pl.when(first)
            def _():
                self.handshake()
                for cm, part in zip(self.comms, parts):
                    cm.start(*part)

    def finish_at(self, in_refs, out_refs, scratch_refs, last):
        if self.comms:
            parts = self._parts(in_refs, out_refs, scratch_refs)

            @pl.when(last)
            def _():
                for cm, part in zip(self.comms, parts):
                    cm.finish(*part)

    def deliver(self, results):
        o0 = self.n_out
        for cm in self.comms:
            cm.results = list(results[o0:o0 + len(cm.outs)])
            o0 += len(cm.outs)
        return list(results[:self.n_out])


def _swap_ij(spec):
    index_map = spec.index_map
    return pl.BlockSpec(spec.block_shape, lambda j, i, k: index_map(i, j, k))


MXU_COLUMNS_V7X = 256


def _mm(name, mode, grid, pairs, extras, outs, acc_shape, epilogue, steps=None, comms=(), j_outer=False,
        chunked=False):
    ni, nj, nk = grid
    n_in = 2 * len(pairs) + len(extras)
    n_out = len(outs)
    tn = acc_shape[1]
    col_chunks = None
    if chunked:
        assert nk == 1 and steps is None and mode in ("nn", "nt")
        col_chunks = [(c0, min(MXU_COLUMNS_V7X, tn - c0)) for c0 in range(0, tn, MXU_COLUMNS_V7X)]
    n_acc = 0 if chunked else (len(pairs) if steps is None else 1)
    plumb = _CommPlumbing(comms, n_in, n_out, n_acc)
    n_all_in = n_in + len(plumb.args)
    n_all_out = n_out + len(plumb.out_shape)
    if j_outer:
        grid = (nj, ni, nk)
        pairs = [(a, _swap_ij(a_spec), b, _swap_ij(b_spec)) for a, a_spec, b, b_spec in pairs]
        extras = [(e, _swap_ij(e_spec)) for e, e_spec in extras]
        outs = [(o, _swap_ij(o_spec)) for o, o_spec in outs]

    def body(*refs):
        in_refs = refs[:n_all_in]
        out_refs = refs[n_all_in:n_all_in + n_all_out]
        scratch = refs[n_all_in + n_all_out:]
        accs = scratch[:n_acc]
        i = pl.program_id(1 if j_outer else 0)
        j = pl.program_id(0 if j_outer else 1)
        k = pl.program_id(2)
        plumb.start_at(in_refs, out_refs, scratch, (i == 0) & (j == 0) & (k == 0))

        def contrib(p, acc):
            acc[...] += _dot(in_refs[2 * p][...], in_refs[2 * p + 1][...], mode)

        if col_chunks:
            def cols(ref, c0, cs):
                if ref.shape[-1] != tn:
                    return ref
                return ref.at[(slice(None),) * (len(ref.shape) - 1) + (pl.ds(c0, cs),)]

            lhs = [in_refs[2 * p][...] for p in range(len(pairs))]
            for ci, (c0, cs) in enumerate(col_chunks):
                vals = []
                for p in range(len(pairs)):
                    b_ref = in_refs[2 * p + 1]
                    rhs = b_ref[:, c0:c0 + cs] if mode == "nn" else b_ref[c0:c0 + cs, :]
                    vals.append(_dot(lhs[p], rhs, mode))
                epilogue(vals, [cols(r, c0, cs) for r in in_refs[2 * len(pairs):n_in]],
                         [cols(r, c0, cs) for r in out_refs[:n_out]], (i, j * len(col_chunks) + ci))
        else:
            @pl.when(k == 0)
            def _():
                for acc in accs:
                    acc[...] = jnp.zeros(acc.shape, F32)

            if steps is None:
                for p in range(len(pairs)):
                    contrib(p, accs[p])
            else:
                lo = 0
                for p, n in enumerate(steps):
                    pl.when((k >= lo) & (k < lo + n))(functools.partial(contrib, p, accs[0]))
                    lo += n

            @pl.when(k == nk - 1)
            def _():
                epilogue([acc[...] for acc in accs], in_refs[2 * len(pairs):n_in], out_refs[:n_out], (i, j))

        plumb.finish_at(in_refs, out_refs, scratch, (i == ni - 1) & (j == nj - 1) & (k == nk - 1))

    args, in_specs = [], []
    for a, a_spec, b, b_spec in pairs:
        args += [a, b]
        in_specs += [a_spec, b_spec]
    for e, e_spec in extras:
        args.append(e)
        in_specs.append(e_spec)
    res = pl.pallas_call(
        body,
        name=name,
        grid=grid,
        in_specs=in_specs + [ANY] * len(plumb.args),
        out_specs=[s for _, s in outs] + [ANY] * len(plumb.out_shape),
        out_shape=[o for o, _ in outs] + plumb.out_shape,
        scratch_shapes=[pltpu.VMEM(acc_shape, F32) for _ in range(n_acc)] + plumb.scratch,
        input_output_aliases=plumb.aliases,
        compiler_params=plumb.params(vmem_limit_bytes=VMEM_LIMIT_V7X),
    )(*args, *plumb.args)
    return plumb.deliver(res)


def _sds(shape, dtype):
    return jax.ShapeDtypeStruct(shape, dtype)


def _accumulate(ref, value, first):
    @pl.when(first)
    def _():
        ref[...] = value

    @pl.when(jnp.logical_not(first))
    def _():
        ref[...] += value


def _rms_fwd(name, x, gain, comms=()):
    t, d = x.shape
    tm = _pick(t, ROW_TILE, 8)
    steps = t // tm
    plumb = _CommPlumbing(comms, 2, 1, 0)
    n_all_in = 2 + len(plumb.args)
    n_all_out = 1 + len(plumb.out_shape)

    def body(*refs):
        x_ref, g_ref = refs[:2]
        y_ref = refs[n_all_in]
        comm_refs = (refs[:n_all_in], refs[n_all_in:n_all_in + n_all_out], refs[n_all_in + n_all_out:])
        i = pl.program_id(0)
        plumb.start_at(*comm_refs, i == 0)
        xv = x_ref[...]
        rstd = lax.rsqrt(jnp.mean(xv * xv, axis=-1, keepdims=True) + EPS)
        y_ref[...] = (xv * rstd * g_ref[...]).astype(BF16)
        plumb.finish_at(*comm_refs, i == steps - 1)

    res = pl.pallas_call(
        body, name=name, grid=(steps,),
        in_specs=[pl.BlockSpec((tm, d), lambda i: (i, 0)), pl.BlockSpec((1, d), lambda i: (0, 0))]
        + [ANY] * len(plumb.args),
        out_specs=[pl.BlockSpec((tm, d), lambda i: (i, 0))] + [ANY] * len(plumb.out_shape),
        out_shape=[_sds((t, d), BF16)] + plumb.out_shape,
        scratch_shapes=plumb.scratch,
        input_output_aliases=plumb.aliases,
        compiler_params=plumb.params(vmem_limit_bytes=VMEM_LIMIT_V7X),
    )(x, gain, *plumb.args)
    return plumb.deliver(res)[0]


def _rms_bwd_epilogue(accs, extras, outs, ij):
    x_ref, g_ref, r_ref = extras
    dh_ref, dhb_ref, dg_ref = outs
    dn = accs[0]
    xv = x_ref[...]
    rstd = lax.rsqrt(jnp.mean(xv * xv, axis=-1, keepdims=True) + EPS)
    xhat = xv * rstd
    gd = dn * g_ref[...]
    dx = rstd * (gd - xhat * jnp.mean(gd * xhat, axis=-1, keepdims=True))
    dh = r_ref[...] + dx
    dh_ref[...] = dh
    dhb_ref[...] = dh.astype(BF16)
    _accumulate(dg_ref, jnp.sum(dn * xhat, axis=0, keepdims=True), ij[0] == 0)


def _rms_bwd_io(x, gain, dres, tm):
    t, d = x.shape
    row = pl.BlockSpec((tm, d), lambda i, j, k: (i, 0))
    extras = [(x, row), (gain, pl.BlockSpec((1, d), lambda i, j, k: (0, 0))), (dres, row)]
    outs = [(_sds((t, d), F32), row), (_sds((t, d), BF16), row),
            (_sds((1, d), F32), pl.BlockSpec((1, d), lambda i, j, k: (0, 0)))]
    return extras, outs


def _residual_norm_epilogue(scale):
    def epilogue(accs, extras, outs, ij):
        hv = extras[0][...] + scale * accs[0]
        outs[0][...] = hv
        rstd = lax.rsqrt(jnp.mean(hv * hv, axis=-1, keepdims=True) + EPS)
        outs[1][...] = (hv * rstd * extras[1][...]).astype(BF16)
    return epilogue


def _ffn_fwd(tag, h, n, wgu, wd, next_gain, hooks):
    t, d = h.shape
    fs = wgu.shape[2]
    f = 2 * fs
    tm = _pick(t, ROW_TILE, 8)

    def up_epilogue(accs, extras, outs, ij):
        g, u = accs
        gu_ref, a_ref = outs
        gu_ref[0] = g.astype(BF16)
        gu_ref[1] = u.astype(BF16)
        a_ref[...] = (g * jax.nn.sigmoid(g) * u).astype(BF16)

    a_spec = pl.BlockSpec((tm, d), lambda i, j, k: (i, 0))
    gu, a = _mm(
        tag + "_up", "nn", (t // tm, 2, 1),
        [(n, a_spec, wgu, pl.BlockSpec((None, d, fs), lambda i, j, k: (j, 0, 0))),
         (n, a_spec, wgu, pl.BlockSpec((None, d, fs), lambda i, j, k: (j + 2, 0, 0)))],
        [],
        [(_sds((2, t, f), BF16), pl.BlockSpec((2, tm, fs), lambda i, j, k: (0, i, j))),
         (_sds((t, f), BF16), pl.BlockSpec((tm, fs), lambda i, j, k: (i, j)))],
        (tm, fs), up_epilogue, comms=hooks.get("up", lambda: ())(), j_outer=True, chunked=True)

    row = pl.BlockSpec((tm, d), lambda i, j, k: (i, 0))
    h_new, n_new = _mm(
        tag + "_down", "nn", (t // tm, 1, 1),
        [(a, pl.BlockSpec((tm, f), lambda i, j, k: (i, 0)), wd, pl.BlockSpec((f, d), lambda i, j, k: (0, 0)))],
        [(h, row), (next_gain, pl.BlockSpec((1, d), lambda i, j, k: (0, 0)))],
        [(_sds((t, d), F32), row), (_sds((t, d), BF16), row)], (tm, d), _residual_norm_epilogue(0.5),
        comms=hooks.get("down", lambda: ())())
    return h_new, n_new, (n, gu, a)


def _ffn_bwd(tag, dh, dh_b, h, gain, wgu, wd, saved, hooks):
    n, gu, a = saved
    t, d = h.shape
    fs = wgu.shape[2]
    f = 2 * fs
    tm = _pick(t, ROW_TILE, 8)
    tk = _pick(t, ROW_TILE, 8)

    def dact_epilogue(accs, extras, outs, ij):
        da = 0.5 * accs[0]
        g = extras[0][0].astype(F32)
        u = extras[0][1].astype(F32)
        sg = jax.nn.sigmoid(g)
        outs[0][0] = (da * u * sg * (1.0 + g * (1.0 - sg))).astype(BF16)
        outs[0][1] = (da * g * sg).astype(BF16)

    gu_spec = pl.BlockSpec((2, tm, fs), lambda i, j, k: (0, i, j))
    (dgu,) = _mm(
        tag + "_dact", "nt", (t // tm, 2, 1),
        [(dh_b, pl.BlockSpec((tm, d), lambda i, j, k: (i, 0)), wd, pl.BlockSpec((fs, d), lambda i, j, k: (j, 0)))],
        [(gu, gu_spec)], [(_sds((2, t, f), BF16), gu_spec)], (tm, fs), dact_epilogue, j_outer=True, chunked=True,
        comms=hooks.get("dact", lambda: ())())

    def cast_epilogue(accs, extras, outs, ij):
        outs[0][...] = accs[0].astype(BF16)

    (dwgu,) = _mm(
        tag + "_dwgu", "tn", (1, 4, t // tk),
        [(n, pl.BlockSpec((tk, d), lambda i, j, k: (k, 0)),
          dgu, pl.BlockSpec((None, tk, fs), lambda i, j, k: (j // 2, k, j % 2)))],
        [], [(_sds((4, d, fs), BF16), pl.BlockSpec((None, d, fs), lambda i, j, k: (j, 0, 0)))], (d, fs), cast_epilogue,
        comms=hooks.get("dwgu", lambda: ())())

    def half_epilogue(accs, extras, outs, ij):
        outs[0][...] = (0.5 * accs[0]).astype(BF16)

    (dwd,) = _mm(
        tag + "_dwd", "tn", (2, 1, t // tk),
        [(a, pl.BlockSpec((tk, fs), lambda i, j, k: (k, i)), dh_b, pl.BlockSpec((tk, d), lambda i, j, k: (k, 0)))],
        [], [(_sds((f, d), BF16), pl.BlockSpec((fs, d), lambda i, j, k: (i, 0)))], (fs, d), half_epilogue,
        comms=hooks.get("dwd", lambda g: ())(dwgu))

    tmn = _pick(t, ROW_TILE, 8)
    extras, outs = _rms_bwd_io(h, gain, dh, tmn)
    dh_in, dh_in_b, dgain = _mm(
        tag + "_dnorm", "nt", (t // tmn, 1, 4),
        [(dgu, pl.BlockSpec((None, tmn, fs), lambda i, j, k: (k // 2, i, k % 2)),
          wgu, pl.BlockSpec((None, d, fs), lambda i, j, k: (k, 0, 0)))],
        extras, outs, (tmn, d), _rms_bwd_epilogue, comms=hooks.get("dnorm", lambda g, w: ())(dwgu, dwd))
    return dh_in, dh_in_b, dgain, dwgu, dwd


def _lane_lo(shape):
    return lax.broadcasted_iota(jnp.int32, shape, 1) < HEAD_DIM


def _pair_norm(xv, gain):
    lo = _lane_lo(xv.shape)
    x2 = xv * xv
    ms_lo = jnp.sum(jnp.where(lo, x2, 0.0), axis=-1, keepdims=True) * (1.0 / HEAD_DIM)
    ms_hi = jnp.sum(jnp.where(lo, 0.0, x2), axis=-1, keepdims=True) * (1.0 / HEAD_DIM)
    rstd = jnp.where(lo, lax.rsqrt(ms_lo + EPS), lax.rsqrt(ms_hi + EPS))
    xhat = xv * rstd
    return xhat * gain, xhat, rstd


def _pair_norm_bwd(dn, xhat, rstd, gain):
    lo = _lane_lo(dn.shape)
    gd = dn * gain
    t = gd * xhat
    m_lo = jnp.sum(jnp.where(lo, t, 0.0), axis=-1, keepdims=True) * (1.0 / HEAD_DIM)
    m_hi = jnp.sum(jnp.where(lo, 0.0, t), axis=-1, keepdims=True) * (1.0 / HEAD_DIM)
    dx = rstd * (gd - xhat * jnp.where(lo, m_lo, m_hi))
    return dx, jnp.sum(dn * xhat, axis=0, keepdims=True)


def _half(xv, hi):
    lo = _lane_lo(xv.shape)
    return jnp.where(lo, 0, xv) if hi else jnp.where(lo, xv, 0)


def _attn_window(i, prev):
    q0 = i * Q_BLOCK
    start = jnp.maximum(q0 - prev, 0)
    off = start - (q0 - prev)
    return pl.multiple_of(start, Q_BLOCK), pl.multiple_of(off, Q_BLOCK)


def _attn_specs(cfg, s, nq):
    kw = cfg["kw"]
    q_spec = pl.BlockSpec((Q_BLOCK, A_WIDTH), lambda b, i: (b * nq + i, cfg["qblk"]))
    k_spec = pl.BlockSpec((s, kw), lambda b, i: (b, cfg["kblk"]))
    v_spec = pl.BlockSpec((s, kw), lambda b, i: (b, cfg["vblk"]))
    return q_spec, k_spec, v_spec


def _const_spec(shape):
    return pl.BlockSpec(shape, lambda b, i: (0,) * len(shape))


KEY_CHUNK = 128


def _pair_bias(bias_t):
    wext = bias_t.shape[1]
    return jnp.transpose(bias_t.reshape(N_HEADS // 2, 2, wext, Q_BLOCK), (0, 2, 1, 3)).reshape(
        N_HEADS // 2, wext, 2 * Q_BLOCK)


def _unpair_bias(db2):
    wext = db2.shape[1]
    return jnp.transpose(db2.reshape(N_HEADS // 2, wext, 2, Q_BLOCK), (0, 2, 1, 3)).reshape(N_HEADS, wext, Q_BLOCK)


def _pair_rows(rows):
    two = rows.reshape(N_HEADS // 2, 2 * rows.shape[1])
    return jnp.broadcast_to(two[:, None, :], (N_HEADS // 2, 8, two.shape[1]))


def _sub_lo(shape):
    return lax.broadcasted_iota(jnp.int32, shape, 0) < HEAD_DIM


def _by_half(lo_row, hi_row, rows):
    return jnp.where(_sub_lo((rows, lo_row.shape[1])), lo_row, hi_row)


def _stack_pair(xn, jq, group):
    parts = []
    for hq in range(2):
        hk = ((2 * jq + hq) // group) % 2
        xm = _half(xn, hq)
        if hq != hk:
            xm = pltpu.roll(xm, HEAD_DIM, 1)
        parts.append(xm)
    return jnp.concatenate(parts, axis=0).astype(BF16)


def _place_transposed(blk, dst_ref, c, heads, group):
    bt = blk.T
    lo = _sub_lo(bt.shape)
    for h in heads:
        src_hi = ((h // group) % 2) == 1
        part = jnp.where(lo, 0.0, bt) if src_hi else jnp.where(lo, bt, 0.0)
        if src_hi != (h % 2 == 1):
            part = pltpu.roll(part, HEAD_DIM, 0)
        dst_ref[h, c] = part.astype(BF16)


def _attn_fwd(name, qkv, bias2, sink2, gq, gk, cfg, n_batch, comms=()):
    t = qkv.shape[0]
    s = t // n_batch
    nq = s // Q_BLOCK
    nkc = s // KEY_CHUNK
    prev, group, kw = cfg["prev"], cfg["group"], cfg["kw"]
    w = prev + Q_BLOCK
    n_chunks = w // KEY_CHUNK
    wext = bias2.shape[1]
    plumb = _CommPlumbing(comms, 7, 2, 4)
    n_all_in = 7 + len(plumb.args)
    n_all_out = 2 + len(plumb.out_shape)

    def body(*refs):
        q_ref, k_ref, v_ref, bias_ref, sink_ref, gq_ref, gk_ref = refs[:7]
        y_ref, lse_ref = refs[n_all_in:n_all_in + 2]
        kn_ref, vt_ref, s_ref, pst_ref = refs[n_all_in + n_all_out:n_all_in + n_all_out + 4]
        i = pl.program_id(1)
        comm_refs = (refs[:n_all_in], refs[n_all_in:n_all_in + n_all_out], refs[n_all_in + n_all_out:])
        plumb.start_at(*comm_refs, (pl.program_id(0) == 0) & (i == 0))

        @pl.when(i == 0)
        def _():
            for jk in range(kw // PAIR):
                cols = pl.ds(jk * PAIR, PAIR)
                heads = [h for h in range(N_HEADS) if (h // group) // 2 == jk]
                kn, _, _ = _pair_norm(k_ref[:, cols].astype(F32), gk_ref[...])
                kn_ref[:, cols] = kn.astype(BF16)
                for c in range(nkc):
                    _place_transposed(v_ref[pl.ds(c * KEY_CHUNK, KEY_CHUNK), cols].astype(F32), vt_ref, c, heads, group)

        start, off = _attn_window(i, prev)
        c0 = start // KEY_CHUNK
        sub8 = lax.broadcasted_iota(jnp.int32, (N_HEADS, Q_BLOCK), 0)
        lse = jnp.zeros((N_HEADS, Q_BLOCK), F32)
        for jq in range(N_HEADS // 2):
            kcols = pl.ds((((2 * jq) // group) // 2) * PAIR, PAIR)
            qn, _, _ = _pair_norm(q_ref[:, pl.ds(jq * PAIR, PAIR)].astype(F32), gq_ref[...])
            qs = _stack_pair(qn * ATTN_SCALE, jq, group)
            s_ref[...] = _dot(kn_ref[pl.ds(start, w), kcols], qs, "nt")
            m = sink_ref[jq, 0:1, :]
            for c in range(n_chunks):
                r = pl.ds(c * KEY_CHUNK, KEY_CHUNK)
                s2 = s_ref[r, :] + bias_ref[jq, pl.ds(off + c * KEY_CHUNK, KEY_CHUNK), :]
                s_ref[r, :] = s2
                m = jnp.maximum(m, jnp.max(s2, axis=0, keepdims=True))
            l = jnp.exp(sink_ref[jq, 0:1, :] - m)
            for c in range(n_chunks):
                p = jnp.exp(s_ref[pl.ds(c * KEY_CHUNK, KEY_CHUNK), :] - m)
                l = l + jnp.sum(p, axis=0, keepdims=True)
                pst_ref[pl.ds(2 * c * KEY_CHUNK, KEY_CHUNK), :] = p[:, :Q_BLOCK].astype(BF16)
                pst_ref[pl.ds((2 * c + 1) * KEY_CHUNK, KEY_CHUNK), :] = p[:, Q_BLOCK:].astype(BF16)
            vl = jnp.concatenate([vt_ref[2 * jq + hq, c0 + c] for c in range(n_chunks) for hq in range(2)], axis=1)
            ot = _dot(vl, pst_ref[...], "nn")
            inv = 1.0 / l
            ot = ot * _by_half(inv[:, :Q_BLOCK], inv[:, Q_BLOCK:], PAIR)
            y_ref[:, pl.ds(jq * PAIR, PAIR)] = ot.T.astype(BF16)
            lse2 = m + jnp.log(l)
            lse = jnp.where(sub8 == 2 * jq, lse2[:, :Q_BLOCK], lse)
            lse = jnp.where(sub8 == 2 * jq + 1, lse2[:, Q_BLOCK:], lse)
        lse_ref[...] = lse
        plumb.finish_at(*comm_refs, (pl.program_id(0) == n_batch - 1) & (i == nq - 1))

    q_spec, k_spec, v_spec = _attn_specs(cfg, s, nq)
    res = pl.pallas_call(
        body, name=name, grid=(n_batch, nq),
        in_specs=[q_spec, k_spec, v_spec, _const_spec((N_HEADS // 2, wext, 2 * Q_BLOCK)),
                  _const_spec((N_HEADS // 2, 8, 2 * Q_BLOCK)), _const_spec((1, PAIR)), _const_spec((1, PAIR))]
        + [ANY] * len(plumb.args),
        out_specs=[pl.BlockSpec((Q_BLOCK, A_WIDTH), lambda b, i: (b * nq + i, 0)),
                   pl.BlockSpec((None, N_HEADS, Q_BLOCK), lambda b, i: (b * nq + i, 0, 0))]
        + [ANY] * len(plumb.out_shape),
        out_shape=[_sds((t, A_WIDTH), BF16), _sds((t // Q_BLOCK, N_HEADS, Q_BLOCK), F32)] + plumb.out_shape,
        scratch_shapes=[pltpu.VMEM((s, kw), BF16), pltpu.VMEM((N_HEADS, nkc, PAIR, KEY_CHUNK), BF16),
                        pltpu.VMEM((w, 2 * Q_BLOCK), F32), pltpu.VMEM((2 * w, Q_BLOCK), BF16)] + plumb.scratch,
        input_output_aliases=plumb.aliases,
        compiler_params=plumb.params(vmem_limit_bytes=VMEM_LIMIT_V7X),
    )(qkv, qkv, qkv, bias2, sink2, gq, gk, *plumb.args)
    return plumb.deliver(res)


def _attn_bwd(name, qkv, bias2, sink2, gq, gk, y, dy, lse, cfg, n_batch, want_dbias, comms=()):
    t = qkv.shape[0]
    s = t // n_batch
    nq = s // Q_BLOCK
    nkc = s // KEY_CHUNK
    prev, group, kw = cfg["prev"], cfg["group"], cfg["kw"]
    w = prev + Q_BLOCK
    n_chunks = w // KEY_CHUNK
    wext = bias2.shape[1]
    plumb = _CommPlumbing(comms, 10, 7, 9)
    n_all_in = 10 + len(plumb.args)
    n_all_out = 7 + len(plumb.out_shape)

    def body(*refs):
        q_ref, k_ref, v_ref, bias_ref, sink_ref, gq_ref, gk_ref, y_ref, dy_ref, lse_ref = refs[:10]
        dq_ref, dk_ref, dv_ref, db_ref, dsink_ref, dgq_ref, dgk_ref = refs[n_all_in:n_all_in + 7]
        kn_ref, knt_ref, dkn_ref, dvs_ref, s_ref, dp_ref, pb_ref, dsb_ref, dst_ref = \
            refs[n_all_in + n_all_out:n_all_in + n_all_out + 9]
        b = pl.program_id(0)
        i = pl.program_id(1)
        first = (b == 0) & (i == 0)
        comm_refs = (refs[:n_all_in], refs[n_all_in:n_all_in + n_all_out], refs[n_all_in + n_all_out:])
        plumb.start_at(*comm_refs, first)

        @pl.when(i == 0)
        def _():
            for jk in range(kw // PAIR):
                cols = pl.ds(jk * PAIR, PAIR)
                heads = [h for h in range(N_HEADS) if (h // group) // 2 == jk]
                for c in range(nkc):
                    rows = pl.ds(c * KEY_CHUNK, KEY_CHUNK)
                    kn, _, _ = _pair_norm(k_ref[rows, cols].astype(F32), gk_ref[...])
                    kn_ref[rows, cols] = kn.astype(BF16)
                    _place_transposed(kn, knt_ref, c, heads, group)
            dkn_ref[...] = jnp.zeros(dkn_ref.shape, F32)
            dvs_ref[...] = jnp.zeros(dvs_ref.shape, F32)

        @pl.when(first)
        def _():
            db_ref[...] = jnp.zeros(db_ref.shape, F32)
            dsink_ref[...] = jnp.zeros(dsink_ref.shape, F32)
            dgq_ref[...] = jnp.zeros(dgq_ref.shape, F32)
            dgk_ref[...] = jnp.zeros(dgk_ref.shape, F32)

        start, off = _attn_window(i, prev)
        c0 = start // KEY_CHUNK
        for jq in range(N_HEADS // 2):
            cols = pl.ds(jq * PAIR, PAIR)
            kcols = pl.ds((((2 * jq) // group) // 2) * PAIR, PAIR)
            qn, q_hat, q_rstd = _pair_norm(q_ref[:, cols].astype(F32), gq_ref[...])
            qs = _stack_pair(qn * ATTN_SCALE, jq, group)
            do_pair = dy_ref[:, cols].astype(F32)
            dos = _stack_pair(do_pair, jq, group)
            prod_t = (do_pair * y_ref[:, cols].astype(F32)).T
            lo = _sub_lo(prod_t.shape)
            delta2 = jnp.concatenate([jnp.sum(jnp.where(lo, prod_t, 0.0), axis=0, keepdims=True),
                                      jnp.sum(jnp.where(lo, 0.0, prod_t), axis=0, keepdims=True)], axis=1)
            lse2 = jnp.concatenate([lse_ref[2 * jq:2 * jq + 1, :], lse_ref[2 * jq + 1:2 * jq + 2, :]], axis=1)
            dsk = -jnp.exp(sink_ref[jq, 0:1, :] - lse2) * delta2
            dsink_ref[2 * jq:2 * jq + 1, :] += dsk[:, :Q_BLOCK]
            dsink_ref[2 * jq + 1:2 * jq + 2, :] += dsk[:, Q_BLOCK:]
            rows_w = pl.ds(start, w)
            s_ref[...] = _dot(kn_ref[rows_w, kcols], qs, "nt")
            dp_ref[...] = _dot(v_ref[rows_w, kcols], dos, "nt")
            for c in range(n_chunks):
                r = pl.ds(c * KEY_CHUNK, KEY_CHUNK)
                brows = pl.ds(off + c * KEY_CHUNK, KEY_CHUNK)
                p = jnp.exp(s_ref[r, :] + bias_ref[jq, brows, :] - lse2)
                ds = p * (dp_ref[r, :] - delta2)
                if want_dbias:
                    db_ref[jq, brows, :] += ds
                ds_b = ds.astype(BF16)
                pb_ref[r, :] = p.astype(BF16)
                dsb_ref[r, :] = ds_b
                dst_ref[pl.ds(2 * c * KEY_CHUNK, KEY_CHUNK), :] = ds_b[:, :Q_BLOCK]
                dst_ref[pl.ds((2 * c + 1) * KEY_CHUNK, KEY_CHUNK), :] = ds_b[:, Q_BLOCK:]
            dkn_ref[rows_w, kcols] += _dot(dsb_ref[...], qs, "nn")
            dvs_ref[rows_w, kcols] += _dot(pb_ref[...], dos, "nn")
            kl = jnp.concatenate([knt_ref[2 * jq + hq, c0 + c] for c in range(n_chunks) for hq in range(2)], axis=1)
            dqt = _dot(kl, dst_ref[...], "nn")
            dq_raw, dg = _pair_norm_bwd(dqt.T * ATTN_SCALE, q_hat, q_rstd, gq_ref[...])
            dq_ref[:, cols] = dq_raw.astype(BF16)
            dgq_ref[...] += dg

        @pl.when(i == nq - 1)
        def _():
            for jk in range(kw // PAIR):
                kcols = pl.ds(jk * PAIR, PAIR)
                _, k_hat, k_rstd = _pair_norm(k_ref[:, kcols].astype(F32), gk_ref[...])
                dk_raw, dg = _pair_norm_bwd(dkn_ref[:, kcols], k_hat, k_rstd, gk_ref[...])
                dk_ref[:, kcols] = dk_raw.astype(BF16)
                dgk_ref[...] += dg
            dv_ref[...] = dvs_ref[...].astype(BF16)

        plumb.finish_at(*comm_refs, (b == n_batch - 1) & (i == nq - 1))

    q_spec, k_spec, v_spec = _attn_specs(cfg, s, nq)
    row = pl.BlockSpec((Q_BLOCK, A_WIDTH), lambda b, i: (b * nq + i, 0))
    kv_out = pl.BlockSpec((s, kw), lambda b, i: (b, 0))
    pair_bias = _const_spec((N_HEADS // 2, wext, 2 * Q_BLOCK))
    res = pl.pallas_call(
        body, name=name, grid=(n_batch, nq),
        in_specs=[q_spec, k_spec, v_spec, pair_bias, _const_spec((N_HEADS // 2, 8, 2 * Q_BLOCK)),
                  _const_spec((1, PAIR)), _const_spec((1, PAIR)), row, row,
                  pl.BlockSpec((None, N_HEADS, Q_BLOCK), lambda b, i: (b * nq + i, 0, 0))] + [ANY] * len(plumb.args),
        out_specs=[row, kv_out, kv_out, pair_bias, _const_spec((N_HEADS, 128)),
                   _const_spec((1, PAIR)), _const_spec((1, PAIR))] + [ANY] * len(plumb.out_shape),
        out_shape=[_sds((t, A_WIDTH), BF16), _sds((t, kw), BF16), _sds((t, kw), BF16),
                   _sds((N_HEADS // 2, wext, 2 * Q_BLOCK), F32), _sds((N_HEADS, 128), F32),
                   _sds((1, PAIR), F32), _sds((1, PAIR), F32)] + plumb.out_shape,
        scratch_shapes=[pltpu.VMEM((s, kw), BF16), pltpu.VMEM((N_HEADS, nkc, PAIR, KEY_CHUNK), BF16),
                        pltpu.VMEM((s, kw), F32), pltpu.VMEM((s, kw), F32),
                        pltpu.VMEM((w, 2 * Q_BLOCK), F32), pltpu.VMEM((w, 2 * Q_BLOCK), F32),
                        pltpu.VMEM((w, 2 * Q_BLOCK), BF16), pltpu.VMEM((w, 2 * Q_BLOCK), BF16),
                        pltpu.VMEM((2 * w, Q_BLOCK), BF16)] + plumb.scratch,
        input_output_aliases=plumb.aliases,
        compiler_params=plumb.params(vmem_limit_bytes=VMEM_LIMIT_V7X),
    )(qkv, qkv, qkv, bias2, sink2, gq, gk, y, dy, lse, *plumb.args)
    return plumb.deliver(res)


def _band_tables(prev_chunks):
    prev = prev_chunks * CHUNK
    wext = 2 * prev + Q_BLOCK
    jj = np.arange(wext)[:, None]
    ii = np.arange(Q_BLOCK)[None, :]
    dist = prev + ii - jj
    rel_chunk = (prev // CHUNK + ii // CHUNK) - jj // CHUNK
    allowed = (rel_chunk >= 0) & (rel_chunk <= prev_chunks)
    return dist, allowed


def _alibi_slopes():
    return np.array([2.0 ** (-8.0 * (h + 1) / N_HEADS) for h in range(N_HEADS)], dtype=np.float32)


def _diag_onehot(prev, wext):
    n_diag = wext + Q_BLOCK - 1
    idx = np.clip(prev + Q_BLOCK - 1 - np.arange(n_diag), -A_MAX_REL, A_MAX_REL) + A_MAX_REL
    onehot = np.zeros((n_diag, 2 * A_MAX_REL + 1), np.float32)
    onehot[np.arange(n_diag), idx] = 1.0
    return onehot


def _bias_a(rel_bias):
    prev = A_PREV_CHUNKS * CHUNK
    _, allowed = _band_tables(A_PREV_CHUNKS)
    wext = allowed.shape[0]
    n_diag = wext + Q_BLOCK - 1
    seq = jnp.dot(rel_bias, jnp.asarray(_diag_onehot(prev, wext).T), precision=lax.Precision.HIGHEST)
    seq = jnp.pad(seq, ((0, 0), (0, 1)))
    rows = jnp.broadcast_to(seq[:, None, :], (N_HEADS, Q_BLOCK, n_diag + 1)).reshape(N_HEADS, -1)
    skew = rows[:, :Q_BLOCK * n_diag].reshape(N_HEADS, Q_BLOCK, n_diag)
    tile = jnp.transpose(skew[:, :, Q_BLOCK - 1:Q_BLOCK - 1 + wext], (0, 2, 1))
    return jnp.where(jnp.asarray(allowed)[None], tile, NEG_INF)


def _bias_b():
    dist, allowed = _band_tables(B_PREV_CHUNKS)
    bias = -_alibi_slopes()[:, None, None] * np.abs(dist).astype(np.float32)[None]
    return jnp.asarray(np.where(allowed[None], bias, np.float32(NEG_INF)).astype(np.float32))


def _rel_bias_grad(db_t):
    prev = A_PREV_CHUNKS * CHUNK
    wext = db_t.shape[1]
    n_diag = wext + Q_BLOCK - 1
    wp = n_diag + Q_BLOCK - 1
    xp = jnp.pad(jnp.transpose(db_t, (0, 2, 1)), ((0, 0), (0, 0), (Q_BLOCK - 1, Q_BLOCK - 1)))
    flat = jnp.pad(xp.reshape(N_HEADS, Q_BLOCK * wp), ((0, 0), (0, Q_BLOCK)))
    skew = flat.reshape(N_HEADS, Q_BLOCK, wp + 1)[:, :, :n_diag]
    diag = jnp.sum(skew, axis=1)
    return jnp.dot(diag, jnp.asarray(_diag_onehot(prev, wext)), precision=lax.Precision.HIGHEST)


def _ew(name, fn, ins, out_dtypes):
    r, c = ins[0].shape
    rb = _pick(r, max(16, (1 << 19) // c), 16)
    spec = pl.BlockSpec((rb, c), lambda i: (i, 0))

    def body(*refs):
        vals = fn(*[ref[...] for ref in refs[:len(ins)]])
        for ref, val in zip(refs[len(ins):], vals):
            ref[...] = val.astype(ref.dtype)

    return pl.pallas_call(
        body, name=name, grid=(r // rb,), in_specs=[spec] * len(ins), out_specs=[spec] * len(out_dtypes),
        out_shape=[_sds((r, c), dt) for dt in out_dtypes], compiler_params=_params(),
    )(*ins)


def _cast_into_slot(name, w, chip):
    r, c = w.shape
    rb = _pick(r, max(16, (1 << 19) // c), 16)

    def body(chip_ref, w_ref, o_ref):
        o_ref[...] = w_ref[...].astype(BF16)

    return pl.pallas_call(
        body, name=name,
        grid_spec=pltpu.PrefetchScalarGridSpec(
            num_scalar_prefetch=1, grid=(r // rb,),
            in_specs=[pl.BlockSpec((rb, c), lambda i, chip_ref: (i, 0))],
            out_specs=pl.BlockSpec((None, rb, c), lambda i, chip_ref: (chip_ref[0], i, 0))),
        out_shape=_sds((N_CHIPS, r, c), BF16), compiler_params=_params(),
    )(chip, w)


def _adamw_math(w, g, m, v):
    m = ADAM_B1 * m + (1.0 - ADAM_B1) * g
    v = ADAM_B2 * v + (1.0 - ADAM_B2) * (g * g)
    m_hat = m / (1.0 - ADAM_B1 ** ADAM_STEP)
    v_hat = v / (1.0 - ADAM_B2 ** ADAM_STEP)
    delta = -ADAM_LR * (m_hat / (jnp.sqrt(v_hat) + ADAM_EPS) + ADAM_WD * w)
    return delta, m, v


def _adamw_terms(name, terms, w, m, v):
    r, c = w.shape
    hr = r // 2
    rb = _pick(hr, max(16, (1 << 19) // c), 16)
    nb = hr // rb

    def body(t_ref, w_ref, m_ref, v_ref, g_ref, d_ref, nm_ref, nv_ref):
        g = t_ref[0].astype(F32)
        for k in range(1, N_CHIPS):
            g = g + t_ref[k].astype(F32)
        delta, nm, nv = _adamw_math(w_ref[...], g, m_ref[...], v_ref[...])
        g_ref[...] = g
        d_ref[...] = delta
        nm_ref[...] = nm
        nv_ref[...] = nv

    spec = pl.BlockSpec((rb, c), lambda h, i: (h * nb + i, 0))
    return pl.pallas_call(
        body, name=name, grid=(2, nb),
        in_specs=[pl.BlockSpec((None, N_CHIPS, rb, c), lambda h, i: (h, 0, i, 0)), spec, spec, spec],
        out_specs=[spec] * 4, out_shape=[_sds((r, c), F32)] * 4, compiler_params=_params(),
    )(terms, w, m, v)


def _mesh_place():
    x, y, c = lax.axis_index("x"), lax.axis_index("y"), lax.axis_index("c")
    chips = [(x, 1 - y), (1 - x, y), (1 - x, 1 - y)]
    return x, y, c, chips


def _all_gather_weights(bufs):
    n = len(bufs)

    def body(*refs):
        outs = refs[n:2 * n]
        ici_send, ici_recv, d2d_send, d2d_recv = refs[2 * n:]
        x, y, c, chips = _mesh_place()
        me = 2 * x + y
        sibling = (x, y, 1 - c)
        barrier = pltpu.get_barrier_semaphore()
        for peer in [sibling] + [(tx, ty, c) for tx, ty in chips]:
            pl.semaphore_signal(barrier, inc=1, device_id=peer, device_id_type=MESH)
        pl.semaphore_wait(barrier, N_CHIPS)
        sent = []
        for wi in range(n):
            for k, (tx, ty) in enumerate(chips):
                own = outs[wi].at[me, c]
                cp = pltpu.make_async_remote_copy(
                    src_ref=own, dst_ref=own, send_sem=ici_send.at[wi * 3 + k], recv_sem=ici_recv.at[wi * 3 + k],
                    device_id=(tx, ty, c), device_id_type=MESH)
                cp.start()
                sent.append(cp)
        passed = []
        for wi in range(n):
            for k, (tx, ty) in enumerate(chips):
                slab = outs[wi].at[2 * tx + ty, c]
                pltpu.make_async_remote_copy(
                    src_ref=slab, dst_ref=slab, send_sem=ici_send.at[wi * 3 + k], recv_sem=ici_recv.at[wi * 3 + k],
                    device_id=(tx, ty, c), device_id_type=MESH).wait_recv()
                fw = pltpu.make_async_remote_copy(
                    src_ref=slab, dst_ref=slab, send_sem=d2d_send.at[wi * 3 + k], recv_sem=d2d_recv.at[wi * 3 + k],
                    device_id=sibling, device_id_type=MESH)
                fw.start()
                passed.append(fw)
        for wi in range(n):
            for k, (tx, ty) in enumerate(chips):
                slab = outs[wi].at[2 * tx + ty, 1 - c]
                pltpu.make_async_remote_copy(
                    src_ref=slab, dst_ref=slab, send_sem=d2d_send.at[wi * 3 + k], recv_sem=d2d_recv.at[wi * 3 + k],
                    device_id=sibling, device_id_type=MESH).wait_recv()
        for cp in sent + passed:
            cp.wait_send()

    return pl.pallas_call(
        body, name="all_gather_weights",
        in_specs=[ANY] * n, out_specs=[ANY] * n,
        out_shape=[_sds(g.shape, g.dtype) for g in bufs],
        scratch_shapes=[pltpu.SemaphoreType.DMA((3 * n,))] * 4,
        input_output_aliases={i: i for i in range(n)},
        compiler_params=pltpu.CompilerParams(collective_id=COLLECTIVE_IDS[("chips", "sibling")]),
    )(*bufs)


def _run_comms(name, comms):
    plumb = _CommPlumbing(comms, 0, 0, 0)
    n_in, n_out = len(plumb.args), len(plumb.out_shape)

    def body(*refs):
        parts = []
        i0, o0, s0 = 0, n_in, n_in + n_out
        for cm in plumb.comms:
            parts.append((refs[i0:i0 + len(cm.ins)], refs[o0:o0 + len(cm.outs)], refs[s0:s0 + len(cm.sems)]))
            i0 += len(cm.ins)
            o0 += len(cm.outs)
            s0 += len(cm.sems)
        plumb.handshake()
        for cm, part in zip(plumb.comms, parts):
            cm.start(*part)
        for cm, part in zip(plumb.comms, parts):
            cm.finish(*part)

    res = pl.pallas_call(
        body, name=name, in_specs=[ANY] * n_in, out_specs=[ANY] * n_out, out_shape=plumb.out_shape,
        scratch_shapes=plumb.scratch, input_output_aliases=plumb.aliases, compiler_params=plumb.params(),
    )(*plumb.args)
    plumb.deliver(res)


def _gather_ici(bufs):
    n = len(bufs)

    def copies(outs, sems):
        send_sem, recv_sem = sems
        x, y, c, chips = _mesh_place()
        me = 2 * x + y
        sends, recvs = [], []
        for wi in range(n):
            for k, (tx, ty) in enumerate(chips):
                sems_k = dict(send_sem=send_sem.at[wi * 3 + k], recv_sem=recv_sem.at[wi * 3 + k],
                              device_id=(tx, ty, c), device_id_type=MESH)
                own = outs[wi].at[me, c]
                sends.append(pltpu.make_async_remote_copy(src_ref=own, dst_ref=own, **sems_k))
                slab = outs[wi].at[2 * tx + ty, c]
                recvs.append(pltpu.make_async_remote_copy(src_ref=slab, dst_ref=slab, **sems_k))
        return sends, recvs

    def start(ins, outs, sems):
        for cp in copies(outs, sems)[0]:
            cp.start()

    def finish(ins, outs, sems):
        sends, recvs = copies(outs, sems)
        for cp in recvs:
            cp.wait_recv()
        for cp in sends:
            cp.wait_send()

    return _Comm("chips", bufs, [_sds(g.shape, g.dtype) for g in bufs], {i: i for i in range(n)},
                 [pltpu.SemaphoreType.DMA((3 * n,)), pltpu.SemaphoreType.DMA((3 * n,))], start, finish)


def _gather_d2d(gathered):
    n = len(gathered)

    def copies(outs, sems):
        send_sem, recv_sem = sems
        x, y, c, chips = _mesh_place()
        sends, recvs = [], []
        for wi in range(n):
            for k, (tx, ty) in enumerate(chips):
                sems_k = dict(send_sem=send_sem.at[wi * 3 + k], recv_sem=recv_sem.at[wi * 3 + k],
                              device_id=(x, y, 1 - c), device_id_type=MESH)
                mine = outs[wi].at[2 * tx + ty, c]
                theirs = outs[wi].at[2 * tx + ty, 1 - c]
                sends.append(pltpu.make_async_remote_copy(src_ref=mine, dst_ref=mine, **sems_k))
                recvs.append(pltpu.make_async_remote_copy(src_ref=theirs, dst_ref=theirs, **sems_k))
        return sends, recvs

    def start(ins, outs, sems):
        for cp in copies(outs, sems)[0]:
            cp.start()

    def finish(ins, outs, sems):
        sends, recvs = copies(outs, sems)
        for cp in recvs:
            cp.wait_recv()
        for cp in sends:
            cp.wait_send()

    return _Comm("sibling", gathered, [_sds(g.shape, g.dtype) for g in gathered], {i: i for i in range(n)},
                 [pltpu.SemaphoreType.DMA((3 * n,)), pltpu.SemaphoreType.DMA((3 * n,))], start, finish)


def _exchange_halves(grads):
    n = len(grads)

    def copies(ins, outs, sems):
        send_sem, recv_sem = sems
        x, y, c, _ = _mesh_place()
        return [pltpu.make_async_remote_copy(
            src_ref=ins[wi].at[t, 1 - c], dst_ref=outs[wi].at[t],
            send_sem=send_sem.at[wi * N_CHIPS + t], recv_sem=recv_sem.at[wi * N_CHIPS + t],
            device_id=(x, y, 1 - c), device_id_type=MESH) for wi in range(n) for t in range(N_CHIPS)]

    def start(ins, outs, sems):
        for cp in copies(ins, outs, sems):
            cp.start()

    def finish(ins, outs, sems):
        for cp in copies(ins, outs, sems):
            cp.wait()

    return _Comm("sibling", grads, [_sds((N_CHIPS,) + g.shape[2:], g.dtype) for g in grads], {},
                 [pltpu.SemaphoreType.DMA((N_CHIPS * n,)), pltpu.SemaphoreType.DMA((N_CHIPS * n,))], start, finish)


def _scatter_ici(sums):
    n = len(sums)

    def copies(ins, outs, sems):
        local_sem, send_sem, recv_sem = sems
        x, y, c, chips = _mesh_place()
        me = 2 * x + y
        local, sends, recvs = [], [], []
        for wi in range(n):
            local.append(pltpu.make_async_copy(ins[wi].at[me], outs[wi].at[c, 0], local_sem.at[wi]))
            for k, (tx, ty) in enumerate(chips):
                sems_k = dict(send_sem=send_sem.at[wi * 3 + k], recv_sem=recv_sem.at[wi * 3 + k],
                              device_id=(tx, ty, c), device_id_type=MESH)
                land = outs[wi].at[c, k + 1]
                sends.append(pltpu.make_async_remote_copy(src_ref=ins[wi].at[2 * tx + ty], dst_ref=land, **sems_k))
                recvs.append(pltpu.make_async_remote_copy(src_ref=land, dst_ref=land, **sems_k))
        return local, sends, recvs

    def start(ins, outs, sems):
        local, sends, _ = copies(ins, outs, sems)
        for cp in local + sends:
            cp.start()

    def finish(ins, outs, sems):
        local, sends, recvs = copies(ins, outs, sems)
        for cp in local:
            cp.wait()
        for cp in recvs:
            cp.wait_recv()
        for cp in sends:
            cp.wait_send()

    return _Comm("chips", sums, [_sds((2, N_CHIPS) + s.shape[1:], s.dtype) for s in sums], {},
                 [pltpu.SemaphoreType.DMA((n,)), pltpu.SemaphoreType.DMA((3 * n,)), pltpu.SemaphoreType.DMA((3 * n,))],
                 start, finish)


def _scatter_d2d(terms):
    n = len(terms)

    def copies(outs, sems):
        send_sem, recv_sem = sems
        x, y, c, _ = _mesh_place()
        sends, recvs = [], []
        for wi in range(n):
            sems_w = dict(send_sem=send_sem.at[wi], recv_sem=recv_sem.at[wi],
                          device_id=(x, y, 1 - c), device_id_type=MESH)
            sends.append(pltpu.make_async_remote_copy(src_ref=outs[wi].at[c], dst_ref=outs[wi].at[c], **sems_w))
            recvs.append(pltpu.make_async_remote_copy(src_ref=outs[wi].at[1 - c], dst_ref=outs[wi].at[1 - c], **sems_w))
        return sends, recvs

    def start(ins, outs, sems):
        for cp in copies(outs, sems)[0]:
            cp.start()

    def finish(ins, outs, sems):
        sends, recvs = copies(outs, sems)
        for cp in recvs:
            cp.wait_recv()
        for cp in sends:
            cp.wait_send()

    return _Comm("sibling", terms, [_sds(t.shape, t.dtype) for t in terms], {i: i for i in range(n)},
                 [pltpu.SemaphoreType.DMA((n,)), pltpu.SemaphoreType.DMA((n,))], start, finish)


def _chip_sum(name, grad, got, core):
    _, _, hr, c = grad.shape
    rb = _pick(hr, max(16, (1 << 19) // c), 16)

    def body(core_ref, a_ref, b_ref, o_ref):
        o_ref[...] = (a_ref[...].astype(F32) + b_ref[...].astype(F32)).astype(BF16)

    out_spec = pl.BlockSpec((None, rb, c), lambda t, i, core_ref: (t, i, 0))
    return pl.pallas_call(
        body, name=name,
        grid_spec=pltpu.PrefetchScalarGridSpec(
            num_scalar_prefetch=1, grid=(N_CHIPS, hr // rb),
            in_specs=[pl.BlockSpec((None, None, rb, c), lambda t, i, core_ref: (t, core_ref[0], i, 0)), out_spec],
            out_specs=out_spec),
        out_shape=_sds((N_CHIPS, hr, c), BF16), compiler_params=_params(),
    )(core, grad, got)


def _all_reduce_small(pack):
    r = pack.shape[0]

    def body(p_ref, o_ref, land_ref, send_sem, recv_sem):
        x, y, c, _ = _mesh_place()
        me = 4 * x + 2 * y + c
        flips = [(k >> 2 & 1, k >> 1 & 1, k & 1) for k in range(1, N_DEV)]

        def peer(fx, fy, fc):
            return (1 - x if fx else x, 1 - y if fy else y, 1 - c if fc else c)

        land_ref[me] = p_ref[...]
        sent = []
        for k, flip in enumerate(flips):
            cp = pltpu.make_async_remote_copy(
                src_ref=p_ref, dst_ref=land_ref.at[me], send_sem=send_sem.at[k], recv_sem=recv_sem.at[k],
                device_id=peer(*flip), device_id_type=MESH)
            cp.start()
            sent.append(cp)
        for k, flip in enumerate(flips):
            px, py, pc = peer(*flip)
            slot = land_ref.at[4 * px + 2 * py + pc]
            pltpu.make_async_remote_copy(
                src_ref=slot, dst_ref=slot, send_sem=send_sem.at[k], recv_sem=recv_sem.at[k],
                device_id=(px, py, pc), device_id_type=MESH).wait_recv()
        total = land_ref[0]
        for d in range(1, N_DEV):
            total = total + land_ref[d]
        o_ref[...] = total
        for cp in sent:
            cp.wait_send()

    vmem = pl.BlockSpec(memory_space=pltpu.VMEM)
    return pl.pallas_call(
        body, name="all_reduce_small", in_specs=[vmem], out_specs=vmem, out_shape=_sds((r, 128), F32),
        scratch_shapes=[pltpu.VMEM((N_DEV, r, 128), F32), pltpu.SemaphoreType.DMA((N_DEV - 1,)),
                        pltpu.SemaphoreType.DMA((N_DEV - 1,))],
    )(pack)


PACK_TILE = 8 * 128


def _pack(items):
    rows, i = [], 0
    while i < len(items):
        j = i
        while j < len(items) and items[j].size == items[i].size:
            j += 1
        group = jnp.stack([it.reshape(-1).astype(F32) for it in items[i:j]])
        rows.append(jnp.pad(group, ((0, 0), (0, -group.shape[1] % PACK_TILE))).reshape(-1, 128))
        i = j
    return jnp.concatenate(rows, axis=0)


def _unpack(pack, shapes):
    out, row = [], 0
    for shp in shapes:
        size = int(np.prod(shp))
        nrow = -(-size // PACK_TILE) * (PACK_TILE // 128)
        out.append(pack[row:row + nrow].reshape(-1)[:size].reshape(shp))
        row += nrow
    return out


BIG = ["ffn1_w_gu", "ffn1_w_down", "w_in", "w_gate", "w_proj_a", "w_proj_b", "w_out",
       "ffn2_w_gu", "ffn2_w_down", "w_ple_gate", "w_ple_proj"]
SMALL = ["ffn1_norm", "mix_norm", "ffn2_norm", "ple_norm", "a_q_norm", "a_k_norm", "b_q_norm", "b_k_norm",
         "a_rel_bias", "b_sinks"]
WEIGHTS = ["ffn1_norm", "ffn1_w_gu", "ffn1_w_down", "mix_norm", "w_in", "a_q_norm", "a_k_norm", "a_rel_bias",
           "b_q_norm", "b_k_norm", "b_sinks", "w_gate", "w_proj_a", "w_proj_b", "w_out", "ffn2_norm",
           "ffn2_w_gu", "ffn2_w_down", "ple_norm", "w_ple_gate", "w_ple_proj"]
ATTN_A = dict(prev=A_PREV_CHUNKS * CHUNK, group=1, kw=A_WIDTH, qblk=0, kblk=1, vblk=2)
ATTN_B = dict(prev=B_PREV_CHUNKS * CHUNK, group=N_HEADS // B_KV_HEADS, kw=B_KV_WIDTH, qblk=3,
              kblk=4 * A_WIDTH // B_KV_WIDTH, vblk=4 * A_WIDTH // B_KV_WIDTH + 1)


def _cast_epilogue(accs, extras, outs, ij):
    for acc, out in zip(accs, outs):
        out[...] = acc.astype(out.dtype)


GATHER_FIRST = ["ffn1_w_gu", "ffn1_w_down"]
ROW_SHARDED = ("ffn1_w_down", "ffn2_w_down", "w_out", "w_ple_gate")


def _slotted(name, grad):
    if name == "w_in":
        rows, cols = grad.shape
        grad = jnp.transpose(grad.reshape(rows, N_CHIPS, cols // N_CHIPS), (1, 0, 2))
    elif name in ROW_SHARDED:
        grad = grad.reshape(N_CHIPS, grad.shape[0] // N_CHIPS, grad.shape[1])
    return grad.reshape(N_CHIPS, 2, grad.shape[1] // 2, grad.shape[2])


def _local_step(xt, pt, tgt, n_batch, bufs, small, core):
    t, d = xt.shape
    tm = _pick(t, ROW_TILE, 8)
    tk = _pick(t, ROW_TILE, 8)
    nt = t // tm
    row = pl.BlockSpec((tm, d), lambda i, j, k: (i, 0))
    gs = bufs["w_gate"].shape[2]
    ps = bufs["w_proj_a"].shape[2]
    es = bufs["w_ple_proj"].shape[2]
    pdim = pt.shape[1]
    ncols = N_CHIPS * bufs["w_in"].shape[2]
    tin = ncols // 2
    assert 2 * gs == d and 4 * ps == d and 4 * es == d and tin % 128 == 0

    w = {}
    halves = {n: b.reshape(N_CHIPS, 2, b.shape[1] // 2, b.shape[2]) for n, b in bufs.items()}

    def publish(names, arrays):
        for name, g in zip(names, arrays):
            g = g.reshape(N_CHIPS, 2 * g.shape[2], g.shape[3])
            if name in ROW_SHARDED:
                g = g.reshape(N_CHIPS * g.shape[1], g.shape[2])
            elif name == "w_in":
                g = jnp.transpose(g, (1, 0, 2)).reshape(g.shape[1], N_CHIPS * g.shape[2])
            w[name] = g

    class GatherPipe:
        def __init__(self, names):
            self.names = names
            self.stage = None

        def ici(self):
            self.stage = _gather_ici(self.bufs())
            return self.stage

        def d2d(self):
            self.stage = _gather_d2d(self.bufs())
            return self.stage

        def bufs(self):
            return self.stage.results if self.stage is not None else [halves[n] for n in self.names]

        def publish(self):
            publish(self.names, self.stage.results)

    class GradPipe:
        def __init__(self, names):
            self.names = names

        def exchange(self, grads):
            self.grads = [_slotted(n, g) for n, g in zip(self.names, grads)]
            self.x = _exchange_halves(self.grads)
            return self.x

        def scatter(self):
            self.sums = [_chip_sum("chip_sum_" + n, g, got, core)
                         for n, g, got in zip(self.names, self.grads, self.x.results)]
            self.s = _scatter_ici(self.sums)
            return self.s

        def forward(self):
            self.f = _scatter_d2d(self.s.results)
            return self.f

        def terms(self):
            return dict(zip(self.names, self.f.results))

    publish(GATHER_FIRST, _all_gather_weights([halves[n] for n in GATHER_FIRST]))
    g_in, g_proj, g_ple = GatherPipe(["w_in", "w_gate"]), GatherPipe(["w_proj_a", "w_proj_b", "w_out"]), \
        GatherPipe(["w_ple_gate", "w_ple_proj"])
    g_down2, g_up2 = GatherPipe(["ffn2_w_down"]), GatherPipe(["ffn2_w_gu"])
    n1 = _rms_fwd("ffn1_norm", xt, small["ffn1_norm"])
    h1, un, ffn1_saved = _ffn_fwd("ffn1", xt, n1, w["ffn1_w_gu"], w["ffn1_w_down"], small["mix_norm"],
                                  {"up": lambda: [g_in.ici()], "down": lambda: [g_in.d2d(), g_proj.ici()]})
    g_in.publish()
    w_in, wgate = w["w_in"], w["w_gate"]
    (qkv,) = _mm(
        "qkv", "nn", (nt, 2, 1),
        [(un, row, w_in, pl.BlockSpec((d, tin), lambda i, j, k: (0, j)))], [],
        [(_sds((t, ncols), BF16), pl.BlockSpec((tm, tin), lambda i, j, k: (i, j)))], (tm, tin), _cast_epilogue,
        j_outer=True, comms=[g_proj.d2d(), g_ple.ici()])
    g_proj.publish()
    wpa, wpb, wout = w["w_proj_a"], w["w_proj_b"], w["w_out"]

    def gate_epilogue(accs, extras, outs, ij):
        outs[0][...] = jax.nn.sigmoid(accs[0]).astype(BF16)

    (gates,) = _mm(
        "gate", "nn", (nt, 4, 1),
        [(un, row, wgate, pl.BlockSpec((None, d, gs), lambda i, j, k: (j, 0, 0)))], [],
        [(_sds((2, t, d), BF16), pl.BlockSpec((None, tm, gs), lambda i, j, k: (j // 2, i, j % 2)))],
        (tm, gs), gate_epilogue, j_outer=True, chunked=True, comms=[g_ple.d2d(), g_down2.ici()])
    g_ple.publish()
    wpg, wpe = w["w_ple_gate"], w["w_ple_proj"]

    bias_a = _pair_bias(_bias_a(small["a_rel_bias"][0]))
    bias_b = _pair_bias(_bias_b())
    sink_a = _pair_rows(jnp.full((N_HEADS, 128), NEG_INF, F32))
    sink_b = _pair_rows(jnp.broadcast_to(small["b_sinks"][0][:, None], (N_HEADS, 128)))
    gqa, gka, gqb, gkb = [jnp.tile(small[k], (1, 2)) for k in ("a_q_norm", "a_k_norm", "b_q_norm", "b_k_norm")]
    ya, lse_a = _attn_fwd("attn_a_fwd", qkv, bias_a, sink_a, gqa, gka, ATTN_A, n_batch,
                          comms=[g_down2.d2d(), g_up2.ici()])
    g_down2.publish()
    yb, lse_b = _attn_fwd("attn_b_fwd", qkv, bias_b, sink_b, gqb, gkb, ATTN_B, n_batch, comms=[g_up2.d2d()])
    g_up2.publish()

    def merge_epilogue(accs, extras, outs, ij):
        pa, pb = accs
        outs[0][...] = (extras[0][...].astype(F32) * pa + extras[1][...].astype(F32) * pb).astype(BF16)
        outs[1][...] = pa.astype(BF16)
        outs[2][...] = pb.astype(BF16)

    y_spec = pl.BlockSpec((tm, A_WIDTH), lambda i, j, k: (i, 0))
    proj_spec = pl.BlockSpec((None, A_WIDTH, ps), lambda i, j, k: (j, 0, 0))
    tile_ps = pl.BlockSpec((tm, ps), lambda i, j, k: (i, j))
    merged, pa, pb = _mm(
        "proj_merge", "nn", (nt, 4, 1),
        [(ya, y_spec, wpa, proj_spec), (yb, y_spec, wpb, proj_spec)],
        [(gates, pl.BlockSpec((None, tm, ps), lambda i, j, k: (0, i, j))),
         (gates, pl.BlockSpec((None, tm, ps), lambda i, j, k: (1, i, j)))],
        [(_sds((t, d), BF16), tile_ps)] * 3, (tm, ps), merge_epilogue)

    h2, n2 = _mm(
        "out_proj", "nn", (nt, 1, 1),
        [(merged, row, wout, pl.BlockSpec((d, d), lambda i, j, k: (0, 0)))],
        [(h1, row), (small["ffn2_norm"], pl.BlockSpec((1, d), lambda i, j, k: (0, 0)))],
        [(_sds((t, d), F32), row), (_sds((t, d), BF16), row)], (tm, d), _residual_norm_epilogue(1.0))

    h3, n3, ffn2_saved = _ffn_fwd("ffn2", h2, n2, w["ffn2_w_gu"], w["ffn2_w_down"], small["ple_norm"], {})
    tile_es = pl.BlockSpec((tm, es), lambda i, j, k: (i, j))
    th = _pick(d, 512)

    def head_epilogue(accs, extras, outs, ij):
        h3_ref, tgt_ref = extras
        dy_ref, dpe_ref, dz_ref, loss_ref = outs
        pg = jax.nn.sigmoid(accs[0])
        pev = accs[1]
        diff = h3_ref[...] + pg * pev - tgt_ref[...]
        dy = diff * (1.0 / d)
        dy_ref[...] = dy
        dpe_ref[...] = (dy * pg).astype(BF16)
        dz_ref[...] = (dy * pev * pg * (1.0 - pg)).astype(BF16)
        _accumulate(loss_ref, jnp.full(loss_ref.shape, jnp.sum(diff * diff), F32), (ij[0] == 0) & (ij[1] == 0))

    tile_h = pl.BlockSpec((tm, th), lambda i, j, k: (i, j))
    dy, dpe, dz, loss_acc = _mm(
        "ple_gate_loss", "nn", (nt, 4, 1),
        [(n3, row, wpg, pl.BlockSpec((d, es), lambda i, j, k: (0, j))),
         (pt, pl.BlockSpec((tm, pdim), lambda i, j, k: (i, 0)), wpe, pl.BlockSpec((None, pdim, es), lambda i, j, k: (j, 0, 0)))],
        [(h3, tile_es), (tgt, tile_es)],
        [(_sds((t, d), F32), tile_es), (_sds((t, d), BF16), tile_es), (_sds((t, d), BF16), tile_es),
         (_sds((8, 128), F32), pl.BlockSpec((8, 128), lambda i, j, k: (0, 0)))],
        (tm, es), head_epilogue, j_outer=True, chunked=True)
    loss = 0.5 * loss_acc[0, 0] / d

    nk = t // tk
    (dwpe,) = _mm(
        "d_w_ple_proj", "tn", (1, 4, nk),
        [(pt, pl.BlockSpec((tk, pdim), lambda i, j, k: (k, 0)), dpe, pl.BlockSpec((tk, es), lambda i, j, k: (k, j)))],
        [], [(_sds((4, pdim, es), BF16), pl.BlockSpec((None, pdim, es), lambda i, j, k: (j, 0, 0)))],
        (pdim, es), _cast_epilogue)

    def dense_grad(name, a, dyb, comms=()):
        (res,) = _mm(
            name, "tn", (1, d // th, nk),
            [(a, pl.BlockSpec((tk, d), lambda i, j, k: (k, 0)), dyb, pl.BlockSpec((tk, th), lambda i, j, k: (k, j)))],
            [], [(_sds((d, d), BF16), pl.BlockSpec((d, th), lambda i, j, k: (0, j)))], (d, th), _cast_epilogue,
            comms=comms)
        return res

    dwpg = dense_grad("d_w_ple_gate", n3, dz)
    tmn = _pick(t, ROW_TILE, 8)
    extras, outs = _rms_bwd_io(h3, small["ple_norm"], dy, tmn)
    dh3, dh3_b, d_ple_norm = _mm(
        "d_ple_norm", "nt", (t // tmn, 1, 1),
        [(dz, pl.BlockSpec((tmn, d), lambda i, j, k: (i, 0)), wpg, pl.BlockSpec((d, d), lambda i, j, k: (0, 0)))],
        extras, outs, (tmn, d), _rms_bwd_epilogue)

    up2, down2, ple = GradPipe(["ffn2_w_gu"]), GradPipe(["ffn2_w_down"]), GradPipe(["w_ple_gate", "w_ple_proj"])
    proj = GradPipe(["w_proj_a", "w_proj_b", "w_out"])
    dh2, dh2_b, d_ffn2_norm, dwgu2, dwd2 = _ffn_bwd(
        "ffn2", dh3, dh3_b, h2, small["ffn2_norm"], w["ffn2_w_gu"], w["ffn2_w_down"], ffn2_saved,
        {"dnorm": lambda dwgu, dwd: [up2.exchange([dwgu]), down2.exchange([dwd]), ple.exchange([dwpg, dwpe])]})

    def dmerge_epilogue(accs, extras, outs, ij):
        dmo = accs[0]
        g_ref, pa_ref, pb_ref = extras
        dg_ref, dpa_ref, dpb_ref = outs
        ga = g_ref[0].astype(F32)
        gb = g_ref[1].astype(F32)
        dg_ref[0] = (dmo * pa_ref[...].astype(F32) * ga * (1.0 - ga)).astype(BF16)
        dg_ref[1] = (dmo * pb_ref[...].astype(F32) * gb * (1.0 - gb)).astype(BF16)
        dpa_ref[...] = (dmo * ga).astype(BF16)
        dpb_ref[...] = (dmo * gb).astype(BF16)

    g_spec = pl.BlockSpec((2, tm, th), lambda i, j, k: (0, i, j))
    dgates, dpa, dpb = _mm(
        "d_merge", "nt", (nt, d // th, 1),
        [(dh2_b, row, wout, pl.BlockSpec((th, d), lambda i, j, k: (j, 0)))],
        [(gates, g_spec), (pa, tile_h), (pb, tile_h)],
        [(_sds((2, t, d), BF16), g_spec), (_sds((t, d), BF16), tile_h), (_sds((t, d), BF16), tile_h)],
        (tm, th), dmerge_epilogue, j_outer=True, chunked=True, comms=[down2.scatter()])
    dwout = dense_grad("d_w_out", merged, dh2_b, comms=[down2.forward(), ple.scatter()])

    yk_spec = pl.BlockSpec((tk, A_WIDTH), lambda i, j, k: (k, 0))
    dk_spec = pl.BlockSpec((tk, ps), lambda i, j, k: (k, j))
    dproj = (_sds((4, A_WIDTH, ps), BF16), proj_spec)
    dwpa, dwpb = _mm(
        "d_w_proj", "tn", (1, 4, nk),
        [(ya, yk_spec, dpa, dk_spec), (yb, yk_spec, dpb, dk_spec)], [], [dproj, dproj], (A_WIDTH, ps), _cast_epilogue,
        comms=[ple.forward()])
    dproj_a = pl.BlockSpec((tm, ps), lambda i, j, k: (i, k))
    wproj_k = pl.BlockSpec((None, A_WIDTH, ps), lambda i, j, k: (k, 0, 0))
    dya, dyb = _mm(
        "d_attn_out", "nt", (nt, 1, 4),
        [(dpa, dproj_a, wpa, wproj_k), (dpb, dproj_a, wpb, wproj_k)], [],
        [(_sds((t, A_WIDTH), BF16), y_spec)] * 2, (tm, A_WIDTH), _cast_epilogue,
        comms=[proj.exchange([dwpa, dwpb, dwout])])

    dqa, dka, dva, dbias_a, _, dgqa, dgka = _attn_bwd(
        "attn_a_bwd", qkv, bias_a, sink_a, gqa, gka, ya, dya, lse_a, ATTN_A, n_batch, True,
        comms=[up2.scatter(), proj.scatter()])
    dqb, dkb, dvb, _, dsink_b, dgqb, dgkb = _attn_bwd(
        "attn_b_bwd", qkv, bias_b, sink_b, gqb, gkb, yb, dyb, lse_b, ATTN_B, n_batch, False,
        comms=[up2.forward(), proj.forward()])
    dqkv = jnp.concatenate([dqa, dka, dva, dqb, dkb, dvb], axis=1)

    (dwgate,) = _mm(
        "d_w_gate", "tn", (1, 4, nk),
        [(un, pl.BlockSpec((tk, d), lambda i, j, k: (k, 0)),
          dgates, pl.BlockSpec((None, tk, gs), lambda i, j, k: (j // 2, k, j % 2)))],
        [], [(_sds((4, d, gs), BF16), pl.BlockSpec((None, d, gs), lambda i, j, k: (j, 0, 0)))], (d, gs), _cast_epilogue)
    (dwin,) = _mm(
        "d_w_in", "tn", (1, 2, nk),
        [(un, pl.BlockSpec((tk, d), lambda i, j, k: (k, 0)), dqkv, pl.BlockSpec((tk, tin), lambda i, j, k: (k, j)))],
        [], [(_sds((d, ncols), BF16), pl.BlockSpec((d, tin), lambda i, j, k: (0, j)))], (d, tin), _cast_epilogue)

    mixer = GradPipe(["w_in", "w_gate"])
    extras, outs = _rms_bwd_io(h1, small["mix_norm"], dh2, tmn)
    dh1, dh1_b, d_mix_norm = _mm(
        "d_mix_norm", "nt", (t // tmn, 1, 6),
        [(dgates, pl.BlockSpec((None, tmn, gs), lambda i, j, k: (jnp.minimum(k, 3) // 2, i, jnp.minimum(k, 3) % 2)),
          wgate, pl.BlockSpec((None, d, gs), lambda i, j, k: (jnp.minimum(k, 3), 0, 0))),
         (dqkv, pl.BlockSpec((tmn, tin), lambda i, j, k: (i, jnp.maximum(k - 4, 0))),
          w_in, pl.BlockSpec((d, tin), lambda i, j, k: (0, jnp.maximum(k - 4, 0))))],
        extras, outs, (tmn, d), _rms_bwd_epilogue, steps=[4, 2],
        comms=[mixer.exchange([dwin, dwgate])])

    up1 = GradPipe(["ffn1_w_gu"])
    down1 = GradPipe(["ffn1_w_down"])
    dx, _, d_ffn1_norm, _, _ = _ffn_bwd(
        "ffn1", dh1, dh1_b, xt, small["ffn1_norm"], w["ffn1_w_gu"], w["ffn1_w_down"], ffn1_saved,
        {"dwgu": lambda: [mixer.scatter()],
         "dwd": lambda dwgu: [mixer.forward(), up1.exchange([dwgu])],
         "dnorm": lambda dwgu, dwd: [up1.scatter(), down1.exchange([dwd])]})
    _run_comms("grad_tail_scatter", [up1.forward(), down1.scatter()])
    _run_comms("grad_tail_forward", [down1.forward()])
    terms = {}
    for pipe in (up2, down2, ple, proj, mixer, up1, down1):
        terms.update(pipe.terms())

    def fold(v):
        return v[0, :HEAD_DIM] + v[0, HEAD_DIM:]

    small_grads = {"ffn1_norm": d_ffn1_norm, "mix_norm": d_mix_norm, "ffn2_norm": d_ffn2_norm,
                   "ple_norm": d_ple_norm, "a_q_norm": fold(dgqa), "a_k_norm": fold(dgka),
                   "b_q_norm": fold(dgqb), "b_k_norm": fold(dgkb), "a_rel_bias": _rel_bias_grad(_unpair_bias(dbias_a)),
                   "b_sinks": jnp.sum(dsink_b, axis=1)}
    return loss, dx, terms, small_grads


def kernel(x, p, ffn1_norm, ffn1_w_gu, ffn1_w_down, mix_norm, w_in, a_q_norm, a_k_norm, a_rel_bias, b_q_norm, b_k_norm, b_sinks, w_gate, w_proj_a, w_proj_b, w_out, ffn2_norm, ffn2_w_gu, ffn2_w_down, ple_norm, w_ple_gate, w_ple_proj, loss_target, m_ffn1_norm, m_ffn1_w_gu, m_ffn1_w_down, m_mix_norm, m_w_in, m_a_q_norm, m_a_k_norm, m_a_rel_bias, m_b_q_norm, m_b_k_norm, m_b_sinks, m_w_gate, m_w_proj_a, m_w_proj_b, m_w_out, m_ffn2_norm, m_ffn2_w_gu, m_ffn2_w_down, m_ple_norm, m_w_ple_gate, m_w_ple_proj, v_ffn1_norm, v_ffn1_w_gu, v_ffn1_w_down, v_mix_norm, v_w_in, v_a_q_norm, v_a_k_norm, v_a_rel_bias, v_b_q_norm, v_b_k_norm, v_b_sinks, v_w_gate, v_w_proj_a, v_w_proj_b, v_w_out, v_ffn2_norm, v_ffn2_w_gu, v_ffn2_w_down, v_ple_norm, v_w_ple_gate, v_w_ple_proj):
    given = dict(locals())
    n_batch, s, d = x.shape
    t = n_batch * s
    xt = x.reshape(t, d)
    pt = p.reshape(t, p.shape[-1])
    tgt = loss_target.reshape(t, d)

    chip = (2 * lax.axis_index("x") + lax.axis_index("y")).astype(jnp.int32).reshape(1)
    bufs = {name: _cast_into_slot("cast_" + name, given[name][0], chip) for name in BIG}
    small = {name: given[name] for name in SMALL}
    core = lax.axis_index("c").astype(jnp.int32).reshape(1)
    loss, dx, terms, small_grads = _local_step(xt, pt, tgt, n_batch, bufs, small, core)

    grads, deltas, new_m, new_v = {}, {}, {}, {}
    for name in BIG:
        gw, dl, nm, nv = _adamw_terms("adamw_" + name, terms[name], given[name][0], given["m_" + name][0],
                                      given["v_" + name][0])
        grads[name], deltas[name], new_m[name], new_v[name] = gw[None], dl[None], nm[None], nv[None]

    small_shapes = [given[name].shape for name in SMALL] + [()]
    g_pack = _all_reduce_small(_pack([small_grads[name] for name in SMALL] + [loss]))
    zero = jnp.zeros((), F32)
    w_pack = _pack([given[name] for name in SMALL] + [zero])
    m_pack = _pack([given["m_" + name] for name in SMALL] + [zero])
    v_pack = _pack([given["v_" + name] for name in SMALL] + [zero])
    d_pack, nm_pack, nv_pack = _ew("adamw_small", lambda wv, gv, mv, vv: _adamw_math(wv, gv, mv, vv),
                                   [w_pack, g_pack, m_pack, v_pack], [F32] * 3)
    g_small = _unpack(g_pack, small_shapes)
    loss_total = g_small[-1]
    for name, gv, dv, mv, vv in zip(SMALL, g_small, _unpack(d_pack, small_shapes), _unpack(nm_pack, small_shapes),
                                    _unpack(nv_pack, small_shapes)):
        grads[name], deltas[name], new_m[name], new_v[name] = gv, dv, mv, vv

    return (loss_total, dx.reshape(x.shape), *[grads[n] for n in WEIGHTS], *[deltas[n] for n in WEIGHTS],
            *[new_m[n] for n in WEIGHTS], *[new_v[n] for n in WEIGHTS])
```

```python
import functools

import numpy as np
import jax
import jax.numpy as jnp
from jax import lax
from jax.experimental import pallas as pl
from jax.experimental.pallas import tpu as pltpu

F32 = jnp.float32
BF16 = jnp.bfloat16

CHUNK = 64
HEAD_DIM = 64
A_PREV_CHUNKS = 8
A_MAX_REL = 128
N_HEADS = 8
B_KV_HEADS = 2
B_PREV_CHUNKS = 2
A_WIDTH = N_HEADS * HEAD_DIM
B_KV_WIDTH = B_KV_HEADS * HEAD_DIM
EPS = 1e-6
NEG_INF = -1e30
ATTN_SCALE = HEAD_DIM ** -0.5
Q_BLOCK = 128
PAIR = 2 * HEAD_DIM

ADAM_LR = 0.001
ADAM_B1 = 0.9
ADAM_B2 = 0.999
ADAM_EPS = 1e-08
ADAM_WD = 0.01
ADAM_STEP = 10

N_CHIPS = 4
N_DEV = 8
VMEM_LIMIT_V7X = 56 * 1024 * 1024
ROW_TILE = 1024
MESH = pl.DeviceIdType.MESH
COLLECTIVE_IDS = {("sibling",): 1, ("chips",): 2, ("chips", "sibling"): 3}
ANY = pl.BlockSpec(memory_space=pl.ANY)

_DN = {
    "nn": (((1,), (0,)), ((), ())),
    "nt": (((1,), (1,)), ((), ())),
    "tn": (((0,), (0,)), ((), ())),
}


def _pick(n, target, mult=128):
    best = None
    for d in range(mult, min(n, target) + 1, mult):
        if n % d == 0:
            best = d
    return n if best is None else best


def _dot(a, b, mode):
    return lax.dot_general(a.astype(BF16), b.astype(BF16), _DN[mode], preferred_element_type=F32)


def _params():
    return pltpu.CompilerParams(vmem_limit_bytes=VMEM_LIMIT_V7X)


class _Comm:
    def __init__(self, peers, ins, outs, aliases, sems, start, finish):
        self.peers = peers
        self.ins, self.outs, self.aliases, self.sems = list(ins), list(outs), dict(aliases), list(sems)
        self.start, self.finish = start, finish
        self.results = None


class _CommPlumbing:
    def __init__(self, comms, n_in, n_out, n_scratch):
        self.comms = list(comms)
        self.n_in, self.n_out, self.n_scratch = n_in, n_out, n_scratch
        self.args = [a for cm in self.comms for a in cm.ins]
        self.out_shape = [o for cm in self.comms for o in cm.outs]
        self.scratch = [s for cm in self.comms for s in cm.sems]
        self.aliases = {}
        i0, o0 = n_in, n_out
        for cm in self.comms:
            for a, b in cm.aliases.items():
                self.aliases[i0 + a] = o0 + b
            i0 += len(cm.ins)
            o0 += len(cm.outs)

    def _parts(self, in_refs, out_refs, scratch_refs):
        parts = []
        i0, o0, s0 = self.n_in, self.n_out, self.n_scratch
        for cm in self.comms:
            parts.append((in_refs[i0:i0 + len(cm.ins)], out_refs[o0:o0 + len(cm.outs)],
                          scratch_refs[s0:s0 + len(cm.sems)]))
            i0 += len(cm.ins)
            o0 += len(cm.outs)
            s0 += len(cm.sems)
        return parts

    def kinds(self):
        return sorted(set(cm.peers for cm in self.comms))

    def params(self, **kwargs):
        if self.comms:
            kwargs["collective_id"] = COLLECTIVE_IDS[tuple(self.kinds())]
        return pltpu.CompilerParams(**kwargs)

    def handshake(self):
        x, y, c, chips = _mesh_place()
        peers = []
        if "sibling" in self.kinds():
            peers.append((x, y, 1 - c))
        if "chips" in self.kinds():
            peers += [(tx, ty, c) for tx, ty in chips]
        barrier = pltpu.get_barrier_semaphore()
        for peer in peers:
            pl.semaphore_signal(barrier, inc=1, device_id=peer, device_id_type=MESH)
        pl.semaphore_wait(barrier, len(peers))

    def start_at(self, in_refs, out_refs, scratch_refs, first):
        if self.comms:
            parts = self._parts(in_refs, out_refs, scratch_refs)

            @pl.when(first)
            def _():
                self.handshake()
                for cm, part in zip(self.comms, parts):
                    cm.start(*part)

    def finish_at(self, in_refs, out_refs, scratch_refs, last):
        if self.comms:
            parts = self._parts(in_refs, out_refs, scratch_refs)

            @pl.when(last)
            def _():
                for cm, part in zip(self.comms, parts):
                    cm.finish(*part)

    def deliver(self, results):
        o0 = self.n_out
        for cm in self.comms:
            cm.results = list(results[o0:o0 + len(cm.outs)])
            o0 += len(cm.outs)
        return list(results[:self.n_out])


def _swap_ij(spec):
    index_map = spec.index_map
    return pl.BlockSpec(spec.block_shape, lambda j, i, k: index_map(i, j, k))


MXU_COLUMNS_V7X = 256


def _mm(name, mode, grid, pairs, extras, outs, acc_shape, epilogue, steps=None, comms=(), j_outer=False,
        chunked=False):
    ni, nj, nk = grid
    n_in = 2 * len(pairs) + len(extras)
    n_out = len(outs)
    tn = acc_shape[1]
    col_chunks = None
    if chunked:
        assert nk == 1 and steps is None and mode in ("nn", "nt")
        col_chunks = [(c0, min(MXU_COLUMNS_V7X, tn - c0)) for c0 in range(0, tn, MXU_COLUMNS_V7X)]
    n_acc = 0 if chunked else (len(pairs) if steps is None else 1)
    plumb = _CommPlumbing(comms, n_in, n_out, n_acc)
    n_all_in = n_in + len(plumb.args)
    n_all_out = n_out + len(plumb.out_shape)
    if j_outer:
        grid = (nj, ni, nk)
        pairs = [(a, _swap_ij(a_spec), b, _swap_ij(b_spec)) for a, a_spec, b, b_spec in pairs]
        extras = [(e, _swap_ij(e_spec)) for e, e_spec in extras]
        outs = [(o, _swap_ij(o_spec)) for o, o_spec in outs]

    def body(*refs):
        in_refs = refs[:n_all_in]
        out_refs = refs[n_all_in:n_all_in + n_all_out]
        scratch = refs[n_all_in + n_all_out:]
        accs = scratch[:n_acc]
        i = pl.program_id(1 if j_outer else 0)
        j = pl.program_id(0 if j_outer else 1)
        k = pl.program_id(2)
        plumb.start_at(in_refs, out_refs, scratch, (i == 0) & (j == 0) & (k == 0))

        def contrib(p, acc):
            acc[...] += _dot(in_refs[2 * p][...], in_refs[2 * p + 1][...], mode)

        if col_chunks:
            def cols(ref, c0, cs):
                if ref.shape[-1] != tn:
                    return ref
                return ref.at[(slice(None),) * (len(ref.shape) - 1) + (pl.ds(c0, cs),)]

            lhs = [in_refs[2 * p][...] for p in range(len(pairs))]
            for ci, (c0, cs) in enumerate(col_chunks):
                vals = []
                for p in range(len(pairs)):
                    b_ref = in_refs[2 * p + 1]
                    rhs = b_ref[:, c0:c0 + cs] if mode == "nn" else b_ref[c0:c0 + cs, :]
                    vals.append(_dot(lhs[p], rhs, mode))
                epilogue(vals, [cols(r, c0, cs) for r in in_refs[2 * len(pairs):n_in]],
                         [cols(r, c0, cs) for r in out_refs[:n_out]], (i, j * len(col_chunks) + ci))
        else:
            @pl.when(k == 0)
            def _():
                for acc in accs:
                    acc[...] = jnp.zeros(acc.shape, F32)

            if steps is None:
                for p in range(len(pairs)):
                    contrib(p, accs[p])
            else:
                lo = 0
                for p, n in enumerate(steps):
                    pl.when((k >= lo) & (k < lo + n))(functools.partial(contrib, p, accs[0]))
                    lo += n

            @pl.when(k == nk - 1)
            def _():
                epilogue([acc[...] for acc in accs], in_refs[2 * len(pairs):n_in], out_refs[:n_out], (i, j))

        plumb.finish_at(in_refs, out_refs, scratch, (i == ni - 1) & (j == nj - 1) & (k == nk - 1))

    args, in_specs = [], []
    for a, a_spec, b, b_spec in pairs:
        args += [a, b]
        in_specs += [a_spec, b_spec]
    for e, e_spec in extras:
        args.append(e)
        in_specs.append(e_spec)
    res = pl.pallas_call(
        body,
        name=name,
        grid=grid,
        in_specs=in_specs + [ANY] * len(plumb.args),
        out_specs=[s for _, s in outs] + [ANY] * len(plumb.out_shape),
        out_shape=[o for o, _ in outs] + plumb.out_shape,
        scratch_shapes=[pltpu.VMEM(acc_shape, F32) for _ in range(n_acc)] + plumb.scratch,
        input_output_aliases=plumb.aliases,
        compiler_params=plumb.params(vmem_limit_bytes=VMEM_LIMIT_V7X),
    )(*args, *plumb.args)
    return plumb.deliver(res)


def _sds(shape, dtype):
    return jax.ShapeDtypeStruct(shape, dtype)


def _accumulate(ref, value, first):
    @pl.when(first)
    def _():
        ref[...] = value

    @pl.when(jnp.logical_not(first))
    def _():
        ref[...] += value


def _rms_fwd(name, x, gain, comms=()):
    t, d = x.shape
    tm = _pick(t, ROW_TILE, 8)
    steps = t // tm
    plumb = _CommPlumbing(comms, 2, 1, 0)
    n_all_in = 2 + len(plumb.args)
    n_all_out = 1 + len(plumb.out_shape)

    def body(*refs):
        x_ref, g_ref = refs[:2]
        y_ref = refs[n_all_in]
        comm_refs = (refs[:n_all_in], refs[n_all_in:n_all_in + n_all_out], refs[n_all_in + n_all_out:])
        i = pl.program_id(0)
        plumb.start_at(*comm_refs, i == 0)
        xv = x_ref[...]
        rstd = lax.rsqrt(jnp.mean(xv * xv, axis=-1, keepdims=True) + EPS)
        y_ref[...] = (xv * rstd * g_ref[...]).astype(BF16)
        plumb.finish_at(*comm_refs, i == steps - 1)

    res = pl.pallas_call(
        body, name=name, grid=(steps,),
        in_specs=[pl.BlockSpec((tm, d), lambda i: (i, 0)), pl.BlockSpec((1, d), lambda i: (0, 0))]
        + [ANY] * len(plumb.args),
        out_specs=[pl.BlockSpec((tm, d), lambda i: (i, 0))] + [ANY] * len(plumb.out_shape),
        out_shape=[_sds((t, d), BF16)] + plumb.out_shape,
        scratch_shapes=plumb.scratch,
        input_output_aliases=plumb.aliases,
        compiler_params=plumb.params(vmem_limit_bytes=VMEM_LIMIT_V7X),
    )(x, gain, *plumb.args)
    return plumb.deliver(res)[0]


def _rms_bwd_epilogue(accs, extras, outs, ij):
    x_ref, g_ref, r_ref = extras
    dh_ref, dhb_ref, dg_ref = outs
    dn = accs[0]
    xv = x_ref[...]
    rstd = lax.rsqrt(jnp.mean(xv * xv, axis=-1, keepdims=True) + EPS)
    xhat = xv * rstd
    gd = dn * g_ref[...]
    dx = rstd * (gd - xhat * jnp.mean(gd * xhat, axis=-1, keepdims=True))
    dh = r_ref[...] + dx
    dh_ref[...] = dh
    dhb_ref[...] = dh.astype(BF16)
    _accumulate(dg_ref, jnp.sum(dn * xhat, axis=0, keepdims=True), ij[0] == 0)


def _rms_bwd_io(x, gain, dres, tm):
    t, d = x.shape
    row = pl.BlockSpec((tm, d), lambda i, j, k: (i, 0))
    extras = [(x, row), (gain, pl.BlockSpec((1, d), lambda i, j, k: (0, 0))), (dres, row)]
    outs = [(_sds((t, d), F32), row), (_sds((t, d), BF16), row),
            (_sds((1, d), F32), pl.BlockSpec((1, d), lambda i, j, k: (0, 0)))]
    return extras, outs


def _residual_norm_epilogue(scale):
    def epilogue(accs, extras, outs, ij):
        hv = extras[0][...] + scale * accs[0]
        outs[0][...] = hv
        rstd = lax.rsqrt(jnp.mean(hv * hv, axis=-1, keepdims=True) + EPS)
        outs[1][...] = (hv * rstd * extras[1][...]).astype(BF16)
    return epilogue


def _ffn_fwd(tag, h, n, wgu, wd, next_gain, hooks):
    t, d = h.shape
    fs = wgu.shape[2]
    f = 2 * fs
    tm = _pick(t, ROW_TILE, 8)

    def up_epilogue(accs, extras, outs, ij):
        g, u = accs
        gu_ref, a_ref = outs
        gu_ref[0] = g.astype(BF16)
        gu_ref[1] = u.astype(BF16)
        a_ref[...] = (g * jax.nn.sigmoid(g) * u).astype(BF16)

    a_spec = pl.BlockSpec((tm, d), lambda i, j, k: (i, 0))
    gu, a = _mm(
        tag + "_up", "nn", (t // tm, 2, 1),
        [(n, a_spec, wgu, pl.BlockSpec((None, d, fs), lambda i, j, k: (j, 0, 0))),
         (n, a_spec, wgu, pl.BlockSpec((None, d, fs), lambda i, j, k: (j + 2, 0, 0)))],
        [],
        [(_sds((2, t, f), BF16), pl.BlockSpec((2, tm, fs), lambda i, j, k: (0, i, j))),
         (_sds((t, f), BF16), pl.BlockSpec((tm, fs), lambda i, j, k: (i, j)))],
        (tm, fs), up_epilogue, comms=hooks.get("up", lambda: ())(), j_outer=True, chunked=True)

    row = pl.BlockSpec((tm, d), lambda i, j, k: (i, 0))
    h_new, n_new = _mm(
        tag + "_down", "nn", (t // tm, 1, 1),
        [(a, pl.BlockSpec((tm, f), lambda i, j, k: (i, 0)), wd, pl.BlockSpec((f, d), lambda i, j, k: (0, 0)))],
        [(h, row), (next_gain, pl.BlockSpec((1, d), lambda i, j, k: (0, 0)))],
        [(_sds((t, d), F32), row), (_sds((t, d), BF16), row)], (tm, d), _residual_norm_epilogue(0.5),
        comms=hooks.get("down", lambda: ())())
    return h_new, n_new, (n, gu, a)


def _ffn_bwd(tag, dh, dh_b, h, gain, wgu, wd, saved, hooks):
    n, gu, a = saved
    t, d = h.shape
    fs = wgu.shape[2]
    f = 2 * fs
    tm = _pick(t, ROW_TILE, 8)
    tk = _pick(t, ROW_TILE, 8)

    def dact_epilogue(accs, extras, outs, ij):
        da = 0.5 * accs[0]
        g = extras[0][0].astype(F32)
        u = extras[0][1].astype(F32)
        sg = jax.nn.sigmoid(g)
        outs[0][0] = (da * u * sg * (1.0 + g * (1.0 - sg))).astype(BF16)
        outs[0][1] = (da * g * sg).astype(BF16)

    gu_spec = pl.BlockSpec((2, tm, fs), lambda i, j, k: (0, i, j))
    (dgu,) = _mm(
        tag + "_dact", "nt", (t // tm, 2, 1),
        [(dh_b, pl.BlockSpec((tm, d), lambda i, j, k: (i, 0)), wd, pl.BlockSpec((fs, d), lambda i, j, k: (j, 0)))],
        [(gu, gu_spec)], [(_sds((2, t, f), BF16), gu_spec)], (tm, fs), dact_epilogue, j_outer=True, chunked=True,
        comms=hooks.get("dact", lambda: ())())

    def cast_epilogue(accs, extras, outs, ij):
        outs[0][...] = accs[0].astype(BF16)

    (dwgu,) = _mm(
        tag + "_dwgu", "tn", (1, 4, t // tk),
        [(n, pl.BlockSpec((tk, d), lambda i, j, k: (k, 0)),
          dgu, pl.BlockSpec((None, tk, fs), lambda i, j, k: (j // 2, k, j % 2)))],
        [], [(_sds((4, d, fs), BF16), pl.BlockSpec((None, d, fs), lambda i, j, k: (j, 0, 0)))], (d, fs), cast_epilogue,
        comms=hooks.get("dwgu", lambda: ())())

    def half_epilogue(accs, extras, outs, ij):
        outs[0][...] = (0.5 * accs[0]).astype(BF16)

    (dwd,) = _mm(
        tag + "_dwd", "tn", (2, 1, t // tk),
        [(a, pl.BlockSpec((tk, fs), lambda i, j, k: (k, i)), dh_b, pl.BlockSpec((tk, d), lambda i, j, k: (k, 0)))],
        [], [(_sds((f, d), BF16), pl.BlockSpec((fs, d), lambda i, j, k: (i, 0)))], (fs, d), half_epilogue,
        comms=hooks.get("dwd", lambda g: ())(dwgu))

    tmn = _pick(t, ROW_TILE, 8)
    extras, outs = _rms_bwd_io(h, gain, dh, tmn)
    dh_in, dh_in_b, dgain = _mm(
        tag + "_dnorm", "nt", (t // tmn, 1, 4),
        [(dgu, pl.BlockSpec((None, tmn, fs), lambda i, j, k: (k // 2, i, k % 2)),
          wgu, pl.BlockSpec((None, d, fs), lambda i, j, k: (k, 0, 0)))],
        extras, outs, (tmn, d), _rms_bwd_epilogue, comms=hooks.get("dnorm", lambda g, w: ())(dwgu, dwd))
    return dh_in, dh_in_b, dgain, dwgu, dwd


def _lane_lo(shape):
    return lax.broadcasted_iota(jnp.int32, shape, 1) < HEAD_DIM


def _pair_norm(xv, gain):
    lo = _lane_lo(xv.shape)
    x2 = xv * xv
    ms_lo = jnp.sum(jnp.where(lo, x2, 0.0), axis=-1, keepdims=True) * (1.0 / HEAD_DIM)
    ms_hi = jnp.sum(jnp.where(lo, 0.0, x2), axis=-1, keepdims=True) * (1.0 / HEAD_DIM)
    rstd = jnp.where(lo, lax.rsqrt(ms_lo + EPS), lax.rsqrt(ms_hi + EPS))
    xhat = xv * rstd
    return xhat * gain, xhat, rstd


def _pair_norm_bwd(dn, xhat, rstd, gain):
    lo = _lane_lo(dn.shape)
    gd = dn * gain
    t = gd * xhat
    m_lo = jnp.sum(jnp.where(lo, t, 0.0), axis=-1, keepdims=True) * (1.0 / HEAD_DIM)
    m_hi = jnp.sum(jnp.where(lo, 0.0, t), axis=-1, keepdims=True) * (1.0 / HEAD_DIM)
    dx = rstd * (gd - xhat * jnp.where(lo, m_lo, m_hi))
    return dx, jnp.sum(dn * xhat, axis=0, keepdims=True)


def _half(xv, hi):
    lo = _lane_lo(xv.shape)
    return jnp.where(lo, 0, xv) if hi else jnp.where(lo, xv, 0)


def _attn_window(i, prev):
    q0 = i * Q_BLOCK
    start = jnp.maximum(q0 - prev, 0)
    off = start - (q0 - prev)
    return pl.multiple_of(start, Q_BLOCK), pl.multiple_of(off, Q_BLOCK)


Q_BLOCKS_PER_STEP = 2
STEP_ROWS = Q_BLOCKS_PER_STEP * Q_BLOCK


def _attn_specs(cfg, s, steps):
    kw = cfg["kw"]
    q_spec = pl.BlockSpec((STEP_ROWS, A_WIDTH), lambda b, i: (b * steps + i, cfg["qblk"]))
    k_spec = pl.BlockSpec((s, kw), lambda b, i: (b, cfg["kblk"]))
    v_spec = pl.BlockSpec((s, kw), lambda b, i: (b, cfg["vblk"]))
    return q_spec, k_spec, v_spec


def _const_spec(shape):
    return pl.BlockSpec(shape, lambda b, i: (0,) * len(shape))


KEY_CHUNK = 128


def _pair_bias(bias_t):
    wext = bias_t.shape[1]
    return jnp.transpose(bias_t.reshape(N_HEADS // 2, 2, wext, Q_BLOCK), (0, 2, 1, 3)).reshape(
        N_HEADS // 2, wext, 2 * Q_BLOCK)


def _unpair_bias(db2):
    wext = db2.shape[1]
    return jnp.transpose(db2.reshape(N_HEADS // 2, wext, 2, Q_BLOCK), (0, 2, 1, 3)).reshape(N_HEADS, wext, Q_BLOCK)


def _pair_rows(rows):
    two = rows.reshape(N_HEADS // 2, 2 * rows.shape[1])
    return jnp.broadcast_to(two[:, None, :], (N_HEADS // 2, 8, two.shape[1]))


def _sub_lo(shape):
    return lax.broadcasted_iota(jnp.int32, shape, 0) < HEAD_DIM


def _by_half(lo_row, hi_row, rows):
    return jnp.where(_sub_lo((rows, lo_row.shape[1])), lo_row, hi_row)


def _stack_pair(xn, jq, group):
    parts = []
    for hq in range(2):
        hk = ((2 * jq + hq) // group) % 2
        xm = _half(xn, hq)
        if hq != hk:
            xm = pltpu.roll(xm, HEAD_DIM, 1)
        parts.append(xm)
    return jnp.concatenate(parts, axis=0).astype(BF16)


def _place_transposed(blk, dst_ref, c, heads, group):
    bt = blk.T
    lo = _sub_lo(bt.shape)
    for h in heads:
        src_hi = ((h // group) % 2) == 1
        part = jnp.where(lo, 0.0, bt) if src_hi else jnp.where(lo, bt, 0.0)
        if src_hi != (h % 2 == 1):
            part = pltpu.roll(part, HEAD_DIM, 0)
        dst_ref[h, c] = part.astype(BF16)


def _attn_fwd(name, qkv, bias2, sink2, gq, gk, cfg, n_batch, comms=()):
    t = qkv.shape[0]
    s = t // n_batch
    steps = s // STEP_ROWS
    nkc = s // KEY_CHUNK
    prev, group, kw = cfg["prev"], cfg["group"], cfg["kw"]
    w = prev + Q_BLOCK
    n_chunks = w // KEY_CHUNK
    wext = bias2.shape[1]
    plumb = _CommPlumbing(comms, 7, 2, 4)
    n_all_in = 7 + len(plumb.args)
    n_all_out = 2 + len(plumb.out_shape)

    def body(*refs):
        q_ref, k_ref, v_ref, bias_ref, sink_ref, gq_ref, gk_ref = refs[:7]
        y_ref, lse_ref = refs[n_all_in:n_all_in + 2]
        kn_ref, vt_ref, s_ref, pst_ref = refs[n_all_in + n_all_out:n_all_in + n_all_out + 4]
        step = pl.program_id(1)
        comm_refs = (refs[:n_all_in], refs[n_all_in:n_all_in + n_all_out], refs[n_all_in + n_all_out:])
        plumb.start_at(*comm_refs, (pl.program_id(0) == 0) & (step == 0))

        @pl.when(step == 0)
        def _():
            for jk in range(kw // PAIR):
                cols = pl.ds(jk * PAIR, PAIR)
                heads = [h for h in range(N_HEADS) if (h // group) // 2 == jk]
                kn, _, _ = _pair_norm(k_ref[:, cols].astype(F32), gk_ref[...])
                kn_ref[:, cols] = kn.astype(BF16)
                for c in range(nkc):
                    _place_transposed(v_ref[pl.ds(c * KEY_CHUNK, KEY_CHUNK), cols].astype(F32), vt_ref, c, heads, group)

        sub8 = lax.broadcasted_iota(jnp.int32, (N_HEADS, Q_BLOCK), 0)
        for sb in range(Q_BLOCKS_PER_STEP):
            qrows = pl.ds(sb * Q_BLOCK, Q_BLOCK)
            start, off = _attn_window(step * Q_BLOCKS_PER_STEP + sb, prev)
            c0 = start // KEY_CHUNK
            lse = jnp.zeros((N_HEADS, Q_BLOCK), F32)
            for jq in range(N_HEADS // 2):
                kcols = pl.ds((((2 * jq) // group) // 2) * PAIR, PAIR)
                qn, _, _ = _pair_norm(q_ref[qrows, pl.ds(jq * PAIR, PAIR)].astype(F32), gq_ref[...])
                qs = _stack_pair(qn * ATTN_SCALE, jq, group)
                s_ref[...] = _dot(kn_ref[pl.ds(start, w), kcols], qs, "nt")
                m = sink_ref[jq, 0:1, :]
                for c in range(n_chunks):
                    r = pl.ds(c * KEY_CHUNK, KEY_CHUNK)
                    s2 = s_ref[r, :] + bias_ref[jq, pl.ds(off + c * KEY_CHUNK, KEY_CHUNK), :]
                    s_ref[r, :] = s2
                    m = jnp.maximum(m, jnp.max(s2, axis=0, keepdims=True))
                l = jnp.exp(sink_ref[jq, 0:1, :] - m)
                for c in range(n_chunks):
                    p = jnp.exp(s_ref[pl.ds(c * KEY_CHUNK, KEY_CHUNK), :] - m)
                    l = l + jnp.sum(p, axis=0, keepdims=True)
                    pst_ref[pl.ds(2 * c * KEY_CHUNK, KEY_CHUNK), :] = p[:, :Q_BLOCK].astype(BF16)
                    pst_ref[pl.ds((2 * c + 1) * KEY_CHUNK, KEY_CHUNK), :] = p[:, Q_BLOCK:].astype(BF16)
                vl = jnp.concatenate([vt_ref[2 * jq + hq, c0 + c] for c in range(n_chunks) for hq in range(2)], axis=1)
                ot = _dot(vl, pst_ref[...], "nn")
                inv = 1.0 / l
                ot = ot * _by_half(inv[:, :Q_BLOCK], inv[:, Q_BLOCK:], PAIR)
                y_ref[qrows, pl.ds(jq * PAIR, PAIR)] = ot.T.astype(BF16)
                lse2 = m + jnp.log(l)
                lse = jnp.where(sub8 == 2 * jq, lse2[:, :Q_BLOCK], lse)
                lse = jnp.where(sub8 == 2 * jq + 1, lse2[:, Q_BLOCK:], lse)
            lse_ref[sb] = lse
        plumb.finish_at(*comm_refs, (pl.program_id(0) == n_batch - 1) & (step == steps - 1))

    q_spec, k_spec, v_spec = _attn_specs(cfg, s, steps)
    res = pl.pallas_call(
        body, name=name, grid=(n_batch, steps),
        in_specs=[q_spec, k_spec, v_spec, _const_spec((N_HEADS // 2, wext, 2 * Q_BLOCK)),
                  _const_spec((N_HEADS // 2, 8, 2 * Q_BLOCK)), _const_spec((1, PAIR)), _const_spec((1, PAIR))]
        + [ANY] * len(plumb.args),
        out_specs=[pl.BlockSpec((STEP_ROWS, A_WIDTH), lambda b, i: (b * steps + i, 0)),
                   pl.BlockSpec((Q_BLOCKS_PER_STEP, N_HEADS, Q_BLOCK), lambda b, i: (b * steps + i, 0, 0))]
        + [ANY] * len(plumb.out_shape),
        out_shape=[_sds((t, A_WIDTH), BF16), _sds((t // Q_BLOCK, N_HEADS, Q_BLOCK), F32)] + plumb.out_shape,
        scratch_shapes=[pltpu.VMEM((s, kw), BF16), pltpu.VMEM((N_HEADS, nkc, PAIR, KEY_CHUNK), BF16),
                        pltpu.VMEM((w, 2 * Q_BLOCK), F32), pltpu.VMEM((2 * w, Q_BLOCK), BF16)] + plumb.scratch,
        input_output_aliases=plumb.aliases,
        compiler_params=plumb.params(vmem_limit_bytes=VMEM_LIMIT_V7X),
    )(qkv, qkv, qkv, bias2, sink2, gq, gk, *plumb.args)
    return plumb.deliver(res)


def _attn_bwd(name, qkv, bias2, sink2, gq, gk, y, dy, lse, cfg, n_batch, want_dbias, comms=()):
    t = qkv.shape[0]
    s = t // n_batch
    steps = s // STEP_ROWS
    nkc = s // KEY_CHUNK
    prev, group, kw = cfg["prev"], cfg["group"], cfg["kw"]
    w = prev + Q_BLOCK
    n_chunks = w // KEY_CHUNK
    wext = bias2.shape[1]
    plumb = _CommPlumbing(comms, 10, 7, 9)
    n_all_in = 10 + len(plumb.args)
    n_all_out = 7 + len(plumb.out_shape)

    def body(*refs):
        q_ref, k_ref, v_ref, bias_ref, sink_ref, gq_ref, gk_ref, y_ref, dy_ref, lse_ref = refs[:10]
        dq_ref, dk_ref, dv_ref, db_ref, dsink_ref, dgq_ref, dgk_ref = refs[n_all_in:n_all_in + 7]
        kn_ref, knt_ref, dkn_ref, dvs_ref, s_ref, dp_ref, pb_ref, dsb_ref, dst_ref = \
            refs[n_all_in + n_all_out:n_all_in + n_all_out + 9]
        b = pl.program_id(0)
        step = pl.program_id(1)
        first = (b == 0) & (step == 0)
        comm_refs = (refs[:n_all_in], refs[n_all_in:n_all_in + n_all_out], refs[n_all_in + n_all_out:])
        plumb.start_at(*comm_refs, first)

        @pl.when(step == 0)
        def _():
            for jk in range(kw // PAIR):
                cols = pl.ds(jk * PAIR, PAIR)
                heads = [h for h in range(N_HEADS) if (h // group) // 2 == jk]
                for c in range(nkc):
                    rows = pl.ds(c * KEY_CHUNK, KEY_CHUNK)
                    kn, _, _ = _pair_norm(k_ref[rows, cols].astype(F32), gk_ref[...])
                    kn_ref[rows, cols] = kn.astype(BF16)
                    _place_transposed(kn, knt_ref, c, heads, group)
            dkn_ref[...] = jnp.zeros(dkn_ref.shape, F32)
            dvs_ref[...] = jnp.zeros(dvs_ref.shape, F32)

        @pl.when(first)
        def _():
            db_ref[...] = jnp.zeros(db_ref.shape, F32)
            dsink_ref[...] = jnp.zeros(dsink_ref.shape, F32)
            dgq_ref[...] = jnp.zeros(dgq_ref.shape, F32)
            dgk_ref[...] = jnp.zeros(dgk_ref.shape, F32)

        for sb in range(Q_BLOCKS_PER_STEP):
            qrows = pl.ds(sb * Q_BLOCK, Q_BLOCK)
            start, off = _attn_window(step * Q_BLOCKS_PER_STEP + sb, prev)
            c0 = start // KEY_CHUNK
            for jq in range(N_HEADS // 2):
                cols = pl.ds(jq * PAIR, PAIR)
                kcols = pl.ds((((2 * jq) // group) // 2) * PAIR, PAIR)
                qn, q_hat, q_rstd = _pair_norm(q_ref[qrows, cols].astype(F32), gq_ref[...])
                qs = _stack_pair(qn * ATTN_SCALE, jq, group)
                do_pair = dy_ref[qrows, cols].astype(F32)
                dos = _stack_pair(do_pair, jq, group)
                prod_t = (do_pair * y_ref[qrows, cols].astype(F32)).T
                lo = _sub_lo(prod_t.shape)
                delta2 = jnp.concatenate([jnp.sum(jnp.where(lo, prod_t, 0.0), axis=0, keepdims=True),
                                          jnp.sum(jnp.where(lo, 0.0, prod_t), axis=0, keepdims=True)], axis=1)
                lse2 = jnp.concatenate([lse_ref[sb, 2 * jq:2 * jq + 1, :], lse_ref[sb, 2 * jq + 1:2 * jq + 2, :]],
                                       axis=1)
                dsk = -jnp.exp(sink_ref[jq, 0:1, :] - lse2) * delta2
                dsink_ref[2 * jq:2 * jq + 1, :] += dsk[:, :Q_BLOCK]
                dsink_ref[2 * jq + 1:2 * jq + 2, :] += dsk[:, Q_BLOCK:]
                rows_w = pl.ds(start, w)
                s_ref[...] = _dot(kn_ref[rows_w, kcols], qs, "nt")
                dp_ref[...] = _dot(v_ref[rows_w, kcols], dos, "nt")
                for c in range(n_chunks):
                    r = pl.ds(c * KEY_CHUNK, KEY_CHUNK)
                    brows = pl.ds(off + c * KEY_CHUNK, KEY_CHUNK)
                    p = jnp.exp(s_ref[r, :] + bias_ref[jq, brows, :] - lse2)
                    ds = p * (dp_ref[r, :] - delta2)
                    if want_dbias:
                        db_ref[jq, brows, :] += ds
                    ds_b = ds.astype(BF16)
                    pb_ref[r, :] = p.astype(BF16)
                    dsb_ref[r, :] = ds_b
                    dst_ref[pl.ds(2 * c * KEY_CHUNK, KEY_CHUNK), :] = ds_b[:, :Q_BLOCK]
                    dst_ref[pl.ds((2 * c + 1) * KEY_CHUNK, KEY_CHUNK), :] = ds_b[:, Q_BLOCK:]
                dkn_ref[rows_w, kcols] += _dot(dsb_ref[...], qs, "nn")
                dvs_ref[rows_w, kcols] += _dot(pb_ref[...], dos, "nn")
                kl = jnp.concatenate([knt_ref[2 * jq + hq, c0 + c] for c in range(n_chunks) for hq in range(2)],
                                     axis=1)
                dqt = _dot(kl, dst_ref[...], "nn")
                dq_raw, dg = _pair_norm_bwd(dqt.T * ATTN_SCALE, q_hat, q_rstd, gq_ref[...])
                dq_ref[qrows, cols] = dq_raw.astype(BF16)
                dgq_ref[...] += dg

        @pl.when(step == steps - 1)
        def _():
            for jk in range(kw // PAIR):
                kcols = pl.ds(jk * PAIR, PAIR)
                _, k_hat, k_rstd = _pair_norm(k_ref[:, kcols].astype(F32), gk_ref[...])
                dk_raw, dg = _pair_norm_bwd(dkn_ref[:, kcols], k_hat, k_rstd, gk_ref[...])
                dk_ref[:, kcols] = dk_raw.astype(BF16)
                dgk_ref[...] += dg
            dv_ref[...] = dvs_ref[...].astype(BF16)

        plumb.finish_at(*comm_refs, (b == n_batch - 1) & (step == steps - 1))

    q_spec, k_spec, v_spec = _attn_specs(cfg, s, steps)
    row = pl.BlockSpec((STEP_ROWS, A_WIDTH), lambda b, i: (b * steps + i, 0))
    kv_out = pl.BlockSpec((s, kw), lambda b, i: (b, 0))
    pair_bias = _const_spec((N_HEADS // 2, wext, 2 * Q_BLOCK))
    res = pl.pallas_call(
        body, name=name, grid=(n_batch, steps),
        in_specs=[q_spec, k_spec, v_spec, pair_bias, _const_spec((N_HEADS // 2, 8, 2 * Q_BLOCK)),
                  _const_spec((1, PAIR)), _const_spec((1, PAIR)), row, row,
                  pl.BlockSpec((Q_BLOCKS_PER_STEP, N_HEADS, Q_BLOCK), lambda b, i: (b * steps + i, 0, 0))]
        + [ANY] * len(plumb.args),
        out_specs=[row, kv_out, kv_out, pair_bias, _const_spec((N_HEADS, 128)),
                   _const_spec((1, PAIR)), _const_spec((1, PAIR))] + [ANY] * len(plumb.out_shape),
        out_shape=[_sds((t, A_WIDTH), BF16), _sds((t, kw), BF16), _sds((t, kw), BF16),
                   _sds((N_HEADS // 2, wext, 2 * Q_BLOCK), F32), _sds((N_HEADS, 128), F32),
                   _sds((1, PAIR), F32), _sds((1, PAIR), F32)] + plumb.out_shape,
        scratch_shapes=[pltpu.VMEM((s, kw), BF16), pltpu.VMEM((N_HEADS, nkc, PAIR, KEY_CHUNK), BF16),
                        pltpu.VMEM((s, kw), F32), pltpu.VMEM((s, kw), F32),
                        pltpu.VMEM((w, 2 * Q_BLOCK), F32), pltpu.VMEM((w, 2 * Q_BLOCK), F32),
                        pltpu.VMEM((w, 2 * Q_BLOCK), BF16), pltpu.VMEM((w, 2 * Q_BLOCK), BF16),
                        pltpu.VMEM((2 * w, Q_BLOCK), BF16)] + plumb.scratch,
        input_output_aliases=plumb.aliases,
        compiler_params=plumb.params(vmem_limit_bytes=VMEM_LIMIT_V7X),
    )(qkv, qkv, qkv, bias2, sink2, gq, gk, y, dy, lse, *plumb.args)
    return plumb.deliver(res)


def _band_tables(prev_chunks):
    prev = prev_chunks * CHUNK
    wext = 2 * prev + Q_BLOCK
    jj = np.arange(wext)[:, None]
    ii = np.arange(Q_BLOCK)[None, :]
    dist = prev + ii - jj
    rel_chunk = (prev // CHUNK + ii // CHUNK) - jj // CHUNK
    allowed = (rel_chunk >= 0) & (rel_chunk <= prev_chunks)
    return dist, allowed


def _alibi_slopes():
    return np.array([2.0 ** (-8.0 * (h + 1) / N_HEADS) for h in range(N_HEADS)], dtype=np.float32)


def _diag_onehot(prev, wext):
    n_diag = wext + Q_BLOCK - 1
    idx = np.clip(prev + Q_BLOCK - 1 - np.arange(n_diag), -A_MAX_REL, A_MAX_REL) + A_MAX_REL
    onehot = np.zeros((n_diag, 2 * A_MAX_REL + 1), np.float32)
    onehot[np.arange(n_diag), idx] = 1.0
    return onehot


def _bias_a(rel_bias):
    prev = A_PREV_CHUNKS * CHUNK
    _, allowed = _band_tables(A_PREV_CHUNKS)
    wext = allowed.shape[0]
    n_diag = wext + Q_BLOCK - 1
    seq = jnp.dot(rel_bias, jnp.asarray(_diag_onehot(prev, wext).T), precision=lax.Precision.HIGHEST)
    seq = jnp.pad(seq, ((0, 0), (0, 1)))
    rows = jnp.broadcast_to(seq[:, None, :], (N_HEADS, Q_BLOCK, n_diag + 1)).reshape(N_HEADS, -1)
    skew = rows[:, :Q_BLOCK * n_diag].reshape(N_HEADS, Q_BLOCK, n_diag)
    tile = jnp.transpose(skew[:, :, Q_BLOCK - 1:Q_BLOCK - 1 + wext], (0, 2, 1))
    return jnp.where(jnp.asarray(allowed)[None], tile, NEG_INF)


def _bias_b():
    dist, allowed = _band_tables(B_PREV_CHUNKS)
    bias = -_alibi_slopes()[:, None, None] * np.abs(dist).astype(np.float32)[None]
    return jnp.asarray(np.where(allowed[None], bias, np.float32(NEG_INF)).astype(np.float32))


def _rel_bias_grad(db_t):
    prev = A_PREV_CHUNKS * CHUNK
    wext = db_t.shape[1]
    n_diag = wext + Q_BLOCK - 1
    wp = n_diag + Q_BLOCK - 1
    xp = jnp.pad(jnp.transpose(db_t, (0, 2, 1)), ((0, 0), (0, 0), (Q_BLOCK - 1, Q_BLOCK - 1)))
    flat = jnp.pad(xp.reshape(N_HEADS, Q_BLOCK * wp), ((0, 0), (0, Q_BLOCK)))
    skew = flat.reshape(N_HEADS, Q_BLOCK, wp + 1)[:, :, :n_diag]
    diag = jnp.sum(skew, axis=1)
    return jnp.dot(diag, jnp.asarray(_diag_onehot(prev, wext)), precision=lax.Precision.HIGHEST)


def _ew(name, fn, ins, out_dtypes):
    r, c = ins[0].shape
    rb = _pick(r, max(16, (1 << 19) // c), 16)
    spec = pl.BlockSpec((rb, c), lambda i: (i, 0))

    def body(*refs):
        vals = fn(*[ref[...] for ref in refs[:len(ins)]])
        for ref, val in zip(refs[len(ins):], vals):
            ref[...] = val.astype(ref.dtype)

    return pl.pallas_call(
        body, name=name, grid=(r // rb,), in_specs=[spec] * len(ins), out_specs=[spec] * len(out_dtypes),
        out_shape=[_sds((r, c), dt) for dt in out_dtypes], compiler_params=_params(),
    )(*ins)


def _cast_into_slot(name, w, chip):
    r, c = w.shape
    rb = _pick(r, max(16, (1 << 19) // c), 16)

    def body(chip_ref, w_ref, o_ref):
        o_ref[...] = w_ref[...].astype(BF16)

    return pl.pallas_call(
        body, name=name,
        grid_spec=pltpu.PrefetchScalarGridSpec(
            num_scalar_prefetch=1, grid=(r // rb,),
            in_specs=[pl.BlockSpec((rb, c), lambda i, chip_ref: (i, 0))],
            out_specs=pl.BlockSpec((None, rb, c), lambda i, chip_ref: (chip_ref[0], i, 0))),
        out_shape=_sds((N_CHIPS, r, c), BF16), compiler_params=_params(),
    )(chip, w)


def _adamw_math(w, g, m, v):
    m = ADAM_B1 * m + (1.0 - ADAM_B1) * g
    v = ADAM_B2 * v + (1.0 - ADAM_B2) * (g * g)
    m_hat = m / (1.0 - ADAM_B1 ** ADAM_STEP)
    v_hat = v / (1.0 - ADAM_B2 ** ADAM_STEP)
    delta = -ADAM_LR * (m_hat / (jnp.sqrt(v_hat) + ADAM_EPS) + ADAM_WD * w)
    return delta, m, v


def _adamw_terms(name, terms, w, m, v):
    r, c = w.shape
    hr = r // 2
    rb = _pick(hr, max(16, (1 << 19) // c), 16)
    nb = hr // rb

    def body(t_ref, w_ref, m_ref, v_ref, g_ref, d_ref, nm_ref, nv_ref):
        g = t_ref[0].astype(F32)
        for k in range(1, N_CHIPS):
            g = g + t_ref[k].astype(F32)
        delta, nm, nv = _adamw_math(w_ref[...], g, m_ref[...], v_ref[...])
        g_ref[...] = g
        d_ref[...] = delta
        nm_ref[...] = nm
        nv_ref[...] = nv

    spec = pl.BlockSpec((rb, c), lambda h, i: (h * nb + i, 0))
    return pl.pallas_call(
        body, name=name, grid=(2, nb),
        in_specs=[pl.BlockSpec((None, N_CHIPS, rb, c), lambda h, i: (h, 0, i, 0)), spec, spec, spec],
        out_specs=[spec] * 4, out_shape=[_sds((r, c), F32)] * 4, compiler_params=_params(),
    )(terms, w, m, v)


def _mesh_place():
    x, y, c = lax.axis_index("x"), lax.axis_index("y"), lax.axis_index("c")
    chips = [(x, 1 - y), (1 - x, y), (1 - x, 1 - y)]
    return x, y, c, chips


def _all_gather_weights(bufs):
    n = len(bufs)

    def body(*refs):
        outs = refs[n:2 * n]
        ici_send, ici_recv, d2d_send, d2d_recv = refs[2 * n:]
        x, y, c, chips = _mesh_place()
        me = 2 * x + y
        sibling = (x, y, 1 - c)
        barrier = pltpu.get_barrier_semaphore()
        for peer in [sibling] + [(tx, ty, c) for tx, ty in chips]:
            pl.semaphore_signal(barrier, inc=1, device_id=peer, device_id_type=MESH)
        pl.semaphore_wait(barrier, N_CHIPS)
        sent = []
        for wi in range(n):
            for k, (tx, ty) in enumerate(chips):
                own = outs[wi].at[me, c]
                cp = pltpu.make_async_remote_copy(
                    src_ref=own, dst_ref=own, send_sem=ici_send.at[wi * 3 + k], recv_sem=ici_recv.at[wi * 3 + k],
                    device_id=(tx, ty, c), device_id_type=MESH)
                cp.start()
                sent.append(cp)
        passed = []
        for wi in range(n):
            for k, (tx, ty) in enumerate(chips):
                slab = outs[wi].at[2 * tx + ty, c]
                pltpu.make_async_remote_copy(
                    src_ref=slab, dst_ref=slab, send_sem=ici_send.at[wi * 3 + k], recv_sem=ici_recv.at[wi * 3 + k],
                    device_id=(tx, ty, c), device_id_type=MESH).wait_recv()
                fw = pltpu.make_async_remote_copy(
                    src_ref=slab, dst_ref=slab, send_sem=d2d_send.at[wi * 3 + k], recv_sem=d2d_recv.at[wi * 3 + k],
                    device_id=sibling, device_id_type=MESH)
                fw.start()
                passed.append(fw)
        for wi in range(n):
            for k, (tx, ty) in enumerate(chips):
                slab = outs[wi].at[2 * tx + ty, 1 - c]
                pltpu.make_async_remote_copy(
                    src_ref=slab, dst_ref=slab, send_sem=d2d_send.at[wi * 3 + k], recv_sem=d2d_recv.at[wi * 3 + k],
                    device_id=sibling, device_id_type=MESH).wait_recv()
        for cp in sent + passed:
            cp.wait_send()

    return pl.pallas_call(
        body, name="all_gather_weights",
        in_specs=[ANY] * n, out_specs=[ANY] * n,
        out_shape=[_sds(g.shape, g.dtype) for g in bufs],
        scratch_shapes=[pltpu.SemaphoreType.DMA((3 * n,))] * 4,
        input_output_aliases={i: i for i in range(n)},
        compiler_params=pltpu.CompilerParams(collective_id=COLLECTIVE_IDS[("chips", "sibling")]),
    )(*bufs)


def _run_comms(name, comms):
    plumb = _CommPlumbing(comms, 0, 0, 0)
    n_in, n_out = len(plumb.args), len(plumb.out_shape)

    def body(*refs):
        parts = []
        i0, o0, s0 = 0, n_in, n_in + n_out
        for cm in plumb.comms:
            parts.append((refs[i0:i0 + len(cm.ins)], refs[o0:o0 + len(cm.outs)], refs[s0:s0 + len(cm.sems)]))
            i0 += len(cm.ins)
            o0 += len(cm.outs)
            s0 += len(cm.sems)
        plumb.handshake()
        for cm, part in zip(plumb.comms, parts):
            cm.start(*part)
        for cm, part in zip(plumb.comms, parts):
            cm.finish(*part)

    res = pl.pallas_call(
        body, name=name, in_specs=[ANY] * n_in, out_specs=[ANY] * n_out, out_shape=plumb.out_shape,
        scratch_shapes=plumb.scratch, input_output_aliases=plumb.aliases, compiler_params=plumb.params(),
    )(*plumb.args)
    plumb.deliver(res)


def _gather_ici(bufs):
    n = len(bufs)

    def copies(outs, sems):
        send_sem, recv_sem = sems
        x, y, c, chips = _mesh_place()
        me = 2 * x + y
        sends, recvs = [], []
        for wi in range(n):
            for k, (tx, ty) in enumerate(chips):
                sems_k = dict(send_sem=send_sem.at[wi * 3 + k], recv_sem=recv_sem.at[wi * 3 + k],
                              device_id=(tx, ty, c), device_id_type=MESH)
                own = outs[wi].at[me, c]
                sends.append(pltpu.make_async_remote_copy(src_ref=own, dst_ref=own, **sems_k))
                slab = outs[wi].at[2 * tx + ty, c]
                recvs.append(pltpu.make_async_remote_copy(src_ref=slab, dst_ref=slab, **sems_k))
        return sends, recvs

    def start(ins, outs, sems):
        for cp in copies(outs, sems)[0]:
            cp.start()

    def finish(ins, outs, sems):
        sends, recvs = copies(outs, sems)
        for cp in recvs:
            cp.wait_recv()
        for cp in sends:
            cp.wait_send()

    return _Comm("chips", bufs, [_sds(g.shape, g.dtype) for g in bufs], {i: i for i in range(n)},
                 [pltpu.SemaphoreType.DMA((3 * n,)), pltpu.SemaphoreType.DMA((3 * n,))], start, finish)


def _gather_d2d(gathered):
    n = len(gathered)

    def copies(outs, sems):
        send_sem, recv_sem = sems
        x, y, c, chips = _mesh_place()
        sends, recvs = [], []
        for wi in range(n):
            for k, (tx, ty) in enumerate(chips):
                sems_k = dict(send_sem=send_sem.at[wi * 3 + k], recv_sem=recv_sem.at[wi * 3 + k],
                              device_id=(x, y, 1 - c), device_id_type=MESH)
                mine = outs[wi].at[2 * tx + ty, c]
                theirs = outs[wi].at[2 * tx + ty, 1 - c]
                sends.append(pltpu.make_async_remote_copy(src_ref=mine, dst_ref=mine, **sems_k))
                recvs.append(pltpu.make_async_remote_copy(src_ref=theirs, dst_ref=theirs, **sems_k))
        return sends, recvs

    def start(ins, outs, sems):
        for cp in copies(outs, sems)[0]:
            cp.start()

    def finish(ins, outs, sems):
        sends, recvs = copies(outs, sems)
        for cp in recvs:
            cp.wait_recv()
        for cp in sends:
            cp.wait_send()

    return _Comm("sibling", gathered, [_sds(g.shape, g.dtype) for g in gathered], {i: i for i in range(n)},
                 [pltpu.SemaphoreType.DMA((3 * n,)), pltpu.SemaphoreType.DMA((3 * n,))], start, finish)


def _exchange_halves(grads):
    n = len(grads)

    def copies(ins, outs, sems):
        send_sem, recv_sem = sems
        x, y, c, _ = _mesh_place()
        return [pltpu.make_async_remote_copy(
            src_ref=ins[wi].at[t, 1 - c], dst_ref=outs[wi].at[t],
            send_sem=send_sem.at[wi * N_CHIPS + t], recv_sem=recv_sem.at[wi * N_CHIPS + t],
            device_id=(x, y, 1 - c), device_id_type=MESH) for wi in range(n) for t in range(N_CHIPS)]

    def start(ins, outs, sems):
        for cp in copies(ins, outs, sems):
            cp.start()

    def finish(ins, outs, sems):
        for cp in copies(ins, outs, sems):
            cp.wait()

    return _Comm("sibling", grads, [_sds((N_CHIPS,) + g.shape[2:], g.dtype) for g in grads], {},
                 [pltpu.SemaphoreType.DMA((N_CHIPS * n,)), pltpu.SemaphoreType.DMA((N_CHIPS * n,))], start, finish)


def _scatter_ici(sums):
    n = len(sums)

    def copies(ins, outs, sems):
        local_sem, send_sem, recv_sem = sems
        x, y, c, chips = _mesh_place()
        me = 2 * x + y
        local, sends, recvs = [], [], []
        for wi in range(n):
            local.append(pltpu.make_async_copy(ins[wi].at[me], outs[wi].at[c, 0], local_sem.at[wi]))
            for k, (tx, ty) in enumerate(chips):
                sems_k = dict(send_sem=send_sem.at[wi * 3 + k], recv_sem=recv_sem.at[wi * 3 + k],
                              device_id=(tx, ty, c), device_id_type=MESH)
                land = outs[wi].at[c, k + 1]
                sends.append(pltpu.make_async_remote_copy(src_ref=ins[wi].at[2 * tx + ty], dst_ref=land, **sems_k))
                recvs.append(pltpu.make_async_remote_copy(src_ref=land, dst_ref=land, **sems_k))
        return local, sends, recvs

    def start(ins, outs, sems):
        local, sends, _ = copies(ins, outs, sems)
        for cp in local + sends:
            cp.start()

    def finish(ins, outs, sems):
        local, sends, recvs = copies(ins, outs, sems)
        for cp in local:
            cp.wait()
        for cp in recvs:
            cp.wait_recv()
        for cp in sends:
            cp.wait_send()

    return _Comm("chips", sums, [_sds((2, N_CHIPS) + s.shape[1:], s.dtype) for s in sums], {},
                 [pltpu.SemaphoreType.DMA((n,)), pltpu.SemaphoreType.DMA((3 * n,)), pltpu.SemaphoreType.DMA((3 * n,))],
                 start, finish)


def _scatter_d2d(terms):
    n = len(terms)

    def copies(outs, sems):
        send_sem, recv_sem = sems
        x, y, c, _ = _mesh_place()
        sends, recvs = [], []
        for wi in range(n):
            sems_w = dict(send_sem=send_sem.at[wi], recv_sem=recv_sem.at[wi],
                          device_id=(x, y, 1 - c), device_id_type=MESH)
            sends.append(pltpu.make_async_remote_copy(src_ref=outs[wi].at[c], dst_ref=outs[wi].at[c], **sems_w))
            recvs.append(pltpu.make_async_remote_copy(src_ref=outs[wi].at[1 - c], dst_ref=outs[wi].at[1 - c], **sems_w))
        return sends, recvs

    def start(ins, outs, sems):
        for cp in copies(outs, sems)[0]:
            cp.start()

    def finish(ins, outs, sems):
        sends, recvs = copies(outs, sems)
        for cp in recvs:
            cp.wait_recv()
        for cp in sends:
            cp.wait_send()

    return _Comm("sibling", terms, [_sds(t.shape, t.dtype) for t in terms], {i: i for i in range(n)},
                 [pltpu.SemaphoreType.DMA((n,)), pltpu.SemaphoreType.DMA((n,))], start, finish)


def _chip_sum(name, grad, got, core):
    _, _, hr, c = grad.shape
    rb = _pick(hr, max(16, (1 << 19) // c), 16)

    def body(core_ref, a_ref, b_ref, o_ref):
        o_ref[...] = (a_ref[...].astype(F32) + b_ref[...].astype(F32)).astype(BF16)

    out_spec = pl.BlockSpec((None, rb, c), lambda t, i, core_ref: (t, i, 0))
    return pl.pallas_call(
        body, name=name,
        grid_spec=pltpu.PrefetchScalarGridSpec(
            num_scalar_prefetch=1, grid=(N_CHIPS, hr // rb),
            in_specs=[pl.BlockSpec((None, None, rb, c), lambda t, i, core_ref: (t, core_ref[0], i, 0)), out_spec],
            out_specs=out_spec),
        out_shape=_sds((N_CHIPS, hr, c), BF16), compiler_params=_params(),
    )(core, grad, got)


def _all_reduce_small(pack):
    r = pack.shape[0]

    def body(p_ref, o_ref, land_ref, send_sem, recv_sem):
        x, y, c, _ = _mesh_place()
        me = 4 * x + 2 * y + c
        flips = [(k >> 2 & 1, k >> 1 & 1, k & 1) for k in range(1, N_DEV)]

        def peer(fx, fy, fc):
            return (1 - x if fx else x, 1 - y if fy else y, 1 - c if fc else c)

        land_ref[me] = p_ref[...]
        sent = []
        for k, flip in enumerate(flips):
            cp = pltpu.make_async_remote_copy(
                src_ref=p_ref, dst_ref=land_ref.at[me], send_sem=send_sem.at[k], recv_sem=recv_sem.at[k],
                device_id=peer(*flip), device_id_type=MESH)
            cp.start()
            sent.append(cp)
        for k, flip in enumerate(flips):
            px, py, pc = peer(*flip)
            slot = land_ref.at[4 * px + 2 * py + pc]
            pltpu.make_async_remote_copy(
                src_ref=slot, dst_ref=slot, send_sem=send_sem.at[k], recv_sem=recv_sem.at[k],
                device_id=(px, py, pc), device_id_type=MESH).wait_recv()
        total = land_ref[0]
        for d in range(1, N_DEV):
            total = total + land_ref[d]
        o_ref[...] = total
        for cp in sent:
            cp.wait_send()

    vmem = pl.BlockSpec(memory_space=pltpu.VMEM)
    return pl.pallas_call(
        body, name="all_reduce_small", in_specs=[vmem], out_specs=vmem, out_shape=_sds((r, 128), F32),
        scratch_shapes=[pltpu.VMEM((N_DEV, r, 128), F32), pltpu.SemaphoreType.DMA((N_DEV - 1,)),
                        pltpu.SemaphoreType.DMA((N_DEV - 1,))],
    )(pack)


PACK_TILE = 8 * 128


def _pack(items):
    rows, i = [], 0
    while i < len(items):
        j = i
        while j < len(items) and items[j].size == items[i].size:
            j += 1
        group = jnp.stack([it.reshape(-1).astype(F32) for it in items[i:j]])
        rows.append(jnp.pad(group, ((0, 0), (0, -group.shape[1] % PACK_TILE))).reshape(-1, 128))
        i = j
    return jnp.concatenate(rows, axis=0)


def _unpack(pack, shapes):
    out, row = [], 0
    for shp in shapes:
        size = int(np.prod(shp))
        nrow = -(-size // PACK_TILE) * (PACK_TILE // 128)
        out.append(pack[row:row + nrow].reshape(-1)[:size].reshape(shp))
        row += nrow
    return out


BIG = ["ffn1_w_gu", "ffn1_w_down", "w_in", "w_gate", "w_proj_a", "w_proj_b", "w_out",
       "ffn2_w_gu", "ffn2_w_down", "w_ple_gate", "w_ple_proj"]
SMALL = ["ffn1_norm", "mix_norm", "ffn2_norm", "ple_norm", "a_q_norm", "a_k_norm", "b_q_norm", "b_k_norm",
         "a_rel_bias", "b_sinks"]
WEIGHTS = ["ffn1_norm", "ffn1_w_gu", "ffn1_w_down", "mix_norm", "w_in", "a_q_norm", "a_k_norm", "a_rel_bias",
           "b_q_norm", "b_k_norm", "b_sinks", "w_gate", "w_proj_a", "w_proj_b", "w_out", "ffn2_norm",
           "ffn2_w_gu", "ffn2_w_down", "ple_norm", "w_ple_gate", "w_ple_proj"]
ATTN_A = dict(prev=A_PREV_CHUNKS * CHUNK, group=1, kw=A_WIDTH, qblk=0, kblk=1, vblk=2)
ATTN_B = dict(prev=B_PREV_CHUNKS * CHUNK, group=N_HEADS // B_KV_HEADS, kw=B_KV_WIDTH, qblk=3,
              kblk=4 * A_WIDTH // B_KV_WIDTH, vblk=4 * A_WIDTH // B_KV_WIDTH + 1)


def _cast_epilogue(accs, extras, outs, ij):
    for acc, out in zip(accs, outs):
        out[...] = acc.astype(out.dtype)


GATHER_FIRST = ["ffn1_w_gu", "ffn1_w_down"]
ROW_SHARDED = ("ffn1_w_down", "ffn2_w_down", "w_out", "w_ple_gate")


def _slotted(name, grad):
    if name == "w_in":
        rows, cols = grad.shape
        grad = jnp.transpose(grad.reshape(rows, N_CHIPS, cols // N_CHIPS), (1, 0, 2))
    elif name in ROW_SHARDED:
        grad = grad.reshape(N_CHIPS, grad.shape[0] // N_CHIPS, grad.shape[1])
    return grad.reshape(N_CHIPS, 2, grad.shape[1] // 2, grad.shape[2])


def _local_step(xt, pt, tgt, n_batch, bufs, small, core):
    t, d = xt.shape
    tm = _pick(t, ROW_TILE, 8)
    tk = _pick(t, ROW_TILE, 8)
    nt = t // tm
    row = pl.BlockSpec((tm, d), lambda i, j, k: (i, 0))
    gs = bufs["w_gate"].shape[2]
    ps = bufs["w_proj_a"].shape[2]
    es = bufs["w_ple_proj"].shape[2]
    pdim = pt.shape[1]
    ncols = N_CHIPS * bufs["w_in"].shape[2]
    tin = ncols // 2
    assert 2 * gs == d and 4 * ps == d and 4 * es == d and tin % 128 == 0

    w = {}
    halves = {n: b.reshape(N_CHIPS, 2, b.shape[1] // 2, b.shape[2]) for n, b in bufs.items()}

    def publish(names, arrays):
        for name, g in zip(names, arrays):
            g = g.reshape(N_CHIPS, 2 * g.shape[2], g.shape[3])
            if name in ROW_SHARDED:
                g = g.reshape(N_CHIPS * g.shape[1], g.shape[2])
            elif name == "w_in":
                g = jnp.transpose(g, (1, 0, 2)).reshape(g.shape[1], N_CHIPS * g.shape[2])
            w[name] = g

    class GatherPipe:
        def __init__(self, names):
            self.names = names
            self.stage = None

        def ici(self):
            self.stage = _gather_ici(self.bufs())
            return self.stage

        def d2d(self):
            self.stage = _gather_d2d(self.bufs())
            return self.stage

        def bufs(self):
            return self.stage.results if self.stage is not None else [halves[n] for n in self.names]

        def publish(self):
            publish(self.names, self.stage.results)

    class GradPipe:
        def __init__(self, names):
            self.names = names

        def exchange(self, grads):
            self.grads = [_slotted(n, g) for n, g in zip(self.names, grads)]
            self.x = _exchange_halves(self.grads)
            return self.x

        def scatter(self):
            self.sums = [_chip_sum("chip_sum_" + n, g, got, core)
                         for n, g, got in zip(self.names, self.grads, self.x.results)]
            self.s = _scatter_ici(self.sums)
            return self.s

        def forward(self):
            self.f = _scatter_d2d(self.s.results)
            return self.f

        def terms(self):
            return dict(zip(self.names, self.f.results))

    publish(GATHER_FIRST, _all_gather_weights([halves[n] for n in GATHER_FIRST]))
    g_in, g_proj, g_ple = GatherPipe(["w_in", "w_gate"]), GatherPipe(["w_proj_a", "w_proj_b", "w_out"]), \
        GatherPipe(["w_ple_gate", "w_ple_proj"])
    g_down2, g_up2 = GatherPipe(["ffn2_w_down"]), GatherPipe(["ffn2_w_gu"])
    n1 = _rms_fwd("ffn1_norm", xt, small["ffn1_norm"])
    h1, un, ffn1_saved = _ffn_fwd("ffn1", xt, n1, w["ffn1_w_gu"], w["ffn1_w_down"], small["mix_norm"],
                                  {"up": lambda: [g_in.ici()], "down": lambda: [g_in.d2d(), g_proj.ici()]})
    g_in.publish()
    w_in, wgate = w["w_in"], w["w_gate"]
    (qkv,) = _mm(
        "qkv", "nn", (nt, 2, 1),
        [(un, row, w_in, pl.BlockSpec((d, tin), lambda i, j, k: (0, j)))], [],
        [(_sds((t, ncols), BF16), pl.BlockSpec((tm, tin), lambda i, j, k: (i, j)))], (tm, tin), _cast_epilogue,
        j_outer=True, comms=[g_proj.d2d(), g_ple.ici()])
    g_proj.publish()
    wpa, wpb, wout = w["w_proj_a"], w["w_proj_b"], w["w_out"]

    def gate_epilogue(accs, extras, outs, ij):
        outs[0][...] = jax.nn.sigmoid(accs[0]).astype(BF16)

    (gates,) = _mm(
        "gate", "nn", (nt, 4, 1),
        [(un, row, wgate, pl.BlockSpec((None, d, gs), lambda i, j, k: (j, 0, 0)))], [],
        [(_sds((2, t, d), BF16), pl.BlockSpec((None, tm, gs), lambda i, j, k: (j // 2, i, j % 2)))],
        (tm, gs), gate_epilogue, j_outer=True, chunked=True, comms=[g_ple.d2d(), g_down2.ici()])
    g_ple.publish()
    wpg, wpe = w["w_ple_gate"], w["w_ple_proj"]

    bias_a = _pair_bias(_bias_a(small["a_rel_bias"][0]))
    bias_b = _pair_bias(_bias_b())
    sink_a = _pair_rows(jnp.full((N_HEADS, 128), NEG_INF, F32))
    sink_b = _pair_rows(jnp.broadcast_to(small["b_sinks"][0][:, None], (N_HEADS, 128)))
    gqa, gka, gqb, gkb = [jnp.tile(small[k], (1, 2)) for k in ("a_q_norm", "a_k_norm", "b_q_norm", "b_k_norm")]
    ya, lse_a = _attn_fwd("attn_a_fwd", qkv, bias_a, sink_a, gqa, gka, ATTN_A, n_batch,
                          comms=[g_down2.d2d(), g_up2.ici()])
    g_down2.publish()
    yb, lse_b = _attn_fwd("attn_b_fwd", qkv, bias_b, sink_b, gqb, gkb, ATTN_B, n_batch, comms=[g_up2.d2d()])
    g_up2.publish()

    def merge_epilogue(accs, extras, outs, ij):
        pa, pb = accs
        outs[0][...] = (extras[0][...].astype(F32) * pa + extras[1][...].astype(F32) * pb).astype(BF16)
        outs[1][...] = pa.astype(BF16)
        outs[2][...] = pb.astype(BF16)

    y_spec = pl.BlockSpec((tm, A_WIDTH), lambda i, j, k: (i, 0))
    proj_spec = pl.BlockSpec((None, A_WIDTH, ps), lambda i, j, k: (j, 0, 0))
    tile_ps = pl.BlockSpec((tm, ps), lambda i, j, k: (i, j))
    merged, pa, pb = _mm(
        "proj_merge", "nn", (nt, 4, 1),
        [(ya, y_spec, wpa, proj_spec), (yb, y_spec, wpb, proj_spec)],
        [(gates, pl.BlockSpec((None, tm, ps), lambda i, j, k: (0, i, j))),
         (gates, pl.BlockSpec((None, tm, ps), lambda i, j, k: (1, i, j)))],
        [(_sds((t, d), BF16), tile_ps)] * 3, (tm, ps), merge_epilogue)

    h2, n2 = _mm(
        "out_proj", "nn", (nt, 1, 1),
        [(merged, row, wout, pl.BlockSpec((d, d), lambda i, j, k: (0, 0)))],
        [(h1, row), (small["ffn2_norm"], pl.BlockSpec((1, d), lambda i, j, k: (0, 0)))],
        [(_sds((t, d), F32), row), (_sds((t, d), BF16), row)], (tm, d), _residual_norm_epilogue(1.0))

    h3, n3, ffn2_saved = _ffn_fwd("ffn2", h2, n2, w["ffn2_w_gu"], w["ffn2_w_down"], small["ple_norm"], {})
    tile_es = pl.BlockSpec((tm, es), lambda i, j, k: (i, j))
    th = _pick(d, 512)

    def head_epilogue(accs, extras, outs, ij):
        h3_ref, tgt_ref = extras
        dy_ref, dpe_ref, dz_ref, loss_ref = outs
        pg = jax.nn.sigmoid(accs[0])
        pev = accs[1]
        diff = h3_ref[...] + pg * pev - tgt_ref[...]
        dy = diff * (1.0 / d)
        dy_ref[...] = dy
        dpe_ref[...] = (dy * pg).astype(BF16)
        dz_ref[...] = (dy * pev * pg * (1.0 - pg)).astype(BF16)
        _accumulate(loss_ref, jnp.full(loss_ref.shape, jnp.sum(diff * diff), F32), (ij[0] == 0) & (ij[1] == 0))

    tile_h = pl.BlockSpec((tm, th), lambda i, j, k: (i, j))
    dy, dpe, dz, loss_acc = _mm(
        "ple_gate_loss", "nn", (nt, 4, 1),
        [(n3, row, wpg, pl.BlockSpec((d, es), lambda i, j, k: (0, j))),
         (pt, pl.BlockSpec((tm, pdim), lambda i, j, k: (i, 0)), wpe, pl.BlockSpec((None, pdim, es), lambda i, j, k: (j, 0, 0)))],
        [(h3, tile_es), (tgt, tile_es)],
        [(_sds((t, d), F32), tile_es), (_sds((t, d), BF16), tile_es), (_sds((t, d), BF16), tile_es),
         (_sds((8, 128), F32), pl.BlockSpec((8, 128), lambda i, j, k: (0, 0)))],
        (tm, es), head_epilogue, j_outer=True, chunked=True)
    loss = 0.5 * loss_acc[0, 0] / d

    nk = t // tk
    (dwpe,) = _mm(
        "d_w_ple_proj", "tn", (1, 4, nk),
        [(pt, pl.BlockSpec((tk, pdim), lambda i, j, k: (k, 0)), dpe, pl.BlockSpec((tk, es), lambda i, j, k: (k, j)))],
        [], [(_sds((4, pdim, es), BF16), pl.BlockSpec((None, pdim, es), lambda i, j, k: (j, 0, 0)))],
        (pdim, es), _cast_epilogue)

    def dense_grad(name, a, dyb, comms=()):
        (res,) = _mm(
            name, "tn", (1, d // th, nk),
            [(a, pl.BlockSpec((tk, d), lambda i, j, k: (k, 0)), dyb, pl.BlockSpec((tk, th), lambda i, j, k: (k, j)))],
            [], [(_sds((d, d), BF16), pl.BlockSpec((d, th), lambda i, j, k: (0, j)))], (d, th), _cast_epilogue,
            comms=comms)
        return res

    dwpg = dense_grad("d_w_ple_gate", n3, dz)
    tmn = _pick(t, ROW_TILE, 8)
    extras, outs = _rms_bwd_io(h3, small["ple_norm"], dy, tmn)
    dh3, dh3_b, d_ple_norm = _mm(
        "d_ple_norm", "nt", (t // tmn, 1, 1),
        [(dz, pl.BlockSpec((tmn, d), lambda i, j, k: (i, 0)), wpg, pl.BlockSpec((d, d), lambda i, j, k: (0, 0)))],
        extras, outs, (tmn, d), _rms_bwd_epilogue)

    up2, down2, ple = GradPipe(["ffn2_w_gu"]), GradPipe(["ffn2_w_down"]), GradPipe(["w_ple_gate", "w_ple_proj"])
    proj = GradPipe(["w_proj_a", "w_proj_b", "w_out"])
    dh2, dh2_b, d_ffn2_norm, dwgu2, dwd2 = _ffn_bwd(
        "ffn2", dh3, dh3_b, h2, small["ffn2_norm"], w["ffn2_w_gu"], w["ffn2_w_down"], ffn2_saved,
        {"dnorm": lambda dwgu, dwd: [up2.exchange([dwgu]), down2.exchange([dwd]), ple.exchange([dwpg, dwpe])]})

    def dmerge_epilogue(accs, extras, outs, ij):
        dmo = accs[0]
        g_ref, pa_ref, pb_ref = extras
        dg_ref, dpa_ref, dpb_ref = outs
        ga = g_ref[0].astype(F32)
        gb = g_ref[1].astype(F32)
        dg_ref[0] = (dmo * pa_ref[...].astype(F32) * ga * (1.0 - ga)).astype(BF16)
        dg_ref[1] = (dmo * pb_ref[...].astype(F32) * gb * (1.0 - gb)).astype(BF16)
        dpa_ref[...] = (dmo * ga).astype(BF16)
        dpb_ref[...] = (dmo * gb).astype(BF16)

    g_spec = pl.BlockSpec((2, tm, th), lambda i, j, k: (0, i, j))
    dgates, dpa, dpb = _mm(
        "d_merge", "nt", (nt, d // th, 1),
        [(dh2_b, row, wout, pl.BlockSpec((th, d), lambda i, j, k: (j, 0)))],
        [(gates, g_spec), (pa, tile_h), (pb, tile_h)],
        [(_sds((2, t, d), BF16), g_spec), (_sds((t, d), BF16), tile_h), (_sds((t, d), BF16), tile_h)],
        (tm, th), dmerge_epilogue, j_outer=True, chunked=True, comms=[down2.scatter()])
    dwout = dense_grad("d_w_out", merged, dh2_b, comms=[down2.forward(), ple.scatter()])

    yk_spec = pl.BlockSpec((tk, A_WIDTH), lambda i, j, k: (k, 0))
    dk_spec = pl.BlockSpec((tk, ps), lambda i, j, k: (k, j))
    dproj = (_sds((4, A_WIDTH, ps), BF16), proj_spec)
    dwpa, dwpb = _mm(
        "d_w_proj", "tn", (1, 4, nk),
        [(ya, yk_spec, dpa, dk_spec), (yb, yk_spec, dpb, dk_spec)], [], [dproj, dproj], (A_WIDTH, ps), _cast_epilogue,
        comms=[ple.forward()])
    dproj_a = pl.BlockSpec((tm, ps), lambda i, j, k: (i, k))
    wproj_k = pl.BlockSpec((None, A_WIDTH, ps), lambda i, j, k: (k, 0, 0))
    dya, dyb = _mm(
        "d_attn_out", "nt", (nt, 1, 4),
        [(dpa, dproj_a, wpa, wproj_k), (dpb, dproj_a, wpb, wproj_k)], [],
        [(_sds((t, A_WIDTH), BF16), y_spec)] * 2, (tm, A_WIDTH), _cast_epilogue,
        comms=[proj.exchange([dwpa, dwpb, dwout])])

    dqa, dka, dva, dbias_a, _, dgqa, dgka = _attn_bwd(
        "attn_a_bwd", qkv, bias_a, sink_a, gqa, gka, ya, dya, lse_a, ATTN_A, n_batch, True,
        comms=[up2.scatter(), proj.scatter()])
    dqb, dkb, dvb, _, dsink_b, dgqb, dgkb = _attn_bwd(
        "attn_b_bwd", qkv, bias_b, sink_b, gqb, gkb, yb, dyb, lse_b, ATTN_B, n_batch, False,
        comms=[up2.forward(), proj.forward()])
    dqkv = jnp.concatenate([dqa, dka, dva, dqb, dkb, dvb], axis=1)

    (dwgate,) = _mm(
        "d_w_gate", "tn", (1, 4, nk),
        [(un, pl.BlockSpec((tk, d), lambda i, j, k: (k, 0)),
          dgates, pl.BlockSpec((None, tk, gs), lambda i, j, k: (j // 2, k, j % 2)))],
        [], [(_sds((4, d, gs), BF16), pl.BlockSpec((None, d, gs), lambda i, j, k: (j, 0, 0)))], (d, gs), _cast_epilogue)
    (dwin,) = _mm(
        "d_w_in", "tn", (1, 2, nk),
        [(un, pl.BlockSpec((tk, d), lambda i, j, k: (k, 0)), dqkv, pl.BlockSpec((tk, tin), lambda i, j, k: (k, j)))],
        [], [(_sds((d, ncols), BF16), pl.BlockSpec((d, tin), lambda i, j, k: (0, j)))], (d, tin), _cast_epilogue)

    mixer = GradPipe(["w_in", "w_gate"])
    extras, outs = _rms_bwd_io(h1, small["mix_norm"], dh2, tmn)
    dh1, dh1_b, d_mix_norm = _mm(
        "d_mix_norm", "nt", (t // tmn, 1, 6),
        [(dgates, pl.BlockSpec((None, tmn, gs), lambda i, j, k: (jnp.minimum(k, 3) // 2, i, jnp.minimum(k, 3) % 2)),
          wgate, pl.BlockSpec((None, d, gs), lambda i, j, k: (jnp.minimum(k, 3), 0, 0))),
         (dqkv, pl.BlockSpec((tmn, tin), lambda i, j, k: (i, jnp.maximum(k - 4, 0))),
          w_in, pl.BlockSpec((d, tin), lambda i, j, k: (0, jnp.maximum(k - 4, 0))))],
        extras, outs, (tmn, d), _rms_bwd_epilogue, steps=[4, 2],
        comms=[mixer.exchange([dwin, dwgate])])

    up1 = GradPipe(["ffn1_w_gu"])
    down1 = GradPipe(["ffn1_w_down"])
    dx, _, d_ffn1_norm, _, _ = _ffn_bwd(
        "ffn1", dh1, dh1_b, xt, small["ffn1_norm"], w["ffn1_w_gu"], w["ffn1_w_down"], ffn1_saved,
        {"dwgu": lambda: [mixer.scatter()],
         "dwd": lambda dwgu: [mixer.forward(), up1.exchange([dwgu])],
         "dnorm": lambda dwgu, dwd: [up1.scatter(), down1.exchange([dwd])]})
    _run_comms("grad_tail_scatter", [up1.forward(), down1.scatter()])
    _run_comms("grad_tail_forward", [down1.forward()])
    terms = {}
    for pipe in (up2, down2, ple, proj, mixer, up1, down1):
        terms.update(pipe.terms())

    def fold(v):
        return v[0, :HEAD_DIM] + v[0, HEAD_DIM:]

    small_grads = {"ffn1_norm": d_ffn1_norm, "mix_norm": d_mix_norm, "ffn2_norm": d_ffn2_norm,
                   "ple_norm": d_ple_norm, "a_q_norm": fold(dgqa), "a_k_norm": fold(dgka),
                   "b_q_norm": fold(dgqb), "b_k_norm": fold(dgkb), "a_rel_bias": _rel_bias_grad(_unpair_bias(dbias_a)),
                   "b_sinks": jnp.sum(dsink_b, axis=1)}
    return loss, dx, terms, small_grads


def kernel(x, p, ffn1_norm, ffn1_w_gu, ffn1_w_down, mix_norm, w_in, a_q_norm, a_k_norm, a_rel_bias, b_q_norm, b_k_norm, b_sinks, w_gate, w_proj_a, w_proj_b, w_out, ffn2_norm, ffn2_w_gu, ffn2_w_down, ple_norm, w_ple_gate, w_ple_proj, loss_target, m_ffn1_norm, m_ffn1_w_gu, m_ffn1_w_down, m_mix_norm, m_w_in, m_a_q_norm, m_a_k_norm, m_a_rel_bias, m_b_q_norm, m_b_k_norm, m_b_sinks, m_w_gate, m_w_proj_a, m_w_proj_b, m_w_out, m_ffn2_norm, m_ffn2_w_gu, m_ffn2_w_down, m_ple_norm, m_w_ple_gate, m_w_ple_proj, v_ffn1_norm, v_ffn1_w_gu, v_ffn1_w_down, v_mix_norm, v_w_in, v_a_q_norm, v_a_k_norm, v_a_rel_bias, v_b_q_norm, v_b_k_norm, v_b_sinks, v_w_gate, v_w_proj_a, v_w_proj_b, v_w_out, v_ffn2_norm, v_ffn2_w_gu, v_ffn2_w_down, v_ple_norm, v_w_ple_gate, v_w_ple_proj):
    given = dict(locals())
    n_batch, s, d = x.shape
    t = n_batch * s
    xt = x.reshape(t, d)
    pt = p.reshape(t, p.shape[-1])
    tgt = loss_target.reshape(t, d)

    chip = (2 * lax.axis_index("x") + lax.axis_index("y")).astype(jnp.int32).reshape(1)
    bufs = {name: _cast_into_slot("cast_" + name, given[name][0], chip) for name in BIG}
    small = {name: given[name] for name in SMALL}
    core = lax.axis_index("c").astype(jnp.int32).reshape(1)
    loss, dx, terms, small_grads = _local_step(xt, pt, tgt, n_batch, bufs, small, core)

    grads, deltas, new_m, new_v = {}, {}, {}, {}
    for name in BIG:
        gw, dl, nm, nv = _adamw_terms("adamw_" + name, terms[name], given[name][0], given["m_" + name][0],
                                      given["v_" + name][0])
        grads[name], deltas[name], new_m[name], new_v[name] = gw[None], dl[None], nm[None], nv[None]

    small_shapes = [given[name].shape for name in SMALL] + [()]
    g_pack = _all_reduce_small(_pack([small_grads[name] for name in SMALL] + [loss]))
    zero = jnp.zeros((), F32)
    w_pack = _pack([given[name] for name in SMALL] + [zero])
    m_pack = _pack([given["m_" + name] for name in SMALL] + [zero])
    v_pack = _pack([given["v_" + name] for name in SMALL] + [zero])
    d_pack, nm_pack, nv_pack = _ew("adamw_small", lambda wv, gv, mv, vv: _adamw_math(wv, gv, mv, vv),
                                   [w_pack, g_pack, m_pack, v_pack], [F32] * 3)
    g_small = _unpack(g_pack, small_shapes)
    loss_total = g_small[-1]
    for name, gv, dv, mv, vv in zip(SMALL, g_small, _unpack(d_pack, small_shapes), _unpack(nm_pack, small_shapes),
                                    _unpack(nv_pack, small_shapes)):
        grads[name], deltas[name], new_m[name], new_v[name] = gv, dv, mv, vv

    return (loss_total, dx.reshape(x.shape), *[grads[n] for n in WEIGHTS], *[deltas[n] for n in WEIGHTS],
            *[new_m[n] for n in WEIGHTS], *[new_v[n] for n in WEIGHTS])
```

```python
import functools

import numpy as np
import jax
import jax.numpy as jnp
from jax import lax
from jax.experimental import pallas as pl
from jax.experimental.pallas import tpu as pltpu

F32 = jnp.float32
BF16 = jnp.bfloat16

CHUNK = 64
HEAD_DIM = 64
A_PREV_CHUNKS = 8
A_MAX_REL = 128
N_HEADS = 8
B_KV_HEADS = 2
B_PREV_CHUNKS = 2
A_WIDTH = N_HEADS * HEAD_DIM
B_KV_WIDTH = B_KV_HEADS * HEAD_DIM
EPS = 1e-6
NEG_INF = -1e30
ATTN_SCALE = HEAD_DIM ** -0.5
Q_BLOCK = 128
PAIR = 2 * HEAD_DIM

ADAM_LR = 0.001
ADAM_B1 = 0.9
ADAM_B2 = 0.999
ADAM_EPS = 1e-08
ADAM_WD = 0.01
ADAM_STEP = 10

N_CHIPS = 4
N_DEV = 8
VMEM_LIMIT_V7X = 56 * 1024 * 1024
ROW_TILE = 1024
MESH = pl.DeviceIdType.MESH
COLLECTIVE_IDS = {("sibling",): 1, ("chips",): 2, ("chips", "sibling"): 3}
ANY = pl.BlockSpec(memory_space=pl.ANY)

_DN = {
    "nn": (((1,), (0,)), ((), ())),
    "nt": (((1,), (1,)), ((), ())),
    "tn": (((0,), (0,)), ((), ())),
}


def _pick(n, target, mult=128):
    best = None
    for d in range(mult, min(n, target) + 1, mult):
        if n % d == 0:
            best = d
    return n if best is None else best


def _dot(a, b, mode):
    return lax.dot_general(a.astype(BF16), b.astype(BF16), _DN[mode], preferred_element_type=F32)


def _params():
    return pltpu.CompilerParams(vmem_limit_bytes=VMEM_LIMIT_V7X)


class _Comm:
    def __init__(self, peers, ins, outs, aliases, sems, start, finish):
        self.peers = peers
        self.ins, self.outs, self.aliases, self.sems = list(ins), list(outs), dict(aliases), list(sems)
        self.start, self.finish = start, finish
        self.results = None


class _CommPlumbing:
    def __init__(self, comms, n_in, n_out, n_scratch):
        self.comms = list(comms)
        self.n_in, self.n_out, self.n_scratch = n_in, n_out, n_scratch
        self.args = [a for cm in self.comms for a in cm.ins]
        self.out_shape = [o for cm in self.comms for o in cm.outs]
        self.scratch = [s for cm in self.comms for s in cm.sems]
        self.aliases = {}
        i0, o0 = n_in, n_out
        for cm in self.comms:
            for a, b in cm.aliases.items():
                self.aliases[i0 + a] = o0 + b
            i0 += len(cm.ins)
            o0 += len(cm.outs)

    def _parts(self, in_refs, out_refs, scratch_refs):
        parts = []
        i0, o0, s0 = self.n_in, self.n_out, self.n_scratch
        for cm in self.comms:
            parts.append((in_refs[i0:i0 + len(cm.ins)], out_refs[o0:o0 + len(cm.outs)],
                          scratch_refs[s0:s0 + len(cm.sems)]))
            i0 += len(cm.ins)
            o0 += len(cm.outs)
            s0 += len(cm.sems)
        return parts

    def kinds(self):
        return sorted(set(cm.peers for cm in self.comms))

    def params(self, **kwargs):
        if self.comms:
            kwargs["collective_id"] = COLLECTIVE_IDS[tuple(self.kinds())]
        return pltpu.CompilerParams(**kwargs)

    def handshake(self):
        x, y, c, chips = _mesh_place()
        peers = []
        if "sibling" in self.kinds():
            peers.append((x, y, 1 - c))
        if "chips" in self.kinds():
            peers += [(tx, ty, c) for tx, ty in chips]
        barrier = pltpu.get_barrier_semaphore()
        for peer in peers:
            pl.semaphore_signal(barrier, inc=1, device_id=peer, device_id_type=MESH)
        pl.semaphore_wait(barrier, len(peers))

    def start_at(self, in_refs, out_refs, scratch_refs, first):
        if self.comms:
            parts = self._parts(in_refs, out_refs, scratch_refs)

            @pl.when(first)
            def _():
                self.handshake()
                for cm, part in zip(self.comms, parts):
                    cm.start(*part)

    def finish_at(self, in_refs, out_refs, scratch_refs, last):
        if self.comms:
            parts = self._parts(in_refs, out_refs, scratch_refs)

            @pl.when(last)
            def _():
                for cm, part in zip(self.comms, parts):
                    cm.finish(*part)

    def deliver(self, results):
        o0 = self.n_out
        for cm in self.comms:
            cm.results = list(results[o0:o0 + len(cm.outs)])
            o0 += len(cm.outs)
        return list(results[:self.n_out])


def _swap_ij(spec):
    index_map = spec.index_map
    return pl.BlockSpec(spec.block_shape, lambda j, i, k: index_map(i, j, k))


MXU_COLUMNS_V7X = 256


def _mm(name, mode, grid, pairs, extras, outs, acc_shape, epilogue, steps=None, comms=(), j_outer=False,
        chunked=False):
    ni, nj, nk = grid
    n_in = 2 * len(pairs) + len(extras)
    n_out = len(outs)
    tn = acc_shape[1]
    col_chunks = None
    if chunked:
        assert nk == 1 and steps is None and mode in ("nn", "nt")
        col_chunks = [(c0, min(MXU_COLUMNS_V7X, tn - c0)) for c0 in range(0, tn, MXU_COLUMNS_V7X)]
    n_acc = 0 if chunked else (len(pairs) if steps is None else 1)
    plumb = _CommPlumbing(comms, n_in, n_out, n_acc)
    n_all_in = n_in + len(plumb.args)
    n_all_out = n_out + len(plumb.out_shape)
    if j_outer:
        grid = (nj, ni, nk)
        pairs = [(a, _swap_ij(a_spec), b, _swap_ij(b_spec)) for a, a_spec, b, b_spec in pairs]
        extras = [(e, _swap_ij(e_spec)) for e, e_spec in extras]
        outs = [(o, _swap_ij(o_spec)) for o, o_spec in outs]

    def body(*refs):
        in_refs = refs[:n_all_in]
        out_refs = refs[n_all_in:n_all_in + n_all_out]
        scratch = refs[n_all_in + n_all_out:]
        accs = scratch[:n_acc]
        i = pl.program_id(1 if j_outer else 0)
        j = pl.program_id(0 if j_outer else 1)
        k = pl.program_id(2)
        plumb.start_at(in_refs, out_refs, scratch, (i == 0) & (j == 0) & (k == 0))

        def contrib(p, acc):
            acc[...] += _dot(in_refs[2 * p][...], in_refs[2 * p + 1][...], mode)

        if col_chunks:
            def cols(ref, c0, cs):
                if ref.shape[-1] != tn:
                    return ref
                return ref.at[(slice(None),) * (len(ref.shape) - 1) + (pl.ds(c0, cs),)]

            lhs = [in_refs[2 * p][...] for p in range(len(pairs))]
            for ci, (c0, cs) in enumerate(col_chunks):
                vals = []
                for p in range(len(pairs)):
                    b_ref = in_refs[2 * p + 1]
                    rhs = b_ref[:, c0:c0 + cs] if mode == "nn" else b_ref[c0:c0 + cs, :]
                    vals.append(_dot(lhs[p], rhs, mode))
                epilogue(vals, [cols(r, c0, cs) for r in in_refs[2 * len(pairs):n_in]],
                         [cols(r, c0, cs) for r in out_refs[:n_out]], (i, j * len(col_chunks) + ci))
        else:
            @pl.when(k == 0)
            def _():
                for acc in accs:
                    acc[...] = jnp.zeros(acc.shape, F32)

            if steps is None:
                for p in range(len(pairs)):
                    contrib(p, accs[p])
            else:
                lo = 0
                for p, n in enumerate(steps):
                    pl.when((k >= lo) & (k < lo + n))(functools.partial(contrib, p, accs[0]))
                    lo += n

            @pl.when(k == nk - 1)
            def _():
                epilogue([acc[...] for acc in accs], in_refs[2 * len(pairs):n_in], out_refs[:n_out], (i, j))

        plumb.finish_at(in_refs, out_refs, scratch, (i == ni - 1) & (j == nj - 1) & (k == nk - 1))

    args, in_specs = [], []
    for a, a_spec, b, b_spec in pairs:
        args += [a, b]
        in_specs += [a_spec, b_spec]
    for e, e_spec in extras:
        args.append(e)
        in_specs.append(e_spec)
    res = pl.pallas_call(
        body,
        name=name,
        grid=grid,
        in_specs=in_specs + [ANY] * len(plumb.args),
        out_specs=[s for _, s in outs] + [ANY] * len(plumb.out_shape),
        out_shape=[o for o, _ in outs] + plumb.out_shape,
        scratch_shapes=[pltpu.VMEM(acc_shape, F32) for _ in range(n_acc)] + plumb.scratch,
        input_output_aliases=plumb.aliases,
        compiler_params=plumb.params(vmem_limit_bytes=VMEM_LIMIT_V7X),
    )(*args, *plumb.args)
    return plumb.deliver(res)


def _sds(shape, dtype):
    return jax.ShapeDtypeStruct(shape, dtype)


def _accumulate(ref, value, first):
    @pl.when(first)
    def _():
        ref[...] = value

    @pl.when(jnp.logical_not(first))
    def _():
        ref[...] += value


def _rms_fwd(name, x, gain, comms=()):
    t, d = x.shape
    tm = _pick(t, ROW_TILE, 8)
    steps = t // tm
    plumb = _CommPlumbing(comms, 2, 1, 0)
    n_all_in = 2 + len(plumb.args)
    n_all_out = 1 + len(plumb.out_shape)

    def body(*refs):
        x_ref, g_ref = refs[:2]
        y_ref = refs[n_all_in]
        comm_refs = (refs[:n_all_in], refs[n_all_in:n_all_in + n_all_out], refs[n_all_in + n_all_out:])
        i = pl.program_id(0)
        plumb.start_at(*comm_refs, i == 0)
        xv = x_ref[...]
        rstd = lax.rsqrt(jnp.mean(xv * xv, axis=-1, keepdims=True) + EPS)
        y_ref[...] = (xv * rstd * g_ref[...]).astype(BF16)
        plumb.finish_at(*comm_refs, i == steps - 1)

    res = pl.pallas_call(
        body, name=name, grid=(steps,),
        in_specs=[pl.BlockSpec((tm, d), lambda i: (i, 0)), pl.BlockSpec((1, d), lambda i: (0, 0))]
        + [ANY] * len(plumb.args),
        out_specs=[pl.BlockSpec((tm, d), lambda i: (i, 0))] + [ANY] * len(plumb.out_shape),
        out_shape=[_sds((t, d), BF16)] + plumb.out_shape,
        scratch_shapes=plumb.scratch,
        input_output_aliases=plumb.aliases,
        compiler_params=plumb.params(vmem_limit_bytes=VMEM_LIMIT_V7X),
    )(x, gain, *plumb.args)
    return plumb.deliver(res)[0]


def _rms_bwd_epilogue(accs, extras, outs, ij):
    x_ref, g_ref, r_ref = extras
    dh_ref, dhb_ref, dg_ref = outs
    dn = accs[0]
    xv = x_ref[...]
    rstd = lax.rsqrt(jnp.mean(xv * xv, axis=-1, keepdims=True) + EPS)
    xhat = xv * rstd
    gd = dn * g_ref[...]
    dx = rstd * (gd - xhat * jnp.mean(gd * xhat, axis=-1, keepdims=True))
    dh = r_ref[...] + dx
    dh_ref[...] = dh
    dhb_ref[...] = dh.astype(BF16)
    _accumulate(dg_ref, jnp.sum(dn * xhat, axis=0, keepdims=True), ij[0] == 0)


def _rms_bwd_io(x, gain, dres, tm):
    t, d = x.shape
    row = pl.BlockSpec((tm, d), lambda i, j, k: (i, 0))
    extras = [(x, row), (gain, pl.BlockSpec((1, d), lambda i, j, k: (0, 0))), (dres, row)]
    outs = [(_sds((t, d), F32), row), (_sds((t, d), BF16), row),
            (_sds((1, d), F32), pl.BlockSpec((1, d), lambda i, j, k: (0, 0)))]
    return extras, outs


def _residual_norm_epilogue(scale):
    def epilogue(accs, extras, outs, ij):
        hv = extras[0][...] + scale * accs[0]
        outs[0][...] = hv
        rstd = lax.rsqrt(jnp.mean(hv * hv, axis=-1, keepdims=True) + EPS)
        outs[1][...] = (hv * rstd * extras[1][...]).astype(BF16)
    return epilogue


def _ffn_fwd(tag, h, n, wgu, wd, next_gain, hooks):
    t, d = h.shape
    fs = wgu.shape[2]
    f = 2 * fs
    tm = _pick(t, ROW_TILE, 8)

    def up_epilogue(accs, extras, outs, ij):
        g, u = accs
        gu_ref, a_ref = outs
        gu_ref[0] = g.astype(BF16)
        gu_ref[1] = u.astype(BF16)
        a_ref[...] = (g * jax.nn.sigmoid(g) * u).astype(BF16)

    a_spec = pl.BlockSpec((tm, d), lambda i, j, k: (i, 0))
    gu, a = _mm(
        tag + "_up", "nn", (t // tm, 2, 1),
        [(n, a_spec, wgu, pl.BlockSpec((None, d, fs), lambda i, j, k: (j, 0, 0))),
         (n, a_spec, wgu, pl.BlockSpec((None, d, fs), lambda i, j, k: (j + 2, 0, 0)))],
        [],
        [(_sds((2, t, f), BF16), pl.BlockSpec((2, tm, fs), lambda i, j, k: (0, i, j))),
         (_sds((t, f), BF16), pl.BlockSpec((tm, fs), lambda i, j, k: (i, j)))],
        (tm, fs), up_epilogue, comms=hooks.get("up", lambda: ())(), j_outer=True, chunked=True)

    row = pl.BlockSpec((tm, d), lambda i, j, k: (i, 0))
    h_new, n_new = _mm(
        tag + "_down", "nn", (t // tm, 1, 1),
        [(a, pl.BlockSpec((tm, f), lambda i, j, k: (i, 0)), wd, pl.BlockSpec((f, d), lambda i, j, k: (0, 0)))],
        [(h, row), (next_gain, pl.BlockSpec((1, d), lambda i, j, k: (0, 0)))],
        [(_sds((t, d), F32), row), (_sds((t, d), BF16), row)], (tm, d), _residual_norm_epilogue(0.5),
        comms=hooks.get("down", lambda: ())())
    return h_new, n_new, (n, gu, a)


def _ffn_bwd(tag, dh, dh_b, h, gain, wgu, wd, saved, hooks):
    n, gu, a = saved
    t, d = h.shape
    fs = wgu.shape[2]
    f = 2 * fs
    tm = _pick(t, ROW_TILE, 8)
    tk = _pick(t, ROW_TILE, 8)

    def dact_epilogue(accs, extras, outs, ij):
        da = 0.5 * accs[0]
        g = extras[0][0].astype(F32)
        u = extras[0][1].astype(F32)
        sg = jax.nn.sigmoid(g)
        outs[0][0] = (da * u * sg * (1.0 + g * (1.0 - sg))).astype(BF16)
        outs[0][1] = (da * g * sg).astype(BF16)

    gu_spec = pl.BlockSpec((2, tm, fs), lambda i, j, k: (0, i, j))
    (dgu,) = _mm(
        tag + "_dact", "nt", (t // tm, 2, 1),
        [(dh_b, pl.BlockSpec((tm, d), lambda i, j, k: (i, 0)), wd, pl.BlockSpec((fs, d), lambda i, j, k: (j, 0)))],
        [(gu, gu_spec)], [(_sds((2, t, f), BF16), gu_spec)], (tm, fs), dact_epilogue, j_outer=True, chunked=True,
        comms=hooks.get("dact", lambda: ())())

    def cast_epilogue(accs, extras, outs, ij):
        outs[0][...] = accs[0].astype(BF16)

    (dwgu,) = _mm(
        tag + "_dwgu", "tn", (1, 4, t // tk),
        [(n, pl.BlockSpec((tk, d), lambda i, j, k: (k, 0)),
          dgu, pl.BlockSpec((None, tk, fs), lambda i, j, k: (j // 2, k, j % 2)))],
        [], [(_sds((4, d, fs), BF16), pl.BlockSpec((None, d, fs), lambda i, j, k: (j, 0, 0)))], (d, fs), cast_epilogue,
        comms=hooks.get("dwgu", lambda: ())())

    def half_epilogue(accs, extras, outs, ij):
        outs[0][...] = (0.5 * accs[0]).astype(BF16)

    (dwd,) = _mm(
        tag + "_dwd", "tn", (2, 1, t // tk),
        [(a, pl.BlockSpec((tk, fs), lambda i, j, k: (k, i)), dh_b, pl.BlockSpec((tk, d), lambda i, j, k: (k, 0)))],
        [], [(_sds((f, d), BF16), pl.BlockSpec((fs, d), lambda i, j, k: (i, 0)))], (fs, d), half_epilogue,
        comms=hooks.get("dwd", lambda g: ())(dwgu))

    tmn = _pick(t, ROW_TILE, 8)
    extras, outs = _rms_bwd_io(h, gain, dh, tmn)
    dh_in, dh_in_b, dgain = _mm(
        tag + "_dnorm", "nt", (t // tmn, 1, 4),
        [(dgu, pl.BlockSpec((None, tmn, fs), lambda i, j, k: (k // 2, i, k % 2)),
          wgu, pl.BlockSpec((None, d, fs), lambda i, j, k: (k, 0, 0)))],
        extras, outs, (tmn, d), _rms_bwd_epilogue, comms=hooks.get("dnorm", lambda g, w: ())(dwgu, dwd))
    return dh_in, dh_in_b, dgain, dwgu, dwd


def _lane_lo(shape):
    return lax.broadcasted_iota(jnp.int32, shape, 1) < HEAD_DIM


def _pair_norm(xv, gain):
    lo = _lane_lo(xv.shape)
    x2 = xv * xv
    ms_lo = jnp.sum(jnp.where(lo, x2, 0.0), axis=-1, keepdims=True) * (1.0 / HEAD_DIM)
    ms_hi = jnp.sum(jnp.where(lo, 0.0, x2), axis=-1, keepdims=True) * (1.0 / HEAD_DIM)
    rstd = jnp.where(lo, lax.rsqrt(ms_lo + EPS), lax.rsqrt(ms_hi + EPS))
    xhat = xv * rstd
    return xhat * gain, xhat, rstd


def _pair_norm_bwd(dn, xhat, rstd, gain):
    lo = _lane_lo(dn.shape)
    gd = dn * gain
    t = gd * xhat
    m_lo = jnp.sum(jnp.where(lo, t, 0.0), axis=-1, keepdims=True) * (1.0 / HEAD_DIM)
    m_hi = jnp.sum(jnp.where(lo, 0.0, t), axis=-1, keepdims=True) * (1.0 / HEAD_DIM)
    dx = rstd * (gd - xhat * jnp.where(lo, m_lo, m_hi))
    return dx, jnp.sum(dn * xhat, axis=0, keepdims=True)


def _half(xv, hi):
    lo = _lane_lo(xv.shape)
    return jnp.where(lo, 0, xv) if hi else jnp.where(lo, xv, 0)


def _attn_window(i, prev):
    q0 = i * Q_BLOCK
    start = jnp.maximum(q0 - prev, 0)
    off = start - (q0 - prev)
    return pl.multiple_of(start, Q_BLOCK), pl.multiple_of(off, Q_BLOCK)


Q_BLOCKS_PER_STEP = 4
STEP_ROWS = Q_BLOCKS_PER_STEP * Q_BLOCK


def _attn_specs(cfg, s, steps):
    kw = cfg["kw"]
    q_spec = pl.BlockSpec((STEP_ROWS, A_WIDTH), lambda b, i: (b * steps + i, cfg["qblk"]))
    k_spec = pl.BlockSpec((s, kw), lambda b, i: (b, cfg["kblk"]))
    v_spec = pl.BlockSpec((s, kw), lambda b, i: (b, cfg["vblk"]))
    return q_spec, k_spec, v_spec


def _const_spec(shape):
    return pl.BlockSpec(shape, lambda b, i: (0,) * len(shape))


KEY_CHUNK = 128


def _pair_bias(bias_t):
    wext = bias_t.shape[1]
    return jnp.transpose(bias_t.reshape(N_HEADS // 2, 2, wext, Q_BLOCK), (0, 2, 1, 3)).reshape(
        N_HEADS // 2, wext, 2 * Q_BLOCK)


def _unpair_bias(db2):
    wext = db2.shape[1]
    return jnp.transpose(db2.reshape(N_HEADS // 2, wext, 2, Q_BLOCK), (0, 2, 1, 3)).reshape(N_HEADS, wext, Q_BLOCK)


def _pair_rows(rows):
    two = rows.reshape(N_HEADS // 2, 2 * rows.shape[1])
    return jnp.broadcast_to(two[:, None, :], (N_HEADS // 2, 8, two.shape[1]))


def _sub_lo(shape):
    return lax.broadcasted_iota(jnp.int32, shape, 0) < HEAD_DIM


def _by_half(lo_row, hi_row, rows):
    return jnp.where(_sub_lo((rows, lo_row.shape[1])), lo_row, hi_row)


def _stack_pair(xn, jq, group):
    parts = []
    for hq in range(2):
        hk = ((2 * jq + hq) // group) % 2
        xm = _half(xn, hq)
        if hq != hk:
            xm = pltpu.roll(xm, HEAD_DIM, 1)
        parts.append(xm)
    return jnp.concatenate(parts, axis=0).astype(BF16)


def _place_transposed(blk, dst_ref, c, heads, group):
    bt = blk.T
    lo = _sub_lo(bt.shape)
    for h in heads:
        src_hi = ((h // group) % 2) == 1
        part = jnp.where(lo, 0.0, bt) if src_hi else jnp.where(lo, bt, 0.0)
        if src_hi != (h % 2 == 1):
            part = pltpu.roll(part, HEAD_DIM, 0)
        dst_ref[h, c] = part.astype(BF16)


def _attn_fwd(name, qkv, bias2, sink2, gq, gk, cfg, n_batch, comms=()):
    t = qkv.shape[0]
    s = t // n_batch
    steps = s // STEP_ROWS
    nkc = s // KEY_CHUNK
    prev, group, kw = cfg["prev"], cfg["group"], cfg["kw"]
    w = prev + Q_BLOCK
    n_chunks = w // KEY_CHUNK
    wext = bias2.shape[1]
    plumb = _CommPlumbing(comms, 7, 2, 4)
    n_all_in = 7 + len(plumb.args)
    n_all_out = 2 + len(plumb.out_shape)

    def body(*refs):
        q_ref, k_ref, v_ref, bias_ref, sink_ref, gq_ref, gk_ref = refs[:7]
        y_ref, lse_ref = refs[n_all_in:n_all_in + 2]
        kn_ref, vt_ref, s_ref, pst_ref = refs[n_all_in + n_all_out:n_all_in + n_all_out + 4]
        step = pl.program_id(1)
        comm_refs = (refs[:n_all_in], refs[n_all_in:n_all_in + n_all_out], refs[n_all_in + n_all_out:])
        plumb.start_at(*comm_refs, (pl.program_id(0) == 0) & (step == 0))

        @pl.when(step == 0)
        def _():
            for jk in range(kw // PAIR):
                cols = pl.ds(jk * PAIR, PAIR)
                heads = [h for h in range(N_HEADS) if (h // group) // 2 == jk]
                kn, _, _ = _pair_norm(k_ref[:, cols].astype(F32), gk_ref[...])
                kn_ref[:, cols] = kn.astype(BF16)
                for c in range(nkc):
                    _place_transposed(v_ref[pl.ds(c * KEY_CHUNK, KEY_CHUNK), cols].astype(F32), vt_ref, c, heads, group)

        sub8 = lax.broadcasted_iota(jnp.int32, (N_HEADS, Q_BLOCK), 0)
        for sb in range(Q_BLOCKS_PER_STEP):
            qrows = pl.ds(sb * Q_BLOCK, Q_BLOCK)
            start, off = _attn_window(step * Q_BLOCKS_PER_STEP + sb, prev)
            c0 = start // KEY_CHUNK
            lse = jnp.zeros((N_HEADS, Q_BLOCK), F32)
            for jq in range(N_HEADS // 2):
                kcols = pl.ds((((2 * jq) // group) // 2) * PAIR, PAIR)
                qn, _, _ = _pair_norm(q_ref[qrows, pl.ds(jq * PAIR, PAIR)].astype(F32), gq_ref[...])
                qs = _stack_pair(qn * ATTN_SCALE, jq, group)
                s_ref[...] = _dot(kn_ref[pl.ds(start, w), kcols], qs, "nt")
                m = sink_ref[jq, 0:1, :]
                for c in range(n_chunks):
                    r = pl.ds(c * KEY_CHUNK, KEY_CHUNK)
                    s2 = s_ref[r, :] + bias_ref[jq, pl.ds(off + c * KEY_CHUNK, KEY_CHUNK), :]
                    s_ref[r, :] = s2
                    m = jnp.maximum(m, jnp.max(s2, axis=0, keepdims=True))
                l = jnp.exp(sink_ref[jq, 0:1, :] - m)
                for c in range(n_chunks):
                    p = jnp.exp(s_ref[pl.ds(c * KEY_CHUNK, KEY_CHUNK), :] - m)
                    l = l + jnp.sum(p, axis=0, keepdims=True)
                    pst_ref[pl.ds(2 * c * KEY_CHUNK, KEY_CHUNK), :] = p[:, :Q_BLOCK].astype(BF16)
                    pst_ref[pl.ds((2 * c + 1) * KEY_CHUNK, KEY_CHUNK), :] = p[:, Q_BLOCK:].astype(BF16)
                vl = jnp.concatenate([vt_ref[2 * jq + hq, c0 + c] for c in range(n_chunks) for hq in range(2)], axis=1)
                ot = _dot(vl, pst_ref[...], "nn")
                inv = 1.0 / l
                ot = ot * _by_half(inv[:, :Q_BLOCK], inv[:, Q_BLOCK:], PAIR)
                y_ref[qrows, pl.ds(jq * PAIR, PAIR)] = ot.T.astype(BF16)
                lse2 = m + jnp.log(l)
                lse = jnp.where(sub8 == 2 * jq, lse2[:, :Q_BLOCK], lse)
                lse = jnp.where(sub8 == 2 * jq + 1, lse2[:, Q_BLOCK:], lse)
            lse_ref[sb] = lse
        plumb.finish_at(*comm_refs, (pl.program_id(0) == n_batch - 1) & (step == steps - 1))

    q_spec, k_spec, v_spec = _attn_specs(cfg, s, steps)
    res = pl.pallas_call(
        body, name=name, grid=(n_batch, steps),
        in_specs=[q_spec, k_spec, v_spec, _const_spec((N_HEADS // 2, wext, 2 * Q_BLOCK)),
                  _const_spec((N_HEADS // 2, 8, 2 * Q_BLOCK)), _const_spec((1, PAIR)), _const_spec((1, PAIR))]
        + [ANY] * len(plumb.args),
        out_specs=[pl.BlockSpec((STEP_ROWS, A_WIDTH), lambda b, i: (b * steps + i, 0)),
                   pl.BlockSpec((Q_BLOCKS_PER_STEP, N_HEADS, Q_BLOCK), lambda b, i: (b * steps + i, 0, 0))]
        + [ANY] * len(plumb.out_shape),
        out_shape=[_sds((t, A_WIDTH), BF16), _sds((t // Q_BLOCK, N_HEADS, Q_BLOCK), F32)] + plumb.out_shape,
        scratch_shapes=[pltpu.VMEM((s, kw), BF16), pltpu.VMEM((N_HEADS, nkc, PAIR, KEY_CHUNK), BF16),
                        pltpu.VMEM((w, 2 * Q_BLOCK), F32), pltpu.VMEM((2 * w, Q_BLOCK), BF16)] + plumb.scratch,
        input_output_aliases=plumb.aliases,
        compiler_params=plumb.params(vmem_limit_bytes=VMEM_LIMIT_V7X),
    )(qkv, qkv, qkv, bias2, sink2, gq, gk, *plumb.args)
    return plumb.deliver(res)


def _attn_bwd(name, qkv, bias2, sink2, gq, gk, y, dy, lse, cfg, n_batch, want_dbias, comms=()):
    t = qkv.shape[0]
    s = t // n_batch
    steps = s // STEP_ROWS
    nkc = s // KEY_CHUNK
    prev, group, kw = cfg["prev"], cfg["group"], cfg["kw"]
    w = prev + Q_BLOCK
    n_chunks = w // KEY_CHUNK
    wext = bias2.shape[1]
    plumb = _CommPlumbing(comms, 10, 7, 9)
    n_all_in = 10 + len(plumb.args)
    n_all_out = 7 + len(plumb.out_shape)

    def body(*refs):
        q_ref, k_ref, v_ref, bias_ref, sink_ref, gq_ref, gk_ref, y_ref, dy_ref, lse_ref = refs[:10]
        dq_ref, dk_ref, dv_ref, db_ref, dsink_ref, dgq_ref, dgk_ref = refs[n_all_in:n_all_in + 7]
        kn_ref, knt_ref, dkn_ref, dvs_ref, s_ref, dp_ref, pb_ref, dsb_ref, dst_ref = \
            refs[n_all_in + n_all_out:n_all_in + n_all_out + 9]
        b = pl.program_id(0)
        step = pl.program_id(1)
        first = (b == 0) & (step == 0)
        comm_refs = (refs[:n_all_in], refs[n_all_in:n_all_in + n_all_out], refs[n_all_in + n_all_out:])
        plumb.start_at(*comm_refs, first)

        @pl.when(step == 0)
        def _():
            for jk in range(kw // PAIR):
                cols = pl.ds(jk * PAIR, PAIR)
                heads = [h for h in range(N_HEADS) if (h // group) // 2 == jk]
                for c in range(nkc):
                    rows = pl.ds(c * KEY_CHUNK, KEY_CHUNK)
                    kn, _, _ = _pair_norm(k_ref[rows, cols].astype(F32), gk_ref[...])
                    kn_ref[rows, cols] = kn.astype(BF16)
                    _place_transposed(kn, knt_ref, c, heads, group)
            dkn_ref[...] = jnp.zeros(dkn_ref.shape, F32)
            dvs_ref[...] = jnp.zeros(dvs_ref.shape, F32)

        @pl.when(first)
        def _():
            db_ref[...] = jnp.zeros(db_ref.shape, F32)
            dsink_ref[...] = jnp.zeros(dsink_ref.shape, F32)
            dgq_ref[...] = jnp.zeros(dgq_ref.shape, F32)
            dgk_ref[...] = jnp.zeros(dgk_ref.shape, F32)

        for sb in range(Q_BLOCKS_PER_STEP):
            qrows = pl.ds(sb * Q_BLOCK, Q_BLOCK)
            start, off = _attn_window(step * Q_BLOCKS_PER_STEP + sb, prev)
            c0 = start // KEY_CHUNK
            for jq in range(N_HEADS // 2):
                cols = pl.ds(jq * PAIR, PAIR)
                kcols = pl.ds((((2 * jq) // group) // 2) * PAIR, PAIR)
                qn, q_hat, q_rstd = _pair_norm(q_ref[qrows, cols].astype(F32), gq_ref[...])
                qs = _stack_pair(qn * ATTN_SCALE, jq, group)
                do_pair = dy_ref[qrows, cols].astype(F32)
                dos = _stack_pair(do_pair, jq, group)
                prod_t = (do_pair * y_ref[qrows, cols].astype(F32)).T
                lo = _sub_lo(prod_t.shape)
                delta2 = jnp.concatenate([jnp.sum(jnp.where(lo, prod_t, 0.0), axis=0, keepdims=True),
                                          jnp.sum(jnp.where(lo, 0.0, prod_t), axis=0, keepdims=True)], axis=1)
                lse2 = jnp.concatenate([lse_ref[sb, 2 * jq:2 * jq + 1, :], lse_ref[sb, 2 * jq + 1:2 * jq + 2, :]],
                                       axis=1)
                dsk = -jnp.exp(sink_ref[jq, 0:1, :] - lse2) * delta2
                dsink_ref[2 * jq:2 * jq + 1, :] += dsk[:, :Q_BLOCK]
                dsink_ref[2 * jq + 1:2 * jq + 2, :] += dsk[:, Q_BLOCK:]
                rows_w = pl.ds(start, w)
                s_ref[...] = _dot(kn_ref[rows_w, kcols], qs, "nt")
                dp_ref[...] = _dot(v_ref[rows_w, kcols], dos, "nt")
                for c in range(n_chunks):
                    r = pl.ds(c * KEY_CHUNK, KEY_CHUNK)
                    brows = pl.ds(off + c * KEY_CHUNK, KEY_CHUNK)
                    p = jnp.exp(s_ref[r, :] + bias_ref[jq, brows, :] - lse2)
                    ds = p * (dp_ref[r, :] - delta2)
                    if want_dbias:
                        db_ref[jq, brows, :] += ds
                    ds_b = ds.astype(BF16)
                    pb_ref[r, :] = p.astype(BF16)
                    dsb_ref[r, :] = ds_b
                    dst_ref[pl.ds(2 * c * KEY_CHUNK, KEY_CHUNK), :] = ds_b[:, :Q_BLOCK]
                    dst_ref[pl.ds((2 * c + 1) * KEY_CHUNK, KEY_CHUNK), :] = ds_b[:, Q_BLOCK:]
                dkn_ref[rows_w, kcols] += _dot(dsb_ref[...], qs, "nn")
                dvs_ref[rows_w, kcols] += _dot(pb_ref[...], dos, "nn")
                kl = jnp.concatenate([knt_ref[2 * jq + hq, c0 + c] for c in range(n_chunks) for hq in range(2)],
                                     axis=1)
                dqt = _dot(kl, dst_ref[...], "nn")
                dq_raw, dg = _pair_norm_bwd(dqt.T * ATTN_SCALE, q_hat, q_rstd, gq_ref[...])
                dq_ref[qrows, cols] = dq_raw.astype(BF16)
                dgq_ref[...] += dg

        @pl.when(step == steps - 1)
        def _():
            for jk in range(kw // PAIR):
                kcols = pl.ds(jk * PAIR, PAIR)
                _, k_hat, k_rstd = _pair_norm(k_ref[:, kcols].astype(F32), gk_ref[...])
                dk_raw, dg = _pair_norm_bwd(dkn_ref[:, kcols], k_hat, k_rstd, gk_ref[...])
                dk_ref[:, kcols] = dk_raw.astype(BF16)
                dgk_ref[...] += dg
            dv_ref[...] = dvs_ref[...].astype(BF16)

        plumb.finish_at(*comm_refs, (b == n_batch - 1) & (step == steps - 1))

    q_spec, k_spec, v_spec = _attn_specs(cfg, s, steps)
    row = pl.BlockSpec((STEP_ROWS, A_WIDTH), lambda b, i: (b * steps + i, 0))
    kv_out = pl.BlockSpec((s, kw), lambda b, i: (b, 0))
    pair_bias = _const_spec((N_HEADS // 2, wext, 2 * Q_BLOCK))
    res = pl.pallas_call(
        body, name=name, grid=(n_batch, steps),
        in_specs=[q_spec, k_spec, v_spec, pair_bias, _const_spec((N_HEADS // 2, 8, 2 * Q_BLOCK)),
                  _const_spec((1, PAIR)), _const_spec((1, PAIR)), row, row,
                  pl.BlockSpec((Q_BLOCKS_PER_STEP, N_HEADS, Q_BLOCK), lambda b, i: (b * steps + i, 0, 0))]
        + [ANY] * len(plumb.args),
        out_specs=[row, kv_out, kv_out, pair_bias, _const_spec((N_HEADS, 128)),
                   _const_spec((1, PAIR)), _const_spec((1, PAIR))] + [ANY] * len(plumb.out_shape),
        out_shape=[_sds((t, A_WIDTH), BF16), _sds((t, kw), BF16), _sds((t, kw), BF16),
                   _sds((N_HEADS // 2, wext, 2 * Q_BLOCK), F32), _sds((N_HEADS, 128), F32),
                   _sds((1, PAIR), F32), _sds((1, PAIR), F32)] + plumb.out_shape,
        scratch_shapes=[pltpu.VMEM((s, kw), BF16), pltpu.VMEM((N_HEADS, nkc, PAIR, KEY_CHUNK), BF16),
                        pltpu.VMEM((s, kw), F32), pltpu.VMEM((s, kw), F32),
                        pltpu.VMEM((w, 2 * Q_BLOCK), F32), pltpu.VMEM((w, 2 * Q_BLOCK), F32),
                        pltpu.VMEM((w, 2 * Q_BLOCK), BF16), pltpu.VMEM((w, 2 * Q_BLOCK), BF16),
                        pltpu.VMEM((2 * w, Q_BLOCK), BF16)] + plumb.scratch,
        input_output_aliases=plumb.aliases,
        compiler_params=plumb.params(vmem_limit_bytes=VMEM_LIMIT_V7X),
    )(qkv, qkv, qkv, bias2, sink2, gq, gk, y, dy, lse, *plumb.args)
    return plumb.deliver(res)


def _band_tables(prev_chunks):
    prev = prev_chunks * CHUNK
    wext = 2 * prev + Q_BLOCK
    jj = np.arange(wext)[:, None]
    ii = np.arange(Q_BLOCK)[None, :]
    dist = prev + ii - jj
    rel_chunk = (prev // CHUNK + ii // CHUNK) - jj // CHUNK
    allowed = (rel_chunk >= 0) & (rel_chunk <= prev_chunks)
    return dist, allowed


def _alibi_slopes():
    return np.array([2.0 ** (-8.0 * (h + 1) / N_HEADS) for h in range(N_HEADS)], dtype=np.float32)


def _diag_onehot(prev, wext):
    n_diag = wext + Q_BLOCK - 1
    idx = np.clip(prev + Q_BLOCK - 1 - np.arange(n_diag), -A_MAX_REL, A_MAX_REL) + A_MAX_REL
    onehot = np.zeros((n_diag, 2 * A_MAX_REL + 1), np.float32)
    onehot[np.arange(n_diag), idx] = 1.0
    return onehot


def _bias_a(rel_bias):
    prev = A_PREV_CHUNKS * CHUNK
    _, allowed = _band_tables(A_PREV_CHUNKS)
    wext = allowed.shape[0]
    n_diag = wext + Q_BLOCK - 1
    seq = jnp.dot(rel_bias, jnp.asarray(_diag_onehot(prev, wext).T), precision=lax.Precision.HIGHEST)
    seq = jnp.pad(seq, ((0, 0), (0, 1)))
    rows = jnp.broadcast_to(seq[:, None, :], (N_HEADS, Q_BLOCK, n_diag + 1)).reshape(N_HEADS, -1)
    skew = rows[:, :Q_BLOCK * n_diag].reshape(N_HEADS, Q_BLOCK, n_diag)
    tile = jnp.transpose(skew[:, :, Q_BLOCK - 1:Q_BLOCK - 1 + wext], (0, 2, 1))
    return jnp.where(jnp.asarray(allowed)[None], tile, NEG_INF)


def _bias_b():
    dist, allowed = _band_tables(B_PREV_CHUNKS)
    bias = -_alibi_slopes()[:, None, None] * np.abs(dist).astype(np.float32)[None]
    return jnp.asarray(np.where(allowed[None], bias, np.float32(NEG_INF)).astype(np.float32))


def _rel_bias_grad(db_t):
    prev = A_PREV_CHUNKS * CHUNK
    wext = db_t.shape[1]
    n_diag = wext + Q_BLOCK - 1
    wp = n_diag + Q_BLOCK - 1
    xp = jnp.pad(jnp.transpose(db_t, (0, 2, 1)), ((0, 0), (0, 0), (Q_BLOCK - 1, Q_BLOCK - 1)))
    flat = jnp.pad(xp.reshape(N_HEADS, Q_BLOCK * wp), ((0, 0), (0, Q_BLOCK)))
    skew = flat.reshape(N_HEADS, Q_BLOCK, wp + 1)[:, :, :n_diag]
    diag = jnp.sum(skew, axis=1)
    return jnp.dot(diag, jnp.asarray(_diag_onehot(prev, wext)), precision=lax.Precision.HIGHEST)


def _ew(name, fn, ins, out_dtypes):
    r, c = ins[0].shape
    rb = _pick(r, max(16, (1 << 19) // c), 16)
    spec = pl.BlockSpec((rb, c), lambda i: (i, 0))

    def body(*refs):
        vals = fn(*[ref[...] for ref in refs[:len(ins)]])
        for ref, val in zip(refs[len(ins):], vals):
            ref[...] = val.astype(ref.dtype)

    return pl.pallas_call(
        body, name=name, grid=(r // rb,), in_specs=[spec] * len(ins), out_specs=[spec] * len(out_dtypes),
        out_shape=[_sds((r, c), dt) for dt in out_dtypes], compiler_params=_params(),
    )(*ins)


def _cast_into_slot(name, w, chip):
    r, c = w.shape
    rb = _pick(r, max(16, (1 << 19) // c), 16)

    def body(chip_ref, w_ref, o_ref):
        o_ref[...] = w_ref[...].astype(BF16)

    return pl.pallas_call(
        body, name=name,
        grid_spec=pltpu.PrefetchScalarGridSpec(
            num_scalar_prefetch=1, grid=(r // rb,),
            in_specs=[pl.BlockSpec((rb, c), lambda i, chip_ref: (i, 0))],
            out_specs=pl.BlockSpec((None, rb, c), lambda i, chip_ref: (chip_ref[0], i, 0))),
        out_shape=_sds((N_CHIPS, r, c), BF16), compiler_params=_params(),
    )(chip, w)


def _adamw_math(w, g, m, v):
    m = ADAM_B1 * m + (1.0 - ADAM_B1) * g
    v = ADAM_B2 * v + (1.0 - ADAM_B2) * (g * g)
    m_hat = m / (1.0 - ADAM_B1 ** ADAM_STEP)
    v_hat = v / (1.0 - ADAM_B2 ** ADAM_STEP)
    delta = -ADAM_LR * (m_hat / (jnp.sqrt(v_hat) + ADAM_EPS) + ADAM_WD * w)
    return delta, m, v


def _adamw_terms(name, terms, w, m, v):
    r, c = w.shape
    hr = r // 2
    rb = _pick(hr, max(16, (1 << 19) // c), 16)
    nb = hr // rb

    def body(t_ref, w_ref, m_ref, v_ref, g_ref, d_ref, nm_ref, nv_ref):
        g = t_ref[0].astype(F32)
        for k in range(1, N_CHIPS):
            g = g + t_ref[k].astype(F32)
        delta, nm, nv = _adamw_math(w_ref[...], g, m_ref[...], v_ref[...])
        g_ref[...] = g
        d_ref[...] = delta
        nm_ref[...] = nm
        nv_ref[...] = nv

    spec = pl.BlockSpec((rb, c), lambda h, i: (h * nb + i, 0))
    return pl.pallas_call(
        body, name=name, grid=(2, nb),
        in_specs=[pl.BlockSpec((None, N_CHIPS, rb, c), lambda h, i: (h, 0, i, 0)), spec, spec, spec],
        out_specs=[spec] * 4, out_shape=[_sds((r, c), F32)] * 4, compiler_params=_params(),
    )(terms, w, m, v)


def _mesh_place():
    x, y, c = lax.axis_index("x"), lax.axis_index("y"), lax.axis_index("c")
    chips = [(x, 1 - y), (1 - x, y), (1 - x, 1 - y)]
    return x, y, c, chips


def _all_gather_weights(bufs):
    n = len(bufs)

    def body(*refs):
        outs = refs[n:2 * n]
        ici_send, ici_recv, d2d_send, d2d_recv = refs[2 * n:]
        x, y, c, chips = _mesh_place()
        me = 2 * x + y
        sibling = (x, y, 1 - c)
        barrier = pltpu.get_barrier_semaphore()
        for peer in [sibling] + [(tx, ty, c) for tx, ty in chips]:
            pl.semaphore_signal(barrier, inc=1, device_id=peer, device_id_type=MESH)
        pl.semaphore_wait(barrier, N_CHIPS)
        sent = []
        for wi in range(n):
            for k, (tx, ty) in enumerate(chips):
                own = outs[wi].at[me, c]
                cp = pltpu.make_async_remote_copy(
                    src_ref=own, dst_ref=own, send_sem=ici_send.at[wi * 3 + k], recv_sem=ici_recv.at[wi * 3 + k],
                    device_id=(tx, ty, c), device_id_type=MESH)
                cp.start()
                sent.append(cp)
        passed = []
        for wi in range(n):
            for k, (tx, ty) in enumerate(chips):
                slab = outs[wi].at[2 * tx + ty, c]
                pltpu.make_async_remote_copy(
                    src_ref=slab, dst_ref=slab, send_sem=ici_send.at[wi * 3 + k], recv_sem=ici_recv.at[wi * 3 + k],
                    device_id=(tx, ty, c), device_id_type=MESH).wait_recv()
                fw = pltpu.make_async_remote_copy(
                    src_ref=slab, dst_ref=slab, send_sem=d2d_send.at[wi * 3 + k], recv_sem=d2d_recv.at[wi * 3 + k],
                    device_id=sibling, device_id_type=MESH)
                fw.start()
                passed.append(fw)
        for wi in range(n):
            for k, (tx, ty) in enumerate(chips):
                slab = outs[wi].at[2 * tx + ty, 1 - c]
                pltpu.make_async_remote_copy(
                    src_ref=slab, dst_ref=slab, send_sem=d2d_send.at[wi * 3 + k], recv_sem=d2d_recv.at[wi * 3 + k],
                    device_id=sibling, device_id_type=MESH).wait_recv()
        for cp in sent + passed:
            cp.wait_send()

    return pl.pallas_call(
        body, name="all_gather_weights",
        in_specs=[ANY] * n, out_specs=[ANY] * n,
        out_shape=[_sds(g.shape, g.dtype) for g in bufs],
        scratch_shapes=[pltpu.SemaphoreType.DMA((3 * n,))] * 4,
        input_output_aliases={i: i for i in range(n)},
        compiler_params=pltpu.CompilerParams(collective_id=COLLECTIVE_IDS[("chips", "sibling")]),
    )(*bufs)


def _run_comms(name, comms):
    plumb = _CommPlumbing(comms, 0, 0, 0)
    n_in, n_out = len(plumb.args), len(plumb.out_shape)

    def body(*refs):
        parts = []
        i0, o0, s0 = 0, n_in, n_in + n_out
        for cm in plumb.comms:
            parts.append((refs[i0:i0 + len(cm.ins)], refs[o0:o0 + len(cm.outs)], refs[s0:s0 + len(cm.sems)]))
            i0 += len(cm.ins)
            o0 += len(cm.outs)
            s0 += len(cm.sems)
        plumb.handshake()
        for cm, part in zip(plumb.comms, parts):
            cm.start(*part)
        for cm, part in zip(plumb.comms, parts):
            cm.finish(*part)

    res = pl.pallas_call(
        body, name=name, in_specs=[ANY] * n_in, out_specs=[ANY] * n_out, out_shape=plumb.out_shape,
        scratch_shapes=plumb.scratch, input_output_aliases=plumb.aliases, compiler_params=plumb.params(),
    )(*plumb.args)
    plumb.deliver(res)


def _gather_ici(bufs):
    n = len(bufs)

    def copies(outs, sems):
        send_sem, recv_sem = sems
        x, y, c, chips = _mesh_place()
        me = 2 * x + y
        sends, recvs = [], []
        for wi in range(n):
            for k, (tx, ty) in enumerate(chips):
                sems_k = dict(send_sem=send_sem.at[wi * 3 + k], recv_sem=recv_sem.at[wi * 3 + k],
                              device_id=(tx, ty, c), device_id_type=MESH)
                own = outs[wi].at[me, c]
                sends.append(pltpu.make_async_remote_copy(src_ref=own, dst_ref=own, **sems_k))
                slab = outs[wi].at[2 * tx + ty, c]
                recvs.append(pltpu.make_async_remote_copy(src_ref=slab, dst_ref=slab, **sems_k))
        return sends, recvs

    def start(ins, outs, sems):
        for cp in copies(outs, sems)[0]:
            cp.start()

    def finish(ins, outs, sems):
        sends, recvs = copies(outs, sems)
        for cp in recvs:
            cp.wait_recv()
        for cp in sends:
            cp.wait_send()

    return _Comm("chips", bufs, [_sds(g.shape, g.dtype) for g in bufs], {i: i for i in range(n)},
                 [pltpu.SemaphoreType.DMA((3 * n,)), pltpu.SemaphoreType.DMA((3 * n,))], start, finish)


def _gather_d2d(gathered):
    n = len(gathered)

    def copies(outs, sems):
        send_sem, recv_sem = sems
        x, y, c, chips = _mesh_place()
        sends, recvs = [], []
        for wi in range(n):
            for k, (tx, ty) in enumerate(chips):
                sems_k = dict(send_sem=send_sem.at[wi * 3 + k], recv_sem=recv_sem.at[wi * 3 + k],
                              device_id=(x, y, 1 - c), device_id_type=MESH)
                mine = outs[wi].at[2 * tx + ty, c]
                theirs = outs[wi].at[2 * tx + ty, 1 - c]
                sends.append(pltpu.make_async_remote_copy(src_ref=mine, dst_ref=mine, **sems_k))
                recvs.append(pltpu.make_async_remote_copy(src_ref=theirs, dst_ref=theirs, **sems_k))
        return sends, recvs

    def start(ins, outs, sems):
        for cp in copies(outs, sems)[0]:
            cp.start()

    def finish(ins, outs, sems):
        sends, recvs = copies(outs, sems)
        for cp in recvs:
            cp.wait_recv()
        for cp in sends:
            cp.wait_send()

    return _Comm("sibling", gathered, [_sds(g.shape, g.dtype) for g in gathered], {i: i for i in range(n)},
                 [pltpu.SemaphoreType.DMA((3 * n,)), pltpu.SemaphoreType.DMA((3 * n,))], start, finish)


def _exchange_halves(grads):
    n = len(grads)

    def copies(ins, outs, sems):
        send_sem, recv_sem = sems
        x, y, c, _ = _mesh_place()
        return [pltpu.make_async_remote_copy(
            src_ref=ins[wi].at[t, 1 - c], dst_ref=outs[wi].at[t],
            send_sem=send_sem.at[wi * N_CHIPS + t], recv_sem=recv_sem.at[wi * N_CHIPS + t],
            device_id=(x, y, 1 - c), device_id_type=MESH) for wi in range(n) for t in range(N_CHIPS)]

    def start(ins, outs, sems):
        for cp in copies(ins, outs, sems):
            cp.start()

    def finish(ins, outs, sems):
        for cp in copies(ins, outs, sems):
            cp.wait()

    return _Comm("sibling", grads, [_sds((N_CHIPS,) + g.shape[2:], g.dtype) for g in grads], {},
                 [pltpu.SemaphoreType.DMA((N_CHIPS * n,)), pltpu.SemaphoreType.DMA((N_CHIPS * n,))], start, finish)


def _scatter_ici(sums):
    n = len(sums)

    def copies(ins, outs, sems):
        local_sem, send_sem, recv_sem = sems
        x, y, c, chips = _mesh_place()
        me = 2 * x + y
        local, sends, recvs = [], [], []
        for wi in range(n):
            local.append(pltpu.make_async_copy(ins[wi].at[me], outs[wi].at[c, 0], local_sem.at[wi]))
            for k, (tx, ty) in enumerate(chips):
                sems_k = dict(send_sem=send_sem.at[wi * 3 + k], recv_sem=recv_sem.at[wi * 3 + k],
                              device_id=(tx, ty, c), device_id_type=MESH)
                land = outs[wi].at[c, k + 1]
                sends.append(pltpu.make_async_remote_copy(src_ref=ins[wi].at[2 * tx + ty], dst_ref=land, **sems_k))
                recvs.append(pltpu.make_async_remote_copy(src_ref=land, dst_ref=land, **sems_k))
        return local, sends, recvs

    def start(ins, outs, sems):
        local, sends, _ = copies(ins, outs, sems)
        for cp in local + sends:
            cp.start()

    def finish(ins, outs, sems):
        local, sends, recvs = copies(ins, outs, sems)
        for cp in local:
            cp.wait()
        for cp in recvs:
            cp.wait_recv()
        for cp in sends:
            cp.wait_send()

    return _Comm("chips", sums, [_sds((2, N_CHIPS) + s.shape[1:], s.dtype) for s in sums], {},
                 [pltpu.SemaphoreType.DMA((n,)), pltpu.SemaphoreType.DMA((3 * n,)), pltpu.SemaphoreType.DMA((3 * n,))],
                 start, finish)


def _scatter_d2d(terms):
    n = len(terms)

    def copies(outs, sems):
        send_sem, recv_sem = sems
        x, y, c, _ = _mesh_place()
        sends, recvs = [], []
        for wi in range(n):
            sems_w = dict(send_sem=send_sem.at[wi], recv_sem=recv_sem.at[wi],
                          device_id=(x, y, 1 - c), device_id_type=MESH)
            sends.append(pltpu.make_async_remote_copy(src_ref=outs[wi].at[c], dst_ref=outs[wi].at[c], **sems_w))
            recvs.append(pltpu.make_async_remote_copy(src_ref=outs[wi].at[1 - c], dst_ref=outs[wi].at[1 - c], **sems_w))
        return sends, recvs

    def start(ins, outs, sems):
        for cp in copies(outs, sems)[0]:
            cp.start()

    def finish(ins, outs, sems):
        sends, recvs = copies(outs, sems)
        for cp in recvs:
            cp.wait_recv()
        for cp in sends:
            cp.wait_send()

    return _Comm("sibling", terms, [_sds(t.shape, t.dtype) for t in terms], {i: i for i in range(n)},
                 [pltpu.SemaphoreType.DMA((n,)), pltpu.SemaphoreType.DMA((n,))], start, finish)


def _chip_sum(name, grad, got, core):
    _, _, hr, c = grad.shape
    rb = _pick(hr, max(16, (1 << 19) // c), 16)

    def body(core_ref, a_ref, b_ref, o_ref):
        o_ref[...] = (a_ref[...].astype(F32) + b_ref[...].astype(F32)).astype(BF16)

    out_spec = pl.BlockSpec((None, rb, c), lambda t, i, core_ref: (t, i, 0))
    return pl.pallas_call(
        body, name=name,
        grid_spec=pltpu.PrefetchScalarGridSpec(
            num_scalar_prefetch=1, grid=(N_CHIPS, hr // rb),
            in_specs=[pl.BlockSpec((None, None, rb, c), lambda t, i, core_ref: (t, core_ref[0], i, 0)), out_spec],
            out_specs=out_spec),
        out_shape=_sds((N_CHIPS, hr, c), BF16), compiler_params=_params(),
    )(core, grad, got)


def _all_reduce_small(pack):
    r = pack.shape[0]

    def body(p_ref, o_ref, land_ref, send_sem, recv_sem):
        x, y, c, _ = _mesh_place()
        me = 4 * x + 2 * y + c
        flips = [(k >> 2 & 1, k >> 1 & 1, k & 1) for k in range(1, N_DEV)]

        def peer(fx, fy, fc):
            return (1 - x if fx else x, 1 - y if fy else y, 1 - c if fc else c)

        land_ref[me] = p_ref[...]
        sent = []
        for k, flip in enumerate(flips):
            cp = pltpu.make_async_remote_copy(
                src_ref=p_ref, dst_ref=land_ref.at[me], send_sem=send_sem.at[k], recv_sem=recv_sem.at[k],
                device_id=peer(*flip), device_id_type=MESH)
            cp.start()
            sent.append(cp)
        for k, flip in enumerate(flips):
            px, py, pc = peer(*flip)
            slot = land_ref.at[4 * px + 2 * py + pc]
            pltpu.make_async_remote_copy(
                src_ref=slot, dst_ref=slot, send_sem=send_sem.at[k], recv_sem=recv_sem.at[k],
                device_id=(px, py, pc), device_id_type=MESH).wait_recv()
        total = land_ref[0]
        for d in range(1, N_DEV):
            total = total + land_ref[d]
        o_ref[...] = total
        for cp in sent:
            cp.wait_send()

    vmem = pl.BlockSpec(memory_space=pltpu.VMEM)
    return pl.pallas_call(
        body, name="all_reduce_small", in_specs=[vmem], out_specs=vmem, out_shape=_sds((r, 128), F32),
        scratch_shapes=[pltpu.VMEM((N_DEV, r, 128), F32), pltpu.SemaphoreType.DMA((N_DEV - 1,)),
                        pltpu.SemaphoreType.DMA((N_DEV - 1,))],
    )(pack)


PACK_TILE = 8 * 128


def _pack(items):
    rows, i = [], 0
    while i < len(items):
        j = i
        while j < len(items) and items[j].size == items[i].size:
            j += 1
        group = jnp.stack([it.reshape(-1).astype(F32) for it in items[i:j]])
        rows.append(jnp.pad(group, ((0, 0), (0, -group.shape[1] % PACK_TILE))).reshape(-1, 128))
        i = j
    return jnp.concatenate(rows, axis=0)


def _unpack(pack, shapes):
    out, row = [], 0
    for shp in shapes:
        size = int(np.prod(shp))
        nrow = -(-size // PACK_TILE) * (PACK_TILE // 128)
        out.append(pack[row:row + nrow].reshape(-1)[:size].reshape(shp))
        row += nrow
    return out


BIG = ["ffn1_w_gu", "ffn1_w_down", "w_in", "w_gate", "w_proj_a", "w_proj_b", "w_out",
       "ffn2_w_gu", "ffn2_w_down", "w_ple_gate", "w_ple_proj"]
SMALL = ["ffn1_norm", "mix_norm", "ffn2_norm", "ple_norm", "a_q_norm", "a_k_norm", "b_q_norm", "b_k_norm",
         "a_rel_bias", "b_sinks"]
WEIGHTS = ["ffn1_norm", "ffn1_w_gu", "ffn1_w_down", "mix_norm", "w_in", "a_q_norm", "a_k_norm", "a_rel_bias",
           "b_q_norm", "b_k_norm", "b_sinks", "w_gate", "w_proj_a", "w_proj_b", "w_out", "ffn2_norm",
           "ffn2_w_gu", "ffn2_w_down", "ple_norm", "w_ple_gate", "w_ple_proj"]
ATTN_A = dict(prev=A_PREV_CHUNKS * CHUNK, group=1, kw=A_WIDTH, qblk=0, kblk=1, vblk=2)
ATTN_B = dict(prev=B_PREV_CHUNKS * CHUNK, group=N_HEADS // B_KV_HEADS, kw=B_KV_WIDTH, qblk=3,
              kblk=4 * A_WIDTH // B_KV_WIDTH, vblk=4 * A_WIDTH // B_KV_WIDTH + 1)


def _cast_epilogue(accs, extras, outs, ij):
    for acc, out in zip(accs, outs):
        out[...] = acc.astype(out.dtype)


GATHER_FIRST = ["ffn1_w_gu", "ffn1_w_down"]
ROW_SHARDED = ("ffn1_w_down", "ffn2_w_down", "w_out", "w_ple_gate")


def _slotted(name, grad):
    if name == "w_in":
        rows, cols = grad.shape
        grad = jnp.transpose(grad.reshape(rows, N_CHIPS, cols // N_CHIPS), (1, 0, 2))
    elif name in ROW_SHARDED:
        grad = grad.reshape(N_CHIPS, grad.shape[0] // N_CHIPS, grad.shape[1])
    return grad.reshape(N_CHIPS, 2, grad.shape[1] // 2, grad.shape[2])


def _local_step(xt, pt, tgt, n_batch, bufs, small, core):
    t, d = xt.shape
    tm = _pick(t, ROW_TILE, 8)
    tk = _pick(t, ROW_TILE, 8)
    nt = t // tm
    row = pl.BlockSpec((tm, d), lambda i, j, k: (i, 0))
    gs = bufs["w_gate"].shape[2]
    ps = bufs["w_proj_a"].shape[2]
    es = bufs["w_ple_proj"].shape[2]
    pdim = pt.shape[1]
    ncols = N_CHIPS * bufs["w_in"].shape[2]
    tin = ncols // 2
    assert 2 * gs == d and 4 * ps == d and 4 * es == d and tin % 128 == 0

    w = {}
    halves = {n: b.reshape(N_CHIPS, 2, b.shape[1] // 2, b.shape[2]) for n, b in bufs.items()}

    def publish(names, arrays):
        for name, g in zip(names, arrays):
            g = g.reshape(N_CHIPS, 2 * g.shape[2], g.shape[3])
            if name in ROW_SHARDED:
                g = g.reshape(N_CHIPS * g.shape[1], g.shape[2])
            elif name == "w_in":
                g = jnp.transpose(g, (1, 0, 2)).reshape(g.shape[1], N_CHIPS * g.shape[2])
            w[name] = g

    class GatherPipe:
        def __init__(self, names):
            self.names = names
            self.stage = None

        def ici(self):
            self.stage = _gather_ici(self.bufs())
            return self.stage

        def d2d(self):
            self.stage = _gather_d2d(self.bufs())
            return self.stage

        def bufs(self):
            return self.stage.results if self.stage is not None else [halves[n] for n in self.names]

        def publish(self):
            publish(self.names, self.stage.results)

    class GradPipe:
        def __init__(self, names):
            self.names = names

        def exchange(self, grads):
            self.grads = [_slotted(n, g) for n, g in zip(self.names, grads)]
            self.x = _exchange_halves(self.grads)
            return self.x

        def scatter(self):
            self.sums = [_chip_sum("chip_sum_" + n, g, got, core)
                         for n, g, got in zip(self.names, self.grads, self.x.results)]
            self.s = _scatter_ici(self.sums)
            return self.s

        def forward(self):
            self.f = _scatter_d2d(self.s.results)
            return self.f

        def terms(self):
            return dict(zip(self.names, self.f.results))

    publish(GATHER_FIRST, _all_gather_weights([halves[n] for n in GATHER_FIRST]))
    g_in, g_proj, g_ple = GatherPipe(["w_in", "w_gate"]), GatherPipe(["w_proj_a", "w_proj_b", "w_out"]), \
        GatherPipe(["w_ple_gate", "w_ple_proj"])
    g_down2, g_up2 = GatherPipe(["ffn2_w_down"]), GatherPipe(["ffn2_w_gu"])
    n1 = _rms_fwd("ffn1_norm", xt, small["ffn1_norm"])
    h1, un, ffn1_saved = _ffn_fwd("ffn1", xt, n1, w["ffn1_w_gu"], w["ffn1_w_down"], small["mix_norm"],
                                  {"up": lambda: [g_in.ici()], "down": lambda: [g_in.d2d(), g_proj.ici()]})
    g_in.publish()
    w_in, wgate = w["w_in"], w["w_gate"]
    (qkv,) = _mm(
        "qkv", "nn", (nt, 2, 1),
        [(un, row, w_in, pl.BlockSpec((d, tin), lambda i, j, k: (0, j)))], [],
        [(_sds((t, ncols), BF16), pl.BlockSpec((tm, tin), lambda i, j, k: (i, j)))], (tm, tin), _cast_epilogue,
        j_outer=True, comms=[g_proj.d2d(), g_ple.ici()])
    g_proj.publish()
    wpa, wpb, wout = w["w_proj_a"], w["w_proj_b"], w["w_out"]

    def gate_epilogue(accs, extras, outs, ij):
        outs[0][...] = jax.nn.sigmoid(accs[0]).astype(BF16)

    (gates,) = _mm(
        "gate", "nn", (nt, 4, 1),
        [(un, row, wgate, pl.BlockSpec((None, d, gs), lambda i, j, k: (j, 0, 0)))], [],
        [(_sds((2, t, d), BF16), pl.BlockSpec((None, tm, gs), lambda i, j, k: (j // 2, i, j % 2)))],
        (tm, gs), gate_epilogue, j_outer=True, chunked=True, comms=[g_ple.d2d(), g_down2.ici()])
    g_ple.publish()
    wpg, wpe = w["w_ple_gate"], w["w_ple_proj"]

    bias_a = _pair_bias(_bias_a(small["a_rel_bias"][0]))
    bias_b = _pair_bias(_bias_b())
    sink_a = _pair_rows(jnp.full((N_HEADS, 128), NEG_INF, F32))
    sink_b = _pair_rows(jnp.broadcast_to(small["b_sinks"][0][:, None], (N_HEADS, 128)))
    gqa, gka, gqb, gkb = [jnp.tile(small[k], (1, 2)) for k in ("a_q_norm", "a_k_norm", "b_q_norm", "b_k_norm")]
    ya, lse_a = _attn_fwd("attn_a_fwd", qkv, bias_a, sink_a, gqa, gka, ATTN_A, n_batch,
                          comms=[g_down2.d2d(), g_up2.ici()])
    g_down2.publish()
    yb, lse_b = _attn_fwd("attn_b_fwd", qkv, bias_b, sink_b, gqb, gkb, ATTN_B, n_batch, comms=[g_up2.d2d()])
    g_up2.publish()

    def merge_epilogue(accs, extras, outs, ij):
        pa, pb = accs
        outs[0][...] = (extras[0][...].astype(F32) * pa + extras[1][...].astype(F32) * pb).astype(BF16)
        outs[1][...] = pa.astype(BF16)
        outs[2][...] = pb.astype(BF16)

    y_spec = pl.BlockSpec((tm, A_WIDTH), lambda i, j, k: (i, 0))
    proj_spec = pl.BlockSpec((None, A_WIDTH, ps), lambda i, j, k: (j, 0, 0))
    tile_ps = pl.BlockSpec((tm, ps), lambda i, j, k: (i, j))
    merged, pa, pb = _mm(
        "proj_merge", "nn", (nt, 4, 1),
        [(ya, y_spec, wpa, proj_spec), (yb, y_spec, wpb, proj_spec)],
        [(gates, pl.BlockSpec((None, tm, ps), lambda i, j, k: (0, i, j))),
         (gates, pl.BlockSpec((None, tm, ps), lambda i, j, k: (1, i, j)))],
        [(_sds((t, d), BF16), tile_ps)] * 3, (tm, ps), merge_epilogue)

    h2, n2 = _mm(
        "out_proj", "nn", (nt, 1, 1),
        [(merged, row, wout, pl.BlockSpec((d, d), lambda i, j, k: (0, 0)))],
        [(h1, row), (small["ffn2_norm"], pl.BlockSpec((1, d), lambda i, j, k: (0, 0)))],
        [(_sds((t, d), F32), row), (_sds((t, d), BF16), row)], (tm, d), _residual_norm_epilogue(1.0))

    h3, n3, ffn2_saved = _ffn_fwd("ffn2", h2, n2, w["ffn2_w_gu"], w["ffn2_w_down"], small["ple_norm"], {})
    tile_es = pl.BlockSpec((tm, es), lambda i, j, k: (i, j))
    th = _pick(d, 512)

    def head_epilogue(accs, extras, outs, ij):
        h3_ref, tgt_ref = extras
        dy_ref, dpe_ref, dz_ref, loss_ref = outs
        pg = jax.nn.sigmoid(accs[0])
        pev = accs[1]
        diff = h3_ref[...] + pg * pev - tgt_ref[...]
        dy = diff * (1.0 / d)
        dy_ref[...] = dy
        dpe_ref[...] = (dy * pg).astype(BF16)
        dz_ref[...] = (dy * pev * pg * (1.0 - pg)).astype(BF16)
        _accumulate(loss_ref, jnp.full(loss_ref.shape, jnp.sum(diff * diff), F32), (ij[0] == 0) & (ij[1] == 0))

    tile_h = pl.BlockSpec((tm, th), lambda i, j, k: (i, j))
    dy, dpe, dz, loss_acc = _mm(
        "ple_gate_loss", "nn", (nt, 4, 1),
        [(n3, row, wpg, pl.BlockSpec((d, es), lambda i, j, k: (0, j))),
         (pt, pl.BlockSpec((tm, pdim), lambda i, j, k: (i, 0)), wpe, pl.BlockSpec((None, pdim, es), lambda i, j, k: (j, 0, 0)))],
        [(h3, tile_es), (tgt, tile_es)],
        [(_sds((t, d), F32), tile_es), (_sds((t, d), BF16), tile_es), (_sds((t, d), BF16), tile_es),
         (_sds((8, 128), F32), pl.BlockSpec((8, 128), lambda i, j, k: (0, 0)))],
        (tm, es), head_epilogue, j_outer=True, chunked=True)
    loss = 0.5 * loss_acc[0, 0] / d

    nk = t // tk
    (dwpe,) = _mm(
        "d_w_ple_proj", "tn", (1, 4, nk),
        [(pt, pl.BlockSpec((tk, pdim), lambda i, j, k: (k, 0)), dpe, pl.BlockSpec((tk, es), lambda i, j, k: (k, j)))],
        [], [(_sds((4, pdim, es), BF16), pl.BlockSpec((None, pdim, es), lambda i, j, k: (j, 0, 0)))],
        (pdim, es), _cast_epilogue)

    def dense_grad(name, a, dyb, comms=()):
        (res,) = _mm(
            name, "tn", (1, d // th, nk),
            [(a, pl.BlockSpec((tk, d), lambda i, j, k: (k, 0)), dyb, pl.BlockSpec((tk, th), lambda i, j, k: (k, j)))],
            [], [(_sds((d, d), BF16), pl.BlockSpec((d, th), lambda i, j, k: (0, j)))], (d, th), _cast_epilogue,
            comms=comms)
        return res

    dwpg = dense_grad("d_w_ple_gate", n3, dz)
    tmn = _pick(t, ROW_TILE, 8)
    extras, outs = _rms_bwd_io(h3, small["ple_norm"], dy, tmn)
    dh3, dh3_b, d_ple_norm = _mm(
        "d_ple_norm", "nt", (t // tmn, 1, 1),
        [(dz, pl.BlockSpec((tmn, d), lambda i, j, k: (i, 0)), wpg, pl.BlockSpec((d, d), lambda i, j, k: (0, 0)))],
        extras, outs, (tmn, d), _rms_bwd_epilogue)

    up2, down2, ple = GradPipe(["ffn2_w_gu"]), GradPipe(["ffn2_w_down"]), GradPipe(["w_ple_gate", "w_ple_proj"])
    proj = GradPipe(["w_proj_a", "w_proj_b", "w_out"])
    dh2, dh2_b, d_ffn2_norm, dwgu2, dwd2 = _ffn_bwd(
        "ffn2", dh3, dh3_b, h2, small["ffn2_norm"], w["ffn2_w_gu"], w["ffn2_w_down"], ffn2_saved,
        {"dnorm": lambda dwgu, dwd: [up2.exchange([dwgu]), down2.exchange([dwd]), ple.exchange([dwpg, dwpe])]})

    def dmerge_epilogue(accs, extras, outs, ij):
        dmo = accs[0]
        g_ref, pa_ref, pb_ref = extras
        dg_ref, dpa_ref, dpb_ref = outs
        ga = g_ref[0].astype(F32)
        gb = g_ref[1].astype(F32)
        dg_ref[0] = (dmo * pa_ref[...].astype(F32) * ga * (1.0 - ga)).astype(BF16)
        dg_ref[1] = (dmo * pb_ref[...].astype(F32) * gb * (1.0 - gb)).astype(BF16)
        dpa_ref[...] = (dmo * ga).astype(BF16)
        dpb_ref[...] = (dmo * gb).astype(BF16)

    g_spec = pl.BlockSpec((2, tm, th), lambda i, j, k: (0, i, j))
    dgates, dpa, dpb = _mm(
        "d_merge", "nt", (nt, d // th, 1),
        [(dh2_b, row, wout, pl.BlockSpec((th, d), lambda i, j, k: (j, 0)))],
        [(gates, g_spec), (pa, tile_h), (pb, tile_h)],
        [(_sds((2, t, d), BF16), g_spec), (_sds((t, d), BF16), tile_h), (_sds((t, d), BF16), tile_h)],
        (tm, th), dmerge_epilogue, j_outer=True, chunked=True, comms=[down2.scatter()])
    dwout = dense_grad("d_w_out", merged, dh2_b, comms=[down2.forward(), ple.scatter()])

    yk_spec = pl.BlockSpec((tk, A_WIDTH), lambda i, j, k: (k, 0))
    dk_spec = pl.BlockSpec((tk, ps), lambda i, j, k: (k, j))
    dproj = (_sds((4, A_WIDTH, ps), BF16), proj_spec)
    dwpa, dwpb = _mm(
        "d_w_proj", "tn", (1, 4, nk),
        [(ya, yk_spec, dpa, dk_spec), (yb, yk_spec, dpb, dk_spec)], [], [dproj, dproj], (A_WIDTH, ps), _cast_epilogue,
        comms=[ple.forward()])
    dproj_a = pl.BlockSpec((tm, ps), lambda i, j, k: (i, k))
    wproj_k = pl.BlockSpec((None, A_WIDTH, ps), lambda i, j, k: (k, 0, 0))
    dya, dyb = _mm(
        "d_attn_out", "nt", (nt, 1, 4),
        [(dpa, dproj_a, wpa, wproj_k), (dpb, dproj_a, wpb, wproj_k)], [],
        [(_sds((t, A_WIDTH), BF16), y_spec)] * 2, (tm, A_WIDTH), _cast_epilogue,
        comms=[proj.exchange([dwpa, dwpb, dwout])])

    dqa, dka, dva, dbias_a, _, dgqa, dgka = _attn_bwd(
        "attn_a_bwd", qkv, bias_a, sink_a, gqa, gka, ya, dya, lse_a, ATTN_A, n_batch, True,
        comms=[up2.scatter(), proj.scatter()])
    dqb, dkb, dvb, _, dsink_b, dgqb, dgkb = _attn_bwd(
        "attn_b_bwd", qkv, bias_b, sink_b, gqb, gkb, yb, dyb, lse_b, ATTN_B, n_batch, False,
        comms=[up2.forward(), proj.forward()])
    dqkv = jnp.concatenate([dqa, dka, dva, dqb, dkb, dvb], axis=1)

    (dwgate,) = _mm(
        "d_w_gate", "tn", (1, 4, nk),
        [(un, pl.BlockSpec((tk, d), lambda i, j, k: (k, 0)),
          dgates, pl.BlockSpec((None, tk, gs), lambda i, j, k: (j // 2, k, j % 2)))],
        [], [(_sds((4, d, gs), BF16), pl.BlockSpec((None, d, gs), lambda i, j, k: (j, 0, 0)))], (d, gs), _cast_epilogue)
    (dwin,) = _mm(
        "d_w_in", "tn", (1, 2, nk),
        [(un, pl.BlockSpec((tk, d), lambda i, j, k: (k, 0)), dqkv, pl.BlockSpec((tk, tin), lambda i, j, k: (k, j)))],
        [], [(_sds((d, ncols), BF16), pl.BlockSpec((d, tin), lambda i, j, k: (0, j)))], (d, tin), _cast_epilogue)

    mixer = GradPipe(["w_in", "w_gate"])
    extras, outs = _rms_bwd_io(h1, small["mix_norm"], dh2, tmn)
    dh1, dh1_b, d_mix_norm = _mm(
        "d_mix_norm", "nt", (t // tmn, 1, 6),
        [(dgates, pl.BlockSpec((None, tmn, gs), lambda i, j, k: (jnp.minimum(k, 3) // 2, i, jnp.minimum(k, 3) % 2)),
          wgate, pl.BlockSpec((None, d, gs), lambda i, j, k: (jnp.minimum(k, 3), 0, 0))),
         (dqkv, pl.BlockSpec((tmn, tin), lambda i, j, k: (i, jnp.maximum(k - 4, 0))),
          w_in, pl.BlockSpec((d, tin), lambda i, j, k: (0, jnp.maximum(k - 4, 0))))],
        extras, outs, (tmn, d), _rms_bwd_epilogue, steps=[4, 2],
        comms=[mixer.exchange([dwin, dwgate])])

    up1 = GradPipe(["ffn1_w_gu"])
    down1 = GradPipe(["ffn1_w_down"])
    dx, _, d_ffn1_norm, _, _ = _ffn_bwd(
        "ffn1", dh1, dh1_b, xt, small["ffn1_norm"], w["ffn1_w_gu"], w["ffn1_w_down"], ffn1_saved,
        {"dwgu": lambda: [mixer.scatter()],
         "dwd": lambda dwgu: [mixer.forward(), up1.exchange([dwgu])],
         "dnorm": lambda dwgu, dwd: [up1.scatter(), down1.exchange([dwd])]})
    _run_comms("grad_tail_scatter", [up1.forward(), down1.scatter()])
    _run_comms("grad_tail_forward", [down1.forward()])
    terms = {}
    for pipe in (up2, down2, ple, proj, mixer, up1, down1):
        terms.update(pipe.terms())

    def fold(v):
        return v[0, :HEAD_DIM] + v[0, HEAD_DIM:]

    small_grads = {"ffn1_norm": d_ffn1_norm, "mix_norm": d_mix_norm, "ffn2_norm": d_ffn2_norm,
                   "ple_norm": d_ple_norm, "a_q_norm": fold(dgqa), "a_k_norm": fold(dgka),
                   "b_q_norm": fold(dgqb), "b_k_norm": fold(dgkb), "a_rel_bias": _rel_bias_grad(_unpair_bias(dbias_a)),
                   "b_sinks": jnp.sum(dsink_b, axis=1)}
    return loss, dx, terms, small_grads


def kernel(x, p, ffn1_norm, ffn1_w_gu, ffn1_w_down, mix_norm, w_in, a_q_norm, a_k_norm, a_rel_bias, b_q_norm, b_k_norm, b_sinks, w_gate, w_proj_a, w_proj_b, w_out, ffn2_norm, ffn2_w_gu, ffn2_w_down, ple_norm, w_ple_gate, w_ple_proj, loss_target, m_ffn1_norm, m_ffn1_w_gu, m_ffn1_w_down, m_mix_norm, m_w_in, m_a_q_norm, m_a_k_norm, m_a_rel_bias, m_b_q_norm, m_b_k_norm, m_b_sinks, m_w_gate, m_w_proj_a, m_w_proj_b, m_w_out, m_ffn2_norm, m_ffn2_w_gu, m_ffn2_w_down, m_ple_norm, m_w_ple_gate, m_w_ple_proj, v_ffn1_norm, v_ffn1_w_gu, v_ffn1_w_down, v_mix_norm, v_w_in, v_a_q_norm, v_a_k_norm, v_a_rel_bias, v_b_q_norm, v_b_k_norm, v_b_sinks, v_w_gate, v_w_proj_a, v_w_proj_b, v_w_out, v_ffn2_norm, v_ffn2_w_gu, v_ffn2_w_down, v_ple_norm, v_w_ple_gate, v_w_ple_proj):
    given = dict(locals())
    n_batch, s, d = x.shape
    t = n_batch * s
    xt = x.reshape(t, d)
    pt = p.reshape(t, p.shape[-1])
    tgt = loss_target.reshape(t, d)

    chip = (2 * lax.axis_index("x") + lax.axis_index("y")).astype(jnp.int32).reshape(1)
    bufs = {name: _cast_into_slot("cast_" + name, given[name][0], chip) for name in BIG}
    small = {name: given[name] for name in SMALL}
    core = lax.axis_index("c").astype(jnp.int32).reshape(1)
    loss, dx, terms, small_grads = _local_step(xt, pt, tgt, n_batch, bufs, small, core)

    grads, deltas, new_m, new_v = {}, {}, {}, {}
    for name in BIG:
        gw, dl, nm, nv = _adamw_terms("adamw_" + name, terms[name], given[name][0], given["m_" + name][0],
                                      given["v_" + name][0])
        grads[name], deltas[name], new_m[name], new_v[name] = gw[None], dl[None], nm[None], nv[None]

    small_shapes = [given[name].shape for name in SMALL] + [()]
    g_pack = _all_reduce_small(_pack([small_grads[name] for name in SMALL] + [loss]))
    zero = jnp.zeros((), F32)
    w_pack = _pack([given[name] for name in SMALL] + [zero])
    m_pack = _pack([given["m_" + name] for name in SMALL] + [zero])
    v_pack = _pack([given["v_" + name] for name in SMALL] + [zero])
    d_pack, nm_pack, nv_pack = _ew("adamw_small", lambda wv, gv, mv, vv: _adamw_math(wv, gv, mv, vv),
                                   [w_pack, g_pack, m_pack, v_pack], [F32] * 3)
    g_small = _unpack(g_pack, small_shapes)
    loss_total = g_small[-1]
    for name, gv, dv, mv, vv in zip(SMALL, g_small, _unpack(d_pack, small_shapes), _unpack(nm_pack, small_shapes),
                                    _unpack(nv_pack, small_shapes)):
        grads[name], deltas[name], new_m[name], new_v[name] = gv, dv, mv, vv

    return (loss_total, dx.reshape(x.shape), *[grads[n] for n in WEIGHTS], *[deltas[n] for n in WEIGHTS],
            *[new_m[n] for n in WEIGHTS], *[new_v[n] for n in WEIGHTS])
```

```python
import functools

import numpy as np
import jax
import jax.numpy as jnp
from jax import lax
from jax.experimental import pallas as pl
from jax.experimental.pallas import tpu as pltpu

F32 = jnp.float32
BF16 = jnp.bfloat16

CHUNK = 64
HEAD_DIM = 64
A_PREV_CHUNKS = 8
A_MAX_REL = 128
N_HEADS = 8
B_KV_HEADS = 2
B_PREV_CHUNKS = 2
A_WIDTH = N_HEADS * HEAD_DIM
B_KV_WIDTH = B_KV_HEADS * HEAD_DIM
EPS = 1e-6
NEG_INF = -1e30
ATTN_SCALE = HEAD_DIM ** -0.5
Q_BLOCK = 128
PAIR = 2 * HEAD_DIM

ADAM_LR = 0.001
ADAM_B1 = 0.9
ADAM_B2 = 0.999
ADAM_EPS = 1e-08
ADAM_WD = 0.01
ADAM_STEP = 10

N_CHIPS = 4
N_DEV = 8
VMEM_LIMIT_V7X = 56 * 1024 * 1024
ROW_TILE = 1024
MESH = pl.DeviceIdType.MESH
COLLECTIVE_IDS = {("sibling",): 1, ("chips",): 2, ("chips", "sibling"): 3}
ANY = pl.BlockSpec(memory_space=pl.ANY)

_DN = {
    "nn": (((1,), (0,)), ((), ())),
    "nt": (((1,), (1,)), ((), ())),
    "tn": (((0,), (0,)), ((), ())),
}


def _pick(n, target, mult=128):
    best = None
    for d in range(mult, min(n, target) + 1, mult):
        if n % d == 0:
            best = d
    return n if best is None else best


def _dot(a, b, mode):
    return lax.dot_general(a.astype(BF16), b.astype(BF16), _DN[mode], preferred_element_type=F32)


def _params():
    return pltpu.CompilerParams(vmem_limit_bytes=VMEM_LIMIT_V7X)


class _Comm:
    def __init__(self, peers, ins, outs, aliases, sems, start, finish):
        self.peers = peers
        self.ins, self.outs, self.aliases, self.sems = list(ins), list(outs), dict(aliases), list(sems)
        self.start, self.finish = start, finish
        self.results = None


class _CommPlumbing:
    def __init__(self, comms, n_in, n_out, n_scratch):
        self.comms = list(comms)
        self.n_in, self.n_out, self.n_scratch = n_in, n_out, n_scratch
        self.args = [a for cm in self.comms for a in cm.ins]
        self.out_shape = [o for cm in self.comms for o in cm.outs]
        self.scratch = [s for cm in self.comms for s in cm.sems]
        self.aliases = {}
        i0, o0 = n_in, n_out
        for cm in self.comms:
            for a, b in cm.aliases.items():
                self.aliases[i0 + a] = o0 + b
            i0 += len(cm.ins)
            o0 += len(cm.outs)

    def _parts(self, in_refs, out_refs, scratch_refs):
        parts = []
        i0, o0, s0 = self.n_in, self.n_out, self.n_scratch
        for cm in self.comms:
            parts.append((in_refs[i0:i0 + len(cm.ins)], out_refs[o0:o0 + len(cm.outs)],
                          scratch_refs[s0:s0 + len(cm.sems)]))
            i0 += len(cm.ins)
            o0 += len(cm.outs)
            s0 += len(cm.sems)
        return parts

    def kinds(self):
        return sorted(set(cm.peers for cm in self.comms))

    def params(self, **kwargs):
        if self.comms:
            kwargs["collective_id"] = COLLECTIVE_IDS[tuple(self.kinds())]
        return pltpu.CompilerParams(**kwargs)

    def handshake(self):
        x, y, c, chips = _mesh_place()
        peers = []
        if "sibling" in self.kinds():
            peers.append((x, y, 1 - c))
        if "chips" in self.kinds():
            peers += [(tx, ty, c) for tx, ty in chips]
        barrier = pltpu.get_barrier_semaphore()
        for peer in peers:
            pl.semaphore_signal(barrier, inc=1, device_id=peer, device_id_type=MESH)
        pl.semaphore_wait(barrier, len(peers))

    def start_at(self, in_refs, out_refs, scratch_refs, first):
        if self.comms:
            parts = self._parts(in_refs, out_refs, scratch_refs)

            @pl.when(first)
            def _():
                self.handshake()
                for cm, part in zip(self.comms, parts):
                    cm.start(*part)

    def finish_at(self, in_refs, out_refs, scratch_refs, last):
        if self.comms:
            parts = self._parts(in_refs, out_refs, scratch_refs)

            @pl.when(last)
            def _():
                for cm, part in zip(self.comms, parts):
                    cm.finish(*part)

    def deliver(self, results):
        o0 = self.n_out
        for cm in self.comms:
            cm.results = list(results[o0:o0 + len(cm.outs)])
            o0 += len(cm.outs)
        return list(results[:self.n_out])


def _swap_ij(spec):
    index_map = spec.index_map
    return pl.BlockSpec(spec.block_shape, lambda j, i, k: index_map(i, j, k))


MXU_COLUMNS_V7X = 256


def _mm(name, mode, grid, pairs, extras, outs, acc_shape, epilogue, steps=None, comms=(), j_outer=False,
        chunked=False):
    ni, nj, nk = grid
    n_in = 2 * len(pairs) + len(extras)
    n_out = len(outs)
    tn = acc_shape[1]
    col_chunks = None
    if chunked:
        assert nk == 1 and steps is None and mode in ("nn", "nt")
        col_chunks = [(c0, min(MXU_COLUMNS_V7X, tn - c0)) for c0 in range(0, tn, MXU_COLUMNS_V7X)]
    n_acc = 0 if chunked else (len(pairs) if steps is None else 1)
    plumb = _CommPlumbing(comms, n_in, n_out, n_acc)
    n_all_in = n_in + len(plumb.args)
    n_all_out = n_out + len(plumb.out_shape)
    if j_outer:
        grid = (nj, ni, nk)
        pairs = [(a, _swap_ij(a_spec), b, _swap_ij(b_spec)) for a, a_spec, b, b_spec in pairs]
        extras = [(e, _swap_ij(e_spec)) for e, e_spec in extras]
        outs = [(o, _swap_ij(o_spec)) for o, o_spec in outs]

    def body(*refs):
        in_refs = refs[:n_all_in]
        out_refs = refs[n_all_in:n_all_in + n_all_out]
        scratch = refs[n_all_in + n_all_out:]
        accs = scratch[:n_acc]
        i = pl.program_id(1 if j_outer else 0)
        j = pl.program_id(0 if j_outer else 1)
        k = pl.program_id(2)
        plumb.start_at(in_refs, out_refs, scratch, (i == 0) & (j == 0) & (k == 0))

        def contrib(p, acc):
            acc[...] += _dot(in_refs[2 * p][...], in_refs[2 * p + 1][...], mode)

        if col_chunks:
            def cols(ref, c0, cs):
                if ref.shape[-1] != tn:
                    return ref
                return ref.at[(slice(None),) * (len(ref.shape) - 1) + (pl.ds(c0, cs),)]

            lhs = [in_refs[2 * p][...] for p in range(len(pairs))]
            for ci, (c0, cs) in enumerate(col_chunks):
                vals = []
                for p in range(len(pairs)):
                    b_ref = in_refs[2 * p + 1]
                    rhs = b_ref[:, c0:c0 + cs] if mode == "nn" else b_ref[c0:c0 + cs, :]
                    vals.append(_dot(lhs[p], rhs, mode))
                epilogue(vals, [cols(r, c0, cs) for r in in_refs[2 * len(pairs):n_in]],
                         [cols(r, c0, cs) for r in out_refs[:n_out]], (i, j * len(col_chunks) + ci))
        else:
            @pl.when(k == 0)
            def _():
                for acc in accs:
                    acc[...] = jnp.zeros(acc.shape, F32)

            if steps is None:
                for p in range(len(pairs)):
                    contrib(p, accs[p])
            else:
                lo = 0
                for p, n in enumerate(steps):
                    pl.when((k >= lo) & (k < lo + n))(functools.partial(contrib, p, accs[0]))
                    lo += n

            @pl.when(k == nk - 1)
            def _():
                epilogue([acc[...] for acc in accs], in_refs[2 * len(pairs):n_in], out_refs[:n_out], (i, j))

        plumb.finish_at(in_refs, out_refs, scratch, (i == ni - 1) & (j == nj - 1) & (k == nk - 1))

    args, in_specs = [], []
    for a, a_spec, b, b_spec in pairs:
        args += [a, b]
        in_specs += [a_spec, b_spec]
    for e, e_spec in extras:
        args.append(e)
        in_specs.append(e_spec)
    res = pl.pallas_call(
        body,
        name=name,
        grid=grid,
        in_specs=in_specs + [ANY] * len(plumb.args),
        out_specs=[s for _, s in outs] + [ANY] * len(plumb.out_shape),
        out_shape=[o for o, _ in outs] + plumb.out_shape,
        scratch_shapes=[pltpu.VMEM(acc_shape, F32) for _ in range(n_acc)] + plumb.scratch,
        input_output_aliases=plumb.aliases,
        compiler_params=plumb.params(vmem_limit_bytes=VMEM_LIMIT_V7X),
    )(*args, *plumb.args)
    return plumb.deliver(res)


def _sds(shape, dtype):
    return jax.ShapeDtypeStruct(shape, dtype)


def _accumulate(ref, value, first):
    @pl.when(first)
    def _():
        ref[...] = value

    @pl.when(jnp.logical_not(first))
    def _():
        ref[...] += value


def _rms_fwd(name, x, gain, comms=()):
    t, d = x.shape
    tm = _pick(t, ROW_TILE, 8)
    steps = t // tm
    plumb = _CommPlumbing(comms, 2, 1, 0)
    n_all_in = 2 + len(plumb.args)
    n_all_out = 1 + len(plumb.out_shape)

    def body(*refs):
        x_ref, g_ref = refs[:2]
        y_ref = refs[n_all_in]
        comm_refs = (refs[:n_all_in], refs[n_all_in:n_all_in + n_all_out], refs[n_all_in + n_all_out:])
        i = pl.program_id(0)
        plumb.start_at(*comm_refs, i == 0)
        xv = x_ref[...]
        rstd = lax.rsqrt(jnp.mean(xv * xv, axis=-1, keepdims=True) + EPS)
        y_ref[...] = (xv * rstd * g_ref[...]).astype(BF16)
        plumb.finish_at(*comm_refs, i == steps - 1)

    res = pl.pallas_call(
        body, name=name, grid=(steps,),
        in_specs=[pl.BlockSpec((tm, d), lambda i: (i, 0)), pl.BlockSpec((1, d), lambda i: (0, 0))]
        + [ANY] * len(plumb.args),
        out_specs=[pl.BlockSpec((tm, d), lambda i: (i, 0))] + [ANY] * len(plumb.out_shape),
        out_shape=[_sds((t, d), BF16)] + plumb.out_shape,
        scratch_shapes=plumb.scratch,
        input_output_aliases=plumb.aliases,
        compiler_params=plumb.params(vmem_limit_bytes=VMEM_LIMIT_V7X),
    )(x, gain, *plumb.args)
    return plumb.deliver(res)[0]


def _rms_bwd_epilogue(accs, extras, outs, ij):
    x_ref, g_ref, r_ref = extras
    dh_ref, dhb_ref, dg_ref = outs
    dn = accs[0]
    xv = x_ref[...]
    rstd = lax.rsqrt(jnp.mean(xv * xv, axis=-1, keepdims=True) + EPS)
    xhat = xv * rstd
    gd = dn * g_ref[...]
    dx = rstd * (gd - xhat * jnp.mean(gd * xhat, axis=-1, keepdims=True))
    dh = r_ref[...] + dx
    dh_ref[...] = dh
    dhb_ref[...] = dh.astype(BF16)
    _accumulate(dg_ref, jnp.sum(dn * xhat, axis=0, keepdims=True), ij[0] == 0)


def _rms_bwd_io(x, gain, dres, tm):
    t, d = x.shape
    row = pl.BlockSpec((tm, d), lambda i, j, k: (i, 0))
    extras = [(x, row), (gain, pl.BlockSpec((1, d), lambda i, j, k: (0, 0))), (dres, row)]
    outs = [(_sds((t, d), F32), row), (_sds((t, d), BF16), row),
            (_sds((1, d), F32), pl.BlockSpec((1, d), lambda i, j, k: (0, 0)))]
    return extras, outs


def _residual_norm_epilogue(scale):
    def epilogue(accs, extras, outs, ij):
        hv = extras[0][...] + scale * accs[0]
        outs[0][...] = hv
        rstd = lax.rsqrt(jnp.mean(hv * hv, axis=-1, keepdims=True) + EPS)
        outs[1][...] = (hv * rstd * extras[1][...]).astype(BF16)
    return epilogue


def _ffn_fwd(tag, h, n, wgu, wd, next_gain, hooks):
    t, d = h.shape
    fs = wgu.shape[2]
    f = 2 * fs
    tm = _pick(t, ROW_TILE, 8)

    def up_epilogue(accs, extras, outs, ij):
        g, u = accs
        gu_ref, a_ref = outs
        gu_ref[0] = g.astype(BF16)
        gu_ref[1] = u.astype(BF16)
        a_ref[...] = (g * jax.nn.sigmoid(g) * u).astype(BF16)

    a_spec = pl.BlockSpec((tm, d), lambda i, j, k: (i, 0))
    gu, a = _mm(
        tag + "_up", "nn", (t // tm, 2, 1),
        [(n, a_spec, wgu, pl.BlockSpec((None, d, fs), lambda i, j, k: (j, 0, 0))),
         (n, a_spec, wgu, pl.BlockSpec((None, d, fs), lambda i, j, k: (j + 2, 0, 0)))],
        [],
        [(_sds((2, t, f), BF16), pl.BlockSpec((2, tm, fs), lambda i, j, k: (0, i, j))),
         (_sds((t, f), BF16), pl.BlockSpec((tm, fs), lambda i, j, k: (i, j)))],
        (tm, fs), up_epilogue, comms=hooks.get("up", lambda: ())(), j_outer=True, chunked=True)

    row = pl.BlockSpec((tm, d), lambda i, j, k: (i, 0))
    h_new, n_new = _mm(
        tag + "_down", "nn", (t // tm, 1, 1),
        [(a, pl.BlockSpec((tm, f), lambda i, j, k: (i, 0)), wd, pl.BlockSpec((f, d), lambda i, j, k: (0, 0)))],
        [(h, row), (next_gain, pl.BlockSpec((1, d), lambda i, j, k: (0, 0)))],
        [(_sds((t, d), F32), row), (_sds((t, d), BF16), row)], (tm, d), _residual_norm_epilogue(0.5),
        comms=hooks.get("down", lambda: ())())
    return h_new, n_new, (n, gu, a)


def _ffn_bwd(tag, dh, dh_b, h, gain, wgu, wd, saved, hooks):
    n, gu, a = saved
    t, d = h.shape
    fs = wgu.shape[2]
    f = 2 * fs
    tm = _pick(t, ROW_TILE, 8)
    tk = _pick(t, ROW_TILE, 8)

    def dact_epilogue(accs, extras, outs, ij):
        da = 0.5 * accs[0]
        g = extras[0][0].astype(F32)
        u = extras[0][1].astype(F32)
        sg = jax.nn.sigmoid(g)
        outs[0][0] = (da * u * sg * (1.0 + g * (1.0 - sg))).astype(BF16)
        outs[0][1] = (da * g * sg).astype(BF16)

    gu_spec = pl.BlockSpec((2, tm, fs), lambda i, j, k: (0, i, j))
    (dgu,) = _mm(
        tag + "_dact", "nt", (t // tm, 2, 1),
        [(dh_b, pl.BlockSpec((tm, d), lambda i, j, k: (i, 0)), wd, pl.BlockSpec((fs, d), lambda i, j, k: (j, 0)))],
        [(gu, gu_spec)], [(_sds((2, t, f), BF16), gu_spec)], (tm, fs), dact_epilogue, j_outer=True, chunked=True,
        comms=hooks.get("dact", lambda: ())())

    def cast_epilogue(accs, extras, outs, ij):
        outs[0][...] = accs[0].astype(BF16)

    (dwgu,) = _mm(
        tag + "_dwgu", "tn", (1, 4, t // tk),
        [(n, pl.BlockSpec((tk, d), lambda i, j, k: (k, 0)),
          dgu, pl.BlockSpec((None, tk, fs), lambda i, j, k: (j // 2, k, j % 2)))],
        [], [(_sds((4, d, fs), BF16), pl.BlockSpec((None, d, fs), lambda i, j, k: (j, 0, 0)))], (d, fs), cast_epilogue,
        comms=hooks.get("dwgu", lambda: ())())

    def half_epilogue(accs, extras, outs, ij):
        outs[0][...] = (0.5 * accs[0]).astype(BF16)

    (dwd,) = _mm(
        tag + "_dwd", "tn", (2, 1, t // tk),
        [(a, pl.BlockSpec((tk, fs), lambda i, j, k: (k, i)), dh_b, pl.BlockSpec((tk, d), lambda i, j, k: (k, 0)))],
        [], [(_sds((f, d), BF16), pl.BlockSpec((fs, d), lambda i, j, k: (i, 0)))], (fs, d), half_epilogue,
        comms=hooks.get("dwd", lambda g: ())(dwgu))

    tmn = _pick(t, ROW_TILE, 8)
    extras, outs = _rms_bwd_io(h, gain, dh, tmn)
    dh_in, dh_in_b, dgain = _mm(
        tag + "_dnorm", "nt", (t // tmn, 1, 4),
        [(dgu, pl.BlockSpec((None, tmn, fs), lambda i, j, k: (k // 2, i, k % 2)),
          wgu, pl.BlockSpec((None, d, fs), lambda i, j, k: (k, 0, 0)))],
        extras, outs, (tmn, d), _rms_bwd_epilogue, comms=hooks.get("dnorm", lambda g, w: ())(dwgu, dwd))
    return dh_in, dh_in_b, dgain, dwgu, dwd


def _lane_lo(shape):
    return lax.broadcasted_iota(jnp.int32, shape, 1) < HEAD_DIM


def _pair_norm(xv, gain):
    lo = _lane_lo(xv.shape)
    x2 = xv * xv
    ms_lo = jnp.sum(jnp.where(lo, x2, 0.0), axis=-1, keepdims=True) * (1.0 / HEAD_DIM)
    ms_hi = jnp.sum(jnp.where(lo, 0.0, x2), axis=-1, keepdims=True) * (1.0 / HEAD_DIM)
    rstd = jnp.where(lo, lax.rsqrt(ms_lo + EPS), lax.rsqrt(ms_hi + EPS))
    xhat = xv * rstd
    return xhat * gain, xhat, rstd


def _pair_norm_bwd(dn, xhat, rstd, gain):
    lo = _lane_lo(dn.shape)
    gd = dn * gain
    t = gd * xhat
    m_lo = jnp.sum(jnp.where(lo, t, 0.0), axis=-1, keepdims=True) * (1.0 / HEAD_DIM)
    m_hi = jnp.sum(jnp.where(lo, 0.0, t), axis=-1, keepdims=True) * (1.0 / HEAD_DIM)
    dx = rstd * (gd - xhat * jnp.where(lo, m_lo, m_hi))
    return dx, jnp.sum(dn * xhat, axis=0, keepdims=True)


def _half(xv, hi):
    lo = _lane_lo(xv.shape)
    return jnp.where(lo, 0, xv) if hi else jnp.where(lo, xv, 0)


def _attn_window(i, prev):
    q0 = i * Q_BLOCK
    start = jnp.maximum(q0 - prev, 0)
    off = start - (q0 - prev)
    return pl.multiple_of(start, Q_BLOCK), pl.multiple_of(off, Q_BLOCK)


Q_BLOCKS_PER_STEP = 4
STEP_ROWS = Q_BLOCKS_PER_STEP * Q_BLOCK


def _attn_specs(cfg, s, steps):
    kw = cfg["kw"]
    q_spec = pl.BlockSpec((STEP_ROWS, A_WIDTH), lambda b, i: (b * steps + i, cfg["qblk"]))
    k_spec = pl.BlockSpec((s, kw), lambda b, i: (b, cfg["kblk"]))
    v_spec = pl.BlockSpec((s, kw), lambda b, i: (b, cfg["vblk"]))
    return q_spec, k_spec, v_spec


def _const_spec(shape):
    return pl.BlockSpec(shape, lambda b, i: (0,) * len(shape))


KEY_CHUNK = 128


def _pair_bias(bias_t):
    wext = bias_t.shape[1]
    return jnp.transpose(bias_t.reshape(N_HEADS // 2, 2, wext, Q_BLOCK), (0, 2, 1, 3)).reshape(
        N_HEADS // 2, wext, 2 * Q_BLOCK)


def _unpair_bias(db2):
    wext = db2.shape[1]
    return jnp.transpose(db2.reshape(N_HEADS // 2, wext, 2, Q_BLOCK), (0, 2, 1, 3)).reshape(N_HEADS, wext, Q_BLOCK)


def _pair_rows(rows):
    two = rows.reshape(N_HEADS // 2, 2 * rows.shape[1])
    return jnp.broadcast_to(two[:, None, :], (N_HEADS // 2, 8, two.shape[1]))


def _sub_lo(shape):
    return lax.broadcasted_iota(jnp.int32, shape, 0) < HEAD_DIM


def _by_half(lo_row, hi_row, rows):
    return jnp.where(_sub_lo((rows, lo_row.shape[1])), lo_row, hi_row)


def _stack_pair(xn, jq, group):
    parts = []
    for hq in range(2):
        hk = ((2 * jq + hq) // group) % 2
        xm = _half(xn, hq)
        if hq != hk:
            xm = pltpu.roll(xm, HEAD_DIM, 1)
        parts.append(xm)
    return jnp.concatenate(parts, axis=0).astype(BF16)


def _place_transposed(blk, dst_ref, c, heads, group):
    bt = blk.T
    lo = _sub_lo(bt.shape)
    for h in heads:
        src_hi = ((h // group) % 2) == 1
        part = jnp.where(lo, 0.0, bt) if src_hi else jnp.where(lo, bt, 0.0)
        if src_hi != (h % 2 == 1):
            part = pltpu.roll(part, HEAD_DIM, 0)
        dst_ref[h, c] = part.astype(BF16)


def _attn_fwd(name, qkv, bias2, sink2, gq, gk, cfg, n_batch, comms=()):
    t = qkv.shape[0]
    s = t // n_batch
    steps = s // STEP_ROWS
    nkc = s // KEY_CHUNK
    prev, group, kw = cfg["prev"], cfg["group"], cfg["kw"]
    w = prev + Q_BLOCK
    n_chunks = w // KEY_CHUNK
    wext = bias2.shape[1]
    plumb = _CommPlumbing(comms, 7, 2, 4)
    n_all_in = 7 + len(plumb.args)
    n_all_out = 2 + len(plumb.out_shape)

    def body(*refs):
        q_ref, k_ref, v_ref, bias_ref, sink_ref, gq_ref, gk_ref = refs[:7]
        y_ref, lse_ref = refs[n_all_in:n_all_in + 2]
        kn_ref, vt_ref, s_ref, pst_ref = refs[n_all_in + n_all_out:n_all_in + n_all_out + 4]
        step = pl.program_id(1)
        comm_refs = (refs[:n_all_in], refs[n_all_in:n_all_in + n_all_out], refs[n_all_in + n_all_out:])
        plumb.start_at(*comm_refs, (pl.program_id(0) == 0) & (step == 0))

        @pl.when(step == 0)
        def _():
            for jk in range(kw // PAIR):
                cols = pl.ds(jk * PAIR, PAIR)
                heads = [h for h in range(N_HEADS) if (h // group) // 2 == jk]
                kn, _, _ = _pair_norm(k_ref[:, cols].astype(F32), gk_ref[...])
                kn_ref[:, cols] = kn.astype(BF16)
                for c in range(nkc):
                    _place_transposed(v_ref[pl.ds(c * KEY_CHUNK, KEY_CHUNK), cols].astype(F32), vt_ref, c, heads, group)

        sub8 = lax.broadcasted_iota(jnp.int32, (N_HEADS, Q_BLOCK), 0)
        for sb in range(Q_BLOCKS_PER_STEP):
            qrows = pl.ds(sb * Q_BLOCK, Q_BLOCK)
            start, off = _attn_window(step * Q_BLOCKS_PER_STEP + sb, prev)
            c0 = start // KEY_CHUNK
            lse = jnp.zeros((N_HEADS, Q_BLOCK), F32)
            for jq in range(N_HEADS // 2):
                kcols = pl.ds((((2 * jq) // group) // 2) * PAIR, PAIR)
                qn, _, _ = _pair_norm(q_ref[qrows, pl.ds(jq * PAIR, PAIR)].astype(F32), gq_ref[...])
                qs = _stack_pair(qn * ATTN_SCALE, jq, group)
                s_ref[...] = _dot(kn_ref[pl.ds(start, w), kcols], qs, "nt")
                m = sink_ref[jq, 0:1, :]
                for c in range(n_chunks):
                    r = pl.ds(c * KEY_CHUNK, KEY_CHUNK)
                    s2 = s_ref[r, :] + bias_ref[jq, pl.ds(off + c * KEY_CHUNK, KEY_CHUNK), :]
                    s_ref[r, :] = s2
                    m = jnp.maximum(m, jnp.max(s2, axis=0, keepdims=True))
                l = jnp.exp(sink_ref[jq, 0:1, :] - m)
                for c in range(n_chunks):
                    p = jnp.exp(s_ref[pl.ds(c * KEY_CHUNK, KEY_CHUNK), :] - m)
                    l = l + jnp.sum(p, axis=0, keepdims=True)
                    pst_ref[pl.ds(2 * c * KEY_CHUNK, KEY_CHUNK), :] = p[:, :Q_BLOCK].astype(BF16)
                    pst_ref[pl.ds((2 * c + 1) * KEY_CHUNK, KEY_CHUNK), :] = p[:, Q_BLOCK:].astype(BF16)
                vl = jnp.concatenate([vt_ref[2 * jq + hq, c0 + c] for c in range(n_chunks) for hq in range(2)], axis=1)
                ot = _dot(vl, pst_ref[...], "nn")
                inv = 1.0 / l
                ot = ot * _by_half(inv[:, :Q_BLOCK], inv[:, Q_BLOCK:], PAIR)
                y_ref[qrows, pl.ds(jq * PAIR, PAIR)] = ot.T.astype(BF16)
                lse2 = m + jnp.log(l)
                lse = jnp.where(sub8 == 2 * jq, lse2[:, :Q_BLOCK], lse)
                lse = jnp.where(sub8 == 2 * jq + 1, lse2[:, Q_BLOCK:], lse)
            lse_ref[sb] = lse
        plumb.finish_at(*comm_refs, (pl.program_id(0) == n_batch - 1) & (step == steps - 1))

    q_spec, k_spec, v_spec = _attn_specs(cfg, s, steps)
    res = pl.pallas_call(
        body, name=name, grid=(n_batch, steps),
        in_specs=[q_spec, k_spec, v_spec, _const_spec((N_HEADS // 2, wext, 2 * Q_BLOCK)),
                  _const_spec((N_HEADS // 2, 8, 2 * Q_BLOCK)), _const_spec((1, PAIR)), _const_spec((1, PAIR))]
        + [ANY] * len(plumb.args),
        out_specs=[pl.BlockSpec((STEP_ROWS, A_WIDTH), lambda b, i: (b * steps + i, 0)),
                   pl.BlockSpec((Q_BLOCKS_PER_STEP, N_HEADS, Q_BLOCK), lambda b, i: (b * steps + i, 0, 0))]
        + [ANY] * len(plumb.out_shape),
        out_shape=[_sds((t, A_WIDTH), BF16), _sds((t // Q_BLOCK, N_HEADS, Q_BLOCK), F32)] + plumb.out_shape,
        scratch_shapes=[pltpu.VMEM((s, kw), BF16), pltpu.VMEM((N_HEADS, nkc, PAIR, KEY_CHUNK), BF16),
                        pltpu.VMEM((w, 2 * Q_BLOCK), F32), pltpu.VMEM((2 * w, Q_BLOCK), BF16)] + plumb.scratch,
        input_output_aliases=plumb.aliases,
        compiler_params=plumb.params(vmem_limit_bytes=VMEM_LIMIT_V7X),
    )(qkv, qkv, qkv, bias2, sink2, gq, gk, *plumb.args)
    return plumb.deliver(res)


def _attn_bwd(name, qkv, bias2, sink2, gq, gk, y, dy, lse, cfg, n_batch, want_dbias, comms=()):
    t = qkv.shape[0]
    s = t // n_batch
    steps = s // STEP_ROWS
    nkc = s // KEY_CHUNK
    prev, group, kw = cfg["prev"], cfg["group"], cfg["kw"]
    w = prev + Q_BLOCK
    n_chunks = w // KEY_CHUNK
    wext = bias2.shape[1]
    plumb = _CommPlumbing(comms, 10, 7, 9)
    n_all_in = 10 + len(plumb.args)
    n_all_out = 7 + len(plumb.out_shape)

    def body(*refs):
        q_ref, k_ref, v_ref, bias_ref, sink_ref, gq_ref, gk_ref, y_ref, dy_ref, lse_ref = refs[:10]
        dq_ref, dk_ref, dv_ref, db_ref, dsink_ref, dgq_ref, dgk_ref = refs[n_all_in:n_all_in + 7]
        kn_ref, knt_ref, dkn_ref, dvs_ref, s_ref, dp_ref, pb_ref, dsb_ref, dst_ref = \
            refs[n_all_in + n_all_out:n_all_in + n_all_out + 9]
        b = pl.program_id(0)
        step = pl.program_id(1)
        first = (b == 0) & (step == 0)
        comm_refs = (refs[:n_all_in], refs[n_all_in:n_all_in + n_all_out], refs[n_all_in + n_all_out:])
        plumb.start_at(*comm_refs, first)

        @pl.when(step == 0)
        def _():
            for jk in range(kw // PAIR):
                cols = pl.ds(jk * PAIR, PAIR)
                heads = [h for h in range(N_HEADS) if (h // group) // 2 == jk]
                for c in range(nkc):
                    rows = pl.ds(c * KEY_CHUNK, KEY_CHUNK)
                    kn, _, _ = _pair_norm(k_ref[rows, cols].astype(F32), gk_ref[...])
                    kn_ref[rows, cols] = kn.astype(BF16)
                    _place_transposed(kn, knt_ref, c, heads, group)
            dkn_ref[...] = jnp.zeros(dkn_ref.shape, F32)
            dvs_ref[...] = jnp.zeros(dvs_ref.shape, F32)

        @pl.when(first)
        def _():
            db_ref[...] = jnp.zeros(db_ref.shape, F32)
            dsink_ref[...] = jnp.zeros(dsink_ref.shape, F32)
            dgq_ref[...] = jnp.zeros(dgq_ref.shape, F32)
            dgk_ref[...] = jnp.zeros(dgk_ref.shape, F32)

        for sb in range(Q_BLOCKS_PER_STEP):
            qrows = pl.ds(sb * Q_BLOCK, Q_BLOCK)
            start, off = _attn_window(step * Q_BLOCKS_PER_STEP + sb, prev)
            c0 = start // KEY_CHUNK
            for jq in range(N_HEADS // 2):
                cols = pl.ds(jq * PAIR, PAIR)
                kcols = pl.ds((((2 * jq) // group) // 2) * PAIR, PAIR)
                qn, q_hat, q_rstd = _pair_norm(q_ref[qrows, cols].astype(F32), gq_ref[...])
                qs = _stack_pair(qn * ATTN_SCALE, jq, group)
                do_pair = dy_ref[qrows, cols].astype(F32)
                dos = _stack_pair(do_pair, jq, group)
                prod_t = (do_pair * y_ref[qrows, cols].astype(F32)).T
                lo = _sub_lo(prod_t.shape)
                delta2 = jnp.concatenate([jnp.sum(jnp.where(lo, prod_t, 0.0), axis=0, keepdims=True),
                                          jnp.sum(jnp.where(lo, 0.0, prod_t), axis=0, keepdims=True)], axis=1)
                lse2 = jnp.concatenate([lse_ref[sb, 2 * jq:2 * jq + 1, :], lse_ref[sb, 2 * jq + 1:2 * jq + 2, :]],
                                       axis=1)
                dsk = -jnp.exp(sink_ref[jq, 0:1, :] - lse2) * delta2
                dsink_ref[2 * jq:2 * jq + 1, :] += dsk[:, :Q_BLOCK]
                dsink_ref[2 * jq + 1:2 * jq + 2, :] += dsk[:, Q_BLOCK:]
                rows_w = pl.ds(start, w)
                s_ref[...] = _dot(kn_ref[rows_w, kcols], qs, "nt")
                dp_ref[...] = _dot(v_ref[rows_w, kcols], dos, "nt")
                for c in range(n_chunks):
                    r = pl.ds(c * KEY_CHUNK, KEY_CHUNK)
                    brows = pl.ds(off + c * KEY_CHUNK, KEY_CHUNK)
                    p = jnp.exp(s_ref[r, :] + bias_ref[jq, brows, :] - lse2)
                    ds = p * (dp_ref[r, :] - delta2)
                    if want_dbias:
                        db_ref[jq, brows, :] += ds
                    ds_b = ds.astype(BF16)
                    pb_ref[r, :] = p.astype(BF16)
                    dsb_ref[r, :] = ds_b
                    dst_ref[pl.ds(2 * c * KEY_CHUNK, KEY_CHUNK), :] = ds_b[:, :Q_BLOCK]
                    dst_ref[pl.ds((2 * c + 1) * KEY_CHUNK, KEY_CHUNK), :] = ds_b[:, Q_BLOCK:]
                dkn_ref[rows_w, kcols] += _dot(dsb_ref[...], qs, "nn")
                dvs_ref[rows_w, kcols] += _dot(pb_ref[...], dos, "nn")
                kl = jnp.concatenate([knt_ref[2 * jq + hq, c0 + c] for c in range(n_chunks) for hq in range(2)],
                                     axis=1)
                dqt = _dot(kl, dst_ref[...], "nn")
                dq_raw, dg = _pair_norm_bwd(dqt.T * ATTN_SCALE, q_hat, q_rstd, gq_ref[...])
                dq_ref[qrows, cols] = dq_raw.astype(BF16)
                dgq_ref[...] += dg

        @pl.when(step == steps - 1)
        def _():
            for jk in range(kw // PAIR):
                kcols = pl.ds(jk * PAIR, PAIR)
                _, k_hat, k_rstd = _pair_norm(k_ref[:, kcols].astype(F32), gk_ref[...])
                dk_raw, dg = _pair_norm_bwd(dkn_ref[:, kcols], k_hat, k_rstd, gk_ref[...])
                dk_ref[:, kcols] = dk_raw.astype(BF16)
                dgk_ref[...] += dg
            dv_ref[...] = dvs_ref[...].astype(BF16)

        plumb.finish_at(*comm_refs, (b == n_batch - 1) & (step == steps - 1))

    q_spec, k_spec, v_spec = _attn_specs(cfg, s, steps)
    row = pl.BlockSpec((STEP_ROWS, A_WIDTH), lambda b, i: (b * steps + i, 0))
    kv_out = pl.BlockSpec((s, kw), lambda b, i: (b, 0))
    pair_bias = _const_spec((N_HEADS // 2, wext, 2 * Q_BLOCK))
    res = pl.pallas_call(
        body, name=name, grid=(n_batch, steps),
        in_specs=[q_spec, k_spec, v_spec, pair_bias, _const_spec((N_HEADS // 2, 8, 2 * Q_BLOCK)),
                  _const_spec((1, PAIR)), _const_spec((1, PAIR)), row, row,
                  pl.BlockSpec((Q_BLOCKS_PER_STEP, N_HEADS, Q_BLOCK), lambda b, i: (b * steps + i, 0, 0))]
        + [ANY] * len(plumb.args),
        out_specs=[row, kv_out, kv_out, pair_bias, _const_spec((N_HEADS, 128)),
                   _const_spec((1, PAIR)), _const_spec((1, PAIR))] + [ANY] * len(plumb.out_shape),
        out_shape=[_sds((t, A_WIDTH), BF16), _sds((t, kw), BF16), _sds((t, kw), BF16),
                   _sds((N_HEADS // 2, wext, 2 * Q_BLOCK), F32), _sds((N_HEADS, 128), F32),
                   _sds((1, PAIR), F32), _sds((1, PAIR), F32)] + plumb.out_shape,
        scratch_shapes=[pltpu.VMEM((s, kw), BF16), pltpu.VMEM((N_HEADS, nkc, PAIR, KEY_CHUNK), BF16),
                        pltpu.VMEM((s, kw), F32), pltpu.VMEM((s, kw), F32),
                        pltpu.VMEM((w, 2 * Q_BLOCK), F32), pltpu.VMEM((w, 2 * Q_BLOCK), F32),
                        pltpu.VMEM((w, 2 * Q_BLOCK), BF16), pltpu.VMEM((w, 2 * Q_BLOCK), BF16),
                        pltpu.VMEM((2 * w, Q_BLOCK), BF16)] + plumb.scratch,
        input_output_aliases=plumb.aliases,
        compiler_params=plumb.params(vmem_limit_bytes=VMEM_LIMIT_V7X),
    )(qkv, qkv, qkv, bias2, sink2, gq, gk, y, dy, lse, *plumb.args)
    return plumb.deliver(res)


def _band_tables(prev_chunks):
    prev = prev_chunks * CHUNK
    wext = 2 * prev + Q_BLOCK
    jj = np.arange(wext)[:, None]
    ii = np.arange(Q_BLOCK)[None, :]
    dist = prev + ii - jj
    rel_chunk = (prev // CHUNK + ii // CHUNK) - jj // CHUNK
    allowed = (rel_chunk >= 0) & (rel_chunk <= prev_chunks)
    return dist, allowed


def _alibi_slopes():
    return np.array([2.0 ** (-8.0 * (h + 1) / N_HEADS) for h in range(N_HEADS)], dtype=np.float32)


def _diag_onehot(prev, wext):
    n_diag = wext + Q_BLOCK - 1
    idx = np.clip(prev + Q_BLOCK - 1 - np.arange(n_diag), -A_MAX_REL, A_MAX_REL) + A_MAX_REL
    onehot = np.zeros((n_diag, 2 * A_MAX_REL + 1), np.float32)
    onehot[np.arange(n_diag), idx] = 1.0
    return onehot


def _bias_a(rel_bias):
    prev = A_PREV_CHUNKS * CHUNK
    _, allowed = _band_tables(A_PREV_CHUNKS)
    wext = allowed.shape[0]
    n_diag = wext + Q_BLOCK - 1
    seq = jnp.dot(rel_bias, jnp.asarray(_diag_onehot(prev, wext).T), precision=lax.Precision.HIGHEST)
    seq = jnp.pad(seq, ((0, 0), (0, 1)))
    rows = jnp.broadcast_to(seq[:, None, :], (N_HEADS, Q_BLOCK, n_diag + 1)).reshape(N_HEADS, -1)
    skew = rows[:, :Q_BLOCK * n_diag].reshape(N_HEADS, Q_BLOCK, n_diag)
    tile = jnp.transpose(skew[:, :, Q_BLOCK - 1:Q_BLOCK - 1 + wext], (0, 2, 1))
    return jnp.where(jnp.asarray(allowed)[None], tile, NEG_INF)


def _bias_b():
    dist, allowed = _band_tables(B_PREV_CHUNKS)
    bias = -_alibi_slopes()[:, None, None] * np.abs(dist).astype(np.float32)[None]
    return jnp.asarray(np.where(allowed[None], bias, np.float32(NEG_INF)).astype(np.float32))


def _rel_bias_grad(db_t):
    prev = A_PREV_CHUNKS * CHUNK
    wext = db_t.shape[1]
    n_diag = wext + Q_BLOCK - 1
    wp = n_diag + Q_BLOCK - 1
    xp = jnp.pad(jnp.transpose(db_t, (0, 2, 1)), ((0, 0), (0, 0), (Q_BLOCK - 1, Q_BLOCK - 1)))
    flat = jnp.pad(xp.reshape(N_HEADS, Q_BLOCK * wp), ((0, 0), (0, Q_BLOCK)))
    skew = flat.reshape(N_HEADS, Q_BLOCK, wp + 1)[:, :, :n_diag]
    diag = jnp.sum(skew, axis=1)
    return jnp.dot(diag, jnp.asarray(_diag_onehot(prev, wext)), precision=lax.Precision.HIGHEST)


def _ew(name, fn, ins, out_dtypes):
    r, c = ins[0].shape
    rb = _pick(r, max(16, (1 << 19) // c), 16)
    spec = pl.BlockSpec((rb, c), lambda i: (i, 0))

    def body(*refs):
        vals = fn(*[ref[...] for ref in refs[:len(ins)]])
        for ref, val in zip(refs[len(ins):], vals):
            ref[...] = val.astype(ref.dtype)

    return pl.pallas_call(
        body, name=name, grid=(r // rb,), in_specs=[spec] * len(ins), out_specs=[spec] * len(out_dtypes),
        out_shape=[_sds((r, c), dt) for dt in out_dtypes], compiler_params=_params(),
    )(*ins)


def _cast_into_slot(name, w, chip):
    r, c = w.shape
    rb = _pick(r, max(16, (1 << 19) // c), 16)

    def body(chip_ref, w_ref, o_ref):
        o_ref[...] = w_ref[...].astype(BF16)

    return pl.pallas_call(
        body, name=name,
        grid_spec=pltpu.PrefetchScalarGridSpec(
            num_scalar_prefetch=1, grid=(r // rb,),
            in_specs=[pl.BlockSpec((rb, c), lambda i, chip_ref: (i, 0))],
            out_specs=pl.BlockSpec((None, rb, c), lambda i, chip_ref: (chip_ref[0], i, 0))),
        out_shape=_sds((N_CHIPS, r, c), BF16), compiler_params=_params(),
    )(chip, w)


def _adamw_math(w, g, m, v):
    m = ADAM_B1 * m + (1.0 - ADAM_B1) * g
    v = ADAM_B2 * v + (1.0 - ADAM_B2) * (g * g)
    m_hat = m / (1.0 - ADAM_B1 ** ADAM_STEP)
    v_hat = v / (1.0 - ADAM_B2 ** ADAM_STEP)
    delta = -ADAM_LR * (m_hat / (jnp.sqrt(v_hat) + ADAM_EPS) + ADAM_WD * w)
    return delta, m, v


def _adamw_terms(name, terms, w, m, v, comms=()):
    r, c = w.shape
    hr = r // 2
    rb = _pick(hr, max(16, (1 << 19) // c), 16)
    nb = hr // rb
    plumb = _CommPlumbing(comms, 4, 4, 0)
    n_all_in = 4 + len(plumb.args)
    n_all_out = 4 + len(plumb.out_shape)

    def body(*refs):
        t_ref, w_ref, m_ref, v_ref = refs[:4]
        g_ref, d_ref, nm_ref, nv_ref = refs[n_all_in:n_all_in + 4]
        comm_refs = (refs[:n_all_in], refs[n_all_in:n_all_in + n_all_out], refs[n_all_in + n_all_out:])
        h, i = pl.program_id(0), pl.program_id(1)
        plumb.start_at(*comm_refs, (h == 0) & (i == 0))
        g = t_ref[0].astype(F32)
        for k in range(1, N_CHIPS):
            g = g + t_ref[k].astype(F32)
        delta, nm, nv = _adamw_math(w_ref[...], g, m_ref[...], v_ref[...])
        g_ref[...] = g
        d_ref[...] = delta
        nm_ref[...] = nm
        nv_ref[...] = nv
        plumb.finish_at(*comm_refs, (h == 1) & (i == nb - 1))

    spec = pl.BlockSpec((rb, c), lambda h, i: (h * nb + i, 0))
    res = pl.pallas_call(
        body, name=name, grid=(2, nb),
        in_specs=[pl.BlockSpec((None, N_CHIPS, rb, c), lambda h, i: (h, 0, i, 0)), spec, spec, spec]
        + [ANY] * len(plumb.args),
        out_specs=[spec] * 4 + [ANY] * len(plumb.out_shape),
        out_shape=[_sds((r, c), F32)] * 4 + plumb.out_shape,
        scratch_shapes=plumb.scratch,
        input_output_aliases=plumb.aliases,
        compiler_params=plumb.params(vmem_limit_bytes=VMEM_LIMIT_V7X),
    )(terms, w, m, v, *plumb.args)
    return plumb.deliver(res)


def _mesh_place():
    x, y, c = lax.axis_index("x"), lax.axis_index("y"), lax.axis_index("c")
    chips = [(x, 1 - y), (1 - x, y), (1 - x, 1 - y)]
    return x, y, c, chips


def _all_gather_weights(bufs):
    n = len(bufs)

    def body(*refs):
        outs = refs[n:2 * n]
        ici_send, ici_recv, d2d_send, d2d_recv = refs[2 * n:]
        x, y, c, chips = _mesh_place()
        me = 2 * x + y
        sibling = (x, y, 1 - c)
        barrier = pltpu.get_barrier_semaphore()
        for peer in [sibling] + [(tx, ty, c) for tx, ty in chips]:
            pl.semaphore_signal(barrier, inc=1, device_id=peer, device_id_type=MESH)
        pl.semaphore_wait(barrier, N_CHIPS)
        sent = []
        for wi in range(n):
            for k, (tx, ty) in enumerate(chips):
                own = outs[wi].at[me, c]
                cp = pltpu.make_async_remote_copy(
                    src_ref=own, dst_ref=own, send_sem=ici_send.at[wi * 3 + k], recv_sem=ici_recv.at[wi * 3 + k],
                    device_id=(tx, ty, c), device_id_type=MESH)
                cp.start()
                sent.append(cp)
        passed = []
        for wi in range(n):
            for k, (tx, ty) in enumerate(chips):
                slab = outs[wi].at[2 * tx + ty, c]
                pltpu.make_async_remote_copy(
                    src_ref=slab, dst_ref=slab, send_sem=ici_send.at[wi * 3 + k], recv_sem=ici_recv.at[wi * 3 + k],
                    device_id=(tx, ty, c), device_id_type=MESH).wait_recv()
                fw = pltpu.make_async_remote_copy(
                    src_ref=slab, dst_ref=slab, send_sem=d2d_send.at[wi * 3 + k], recv_sem=d2d_recv.at[wi * 3 + k],
                    device_id=sibling, device_id_type=MESH)
                fw.start()
                passed.append(fw)
        for wi in range(n):
            for k, (tx, ty) in enumerate(chips):
                slab = outs[wi].at[2 * tx + ty, 1 - c]
                pltpu.make_async_remote_copy(
                    src_ref=slab, dst_ref=slab, send_sem=d2d_send.at[wi * 3 + k], recv_sem=d2d_recv.at[wi * 3 + k],
                    device_id=sibling, device_id_type=MESH).wait_recv()
        for cp in sent + passed:
            cp.wait_send()

    return pl.pallas_call(
        body, name="all_gather_weights",
        in_specs=[ANY] * n, out_specs=[ANY] * n,
        out_shape=[_sds(g.shape, g.dtype) for g in bufs],
        scratch_shapes=[pltpu.SemaphoreType.DMA((3 * n,))] * 4,
        input_output_aliases={i: i for i in range(n)},
        compiler_params=pltpu.CompilerParams(collective_id=COLLECTIVE_IDS[("chips", "sibling")]),
    )(*bufs)


def _gather_ici(bufs):
    n = len(bufs)

    def copies(outs, sems):
        send_sem, recv_sem = sems
        x, y, c, chips = _mesh_place()
        me = 2 * x + y
        sends, recvs = [], []
        for wi in range(n):
            for k, (tx, ty) in enumerate(chips):
                sems_k = dict(send_sem=send_sem.at[wi * 3 + k], recv_sem=recv_sem.at[wi * 3 + k],
                              device_id=(tx, ty, c), device_id_type=MESH)
                own = outs[wi].at[me, c]
                sends.append(pltpu.make_async_remote_copy(src_ref=own, dst_ref=own, **sems_k))
                slab = outs[wi].at[2 * tx + ty, c]
                recvs.append(pltpu.make_async_remote_copy(src_ref=slab, dst_ref=slab, **sems_k))
        return sends, recvs

    def start(ins, outs, sems):
        for cp in copies(outs, sems)[0]:
            cp.start()

    def finish(ins, outs, sems):
        sends, recvs = copies(outs, sems)
        for cp in recvs:
            cp.wait_recv()
        for cp in sends:
            cp.wait_send()

    return _Comm("chips", bufs, [_sds(g.shape, g.dtype) for g in bufs], {i: i for i in range(n)},
                 [pltpu.SemaphoreType.DMA((3 * n,)), pltpu.SemaphoreType.DMA((3 * n,))], start, finish)


def _gather_d2d(gathered):
    n = len(gathered)

    def copies(outs, sems):
        send_sem, recv_sem = sems
        x, y, c, chips = _mesh_place()
        sends, recvs = [], []
        for wi in range(n):
            for k, (tx, ty) in enumerate(chips):
                sems_k = dict(send_sem=send_sem.at[wi * 3 + k], recv_sem=recv_sem.at[wi * 3 + k],
                              device_id=(x, y, 1 - c), device_id_type=MESH)
                mine = outs[wi].at[2 * tx + ty, c]
                theirs = outs[wi].at[2 * tx + ty, 1 - c]
                sends.append(pltpu.make_async_remote_copy(src_ref=mine, dst_ref=mine, **sems_k))
                recvs.append(pltpu.make_async_remote_copy(src_ref=theirs, dst_ref=theirs, **sems_k))
        return sends, recvs

    def start(ins, outs, sems):
        for cp in copies(outs, sems)[0]:
            cp.start()

    def finish(ins, outs, sems):
        sends, recvs = copies(outs, sems)
        for cp in recvs:
            cp.wait_recv()
        for cp in sends:
            cp.wait_send()

    return _Comm("sibling", gathered, [_sds(g.shape, g.dtype) for g in gathered], {i: i for i in range(n)},
                 [pltpu.SemaphoreType.DMA((3 * n,)), pltpu.SemaphoreType.DMA((3 * n,))], start, finish)


def _exchange_halves(grads):
    n = len(grads)

    def copies(ins, outs, sems):
        send_sem, recv_sem = sems
        x, y, c, _ = _mesh_place()
        return [pltpu.make_async_remote_copy(
            src_ref=ins[wi].at[t, 1 - c], dst_ref=outs[wi].at[t],
            send_sem=send_sem.at[wi * N_CHIPS + t], recv_sem=recv_sem.at[wi * N_CHIPS + t],
            device_id=(x, y, 1 - c), device_id_type=MESH) for wi in range(n) for t in range(N_CHIPS)]

    def start(ins, outs, sems):
        for cp in copies(ins, outs, sems):
            cp.start()

    def finish(ins, outs, sems):
        for cp in copies(ins, outs, sems):
            cp.wait()

    return _Comm("sibling", grads, [_sds((N_CHIPS,) + g.shape[2:], g.dtype) for g in grads], {},
                 [pltpu.SemaphoreType.DMA((N_CHIPS * n,)), pltpu.SemaphoreType.DMA((N_CHIPS * n,))], start, finish)


def _scatter_ici(sums):
    n = len(sums)

    def copies(ins, outs, sems):
        local_sem, send_sem, recv_sem = sems
        x, y, c, chips = _mesh_place()
        me = 2 * x + y
        local, sends, recvs = [], [], []
        for wi in range(n):
            local.append(pltpu.make_async_copy(ins[wi].at[me], outs[wi].at[c, 0], local_sem.at[wi]))
            for k, (tx, ty) in enumerate(chips):
                sems_k = dict(send_sem=send_sem.at[wi * 3 + k], recv_sem=recv_sem.at[wi * 3 + k],
                              device_id=(tx, ty, c), device_id_type=MESH)
                land = outs[wi].at[c, k + 1]
                sends.append(pltpu.make_async_remote_copy(src_ref=ins[wi].at[2 * tx + ty], dst_ref=land, **sems_k))
                recvs.append(pltpu.make_async_remote_copy(src_ref=land, dst_ref=land, **sems_k))
        return local, sends, recvs

    def start(ins, outs, sems):
        local, sends, _ = copies(ins, outs, sems)
        for cp in local + sends:
            cp.start()

    def finish(ins, outs, sems):
        local, sends, recvs = copies(ins, outs, sems)
        for cp in local:
            cp.wait()
        for cp in recvs:
            cp.wait_recv()
        for cp in sends:
            cp.wait_send()

    return _Comm("chips", sums, [_sds((2, N_CHIPS) + s.shape[1:], s.dtype) for s in sums], {},
                 [pltpu.SemaphoreType.DMA((n,)), pltpu.SemaphoreType.DMA((3 * n,)), pltpu.SemaphoreType.DMA((3 * n,))],
                 start, finish)


def _scatter_d2d(terms):
    n = len(terms)

    def copies(outs, sems):
        send_sem, recv_sem = sems
        x, y, c, _ = _mesh_place()
        sends, recvs = [], []
        for wi in range(n):
            sems_w = dict(send_sem=send_sem.at[wi], recv_sem=recv_sem.at[wi],
                          device_id=(x, y, 1 - c), device_id_type=MESH)
            sends.append(pltpu.make_async_remote_copy(src_ref=outs[wi].at[c], dst_ref=outs[wi].at[c], **sems_w))
            recvs.append(pltpu.make_async_remote_copy(src_ref=outs[wi].at[1 - c], dst_ref=outs[wi].at[1 - c], **sems_w))
        return sends, recvs

    def start(ins, outs, sems):
        for cp in copies(outs, sems)[0]:
            cp.start()

    def finish(ins, outs, sems):
        sends, recvs = copies(outs, sems)
        for cp in recvs:
            cp.wait_recv()
        for cp in sends:
            cp.wait_send()

    return _Comm("sibling", terms, [_sds(t.shape, t.dtype) for t in terms], {i: i for i in range(n)},
                 [pltpu.SemaphoreType.DMA((n,)), pltpu.SemaphoreType.DMA((n,))], start, finish)


def _chip_sum(name, grad, got, core):
    _, _, hr, c = grad.shape
    rb = _pick(hr, max(16, (1 << 19) // c), 16)

    def body(core_ref, a_ref, b_ref, o_ref):
        o_ref[...] = (a_ref[...].astype(F32) + b_ref[...].astype(F32)).astype(BF16)

    out_spec = pl.BlockSpec((None, rb, c), lambda t, i, core_ref: (t, i, 0))
    return pl.pallas_call(
        body, name=name,
        grid_spec=pltpu.PrefetchScalarGridSpec(
            num_scalar_prefetch=1, grid=(N_CHIPS, hr // rb),
            in_specs=[pl.BlockSpec((None, None, rb, c), lambda t, i, core_ref: (t, core_ref[0], i, 0)), out_spec],
            out_specs=out_spec),
        out_shape=_sds((N_CHIPS, hr, c), BF16), compiler_params=_params(),
    )(core, grad, got)


def _all_reduce_small(pack):
    r = pack.shape[0]

    def body(p_ref, o_ref, land_ref, send_sem, recv_sem):
        x, y, c, _ = _mesh_place()
        me = 4 * x + 2 * y + c
        flips = [(k >> 2 & 1, k >> 1 & 1, k & 1) for k in range(1, N_DEV)]

        def peer(fx, fy, fc):
            return (1 - x if fx else x, 1 - y if fy else y, 1 - c if fc else c)

        land_ref[me] = p_ref[...]
        sent = []
        for k, flip in enumerate(flips):
            cp = pltpu.make_async_remote_copy(
                src_ref=p_ref, dst_ref=land_ref.at[me], send_sem=send_sem.at[k], recv_sem=recv_sem.at[k],
                device_id=peer(*flip), device_id_type=MESH)
            cp.start()
            sent.append(cp)
        for k, flip in enumerate(flips):
            px, py, pc = peer(*flip)
            slot = land_ref.at[4 * px + 2 * py + pc]
            pltpu.make_async_remote_copy(
                src_ref=slot, dst_ref=slot, send_sem=send_sem.at[k], recv_sem=recv_sem.at[k],
                device_id=(px, py, pc), device_id_type=MESH).wait_recv()
        total = land_ref[0]
        for d in range(1, N_DEV):
            total = total + land_ref[d]
        o_ref[...] = total
        for cp in sent:
            cp.wait_send()

    vmem = pl.BlockSpec(memory_space=pltpu.VMEM)
    return pl.pallas_call(
        body, name="all_reduce_small", in_specs=[vmem], out_specs=vmem, out_shape=_sds((r, 128), F32),
        scratch_shapes=[pltpu.VMEM((N_DEV, r, 128), F32), pltpu.SemaphoreType.DMA((N_DEV - 1,)),
                        pltpu.SemaphoreType.DMA((N_DEV - 1,))],
    )(pack)


PACK_TILE = 8 * 128


def _pack(items):
    rows, i = [], 0
    while i < len(items):
        j = i
        while j < len(items) and items[j].size == items[i].size:
            j += 1
        group = jnp.stack([it.reshape(-1).astype(F32) for it in items[i:j]])
        rows.append(jnp.pad(group, ((0, 0), (0, -group.shape[1] % PACK_TILE))).reshape(-1, 128))
        i = j
    return jnp.concatenate(rows, axis=0)


def _unpack(pack, shapes):
    out, row = [], 0
    for shp in shapes:
        size = int(np.prod(shp))
        nrow = -(-size // PACK_TILE) * (PACK_TILE // 128)
        out.append(pack[row:row + nrow].reshape(-1)[:size].reshape(shp))
        row += nrow
    return out


BIG = ["ffn1_w_gu", "ffn1_w_down", "w_in", "w_gate", "w_proj_a", "w_proj_b", "w_out",
       "ffn2_w_gu", "ffn2_w_down", "w_ple_gate", "w_ple_proj"]
SMALL = ["ffn1_norm", "mix_norm", "ffn2_norm", "ple_norm", "a_q_norm", "a_k_norm", "b_q_norm", "b_k_norm",
         "a_rel_bias", "b_sinks"]
WEIGHTS = ["ffn1_norm", "ffn1_w_gu", "ffn1_w_down", "mix_norm", "w_in", "a_q_norm", "a_k_norm", "a_rel_bias",
           "b_q_norm", "b_k_norm", "b_sinks", "w_gate", "w_proj_a", "w_proj_b", "w_out", "ffn2_norm",
           "ffn2_w_gu", "ffn2_w_down", "ple_norm", "w_ple_gate", "w_ple_proj"]
ATTN_A = dict(prev=A_PREV_CHUNKS * CHUNK, group=1, kw=A_WIDTH, qblk=0, kblk=1, vblk=2)
ATTN_B = dict(prev=B_PREV_CHUNKS * CHUNK, group=N_HEADS // B_KV_HEADS, kw=B_KV_WIDTH, qblk=3,
              kblk=4 * A_WIDTH // B_KV_WIDTH, vblk=4 * A_WIDTH // B_KV_WIDTH + 1)


def _cast_epilogue(accs, extras, outs, ij):
    for acc, out in zip(accs, outs):
        out[...] = acc.astype(out.dtype)


GATHER_FIRST = ["ffn1_w_gu", "ffn1_w_down"]
TAIL_HOSTS = ("ffn2_w_gu", "ffn2_w_down")
ROW_SHARDED = ("ffn1_w_down", "ffn2_w_down", "w_out", "w_ple_gate")


def _slotted(name, grad):
    if name == "w_in":
        rows, cols = grad.shape
        grad = jnp.transpose(grad.reshape(rows, N_CHIPS, cols // N_CHIPS), (1, 0, 2))
    elif name in ROW_SHARDED:
        grad = grad.reshape(N_CHIPS, grad.shape[0] // N_CHIPS, grad.shape[1])
    return grad.reshape(N_CHIPS, 2, grad.shape[1] // 2, grad.shape[2])


def _local_step(xt, pt, tgt, n_batch, bufs, small, core):
    t, d = xt.shape
    tm = _pick(t, ROW_TILE, 8)
    tk = _pick(t, ROW_TILE, 8)
    nt = t // tm
    row = pl.BlockSpec((tm, d), lambda i, j, k: (i, 0))
    gs = bufs["w_gate"].shape[2]
    ps = bufs["w_proj_a"].shape[2]
    es = bufs["w_ple_proj"].shape[2]
    pdim = pt.shape[1]
    ncols = N_CHIPS * bufs["w_in"].shape[2]
    tin = ncols // 2
    assert 2 * gs == d and 4 * ps == d and 4 * es == d and tin % 128 == 0

    w = {}
    halves = {n: b.reshape(N_CHIPS, 2, b.shape[1] // 2, b.shape[2]) for n, b in bufs.items()}

    def publish(names, arrays):
        for name, g in zip(names, arrays):
            g = g.reshape(N_CHIPS, 2 * g.shape[2], g.shape[3])
            if name in ROW_SHARDED:
                g = g.reshape(N_CHIPS * g.shape[1], g.shape[2])
            elif name == "w_in":
                g = jnp.transpose(g, (1, 0, 2)).reshape(g.shape[1], N_CHIPS * g.shape[2])
            w[name] = g

    class GatherPipe:
        def __init__(self, names):
            self.names = names
            self.stage = None

        def ici(self):
            self.stage = _gather_ici(self.bufs())
            return self.stage

        def d2d(self):
            self.stage = _gather_d2d(self.bufs())
            return self.stage

        def bufs(self):
            return self.stage.results if self.stage is not None else [halves[n] for n in self.names]

        def publish(self):
            publish(self.names, self.stage.results)

    class GradPipe:
        def __init__(self, names):
            self.names = names

        def exchange(self, grads):
            self.grads = [_slotted(n, g) for n, g in zip(self.names, grads)]
            self.x = _exchange_halves(self.grads)
            return self.x

        def scatter(self):
            self.sums = [_chip_sum("chip_sum_" + n, g, got, core)
                         for n, g, got in zip(self.names, self.grads, self.x.results)]
            self.s = _scatter_ici(self.sums)
            return self.s

        def forward(self):
            self.f = _scatter_d2d(self.s.results)
            return self.f

        def terms(self):
            return dict(zip(self.names, self.f.results))

    publish(GATHER_FIRST, _all_gather_weights([halves[n] for n in GATHER_FIRST]))
    g_in, g_proj, g_ple = GatherPipe(["w_in", "w_gate"]), GatherPipe(["w_proj_a", "w_proj_b", "w_out"]), \
        GatherPipe(["w_ple_gate", "w_ple_proj"])
    g_down2, g_up2 = GatherPipe(["ffn2_w_down"]), GatherPipe(["ffn2_w_gu"])
    n1 = _rms_fwd("ffn1_norm", xt, small["ffn1_norm"])
    h1, un, ffn1_saved = _ffn_fwd("ffn1", xt, n1, w["ffn1_w_gu"], w["ffn1_w_down"], small["mix_norm"],
                                  {"up": lambda: [g_in.ici()], "down": lambda: [g_in.d2d(), g_proj.ici()]})
    g_in.publish()
    w_in, wgate = w["w_in"], w["w_gate"]
    (qkv,) = _mm(
        "qkv", "nn", (nt, 2, 1),
        [(un, row, w_in, pl.BlockSpec((d, tin), lambda i, j, k: (0, j)))], [],
        [(_sds((t, ncols), BF16), pl.BlockSpec((tm, tin), lambda i, j, k: (i, j)))], (tm, tin), _cast_epilogue,
        j_outer=True, comms=[g_proj.d2d(), g_ple.ici()])
    g_proj.publish()
    wpa, wpb, wout = w["w_proj_a"], w["w_proj_b"], w["w_out"]

    def gate_epilogue(accs, extras, outs, ij):
        outs[0][...] = jax.nn.sigmoid(accs[0]).astype(BF16)

    (gates,) = _mm(
        "gate", "nn", (nt, 4, 1),
        [(un, row, wgate, pl.BlockSpec((None, d, gs), lambda i, j, k: (j, 0, 0)))], [],
        [(_sds((2, t, d), BF16), pl.BlockSpec((None, tm, gs), lambda i, j, k: (j // 2, i, j % 2)))],
        (tm, gs), gate_epilogue, j_outer=True, chunked=True, comms=[g_ple.d2d(), g_down2.ici()])
    g_ple.publish()
    wpg, wpe = w["w_ple_gate"], w["w_ple_proj"]

    bias_a = _pair_bias(_bias_a(small["a_rel_bias"][0]))
    bias_b = _pair_bias(_bias_b())
    sink_a = _pair_rows(jnp.full((N_HEADS, 128), NEG_INF, F32))
    sink_b = _pair_rows(jnp.broadcast_to(small["b_sinks"][0][:, None], (N_HEADS, 128)))
    gqa, gka, gqb, gkb = [jnp.tile(small[k], (1, 2)) for k in ("a_q_norm", "a_k_norm", "b_q_norm", "b_k_norm")]
    ya, lse_a = _attn_fwd("attn_a_fwd", qkv, bias_a, sink_a, gqa, gka, ATTN_A, n_batch,
                          comms=[g_down2.d2d(), g_up2.ici()])
    g_down2.publish()
    yb, lse_b = _attn_fwd("attn_b_fwd", qkv, bias_b, sink_b, gqb, gkb, ATTN_B, n_batch, comms=[g_up2.d2d()])
    g_up2.publish()

    def merge_epilogue(accs, extras, outs, ij):
        pa, pb = accs
        outs[0][...] = (extras[0][...].astype(F32) * pa + extras[1][...].astype(F32) * pb).astype(BF16)
        outs[1][...] = pa.astype(BF16)
        outs[2][...] = pb.astype(BF16)

    y_spec = pl.BlockSpec((tm, A_WIDTH), lambda i, j, k: (i, 0))
    proj_spec = pl.BlockSpec((None, A_WIDTH, ps), lambda i, j, k: (j, 0, 0))
    tile_ps = pl.BlockSpec((tm, ps), lambda i, j, k: (i, j))
    merged, pa, pb = _mm(
        "proj_merge", "nn", (nt, 4, 1),
        [(ya, y_spec, wpa, proj_spec), (yb, y_spec, wpb, proj_spec)],
        [(gates, pl.BlockSpec((None, tm, ps), lambda i, j, k: (0, i, j))),
         (gates, pl.BlockSpec((None, tm, ps), lambda i, j, k: (1, i, j)))],
        [(_sds((t, d), BF16), tile_ps)] * 3, (tm, ps), merge_epilogue)

    h2, n2 = _mm(
        "out_proj", "nn", (nt, 1, 1),
        [(merged, row, wout, pl.BlockSpec((d, d), lambda i, j, k: (0, 0)))],
        [(h1, row), (small["ffn2_norm"], pl.BlockSpec((1, d), lambda i, j, k: (0, 0)))],
        [(_sds((t, d), F32), row), (_sds((t, d), BF16), row)], (tm, d), _residual_norm_epilogue(1.0))

    h3, n3, ffn2_saved = _ffn_fwd("ffn2", h2, n2, w["ffn2_w_gu"], w["ffn2_w_down"], small["ple_norm"], {})
    tile_es = pl.BlockSpec((tm, es), lambda i, j, k: (i, j))
    th = _pick(d, 512)

    def head_epilogue(accs, extras, outs, ij):
        h3_ref, tgt_ref = extras
        dy_ref, dpe_ref, dz_ref, loss_ref = outs
        pg = jax.nn.sigmoid(accs[0])
        pev = accs[1]
        diff = h3_ref[...] + pg * pev - tgt_ref[...]
        dy = diff * (1.0 / d)
        dy_ref[...] = dy
        dpe_ref[...] = (dy * pg).astype(BF16)
        dz_ref[...] = (dy * pev * pg * (1.0 - pg)).astype(BF16)
        _accumulate(loss_ref, jnp.full(loss_ref.shape, jnp.sum(diff * diff), F32), (ij[0] == 0) & (ij[1] == 0))

    tile_h = pl.BlockSpec((tm, th), lambda i, j, k: (i, j))
    dy, dpe, dz, loss_acc = _mm(
        "ple_gate_loss", "nn", (nt, 4, 1),
        [(n3, row, wpg, pl.BlockSpec((d, es), lambda i, j, k: (0, j))),
         (pt, pl.BlockSpec((tm, pdim), lambda i, j, k: (i, 0)), wpe, pl.BlockSpec((None, pdim, es), lambda i, j, k: (j, 0, 0)))],
        [(h3, tile_es), (tgt, tile_es)],
        [(_sds((t, d), F32), tile_es), (_sds((t, d), BF16), tile_es), (_sds((t, d), BF16), tile_es),
         (_sds((8, 128), F32), pl.BlockSpec((8, 128), lambda i, j, k: (0, 0)))],
        (tm, es), head_epilogue, j_outer=True, chunked=True)
    loss = 0.5 * loss_acc[0, 0] / d

    nk = t // tk
    (dwpe,) = _mm(
        "d_w_ple_proj", "tn", (1, 4, nk),
        [(pt, pl.BlockSpec((tk, pdim), lambda i, j, k: (k, 0)), dpe, pl.BlockSpec((tk, es), lambda i, j, k: (k, j)))],
        [], [(_sds((4, pdim, es), BF16), pl.BlockSpec((None, pdim, es), lambda i, j, k: (j, 0, 0)))],
        (pdim, es), _cast_epilogue)

    def dense_grad(name, a, dyb, comms=()):
        (res,) = _mm(
            name, "tn", (1, d // th, nk),
            [(a, pl.BlockSpec((tk, d), lambda i, j, k: (k, 0)), dyb, pl.BlockSpec((tk, th), lambda i, j, k: (k, j)))],
            [], [(_sds((d, d), BF16), pl.BlockSpec((d, th), lambda i, j, k: (0, j)))], (d, th), _cast_epilogue,
            comms=comms)
        return res

    dwpg = dense_grad("d_w_ple_gate", n3, dz)
    tmn = _pick(t, ROW_TILE, 8)
    extras, outs = _rms_bwd_io(h3, small["ple_norm"], dy, tmn)
    dh3, dh3_b, d_ple_norm = _mm(
        "d_ple_norm", "nt", (t // tmn, 1, 1),
        [(dz, pl.BlockSpec((tmn, d), lambda i, j, k: (i, 0)), wpg, pl.BlockSpec((d, d), lambda i, j, k: (0, 0)))],
        extras, outs, (tmn, d), _rms_bwd_epilogue)

    up2, down2, ple = GradPipe(["ffn2_w_gu"]), GradPipe(["ffn2_w_down"]), GradPipe(["w_ple_gate", "w_ple_proj"])
    proj = GradPipe(["w_proj_a", "w_proj_b", "w_out"])
    dh2, dh2_b, d_ffn2_norm, dwgu2, dwd2 = _ffn_bwd(
        "ffn2", dh3, dh3_b, h2, small["ffn2_norm"], w["ffn2_w_gu"], w["ffn2_w_down"], ffn2_saved,
        {"dnorm": lambda dwgu, dwd: [up2.exchange([dwgu]), down2.exchange([dwd]), ple.exchange([dwpg, dwpe])]})

    def dmerge_epilogue(accs, extras, outs, ij):
        dmo = accs[0]
        g_ref, pa_ref, pb_ref = extras
        dg_ref, dpa_ref, dpb_ref = outs
        ga = g_ref[0].astype(F32)
        gb = g_ref[1].astype(F32)
        dg_ref[0] = (dmo * pa_ref[...].astype(F32) * ga * (1.0 - ga)).astype(BF16)
        dg_ref[1] = (dmo * pb_ref[...].astype(F32) * gb * (1.0 - gb)).astype(BF16)
        dpa_ref[...] = (dmo * ga).astype(BF16)
        dpb_ref[...] = (dmo * gb).astype(BF16)

    g_spec = pl.BlockSpec((2, tm, th), lambda i, j, k: (0, i, j))
    dgates, dpa, dpb = _mm(
        "d_merge", "nt", (nt, d // th, 1),
        [(dh2_b, row, wout, pl.BlockSpec((th, d), lambda i, j, k: (j, 0)))],
        [(gates, g_spec), (pa, tile_h), (pb, tile_h)],
        [(_sds((2, t, d), BF16), g_spec), (_sds((t, d), BF16), tile_h), (_sds((t, d), BF16), tile_h)],
        (tm, th), dmerge_epilogue, j_outer=True, chunked=True, comms=[down2.scatter()])
    dwout = dense_grad("d_w_out", merged, dh2_b, comms=[down2.forward(), ple.scatter()])

    yk_spec = pl.BlockSpec((tk, A_WIDTH), lambda i, j, k: (k, 0))
    dk_spec = pl.BlockSpec((tk, ps), lambda i, j, k: (k, j))
    dproj = (_sds((4, A_WIDTH, ps), BF16), proj_spec)
    dwpa, dwpb = _mm(
        "d_w_proj", "tn", (1, 4, nk),
        [(ya, yk_spec, dpa, dk_spec), (yb, yk_spec, dpb, dk_spec)], [], [dproj, dproj], (A_WIDTH, ps), _cast_epilogue,
        comms=[ple.forward()])
    dproj_a = pl.BlockSpec((tm, ps), lambda i, j, k: (i, k))
    wproj_k = pl.BlockSpec((None, A_WIDTH, ps), lambda i, j, k: (k, 0, 0))
    dya, dyb = _mm(
        "d_attn_out", "nt", (nt, 1, 4),
        [(dpa, dproj_a, wpa, wproj_k), (dpb, dproj_a, wpb, wproj_k)], [],
        [(_sds((t, A_WIDTH), BF16), y_spec)] * 2, (tm, A_WIDTH), _cast_epilogue,
        comms=[proj.exchange([dwpa, dwpb, dwout])])

    dqa, dka, dva, dbias_a, _, dgqa, dgka = _attn_bwd(
        "attn_a_bwd", qkv, bias_a, sink_a, gqa, gka, ya, dya, lse_a, ATTN_A, n_batch, True,
        comms=[up2.scatter(), proj.scatter()])
    dqb, dkb, dvb, _, dsink_b, dgqb, dgkb = _attn_bwd(
        "attn_b_bwd", qkv, bias_b, sink_b, gqb, gkb, yb, dyb, lse_b, ATTN_B, n_batch, False,
        comms=[up2.forward(), proj.forward()])
    dqkv = jnp.concatenate([dqa, dka, dva, dqb, dkb, dvb], axis=1)

    (dwgate,) = _mm(
        "d_w_gate", "tn", (1, 4, nk),
        [(un, pl.BlockSpec((tk, d), lambda i, j, k: (k, 0)),
          dgates, pl.BlockSpec((None, tk, gs), lambda i, j, k: (j // 2, k, j % 2)))],
        [], [(_sds((4, d, gs), BF16), pl.BlockSpec((None, d, gs), lambda i, j, k: (j, 0, 0)))], (d, gs), _cast_epilogue)
    (dwin,) = _mm(
        "d_w_in", "tn", (1, 2, nk),
        [(un, pl.BlockSpec((tk, d), lambda i, j, k: (k, 0)), dqkv, pl.BlockSpec((tk, tin), lambda i, j, k: (k, j)))],
        [], [(_sds((d, ncols), BF16), pl.BlockSpec((d, tin), lambda i, j, k: (0, j)))], (d, tin), _cast_epilogue)

    mixer = GradPipe(["w_in", "w_gate"])
    extras, outs = _rms_bwd_io(h1, small["mix_norm"], dh2, tmn)
    dh1, dh1_b, d_mix_norm = _mm(
        "d_mix_norm", "nt", (t // tmn, 1, 6),
        [(dgates, pl.BlockSpec((None, tmn, gs), lambda i, j, k: (jnp.minimum(k, 3) // 2, i, jnp.minimum(k, 3) % 2)),
          wgate, pl.BlockSpec((None, d, gs), lambda i, j, k: (jnp.minimum(k, 3), 0, 0))),
         (dqkv, pl.BlockSpec((tmn, tin), lambda i, j, k: (i, jnp.maximum(k - 4, 0))),
          w_in, pl.BlockSpec((d, tin), lambda i, j, k: (0, jnp.maximum(k - 4, 0))))],
        extras, outs, (tmn, d), _rms_bwd_epilogue, steps=[4, 2],
        comms=[mixer.exchange([dwin, dwgate])])

    up1 = GradPipe(["ffn1_w_gu"])
    down1 = GradPipe(["ffn1_w_down"])
    dx, _, d_ffn1_norm, _, _ = _ffn_bwd(
        "ffn1", dh1, dh1_b, xt, small["ffn1_norm"], w["ffn1_w_gu"], w["ffn1_w_down"], ffn1_saved,
        {"dwgu": lambda: [mixer.scatter()],
         "dwd": lambda dwgu: [mixer.forward(), up1.exchange([dwgu])],
         "dnorm": lambda dwgu, dwd: [up1.scatter(), down1.exchange([dwd])]})
    terms = {}
    for pipe in (up2, down2, ple, proj, mixer):
        terms.update(pipe.terms())
    tail = [lambda: [up1.forward(), down1.scatter()], lambda: [down1.forward()]]

    def tail_terms():
        return {**up1.terms(), **down1.terms()}

    def fold(v):
        return v[0, :HEAD_DIM] + v[0, HEAD_DIM:]

    small_grads = {"ffn1_norm": d_ffn1_norm, "mix_norm": d_mix_norm, "ffn2_norm": d_ffn2_norm,
                   "ple_norm": d_ple_norm, "a_q_norm": fold(dgqa), "a_k_norm": fold(dgka),
                   "b_q_norm": fold(dgqb), "b_k_norm": fold(dgkb), "a_rel_bias": _rel_bias_grad(_unpair_bias(dbias_a)),
                   "b_sinks": jnp.sum(dsink_b, axis=1)}
    return loss, dx, terms, small_grads, tail, tail_terms


def kernel(x, p, ffn1_norm, ffn1_w_gu, ffn1_w_down, mix_norm, w_in, a_q_norm, a_k_norm, a_rel_bias, b_q_norm, b_k_norm, b_sinks, w_gate, w_proj_a, w_proj_b, w_out, ffn2_norm, ffn2_w_gu, ffn2_w_down, ple_norm, w_ple_gate, w_ple_proj, loss_target, m_ffn1_norm, m_ffn1_w_gu, m_ffn1_w_down, m_mix_norm, m_w_in, m_a_q_norm, m_a_k_norm, m_a_rel_bias, m_b_q_norm, m_b_k_norm, m_b_sinks, m_w_gate, m_w_proj_a, m_w_proj_b, m_w_out, m_ffn2_norm, m_ffn2_w_gu, m_ffn2_w_down, m_ple_norm, m_w_ple_gate, m_w_ple_proj, v_ffn1_norm, v_ffn1_w_gu, v_ffn1_w_down, v_mix_norm, v_w_in, v_a_q_norm, v_a_k_norm, v_a_rel_bias, v_b_q_norm, v_b_k_norm, v_b_sinks, v_w_gate, v_w_proj_a, v_w_proj_b, v_w_out, v_ffn2_norm, v_ffn2_w_gu, v_ffn2_w_down, v_ple_norm, v_w_ple_gate, v_w_ple_proj):
    given = dict(locals())
    n_batch, s, d = x.shape
    t = n_batch * s
    xt = x.reshape(t, d)
    pt = p.reshape(t, p.shape[-1])
    tgt = loss_target.reshape(t, d)

    chip = (2 * lax.axis_index("x") + lax.axis_index("y")).astype(jnp.int32).reshape(1)
    bufs = {name: _cast_into_slot("cast_" + name, given[name][0], chip) for name in BIG}
    small = {name: given[name] for name in SMALL}
    core = lax.axis_index("c").astype(jnp.int32).reshape(1)
    loss, dx, terms, small_grads, tail, tail_terms = _local_step(xt, pt, tgt, n_batch, bufs, small, core)

    grads, deltas, new_m, new_v = {}, {}, {}, {}
    for name in TAIL_HOSTS + tuple(n for n in BIG if n not in TAIL_HOSTS):
        if name not in terms:
            terms.update(tail_terms())
        comms = tail.pop(0)() if tail else ()
        gw, dl, nm, nv = _adamw_terms("adamw_" + name, terms[name], given[name][0], given["m_" + name][0],
                                      given["v_" + name][0], comms)
        grads[name], deltas[name], new_m[name], new_v[name] = gw[None], dl[None], nm[None], nv[None]

    small_shapes = [given[name].shape for name in SMALL] + [()]
    g_pack = _all_reduce_small(_pack([small_grads[name] for name in SMALL] + [loss]))
    zero = jnp.zeros((), F32)
    w_pack = _pack([given[name] for name in SMALL] + [zero])
    m_pack = _pack([given["m_" + name] for name in SMALL] + [zero])
    v_pack = _pack([given["v_" + name] for name in SMALL] + [zero])
    d_pack, nm_pack, nv_pack = _ew("adamw_small", lambda wv, gv, mv, vv: _adamw_math(wv, gv, mv, vv),
                                   [w_pack, g_pack, m_pack, v_pack], [F32] * 3)
    g_small = _unpack(g_pack, small_shapes)
    loss_total = g_small[-1]
    for name, gv, dv, mv, vv in zip(SMALL, g_small, _unpack(d_pack, small_shapes), _unpack(nm_pack, small_shapes),
                                    _unpack(nv_pack, small_shapes)):
        grads[name], deltas[name], new_m[name], new_v[name] = gv, dv, mv, vv

    return (loss_total, dx.reshape(x.shape), *[grads[n] for n in WEIGHTS], *[deltas[n] for n in WEIGHTS],
            *[new_m[n] for n in WEIGHTS], *[new_v[n] for n in WEIGHTS])
```

```python
import functools

import numpy as np
import jax
import jax.numpy as jnp
from jax import lax
from jax.experimental import pallas as pl
from jax.experimental.pallas import tpu as pltpu

F32 = jnp.float32
BF16 = jnp.bfloat16

CHUNK = 64
HEAD_DIM = 64
A_PREV_CHUNKS = 8
A_MAX_REL = 128
N_HEADS = 8
B_KV_HEADS = 2
B_PREV_CHUNKS = 2
A_WIDTH = N_HEADS * HEAD_DIM
B_KV_WIDTH = B_KV_HEADS * HEAD_DIM
EPS = 1e-6
NEG_INF = -1e30
ATTN_SCALE = HEAD_DIM ** -0.5
Q_BLOCK = 128
PAIR = 2 * HEAD_DIM

ADAM_LR = 0.001
ADAM_B1 = 0.9
ADAM_B2 = 0.999
ADAM_EPS = 1e-08
ADAM_WD = 0.01
ADAM_STEP = 10

N_CHIPS = 4
N_DEV = 8
VMEM_LIMIT_V7X = 56 * 1024 * 1024
ROW_TILE = 1024
MESH = pl.DeviceIdType.MESH
COLLECTIVE_IDS = {("sibling",): 1, ("chips",): 2, ("chips", "sibling"): 3}
ANY = pl.BlockSpec(memory_space=pl.ANY)

_DN = {
    "nn": (((1,), (0,)), ((), ())),
    "nt": (((1,), (1,)), ((), ())),
    "tn": (((0,), (0,)), ((), ())),
}


def _pick(n, target, mult=128):
    best = None
    for d in range(mult, min(n, target) + 1, mult):
        if n % d == 0:
            best = d
    return n if best is None else best


def _dot(a, b, mode):
    return lax.dot_general(a.astype(BF16), b.astype(BF16), _DN[mode], preferred_element_type=F32)


def _params():
    return pltpu.CompilerParams(vmem_limit_bytes=VMEM_LIMIT_V7X)


class _Comm:
    def __init__(self, peers, ins, outs, aliases, sems, start, finish):
        self.peers = peers
        self.ins, self.outs, self.aliases, self.sems = list(ins), list(outs), dict(aliases), list(sems)
        self.start, self.finish = start, finish
        self.results = None


class _CommPlumbing:
    def __init__(self, comms, n_in, n_out, n_scratch):
        self.comms = list(comms)
        self.n_in, self.n_out, self.n_scratch = n_in, n_out, n_scratch
        self.args = [a for cm in self.comms for a in cm.ins]
        self.out_shape = [o for cm in self.comms for o in cm.outs]
        self.scratch = [s for cm in self.comms for s in cm.sems]
        self.aliases = {}
        i0, o0 = n_in, n_out
        for cm in self.comms:
            for a, b in cm.aliases.items():
                self.aliases[i0 + a] = o0 + b
            i0 += len(cm.ins)
            o0 += len(cm.outs)

    def _parts(self, in_refs, out_refs, scratch_refs):
        parts = []
        i0, o0, s0 = self.n_in, self.n_out, self.n_scratch
        for cm in self.comms:
            parts.append((in_refs[i0:i0 + len(cm.ins)], out_refs[o0:o0 + len(cm.outs)],
                          scratch_refs[s0:s0 + len(cm.sems)]))
            i0 += len(cm.ins)
            o0 += len(cm.outs)
            s0 += len(cm.sems)
        return parts

    def kinds(self):
        return sorted(set(kind for cm in self.comms for kind in cm.peers))

    def params(self, **kwargs):
        if self.comms:
            kwargs["collective_id"] = COLLECTIVE_IDS[tuple(self.kinds())]
        return pltpu.CompilerParams(**kwargs)

    def handshake(self):
        x, y, c, chips = _mesh_place()
        peers = []
        if "sibling" in self.kinds():
            peers.append((x, y, 1 - c))
        if "chips" in self.kinds():
            peers += [(tx, ty, c) for tx, ty in chips]
        barrier = pltpu.get_barrier_semaphore()
        for peer in peers:
            pl.semaphore_signal(barrier, inc=1, device_id=peer, device_id_type=MESH)
        pl.semaphore_wait(barrier, len(peers))

    def start_at(self, in_refs, out_refs, scratch_refs, first):
        if self.comms:
            parts = self._parts(in_refs, out_refs, scratch_refs)

            @pl.when(first)
            def _():
                self.handshake()
                for cm, part in zip(self.comms, parts):
                    cm.start(*part)

    def finish_at(self, in_refs, out_refs, scratch_refs, last):
        if self.comms:
            parts = self._parts(in_refs, out_refs, scratch_refs)

            @pl.when(last)
            def _():
                for cm, part in zip(self.comms, parts):
                    cm.finish(*part)

    def deliver(self, results):
        o0 = self.n_out
        for cm in self.comms:
            cm.results = list(results[o0:o0 + len(cm.outs)])
            o0 += len(cm.outs)
        return list(results[:self.n_out])


def _swap_ij(spec):
    index_map = spec.index_map
    return pl.BlockSpec(spec.block_shape, lambda j, i, k: index_map(i, j, k))


MXU_COLUMNS_V7X = 256


def _mm(name, mode, grid, pairs, extras, outs, acc_shape, epilogue, steps=None, comms=(), j_outer=False,
        chunked=False):
    ni, nj, nk = grid
    n_in = 2 * len(pairs) + len(extras)
    n_out = len(outs)
    tn = acc_shape[1]
    col_chunks = None
    if chunked:
        assert nk == 1 and steps is None and mode in ("nn", "nt")
        col_chunks = [(c0, min(MXU_COLUMNS_V7X, tn - c0)) for c0 in range(0, tn, MXU_COLUMNS_V7X)]
    n_acc = 0 if chunked else (len(pairs) if steps is None else 1)
    plumb = _CommPlumbing(comms, n_in, n_out, n_acc)
    n_all_in = n_in + len(plumb.args)
    n_all_out = n_out + len(plumb.out_shape)
    if j_outer:
        grid = (nj, ni, nk)
        pairs = [(a, _swap_ij(a_spec), b, _swap_ij(b_spec)) for a, a_spec, b, b_spec in pairs]
        extras = [(e, _swap_ij(e_spec)) for e, e_spec in extras]
        outs = [(o, _swap_ij(o_spec)) for o, o_spec in outs]

    def body(*refs):
        in_refs = refs[:n_all_in]
        out_refs = refs[n_all_in:n_all_in + n_all_out]
        scratch = refs[n_all_in + n_all_out:]
        accs = scratch[:n_acc]
        i = pl.program_id(1 if j_outer else 0)
        j = pl.program_id(0 if j_outer else 1)
        k = pl.program_id(2)
        plumb.start_at(in_refs, out_refs, scratch, (i == 0) & (j == 0) & (k == 0))

        def contrib(p, acc):
            acc[...] += _dot(in_refs[2 * p][...], in_refs[2 * p + 1][...], mode)

        if col_chunks:
            def cols(ref, c0, cs):
                if ref.shape[-1] != tn:
                    return ref
                return ref.at[(slice(None),) * (len(ref.shape) - 1) + (pl.ds(c0, cs),)]

            lhs = [in_refs[2 * p][...] for p in range(len(pairs))]
            for ci, (c0, cs) in enumerate(col_chunks):
                vals = []
                for p in range(len(pairs)):
                    b_ref = in_refs[2 * p + 1]
                    rhs = b_ref[:, c0:c0 + cs] if mode == "nn" else b_ref[c0:c0 + cs, :]
                    vals.append(_dot(lhs[p], rhs, mode))
                epilogue(vals, [cols(r, c0, cs) for r in in_refs[2 * len(pairs):n_in]],
                         [cols(r, c0, cs) for r in out_refs[:n_out]], (i, j * len(col_chunks) + ci))
        else:
            @pl.when(k == 0)
            def _():
                for acc in accs:
                    acc[...] = jnp.zeros(acc.shape, F32)

            if steps is None:
                for p in range(len(pairs)):
                    contrib(p, accs[p])
            else:
                lo = 0
                for p, n in enumerate(steps):
                    pl.when((k >= lo) & (k < lo + n))(functools.partial(contrib, p, accs[0]))
                    lo += n

            @pl.when(k == nk - 1)
            def _():
                epilogue([acc[...] for acc in accs], in_refs[2 * len(pairs):n_in], out_refs[:n_out], (i, j))

        plumb.finish_at(in_refs, out_refs, scratch, (i == ni - 1) & (j == nj - 1) & (k == nk - 1))

    args, in_specs = [], []
    for a, a_spec, b, b_spec in pairs:
        args += [a, b]
        in_specs += [a_spec, b_spec]
    for e, e_spec in extras:
        args.append(e)
        in_specs.append(e_spec)
    res = pl.pallas_call(
        body,
        name=name,
        grid=grid,
        in_specs=in_specs + [ANY] * len(plumb.args),
        out_specs=[s for _, s in outs] + [ANY] * len(plumb.out_shape),
        out_shape=[o for o, _ in outs] + plumb.out_shape,
        scratch_shapes=[pltpu.VMEM(acc_shape, F32) for _ in range(n_acc)] + plumb.scratch,
        input_output_aliases=plumb.aliases,
        compiler_params=plumb.params(vmem_limit_bytes=VMEM_LIMIT_V7X),
    )(*args, *plumb.args)
    return plumb.deliver(res)


def _sds(shape, dtype):
    return jax.ShapeDtypeStruct(shape, dtype)


def _accumulate(ref, value, first):
    @pl.when(first)
    def _():
        ref[...] = value

    @pl.when(jnp.logical_not(first))
    def _():
        ref[...] += value


def _rms_fwd(name, x, gain, comms=()):
    t, d = x.shape
    tm = _pick(t, ROW_TILE, 8)
    steps = t // tm
    plumb = _CommPlumbing(comms, 2, 1, 0)
    n_all_in = 2 + len(plumb.args)
    n_all_out = 1 + len(plumb.out_shape)

    def body(*refs):
        x_ref, g_ref = refs[:2]
        y_ref = refs[n_all_in]
        comm_refs = (refs[:n_all_in], refs[n_all_in:n_all_in + n_all_out], refs[n_all_in + n_all_out:])
        i = pl.program_id(0)
        plumb.start_at(*comm_refs, i == 0)
        xv = x_ref[...]
        rstd = lax.rsqrt(jnp.mean(xv * xv, axis=-1, keepdims=True) + EPS)
        y_ref[...] = (xv * rstd * g_ref[...]).astype(BF16)
        plumb.finish_at(*comm_refs, i == steps - 1)

    res = pl.pallas_call(
        body, name=name, grid=(steps,),
        in_specs=[pl.BlockSpec((tm, d), lambda i: (i, 0)), pl.BlockSpec((1, d), lambda i: (0, 0))]
        + [ANY] * len(plumb.args),
        out_specs=[pl.BlockSpec((tm, d), lambda i: (i, 0))] + [ANY] * len(plumb.out_shape),
        out_shape=[_sds((t, d), BF16)] + plumb.out_shape,
        scratch_shapes=plumb.scratch,
        input_output_aliases=plumb.aliases,
        compiler_params=plumb.params(vmem_limit_bytes=VMEM_LIMIT_V7X),
    )(x, gain, *plumb.args)
    return plumb.deliver(res)[0]


def _rms_bwd_epilogue(accs, extras, outs, ij):
    x_ref, g_ref, r_ref = extras
    dh_ref, dhb_ref, dg_ref = outs
    dn = accs[0]
    xv = x_ref[...]
    rstd = lax.rsqrt(jnp.mean(xv * xv, axis=-1, keepdims=True) + EPS)
    xhat = xv * rstd
    gd = dn * g_ref[...]
    dx = rstd * (gd - xhat * jnp.mean(gd * xhat, axis=-1, keepdims=True))
    dh = r_ref[...] + dx
    dh_ref[...] = dh
    dhb_ref[...] = dh.astype(BF16)
    _accumulate(dg_ref, jnp.sum(dn * xhat, axis=0, keepdims=True), ij[0] == 0)


def _rms_bwd_io(x, gain, dres, tm):
    t, d = x.shape
    row = pl.BlockSpec((tm, d), lambda i, j, k: (i, 0))
    extras = [(x, row), (gain, pl.BlockSpec((1, d), lambda i, j, k: (0, 0))), (dres, row)]
    outs = [(_sds((t, d), F32), row), (_sds((t, d), BF16), row),
            (_sds((1, d), F32), pl.BlockSpec((1, d), lambda i, j, k: (0, 0)))]
    return extras, outs


def _residual_norm_epilogue(scale):
    def epilogue(accs, extras, outs, ij):
        hv = extras[0][...] + scale * accs[0]
        outs[0][...] = hv
        rstd = lax.rsqrt(jnp.mean(hv * hv, axis=-1, keepdims=True) + EPS)
        outs[1][...] = (hv * rstd * extras[1][...]).astype(BF16)
    return epilogue


def _ffn_fwd(tag, h, n, wgu, wd, next_gain, hooks):
    t, d = h.shape
    fs = wgu.shape[2]
    f = 2 * fs
    tm = _pick(t, ROW_TILE, 8)

    def up_epilogue(accs, extras, outs, ij):
        g, u = accs
        gu_ref, a_ref = outs
        gu_ref[0] = g.astype(BF16)
        gu_ref[1] = u.astype(BF16)
        a_ref[...] = (g * jax.nn.sigmoid(g) * u).astype(BF16)

    a_spec = pl.BlockSpec((tm, d), lambda i, j, k: (i, 0))
    gu, a = _mm(
        tag + "_up", "nn", (t // tm, 2, 1),
        [(n, a_spec, wgu, pl.BlockSpec((None, d, fs), lambda i, j, k: (j, 0, 0))),
         (n, a_spec, wgu, pl.BlockSpec((None, d, fs), lambda i, j, k: (j + 2, 0, 0)))],
        [],
        [(_sds((2, t, f), BF16), pl.BlockSpec((2, tm, fs), lambda i, j, k: (0, i, j))),
         (_sds((t, f), BF16), pl.BlockSpec((tm, fs), lambda i, j, k: (i, j)))],
        (tm, fs), up_epilogue, comms=hooks.get("up", lambda: ())(), j_outer=True, chunked=True)

    row = pl.BlockSpec((tm, d), lambda i, j, k: (i, 0))
    h_new, n_new = _mm(
        tag + "_down", "nn", (t // tm, 1, 1),
        [(a, pl.BlockSpec((tm, f), lambda i, j, k: (i, 0)), wd, pl.BlockSpec((f, d), lambda i, j, k: (0, 0)))],
        [(h, row), (next_gain, pl.BlockSpec((1, d), lambda i, j, k: (0, 0)))],
        [(_sds((t, d), F32), row), (_sds((t, d), BF16), row)], (tm, d), _residual_norm_epilogue(0.5),
        comms=hooks.get("down", lambda: ())())
    return h_new, n_new, (n, gu, a)


def _ffn_bwd(tag, dh, dh_b, h, gain, wgu, wd, saved, hooks):
    n, gu, a = saved
    t, d = h.shape
    fs = wgu.shape[2]
    f = 2 * fs
    tm = _pick(t, ROW_TILE, 8)
    tk = _pick(t, ROW_TILE, 8)

    def dact_epilogue(accs, extras, outs, ij):
        da = 0.5 * accs[0]
        g = extras[0][0].astype(F32)
        u = extras[0][1].astype(F32)
        sg = jax.nn.sigmoid(g)
        outs[0][0] = (da * u * sg * (1.0 + g * (1.0 - sg))).astype(BF16)
        outs[0][1] = (da * g * sg).astype(BF16)

    gu_spec = pl.BlockSpec((2, tm, fs), lambda i, j, k: (0, i, j))
    (dgu,) = _mm(
        tag + "_dact", "nt", (t // tm, 2, 1),
        [(dh_b, pl.BlockSpec((tm, d), lambda i, j, k: (i, 0)), wd, pl.BlockSpec((fs, d), lambda i, j, k: (j, 0)))],
        [(gu, gu_spec)], [(_sds((2, t, f), BF16), gu_spec)], (tm, fs), dact_epilogue, j_outer=True, chunked=True,
        comms=hooks.get("dact", lambda: ())())

    def cast_epilogue(accs, extras, outs, ij):
        outs[0][...] = accs[0].astype(BF16)

    (dwgu,) = _mm(
        tag + "_dwgu", "tn", (1, 4, t // tk),
        [(n, pl.BlockSpec((tk, d), lambda i, j, k: (k, 0)),
          dgu, pl.BlockSpec((None, tk, fs), lambda i, j, k: (j // 2, k, j % 2)))],
        [], [(_sds((4, d, fs), BF16), pl.BlockSpec((None, d, fs), lambda i, j, k: (j, 0, 0)))], (d, fs), cast_epilogue,
        comms=hooks.get("dwgu", lambda: ())())

    def half_epilogue(accs, extras, outs, ij):
        outs[0][...] = (0.5 * accs[0]).astype(BF16)

    (dwd,) = _mm(
        tag + "_dwd", "tn", (2, 1, t // tk),
        [(a, pl.BlockSpec((tk, fs), lambda i, j, k: (k, i)), dh_b, pl.BlockSpec((tk, d), lambda i, j, k: (k, 0)))],
        [], [(_sds((f, d), BF16), pl.BlockSpec((fs, d), lambda i, j, k: (i, 0)))], (fs, d), half_epilogue,
        comms=hooks.get("dwd", lambda g: ())(dwgu))

    tmn = _pick(t, ROW_TILE, 8)
    extras, outs = _rms_bwd_io(h, gain, dh, tmn)
    dh_in, dh_in_b, dgain = _mm(
        tag + "_dnorm", "nt", (t // tmn, 1, 4),
        [(dgu, pl.BlockSpec((None, tmn, fs), lambda i, j, k: (k // 2, i, k % 2)),
          wgu, pl.BlockSpec((None, d, fs), lambda i, j, k: (k, 0, 0)))],
        extras, outs, (tmn, d), _rms_bwd_epilogue, comms=hooks.get("dnorm", lambda g, w: ())(dwgu, dwd))
    return dh_in, dh_in_b, dgain, dwgu, dwd


def _lane_lo(shape):
    return lax.broadcasted_iota(jnp.int32, shape, 1) < HEAD_DIM


def _pair_norm(xv, gain):
    lo = _lane_lo(xv.shape)
    x2 = xv * xv
    ms_lo = jnp.sum(jnp.where(lo, x2, 0.0), axis=-1, keepdims=True) * (1.0 / HEAD_DIM)
    ms_hi = jnp.sum(jnp.where(lo, 0.0, x2), axis=-1, keepdims=True) * (1.0 / HEAD_DIM)
    rstd = jnp.where(lo, lax.rsqrt(ms_lo + EPS), lax.rsqrt(ms_hi + EPS))
    xhat = xv * rstd
    return xhat * gain, xhat, rstd


def _pair_norm_bwd(dn, xhat, rstd, gain):
    lo = _lane_lo(dn.shape)
    gd = dn * gain
    t = gd * xhat
    m_lo = jnp.sum(jnp.where(lo, t, 0.0), axis=-1, keepdims=True) * (1.0 / HEAD_DIM)
    m_hi = jnp.sum(jnp.where(lo, 0.0, t), axis=-1, keepdims=True) * (1.0 / HEAD_DIM)
    dx = rstd * (gd - xhat * jnp.where(lo, m_lo, m_hi))
    return dx, jnp.sum(dn * xhat, axis=0, keepdims=True)


def _half(xv, hi):
    lo = _lane_lo(xv.shape)
    return jnp.where(lo, 0, xv) if hi else jnp.where(lo, xv, 0)


def _attn_window(i, prev):
    q0 = i * Q_BLOCK
    start = jnp.maximum(q0 - prev, 0)
    off = start - (q0 - prev)
    return pl.multiple_of(start, Q_BLOCK), pl.multiple_of(off, Q_BLOCK)


Q_BLOCKS_PER_STEP = 4
STEP_ROWS = Q_BLOCKS_PER_STEP * Q_BLOCK


def _attn_specs(cfg, s, steps):
    kw = cfg["kw"]
    q_spec = pl.BlockSpec((STEP_ROWS, A_WIDTH), lambda b, i: (b * steps + i, cfg["qblk"]))
    k_spec = pl.BlockSpec((s, kw), lambda b, i: (b, cfg["kblk"]))
    v_spec = pl.BlockSpec((s, kw), lambda b, i: (b, cfg["vblk"]))
    return q_spec, k_spec, v_spec


def _const_spec(shape):
    return pl.BlockSpec(shape, lambda b, i: (0,) * len(shape))


KEY_CHUNK = 128


def _pair_bias(bias_t):
    wext = bias_t.shape[1]
    return jnp.transpose(bias_t.reshape(N_HEADS // 2, 2, wext, Q_BLOCK), (0, 2, 1, 3)).reshape(
        N_HEADS // 2, wext, 2 * Q_BLOCK)


def _unpair_bias(db2):
    wext = db2.shape[1]
    return jnp.transpose(db2.reshape(N_HEADS // 2, wext, 2, Q_BLOCK), (0, 2, 1, 3)).reshape(N_HEADS, wext, Q_BLOCK)


def _pair_rows(rows):
    two = rows.reshape(N_HEADS // 2, 2 * rows.shape[1])
    return jnp.broadcast_to(two[:, None, :], (N_HEADS // 2, 8, two.shape[1]))


def _sub_lo(shape):
    return lax.broadcasted_iota(jnp.int32, shape, 0) < HEAD_DIM


def _by_half(lo_row, hi_row, rows):
    return jnp.where(_sub_lo((rows, lo_row.shape[1])), lo_row, hi_row)


def _stack_pair(xn, jq, group):
    parts = []
    for hq in range(2):
        hk = ((2 * jq + hq) // group) % 2
        xm = _half(xn, hq)
        if hq != hk:
            xm = pltpu.roll(xm, HEAD_DIM, 1)
        parts.append(xm)
    return jnp.concatenate(parts, axis=0).astype(BF16)


def _place_transposed(blk, dst_ref, c, heads, group):
    bt = blk.T
    lo = _sub_lo(bt.shape)
    for h in heads:
        src_hi = ((h // group) % 2) == 1
        part = jnp.where(lo, 0.0, bt) if src_hi else jnp.where(lo, bt, 0.0)
        if src_hi != (h % 2 == 1):
            part = pltpu.roll(part, HEAD_DIM, 0)
        dst_ref[h, c] = part.astype(BF16)


def _attn_fwd(name, qkv, bias2, sink2, gq, gk, cfg, n_batch, comms=()):
    t = qkv.shape[0]
    s = t // n_batch
    steps = s // STEP_ROWS
    nkc = s // KEY_CHUNK
    prev, group, kw = cfg["prev"], cfg["group"], cfg["kw"]
    w = prev + Q_BLOCK
    n_chunks = w // KEY_CHUNK
    wext = bias2.shape[1]
    plumb = _CommPlumbing(comms, 7, 2, 4)
    n_all_in = 7 + len(plumb.args)
    n_all_out = 2 + len(plumb.out_shape)

    def body(*refs):
        q_ref, k_ref, v_ref, bias_ref, sink_ref, gq_ref, gk_ref = refs[:7]
        y_ref, lse_ref = refs[n_all_in:n_all_in + 2]
        kn_ref, vt_ref, s_ref, pst_ref = refs[n_all_in + n_all_out:n_all_in + n_all_out + 4]
        step = pl.program_id(1)
        comm_refs = (refs[:n_all_in], refs[n_all_in:n_all_in + n_all_out], refs[n_all_in + n_all_out:])
        plumb.start_at(*comm_refs, (pl.program_id(0) == 0) & (step == 0))

        @pl.when(step == 0)
        def _():
            for jk in range(kw // PAIR):
                cols = pl.ds(jk * PAIR, PAIR)
                heads = [h for h in range(N_HEADS) if (h // group) // 2 == jk]
                kn, _, _ = _pair_norm(k_ref[:, cols].astype(F32), gk_ref[...])
                kn_ref[:, cols] = kn.astype(BF16)
                for c in range(nkc):
                    _place_transposed(v_ref[pl.ds(c * KEY_CHUNK, KEY_CHUNK), cols].astype(F32), vt_ref, c, heads, group)

        sub8 = lax.broadcasted_iota(jnp.int32, (N_HEADS, Q_BLOCK), 0)
        for sb in range(Q_BLOCKS_PER_STEP):
            qrows = pl.ds(sb * Q_BLOCK, Q_BLOCK)
            start, off = _attn_window(step * Q_BLOCKS_PER_STEP + sb, prev)
            c0 = start // KEY_CHUNK
            lse = jnp.zeros((N_HEADS, Q_BLOCK), F32)
            for jq in range(N_HEADS // 2):
                kcols = pl.ds((((2 * jq) // group) // 2) * PAIR, PAIR)
                qn, _, _ = _pair_norm(q_ref[qrows, pl.ds(jq * PAIR, PAIR)].astype(F32), gq_ref[...])
                qs = _stack_pair(qn * ATTN_SCALE, jq, group)
                s_ref[...] = _dot(kn_ref[pl.ds(start, w), kcols], qs, "nt")
                m = sink_ref[jq, 0:1, :]
                for c in range(n_chunks):
                    r = pl.ds(c * KEY_CHUNK, KEY_CHUNK)
                    s2 = s_ref[r, :] + bias_ref[jq, pl.ds(off + c * KEY_CHUNK, KEY_CHUNK), :]
                    s_ref[r, :] = s2
                    m = jnp.maximum(m, jnp.max(s2, axis=0, keepdims=True))
                l = jnp.exp(sink_ref[jq, 0:1, :] - m)
                for c in range(n_chunks):
                    p = jnp.exp(s_ref[pl.ds(c * KEY_CHUNK, KEY_CHUNK), :] - m)
                    l = l + jnp.sum(p, axis=0, keepdims=True)
                    pst_ref[pl.ds(2 * c * KEY_CHUNK, KEY_CHUNK), :] = p[:, :Q_BLOCK].astype(BF16)
                    pst_ref[pl.ds((2 * c + 1) * KEY_CHUNK, KEY_CHUNK), :] = p[:, Q_BLOCK:].astype(BF16)
                vl = jnp.concatenate([vt_ref[2 * jq + hq, c0 + c] for c in range(n_chunks) for hq in range(2)], axis=1)
                ot = _dot(vl, pst_ref[...], "nn")
                inv = 1.0 / l
                ot = ot * _by_half(inv[:, :Q_BLOCK], inv[:, Q_BLOCK:], PAIR)
                y_ref[qrows, pl.ds(jq * PAIR, PAIR)] = ot.T.astype(BF16)
                lse2 = m + jnp.log(l)
                lse = jnp.where(sub8 == 2 * jq, lse2[:, :Q_BLOCK], lse)
                lse = jnp.where(sub8 == 2 * jq + 1, lse2[:, Q_BLOCK:], lse)
            lse_ref[sb] = lse
        plumb.finish_at(*comm_refs, (pl.program_id(0) == n_batch - 1) & (step == steps - 1))

    q_spec, k_spec, v_spec = _attn_specs(cfg, s, steps)
    res = pl.pallas_call(
        body, name=name, grid=(n_batch, steps),
        in_specs=[q_spec, k_spec, v_spec, _const_spec((N_HEADS // 2, wext, 2 * Q_BLOCK)),
                  _const_spec((N_HEADS // 2, 8, 2 * Q_BLOCK)), _const_spec((1, PAIR)), _const_spec((1, PAIR))]
        + [ANY] * len(plumb.args),
        out_specs=[pl.BlockSpec((STEP_ROWS, A_WIDTH), lambda b, i: (b * steps + i, 0)),
                   pl.BlockSpec((Q_BLOCKS_PER_STEP, N_HEADS, Q_BLOCK), lambda b, i: (b * steps + i, 0, 0))]
        + [ANY] * len(plumb.out_shape),
        out_shape=[_sds((t, A_WIDTH), BF16), _sds((t // Q_BLOCK, N_HEADS, Q_BLOCK), F32)] + plumb.out_shape,
        scratch_shapes=[pltpu.VMEM((s, kw), BF16), pltpu.VMEM((N_HEADS, nkc, PAIR, KEY_CHUNK), BF16),
                        pltpu.VMEM((w, 2 * Q_BLOCK), F32), pltpu.VMEM((2 * w, Q_BLOCK), BF16)] + plumb.scratch,
        input_output_aliases=plumb.aliases,
        compiler_params=plumb.params(vmem_limit_bytes=VMEM_LIMIT_V7X),
    )(qkv, qkv, qkv, bias2, sink2, gq, gk, *plumb.args)
    return plumb.deliver(res)


def _attn_bwd(name, qkv, bias2, sink2, gq, gk, y, dy, lse, cfg, n_batch, want_dbias, comms=()):
    t = qkv.shape[0]
    s = t // n_batch
    steps = s // STEP_ROWS
    nkc = s // KEY_CHUNK
    prev, group, kw = cfg["prev"], cfg["group"], cfg["kw"]
    w = prev + Q_BLOCK
    n_chunks = w // KEY_CHUNK
    wext = bias2.shape[1]
    plumb = _CommPlumbing(comms, 10, 7, 9)
    n_all_in = 10 + len(plumb.args)
    n_all_out = 7 + len(plumb.out_shape)

    def body(*refs):
        q_ref, k_ref, v_ref, bias_ref, sink_ref, gq_ref, gk_ref, y_ref, dy_ref, lse_ref = refs[:10]
        dq_ref, dk_ref, dv_ref, db_ref, dsink_ref, dgq_ref, dgk_ref = refs[n_all_in:n_all_in + 7]
        kn_ref, knt_ref, dkn_ref, dvs_ref, s_ref, dp_ref, pb_ref, dsb_ref, dst_ref = \
            refs[n_all_in + n_all_out:n_all_in + n_all_out + 9]
        b = pl.program_id(0)
        step = pl.program_id(1)
        first = (b == 0) & (step == 0)
        comm_refs = (refs[:n_all_in], refs[n_all_in:n_all_in + n_all_out], refs[n_all_in + n_all_out:])
        plumb.start_at(*comm_refs, first)

        @pl.when(step == 0)
        def _():
            for jk in range(kw // PAIR):
                cols = pl.ds(jk * PAIR, PAIR)
                heads = [h for h in range(N_HEADS) if (h // group) // 2 == jk]
                for c in range(nkc):
                    rows = pl.ds(c * KEY_CHUNK, KEY_CHUNK)
                    kn, _, _ = _pair_norm(k_ref[rows, cols].astype(F32), gk_ref[...])
                    kn_ref[rows, cols] = kn.astype(BF16)
                    _place_transposed(kn, knt_ref, c, heads, group)
            dkn_ref[...] = jnp.zeros(dkn_ref.shape, F32)
            dvs_ref[...] = jnp.zeros(dvs_ref.shape, F32)

        @pl.when(first)
        def _():
            db_ref[...] = jnp.zeros(db_ref.shape, F32)
            dsink_ref[...] = jnp.zeros(dsink_ref.shape, F32)
            dgq_ref[...] = jnp.zeros(dgq_ref.shape, F32)
            dgk_ref[...] = jnp.zeros(dgk_ref.shape, F32)

        for sb in range(Q_BLOCKS_PER_STEP):
            qrows = pl.ds(sb * Q_BLOCK, Q_BLOCK)
            start, off = _attn_window(step * Q_BLOCKS_PER_STEP + sb, prev)
            c0 = start // KEY_CHUNK
            for jq in range(N_HEADS // 2):
                cols = pl.ds(jq * PAIR, PAIR)
                kcols = pl.ds((((2 * jq) // group) // 2) * PAIR, PAIR)
                qn, q_hat, q_rstd = _pair_norm(q_ref[qrows, cols].astype(F32), gq_ref[...])
                qs = _stack_pair(qn * ATTN_SCALE, jq, group)
                do_pair = dy_ref[qrows, cols].astype(F32)
                dos = _stack_pair(do_pair, jq, group)
                prod_t = (do_pair * y_ref[qrows, cols].astype(F32)).T
                lo = _sub_lo(prod_t.shape)
                delta2 = jnp.concatenate([jnp.sum(jnp.where(lo, prod_t, 0.0), axis=0, keepdims=True),
                                          jnp.sum(jnp.where(lo, 0.0, prod_t), axis=0, keepdims=True)], axis=1)
                lse2 = jnp.concatenate([lse_ref[sb, 2 * jq:2 * jq + 1, :], lse_ref[sb, 2 * jq + 1:2 * jq + 2, :]],
                                       axis=1)
                dsk = -jnp.exp(sink_ref[jq, 0:1, :] - lse2) * delta2
                dsink_ref[2 * jq:2 * jq + 1, :] += dsk[:, :Q_BLOCK]
                dsink_ref[2 * jq + 1:2 * jq + 2, :] += dsk[:, Q_BLOCK:]
                rows_w = pl.ds(start, w)
                s_ref[...] = _dot(kn_ref[rows_w, kcols], qs, "nt")
                dp_ref[...] = _dot(v_ref[rows_w, kcols], dos, "nt")
                for c in range(n_chunks):
                    r = pl.ds(c * KEY_CHUNK, KEY_CHUNK)
                    brows = pl.ds(off + c * KEY_CHUNK, KEY_CHUNK)
                    p = jnp.exp(s_ref[r, :] + bias_ref[jq, brows, :] - lse2)
                    ds = p * (dp_ref[r, :] - delta2)
                    if want_dbias:
                        db_ref[jq, brows, :] += ds
                    ds_b = ds.astype(BF16)
                    pb_ref[r, :] = p.astype(BF16)
                    dsb_ref[r, :] = ds_b
                    dst_ref[pl.ds(2 * c * KEY_CHUNK, KEY_CHUNK), :] = ds_b[:, :Q_BLOCK]
                    dst_ref[pl.ds((2 * c + 1) * KEY_CHUNK, KEY_CHUNK), :] = ds_b[:, Q_BLOCK:]
                dkn_ref[rows_w, kcols] += _dot(dsb_ref[...], qs, "nn")
                dvs_ref[rows_w, kcols] += _dot(pb_ref[...], dos, "nn")
                kl = jnp.concatenate([knt_ref[2 * jq + hq, c0 + c] for c in range(n_chunks) for hq in range(2)],
                                     axis=1)
                dqt = _dot(kl, dst_ref[...], "nn")
                dq_raw, dg = _pair_norm_bwd(dqt.T * ATTN_SCALE, q_hat, q_rstd, gq_ref[...])
                dq_ref[qrows, cols] = dq_raw.astype(BF16)
                dgq_ref[...] += dg

        @pl.when(step == steps - 1)
        def _():
            for jk in range(kw // PAIR):
                kcols = pl.ds(jk * PAIR, PAIR)
                _, k_hat, k_rstd = _pair_norm(k_ref[:, kcols].astype(F32), gk_ref[...])
                dk_raw, dg = _pair_norm_bwd(dkn_ref[:, kcols], k_hat, k_rstd, gk_ref[...])
                dk_ref[:, kcols] = dk_raw.astype(BF16)
                dgk_ref[...] += dg
            dv_ref[...] = dvs_ref[...].astype(BF16)

        plumb.finish_at(*comm_refs, (b == n_batch - 1) & (step == steps - 1))

    q_spec, k_spec, v_spec = _attn_specs(cfg, s, steps)
    row = pl.BlockSpec((STEP_ROWS, A_WIDTH), lambda b, i: (b * steps + i, 0))
    kv_out = pl.BlockSpec((s, kw), lambda b, i: (b, 0))
    pair_bias = _const_spec((N_HEADS // 2, wext, 2 * Q_BLOCK))
    res = pl.pallas_call(
        body, name=name, grid=(n_batch, steps),
        in_specs=[q_spec, k_spec, v_spec, pair_bias, _const_spec((N_HEADS // 2, 8, 2 * Q_BLOCK)),
                  _const_spec((1, PAIR)), _const_spec((1, PAIR)), row, row,
                  pl.BlockSpec((Q_BLOCKS_PER_STEP, N_HEADS, Q_BLOCK), lambda b, i: (b * steps + i, 0, 0))]
        + [ANY] * len(plumb.args),
        out_specs=[row, kv_out, kv_out, pair_bias, _const_spec((N_HEADS, 128)),
                   _const_spec((1, PAIR)), _const_spec((1, PAIR))] + [ANY] * len(plumb.out_shape),
        out_shape=[_sds((t, A_WIDTH), BF16), _sds((t, kw), BF16), _sds((t, kw), BF16),
                   _sds((N_HEADS // 2, wext, 2 * Q_BLOCK), F32), _sds((N_HEADS, 128), F32),
                   _sds((1, PAIR), F32), _sds((1, PAIR), F32)] + plumb.out_shape,
        scratch_shapes=[pltpu.VMEM((s, kw), BF16), pltpu.VMEM((N_HEADS, nkc, PAIR, KEY_CHUNK), BF16),
                        pltpu.VMEM((s, kw), F32), pltpu.VMEM((s, kw), F32),
                        pltpu.VMEM((w, 2 * Q_BLOCK), F32), pltpu.VMEM((w, 2 * Q_BLOCK), F32),
                        pltpu.VMEM((w, 2 * Q_BLOCK), BF16), pltpu.VMEM((w, 2 * Q_BLOCK), BF16),
                        pltpu.VMEM((2 * w, Q_BLOCK), BF16)] + plumb.scratch,
        input_output_aliases=plumb.aliases,
        compiler_params=plumb.params(vmem_limit_bytes=VMEM_LIMIT_V7X),
    )(qkv, qkv, qkv, bias2, sink2, gq, gk, y, dy, lse, *plumb.args)
    return plumb.deliver(res)


def _band_tables(prev_chunks):
    prev = prev_chunks * CHUNK
    wext = 2 * prev + Q_BLOCK
    jj = np.arange(wext)[:, None]
    ii = np.arange(Q_BLOCK)[None, :]
    dist = prev + ii - jj
    rel_chunk = (prev // CHUNK + ii // CHUNK) - jj // CHUNK
    allowed = (rel_chunk >= 0) & (rel_chunk <= prev_chunks)
    return dist, allowed


def _alibi_slopes():
    return np.array([2.0 ** (-8.0 * (h + 1) / N_HEADS) for h in range(N_HEADS)], dtype=np.float32)


def _diag_onehot(prev, wext):
    n_diag = wext + Q_BLOCK - 1
    idx = np.clip(prev + Q_BLOCK - 1 - np.arange(n_diag), -A_MAX_REL, A_MAX_REL) + A_MAX_REL
    onehot = np.zeros((n_diag, 2 * A_MAX_REL + 1), np.float32)
    onehot[np.arange(n_diag), idx] = 1.0
    return onehot


def _bias_a(rel_bias):
    prev = A_PREV_CHUNKS * CHUNK
    _, allowed = _band_tables(A_PREV_CHUNKS)
    wext = allowed.shape[0]
    n_diag = wext + Q_BLOCK - 1
    seq = jnp.dot(rel_bias, jnp.asarray(_diag_onehot(prev, wext).T), precision=lax.Precision.HIGHEST)
    seq = jnp.pad(seq, ((0, 0), (0, 1)))
    rows = jnp.broadcast_to(seq[:, None, :], (N_HEADS, Q_BLOCK, n_diag + 1)).reshape(N_HEADS, -1)
    skew = rows[:, :Q_BLOCK * n_diag].reshape(N_HEADS, Q_BLOCK, n_diag)
    tile = jnp.transpose(skew[:, :, Q_BLOCK - 1:Q_BLOCK - 1 + wext], (0, 2, 1))
    return jnp.where(jnp.asarray(allowed)[None], tile, NEG_INF)


def _bias_b():
    dist, allowed = _band_tables(B_PREV_CHUNKS)
    bias = -_alibi_slopes()[:, None, None] * np.abs(dist).astype(np.float32)[None]
    return jnp.asarray(np.where(allowed[None], bias, np.float32(NEG_INF)).astype(np.float32))


def _rel_bias_grad(db_t):
    prev = A_PREV_CHUNKS * CHUNK
    wext = db_t.shape[1]
    n_diag = wext + Q_BLOCK - 1
    wp = n_diag + Q_BLOCK - 1
    xp = jnp.pad(jnp.transpose(db_t, (0, 2, 1)), ((0, 0), (0, 0), (Q_BLOCK - 1, Q_BLOCK - 1)))
    flat = jnp.pad(xp.reshape(N_HEADS, Q_BLOCK * wp), ((0, 0), (0, Q_BLOCK)))
    skew = flat.reshape(N_HEADS, Q_BLOCK, wp + 1)[:, :, :n_diag]
    diag = jnp.sum(skew, axis=1)
    return jnp.dot(diag, jnp.asarray(_diag_onehot(prev, wext)), precision=lax.Precision.HIGHEST)


def _ew(name, fn, ins, out_dtypes):
    r, c = ins[0].shape
    rb = _pick(r, max(16, (1 << 19) // c), 16)
    spec = pl.BlockSpec((rb, c), lambda i: (i, 0))

    def body(*refs):
        vals = fn(*[ref[...] for ref in refs[:len(ins)]])
        for ref, val in zip(refs[len(ins):], vals):
            ref[...] = val.astype(ref.dtype)

    return pl.pallas_call(
        body, name=name, grid=(r // rb,), in_specs=[spec] * len(ins), out_specs=[spec] * len(out_dtypes),
        out_shape=[_sds((r, c), dt) for dt in out_dtypes], compiler_params=_params(),
    )(*ins)


def _cast_into_slot(name, w, chip):
    r, c = w.shape
    rb = _pick(r, max(16, (1 << 19) // c), 16)

    def body(chip_ref, w_ref, o_ref):
        o_ref[...] = w_ref[...].astype(BF16)

    return pl.pallas_call(
        body, name=name,
        grid_spec=pltpu.PrefetchScalarGridSpec(
            num_scalar_prefetch=1, grid=(r // rb,),
            in_specs=[pl.BlockSpec((rb, c), lambda i, chip_ref: (i, 0))],
            out_specs=pl.BlockSpec((None, rb, c), lambda i, chip_ref: (chip_ref[0], i, 0))),
        out_shape=_sds((N_CHIPS, r, c), BF16), compiler_params=_params(),
    )(chip, w)


def _adamw_math(w, g, m, v):
    m = ADAM_B1 * m + (1.0 - ADAM_B1) * g
    v = ADAM_B2 * v + (1.0 - ADAM_B2) * (g * g)
    m_hat = m / (1.0 - ADAM_B1 ** ADAM_STEP)
    v_hat = v / (1.0 - ADAM_B2 ** ADAM_STEP)
    delta = -ADAM_LR * (m_hat / (jnp.sqrt(v_hat) + ADAM_EPS) + ADAM_WD * w)
    return delta, m, v


def _adamw_terms(name, terms, w, m, v):
    r, c = w.shape
    hr = r // 2
    rb = _pick(hr, max(16, (1 << 19) // c), 16)
    nb = hr // rb

    def body(t_ref, w_ref, m_ref, v_ref, g_ref, d_ref, nm_ref, nv_ref):
        g = t_ref[0].astype(F32)
        for k in range(1, N_CHIPS):
            g = g + t_ref[k].astype(F32)
        delta, nm, nv = _adamw_math(w_ref[...], g, m_ref[...], v_ref[...])
        g_ref[...] = g
        d_ref[...] = delta
        nm_ref[...] = nm
        nv_ref[...] = nv

    spec = pl.BlockSpec((rb, c), lambda h, i: (h * nb + i, 0))
    return pl.pallas_call(
        body, name=name, grid=(2, nb),
        in_specs=[pl.BlockSpec((None, N_CHIPS, rb, c), lambda h, i: (h, 0, i, 0)), spec, spec, spec],
        out_specs=[spec] * 4, out_shape=[_sds((r, c), F32)] * 4, compiler_params=_params(),
    )(terms, w, m, v)


def _mesh_place():
    x, y, c = lax.axis_index("x"), lax.axis_index("y"), lax.axis_index("c")
    chips = [(x, 1 - y), (1 - x, y), (1 - x, 1 - y)]
    return x, y, c, chips


def _run_comms(name, comms):
    plumb = _CommPlumbing(comms, 0, 0, 0)
    n_in, n_out = len(plumb.args), len(plumb.out_shape)

    def body(*refs):
        parts = []
        i0, o0, s0 = 0, n_in, n_in + n_out
        for cm in plumb.comms:
            parts.append((refs[i0:i0 + len(cm.ins)], refs[o0:o0 + len(cm.outs)], refs[s0:s0 + len(cm.sems)]))
            i0 += len(cm.ins)
            o0 += len(cm.outs)
            s0 += len(cm.sems)
        plumb.handshake()
        for cm, part in zip(plumb.comms, parts):
            cm.start(*part)
        for cm, part in zip(plumb.comms, parts):
            cm.finish(*part)

    res = pl.pallas_call(
        body, name=name, in_specs=[ANY] * n_in, out_specs=[ANY] * n_out, out_shape=plumb.out_shape,
        scratch_shapes=plumb.scratch, input_output_aliases=plumb.aliases, compiler_params=plumb.params(),
    )(*plumb.args)
    plumb.deliver(res)


def _gather_ici_copies(outs, sems):
    send_sem, recv_sem = sems
    x, y, c, chips = _mesh_place()
    me = 2 * x + y
    sends, recvs = [], []
    for wi in range(len(outs)):
        for k, (tx, ty) in enumerate(chips):
            sems_k = dict(send_sem=send_sem.at[wi * 3 + k], recv_sem=recv_sem.at[wi * 3 + k],
                          device_id=(tx, ty, c), device_id_type=MESH)
            own = outs[wi].at[me, c]
            sends.append(pltpu.make_async_remote_copy(src_ref=own, dst_ref=own, **sems_k))
            slab = outs[wi].at[2 * tx + ty, c]
            recvs.append(pltpu.make_async_remote_copy(src_ref=slab, dst_ref=slab, **sems_k))
    return sends, recvs


def _gather_d2d_copies(outs, sems):
    send_sem, recv_sem = sems
    x, y, c, chips = _mesh_place()
    sends, recvs = [], []
    for wi in range(len(outs)):
        for k, (tx, ty) in enumerate(chips):
            sems_k = dict(send_sem=send_sem.at[wi * 3 + k], recv_sem=recv_sem.at[wi * 3 + k],
                          device_id=(x, y, 1 - c), device_id_type=MESH)
            mine = outs[wi].at[2 * tx + ty, c]
            theirs = outs[wi].at[2 * tx + ty, 1 - c]
            sends.append(pltpu.make_async_remote_copy(src_ref=mine, dst_ref=mine, **sems_k))
            recvs.append(pltpu.make_async_remote_copy(src_ref=theirs, dst_ref=theirs, **sems_k))
    return sends, recvs


def _gather_comm(peers, bufs, n_sems, start, finish):
    n = len(bufs)
    return _Comm(peers, bufs, [_sds(g.shape, g.dtype) for g in bufs], {i: i for i in range(n)},
                 [pltpu.SemaphoreType.DMA((3 * n,))] * n_sems, start, finish)


def _gather_ici(bufs):
    def start(ins, outs, sems):
        for cp in _gather_ici_copies(outs, sems)[0]:
            cp.start()

    def finish(ins, outs, sems):
        sends, recvs = _gather_ici_copies(outs, sems)
        for cp in recvs:
            cp.wait_recv()
        for cp in sends:
            cp.wait_send()

    return _gather_comm(("chips",), bufs, 2, start, finish)


def _gather_d2d(gathered):
    def start(ins, outs, sems):
        for cp in _gather_d2d_copies(outs, sems)[0]:
            cp.start()

    def finish(ins, outs, sems):
        sends, recvs = _gather_d2d_copies(outs, sems)
        for cp in recvs:
            cp.wait_recv()
        for cp in sends:
            cp.wait_send()

    return _gather_comm(("sibling",), gathered, 2, start, finish)


def _gather_both(bufs):
    def start(ins, outs, sems):
        for cp in _gather_ici_copies(outs, sems[:2])[0]:
            cp.start()

    def finish(ins, outs, sems):
        sends, recvs = _gather_ici_copies(outs, sems[:2])
        passes, lands = _gather_d2d_copies(outs, sems[2:])
        for arrived, onward in zip(recvs, passes):
            arrived.wait_recv()
            onward.start()
        for cp in lands:
            cp.wait_recv()
        for cp in sends + passes:
            cp.wait_send()

    return _gather_comm(("chips", "sibling"), bufs, 4, start, finish)


def _exchange_halves(grads):
    n = len(grads)

    def copies(ins, outs, sems):
        send_sem, recv_sem = sems
        x, y, c, _ = _mesh_place()
        return [pltpu.make_async_remote_copy(
            src_ref=ins[wi].at[t, 1 - c], dst_ref=outs[wi].at[t],
            send_sem=send_sem.at[wi * N_CHIPS + t], recv_sem=recv_sem.at[wi * N_CHIPS + t],
            device_id=(x, y, 1 - c), device_id_type=MESH) for wi in range(n) for t in range(N_CHIPS)]

    def start(ins, outs, sems):
        for cp in copies(ins, outs, sems):
            cp.start()

    def finish(ins, outs, sems):
        for cp in copies(ins, outs, sems):
            cp.wait()

    return _Comm(("sibling",), grads, [_sds((N_CHIPS,) + g.shape[2:], g.dtype) for g in grads], {},
                 [pltpu.SemaphoreType.DMA((N_CHIPS * n,)), pltpu.SemaphoreType.DMA((N_CHIPS * n,))], start, finish)


def _scatter_ici(sums):
    n = len(sums)

    def copies(ins, outs, sems):
        local_sem, send_sem, recv_sem = sems
        x, y, c, chips = _mesh_place()
        me = 2 * x + y
        local, sends, recvs = [], [], []
        for wi in range(n):
            local.append(pltpu.make_async_copy(ins[wi].at[me], outs[wi].at[c, 0], local_sem.at[wi]))
            for k, (tx, ty) in enumerate(chips):
                sems_k = dict(send_sem=send_sem.at[wi * 3 + k], recv_sem=recv_sem.at[wi * 3 + k],
                              device_id=(tx, ty, c), device_id_type=MESH)
                land = outs[wi].at[c, k + 1]
                sends.append(pltpu.make_async_remote_copy(src_ref=ins[wi].at[2 * tx + ty], dst_ref=land, **sems_k))
                recvs.append(pltpu.make_async_remote_copy(src_ref=land, dst_ref=land, **sems_k))
        return local, sends, recvs

    def start(ins, outs, sems):
        local, sends, _ = copies(ins, outs, sems)
        for cp in local + sends:
            cp.start()

    def finish(ins, outs, sems):
        local, sends, recvs = copies(ins, outs, sems)
        for cp in local:
            cp.wait()
        for cp in recvs:
            cp.wait_recv()
        for cp in sends:
            cp.wait_send()

    return _Comm(("chips",), sums, [_sds((2, N_CHIPS) + s.shape[1:], s.dtype) for s in sums], {},
                 [pltpu.SemaphoreType.DMA((n,)), pltpu.SemaphoreType.DMA((3 * n,)), pltpu.SemaphoreType.DMA((3 * n,))],
                 start, finish)


def _scatter_d2d(terms):
    n = len(terms)

    def copies(outs, sems):
        send_sem, recv_sem = sems
        x, y, c, _ = _mesh_place()
        sends, recvs = [], []
        for wi in range(n):
            sems_w = dict(send_sem=send_sem.at[wi], recv_sem=recv_sem.at[wi],
                          device_id=(x, y, 1 - c), device_id_type=MESH)
            sends.append(pltpu.make_async_remote_copy(src_ref=outs[wi].at[c], dst_ref=outs[wi].at[c], **sems_w))
            recvs.append(pltpu.make_async_remote_copy(src_ref=outs[wi].at[1 - c], dst_ref=outs[wi].at[1 - c], **sems_w))
        return sends, recvs

    def start(ins, outs, sems):
        for cp in copies(outs, sems)[0]:
            cp.start()

    def finish(ins, outs, sems):
        sends, recvs = copies(outs, sems)
        for cp in recvs:
            cp.wait_recv()
        for cp in sends:
            cp.wait_send()

    return _Comm(("sibling",), terms, [_sds(t.shape, t.dtype) for t in terms], {i: i for i in range(n)},
                 [pltpu.SemaphoreType.DMA((n,)), pltpu.SemaphoreType.DMA((n,))], start, finish)


def _chip_sum(name, grad, got, core):
    _, _, hr, c = grad.shape
    rb = _pick(hr, max(16, (1 << 19) // c), 16)

    def body(core_ref, a_ref, b_ref, o_ref):
        o_ref[...] = (a_ref[...].astype(F32) + b_ref[...].astype(F32)).astype(BF16)

    out_spec = pl.BlockSpec((None, rb, c), lambda t, i, core_ref: (t, i, 0))
    return pl.pallas_call(
        body, name=name,
        grid_spec=pltpu.PrefetchScalarGridSpec(
            num_scalar_prefetch=1, grid=(N_CHIPS, hr // rb),
            in_specs=[pl.BlockSpec((None, None, rb, c), lambda t, i, core_ref: (t, core_ref[0], i, 0)), out_spec],
            out_specs=out_spec),
        out_shape=_sds((N_CHIPS, hr, c), BF16), compiler_params=_params(),
    )(core, grad, got)


def _all_reduce_small(pack):
    r = pack.shape[0]

    def body(p_ref, o_ref, land_ref, send_sem, recv_sem):
        x, y, c, _ = _mesh_place()
        me = 4 * x + 2 * y + c
        flips = [(k >> 2 & 1, k >> 1 & 1, k & 1) for k in range(1, N_DEV)]

        def peer(fx, fy, fc):
            return (1 - x if fx else x, 1 - y if fy else y, 1 - c if fc else c)

        land_ref[me] = p_ref[...]
        sent = []
        for k, flip in enumerate(flips):
            cp = pltpu.make_async_remote_copy(
                src_ref=p_ref, dst_ref=land_ref.at[me], send_sem=send_sem.at[k], recv_sem=recv_sem.at[k],
                device_id=peer(*flip), device_id_type=MESH)
            cp.start()
            sent.append(cp)
        for k, flip in enumerate(flips):
            px, py, pc = peer(*flip)
            slot = land_ref.at[4 * px + 2 * py + pc]
            pltpu.make_async_remote_copy(
                src_ref=slot, dst_ref=slot, send_sem=send_sem.at[k], recv_sem=recv_sem.at[k],
                device_id=(px, py, pc), device_id_type=MESH).wait_recv()
        total = land_ref[0]
        for d in range(1, N_DEV):
            total = total + land_ref[d]
        o_ref[...] = total
        for cp in sent:
            cp.wait_send()

    vmem = pl.BlockSpec(memory_space=pltpu.VMEM)
    return pl.pallas_call(
        body, name="all_reduce_small", in_specs=[vmem], out_specs=vmem, out_shape=_sds((r, 128), F32),
        scratch_shapes=[pltpu.VMEM((N_DEV, r, 128), F32), pltpu.SemaphoreType.DMA((N_DEV - 1,)),
                        pltpu.SemaphoreType.DMA((N_DEV - 1,))],
    )(pack)


PACK_TILE = 8 * 128


def _pack(items):
    rows, i = [], 0
    while i < len(items):
        j = i
        while j < len(items) and items[j].size == items[i].size:
            j += 1
        group = jnp.stack([it.reshape(-1).astype(F32) for it in items[i:j]])
        rows.append(jnp.pad(group, ((0, 0), (0, -group.shape[1] % PACK_TILE))).reshape(-1, 128))
        i = j
    return jnp.concatenate(rows, axis=0)


def _unpack(pack, shapes):
    out, row = [], 0
    for shp in shapes:
        size = int(np.prod(shp))
        nrow = -(-size // PACK_TILE) * (PACK_TILE // 128)
        out.append(pack[row:row + nrow].reshape(-1)[:size].reshape(shp))
        row += nrow
    return out


BIG = ["ffn1_w_gu", "ffn1_w_down", "w_in", "w_gate", "w_proj_a", "w_proj_b", "w_out",
       "ffn2_w_gu", "ffn2_w_down", "w_ple_gate", "w_ple_proj"]
SMALL = ["ffn1_norm", "mix_norm", "ffn2_norm", "ple_norm", "a_q_norm", "a_k_norm", "b_q_norm", "b_k_norm",
         "a_rel_bias", "b_sinks"]
WEIGHTS = ["ffn1_norm", "ffn1_w_gu", "ffn1_w_down", "mix_norm", "w_in", "a_q_norm", "a_k_norm", "a_rel_bias",
           "b_q_norm", "b_k_norm", "b_sinks", "w_gate", "w_proj_a", "w_proj_b", "w_out", "ffn2_norm",
           "ffn2_w_gu", "ffn2_w_down", "ple_norm", "w_ple_gate", "w_ple_proj"]
ATTN_A = dict(prev=A_PREV_CHUNKS * CHUNK, group=1, kw=A_WIDTH, qblk=0, kblk=1, vblk=2)
ATTN_B = dict(prev=B_PREV_CHUNKS * CHUNK, group=N_HEADS // B_KV_HEADS, kw=B_KV_WIDTH, qblk=3,
              kblk=4 * A_WIDTH // B_KV_WIDTH, vblk=4 * A_WIDTH // B_KV_WIDTH + 1)


def _cast_epilogue(accs, extras, outs, ij):
    for acc, out in zip(accs, outs):
        out[...] = acc.astype(out.dtype)


GATHER_FIRST = ["ffn1_w_gu", "ffn1_w_down"]
ROW_SHARDED = ("ffn1_w_down", "ffn2_w_down", "w_out", "w_ple_gate")


def _slotted(name, grad):
    if name == "w_in":
        rows, cols = grad.shape
        grad = jnp.transpose(grad.reshape(rows, N_CHIPS, cols // N_CHIPS), (1, 0, 2))
    elif name in ROW_SHARDED:
        grad = grad.reshape(N_CHIPS, grad.shape[0] // N_CHIPS, grad.shape[1])
    return grad.reshape(N_CHIPS, 2, grad.shape[1] // 2, grad.shape[2])


def _local_step(xt, pt, tgt, n_batch, bufs, small, core):
    t, d = xt.shape
    tm = _pick(t, ROW_TILE, 8)
    tk = _pick(t, ROW_TILE, 8)
    nt = t // tm
    row = pl.BlockSpec((tm, d), lambda i, j, k: (i, 0))
    gs = bufs["w_gate"].shape[2]
    ps = bufs["w_proj_a"].shape[2]
    es = bufs["w_ple_proj"].shape[2]
    pdim = pt.shape[1]
    ncols = N_CHIPS * bufs["w_in"].shape[2]
    tin = ncols // 2
    assert 2 * gs == d and 4 * ps == d and 4 * es == d and tin % 128 == 0

    w = {}
    halves = {n: b.reshape(N_CHIPS, 2, b.shape[1] // 2, b.shape[2]) for n, b in bufs.items()}

    def publish(names, arrays):
        for name, g in zip(names, arrays):
            g = g.reshape(N_CHIPS, 2 * g.shape[2], g.shape[3])
            if name in ROW_SHARDED:
                g = g.reshape(N_CHIPS * g.shape[1], g.shape[2])
            elif name == "w_in":
                g = jnp.transpose(g, (1, 0, 2)).reshape(g.shape[1], N_CHIPS * g.shape[2])
            w[name] = g

    class GatherPipe:
        def __init__(self, names):
            self.names = names
            self.stage = None

        def ici(self):
            self.stage = _gather_ici(self.bufs())
            return self.stage

        def d2d(self):
            self.stage = _gather_d2d(self.bufs())
            return self.stage

        def bufs(self):
            return self.stage.results if self.stage is not None else [halves[n] for n in self.names]

        def publish(self):
            publish(self.names, self.stage.results)

    class GradPipe:
        def __init__(self, names):
            self.names = names

        def exchange(self, grads):
            self.grads = [_slotted(n, g) for n, g in zip(self.names, grads)]
            self.x = _exchange_halves(self.grads)
            return self.x

        def scatter(self):
            self.sums = [_chip_sum("chip_sum_" + n, g, got, core)
                         for n, g, got in zip(self.names, self.grads, self.x.results)]
            self.s = _scatter_ici(self.sums)
            return self.s

        def forward(self):
            self.f = _scatter_d2d(self.s.results)
            return self.f

        def terms(self):
            return dict(zip(self.names, self.f.results))

    g_first = _gather_both([halves[n] for n in GATHER_FIRST])
    n1 = _rms_fwd("ffn1_norm", xt, small["ffn1_norm"], comms=[g_first])
    publish(GATHER_FIRST, g_first.results)
    g_in, g_proj, g_ple = GatherPipe(["w_in", "w_gate"]), GatherPipe(["w_proj_a", "w_proj_b", "w_out"]), \
        GatherPipe(["w_ple_gate", "w_ple_proj"])
    g_down2, g_up2 = GatherPipe(["ffn2_w_down"]), GatherPipe(["ffn2_w_gu"])
    h1, un, ffn1_saved = _ffn_fwd("ffn1", xt, n1, w["ffn1_w_gu"], w["ffn1_w_down"], small["mix_norm"],
                                  {"up": lambda: [g_in.ici()], "down": lambda: [g_in.d2d(), g_proj.ici()]})
    g_in.publish()
    w_in, wgate = w["w_in"], w["w_gate"]
    (qkv,) = _mm(
        "qkv", "nn", (nt, 2, 1),
        [(un, row, w_in, pl.BlockSpec((d, tin), lambda i, j, k: (0, j)))], [],
        [(_sds((t, ncols), BF16), pl.BlockSpec((tm, tin), lambda i, j, k: (i, j)))], (tm, tin), _cast_epilogue,
        j_outer=True, comms=[g_proj.d2d(), g_ple.ici()])
    g_proj.publish()
    wpa, wpb, wout = w["w_proj_a"], w["w_proj_b"], w["w_out"]

    def gate_epilogue(accs, extras, outs, ij):
        outs[0][...] = jax.nn.sigmoid(accs[0]).astype(BF16)

    (gates,) = _mm(
        "gate", "nn", (nt, 4, 1),
        [(un, row, wgate, pl.BlockSpec((None, d, gs), lambda i, j, k: (j, 0, 0)))], [],
        [(_sds((2, t, d), BF16), pl.BlockSpec((None, tm, gs), lambda i, j, k: (j // 2, i, j % 2)))],
        (tm, gs), gate_epilogue, j_outer=True, chunked=True, comms=[g_ple.d2d(), g_down2.ici()])
    g_ple.publish()
    wpg, wpe = w["w_ple_gate"], w["w_ple_proj"]

    bias_a = _pair_bias(_bias_a(small["a_rel_bias"][0]))
    bias_b = _pair_bias(_bias_b())
    sink_a = _pair_rows(jnp.full((N_HEADS, 128), NEG_INF, F32))
    sink_b = _pair_rows(jnp.broadcast_to(small["b_sinks"][0][:, None], (N_HEADS, 128)))
    gqa, gka, gqb, gkb = [jnp.tile(small[k], (1, 2)) for k in ("a_q_norm", "a_k_norm", "b_q_norm", "b_k_norm")]
    ya, lse_a = _attn_fwd("attn_a_fwd", qkv, bias_a, sink_a, gqa, gka, ATTN_A, n_batch,
                          comms=[g_down2.d2d(), g_up2.ici()])
    g_down2.publish()
    yb, lse_b = _attn_fwd("attn_b_fwd", qkv, bias_b, sink_b, gqb, gkb, ATTN_B, n_batch, comms=[g_up2.d2d()])
    g_up2.publish()

    def merge_epilogue(accs, extras, outs, ij):
        pa, pb = accs
        outs[0][...] = (extras[0][...].astype(F32) * pa + extras[1][...].astype(F32) * pb).astype(BF16)
        outs[1][...] = pa.astype(BF16)
        outs[2][...] = pb.astype(BF16)

    y_spec = pl.BlockSpec((tm, A_WIDTH), lambda i, j, k: (i, 0))
    proj_spec = pl.BlockSpec((None, A_WIDTH, ps), lambda i, j, k: (j, 0, 0))
    tile_ps = pl.BlockSpec((tm, ps), lambda i, j, k: (i, j))
    merged, pa, pb = _mm(
        "proj_merge", "nn", (nt, 4, 1),
        [(ya, y_spec, wpa, proj_spec), (yb, y_spec, wpb, proj_spec)],
        [(gates, pl.BlockSpec((None, tm, ps), lambda i, j, k: (0, i, j))),
         (gates, pl.BlockSpec((None, tm, ps), lambda i, j, k: (1, i, j)))],
        [(_sds((t, d), BF16), tile_ps)] * 3, (tm, ps), merge_epilogue)

    h2, n2 = _mm(
        "out_proj", "nn", (nt, 1, 1),
        [(merged, row, wout, pl.BlockSpec((d, d), lambda i, j, k: (0, 0)))],
        [(h1, row), (small["ffn2_norm"], pl.BlockSpec((1, d), lambda i, j, k: (0, 0)))],
        [(_sds((t, d), F32), row), (_sds((t, d), BF16), row)], (tm, d), _residual_norm_epilogue(1.0))

    h3, n3, ffn2_saved = _ffn_fwd("ffn2", h2, n2, w["ffn2_w_gu"], w["ffn2_w_down"], small["ple_norm"], {})
    tile_es = pl.BlockSpec((tm, es), lambda i, j, k: (i, j))
    th = _pick(d, 512)

    def head_epilogue(accs, extras, outs, ij):
        h3_ref, tgt_ref = extras
        dy_ref, dpe_ref, dz_ref, loss_ref = outs
        pg = jax.nn.sigmoid(accs[0])
        pev = accs[1]
        diff = h3_ref[...] + pg * pev - tgt_ref[...]
        dy = diff * (1.0 / d)
        dy_ref[...] = dy
        dpe_ref[...] = (dy * pg).astype(BF16)
        dz_ref[...] = (dy * pev * pg * (1.0 - pg)).astype(BF16)
        _accumulate(loss_ref, jnp.full(loss_ref.shape, jnp.sum(diff * diff), F32), (ij[0] == 0) & (ij[1] == 0))

    tile_h = pl.BlockSpec((tm, th), lambda i, j, k: (i, j))
    dy, dpe, dz, loss_acc = _mm(
        "ple_gate_loss", "nn", (nt, 4, 1),
        [(n3, row, wpg, pl.BlockSpec((d, es), lambda i, j, k: (0, j))),
         (pt, pl.BlockSpec((tm, pdim), lambda i, j, k: (i, 0)), wpe, pl.BlockSpec((None, pdim, es), lambda i, j, k: (j, 0, 0)))],
        [(h3, tile_es), (tgt, tile_es)],
        [(_sds((t, d), F32), tile_es), (_sds((t, d), BF16), tile_es), (_sds((t, d), BF16), tile_es),
         (_sds((8, 128), F32), pl.BlockSpec((8, 128), lambda i, j, k: (0, 0)))],
        (tm, es), head_epilogue, j_outer=True, chunked=True)
    loss = 0.5 * loss_acc[0, 0] / d

    nk = t // tk
    (dwpe,) = _mm(
        "d_w_ple_proj", "tn", (1, 4, nk),
        [(pt, pl.BlockSpec((tk, pdim), lambda i, j, k: (k, 0)), dpe, pl.BlockSpec((tk, es), lambda i, j, k: (k, j)))],
        [], [(_sds((4, pdim, es), BF16), pl.BlockSpec((None, pdim, es), lambda i, j, k: (j, 0, 0)))],
        (pdim, es), _cast_epilogue)

    def dense_grad(name, a, dyb, comms=()):
        (res,) = _mm(
            name, "tn", (1, d // th, nk),
            [(a, pl.BlockSpec((tk, d), lambda i, j, k: (k, 0)), dyb, pl.BlockSpec((tk, th), lambda i, j, k: (k, j)))],
            [], [(_sds((d, d), BF16), pl.BlockSpec((d, th), lambda i, j, k: (0, j)))], (d, th), _cast_epilogue,
            comms=comms)
        return res

    dwpg = dense_grad("d_w_ple_gate", n3, dz)
    tmn = _pick(t, ROW_TILE, 8)
    extras, outs = _rms_bwd_io(h3, small["ple_norm"], dy, tmn)
    dh3, dh3_b, d_ple_norm = _mm(
        "d_ple_norm", "nt", (t // tmn, 1, 1),
        [(dz, pl.BlockSpec((tmn, d), lambda i, j, k: (i, 0)), wpg, pl.BlockSpec((d, d), lambda i, j, k: (0, 0)))],
        extras, outs, (tmn, d), _rms_bwd_epilogue)

    up2, down2, ple = GradPipe(["ffn2_w_gu"]), GradPipe(["ffn2_w_down"]), GradPipe(["w_ple_gate", "w_ple_proj"])
    proj = GradPipe(["w_proj_a", "w_proj_b", "w_out"])
    dh2, dh2_b, d_ffn2_norm, dwgu2, dwd2 = _ffn_bwd(
        "ffn2", dh3, dh3_b, h2, small["ffn2_norm"], w["ffn2_w_gu"], w["ffn2_w_down"], ffn2_saved,
        {"dnorm": lambda dwgu, dwd: [up2.exchange([dwgu]), down2.exchange([dwd]), ple.exchange([dwpg, dwpe])]})

    def dmerge_epilogue(accs, extras, outs, ij):
        dmo = accs[0]
        g_ref, pa_ref, pb_ref = extras
        dg_ref, dpa_ref, dpb_ref = outs
        ga = g_ref[0].astype(F32)
        gb = g_ref[1].astype(F32)
        dg_ref[0] = (dmo * pa_ref[...].astype(F32) * ga * (1.0 - ga)).astype(BF16)
        dg_ref[1] = (dmo * pb_ref[...].astype(F32) * gb * (1.0 - gb)).astype(BF16)
        dpa_ref[...] = (dmo * ga).astype(BF16)
        dpb_ref[...] = (dmo * gb).astype(BF16)

    g_spec = pl.BlockSpec((2, tm, th), lambda i, j, k: (0, i, j))
    dgates, dpa, dpb = _mm(
        "d_merge", "nt", (nt, d // th, 1),
        [(dh2_b, row, wout, pl.BlockSpec((th, d), lambda i, j, k: (j, 0)))],
        [(gates, g_spec), (pa, tile_h), (pb, tile_h)],
        [(_sds((2, t, d), BF16), g_spec), (_sds((t, d), BF16), tile_h), (_sds((t, d), BF16), tile_h)],
        (tm, th), dmerge_epilogue, j_outer=True, chunked=True, comms=[down2.scatter()])
    dwout = dense_grad("d_w_out", merged, dh2_b, comms=[down2.forward(), ple.scatter()])

    yk_spec = pl.BlockSpec((tk, A_WIDTH), lambda i, j, k: (k, 0))
    dk_spec = pl.BlockSpec((tk, ps), lambda i, j, k: (k, j))
    dproj = (_sds((4, A_WIDTH, ps), BF16), proj_spec)
    dwpa, dwpb = _mm(
        "d_w_proj", "tn", (1, 4, nk),
        [(ya, yk_spec, dpa, dk_spec), (yb, yk_spec, dpb, dk_spec)], [], [dproj, dproj], (A_WIDTH, ps), _cast_epilogue,
        comms=[ple.forward()])
    dproj_a = pl.BlockSpec((tm, ps), lambda i, j, k: (i, k))
    wproj_k = pl.BlockSpec((None, A_WIDTH, ps), lambda i, j, k: (k, 0, 0))
    dya, dyb = _mm(
        "d_attn_out", "nt", (nt, 1, 4),
        [(dpa, dproj_a, wpa, wproj_k), (dpb, dproj_a, wpb, wproj_k)], [],
        [(_sds((t, A_WIDTH), BF16), y_spec)] * 2, (tm, A_WIDTH), _cast_epilogue,
        comms=[proj.exchange([dwpa, dwpb, dwout])])

    dqa, dka, dva, dbias_a, _, dgqa, dgka = _attn_bwd(
        "attn_a_bwd", qkv, bias_a, sink_a, gqa, gka, ya, dya, lse_a, ATTN_A, n_batch, True,
        comms=[up2.scatter(), proj.scatter()])
    dqb, dkb, dvb, _, dsink_b, dgqb, dgkb = _attn_bwd(
        "attn_b_bwd", qkv, bias_b, sink_b, gqb, gkb, yb, dyb, lse_b, ATTN_B, n_batch, False,
        comms=[up2.forward(), proj.forward()])
    dqkv = jnp.concatenate([dqa, dka, dva, dqb, dkb, dvb], axis=1)

    (dwgate,) = _mm(
        "d_w_gate", "tn", (1, 4, nk),
        [(un, pl.BlockSpec((tk, d), lambda i, j, k: (k, 0)),
          dgates, pl.BlockSpec((None, tk, gs), lambda i, j, k: (j // 2, k, j % 2)))],
        [], [(_sds((4, d, gs), BF16), pl.BlockSpec((None, d, gs), lambda i, j, k: (j, 0, 0)))], (d, gs), _cast_epilogue)
    (dwin,) = _mm(
        "d_w_in", "tn", (1, 2, nk),
        [(un, pl.BlockSpec((tk, d), lambda i, j, k: (k, 0)), dqkv, pl.BlockSpec((tk, tin), lambda i, j, k: (k, j)))],
        [], [(_sds((d, ncols), BF16), pl.BlockSpec((d, tin), lambda i, j, k: (0, j)))], (d, tin), _cast_epilogue)

    mixer = GradPipe(["w_in", "w_gate"])
    extras, outs = _rms_bwd_io(h1, small["mix_norm"], dh2, tmn)
    dh1, dh1_b, d_mix_norm = _mm(
        "d_mix_norm", "nt", (t // tmn, 1, 6),
        [(dgates, pl.BlockSpec((None, tmn, gs), lambda i, j, k: (jnp.minimum(k, 3) // 2, i, jnp.minimum(k, 3) % 2)),
          wgate, pl.BlockSpec((None, d, gs), lambda i, j, k: (jnp.minimum(k, 3), 0, 0))),
         (dqkv, pl.BlockSpec((tmn, tin), lambda i, j, k: (i, jnp.maximum(k - 4, 0))),
          w_in, pl.BlockSpec((d, tin), lambda i, j, k: (0, jnp.maximum(k - 4, 0))))],
        extras, outs, (tmn, d), _rms_bwd_epilogue, steps=[4, 2],
        comms=[mixer.exchange([dwin, dwgate])])

    up1 = GradPipe(["ffn1_w_gu"])
    down1 = GradPipe(["ffn1_w_down"])
    dx, _, d_ffn1_norm, _, _ = _ffn_bwd(
        "ffn1", dh1, dh1_b, xt, small["ffn1_norm"], w["ffn1_w_gu"], w["ffn1_w_down"], ffn1_saved,
        {"dwgu": lambda: [mixer.scatter()],
         "dwd": lambda dwgu: [mixer.forward(), up1.exchange([dwgu])],
         "dnorm": lambda dwgu, dwd: [up1.scatter(), down1.exchange([dwd])]})
    _run_comms("grad_tail_scatter", [up1.forward(), down1.scatter()])
    _run_comms("grad_tail_forward", [down1.forward()])
    terms = {}
    for pipe in (up2, down2, ple, proj, mixer, up1, down1):
        terms.update(pipe.terms())

    def fold(v):
        return v[0, :HEAD_DIM] + v[0, HEAD_DIM:]

    small_grads = {"ffn1_norm": d_ffn1_norm, "mix_norm": d_mix_norm, "ffn2_norm": d_ffn2_norm,
                   "ple_norm": d_ple_norm, "a_q_norm": fold(dgqa), "a_k_norm": fold(dgka),
                   "b_q_norm": fold(dgqb), "b_k_norm": fold(dgkb), "a_rel_bias": _rel_bias_grad(_unpair_bias(dbias_a)),
                   "b_sinks": jnp.sum(dsink_b, axis=1)}
    return loss, dx, terms, small_grads


def kernel(x, p, ffn1_norm, ffn1_w_gu, ffn1_w_down, mix_norm, w_in, a_q_norm, a_k_norm, a_rel_bias, b_q_norm, b_k_norm, b_sinks, w_gate, w_proj_a, w_proj_b, w_out, ffn2_norm, ffn2_w_gu, ffn2_w_down, ple_norm, w_ple_gate, w_ple_proj, loss_target, m_ffn1_norm, m_ffn1_w_gu, m_ffn1_w_down, m_mix_norm, m_w_in, m_a_q_norm, m_a_k_norm, m_a_rel_bias, m_b_q_norm, m_b_k_norm, m_b_sinks, m_w_gate, m_w_proj_a, m_w_proj_b, m_w_out, m_ffn2_norm, m_ffn2_w_gu, m_ffn2_w_down, m_ple_norm, m_w_ple_gate, m_w_ple_proj, v_ffn1_norm, v_ffn1_w_gu, v_ffn1_w_down, v_mix_norm, v_w_in, v_a_q_norm, v_a_k_norm, v_a_rel_bias, v_b_q_norm, v_b_k_norm, v_b_sinks, v_w_gate, v_w_proj_a, v_w_proj_b, v_w_out, v_ffn2_norm, v_ffn2_w_gu, v_ffn2_w_down, v_ple_norm, v_w_ple_gate, v_w_ple_proj):
    given = dict(locals())
    n_batch, s, d = x.shape
    t = n_batch * s
    xt = x.reshape(t, d)
    pt = p.reshape(t, p.shape[-1])
    tgt = loss_target.reshape(t, d)

    chip = (2 * lax.axis_index("x") + lax.axis_index("y")).astype(jnp.int32).reshape(1)
    bufs = {name: _cast_into_slot("cast_" + name, given[name][0], chip) for name in BIG}
    small = {name: given[name] for name in SMALL}
    core = lax.axis_index("c").astype(jnp.int32).reshape(1)
    loss, dx, terms, small_grads = _local_step(xt, pt, tgt, n_batch, bufs, small, core)

    grads, deltas, new_m, new_v = {}, {}, {}, {}
    for name in BIG:
        gw, dl, nm, nv = _adamw_terms("adamw_" + name, terms[name], given[name][0], given["m_" + name][0],
                                      given["v_" + name][0])
        grads[name], deltas[name], new_m[name], new_v[name] = gw[None], dl[None], nm[None], nv[None]

    small_shapes = [given[name].shape for name in SMALL] + [()]
    g_pack = _all_reduce_small(_pack([small_grads[name] for name in SMALL] + [loss]))
    zero = jnp.zeros((), F32)
    w_pack = _pack([given[name] for name in SMALL] + [zero])
    m_pack = _pack([given["m_" + name] for name in SMALL] + [zero])
    v_pack = _pack([given["v_" + name] for name in SMALL] + [zero])
    d_pack, nm_pack, nv_pack = _ew("adamw_small", lambda wv, gv, mv, vv: _adamw_math(wv, gv, mv, vv),
                                   [w_pack, g_pack, m_pack, v_pack], [F32] * 3)
    g_small = _unpack(g_pack, small_shapes)
    loss_total = g_small[-1]
    for name, gv, dv, mv, vv in zip(SMALL, g_small, _unpack(d_pack, small_shapes), _unpack(nm_pack, small_shapes),
                                    _unpack(nv_pack, small_shapes)):
        grads[name], deltas[name], new_m[name], new_v[name] = gv, dv, mv, vv

    return (loss_total, dx.reshape(x.shape), *[grads[n] for n in WEIGHTS], *[deltas[n] for n in WEIGHTS],
            *[new_m[n] for n in WEIGHTS], *[new_v[n] for n in WEIGHTS])
```

```python
import functools

import numpy as np
import jax
import jax.numpy as jnp
from jax import lax
from jax.experimental import pallas as pl
from jax.experimental.pallas import tpu as pltpu

F32 = jnp.float32
BF16 = jnp.bfloat16

CHUNK = 64
HEAD_DIM = 64
A_PREV_CHUNKS = 8
A_MAX_REL = 128
N_HEADS = 8
B_KV_HEADS = 2
B_PREV_CHUNKS = 2
A_WIDTH = N_HEADS * HEAD_DIM
B_KV_WIDTH = B_KV_HEADS * HEAD_DIM
EPS = 1e-6
NEG_INF = -1e30
ATTN_SCALE = HEAD_DIM ** -0.5
Q_BLOCK = 128
PAIR = 2 * HEAD_DIM

ADAM_LR = 0.001
ADAM_B1 = 0.9
ADAM_B2 = 0.999
ADAM_EPS = 1e-08
ADAM_WD = 0.01
ADAM_STEP = 10

N_CHIPS = 4
N_DEV = 8
VMEM_LIMIT_V7X = 56 * 1024 * 1024
ROW_TILE = 1024
MESH = pl.DeviceIdType.MESH
COLLECTIVE_IDS = {("sibling",): 1, ("chips",): 2, ("chips", "sibling"): 3}
ANY = pl.BlockSpec(memory_space=pl.ANY)

_DN = {
    "nn": (((1,), (0,)), ((), ())),
    "nt": (((1,), (1,)), ((), ())),
    "tn": (((0,), (0,)), ((), ())),
}


def _pick(n, target, mult=128):
    best = None
    for d in range(mult, min(n, target) + 1, mult):
        if n % d == 0:
            best = d
    return n if best is None else best


def _dot(a, b, mode):
    return lax.dot_general(a.astype(BF16), b.astype(BF16), _DN[mode], preferred_element_type=F32)


def _params():
    return pltpu.CompilerParams(vmem_limit_bytes=VMEM_LIMIT_V7X)


class _Comm:
    def __init__(self, peers, ins, outs, aliases, sems, start, finish):
        self.peers = peers
        self.ins, self.outs, self.aliases, self.sems = list(ins), list(outs), dict(aliases), list(sems)
        self.start, self.finish = start, finish
        self.results = None


class _CommPlumbing:
    def __init__(self, comms, n_in, n_out, n_scratch):
        self.comms = list(comms)
        self.n_in, self.n_out, self.n_scratch = n_in, n_out, n_scratch
        self.args = [a for cm in self.comms for a in cm.ins]
        self.out_shape = [o for cm in self.comms for o in cm.outs]
        self.scratch = [s for cm in self.comms for s in cm.sems]
        self.aliases = {}
        i0, o0 = n_in, n_out
        for cm in self.comms:
            for a, b in cm.aliases.items():
                self.aliases[i0 + a] = o0 + b
            i0 += len(cm.ins)
            o0 += len(cm.outs)

    def _parts(self, in_refs, out_refs, scratch_refs):
        parts = []
        i0, o0, s0 = self.n_in, self.n_out, self.n_scratch
        for cm in self.comms:
            parts.append((in_refs[i0:i0 + len(cm.ins)], out_refs[o0:o0 + len(cm.outs)],
                          scratch_refs[s0:s0 + len(cm.sems)]))
            i0 += len(cm.ins)
            o0 += len(cm.outs)
            s0 += len(cm.sems)
        return parts

    def kinds(self):
        return sorted(set(kind for cm in self.comms for kind in cm.peers))

    def params(self, **kwargs):
        if self.comms:
            kwargs["collective_id"] = COLLECTIVE_IDS[tuple(self.kinds())]
        return pltpu.CompilerParams(**kwargs)

    def handshake(self):
        x, y, c, chips = _mesh_place()
        peers = []
        if "sibling" in self.kinds():
            peers.append((x, y, 1 - c))
        if "chips" in self.kinds():
            peers += [(tx, ty, c) for tx, ty in chips]
        barrier = pltpu.get_barrier_semaphore()
        for peer in peers:
            pl.semaphore_signal(barrier, inc=1, device_id=peer, device_id_type=MESH)
        pl.semaphore_wait(barrier, len(peers))

    def start_at(self, in_refs, out_refs, scratch_refs, first):
        if self.comms:
            parts = self._parts(in_refs, out_refs, scratch_refs)

            @pl.when(first)
            def _():
                self.handshake()
                for cm, part in zip(self.comms, parts):
                    cm.start(*part)

    def finish_at(self, in_refs, out_refs, scratch_refs, last):
        if self.comms:
            parts = self._parts(in_refs, out_refs, scratch_refs)

            @pl.when(last)
            def _():
                for cm, part in zip(self.comms, parts):
                    cm.finish(*part)

    def deliver(self, results):
        o0 = self.n_out
        for cm in self.comms:
            cm.results = list(results[o0:o0 + len(cm.outs)])
            o0 += len(cm.outs)
        return list(results[:self.n_out])


def _swap_ij(spec):
    index_map = spec.index_map
    return pl.BlockSpec(spec.block_shape, lambda j, i, k: index_map(i, j, k))


MXU_COLUMNS_V7X = 256


def _mm(name, mode, grid, pairs, extras, outs, acc_shape, epilogue, steps=None, comms=(), j_outer=False,
        chunked=False):
    ni, nj, nk = grid
    n_in = 2 * len(pairs) + len(extras)
    n_out = len(outs)
    tn = acc_shape[1]
    col_chunks = None
    if chunked:
        assert nk == 1 and steps is None and mode in ("nn", "nt")
        col_chunks = [(c0, min(MXU_COLUMNS_V7X, tn - c0)) for c0 in range(0, tn, MXU_COLUMNS_V7X)]
    n_acc = 0 if chunked else (len(pairs) if steps is None else 1)
    plumb = _CommPlumbing(comms, n_in, n_out, n_acc)
    n_all_in = n_in + len(plumb.args)
    n_all_out = n_out + len(plumb.out_shape)
    if j_outer:
        grid = (nj, ni, nk)
        pairs = [(a, _swap_ij(a_spec), b, _swap_ij(b_spec)) for a, a_spec, b, b_spec in pairs]
        extras = [(e, _swap_ij(e_spec)) for e, e_spec in extras]
        outs = [(o, _swap_ij(o_spec)) for o, o_spec in outs]

    def body(*refs):
        in_refs = refs[:n_all_in]
        out_refs = refs[n_all_in:n_all_in + n_all_out]
        scratch = refs[n_all_in + n_all_out:]
        accs = scratch[:n_acc]
        i = pl.program_id(1 if j_outer else 0)
        j = pl.program_id(0 if j_outer else 1)
        k = pl.program_id(2)
        plumb.start_at(in_refs, out_refs, scratch, (i == 0) & (j == 0) & (k == 0))

        def contrib(p, acc):
            acc[...] += _dot(in_refs[2 * p][...], in_refs[2 * p + 1][...], mode)

        if col_chunks:
            def cols(ref, c0, cs):
                if ref.shape[-1] != tn:
                    return ref
                return ref.at[(slice(None),) * (len(ref.shape) - 1) + (pl.ds(c0, cs),)]

            lhs = [in_refs[2 * p][...] for p in range(len(pairs))]
            for ci, (c0, cs) in enumerate(col_chunks):
                vals = []
                for p in range(len(pairs)):
                    b_ref = in_refs[2 * p + 1]
                    rhs = b_ref[:, c0:c0 + cs] if mode == "nn" else b_ref[c0:c0 + cs, :]
                    vals.append(_dot(lhs[p], rhs, mode))
                epilogue(vals, [cols(r, c0, cs) for r in in_refs[2 * len(pairs):n_in]],
                         [cols(r, c0, cs) for r in out_refs[:n_out]], (i, j * len(col_chunks) + ci))
        else:
            @pl.when(k == 0)
            def _():
                for acc in accs:
                    acc[...] = jnp.zeros(acc.shape, F32)

            if steps is None:
                for p in range(len(pairs)):
                    contrib(p, accs[p])
            else:
                lo = 0
                for p, n in enumerate(steps):
                    pl.when((k >= lo) & (k < lo + n))(functools.partial(contrib, p, accs[0]))
                    lo += n

            @pl.when(k == nk - 1)
            def _():
                epilogue([acc[...] for acc in accs], in_refs[2 * len(pairs):n_in], out_refs[:n_out], (i, j))

        plumb.finish_at(in_refs, out_refs, scratch, (i == ni - 1) & (j == nj - 1) & (k == nk - 1))

    args, in_specs = [], []
    for a, a_spec, b, b_spec in pairs:
        args += [a, b]
        in_specs += [a_spec, b_spec]
    for e, e_spec in extras:
        args.append(e)
        in_specs.append(e_spec)
    res = pl.pallas_call(
        body,
        name=name,
        grid=grid,
        in_specs=in_specs + [ANY] * len(plumb.args),
        out_specs=[s for _, s in outs] + [ANY] * len(plumb.out_shape),
        out_shape=[o for o, _ in outs] + plumb.out_shape,
        scratch_shapes=[pltpu.VMEM(acc_shape, F32) for _ in range(n_acc)] + plumb.scratch,
        input_output_aliases=plumb.aliases,
        compiler_params=plumb.params(vmem_limit_bytes=VMEM_LIMIT_V7X),
    )(*args, *plumb.args)
    return plumb.deliver(res)


def _sds(shape, dtype):
    return jax.ShapeDtypeStruct(shape, dtype)


def _accumulate(ref, value, first):
    @pl.when(first)
    def _():
        ref[...] = value

    @pl.when(jnp.logical_not(first))
    def _():
        ref[...] += value


def _rms_fwd(name, x, gain, comms=()):
    t, d = x.shape
    tm = _pick(t, ROW_TILE, 8)
    steps = t // tm
    plumb = _CommPlumbing(comms, 2, 1, 0)
    n_all_in = 2 + len(plumb.args)
    n_all_out = 1 + len(plumb.out_shape)

    def body(*refs):
        x_ref, g_ref = refs[:2]
        y_ref = refs[n_all_in]
        comm_refs = (refs[:n_all_in], refs[n_all_in:n_all_in + n_all_out], refs[n_all_in + n_all_out:])
        i = pl.program_id(0)
        plumb.start_at(*comm_refs, i == 0)
        xv = x_ref[...]
        rstd = lax.rsqrt(jnp.mean(xv * xv, axis=-1, keepdims=True) + EPS)
        y_ref[...] = (xv * rstd * g_ref[...]).astype(BF16)
        plumb.finish_at(*comm_refs, i == steps - 1)

    res = pl.pallas_call(
        body, name=name, grid=(steps,),
        in_specs=[pl.BlockSpec((tm, d), lambda i: (i, 0)), pl.BlockSpec((1, d), lambda i: (0, 0))]
        + [ANY] * len(plumb.args),
        out_specs=[pl.BlockSpec((tm, d), lambda i: (i, 0))] + [ANY] * len(plumb.out_shape),
        out_shape=[_sds((t, d), BF16)] + plumb.out_shape,
        scratch_shapes=plumb.scratch,
        input_output_aliases=plumb.aliases,
        compiler_params=plumb.params(vmem_limit_bytes=VMEM_LIMIT_V7X),
    )(x, gain, *plumb.args)
    return plumb.deliver(res)[0]


def _rms_bwd_epilogue(accs, extras, outs, ij):
    x_ref, g_ref, r_ref = extras
    dh_ref, dhb_ref, dg_ref = outs
    dn = accs[0]
    xv = x_ref[...]
    rstd = lax.rsqrt(jnp.mean(xv * xv, axis=-1, keepdims=True) + EPS)
    xhat = xv * rstd
    gd = dn * g_ref[...]
    dx = rstd * (gd - xhat * jnp.mean(gd * xhat, axis=-1, keepdims=True))
    dh = r_ref[...] + dx
    dh_ref[...] = dh
    dhb_ref[...] = dh.astype(BF16)
    _accumulate(dg_ref, jnp.sum(dn * xhat, axis=0, keepdims=True), ij[0] == 0)


def _rms_bwd_io(x, gain, dres, tm):
    t, d = x.shape
    row = pl.BlockSpec((tm, d), lambda i, j, k: (i, 0))
    extras = [(x, row), (gain, pl.BlockSpec((1, d), lambda i, j, k: (0, 0))), (dres, row)]
    outs = [(_sds((t, d), F32), row), (_sds((t, d), BF16), row),
            (_sds((1, d), F32), pl.BlockSpec((1, d), lambda i, j, k: (0, 0)))]
    return extras, outs


def _residual_norm_epilogue(scale):
    def epilogue(accs, extras, outs, ij):
        hv = extras[0][...] + scale * accs[0]
        outs[0][...] = hv
        rstd = lax.rsqrt(jnp.mean(hv * hv, axis=-1, keepdims=True) + EPS)
        outs[1][...] = (hv * rstd * extras[1][...]).astype(BF16)
    return epilogue


def _ffn_fwd(tag, h, n, wgu, wd, next_gain, hooks):
    t, d = h.shape
    fs = wgu.shape[2]
    f = 2 * fs
    tm = _pick(t, ROW_TILE, 8)

    def up_epilogue(accs, extras, outs, ij):
        g, u = accs
        gu_ref, a_ref = outs
        gu_ref[0] = g.astype(BF16)
        gu_ref[1] = u.astype(BF16)
        a_ref[...] = (g * jax.nn.sigmoid(g) * u).astype(BF16)

    a_spec = pl.BlockSpec((tm, d), lambda i, j, k: (i, 0))
    gu, a = _mm(
        tag + "_up", "nn", (t // tm, 2, 1),
        [(n, a_spec, wgu, pl.BlockSpec((None, d, fs), lambda i, j, k: (j, 0, 0))),
         (n, a_spec, wgu, pl.BlockSpec((None, d, fs), lambda i, j, k: (j + 2, 0, 0)))],
        [],
        [(_sds((2, t, f), BF16), pl.BlockSpec((2, tm, fs), lambda i, j, k: (0, i, j))),
         (_sds((t, f), BF16), pl.BlockSpec((tm, fs), lambda i, j, k: (i, j)))],
        (tm, fs), up_epilogue, comms=hooks.get("up", lambda: ())(), j_outer=True, chunked=True)

    row = pl.BlockSpec((tm, d), lambda i, j, k: (i, 0))
    h_new, n_new = _mm(
        tag + "_down", "nn", (t // tm, 1, 1),
        [(a, pl.BlockSpec((tm, f), lambda i, j, k: (i, 0)), wd, pl.BlockSpec((f, d), lambda i, j, k: (0, 0)))],
        [(h, row), (next_gain, pl.BlockSpec((1, d), lambda i, j, k: (0, 0)))],
        [(_sds((t, d), F32), row), (_sds((t, d), BF16), row)], (tm, d), _residual_norm_epilogue(0.5),
        comms=hooks.get("down", lambda: ())())
    return h_new, n_new, (n, gu, a)


def _ffn_bwd(tag, dh, dh_b, h, gain, wgu, wd, saved, hooks):
    n, gu, a = saved
    t, d = h.shape
    fs = wgu.shape[2]
    f = 2 * fs
    tm = _pick(t, ROW_TILE, 8)
    tk = _pick(t, ROW_TILE, 8)

    def dact_epilogue(accs, extras, outs, ij):
        da = 0.5 * accs[0]
        g = extras[0][0].astype(F32)
        u = extras[0][1].astype(F32)
        sg = jax.nn.sigmoid(g)
        outs[0][0] = (da * u * sg * (1.0 + g * (1.0 - sg))).astype(BF16)
        outs[0][1] = (da * g * sg).astype(BF16)

    gu_spec = pl.BlockSpec((2, tm, fs), lambda i, j, k: (0, i, j))
    (dgu,) = _mm(
        tag + "_dact", "nt", (t // tm, 2, 1),
        [(dh_b, pl.BlockSpec((tm, d), lambda i, j, k: (i, 0)), wd, pl.BlockSpec((fs, d), lambda i, j, k: (j, 0)))],
        [(gu, gu_spec)], [(_sds((2, t, f), BF16), gu_spec)], (tm, fs), dact_epilogue, j_outer=True, chunked=True,
        comms=hooks.get("dact", lambda: ())())

    def cast_epilogue(accs, extras, outs, ij):
        outs[0][...] = accs[0].astype(BF16)

    (dwgu,) = _mm(
        tag + "_dwgu", "tn", (1, 4, t // tk),
        [(n, pl.BlockSpec((tk, d), lambda i, j, k: (k, 0)),
          dgu, pl.BlockSpec((None, tk, fs), lambda i, j, k: (j // 2, k, j % 2)))],
        [], [(_sds((4, d, fs), BF16), pl.BlockSpec((None, d, fs), lambda i, j, k: (j, 0, 0)))], (d, fs), cast_epilogue,
        comms=hooks.get("dwgu", lambda: ())())

    def half_epilogue(accs, extras, outs, ij):
        outs[0][...] = (0.5 * accs[0]).astype(BF16)

    (dwd,) = _mm(
        tag + "_dwd", "tn", (2, 1, t // tk),
        [(a, pl.BlockSpec((tk, fs), lambda i, j, k: (k, i)), dh_b, pl.BlockSpec((tk, d), lambda i, j, k: (k, 0)))],
        [], [(_sds((f, d), BF16), pl.BlockSpec((fs, d), lambda i, j, k: (i, 0)))], (fs, d), half_epilogue,
        comms=hooks.get("dwd", lambda g: ())(dwgu))

    tmn = _pick(t, ROW_TILE, 8)
    extras, outs = _rms_bwd_io(h, gain, dh, tmn)
    dh_in, dh_in_b, dgain = _mm(
        tag + "_dnorm", "nt", (t // tmn, 1, 4),
        [(dgu, pl.BlockSpec((None, tmn, fs), lambda i, j, k: (k // 2, i, k % 2)),
          wgu, pl.BlockSpec((None, d, fs), lambda i, j, k: (k, 0, 0)))],
        extras, outs, (tmn, d), _rms_bwd_epilogue, comms=hooks.get("dnorm", lambda g, w: ())(dwgu, dwd))
    return dh_in, dh_in_b, dgain, dwgu, dwd


def _lane_lo(shape):
    return lax.broadcasted_iota(jnp.int32, shape, 1) < HEAD_DIM


def _pair_norm(xv, gain):
    lo = _lane_lo(xv.shape)
    x2 = xv * xv
    ms_lo = jnp.sum(jnp.where(lo, x2, 0.0), axis=-1, keepdims=True) * (1.0 / HEAD_DIM)
    ms_hi = jnp.sum(jnp.where(lo, 0.0, x2), axis=-1, keepdims=True) * (1.0 / HEAD_DIM)
    rstd = jnp.where(lo, lax.rsqrt(ms_lo + EPS), lax.rsqrt(ms_hi + EPS))
    xhat = xv * rstd
    return xhat * gain, xhat, rstd


def _pair_norm_bwd(dn, xhat, rstd, gain):
    lo = _lane_lo(dn.shape)
    gd = dn * gain
    t = gd * xhat
    m_lo = jnp.sum(jnp.where(lo, t, 0.0), axis=-1, keepdims=True) * (1.0 / HEAD_DIM)
    m_hi = jnp.sum(jnp.where(lo, 0.0, t), axis=-1, keepdims=True) * (1.0 / HEAD_DIM)
    dx = rstd * (gd - xhat * jnp.where(lo, m_lo, m_hi))
    return dx, jnp.sum(dn * xhat, axis=0, keepdims=True)


def _half(xv, hi):
    lo = _lane_lo(xv.shape)
    return jnp.where(lo, 0, xv) if hi else jnp.where(lo, xv, 0)


def _attn_window(i, prev):
    q0 = i * Q_BLOCK
    start = jnp.maximum(q0 - prev, 0)
    off = start - (q0 - prev)
    return pl.multiple_of(start, Q_BLOCK), pl.multiple_of(off, Q_BLOCK)


Q_BLOCKS_PER_STEP = 4
STEP_ROWS = Q_BLOCKS_PER_STEP * Q_BLOCK


def _attn_specs(cfg, s, steps):
    kw = cfg["kw"]
    q_spec = pl.BlockSpec((STEP_ROWS, A_WIDTH), lambda b, i: (b * steps + i, cfg["qblk"]))
    k_spec = pl.BlockSpec((s, kw), lambda b, i: (b, cfg["kblk"]))
    v_spec = pl.BlockSpec((s, kw), lambda b, i: (b, cfg["vblk"]))
    return q_spec, k_spec, v_spec


def _const_spec(shape):
    return pl.BlockSpec(shape, lambda b, i: (0,) * len(shape))


KEY_CHUNK = 128


def _pair_bias(bias_t):
    wext = bias_t.shape[1]
    return jnp.transpose(bias_t.reshape(N_HEADS // 2, 2, wext, Q_BLOCK), (0, 2, 1, 3)).reshape(
        N_HEADS // 2, wext, 2 * Q_BLOCK)


def _unpair_bias(db2):
    wext = db2.shape[1]
    return jnp.transpose(db2.reshape(N_HEADS // 2, wext, 2, Q_BLOCK), (0, 2, 1, 3)).reshape(N_HEADS, wext, Q_BLOCK)


def _pair_rows(rows):
    two = rows.reshape(N_HEADS // 2, 2 * rows.shape[1])
    return jnp.broadcast_to(two[:, None, :], (N_HEADS // 2, 8, two.shape[1]))


def _sub_lo(shape):
    return lax.broadcasted_iota(jnp.int32, shape, 0) < HEAD_DIM


def _by_half(lo_row, hi_row, rows):
    return jnp.where(_sub_lo((rows, lo_row.shape[1])), lo_row, hi_row)


def _stack_pair(xn, jq, group):
    parts = []
    for hq in range(2):
        hk = ((2 * jq + hq) // group) % 2
        xm = _half(xn, hq)
        if hq != hk:
            xm = pltpu.roll(xm, HEAD_DIM, 1)
        parts.append(xm)
    return jnp.concatenate(parts, axis=0).astype(BF16)


def _place_transposed(blk, dst_ref, c, heads, group):
    bt = blk.T
    lo = _sub_lo(bt.shape)
    for h in heads:
        src_hi = ((h // group) % 2) == 1
        part = jnp.where(lo, 0.0, bt) if src_hi else jnp.where(lo, bt, 0.0)
        if src_hi != (h % 2 == 1):
            part = pltpu.roll(part, HEAD_DIM, 0)
        dst_ref[h, c] = part.astype(BF16)


def _attn_fwd(name, qkv, bias2, sink2, gq, gk, cfg, n_batch, comms=()):
    t = qkv.shape[0]
    s = t // n_batch
    steps = s // STEP_ROWS
    nkc = s // KEY_CHUNK
    prev, group, kw = cfg["prev"], cfg["group"], cfg["kw"]
    w = prev + Q_BLOCK
    n_chunks = w // KEY_CHUNK
    wext = bias2.shape[1]
    plumb = _CommPlumbing(comms, 7, 2, 4)
    n_all_in = 7 + len(plumb.args)
    n_all_out = 2 + len(plumb.out_shape)

    def body(*refs):
        q_ref, k_ref, v_ref, bias_ref, sink_ref, gq_ref, gk_ref = refs[:7]
        y_ref, lse_ref = refs[n_all_in:n_all_in + 2]
        kn_ref, vt_ref, s_ref, pst_ref = refs[n_all_in + n_all_out:n_all_in + n_all_out + 4]
        step = pl.program_id(1)
        comm_refs = (refs[:n_all_in], refs[n_all_in:n_all_in + n_all_out], refs[n_all_in + n_all_out:])
        plumb.start_at(*comm_refs, (pl.program_id(0) == 0) & (step == 0))

        @pl.when(step == 0)
        def _():
            for jk in range(kw // PAIR):
                cols = pl.ds(jk * PAIR, PAIR)
                heads = [h for h in range(N_HEADS) if (h // group) // 2 == jk]
                kn, _, _ = _pair_norm(k_ref[:, cols].astype(F32), gk_ref[...])
                kn_ref[:, cols] = kn.astype(BF16)
                for c in range(nkc):
                    _place_transposed(v_ref[pl.ds(c * KEY_CHUNK, KEY_CHUNK), cols].astype(F32), vt_ref, c, heads, group)

        sub8 = lax.broadcasted_iota(jnp.int32, (N_HEADS, Q_BLOCK), 0)
        for sb in range(Q_BLOCKS_PER_STEP):
            qrows = pl.ds(sb * Q_BLOCK, Q_BLOCK)
            start, off = _attn_window(step * Q_BLOCKS_PER_STEP + sb, prev)
            c0 = start // KEY_CHUNK
            lse = jnp.zeros((N_HEADS, Q_BLOCK), F32)
            for jq in range(N_HEADS // 2):
                kcols = pl.ds((((2 * jq) // group) // 2) * PAIR, PAIR)
                qn, _, _ = _pair_norm(q_ref[qrows, pl.ds(jq * PAIR, PAIR)].astype(F32), gq_ref[...])
                qs = _stack_pair(qn * ATTN_SCALE, jq, group)
                s_ref[...] = _dot(kn_ref[pl.ds(start, w), kcols], qs, "nt")
                m = sink_ref[jq, 0:1, :]
                for c in range(n_chunks):
                    r = pl.ds(c * KEY_CHUNK, KEY_CHUNK)
                    s2 = s_ref[r, :] + bias_ref[jq, pl.ds(off + c * KEY_CHUNK, KEY_CHUNK), :]
                    s_ref[r, :] = s2
                    m = jnp.maximum(m, jnp.max(s2, axis=0, keepdims=True))
                l = jnp.exp(sink_ref[jq, 0:1, :] - m)
                for c in range(n_chunks):
                    p = jnp.exp(s_ref[pl.ds(c * KEY_CHUNK, KEY_CHUNK), :] - m)
                    l = l + jnp.sum(p, axis=0, keepdims=True)
                    pst_ref[pl.ds(2 * c * KEY_CHUNK, KEY_CHUNK), :] = p[:, :Q_BLOCK].astype(BF16)
                    pst_ref[pl.ds((2 * c + 1) * KEY_CHUNK, KEY_CHUNK), :] = p[:, Q_BLOCK:].astype(BF16)
                vl = jnp.concatenate([vt_ref[2 * jq + hq, c0 + c] for c in range(n_chunks) for hq in range(2)], axis=1)
                ot = _dot(vl, pst_ref[...], "nn")
                inv = 1.0 / l
                ot = ot * _by_half(inv[:, :Q_BLOCK], inv[:, Q_BLOCK:], PAIR)
                y_ref[qrows, pl.ds(jq * PAIR, PAIR)] = ot.T.astype(BF16)
                lse2 = m + jnp.log(l)
                lse = jnp.where(sub8 == 2 * jq, lse2[:, :Q_BLOCK], lse)
                lse = jnp.where(sub8 == 2 * jq + 1, lse2[:, Q_BLOCK:], lse)
            lse_ref[sb] = lse
        plumb.finish_at(*comm_refs, (pl.program_id(0) == n_batch - 1) & (step == steps - 1))

    q_spec, k_spec, v_spec = _attn_specs(cfg, s, steps)
    res = pl.pallas_call(
        body, name=name, grid=(n_batch, steps),
        in_specs=[q_spec, k_spec, v_spec, _const_spec((N_HEADS // 2, wext, 2 * Q_BLOCK)),
                  _const_spec((N_HEADS // 2, 8, 2 * Q_BLOCK)), _const_spec((1, PAIR)), _const_spec((1, PAIR))]
        + [ANY] * len(plumb.args),
        out_specs=[pl.BlockSpec((STEP_ROWS, A_WIDTH), lambda b, i: (b * steps + i, 0)),
                   pl.BlockSpec((Q_BLOCKS_PER_STEP, N_HEADS, Q_BLOCK), lambda b, i: (b * steps + i, 0, 0))]
        + [ANY] * len(plumb.out_shape),
        out_shape=[_sds((t, A_WIDTH), BF16), _sds((t // Q_BLOCK, N_HEADS, Q_BLOCK), F32)] + plumb.out_shape,
        scratch_shapes=[pltpu.VMEM((s, kw), BF16), pltpu.VMEM((N_HEADS, nkc, PAIR, KEY_CHUNK), BF16),
                        pltpu.VMEM((w, 2 * Q_BLOCK), F32), pltpu.VMEM((2 * w, Q_BLOCK), BF16)] + plumb.scratch,
        input_output_aliases=plumb.aliases,
        compiler_params=plumb.params(vmem_limit_bytes=VMEM_LIMIT_V7X),
    )(qkv, qkv, qkv, bias2, sink2, gq, gk, *plumb.args)
    return plumb.deliver(res)


def _attn_bwd(name, qkv, bias2, sink2, gq, gk, y, dy, lse, cfg, n_batch, want_dbias, comms=()):
    t = qkv.shape[0]
    s = t // n_batch
    steps = s // STEP_ROWS
    nkc = s // KEY_CHUNK
    prev, group, kw = cfg["prev"], cfg["group"], cfg["kw"]
    w = prev + Q_BLOCK
    n_chunks = w // KEY_CHUNK
    wext = bias2.shape[1]
    plumb = _CommPlumbing(comms, 10, 7, 9)
    n_all_in = 10 + len(plumb.args)
    n_all_out = 7 + len(plumb.out_shape)

    def body(*refs):
        q_ref, k_ref, v_ref, bias_ref, sink_ref, gq_ref, gk_ref, y_ref, dy_ref, lse_ref = refs[:10]
        dq_ref, dk_ref, dv_ref, db_ref, dsink_ref, dgq_ref, dgk_ref = refs[n_all_in:n_all_in + 7]
        kn_ref, knt_ref, dkn_ref, dvs_ref, s_ref, dp_ref, pb_ref, dsb_ref, dst_ref = \
            refs[n_all_in + n_all_out:n_all_in + n_all_out + 9]
        b = pl.program_id(0)
        step = pl.program_id(1)
        first = (b == 0) & (step == 0)
        comm_refs = (refs[:n_all_in], refs[n_all_in:n_all_in + n_all_out], refs[n_all_in + n_all_out:])
        plumb.start_at(*comm_refs, first)

        @pl.when(step == 0)
        def _():
            for jk in range(kw // PAIR):
                cols = pl.ds(jk * PAIR, PAIR)
                heads = [h for h in range(N_HEADS) if (h // group) // 2 == jk]
                for c in range(nkc):
                    rows = pl.ds(c * KEY_CHUNK, KEY_CHUNK)
                    kn, _, _ = _pair_norm(k_ref[rows, cols].astype(F32), gk_ref[...])
                    kn_ref[rows, cols] = kn.astype(BF16)
                    _place_transposed(kn, knt_ref, c, heads, group)
            dkn_ref[...] = jnp.zeros(dkn_ref.shape, F32)
            dvs_ref[...] = jnp.zeros(dvs_ref.shape, F32)

        @pl.when(first)
        def _():
            db_ref[...] = jnp.zeros(db_ref.shape, F32)
            dsink_ref[...] = jnp.zeros(dsink_ref.shape, F32)
            dgq_ref[...] = jnp.zeros(dgq_ref.shape, F32)
            dgk_ref[...] = jnp.zeros(dgk_ref.shape, F32)

        for sb in range(Q_BLOCKS_PER_STEP):
            qrows = pl.ds(sb * Q_BLOCK, Q_BLOCK)
            start, off = _attn_window(step * Q_BLOCKS_PER_STEP + sb, prev)
            c0 = start // KEY_CHUNK
            for jq in range(N_HEADS // 2):
                cols = pl.ds(jq * PAIR, PAIR)
                kcols = pl.ds((((2 * jq) // group) // 2) * PAIR, PAIR)
                qn, q_hat, q_rstd = _pair_norm(q_ref[qrows, cols].astype(F32), gq_ref[...])
                qs = _stack_pair(qn * ATTN_SCALE, jq, group)
                do_pair = dy_ref[qrows, cols].astype(F32)
                dos = _stack_pair(do_pair, jq, group)
                prod_t = (do_pair * y_ref[qrows, cols].astype(F32)).T
                lo = _sub_lo(prod_t.shape)
                delta2 = jnp.concatenate([jnp.sum(jnp.where(lo, prod_t, 0.0), axis=0, keepdims=True),
                                          jnp.sum(jnp.where(lo, 0.0, prod_t), axis=0, keepdims=True)], axis=1)
                lse2 = jnp.concatenate([lse_ref[sb, 2 * jq:2 * jq + 1, :], lse_ref[sb, 2 * jq + 1:2 * jq + 2, :]],
                                       axis=1)
                dsk = -jnp.exp(sink_ref[jq, 0:1, :] - lse2) * delta2
                dsink_ref[2 * jq:2 * jq + 1, :] += dsk[:, :Q_BLOCK]
                dsink_ref[2 * jq + 1:2 * jq + 2, :] += dsk[:, Q_BLOCK:]
                rows_w = pl.ds(start, w)
                s_ref[...] = _dot(kn_ref[rows_w, kcols], qs, "nt")
                dp_ref[...] = _dot(v_ref[rows_w, kcols], dos, "nt")
                for c in range(n_chunks):
                    r = pl.ds(c * KEY_CHUNK, KEY_CHUNK)
                    brows = pl.ds(off + c * KEY_CHUNK, KEY_CHUNK)
                    p = jnp.exp(s_ref[r, :] + bias_ref[jq, brows, :] - lse2)
                    ds = p * (dp_ref[r, :] - delta2)
                    if want_dbias:
                        db_ref[jq, brows, :] += ds
                    ds_b = ds.astype(BF16)
                    pb_ref[r, :] = p.astype(BF16)
                    dsb_ref[r, :] = ds_b
                    dst_ref[pl.ds(2 * c * KEY_CHUNK, KEY_CHUNK), :] = ds_b[:, :Q_BLOCK]
                    dst_ref[pl.ds((2 * c + 1) * KEY_CHUNK, KEY_CHUNK), :] = ds_b[:, Q_BLOCK:]
                dkn_ref[rows_w, kcols] += _dot(dsb_ref[...], qs, "nn")
                dvs_ref[rows_w, kcols] += _dot(pb_ref[...], dos, "nn")
                kl = jnp.concatenate([knt_ref[2 * jq + hq, c0 + c] for c in range(n_chunks) for hq in range(2)],
                                     axis=1)
                dqt = _dot(kl, dst_ref[...], "nn")
                dq_raw, dg = _pair_norm_bwd(dqt.T * ATTN_SCALE, q_hat, q_rstd, gq_ref[...])
                dq_ref[qrows, cols] = dq_raw.astype(BF16)
                dgq_ref[...] += dg

        @pl.when(step == steps - 1)
        def _():
            for jk in range(kw // PAIR):
                kcols = pl.ds(jk * PAIR, PAIR)
                _, k_hat, k_rstd = _pair_norm(k_ref[:, kcols].astype(F32), gk_ref[...])
                dk_raw, dg = _pair_norm_bwd(dkn_ref[:, kcols], k_hat, k_rstd, gk_ref[...])
                dk_ref[:, kcols] = dk_raw.astype(BF16)
                dgk_ref[...] += dg
            dv_ref[...] = dvs_ref[...].astype(BF16)

        plumb.finish_at(*comm_refs, (b == n_batch - 1) & (step == steps - 1))

    q_spec, k_spec, v_spec = _attn_specs(cfg, s, steps)
    row = pl.BlockSpec((STEP_ROWS, A_WIDTH), lambda b, i: (b * steps + i, 0))
    kv_out = pl.BlockSpec((s, kw), lambda b, i: (b, 0))
    pair_bias = _const_spec((N_HEADS // 2, wext, 2 * Q_BLOCK))
    res = pl.pallas_call(
        body, name=name, grid=(n_batch, steps),
        in_specs=[q_spec, k_spec, v_spec, pair_bias, _const_spec((N_HEADS // 2, 8, 2 * Q_BLOCK)),
                  _const_spec((1, PAIR)), _const_spec((1, PAIR)), row, row,
                  pl.BlockSpec((Q_BLOCKS_PER_STEP, N_HEADS, Q_BLOCK), lambda b, i: (b * steps + i, 0, 0))]
        + [ANY] * len(plumb.args),
        out_specs=[row, kv_out, kv_out, pair_bias, _const_spec((N_HEADS, 128)),
                   _const_spec((1, PAIR)), _const_spec((1, PAIR))] + [ANY] * len(plumb.out_shape),
        out_shape=[_sds((t, A_WIDTH), BF16), _sds((t, kw), BF16), _sds((t, kw), BF16),
                   _sds((N_HEADS // 2, wext, 2 * Q_BLOCK), F32), _sds((N_HEADS, 128), F32),
                   _sds((1, PAIR), F32), _sds((1, PAIR), F32)] + plumb.out_shape,
        scratch_shapes=[pltpu.VMEM((s, kw), BF16), pltpu.VMEM((N_HEADS, nkc, PAIR, KEY_CHUNK), BF16),
                        pltpu.VMEM((s, kw), F32), pltpu.VMEM((s, kw), F32),
                        pltpu.VMEM((w, 2 * Q_BLOCK), F32), pltpu.VMEM((w, 2 * Q_BLOCK), F32),
                        pltpu.VMEM((w, 2 * Q_BLOCK), BF16), pltpu.VMEM((w, 2 * Q_BLOCK), BF16),
                        pltpu.VMEM((2 * w, Q_BLOCK), BF16)] + plumb.scratch,
        input_output_aliases=plumb.aliases,
        compiler_params=plumb.params(vmem_limit_bytes=VMEM_LIMIT_V7X),
    )(qkv, qkv, qkv, bias2, sink2, gq, gk, y, dy, lse, *plumb.args)
    return plumb.deliver(res)


def _band_tables(prev_chunks):
    prev = prev_chunks * CHUNK
    wext = 2 * prev + Q_BLOCK
    jj = np.arange(wext)[:, None]
    ii = np.arange(Q_BLOCK)[None, :]
    dist = prev + ii - jj
    rel_chunk = (prev // CHUNK + ii // CHUNK) - jj // CHUNK
    allowed = (rel_chunk >= 0) & (rel_chunk <= prev_chunks)
    return dist, allowed


def _alibi_slopes():
    return np.array([2.0 ** (-8.0 * (h + 1) / N_HEADS) for h in range(N_HEADS)], dtype=np.float32)


def _diag_onehot(prev, wext):
    n_diag = wext + Q_BLOCK - 1
    idx = np.clip(prev + Q_BLOCK - 1 - np.arange(n_diag), -A_MAX_REL, A_MAX_REL) + A_MAX_REL
    onehot = np.zeros((n_diag, 2 * A_MAX_REL + 1), np.float32)
    onehot[np.arange(n_diag), idx] = 1.0
    return onehot


def _bias_a(rel_bias):
    prev = A_PREV_CHUNKS * CHUNK
    _, allowed = _band_tables(A_PREV_CHUNKS)
    wext = allowed.shape[0]
    n_diag = wext + Q_BLOCK - 1
    seq = jnp.dot(rel_bias, jnp.asarray(_diag_onehot(prev, wext).T), precision=lax.Precision.HIGHEST)
    seq = jnp.pad(seq, ((0, 0), (0, 1)))
    rows = jnp.broadcast_to(seq[:, None, :], (N_HEADS, Q_BLOCK, n_diag + 1)).reshape(N_HEADS, -1)
    skew = rows[:, :Q_BLOCK * n_diag].reshape(N_HEADS, Q_BLOCK, n_diag)
    tile = jnp.transpose(skew[:, :, Q_BLOCK - 1:Q_BLOCK - 1 + wext], (0, 2, 1))
    return jnp.where(jnp.asarray(allowed)[None], tile, NEG_INF)


def _bias_b():
    dist, allowed = _band_tables(B_PREV_CHUNKS)
    bias = -_alibi_slopes()[:, None, None] * np.abs(dist).astype(np.float32)[None]
    return jnp.asarray(np.where(allowed[None], bias, np.float32(NEG_INF)).astype(np.float32))


def _rel_bias_grad(db_t):
    prev = A_PREV_CHUNKS * CHUNK
    wext = db_t.shape[1]
    n_diag = wext + Q_BLOCK - 1
    wp = n_diag + Q_BLOCK - 1
    xp = jnp.pad(jnp.transpose(db_t, (0, 2, 1)), ((0, 0), (0, 0), (Q_BLOCK - 1, Q_BLOCK - 1)))
    flat = jnp.pad(xp.reshape(N_HEADS, Q_BLOCK * wp), ((0, 0), (0, Q_BLOCK)))
    skew = flat.reshape(N_HEADS, Q_BLOCK, wp + 1)[:, :, :n_diag]
    diag = jnp.sum(skew, axis=1)
    return jnp.dot(diag, jnp.asarray(_diag_onehot(prev, wext)), precision=lax.Precision.HIGHEST)


def _ew(name, fn, ins, out_dtypes):
    r, c = ins[0].shape
    rb = _pick(r, max(16, (1 << 19) // c), 16)
    spec = pl.BlockSpec((rb, c), lambda i: (i, 0))

    def body(*refs):
        vals = fn(*[ref[...] for ref in refs[:len(ins)]])
        for ref, val in zip(refs[len(ins):], vals):
            ref[...] = val.astype(ref.dtype)

    return pl.pallas_call(
        body, name=name, grid=(r // rb,), in_specs=[spec] * len(ins), out_specs=[spec] * len(out_dtypes),
        out_shape=[_sds((r, c), dt) for dt in out_dtypes], compiler_params=_params(),
    )(*ins)


def _cast_into_slot(name, w, chip):
    r, c = w.shape
    rb = _pick(r, max(16, (1 << 19) // c), 16)

    def body(chip_ref, w_ref, o_ref):
        o_ref[...] = w_ref[...].astype(BF16)

    return pl.pallas_call(
        body, name=name,
        grid_spec=pltpu.PrefetchScalarGridSpec(
            num_scalar_prefetch=1, grid=(r // rb,),
            in_specs=[pl.BlockSpec((rb, c), lambda i, chip_ref: (i, 0))],
            out_specs=pl.BlockSpec((None, rb, c), lambda i, chip_ref: (chip_ref[0], i, 0))),
        out_shape=_sds((N_CHIPS, r, c), BF16), compiler_params=_params(),
    )(chip, w)


def _adamw_math(w, g, m, v):
    m = ADAM_B1 * m + (1.0 - ADAM_B1) * g
    v = ADAM_B2 * v + (1.0 - ADAM_B2) * (g * g)
    m_hat = m / (1.0 - ADAM_B1 ** ADAM_STEP)
    v_hat = v / (1.0 - ADAM_B2 ** ADAM_STEP)
    delta = -ADAM_LR * (m_hat / (jnp.sqrt(v_hat) + ADAM_EPS) + ADAM_WD * w)
    return delta, m, v


def _adamw_terms(name, terms, w, m, v):
    r, c = w.shape
    hr = r // 2
    rb = _pick(hr, max(16, (1 << 19) // c), 16)
    nb = hr // rb

    def body(t_ref, w_ref, m_ref, v_ref, g_ref, d_ref, nm_ref, nv_ref):
        g = t_ref[0].astype(F32)
        for k in range(1, N_CHIPS):
            g = g + t_ref[k].astype(F32)
        delta, nm, nv = _adamw_math(w_ref[...], g, m_ref[...], v_ref[...])
        g_ref[...] = g
        d_ref[...] = delta
        nm_ref[...] = nm
        nv_ref[...] = nv

    spec = pl.BlockSpec((rb, c), lambda h, i: (h * nb + i, 0))
    return pl.pallas_call(
        body, name=name, grid=(2, nb),
        in_specs=[pl.BlockSpec((None, N_CHIPS, rb, c), lambda h, i: (h, 0, i, 0)), spec, spec, spec],
        out_specs=[spec] * 4, out_shape=[_sds((r, c), F32)] * 4, compiler_params=_params(),
    )(terms, w, m, v)


def _mesh_place():
    x, y, c = lax.axis_index("x"), lax.axis_index("y"), lax.axis_index("c")
    chips = [(x, 1 - y), (1 - x, y), (1 - x, 1 - y)]
    return x, y, c, chips


def _run_comms(name, comms):
    plumb = _CommPlumbing(comms, 0, 0, 0)
    n_in, n_out = len(plumb.args), len(plumb.out_shape)

    def body(*refs):
        parts = []
        i0, o0, s0 = 0, n_in, n_in + n_out
        for cm in plumb.comms:
            parts.append((refs[i0:i0 + len(cm.ins)], refs[o0:o0 + len(cm.outs)], refs[s0:s0 + len(cm.sems)]))
            i0 += len(cm.ins)
            o0 += len(cm.outs)
            s0 += len(cm.sems)
        plumb.handshake()
        for cm, part in zip(plumb.comms, parts):
            cm.start(*part)
        for cm, part in zip(plumb.comms, parts):
            cm.finish(*part)

    res = pl.pallas_call(
        body, name=name, in_specs=[ANY] * n_in, out_specs=[ANY] * n_out, out_shape=plumb.out_shape,
        scratch_shapes=plumb.scratch, input_output_aliases=plumb.aliases, compiler_params=plumb.params(),
    )(*plumb.args)
    plumb.deliver(res)


def _gather_ici_copies(outs, sems):
    send_sem, recv_sem = sems
    x, y, c, chips = _mesh_place()
    me = 2 * x + y
    sends, recvs = [], []
    for wi in range(len(outs)):
        for k, (tx, ty) in enumerate(chips):
            sems_k = dict(send_sem=send_sem.at[wi * 3 + k], recv_sem=recv_sem.at[wi * 3 + k],
                          device_id=(tx, ty, c), device_id_type=MESH)
            own = outs[wi].at[me, c]
            sends.append(pltpu.make_async_remote_copy(src_ref=own, dst_ref=own, **sems_k))
            slab = outs[wi].at[2 * tx + ty, c]
            recvs.append(pltpu.make_async_remote_copy(src_ref=slab, dst_ref=slab, **sems_k))
    return sends, recvs


def _gather_d2d_copies(outs, sems):
    send_sem, recv_sem = sems
    x, y, c, chips = _mesh_place()
    sends, recvs = [], []
    for wi in range(len(outs)):
        for k, (tx, ty) in enumerate(chips):
            sems_k = dict(send_sem=send_sem.at[wi * 3 + k], recv_sem=recv_sem.at[wi * 3 + k],
                          device_id=(x, y, 1 - c), device_id_type=MESH)
            mine = outs[wi].at[2 * tx + ty, c]
            theirs = outs[wi].at[2 * tx + ty, 1 - c]
            sends.append(pltpu.make_async_remote_copy(src_ref=mine, dst_ref=mine, **sems_k))
            recvs.append(pltpu.make_async_remote_copy(src_ref=theirs, dst_ref=theirs, **sems_k))
    return sends, recvs


def _gather_comm(peers, bufs, n_sems, start, finish):
    n = len(bufs)
    return _Comm(peers, bufs, [_sds(g.shape, g.dtype) for g in bufs], {i: i for i in range(n)},
                 [pltpu.SemaphoreType.DMA((3 * n,))] * n_sems, start, finish)


def _gather_ici(bufs):
    def start(ins, outs, sems):
        for cp in _gather_ici_copies(outs, sems)[0]:
            cp.start()

    def finish(ins, outs, sems):
        sends, recvs = _gather_ici_copies(outs, sems)
        for cp in recvs:
            cp.wait_recv()
        for cp in sends:
            cp.wait_send()

    return _gather_comm(("chips",), bufs, 2, start, finish)


def _gather_d2d(gathered):
    def start(ins, outs, sems):
        for cp in _gather_d2d_copies(outs, sems)[0]:
            cp.start()

    def finish(ins, outs, sems):
        sends, recvs = _gather_d2d_copies(outs, sems)
        for cp in recvs:
            cp.wait_recv()
        for cp in sends:
            cp.wait_send()

    return _gather_comm(("sibling",), gathered, 2, start, finish)


def _exchange_halves(grads):
    n = len(grads)

    def copies(ins, outs, sems):
        send_sem, recv_sem = sems
        x, y, c, _ = _mesh_place()
        return [pltpu.make_async_remote_copy(
            src_ref=ins[wi].at[t, 1 - c], dst_ref=outs[wi].at[t],
            send_sem=send_sem.at[wi * N_CHIPS + t], recv_sem=recv_sem.at[wi * N_CHIPS + t],
            device_id=(x, y, 1 - c), device_id_type=MESH) for wi in range(n) for t in range(N_CHIPS)]

    def start(ins, outs, sems):
        for cp in copies(ins, outs, sems):
            cp.start()

    def finish(ins, outs, sems):
        for cp in copies(ins, outs, sems):
            cp.wait()

    return _Comm(("sibling",), grads, [_sds((N_CHIPS,) + g.shape[2:], g.dtype) for g in grads], {},
                 [pltpu.SemaphoreType.DMA((N_CHIPS * n,)), pltpu.SemaphoreType.DMA((N_CHIPS * n,))], start, finish)


def _scatter_ici(sums):
    n = len(sums)

    def copies(ins, outs, sems):
        local_sem, send_sem, recv_sem = sems
        x, y, c, chips = _mesh_place()
        me = 2 * x + y
        local, sends, recvs = [], [], []
        for wi in range(n):
            local.append(pltpu.make_async_copy(ins[wi].at[me], outs[wi].at[c, 0], local_sem.at[wi]))
            for k, (tx, ty) in enumerate(chips):
                sems_k = dict(send_sem=send_sem.at[wi * 3 + k], recv_sem=recv_sem.at[wi * 3 + k],
                              device_id=(tx, ty, c), device_id_type=MESH)
                land = outs[wi].at[c, k + 1]
                sends.append(pltpu.make_async_remote_copy(src_ref=ins[wi].at[2 * tx + ty], dst_ref=land, **sems_k))
                recvs.append(pltpu.make_async_remote_copy(src_ref=land, dst_ref=land, **sems_k))
        return local, sends, recvs

    def start(ins, outs, sems):
        local, sends, _ = copies(ins, outs, sems)
        for cp in local + sends:
            cp.start()

    def finish(ins, outs, sems):
        local, sends, recvs = copies(ins, outs, sems)
        for cp in local:
            cp.wait()
        for cp in recvs:
            cp.wait_recv()
        for cp in sends:
            cp.wait_send()

    return _Comm(("chips",), sums, [_sds((2, N_CHIPS) + s.shape[1:], s.dtype) for s in sums], {},
                 [pltpu.SemaphoreType.DMA((n,)), pltpu.SemaphoreType.DMA((3 * n,)), pltpu.SemaphoreType.DMA((3 * n,))],
                 start, finish)


def _scatter_d2d(terms):
    n = len(terms)

    def copies(outs, sems):
        send_sem, recv_sem = sems
        x, y, c, _ = _mesh_place()
        sends, recvs = [], []
        for wi in range(n):
            sems_w = dict(send_sem=send_sem.at[wi], recv_sem=recv_sem.at[wi],
                          device_id=(x, y, 1 - c), device_id_type=MESH)
            sends.append(pltpu.make_async_remote_copy(src_ref=outs[wi].at[c], dst_ref=outs[wi].at[c], **sems_w))
            recvs.append(pltpu.make_async_remote_copy(src_ref=outs[wi].at[1 - c], dst_ref=outs[wi].at[1 - c], **sems_w))
        return sends, recvs

    def start(ins, outs, sems):
        for cp in copies(outs, sems)[0]:
            cp.start()

    def finish(ins, outs, sems):
        sends, recvs = copies(outs, sems)
        for cp in recvs:
            cp.wait_recv()
        for cp in sends:
            cp.wait_send()

    return _Comm(("sibling",), terms, [_sds(t.shape, t.dtype) for t in terms], {i: i for i in range(n)},
                 [pltpu.SemaphoreType.DMA((n,)), pltpu.SemaphoreType.DMA((n,))], start, finish)


def _chip_sum(name, grad, got, core):
    _, _, hr, c = grad.shape
    rb = _pick(hr, max(16, (1 << 19) // c), 16)

    def body(core_ref, a_ref, b_ref, o_ref):
        o_ref[...] = (a_ref[...].astype(F32) + b_ref[...].astype(F32)).astype(BF16)

    out_spec = pl.BlockSpec((None, rb, c), lambda t, i, core_ref: (t, i, 0))
    return pl.pallas_call(
        body, name=name,
        grid_spec=pltpu.PrefetchScalarGridSpec(
            num_scalar_prefetch=1, grid=(N_CHIPS, hr // rb),
            in_specs=[pl.BlockSpec((None, None, rb, c), lambda t, i, core_ref: (t, core_ref[0], i, 0)), out_spec],
            out_specs=out_spec),
        out_shape=_sds((N_CHIPS, hr, c), BF16), compiler_params=_params(),
    )(core, grad, got)


def _all_reduce_small(pack):
    r = pack.shape[0]

    def body(p_ref, o_ref, land_ref, send_sem, recv_sem):
        x, y, c, _ = _mesh_place()
        me = 4 * x + 2 * y + c
        flips = [(k >> 2 & 1, k >> 1 & 1, k & 1) for k in range(1, N_DEV)]

        def peer(fx, fy, fc):
            return (1 - x if fx else x, 1 - y if fy else y, 1 - c if fc else c)

        land_ref[me] = p_ref[...]
        sent = []
        for k, flip in enumerate(flips):
            cp = pltpu.make_async_remote_copy(
                src_ref=p_ref, dst_ref=land_ref.at[me], send_sem=send_sem.at[k], recv_sem=recv_sem.at[k],
                device_id=peer(*flip), device_id_type=MESH)
            cp.start()
            sent.append(cp)
        for k, flip in enumerate(flips):
            px, py, pc = peer(*flip)
            slot = land_ref.at[4 * px + 2 * py + pc]
            pltpu.make_async_remote_copy(
                src_ref=slot, dst_ref=slot, send_sem=send_sem.at[k], recv_sem=recv_sem.at[k],
                device_id=(px, py, pc), device_id_type=MESH).wait_recv()
        total = land_ref[0]
        for d in range(1, N_DEV):
            total = total + land_ref[d]
        o_ref[...] = total
        for cp in sent:
            cp.wait_send()

    vmem = pl.BlockSpec(memory_space=pltpu.VMEM)
    return pl.pallas_call(
        body, name="all_reduce_small", in_specs=[vmem], out_specs=vmem, out_shape=_sds((r, 128), F32),
        scratch_shapes=[pltpu.VMEM((N_DEV, r, 128), F32), pltpu.SemaphoreType.DMA((N_DEV - 1,)),
                        pltpu.SemaphoreType.DMA((N_DEV - 1,))],
    )(pack)


PACK_TILE = 8 * 128


def _pack(items):
    rows, i = [], 0
    while i < len(items):
        j = i
        while j < len(items) and items[j].size == items[i].size:
            j += 1
        group = jnp.stack([it.reshape(-1).astype(F32) for it in items[i:j]])
        rows.append(jnp.pad(group, ((0, 0), (0, -group.shape[1] % PACK_TILE))).reshape(-1, 128))
        i = j
    return jnp.concatenate(rows, axis=0)


def _unpack(pack, shapes):
    out, row = [], 0
    for shp in shapes:
        size = int(np.prod(shp))
        nrow = -(-size // PACK_TILE) * (PACK_TILE // 128)
        out.append(pack[row:row + nrow].reshape(-1)[:size].reshape(shp))
        row += nrow
    return out


BIG = ["ffn1_w_gu", "ffn1_w_down", "w_in", "w_gate", "w_proj_a", "w_proj_b", "w_out",
       "ffn2_w_gu", "ffn2_w_down", "w_ple_gate", "w_ple_proj"]
SMALL = ["ffn1_norm", "mix_norm", "ffn2_norm", "ple_norm", "a_q_norm", "a_k_norm", "b_q_norm", "b_k_norm",
         "a_rel_bias", "b_sinks"]
WEIGHTS = ["ffn1_norm", "ffn1_w_gu", "ffn1_w_down", "mix_norm", "w_in", "a_q_norm", "a_k_norm", "a_rel_bias",
           "b_q_norm", "b_k_norm", "b_sinks", "w_gate", "w_proj_a", "w_proj_b", "w_out", "ffn2_norm",
           "ffn2_w_gu", "ffn2_w_down", "ple_norm", "w_ple_gate", "w_ple_proj"]
ATTN_A = dict(prev=A_PREV_CHUNKS * CHUNK, group=1, kw=A_WIDTH, qblk=0, kblk=1, vblk=2)
ATTN_B = dict(prev=B_PREV_CHUNKS * CHUNK, group=N_HEADS // B_KV_HEADS, kw=B_KV_WIDTH, qblk=3,
              kblk=4 * A_WIDTH // B_KV_WIDTH, vblk=4 * A_WIDTH // B_KV_WIDTH + 1)


def _cast_epilogue(accs, extras, outs, ij):
    for acc, out in zip(accs, outs):
        out[...] = acc.astype(out.dtype)


GATHER_FIRST = ["ffn1_w_gu", "ffn1_w_down"]
ROW_SHARDED = ("ffn1_w_down", "ffn2_w_down", "w_out", "w_ple_gate")


def _slotted(name, grad):
    if name == "w_in":
        rows, cols = grad.shape
        grad = jnp.transpose(grad.reshape(rows, N_CHIPS, cols // N_CHIPS), (1, 0, 2))
    elif name in ROW_SHARDED:
        grad = grad.reshape(N_CHIPS, grad.shape[0] // N_CHIPS, grad.shape[1])
    return grad.reshape(N_CHIPS, 2, grad.shape[1] // 2, grad.shape[2])


def _local_step(xt, pt, tgt, n_batch, bufs, small, core):
    t, d = xt.shape
    tm = _pick(t, ROW_TILE, 8)
    tk = _pick(t, ROW_TILE, 8)
    nt = t // tm
    row = pl.BlockSpec((tm, d), lambda i, j, k: (i, 0))
    gs = bufs["w_gate"].shape[2]
    ps = bufs["w_proj_a"].shape[2]
    es = bufs["w_ple_proj"].shape[2]
    pdim = pt.shape[1]
    ncols = N_CHIPS * bufs["w_in"].shape[2]
    tin = ncols // 2
    assert 2 * gs == d and 4 * ps == d and 4 * es == d and tin % 128 == 0

    w = {}
    halves = {n: b.reshape(N_CHIPS, 2, b.shape[1] // 2, b.shape[2]) for n, b in bufs.items()}

    def publish(names, arrays):
        for name, g in zip(names, arrays):
            g = g.reshape(N_CHIPS, 2 * g.shape[2], g.shape[3])
            if name in ROW_SHARDED:
                g = g.reshape(N_CHIPS * g.shape[1], g.shape[2])
            elif name == "w_in":
                g = jnp.transpose(g, (1, 0, 2)).reshape(g.shape[1], N_CHIPS * g.shape[2])
            w[name] = g

    class GatherPipe:
        def __init__(self, names):
            self.names = names
            self.stage = None

        def ici(self):
            self.stage = _gather_ici(self.bufs())
            return self.stage

        def d2d(self):
            self.stage = _gather_d2d(self.bufs())
            return self.stage

        def bufs(self):
            return self.stage.results if self.stage is not None else [halves[n] for n in self.names]

        def publish(self):
            publish(self.names, self.stage.results)

    class GradPipe:
        def __init__(self, names):
            self.names = names

        def exchange(self, grads):
            self.grads = [_slotted(n, g) for n, g in zip(self.names, grads)]
            self.x = _exchange_halves(self.grads)
            return self.x

        def scatter(self):
            self.sums = [_chip_sum("chip_sum_" + n, g, got, core)
                         for n, g, got in zip(self.names, self.grads, self.x.results)]
            self.s = _scatter_ici(self.sums)
            return self.s

        def forward(self):
            self.f = _scatter_d2d(self.s.results)
            return self.f

        def terms(self):
            return dict(zip(self.names, self.f.results))

    g_first = GatherPipe(GATHER_FIRST)
    n1 = _rms_fwd("ffn1_norm", xt, small["ffn1_norm"], comms=[g_first.ici()])
    _run_comms("gather_first_pass", [g_first.d2d()])
    g_first.publish()
    g_in, g_proj, g_ple = GatherPipe(["w_in", "w_gate"]), GatherPipe(["w_proj_a", "w_proj_b", "w_out"]), \
        GatherPipe(["w_ple_gate", "w_ple_proj"])
    g_down2, g_up2 = GatherPipe(["ffn2_w_down"]), GatherPipe(["ffn2_w_gu"])
    h1, un, ffn1_saved = _ffn_fwd("ffn1", xt, n1, w["ffn1_w_gu"], w["ffn1_w_down"], small["mix_norm"],
                                  {"up": lambda: [g_in.ici()], "down": lambda: [g_in.d2d(), g_proj.ici()]})
    g_in.publish()
    w_in, wgate = w["w_in"], w["w_gate"]
    (qkv,) = _mm(
        "qkv", "nn", (nt, 2, 1),
        [(un, row, w_in, pl.BlockSpec((d, tin), lambda i, j, k: (0, j)))], [],
        [(_sds((t, ncols), BF16), pl.BlockSpec((tm, tin), lambda i, j, k: (i, j)))], (tm, tin), _cast_epilogue,
        j_outer=True, comms=[g_proj.d2d(), g_ple.ici()])
    g_proj.publish()
    wpa, wpb, wout = w["w_proj_a"], w["w_proj_b"], w["w_out"]

    def gate_epilogue(accs, extras, outs, ij):
        outs[0][...] = jax.nn.sigmoid(accs[0]).astype(BF16)

    (gates,) = _mm(
        "gate", "nn", (nt, 4, 1),
        [(un, row, wgate, pl.BlockSpec((None, d, gs), lambda i, j, k: (j, 0, 0)))], [],
        [(_sds((2, t, d), BF16), pl.BlockSpec((None, tm, gs), lambda i, j, k: (j // 2, i, j % 2)))],
        (tm, gs), gate_epilogue, j_outer=True, chunked=True, comms=[g_ple.d2d(), g_down2.ici()])
    g_ple.publish()
    wpg, wpe = w["w_ple_gate"], w["w_ple_proj"]

    bias_a = _pair_bias(_bias_a(small["a_rel_bias"][0]))
    bias_b = _pair_bias(_bias_b())
    sink_a = _pair_rows(jnp.full((N_HEADS, 128), NEG_INF, F32))
    sink_b = _pair_rows(jnp.broadcast_to(small["b_sinks"][0][:, None], (N_HEADS, 128)))
    gqa, gka, gqb, gkb = [jnp.tile(small[k], (1, 2)) for k in ("a_q_norm", "a_k_norm", "b_q_norm", "b_k_norm")]
    ya, lse_a = _attn_fwd("attn_a_fwd", qkv, bias_a, sink_a, gqa, gka, ATTN_A, n_batch,
                          comms=[g_down2.d2d(), g_up2.ici()])
    g_down2.publish()
    yb, lse_b = _attn_fwd("attn_b_fwd", qkv, bias_b, sink_b, gqb, gkb, ATTN_B, n_batch, comms=[g_up2.d2d()])
    g_up2.publish()

    def merge_epilogue(accs, extras, outs, ij):
        pa, pb = accs
        outs[0][...] = (extras[0][...].astype(F32) * pa + extras[1][...].astype(F32) * pb).astype(BF16)
        outs[1][...] = pa.astype(BF16)
        outs[2][...] = pb.astype(BF16)

    y_spec = pl.BlockSpec((tm, A_WIDTH), lambda i, j, k: (i, 0))
    proj_spec = pl.BlockSpec((None, A_WIDTH, ps), lambda i, j, k: (j, 0, 0))
    tile_ps = pl.BlockSpec((tm, ps), lambda i, j, k: (i, j))
    merged, pa, pb = _mm(
        "proj_merge", "nn", (nt, 4, 1),
        [(ya, y_spec, wpa, proj_spec), (yb, y_spec, wpb, proj_spec)],
        [(gates, pl.BlockSpec((None, tm, ps), lambda i, j, k: (0, i, j))),
         (gates, pl.BlockSpec((None, tm, ps), lambda i, j, k: (1, i, j)))],
        [(_sds((t, d), BF16), tile_ps)] * 3, (tm, ps), merge_epilogue)

    h2, n2 = _mm(
        "out_proj", "nn", (nt, 1, 1),
        [(merged, row, wout, pl.BlockSpec((d, d), lambda i, j, k: (0, 0)))],
        [(h1, row), (small["ffn2_norm"], pl.BlockSpec((1, d), lambda i, j, k: (0, 0)))],
        [(_sds((t, d), F32), row), (_sds((t, d), BF16), row)], (tm, d), _residual_norm_epilogue(1.0))

    h3, n3, ffn2_saved = _ffn_fwd("ffn2", h2, n2, w["ffn2_w_gu"], w["ffn2_w_down"], small["ple_norm"], {})
    tile_es = pl.BlockSpec((tm, es), lambda i, j, k: (i, j))
    th = _pick(d, 512)

    def head_epilogue(accs, extras, outs, ij):
        h3_ref, tgt_ref = extras
        dy_ref, dpe_ref, dz_ref, loss_ref = outs
        pg = jax.nn.sigmoid(accs[0])
        pev = accs[1]
        diff = h3_ref[...] + pg * pev - tgt_ref[...]
        dy = diff * (1.0 / d)
        dy_ref[...] = dy
        dpe_ref[...] = (dy * pg).astype(BF16)
        dz_ref[...] = (dy * pev * pg * (1.0 - pg)).astype(BF16)
        _accumulate(loss_ref, jnp.full(loss_ref.shape, jnp.sum(diff * diff), F32), (ij[0] == 0) & (ij[1] == 0))

    tile_h = pl.BlockSpec((tm, th), lambda i, j, k: (i, j))
    dy, dpe, dz, loss_acc = _mm(
        "ple_gate_loss", "nn", (nt, 4, 1),
        [(n3, row, wpg, pl.BlockSpec((d, es), lambda i, j, k: (0, j))),
         (pt, pl.BlockSpec((tm, pdim), lambda i, j, k: (i, 0)), wpe, pl.BlockSpec((None, pdim, es), lambda i, j, k: (j, 0, 0)))],
        [(h3, tile_es), (tgt, tile_es)],
        [(_sds((t, d), F32), tile_es), (_sds((t, d), BF16), tile_es), (_sds((t, d), BF16), tile_es),
         (_sds((8, 128), F32), pl.BlockSpec((8, 128), lambda i, j, k: (0, 0)))],
        (tm, es), head_epilogue, j_outer=True, chunked=True)
    loss = 0.5 * loss_acc[0, 0] / d

    nk = t // tk
    (dwpe,) = _mm(
        "d_w_ple_proj", "tn", (1, 4, nk),
        [(pt, pl.BlockSpec((tk, pdim), lambda i, j, k: (k, 0)), dpe, pl.BlockSpec((tk, es), lambda i, j, k: (k, j)))],
        [], [(_sds((4, pdim, es), BF16), pl.BlockSpec((None, pdim, es), lambda i, j, k: (j, 0, 0)))],
        (pdim, es), _cast_epilogue)

    def dense_grad(name, a, dyb, comms=()):
        (res,) = _mm(
            name, "tn", (1, d // th, nk),
            [(a, pl.BlockSpec((tk, d), lambda i, j, k: (k, 0)), dyb, pl.BlockSpec((tk, th), lambda i, j, k: (k, j)))],
            [], [(_sds((d, d), BF16), pl.BlockSpec((d, th), lambda i, j, k: (0, j)))], (d, th), _cast_epilogue,
            comms=comms)
        return res

    dwpg = dense_grad("d_w_ple_gate", n3, dz)
    tmn = _pick(t, ROW_TILE, 8)
    extras, outs = _rms_bwd_io(h3, small["ple_norm"], dy, tmn)
    dh3, dh3_b, d_ple_norm = _mm(
        "d_ple_norm", "nt", (t // tmn, 1, 1),
        [(dz, pl.BlockSpec((tmn, d), lambda i, j, k: (i, 0)), wpg, pl.BlockSpec((d, d), lambda i, j, k: (0, 0)))],
        extras, outs, (tmn, d), _rms_bwd_epilogue)

    up2, down2, ple = GradPipe(["ffn2_w_gu"]), GradPipe(["ffn2_w_down"]), GradPipe(["w_ple_gate", "w_ple_proj"])
    proj = GradPipe(["w_proj_a", "w_proj_b", "w_out"])
    dh2, dh2_b, d_ffn2_norm, dwgu2, dwd2 = _ffn_bwd(
        "ffn2", dh3, dh3_b, h2, small["ffn2_norm"], w["ffn2_w_gu"], w["ffn2_w_down"], ffn2_saved,
        {"dnorm": lambda dwgu, dwd: [up2.exchange([dwgu]), down2.exchange([dwd]), ple.exchange([dwpg, dwpe])]})

    def dmerge_epilogue(accs, extras, outs, ij):
        dmo = accs[0]
        g_ref, pa_ref, pb_ref = extras
        dg_ref, dpa_ref, dpb_ref = outs
        ga = g_ref[0].astype(F32)
        gb = g_ref[1].astype(F32)
        dg_ref[0] = (dmo * pa_ref[...].astype(F32) * ga * (1.0 - ga)).astype(BF16)
        dg_ref[1] = (dmo * pb_ref[...].astype(F32) * gb * (1.0 - gb)).astype(BF16)
        dpa_ref[...] = (dmo * ga).astype(BF16)
        dpb_ref[...] = (dmo * gb).astype(BF16)

    g_spec = pl.BlockSpec((2, tm, th), lambda i, j, k: (0, i, j))
    dgates, dpa, dpb = _mm(
        "d_merge", "nt", (nt, d // th, 1),
        [(dh2_b, row, wout, pl.BlockSpec((th, d), lambda i, j, k: (j, 0)))],
        [(gates, g_spec), (pa, tile_h), (pb, tile_h)],
        [(_sds((2, t, d), BF16), g_spec), (_sds((t, d), BF16), tile_h), (_sds((t, d), BF16), tile_h)],
        (tm, th), dmerge_epilogue, j_outer=True, chunked=True, comms=[down2.scatter()])
    dwout = dense_grad("d_w_out", merged, dh2_b, comms=[down2.forward(), ple.scatter()])

    yk_spec = pl.BlockSpec((tk, A_WIDTH), lambda i, j, k: (k, 0))
    dk_spec = pl.BlockSpec((tk, ps), lambda i, j, k: (k, j))
    dproj = (_sds((4, A_WIDTH, ps), BF16), proj_spec)
    dwpa, dwpb = _mm(
        "d_w_proj", "tn", (1, 4, nk),
        [(ya, yk_spec, dpa, dk_spec), (yb, yk_spec, dpb, dk_spec)], [], [dproj, dproj], (A_WIDTH, ps), _cast_epilogue,
        comms=[ple.forward()])
    dproj_a = pl.BlockSpec((tm, ps), lambda i, j, k: (i, k))
    wproj_k = pl.BlockSpec((None, A_WIDTH, ps), lambda i, j, k: (k, 0, 0))
    dya, dyb = _mm(
        "d_attn_out", "nt", (nt, 1, 4),
        [(dpa, dproj_a, wpa, wproj_k), (dpb, dproj_a, wpb, wproj_k)], [],
        [(_sds((t, A_WIDTH), BF16), y_spec)] * 2, (tm, A_WIDTH), _cast_epilogue,
        comms=[proj.exchange([dwpa, dwpb, dwout])])

    dqa, dka, dva, dbias_a, _, dgqa, dgka = _attn_bwd(
        "attn_a_bwd", qkv, bias_a, sink_a, gqa, gka, ya, dya, lse_a, ATTN_A, n_batch, True,
        comms=[up2.scatter(), proj.scatter()])
    dqb, dkb, dvb, _, dsink_b, dgqb, dgkb = _attn_bwd(
        "attn_b_bwd", qkv, bias_b, sink_b, gqb, gkb, yb, dyb, lse_b, ATTN_B, n_batch, False,
        comms=[up2.forward(), proj.forward()])
    dqkv = jnp.concatenate([dqa, dka, dva, dqb, dkb, dvb], axis=1)

    (dwgate,) = _mm(
        "d_w_gate", "tn", (1, 4, nk),
        [(un, pl.BlockSpec((tk, d), lambda i, j, k: (k, 0)),
          dgates, pl.BlockSpec((None, tk, gs), lambda i, j, k: (j // 2, k, j % 2)))],
        [], [(_sds((4, d, gs), BF16), pl.BlockSpec((None, d, gs), lambda i, j, k: (j, 0, 0)))], (d, gs), _cast_epilogue)
    (dwin,) = _mm(
        "d_w_in", "tn", (1, 2, nk),
        [(un, pl.BlockSpec((tk, d), lambda i, j, k: (k, 0)), dqkv, pl.BlockSpec((tk, tin), lambda i, j, k: (k, j)))],
        [], [(_sds((d, ncols), BF16), pl.BlockSpec((d, tin), lambda i, j, k: (0, j)))], (d, tin), _cast_epilogue)

    mixer = GradPipe(["w_in", "w_gate"])
    extras, outs = _rms_bwd_io(h1, small["mix_norm"], dh2, tmn)
    dh1, dh1_b, d_mix_norm = _mm(
        "d_mix_norm", "nt", (t // tmn, 1, 6),
        [(dgates, pl.BlockSpec((None, tmn, gs), lambda i, j, k: (jnp.minimum(k, 3) // 2, i, jnp.minimum(k, 3) % 2)),
          wgate, pl.BlockSpec((None, d, gs), lambda i, j, k: (jnp.minimum(k, 3), 0, 0))),
         (dqkv, pl.BlockSpec((tmn, tin), lambda i, j, k: (i, jnp.maximum(k - 4, 0))),
          w_in, pl.BlockSpec((d, tin), lambda i, j, k: (0, jnp.maximum(k - 4, 0))))],
        extras, outs, (tmn, d), _rms_bwd_epilogue, steps=[4, 2],
        comms=[mixer.exchange([dwin, dwgate])])

    up1 = GradPipe(["ffn1_w_gu"])
    down1 = GradPipe(["ffn1_w_down"])
    dx, _, d_ffn1_norm, _, _ = _ffn_bwd(
        "ffn1", dh1, dh1_b, xt, small["ffn1_norm"], w["ffn1_w_gu"], w["ffn1_w_down"], ffn1_saved,
        {"dwgu": lambda: [mixer.scatter()],
         "dwd": lambda dwgu: [mixer.forward(), up1.exchange([dwgu])],
         "dnorm": lambda dwgu, dwd: [up1.scatter(), down1.exchange([dwd])]})
    _run_comms("grad_tail_scatter", [up1.forward(), down1.scatter()])
    _run_comms("grad_tail_forward", [down1.forward()])
    terms = {}
    for pipe in (up2, down2, ple, proj, mixer, up1, down1):
        terms.update(pipe.terms())

    def fold(v):
        return v[0, :HEAD_DIM] + v[0, HEAD_DIM:]

    small_grads = {"ffn1_norm": d_ffn1_norm, "mix_norm": d_mix_norm, "ffn2_norm": d_ffn2_norm,
                   "ple_norm": d_ple_norm, "a_q_norm": fold(dgqa), "a_k_norm": fold(dgka),
                   "b_q_norm": fold(dgqb), "b_k_norm": fold(dgkb), "a_rel_bias": _rel_bias_grad(_unpair_bias(dbias_a)),
                   "b_sinks": jnp.sum(dsink_b, axis=1)}
    return loss, dx, terms, small_grads


def kernel(x, p, ffn1_norm, ffn1_w_gu, ffn1_w_down, mix_norm, w_in, a_q_norm, a_k_norm, a_rel_bias, b_q_norm, b_k_norm, b_sinks, w_gate, w_proj_a, w_proj_b, w_out, ffn2_norm, ffn2_w_gu, ffn2_w_down, ple_norm, w_ple_gate, w_ple_proj, loss_target, m_ffn1_norm, m_ffn1_w_gu, m_ffn1_w_down, m_mix_norm, m_w_in, m_a_q_norm, m_a_k_norm, m_a_rel_bias, m_b_q_norm, m_b_k_norm, m_b_sinks, m_w_gate, m_w_proj_a, m_w_proj_b, m_w_out, m_ffn2_norm, m_ffn2_w_gu, m_ffn2_w_down, m_ple_norm, m_w_ple_gate, m_w_ple_proj, v_ffn1_norm, v_ffn1_w_gu, v_ffn1_w_down, v_mix_norm, v_w_in, v_a_q_norm, v_a_k_norm, v_a_rel_bias, v_b_q_norm, v_b_k_norm, v_b_sinks, v_w_gate, v_w_proj_a, v_w_proj_b, v_w_out, v_ffn2_norm, v_ffn2_w_gu, v_ffn2_w_down, v_ple_norm, v_w_ple_gate, v_w_ple_proj):
    given = dict(locals())
    n_batch, s, d = x.shape
    t = n_batch * s
    xt = x.reshape(t, d)
    pt = p.reshape(t, p.shape[-1])
    tgt = loss_target.reshape(t, d)

    chip = (2 * lax.axis_index("x") + lax.axis_index("y")).astype(jnp.int32).reshape(1)
    bufs = {name: _cast_into_slot("cast_" + name, given[name][0], chip) for name in BIG}
    small = {name: given[name] for name in SMALL}
    core = lax.axis_index("c").astype(jnp.int32).reshape(1)
    loss, dx, terms, small_grads = _local_step(xt, pt, tgt, n_batch, bufs, small, core)

    grads, deltas, new_m, new_v = {}, {}, {}, {}
    for name in BIG:
        gw, dl, nm, nv = _adamw_terms("adamw_" + name, terms[name], given[name][0], given["m_" + name][0],
                                      given["v_" + name][0])
        grads[name], deltas[name], new_m[name], new_v[name] = gw[None], dl[None], nm[None], nv[None]

    small_shapes = [given[name].shape for name in SMALL] + [()]
    g_pack = _all_reduce_small(_pack([small_grads[name] for name in SMALL] + [loss]))
    zero = jnp.zeros((), F32)
    w_pack = _pack([given[name] for name in SMALL] + [zero])
    m_pack = _pack([given["m_" + name] for name in SMALL] + [zero])
    v_pack = _pack([given["v_" + name] for name in SMALL] + [zero])
    d_pack, nm_pack, nv_pack = _ew("adamw_small", lambda wv, gv, mv, vv: _adamw_math(wv, gv, mv, vv),
                                   [w_pack, g_pack, m_pack, v_pack], [F32] * 3)
    g_small = _unpack(g_pack, small_shapes)
    loss_total = g_small[-1]
    for name, gv, dv, mv, vv in zip(SMALL, g_small, _unpack(d_pack, small_shapes), _unpack(nm_pack, small_shapes),
                                    _unpack(nv_pack, small_shapes)):
        grads[name], deltas[name], new_m[name], new_v[name] = gv, dv, mv, vv

    return (loss_total, dx.reshape(x.shape), *[grads[n] for n in WEIGHTS], *[deltas[n] for n in WEIGHTS],
            *[new_m[n] for n in WEIGHTS], *[new_v[n] for n in WEIGHTS])
```

```python
import functools

import numpy as np
import jax
import jax.numpy as jnp
from jax import lax
from jax.experimental import pallas as pl
from jax.experimental.pallas import tpu as pltpu

F32 = jnp.float32
BF16 = jnp.bfloat16

CHUNK = 64
HEAD_DIM = 64
A_PREV_CHUNKS = 8
A_MAX_REL = 128
N_HEADS = 8
B_KV_HEADS = 2
B_PREV_CHUNKS = 2
A_WIDTH = N_HEADS * HEAD_DIM
B_KV_WIDTH = B_KV_HEADS * HEAD_DIM
EPS = 1e-6
NEG_INF = -1e30
ATTN_SCALE = HEAD_DIM ** -0.5
Q_BLOCK = 128
PAIR = 2 * HEAD_DIM

ADAM_LR = 0.001
ADAM_B1 = 0.9
ADAM_B2 = 0.999
ADAM_EPS = 1e-08
ADAM_WD = 0.01
ADAM_STEP = 10

N_CHIPS = 4
N_DEV = 8
VMEM_LIMIT_V7X = 56 * 1024 * 1024
ROW_TILE = 1024
MESH = pl.DeviceIdType.MESH
LOCAL_COPY_PRIORITY = 1
COLLECTIVE_IDS = {("sibling",): 1, ("chips",): 2, ("chips", "sibling"): 3}
ANY = pl.BlockSpec(memory_space=pl.ANY)

_DN = {
    "nn": (((1,), (0,)), ((), ())),
    "nt": (((1,), (1,)), ((), ())),
    "tn": (((0,), (0,)), ((), ())),
}


def _pick(n, target, mult=128):
    best = None
    for d in range(mult, min(n, target) + 1, mult):
        if n % d == 0:
            best = d
    return n if best is None else best


def _dot(a, b, mode):
    return lax.dot_general(a.astype(BF16), b.astype(BF16), _DN[mode], preferred_element_type=F32)


def _params():
    return pltpu.CompilerParams(vmem_limit_bytes=VMEM_LIMIT_V7X)


class _Comm:
    def __init__(self, peers, ins, outs, aliases, sems, start, finish):
        self.peers = peers
        self.ins, self.outs, self.aliases, self.sems = list(ins), list(outs), dict(aliases), list(sems)
        self.start, self.finish = start, finish
        self.results = None


class _CommPlumbing:
    def __init__(self, comms, n_in, n_out, n_scratch):
        self.comms = list(comms)
        self.n_in, self.n_out, self.n_scratch = n_in, n_out, n_scratch
        self.args = [a for cm in self.comms for a in cm.ins]
        self.out_shape = [o for cm in self.comms for o in cm.outs]
        self.scratch = [s for cm in self.comms for s in cm.sems]
        self.aliases = {}
        i0, o0 = n_in, n_out
        for cm in self.comms:
            for a, b in cm.aliases.items():
                self.aliases[i0 + a] = o0 + b
            i0 += len(cm.ins)
            o0 += len(cm.outs)

    def _parts(self, in_refs, out_refs, scratch_refs):
        parts = []
        i0, o0, s0 = self.n_in, self.n_out, self.n_scratch
        for cm in self.comms:
            parts.append((in_refs[i0:i0 + len(cm.ins)], out_refs[o0:o0 + len(cm.outs)],
                          scratch_refs[s0:s0 + len(cm.sems)]))
            i0 += len(cm.ins)
            o0 += len(cm.outs)
            s0 += len(cm.sems)
        return parts

    def kinds(self):
        return sorted(set(kind for cm in self.comms for kind in cm.peers))

    def params(self, **kwargs):
        if self.comms:
            kwargs["collective_id"] = COLLECTIVE_IDS[tuple(self.kinds())]
        return pltpu.CompilerParams(**kwargs)

    def handshake(self):
        x, y, c, chips = _mesh_place()
        peers = []
        if "sibling" in self.kinds():
            peers.append((x, y, 1 - c))
        if "chips" in self.kinds():
            peers += [(tx, ty, c) for tx, ty in chips]
        barrier = pltpu.get_barrier_semaphore()
        for peer in peers:
            pl.semaphore_signal(barrier, inc=1, device_id=peer, device_id_type=MESH)
        pl.semaphore_wait(barrier, len(peers))

    def start_at(self, in_refs, out_refs, scratch_refs, first):
        if self.comms:
            parts = self._parts(in_refs, out_refs, scratch_refs)

            @pl.when(first)
            def _():
                self.handshake()
                for cm, part in zip(self.comms, parts):
                    cm.start(*part)

    def finish_at(self, in_refs, out_refs, scratch_refs, last):
        if self.comms:
            parts = self._parts(in_refs, out_refs, scratch_refs)

            @pl.when(last)
            def _():
                for cm, part in zip(self.comms, parts):
                    cm.finish(*part)

    def deliver(self, results):
        o0 = self.n_out
        for cm in self.comms:
            cm.results = list(results[o0:o0 + len(cm.outs)])
            o0 += len(cm.outs)
        return list(results[:self.n_out])


def _swap_ij(spec):
    index_map = spec.index_map
    return pl.BlockSpec(spec.block_shape, lambda j, i, k: index_map(i, j, k))


MXU_COLUMNS_V7X = 256


def _mm(name, mode, grid, pairs, extras, outs, acc_shape, epilogue, steps=None, comms=(), j_outer=False,
        chunked=False):
    ni, nj, nk = grid
    n_in = 2 * len(pairs) + len(extras)
    n_out = len(outs)
    tn = acc_shape[1]
    col_chunks = None
    if chunked:
        assert nk == 1 and steps is None and mode in ("nn", "nt")
        col_chunks = [(c0, min(MXU_COLUMNS_V7X, tn - c0)) for c0 in range(0, tn, MXU_COLUMNS_V7X)]
    n_acc = 0 if chunked else (len(pairs) if steps is None else 1)
    plumb = _CommPlumbing(comms, n_in, n_out, n_acc)
    n_all_in = n_in + len(plumb.args)
    n_all_out = n_out + len(plumb.out_shape)
    if j_outer:
        grid = (nj, ni, nk)
        pairs = [(a, _swap_ij(a_spec), b, _swap_ij(b_spec)) for a, a_spec, b, b_spec in pairs]
        extras = [(e, _swap_ij(e_spec)) for e, e_spec in extras]
        outs = [(o, _swap_ij(o_spec)) for o, o_spec in outs]

    def body(*refs):
        in_refs = refs[:n_all_in]
        out_refs = refs[n_all_in:n_all_in + n_all_out]
        scratch = refs[n_all_in + n_all_out:]
        accs = scratch[:n_acc]
        i = pl.program_id(1 if j_outer else 0)
        j = pl.program_id(0 if j_outer else 1)
        k = pl.program_id(2)
        plumb.start_at(in_refs, out_refs, scratch, (i == 0) & (j == 0) & (k == 0))

        def contrib(p, acc):
            acc[...] += _dot(in_refs[2 * p][...], in_refs[2 * p + 1][...], mode)

        if col_chunks:
            def cols(ref, c0, cs):
                if ref.shape[-1] != tn:
                    return ref
                return ref.at[(slice(None),) * (len(ref.shape) - 1) + (pl.ds(c0, cs),)]

            lhs = [in_refs[2 * p][...] for p in range(len(pairs))]
            for ci, (c0, cs) in enumerate(col_chunks):
                vals = []
                for p in range(len(pairs)):
                    b_ref = in_refs[2 * p + 1]
                    rhs = b_ref[:, c0:c0 + cs] if mode == "nn" else b_ref[c0:c0 + cs, :]
                    vals.append(_dot(lhs[p], rhs, mode))
                epilogue(vals, [cols(r, c0, cs) for r in in_refs[2 * len(pairs):n_in]],
                         [cols(r, c0, cs) for r in out_refs[:n_out]], (i, j * len(col_chunks) + ci))
        else:
            @pl.when(k == 0)
            def _():
                for acc in accs:
                    acc[...] = jnp.zeros(acc.shape, F32)

            if steps is None:
                for p in range(len(pairs)):
                    contrib(p, accs[p])
            else:
                lo = 0
                for p, n in enumerate(steps):
                    pl.when((k >= lo) & (k < lo + n))(functools.partial(contrib, p, accs[0]))
                    lo += n

            @pl.when(k == nk - 1)
            def _():
                epilogue([acc[...] for acc in accs], in_refs[2 * len(pairs):n_in], out_refs[:n_out], (i, j))

        plumb.finish_at(in_refs, out_refs, scratch, (i == ni - 1) & (j == nj - 1) & (k == nk - 1))

    args, in_specs = [], []
    for a, a_spec, b, b_spec in pairs:
        args += [a, b]
        in_specs += [a_spec, b_spec]
    for e, e_spec in extras:
        args.append(e)
        in_specs.append(e_spec)
    res = pl.pallas_call(
        body,
        name=name,
        grid=grid,
        in_specs=in_specs + [ANY] * len(plumb.args),
        out_specs=[s for _, s in outs] + [ANY] * len(plumb.out_shape),
        out_shape=[o for o, _ in outs] + plumb.out_shape,
        scratch_shapes=[pltpu.VMEM(acc_shape, F32) for _ in range(n_acc)] + plumb.scratch,
        input_output_aliases=plumb.aliases,
        compiler_params=plumb.params(vmem_limit_bytes=VMEM_LIMIT_V7X),
    )(*args, *plumb.args)
    return plumb.deliver(res)


def _sds(shape, dtype):
    return jax.ShapeDtypeStruct(shape, dtype)


def _accumulate(ref, value, first):
    @pl.when(first)
    def _():
        ref[...] = value

    @pl.when(jnp.logical_not(first))
    def _():
        ref[...] += value


def _rms_fwd(name, x, gain, comms=()):
    t, d = x.shape
    tm = _pick(t, ROW_TILE, 8)
    steps = t // tm
    plumb = _CommPlumbing(comms, 2, 1, 0)
    n_all_in = 2 + len(plumb.args)
    n_all_out = 1 + len(plumb.out_shape)

    def body(*refs):
        x_ref, g_ref = refs[:2]
        y_ref = refs[n_all_in]
        comm_refs = (refs[:n_all_in], refs[n_all_in:n_all_in + n_all_out], refs[n_all_in + n_all_out:])
        i = pl.program_id(0)
        plumb.start_at(*comm_refs, i == 0)
        xv = x_ref[...]
        rstd = lax.rsqrt(jnp.mean(xv * xv, axis=-1, keepdims=True) + EPS)
        y_ref[...] = (xv * rstd * g_ref[...]).astype(BF16)
        plumb.finish_at(*comm_refs, i == steps - 1)

    res = pl.pallas_call(
        body, name=name, grid=(steps,),
        in_specs=[pl.BlockSpec((tm, d), lambda i: (i, 0)), pl.BlockSpec((1, d), lambda i: (0, 0))]
        + [ANY] * len(plumb.args),
        out_specs=[pl.BlockSpec((tm, d), lambda i: (i, 0))] + [ANY] * len(plumb.out_shape),
        out_shape=[_sds((t, d), BF16)] + plumb.out_shape,
        scratch_shapes=plumb.scratch,
        input_output_aliases=plumb.aliases,
        compiler_params=plumb.params(vmem_limit_bytes=VMEM_LIMIT_V7X),
    )(x, gain, *plumb.args)
    return plumb.deliver(res)[0]


def _rms_bwd_epilogue(accs, extras, outs, ij):
    x_ref, g_ref, r_ref = extras
    dh_ref, dhb_ref, dg_ref = outs
    dn = accs[0]
    xv = x_ref[...]
    rstd = lax.rsqrt(jnp.mean(xv * xv, axis=-1, keepdims=True) + EPS)
    xhat = xv * rstd
    gd = dn * g_ref[...]
    dx = rstd * (gd - xhat * jnp.mean(gd * xhat, axis=-1, keepdims=True))
    dh = r_ref[...] + dx
    dh_ref[...] = dh
    dhb_ref[...] = dh.astype(BF16)
    _accumulate(dg_ref, jnp.sum(dn * xhat, axis=0, keepdims=True), ij[0] == 0)


def _rms_bwd_io(x, gain, dres, tm):
    t, d = x.shape
    row = pl.BlockSpec((tm, d), lambda i, j, k: (i, 0))
    extras = [(x, row), (gain, pl.BlockSpec((1, d), lambda i, j, k: (0, 0))), (dres, row)]
    outs = [(_sds((t, d), F32), row), (_sds((t, d), BF16), row),
            (_sds((1, d), F32), pl.BlockSpec((1, d), lambda i, j, k: (0, 0)))]
    return extras, outs


def _residual_norm_epilogue(scale):
    def epilogue(accs, extras, outs, ij):
        hv = extras[0][...] + scale * accs[0]
        outs[0][...] = hv
        rstd = lax.rsqrt(jnp.mean(hv * hv, axis=-1, keepdims=True) + EPS)
        outs[1][...] = (hv * rstd * extras[1][...]).astype(BF16)
    return epilogue


def _ffn_fwd(tag, h, n, wgu, wd, next_gain, hooks):
    t, d = h.shape
    fs = wgu.shape[2]
    f = 2 * fs
    tm = _pick(t, ROW_TILE, 8)

    def up_epilogue(accs, extras, outs, ij):
        g, u = accs
        gu_ref, a_ref = outs
        gu_ref[0] = g.astype(BF16)
        gu_ref[1] = u.astype(BF16)
        a_ref[...] = (g * jax.nn.sigmoid(g) * u).astype(BF16)

    a_spec = pl.BlockSpec((tm, d), lambda i, j, k: (i, 0))
    gu, a = _mm(
        tag + "_up", "nn", (t // tm, 2, 1),
        [(n, a_spec, wgu, pl.BlockSpec((None, d, fs), lambda i, j, k: (j, 0, 0))),
         (n, a_spec, wgu, pl.BlockSpec((None, d, fs), lambda i, j, k: (j + 2, 0, 0)))],
        [],
        [(_sds((2, t, f), BF16), pl.BlockSpec((2, tm, fs), lambda i, j, k: (0, i, j))),
         (_sds((t, f), BF16), pl.BlockSpec((tm, fs), lambda i, j, k: (i, j)))],
        (tm, fs), up_epilogue, comms=hooks.get("up", lambda: ())(), j_outer=True, chunked=True)

    row = pl.BlockSpec((tm, d), lambda i, j, k: (i, 0))
    h_new, n_new = _mm(
        tag + "_down", "nn", (t // tm, 1, 1),
        [(a, pl.BlockSpec((tm, f), lambda i, j, k: (i, 0)), wd, pl.BlockSpec((f, d), lambda i, j, k: (0, 0)))],
        [(h, row), (next_gain, pl.BlockSpec((1, d), lambda i, j, k: (0, 0)))],
        [(_sds((t, d), F32), row), (_sds((t, d), BF16), row)], (tm, d), _residual_norm_epilogue(0.5),
        comms=hooks.get("down", lambda: ())())
    return h_new, n_new, (n, gu, a)


def _ffn_bwd(tag, dh, dh_b, h, gain, wgu, wd, saved, hooks):
    n, gu, a = saved
    t, d = h.shape
    fs = wgu.shape[2]
    f = 2 * fs
    tm = _pick(t, ROW_TILE, 8)
    tk = _pick(t, ROW_TILE, 8)

    def dact_epilogue(accs, extras, outs, ij):
        da = 0.5 * accs[0]
        g = extras[0][0].astype(F32)
        u = extras[0][1].astype(F32)
        sg = jax.nn.sigmoid(g)
        outs[0][0] = (da * u * sg * (1.0 + g * (1.0 - sg))).astype(BF16)
        outs[0][1] = (da * g * sg).astype(BF16)

    gu_spec = pl.BlockSpec((2, tm, fs), lambda i, j, k: (0, i, j))
    (dgu,) = _mm(
        tag + "_dact", "nt", (t // tm, 2, 1),
        [(dh_b, pl.BlockSpec((tm, d), lambda i, j, k: (i, 0)), wd, pl.BlockSpec((fs, d), lambda i, j, k: (j, 0)))],
        [(gu, gu_spec)], [(_sds((2, t, f), BF16), gu_spec)], (tm, fs), dact_epilogue, j_outer=True, chunked=True,
        comms=hooks.get("dact", lambda: ())())

    def cast_epilogue(accs, extras, outs, ij):
        outs[0][...] = accs[0].astype(BF16)

    (dwgu,) = _mm(
        tag + "_dwgu", "tn", (1, 4, t // tk),
        [(n, pl.BlockSpec((tk, d), lambda i, j, k: (k, 0)),
          dgu, pl.BlockSpec((None, tk, fs), lambda i, j, k: (j // 2, k, j % 2)))],
        [], [(_sds((4, d, fs), BF16), pl.BlockSpec((None, d, fs), lambda i, j, k: (j, 0, 0)))], (d, fs), cast_epilogue,
        comms=hooks.get("dwgu", lambda: ())())

    def half_epilogue(accs, extras, outs, ij):
        outs[0][...] = (0.5 * accs[0]).astype(BF16)

    (dwd,) = _mm(
        tag + "_dwd", "tn", (2, 1, t // tk),
        [(a, pl.BlockSpec((tk, fs), lambda i, j, k: (k, i)), dh_b, pl.BlockSpec((tk, d), lambda i, j, k: (k, 0)))],
        [], [(_sds((f, d), BF16), pl.BlockSpec((fs, d), lambda i, j, k: (i, 0)))], (fs, d), half_epilogue,
        comms=hooks.get("dwd", lambda g: ())(dwgu))

    tmn = _pick(t, ROW_TILE, 8)
    extras, outs = _rms_bwd_io(h, gain, dh, tmn)
    dh_in, dh_in_b, dgain = _mm(
        tag + "_dnorm", "nt", (t // tmn, 1, 4),
        [(dgu, pl.BlockSpec((None, tmn, fs), lambda i, j, k: (k // 2, i, k % 2)),
          wgu, pl.BlockSpec((None, d, fs), lambda i, j, k: (k, 0, 0)))],
        extras, outs, (tmn, d), _rms_bwd_epilogue, comms=hooks.get("dnorm", lambda g, w: ())(dwgu, dwd))
    return dh_in, dh_in_b, dgain, dwgu, dwd


def _lane_lo(shape):
    return lax.broadcasted_iota(jnp.int32, shape, 1) < HEAD_DIM


def _pair_norm(xv, gain):
    lo = _lane_lo(xv.shape)
    x2 = xv * xv
    ms_lo = jnp.sum(jnp.where(lo, x2, 0.0), axis=-1, keepdims=True) * (1.0 / HEAD_DIM)
    ms_hi = jnp.sum(jnp.where(lo, 0.0, x2), axis=-1, keepdims=True) * (1.0 / HEAD_DIM)
    rstd = jnp.where(lo, lax.rsqrt(ms_lo + EPS), lax.rsqrt(ms_hi + EPS))
    xhat = xv * rstd
    return xhat * gain, xhat, rstd


def _pair_norm_bwd(dn, xhat, rstd, gain):
    lo = _lane_lo(dn.shape)
    gd = dn * gain
    t = gd * xhat
    m_lo = jnp.sum(jnp.where(lo, t, 0.0), axis=-1, keepdims=True) * (1.0 / HEAD_DIM)
    m_hi = jnp.sum(jnp.where(lo, 0.0, t), axis=-1, keepdims=True) * (1.0 / HEAD_DIM)
    dx = rstd * (gd - xhat * jnp.where(lo, m_lo, m_hi))
    return dx, jnp.sum(dn * xhat, axis=0, keepdims=True)


def _half(xv, hi):
    lo = _lane_lo(xv.shape)
    return jnp.where(lo, 0, xv) if hi else jnp.where(lo, xv, 0)


def _attn_window(i, prev):
    q0 = i * Q_BLOCK
    start = jnp.maximum(q0 - prev, 0)
    off = start - (q0 - prev)
    return pl.multiple_of(start, Q_BLOCK), pl.multiple_of(off, Q_BLOCK)


Q_BLOCKS_PER_STEP = 4
STEP_ROWS = Q_BLOCKS_PER_STEP * Q_BLOCK


def _attn_specs(cfg, s, steps):
    kw = cfg["kw"]
    q_spec = pl.BlockSpec((STEP_ROWS, A_WIDTH), lambda b, i: (b * steps + i, cfg["qblk"]))
    k_spec = pl.BlockSpec((s, kw), lambda b, i: (b, cfg["kblk"]))
    v_spec = pl.BlockSpec((s, kw), lambda b, i: (b, cfg["vblk"]))
    return q_spec, k_spec, v_spec


def _const_spec(shape):
    return pl.BlockSpec(shape, lambda b, i: (0,) * len(shape))


KEY_CHUNK = 128


def _pair_bias(bias_t):
    wext = bias_t.shape[1]
    return jnp.transpose(bias_t.reshape(N_HEADS // 2, 2, wext, Q_BLOCK), (0, 2, 1, 3)).reshape(
        N_HEADS // 2, wext, 2 * Q_BLOCK)


def _unpair_bias(db2):
    wext = db2.shape[1]
    return jnp.transpose(db2.reshape(N_HEADS // 2, wext, 2, Q_BLOCK), (0, 2, 1, 3)).reshape(N_HEADS, wext, Q_BLOCK)


def _pair_rows(rows):
    two = rows.reshape(N_HEADS // 2, 2 * rows.shape[1])
    return jnp.broadcast_to(two[:, None, :], (N_HEADS // 2, 8, two.shape[1]))


def _sub_lo(shape):
    return lax.broadcasted_iota(jnp.int32, shape, 0) < HEAD_DIM


def _by_half(lo_row, hi_row, rows):
    return jnp.where(_sub_lo((rows, lo_row.shape[1])), lo_row, hi_row)


def _stack_pair(xn, jq, group):
    parts = []
    for hq in range(2):
        hk = ((2 * jq + hq) // group) % 2
        xm = _half(xn, hq)
        if hq != hk:
            xm = pltpu.roll(xm, HEAD_DIM, 1)
        parts.append(xm)
    return jnp.concatenate(parts, axis=0).astype(BF16)


def _place_transposed(blk, dst_ref, c, heads, group):
    bt = blk.T
    lo = _sub_lo(bt.shape)
    for h in heads:
        src_hi = ((h // group) % 2) == 1
        part = jnp.where(lo, 0.0, bt) if src_hi else jnp.where(lo, bt, 0.0)
        if src_hi != (h % 2 == 1):
            part = pltpu.roll(part, HEAD_DIM, 0)
        dst_ref[h, c] = part.astype(BF16)


def _attn_fwd(name, qkv, bias2, sink2, gq, gk, cfg, n_batch, comms=()):
    t = qkv.shape[0]
    s = t // n_batch
    steps = s // STEP_ROWS
    nkc = s // KEY_CHUNK
    prev, group, kw = cfg["prev"], cfg["group"], cfg["kw"]
    w = prev + Q_BLOCK
    n_chunks = w // KEY_CHUNK
    wext = bias2.shape[1]
    plumb = _CommPlumbing(comms, 7, 2, 4)
    n_all_in = 7 + len(plumb.args)
    n_all_out = 2 + len(plumb.out_shape)

    def body(*refs):
        q_ref, k_ref, v_ref, bias_ref, sink_ref, gq_ref, gk_ref = refs[:7]
        y_ref, lse_ref = refs[n_all_in:n_all_in + 2]
        kn_ref, vt_ref, s_ref, pst_ref = refs[n_all_in + n_all_out:n_all_in + n_all_out + 4]
        step = pl.program_id(1)
        comm_refs = (refs[:n_all_in], refs[n_all_in:n_all_in + n_all_out], refs[n_all_in + n_all_out:])
        plumb.start_at(*comm_refs, (pl.program_id(0) == 0) & (step == 0))

        @pl.when(step == 0)
        def _():
            for jk in range(kw // PAIR):
                cols = pl.ds(jk * PAIR, PAIR)
                heads = [h for h in range(N_HEADS) if (h // group) // 2 == jk]
                kn, _, _ = _pair_norm(k_ref[:, cols].astype(F32), gk_ref[...])
                kn_ref[:, cols] = kn.astype(BF16)
                for c in range(nkc):
                    _place_transposed(v_ref[pl.ds(c * KEY_CHUNK, KEY_CHUNK), cols].astype(F32), vt_ref, c, heads, group)

        sub8 = lax.broadcasted_iota(jnp.int32, (N_HEADS, Q_BLOCK), 0)
        for sb in range(Q_BLOCKS_PER_STEP):
            qrows = pl.ds(sb * Q_BLOCK, Q_BLOCK)
            start, off = _attn_window(step * Q_BLOCKS_PER_STEP + sb, prev)
            c0 = start // KEY_CHUNK
            lse = jnp.zeros((N_HEADS, Q_BLOCK), F32)
            for jq in range(N_HEADS // 2):
                kcols = pl.ds((((2 * jq) // group) // 2) * PAIR, PAIR)
                qn, _, _ = _pair_norm(q_ref[qrows, pl.ds(jq * PAIR, PAIR)].astype(F32), gq_ref[...])
                qs = _stack_pair(qn * ATTN_SCALE, jq, group)
                s_ref[...] = _dot(kn_ref[pl.ds(start, w), kcols], qs, "nt")
                m = sink_ref[jq, 0:1, :]
                for c in range(n_chunks):
                    r = pl.ds(c * KEY_CHUNK, KEY_CHUNK)
                    s2 = s_ref[r, :] + bias_ref[jq, pl.ds(off + c * KEY_CHUNK, KEY_CHUNK), :]
                    s_ref[r, :] = s2
                    m = jnp.maximum(m, jnp.max(s2, axis=0, keepdims=True))
                l = jnp.exp(sink_ref[jq, 0:1, :] - m)
                for c in range(n_chunks):
                    p = jnp.exp(s_ref[pl.ds(c * KEY_CHUNK, KEY_CHUNK), :] - m)
                    l = l + jnp.sum(p, axis=0, keepdims=True)
                    pst_ref[pl.ds(2 * c * KEY_CHUNK, KEY_CHUNK), :] = p[:, :Q_BLOCK].astype(BF16)
                    pst_ref[pl.ds((2 * c + 1) * KEY_CHUNK, KEY_CHUNK), :] = p[:, Q_BLOCK:].astype(BF16)
                vl = jnp.concatenate([vt_ref[2 * jq + hq, c0 + c] for c in range(n_chunks) for hq in range(2)], axis=1)
                ot = _dot(vl, pst_ref[...], "nn")
                inv = 1.0 / l
                ot = ot * _by_half(inv[:, :Q_BLOCK], inv[:, Q_BLOCK:], PAIR)
                y_ref[qrows, pl.ds(jq * PAIR, PAIR)] = ot.T.astype(BF16)
                lse2 = m + jnp.log(l)
                lse = jnp.where(sub8 == 2 * jq, lse2[:, :Q_BLOCK], lse)
                lse = jnp.where(sub8 == 2 * jq + 1, lse2[:, Q_BLOCK:], lse)
            lse_ref[sb] = lse
        plumb.finish_at(*comm_refs, (pl.program_id(0) == n_batch - 1) & (step == steps - 1))

    q_spec, k_spec, v_spec = _attn_specs(cfg, s, steps)
    res = pl.pallas_call(
        body, name=name, grid=(n_batch, steps),
        in_specs=[q_spec, k_spec, v_spec, _const_spec((N_HEADS // 2, wext, 2 * Q_BLOCK)),
                  _const_spec((N_HEADS // 2, 8, 2 * Q_BLOCK)), _const_spec((1, PAIR)), _const_spec((1, PAIR))]
        + [ANY] * len(plumb.args),
        out_specs=[pl.BlockSpec((STEP_ROWS, A_WIDTH), lambda b, i: (b * steps + i, 0)),
                   pl.BlockSpec((Q_BLOCKS_PER_STEP, N_HEADS, Q_BLOCK), lambda b, i: (b * steps + i, 0, 0))]
        + [ANY] * len(plumb.out_shape),
        out_shape=[_sds((t, A_WIDTH), BF16), _sds((t // Q_BLOCK, N_HEADS, Q_BLOCK), F32)] + plumb.out_shape,
        scratch_shapes=[pltpu.VMEM((s, kw), BF16), pltpu.VMEM((N_HEADS, nkc, PAIR, KEY_CHUNK), BF16),
                        pltpu.VMEM((w, 2 * Q_BLOCK), F32), pltpu.VMEM((2 * w, Q_BLOCK), BF16)] + plumb.scratch,
        input_output_aliases=plumb.aliases,
        compiler_params=plumb.params(vmem_limit_bytes=VMEM_LIMIT_V7X),
    )(qkv, qkv, qkv, bias2, sink2, gq, gk, *plumb.args)
    return plumb.deliver(res)


def _attn_bwd(name, qkv, bias2, sink2, gq, gk, y, dy, lse, cfg, n_batch, want_dbias, comms=()):
    t = qkv.shape[0]
    s = t // n_batch
    steps = s // STEP_ROWS
    nkc = s // KEY_CHUNK
    prev, group, kw = cfg["prev"], cfg["group"], cfg["kw"]
    w = prev + Q_BLOCK
    n_chunks = w // KEY_CHUNK
    wext = bias2.shape[1]
    plumb = _CommPlumbing(comms, 10, 7, 9)
    n_all_in = 10 + len(plumb.args)
    n_all_out = 7 + len(plumb.out_shape)

    def body(*refs):
        q_ref, k_ref, v_ref, bias_ref, sink_ref, gq_ref, gk_ref, y_ref, dy_ref, lse_ref = refs[:10]
        dq_ref, dk_ref, dv_ref, db_ref, dsink_ref, dgq_ref, dgk_ref = refs[n_all_in:n_all_in + 7]
        kn_ref, knt_ref, dkn_ref, dvs_ref, s_ref, dp_ref, pb_ref, dsb_ref, dst_ref = \
            refs[n_all_in + n_all_out:n_all_in + n_all_out + 9]
        b = pl.program_id(0)
        step = pl.program_id(1)
        first = (b == 0) & (step == 0)
        comm_refs = (refs[:n_all_in], refs[n_all_in:n_all_in + n_all_out], refs[n_all_in + n_all_out:])
        plumb.start_at(*comm_refs, first)

        @pl.when(step == 0)
        def _():
            for jk in range(kw // PAIR):
                cols = pl.ds(jk * PAIR, PAIR)
                heads = [h for h in range(N_HEADS) if (h // group) // 2 == jk]
                for c in range(nkc):
                    rows = pl.ds(c * KEY_CHUNK, KEY_CHUNK)
                    kn, _, _ = _pair_norm(k_ref[rows, cols].astype(F32), gk_ref[...])
                    kn_ref[rows, cols] = kn.astype(BF16)
                    _place_transposed(kn, knt_ref, c, heads, group)
            dkn_ref[...] = jnp.zeros(dkn_ref.shape, F32)
            dvs_ref[...] = jnp.zeros(dvs_ref.shape, F32)

        @pl.when(first)
        def _():
            db_ref[...] = jnp.zeros(db_ref.shape, F32)
            dsink_ref[...] = jnp.zeros(dsink_ref.shape, F32)
            dgq_ref[...] = jnp.zeros(dgq_ref.shape, F32)
            dgk_ref[...] = jnp.zeros(dgk_ref.shape, F32)

        for sb in range(Q_BLOCKS_PER_STEP):
            qrows = pl.ds(sb * Q_BLOCK, Q_BLOCK)
            start, off = _attn_window(step * Q_BLOCKS_PER_STEP + sb, prev)
            c0 = start // KEY_CHUNK
            for jq in range(N_HEADS // 2):
                cols = pl.ds(jq * PAIR, PAIR)
                kcols = pl.ds((((2 * jq) // group) // 2) * PAIR, PAIR)
                qn, q_hat, q_rstd = _pair_norm(q_ref[qrows, cols].astype(F32), gq_ref[...])
                qs = _stack_pair(qn * ATTN_SCALE, jq, group)
                do_pair = dy_ref[qrows, cols].astype(F32)
                dos = _stack_pair(do_pair, jq, group)
                prod_t = (do_pair * y_ref[qrows, cols].astype(F32)).T
                lo = _sub_lo(prod_t.shape)
                delta2 = jnp.concatenate([jnp.sum(jnp.where(lo, prod_t, 0.0), axis=0, keepdims=True),
                                          jnp.sum(jnp.where(lo, 0.0, prod_t), axis=0, keepdims=True)], axis=1)
                lse2 = jnp.concatenate([lse_ref[sb, 2 * jq:2 * jq + 1, :], lse_ref[sb, 2 * jq + 1:2 * jq + 2, :]],
                                       axis=1)
                dsk = -jnp.exp(sink_ref[jq, 0:1, :] - lse2) * delta2
                dsink_ref[2 * jq:2 * jq + 1, :] += dsk[:, :Q_BLOCK]
                dsink_ref[2 * jq + 1:2 * jq + 2, :] += dsk[:, Q_BLOCK:]
                rows_w = pl.ds(start, w)
                s_ref[...] = _dot(kn_ref[rows_w, kcols], qs, "nt")
                dp_ref[...] = _dot(v_ref[rows_w, kcols], dos, "nt")
                for c in range(n_chunks):
                    r = pl.ds(c * KEY_CHUNK, KEY_CHUNK)
                    brows = pl.ds(off + c * KEY_CHUNK, KEY_CHUNK)
                    p = jnp.exp(s_ref[r, :] + bias_ref[jq, brows, :] - lse2)
                    ds = p * (dp_ref[r, :] - delta2)
                    if want_dbias:
                        db_ref[jq, brows, :] += ds
                    ds_b = ds.astype(BF16)
                    pb_ref[r, :] = p.astype(BF16)
                    dsb_ref[r, :] = ds_b
                    dst_ref[pl.ds(2 * c * KEY_CHUNK, KEY_CHUNK), :] = ds_b[:, :Q_BLOCK]
                    dst_ref[pl.ds((2 * c + 1) * KEY_CHUNK, KEY_CHUNK), :] = ds_b[:, Q_BLOCK:]
                dkn_ref[rows_w, kcols] += _dot(dsb_ref[...], qs, "nn")
                dvs_ref[rows_w, kcols] += _dot(pb_ref[...], dos, "nn")
                kl = jnp.concatenate([knt_ref[2 * jq + hq, c0 + c] for c in range(n_chunks) for hq in range(2)],
                                     axis=1)
                dqt = _dot(kl, dst_ref[...], "nn")
                dq_raw, dg = _pair_norm_bwd(dqt.T * ATTN_SCALE, q_hat, q_rstd, gq_ref[...])
                dq_ref[qrows, cols] = dq_raw.astype(BF16)
                dgq_ref[...] += dg

        @pl.when(step == steps - 1)
        def _():
            for jk in range(kw // PAIR):
                kcols = pl.ds(jk * PAIR, PAIR)
                _, k_hat, k_rstd = _pair_norm(k_ref[:, kcols].astype(F32), gk_ref[...])
                dk_raw, dg = _pair_norm_bwd(dkn_ref[:, kcols], k_hat, k_rstd, gk_ref[...])
                dk_ref[:, kcols] = dk_raw.astype(BF16)
                dgk_ref[...] += dg
            dv_ref[...] = dvs_ref[...].astype(BF16)

        plumb.finish_at(*comm_refs, (b == n_batch - 1) & (step == steps - 1))

    q_spec, k_spec, v_spec = _attn_specs(cfg, s, steps)
    row = pl.BlockSpec((STEP_ROWS, A_WIDTH), lambda b, i: (b * steps + i, 0))
    kv_out = pl.BlockSpec((s, kw), lambda b, i: (b, 0))
    pair_bias = _const_spec((N_HEADS // 2, wext, 2 * Q_BLOCK))
    res = pl.pallas_call(
        body, name=name, grid=(n_batch, steps),
        in_specs=[q_spec, k_spec, v_spec, pair_bias, _const_spec((N_HEADS // 2, 8, 2 * Q_BLOCK)),
                  _const_spec((1, PAIR)), _const_spec((1, PAIR)), row, row,
                  pl.BlockSpec((Q_BLOCKS_PER_STEP, N_HEADS, Q_BLOCK), lambda b, i: (b * steps + i, 0, 0))]
        + [ANY] * len(plumb.args),
        out_specs=[row, kv_out, kv_out, pair_bias, _const_spec((N_HEADS, 128)),
                   _const_spec((1, PAIR)), _const_spec((1, PAIR))] + [ANY] * len(plumb.out_shape),
        out_shape=[_sds((t, A_WIDTH), BF16), _sds((t, kw), BF16), _sds((t, kw), BF16),
                   _sds((N_HEADS // 2, wext, 2 * Q_BLOCK), F32), _sds((N_HEADS, 128), F32),
                   _sds((1, PAIR), F32), _sds((1, PAIR), F32)] + plumb.out_shape,
        scratch_shapes=[pltpu.VMEM((s, kw), BF16), pltpu.VMEM((N_HEADS, nkc, PAIR, KEY_CHUNK), BF16),
                        pltpu.VMEM((s, kw), F32), pltpu.VMEM((s, kw), F32),
                        pltpu.VMEM((w, 2 * Q_BLOCK), F32), pltpu.VMEM((w, 2 * Q_BLOCK), F32),
                        pltpu.VMEM((w, 2 * Q_BLOCK), BF16), pltpu.VMEM((w, 2 * Q_BLOCK), BF16),
                        pltpu.VMEM((2 * w, Q_BLOCK), BF16)] + plumb.scratch,
        input_output_aliases=plumb.aliases,
        compiler_params=plumb.params(vmem_limit_bytes=VMEM_LIMIT_V7X),
    )(qkv, qkv, qkv, bias2, sink2, gq, gk, y, dy, lse, *plumb.args)
    return plumb.deliver(res)


def _band_tables(prev_chunks):
    prev = prev_chunks * CHUNK
    wext = 2 * prev + Q_BLOCK
    jj = np.arange(wext)[:, None]
    ii = np.arange(Q_BLOCK)[None, :]
    dist = prev + ii - jj
    rel_chunk = (prev // CHUNK + ii // CHUNK) - jj // CHUNK
    allowed = (rel_chunk >= 0) & (rel_chunk <= prev_chunks)
    return dist, allowed


def _alibi_slopes():
    return np.array([2.0 ** (-8.0 * (h + 1) / N_HEADS) for h in range(N_HEADS)], dtype=np.float32)


def _diag_onehot(prev, wext):
    n_diag = wext + Q_BLOCK - 1
    idx = np.clip(prev + Q_BLOCK - 1 - np.arange(n_diag), -A_MAX_REL, A_MAX_REL) + A_MAX_REL
    onehot = np.zeros((n_diag, 2 * A_MAX_REL + 1), np.float32)
    onehot[np.arange(n_diag), idx] = 1.0
    return onehot


def _bias_a(rel_bias):
    prev = A_PREV_CHUNKS * CHUNK
    _, allowed = _band_tables(A_PREV_CHUNKS)
    wext = allowed.shape[0]
    n_diag = wext + Q_BLOCK - 1
    seq = jnp.dot(rel_bias, jnp.asarray(_diag_onehot(prev, wext).T), precision=lax.Precision.HIGHEST)
    seq = jnp.pad(seq, ((0, 0), (0, 1)))
    rows = jnp.broadcast_to(seq[:, None, :], (N_HEADS, Q_BLOCK, n_diag + 1)).reshape(N_HEADS, -1)
    skew = rows[:, :Q_BLOCK * n_diag].reshape(N_HEADS, Q_BLOCK, n_diag)
    tile = jnp.transpose(skew[:, :, Q_BLOCK - 1:Q_BLOCK - 1 + wext], (0, 2, 1))
    return jnp.where(jnp.asarray(allowed)[None], tile, NEG_INF)


def _bias_b():
    dist, allowed = _band_tables(B_PREV_CHUNKS)
    bias = -_alibi_slopes()[:, None, None] * np.abs(dist).astype(np.float32)[None]
    return jnp.asarray(np.where(allowed[None], bias, np.float32(NEG_INF)).astype(np.float32))


def _rel_bias_grad(db_t):
    prev = A_PREV_CHUNKS * CHUNK
    wext = db_t.shape[1]
    n_diag = wext + Q_BLOCK - 1
    wp = n_diag + Q_BLOCK - 1
    xp = jnp.pad(jnp.transpose(db_t, (0, 2, 1)), ((0, 0), (0, 0), (Q_BLOCK - 1, Q_BLOCK - 1)))
    flat = jnp.pad(xp.reshape(N_HEADS, Q_BLOCK * wp), ((0, 0), (0, Q_BLOCK)))
    skew = flat.reshape(N_HEADS, Q_BLOCK, wp + 1)[:, :, :n_diag]
    diag = jnp.sum(skew, axis=1)
    return jnp.dot(diag, jnp.asarray(_diag_onehot(prev, wext)), precision=lax.Precision.HIGHEST)


def _ew(name, fn, ins, out_dtypes):
    r, c = ins[0].shape
    rb = _pick(r, max(16, (1 << 19) // c), 16)
    spec = pl.BlockSpec((rb, c), lambda i: (i, 0))

    def body(*refs):
        vals = fn(*[ref[...] for ref in refs[:len(ins)]])
        for ref, val in zip(refs[len(ins):], vals):
            ref[...] = val.astype(ref.dtype)

    return pl.pallas_call(
        body, name=name, grid=(r // rb,), in_specs=[spec] * len(ins), out_specs=[spec] * len(out_dtypes),
        out_shape=[_sds((r, c), dt) for dt in out_dtypes], compiler_params=_params(),
    )(*ins)


def _cast_into_slot(name, w, chip):
    r, c = w.shape
    rb = _pick(r, max(16, (1 << 19) // c), 16)

    def body(chip_ref, w_ref, o_ref):
        o_ref[...] = w_ref[...].astype(BF16)

    return pl.pallas_call(
        body, name=name,
        grid_spec=pltpu.PrefetchScalarGridSpec(
            num_scalar_prefetch=1, grid=(r // rb,),
            in_specs=[pl.BlockSpec((rb, c), lambda i, chip_ref: (i, 0))],
            out_specs=pl.BlockSpec((None, rb, c), lambda i, chip_ref: (chip_ref[0], i, 0))),
        out_shape=_sds((N_CHIPS, r, c), BF16), compiler_params=_params(),
    )(chip, w)


def _adamw_math(w, g, m, v):
    m = ADAM_B1 * m + (1.0 - ADAM_B1) * g
    v = ADAM_B2 * v + (1.0 - ADAM_B2) * (g * g)
    m_hat = m / (1.0 - ADAM_B1 ** ADAM_STEP)
    v_hat = v / (1.0 - ADAM_B2 ** ADAM_STEP)
    delta = -ADAM_LR * (m_hat / (jnp.sqrt(v_hat) + ADAM_EPS) + ADAM_WD * w)
    return delta, m, v


def _adamw_terms(name, terms, w, m, v):
    r, c = w.shape
    hr = r // 2
    rb = _pick(hr, max(16, (1 << 19) // c), 16)
    nb = hr // rb

    def body(t_ref, w_ref, m_ref, v_ref, g_ref, d_ref, nm_ref, nv_ref):
        g = t_ref[0].astype(F32)
        for k in range(1, N_CHIPS):
            g = g + t_ref[k].astype(F32)
        delta, nm, nv = _adamw_math(w_ref[...], g, m_ref[...], v_ref[...])
        g_ref[...] = g
        d_ref[...] = delta
        nm_ref[...] = nm
        nv_ref[...] = nv

    spec = pl.BlockSpec((rb, c), lambda h, i: (h * nb + i, 0))
    return pl.pallas_call(
        body, name=name, grid=(2, nb),
        in_specs=[pl.BlockSpec((None, N_CHIPS, rb, c), lambda h, i: (h, 0, i, 0)), spec, spec, spec],
        out_specs=[spec] * 4, out_shape=[_sds((r, c), F32)] * 4, compiler_params=_params(),
    )(terms, w, m, v)


def _mesh_place():
    x, y, c = lax.axis_index("x"), lax.axis_index("y"), lax.axis_index("c")
    chips = [(x, 1 - y), (1 - x, y), (1 - x, 1 - y)]
    return x, y, c, chips


def _run_comms(name, comms):
    plumb = _CommPlumbing(comms, 0, 0, 0)
    n_in, n_out = len(plumb.args), len(plumb.out_shape)

    def body(*refs):
        parts = []
        i0, o0, s0 = 0, n_in, n_in + n_out
        for cm in plumb.comms:
            parts.append((refs[i0:i0 + len(cm.ins)], refs[o0:o0 + len(cm.outs)], refs[s0:s0 + len(cm.sems)]))
            i0 += len(cm.ins)
            o0 += len(cm.outs)
            s0 += len(cm.sems)
        plumb.handshake()
        for cm, part in zip(plumb.comms, parts):
            cm.start(*part)
        for cm, part in zip(plumb.comms, parts):
            cm.finish(*part)

    res = pl.pallas_call(
        body, name=name, in_specs=[ANY] * n_in, out_specs=[ANY] * n_out, out_shape=plumb.out_shape,
        scratch_shapes=plumb.scratch, input_output_aliases=plumb.aliases, compiler_params=plumb.params(),
    )(*plumb.args)
    plumb.deliver(res)


def _gather_ici_copies(outs, sems):
    send_sem, recv_sem = sems
    x, y, c, chips = _mesh_place()
    me = 2 * x + y
    sends, recvs = [], []
    for wi in range(len(outs)):
        for k, (tx, ty) in enumerate(chips):
            sems_k = dict(send_sem=send_sem.at[wi * 3 + k], recv_sem=recv_sem.at[wi * 3 + k],
                          device_id=(tx, ty, c), device_id_type=MESH)
            own = outs[wi].at[me, c]
            sends.append(pltpu.make_async_remote_copy(src_ref=own, dst_ref=own, **sems_k))
            slab = outs[wi].at[2 * tx + ty, c]
            recvs.append(pltpu.make_async_remote_copy(src_ref=slab, dst_ref=slab, **sems_k))
    return sends, recvs


def _gather_d2d_copies(outs, sems):
    send_sem, recv_sem = sems
    x, y, c, chips = _mesh_place()
    sends, recvs = [], []
    for wi in range(len(outs)):
        for k, (tx, ty) in enumerate(chips):
            sems_k = dict(send_sem=send_sem.at[wi * 3 + k], recv_sem=recv_sem.at[wi * 3 + k],
                          device_id=(x, y, 1 - c), device_id_type=MESH)
            mine = outs[wi].at[2 * tx + ty, c]
            theirs = outs[wi].at[2 * tx + ty, 1 - c]
            sends.append(pltpu.make_async_remote_copy(src_ref=mine, dst_ref=mine, **sems_k))
            recvs.append(pltpu.make_async_remote_copy(src_ref=theirs, dst_ref=theirs, **sems_k))
    return sends, recvs


def _gather_comm(peers, bufs, n_sems, start, finish):
    n = len(bufs)
    return _Comm(peers, bufs, [_sds(g.shape, g.dtype) for g in bufs], {i: i for i in range(n)},
                 [pltpu.SemaphoreType.DMA((3 * n,))] * n_sems, start, finish)


def _gather_ici(bufs):
    def start(ins, outs, sems):
        for cp in _gather_ici_copies(outs, sems)[0]:
            cp.start()

    def finish(ins, outs, sems):
        sends, recvs = _gather_ici_copies(outs, sems)
        for cp in recvs:
            cp.wait_recv()
        for cp in sends:
            cp.wait_send()

    return _gather_comm(("chips",), bufs, 2, start, finish)


def _gather_d2d(gathered):
    def start(ins, outs, sems):
        for cp in _gather_d2d_copies(outs, sems)[0]:
            cp.start()

    def finish(ins, outs, sems):
        sends, recvs = _gather_d2d_copies(outs, sems)
        for cp in recvs:
            cp.wait_recv()
        for cp in sends:
            cp.wait_send()

    return _gather_comm(("sibling",), gathered, 2, start, finish)


def _gather_both(bufs):
    def start(ins, outs, sems):
        for cp in _gather_ici_copies(outs, sems[:2])[0]:
            cp.start()

    def finish(ins, outs, sems):
        sends, recvs = _gather_ici_copies(outs, sems[:2])
        passes, lands = _gather_d2d_copies(outs, sems[2:])
        for arrived, onward in zip(recvs, passes):
            arrived.wait_recv()
            onward.start()
        for cp in lands:
            cp.wait_recv()
        for cp in sends + passes:
            cp.wait_send()

    return _gather_comm(("chips", "sibling"), bufs, 4, start, finish)


def _exchange_halves(grads):
    n = len(grads)

    def copies(ins, outs, sems):
        send_sem, recv_sem = sems
        x, y, c, _ = _mesh_place()
        return [pltpu.make_async_remote_copy(
            src_ref=ins[wi].at[t, 1 - c], dst_ref=outs[wi].at[t],
            send_sem=send_sem.at[wi * N_CHIPS + t], recv_sem=recv_sem.at[wi * N_CHIPS + t],
            device_id=(x, y, 1 - c), device_id_type=MESH) for wi in range(n) for t in range(N_CHIPS)]

    def start(ins, outs, sems):
        for cp in copies(ins, outs, sems):
            cp.start()

    def finish(ins, outs, sems):
        for cp in copies(ins, outs, sems):
            cp.wait()

    return _Comm(("sibling",), grads, [_sds((N_CHIPS,) + g.shape[2:], g.dtype) for g in grads], {},
                 [pltpu.SemaphoreType.DMA((N_CHIPS * n,)), pltpu.SemaphoreType.DMA((N_CHIPS * n,))], start, finish)


def _scatter_ici(sums):
    n = len(sums)

    def copies(ins, outs, sems):
        local_sem, send_sem, recv_sem = sems
        x, y, c, chips = _mesh_place()
        me = 2 * x + y
        local, sends, recvs = [], [], []
        for wi in range(n):
            local.append(pltpu.make_async_copy(ins[wi].at[me], outs[wi].at[c, 0], local_sem.at[wi]))
            for k, (tx, ty) in enumerate(chips):
                sems_k = dict(send_sem=send_sem.at[wi * 3 + k], recv_sem=recv_sem.at[wi * 3 + k],
                              device_id=(tx, ty, c), device_id_type=MESH)
                land = outs[wi].at[c, k + 1]
                sends.append(pltpu.make_async_remote_copy(src_ref=ins[wi].at[2 * tx + ty], dst_ref=land, **sems_k))
                recvs.append(pltpu.make_async_remote_copy(src_ref=land, dst_ref=land, **sems_k))
        return local, sends, recvs

    def start(ins, outs, sems):
        local, sends, _ = copies(ins, outs, sems)
        for cp in sends:
            cp.start()
        for cp in local:
            cp.start(priority=LOCAL_COPY_PRIORITY)

    def finish(ins, outs, sems):
        local, sends, recvs = copies(ins, outs, sems)
        for cp in local:
            cp.wait()
        for cp in recvs:
            cp.wait_recv()
        for cp in sends:
            cp.wait_send()

    return _Comm(("chips",), sums, [_sds((2, N_CHIPS) + s.shape[1:], s.dtype) for s in sums], {},
                 [pltpu.SemaphoreType.DMA((n,)), pltpu.SemaphoreType.DMA((3 * n,)), pltpu.SemaphoreType.DMA((3 * n,))],
                 start, finish)


def _scatter_d2d(terms):
    n = len(terms)

    def copies(outs, sems):
        send_sem, recv_sem = sems
        x, y, c, _ = _mesh_place()
        sends, recvs = [], []
        for wi in range(n):
            sems_w = dict(send_sem=send_sem.at[wi], recv_sem=recv_sem.at[wi],
                          device_id=(x, y, 1 - c), device_id_type=MESH)
            sends.append(pltpu.make_async_remote_copy(src_ref=outs[wi].at[c], dst_ref=outs[wi].at[c], **sems_w))
            recvs.append(pltpu.make_async_remote_copy(src_ref=outs[wi].at[1 - c], dst_ref=outs[wi].at[1 - c], **sems_w))
        return sends, recvs

    def start(ins, outs, sems):
        for cp in copies(outs, sems)[0]:
            cp.start()

    def finish(ins, outs, sems):
        sends, recvs = copies(outs, sems)
        for cp in recvs:
            cp.wait_recv()
        for cp in sends:
            cp.wait_send()

    return _Comm(("sibling",), terms, [_sds(t.shape, t.dtype) for t in terms], {i: i for i in range(n)},
                 [pltpu.SemaphoreType.DMA((n,)), pltpu.SemaphoreType.DMA((n,))], start, finish)


def _chip_sum(name, grad, got, core):
    _, _, hr, c = grad.shape
    rb = _pick(hr, max(16, (1 << 19) // c), 16)

    def body(core_ref, a_ref, b_ref, o_ref):
        o_ref[...] = (a_ref[...].astype(F32) + b_ref[...].astype(F32)).astype(BF16)

    out_spec = pl.BlockSpec((None, rb, c), lambda t, i, core_ref: (t, i, 0))
    return pl.pallas_call(
        body, name=name,
        grid_spec=pltpu.PrefetchScalarGridSpec(
            num_scalar_prefetch=1, grid=(N_CHIPS, hr // rb),
            in_specs=[pl.BlockSpec((None, None, rb, c), lambda t, i, core_ref: (t, core_ref[0], i, 0)), out_spec],
            out_specs=out_spec),
        out_shape=_sds((N_CHIPS, hr, c), BF16), compiler_params=_params(),
    )(core, grad, got)


def _all_reduce_small(pack):
    r = pack.shape[0]

    def body(p_ref, o_ref, land_ref, send_sem, recv_sem):
        x, y, c, _ = _mesh_place()
        me = 4 * x + 2 * y + c
        flips = [(k >> 2 & 1, k >> 1 & 1, k & 1) for k in range(1, N_DEV)]

        def peer(fx, fy, fc):
            return (1 - x if fx else x, 1 - y if fy else y, 1 - c if fc else c)

        land_ref[me] = p_ref[...]
        sent = []
        for k, flip in enumerate(flips):
            cp = pltpu.make_async_remote_copy(
                src_ref=p_ref, dst_ref=land_ref.at[me], send_sem=send_sem.at[k], recv_sem=recv_sem.at[k],
                device_id=peer(*flip), device_id_type=MESH)
            cp.start()
            sent.append(cp)
        for k, flip in enumerate(flips):
            px, py, pc = peer(*flip)
            slot = land_ref.at[4 * px + 2 * py + pc]
            pltpu.make_async_remote_copy(
                src_ref=slot, dst_ref=slot, send_sem=send_sem.at[k], recv_sem=recv_sem.at[k],
                device_id=(px, py, pc), device_id_type=MESH).wait_recv()
        total = land_ref[0]
        for d in range(1, N_DEV):
            total = total + land_ref[d]
        o_ref[...] = total
        for cp in sent:
            cp.wait_send()

    vmem = pl.BlockSpec(memory_space=pltpu.VMEM)
    return pl.pallas_call(
        body, name="all_reduce_small", in_specs=[vmem], out_specs=vmem, out_shape=_sds((r, 128), F32),
        scratch_shapes=[pltpu.VMEM((N_DEV, r, 128), F32), pltpu.SemaphoreType.DMA((N_DEV - 1,)),
                        pltpu.SemaphoreType.DMA((N_DEV - 1,))],
    )(pack)


PACK_TILE = 8 * 128


def _pack(items):
    rows, i = [], 0
    while i < len(items):
        j = i
        while j < len(items) and items[j].size == items[i].size:
            j += 1
        group = jnp.stack([it.reshape(-1).astype(F32) for it in items[i:j]])
        rows.append(jnp.pad(group, ((0, 0), (0, -group.shape[1] % PACK_TILE))).reshape(-1, 128))
        i = j
    return jnp.concatenate(rows, axis=0)


def _unpack(pack, shapes):
    out, row = [], 0
    for shp in shapes:
        size = int(np.prod(shp))
        nrow = -(-size // PACK_TILE) * (PACK_TILE // 128)
        out.append(pack[row:row + nrow].reshape(-1)[:size].reshape(shp))
        row += nrow
    return out


BIG = ["ffn1_w_gu", "ffn1_w_down", "w_in", "w_gate", "w_proj_a", "w_proj_b", "w_out",
       "ffn2_w_gu", "ffn2_w_down", "w_ple_gate", "w_ple_proj"]
SMALL = ["ffn1_norm", "mix_norm", "ffn2_norm", "ple_norm", "a_q_norm", "a_k_norm", "b_q_norm", "b_k_norm",
         "a_rel_bias", "b_sinks"]
WEIGHTS = ["ffn1_norm", "ffn1_w_gu", "ffn1_w_down", "mix_norm", "w_in", "a_q_norm", "a_k_norm", "a_rel_bias",
           "b_q_norm", "b_k_norm", "b_sinks", "w_gate", "w_proj_a", "w_proj_b", "w_out", "ffn2_norm",
           "ffn2_w_gu", "ffn2_w_down", "ple_norm", "w_ple_gate", "w_ple_proj"]
ATTN_A = dict(prev=A_PREV_CHUNKS * CHUNK, group=1, kw=A_WIDTH, qblk=0, kblk=1, vblk=2)
ATTN_B = dict(prev=B_PREV_CHUNKS * CHUNK, group=N_HEADS // B_KV_HEADS, kw=B_KV_WIDTH, qblk=3,
              kblk=4 * A_WIDTH // B_KV_WIDTH, vblk=4 * A_WIDTH // B_KV_WIDTH + 1)


def _cast_epilogue(accs, extras, outs, ij):
    for acc, out in zip(accs, outs):
        out[...] = acc.astype(out.dtype)


GATHER_FIRST = ["ffn1_w_gu", "ffn1_w_down"]
ROW_SHARDED = ("ffn1_w_down", "ffn2_w_down", "w_out", "w_ple_gate")


def _slotted(name, grad):
    if name == "w_in":
        rows, cols = grad.shape
        grad = jnp.transpose(grad.reshape(rows, N_CHIPS, cols // N_CHIPS), (1, 0, 2))
    elif name in ROW_SHARDED:
        grad = grad.reshape(N_CHIPS, grad.shape[0] // N_CHIPS, grad.shape[1])
    return grad.reshape(N_CHIPS, 2, grad.shape[1] // 2, grad.shape[2])


def _local_step(xt, pt, tgt, n_batch, bufs, small, core):
    t, d = xt.shape
    tm = _pick(t, ROW_TILE, 8)
    tk = _pick(t, ROW_TILE, 8)
    nt = t // tm
    row = pl.BlockSpec((tm, d), lambda i, j, k: (i, 0))
    gs = bufs["w_gate"].shape[2]
    ps = bufs["w_proj_a"].shape[2]
    es = bufs["w_ple_proj"].shape[2]
    pdim = pt.shape[1]
    ncols = N_CHIPS * bufs["w_in"].shape[2]
    tin = ncols // 2
    assert 2 * gs == d and 4 * ps == d and 4 * es == d and tin % 128 == 0

    w = {}
    halves = {n: b.reshape(N_CHIPS, 2, b.shape[1] // 2, b.shape[2]) for n, b in bufs.items()}

    def publish(names, arrays):
        for name, g in zip(names, arrays):
            g = g.reshape(N_CHIPS, 2 * g.shape[2], g.shape[3])
            if name in ROW_SHARDED:
                g = g.reshape(N_CHIPS * g.shape[1], g.shape[2])
            elif name == "w_in":
                g = jnp.transpose(g, (1, 0, 2)).reshape(g.shape[1], N_CHIPS * g.shape[2])
            w[name] = g

    class GatherPipe:
        def __init__(self, names):
            self.names = names
            self.stage = None

        def ici(self):
            self.stage = _gather_ici(self.bufs())
            return self.stage

        def d2d(self):
            self.stage = _gather_d2d(self.bufs())
            return self.stage

        def bufs(self):
            return self.stage.results if self.stage is not None else [halves[n] for n in self.names]

        def publish(self):
            publish(self.names, self.stage.results)

    class GradPipe:
        def __init__(self, names):
            self.names = names

        def exchange(self, grads):
            self.grads = [_slotted(n, g) for n, g in zip(self.names, grads)]
            self.x = _exchange_halves(self.grads)
            return self.x

        def scatter(self):
            self.sums = [_chip_sum("chip_sum_" + n, g, got, core)
                         for n, g, got in zip(self.names, self.grads, self.x.results)]
            self.s = _scatter_ici(self.sums)
            return self.s

        def forward(self):
            self.f = _scatter_d2d(self.s.results)
            return self.f

        def terms(self):
            return dict(zip(self.names, self.f.results))

    g_first = _gather_both([halves[n] for n in GATHER_FIRST])
    n1 = _rms_fwd("ffn1_norm", xt, small["ffn1_norm"], comms=[g_first])
    publish(GATHER_FIRST, g_first.results)
    g_in, g_proj, g_ple = GatherPipe(["w_in", "w_gate"]), GatherPipe(["w_proj_a", "w_proj_b", "w_out"]), \
        GatherPipe(["w_ple_gate", "w_ple_proj"])
    g_down2, g_up2 = GatherPipe(["ffn2_w_down"]), GatherPipe(["ffn2_w_gu"])
    h1, un, ffn1_saved = _ffn_fwd("ffn1", xt, n1, w["ffn1_w_gu"], w["ffn1_w_down"], small["mix_norm"],
                                  {"up": lambda: [g_in.ici()], "down": lambda: [g_in.d2d(), g_proj.ici()]})
    g_in.publish()
    w_in, wgate = w["w_in"], w["w_gate"]
    (qkv,) = _mm(
        "qkv", "nn", (nt, 2, 1),
        [(un, row, w_in, pl.BlockSpec((d, tin), lambda i, j, k: (0, j)))], [],
        [(_sds((t, ncols), BF16), pl.BlockSpec((tm, tin), lambda i, j, k: (i, j)))], (tm, tin), _cast_epilogue,
        j_outer=True, comms=[g_proj.d2d(), g_ple.ici()])
    g_proj.publish()
    wpa, wpb, wout = w["w_proj_a"], w["w_proj_b"], w["w_out"]

    def gate_epilogue(accs, extras, outs, ij):
        outs[0][...] = jax.nn.sigmoid(accs[0]).astype(BF16)

    (gates,) = _mm(
        "gate", "nn", (nt, 4, 1),
        [(un, row, wgate, pl.BlockSpec((None, d, gs), lambda i, j, k: (j, 0, 0)))], [],
        [(_sds((2, t, d), BF16), pl.BlockSpec((None, tm, gs), lambda i, j, k: (j // 2, i, j % 2)))],
        (tm, gs), gate_epilogue, j_outer=True, chunked=True, comms=[g_ple.d2d(), g_down2.ici()])
    g_ple.publish()
    wpg, wpe = w["w_ple_gate"], w["w_ple_proj"]

    bias_a = _pair_bias(_bias_a(small["a_rel_bias"][0]))
    bias_b = _pair_bias(_bias_b())
    sink_a = _pair_rows(jnp.full((N_HEADS, 128), NEG_INF, F32))
    sink_b = _pair_rows(jnp.broadcast_to(small["b_sinks"][0][:, None], (N_HEADS, 128)))
    gqa, gka, gqb, gkb = [jnp.tile(small[k], (1, 2)) for k in ("a_q_norm", "a_k_norm", "b_q_norm", "b_k_norm")]
    ya, lse_a = _attn_fwd("attn_a_fwd", qkv, bias_a, sink_a, gqa, gka, ATTN_A, n_batch,
                          comms=[g_down2.d2d(), g_up2.ici()])
    g_down2.publish()
    yb, lse_b = _attn_fwd("attn_b_fwd", qkv, bias_b, sink_b, gqb, gkb, ATTN_B, n_batch, comms=[g_up2.d2d()])
    g_up2.publish()

    def merge_epilogue(accs, extras, outs, ij):
        pa, pb = accs
        outs[0][...] = (extras[0][...].astype(F32) * pa + extras[1][...].astype(F32) * pb).astype(BF16)
        outs[1][...] = pa.astype(BF16)
        outs[2][...] = pb.astype(BF16)

    y_spec = pl.BlockSpec((tm, A_WIDTH), lambda i, j, k: (i, 0))
    proj_spec = pl.BlockSpec((None, A_WIDTH, ps), lambda i, j, k: (j, 0, 0))
    tile_ps = pl.BlockSpec((tm, ps), lambda i, j, k: (i, j))
    merged, pa, pb = _mm(
        "proj_merge", "nn", (nt, 4, 1),
        [(ya, y_spec, wpa, proj_spec), (yb, y_spec, wpb, proj_spec)],
        [(gates, pl.BlockSpec((None, tm, ps), lambda i, j, k: (0, i, j))),
         (gates, pl.BlockSpec((None, tm, ps), lambda i, j, k: (1, i, j)))],
        [(_sds((t, d), BF16), tile_ps)] * 3, (tm, ps), merge_epilogue)

    h2, n2 = _mm(
        "out_proj", "nn", (nt, 1, 1),
        [(merged, row, wout, pl.BlockSpec((d, d), lambda i, j, k: (0, 0)))],
        [(h1, row), (small["ffn2_norm"], pl.BlockSpec((1, d), lambda i, j, k: (0, 0)))],
        [(_sds((t, d), F32), row), (_sds((t, d), BF16), row)], (tm, d), _residual_norm_epilogue(1.0))

    h3, n3, ffn2_saved = _ffn_fwd("ffn2", h2, n2, w["ffn2_w_gu"], w["ffn2_w_down"], small["ple_norm"], {})
    tile_es = pl.BlockSpec((tm, es), lambda i, j, k: (i, j))
    th = _pick(d, 512)

    def head_epilogue(accs, extras, outs, ij):
        h3_ref, tgt_ref = extras
        dy_ref, dpe_ref, dz_ref, loss_ref = outs
        pg = jax.nn.sigmoid(accs[0])
        pev = accs[1]
        diff = h3_ref[...] + pg * pev - tgt_ref[...]
        dy = diff * (1.0 / d)
        dy_ref[...] = dy
        dpe_ref[...] = (dy * pg).astype(BF16)
        dz_ref[...] = (dy * pev * pg * (1.0 - pg)).astype(BF16)
        _accumulate(loss_ref, jnp.full(loss_ref.shape, jnp.sum(diff * diff), F32), (ij[0] == 0) & (ij[1] == 0))

    tile_h = pl.BlockSpec((tm, th), lambda i, j, k: (i, j))
    dy, dpe, dz, loss_acc = _mm(
        "ple_gate_loss", "nn", (nt, 4, 1),
        [(n3, row, wpg, pl.BlockSpec((d, es), lambda i, j, k: (0, j))),
         (pt, pl.BlockSpec((tm, pdim), lambda i, j, k: (i, 0)), wpe, pl.BlockSpec((None, pdim, es), lambda i, j, k: (j, 0, 0)))],
        [(h3, tile_es), (tgt, tile_es)],
        [(_sds((t, d), F32), tile_es), (_sds((t, d), BF16), tile_es), (_sds((t, d), BF16), tile_es),
         (_sds((8, 128), F32), pl.BlockSpec((8, 128), lambda i, j, k: (0, 0)))],
        (tm, es), head_epilogue, j_outer=True, chunked=True)
    loss = 0.5 * loss_acc[0, 0] / d

    nk = t // tk
    (dwpe,) = _mm(
        "d_w_ple_proj", "tn", (1, 4, nk),
        [(pt, pl.BlockSpec((tk, pdim), lambda i, j, k: (k, 0)), dpe, pl.BlockSpec((tk, es), lambda i, j, k: (k, j)))],
        [], [(_sds((4, pdim, es), BF16), pl.BlockSpec((None, pdim, es), lambda i, j, k: (j, 0, 0)))],
        (pdim, es), _cast_epilogue)

    def dense_grad(name, a, dyb, comms=()):
        (res,) = _mm(
            name, "tn", (1, d // th, nk),
            [(a, pl.BlockSpec((tk, d), lambda i, j, k: (k, 0)), dyb, pl.BlockSpec((tk, th), lambda i, j, k: (k, j)))],
            [], [(_sds((d, d), BF16), pl.BlockSpec((d, th), lambda i, j, k: (0, j)))], (d, th), _cast_epilogue,
            comms=comms)
        return res

    dwpg = dense_grad("d_w_ple_gate", n3, dz)
    tmn = _pick(t, ROW_TILE, 8)
    extras, outs = _rms_bwd_io(h3, small["ple_norm"], dy, tmn)
    dh3, dh3_b, d_ple_norm = _mm(
        "d_ple_norm", "nt", (t // tmn, 1, 1),
        [(dz, pl.BlockSpec((tmn, d), lambda i, j, k: (i, 0)), wpg, pl.BlockSpec((d, d), lambda i, j, k: (0, 0)))],
        extras, outs, (tmn, d), _rms_bwd_epilogue)

    up2, down2, ple = GradPipe(["ffn2_w_gu"]), GradPipe(["ffn2_w_down"]), GradPipe(["w_ple_gate", "w_ple_proj"])
    proj = GradPipe(["w_proj_a", "w_proj_b", "w_out"])
    dh2, dh2_b, d_ffn2_norm, dwgu2, dwd2 = _ffn_bwd(
        "ffn2", dh3, dh3_b, h2, small["ffn2_norm"], w["ffn2_w_gu"], w["ffn2_w_down"], ffn2_saved,
        {"dnorm": lambda dwgu, dwd: [up2.exchange([dwgu]), down2.exchange([dwd]), ple.exchange([dwpg, dwpe])]})

    def dmerge_epilogue(accs, extras, outs, ij):
        dmo = accs[0]
        g_ref, pa_ref, pb_ref = extras
        dg_ref, dpa_ref, dpb_ref = outs
        ga = g_ref[0].astype(F32)
        gb = g_ref[1].astype(F32)
        dg_ref[0] = (dmo * pa_ref[...].astype(F32) * ga * (1.0 - ga)).astype(BF16)
        dg_ref[1] = (dmo * pb_ref[...].astype(F32) * gb * (1.0 - gb)).astype(BF16)
        dpa_ref[...] = (dmo * ga).astype(BF16)
        dpb_ref[...] = (dmo * gb).astype(BF16)

    g_spec = pl.BlockSpec((2, tm, th), lambda i, j, k: (0, i, j))
    dgates, dpa, dpb = _mm(
        "d_merge", "nt", (nt, d // th, 1),
        [(dh2_b, row, wout, pl.BlockSpec((th, d), lambda i, j, k: (j, 0)))],
        [(gates, g_spec), (pa, tile_h), (pb, tile_h)],
        [(_sds((2, t, d), BF16), g_spec), (_sds((t, d), BF16), tile_h), (_sds((t, d), BF16), tile_h)],
        (tm, th), dmerge_epilogue, j_outer=True, chunked=True, comms=[down2.scatter()])
    dwout = dense_grad("d_w_out", merged, dh2_b, comms=[down2.forward(), ple.scatter()])

    yk_spec = pl.BlockSpec((tk, A_WIDTH), lambda i, j, k: (k, 0))
    dk_spec = pl.BlockSpec((tk, ps), lambda i, j, k: (k, j))
    dproj = (_sds((4, A_WIDTH, ps), BF16), proj_spec)
    dwpa, dwpb = _mm(
        "d_w_proj", "tn", (1, 4, nk),
        [(ya, yk_spec, dpa, dk_spec), (yb, yk_spec, dpb, dk_spec)], [], [dproj, dproj], (A_WIDTH, ps), _cast_epilogue,
        comms=[ple.forward()])
    dproj_a = pl.BlockSpec((tm, ps), lambda i, j, k: (i, k))
    wproj_k = pl.BlockSpec((None, A_WIDTH, ps), lambda i, j, k: (k, 0, 0))
    dya, dyb = _mm(
        "d_attn_out", "nt", (nt, 1, 4),
        [(dpa, dproj_a, wpa, wproj_k), (dpb, dproj_a, wpb, wproj_k)], [],
        [(_sds((t, A_WIDTH), BF16), y_spec)] * 2, (tm, A_WIDTH), _cast_epilogue,
        comms=[proj.exchange([dwpa, dwpb, dwout])])

    dqa, dka, dva, dbias_a, _, dgqa, dgka = _attn_bwd(
        "attn_a_bwd", qkv, bias_a, sink_a, gqa, gka, ya, dya, lse_a, ATTN_A, n_batch, True,
        comms=[up2.scatter(), proj.scatter()])
    dqb, dkb, dvb, _, dsink_b, dgqb, dgkb = _attn_bwd(
        "attn_b_bwd", qkv, bias_b, sink_b, gqb, gkb, yb, dyb, lse_b, ATTN_B, n_batch, False,
        comms=[up2.forward(), proj.forward()])
    dqkv = jnp.concatenate([dqa, dka, dva, dqb, dkb, dvb], axis=1)

    (dwgate,) = _mm(
        "d_w_gate", "tn", (1, 4, nk),
        [(un, pl.BlockSpec((tk, d), lambda i, j, k: (k, 0)),
          dgates, pl.BlockSpec((None, tk, gs), lambda i, j, k: (j // 2, k, j % 2)))],
        [], [(_sds((4, d, gs), BF16), pl.BlockSpec((None, d, gs), lambda i, j, k: (j, 0, 0)))], (d, gs), _cast_epilogue)
    (dwin,) = _mm(
        "d_w_in", "tn", (1, 2, nk),
        [(un, pl.BlockSpec((tk, d), lambda i, j, k: (k, 0)), dqkv, pl.BlockSpec((tk, tin), lambda i, j, k: (k, j)))],
        [], [(_sds((d, ncols), BF16), pl.BlockSpec((d, tin), lambda i, j, k: (0, j)))], (d, tin), _cast_epilogue)

    mixer = GradPipe(["w_in", "w_gate"])
    extras, outs = _rms_bwd_io(h1, small["mix_norm"], dh2, tmn)
    dh1, dh1_b, d_mix_norm = _mm(
        "d_mix_norm", "nt", (t // tmn, 1, 6),
        [(dgates, pl.BlockSpec((None, tmn, gs), lambda i, j, k: (jnp.minimum(k, 3) // 2, i, jnp.minimum(k, 3) % 2)),
          wgate, pl.BlockSpec((None, d, gs), lambda i, j, k: (jnp.minimum(k, 3), 0, 0))),
         (dqkv, pl.BlockSpec((tmn, tin), lambda i, j, k: (i, jnp.maximum(k - 4, 0))),
          w_in, pl.BlockSpec((d, tin), lambda i, j, k: (0, jnp.maximum(k - 4, 0))))],
        extras, outs, (tmn, d), _rms_bwd_epilogue, steps=[4, 2],
        comms=[mixer.exchange([dwin, dwgate])])

    up1 = GradPipe(["ffn1_w_gu"])
    down1 = GradPipe(["ffn1_w_down"])
    dx, _, d_ffn1_norm, _, _ = _ffn_bwd(
        "ffn1", dh1, dh1_b, xt, small["ffn1_norm"], w["ffn1_w_gu"], w["ffn1_w_down"], ffn1_saved,
        {"dwgu": lambda: [mixer.scatter()],
         "dwd": lambda dwgu: [mixer.forward(), up1.exchange([dwgu])],
         "dnorm": lambda dwgu, dwd: [up1.scatter(), down1.exchange([dwd])]})
    _run_comms("grad_tail_scatter", [up1.forward(), down1.scatter()])
    _run_comms("grad_tail_forward", [down1.forward()])
    terms = {}
    for pipe in (up2, down2, ple, proj, mixer, up1, down1):
        terms.update(pipe.terms())

    def fold(v):
        return v[0, :HEAD_DIM] + v[0, HEAD_DIM:]

    small_grads = {"ffn1_norm": d_ffn1_norm, "mix_norm": d_mix_norm, "ffn2_norm": d_ffn2_norm,
                   "ple_norm": d_ple_norm, "a_q_norm": fold(dgqa), "a_k_norm": fold(dgka),
                   "b_q_norm": fold(dgqb), "b_k_norm": fold(dgkb), "a_rel_bias": _rel_bias_grad(_unpair_bias(dbias_a)),
                   "b_sinks": jnp.sum(dsink_b, axis=1)}
    return loss, dx, terms, small_grads


def kernel(x, p, ffn1_norm, ffn1_w_gu, ffn1_w_down, mix_norm, w_in, a_q_norm, a_k_norm, a_rel_bias, b_q_norm, b_k_norm, b_sinks, w_gate, w_proj_a, w_proj_b, w_out, ffn2_norm, ffn2_w_gu, ffn2_w_down, ple_norm, w_ple_gate, w_ple_proj, loss_target, m_ffn1_norm, m_ffn1_w_gu, m_ffn1_w_down, m_mix_norm, m_w_in, m_a_q_norm, m_a_k_norm, m_a_rel_bias, m_b_q_norm, m_b_k_norm, m_b_sinks, m_w_gate, m_w_proj_a, m_w_proj_b, m_w_out, m_ffn2_norm, m_ffn2_w_gu, m_ffn2_w_down, m_ple_norm, m_w_ple_gate, m_w_ple_proj, v_ffn1_norm, v_ffn1_w_gu, v_ffn1_w_down, v_mix_norm, v_w_in, v_a_q_norm, v_a_k_norm, v_a_rel_bias, v_b_q_norm, v_b_k_norm, v_b_sinks, v_w_gate, v_w_proj_a, v_w_proj_b, v_w_out, v_ffn2_norm, v_ffn2_w_gu, v_ffn2_w_down, v_ple_norm, v_w_ple_gate, v_w_ple_proj):
    given = dict(locals())
    n_batch, s, d = x.shape
    t = n_batch * s
    xt = x.reshape(t, d)
    pt = p.reshape(t, p.shape[-1])
    tgt = loss_target.reshape(t, d)

    chip = (2 * lax.axis_index("x") + lax.axis_index("y")).astype(jnp.int32).reshape(1)
    bufs = {name: _cast_into_slot("cast_" + name, given[name][0], chip) for name in BIG}
    small = {name: given[name] for name in SMALL}
    core = lax.axis_index("c").astype(jnp.int32).reshape(1)
    loss, dx, terms, small_grads = _local_step(xt, pt, tgt, n_batch, bufs, small, core)

    grads, deltas, new_m, new_v = {}, {}, {}, {}
    for name in BIG:
        gw, dl, nm, nv = _adamw_terms("adamw_" + name, terms[name], given[name][0], given["m_" + name][0],
                                      given["v_" + name][0])
        grads[name], deltas[name], new_m[name], new_v[name] = gw[None], dl[None], nm[None], nv[None]

    small_shapes = [given[name].shape for name in SMALL] + [()]
    g_pack = _all_reduce_small(_pack([small_grads[name] for name in SMALL] + [loss]))
    zero = jnp.zeros((), F32)
    w_pack = _pack([given[name] for name in SMALL] + [zero])
    m_pack = _pack([given["m_" + name] for name in SMALL] + [zero])
    v_pack = _pack([given["v_" + name] for name in SMALL] + [zero])
    d_pack, nm_pack, nv_pack = _ew("adamw_small", lambda wv, gv, mv, vv: _adamw_math(wv, gv, mv, vv),
                                   [w_pack, g_pack, m_pack, v_pack], [F32] * 3)
    g_small = _unpack(g_pack, small_shapes)
    loss_total = g_small[-1]
    for name, gv, dv, mv, vv in zip(SMALL, g_small, _unpack(d_pack, small_shapes), _unpack(nm_pack, small_shapes),
                                    _unpack(nv_pack, small_shapes)):
        grads[name], deltas[name], new_m[name], new_v[name] = gv, dv, mv, vv

    return (loss_total, dx.reshape(x.shape), *[grads[n] for n in WEIGHTS], *[deltas[n] for n in WEIGHTS],
            *[new_m[n] for n in WEIGHTS], *[new_v[n] for n in WEIGHTS])
```
